```python
import math
import jax, jax.numpy as jnp
from jax import lax
import numpy as np

D_MODEL = 1024
BATCH = 16
SEQ = 2048
DEPTH = 1

N_META = 16
D_MIX = D_MODEL
N_HEADS = 8
QK_NOPE = 64
QK_ROPE = 32
QK_HEAD = QK_NOPE + QK_ROPE
V_HEAD = 64
D_ATTN = N_HEADS * V_HEAD
Q_LORA = 384
KV_LORA = 256
D_RNN = D_MIX - D_ATTN
RNN_BLOCKS = 8
RNN_BW = D_RNN // RNN_BLOCKS
CONV_W = 4
CONV_PAD = (2, 1)
LRU_C = 8.0
ROPE_THETA = 10000.0
Q_BLOCK = 128
OFF_CQ = Q_LORA
OFF_CKV = OFF_CQ + KV_LORA
OFF_KR = OFF_CKV + QK_ROPE
OFF_XR = OFF_KR + D_RNN
IN_COLS = OFF_XR + D_RNN
D_FF = int(math.ceil(8 * D_MODEL / 3 / 256) * 256)
EPS = 1e-6

kernel_name = "hymba_mla_rglru_hybrid_encoder"


def rms_norm(x, g):
    xf = x.astype(jnp.float32)
    y = xf * lax.rsqrt(jnp.mean(xf * xf, axis=-1, keepdims=True) + EPS)
    return (y * g.astype(jnp.float32)).astype(x.dtype)


def rope(x, pos):
    half = x.shape[-1] // 2
    freqs = 1.0 / (ROPE_THETA ** (jnp.arange(half, dtype=jnp.float32) / half))
    ang = pos[:, None] * freqs[None, :]
    cos = jnp.cos(ang)[None, :, None, :]
    sin = jnp.sin(ang)[None, :, None, :]
    xf = x.astype(jnp.float32)
    x1, x2 = xf[..., :half], xf[..., half:]
    out = jnp.concatenate([x1 * cos - x2 * sin, x1 * sin + x2 * cos], axis=-1)
    return out.astype(x.dtype)


def attend_block(q_blk, k, v):
    s = jnp.einsum('bhqd,bhkd->bhqk', q_blk, k).astype(jnp.float32) * (QK_HEAD ** -0.5)
    p = jax.nn.softmax(s, axis=-1)
    return jnp.einsum('bhqk,bhkd->bhqd', p.astype(v.dtype), v)


def mla_group(c_q, c_kv, k_r, q_a_g, w_uq, kv_a_g, w_ukv, q_g, k_g, pos):
    B, T, _ = c_q.shape
    q = (rms_norm(c_q, q_a_g) @ w_uq).reshape(B, T, N_HEADS, QK_HEAD)
    kv = (rms_norm(c_kv, kv_a_g) @ w_ukv).reshape(B, T, N_HEADS, QK_NOPE + V_HEAD)
    k_nope, v = kv[..., :QK_NOPE], kv[..., QK_NOPE:]
    k = jnp.concatenate([k_nope, jnp.broadcast_to(k_r[:, :, None, :], (B, T, N_HEADS, QK_ROPE))], axis=-1)
    q = rms_norm(q, q_g)
    k = rms_norm(k, k_g)
    q = jnp.concatenate([q[..., :QK_NOPE], rope(q[..., QK_NOPE:], pos)], axis=-1)
    k = jnp.concatenate([k[..., :QK_NOPE], rope(k[..., QK_NOPE:], pos)], axis=-1)
    q = q.transpose(0, 2, 1, 3)
    k = k.transpose(0, 2, 1, 3)
    v = v.transpose(0, 2, 1, 3)
    o_meta = attend_block(q[:, :, :N_META], k, v)
    q_real = q[:, :, N_META:]
    n_blk = q_real.shape[2] // Q_BLOCK
    q_blocks = q_real.reshape(B, N_HEADS, n_blk, Q_BLOCK, QK_HEAD).transpose(2, 0, 1, 3, 4)
    o_blocks = lax.map(lambda qb: attend_block(qb, k, v), q_blocks)
    o_real = o_blocks.transpose(1, 2, 0, 3, 4).reshape(B, N_HEADS, n_blk * Q_BLOCK, V_HEAD)
    o = jnp.concatenate([o_meta, o_real], axis=2)
    return o.transpose(0, 2, 1, 3).reshape(B, T, D_ATTN)


def _linear_combine(c1, c2):
    a1, b1 = c1
    a2, b2 = c2
    return a1 * a2, a2 * b1 + b2


def rg_lru(xc, wa, ba, wi, bi, lam, reverse):
    B, T, _ = xc.shape
    xg = xc.reshape(B, T, RNN_BLOCKS, RNN_BW)
    r = jax.nn.sigmoid((jnp.einsum('btgi,gij->btgj', xg, wa).reshape(B, T, D_RNN) + ba).astype(jnp.float32))
    i = jax.nn.sigmoid((jnp.einsum('btgi,gij->btgj', xg, wi).reshape(B, T, D_RNN) + bi).astype(jnp.float32))
    log_a = -LRU_C * r * jax.nn.softplus(-lam.astype(jnp.float32))
    a = jnp.exp(log_a)
    b = jnp.sqrt(jnp.maximum(-jnp.expm1(2.0 * log_a), 0.0)) * (i * xc.astype(jnp.float32))
    _, h = lax.associative_scan(_linear_combine, (a, b), axis=1, reverse=reverse)
    return h.astype(xc.dtype)


def rglru_group(x_r, x_gate, conv_w, conv_b, wa, ba, wi, bi, lam):
    xc = lax.conv_general_dilated(
        x_r, conv_w[:, None, :], window_strides=(1,), padding=[CONV_PAD],
        dimension_numbers=('NWC', 'WIO', 'NWC'), feature_group_count=D_RNN) + conv_b
    y = rg_lru(xc, wa[0], ba[0], wi[0], bi[0], lam[0], reverse=False) \
        + rg_lru(xc, wa[1], ba[1], wi[1], bi[1], lam[1], reverse=True)
    return y * jax.nn.gelu(x_gate)


def _fwd_setup_inputs(seed: int = 0) -> dict:
    key = jax.random.key(seed)
    ks = iter(jax.random.split(key, 40))
    L = DEPTH
    f32 = jnp.float32

    def nrm(shape, fan_in):
        return jax.random.normal(next(ks), shape, f32) * (fan_in ** -0.5)

    def gain(shape):
        return 1.0 + 0.02 * jax.random.normal(next(ks), shape, f32)

    def bias(shape):
        return 0.01 * jax.random.normal(next(ks), shape, f32)

    x = jax.random.normal(next(ks), (BATCH, SEQ, D_MODEL), f32)
    meta_tokens = jax.random.normal(next(ks), (N_META, D_MODEL), f32)
    u = jax.random.uniform(next(ks), (L, 2, D_RNN), f32, 0.9, 0.999)
    s = u ** (1.0 / LRU_C)
    lru_lambda = jnp.log(s) - jnp.log1p(-s)
    return {
        "x": x,
        "meta_tokens": meta_tokens,
        "ln1_g": gain((L, D_MODEL)),
        "w_in": nrm((L, D_MODEL, IN_COLS), D_MODEL),
        "q_a_norm_g": gain((L, Q_LORA)),
        "w_uq": nrm((L, Q_LORA, N_HEADS * QK_HEAD), Q_LORA),
        "kv_a_norm_g": gain((L, KV_LORA)),
        "w_ukv": nrm((L, KV_LORA, N_HEADS * (QK_NOPE + V_HEAD)), KV_LORA),
        "q_norm_g": gain((L, QK_HEAD)),
        "k_norm_g": gain((L, QK_HEAD)),
        "conv_w": nrm((L, CONV_W, D_RNN), CONV_W),
        "conv_b": bias((L, D_RNN)),
        "lru_wa": nrm((L, 2, RNN_BLOCKS, RNN_BW, RNN_BW), RNN_BW),
        "lru_ba": bias((L, 2, D_RNN)),
        "lru_wi": nrm((L, 2, RNN_BLOCKS, RNN_BW, RNN_BW), RNN_BW),
        "lru_bi": bias((L, 2, D_RNN)),
        "lru_lambda": lru_lambda,
        "attn_out_g": gain((L, D_ATTN)),
        "rnn_out_g": gain((L, D_RNN)),
        "w_out": nrm((L, D_MIX, D_MODEL), D_MIX),
        "ln2_g": gain((L, D_MODEL)),
        "w_gate": nrm((L, D_MODEL, D_FF), D_MODEL),
        "w_up": nrm((L, D_MODEL, D_FF), D_MODEL),
        "w_down": nrm((L, D_FF, D_MODEL), D_FF),
    }


def _fwd_reference(x, meta_tokens, ln1_g, w_in, q_a_norm_g, w_uq, kv_a_norm_g, w_ukv,
              q_norm_g, k_norm_g, conv_w, conv_b, lru_wa, lru_ba, lru_wi, lru_bi,
              lru_lambda, attn_out_g, rnn_out_g, w_out, ln2_g, w_gate, w_up, w_down):
    B = x.shape[0]
    meta = jnp.broadcast_to(meta_tokens[None].astype(x.dtype), (B, N_META, x.shape[-1]))
    h = jnp.concatenate([meta, x], axis=1)
    T = h.shape[1]
    pos = jnp.arange(T, dtype=jnp.float32)
    for l in range(DEPTH):
        hn = rms_norm(h, ln1_g[l])
        p = hn @ w_in[l]
        c_q = p[..., :OFF_CQ]
        c_kv = p[..., OFF_CQ:OFF_CKV]
        k_r = p[..., OFF_CKV:OFF_KR]
        x_r = p[..., OFF_KR:OFF_XR]
        x_gate = p[..., OFF_XR:]
        o_attn = mla_group(c_q, c_kv, k_r, q_a_norm_g[l], w_uq[l], kv_a_norm_g[l],
                           w_ukv[l], q_norm_g[l], k_norm_g[l], pos)
        o_rnn = rglru_group(x_r, x_gate, conv_w[l], conv_b[l], lru_wa[l], lru_ba[l],
                            lru_wi[l], lru_bi[l], lru_lambda[l])
        mix = jnp.concatenate([rms_norm(o_attn, attn_out_g[l]), rms_norm(o_rnn, rnn_out_g[l])], axis=-1)
        h = h + mix @ w_out[l]
        hn = rms_norm(h, ln2_g[l])
        h = h + (jax.nn.silu(hn @ w_gate[l]) * (hn @ w_up[l])) @ w_down[l]
    return h[:, N_META:]


import jax as _jax
import jax.numpy as _jnp

TWIN_FORMAT = 'train_step'
FWD_PARAMS = ['x', 'meta_tokens', 'ln1_g', 'w_in', 'q_a_norm_g', 'w_uq', 'kv_a_norm_g', 'w_ukv', 'q_norm_g', 'k_norm_g', 'conv_w', 'conv_b', 'lru_wa', 'lru_ba', 'lru_wi', 'lru_bi', 'lru_lambda', 'attn_out_g', 'rnn_out_g', 'w_out', 'ln2_g', 'w_gate', 'w_up', 'w_down']
TWIN_WEIGHTS = ['meta_tokens', 'ln1_g', 'w_in', 'q_a_norm_g', 'w_uq', 'kv_a_norm_g', 'w_ukv', 'q_norm_g', 'k_norm_g', 'conv_w', 'conv_b', 'lru_wa', 'lru_ba', 'lru_wi', 'lru_bi', 'lru_lambda', 'attn_out_g', 'rnn_out_g', 'w_out', 'ln2_g', 'w_gate', 'w_up', 'w_down']
TWIN_DIFF_INPUT = 'x'
TWIN_INPUTS = ['x', 'meta_tokens', 'ln1_g', 'w_in', 'q_a_norm_g', 'w_uq', 'kv_a_norm_g', 'w_ukv', 'q_norm_g', 'k_norm_g', 'conv_w', 'conv_b', 'lru_wa', 'lru_ba', 'lru_wi', 'lru_bi', 'lru_lambda', 'attn_out_g', 'rnn_out_g', 'w_out', 'ln2_g', 'w_gate', 'w_up', 'w_down', 'loss_target', 'm_meta_tokens', 'm_ln1_g', 'm_w_in', 'm_q_a_norm_g', 'm_w_uq', 'm_kv_a_norm_g', 'm_w_ukv', 'm_q_norm_g', 'm_k_norm_g', 'm_conv_w', 'm_conv_b', 'm_lru_wa', 'm_lru_ba', 'm_lru_wi', 'm_lru_bi', 'm_lru_lambda', 'm_attn_out_g', 'm_rnn_out_g', 'm_w_out', 'm_ln2_g', 'm_w_gate', 'm_w_up', 'm_w_down', 'v_meta_tokens', 'v_ln1_g', 'v_w_in', 'v_q_a_norm_g', 'v_w_uq', 'v_kv_a_norm_g', 'v_w_ukv', 'v_q_norm_g', 'v_k_norm_g', 'v_conv_w', 'v_conv_b', 'v_lru_wa', 'v_lru_ba', 'v_lru_wi', 'v_lru_bi', 'v_lru_lambda', 'v_attn_out_g', 'v_rnn_out_g', 'v_w_out', 'v_ln2_g', 'v_w_gate', 'v_w_up', 'v_w_down']
TWIN_OUTPUTS = ['loss', 'grad_x', 'grad_meta_tokens', 'grad_ln1_g', 'grad_w_in', 'grad_q_a_norm_g', 'grad_w_uq', 'grad_kv_a_norm_g', 'grad_w_ukv', 'grad_q_norm_g', 'grad_k_norm_g', 'grad_conv_w', 'grad_conv_b', 'grad_lru_wa', 'grad_lru_ba', 'grad_lru_wi', 'grad_lru_bi', 'grad_lru_lambda', 'grad_attn_out_g', 'grad_rnn_out_g', 'grad_w_out', 'grad_ln2_g', 'grad_w_gate', 'grad_w_up', 'grad_w_down', 'delta_meta_tokens', 'delta_ln1_g', 'delta_w_in', 'delta_q_a_norm_g', 'delta_w_uq', 'delta_kv_a_norm_g', 'delta_w_ukv', 'delta_q_norm_g', 'delta_k_norm_g', 'delta_conv_w', 'delta_conv_b', 'delta_lru_wa', 'delta_lru_ba', 'delta_lru_wi', 'delta_lru_bi', 'delta_lru_lambda', 'delta_attn_out_g', 'delta_rnn_out_g', 'delta_w_out', 'delta_ln2_g', 'delta_w_gate', 'delta_w_up', 'delta_w_down', 'new_m_meta_tokens', 'new_m_ln1_g', 'new_m_w_in', 'new_m_q_a_norm_g', 'new_m_w_uq', 'new_m_kv_a_norm_g', 'new_m_w_ukv', 'new_m_q_norm_g', 'new_m_k_norm_g', 'new_m_conv_w', 'new_m_conv_b', 'new_m_lru_wa', 'new_m_lru_ba', 'new_m_lru_wi', 'new_m_lru_bi', 'new_m_lru_lambda', 'new_m_attn_out_g', 'new_m_rnn_out_g', 'new_m_w_out', 'new_m_ln2_g', 'new_m_w_gate', 'new_m_w_up', 'new_m_w_down', 'new_v_meta_tokens', 'new_v_ln1_g', 'new_v_w_in', 'new_v_q_a_norm_g', 'new_v_w_uq', 'new_v_kv_a_norm_g', 'new_v_w_ukv', 'new_v_q_norm_g', 'new_v_k_norm_g', 'new_v_conv_w', 'new_v_conv_b', 'new_v_lru_wa', 'new_v_lru_ba', 'new_v_lru_wi', 'new_v_lru_bi', 'new_v_lru_lambda', 'new_v_attn_out_g', 'new_v_rnn_out_g', 'new_v_w_out', 'new_v_ln2_g', 'new_v_w_gate', 'new_v_w_up', 'new_v_w_down']
TWIN_LEAF_KINDS = {'loss': 'loss', 'grad_x': 'grad_x', 'grad_meta_tokens': 'grad_w', 'grad_ln1_g': 'grad_w', 'grad_w_in': 'grad_w', 'grad_q_a_norm_g': 'grad_w', 'grad_w_uq': 'grad_w', 'grad_kv_a_norm_g': 'grad_w', 'grad_w_ukv': 'grad_w', 'grad_q_norm_g': 'grad_w', 'grad_k_norm_g': 'grad_w', 'grad_conv_w': 'grad_w', 'grad_conv_b': 'grad_w', 'grad_lru_wa': 'grad_w', 'grad_lru_ba': 'grad_w', 'grad_lru_wi': 'grad_w', 'grad_lru_bi': 'grad_w', 'grad_lru_lambda': 'grad_w', 'grad_attn_out_g': 'grad_w', 'grad_rnn_out_g': 'grad_w', 'grad_w_out': 'grad_w', 'grad_ln2_g': 'grad_w', 'grad_w_gate': 'grad_w', 'grad_w_up': 'grad_w', 'grad_w_down': 'grad_w', 'delta_meta_tokens': 'delta_w', 'delta_ln1_g': 'delta_w', 'delta_w_in': 'delta_w', 'delta_q_a_norm_g': 'delta_w', 'delta_w_uq': 'delta_w', 'delta_kv_a_norm_g': 'delta_w', 'delta_w_ukv': 'delta_w', 'delta_q_norm_g': 'delta_w', 'delta_k_norm_g': 'delta_w', 'delta_conv_w': 'delta_w', 'delta_conv_b': 'delta_w', 'delta_lru_wa': 'delta_w', 'delta_lru_ba': 'delta_w', 'delta_lru_wi': 'delta_w', 'delta_lru_bi': 'delta_w', 'delta_lru_lambda': 'delta_w', 'delta_attn_out_g': 'delta_w', 'delta_rnn_out_g': 'delta_w', 'delta_w_out': 'delta_w', 'delta_ln2_g': 'delta_w', 'delta_w_gate': 'delta_w', 'delta_w_up': 'delta_w', 'delta_w_down': 'delta_w', 'new_m_meta_tokens': 'new_m', 'new_m_ln1_g': 'new_m', 'new_m_w_in': 'new_m', 'new_m_q_a_norm_g': 'new_m', 'new_m_w_uq': 'new_m', 'new_m_kv_a_norm_g': 'new_m', 'new_m_w_ukv': 'new_m', 'new_m_q_norm_g': 'new_m', 'new_m_k_norm_g': 'new_m', 'new_m_conv_w': 'new_m', 'new_m_conv_b': 'new_m', 'new_m_lru_wa': 'new_m', 'new_m_lru_ba': 'new_m', 'new_m_lru_wi': 'new_m', 'new_m_lru_bi': 'new_m', 'new_m_lru_lambda': 'new_m', 'new_m_attn_out_g': 'new_m', 'new_m_rnn_out_g': 'new_m', 'new_m_w_out': 'new_m', 'new_m_ln2_g': 'new_m', 'new_m_w_gate': 'new_m', 'new_m_w_up': 'new_m', 'new_m_w_down': 'new_m', 'new_v_meta_tokens': 'new_v', 'new_v_ln1_g': 'new_v', 'new_v_w_in': 'new_v', 'new_v_q_a_norm_g': 'new_v', 'new_v_w_uq': 'new_v', 'new_v_kv_a_norm_g': 'new_v', 'new_v_w_ukv': 'new_v', 'new_v_q_norm_g': 'new_v', 'new_v_k_norm_g': 'new_v', 'new_v_conv_w': 'new_v', 'new_v_conv_b': 'new_v', 'new_v_lru_wa': 'new_v', 'new_v_lru_ba': 'new_v', 'new_v_lru_wi': 'new_v', 'new_v_lru_bi': 'new_v', 'new_v_lru_lambda': 'new_v', 'new_v_attn_out_g': 'new_v', 'new_v_rnn_out_g': 'new_v', 'new_v_w_out': 'new_v', 'new_v_ln2_g': 'new_v', 'new_v_w_gate': 'new_v', 'new_v_w_up': 'new_v', 'new_v_w_down': 'new_v'}


def _forward(args):
    return _fwd_reference(*[args[k] for k in FWD_PARAMS])


def _output_shape():
    out = _jax.eval_shape(lambda: _forward(_fwd_setup_inputs(0)))
    return out.shape, out.dtype

N_MICROBATCH = 1
ADAM_LR = 0.001
ADAM_B1 = 0.9
ADAM_B2 = 0.999
ADAM_EPS = 1e-08
ADAM_WD = 0.01
ADAM_STEP = 10
PER_EXAMPLE_BATCH_AXIS = {'x': 0, 'loss_target': 0}
SHARED_INPUTS = []
_WEIGHT_DTYPES = {'meta_tokens': _jnp.float32, 'ln1_g': _jnp.float32, 'w_in': _jnp.float32, 'q_a_norm_g': _jnp.float32, 'w_uq': _jnp.float32, 'kv_a_norm_g': _jnp.float32, 'w_ukv': _jnp.float32, 'q_norm_g': _jnp.float32, 'k_norm_g': _jnp.float32, 'conv_w': _jnp.float32, 'conv_b': _jnp.float32, 'lru_wa': _jnp.float32, 'lru_ba': _jnp.float32, 'lru_wi': _jnp.float32, 'lru_bi': _jnp.float32, 'lru_lambda': _jnp.float32, 'attn_out_g': _jnp.float32, 'rnn_out_g': _jnp.float32, 'w_out': _jnp.float32, 'ln2_g': _jnp.float32, 'w_gate': _jnp.float32, 'w_up': _jnp.float32, 'w_down': _jnp.float32}
MOMENT_SCALE = {'meta_tokens': 4.770893e-02, 'ln1_g': 1.492618e+00, 'w_in': 9.303246e-01, 'q_a_norm_g': 1.345861e+00, 'w_uq': 7.931128e-01, 'kv_a_norm_g': 3.351663e+00, 'w_ukv': 1.313895e+00, 'q_norm_g': 3.151155e+00, 'k_norm_g': 3.011075e+00, 'conv_w': 1.659703e+00, 'conv_b': 4.233726e+01, 'lru_wa': 6.863120e-01, 'lru_ba': 4.859869e-01, 'lru_wi': 1.276493e+00, 'lru_bi': 3.820616e-01, 'lru_lambda': 5.886733e-01, 'attn_out_g': 3.209115e+01, 'rnn_out_g': 4.893382e+01, 'w_out': 1.773317e+00, 'ln2_g': 2.488659e+01, 'w_gate': 3.718241e-01, 'w_up': 2.656580e-01, 'w_down': 3.988154e-01}


def _to_microbatches(a, axis):
    t = _jnp.moveaxis(a, axis, 0)
    t = t.reshape((N_MICROBATCH, t.shape[0] // N_MICROBATCH) + t.shape[1:])
    return _jnp.moveaxis(t, 1, axis + 1)


def setup_inputs(seed: int = 0) -> dict:
    inp = _fwd_setup_inputs(seed)
    key = _jax.random.fold_in(_jax.random.key(seed), 7919)
    shape, _ = _output_shape()
    out = dict(inp)
    out["loss_target"] = _jax.random.normal(_jax.random.fold_in(key, 0), shape, _jnp.float32)
    for i, name in enumerate(TWIN_WEIGHTS):
        w = inp[name].astype(_jnp.float32)
        if MOMENT_SCALE is None:
            s = _jnp.sqrt(_jnp.mean(_jnp.square(w)) + 1e-30)
        else:
            s = MOMENT_SCALE[name]
        km, kv = _jax.random.split(_jax.random.fold_in(key, i + 1))
        out[name] = w
        out["m_" + name] = s * _jax.random.normal(km, w.shape, _jnp.float32)
        out["v_" + name] = (s * s) * _jax.random.uniform(kv, w.shape, _jnp.float32, 0.5, 1.5)
    if N_MICROBATCH > 1:
        for name, axis in PER_EXAMPLE_BATCH_AXIS.items():
            out[name] = _to_microbatches(out[name], axis)
    return {'x': out['x'], 'meta_tokens': out['meta_tokens'], 'ln1_g': out['ln1_g'], 'w_in': out['w_in'], 'q_a_norm_g': out['q_a_norm_g'], 'w_uq': out['w_uq'], 'kv_a_norm_g': out['kv_a_norm_g'], 'w_ukv': out['w_ukv'], 'q_norm_g': out['q_norm_g'], 'k_norm_g': out['k_norm_g'], 'conv_w': out['conv_w'], 'conv_b': out['conv_b'], 'lru_wa': out['lru_wa'], 'lru_ba': out['lru_ba'], 'lru_wi': out['lru_wi'], 'lru_bi': out['lru_bi'], 'lru_lambda': out['lru_lambda'], 'attn_out_g': out['attn_out_g'], 'rnn_out_g': out['rnn_out_g'], 'w_out': out['w_out'], 'ln2_g': out['ln2_g'], 'w_gate': out['w_gate'], 'w_up': out['w_up'], 'w_down': out['w_down'], 'loss_target': out['loss_target'], 'm_meta_tokens': out['m_meta_tokens'], 'm_ln1_g': out['m_ln1_g'], 'm_w_in': out['m_w_in'], 'm_q_a_norm_g': out['m_q_a_norm_g'], 'm_w_uq': out['m_w_uq'], 'm_kv_a_norm_g': out['m_kv_a_norm_g'], 'm_w_ukv': out['m_w_ukv'], 'm_q_norm_g': out['m_q_norm_g'], 'm_k_norm_g': out['m_k_norm_g'], 'm_conv_w': out['m_conv_w'], 'm_conv_b': out['m_conv_b'], 'm_lru_wa': out['m_lru_wa'], 'm_lru_ba': out['m_lru_ba'], 'm_lru_wi': out['m_lru_wi'], 'm_lru_bi': out['m_lru_bi'], 'm_lru_lambda': out['m_lru_lambda'], 'm_attn_out_g': out['m_attn_out_g'], 'm_rnn_out_g': out['m_rnn_out_g'], 'm_w_out': out['m_w_out'], 'm_ln2_g': out['m_ln2_g'], 'm_w_gate': out['m_w_gate'], 'm_w_up': out['m_w_up'], 'm_w_down': out['m_w_down'], 'v_meta_tokens': out['v_meta_tokens'], 'v_ln1_g': out['v_ln1_g'], 'v_w_in': out['v_w_in'], 'v_q_a_norm_g': out['v_q_a_norm_g'], 'v_w_uq': out['v_w_uq'], 'v_kv_a_norm_g': out['v_kv_a_norm_g'], 'v_w_ukv': out['v_w_ukv'], 'v_q_norm_g': out['v_q_norm_g'], 'v_k_norm_g': out['v_k_norm_g'], 'v_conv_w': out['v_conv_w'], 'v_conv_b': out['v_conv_b'], 'v_lru_wa': out['v_lru_wa'], 'v_lru_ba': out['v_lru_ba'], 'v_lru_wi': out['v_lru_wi'], 'v_lru_bi': out['v_lru_bi'], 'v_lru_lambda': out['v_lru_lambda'], 'v_attn_out_g': out['v_attn_out_g'], 'v_rnn_out_g': out['v_rnn_out_g'], 'v_w_out': out['v_w_out'], 'v_ln2_g': out['v_ln2_g'], 'v_w_gate': out['v_w_gate'], 'v_w_up': out['v_w_up'], 'v_w_down': out['v_w_down']}


def _loss(weights, diff, rest, loss_target):
    with _jax.named_scope("forward"):
        args = {**rest, TWIN_DIFF_INPUT: diff, **{k: w.astype(_WEIGHT_DTYPES[k]) for k, w in weights.items()}}
        y = _forward(args)
    with _jax.named_scope("loss_head"):
        err = _jnp.square(y.astype(_jnp.float32) - loss_target)
        return 0.5 * _jnp.sum(_jnp.mean(err, axis=-1)) if err.ndim else 0.5 * err


def _adamw(w, g, m, v):
    m = ADAM_B1 * m + (1.0 - ADAM_B1) * g
    v = ADAM_B2 * v + (1.0 - ADAM_B2) * _jnp.square(g)
    m_hat = m / (1.0 - ADAM_B1 ** ADAM_STEP)
    v_hat = v / (1.0 - ADAM_B2 ** ADAM_STEP)
    delta = -ADAM_LR * (m_hat / (_jnp.sqrt(v_hat) + ADAM_EPS) + ADAM_WD * w)
    return delta, m, v


def reference(x, meta_tokens, ln1_g, w_in, q_a_norm_g, w_uq, kv_a_norm_g, w_ukv, q_norm_g, k_norm_g, conv_w, conv_b, lru_wa, lru_ba, lru_wi, lru_bi, lru_lambda, attn_out_g, rnn_out_g, w_out, ln2_g, w_gate, w_up, w_down, loss_target, m_meta_tokens, m_ln1_g, m_w_in, m_q_a_norm_g, m_w_uq, m_kv_a_norm_g, m_w_ukv, m_q_norm_g, m_k_norm_g, m_conv_w, m_conv_b, m_lru_wa, m_lru_ba, m_lru_wi, m_lru_bi, m_lru_lambda, m_attn_out_g, m_rnn_out_g, m_w_out, m_ln2_g, m_w_gate, m_w_up, m_w_down, v_meta_tokens, v_ln1_g, v_w_in, v_q_a_norm_g, v_w_uq, v_kv_a_norm_g, v_w_ukv, v_q_norm_g, v_k_norm_g, v_conv_w, v_conv_b, v_lru_wa, v_lru_ba, v_lru_wi, v_lru_bi, v_lru_lambda, v_attn_out_g, v_rnn_out_g, v_w_out, v_ln2_g, v_w_gate, v_w_up, v_w_down):
    given = dict(x=x, meta_tokens=meta_tokens, ln1_g=ln1_g, w_in=w_in, q_a_norm_g=q_a_norm_g, w_uq=w_uq, kv_a_norm_g=kv_a_norm_g, w_ukv=w_ukv, q_norm_g=q_norm_g, k_norm_g=k_norm_g, conv_w=conv_w, conv_b=conv_b, lru_wa=lru_wa, lru_ba=lru_ba, lru_wi=lru_wi, lru_bi=lru_bi, lru_lambda=lru_lambda, attn_out_g=attn_out_g, rnn_out_g=rnn_out_g, w_out=w_out, ln2_g=ln2_g, w_gate=w_gate, w_up=w_up, w_down=w_down, loss_target=loss_target, m_meta_tokens=m_meta_tokens, m_ln1_g=m_ln1_g, m_w_in=m_w_in, m_q_a_norm_g=m_q_a_norm_g, m_w_uq=m_w_uq, m_kv_a_norm_g=m_kv_a_norm_g, m_w_ukv=m_w_ukv, m_q_norm_g=m_q_norm_g, m_k_norm_g=m_k_norm_g, m_conv_w=m_conv_w, m_conv_b=m_conv_b, m_lru_wa=m_lru_wa, m_lru_ba=m_lru_ba, m_lru_wi=m_lru_wi, m_lru_bi=m_lru_bi, m_lru_lambda=m_lru_lambda, m_attn_out_g=m_attn_out_g, m_rnn_out_g=m_rnn_out_g, m_w_out=m_w_out, m_ln2_g=m_ln2_g, m_w_gate=m_w_gate, m_w_up=m_w_up, m_w_down=m_w_down, v_meta_tokens=v_meta_tokens, v_ln1_g=v_ln1_g, v_w_in=v_w_in, v_q_a_norm_g=v_q_a_norm_g, v_w_uq=v_w_uq, v_kv_a_norm_g=v_kv_a_norm_g, v_w_ukv=v_w_ukv, v_q_norm_g=v_q_norm_g, v_k_norm_g=v_k_norm_g, v_conv_w=v_conv_w, v_conv_b=v_conv_b, v_lru_wa=v_lru_wa, v_lru_ba=v_lru_ba, v_lru_wi=v_lru_wi, v_lru_bi=v_lru_bi, v_lru_lambda=v_lru_lambda, v_attn_out_g=v_attn_out_g, v_rnn_out_g=v_rnn_out_g, v_w_out=v_w_out, v_ln2_g=v_ln2_g, v_w_gate=v_w_gate, v_w_up=v_w_up, v_w_down=v_w_down)
    weights = {n: given[n] for n in TWIN_WEIGHTS}
    shared = {n: given[n] for n in SHARED_INPUTS}
    per_example = {n: given[n] for n in ['x']}
    grad_fn = _jax.value_and_grad(_loss, argnums=(0, 1))

    def one_microbatch(ex, loss_target):
        ex = dict(ex)
        diff = ex.pop(TWIN_DIFF_INPUT)
        return grad_fn(weights, diff, {**shared, **ex}, loss_target)

    if N_MICROBATCH == 1:
        loss, (grad_w, grad_x) = one_microbatch(per_example, given["loss_target"])
    else:
        def body(carry, xs):
            loss_sum, grad_sum = carry
            l_k, (gw_k, gx_k) = one_microbatch(xs[0], xs[1])
            with _jax.named_scope("update"):
                return (loss_sum + l_k, _jax.tree.map(_jnp.add, grad_sum, gw_k)), gx_k

        init = (_jnp.zeros((), _jnp.float32), _jax.tree.map(_jnp.zeros_like, weights))
        (loss, grad_w), grad_x = _jax.lax.scan(body, init, (per_example, given["loss_target"]))
    with _jax.named_scope("update"):
        delta_w, new_m, new_v = {}, {}, {}
        for n in TWIN_WEIGHTS:
            delta_w[n], new_m[n], new_v[n] = _adamw(weights[n], grad_w[n], given["m_" + n], given["v_" + n])
    return (loss, grad_x, *[grad_w[n] for n in TWIN_WEIGHTS], *[delta_w[n] for n in TWIN_WEIGHTS],
            *[new_m[n] for n in TWIN_WEIGHTS], *[new_v[n] for n in TWIN_WEIGHTS])
```

```python
import functools
import math

import jax
import jax.numpy as jnp
from jax import lax
from jax.experimental import pallas as pl
from jax.experimental.pallas import tpu as pltpu

F32 = jnp.float32
BF = jnp.bfloat16
MESH = pl.DeviceIdType.MESH

D = 1024
SEQ = 2048
N_META = 16
T = N_META + SEQ
N_HEADS = 8
QK_NOPE = 64
QK_ROPE = 32
QK_HEAD = 96
V_HEAD = 64
Q_LORA = 384
KV_LORA = 256
D_ATTN = 512
D_RNN = 512
RNN_BW = 64
CONV_W = 4
LRU_C = 8.0
ROPE_THETA = 10000.0
D_FF = 2816
EPS = 1e-6
IN_COLS = 1696
ADAM_LR, ADAM_B1, ADAM_B2, ADAM_EPS, ADAM_WD, ADAM_STEP = 0.001, 0.9, 0.999, 1e-08, 0.01, 10

LANES = 128
TP = 2176
NB = 2
R = NB * TP
TR = 256
TQ = 544
HP = LANES
PC = 1792
O_CKV, O_KR, O_XR, O_XG = 384, 640, 768, 1280
CG = 128
N_CG = D_RNN // CG
VMEM_LIMIT = 56 * 1024 * 1024
N_CHIPS = 4
SCALE = QK_HEAD ** -0.5


def _nt(a, b):
    return lax.dot_general(a, b, (((1,), (1,)), ((), ())), preferred_element_type=F32)


def _nn(a, b):
    return jnp.dot(a, b, preferred_element_type=F32)


def _tn(a, b):
    return lax.dot_general(a, b, (((0,), (0,)), ((), ())), preferred_element_type=F32)


def _rms(x, g, n):
    ms = jnp.sum(x * x, axis=-1, keepdims=True) * (1.0 / n)
    return x * lax.rsqrt(ms + EPS) * g


def _rot_impl(x):
    lane = lax.broadcasted_iota(jnp.int32, x.shape, 1)
    left = pltpu.roll(x, HP - 16, 1)
    right = pltpu.roll(x, 16, 1)
    lo = (lane >= QK_NOPE) & (lane < QK_NOPE + 16)
    hi = (lane >= QK_NOPE + 16) & (lane < QK_HEAD)
    return jnp.where(lo, -left, jnp.where(hi, right, 0.0))


@jax.custom_vjp
def _rot(x):
    return _rot_impl(x)


def _rot_fwd(x):
    return _rot_impl(x), None


def _rot_bwd(_, g):
    return (-_rot_impl(g),)


_rot.defvjp(_rot_fwd, _rot_bwd)


def _head(x, g, cs, sn):
    n = _rms(x, g, QK_HEAD)
    return n * cs + _rot(n) * sn


def _const_spec(shape):
    return pl.BlockSpec(shape, lambda *_: (0,) * len(shape), pipeline_mode=pl.Buffered(1))


def _row_spec(n, tr=TR):
    return pl.BlockSpec((tr, n), lambda i: (i, 0))


def _params(*sem):
    return pltpu.CompilerParams(dimension_semantics=sem, vmem_limit_bytes=VMEM_LIMIT)


def _stage_a_fwd(hp, cs, sn, cw):
    def body(hp_ref, cs_ref, sn_ref, ln1, win, qag, wq, kvag, wk, wv, qg, kg,
             pa_ref, xr_ref, xg_ref, q_ref, k_ref, v_ref):
        hn = _rms(hp_ref[...], ln1[...], D).astype(BF)
        p = _nt(hn, win[...])
        pa_ref[...] = p[:, :O_XR]
        xr_ref[...] = p[:, O_XR:O_XG]
        xg_ref[...] = p[:, O_XG:]
        cqn = _rms(p[:, :O_CKV], qag[...], Q_LORA).astype(BF)
        ckvn = _rms(p[:, O_CKV:O_KR], kvag[...], KV_LORA).astype(BF)
        kr = p[:, O_KR:O_XR]
        c, s = cs_ref[...], sn_ref[...]
        for h in range(N_HEADS):
            sl = slice(h * HP, (h + 1) * HP)
            q_ref[:, sl] = _head(_nt(cqn, wq[h]), qg[...], c, s).astype(BF)
            k_ref[:, sl] = _head(_nt(ckvn, wk[h]) + kr, kg[...], c, s).astype(BF)
        v_ref[...] = _nt(ckvn, wv[...]).astype(BF)

    return pl.pallas_call(
        body, grid=(R // TR,), name="stage_a_fwd",
        in_specs=[_row_spec(D), _row_spec(HP), _row_spec(HP), _const_spec((1, D)), _const_spec((PC, D)),
                  _const_spec((1, Q_LORA)), _const_spec((N_HEADS, HP, Q_LORA)), _const_spec((1, KV_LORA)),
                  _const_spec((N_HEADS, HP, KV_LORA)), _const_spec((D_ATTN, KV_LORA)), _const_spec((1, HP)),
                  _const_spec((1, HP))],
        out_specs=[_row_spec(O_XR), _row_spec(D_RNN), _row_spec(D_RNN), _row_spec(N_HEADS * HP),
                   _row_spec(N_HEADS * HP), _row_spec(D_ATTN)],
        out_shape=[jax.ShapeDtypeStruct((R, O_XR), F32), jax.ShapeDtypeStruct((R, D_RNN), F32),
                   jax.ShapeDtypeStruct((R, D_RNN), F32), jax.ShapeDtypeStruct((R, N_HEADS * HP), BF),
                   jax.ShapeDtypeStruct((R, N_HEADS * HP), BF), jax.ShapeDtypeStruct((R, D_ATTN), BF)],
        compiler_params=_params("arbitrary"),
    )(hp, cs, sn, cw["ln1_g"], cw["win"], cw["qa_g"], cw["wq"], cw["kva_g"], cw["wk"], cw["wv"], cw["q_g"], cw["k_g"])


def _stage_a_bwd(dq, dk, dv, dxr, dxg, dh1, hp, pa, cs, sn, cw):
    def body(dq_ref, dk_ref, dv_ref, dxr_ref, dxg_ref, dh1_ref, hp_ref, pa_ref, cs_ref, sn_ref,
             ln1, win, qag, wq, kvag, wk, wv, qg, kg,
             dhp_ref, dp_ref, dqraw_ref, dkraw_ref, hn_ref, cqn_ref, ckvn_ref,
             dln1_ref, dqag_ref, dkvag_ref, dqg_ref, dkg_ref):
        @pl.when(pl.program_id(0) == 0)
        def _():
            for r in (dln1_ref, dqag_ref, dkvag_ref, dqg_ref, dkg_ref):
                r[...] = jnp.zeros_like(r)

        hn, vjp_ln1 = jax.vjp(lambda h, g: _rms(h, g, D), hp_ref[...], ln1[...])
        hn_ref[...] = hn.astype(BF)
        pa_v = pa_ref[...]
        cqn, vjp_qa = jax.vjp(lambda x, g: _rms(x, g, Q_LORA), pa_v[:, :O_CKV], qag[...])
        ckvn, vjp_kva = jax.vjp(lambda x, g: _rms(x, g, KV_LORA), pa_v[:, O_CKV:O_KR], kvag[...])
        kr = pa_v[:, O_KR:O_XR]
        cqnb, ckvnb = cqn.astype(BF), ckvn.astype(BF)
        cqn_ref[...] = cqnb
        ckvn_ref[...] = ckvnb
        c, s = cs_ref[...], sn_ref[...]
        lane = lax.broadcasted_iota(jnp.int32, (1, HP), 1)
        rope_lanes = ((lane >= QK_NOPE) & (lane < QK_HEAD)).astype(F32)
        dcqn = jnp.zeros((TR, Q_LORA), F32)
        dckvn = jnp.zeros((TR, KV_LORA), F32)
        dkr = jnp.zeros((TR, HP), F32)
        dqg = jnp.zeros((1, HP), F32)
        dkg = jnp.zeros((1, HP), F32)
        for h in range(N_HEADS):
            sl = slice(h * HP, (h + 1) * HP)
            _, vjp_q = jax.vjp(lambda x, g: _head(x, g, c, s), _nt(cqnb, wq[h]), qg[...])
            dqraw, dg = vjp_q(dq_ref[:, sl])
            dqg = dqg + dg
            dqb = dqraw.astype(BF)
            dqraw_ref[:, sl] = dqb
            dcqn = dcqn + _nn(dqb, wq[h])
            _, vjp_k = jax.vjp(lambda x, g: _head(x, g, c, s), _nt(ckvnb, wk[h]) + kr, kg[...])
            dkraw, dg = vjp_k(dk_ref[:, sl])
            dkg = dkg + dg
            dkb = dkraw.astype(BF)
            dkraw_ref[:, sl] = dkb
            dckvn = dckvn + _nn(dkb, wk[h])
            dkr = dkr + dkraw * rope_lanes
        dckvn = dckvn + _nn(dv_ref[...].astype(BF), wv[...])
        dcq, dqag = vjp_qa(dcqn)
        dckv, dkvag = vjp_kva(dckvn)
        dpb = jnp.concatenate([dcq, dckv, dkr, dxr_ref[...], dxg_ref[...]], axis=1).astype(BF)
        dp_ref[...] = dpb
        dh, dln1 = vjp_ln1(_nn(dpb, win[...]))
        dhp_ref[...] = dh + dh1_ref[...]
        dln1_ref[...] += dln1
        dqag_ref[...] += dqag
        dkvag_ref[...] += dkvag
        dqg_ref[...] += dqg
        dkg_ref[...] += dkg

    acc = lambda n: pl.BlockSpec((1, n), lambda i: (0, 0))
    return pl.pallas_call(
        body, grid=(R // TR,), name="stage_a_bwd",
        in_specs=[_row_spec(N_HEADS * HP), _row_spec(N_HEADS * HP), _row_spec(D_ATTN), _row_spec(D_RNN),
                  _row_spec(D_RNN), _row_spec(D), _row_spec(D), _row_spec(O_XR), _row_spec(HP), _row_spec(HP),
                  _const_spec((1, D)), _const_spec((PC, D)), _const_spec((1, Q_LORA)),
                  _const_spec((N_HEADS, HP, Q_LORA)), _const_spec((1, KV_LORA)),
                  _const_spec((N_HEADS, HP, KV_LORA)), _const_spec((D_ATTN, KV_LORA)), _const_spec((1, HP)),
                  _const_spec((1, HP))],
        out_specs=[_row_spec(D), _row_spec(PC), _row_spec(N_HEADS * HP), _row_spec(N_HEADS * HP), _row_spec(D),
                   _row_spec(Q_LORA), _row_spec(KV_LORA), acc(D), acc(Q_LORA), acc(KV_LORA), acc(HP), acc(HP)],
        out_shape=[jax.ShapeDtypeStruct((R, D), F32), jax.ShapeDtypeStruct((R, PC), BF),
                   jax.ShapeDtypeStruct((R, N_HEADS * HP), BF), jax.ShapeDtypeStruct((R, N_HEADS * HP), BF),
                   jax.ShapeDtypeStruct((R, D), BF), jax.ShapeDtypeStruct((R, Q_LORA), BF),
                   jax.ShapeDtypeStruct((R, KV_LORA), BF), jax.ShapeDtypeStruct((1, D), F32),
                   jax.ShapeDtypeStruct((1, Q_LORA), F32), jax.ShapeDtypeStruct((1, KV_LORA), F32),
                   jax.ShapeDtypeStruct((1, HP), F32), jax.ShapeDtypeStruct((1, HP), F32)],
        compiler_params=_params("arbitrary"),
    )(dq, dk, dv, dxr, dxg, dh1, hp, pa, cs, sn, cw["ln1_g"], cw["win"], cw["qa_g"], cw["wq"], cw["kva_g"],
      cw["wk"], cw["wv"], cw["q_g"], cw["k_g"])


def _attn_head(q, k, v2, half):
    s = _nt(q.astype(BF), k.astype(BF)) * SCALE
    col = lax.broadcasted_iota(jnp.int32, s.shape, 1)
    s = jnp.where(col < T, s, -1e30)
    m = lax.stop_gradient(jnp.max(s, axis=-1, keepdims=True))
    e = jnp.exp(s - m)
    p = e / jnp.sum(e, axis=-1, keepdims=True)
    lane = lax.broadcasted_iota(jnp.int32, (1, 2 * V_HEAD), 1)
    hm = ((lane >= V_HEAD) == (half == 1)).astype(F32)
    return _nn(p.astype(BF), (v2 * hm).astype(BF))


_ATTN_GRID = (NB, N_HEADS // 2, TP // TQ)
_Q_SPEC = pl.BlockSpec((TQ, 2 * HP), lambda b, j, i: (b * (TP // TQ) + i, j))
_K_SPEC = pl.BlockSpec((TP, 2 * HP), lambda b, j, i: (b, j))
_V_SPEC = pl.BlockSpec((TP, 2 * V_HEAD), lambda b, j, i: (b, j))
_O_SPEC = pl.BlockSpec((TQ, 2 * V_HEAD), lambda b, j, i: (b * (TP // TQ) + i, j))


def _attn_fwd(q, k, v):
    def body(q_ref, k_ref, v_ref, o_ref):
        v2 = v_ref[...].astype(F32)
        o = jnp.zeros((TQ, 2 * V_HEAD), F32)
        for hh in range(2):
            sl = slice(hh * HP, (hh + 1) * HP)
            o = o + _attn_head(q_ref[:, sl], k_ref[:, sl], v2, hh)
        o_ref[...] = o

    return pl.pallas_call(
        body, grid=_ATTN_GRID, name="attn_fwd", in_specs=[_Q_SPEC, _K_SPEC, _V_SPEC], out_specs=_O_SPEC,
        out_shape=jax.ShapeDtypeStruct((R, D_ATTN), F32),
        compiler_params=_params("arbitrary", "arbitrary", "arbitrary"),
    )(q, k, v)


def _attn_bwd(q, k, v, do):
    def body(q_ref, k_ref, v_ref, do_ref, dq_ref, dk_ref, dv_ref):
        @pl.when(pl.program_id(2) == 0)
        def _():
            dk_ref[...] = jnp.zeros_like(dk_ref)
            dv_ref[...] = jnp.zeros_like(dv_ref)

        do = do_ref[...]
        v2 = v_ref[...].astype(F32)
        dv_sum = jnp.zeros((TP, 2 * V_HEAD), F32)
        for hh in range(2):
            sl = slice(hh * HP, (hh + 1) * HP)
            _, vjp = jax.vjp(functools.partial(_attn_head, half=hh), q_ref[:, sl].astype(F32),
                             k_ref[:, sl].astype(F32), v2)
            dqh, dkh, dvh = vjp(do)
            dq_ref[:, sl] = dqh
            dk_ref[:, sl] += dkh
            dv_sum = dv_sum + dvh
        dv_ref[...] += dv_sum

    return pl.pallas_call(
        body, grid=_ATTN_GRID, name="attn_bwd", in_specs=[_Q_SPEC, _K_SPEC, _V_SPEC, _O_SPEC],
        out_specs=[_Q_SPEC, _K_SPEC, _V_SPEC],
        out_shape=[jax.ShapeDtypeStruct((R, N_HEADS * HP), F32), jax.ShapeDtypeStruct((R, N_HEADS * HP), F32),
                   jax.ShapeDtypeStruct((R, D_ATTN), F32)],
        compiler_params=_params("arbitrary", "arbitrary", "arbitrary"),
    )(q, k, v, do)


def _scan(a_ref, b_ref, h_ref, reverse):
    n_tiles = TP // 8
    row = lax.broadcasted_iota(jnp.int32, (8, CG), 0)

    def step(i, carry):
        j = n_tiles - 1 - i if reverse else i
        r = pl.multiple_of(j * 8, 8)
        a = a_ref[pl.ds(r, 8), :]
        b = b_ref[pl.ds(r, 8), :]
        for s in (1, 2, 4):
            shift = 8 - s if reverse else s
            keep = (row < 8 - s) if reverse else (row >= s)
            b = jnp.where(keep, a * pltpu.roll(b, shift, 0) + b, b)
            a = jnp.where(keep, a * pltpu.roll(a, shift, 0), a)
        h = b + a * carry
        h_ref[pl.ds(r, 8), :] = h
        edge = h_ref[pl.ds(r, 1), :] if reverse else h_ref[pl.ds(r + 7, 1), :]
        return edge

    lax.fori_loop(0, n_tiles, step, jnp.zeros((1, CG), F32))


def _shifts(x):
    t = lax.broadcasted_iota(jnp.int32, x.shape, 0)
    xm2 = jnp.where(t >= 2, pltpu.roll(x, 2, 0), 0.0)
    xm1 = jnp.where(t >= 1, pltpu.roll(x, 1, 0), 0.0)
    xp1 = jnp.where(t < TP - 1, pltpu.roll(x, TP - 1, 0), 0.0)
    return xm2, xm1, xp1


def _softplus(z):
    e = jnp.exp(-jnp.abs(z))
    small = e * (1.0 - e * (0.5 - e * (1.0 / 3.0)))
    return jnp.maximum(z, 0.0) + jnp.where(e < 0.01, small, jnp.log(1.0 + e))


def _neg_expm1(x):
    series = -x * (1.0 + x * 0.5 * (1.0 + x * (1.0 / 3.0) * (1.0 + x * 0.25)))
    return jnp.where(x > -0.05, series, 1.0 - jnp.exp(x))


def _gates(row0, xc, pa_f, pi_f, pa_b, pi_b, lam_f, lam_b):
    t = row0 + lax.broadcasted_iota(jnp.int32, xc.shape, 0)
    valid = t < T
    out = []
    for pa, pi_, lam in ((pa_f, pi_f, lam_f), (pa_b, pi_b, lam_b)):
        r = jax.nn.sigmoid(pa)
        gate_i = jax.nn.sigmoid(pi_)
        log_a = -LRU_C * r * _softplus(-lam)
        a = jnp.exp(log_a)
        mult = jnp.sqrt(jnp.maximum(_neg_expm1(2.0 * log_a), 0.0))
        out += [a, jnp.where(valid, mult * (gate_i * xc), 0.0)]
    return tuple(out)


def _rnn_specs():
    seq = pl.BlockSpec((TP, CG), lambda g, b: (b, g))
    return dict(
        seq=seq,
        cw=pl.BlockSpec((CONV_W, CG), lambda g, b: (0, g)),
        cb=pl.BlockSpec((1, CG), lambda g, b: (0, g)),
        w4=pl.BlockSpec((None, CG, 4 * CG), lambda g, b: (g, 0, 0)),
        b4=pl.BlockSpec((None, 1, 4 * CG), lambda g, b: (g, 0, 0)),
        lam=pl.BlockSpec((None, 1, 2 * CG), lambda g, b: (g, 0, 0)),
    )


def _conv(x, xm2, xm1, xp1, cw_ref, cb_ref):
    return cw_ref[0:1, :] * xm2 + cw_ref[1:2, :] * xm1 + cw_ref[2:3, :] * x + cw_ref[3:4, :] * xp1 + cb_ref[...]


TC = 128
N_TC = TP // TC


def _split4(pre):
    return pre[:, :CG], pre[:, CG:2 * CG], pre[:, 2 * CG:3 * CG], pre[:, 3 * CG:]


def _rnn_fwd(xr, xg, cw):
    def body(xr_ref, xg_ref, cw_ref, cb_ref, w4_ref, b4_ref, lam_ref, y_ref, hf_ref, hb_ref, xc_s, af, bf, ab, bb):
        x = xr_ref[...]
        xc_s[...] = _conv(x, *_shifts(x), cw_ref, cb_ref)
        lam = lam_ref[...]

        def chunk(i, _):
            rows = pl.ds(pl.multiple_of(i * TC, TC), TC)
            xc = xc_s[rows, :]
            pre = _nn(xc.astype(BF), w4_ref[...]) + b4_ref[...]
            a_f, b_f, a_b, b_b = _gates(i * TC, xc, *_split4(pre), lam[:, :CG], lam[:, CG:])
            af[rows, :] = a_f
            bf[rows, :] = b_f
            ab[rows, :] = a_b
            bb[rows, :] = b_b
            return 0

        lax.fori_loop(0, N_TC, chunk, 0)
        _scan(af, bf, hf_ref, False)
        _scan(ab, bb, hb_ref, True)
        y_ref[...] = (hf_ref[...] + hb_ref[...]) * jax.nn.gelu(xg_ref[...])

    sp = _rnn_specs()
    return pl.pallas_call(
        body, grid=(N_CG, NB), name="rnn_fwd",
        in_specs=[sp["seq"], sp["seq"], sp["cw"], sp["cb"], sp["w4"], sp["b4"], sp["lam"]],
        out_specs=[sp["seq"]] * 3, out_shape=[jax.ShapeDtypeStruct((R, D_RNN), F32)] * 3,
        scratch_shapes=[pltpu.VMEM((TP, CG), F32)] * 5,
        compiler_params=_params("arbitrary", "arbitrary"),
    )(xr, xg, cw["conv_w"], cw["conv_b"], cw["w4"], cw["b4"], cw["lam"])


def _rnn_bwd(dy, xr, xg, hf, hb, cw):
    def body(dy_ref, xr_ref, xg_ref, hf_ref, hb_ref, cw_ref, cb_ref, w4_ref, b4_ref, lam_ref,
             dxr_ref, dxg_ref, dcw_ref, dcb_ref, dw4_ref, db4_ref, dlam_ref,
             xc_s, af_s, ab_s, sh_s, dhs_s, lf_s, lb_s, daf_s, dab_s, dxc_s):
        @pl.when(pl.program_id(1) == 0)
        def _():
            for r in (dcw_ref, dcb_ref, dw4_ref, db4_ref, dlam_ref):
                r[...] = jnp.zeros_like(r)

        x = xr_ref[...]
        xc_s[...] = _conv(x, *_shifts(x), cw_ref, cb_ref)
        lam = lam_ref[...]

        def chunk1(i, _):
            rows = pl.ds(pl.multiple_of(i * TC, TC), TC)
            xc = xc_s[rows, :]
            pre = _nn(xc.astype(BF), w4_ref[...]) + b4_ref[...]
            a_f, _, a_b, _ = _gates(i * TC, xc, *_split4(pre), lam[:, :CG], lam[:, CG:])
            af_s[rows, :] = a_f
            ab_s[rows, :] = a_b
            _, vjp_y = jax.vjp(lambda h, g: h * jax.nn.gelu(g), hf_ref[rows, :] + hb_ref[rows, :], xg_ref[rows, :])
            dhs, dxg = vjp_y(dy_ref[rows, :])
            dhs_s[rows, :] = dhs
            dxg_ref[rows, :] = dxg
            return 0

        lax.fori_loop(0, N_TC, chunk1, 0)
        t = lax.broadcasted_iota(jnp.int32, (TP, CG), 0)
        sh_s[...] = pltpu.roll(af_s[...], TP - 1, 0)
        _scan(sh_s, dhs_s, lf_s, True)
        daf_s[...] = lf_s[...] * jnp.where(t >= 1, pltpu.roll(hf_ref[...], 1, 0), 0.0)
        sh_s[...] = pltpu.roll(ab_s[...], 1, 0)
        _scan(sh_s, dhs_s, lb_s, False)
        dab_s[...] = lb_s[...] * jnp.where(t < TP - 1, pltpu.roll(hb_ref[...], TP - 1, 0), 0.0)

        def chunk2(i, _):
            rows = pl.ds(pl.multiple_of(i * TC, TC), TC)
            xc = xc_s[rows, :]
            xcb = xc.astype(BF)
            pre = _nn(xcb, w4_ref[...]) + b4_ref[...]
            _, vjp_gates = jax.vjp(functools.partial(_gates, i * TC), xc, *_split4(pre), lam[:, :CG], lam[:, CG:])
            dxc, dpa_f, dpi_f, dpa_b, dpi_b, dlam_f, dlam_b = vjp_gates(
                (daf_s[rows, :], lf_s[rows, :], dab_s[rows, :], lb_s[rows, :]))
            dpre = jnp.concatenate([dpa_f, dpi_f, dpa_b, dpi_b], axis=1)
            dpreb = dpre.astype(BF)
            dxc_s[rows, :] = dxc + _nt(dpreb, w4_ref[...])
            dw4_ref[...] += _tn(xcb, dpreb)
            db4_ref[...] += jnp.sum(dpre, axis=0, keepdims=True)
            dlam_ref[...] += jnp.concatenate([dlam_f, dlam_b], axis=1)
            return 0

        lax.fori_loop(0, N_TC, chunk2, 0)
        dxc = dxc_s[...]
        dcb_ref[...] += jnp.sum(dxc, axis=0, keepdims=True)
        for tap, xs in enumerate(_shifts(x)[:2] + (x,) + _shifts(x)[2:]):
            dcw_ref[tap:tap + 1, :] += jnp.sum(xs * dxc, axis=0, keepdims=True)
        dxr_ref[...] = (cw_ref[0:1, :] * jnp.where(t < TP - 2, pltpu.roll(dxc, TP - 2, 0), 0.0)
                        + cw_ref[1:2, :] * jnp.where(t < TP - 1, pltpu.roll(dxc, TP - 1, 0), 0.0)
                        + cw_ref[2:3, :] * dxc
                        + cw_ref[3:4, :] * jnp.where(t >= 1, pltpu.roll(dxc, 1, 0), 0.0))

    sp = _rnn_specs()
    return pl.pallas_call(
        body, grid=(N_CG, NB), name="rnn_bwd",
        in_specs=[sp["seq"]] * 5 + [sp["cw"], sp["cb"], sp["w4"], sp["b4"], sp["lam"]],
        out_specs=[sp["seq"], sp["seq"], sp["cw"], sp["cb"], sp["w4"], sp["b4"], sp["lam"]],
        out_shape=[jax.ShapeDtypeStruct((R, D_RNN), F32), jax.ShapeDtypeStruct((R, D_RNN), F32),
                   jax.ShapeDtypeStruct((CONV_W, D_RNN), F32), jax.ShapeDtypeStruct((1, D_RNN), F32),
                   jax.ShapeDtypeStruct((N_CG, CG, 4 * CG), F32), jax.ShapeDtypeStruct((N_CG, 1, 4 * CG), F32),
                   jax.ShapeDtypeStruct((N_CG, 1, 2 * CG), F32)],
        scratch_shapes=[pltpu.VMEM((TP, CG), F32)] * 10,
        compiler_params=_params("arbitrary", "arbitrary"),
    )(dy, xr, xg, hf, hb, cw["conv_w"], cw["conv_b"], cw["w4"], cw["b4"], cw["lam"])


TD = 128


def _stage_d(hp, o, y, tgt, cw):
    def body(hp_ref, o_ref, y_ref, tgt_ref, ga, gr, wout, ln2, wg, wu, wd,
             do_ref, dy_ref, dh1_ref, mix_ref, dh1b_ref, hn2_ref, dg_ref, du_ref, act_ref, dh2b_ref,
             loss_ref, dga_ref, dgr_ref, dln2_ref):
        i = pl.program_id(0)

        @pl.when(i == 0)
        def _():
            for r in (loss_ref, dga_ref, dgr_ref, dln2_ref):
                r[...] = jnp.zeros_like(r)

        mix_a, vjp_a = jax.vjp(lambda x, g: _rms(x, g, D_ATTN), o_ref[...], ga[...])
        mix_r, vjp_r = jax.vjp(lambda x, g: _rms(x, g, D_RNN), y_ref[...], gr[...])
        mab, mrb = mix_a.astype(BF), mix_r.astype(BF)
        mix_ref[:, :D_ATTN] = mab
        mix_ref[:, D_ATTN:] = mrb
        h1 = hp_ref[...] + _nn(mab, wout[:D_ATTN, :]) + _nn(mrb, wout[D_ATTN:, :])
        hn2, vjp_ln2 = jax.vjp(lambda x, g: _rms(x, g, D), h1, ln2[...])
        hn2b = hn2.astype(BF)
        hn2_ref[...] = hn2b
        act, vjp_act = jax.vjp(lambda g, u: jax.nn.silu(g) * u, _nt(hn2b, wg[...]), _nt(hn2b, wu[...]))
        actb = act.astype(BF)
        act_ref[...] = actb
        h2 = h1 + _nn(actb, wd[...])
        row = i * TD + lax.broadcasted_iota(jnp.int32, (TD, 1), 0)
        t = jnp.where(row >= TP, row - TP, row)
        err = jnp.where((t >= N_META) & (t < T), h2 - tgt_ref[...], 0.0)
        loss_ref[...] += jnp.sum(err * err) * (0.5 / D)
        dh2b = (err * (1.0 / D)).astype(BF)
        dh2b_ref[...] = dh2b
        dg, du = vjp_act(_nt(dh2b, wd[...]))
        dgb, dub = dg.astype(BF), du.astype(BF)
        dg_ref[...] = dgb
        du_ref[...] = dub
        dh1n, dln2 = vjp_ln2(_nn(dgb, wg[...]) + _nn(dub, wu[...]))
        dh1 = err * (1.0 / D) + dh1n
        dh1_ref[...] = dh1
        dh1b = dh1.astype(BF)
        dh1b_ref[...] = dh1b
        dmix = _nt(dh1b, wout[...])
        do, dga = vjp_a(dmix[:, :D_ATTN])
        dyr, dgr = vjp_r(dmix[:, D_ATTN:])
        do_ref[...] = do
        dy_ref[...] = dyr
        dga_ref[...] += dga
        dgr_ref[...] += dgr
        dln2_ref[...] += dln2

    rs = lambda n: _row_spec(n, TD)
    acc = lambda n: pl.BlockSpec((1, n), lambda i: (0, 0))
    return pl.pallas_call(
        body, grid=(R // TD,), name="stage_d",
        in_specs=[rs(D), rs(D_ATTN), rs(D_RNN), rs(D), _const_spec((1, D_ATTN)), _const_spec((1, D_RNN)),
                  _const_spec((D, D)), _const_spec((1, D)), _const_spec((D_FF, D)), _const_spec((D_FF, D)),
                  _const_spec((D_FF, D))],
        out_specs=[rs(D_ATTN), rs(D_RNN), rs(D), rs(D), rs(D), rs(D), rs(D_FF), rs(D_FF), rs(D_FF), rs(D),
                   acc(1), acc(D_ATTN), acc(D_RNN), acc(D)],
        out_shape=[jax.ShapeDtypeStruct((R, D_ATTN), F32), jax.ShapeDtypeStruct((R, D_RNN), F32),
                   jax.ShapeDtypeStruct((R, D), F32), jax.ShapeDtypeStruct((R, D), BF),
                   jax.ShapeDtypeStruct((R, D), BF), jax.ShapeDtypeStruct((R, D), BF),
                   jax.ShapeDtypeStruct((R, D_FF), BF), jax.ShapeDtypeStruct((R, D_FF), BF),
                   jax.ShapeDtypeStruct((R, D_FF), BF), jax.ShapeDtypeStruct((R, D), BF),
                   jax.ShapeDtypeStruct((1, 1), F32), jax.ShapeDtypeStruct((1, D_ATTN), F32),
                   jax.ShapeDtypeStruct((1, D_RNN), F32), jax.ShapeDtypeStruct((1, D), F32)],
        compiler_params=_params("arbitrary"),
    )(hp, o, y, tgt, cw["ga"], cw["gr"], cw["wout"], cw["ln2_g"], cw["wg"], cw["wu"], cw["wd"])


TW = 544


def _wgrad(a, b, name, tk=None):
    ka, nb = a.shape[1], b.shape[1]
    tk = ka if tk is None else tk

    def body(a_ref, b_ref, o_ref):
        @pl.when(pl.program_id(1) == 0)
        def _():
            o_ref[...] = jnp.zeros_like(o_ref)

        o_ref[...] += _tn(a_ref[...].astype(BF), b_ref[...].astype(BF))

    return pl.pallas_call(
        body, grid=(ka // tk, R // TW), name=name,
        in_specs=[pl.BlockSpec((TW, tk), lambda k, r: (r, k)), pl.BlockSpec((TW, nb), lambda k, r: (r, 0))],
        out_specs=pl.BlockSpec((tk, nb), lambda k, r: (k, 0)),
        out_shape=jax.ShapeDtypeStruct((ka, nb), F32),
        compiler_params=_params("arbitrary", "arbitrary"),
    )(a, b)


def _rope_tables():
    half = QK_ROPE // 2
    freqs = 1.0 / (ROPE_THETA ** (jnp.arange(half, dtype=F32) / half))
    ang = jnp.arange(TP, dtype=F32)[:, None] * freqs[None, :]
    ones = jnp.ones((TP, QK_NOPE), F32)
    zeros = jnp.zeros((TP, QK_NOPE), F32)
    pad1 = jnp.ones((TP, HP - QK_HEAD), F32)
    pad0 = jnp.zeros((TP, HP - QK_HEAD), F32)
    cs = jnp.concatenate([ones, jnp.cos(ang), jnp.cos(ang), pad1], axis=1)
    sn = jnp.concatenate([zeros, jnp.sin(ang), jnp.sin(ang), pad0], axis=1)
    return jnp.tile(cs, (NB, 1)), jnp.tile(sn, (NB, 1))


def _pad_rows(a, lo, hi):
    return jnp.pad(a, ((0, 0), (lo, hi), (0, 0)))


def _compute_weights(w):
    win_t = w["w_in_t"]
    kr = win_t[O_KR:O_KR + QK_ROPE]
    win = jnp.concatenate([win_t[:O_KR], jnp.zeros((QK_NOPE, D), F32), kr,
                           jnp.zeros((HP - QK_HEAD, D), F32), win_t[O_KR + QK_ROPE:]], axis=0)
    wq = _pad_rows(w["w_uq_t"].reshape(N_HEADS, QK_HEAD, Q_LORA), 0, HP - QK_HEAD)
    wkv = w["w_ukv_t"].reshape(N_HEADS, QK_NOPE + V_HEAD, KV_LORA)
    wk = _pad_rows(wkv[:, :QK_NOPE], 0, HP - QK_NOPE)
    wv = wkv[:, QK_NOPE:].reshape(D_ATTN, KV_LORA)
    gates = jnp.stack([w["lru_wa"][0], w["lru_wi"][0], w["lru_wa"][1], w["lru_wi"][1]])
    blk = gates.reshape(4, N_CG, 2, RNN_BW, RNN_BW)
    dense = jnp.einsum("tcaij,ab->tcaibj", blk, jnp.eye(2, dtype=F32)).reshape(4, N_CG, CG, CG)
    w4 = dense.transpose(1, 2, 0, 3).reshape(N_CG, CG, 4 * CG)
    bias = jnp.stack([w["lru_ba"][0], w["lru_bi"][0], w["lru_ba"][1], w["lru_bi"][1]])
    b4 = bias.reshape(4, N_CG, CG).transpose(1, 0, 2).reshape(N_CG, 1, 4 * CG)
    lam = w["lru_lambda"].reshape(2, N_CG, CG).transpose(1, 0, 2).reshape(N_CG, 1, 2 * CG)
    pad_g = lambda g: jnp.pad(g.reshape(1, QK_HEAD), ((0, 0), (0, HP - QK_HEAD)))
    return dict(
        ln1_g=w["ln1_g"].reshape(1, D), win=win.astype(BF), qa_g=w["q_a_norm_g"].reshape(1, Q_LORA),
        wq=wq.astype(BF), kva_g=w["kv_a_norm_g"].reshape(1, KV_LORA), wk=wk.astype(BF), wv=wv.astype(BF),
        q_g=pad_g(w["q_norm_g"]), k_g=pad_g(w["k_norm_g"]),
        conv_w=w["conv_w"].reshape(CONV_W, D_RNN), conv_b=w["conv_b"].reshape(1, D_RNN),
        w4=w4.astype(BF), b4=b4, lam=lam,
        ga=w["attn_out_g"].reshape(1, D_ATTN), gr=w["rnn_out_g"].reshape(1, D_RNN), wout=w["w_out"].astype(BF),
        ln2_g=w["ln2_g"].reshape(1, D), wg=w["w_gate_t"].astype(BF), wu=w["w_up_t"].astype(BF),
        wd=w["w_down"].astype(BF),
    )


def _local_step(x, target, meta, w):
    cw = _compute_weights(w)
    cs, sn = _rope_tables()
    hp = jnp.concatenate([jnp.broadcast_to(meta[None], (NB, N_META, D)), x,
                          jnp.zeros((NB, TP - T, D), F32)], axis=1).reshape(R, D)
    tgt = _pad_rows(target, N_META, TP - T).reshape(R, D)

    pa, xr, xg, q, k, v = _stage_a_fwd(hp, cs, sn, cw)
    o = _attn_fwd(q, k, v)
    y, hf, hb = _rnn_fwd(xr, xg, cw)
    (do, dy, dh1, mixb, dh1b, hn2b, dgb, dub, actb, dh2b, loss, dga, dgr, dln2) = _stage_d(hp, o, y, tgt, cw)
    dxr, dxg, dcw, dcb, dw4, db4, dlam = _rnn_bwd(dy, xr, xg, hf, hb, cw)
    dq, dk, dv = _attn_bwd(q, k, v, do)
    (dhp, dpb, dqrawb, dkrawb, hn1b, cqnb, ckvnb, dln1, dqag, dkvag, dqg, dkg) = _stage_a_bwd(
        dq, dk, dv, dxr, dxg, dh1, hp, pa, cs, sn, cw)

    dwin = _wgrad(dpb, hn1b, "wgrad_in", tk=PC // 2)
    dwq = _wgrad(dqrawb, cqnb, "wgrad_uq")
    dwk = _wgrad(dkrawb, ckvnb, "wgrad_uk")
    dwv = _wgrad(dv, ckvnb, "wgrad_uv")
    dwout = _wgrad(mixb, dh1b, "wgrad_out")
    dwg = _wgrad(dgb, hn2b, "wgrad_gate", tk=D_FF // 2)
    dwu = _wgrad(dub, hn2b, "wgrad_up", tk=D_FF // 2)
    dwd = _wgrad(actb, dh2b, "wgrad_down", tk=D_FF // 2)

    dwin_t = jnp.concatenate([dwin[:O_KR], dwin[O_KR + QK_NOPE:O_KR + QK_HEAD], dwin[O_XR:]], axis=0)
    dwq_t = dwq.reshape(N_HEADS, HP, Q_LORA)[:, :QK_HEAD].reshape(N_HEADS * QK_HEAD, Q_LORA)
    dwkv_t = jnp.concatenate([dwk.reshape(N_HEADS, HP, KV_LORA)[:, :QK_NOPE],
                              dwv.reshape(N_HEADS, V_HEAD, KV_LORA)], axis=1).reshape(2 * D_ATTN, KV_LORA)
    d4 = dw4.reshape(N_CG, 2, RNN_BW, 4, 2, RNN_BW)
    dgates = jnp.stack([d4[:, 0, :, :, 0, :], d4[:, 1, :, :, 1, :]], axis=1)
    dgates = dgates.transpose(3, 0, 1, 2, 4).reshape(4, N_HEADS, RNN_BW, RNN_BW)
    dbias = db4.reshape(N_CG, 4, CG).transpose(1, 0, 2).reshape(4, D_RNN)
    dhp3 = dhp.reshape(NB, TP, D)
    grads = dict(
        meta_tokens=jnp.sum(dhp3[:, :N_META], axis=0),
        ln1_g=dln1, w_in_t=dwin_t, q_a_norm_g=dqag, w_uq_t=dwq_t, kv_a_norm_g=dkvag, w_ukv_t=dwkv_t,
        q_norm_g=dqg[:, :QK_HEAD], k_norm_g=dkg[:, :QK_HEAD], conv_w=dcw[None], conv_b=dcb,
        lru_wa=jnp.stack([dgates[0], dgates[2]])[None], lru_ba=jnp.stack([dbias[0], dbias[2]])[None],
        lru_wi=jnp.stack([dgates[1], dgates[3]])[None], lru_bi=jnp.stack([dbias[1], dbias[3]])[None],
        lru_lambda=dlam.reshape(N_CG, 2, CG).transpose(1, 0, 2).reshape(1, 2, D_RNN),
        attn_out_g=dga, rnn_out_g=dgr, w_out=dwout, ln2_g=dln2, w_gate_t=dwg, w_up_t=dwu, w_down=dwd,
    )
    return loss[0, 0], dhp3[:, N_META:T], grads


_ANY = pl.BlockSpec(memory_space=pl.ANY)


def _place():
    return lax.axis_index("x"), lax.axis_index("y"), lax.axis_index("c")


def _other_chips(x, y):
    return [(1 - x, y), (x, 1 - y), (1 - x, 1 - y)]


def _all_gather(xs, name):
    m, n = xs.shape

    def body(x_ref, out_ref, send_sems, recv_sems, local_sem):
        x, y, c = _place()
        me, sibling = (x, y, c), (x, y, 1 - c)
        chips = _other_chips(x, y)

        def rows(px, py, pc):
            return out_ref.at[pl.ds((4 * px + 2 * py + pc) * m, m), :]

        def copy(k, block, to, src=None):
            return pltpu.make_async_remote_copy(
                src_ref=rows(*block) if src is None else src, dst_ref=rows(*block),
                send_sem=send_sems.at[k], recv_sem=recv_sems.at[k], device_id=to, device_id_type=MESH)

        mine = pltpu.make_async_copy(x_ref, rows(*me), local_sem)
        mine.start()
        first = [copy(0, me, sibling, src=x_ref)]
        first += [copy(1 + j, me, (*chip, c), src=x_ref) for j, chip in enumerate(chips)]
        for cp in first:
            cp.start()
        passed = [copy(4 + j, (*chip, c), sibling) for j, chip in enumerate(chips)]
        for j, chip in enumerate(chips):
            copy(1 + j, (*chip, c), me).wait_recv()
            passed[j].start()
        copy(0, sibling, me).wait_recv()
        for j, chip in enumerate(chips):
            copy(4 + j, (*chip, 1 - c), me).wait_recv()
        for cp in first + passed:
            cp.wait_send()
        mine.wait()

    return pl.pallas_call(
        body, name=name, out_shape=jax.ShapeDtypeStruct((8 * m, n), xs.dtype), in_specs=[_ANY], out_specs=_ANY,
        scratch_shapes=[pltpu.SemaphoreType.DMA((7,)), pltpu.SemaphoreType.DMA((7,)), pltpu.SemaphoreType.DMA],
    )(xs)


def _pair_exchange(big, small):
    n_s, _, m, n = big.shape

    def body(big_ref, small_ref, rbig_ref, rsmall_ref, send_sems, recv_sems):
        x, y, c = _place()
        sibling = (x, y, 1 - c)
        copies = [pltpu.make_async_remote_copy(
            src_ref=big_ref.at[s, 1 - c], dst_ref=rbig_ref.at[s], send_sem=send_sems.at[s], recv_sem=recv_sems.at[s],
            device_id=sibling, device_id_type=MESH) for s in range(n_s)]
        copies.append(pltpu.make_async_remote_copy(
            src_ref=small_ref, dst_ref=rsmall_ref, send_sem=send_sems.at[n_s], recv_sem=recv_sems.at[n_s],
            device_id=sibling, device_id_type=MESH))
        for cp in copies:
            cp.start()
        for cp in copies:
            cp.wait()

    return pl.pallas_call(
        body, name="grad_pair_exchange",
        out_shape=[jax.ShapeDtypeStruct((n_s, m, n), big.dtype), jax.ShapeDtypeStruct(small.shape, small.dtype)],
        in_specs=[_ANY, _ANY], out_specs=[_ANY, _ANY],
        scratch_shapes=[pltpu.SemaphoreType.DMA((n_s + 1,)), pltpu.SemaphoreType.DMA((n_s + 1,))],
    )(big, small)


def _chip_exchange(big, small):
    _, m, n = big.shape
    ms = small.shape[0]

    def body(big_ref, small_ref, rbig_ref, rsmall_ref, send_sems, recv_sems):
        x, y, c = _place()
        copies = []
        for j, (tx, ty) in enumerate(_other_chips(x, y)):
            copies.append(pltpu.make_async_remote_copy(
                src_ref=big_ref.at[2 * tx + ty], dst_ref=rbig_ref.at[j], send_sem=send_sems.at[j],
                recv_sem=recv_sems.at[j], device_id=(tx, ty, c), device_id_type=MESH))
            copies.append(pltpu.make_async_remote_copy(
                src_ref=small_ref, dst_ref=rsmall_ref.at[j], send_sem=send_sems.at[3 + j],
                recv_sem=recv_sems.at[3 + j], device_id=(tx, ty, c), device_id_type=MESH))
        for cp in copies:
            cp.start()
        for cp in copies:
            cp.wait()

    return pl.pallas_call(
        body, name="grad_chip_exchange",
        out_shape=[jax.ShapeDtypeStruct((3, m, n), big.dtype), jax.ShapeDtypeStruct((3, ms, n), small.dtype)],
        in_specs=[_ANY, _ANY], out_specs=[_ANY, _ANY],
        scratch_shapes=[pltpu.SemaphoreType.DMA((6,)), pltpu.SemaphoreType.DMA((6,))],
    )(big, small)


def _pair_swap(a):
    def body(a_ref, r_ref, send_sem, recv_sem):
        x, y, c = _place()
        cp = pltpu.make_async_remote_copy(src_ref=a_ref, dst_ref=r_ref, send_sem=send_sem, recv_sem=recv_sem,
                                          device_id=(x, y, 1 - c), device_id_type=MESH)
        cp.start()
        cp.wait()

    return pl.pallas_call(
        body, name="grad_pair_swap", out_shape=jax.ShapeDtypeStruct(a.shape, a.dtype), in_specs=[_ANY],
        out_specs=_ANY, scratch_shapes=[pltpu.SemaphoreType.DMA, pltpu.SemaphoreType.DMA],
    )(a)


def _row_tile(rows, cap=512):
    for t in range(cap - cap % 8, 7, -8):
        if rows % t == 0:
            return t
    return rows


def _elementwise(fn, n_out, name, *arrs):
    rows, cols = arrs[0].shape
    tr = _row_tile(rows)
    n_in = len(arrs)

    def body(*refs):
        outs = fn(*[r[...] for r in refs[:n_in]])
        for r, o in zip(refs[n_in:], outs):
            r[...] = o

    spec = pl.BlockSpec((tr, cols), lambda i: (i, 0))
    return pl.pallas_call(
        body, grid=(rows // tr,), name=name, in_specs=[spec] * n_in, out_specs=[spec] * n_out,
        out_shape=[jax.ShapeDtypeStruct((rows, cols), F32)] * n_out, compiler_params=_params("arbitrary"),
    )(*arrs)


def _add2(a, b):
    return (a + b,)


def _add4(own, r0, r1, r2):
    return ((own + r2) + (r0 + r1),)


def _adamw_math(w, g, m, v):
    m = ADAM_B1 * m + (1.0 - ADAM_B1) * g
    v = ADAM_B2 * v + (1.0 - ADAM_B2) * (g * g)
    m_hat = m / (1.0 - ADAM_B1 ** ADAM_STEP)
    v_hat = v / (1.0 - ADAM_B2 ** ADAM_STEP)
    delta = -ADAM_LR * (m_hat / (jnp.sqrt(v_hat) + ADAM_EPS) + ADAM_WD * w)
    return delta, m, v


WEIGHTS = ["meta_tokens", "ln1_g", "w_in", "q_a_norm_g", "w_uq", "kv_a_norm_g", "w_ukv", "q_norm_g", "k_norm_g",
           "conv_w", "conv_b", "lru_wa", "lru_ba", "lru_wi", "lru_bi", "lru_lambda", "attn_out_g", "rnn_out_g",
           "w_out", "ln2_g", "w_gate", "w_up", "w_down"]
BIG = ["w_in", "w_uq", "w_ukv", "w_out", "w_gate", "w_up", "w_down"]
BIG_T = {"w_in": True, "w_uq": True, "w_ukv": True, "w_out": False, "w_gate": True, "w_up": True, "w_down": False}
BIG_ROWS = {"w_in": 424, "w_uq": 72, "w_ukv": 64, "w_out": 256, "w_gate": 704, "w_up": 704, "w_down": 704}
PACK_ROWS = 2944
HALF_ROWS = PACK_ROWS // 2
SMALL_SHARDED = ["meta_tokens", "conv_w", "lru_ba", "lru_bi", "lru_lambda"]
SMALL = [n for n in WEIGHTS if n not in BIG]
SMALL_PACK_ROWS = 160
SMALL_ADAM_ROWS = 144


def _big_offsets():
    off, o = {}, 0
    for n in BIG:
        off[n] = o
        o += BIG_ROWS[n]
    return off


def _to_pack_piece(name, shard):
    a = shard[0].T if BIG_T[name] else shard[0]
    return a.reshape(BIG_ROWS[name], D)


def _from_pack_piece(name, piece, shard_shape):
    _, k, n = shard_shape
    return piece.reshape(n, k).T[None] if BIG_T[name] else piece.reshape(k, n)[None]


def _flat_pack(arrs, rows):
    flat = jnp.concatenate([a.reshape(-1) for a in arrs])
    return jnp.pad(flat, (0, rows * D - flat.shape[0])).reshape(rows, D)


def _flat_unpack(pack, shapes):
    flat, out, o = pack.reshape(-1), [], 0
    for s in shapes:
        n = math.prod(s)
        out.append(flat[o:o + n].reshape(s))
        o += n
    return out


def kernel(x, meta_tokens, ln1_g, w_in, q_a_norm_g, w_uq, kv_a_norm_g, w_ukv, q_norm_g, k_norm_g, conv_w, conv_b, lru_wa, lru_ba, lru_wi, lru_bi, lru_lambda, attn_out_g, rnn_out_g, w_out, ln2_g, w_gate, w_up, w_down, loss_target, m_meta_tokens, m_ln1_g, m_w_in, m_q_a_norm_g, m_w_uq, m_kv_a_norm_g, m_w_ukv, m_q_norm_g, m_k_norm_g, m_conv_w, m_conv_b, m_lru_wa, m_lru_ba, m_lru_wi, m_lru_bi, m_lru_lambda, m_attn_out_g, m_rnn_out_g, m_w_out, m_ln2_g, m_w_gate, m_w_up, m_w_down, v_meta_tokens, v_ln1_g, v_w_in, v_q_a_norm_g, v_w_uq, v_kv_a_norm_g, v_w_ukv, v_q_norm_g, v_k_norm_g, v_conv_w, v_conv_b, v_lru_wa, v_lru_ba, v_lru_wi, v_lru_bi, v_lru_lambda, v_attn_out_g, v_rnn_out_g, v_w_out, v_ln2_g, v_w_gate, v_w_up, v_w_down):
    wts = dict(zip(WEIGHTS, (meta_tokens, ln1_g, w_in, q_a_norm_g, w_uq, kv_a_norm_g, w_ukv, q_norm_g, k_norm_g, conv_w, conv_b, lru_wa, lru_ba, lru_wi, lru_bi, lru_lambda, attn_out_g, rnn_out_g, w_out, ln2_g, w_gate, w_up, w_down)))
    mom = dict(zip(WEIGHTS, (m_meta_tokens, m_ln1_g, m_w_in, m_q_a_norm_g, m_w_uq, m_kv_a_norm_g, m_w_ukv, m_q_norm_g, m_k_norm_g, m_conv_w, m_conv_b, m_lru_wa, m_lru_ba, m_lru_wi, m_lru_bi, m_lru_lambda, m_attn_out_g, m_rnn_out_g, m_w_out, m_ln2_g, m_w_gate, m_w_up, m_w_down)))
    var = dict(zip(WEIGHTS, (v_meta_tokens, v_ln1_g, v_w_in, v_q_a_norm_g, v_w_uq, v_kv_a_norm_g, v_w_ukv, v_q_norm_g, v_k_norm_g, v_conv_w, v_conv_b, v_lru_wa, v_lru_ba, v_lru_wi, v_lru_bi, v_lru_lambda, v_attn_out_g, v_rnn_out_g, v_w_out, v_ln2_g, v_w_gate, v_w_up, v_w_down)))
    xi, yi, ci = _place()
    chip = 2 * xi + yi
    off = _big_offsets()

    pack = jnp.concatenate([_to_pack_piece(n, wts[n]) for n in BIG] + [jnp.zeros((PACK_ROWS - 2928, D), F32)], axis=0)
    half = lax.dynamic_slice_in_dim(pack.astype(BF), ci * HALF_ROWS, HALF_ROWS, axis=0)
    gw = _all_gather(half, "gather_weights").reshape(N_CHIPS, PACK_ROWS, D)
    full = {n: gw[:, off[n]:off[n] + BIG_ROWS[n]] for n in BIG}
    spack = jnp.concatenate([meta_tokens[:, :LANES], meta_tokens[:, LANES:], conv_w[0], lru_ba[0], lru_bi[0],
                             lru_lambda[0], jnp.zeros((6, LANES), F32)], axis=0)
    shalf = lax.dynamic_slice_in_dim(spack, ci * 24, 24, axis=0)
    gs = _all_gather(shalf, "gather_small").reshape(N_CHIPS, 48, LANES)
    cols = lambda a: a.transpose(1, 0, 2).reshape(a.shape[1], N_CHIPS * a.shape[2])
    meta_full = cols(jnp.concatenate([gs[:, 0:16], gs[:, 16:32]], axis=2))
    w = dict(
        w_in_t=full["w_in"].reshape(IN_COLS, D), w_uq_t=full["w_uq"].reshape(N_HEADS * QK_HEAD, Q_LORA),
        w_ukv_t=full["w_ukv"].reshape(2 * D_ATTN, KV_LORA), w_out=full["w_out"].reshape(D, D),
        w_gate_t=full["w_gate"].reshape(D_FF, D), w_up_t=full["w_up"].reshape(D_FF, D),
        w_down=full["w_down"].reshape(D_FF, D),
        ln1_g=ln1_g, q_a_norm_g=q_a_norm_g, kv_a_norm_g=kv_a_norm_g, q_norm_g=q_norm_g, k_norm_g=k_norm_g,
        conv_w=cols(gs[:, 32:36]), conv_b=conv_b, lru_wa=lru_wa[0], lru_ba=cols(gs[:, 36:38]), lru_wi=lru_wi[0],
        lru_bi=cols(gs[:, 38:40]), lru_lambda=cols(gs[:, 40:42]), attn_out_g=attn_out_g, rnn_out_g=rnn_out_g,
        ln2_g=ln2_g,
    )

    loss_local, grad_x, g = _local_step(x, loss_target, meta_full, w)
    loss = lax.psum(loss_local, ("x", "y", "c"))

    g_t = {"w_in": g["w_in_t"], "w_uq": g["w_uq_t"], "w_ukv": g["w_ukv_t"], "w_out": g["w_out"],
           "w_gate": g["w_gate_t"], "w_up": g["w_up_t"], "w_down": g["w_down"]}
    gpack = jnp.concatenate([g_t[n].reshape(N_CHIPS, BIG_ROWS[n], D) for n in BIG]
                            + [jnp.zeros((N_CHIPS, PACK_ROWS - 2928, D), F32)], axis=1)
    gpack = gpack.reshape(N_CHIPS, 2, HALF_ROWS, D)
    full_shapes = {n: wts[n].shape for n in SMALL}
    full_shapes.update(meta_tokens=(N_META, D), conv_w=(1, CONV_W, D_RNN), lru_ba=(1, 2, D_RNN), lru_bi=(1, 2, D_RNN),
                       lru_lambda=(1, 2, D_RNN))
    gsmall = _flat_pack([g[n] for n in SMALL], SMALL_PACK_ROWS)
    rbig, rsmall = _pair_exchange(gpack, gsmall)
    mine = lax.dynamic_index_in_dim(gpack, ci, axis=1, keepdims=False).reshape(N_CHIPS * HALF_ROWS, D)
    (chip_big,) = _elementwise(_add2, 1, "grad_pair_sum", mine, rbig.reshape(N_CHIPS * HALF_ROWS, D))
    (chip_small,) = _elementwise(_add2, 1, "grad_pair_sum_small", gsmall, rsmall)
    chip_big = chip_big.reshape(N_CHIPS, HALF_ROWS, D)
    xbig, xsmall = _chip_exchange(chip_big, chip_small)
    own = lax.dynamic_index_in_dim(chip_big, chip, axis=0, keepdims=False)
    (half_sum,) = _elementwise(_add4, 1, "grad_chip_sum", own, xbig[0], xbig[1], xbig[2])
    (small_sum,) = _elementwise(_add4, 1, "grad_chip_sum_small", chip_small, xsmall[0], xsmall[1], xsmall[2])
    other_half = _pair_swap(half_sum)
    gshard = jnp.where(ci == 0, jnp.concatenate([half_sum, other_half], axis=0),
                       jnp.concatenate([other_half, half_sum], axis=0))

    grads = {n: _from_pack_piece(n, gshard[off[n]:off[n] + BIG_ROWS[n]], wts[n].shape) for n in BIG}
    small_full = dict(zip(SMALL, _flat_unpack(small_sum, [full_shapes[n] for n in SMALL])))
    for n in SMALL:
        a = small_full[n]
        if n in SMALL_SHARDED:
            width = wts[n].shape[-1]
            a = lax.dynamic_slice_in_dim(a, chip * width, width, axis=a.ndim - 1)
        grads[n] = a.reshape(wts[n].shape)

    delta, new_m, new_v = {}, {}, {}
    for n in BIG:
        two_d = lambda a: a.reshape(a.shape[-2], a.shape[-1])
        d_, m_, v_ = _elementwise(_adamw_math, 3, "adamw_" + n, two_d(wts[n]), two_d(grads[n]), two_d(mom[n]), two_d(var[n]))
        delta[n], new_m[n], new_v[n] = (a.reshape(wts[n].shape) for a in (d_, m_, v_))
    packs = [_flat_pack([src[n] for n in SMALL], SMALL_ADAM_ROWS) for src in (wts, grads, mom, var)]
    outs = _elementwise(_adamw_math, 3, "adamw_small", *packs)
    for dst, o in zip((delta, new_m, new_v), outs):
        dst.update(zip(SMALL, _flat_unpack(o, [wts[n].shape for n in SMALL])))

    return (loss, grad_x, *[grads[n] for n in WEIGHTS], *[delta[n] for n in WEIGHTS],
            *[new_m[n] for n in WEIGHTS], *[new_v[n] for n in WEIGHTS])
```

```python
import functools
import math

import jax
import jax.numpy as jnp
from jax import lax
from jax.experimental import pallas as pl
from jax.experimental.pallas import tpu as pltpu

F32 = jnp.float32
BF = jnp.bfloat16
MESH = pl.DeviceIdType.MESH

D = 1024
SEQ = 2048
N_META = 16
T = N_META + SEQ
N_HEADS = 8
QK_NOPE = 64
QK_ROPE = 32
QK_HEAD = 96
V_HEAD = 64
Q_LORA = 384
KV_LORA = 256
D_ATTN = 512
D_RNN = 512
RNN_BW = 64
CONV_W = 4
LRU_C = 8.0
ROPE_THETA = 10000.0
D_FF = 2816
EPS = 1e-6
IN_COLS = 1696
ADAM_LR, ADAM_B1, ADAM_B2, ADAM_EPS, ADAM_WD, ADAM_STEP = 0.001, 0.9, 0.999, 1e-08, 0.01, 10

LANES = 128
TP = 2176
NB = 2
R = NB * TP
TR = 256
TQ = 544
HP = LANES
PC = 1792
O_CKV, O_KR, O_XR, O_XG = 384, 640, 768, 1280
CG = 128
N_CG = D_RNN // CG
VMEM_LIMIT = 56 * 1024 * 1024
N_CHIPS = 4
SCALE = QK_HEAD ** -0.5
KEY_MASK = -30000.0


def _nt(a, b):
    return lax.dot_general(a, b, (((1,), (1,)), ((), ())), preferred_element_type=F32)


def _nn(a, b):
    return jnp.dot(a, b, preferred_element_type=F32)


def _tn(a, b):
    return lax.dot_general(a, b, (((0,), (0,)), ((), ())), preferred_element_type=F32)


def _rms(x, g, n):
    ms = jnp.sum(x * x, axis=-1, keepdims=True) * (1.0 / n)
    return x * lax.rsqrt(ms + EPS) * g


def _rot_impl(x):
    lane = lax.broadcasted_iota(jnp.int32, x.shape, 1)
    left = pltpu.roll(x, HP - 16, 1)
    right = pltpu.roll(x, 16, 1)
    lo = (lane >= QK_NOPE) & (lane < QK_NOPE + 16)
    hi = (lane >= QK_NOPE + 16) & (lane < QK_HEAD)
    return jnp.where(lo, -left, jnp.where(hi, right, 0.0))


@jax.custom_vjp
def _rot(x):
    return _rot_impl(x)


def _rot_fwd(x):
    return _rot_impl(x), None


def _rot_bwd(_, g):
    return (-_rot_impl(g),)


_rot.defvjp(_rot_fwd, _rot_bwd)


def _head(x, g, cs, sn):
    n = _rms(x, g, QK_HEAD)
    return n * cs + _rot(n) * sn


def _const_spec(shape):
    return pl.BlockSpec(shape, lambda *_: (0,) * len(shape), pipeline_mode=pl.Buffered(1))


def _row_spec(n, tr=TR):
    return pl.BlockSpec((tr, n), lambda i: (i, 0))


def _params(*sem):
    return pltpu.CompilerParams(dimension_semantics=sem, vmem_limit_bytes=VMEM_LIMIT)


def _stage_a_fwd(hp, cs, sn, cw):
    def body(hp_ref, cs_ref, sn_ref, ln1, win, qag, wq, kvag, wk, wv, qg, kg,
             pa_ref, xr_ref, xg_ref, q_ref, k_ref, v_ref):
        hn = _rms(hp_ref[...], ln1[...], D).astype(BF)
        p = _nt(hn, win[...])
        pa_ref[...] = p[:, :O_XR]
        xr_ref[...] = p[:, O_XR:O_XG]
        xg_ref[...] = p[:, O_XG:]
        cqn = _rms(p[:, :O_CKV], qag[...], Q_LORA).astype(BF)
        ckvn = _rms(p[:, O_CKV:O_KR], kvag[...], KV_LORA).astype(BF)
        kr = p[:, O_KR:O_XR]
        c, s = cs_ref[...], sn_ref[...]
        mask_lane = lax.broadcasted_iota(jnp.int32, (1, HP), 1) == QK_HEAD
        row = pl.program_id(0) * TR + lax.broadcasted_iota(jnp.int32, (TR, 1), 0)
        key_mask = jnp.where(jnp.where(row >= TP, row - TP, row) < T, 0.0, KEY_MASK)
        for h in range(N_HEADS):
            sl = slice(h * HP, (h + 1) * HP)
            q_ref[:, sl] = jnp.where(mask_lane, 1.0, _head(_nt(cqn, wq[h]), qg[...], c, s)).astype(BF)
            k_ref[:, sl] = jnp.where(mask_lane, key_mask, _head(_nt(ckvn, wk[h]) + kr, kg[...], c, s)).astype(BF)
        v_ref[...] = _nt(ckvn, wv[...]).astype(BF)

    return pl.pallas_call(
        body, grid=(R // TR,), name="stage_a_fwd",
        in_specs=[_row_spec(D), _row_spec(HP), _row_spec(HP), _const_spec((1, D)), _const_spec((PC, D)),
                  _const_spec((1, Q_LORA)), _const_spec((N_HEADS, HP, Q_LORA)), _const_spec((1, KV_LORA)),
                  _const_spec((N_HEADS, HP, KV_LORA)), _const_spec((D_ATTN, KV_LORA)), _const_spec((1, HP)),
                  _const_spec((1, HP))],
        out_specs=[_row_spec(O_XR), _row_spec(D_RNN), _row_spec(D_RNN), _row_spec(N_HEADS * HP),
                   _row_spec(N_HEADS * HP), _row_spec(D_ATTN)],
        out_shape=[jax.ShapeDtypeStruct((R, O_XR), F32), jax.ShapeDtypeStruct((R, D_RNN), F32),
                   jax.ShapeDtypeStruct((R, D_RNN), F32), jax.ShapeDtypeStruct((R, N_HEADS * HP), BF),
                   jax.ShapeDtypeStruct((R, N_HEADS * HP), BF), jax.ShapeDtypeStruct((R, D_ATTN), BF)],
        compiler_params=_params("arbitrary"),
    )(hp, cs, sn, cw["ln1_g"], cw["win"], cw["qa_g"], cw["wq"], cw["kva_g"], cw["wk"], cw["wv"], cw["q_g"], cw["k_g"])


def _stage_a_bwd(dq, dk, dv, dxr, dxg, dh1, hp, pa, cs, sn, cw):
    def body(dq_ref, dk_ref, dv_ref, dxr_ref, dxg_ref, dh1_ref, hp_ref, pa_ref, cs_ref, sn_ref,
             ln1, win, qag, wq, kvag, wk, wv, qg, kg,
             dhp_ref, dp_ref, dqraw_ref, dkraw_ref, hn_ref, cqn_ref, ckvn_ref,
             dln1_ref, dqag_ref, dkvag_ref, dqg_ref, dkg_ref):
        @pl.when(pl.program_id(0) == 0)
        def _():
            for r in (dln1_ref, dqag_ref, dkvag_ref, dqg_ref, dkg_ref):
                r[...] = jnp.zeros_like(r)

        hn, vjp_ln1 = jax.vjp(lambda h, g: _rms(h, g, D), hp_ref[...], ln1[...])
        hn_ref[...] = hn.astype(BF)
        pa_v = pa_ref[...]
        cqn, vjp_qa = jax.vjp(lambda x, g: _rms(x, g, Q_LORA), pa_v[:, :O_CKV], qag[...])
        ckvn, vjp_kva = jax.vjp(lambda x, g: _rms(x, g, KV_LORA), pa_v[:, O_CKV:O_KR], kvag[...])
        kr = pa_v[:, O_KR:O_XR]
        cqnb, ckvnb = cqn.astype(BF), ckvn.astype(BF)
        cqn_ref[...] = cqnb
        ckvn_ref[...] = ckvnb
        c, s = cs_ref[...], sn_ref[...]
        lane = lax.broadcasted_iota(jnp.int32, (1, HP), 1)
        rope_lanes = ((lane >= QK_NOPE) & (lane < QK_HEAD)).astype(F32)
        dcqn = jnp.zeros((TR, Q_LORA), F32)
        dckvn = jnp.zeros((TR, KV_LORA), F32)
        dkr = jnp.zeros((TR, HP), F32)
        dqg = jnp.zeros((1, HP), F32)
        dkg = jnp.zeros((1, HP), F32)
        for h in range(N_HEADS):
            sl = slice(h * HP, (h + 1) * HP)
            _, vjp_q = jax.vjp(lambda x, g: _head(x, g, c, s), _nt(cqnb, wq[h]), qg[...])
            dqraw, dg = vjp_q(dq_ref[:, sl])
            dqg = dqg + dg
            dqb = dqraw.astype(BF)
            dqraw_ref[:, sl] = dqb
            dcqn = dcqn + _nn(dqb, wq[h])
            _, vjp_k = jax.vjp(lambda x, g: _head(x, g, c, s), _nt(ckvnb, wk[h]) + kr, kg[...])
            dkraw, dg = vjp_k(dk_ref[:, sl])
            dkg = dkg + dg
            dkb = dkraw.astype(BF)
            dkraw_ref[:, sl] = dkb
            dckvn = dckvn + _nn(dkb, wk[h])
            dkr = dkr + dkraw * rope_lanes
        dckvn = dckvn + _nn(dv_ref[...].astype(BF), wv[...])
        dcq, dqag = vjp_qa(dcqn)
        dckv, dkvag = vjp_kva(dckvn)
        dpb = jnp.concatenate([dcq, dckv, dkr, dxr_ref[...], dxg_ref[...]], axis=1).astype(BF)
        dp_ref[...] = dpb
        dh, dln1 = vjp_ln1(_nn(dpb, win[...]))
        dhp_ref[...] = dh + dh1_ref[...]
        dln1_ref[...] += dln1
        dqag_ref[...] += dqag
        dkvag_ref[...] += dkvag
        dqg_ref[...] += dqg
        dkg_ref[...] += dkg

    acc = lambda n: pl.BlockSpec((1, n), lambda i: (0, 0))
    return pl.pallas_call(
        body, grid=(R // TR,), name="stage_a_bwd",
        in_specs=[_row_spec(N_HEADS * HP), _row_spec(N_HEADS * HP), _row_spec(D_ATTN), _row_spec(D_RNN),
                  _row_spec(D_RNN), _row_spec(D), _row_spec(D), _row_spec(O_XR), _row_spec(HP), _row_spec(HP),
                  _const_spec((1, D)), _const_spec((PC, D)), _const_spec((1, Q_LORA)),
                  _const_spec((N_HEADS, HP, Q_LORA)), _const_spec((1, KV_LORA)),
                  _const_spec((N_HEADS, HP, KV_LORA)), _const_spec((D_ATTN, KV_LORA)), _const_spec((1, HP)),
                  _const_spec((1, HP))],
        out_specs=[_row_spec(D), _row_spec(PC), _row_spec(N_HEADS * HP), _row_spec(N_HEADS * HP), _row_spec(D),
                   _row_spec(Q_LORA), _row_spec(KV_LORA), acc(D), acc(Q_LORA), acc(KV_LORA), acc(HP), acc(HP)],
        out_shape=[jax.ShapeDtypeStruct((R, D), F32), jax.ShapeDtypeStruct((R, PC), BF),
                   jax.ShapeDtypeStruct((R, N_HEADS * HP), BF), jax.ShapeDtypeStruct((R, N_HEADS * HP), BF),
                   jax.ShapeDtypeStruct((R, D), BF), jax.ShapeDtypeStruct((R, Q_LORA), BF),
                   jax.ShapeDtypeStruct((R, KV_LORA), BF), jax.ShapeDtypeStruct((1, D), F32),
                   jax.ShapeDtypeStruct((1, Q_LORA), F32), jax.ShapeDtypeStruct((1, KV_LORA), F32),
                   jax.ShapeDtypeStruct((1, HP), F32), jax.ShapeDtypeStruct((1, HP), F32)],
        compiler_params=_params("arbitrary"),
    )(dq, dk, dv, dxr, dxg, dh1, hp, pa, cs, sn, cw["ln1_g"], cw["win"], cw["qa_g"], cw["wq"], cw["kva_g"],
      cw["wk"], cw["wv"], cw["q_g"], cw["k_g"])


def _head_mask(half, dtype):
    lane = lax.broadcasted_iota(jnp.int32, (1, 2 * V_HEAD), 1)
    return ((lane >= V_HEAD) == (half == 1)).astype(dtype)


_ATTN_GRID = (NB, N_HEADS // 2, TP // TQ)
_Q_SPEC = pl.BlockSpec((TQ, 2 * HP), lambda b, j, i: (b * (TP // TQ) + i, j))
_K_SPEC = pl.BlockSpec((TP, 2 * HP), lambda b, j, i: (b, j))
_V_SPEC = pl.BlockSpec((TP, 2 * V_HEAD), lambda b, j, i: (b, j))
_O_SPEC = pl.BlockSpec((TQ, 2 * V_HEAD), lambda b, j, i: (b * (TP // TQ) + i, j))
_LSE_SPEC = pl.BlockSpec((None, TQ, 2), lambda b, j, i: (j, b * (TP // TQ) + i, 0))


def _attn_fwd(q, k, v):
    def body(q_ref, k_ref, v_ref, o_ref, lse_ref):
        v2 = v_ref[...]
        o = jnp.zeros((TQ, 2 * V_HEAD), F32)
        lse = []
        for hh in range(2):
            sl = slice(hh * HP, (hh + 1) * HP)
            s = _nt(q_ref[:, sl], k_ref[:, sl]) * SCALE
            m = jnp.max(s, axis=-1, keepdims=True)
            e = jnp.exp(s - m)
            l = jnp.sum(e, axis=-1, keepdims=True)
            p = e * (1.0 / l)
            o = o + _nn(p.astype(BF), v2 * _head_mask(hh, BF))
            lse.append(m + jnp.log(l))
        o_ref[...] = o
        lane = lax.broadcasted_iota(jnp.int32, (TQ, 2), 1)
        lse_ref[...] = jnp.where(lane == 0, lse[0], lse[1])

    return pl.pallas_call(
        body, grid=_ATTN_GRID, name="attn_fwd", in_specs=[_Q_SPEC, _K_SPEC, _V_SPEC], out_specs=[_O_SPEC, _LSE_SPEC],
        out_shape=[jax.ShapeDtypeStruct((R, D_ATTN), F32), jax.ShapeDtypeStruct((N_HEADS // 2, R, 2), F32)],
        compiler_params=_params("arbitrary", "arbitrary", "arbitrary"),
    )(q, k, v)


def _attn_bwd(q, k, v, o, lse, do):
    def body(q_ref, k_ref, v_ref, o_ref, lse_ref, do_ref, dq_ref, dk_ref, dv_ref):
        @pl.when(pl.program_id(2) == 0)
        def _():
            dk_ref[...] = jnp.zeros_like(dk_ref)
            dv_ref[...] = jnp.zeros_like(dv_ref)

        do = do_ref[...]
        dob = do.astype(BF)
        do_o = do * o_ref[...]
        v2 = v_ref[...]
        dv_sum = jnp.zeros((TP, 2 * V_HEAD), F32)
        for hh in range(2):
            sl = slice(hh * HP, (hh + 1) * HP)
            qb, kb = q_ref[:, sl], k_ref[:, sl]
            p = jnp.exp(_nt(qb, kb) * SCALE - lse_ref[:, hh:hh + 1])
            dp = _nt(dob, v2 * _head_mask(hh, BF))
            delta = jnp.sum(do_o * _head_mask(hh, F32), axis=-1, keepdims=True)
            dsb = (p * (dp - delta) * SCALE).astype(BF)
            dq_ref[:, sl] = _nn(dsb, kb)
            dk_ref[:, sl] += _tn(dsb, qb)
            dv_sum = dv_sum + _tn(p.astype(BF), dob) * _head_mask(hh, F32)
        dv_ref[...] += dv_sum

    return pl.pallas_call(
        body, grid=_ATTN_GRID, name="attn_bwd", in_specs=[_Q_SPEC, _K_SPEC, _V_SPEC, _O_SPEC, _LSE_SPEC, _O_SPEC],
        out_specs=[_Q_SPEC, _K_SPEC, _V_SPEC],
        out_shape=[jax.ShapeDtypeStruct((R, N_HEADS * HP), F32), jax.ShapeDtypeStruct((R, N_HEADS * HP), F32),
                   jax.ShapeDtypeStruct((R, D_ATTN), F32)],
        compiler_params=_params("arbitrary", "arbitrary", "arbitrary"),
    )(q, k, v, o, lse, do)


def _tile_prefix(a_ref, b_ref, reverse):
    r8 = lax.broadcasted_iota(jnp.int32, (TP, CG), 0) & 7
    a, b = a_ref[...], b_ref[...]
    for s in (1, 2, 4):
        shift = TP - s if reverse else s
        keep = (r8 < 8 - s) if reverse else (r8 >= s)
        b = jnp.where(keep, a * pltpu.roll(b, shift, 0) + b, b)
        a = jnp.where(keep, a * pltpu.roll(a, shift, 0), a)
    a_ref[...] = a
    b_ref[...] = b


def _scan_pair(af_ref, bf_ref, hf_ref, ab_ref, bb_ref, hb_ref):
    _tile_prefix(af_ref, bf_ref, False)
    _tile_prefix(ab_ref, bb_ref, True)
    n_tiles = TP // 8

    def step(i, carry):
        cf, cb = carry
        rf = pl.multiple_of(i * 8, 8)
        rb = pl.multiple_of((n_tiles - 1 - i) * 8, 8)
        hf_ref[pl.ds(rf, 8), :] = bf_ref[pl.ds(rf, 8), :] + af_ref[pl.ds(rf, 8), :] * cf
        hb_ref[pl.ds(rb, 8), :] = bb_ref[pl.ds(rb, 8), :] + ab_ref[pl.ds(rb, 8), :] * cb
        cf = bf_ref[pl.ds(rf + 7, 1), :] + af_ref[pl.ds(rf + 7, 1), :] * cf
        cb = bb_ref[pl.ds(rb, 1), :] + ab_ref[pl.ds(rb, 1), :] * cb
        return cf, cb

    zero = jnp.zeros((1, CG), F32)
    lax.fori_loop(0, n_tiles, step, (zero, zero), unroll=8)


def _shifts(x):
    t = lax.broadcasted_iota(jnp.int32, x.shape, 0)
    xm2 = jnp.where(t >= 2, pltpu.roll(x, 2, 0), 0.0)
    xm1 = jnp.where(t >= 1, pltpu.roll(x, 1, 0), 0.0)
    xp1 = jnp.where(t < TP - 1, pltpu.roll(x, TP - 1, 0), 0.0)
    return xm2, xm1, xp1


def _softplus(z):
    e = jnp.exp(-jnp.abs(z))
    small = e * (1.0 - e * (0.5 - e * (1.0 / 3.0)))
    return jnp.maximum(z, 0.0) + jnp.where(e < 0.01, small, jnp.log(1.0 + e))


def _neg_expm1(x):
    series = -x * (1.0 + x * 0.5 * (1.0 + x * (1.0 / 3.0) * (1.0 + x * 0.25)))
    return jnp.where(x > -0.05, series, 1.0 - jnp.exp(x))


def _gates(row0, xc, pa_f, pi_f, pa_b, pi_b, lam_f, lam_b):
    t = row0 + lax.broadcasted_iota(jnp.int32, xc.shape, 0)
    valid = t < T
    out = []
    for pa, pi_, lam in ((pa_f, pi_f, lam_f), (pa_b, pi_b, lam_b)):
        r = jax.nn.sigmoid(pa)
        gate_i = jax.nn.sigmoid(pi_)
        log_a = -LRU_C * r * _softplus(-lam)
        a = jnp.exp(log_a)
        mult = jnp.sqrt(jnp.maximum(_neg_expm1(2.0 * log_a), 0.0))
        out += [a, jnp.where(valid, mult * (gate_i * xc), 0.0)]
    return tuple(out)


def _rnn_specs():
    seq = pl.BlockSpec((TP, CG), lambda g, b: (b, g))
    return dict(
        seq=seq,
        cw=pl.BlockSpec((CONV_W, CG), lambda g, b: (0, g)),
        cb=pl.BlockSpec((1, CG), lambda g, b: (0, g)),
        w4=pl.BlockSpec((None, CG, 4 * CG), lambda g, b: (g, 0, 0)),
        b4=pl.BlockSpec((None, 1, 4 * CG), lambda g, b: (g, 0, 0)),
        lam=pl.BlockSpec((None, 1, 2 * CG), lambda g, b: (g, 0, 0)),
    )


def _conv(x, xm2, xm1, xp1, cw_ref, cb_ref):
    return cw_ref[0:1, :] * xm2 + cw_ref[1:2, :] * xm1 + cw_ref[2:3, :] * x + cw_ref[3:4, :] * xp1 + cb_ref[...]


TC = 128
N_TC = TP // TC


def _split4(pre):
    return pre[:, :CG], pre[:, CG:2 * CG], pre[:, 2 * CG:3 * CG], pre[:, 3 * CG:]


def _rnn_fwd(xr, xg, cw):
    def body(xr_ref, xg_ref, cw_ref, cb_ref, w4_ref, b4_ref, lam_ref, y_ref, hf_ref, hb_ref, xc_s, af, bf, ab, bb):
        x = xr_ref[...]
        xc_s[...] = _conv(x, *_shifts(x), cw_ref, cb_ref)
        lam = lam_ref[...]

        def chunk(i, _):
            rows = pl.ds(pl.multiple_of(i * TC, TC), TC)
            xc = xc_s[rows, :]
            pre = _nn(xc.astype(BF), w4_ref[...]) + b4_ref[...]
            a_f, b_f, a_b, b_b = _gates(i * TC, xc, *_split4(pre), lam[:, :CG], lam[:, CG:])
            af[rows, :] = a_f
            bf[rows, :] = b_f
            ab[rows, :] = a_b
            bb[rows, :] = b_b
            return 0

        lax.fori_loop(0, N_TC, chunk, 0)
        _scan_pair(af, bf, hf_ref, ab, bb, hb_ref)
        y_ref[...] = (hf_ref[...] + hb_ref[...]) * jax.nn.gelu(xg_ref[...])

    sp = _rnn_specs()
    return pl.pallas_call(
        body, grid=(N_CG, NB), name="rnn_fwd",
        in_specs=[sp["seq"], sp["seq"], sp["cw"], sp["cb"], sp["w4"], sp["b4"], sp["lam"]],
        out_specs=[sp["seq"]] * 3, out_shape=[jax.ShapeDtypeStruct((R, D_RNN), F32)] * 3,
        scratch_shapes=[pltpu.VMEM((TP, CG), F32)] * 5,
        compiler_params=_params("arbitrary", "arbitrary"),
    )(xr, xg, cw["conv_w"], cw["conv_b"], cw["w4"], cw["b4"], cw["lam"])


def _rnn_bwd(dy, xr, xg, hf, hb, cw):
    def body(dy_ref, xr_ref, xg_ref, hf_ref, hb_ref, cw_ref, cb_ref, w4_ref, b4_ref, lam_ref,
             dxr_ref, dxg_ref, dcw_ref, dcb_ref, dw4_ref, db4_ref, dlam_ref,
             xc_s, af_s, ab_s, dhs_s, dhs2_s, lf_s, lb_s, daf_s, dab_s, dxc_s):
        @pl.when(pl.program_id(1) == 0)
        def _():
            for r in (dcw_ref, dcb_ref, dw4_ref, db4_ref, dlam_ref):
                r[...] = jnp.zeros_like(r)

        x = xr_ref[...]
        xc_s[...] = _conv(x, *_shifts(x), cw_ref, cb_ref)
        lam = lam_ref[...]

        def chunk1(i, _):
            rows = pl.ds(pl.multiple_of(i * TC, TC), TC)
            xc = xc_s[rows, :]
            pre = _nn(xc.astype(BF), w4_ref[...]) + b4_ref[...]
            a_f, _, a_b, _ = _gates(i * TC, xc, *_split4(pre), lam[:, :CG], lam[:, CG:])
            af_s[rows, :] = a_f
            ab_s[rows, :] = a_b
            _, vjp_y = jax.vjp(lambda h, g: h * jax.nn.gelu(g), hf_ref[rows, :] + hb_ref[rows, :], xg_ref[rows, :])
            dhs, dxg = vjp_y(dy_ref[rows, :])
            dhs_s[rows, :] = dhs
            dhs2_s[rows, :] = dhs
            dxg_ref[rows, :] = dxg
            return 0

        lax.fori_loop(0, N_TC, chunk1, 0)
        t = lax.broadcasted_iota(jnp.int32, (TP, CG), 0)
        af_s[...] = pltpu.roll(af_s[...], TP - 1, 0)
        ab_s[...] = pltpu.roll(ab_s[...], 1, 0)
        _scan_pair(ab_s, dhs_s, lb_s, af_s, dhs2_s, lf_s)
        daf_s[...] = lf_s[...] * jnp.where(t >= 1, pltpu.roll(hf_ref[...], 1, 0), 0.0)
        dab_s[...] = lb_s[...] * jnp.where(t < TP - 1, pltpu.roll(hb_ref[...], TP - 1, 0), 0.0)

        def chunk2(i, _):
            rows = pl.ds(pl.multiple_of(i * TC, TC), TC)
            xc = xc_s[rows, :]
            xcb = xc.astype(BF)
            pre = _nn(xcb, w4_ref[...]) + b4_ref[...]
            _, vjp_gates = jax.vjp(functools.partial(_gates, i * TC), xc, *_split4(pre), lam[:, :CG], lam[:, CG:])
            dxc, dpa_f, dpi_f, dpa_b, dpi_b, dlam_f, dlam_b = vjp_gates(
                (daf_s[rows, :], lf_s[rows, :], dab_s[rows, :], lb_s[rows, :]))
            dpre = jnp.concatenate([dpa_f, dpi_f, dpa_b, dpi_b], axis=1)
            dpreb = dpre.astype(BF)
            dxc_s[rows, :] = dxc + _nt(dpreb, w4_ref[...])
            dw4_ref[...] += _tn(xcb, dpreb)
            db4_ref[...] += jnp.sum(dpre, axis=0, keepdims=True)
            dlam_ref[...] += jnp.concatenate([dlam_f, dlam_b], axis=1)
            return 0

        lax.fori_loop(0, N_TC, chunk2, 0)
        dxc = dxc_s[...]
        dcb_ref[...] += jnp.sum(dxc, axis=0, keepdims=True)
        for tap, xs in enumerate(_shifts(x)[:2] + (x,) + _shifts(x)[2:]):
            dcw_ref[tap:tap + 1, :] += jnp.sum(xs * dxc, axis=0, keepdims=True)
        dxr_ref[...] = (cw_ref[0:1, :] * jnp.where(t < TP - 2, pltpu.roll(dxc, TP - 2, 0), 0.0)
                        + cw_ref[1:2, :] * jnp.where(t < TP - 1, pltpu.roll(dxc, TP - 1, 0), 0.0)
                        + cw_ref[2:3, :] * dxc
                        + cw_ref[3:4, :] * jnp.where(t >= 1, pltpu.roll(dxc, 1, 0), 0.0))

    sp = _rnn_specs()
    return pl.pallas_call(
        body, grid=(N_CG, NB), name="rnn_bwd",
        in_specs=[sp["seq"]] * 5 + [sp["cw"], sp["cb"], sp["w4"], sp["b4"], sp["lam"]],
        out_specs=[sp["seq"], sp["seq"], sp["cw"], sp["cb"], sp["w4"], sp["b4"], sp["lam"]],
        out_shape=[jax.ShapeDtypeStruct((R, D_RNN), F32), jax.ShapeDtypeStruct((R, D_RNN), F32),
                   jax.ShapeDtypeStruct((CONV_W, D_RNN), F32), jax.ShapeDtypeStruct((1, D_RNN), F32),
                   jax.ShapeDtypeStruct((N_CG, CG, 4 * CG), F32), jax.ShapeDtypeStruct((N_CG, 1, 4 * CG), F32),
                   jax.ShapeDtypeStruct((N_CG, 1, 2 * CG), F32)],
        scratch_shapes=[pltpu.VMEM((TP, CG), F32)] * 10,
        compiler_params=_params("arbitrary", "arbitrary"),
    )(dy, xr, xg, hf, hb, cw["conv_w"], cw["conv_b"], cw["w4"], cw["b4"], cw["lam"])


TD = 128


def _stage_d(hp, o, y, tgt, cw):
    def body(hp_ref, o_ref, y_ref, tgt_ref, ga, gr, wout, ln2, wg, wu, wd,
             do_ref, dy_ref, dh1_ref, mix_ref, dh1b_ref, hn2_ref, dg_ref, du_ref, act_ref, dh2b_ref,
             loss_ref, dga_ref, dgr_ref, dln2_ref):
        i = pl.program_id(0)

        @pl.when(i == 0)
        def _():
            for r in (loss_ref, dga_ref, dgr_ref, dln2_ref):
                r[...] = jnp.zeros_like(r)

        mix_a, vjp_a = jax.vjp(lambda x, g: _rms(x, g, D_ATTN), o_ref[...], ga[...])
        mix_r, vjp_r = jax.vjp(lambda x, g: _rms(x, g, D_RNN), y_ref[...], gr[...])
        mab, mrb = mix_a.astype(BF), mix_r.astype(BF)
        mix_ref[:, :D_ATTN] = mab
        mix_ref[:, D_ATTN:] = mrb
        h1 = hp_ref[...] + _nn(mab, wout[:D_ATTN, :]) + _nn(mrb, wout[D_ATTN:, :])
        hn2, vjp_ln2 = jax.vjp(lambda x, g: _rms(x, g, D), h1, ln2[...])
        hn2b = hn2.astype(BF)
        hn2_ref[...] = hn2b
        act, vjp_act = jax.vjp(lambda g, u: jax.nn.silu(g) * u, _nt(hn2b, wg[...]), _nt(hn2b, wu[...]))
        actb = act.astype(BF)
        act_ref[...] = actb
        h2 = h1 + _nn(actb, wd[...])
        row = i * TD + lax.broadcasted_iota(jnp.int32, (TD, 1), 0)
        t = jnp.where(row >= TP, row - TP, row)
        err = jnp.where((t >= N_META) & (t < T), h2 - tgt_ref[...], 0.0)
        loss_ref[...] += jnp.sum(err * err) * (0.5 / D)
        dh2b = (err * (1.0 / D)).astype(BF)
        dh2b_ref[...] = dh2b
        dg, du = vjp_act(_nt(dh2b, wd[...]))
        dgb, dub = dg.astype(BF), du.astype(BF)
        dg_ref[...] = dgb
        du_ref[...] = dub
        dh1n, dln2 = vjp_ln2(_nn(dgb, wg[...]) + _nn(dub, wu[...]))
        dh1 = err * (1.0 / D) + dh1n
        dh1_ref[...] = dh1
        dh1b = dh1.astype(BF)
        dh1b_ref[...] = dh1b
        dmix = _nt(dh1b, wout[...])
        do, dga = vjp_a(dmix[:, :D_ATTN])
        dyr, dgr = vjp_r(dmix[:, D_ATTN:])
        do_ref[...] = do
        dy_ref[...] = dyr
        dga_ref[...] += dga
        dgr_ref[...] += dgr
        dln2_ref[...] += dln2

    rs = lambda n: _row_spec(n, TD)
    acc = lambda n: pl.BlockSpec((1, n), lambda i: (0, 0))
    return pl.pallas_call(
        body, grid=(R // TD,), name="stage_d",
        in_specs=[rs(D), rs(D_ATTN), rs(D_RNN), rs(D), _const_spec((1, D_ATTN)), _const_spec((1, D_RNN)),
                  _const_spec((D, D)), _const_spec((1, D)), _const_spec((D_FF, D)), _const_spec((D_FF, D)),
                  _const_spec((D_FF, D))],
        out_specs=[rs(D_ATTN), rs(D_RNN), rs(D), rs(D), rs(D), rs(D), rs(D_FF), rs(D_FF), rs(D_FF), rs(D),
                   acc(1), acc(D_ATTN), acc(D_RNN), acc(D)],
        out_shape=[jax.ShapeDtypeStruct((R, D_ATTN), F32), jax.ShapeDtypeStruct((R, D_RNN), F32),
                   jax.ShapeDtypeStruct((R, D), F32), jax.ShapeDtypeStruct((R, D), BF),
                   jax.ShapeDtypeStruct((R, D), BF), jax.ShapeDtypeStruct((R, D), BF),
                   jax.ShapeDtypeStruct((R, D_FF), BF), jax.ShapeDtypeStruct((R, D_FF), BF),
                   jax.ShapeDtypeStruct((R, D_FF), BF), jax.ShapeDtypeStruct((R, D), BF),
                   jax.ShapeDtypeStruct((1, 1), F32), jax.ShapeDtypeStruct((1, D_ATTN), F32),
                   jax.ShapeDtypeStruct((1, D_RNN), F32), jax.ShapeDtypeStruct((1, D), F32)],
        compiler_params=_params("arbitrary"),
    )(hp, o, y, tgt, cw["ga"], cw["gr"], cw["wout"], cw["ln2_g"], cw["wg"], cw["wu"], cw["wd"])


TW = 544


def _wgrad(a, b, name, tk=None):
    ka, nb = a.shape[1], b.shape[1]
    tk = ka if tk is None else tk

    def body(a_ref, b_ref, o_ref):
        @pl.when(pl.program_id(1) == 0)
        def _():
            o_ref[...] = jnp.zeros_like(o_ref)

        o_ref[...] += _tn(a_ref[...].astype(BF), b_ref[...].astype(BF))

    return pl.pallas_call(
        body, grid=(ka // tk, R // TW), name=name,
        in_specs=[pl.BlockSpec((TW, tk), lambda k, r: (r, k)), pl.BlockSpec((TW, nb), lambda k, r: (r, 0))],
        out_specs=pl.BlockSpec((tk, nb), lambda k, r: (k, 0)),
        out_shape=jax.ShapeDtypeStruct((ka, nb), F32),
        compiler_params=_params("arbitrary", "arbitrary"),
    )(a, b)


def _rope_tables():
    half = QK_ROPE // 2
    freqs = 1.0 / (ROPE_THETA ** (jnp.arange(half, dtype=F32) / half))
    ang = jnp.arange(TP, dtype=F32)[:, None] * freqs[None, :]
    ones = jnp.ones((TP, QK_NOPE), F32)
    zeros = jnp.zeros((TP, QK_NOPE), F32)
    pad1 = jnp.ones((TP, HP - QK_HEAD), F32)
    pad0 = jnp.zeros((TP, HP - QK_HEAD), F32)
    cs = jnp.concatenate([ones, jnp.cos(ang), jnp.cos(ang), pad1], axis=1)
    sn = jnp.concatenate([zeros, jnp.sin(ang), jnp.sin(ang), pad0], axis=1)
    return jnp.tile(cs, (NB, 1)), jnp.tile(sn, (NB, 1))


def _pad_rows(a, lo, hi):
    return jnp.pad(a, ((0, 0), (lo, hi), (0, 0)))


def _compute_weights(w):
    win_t = w["w_in_t"]
    kr = win_t[O_KR:O_KR + QK_ROPE]
    win = jnp.concatenate([win_t[:O_KR], jnp.zeros((QK_NOPE, D), F32), kr,
                           jnp.zeros((HP - QK_HEAD, D), F32), win_t[O_KR + QK_ROPE:]], axis=0)
    wq = _pad_rows(w["w_uq_t"].reshape(N_HEADS, QK_HEAD, Q_LORA), 0, HP - QK_HEAD)
    wkv = w["w_ukv_t"].reshape(N_HEADS, QK_NOPE + V_HEAD, KV_LORA)
    wk = _pad_rows(wkv[:, :QK_NOPE], 0, HP - QK_NOPE)
    wv = wkv[:, QK_NOPE:].reshape(D_ATTN, KV_LORA)
    gates = jnp.stack([w["lru_wa"][0], w["lru_wi"][0], w["lru_wa"][1], w["lru_wi"][1]])
    blk = gates.reshape(4, N_CG, 2, RNN_BW, RNN_BW)
    dense = jnp.einsum("tcaij,ab->tcaibj", blk, jnp.eye(2, dtype=F32)).reshape(4, N_CG, CG, CG)
    w4 = dense.transpose(1, 2, 0, 3).reshape(N_CG, CG, 4 * CG)
    bias = jnp.stack([w["lru_ba"][0], w["lru_bi"][0], w["lru_ba"][1], w["lru_bi"][1]])
    b4 = bias.reshape(4, N_CG, CG).transpose(1, 0, 2).reshape(N_CG, 1, 4 * CG)
    lam = w["lru_lambda"].reshape(2, N_CG, CG).transpose(1, 0, 2).reshape(N_CG, 1, 2 * CG)
    pad_g = lambda g: jnp.pad(g.reshape(1, QK_HEAD), ((0, 0), (0, HP - QK_HEAD)))
    return dict(
        ln1_g=w["ln1_g"].reshape(1, D), win=win.astype(BF), qa_g=w["q_a_norm_g"].reshape(1, Q_LORA),
        wq=wq.astype(BF), kva_g=w["kv_a_norm_g"].reshape(1, KV_LORA), wk=wk.astype(BF), wv=wv.astype(BF),
        q_g=pad_g(w["q_norm_g"]), k_g=pad_g(w["k_norm_g"]),
        conv_w=w["conv_w"].reshape(CONV_W, D_RNN), conv_b=w["conv_b"].reshape(1, D_RNN),
        w4=w4.astype(BF), b4=b4, lam=lam,
        ga=w["attn_out_g"].reshape(1, D_ATTN), gr=w["rnn_out_g"].reshape(1, D_RNN), wout=w["w_out"].astype(BF),
        ln2_g=w["ln2_g"].reshape(1, D), wg=w["w_gate_t"].astype(BF), wu=w["w_up_t"].astype(BF),
        wd=w["w_down"].astype(BF),
    )


def _local_step(x, target, meta, w):
    cw = _compute_weights(w)
    cs, sn = _rope_tables()
    hp = jnp.concatenate([jnp.broadcast_to(meta[None], (NB, N_META, D)), x,
                          jnp.zeros((NB, TP - T, D), F32)], axis=1).reshape(R, D)
    tgt = _pad_rows(target, N_META, TP - T).reshape(R, D)

    pa, xr, xg, q, k, v = _stage_a_fwd(hp, cs, sn, cw)
    o, lse = _attn_fwd(q, k, v)
    y, hf, hb = _rnn_fwd(xr, xg, cw)
    (do, dy, dh1, mixb, dh1b, hn2b, dgb, dub, actb, dh2b, loss, dga, dgr, dln2) = _stage_d(hp, o, y, tgt, cw)
    dxr, dxg, dcw, dcb, dw4, db4, dlam = _rnn_bwd(dy, xr, xg, hf, hb, cw)
    dq, dk, dv = _attn_bwd(q, k, v, o, lse, do)
    (dhp, dpb, dqrawb, dkrawb, hn1b, cqnb, ckvnb, dln1, dqag, dkvag, dqg, dkg) = _stage_a_bwd(
        dq, dk, dv, dxr, dxg, dh1, hp, pa, cs, sn, cw)

    dwin = _wgrad(dpb, hn1b, "wgrad_in", tk=PC // 2)
    dwq = _wgrad(dqrawb, cqnb, "wgrad_uq")
    dwk = _wgrad(dkrawb, ckvnb, "wgrad_uk")
    dwv = _wgrad(dv, ckvnb, "wgrad_uv")
    dwout = _wgrad(mixb, dh1b, "wgrad_out")
    dwg = _wgrad(dgb, hn2b, "wgrad_gate", tk=D_FF // 2)
    dwu = _wgrad(dub, hn2b, "wgrad_up", tk=D_FF // 2)
    dwd = _wgrad(actb, dh2b, "wgrad_down", tk=D_FF // 2)

    dwin_t = jnp.concatenate([dwin[:O_KR], dwin[O_KR + QK_NOPE:O_KR + QK_HEAD], dwin[O_XR:]], axis=0)
    dwq_t = dwq.reshape(N_HEADS, HP, Q_LORA)[:, :QK_HEAD].reshape(N_HEADS * QK_HEAD, Q_LORA)
    dwkv_t = jnp.concatenate([dwk.reshape(N_HEADS, HP, KV_LORA)[:, :QK_NOPE],
                              dwv.reshape(N_HEADS, V_HEAD, KV_LORA)], axis=1).reshape(2 * D_ATTN, KV_LORA)
    d4 = dw4.reshape(N_CG, 2, RNN_BW, 4, 2, RNN_BW)
    dgates = jnp.stack([d4[:, 0, :, :, 0, :], d4[:, 1, :, :, 1, :]], axis=1)
    dgates = dgates.transpose(3, 0, 1, 2, 4).reshape(4, N_HEADS, RNN_BW, RNN_BW)
    dbias = db4.reshape(N_CG, 4, CG).transpose(1, 0, 2).reshape(4, D_RNN)
    dhp3 = dhp.reshape(NB, TP, D)
    grads = dict(
        meta_tokens=jnp.sum(dhp3[:, :N_META], axis=0),
        ln1_g=dln1, w_in_t=dwin_t, q_a_norm_g=dqag, w_uq_t=dwq_t, kv_a_norm_g=dkvag, w_ukv_t=dwkv_t,
        q_norm_g=dqg[:, :QK_HEAD], k_norm_g=dkg[:, :QK_HEAD], conv_w=dcw[None], conv_b=dcb,
        lru_wa=jnp.stack([dgates[0], dgates[2]])[None], lru_ba=jnp.stack([dbias[0], dbias[2]])[None],
        lru_wi=jnp.stack([dgates[1], dgates[3]])[None], lru_bi=jnp.stack([dbias[1], dbias[3]])[None],
        lru_lambda=dlam.reshape(N_CG, 2, CG).transpose(1, 0, 2).reshape(1, 2, D_RNN),
        attn_out_g=dga, rnn_out_g=dgr, w_out=dwout, ln2_g=dln2, w_gate_t=dwg, w_up_t=dwu, w_down=dwd,
    )
    return loss[0, 0], dhp3[:, N_META:T], grads


_ANY = pl.BlockSpec(memory_space=pl.ANY)


def _place():
    return lax.axis_index("x"), lax.axis_index("y"), lax.axis_index("c")


def _other_chips(x, y):
    return [(1 - x, y), (x, 1 - y), (1 - x, 1 - y)]


def _all_gather(xs, name):
    m, n = xs.shape

    def body(x_ref, out_ref, send_sems, recv_sems, local_sem):
        x, y, c = _place()
        me, sibling = (x, y, c), (x, y, 1 - c)
        chips = _other_chips(x, y)

        def rows(px, py, pc):
            return out_ref.at[pl.ds((4 * px + 2 * py + pc) * m, m), :]

        def copy(k, block, to, src=None):
            return pltpu.make_async_remote_copy(
                src_ref=rows(*block) if src is None else src, dst_ref=rows(*block),
                send_sem=send_sems.at[k], recv_sem=recv_sems.at[k], device_id=to, device_id_type=MESH)

        mine = pltpu.make_async_copy(x_ref, rows(*me), local_sem)
        mine.start()
        first = [copy(0, me, sibling, src=x_ref)]
        first += [copy(1 + j, me, (*chip, c), src=x_ref) for j, chip in enumerate(chips)]
        for cp in first:
            cp.start()
        passed = [copy(4 + j, (*chip, c), sibling) for j, chip in enumerate(chips)]
        for j, chip in enumerate(chips):
            copy(1 + j, (*chip, c), me).wait_recv()
            passed[j].start()
        copy(0, sibling, me).wait_recv()
        for j, chip in enumerate(chips):
            copy(4 + j, (*chip, 1 - c), me).wait_recv()
        for cp in first + passed:
            cp.wait_send()
        mine.wait()

    return pl.pallas_call(
        body, name=name, out_shape=jax.ShapeDtypeStruct((8 * m, n), xs.dtype), in_specs=[_ANY], out_specs=_ANY,
        scratch_shapes=[pltpu.SemaphoreType.DMA((7,)), pltpu.SemaphoreType.DMA((7,)), pltpu.SemaphoreType.DMA],
    )(xs)


def _pair_exchange(big, small):
    n_s, _, m, n = big.shape

    def body(big_ref, small_ref, rbig_ref, rsmall_ref, send_sems, recv_sems):
        x, y, c = _place()
        sibling = (x, y, 1 - c)
        copies = [pltpu.make_async_remote_copy(
            src_ref=big_ref.at[s, 1 - c], dst_ref=rbig_ref.at[s], send_sem=send_sems.at[s], recv_sem=recv_sems.at[s],
            device_id=sibling, device_id_type=MESH) for s in range(n_s)]
        copies.append(pltpu.make_async_remote_copy(
            src_ref=small_ref, dst_ref=rsmall_ref, send_sem=send_sems.at[n_s], recv_sem=recv_sems.at[n_s],
            device_id=sibling, device_id_type=MESH))
        for cp in copies:
            cp.start()
        for cp in copies:
            cp.wait()

    return pl.pallas_call(
        body, name="grad_pair_exchange",
        out_shape=[jax.ShapeDtypeStruct((n_s, m, n), big.dtype), jax.ShapeDtypeStruct(small.shape, small.dtype)],
        in_specs=[_ANY, _ANY], out_specs=[_ANY, _ANY],
        scratch_shapes=[pltpu.SemaphoreType.DMA((n_s + 1,)), pltpu.SemaphoreType.DMA((n_s + 1,))],
    )(big, small)


def _chip_exchange(big, small):
    _, m, n = big.shape
    ms = small.shape[0]

    def body(big_ref, small_ref, rbig_ref, rsmall_ref, send_sems, recv_sems):
        x, y, c = _place()
        copies = []
        for j, (tx, ty) in enumerate(_other_chips(x, y)):
            copies.append(pltpu.make_async_remote_copy(
                src_ref=big_ref.at[2 * tx + ty], dst_ref=rbig_ref.at[j], send_sem=send_sems.at[j],
                recv_sem=recv_sems.at[j], device_id=(tx, ty, c), device_id_type=MESH))
            copies.append(pltpu.make_async_remote_copy(
                src_ref=small_ref, dst_ref=rsmall_ref.at[j], send_sem=send_sems.at[3 + j],
                recv_sem=recv_sems.at[3 + j], device_id=(tx, ty, c), device_id_type=MESH))
        for cp in copies:
            cp.start()
        for cp in copies:
            cp.wait()

    return pl.pallas_call(
        body, name="grad_chip_exchange",
        out_shape=[jax.ShapeDtypeStruct((3, m, n), big.dtype), jax.ShapeDtypeStruct((3, ms, n), small.dtype)],
        in_specs=[_ANY, _ANY], out_specs=[_ANY, _ANY],
        scratch_shapes=[pltpu.SemaphoreType.DMA((6,)), pltpu.SemaphoreType.DMA((6,))],
    )(big, small)


def _pair_swap(a):
    def body(a_ref, r_ref, send_sem, recv_sem):
        x, y, c = _place()
        cp = pltpu.make_async_remote_copy(src_ref=a_ref, dst_ref=r_ref, send_sem=send_sem, recv_sem=recv_sem,
                                          device_id=(x, y, 1 - c), device_id_type=MESH)
        cp.start()
        cp.wait()

    return pl.pallas_call(
        body, name="grad_pair_swap", out_shape=jax.ShapeDtypeStruct(a.shape, a.dtype), in_specs=[_ANY],
        out_specs=_ANY, scratch_shapes=[pltpu.SemaphoreType.DMA, pltpu.SemaphoreType.DMA],
    )(a)


def _row_tile(rows, cap=512):
    for t in range(cap - cap % 8, 7, -8):
        if rows % t == 0:
            return t
    return rows


def _elementwise(fn, n_out, name, *arrs, out_dtype=F32):
    rows, cols = arrs[0].shape
    tr = _row_tile(rows)
    n_in = len(arrs)

    def body(*refs):
        outs = fn(*[r[...].astype(F32) for r in refs[:n_in]])
        for r, o in zip(refs[n_in:], outs):
            r[...] = o.astype(out_dtype)

    spec = pl.BlockSpec((tr, cols), lambda i: (i, 0))
    return pl.pallas_call(
        body, grid=(rows // tr,), name=name, in_specs=[spec] * n_in, out_specs=[spec] * n_out,
        out_shape=[jax.ShapeDtypeStruct((rows, cols), out_dtype)] * n_out, compiler_params=_params("arbitrary"),
    )(*arrs)


def _add2(a, b):
    return (a + b,)


def _add4(own, r0, r1, r2):
    return ((own + r2) + (r0 + r1),)


def _adamw_math(w, g, m, v):
    m = ADAM_B1 * m + (1.0 - ADAM_B1) * g
    v = ADAM_B2 * v + (1.0 - ADAM_B2) * (g * g)
    m_hat = m / (1.0 - ADAM_B1 ** ADAM_STEP)
    v_hat = v / (1.0 - ADAM_B2 ** ADAM_STEP)
    delta = -ADAM_LR * (m_hat / (jnp.sqrt(v_hat) + ADAM_EPS) + ADAM_WD * w)
    return delta, m, v


WEIGHTS = ["meta_tokens", "ln1_g", "w_in", "q_a_norm_g", "w_uq", "kv_a_norm_g", "w_ukv", "q_norm_g", "k_norm_g",
           "conv_w", "conv_b", "lru_wa", "lru_ba", "lru_wi", "lru_bi", "lru_lambda", "attn_out_g", "rnn_out_g",
           "w_out", "ln2_g", "w_gate", "w_up", "w_down"]
BIG = ["w_in", "w_uq", "w_ukv", "w_out", "w_gate", "w_up", "w_down"]
BIG_T = {"w_in": True, "w_uq": True, "w_ukv": True, "w_out": False, "w_gate": True, "w_up": True, "w_down": False}
BIG_ROWS = {"w_in": 424, "w_uq": 72, "w_ukv": 64, "w_out": 256, "w_gate": 704, "w_up": 704, "w_down": 704}
PACK_ROWS = 2944
HALF_ROWS = PACK_ROWS // 2
SMALL_SHARDED = ["meta_tokens", "conv_w", "lru_ba", "lru_bi", "lru_lambda"]
SMALL = [n for n in WEIGHTS if n not in BIG]
SMALL_PACK_ROWS = 160
SMALL_ADAM_ROWS = 144


def _big_offsets():
    off, o = {}, 0
    for n in BIG:
        off[n] = o
        o += BIG_ROWS[n]
    return off


def _to_pack_piece(name, shard):
    a = shard[0].T if BIG_T[name] else shard[0]
    return a.reshape(BIG_ROWS[name], D)


def _from_pack_piece(name, piece, shard_shape):
    _, k, n = shard_shape
    return piece.reshape(n, k).T[None] if BIG_T[name] else piece.reshape(k, n)[None]


def _flat_pack(arrs, rows):
    flat = jnp.concatenate([a.reshape(-1) for a in arrs])
    return jnp.pad(flat, (0, rows * D - flat.shape[0])).reshape(rows, D)


def _flat_unpack(pack, shapes):
    flat, out, o = pack.reshape(-1), [], 0
    for s in shapes:
        n = math.prod(s)
        out.append(flat[o:o + n].reshape(s))
        o += n
    return out


def kernel(x, meta_tokens, ln1_g, w_in, q_a_norm_g, w_uq, kv_a_norm_g, w_ukv, q_norm_g, k_norm_g, conv_w, conv_b, lru_wa, lru_ba, lru_wi, lru_bi, lru_lambda, attn_out_g, rnn_out_g, w_out, ln2_g, w_gate, w_up, w_down, loss_target, m_meta_tokens, m_ln1_g, m_w_in, m_q_a_norm_g, m_w_uq, m_kv_a_norm_g, m_w_ukv, m_q_norm_g, m_k_norm_g, m_conv_w, m_conv_b, m_lru_wa, m_lru_ba, m_lru_wi, m_lru_bi, m_lru_lambda, m_attn_out_g, m_rnn_out_g, m_w_out, m_ln2_g, m_w_gate, m_w_up, m_w_down, v_meta_tokens, v_ln1_g, v_w_in, v_q_a_norm_g, v_w_uq, v_kv_a_norm_g, v_w_ukv, v_q_norm_g, v_k_norm_g, v_conv_w, v_conv_b, v_lru_wa, v_lru_ba, v_lru_wi, v_lru_bi, v_lru_lambda, v_attn_out_g, v_rnn_out_g, v_w_out, v_ln2_g, v_w_gate, v_w_up, v_w_down):
    wts = dict(zip(WEIGHTS, (meta_tokens, ln1_g, w_in, q_a_norm_g, w_uq, kv_a_norm_g, w_ukv, q_norm_g, k_norm_g, conv_w, conv_b, lru_wa, lru_ba, lru_wi, lru_bi, lru_lambda, attn_out_g, rnn_out_g, w_out, ln2_g, w_gate, w_up, w_down)))
    mom = dict(zip(WEIGHTS, (m_meta_tokens, m_ln1_g, m_w_in, m_q_a_norm_g, m_w_uq, m_kv_a_norm_g, m_w_ukv, m_q_norm_g, m_k_norm_g, m_conv_w, m_conv_b, m_lru_wa, m_lru_ba, m_lru_wi, m_lru_bi, m_lru_lambda, m_attn_out_g, m_rnn_out_g, m_w_out, m_ln2_g, m_w_gate, m_w_up, m_w_down)))
    var = dict(zip(WEIGHTS, (v_meta_tokens, v_ln1_g, v_w_in, v_q_a_norm_g, v_w_uq, v_kv_a_norm_g, v_w_ukv, v_q_norm_g, v_k_norm_g, v_conv_w, v_conv_b, v_lru_wa, v_lru_ba, v_lru_wi, v_lru_bi, v_lru_lambda, v_attn_out_g, v_rnn_out_g, v_w_out, v_ln2_g, v_w_gate, v_w_up, v_w_down)))
    xi, yi, ci = _place()
    chip = 2 * xi + yi
    off = _big_offsets()

    pack = jnp.concatenate([_to_pack_piece(n, wts[n]) for n in BIG] + [jnp.zeros((PACK_ROWS - 2928, D), F32)], axis=0)
    half = lax.dynamic_slice_in_dim(pack.astype(BF), ci * HALF_ROWS, HALF_ROWS, axis=0)
    gw = _all_gather(half, "gather_weights").reshape(N_CHIPS, PACK_ROWS, D)
    full = {n: gw[:, off[n]:off[n] + BIG_ROWS[n]] for n in BIG}
    spack = jnp.concatenate([meta_tokens[:, :LANES], meta_tokens[:, LANES:], conv_w[0], lru_ba[0], lru_bi[0],
                             lru_lambda[0], jnp.zeros((6, LANES), F32)], axis=0)
    shalf = lax.dynamic_slice_in_dim(spack, ci * 24, 24, axis=0)
    gs = _all_gather(shalf, "gather_small").reshape(N_CHIPS, 48, LANES)
    cols = lambda a: a.transpose(1, 0, 2).reshape(a.shape[1], N_CHIPS * a.shape[2])
    meta_full = cols(jnp.concatenate([gs[:, 0:16], gs[:, 16:32]], axis=2))
    w = dict(
        w_in_t=full["w_in"].reshape(IN_COLS, D), w_uq_t=full["w_uq"].reshape(N_HEADS * QK_HEAD, Q_LORA),
        w_ukv_t=full["w_ukv"].reshape(2 * D_ATTN, KV_LORA), w_out=full["w_out"].reshape(D, D),
        w_gate_t=full["w_gate"].reshape(D_FF, D), w_up_t=full["w_up"].reshape(D_FF, D),
        w_down=full["w_down"].reshape(D_FF, D),
        ln1_g=ln1_g, q_a_norm_g=q_a_norm_g, kv_a_norm_g=kv_a_norm_g, q_norm_g=q_norm_g, k_norm_g=k_norm_g,
        conv_w=cols(gs[:, 32:36]), conv_b=conv_b, lru_wa=lru_wa[0], lru_ba=cols(gs[:, 36:38]), lru_wi=lru_wi[0],
        lru_bi=cols(gs[:, 38:40]), lru_lambda=cols(gs[:, 40:42]), attn_out_g=attn_out_g, rnn_out_g=rnn_out_g,
        ln2_g=ln2_g,
    )

    loss_local, grad_x, g = _local_step(x, loss_target, meta_full, w)
    loss = lax.psum(loss_local, ("x", "y", "c"))

    g_t = {"w_in": g["w_in_t"], "w_uq": g["w_uq_t"], "w_ukv": g["w_ukv_t"], "w_out": g["w_out"],
           "w_gate": g["w_gate_t"], "w_up": g["w_up_t"], "w_down": g["w_down"]}
    gpack = jnp.concatenate([g_t[n].reshape(N_CHIPS, BIG_ROWS[n], D) for n in BIG]
                            + [jnp.zeros((N_CHIPS, PACK_ROWS - 2928, D), F32)], axis=1)
    gpack = gpack.reshape(N_CHIPS, 2, HALF_ROWS, D)
    full_shapes = {n: wts[n].shape for n in SMALL}
    full_shapes.update(meta_tokens=(N_META, D), conv_w=(1, CONV_W, D_RNN), lru_ba=(1, 2, D_RNN), lru_bi=(1, 2, D_RNN),
                       lru_lambda=(1, 2, D_RNN))
    gsmall = _flat_pack([g[n] for n in SMALL], SMALL_PACK_ROWS)
    rbig, rsmall = _pair_exchange(gpack, gsmall)
    mine = lax.dynamic_index_in_dim(gpack, ci, axis=1, keepdims=False).reshape(N_CHIPS * HALF_ROWS, D)
    (chip_big,) = _elementwise(_add2, 1, "grad_pair_sum", mine, rbig.reshape(N_CHIPS * HALF_ROWS, D), out_dtype=BF)
    (chip_small,) = _elementwise(_add2, 1, "grad_pair_sum_small", gsmall, rsmall)
    chip_big = chip_big.reshape(N_CHIPS, HALF_ROWS, D)
    xbig, xsmall = _chip_exchange(chip_big, chip_small)
    own = lax.dynamic_index_in_dim(chip_big, chip, axis=0, keepdims=False)
    (half_sum,) = _elementwise(_add4, 1, "grad_chip_sum", own, xbig[0], xbig[1], xbig[2])
    (small_sum,) = _elementwise(_add4, 1, "grad_chip_sum_small", chip_small, xsmall[0], xsmall[1], xsmall[2])
    other_half = _pair_swap(half_sum)
    gshard = jnp.where(ci == 0, jnp.concatenate([half_sum, other_half], axis=0),
                       jnp.concatenate([other_half, half_sum], axis=0))

    grads = {n: _from_pack_piece(n, gshard[off[n]:off[n] + BIG_ROWS[n]], wts[n].shape) for n in BIG}
    small_full = dict(zip(SMALL, _flat_unpack(small_sum, [full_shapes[n] for n in SMALL])))
    for n in SMALL:
        a = small_full[n]
        if n in SMALL_SHARDED:
            width = wts[n].shape[-1]
            a = lax.dynamic_slice_in_dim(a, chip * width, width, axis=a.ndim - 1)
        grads[n] = a.reshape(wts[n].shape)

    delta, new_m, new_v = {}, {}, {}
    for n in BIG:
        two_d = lambda a: a.reshape(a.shape[-2], a.shape[-1])
        d_, m_, v_ = _elementwise(_adamw_math, 3, "adamw_" + n, two_d(wts[n]), two_d(grads[n]), two_d(mom[n]), two_d(var[n]))
        delta[n], new_m[n], new_v[n] = (a.reshape(wts[n].shape) for a in (d_, m_, v_))
    packs = [_flat_pack([src[n] for n in SMALL], SMALL_ADAM_ROWS) for src in (wts, grads, mom, var)]
    outs = _elementwise(_adamw_math, 3, "adamw_small", *packs)
    for dst, o in zip((delta, new_m, new_v), outs):
        dst.update(zip(SMALL, _flat_unpack(o, [wts[n].shape for n in SMALL])))

    return (loss, grad_x, *[grads[n] for n in WEIGHTS], *[delta[n] for n in WEIGHTS],
            *[new_m[n] for n in WEIGHTS], *[new_v[n] for n in WEIGHTS])
```

```python
import functools
import math

import jax
import jax.numpy as jnp
from jax import lax
from jax.experimental import pallas as pl
from jax.experimental.pallas import tpu as pltpu

F32 = jnp.float32
BF = jnp.bfloat16
MESH = pl.DeviceIdType.MESH

D = 1024
SEQ = 2048
N_META = 16
T = N_META + SEQ
N_HEADS = 8
QK_NOPE = 64
QK_ROPE = 32
QK_HEAD = 96
V_HEAD = 64
Q_LORA = 384
KV_LORA = 256
D_ATTN = 512
D_RNN = 512
RNN_BW = 64
CONV_W = 4
LRU_C = 8.0
ROPE_THETA = 10000.0
D_FF = 2816
EPS = 1e-6
IN_COLS = 1696
ADAM_LR, ADAM_B1, ADAM_B2, ADAM_EPS, ADAM_WD, ADAM_STEP = 0.001, 0.9, 0.999, 1e-08, 0.01, 10

LANES = 128
TP = 2176
NB = 2
R = NB * TP
TR = 256
TQ = 544
HP = LANES
PC = 1792
O_CKV, O_KR, O_XR, O_XG = 384, 640, 768, 1280
CG = 128
N_CG = D_RNN // CG
VMEM_LIMIT = 56 * 1024 * 1024
N_CHIPS = 4
SCALE = QK_HEAD ** -0.5
KEY_MASK = -30000.0


def _nt(a, b):
    return lax.dot_general(a, b, (((1,), (1,)), ((), ())), preferred_element_type=F32)


def _nn(a, b):
    return jnp.dot(a, b, preferred_element_type=F32)


def _tn(a, b):
    return lax.dot_general(a, b, (((0,), (0,)), ((), ())), preferred_element_type=F32)


def _rms(x, g, n):
    ms = jnp.sum(x * x, axis=-1, keepdims=True) * (1.0 / n)
    return x * lax.rsqrt(ms + EPS) * g


def _rot_impl(x):
    lane = lax.broadcasted_iota(jnp.int32, x.shape, 1)
    left = pltpu.roll(x, HP - 16, 1)
    right = pltpu.roll(x, 16, 1)
    lo = (lane >= QK_NOPE) & (lane < QK_NOPE + 16)
    hi = (lane >= QK_NOPE + 16) & (lane < QK_HEAD)
    return jnp.where(lo, -left, jnp.where(hi, right, 0.0))


@jax.custom_vjp
def _rot(x):
    return _rot_impl(x)


def _rot_fwd(x):
    return _rot_impl(x), None


def _rot_bwd(_, g):
    return (-_rot_impl(g),)


_rot.defvjp(_rot_fwd, _rot_bwd)


def _head(x, g, cs, sn):
    n = _rms(x, g, QK_HEAD)
    return n * cs + _rot(n) * sn


def _const_spec(shape):
    return pl.BlockSpec(shape, lambda *_: (0,) * len(shape), pipeline_mode=pl.Buffered(1))


def _row_spec(n, tr=TR):
    return pl.BlockSpec((tr, n), lambda i: (i, 0))


def _params(*sem):
    return pltpu.CompilerParams(dimension_semantics=sem, vmem_limit_bytes=VMEM_LIMIT)


def _stage_a_fwd(hp, cs, sn, cw):
    def body(hp_ref, cs_ref, sn_ref, ln1, win, qag, wq, kvag, wk, wv, qg, kg,
             pa_ref, xr_ref, xg_ref, q_ref, k_ref, v_ref):
        hn = _rms(hp_ref[...], ln1[...], D).astype(BF)
        p = _nt(hn, win[...])
        pa_ref[...] = p[:, :O_XR]
        xr_ref[...] = p[:, O_XR:O_XG]
        xg_ref[...] = p[:, O_XG:]
        cqn = _rms(p[:, :O_CKV], qag[...], Q_LORA).astype(BF)
        ckvn = _rms(p[:, O_CKV:O_KR], kvag[...], KV_LORA).astype(BF)
        kr = p[:, O_KR:O_XR]
        c, s = cs_ref[...], sn_ref[...]
        mask_lane = lax.broadcasted_iota(jnp.int32, (1, HP), 1) == QK_HEAD
        row = pl.program_id(0) * TR + lax.broadcasted_iota(jnp.int32, (TR, 1), 0)
        key_mask = jnp.where(jnp.where(row >= TP, row - TP, row) < T, 0.0, KEY_MASK)
        for h in range(N_HEADS):
            sl = slice(h * HP, (h + 1) * HP)
            q_ref[:, sl] = jnp.where(mask_lane, 1.0, _head(_nt(cqn, wq[h]), qg[...], c, s)).astype(BF)
            k_ref[:, sl] = jnp.where(mask_lane, key_mask, _head(_nt(ckvn, wk[h]) + kr, kg[...], c, s)).astype(BF)
        v_ref[...] = _nt(ckvn, wv[...]).astype(BF)

    return pl.pallas_call(
        body, grid=(R // TR,), name="stage_a_fwd",
        in_specs=[_row_spec(D), _row_spec(HP), _row_spec(HP), _const_spec((1, D)), _const_spec((PC, D)),
                  _const_spec((1, Q_LORA)), _const_spec((N_HEADS, HP, Q_LORA)), _const_spec((1, KV_LORA)),
                  _const_spec((N_HEADS, HP, KV_LORA)), _const_spec((D_ATTN, KV_LORA)), _const_spec((1, HP)),
                  _const_spec((1, HP))],
        out_specs=[_row_spec(O_XR), _row_spec(D_RNN), _row_spec(D_RNN), _row_spec(N_HEADS * HP),
                   _row_spec(N_HEADS * HP), _row_spec(D_ATTN)],
        out_shape=[jax.ShapeDtypeStruct((R, O_XR), F32), jax.ShapeDtypeStruct((R, D_RNN), F32),
                   jax.ShapeDtypeStruct((R, D_RNN), F32), jax.ShapeDtypeStruct((R, N_HEADS * HP), BF),
                   jax.ShapeDtypeStruct((R, N_HEADS * HP), BF), jax.ShapeDtypeStruct((R, D_ATTN), BF)],
        compiler_params=_params("arbitrary"),
    )(hp, cs, sn, cw["ln1_g"], cw["win"], cw["qa_g"], cw["wq"], cw["kva_g"], cw["wk"], cw["wv"], cw["q_g"], cw["k_g"])


def _stage_a_bwd(dq, dk, dv, dxr, dxg, dh1, hp, pa, cs, sn, cw):
    def body(dq_ref, dk_ref, dv_ref, dxr_ref, dxg_ref, dh1_ref, hp_ref, pa_ref, cs_ref, sn_ref,
             ln1, win, qag, wq, kvag, wk, wv, qg, kg,
             dhp_ref, dp_ref, dqraw_ref, dkraw_ref, hn_ref, cqn_ref, ckvn_ref,
             dln1_ref, dqag_ref, dkvag_ref, dqg_ref, dkg_ref):
        @pl.when(pl.program_id(0) == 0)
        def _():
            for r in (dln1_ref, dqag_ref, dkvag_ref, dqg_ref, dkg_ref):
                r[...] = jnp.zeros_like(r)

        hn, vjp_ln1 = jax.vjp(lambda h, g: _rms(h, g, D), hp_ref[...], ln1[...])
        hn_ref[...] = hn.astype(BF)
        pa_v = pa_ref[...]
        cqn, vjp_qa = jax.vjp(lambda x, g: _rms(x, g, Q_LORA), pa_v[:, :O_CKV], qag[...])
        ckvn, vjp_kva = jax.vjp(lambda x, g: _rms(x, g, KV_LORA), pa_v[:, O_CKV:O_KR], kvag[...])
        kr = pa_v[:, O_KR:O_XR]
        cqnb, ckvnb = cqn.astype(BF), ckvn.astype(BF)
        cqn_ref[...] = cqnb
        ckvn_ref[...] = ckvnb
        c, s = cs_ref[...], sn_ref[...]
        lane = lax.broadcasted_iota(jnp.int32, (1, HP), 1)
        rope_lanes = ((lane >= QK_NOPE) & (lane < QK_HEAD)).astype(F32)
        dcqn = jnp.zeros((TR, Q_LORA), F32)
        dckvn = jnp.zeros((TR, KV_LORA), F32)
        dkr = jnp.zeros((TR, HP), F32)
        dqg = jnp.zeros((1, HP), F32)
        dkg = jnp.zeros((1, HP), F32)
        for h in range(N_HEADS):
            sl = slice(h * HP, (h + 1) * HP)
            _, vjp_q = jax.vjp(lambda x, g: _head(x, g, c, s), _nt(cqnb, wq[h]), qg[...])
            dqraw, dg = vjp_q(dq_ref[:, sl])
            dqg = dqg + dg
            dqb = dqraw.astype(BF)
            dqraw_ref[:, sl] = dqb
            dcqn = dcqn + _nn(dqb, wq[h])
            _, vjp_k = jax.vjp(lambda x, g: _head(x, g, c, s), _nt(ckvnb, wk[h]) + kr, kg[...])
            dkraw, dg = vjp_k(dk_ref[:, sl])
            dkg = dkg + dg
            dkb = dkraw.astype(BF)
            dkraw_ref[:, sl] = dkb
            dckvn = dckvn + _nn(dkb, wk[h])
            dkr = dkr + dkraw * rope_lanes
        dckvn = dckvn + _nn(dv_ref[...].astype(BF), wv[...])
        dcq, dqag = vjp_qa(dcqn)
        dckv, dkvag = vjp_kva(dckvn)
        dpb = jnp.concatenate([dcq, dckv, dkr, dxr_ref[...], dxg_ref[...]], axis=1).astype(BF)
        dp_ref[...] = dpb
        dh, dln1 = vjp_ln1(_nn(dpb, win[...]))
        dhp_ref[...] = dh + dh1_ref[...]
        dln1_ref[...] += dln1
        dqag_ref[...] += dqag
        dkvag_ref[...] += dkvag
        dqg_ref[...] += dqg
        dkg_ref[...] += dkg

    acc = lambda n: pl.BlockSpec((1, n), lambda i: (0, 0))
    return pl.pallas_call(
        body, grid=(R // TR,), name="stage_a_bwd",
        in_specs=[_row_spec(N_HEADS * HP), _row_spec(N_HEADS * HP), _row_spec(D_ATTN), _row_spec(D_RNN),
                  _row_spec(D_RNN), _row_spec(D), _row_spec(D), _row_spec(O_XR), _row_spec(HP), _row_spec(HP),
                  _const_spec((1, D)), _const_spec((PC, D)), _const_spec((1, Q_LORA)),
                  _const_spec((N_HEADS, HP, Q_LORA)), _const_spec((1, KV_LORA)),
                  _const_spec((N_HEADS, HP, KV_LORA)), _const_spec((D_ATTN, KV_LORA)), _const_spec((1, HP)),
                  _const_spec((1, HP))],
        out_specs=[_row_spec(D), _row_spec(PC), _row_spec(N_HEADS * HP), _row_spec(N_HEADS * HP), _row_spec(D),
                   _row_spec(Q_LORA), _row_spec(KV_LORA), acc(D), acc(Q_LORA), acc(KV_LORA), acc(HP), acc(HP)],
        out_shape=[jax.ShapeDtypeStruct((R, D), F32), jax.ShapeDtypeStruct((R, PC), BF),
                   jax.ShapeDtypeStruct((R, N_HEADS * HP), BF), jax.ShapeDtypeStruct((R, N_HEADS * HP), BF),
                   jax.ShapeDtypeStruct((R, D), BF), jax.ShapeDtypeStruct((R, Q_LORA), BF),
                   jax.ShapeDtypeStruct((R, KV_LORA), BF), jax.ShapeDtypeStruct((1, D), F32),
                   jax.ShapeDtypeStruct((1, Q_LORA), F32), jax.ShapeDtypeStruct((1, KV_LORA), F32),
                   jax.ShapeDtypeStruct((1, HP), F32), jax.ShapeDtypeStruct((1, HP), F32)],
        compiler_params=_params("arbitrary"),
    )(dq, dk, dv, dxr, dxg, dh1, hp, pa, cs, sn, cw["ln1_g"], cw["win"], cw["qa_g"], cw["wq"], cw["kva_g"],
      cw["wk"], cw["wv"], cw["q_g"], cw["k_g"])


def _head_mask(half, dtype):
    lane = lax.broadcasted_iota(jnp.int32, (1, 2 * V_HEAD), 1)
    return ((lane >= V_HEAD) == (half == 1)).astype(dtype)


_ATTN_GRID = (NB, N_HEADS // 2, TP // TQ)
_Q_SPEC = pl.BlockSpec((TQ, 2 * HP), lambda b, j, i: (b * (TP // TQ) + i, j))
_K_SPEC = pl.BlockSpec((TP, 2 * HP), lambda b, j, i: (b, j))
_V_SPEC = pl.BlockSpec((TP, 2 * V_HEAD), lambda b, j, i: (b, j))
_O_SPEC = pl.BlockSpec((TQ, 2 * V_HEAD), lambda b, j, i: (b * (TP // TQ) + i, j))
_LSE_SPEC = pl.BlockSpec((None, TQ, 2), lambda b, j, i: (j, b * (TP // TQ) + i, 0))


def _attn_fwd(q, k, v):
    def body(q_ref, k_ref, v_ref, o_ref, lse_ref):
        v2 = v_ref[...]
        o = jnp.zeros((TQ, 2 * V_HEAD), F32)
        lse = []
        for hh in range(2):
            sl = slice(hh * HP, (hh + 1) * HP)
            s = _nt(q_ref[:, sl], k_ref[:, sl]) * SCALE
            m = jnp.max(s, axis=-1, keepdims=True)
            e = jnp.exp(s - m)
            l = jnp.sum(e, axis=-1, keepdims=True)
            p = e * (1.0 / l)
            o = o + _nn(p.astype(BF), v2 * _head_mask(hh, BF))
            lse.append(m + jnp.log(l))
        o_ref[...] = o
        lane = lax.broadcasted_iota(jnp.int32, (TQ, 2), 1)
        lse_ref[...] = jnp.where(lane == 0, lse[0], lse[1])

    return pl.pallas_call(
        body, grid=_ATTN_GRID, name="attn_fwd", in_specs=[_Q_SPEC, _K_SPEC, _V_SPEC], out_specs=[_O_SPEC, _LSE_SPEC],
        out_shape=[jax.ShapeDtypeStruct((R, D_ATTN), F32), jax.ShapeDtypeStruct((N_HEADS // 2, R, 2), F32)],
        compiler_params=_params("arbitrary", "arbitrary", "arbitrary"),
    )(q, k, v)


def _attn_bwd(q, k, v, o, lse, do):
    def body(q_ref, k_ref, v_ref, o_ref, lse_ref, do_ref, dq_ref, dk_ref, dv_ref):
        @pl.when(pl.program_id(2) == 0)
        def _():
            dk_ref[...] = jnp.zeros_like(dk_ref)
            dv_ref[...] = jnp.zeros_like(dv_ref)

        do = do_ref[...]
        dob = do.astype(BF)
        do_o = do * o_ref[...]
        v2 = v_ref[...]
        dv_sum = jnp.zeros((TP, 2 * V_HEAD), F32)
        for hh in range(2):
            sl = slice(hh * HP, (hh + 1) * HP)
            qb, kb = q_ref[:, sl], k_ref[:, sl]
            p = jnp.exp(_nt(qb, kb) * SCALE - lse_ref[:, hh:hh + 1])
            dp = _nt(dob, v2 * _head_mask(hh, BF))
            delta = jnp.sum(do_o * _head_mask(hh, F32), axis=-1, keepdims=True)
            dsb = (p * (dp - delta) * SCALE).astype(BF)
            dq_ref[:, sl] = _nn(dsb, kb)
            dk_ref[:, sl] += _tn(dsb, qb)
            dv_sum = dv_sum + _tn(p.astype(BF), dob) * _head_mask(hh, F32)
        dv_ref[...] += dv_sum

    return pl.pallas_call(
        body, grid=_ATTN_GRID, name="attn_bwd", in_specs=[_Q_SPEC, _K_SPEC, _V_SPEC, _O_SPEC, _LSE_SPEC, _O_SPEC],
        out_specs=[_Q_SPEC, _K_SPEC, _V_SPEC],
        out_shape=[jax.ShapeDtypeStruct((R, N_HEADS * HP), F32), jax.ShapeDtypeStruct((R, N_HEADS * HP), F32),
                   jax.ShapeDtypeStruct((R, D_ATTN), F32)],
        compiler_params=_params("arbitrary", "arbitrary", "arbitrary"),
    )(q, k, v, o, lse, do)


def _tile_prefix(a_ref, b_ref, reverse):
    r8 = lax.broadcasted_iota(jnp.int32, (TP, CG), 0) & 7
    a, b = a_ref[...], b_ref[...]
    for s in (1, 2, 4):
        shift = TP - s if reverse else s
        keep = (r8 < 8 - s) if reverse else (r8 >= s)
        b = jnp.where(keep, a * pltpu.roll(b, shift, 0) + b, b)
        a = jnp.where(keep, a * pltpu.roll(a, shift, 0), a)
    a_ref[...] = a
    b_ref[...] = b


def _scan_pair(af_ref, bf_ref, hf_ref, ab_ref, bb_ref, hb_ref):
    _tile_prefix(af_ref, bf_ref, False)
    _tile_prefix(ab_ref, bb_ref, True)
    n_tiles = TP // 8

    def step(i, carry):
        cf, cb = carry
        rf = pl.multiple_of(i * 8, 8)
        rb = pl.multiple_of((n_tiles - 1 - i) * 8, 8)
        hf_ref[pl.ds(rf, 8), :] = bf_ref[pl.ds(rf, 8), :] + af_ref[pl.ds(rf, 8), :] * cf
        hb_ref[pl.ds(rb, 8), :] = bb_ref[pl.ds(rb, 8), :] + ab_ref[pl.ds(rb, 8), :] * cb
        cf = bf_ref[pl.ds(rf + 7, 1), :] + af_ref[pl.ds(rf + 7, 1), :] * cf
        cb = bb_ref[pl.ds(rb, 1), :] + ab_ref[pl.ds(rb, 1), :] * cb
        return cf, cb

    zero = jnp.zeros((1, CG), F32)
    lax.fori_loop(0, n_tiles, step, (zero, zero), unroll=8)


def _shifts(x):
    t = lax.broadcasted_iota(jnp.int32, x.shape, 0)
    xm2 = jnp.where(t >= 2, pltpu.roll(x, 2, 0), 0.0)
    xm1 = jnp.where(t >= 1, pltpu.roll(x, 1, 0), 0.0)
    xp1 = jnp.where(t < TP - 1, pltpu.roll(x, TP - 1, 0), 0.0)
    return xm2, xm1, xp1


def _softplus(z):
    e = jnp.exp(-jnp.abs(z))
    small = e * (1.0 - e * (0.5 - e * (1.0 / 3.0)))
    return jnp.maximum(z, 0.0) + jnp.where(e < 0.01, small, jnp.log(1.0 + e))


def _neg_expm1(x):
    series = -x * (1.0 + x * 0.5 * (1.0 + x * (1.0 / 3.0) * (1.0 + x * 0.25)))
    return jnp.where(x > -0.05, series, 1.0 - jnp.exp(x))


def _gates(row0, xc, pa_f, pi_f, pa_b, pi_b, lam_f, lam_b):
    t = row0 + lax.broadcasted_iota(jnp.int32, xc.shape, 0)
    valid = t < T
    out = []
    for pa, pi_, lam in ((pa_f, pi_f, lam_f), (pa_b, pi_b, lam_b)):
        r = jax.nn.sigmoid(pa)
        gate_i = jax.nn.sigmoid(pi_)
        log_a = -LRU_C * r * _softplus(-lam)
        a = jnp.exp(log_a)
        mult = jnp.sqrt(jnp.maximum(_neg_expm1(2.0 * log_a), 0.0))
        out += [a, jnp.where(valid, mult * (gate_i * xc), 0.0)]
    return tuple(out)


def _rnn_specs():
    seq = pl.BlockSpec((TP, CG), lambda g, b: (b, g))
    return dict(
        seq=seq,
        cw=pl.BlockSpec((CONV_W, CG), lambda g, b: (0, g)),
        cb=pl.BlockSpec((1, CG), lambda g, b: (0, g)),
        w4=pl.BlockSpec((None, CG, 4 * CG), lambda g, b: (g, 0, 0)),
        b4=pl.BlockSpec((None, 1, 4 * CG), lambda g, b: (g, 0, 0)),
        lam=pl.BlockSpec((None, 1, 2 * CG), lambda g, b: (g, 0, 0)),
    )


def _conv(x, xm2, xm1, xp1, cw_ref, cb_ref):
    return cw_ref[0:1, :] * xm2 + cw_ref[1:2, :] * xm1 + cw_ref[2:3, :] * x + cw_ref[3:4, :] * xp1 + cb_ref[...]


TC = 128
N_TC = TP // TC


def _split4(pre):
    return pre[:, :CG], pre[:, CG:2 * CG], pre[:, 2 * CG:3 * CG], pre[:, 3 * CG:]


def _rnn_fwd(xr, xg, cw):
    def body(xr_ref, xg_ref, cw_ref, cb_ref, w4_ref, b4_ref, lam_ref, y_ref, hf_ref, hb_ref, xc_s, af, bf, ab, bb):
        x = xr_ref[...]
        xc_s[...] = _conv(x, *_shifts(x), cw_ref, cb_ref)
        lam = lam_ref[...]

        def chunk(i, _):
            rows = pl.ds(pl.multiple_of(i * TC, TC), TC)
            xc = xc_s[rows, :]
            pre = _nn(xc.astype(BF), w4_ref[...]) + b4_ref[...]
            a_f, b_f, a_b, b_b = _gates(i * TC, xc, *_split4(pre), lam[:, :CG], lam[:, CG:])
            af[rows, :] = a_f
            bf[rows, :] = b_f
            ab[rows, :] = a_b
            bb[rows, :] = b_b
            return 0

        lax.fori_loop(0, N_TC, chunk, 0)
        _scan_pair(af, bf, hf_ref, ab, bb, hb_ref)
        y_ref[...] = (hf_ref[...] + hb_ref[...]) * jax.nn.gelu(xg_ref[...])

    sp = _rnn_specs()
    return pl.pallas_call(
        body, grid=(N_CG, NB), name="rnn_fwd",
        in_specs=[sp["seq"], sp["seq"], sp["cw"], sp["cb"], sp["w4"], sp["b4"], sp["lam"]],
        out_specs=[sp["seq"]] * 3, out_shape=[jax.ShapeDtypeStruct((R, D_RNN), F32)] * 3,
        scratch_shapes=[pltpu.VMEM((TP, CG), F32)] * 5,
        compiler_params=_params("arbitrary", "arbitrary"),
    )(xr, xg, cw["conv_w"], cw["conv_b"], cw["w4"], cw["b4"], cw["lam"])


def _rnn_bwd(dy, xr, xg, hf, hb, cw):
    def body(dy_ref, xr_ref, xg_ref, hf_ref, hb_ref, cw_ref, cb_ref, w4_ref, b4_ref, lam_ref,
             dxr_ref, dxg_ref, dcw_ref, dcb_ref, dw4_ref, db4_ref, dlam_ref,
             xc_s, af_s, ab_s, dhs_s, dhs2_s, lf_s, lb_s, daf_s, dab_s, dxc_s):
        @pl.when(pl.program_id(1) == 0)
        def _():
            for r in (dcw_ref, dcb_ref, dw4_ref, db4_ref, dlam_ref):
                r[...] = jnp.zeros_like(r)

        x = xr_ref[...]
        xc_s[...] = _conv(x, *_shifts(x), cw_ref, cb_ref)
        lam = lam_ref[...]

        def chunk1(i, _):
            rows = pl.ds(pl.multiple_of(i * TC, TC), TC)
            xc = xc_s[rows, :]
            pre = _nn(xc.astype(BF), w4_ref[...]) + b4_ref[...]
            a_f, _, a_b, _ = _gates(i * TC, xc, *_split4(pre), lam[:, :CG], lam[:, CG:])
            af_s[rows, :] = a_f
            ab_s[rows, :] = a_b
            _, vjp_y = jax.vjp(lambda h, g: h * jax.nn.gelu(g), hf_ref[rows, :] + hb_ref[rows, :], xg_ref[rows, :])
            dhs, dxg = vjp_y(dy_ref[rows, :])
            dhs_s[rows, :] = dhs
            dhs2_s[rows, :] = dhs
            dxg_ref[rows, :] = dxg
            return 0

        lax.fori_loop(0, N_TC, chunk1, 0)
        t = lax.broadcasted_iota(jnp.int32, (TP, CG), 0)
        af_s[...] = pltpu.roll(af_s[...], TP - 1, 0)
        ab_s[...] = pltpu.roll(ab_s[...], 1, 0)
        _scan_pair(ab_s, dhs_s, lb_s, af_s, dhs2_s, lf_s)
        daf_s[...] = lf_s[...] * jnp.where(t >= 1, pltpu.roll(hf_ref[...], 1, 0), 0.0)
        dab_s[...] = lb_s[...] * jnp.where(t < TP - 1, pltpu.roll(hb_ref[...], TP - 1, 0), 0.0)

        def chunk2(i, _):
            rows = pl.ds(pl.multiple_of(i * TC, TC), TC)
            xc = xc_s[rows, :]
            xcb = xc.astype(BF)
            pre = _nn(xcb, w4_ref[...]) + b4_ref[...]
            _, vjp_gates = jax.vjp(functools.partial(_gates, i * TC), xc, *_split4(pre), lam[:, :CG], lam[:, CG:])
            dxc, dpa_f, dpi_f, dpa_b, dpi_b, dlam_f, dlam_b = vjp_gates(
                (daf_s[rows, :], lf_s[rows, :], dab_s[rows, :], lb_s[rows, :]))
            dpre = jnp.concatenate([dpa_f, dpi_f, dpa_b, dpi_b], axis=1)
            dpreb = dpre.astype(BF)
            dxc_s[rows, :] = dxc + _nt(dpreb, w4_ref[...])
            dw4_ref[...] += _tn(xcb, dpreb)
            db4_ref[...] += jnp.sum(dpre, axis=0, keepdims=True)
            dlam_ref[...] += jnp.concatenate([dlam_f, dlam_b], axis=1)
            return 0

        lax.fori_loop(0, N_TC, chunk2, 0)
        dxc = dxc_s[...]
        dcb_ref[...] += jnp.sum(dxc, axis=0, keepdims=True)
        for tap, xs in enumerate(_shifts(x)[:2] + (x,) + _shifts(x)[2:]):
            dcw_ref[tap:tap + 1, :] += jnp.sum(xs * dxc, axis=0, keepdims=True)
        dxr_ref[...] = (cw_ref[0:1, :] * jnp.where(t < TP - 2, pltpu.roll(dxc, TP - 2, 0), 0.0)
                        + cw_ref[1:2, :] * jnp.where(t < TP - 1, pltpu.roll(dxc, TP - 1, 0), 0.0)
                        + cw_ref[2:3, :] * dxc
                        + cw_ref[3:4, :] * jnp.where(t >= 1, pltpu.roll(dxc, 1, 0), 0.0))

    sp = _rnn_specs()
    return pl.pallas_call(
        body, grid=(N_CG, NB), name="rnn_bwd",
        in_specs=[sp["seq"]] * 5 + [sp["cw"], sp["cb"], sp["w4"], sp["b4"], sp["lam"]],
        out_specs=[sp["seq"], sp["seq"], sp["cw"], sp["cb"], sp["w4"], sp["b4"], sp["lam"]],
        out_shape=[jax.ShapeDtypeStruct((R, D_RNN), F32), jax.ShapeDtypeStruct((R, D_RNN), F32),
                   jax.ShapeDtypeStruct((CONV_W, D_RNN), F32), jax.ShapeDtypeStruct((1, D_RNN), F32),
                   jax.ShapeDtypeStruct((N_CG, CG, 4 * CG), F32), jax.ShapeDtypeStruct((N_CG, 1, 4 * CG), F32),
                   jax.ShapeDtypeStruct((N_CG, 1, 2 * CG), F32)],
        scratch_shapes=[pltpu.VMEM((TP, CG), F32)] * 10,
        compiler_params=_params("arbitrary", "arbitrary"),
    )(dy, xr, xg, hf, hb, cw["conv_w"], cw["conv_b"], cw["w4"], cw["b4"], cw["lam"])


TD = 128


def _stage_d(hp, o, y, tgt, cw):
    def body(hp_ref, o_ref, y_ref, tgt_ref, ga, gr, wout, ln2, wg, wu, wd,
             do_ref, dy_ref, dh1_ref, mix_ref, dh1b_ref, hn2_ref, dg_ref, du_ref, act_ref, dh2b_ref,
             loss_ref, dga_ref, dgr_ref, dln2_ref):
        i = pl.program_id(0)

        @pl.when(i == 0)
        def _():
            for r in (loss_ref, dga_ref, dgr_ref, dln2_ref):
                r[...] = jnp.zeros_like(r)

        mix_a, vjp_a = jax.vjp(lambda x, g: _rms(x, g, D_ATTN), o_ref[...], ga[...])
        mix_r, vjp_r = jax.vjp(lambda x, g: _rms(x, g, D_RNN), y_ref[...], gr[...])
        mab, mrb = mix_a.astype(BF), mix_r.astype(BF)
        mix_ref[:, :D_ATTN] = mab
        mix_ref[:, D_ATTN:] = mrb
        h1 = hp_ref[...] + _nn(mab, wout[:D_ATTN, :]) + _nn(mrb, wout[D_ATTN:, :])
        hn2, vjp_ln2 = jax.vjp(lambda x, g: _rms(x, g, D), h1, ln2[...])
        hn2b = hn2.astype(BF)
        hn2_ref[...] = hn2b
        act, vjp_act = jax.vjp(lambda g, u: jax.nn.silu(g) * u, _nt(hn2b, wg[...]), _nt(hn2b, wu[...]))
        actb = act.astype(BF)
        act_ref[...] = actb
        h2 = h1 + _nn(actb, wd[...])
        row = i * TD + lax.broadcasted_iota(jnp.int32, (TD, 1), 0)
        t = jnp.where(row >= TP, row - TP, row)
        err = jnp.where((t >= N_META) & (t < T), h2 - tgt_ref[...], 0.0)
        loss_ref[...] += jnp.sum(err * err) * (0.5 / D)
        dh2b = (err * (1.0 / D)).astype(BF)
        dh2b_ref[...] = dh2b
        dg, du = vjp_act(_nt(dh2b, wd[...]))
        dgb, dub = dg.astype(BF), du.astype(BF)
        dg_ref[...] = dgb
        du_ref[...] = dub
        dh1n, dln2 = vjp_ln2(_nn(dgb, wg[...]) + _nn(dub, wu[...]))
        dh1 = err * (1.0 / D) + dh1n
        dh1_ref[...] = dh1
        dh1b = dh1.astype(BF)
        dh1b_ref[...] = dh1b
        dmix = _nt(dh1b, wout[...])
        do, dga = vjp_a(dmix[:, :D_ATTN])
        dyr, dgr = vjp_r(dmix[:, D_ATTN:])
        do_ref[...] = do
        dy_ref[...] = dyr
        dga_ref[...] += dga
        dgr_ref[...] += dgr
        dln2_ref[...] += dln2

    rs = lambda n: _row_spec(n, TD)
    acc = lambda n: pl.BlockSpec((1, n), lambda i: (0, 0))
    return pl.pallas_call(
        body, grid=(R // TD,), name="stage_d",
        in_specs=[rs(D), rs(D_ATTN), rs(D_RNN), rs(D), _const_spec((1, D_ATTN)), _const_spec((1, D_RNN)),
                  _const_spec((D, D)), _const_spec((1, D)), _const_spec((D_FF, D)), _const_spec((D_FF, D)),
                  _const_spec((D_FF, D))],
        out_specs=[rs(D_ATTN), rs(D_RNN), rs(D), rs(D), rs(D), rs(D), rs(D_FF), rs(D_FF), rs(D_FF), rs(D),
                   acc(1), acc(D_ATTN), acc(D_RNN), acc(D)],
        out_shape=[jax.ShapeDtypeStruct((R, D_ATTN), F32), jax.ShapeDtypeStruct((R, D_RNN), F32),
                   jax.ShapeDtypeStruct((R, D), F32), jax.ShapeDtypeStruct((R, D), BF),
                   jax.ShapeDtypeStruct((R, D), BF), jax.ShapeDtypeStruct((R, D), BF),
                   jax.ShapeDtypeStruct((R, D_FF), BF), jax.ShapeDtypeStruct((R, D_FF), BF),
                   jax.ShapeDtypeStruct((R, D_FF), BF), jax.ShapeDtypeStruct((R, D), BF),
                   jax.ShapeDtypeStruct((1, 1), F32), jax.ShapeDtypeStruct((1, D_ATTN), F32),
                   jax.ShapeDtypeStruct((1, D_RNN), F32), jax.ShapeDtypeStruct((1, D), F32)],
        compiler_params=_params("arbitrary"),
    )(hp, o, y, tgt, cw["ga"], cw["gr"], cw["wout"], cw["ln2_g"], cw["wg"], cw["wu"], cw["wd"])


TW = 2176


def _wgrad(a, b, name, tk=None):
    ka, nb = a.shape[1], b.shape[1]
    tk = ka if tk is None else tk

    def body(a_ref, b_ref, o_ref):
        @pl.when(pl.program_id(1) == 0)
        def _():
            o_ref[...] = jnp.zeros_like(o_ref)

        o_ref[...] += _tn(a_ref[...].astype(BF), b_ref[...].astype(BF))

    return pl.pallas_call(
        body, grid=(ka // tk, R // TW), name=name,
        in_specs=[pl.BlockSpec((TW, tk), lambda k, r: (r, k)), pl.BlockSpec((TW, nb), lambda k, r: (r, 0))],
        out_specs=pl.BlockSpec((tk, nb), lambda k, r: (k, 0)),
        out_shape=jax.ShapeDtypeStruct((ka, nb), F32),
        compiler_params=_params("arbitrary", "arbitrary"),
    )(a, b)


def _rope_tables():
    half = QK_ROPE // 2
    freqs = 1.0 / (ROPE_THETA ** (jnp.arange(half, dtype=F32) / half))
    ang = jnp.arange(TP, dtype=F32)[:, None] * freqs[None, :]
    ones = jnp.ones((TP, QK_NOPE), F32)
    zeros = jnp.zeros((TP, QK_NOPE), F32)
    pad1 = jnp.ones((TP, HP - QK_HEAD), F32)
    pad0 = jnp.zeros((TP, HP - QK_HEAD), F32)
    cs = jnp.concatenate([ones, jnp.cos(ang), jnp.cos(ang), pad1], axis=1)
    sn = jnp.concatenate([zeros, jnp.sin(ang), jnp.sin(ang), pad0], axis=1)
    return jnp.tile(cs, (NB, 1)), jnp.tile(sn, (NB, 1))


def _pad_rows(a, lo, hi):
    return jnp.pad(a, ((0, 0), (lo, hi), (0, 0)))


def _compute_weights(w):
    win_t = w["w_in_t"]
    kr = win_t[O_KR:O_KR + QK_ROPE]
    win = jnp.concatenate([win_t[:O_KR], jnp.zeros((QK_NOPE, D), F32), kr,
                           jnp.zeros((HP - QK_HEAD, D), F32), win_t[O_KR + QK_ROPE:]], axis=0)
    wq = _pad_rows(w["w_uq_t"].reshape(N_HEADS, QK_HEAD, Q_LORA), 0, HP - QK_HEAD)
    wkv = w["w_ukv_t"].reshape(N_HEADS, QK_NOPE + V_HEAD, KV_LORA)
    wk = _pad_rows(wkv[:, :QK_NOPE], 0, HP - QK_NOPE)
    wv = wkv[:, QK_NOPE:].reshape(D_ATTN, KV_LORA)
    gates = jnp.stack([w["lru_wa"][0], w["lru_wi"][0], w["lru_wa"][1], w["lru_wi"][1]])
    blk = gates.reshape(4, N_CG, 2, RNN_BW, RNN_BW)
    dense = jnp.einsum("tcaij,ab->tcaibj", blk, jnp.eye(2, dtype=F32)).reshape(4, N_CG, CG, CG)
    w4 = dense.transpose(1, 2, 0, 3).reshape(N_CG, CG, 4 * CG)
    bias = jnp.stack([w["lru_ba"][0], w["lru_bi"][0], w["lru_ba"][1], w["lru_bi"][1]])
    b4 = bias.reshape(4, N_CG, CG).transpose(1, 0, 2).reshape(N_CG, 1, 4 * CG)
    lam = w["lru_lambda"].reshape(2, N_CG, CG).transpose(1, 0, 2).reshape(N_CG, 1, 2 * CG)
    pad_g = lambda g: jnp.pad(g.reshape(1, QK_HEAD), ((0, 0), (0, HP - QK_HEAD)))
    return dict(
        ln1_g=w["ln1_g"].reshape(1, D), win=win.astype(BF), qa_g=w["q_a_norm_g"].reshape(1, Q_LORA),
        wq=wq.astype(BF), kva_g=w["kv_a_norm_g"].reshape(1, KV_LORA), wk=wk.astype(BF), wv=wv.astype(BF),
        q_g=pad_g(w["q_norm_g"]), k_g=pad_g(w["k_norm_g"]),
        conv_w=w["conv_w"].reshape(CONV_W, D_RNN), conv_b=w["conv_b"].reshape(1, D_RNN),
        w4=w4.astype(BF), b4=b4, lam=lam,
        ga=w["attn_out_g"].reshape(1, D_ATTN), gr=w["rnn_out_g"].reshape(1, D_RNN), ln2_g=w["ln2_g"].reshape(1, D),
    )


def _local_step(x, target, meta, w, late_weights, early_grads):
    cw = _compute_weights(w)
    cs, sn = _rope_tables()
    hp = jnp.concatenate([jnp.broadcast_to(meta[None], (NB, N_META, D)), x,
                          jnp.zeros((NB, TP - T, D), F32)], axis=1).reshape(R, D)
    tgt = _pad_rows(target, N_META, TP - T).reshape(R, D)

    pa, xr, xg, q, k, v = _stage_a_fwd(hp, cs, sn, cw)
    o, lse = _attn_fwd(q, k, v)
    y, hf, hb = _rnn_fwd(xr, xg, cw)
    late = late_weights([o, y])
    cw.update(wout=late["w_out"], wg=late["w_gate_t"], wu=late["w_up_t"], wd=late["w_down"])
    (do, dy, dh1, mixb, dh1b, hn2b, dgb, dub, actb, dh2b, loss, dga, dgr, dln2) = _stage_d(hp, o, y, tgt, cw)
    dwout = _wgrad(mixb, dh1b, "wgrad_out")
    dwg = _wgrad(dgb, hn2b, "wgrad_gate", tk=D_FF // 2)
    dwu = _wgrad(dub, hn2b, "wgrad_up", tk=D_FF // 2)
    dwd = _wgrad(actb, dh2b, "wgrad_down", tk=D_FF // 2)
    zero = early_grads(dict(w_out=dwout, w_gate=dwg, w_up=dwu, w_down=dwd))
    cw["conv_b"] = cw["conv_b"] + zero
    dxr, dxg, dcw, dcb, dw4, db4, dlam = _rnn_bwd(dy, xr, xg, hf, hb, cw)
    dq, dk, dv = _attn_bwd(q, k, v, o, lse + zero, do)
    (dhp, dpb, dqrawb, dkrawb, hn1b, cqnb, ckvnb, dln1, dqag, dkvag, dqg, dkg) = _stage_a_bwd(
        dq, dk, dv, dxr, dxg, dh1, hp, pa, cs, sn, cw)

    dwin = _wgrad(dpb, hn1b, "wgrad_in", tk=PC // 2)
    dwq = _wgrad(dqrawb, cqnb, "wgrad_uq")
    dwk = _wgrad(dkrawb, ckvnb, "wgrad_uk")
    dwv = _wgrad(dv, ckvnb, "wgrad_uv")

    dwin_t = jnp.concatenate([dwin[:O_KR], dwin[O_KR + QK_NOPE:O_KR + QK_HEAD], dwin[O_XR:]], axis=0)
    dwq_t = dwq.reshape(N_HEADS, HP, Q_LORA)[:, :QK_HEAD].reshape(N_HEADS * QK_HEAD, Q_LORA)
    dwkv_t = jnp.concatenate([dwk.reshape(N_HEADS, HP, KV_LORA)[:, :QK_NOPE],
                              dwv.reshape(N_HEADS, V_HEAD, KV_LORA)], axis=1).reshape(2 * D_ATTN, KV_LORA)
    d4 = dw4.reshape(N_CG, 2, RNN_BW, 4, 2, RNN_BW)
    dgates = jnp.stack([d4[:, 0, :, :, 0, :], d4[:, 1, :, :, 1, :]], axis=1)
    dgates = dgates.transpose(3, 0, 1, 2, 4).reshape(4, N_HEADS, RNN_BW, RNN_BW)
    dbias = db4.reshape(N_CG, 4, CG).transpose(1, 0, 2).reshape(4, D_RNN)
    dhp3 = dhp.reshape(NB, TP, D)
    grads = dict(
        meta_tokens=jnp.sum(dhp3[:, :N_META], axis=0),
        ln1_g=dln1, w_in_t=dwin_t, q_a_norm_g=dqag, w_uq_t=dwq_t, kv_a_norm_g=dkvag, w_ukv_t=dwkv_t,
        q_norm_g=dqg[:, :QK_HEAD], k_norm_g=dkg[:, :QK_HEAD], conv_w=dcw[None], conv_b=dcb,
        lru_wa=jnp.stack([dgates[0], dgates[2]])[None], lru_ba=jnp.stack([dbias[0], dbias[2]])[None],
        lru_wi=jnp.stack([dgates[1], dgates[3]])[None], lru_bi=jnp.stack([dbias[1], dbias[3]])[None],
        lru_lambda=dlam.reshape(N_CG, 2, CG).transpose(1, 0, 2).reshape(1, 2, D_RNN),
        attn_out_g=dga, rnn_out_g=dgr, ln2_g=dln2,
    )
    return loss[0, 0], dhp3[:, N_META:T], grads, [dhp, dwin]


_ANY = pl.BlockSpec(memory_space=pl.ANY)


def _place():
    return lax.axis_index("x"), lax.axis_index("y"), lax.axis_index("c")


def _other_chips(x, y):
    return [(1 - x, y), (x, 1 - y), (1 - x, 1 - y)]


def _all_gather(xs, name):
    m, n = xs.shape

    def body(x_ref, out_ref, send_sems, recv_sems, local_sem):
        x, y, c = _place()
        me, sibling = (x, y, c), (x, y, 1 - c)
        chips = _other_chips(x, y)

        def rows(px, py, pc):
            return out_ref.at[pl.ds((4 * px + 2 * py + pc) * m, m), :]

        def copy(k, block, to, src=None):
            return pltpu.make_async_remote_copy(
                src_ref=rows(*block) if src is None else src, dst_ref=rows(*block),
                send_sem=send_sems.at[k], recv_sem=recv_sems.at[k], device_id=to, device_id_type=MESH)

        mine = pltpu.make_async_copy(x_ref, rows(*me), local_sem)
        mine.start()
        first = [copy(0, me, sibling, src=x_ref)]
        first += [copy(1 + j, me, (*chip, c), src=x_ref) for j, chip in enumerate(chips)]
        for cp in first:
            cp.start()
        passed = [copy(4 + j, (*chip, c), sibling) for j, chip in enumerate(chips)]
        for j, chip in enumerate(chips):
            copy(1 + j, (*chip, c), me).wait_recv()
            passed[j].start()
        copy(0, sibling, me).wait_recv()
        for j, chip in enumerate(chips):
            copy(4 + j, (*chip, 1 - c), me).wait_recv()
        for cp in first + passed:
            cp.wait_send()
        mine.wait()

    return pl.pallas_call(
        body, name=name, out_shape=jax.ShapeDtypeStruct((8 * m, n), xs.dtype), in_specs=[_ANY], out_specs=_ANY,
        scratch_shapes=[pltpu.SemaphoreType.DMA((7,)), pltpu.SemaphoreType.DMA((7,)), pltpu.SemaphoreType.DMA],
    )(xs)


def _pair_exchange(big, whole, name):
    n_s, _, m, n = big.shape
    n_copies = n_s + len(whole)

    def body(*refs):
        big_ref, whole_refs = refs[0], refs[1:1 + len(whole)]
        rbig_ref, rwhole_refs = refs[1 + len(whole)], refs[2 + len(whole):2 + 2 * len(whole)]
        send_sems, recv_sems = refs[-2:]
        x, y, c = _place()
        sibling = (x, y, 1 - c)
        copies = [pltpu.make_async_remote_copy(
            src_ref=big_ref.at[s, 1 - c], dst_ref=rbig_ref.at[s], send_sem=send_sems.at[s], recv_sem=recv_sems.at[s],
            device_id=sibling, device_id_type=MESH) for s in range(n_s)]
        copies += [pltpu.make_async_remote_copy(
            src_ref=a, dst_ref=r, send_sem=send_sems.at[n_s + i], recv_sem=recv_sems.at[n_s + i],
            device_id=sibling, device_id_type=MESH) for i, (a, r) in enumerate(zip(whole_refs, rwhole_refs))]
        for cp in copies:
            cp.start()
        for cp in copies:
            cp.wait()

    return pl.pallas_call(
        body, name=name,
        out_shape=[jax.ShapeDtypeStruct((n_s, m, n), big.dtype)] + [jax.ShapeDtypeStruct(a.shape, a.dtype) for a in whole],
        in_specs=[_ANY] * (1 + len(whole)), out_specs=[_ANY] * (1 + len(whole)),
        scratch_shapes=[pltpu.SemaphoreType.DMA((n_copies,)), pltpu.SemaphoreType.DMA((n_copies,))],
    )(big, *whole)


def _chip_exchange(big, small, name):
    _, m, n = big.shape
    ms = small.shape[0]

    def body(big_ref, small_ref, rbig_ref, rsmall_ref, send_sems, recv_sems):
        x, y, c = _place()
        copies = []
        for j, (tx, ty) in enumerate(_other_chips(x, y)):
            copies.append(pltpu.make_async_remote_copy(
                src_ref=big_ref.at[2 * tx + ty], dst_ref=rbig_ref.at[j], send_sem=send_sems.at[j],
                recv_sem=recv_sems.at[j], device_id=(tx, ty, c), device_id_type=MESH))
            copies.append(pltpu.make_async_remote_copy(
                src_ref=small_ref, dst_ref=rsmall_ref.at[j], send_sem=send_sems.at[3 + j],
                recv_sem=recv_sems.at[3 + j], device_id=(tx, ty, c), device_id_type=MESH))
        for cp in copies:
            cp.start()
        for cp in copies:
            cp.wait()

    return pl.pallas_call(
        body, name=name,
        out_shape=[jax.ShapeDtypeStruct((3, m, n), big.dtype), jax.ShapeDtypeStruct((3, ms, n), small.dtype)],
        in_specs=[_ANY, _ANY], out_specs=[_ANY, _ANY],
        scratch_shapes=[pltpu.SemaphoreType.DMA((6,)), pltpu.SemaphoreType.DMA((6,))],
    )(big, small)


def _pair_swap(arrs, name):
    k = len(arrs)

    def body(*refs):
        send_sems, recv_sems = refs[-2:]
        x, y, c = _place()
        copies = [pltpu.make_async_remote_copy(
            src_ref=refs[i], dst_ref=refs[k + i], send_sem=send_sems.at[i], recv_sem=recv_sems.at[i],
            device_id=(x, y, 1 - c), device_id_type=MESH) for i in range(k)]
        for cp in copies:
            cp.start()
        for cp in copies:
            cp.wait()

    return pl.pallas_call(
        body, name=name, out_shape=[jax.ShapeDtypeStruct(a.shape, a.dtype) for a in arrs], in_specs=[_ANY] * k,
        out_specs=[_ANY] * k, scratch_shapes=[pltpu.SemaphoreType.DMA((k,)), pltpu.SemaphoreType.DMA((k,))],
    )(*arrs)


_HBM = pl.BlockSpec(memory_space=pltpu.HBM)
_SEM = pl.BlockSpec(memory_space=pltpu.SEMAPHORE)
_EFFECT = pltpu.SideEffectType.DATAFLOW_SIDE_EFFECTING


def _chip_copies(src_ref, land_ref, sems, src_at, land_at, sending):
    x, y, c = _place()
    copies = []
    for j, (tx, ty) in enumerate(_other_chips(x, y)):
        owner = (x, y) if sending else (tx, ty)
        copies.append(pltpu.make_async_remote_copy(
            src_ref=src_at(src_ref, tx, ty), dst_ref=land_at(land_ref, j, *owner, c), send_sem=sems[j],
            recv_sem=sems[3 + j], device_id=(tx, ty, c), device_id_type=MESH))
    return copies


def _chips_start(name, src, land, src_at, land_at):
    def body(src_ref, land_ref, *outs):
        for cp in _chip_copies(src_ref, land_ref, outs[:6], src_at, land_at, True):
            cp.start()
        outs[8][...] = jnp.zeros_like(outs[8])

    outs = pl.pallas_call(
        body, name=name,
        out_shape=(pltpu.SemaphoreType.DMA(()),) * 6 + (pltpu.HBM(src.shape, src.dtype), pltpu.HBM(land.shape, land.dtype),
                                                        jax.ShapeDtypeStruct((8, LANES), F32)),
        in_specs=(_HBM, _HBM), out_specs=(_SEM,) * 6 + (_HBM, _HBM, pl.BlockSpec(memory_space=pltpu.VMEM)),
        input_output_aliases={0: 6, 1: 7},
        compiler_params=pltpu.CompilerParams(has_side_effects=_EFFECT),
    )(pltpu.with_memory_space_constraint(src, pltpu.HBM), pltpu.with_memory_space_constraint(land, pltpu.HBM))
    return outs[:6], outs[6], outs[7], outs[8]


def _chips_wait(name, sems, src, land, after, src_at, land_at):
    def body(src_ref, land_ref, *rest):
        for cp in _chip_copies(src_ref, land_ref, rest[:6], src_at, land_at, False):
            cp.wait_send()
            cp.wait_recv()

    return pl.pallas_call(
        body, name=name, out_shape=(pltpu.HBM(src.shape, src.dtype), pltpu.HBM(land.shape, land.dtype)),
        in_specs=(_HBM, _HBM) + (_SEM,) * 6 + (_ANY,) * len(after), out_specs=(_HBM, _HBM),
        input_output_aliases={0: 0, 1: 1}, compiler_params=pltpu.CompilerParams(has_side_effects=_EFFECT),
    )(src, land, *sems, *after)


def _gather_finish(land, pack, m):
    def body(land_ref, pack_ref, out_ref, send_sems, recv_sems, local_sems):
        x, y, c = _place()

        def rows(px, py, pc, ref=out_ref):
            return ref.at[pl.ds((4 * px + 2 * py + pc) * m, m), :]

        copies = [pltpu.make_async_remote_copy(
            src_ref=rows(tx, ty, c, land_ref), dst_ref=rows(tx, ty, c), send_sem=send_sems.at[j], recv_sem=recv_sems.at[j],
            device_id=(x, y, 1 - c), device_id_type=MESH) for j, (tx, ty) in enumerate(_other_chips(x, y))]
        local = [pltpu.make_async_copy(pack_ref.at[pl.ds(h * m, m), :], rows(x, y, h), local_sems.at[h]) for h in range(2)]
        for cp in copies + local:
            cp.start()
        for j, (tx, ty) in enumerate(_other_chips(x, y)):
            copies[j].wait_send()
            pltpu.make_async_remote_copy(
                src_ref=rows(tx, ty, 1 - c), dst_ref=rows(tx, ty, 1 - c), send_sem=send_sems.at[j],
                recv_sem=recv_sems.at[j], device_id=(x, y, 1 - c), device_id_type=MESH).wait_recv()
        for cp in local:
            cp.wait()

    return pl.pallas_call(
        body, name="gather_late_finish", out_shape=jax.ShapeDtypeStruct(land.shape, land.dtype),
        in_specs=[_ANY, _ANY], out_specs=_ANY, input_output_aliases={0: 0},
        scratch_shapes=[pltpu.SemaphoreType.DMA((3,)), pltpu.SemaphoreType.DMA((3,)), pltpu.SemaphoreType.DMA((2,))],
    )(land, pack)


def _row_tile(rows, cap=512):
    for t in range(cap - cap % 8, 7, -8):
        if rows % t == 0:
            return t
    return rows


def _elementwise(fn, n_out, name, *arrs, out_dtype=F32):
    rows, cols = arrs[0].shape
    tr = _row_tile(rows)
    n_in = len(arrs)

    def body(*refs):
        outs = fn(*[r[...].astype(F32) for r in refs[:n_in]])
        for r, o in zip(refs[n_in:], outs):
            r[...] = o.astype(out_dtype)

    spec = pl.BlockSpec((tr, cols), lambda i: (i, 0))
    return pl.pallas_call(
        body, grid=(rows // tr,), name=name, in_specs=[spec] * n_in, out_specs=[spec] * n_out,
        out_shape=[jax.ShapeDtypeStruct((rows, cols), out_dtype)] * n_out, compiler_params=_params("arbitrary"),
    )(*arrs)


def _add2(a, b):
    return (a + b,)


def _add4(own, r0, r1, r2):
    return ((own + r2) + (r0 + r1),)


def _adamw_math(w, g, m, v):
    m = ADAM_B1 * m + (1.0 - ADAM_B1) * g
    v = ADAM_B2 * v + (1.0 - ADAM_B2) * (g * g)
    m_hat = m / (1.0 - ADAM_B1 ** ADAM_STEP)
    v_hat = v / (1.0 - ADAM_B2 ** ADAM_STEP)
    delta = -ADAM_LR * (m_hat / (jnp.sqrt(v_hat) + ADAM_EPS) + ADAM_WD * w)
    return delta, m, v


WEIGHTS = ["meta_tokens", "ln1_g", "w_in", "q_a_norm_g", "w_uq", "kv_a_norm_g", "w_ukv", "q_norm_g", "k_norm_g",
           "conv_w", "conv_b", "lru_wa", "lru_ba", "lru_wi", "lru_bi", "lru_lambda", "attn_out_g", "rnn_out_g",
           "w_out", "ln2_g", "w_gate", "w_up", "w_down"]
BIG = ["w_in", "w_uq", "w_ukv", "w_out", "w_gate", "w_up", "w_down"]
BIG_T = {"w_in": True, "w_uq": True, "w_ukv": True, "w_out": False, "w_gate": True, "w_up": True, "w_down": False}
BIG_ROWS = {"w_in": 424, "w_uq": 72, "w_ukv": 64, "w_out": 256, "w_gate": 704, "w_up": 704, "w_down": 704}
EARLY = ["w_in", "w_uq", "w_ukv"]
LATE = ["w_out", "w_gate", "w_up", "w_down"]
EARLY_ROWS = 576
LATE_ROWS = 2368
SMALL_SHARDED = ["meta_tokens", "conv_w", "lru_ba", "lru_bi", "lru_lambda"]
SMALL = [n for n in WEIGHTS if n not in BIG]
SMALL_PACK_ROWS = 160
SMALL_ADAM_ROWS = 144


def _offsets(names):
    off, o = {}, 0
    for n in names:
        off[n] = o
        o += BIG_ROWS[n]
    return off


def _shard_pack(names, src, rows):
    parts = [_to_pack_piece(n, src[n]) for n in names]
    used = sum(BIG_ROWS[n] for n in names)
    if rows > used:
        parts.append(jnp.zeros((rows - used, D), F32))
    return jnp.concatenate(parts, axis=0)


def _grad_pack(names, g, rows):
    parts = [g[n].reshape(N_CHIPS, BIG_ROWS[n], D) for n in names]
    used = sum(BIG_ROWS[n] for n in names)
    if rows > used:
        parts.append(jnp.zeros((N_CHIPS, rows - used, D), F32))
    return jnp.concatenate(parts, axis=1).reshape(N_CHIPS, 2, rows // 2, D)


def _own_half(gpack, ci):
    n_s, _, m, n = gpack.shape
    return lax.dynamic_index_in_dim(gpack, ci, axis=1, keepdims=False).reshape(n_s * m, n)


def _both_halves(mine, other, ci):
    return jnp.where(ci == 0, jnp.concatenate([mine, other], axis=0), jnp.concatenate([other, mine], axis=0))


def _to_pack_piece(name, shard):
    a = shard[0].T if BIG_T[name] else shard[0]
    return a.reshape(BIG_ROWS[name], D)


def _from_pack_piece(name, piece, shard_shape):
    _, k, n = shard_shape
    return piece.reshape(n, k).T[None] if BIG_T[name] else piece.reshape(k, n)[None]


def _flat_pack(arrs, rows):
    flat = jnp.concatenate([a.reshape(-1) for a in arrs])
    return jnp.pad(flat, (0, rows * D - flat.shape[0])).reshape(rows, D)


def _flat_unpack(pack, shapes):
    flat, out, o = pack.reshape(-1), [], 0
    for s in shapes:
        n = math.prod(s)
        out.append(flat[o:o + n].reshape(s))
        o += n
    return out


def kernel(x, meta_tokens, ln1_g, w_in, q_a_norm_g, w_uq, kv_a_norm_g, w_ukv, q_norm_g, k_norm_g, conv_w, conv_b, lru_wa, lru_ba, lru_wi, lru_bi, lru_lambda, attn_out_g, rnn_out_g, w_out, ln2_g, w_gate, w_up, w_down, loss_target, m_meta_tokens, m_ln1_g, m_w_in, m_q_a_norm_g, m_w_uq, m_kv_a_norm_g, m_w_ukv, m_q_norm_g, m_k_norm_g, m_conv_w, m_conv_b, m_lru_wa, m_lru_ba, m_lru_wi, m_lru_bi, m_lru_lambda, m_attn_out_g, m_rnn_out_g, m_w_out, m_ln2_g, m_w_gate, m_w_up, m_w_down, v_meta_tokens, v_ln1_g, v_w_in, v_q_a_norm_g, v_w_uq, v_kv_a_norm_g, v_w_ukv, v_q_norm_g, v_k_norm_g, v_conv_w, v_conv_b, v_lru_wa, v_lru_ba, v_lru_wi, v_lru_bi, v_lru_lambda, v_attn_out_g, v_rnn_out_g, v_w_out, v_ln2_g, v_w_gate, v_w_up, v_w_down):
    wts = dict(zip(WEIGHTS, (meta_tokens, ln1_g, w_in, q_a_norm_g, w_uq, kv_a_norm_g, w_ukv, q_norm_g, k_norm_g, conv_w, conv_b, lru_wa, lru_ba, lru_wi, lru_bi, lru_lambda, attn_out_g, rnn_out_g, w_out, ln2_g, w_gate, w_up, w_down)))
    mom = dict(zip(WEIGHTS, (m_meta_tokens, m_ln1_g, m_w_in, m_q_a_norm_g, m_w_uq, m_kv_a_norm_g, m_w_ukv, m_q_norm_g, m_k_norm_g, m_conv_w, m_conv_b, m_lru_wa, m_lru_ba, m_lru_wi, m_lru_bi, m_lru_lambda, m_attn_out_g, m_rnn_out_g, m_w_out, m_ln2_g, m_w_gate, m_w_up, m_w_down)))
    var = dict(zip(WEIGHTS, (v_meta_tokens, v_ln1_g, v_w_in, v_q_a_norm_g, v_w_uq, v_kv_a_norm_g, v_w_ukv, v_q_norm_g, v_k_norm_g, v_conv_w, v_conv_b, v_lru_wa, v_lru_ba, v_lru_wi, v_lru_bi, v_lru_lambda, v_attn_out_g, v_rnn_out_g, v_w_out, v_ln2_g, v_w_gate, v_w_up, v_w_down)))
    xi, yi, ci = _place()
    chip = 2 * xi + yi
    off_e, off_l = _offsets(EARLY), _offsets(LATE)
    half_e, half_l = EARLY_ROWS // 2, LATE_ROWS // 2
    block_rows = lambda ref, j, px, py, c: ref.at[pl.ds((4 * px + 2 * py + c) * half_l, half_l), :]
    whole = lambda ref, tx, ty: ref
    shard_of = lambda ref, tx, ty: ref.at[2 * tx + ty]
    slot = lambda ref, j, px, py, c: ref.at[j]

    pack_l = _shard_pack(LATE, wts, LATE_ROWS).astype(BF)
    sems_l, src_l, land_l, tied = _chips_start(
        "gather_late_start", lax.dynamic_slice_in_dim(pack_l, ci * half_l, half_l, axis=0),
        lax.empty((8 * half_l, D), BF), whole, block_rows)
    pack_e = _shard_pack(EARLY, wts, EARLY_ROWS).astype(BF)
    ge = _all_gather(lax.dynamic_slice_in_dim(pack_e, ci * half_e, half_e, axis=0), "gather_early")
    ge = ge.reshape(N_CHIPS, EARLY_ROWS, D)
    full = {n: ge[:, off_e[n]:off_e[n] + BIG_ROWS[n]] for n in EARLY}

    def late_weights(after):
        _, land = _chips_wait("gather_late_wait", sems_l, src_l, land_l, after, whole, block_rows)
        gl = _gather_finish(land, pack_l, half_l).reshape(N_CHIPS, LATE_ROWS, D)
        part = lambda n: gl[:, off_l[n]:off_l[n] + BIG_ROWS[n]].reshape(N_CHIPS * BIG_ROWS[n], D)
        return dict(w_out=part("w_out"), w_gate_t=part("w_gate"), w_up_t=part("w_up"), w_down=part("w_down"))

    late = {}

    def early_grads(g_late):
        gpack = _grad_pack(LATE, g_late, LATE_ROWS)
        (rbig,) = _pair_exchange(gpack, [], "grad_pair_exchange_late")
        (chip_big,) = _elementwise(_add2, 1, "grad_pair_sum_late", _own_half(gpack, ci),
                                   rbig.reshape(N_CHIPS * half_l, D), out_dtype=BF)
        late["sems"], late["src"], late["land"], zeros = _chips_start(
            "grad_chip_late_start", chip_big.reshape(N_CHIPS, half_l, D), lax.empty((3, half_l, D), BF), shard_of, slot)
        return zeros[0, 0]

    spack = jnp.concatenate([meta_tokens[:, :LANES], meta_tokens[:, LANES:], conv_w[0], lru_ba[0], lru_bi[0],
                             lru_lambda[0], jnp.zeros((6, LANES), F32)], axis=0)
    shalf = lax.dynamic_slice_in_dim(spack, ci * 24, 24, axis=0)
    gs = _all_gather(shalf, "gather_small").reshape(N_CHIPS, 48, LANES)
    cols = lambda a: a.transpose(1, 0, 2).reshape(a.shape[1], N_CHIPS * a.shape[2])
    meta_full = cols(jnp.concatenate([gs[:, 0:16], gs[:, 16:32]], axis=2))
    w = dict(
        w_in_t=full["w_in"].reshape(IN_COLS, D), w_uq_t=full["w_uq"].reshape(N_HEADS * QK_HEAD, Q_LORA),
        w_ukv_t=full["w_ukv"].reshape(2 * D_ATTN, KV_LORA),
        ln1_g=ln1_g, q_a_norm_g=q_a_norm_g, kv_a_norm_g=kv_a_norm_g, q_norm_g=q_norm_g, k_norm_g=k_norm_g,
        conv_w=cols(gs[:, 32:36]), conv_b=conv_b, lru_wa=lru_wa[0], lru_ba=cols(gs[:, 36:38]), lru_wi=lru_wi[0],
        lru_bi=cols(gs[:, 38:40]), lru_lambda=cols(gs[:, 40:42]), attn_out_g=attn_out_g, rnn_out_g=rnn_out_g,
        ln2_g=ln2_g,
    )

    loss_local, grad_x, g, last = _local_step(x + tied[0, 0], loss_target, meta_full, w, late_weights, early_grads)
    loss = lax.psum(loss_local, ("x", "y", "c"))

    gpack = _grad_pack(EARLY, {"w_in": g["w_in_t"], "w_uq": g["w_uq_t"], "w_ukv": g["w_ukv_t"]}, EARLY_ROWS)
    full_shapes = {n: wts[n].shape for n in SMALL}
    full_shapes.update(meta_tokens=(N_META, D), conv_w=(1, CONV_W, D_RNN), lru_ba=(1, 2, D_RNN), lru_bi=(1, 2, D_RNN),
                       lru_lambda=(1, 2, D_RNN))
    gsmall = _flat_pack([g[n] for n in SMALL], SMALL_PACK_ROWS)
    rbig, rsmall = _pair_exchange(gpack, [gsmall], "grad_pair_exchange")
    (chip_big,) = _elementwise(_add2, 1, "grad_pair_sum", _own_half(gpack, ci), rbig.reshape(N_CHIPS * half_e, D),
                               out_dtype=BF)
    (chip_small,) = _elementwise(_add2, 1, "grad_pair_sum_small", gsmall, rsmall)
    chip_big = chip_big.reshape(N_CHIPS, half_e, D)
    xbig, xsmall = _chip_exchange(chip_big, chip_small, "grad_chip_exchange")
    own = lax.dynamic_index_in_dim(chip_big, chip, axis=0, keepdims=False)
    (sum_e,) = _elementwise(_add4, 1, "grad_chip_sum", own, xbig[0], xbig[1], xbig[2])
    (small_sum,) = _elementwise(_add4, 1, "grad_chip_sum_small", chip_small, xsmall[0], xsmall[1], xsmall[2])
    src, land = _chips_wait("grad_chip_late_wait", late["sems"], late["src"], late["land"], last + [sum_e], shard_of, slot)
    own = lax.dynamic_index_in_dim(src, chip, axis=0, keepdims=False)
    (sum_l,) = _elementwise(_add4, 1, "grad_chip_sum_late", own, land[0], land[1], land[2])
    other_e, other_l = _pair_swap([sum_e, sum_l], "grad_pair_swap")
    gshard_e, gshard_l = _both_halves(sum_e, other_e, ci), _both_halves(sum_l, other_l, ci)

    grads = {n: _from_pack_piece(n, gshard_e[off_e[n]:off_e[n] + BIG_ROWS[n]], wts[n].shape) for n in EARLY}
    grads.update({n: _from_pack_piece(n, gshard_l[off_l[n]:off_l[n] + BIG_ROWS[n]], wts[n].shape) for n in LATE})
    small_full = dict(zip(SMALL, _flat_unpack(small_sum, [full_shapes[n] for n in SMALL])))
    for n in SMALL:
        a = small_full[n]
        if n in SMALL_SHARDED:
            width = wts[n].shape[-1]
            a = lax.dynamic_slice_in_dim(a, chip * width, width, axis=a.ndim - 1)
        grads[n] = a.reshape(wts[n].shape)

    delta, new_m, new_v = {}, {}, {}
    for n in BIG:
        two_d = lambda a: a.reshape(a.shape[-2], a.shape[-1])
        d_, m_, v_ = _elementwise(_adamw_math, 3, "adamw_" + n, two_d(wts[n]), two_d(grads[n]), two_d(mom[n]), two_d(var[n]))
        delta[n], new_m[n], new_v[n] = (a.reshape(wts[n].shape) for a in (d_, m_, v_))
    packs = [_flat_pack([src[n] for n in SMALL], SMALL_ADAM_ROWS) for src in (wts, grads, mom, var)]
    outs = _elementwise(_adamw_math, 3, "adamw_small", *packs)
    for dst, o in zip((delta, new_m, new_v), outs):
        dst.update(zip(SMALL, _flat_unpack(o, [wts[n].shape for n in SMALL])))

    return (loss, grad_x, *[grads[n] for n in WEIGHTS], *[delta[n] for n in WEIGHTS],
            *[new_m[n] for n in WEIGHTS], *[new_v[n] for n in WEIGHTS])
```

```python
import functools
import math

import jax
import jax.numpy as jnp
from jax import lax
from jax.experimental import pallas as pl
from jax.experimental.pallas import tpu as pltpu

F32 = jnp.float32
BF = jnp.bfloat16
MESH = pl.DeviceIdType.MESH

D = 1024
SEQ = 2048
N_META = 16
T = N_META + SEQ
N_HEADS = 8
QK_NOPE = 64
QK_ROPE = 32
QK_HEAD = 96
V_HEAD = 64
Q_LORA = 384
KV_LORA = 256
D_ATTN = 512
D_RNN = 512
RNN_BW = 64
CONV_W = 4
LRU_C = 8.0
ROPE_THETA = 10000.0
D_FF = 2816
EPS = 1e-6
IN_COLS = 1696
ADAM_LR, ADAM_B1, ADAM_B2, ADAM_EPS, ADAM_WD, ADAM_STEP = 0.001, 0.9, 0.999, 1e-08, 0.01, 10

LANES = 128
TP = 2176
NB = 2
R = NB * TP
TR = 256
TQ = 544
HP = LANES
PC = 1792
O_CKV, O_KR, O_XR, O_XG = 384, 640, 768, 1280
CG = 128
N_CG = D_RNN // CG
VMEM_LIMIT = 56 * 1024 * 1024
N_CHIPS = 4
SCALE = QK_HEAD ** -0.5
KEY_MASK = -30000.0


def _nt(a, b):
    return lax.dot_general(a, b, (((1,), (1,)), ((), ())), preferred_element_type=F32)


def _nn(a, b):
    return jnp.dot(a, b, preferred_element_type=F32)


def _tn(a, b):
    return lax.dot_general(a, b, (((0,), (0,)), ((), ())), preferred_element_type=F32)


def _rms(x, g, n):
    ms = jnp.sum(x * x, axis=-1, keepdims=True) * (1.0 / n)
    return x * lax.rsqrt(ms + EPS) * g


def _rot_impl(x):
    lane = lax.broadcasted_iota(jnp.int32, x.shape, 1)
    left = pltpu.roll(x, HP - 16, 1)
    right = pltpu.roll(x, 16, 1)
    lo = (lane >= QK_NOPE) & (lane < QK_NOPE + 16)
    hi = (lane >= QK_NOPE + 16) & (lane < QK_HEAD)
    return jnp.where(lo, -left, jnp.where(hi, right, 0.0))


@jax.custom_vjp
def _rot(x):
    return _rot_impl(x)


def _rot_fwd(x):
    return _rot_impl(x), None


def _rot_bwd(_, g):
    return (-_rot_impl(g),)


_rot.defvjp(_rot_fwd, _rot_bwd)


def _head(x, g, cs, sn):
    n = _rms(x, g, QK_HEAD)
    return n * cs + _rot(n) * sn


def _const_spec(shape):
    return pl.BlockSpec(shape, lambda *_: (0,) * len(shape), pipeline_mode=pl.Buffered(1))


def _row_spec(n, tr=TR):
    return pl.BlockSpec((tr, n), lambda i: (i, 0))


def _params(*sem, vmem=VMEM_LIMIT):
    return pltpu.CompilerParams(dimension_semantics=sem, vmem_limit_bytes=vmem)


def _stage_a_fwd(hp, cs, sn, cw):
    def body(hp_ref, cs_ref, sn_ref, ln1, win, qag, wq, kvag, wk, wv, qg, kg,
             pa_ref, xr_ref, xg_ref, q_ref, k_ref, v_ref):
        hn = _rms(hp_ref[...], ln1[...], D).astype(BF)
        p = _nt(hn, win[...])
        pa_ref[...] = p[:, :O_XR]
        xr_ref[...] = p[:, O_XR:O_XG]
        xg_ref[...] = p[:, O_XG:]
        cqn = _rms(p[:, :O_CKV], qag[...], Q_LORA).astype(BF)
        ckvn = _rms(p[:, O_CKV:O_KR], kvag[...], KV_LORA).astype(BF)
        kr = p[:, O_KR:O_XR]
        c, s = cs_ref[...], sn_ref[...]
        mask_lane = lax.broadcasted_iota(jnp.int32, (1, HP), 1) == QK_HEAD
        row = pl.program_id(0) * TR + lax.broadcasted_iota(jnp.int32, (TR, 1), 0)
        key_mask = jnp.where(jnp.where(row >= TP, row - TP, row) < T, 0.0, KEY_MASK)
        qraw = _nt(cqn, wq[...])
        kraw = _nt(ckvn, wk[...])
        for h in range(N_HEADS):
            sl = slice(h * HP, (h + 1) * HP)
            q_ref[:, sl] = jnp.where(mask_lane, 1.0, _head(qraw[:, sl], qg[...], c, s)).astype(BF)
            k_ref[:, sl] = jnp.where(mask_lane, key_mask, _head(kraw[:, sl] + kr, kg[...], c, s)).astype(BF)
        v_ref[...] = _nt(ckvn, wv[...]).astype(BF)

    return pl.pallas_call(
        body, grid=(R // TR,), name="stage_a_fwd",
        in_specs=[_row_spec(D), _row_spec(HP), _row_spec(HP), _const_spec((1, D)), _const_spec((PC, D)),
                  _const_spec((1, Q_LORA)), _const_spec((N_HEADS * HP, Q_LORA)), _const_spec((1, KV_LORA)),
                  _const_spec((N_HEADS * HP, KV_LORA)), _const_spec((D_ATTN, KV_LORA)), _const_spec((1, HP)),
                  _const_spec((1, HP))],
        out_specs=[_row_spec(O_XR), _row_spec(D_RNN), _row_spec(D_RNN), _row_spec(N_HEADS * HP),
                   _row_spec(N_HEADS * HP), _row_spec(D_ATTN)],
        out_shape=[jax.ShapeDtypeStruct((R, O_XR), F32), jax.ShapeDtypeStruct((R, D_RNN), F32),
                   jax.ShapeDtypeStruct((R, D_RNN), F32), jax.ShapeDtypeStruct((R, N_HEADS * HP), BF),
                   jax.ShapeDtypeStruct((R, N_HEADS * HP), BF), jax.ShapeDtypeStruct((R, D_ATTN), BF)],
        compiler_params=_params("arbitrary"),
    )(hp, cs, sn, cw["ln1_g"], cw["win"], cw["qa_g"], cw["wq"], cw["kva_g"], cw["wk"], cw["wv"], cw["q_g"], cw["k_g"])


def _stage_a_bwd(dq, dk, dv, dxr, dxg, dh1, hp, pa, cs, sn, cw):
    def body(dq_ref, dk_ref, dv_ref, dxr_ref, dxg_ref, dh1_ref, hp_ref, pa_ref, cs_ref, sn_ref,
             ln1, win, qag, wq, kvag, wk, wv, qg, kg,
             dhp_ref, dp_ref, dqraw_ref, dkraw_ref, hn_ref, cqn_ref, ckvn_ref,
             dln1_ref, dqag_ref, dkvag_ref, dqg_ref, dkg_ref):
        @pl.when(pl.program_id(0) == 0)
        def _():
            for r in (dln1_ref, dqag_ref, dkvag_ref, dqg_ref, dkg_ref):
                r[...] = jnp.zeros_like(r)

        hn, vjp_ln1 = jax.vjp(lambda h, g: _rms(h, g, D), hp_ref[...], ln1[...])
        hn_ref[...] = hn.astype(BF)
        pa_v = pa_ref[...]
        cqn, vjp_qa = jax.vjp(lambda x, g: _rms(x, g, Q_LORA), pa_v[:, :O_CKV], qag[...])
        ckvn, vjp_kva = jax.vjp(lambda x, g: _rms(x, g, KV_LORA), pa_v[:, O_CKV:O_KR], kvag[...])
        kr = pa_v[:, O_KR:O_XR]
        cqnb, ckvnb = cqn.astype(BF), ckvn.astype(BF)
        cqn_ref[...] = cqnb
        ckvn_ref[...] = ckvnb
        c, s = cs_ref[...], sn_ref[...]
        lane = lax.broadcasted_iota(jnp.int32, (1, HP), 1)
        rope_lanes = ((lane >= QK_NOPE) & (lane < QK_HEAD)).astype(F32)
        dkr = jnp.zeros((TR, HP), F32)
        dqg = jnp.zeros((1, HP), F32)
        dkg = jnp.zeros((1, HP), F32)
        qraw = _nt(cqnb, wq[...])
        kraw = _nt(ckvnb, wk[...])
        for h in range(N_HEADS):
            sl = slice(h * HP, (h + 1) * HP)
            _, vjp_q = jax.vjp(lambda x, g: _head(x, g, c, s), qraw[:, sl], qg[...])
            dqraw, dg = vjp_q(dq_ref[:, sl])
            dqg = dqg + dg
            dqraw_ref[:, sl] = dqraw.astype(BF)
            _, vjp_k = jax.vjp(lambda x, g: _head(x, g, c, s), kraw[:, sl] + kr, kg[...])
            dkraw, dg = vjp_k(dk_ref[:, sl])
            dkg = dkg + dg
            dkraw_ref[:, sl] = dkraw.astype(BF)
            dkr = dkr + dkraw * rope_lanes
        dcq, dqag = vjp_qa(_nn(dqraw_ref[...], wq[...]))
        dckv, dkvag = vjp_kva(_nn(dkraw_ref[...], wk[...]) + _nn(dv_ref[...].astype(BF), wv[...]))
        dpb = jnp.concatenate([dcq, dckv, dkr, dxr_ref[...], dxg_ref[...]], axis=1).astype(BF)
        dp_ref[...] = dpb
        dh, dln1 = vjp_ln1(_nn(dpb, win[...]))
        dhp_ref[...] = dh + dh1_ref[...]
        dln1_ref[...] += dln1
        dqag_ref[...] += dqag
        dkvag_ref[...] += dkvag
        dqg_ref[...] += dqg
        dkg_ref[...] += dkg

    acc = lambda n: pl.BlockSpec((1, n), lambda i: (0, 0))
    return pl.pallas_call(
        body, grid=(R // TR,), name="stage_a_bwd",
        in_specs=[_row_spec(N_HEADS * HP), _row_spec(N_HEADS * HP), _row_spec(D_ATTN), _row_spec(D_RNN),
                  _row_spec(D_RNN), _row_spec(D), _row_spec(D), _row_spec(O_XR), _row_spec(HP), _row_spec(HP),
                  _const_spec((1, D)), _const_spec((PC, D)), _const_spec((1, Q_LORA)),
                  _const_spec((N_HEADS * HP, Q_LORA)), _const_spec((1, KV_LORA)),
                  _const_spec((N_HEADS * HP, KV_LORA)), _const_spec((D_ATTN, KV_LORA)), _const_spec((1, HP)),
                  _const_spec((1, HP))],
        out_specs=[_row_spec(D), _row_spec(PC), _row_spec(N_HEADS * HP), _row_spec(N_HEADS * HP), _row_spec(D),
                   _row_spec(Q_LORA), _row_spec(KV_LORA), acc(D), acc(Q_LORA), acc(KV_LORA), acc(HP), acc(HP)],
        out_shape=[jax.ShapeDtypeStruct((R, D), F32), jax.ShapeDtypeStruct((R, PC), BF),
                   jax.ShapeDtypeStruct((R, N_HEADS * HP), BF), jax.ShapeDtypeStruct((R, N_HEADS * HP), BF),
                   jax.ShapeDtypeStruct((R, D), BF), jax.ShapeDtypeStruct((R, Q_LORA), BF),
                   jax.ShapeDtypeStruct((R, KV_LORA), BF), jax.ShapeDtypeStruct((1, D), F32),
                   jax.ShapeDtypeStruct((1, Q_LORA), F32), jax.ShapeDtypeStruct((1, KV_LORA), F32),
                   jax.ShapeDtypeStruct((1, HP), F32), jax.ShapeDtypeStruct((1, HP), F32)],
        compiler_params=_params("arbitrary"),
    )(dq, dk, dv, dxr, dxg, dh1, hp, pa, cs, sn, cw["ln1_g"], cw["win"], cw["qa_g"], cw["wq"], cw["kva_g"],
      cw["wk"], cw["wv"], cw["q_g"], cw["k_g"])


def _head_mask(half, dtype):
    lane = lax.broadcasted_iota(jnp.int32, (1, 2 * V_HEAD), 1)
    return ((lane >= V_HEAD) == (half == 1)).astype(dtype)


_ATTN_GRID = (NB, N_HEADS // 2, TP // TQ)
_Q_SPEC = pl.BlockSpec((TQ, 2 * HP), lambda b, j, i: (b * (TP // TQ) + i, j))
_K_SPEC = pl.BlockSpec((TP, 2 * HP), lambda b, j, i: (b, j))
_V_SPEC = pl.BlockSpec((TP, 2 * V_HEAD), lambda b, j, i: (b, j))
_O_SPEC = pl.BlockSpec((TQ, 2 * V_HEAD), lambda b, j, i: (b * (TP // TQ) + i, j))
_LSE_SPEC = pl.BlockSpec((None, TQ, 2), lambda b, j, i: (j, b * (TP // TQ) + i, 0))


def _attn_fwd(q, k, v):
    def body(q_ref, k_ref, v_ref, o_ref, lse_ref):
        v2 = v_ref[...]
        o = jnp.zeros((TQ, 2 * V_HEAD), F32)
        lse = []
        for hh in range(2):
            sl = slice(hh * HP, (hh + 1) * HP)
            s = _nt(q_ref[:, sl], k_ref[:, sl]) * SCALE
            m = jnp.max(s, axis=-1, keepdims=True)
            e = jnp.exp(s - m)
            l = jnp.sum(e, axis=-1, keepdims=True)
            o = o + _nn(e.astype(BF), v2 * _head_mask(hh, BF)) * (1.0 / l)
            lse.append(m + jnp.log(l))
        o_ref[...] = o
        lane = lax.broadcasted_iota(jnp.int32, (TQ, 2), 1)
        lse_ref[...] = jnp.where(lane == 0, lse[0], lse[1])

    return pl.pallas_call(
        body, grid=_ATTN_GRID, name="attn_fwd", in_specs=[_Q_SPEC, _K_SPEC, _V_SPEC], out_specs=[_O_SPEC, _LSE_SPEC],
        out_shape=[jax.ShapeDtypeStruct((R, D_ATTN), F32), jax.ShapeDtypeStruct((N_HEADS // 2, R, 2), F32)],
        compiler_params=_params("arbitrary", "arbitrary", "arbitrary"),
    )(q, k, v)


def _attn_bwd(q, k, v, o, lse, do):
    def body(q_ref, k_ref, v_ref, o_ref, lse_ref, do_ref, dq_ref, dk_ref, dv_ref):
        @pl.when(pl.program_id(2) == 0)
        def _():
            dk_ref[...] = jnp.zeros_like(dk_ref)
            dv_ref[...] = jnp.zeros_like(dv_ref)

        do = do_ref[...]
        dob = do.astype(BF)
        do_o = do * o_ref[...]
        v2 = v_ref[...]
        dv_sum = jnp.zeros((TP, 2 * V_HEAD), F32)
        for hh in range(2):
            sl = slice(hh * HP, (hh + 1) * HP)
            qb, kb = q_ref[:, sl], k_ref[:, sl]
            p = jnp.exp(_nt(qb, kb) * SCALE - lse_ref[:, hh:hh + 1])
            dp = _nt(dob, v2 * _head_mask(hh, BF))
            delta = jnp.sum(do_o * _head_mask(hh, F32), axis=-1, keepdims=True)
            dsb = (p * (dp - delta) * SCALE).astype(BF)
            dq_ref[:, sl] = _nn(dsb, kb)
            dk_ref[:, sl] += _tn(dsb, qb)
            dv_sum = dv_sum + _tn(p.astype(BF), dob) * _head_mask(hh, F32)
        dv_ref[...] += dv_sum

    return pl.pallas_call(
        body, grid=_ATTN_GRID, name="attn_bwd", in_specs=[_Q_SPEC, _K_SPEC, _V_SPEC, _O_SPEC, _LSE_SPEC, _O_SPEC],
        out_specs=[_Q_SPEC, _K_SPEC, _V_SPEC],
        out_shape=[jax.ShapeDtypeStruct((R, N_HEADS * HP), F32), jax.ShapeDtypeStruct((R, N_HEADS * HP), F32),
                   jax.ShapeDtypeStruct((R, D_ATTN), F32)],
        compiler_params=_params("arbitrary", "arbitrary", "arbitrary"),
    )(q, k, v, o, lse, do)


def _tile_prefix(a_ref, b_ref, reverse):
    r8 = lax.broadcasted_iota(jnp.int32, (TP, CG), 0) & 7
    a, b = a_ref[...], b_ref[...]
    for s in (1, 2, 4):
        shift = TP - s if reverse else s
        keep = (r8 < 8 - s) if reverse else (r8 >= s)
        b = jnp.where(keep, a * pltpu.roll(b, shift, 0) + b, b)
        a = jnp.where(keep, a * pltpu.roll(a, shift, 0), a)
    a_ref[...] = a
    b_ref[...] = b


def _scan_pair(af_ref, bf_ref, hf_ref, ab_ref, bb_ref, hb_ref):
    _tile_prefix(af_ref, bf_ref, False)
    _tile_prefix(ab_ref, bb_ref, True)
    n_tiles = TP // 8

    def step(i, carry):
        cf, cb = carry
        rf = pl.multiple_of(i * 8, 8)
        rb = pl.multiple_of((n_tiles - 1 - i) * 8, 8)
        hf_ref[pl.ds(rf, 8), :] = bf_ref[pl.ds(rf, 8), :] + af_ref[pl.ds(rf, 8), :] * cf
        hb_ref[pl.ds(rb, 8), :] = bb_ref[pl.ds(rb, 8), :] + ab_ref[pl.ds(rb, 8), :] * cb
        cf = bf_ref[pl.ds(rf + 7, 1), :] + af_ref[pl.ds(rf + 7, 1), :] * cf
        cb = bb_ref[pl.ds(rb, 1), :] + ab_ref[pl.ds(rb, 1), :] * cb
        return cf, cb

    zero = jnp.zeros((1, CG), F32)
    lax.fori_loop(0, n_tiles, step, (zero, zero), unroll=8)


def _shifts(x):
    t = lax.broadcasted_iota(jnp.int32, x.shape, 0)
    xm2 = jnp.where(t >= 2, pltpu.roll(x, 2, 0), 0.0)
    xm1 = jnp.where(t >= 1, pltpu.roll(x, 1, 0), 0.0)
    xp1 = jnp.where(t < TP - 1, pltpu.roll(x, TP - 1, 0), 0.0)
    return xm2, xm1, xp1


def _softplus(z):
    e = jnp.exp(-jnp.abs(z))
    small = e * (1.0 - e * (0.5 - e * (1.0 / 3.0)))
    return jnp.maximum(z, 0.0) + jnp.where(e < 0.01, small, jnp.log(1.0 + e))


def _neg_expm1(x):
    series = -x * (1.0 + x * 0.5 * (1.0 + x * (1.0 / 3.0) * (1.0 + x * 0.25)))
    return jnp.where(x > -0.05, series, 1.0 - jnp.exp(x))


def _gates(row0, xc, pa_f, pi_f, pa_b, pi_b, lam_f, lam_b):
    t = row0 + lax.broadcasted_iota(jnp.int32, xc.shape, 0)
    valid = t < T
    out = []
    for pa, pi_, lam in ((pa_f, pi_f, lam_f), (pa_b, pi_b, lam_b)):
        r = jax.nn.sigmoid(pa)
        gate_i = jax.nn.sigmoid(pi_)
        log_a = -LRU_C * r * _softplus(-lam)
        a = jnp.exp(log_a)
        mult = jnp.sqrt(jnp.maximum(_neg_expm1(2.0 * log_a), 0.0))
        out += [a, jnp.where(valid, mult * (gate_i * xc), 0.0)]
    return tuple(out)


def _rnn_specs():
    seq = pl.BlockSpec((TP, CG), lambda g, b: (b, g))
    return dict(
        seq=seq,
        cw=pl.BlockSpec((CONV_W, CG), lambda g, b: (0, g)),
        cb=pl.BlockSpec((1, CG), lambda g, b: (0, g)),
        w4=pl.BlockSpec((None, CG, 4 * CG), lambda g, b: (g, 0, 0)),
        b4=pl.BlockSpec((None, 1, 4 * CG), lambda g, b: (g, 0, 0)),
        lam=pl.BlockSpec((None, 1, 2 * CG), lambda g, b: (g, 0, 0)),
    )


def _conv(x, xm2, xm1, xp1, cw_ref, cb_ref):
    return cw_ref[0:1, :] * xm2 + cw_ref[1:2, :] * xm1 + cw_ref[2:3, :] * x + cw_ref[3:4, :] * xp1 + cb_ref[...]


TC = 128
N_TC = TP // TC


def _split4(pre):
    return pre[:, :CG], pre[:, CG:2 * CG], pre[:, 2 * CG:3 * CG], pre[:, 3 * CG:]


def _rnn_fwd(xr, xg, cw):
    def body(xr_ref, xg_ref, cw_ref, cb_ref, w4_ref, b4_ref, lam_ref, y_ref, hf_ref, hb_ref, xc_s, af, bf, ab, bb):
        x = xr_ref[...]
        xc_s[...] = _conv(x, *_shifts(x), cw_ref, cb_ref)
        lam = lam_ref[...]

        def chunk(i, _):
            rows = pl.ds(pl.multiple_of(i * TC, TC), TC)
            xc = xc_s[rows, :]
            pre = _nn(xc.astype(BF), w4_ref[...]) + b4_ref[...]
            a_f, b_f, a_b, b_b = _gates(i * TC, xc, *_split4(pre), lam[:, :CG], lam[:, CG:])
            af[rows, :] = a_f
            bf[rows, :] = b_f
            ab[rows, :] = a_b
            bb[rows, :] = b_b
            return 0

        lax.fori_loop(0, N_TC, chunk, 0)
        _scan_pair(af, bf, hf_ref, ab, bb, hb_ref)
        y_ref[...] = (hf_ref[...] + hb_ref[...]) * jax.nn.gelu(xg_ref[...])

    sp = _rnn_specs()
    return pl.pallas_call(
        body, grid=(N_CG, NB), name="rnn_fwd",
        in_specs=[sp["seq"], sp["seq"], sp["cw"], sp["cb"], sp["w4"], sp["b4"], sp["lam"]],
        out_specs=[sp["seq"]] * 3, out_shape=[jax.ShapeDtypeStruct((R, D_RNN), F32)] * 3,
        scratch_shapes=[pltpu.VMEM((TP, CG), F32)] * 5,
        compiler_params=_params("arbitrary", "arbitrary"),
    )(xr, xg, cw["conv_w"], cw["conv_b"], cw["w4"], cw["b4"], cw["lam"])


def _rnn_bwd(dy, xr, xg, hf, hb, cw):
    def body(dy_ref, xr_ref, xg_ref, hf_ref, hb_ref, cw_ref, cb_ref, w4_ref, b4_ref, lam_ref,
             dxr_ref, dxg_ref, dcw_ref, dcb_ref, dw4_ref, db4_ref, dlam_ref,
             xc_s, af_s, ab_s, dhs_s, dhs2_s, lf_s, lb_s, daf_s, dab_s, dxc_s):
        @pl.when(pl.program_id(1) == 0)
        def _():
            for r in (dcw_ref, dcb_ref, dw4_ref, db4_ref, dlam_ref):
                r[...] = jnp.zeros_like(r)

        x = xr_ref[...]
        xc_s[...] = _conv(x, *_shifts(x), cw_ref, cb_ref)
        lam = lam_ref[...]

        def chunk1(i, _):
            rows = pl.ds(pl.multiple_of(i * TC, TC), TC)
            xc = xc_s[rows, :]
            pre = _nn(xc.astype(BF), w4_ref[...]) + b4_ref[...]
            a_f, _, a_b, _ = _gates(i * TC, xc, *_split4(pre), lam[:, :CG], lam[:, CG:])
            af_s[rows, :] = a_f
            ab_s[rows, :] = a_b
            _, vjp_y = jax.vjp(lambda h, g: h * jax.nn.gelu(g), hf_ref[rows, :] + hb_ref[rows, :], xg_ref[rows, :])
            dhs, dxg = vjp_y(dy_ref[rows, :])
            dhs_s[rows, :] = dhs
            dhs2_s[rows, :] = dhs
            dxg_ref[rows, :] = dxg
            return 0

        lax.fori_loop(0, N_TC, chunk1, 0)
        t = lax.broadcasted_iota(jnp.int32, (TP, CG), 0)
        af_s[...] = pltpu.roll(af_s[...], TP - 1, 0)
        ab_s[...] = pltpu.roll(ab_s[...], 1, 0)
        _scan_pair(ab_s, dhs_s, lb_s, af_s, dhs2_s, lf_s)
        daf_s[...] = lf_s[...] * jnp.where(t >= 1, pltpu.roll(hf_ref[...], 1, 0), 0.0)
        dab_s[...] = lb_s[...] * jnp.where(t < TP - 1, pltpu.roll(hb_ref[...], TP - 1, 0), 0.0)

        def chunk2(i, _):
            rows = pl.ds(pl.multiple_of(i * TC, TC), TC)
            xc = xc_s[rows, :]
            xcb = xc.astype(BF)
            pre = _nn(xcb, w4_ref[...]) + b4_ref[...]
            _, vjp_gates = jax.vjp(functools.partial(_gates, i * TC), xc, *_split4(pre), lam[:, :CG], lam[:, CG:])
            dxc, dpa_f, dpi_f, dpa_b, dpi_b, dlam_f, dlam_b = vjp_gates(
                (daf_s[rows, :], lf_s[rows, :], dab_s[rows, :], lb_s[rows, :]))
            dpre = jnp.concatenate([dpa_f, dpi_f, dpa_b, dpi_b], axis=1)
            dpreb = dpre.astype(BF)
            dxc_s[rows, :] = dxc + _nt(dpreb, w4_ref[...])
            dw4_ref[...] += _tn(xcb, dpreb)
            db4_ref[...] += jnp.sum(dpre, axis=0, keepdims=True)
            dlam_ref[...] += jnp.concatenate([dlam_f, dlam_b], axis=1)
            return 0

        lax.fori_loop(0, N_TC, chunk2, 0)
        dxc = dxc_s[...]
        dcb_ref[...] += jnp.sum(dxc, axis=0, keepdims=True)
        for tap, xs in enumerate(_shifts(x)[:2] + (x,) + _shifts(x)[2:]):
            dcw_ref[tap:tap + 1, :] += jnp.sum(xs * dxc, axis=0, keepdims=True)
        dxr_ref[...] = (cw_ref[0:1, :] * jnp.where(t < TP - 2, pltpu.roll(dxc, TP - 2, 0), 0.0)
                        + cw_ref[1:2, :] * jnp.where(t < TP - 1, pltpu.roll(dxc, TP - 1, 0), 0.0)
                        + cw_ref[2:3, :] * dxc
                        + cw_ref[3:4, :] * jnp.where(t >= 1, pltpu.roll(dxc, 1, 0), 0.0))

    sp = _rnn_specs()
    return pl.pallas_call(
        body, grid=(N_CG, NB), name="rnn_bwd",
        in_specs=[sp["seq"]] * 5 + [sp["cw"], sp["cb"], sp["w4"], sp["b4"], sp["lam"]],
        out_specs=[sp["seq"], sp["seq"], sp["cw"], sp["cb"], sp["w4"], sp["b4"], sp["lam"]],
        out_shape=[jax.ShapeDtypeStruct((R, D_RNN), F32), jax.ShapeDtypeStruct((R, D_RNN), F32),
                   jax.ShapeDtypeStruct((CONV_W, D_RNN), F32), jax.ShapeDtypeStruct((1, D_RNN), F32),
                   jax.ShapeDtypeStruct((N_CG, CG, 4 * CG), F32), jax.ShapeDtypeStruct((N_CG, 1, 4 * CG), F32),
                   jax.ShapeDtypeStruct((N_CG, 1, 2 * CG), F32)],
        scratch_shapes=[pltpu.VMEM((TP, CG), F32)] * 10,
        compiler_params=_params("arbitrary", "arbitrary"),
    )(dy, xr, xg, hf, hb, cw["conv_w"], cw["conv_b"], cw["w4"], cw["b4"], cw["lam"])


TD = 256
STAGE_D_VMEM = 58 * 1024 * 1024


def _stage_d(hp, o, y, tgt, cw):
    def body(hp_ref, o_ref, y_ref, tgt_ref, ga, gr, wout, ln2, wg, wu, wd,
             do_ref, dy_ref, dh1_ref, mix_ref, dh1b_ref, hn2_ref, dg_ref, du_ref, act_ref, dh2b_ref,
             loss_ref, dga_ref, dgr_ref, dln2_ref):
        i = pl.program_id(0)

        @pl.when(i == 0)
        def _():
            for r in (loss_ref, dga_ref, dgr_ref, dln2_ref):
                r[...] = jnp.zeros_like(r)

        mix_a, vjp_a = jax.vjp(lambda x, g: _rms(x, g, D_ATTN), o_ref[...], ga[...])
        mix_r, vjp_r = jax.vjp(lambda x, g: _rms(x, g, D_RNN), y_ref[...], gr[...])
        mab, mrb = mix_a.astype(BF), mix_r.astype(BF)
        mix_ref[:, :D_ATTN] = mab
        mix_ref[:, D_ATTN:] = mrb
        h1 = hp_ref[...] + _nn(mab, wout[:D_ATTN, :]) + _nn(mrb, wout[D_ATTN:, :])
        hn2, vjp_ln2 = jax.vjp(lambda x, g: _rms(x, g, D), h1, ln2[...])
        hn2b = hn2.astype(BF)
        hn2_ref[...] = hn2b
        act, vjp_act = jax.vjp(lambda g, u: jax.nn.silu(g) * u, _nt(hn2b, wg[...]), _nt(hn2b, wu[...]))
        actb = act.astype(BF)
        act_ref[...] = actb
        h2 = h1 + _nn(actb, wd[...])
        row = i * TD + lax.broadcasted_iota(jnp.int32, (TD, 1), 0)
        t = jnp.where(row >= TP, row - TP, row)
        err = jnp.where((t >= N_META) & (t < T), h2 - tgt_ref[...], 0.0)
        loss_ref[...] += jnp.sum(err * err) * (0.5 / D)
        dh2b = (err * (1.0 / D)).astype(BF)
        dh2b_ref[...] = dh2b
        dg, du = vjp_act(_nt(dh2b, wd[...]))
        dgb, dub = dg.astype(BF), du.astype(BF)
        dg_ref[...] = dgb
        du_ref[...] = dub
        dh1n, dln2 = vjp_ln2(_nn(dgb, wg[...]) + _nn(dub, wu[...]))
        dh1 = err * (1.0 / D) + dh1n
        dh1_ref[...] = dh1
        dh1b = dh1.astype(BF)
        dh1b_ref[...] = dh1b
        dmix = _nt(dh1b, wout[...])
        do, dga = vjp_a(dmix[:, :D_ATTN])
        dyr, dgr = vjp_r(dmix[:, D_ATTN:])
        do_ref[...] = do
        dy_ref[...] = dyr
        dga_ref[...] += dga
        dgr_ref[...] += dgr
        dln2_ref[...] += dln2

    rs = lambda n: _row_spec(n, TD)
    acc = lambda n: pl.BlockSpec((1, n), lambda i: (0, 0))
    return pl.pallas_call(
        body, grid=(R // TD,), name="stage_d",
        in_specs=[rs(D), rs(D_ATTN), rs(D_RNN), rs(D), _const_spec((1, D_ATTN)), _const_spec((1, D_RNN)),
                  _const_spec((D, D)), _const_spec((1, D)), _const_spec((D_FF, D)), _const_spec((D_FF, D)),
                  _const_spec((D_FF, D))],
        out_specs=[rs(D_ATTN), rs(D_RNN), rs(D), rs(D), rs(D), rs(D), rs(D_FF), rs(D_FF), rs(D_FF), rs(D),
                   acc(1), acc(D_ATTN), acc(D_RNN), acc(D)],
        out_shape=[jax.ShapeDtypeStruct((R, D_ATTN), F32), jax.ShapeDtypeStruct((R, D_RNN), F32),
                   jax.ShapeDtypeStruct((R, D), F32), jax.ShapeDtypeStruct((R, D), BF),
                   jax.ShapeDtypeStruct((R, D), BF), jax.ShapeDtypeStruct((R, D), BF),
                   jax.ShapeDtypeStruct((R, D_FF), BF), jax.ShapeDtypeStruct((R, D_FF), BF),
                   jax.ShapeDtypeStruct((R, D_FF), BF), jax.ShapeDtypeStruct((R, D), BF),
                   jax.ShapeDtypeStruct((1, 1), F32), jax.ShapeDtypeStruct((1, D_ATTN), F32),
                   jax.ShapeDtypeStruct((1, D_RNN), F32), jax.ShapeDtypeStruct((1, D), F32)],
        compiler_params=_params("arbitrary", vmem=STAGE_D_VMEM),
    )(hp, o, y, tgt, cw["ga"], cw["gr"], cw["wout"], cw["ln2_g"], cw["wg"], cw["wu"], cw["wd"])


TW = 2176


def _wgrad(a, b, name, tk=None):
    ka, nb = a.shape[1], b.shape[1]
    tk = ka if tk is None else tk

    def body(a_ref, b_ref, o_ref):
        @pl.when(pl.program_id(1) == 0)
        def _():
            o_ref[...] = jnp.zeros_like(o_ref)

        o_ref[...] += _tn(a_ref[...].astype(BF), b_ref[...].astype(BF))

    return pl.pallas_call(
        body, grid=(ka // tk, R // TW), name=name,
        in_specs=[pl.BlockSpec((TW, tk), lambda k, r: (r, k)), pl.BlockSpec((TW, nb), lambda k, r: (r, 0))],
        out_specs=pl.BlockSpec((tk, nb), lambda k, r: (k, 0)),
        out_shape=jax.ShapeDtypeStruct((ka, nb), F32),
        compiler_params=_params("arbitrary", "arbitrary"),
    )(a, b)


def _rope_tables():
    half = QK_ROPE // 2
    freqs = 1.0 / (ROPE_THETA ** (jnp.arange(half, dtype=F32) / half))
    ang = jnp.arange(TP, dtype=F32)[:, None] * freqs[None, :]
    ones = jnp.ones((TP, QK_NOPE), F32)
    zeros = jnp.zeros((TP, QK_NOPE), F32)
    pad1 = jnp.ones((TP, HP - QK_HEAD), F32)
    pad0 = jnp.zeros((TP, HP - QK_HEAD), F32)
    cs = jnp.concatenate([ones, jnp.cos(ang), jnp.cos(ang), pad1], axis=1)
    sn = jnp.concatenate([zeros, jnp.sin(ang), jnp.sin(ang), pad0], axis=1)
    return jnp.tile(cs, (NB, 1)), jnp.tile(sn, (NB, 1))


def _pad_rows(a, lo, hi):
    return jnp.pad(a, ((0, 0), (lo, hi), (0, 0)))


def _compute_weights(w):
    win_t = w["w_in_t"]
    kr = win_t[O_KR:O_KR + QK_ROPE]
    win = jnp.concatenate([win_t[:O_KR], jnp.zeros((QK_NOPE, D), F32), kr,
                           jnp.zeros((HP - QK_HEAD, D), F32), win_t[O_KR + QK_ROPE:]], axis=0)
    wq = _pad_rows(w["w_uq_t"].reshape(N_HEADS, QK_HEAD, Q_LORA), 0, HP - QK_HEAD)
    wkv = w["w_ukv_t"].reshape(N_HEADS, QK_NOPE + V_HEAD, KV_LORA)
    wk = _pad_rows(wkv[:, :QK_NOPE], 0, HP - QK_NOPE)
    wv = wkv[:, QK_NOPE:].reshape(D_ATTN, KV_LORA)
    gates = jnp.stack([w["lru_wa"][0], w["lru_wi"][0], w["lru_wa"][1], w["lru_wi"][1]])
    blk = gates.reshape(4, N_CG, 2, RNN_BW, RNN_BW)
    dense = jnp.einsum("tcaij,ab->tcaibj", blk, jnp.eye(2, dtype=F32)).reshape(4, N_CG, CG, CG)
    w4 = dense.transpose(1, 2, 0, 3).reshape(N_CG, CG, 4 * CG)
    bias = jnp.stack([w["lru_ba"][0], w["lru_bi"][0], w["lru_ba"][1], w["lru_bi"][1]])
    b4 = bias.reshape(4, N_CG, CG).transpose(1, 0, 2).reshape(N_CG, 1, 4 * CG)
    lam = w["lru_lambda"].reshape(2, N_CG, CG).transpose(1, 0, 2).reshape(N_CG, 1, 2 * CG)
    pad_g = lambda g: jnp.pad(g.reshape(1, QK_HEAD), ((0, 0), (0, HP - QK_HEAD)))
    return dict(
        ln1_g=w["ln1_g"].reshape(1, D), win=win.astype(BF), qa_g=w["q_a_norm_g"].reshape(1, Q_LORA),
        wq=wq.astype(BF).reshape(N_HEADS * HP, Q_LORA), kva_g=w["kv_a_norm_g"].reshape(1, KV_LORA),
        wk=wk.astype(BF).reshape(N_HEADS * HP, KV_LORA), wv=wv.astype(BF),
        q_g=pad_g(w["q_norm_g"]), k_g=pad_g(w["k_norm_g"]),
        conv_w=w["conv_w"].reshape(CONV_W, D_RNN), conv_b=w["conv_b"].reshape(1, D_RNN),
        w4=w4.astype(BF), b4=b4, lam=lam,
        ga=w["attn_out_g"].reshape(1, D_ATTN), gr=w["rnn_out_g"].reshape(1, D_RNN), ln2_g=w["ln2_g"].reshape(1, D),
    )


def _local_step(x, target, meta, w, late_weights, early_grads):
    cw = _compute_weights(w)
    cs, sn = _rope_tables()
    hp = jnp.concatenate([jnp.broadcast_to(meta[None], (NB, N_META, D)), x,
                          jnp.zeros((NB, TP - T, D), F32)], axis=1).reshape(R, D)
    tgt = _pad_rows(target, N_META, TP - T).reshape(R, D)

    pa, xr, xg, q, k, v = _stage_a_fwd(hp, cs, sn, cw)
    o, lse = _attn_fwd(q, k, v)
    y, hf, hb = _rnn_fwd(xr, xg, cw)
    late = late_weights([o, y])
    cw.update(wout=late["w_out"], wg=late["w_gate_t"], wu=late["w_up_t"], wd=late["w_down"])
    (do, dy, dh1, mixb, dh1b, hn2b, dgb, dub, actb, dh2b, loss, dga, dgr, dln2) = _stage_d(hp, o, y, tgt, cw)
    dwout = _wgrad(mixb, dh1b, "wgrad_out")
    dwg = _wgrad(dgb, hn2b, "wgrad_gate", tk=D_FF // 2)
    dwu = _wgrad(dub, hn2b, "wgrad_up", tk=D_FF // 2)
    dwd = _wgrad(actb, dh2b, "wgrad_down", tk=D_FF // 2)
    zero = early_grads(dict(w_out=dwout, w_gate=dwg, w_up=dwu, w_down=dwd))
    cw["conv_b"] = cw["conv_b"] + zero
    dxr, dxg, dcw, dcb, dw4, db4, dlam = _rnn_bwd(dy, xr, xg, hf, hb, cw)
    dq, dk, dv = _attn_bwd(q, k, v, o, lse, do)
    (dhp, dpb, dqrawb, dkrawb, hn1b, cqnb, ckvnb, dln1, dqag, dkvag, dqg, dkg) = _stage_a_bwd(
        dq, dk, dv, dxr, dxg, dh1, hp, pa, cs, sn, cw)

    dwin = _wgrad(dpb, hn1b, "wgrad_in", tk=PC // 2)
    dwq = _wgrad(dqrawb, cqnb, "wgrad_uq")
    dwk = _wgrad(dkrawb, ckvnb, "wgrad_uk")
    dwv = _wgrad(dv, ckvnb, "wgrad_uv")

    dwin_t = jnp.concatenate([dwin[:O_KR], dwin[O_KR + QK_NOPE:O_KR + QK_HEAD], dwin[O_XR:]], axis=0)
    dwq_t = dwq.reshape(N_HEADS, HP, Q_LORA)[:, :QK_HEAD].reshape(N_HEADS * QK_HEAD, Q_LORA)
    dwkv_t = jnp.concatenate([dwk.reshape(N_HEADS, HP, KV_LORA)[:, :QK_NOPE],
                              dwv.reshape(N_HEADS, V_HEAD, KV_LORA)], axis=1).reshape(2 * D_ATTN, KV_LORA)
    d4 = dw4.reshape(N_CG, 2, RNN_BW, 4, 2, RNN_BW)
    dgates = jnp.stack([d4[:, 0, :, :, 0, :], d4[:, 1, :, :, 1, :]], axis=1)
    dgates = dgates.transpose(3, 0, 1, 2, 4).reshape(4, N_HEADS, RNN_BW, RNN_BW)
    dbias = db4.reshape(N_CG, 4, CG).transpose(1, 0, 2).reshape(4, D_RNN)
    dhp3 = dhp.reshape(NB, TP, D)
    grads = dict(
        meta_tokens=jnp.sum(dhp3[:, :N_META], axis=0),
        ln1_g=dln1, w_in_t=dwin_t, q_a_norm_g=dqag, w_uq_t=dwq_t, kv_a_norm_g=dkvag, w_ukv_t=dwkv_t,
        q_norm_g=dqg[:, :QK_HEAD], k_norm_g=dkg[:, :QK_HEAD], conv_w=dcw[None], conv_b=dcb,
        lru_wa=jnp.stack([dgates[0], dgates[2]])[None], lru_ba=jnp.stack([dbias[0], dbias[2]])[None],
        lru_wi=jnp.stack([dgates[1], dgates[3]])[None], lru_bi=jnp.stack([dbias[1], dbias[3]])[None],
        lru_lambda=dlam.reshape(N_CG, 2, CG).transpose(1, 0, 2).reshape(1, 2, D_RNN),
        attn_out_g=dga, rnn_out_g=dgr, ln2_g=dln2,
    )
    return loss[0, 0], dhp3[:, N_META:T], grads, [dhp, dwin]


_ANY = pl.BlockSpec(memory_space=pl.ANY)


def _place():
    return lax.axis_index("x"), lax.axis_index("y"), lax.axis_index("c")


def _other_chips(x, y):
    return [(1 - x, y), (x, 1 - y), (1 - x, 1 - y)]


def _all_gather(xs, name):
    m, n = xs.shape

    def body(x_ref, out_ref, send_sems, recv_sems, local_sem):
        x, y, c = _place()
        me, sibling = (x, y, c), (x, y, 1 - c)
        chips = _other_chips(x, y)

        def rows(px, py, pc):
            return out_ref.at[pl.ds((4 * px + 2 * py + pc) * m, m), :]

        def copy(k, block, to, src=None):
            return pltpu.make_async_remote_copy(
                src_ref=rows(*block) if src is None else src, dst_ref=rows(*block),
                send_sem=send_sems.at[k], recv_sem=recv_sems.at[k], device_id=to, device_id_type=MESH)

        mine = pltpu.make_async_copy(x_ref, rows(*me), local_sem)
        mine.start()
        first = [copy(0, me, sibling, src=x_ref)]
        first += [copy(1 + j, me, (*chip, c), src=x_ref) for j, chip in enumerate(chips)]
        for cp in first:
            cp.start()
        passed = [copy(4 + j, (*chip, c), sibling) for j, chip in enumerate(chips)]
        for j, chip in enumerate(chips):
            copy(1 + j, (*chip, c), me).wait_recv()
            passed[j].start()
        copy(0, sibling, me).wait_recv()
        for j, chip in enumerate(chips):
            copy(4 + j, (*chip, 1 - c), me).wait_recv()
        for cp in first + passed:
            cp.wait_send()
        mine.wait()

    return pl.pallas_call(
        body, name=name, out_shape=jax.ShapeDtypeStruct((8 * m, n), xs.dtype), in_specs=[_ANY], out_specs=_ANY,
        scratch_shapes=[pltpu.SemaphoreType.DMA((7,)), pltpu.SemaphoreType.DMA((7,)), pltpu.SemaphoreType.DMA],
    )(xs)


def _pair_exchange(big, whole, name):
    n_s, _, m, n = big.shape
    n_copies = n_s + len(whole)

    def body(*refs):
        big_ref, whole_refs = refs[0], refs[1:1 + len(whole)]
        rbig_ref, rwhole_refs = refs[1 + len(whole)], refs[2 + len(whole):2 + 2 * len(whole)]
        send_sems, recv_sems = refs[-2:]
        x, y, c = _place()
        sibling = (x, y, 1 - c)
        copies = [pltpu.make_async_remote_copy(
            src_ref=big_ref.at[s, 1 - c], dst_ref=rbig_ref.at[s], send_sem=send_sems.at[s], recv_sem=recv_sems.at[s],
            device_id=sibling, device_id_type=MESH) for s in range(n_s)]
        copies += [pltpu.make_async_remote_copy(
            src_ref=a, dst_ref=r, send_sem=send_sems.at[n_s + i], recv_sem=recv_sems.at[n_s + i],
            device_id=sibling, device_id_type=MESH) for i, (a, r) in enumerate(zip(whole_refs, rwhole_refs))]
        for cp in copies:
            cp.start()
        for cp in copies:
            cp.wait()

    return pl.pallas_call(
        body, name=name,
        out_shape=[jax.ShapeDtypeStruct((n_s, m, n), big.dtype)] + [jax.ShapeDtypeStruct(a.shape, a.dtype) for a in whole],
        in_specs=[_ANY] * (1 + len(whole)), out_specs=[_ANY] * (1 + len(whole)),
        scratch_shapes=[pltpu.SemaphoreType.DMA((n_copies,)), pltpu.SemaphoreType.DMA((n_copies,))],
    )(big, *whole)


def _chip_exchange(big, small, name):
    _, m, n = big.shape
    ms = small.shape[0]

    def body(big_ref, small_ref, rbig_ref, rsmall_ref, send_sems, recv_sems):
        x, y, c = _place()
        copies = []
        for j, (tx, ty) in enumerate(_other_chips(x, y)):
            copies.append(pltpu.make_async_remote_copy(
                src_ref=big_ref.at[2 * tx + ty], dst_ref=rbig_ref.at[j], send_sem=send_sems.at[j],
                recv_sem=recv_sems.at[j], device_id=(tx, ty, c), device_id_type=MESH))
            copies.append(pltpu.make_async_remote_copy(
                src_ref=small_ref, dst_ref=rsmall_ref.at[j], send_sem=send_sems.at[3 + j],
                recv_sem=recv_sems.at[3 + j], device_id=(tx, ty, c), device_id_type=MESH))
        for cp in copies:
            cp.start()
        for cp in copies:
            cp.wait()

    return pl.pallas_call(
        body, name=name,
        out_shape=[jax.ShapeDtypeStruct((3, m, n), big.dtype), jax.ShapeDtypeStruct((3, ms, n), small.dtype)],
        in_specs=[_ANY, _ANY], out_specs=[_ANY, _ANY],
        scratch_shapes=[pltpu.SemaphoreType.DMA((6,)), pltpu.SemaphoreType.DMA((6,))],
    )(big, small)


def _pair_swap(arrs, name):
    k = len(arrs)

    def body(*refs):
        send_sems, recv_sems = refs[-2:]
        x, y, c = _place()
        copies = [pltpu.make_async_remote_copy(
            src_ref=refs[i], dst_ref=refs[k + i], send_sem=send_sems.at[i], recv_sem=recv_sems.at[i],
            device_id=(x, y, 1 - c), device_id_type=MESH) for i in range(k)]
        for cp in copies:
            cp.start()
        for cp in copies:
            cp.wait()

    return pl.pallas_call(
        body, name=name, out_shape=[jax.ShapeDtypeStruct(a.shape, a.dtype) for a in arrs], in_specs=[_ANY] * k,
        out_specs=[_ANY] * k, scratch_shapes=[pltpu.SemaphoreType.DMA((k,)), pltpu.SemaphoreType.DMA((k,))],
    )(*arrs)


_HBM = pl.BlockSpec(memory_space=pltpu.HBM)
_SEM = pl.BlockSpec(memory_space=pltpu.SEMAPHORE)
_EFFECT = pltpu.SideEffectType.DATAFLOW_SIDE_EFFECTING


def _chip_copies(src_ref, land_ref, sems, src_at, land_at, sending):
    x, y, c = _place()
    copies = []
    for j, (tx, ty) in enumerate(_other_chips(x, y)):
        owner = (x, y) if sending else (tx, ty)
        copies.append(pltpu.make_async_remote_copy(
            src_ref=src_at(src_ref, tx, ty), dst_ref=land_at(land_ref, j, *owner, c), send_sem=sems[j],
            recv_sem=sems[3 + j], device_id=(tx, ty, c), device_id_type=MESH))
    return copies


def _chips_start(name, src, land, src_at, land_at):
    def body(src_ref, land_ref, *outs):
        for cp in _chip_copies(src_ref, land_ref, outs[:6], src_at, land_at, True):
            cp.start()
        outs[8][...] = jnp.zeros_like(outs[8])

    outs = pl.pallas_call(
        body, name=name,
        out_shape=(pltpu.SemaphoreType.DMA(()),) * 6 + (pltpu.HBM(src.shape, src.dtype), pltpu.HBM(land.shape, land.dtype),
                                                        jax.ShapeDtypeStruct((8, LANES), F32)),
        in_specs=(_HBM, _HBM), out_specs=(_SEM,) * 6 + (_HBM, _HBM, pl.BlockSpec(memory_space=pltpu.VMEM)),
        input_output_aliases={0: 6, 1: 7},
        compiler_params=pltpu.CompilerParams(has_side_effects=_EFFECT),
    )(pltpu.with_memory_space_constraint(src, pltpu.HBM), pltpu.with_memory_space_constraint(land, pltpu.HBM))
    return outs[:6], outs[6], outs[7], outs[8]


def _chips_wait(name, sems, src, land, after, src_at, land_at):
    def body(src_ref, land_ref, *rest):
        for cp in _chip_copies(src_ref, land_ref, rest[:6], src_at, land_at, False):
            cp.wait_send()
            cp.wait_recv()

    return pl.pallas_call(
        body, name=name, out_shape=(pltpu.HBM(src.shape, src.dtype), pltpu.HBM(land.shape, land.dtype)),
        in_specs=(_HBM, _HBM) + (_SEM,) * 6 + (_ANY,) * len(after), out_specs=(_HBM, _HBM),
        input_output_aliases={0: 0, 1: 1}, compiler_params=pltpu.CompilerParams(has_side_effects=_EFFECT),
    )(src, land, *sems, *after)


def _gather_finish(land, pack, m):
    def body(land_ref, pack_ref, out_ref, stage, send_sems, recv_sems, load_sems, store_sems):
        x, y, c = _place()

        def rows(px, py, pc, ref=out_ref):
            return ref.at[pl.ds((4 * px + 2 * py + pc) * m, m), :]

        copies = [pltpu.make_async_remote_copy(
            src_ref=rows(tx, ty, c, land_ref), dst_ref=rows(tx, ty, c), send_sem=send_sems.at[j], recv_sem=recv_sems.at[j],
            device_id=(x, y, 1 - c), device_id_type=MESH) for j, (tx, ty) in enumerate(_other_chips(x, y))]
        loads = [pltpu.make_async_copy(pack_ref.at[pl.ds(h * m, m), :], stage.at[h], load_sems.at[h]) for h in range(2)]
        stores = [pltpu.make_async_copy(stage.at[h], rows(x, y, h), store_sems.at[h]) for h in range(2)]
        for cp in copies + loads:
            cp.start()
        for h in range(2):
            loads[h].wait()
            stores[h].start()
        for j, (tx, ty) in enumerate(_other_chips(x, y)):
            copies[j].wait_send()
            pltpu.make_async_remote_copy(
                src_ref=rows(tx, ty, 1 - c), dst_ref=rows(tx, ty, 1 - c), send_sem=send_sems.at[j],
                recv_sem=recv_sems.at[j], device_id=(x, y, 1 - c), device_id_type=MESH).wait_recv()
        for cp in stores:
            cp.wait()

    return pl.pallas_call(
        body, name="gather_late_finish", out_shape=jax.ShapeDtypeStruct(land.shape, land.dtype),
        in_specs=[_ANY, _ANY], out_specs=_ANY, input_output_aliases={0: 0},
        scratch_shapes=[pltpu.VMEM((2, m, land.shape[1]), land.dtype), pltpu.SemaphoreType.DMA((3,)),
                        pltpu.SemaphoreType.DMA((3,)), pltpu.SemaphoreType.DMA((2,)), pltpu.SemaphoreType.DMA((2,))],
    )(land, pack)


def _row_tile(rows, cap=512):
    for t in range(cap - cap % 8, 7, -8):
        if rows % t == 0:
            return t
    return rows


def _elementwise(fn, n_out, name, *arrs, out_dtype=F32):
    rows, cols = arrs[0].shape
    tr = _row_tile(rows)
    n_in = len(arrs)

    def body(*refs):
        outs = fn(*[r[...].astype(F32) for r in refs[:n_in]])
        for r, o in zip(refs[n_in:], outs):
            r[...] = o.astype(out_dtype)

    spec = pl.BlockSpec((tr, cols), lambda i: (i, 0))
    return pl.pallas_call(
        body, grid=(rows // tr,), name=name, in_specs=[spec] * n_in, out_specs=[spec] * n_out,
        out_shape=[jax.ShapeDtypeStruct((rows, cols), out_dtype)] * n_out, compiler_params=_params("arbitrary"),
    )(*arrs)


def _pair_sum(gpack, rbig, ci, name):
    n_s, _, m, n = gpack.shape
    tr = _row_tile(m)

    def body(c_ref, g_ref, r_ref, o_ref):
        o_ref[...] = (g_ref[...] + r_ref[...]).astype(BF)

    return pl.pallas_call(
        body, name=name, out_shape=jax.ShapeDtypeStruct((n_s, m, n), BF),
        grid_spec=pltpu.PrefetchScalarGridSpec(
            num_scalar_prefetch=1, grid=(n_s, m // tr),
            in_specs=[pl.BlockSpec((None, None, tr, n), lambda s, i, c: (s, c[0], i, 0)),
                      pl.BlockSpec((None, tr, n), lambda s, i, c: (s, i, 0))],
            out_specs=pl.BlockSpec((None, tr, n), lambda s, i, c: (s, i, 0))),
        compiler_params=_params("arbitrary", "arbitrary"),
    )(ci.reshape(1), gpack, rbig)


def _chip_sum(sums, landed, chip, name):
    _, m, n = sums.shape
    tr = _row_tile(m)

    def body(c_ref, own_ref, r0_ref, r1_ref, r2_ref, o_ref):
        f = lambda r: r[...].astype(F32)
        o_ref[...] = _add4(f(own_ref), f(r0_ref), f(r1_ref), f(r2_ref))[0]

    slot = lambda j: pl.BlockSpec((None, tr, n), lambda i, c: (j, i, 0))
    return pl.pallas_call(
        body, name=name, out_shape=jax.ShapeDtypeStruct((m, n), F32),
        grid_spec=pltpu.PrefetchScalarGridSpec(
            num_scalar_prefetch=1, grid=(m // tr,),
            in_specs=[pl.BlockSpec((None, tr, n), lambda i, c: (c[0], i, 0)), slot(0), slot(1), slot(2)],
            out_specs=pl.BlockSpec((tr, n), lambda i, c: (i, 0))),
        compiler_params=_params("arbitrary"),
    )(chip.reshape(1), sums, landed, landed, landed)


def _add2(a, b):
    return (a + b,)


def _add4(own, r0, r1, r2):
    return ((own + r2) + (r0 + r1),)


def _adamw_math(w, g, m, v):
    m = ADAM_B1 * m + (1.0 - ADAM_B1) * g
    v = ADAM_B2 * v + (1.0 - ADAM_B2) * (g * g)
    m_hat = m / (1.0 - ADAM_B1 ** ADAM_STEP)
    v_hat = v / (1.0 - ADAM_B2 ** ADAM_STEP)
    delta = -ADAM_LR * (m_hat / (jnp.sqrt(v_hat) + ADAM_EPS) + ADAM_WD * w)
    return delta, m, v


WEIGHTS = ["meta_tokens", "ln1_g", "w_in", "q_a_norm_g", "w_uq", "kv_a_norm_g", "w_ukv", "q_norm_g", "k_norm_g",
           "conv_w", "conv_b", "lru_wa", "lru_ba", "lru_wi", "lru_bi", "lru_lambda", "attn_out_g", "rnn_out_g",
           "w_out", "ln2_g", "w_gate", "w_up", "w_down"]
BIG = ["w_in", "w_uq", "w_ukv", "w_out", "w_gate", "w_up", "w_down"]
BIG_T = {"w_in": True, "w_uq": True, "w_ukv": True, "w_out": False, "w_gate": True, "w_up": True, "w_down": False}
BIG_ROWS = {"w_in": 424, "w_uq": 72, "w_ukv": 64, "w_out": 256, "w_gate": 704, "w_up": 704, "w_down": 704}
EARLY = ["w_in", "w_uq", "w_ukv"]
LATE = ["w_out", "w_gate", "w_up", "w_down"]
EARLY_ROWS = 576
LATE_ROWS = 2368
SMALL_SHARDED = ["meta_tokens", "conv_w", "lru_ba", "lru_bi", "lru_lambda"]
SMALL = [n for n in WEIGHTS if n not in BIG]
SMALL_PACK_ROWS = 160
SMALL_ADAM_ROWS = 144


def _offsets(names):
    off, o = {}, 0
    for n in names:
        off[n] = o
        o += BIG_ROWS[n]
    return off


def _shard_pack(names, src, rows):
    parts = [_to_pack_piece(n, src[n]) for n in names]
    used = sum(BIG_ROWS[n] for n in names)
    if rows > used:
        parts.append(jnp.zeros((rows - used, D), F32))
    return jnp.concatenate(parts, axis=0)


def _grad_pack(names, g, rows):
    parts = [g[n].reshape(N_CHIPS, BIG_ROWS[n], D) for n in names]
    used = sum(BIG_ROWS[n] for n in names)
    if rows > used:
        parts.append(jnp.zeros((N_CHIPS, rows - used, D), F32))
    return jnp.concatenate(parts, axis=1).reshape(N_CHIPS, 2, rows // 2, D)


def _both_halves(mine, other, ci):
    return jnp.where(ci == 0, jnp.concatenate([mine, other], axis=0), jnp.concatenate([other, mine], axis=0))


def _to_pack_piece(name, shard):
    a = shard[0].T if BIG_T[name] else shard[0]
    return a.reshape(BIG_ROWS[name], D)


def _from_pack_piece(name, piece, shard_shape):
    _, k, n = shard_shape
    return piece.reshape(n, k).T[None] if BIG_T[name] else piece.reshape(k, n)[None]


def _flat_pack(arrs, rows):
    flat = jnp.concatenate([a.reshape(-1) for a in arrs])
    return jnp.pad(flat, (0, rows * D - flat.shape[0])).reshape(rows, D)


def _flat_unpack(pack, shapes):
    flat, out, o = pack.reshape(-1), [], 0
    for s in shapes:
        n = math.prod(s)
        out.append(flat[o:o + n].reshape(s))
        o += n
    return out


def kernel(x, meta_tokens, ln1_g, w_in, q_a_norm_g, w_uq, kv_a_norm_g, w_ukv, q_norm_g, k_norm_g, conv_w, conv_b, lru_wa, lru_ba, lru_wi, lru_bi, lru_lambda, attn_out_g, rnn_out_g, w_out, ln2_g, w_gate, w_up, w_down, loss_target, m_meta_tokens, m_ln1_g, m_w_in, m_q_a_norm_g, m_w_uq, m_kv_a_norm_g, m_w_ukv, m_q_norm_g, m_k_norm_g, m_conv_w, m_conv_b, m_lru_wa, m_lru_ba, m_lru_wi, m_lru_bi, m_lru_lambda, m_attn_out_g, m_rnn_out_g, m_w_out, m_ln2_g, m_w_gate, m_w_up, m_w_down, v_meta_tokens, v_ln1_g, v_w_in, v_q_a_norm_g, v_w_uq, v_kv_a_norm_g, v_w_ukv, v_q_norm_g, v_k_norm_g, v_conv_w, v_conv_b, v_lru_wa, v_lru_ba, v_lru_wi, v_lru_bi, v_lru_lambda, v_attn_out_g, v_rnn_out_g, v_w_out, v_ln2_g, v_w_gate, v_w_up, v_w_down):
    wts = dict(zip(WEIGHTS, (meta_tokens, ln1_g, w_in, q_a_norm_g, w_uq, kv_a_norm_g, w_ukv, q_norm_g, k_norm_g, conv_w, conv_b, lru_wa, lru_ba, lru_wi, lru_bi, lru_lambda, attn_out_g, rnn_out_g, w_out, ln2_g, w_gate, w_up, w_down)))
    mom = dict(zip(WEIGHTS, (m_meta_tokens, m_ln1_g, m_w_in, m_q_a_norm_g, m_w_uq, m_kv_a_norm_g, m_w_ukv, m_q_norm_g, m_k_norm_g, m_conv_w, m_conv_b, m_lru_wa, m_lru_ba, m_lru_wi, m_lru_bi, m_lru_lambda, m_attn_out_g, m_rnn_out_g, m_w_out, m_ln2_g, m_w_gate, m_w_up, m_w_down)))
    var = dict(zip(WEIGHTS, (v_meta_tokens, v_ln1_g, v_w_in, v_q_a_norm_g, v_w_uq, v_kv_a_norm_g, v_w_ukv, v_q_norm_g, v_k_norm_g, v_conv_w, v_conv_b, v_lru_wa, v_lru_ba, v_lru_wi, v_lru_bi, v_lru_lambda, v_attn_out_g, v_rnn_out_g, v_w_out, v_ln2_g, v_w_gate, v_w_up, v_w_down)))
    xi, yi, ci = _place()
    chip = 2 * xi + yi
    off_e, off_l = _offsets(EARLY), _offsets(LATE)
    half_e, half_l = EARLY_ROWS // 2, LATE_ROWS // 2
    block_rows = lambda ref, j, px, py, c: ref.at[pl.ds((4 * px + 2 * py + c) * half_l, half_l), :]
    whole = lambda ref, tx, ty: ref
    shard_of = lambda ref, tx, ty: ref.at[2 * tx + ty]
    slot = lambda ref, j, px, py, c: ref.at[j]

    pack_l = _shard_pack(LATE, wts, LATE_ROWS).astype(BF)
    sems_l, src_l, land_l, tied = _chips_start(
        "gather_late_start", lax.dynamic_slice_in_dim(pack_l, ci * half_l, half_l, axis=0),
        lax.empty((8 * half_l, D), BF), whole, block_rows)
    pack_e = _shard_pack(EARLY, wts, EARLY_ROWS).astype(BF)
    ge = _all_gather(lax.dynamic_slice_in_dim(pack_e, ci * half_e, half_e, axis=0), "gather_early")
    ge = ge.reshape(N_CHIPS, EARLY_ROWS, D)
    full = {n: ge[:, off_e[n]:off_e[n] + BIG_ROWS[n]] for n in EARLY}

    def late_weights(after):
        _, land = _chips_wait("gather_late_wait", sems_l, src_l, land_l, after, whole, block_rows)
        gl = _gather_finish(land, pack_l, half_l).reshape(N_CHIPS, LATE_ROWS, D)
        part = lambda n: gl[:, off_l[n]:off_l[n] + BIG_ROWS[n]].reshape(N_CHIPS * BIG_ROWS[n], D)
        return dict(w_out=part("w_out"), w_gate_t=part("w_gate"), w_up_t=part("w_up"), w_down=part("w_down"))

    late = {}

    def early_grads(g_late):
        gpack = _grad_pack(LATE, g_late, LATE_ROWS)
        (rbig,) = _pair_exchange(gpack, [], "grad_pair_exchange_late")
        chip_big = _pair_sum(gpack, rbig, ci, "grad_pair_sum_late")
        late["sems"], late["src"], late["land"], zeros = _chips_start(
            "grad_chip_late_start", chip_big, lax.empty((3, half_l, D), BF), shard_of, slot)
        return zeros[0, 0]

    spack = jnp.concatenate([meta_tokens[:, :LANES], meta_tokens[:, LANES:], conv_w[0], lru_ba[0], lru_bi[0],
                             lru_lambda[0], jnp.zeros((6, LANES), F32)], axis=0)
    shalf = lax.dynamic_slice_in_dim(spack, ci * 24, 24, axis=0)
    gs = _all_gather(shalf, "gather_small").reshape(N_CHIPS, 48, LANES)
    cols = lambda a: a.transpose(1, 0, 2).reshape(a.shape[1], N_CHIPS * a.shape[2])
    meta_full = cols(jnp.concatenate([gs[:, 0:16], gs[:, 16:32]], axis=2))
    w = dict(
        w_in_t=full["w_in"].reshape(IN_COLS, D), w_uq_t=full["w_uq"].reshape(N_HEADS * QK_HEAD, Q_LORA),
        w_ukv_t=full["w_ukv"].reshape(2 * D_ATTN, KV_LORA),
        ln1_g=ln1_g, q_a_norm_g=q_a_norm_g, kv_a_norm_g=kv_a_norm_g, q_norm_g=q_norm_g, k_norm_g=k_norm_g,
        conv_w=cols(gs[:, 32:36]), conv_b=conv_b, lru_wa=lru_wa[0], lru_ba=cols(gs[:, 36:38]), lru_wi=lru_wi[0],
        lru_bi=cols(gs[:, 38:40]), lru_lambda=cols(gs[:, 40:42]), attn_out_g=attn_out_g, rnn_out_g=rnn_out_g,
        ln2_g=ln2_g,
    )

    loss_local, grad_x, g, last = _local_step(x, loss_target, meta_full + tied[0, 0], w, late_weights, early_grads)
    loss = lax.psum(loss_local, ("x", "y", "c"))

    gpack = _grad_pack(EARLY, {"w_in": g["w_in_t"], "w_uq": g["w_uq_t"], "w_ukv": g["w_ukv_t"]}, EARLY_ROWS)
    full_shapes = {n: wts[n].shape for n in SMALL}
    full_shapes.update(meta_tokens=(N_META, D), conv_w=(1, CONV_W, D_RNN), lru_ba=(1, 2, D_RNN), lru_bi=(1, 2, D_RNN),
                       lru_lambda=(1, 2, D_RNN))
    gsmall = _flat_pack([g[n] for n in SMALL], SMALL_PACK_ROWS)
    rbig, rsmall = _pair_exchange(gpack, [gsmall], "grad_pair_exchange")
    chip_big = _pair_sum(gpack, rbig, ci, "grad_pair_sum")
    (chip_small,) = _elementwise(_add2, 1, "grad_pair_sum_small", gsmall, rsmall)
    xbig, xsmall = _chip_exchange(chip_big, chip_small, "grad_chip_exchange")
    sum_e = _chip_sum(chip_big, xbig, chip, "grad_chip_sum")
    (small_sum,) = _elementwise(_add4, 1, "grad_chip_sum_small", chip_small, xsmall[0], xsmall[1], xsmall[2])
    src, land = _chips_wait("grad_chip_late_wait", late["sems"], late["src"], late["land"], last + [sum_e], shard_of, slot)
    sum_l = _chip_sum(src, land, chip, "grad_chip_sum_late")
    other_e, other_l = _pair_swap([sum_e, sum_l], "grad_pair_swap")
    gshard_e, gshard_l = _both_halves(sum_e, other_e, ci), _both_halves(sum_l, other_l, ci)

    grads = {n: _from_pack_piece(n, gshard_e[off_e[n]:off_e[n] + BIG_ROWS[n]], wts[n].shape) for n in EARLY}
    grads.update({n: _from_pack_piece(n, gshard_l[off_l[n]:off_l[n] + BIG_ROWS[n]], wts[n].shape) for n in LATE})
    small_full = dict(zip(SMALL, _flat_unpack(small_sum, [full_shapes[n] for n in SMALL])))
    for n in SMALL:
        a = small_full[n]
        if n in SMALL_SHARDED:
            width = wts[n].shape[-1]
            a = lax.dynamic_slice_in_dim(a, chip * width, width, axis=a.ndim - 1)
        grads[n] = a.reshape(wts[n].shape)

    delta, new_m, new_v = {}, {}, {}
    for n in BIG:
        two_d = lambda a: a.reshape(a.shape[-2], a.shape[-1])
        d_, m_, v_ = _elementwise(_adamw_math, 3, "adamw_" + n, two_d(wts[n]), two_d(grads[n]), two_d(mom[n]), two_d(var[n]))
        delta[n], new_m[n], new_v[n] = (a.reshape(wts[n].shape) for a in (d_, m_, v_))
    packs = [_flat_pack([src[n] for n in SMALL], SMALL_ADAM_ROWS) for src in (wts, grads, mom, var)]
    outs = _elementwise(_adamw_math, 3, "adamw_small", *packs)
    for dst, o in zip((delta, new_m, new_v), outs):
        dst.update(zip(SMALL, _flat_unpack(o, [wts[n].shape for n in SMALL])))

    return (loss, grad_x, *[grads[n] for n in WEIGHTS], *[delta[n] for n in WEIGHTS],
            *[new_m[n] for n in WEIGHTS], *[new_v[n] for n in WEIGHTS])
```

```python
import functools
import math

import jax
import jax.numpy as jnp
from jax import lax
from jax.experimental import pallas as pl
from jax.experimental.pallas import tpu as pltpu

F32 = jnp.float32
BF = jnp.bfloat16
MESH = pl.DeviceIdType.MESH

D = 1024
SEQ = 2048
N_META = 16
T = N_META + SEQ
N_HEADS = 8
QK_NOPE = 64
QK_ROPE = 32
QK_HEAD = 96
V_HEAD = 64
Q_LORA = 384
KV_LORA = 256
D_ATTN = 512
D_RNN = 512
RNN_BW = 64
CONV_W = 4
LRU_C = 8.0
ROPE_THETA = 10000.0
D_FF = 2816
EPS = 1e-6
IN_COLS = 1696
ADAM_LR, ADAM_B1, ADAM_B2, ADAM_EPS, ADAM_WD, ADAM_STEP = 0.001, 0.9, 0.999, 1e-08, 0.01, 10

LANES = 128
TP = 2176
NB = 2
R = NB * TP
TR = 256
TQ = 544
HP = LANES
PC = 1792
O_CKV, O_KR, O_XR, O_XG = 384, 640, 768, 1280
CG = 128
N_CG = D_RNN // CG
VMEM_LIMIT = 56 * 1024 * 1024
N_CHIPS = 4
SCALE = QK_HEAD ** -0.5
KEY_MASK = -30000.0
LOG2_E = 1.4426950408889634
SCALE_LOG2 = SCALE * LOG2_E


def _nt(a, b):
    return lax.dot_general(a, b, (((1,), (1,)), ((), ())), preferred_element_type=F32)


def _nn(a, b):
    return jnp.dot(a, b, preferred_element_type=F32)


def _tn(a, b):
    return lax.dot_general(a, b, (((0,), (0,)), ((), ())), preferred_element_type=F32)


def _rms(x, g, n):
    ms = jnp.sum(x * x, axis=-1, keepdims=True) * (1.0 / n)
    return x * lax.rsqrt(ms + EPS) * g


def _rot_impl(x):
    lane = lax.broadcasted_iota(jnp.int32, x.shape, 1)
    left = pltpu.roll(x, HP - 16, 1)
    right = pltpu.roll(x, 16, 1)
    lo = (lane >= QK_NOPE) & (lane < QK_NOPE + 16)
    hi = (lane >= QK_NOPE + 16) & (lane < QK_HEAD)
    return jnp.where(lo, -left, jnp.where(hi, right, 0.0))


@jax.custom_vjp
def _rot(x):
    return _rot_impl(x)


def _rot_fwd(x):
    return _rot_impl(x), None


def _rot_bwd(_, g):
    return (-_rot_impl(g),)


_rot.defvjp(_rot_fwd, _rot_bwd)


def _head(x, g, cs, sn):
    n = _rms(x, g, QK_HEAD)
    return n * cs + _rot(n) * sn


def _const_spec(shape):
    return pl.BlockSpec(shape, lambda *_: (0,) * len(shape), pipeline_mode=pl.Buffered(1))


def _row_spec(n, tr=TR):
    return pl.BlockSpec((tr, n), lambda i: (i, 0))


def _params(*sem, vmem=VMEM_LIMIT):
    return pltpu.CompilerParams(dimension_semantics=sem, vmem_limit_bytes=vmem)


def _stage_a_fwd(hp, cs, sn, cw):
    def body(hp_ref, cs_ref, sn_ref, ln1, win, qag, wq, kvag, wk, wv, qg, kg,
             pa_ref, xr_ref, xg_ref, q_ref, k_ref, v_ref):
        hn = _rms(hp_ref[...], ln1[...], D).astype(BF)
        p = _nt(hn, win[...])
        pa_ref[...] = p[:, :O_XR]
        xr_ref[...] = p[:, O_XR:O_XG]
        xg_ref[...] = p[:, O_XG:]
        cqn = _rms(p[:, :O_CKV], qag[...], Q_LORA).astype(BF)
        ckvn = _rms(p[:, O_CKV:O_KR], kvag[...], KV_LORA).astype(BF)
        kr = p[:, O_KR:O_XR]
        c, s = cs_ref[...], sn_ref[...]
        mask_lane = lax.broadcasted_iota(jnp.int32, (1, HP), 1) == QK_HEAD
        row = pl.program_id(0) * TR + lax.broadcasted_iota(jnp.int32, (TR, 1), 0)
        key_mask = jnp.where(jnp.where(row >= TP, row - TP, row) < T, 0.0, KEY_MASK)
        qraw = _nt(cqn, wq[...])
        kraw = _nt(ckvn, wk[...])
        for h in range(N_HEADS):
            sl = slice(h * HP, (h + 1) * HP)
            q_ref[:, sl] = jnp.where(mask_lane, 1.0, _head(qraw[:, sl], qg[...], c, s)).astype(BF)
            k_ref[:, sl] = jnp.where(mask_lane, key_mask, _head(kraw[:, sl] + kr, kg[...], c, s)).astype(BF)
        v_ref[...] = _nt(ckvn, wv[...]).astype(BF)

    return pl.pallas_call(
        body, grid=(R // TR,), name="stage_a_fwd",
        in_specs=[_row_spec(D), _row_spec(HP), _row_spec(HP), _const_spec((1, D)), _const_spec((PC, D)),
                  _const_spec((1, Q_LORA)), _const_spec((N_HEADS * HP, Q_LORA)), _const_spec((1, KV_LORA)),
                  _const_spec((N_HEADS * HP, KV_LORA)), _const_spec((D_ATTN, KV_LORA)), _const_spec((1, HP)),
                  _const_spec((1, HP))],
        out_specs=[_row_spec(O_XR), _row_spec(D_RNN), _row_spec(D_RNN), _row_spec(N_HEADS * HP),
                   _row_spec(N_HEADS * HP), _row_spec(D_ATTN)],
        out_shape=[jax.ShapeDtypeStruct((R, O_XR), F32), jax.ShapeDtypeStruct((R, D_RNN), F32),
                   jax.ShapeDtypeStruct((R, D_RNN), F32), jax.ShapeDtypeStruct((R, N_HEADS * HP), BF),
                   jax.ShapeDtypeStruct((R, N_HEADS * HP), BF), jax.ShapeDtypeStruct((R, D_ATTN), BF)],
        compiler_params=_params("arbitrary"),
    )(hp, cs, sn, cw["ln1_g"], cw["win"], cw["qa_g"], cw["wq"], cw["kva_g"], cw["wk"], cw["wv"], cw["q_g"], cw["k_g"])


def _stage_a_bwd(dq, dk, dv, dxr, dxg, dh1, hp, pa, cs, sn, cw):
    def body(dq_ref, dk_ref, dv_ref, dxr_ref, dxg_ref, dh1_ref, hp_ref, pa_ref, cs_ref, sn_ref,
             ln1, win, qag, wq, kvag, wk, wv, qg, kg,
             dhp_ref, dp_ref, dqraw_ref, dkraw_ref, hn_ref, cqn_ref, ckvn_ref,
             dln1_ref, dqag_ref, dkvag_ref, dqg_ref, dkg_ref):
        @pl.when(pl.program_id(0) == 0)
        def _():
            for r in (dln1_ref, dqag_ref, dkvag_ref, dqg_ref, dkg_ref):
                r[...] = jnp.zeros_like(r)

        hn, vjp_ln1 = jax.vjp(lambda h, g: _rms(h, g, D), hp_ref[...], ln1[...])
        hn_ref[...] = hn.astype(BF)
        pa_v = pa_ref[...]
        cqn, vjp_qa = jax.vjp(lambda x, g: _rms(x, g, Q_LORA), pa_v[:, :O_CKV], qag[...])
        ckvn, vjp_kva = jax.vjp(lambda x, g: _rms(x, g, KV_LORA), pa_v[:, O_CKV:O_KR], kvag[...])
        kr = pa_v[:, O_KR:O_XR]
        cqnb, ckvnb = cqn.astype(BF), ckvn.astype(BF)
        cqn_ref[...] = cqnb
        ckvn_ref[...] = ckvnb
        c, s = cs_ref[...], sn_ref[...]
        lane = lax.broadcasted_iota(jnp.int32, (1, HP), 1)
        rope_lanes = ((lane >= QK_NOPE) & (lane < QK_HEAD)).astype(F32)
        dkr = jnp.zeros((TR, HP), F32)
        dqg = jnp.zeros((1, HP), F32)
        dkg = jnp.zeros((1, HP), F32)
        qraw = _nt(cqnb, wq[...])
        kraw = _nt(ckvnb, wk[...])
        for h in range(N_HEADS):
            sl = slice(h * HP, (h + 1) * HP)
            _, vjp_q = jax.vjp(lambda x, g: _head(x, g, c, s), qraw[:, sl], qg[...])
            dqraw, dg = vjp_q(dq_ref[:, sl])
            dqg = dqg + dg
            dqraw_ref[:, sl] = dqraw.astype(BF)
            _, vjp_k = jax.vjp(lambda x, g: _head(x, g, c, s), kraw[:, sl] + kr, kg[...])
            dkraw, dg = vjp_k(dk_ref[:, sl])
            dkg = dkg + dg
            dkraw_ref[:, sl] = dkraw.astype(BF)
            dkr = dkr + dkraw * rope_lanes
        dcq, dqag = vjp_qa(_nn(dqraw_ref[...], wq[...]))
        dckv, dkvag = vjp_kva(_nn(dkraw_ref[...], wk[...]) + _nn(dv_ref[...].astype(BF), wv[...]))
        dpb = jnp.concatenate([dcq, dckv, dkr, dxr_ref[...], dxg_ref[...]], axis=1).astype(BF)
        dp_ref[...] = dpb
        dh, dln1 = vjp_ln1(_nn(dpb, win[...]))
        dhp_ref[...] = dh + dh1_ref[...]
        dln1_ref[...] += dln1
        dqag_ref[...] += dqag
        dkvag_ref[...] += dkvag
        dqg_ref[...] += dqg
        dkg_ref[...] += dkg

    acc = lambda n: pl.BlockSpec((1, n), lambda i: (0, 0))
    return pl.pallas_call(
        body, grid=(R // TR,), name="stage_a_bwd",
        in_specs=[_row_spec(N_HEADS * HP), _row_spec(N_HEADS * HP), _row_spec(D_ATTN), _row_spec(D_RNN),
                  _row_spec(D_RNN), _row_spec(D), _row_spec(D), _row_spec(O_XR), _row_spec(HP), _row_spec(HP),
                  _const_spec((1, D)), _const_spec((PC, D)), _const_spec((1, Q_LORA)),
                  _const_spec((N_HEADS * HP, Q_LORA)), _const_spec((1, KV_LORA)),
                  _const_spec((N_HEADS * HP, KV_LORA)), _const_spec((D_ATTN, KV_LORA)), _const_spec((1, HP)),
                  _const_spec((1, HP))],
        out_specs=[_row_spec(D), _row_spec(PC), _row_spec(N_HEADS * HP), _row_spec(N_HEADS * HP), _row_spec(D),
                   _row_spec(Q_LORA), _row_spec(KV_LORA), acc(D), acc(Q_LORA), acc(KV_LORA), acc(HP), acc(HP)],
        out_shape=[jax.ShapeDtypeStruct((R, D), F32), jax.ShapeDtypeStruct((R, PC), BF),
                   jax.ShapeDtypeStruct((R, N_HEADS * HP), BF), jax.ShapeDtypeStruct((R, N_HEADS * HP), BF),
                   jax.ShapeDtypeStruct((R, D), BF), jax.ShapeDtypeStruct((R, Q_LORA), BF),
                   jax.ShapeDtypeStruct((R, KV_LORA), BF), jax.ShapeDtypeStruct((1, D), F32),
                   jax.ShapeDtypeStruct((1, Q_LORA), F32), jax.ShapeDtypeStruct((1, KV_LORA), F32),
                   jax.ShapeDtypeStruct((1, HP), F32), jax.ShapeDtypeStruct((1, HP), F32)],
        compiler_params=_params("arbitrary"),
    )(dq, dk, dv, dxr, dxg, dh1, hp, pa, cs, sn, cw["ln1_g"], cw["win"], cw["qa_g"], cw["wq"], cw["kva_g"],
      cw["wk"], cw["wv"], cw["q_g"], cw["k_g"])


def _head_mask(half, dtype):
    lane = lax.broadcasted_iota(jnp.int32, (1, 2 * V_HEAD), 1)
    return ((lane >= V_HEAD) == (half == 1)).astype(dtype)


_ATTN_GRID = (NB, N_HEADS // 2, TP // TQ)
_Q_SPEC = pl.BlockSpec((TQ, 2 * HP), lambda b, j, i: (b * (TP // TQ) + i, j))
_K_SPEC = pl.BlockSpec((TP, 2 * HP), lambda b, j, i: (b, j))
_V_SPEC = pl.BlockSpec((TP, 2 * V_HEAD), lambda b, j, i: (b, j))
_O_SPEC = pl.BlockSpec((TQ, 2 * V_HEAD), lambda b, j, i: (b * (TP // TQ) + i, j))
_LSE_SPEC = pl.BlockSpec((None, TQ, 2), lambda b, j, i: (j, b * (TP // TQ) + i, 0))


def _attn_fwd(q, k, v):
    def body(q_ref, k_ref, v_ref, o_ref, lse_ref):
        v2 = v_ref[...]
        o = jnp.zeros((TQ, 2 * V_HEAD), F32)
        lse = []
        for hh in range(2):
            sl = slice(hh * HP, (hh + 1) * HP)
            raw = _nt(q_ref[:, sl], k_ref[:, sl])
            m = jnp.max(raw, axis=-1, keepdims=True)
            e = jnp.exp2((raw - m) * SCALE_LOG2)
            l = jnp.sum(e, axis=-1, keepdims=True)
            o = o + _nn(e.astype(BF), v2 * _head_mask(hh, BF)) * (1.0 / l)
            lse.append(m * SCALE_LOG2 + jnp.log(l) * LOG2_E)
        o_ref[...] = o
        lane = lax.broadcasted_iota(jnp.int32, (TQ, 2), 1)
        lse_ref[...] = jnp.where(lane == 0, lse[0], lse[1])

    return pl.pallas_call(
        body, grid=_ATTN_GRID, name="attn_fwd", in_specs=[_Q_SPEC, _K_SPEC, _V_SPEC], out_specs=[_O_SPEC, _LSE_SPEC],
        out_shape=[jax.ShapeDtypeStruct((R, D_ATTN), F32), jax.ShapeDtypeStruct((N_HEADS // 2, R, 2), F32)],
        compiler_params=_params("arbitrary", "arbitrary", "arbitrary"),
    )(q, k, v)


def _attn_bwd(q, k, v, o, lse, do):
    def body(q_ref, k_ref, v_ref, o_ref, lse_ref, do_ref, dq_ref, dk_ref, dv_ref):
        @pl.when(pl.program_id(2) == 0)
        def _():
            dk_ref[...] = jnp.zeros_like(dk_ref)
            dv_ref[...] = jnp.zeros_like(dv_ref)

        do = do_ref[...]
        dob = do.astype(BF)
        do_o = do * o_ref[...]
        v2 = v_ref[...]
        dv_sum = jnp.zeros((TP, 2 * V_HEAD), F32)
        for hh in range(2):
            sl = slice(hh * HP, (hh + 1) * HP)
            qb, kb = q_ref[:, sl], k_ref[:, sl]
            p = jnp.exp2(_nt(qb, kb) * SCALE_LOG2 - lse_ref[:, hh:hh + 1])
            dp = _nt(dob, v2 * _head_mask(hh, BF))
            delta = jnp.sum(do_o * _head_mask(hh, F32), axis=-1, keepdims=True)
            dsb = (p * (dp - delta) * SCALE).astype(BF)
            dq_ref[:, sl] = _nn(dsb, kb)
            dk_ref[:, sl] += _tn(dsb, qb)
            dv_sum = dv_sum + _tn(p.astype(BF), dob) * _head_mask(hh, F32)
        dv_ref[...] += dv_sum

    return pl.pallas_call(
        body, grid=_ATTN_GRID, name="attn_bwd", in_specs=[_Q_SPEC, _K_SPEC, _V_SPEC, _O_SPEC, _LSE_SPEC, _O_SPEC],
        out_specs=[_Q_SPEC, _K_SPEC, _V_SPEC],
        out_shape=[jax.ShapeDtypeStruct((R, N_HEADS * HP), F32), jax.ShapeDtypeStruct((R, N_HEADS * HP), F32),
                   jax.ShapeDtypeStruct((R, D_ATTN), F32)],
        compiler_params=_params("arbitrary", "arbitrary", "arbitrary"),
    )(q, k, v, o, lse, do)


def _tile_prefix(a_ref, b_ref, reverse):
    r8 = lax.broadcasted_iota(jnp.int32, (TP, CG), 0) & 7
    a, b = a_ref[...], b_ref[...]
    for s in (1, 2, 4):
        shift = TP - s if reverse else s
        keep = (r8 < 8 - s) if reverse else (r8 >= s)
        b = jnp.where(keep, a * pltpu.roll(b, shift, 0) + b, b)
        a = jnp.where(keep, a * pltpu.roll(a, shift, 0), a)
    a_ref[...] = a
    b_ref[...] = b


def _scan_pair(af_ref, bf_ref, hf_ref, ab_ref, bb_ref, hb_ref):
    _tile_prefix(af_ref, bf_ref, False)
    _tile_prefix(ab_ref, bb_ref, True)
    n_tiles = TP // 8

    def step(i, carry):
        cf, cb = carry
        rf = pl.multiple_of(i * 8, 8)
        rb = pl.multiple_of((n_tiles - 1 - i) * 8, 8)
        hf_ref[pl.ds(rf, 8), :] = bf_ref[pl.ds(rf, 8), :] + af_ref[pl.ds(rf, 8), :] * cf
        hb_ref[pl.ds(rb, 8), :] = bb_ref[pl.ds(rb, 8), :] + ab_ref[pl.ds(rb, 8), :] * cb
        cf = bf_ref[pl.ds(rf + 7, 1), :] + af_ref[pl.ds(rf + 7, 1), :] * cf
        cb = bb_ref[pl.ds(rb, 1), :] + ab_ref[pl.ds(rb, 1), :] * cb
        return cf, cb

    zero = jnp.zeros((1, CG), F32)
    lax.fori_loop(0, n_tiles, step, (zero, zero), unroll=8)


def _shifts(x):
    t = lax.broadcasted_iota(jnp.int32, x.shape, 0)
    xm2 = jnp.where(t >= 2, pltpu.roll(x, 2, 0), 0.0)
    xm1 = jnp.where(t >= 1, pltpu.roll(x, 1, 0), 0.0)
    xp1 = jnp.where(t < TP - 1, pltpu.roll(x, TP - 1, 0), 0.0)
    return xm2, xm1, xp1


def _softplus(z):
    e = jnp.exp(-jnp.abs(z))
    small = e * (1.0 - e * (0.5 - e * (1.0 / 3.0)))
    return jnp.maximum(z, 0.0) + jnp.where(e < 0.01, small, jnp.log(1.0 + e))


def _neg_expm1(x):
    series = -x * (1.0 + x * 0.5 * (1.0 + x * (1.0 / 3.0) * (1.0 + x * 0.25)))
    return jnp.where(x > -0.05, series, 1.0 - jnp.exp(x))


def _gates(row0, xc, pa_f, pi_f, pa_b, pi_b, lam_f, lam_b):
    t = row0 + lax.broadcasted_iota(jnp.int32, xc.shape, 0)
    valid = t < T
    out = []
    for pa, pi_, lam in ((pa_f, pi_f, lam_f), (pa_b, pi_b, lam_b)):
        r = jax.nn.sigmoid(pa)
        gate_i = jax.nn.sigmoid(pi_)
        log_a = -LRU_C * r * _softplus(-lam)
        a = jnp.exp(log_a)
        mult = jnp.sqrt(jnp.maximum(_neg_expm1(2.0 * log_a), 0.0))
        out += [a, jnp.where(valid, mult * (gate_i * xc), 0.0)]
    return tuple(out)


def _gates_bwd(row0, xc, pres, lams, cots):
    t = row0 + lax.broadcasted_iota(jnp.int32, xc.shape, 0)
    valid = t < T
    dxc = jnp.zeros_like(xc)
    dpres, dlams = [], []
    for d in range(2):
        pa, pi_, lam = pres[2 * d], pres[2 * d + 1], lams[d]
        da, db = cots[2 * d], jnp.where(valid, cots[2 * d + 1], 0.0)
        r = jax.nn.sigmoid(pa)
        gate_i = jax.nn.sigmoid(pi_)
        sp = _softplus(-lam)
        log_a = -LRU_C * r * sp
        a = jnp.exp(log_a)
        m2 = jnp.maximum(_neg_expm1(2.0 * log_a), 0.0)
        mult = jnp.sqrt(m2)
        dxc = dxc + db * (mult * gate_i)
        d_gate = db * (mult * xc)
        d_m2 = jnp.where(m2 > 0.0, db * (gate_i * xc) * (0.5 * lax.rsqrt(m2)), 0.0)
        d_log_a = da * a - 2.0 * d_m2 * (a * a)
        dpres += [d_log_a * (-LRU_C * sp) * (r * (1.0 - r)), d_gate * (gate_i * (1.0 - gate_i))]
        d_sp = jnp.sum(d_log_a * (-LRU_C * r), axis=0, keepdims=True)
        dlams.append(-d_sp * jax.nn.sigmoid(-lam))
    return dxc, dpres, dlams


def _rnn_specs():
    seq = pl.BlockSpec((TP, CG), lambda g, b: (b, g))
    return dict(
        seq=seq,
        cw=pl.BlockSpec((CONV_W, CG), lambda g, b: (0, g)),
        cb=pl.BlockSpec((1, CG), lambda g, b: (0, g)),
        w4=pl.BlockSpec((None, CG, 4 * CG), lambda g, b: (g, 0, 0)),
        b4=pl.BlockSpec((None, 1, 4 * CG), lambda g, b: (g, 0, 0)),
        lam=pl.BlockSpec((None, 1, 2 * CG), lambda g, b: (g, 0, 0)),
    )


def _conv(x, xm2, xm1, xp1, cw_ref, cb_ref):
    return cw_ref[0:1, :] * xm2 + cw_ref[1:2, :] * xm1 + cw_ref[2:3, :] * x + cw_ref[3:4, :] * xp1 + cb_ref[...]


TC = 128
N_TC = TP // TC


def _split4(pre):
    return pre[:, :CG], pre[:, CG:2 * CG], pre[:, 2 * CG:3 * CG], pre[:, 3 * CG:]


def _rnn_fwd(xr, xg, cw):
    def body(xr_ref, xg_ref, cw_ref, cb_ref, w4_ref, b4_ref, lam_ref, y_ref, hf_ref, hb_ref, xc_s, af, bf, ab, bb):
        x = xr_ref[...]
        xc_s[...] = _conv(x, *_shifts(x), cw_ref, cb_ref)
        lam = lam_ref[...]

        def chunk(i, _):
            rows = pl.ds(pl.multiple_of(i * TC, TC), TC)
            xc = xc_s[rows, :]
            pre = _nn(xc.astype(BF), w4_ref[...]) + b4_ref[...]
            a_f, b_f, a_b, b_b = _gates(i * TC, xc, *_split4(pre), lam[:, :CG], lam[:, CG:])
            af[rows, :] = a_f
            bf[rows, :] = b_f
            ab[rows, :] = a_b
            bb[rows, :] = b_b
            return 0

        lax.fori_loop(0, N_TC, chunk, 0)
        _scan_pair(af, bf, hf_ref, ab, bb, hb_ref)
        y_ref[...] = (hf_ref[...] + hb_ref[...]) * jax.nn.gelu(xg_ref[...])

    sp = _rnn_specs()
    return pl.pallas_call(
        body, grid=(N_CG, NB), name="rnn_fwd",
        in_specs=[sp["seq"], sp["seq"], sp["cw"], sp["cb"], sp["w4"], sp["b4"], sp["lam"]],
        out_specs=[sp["seq"]] * 3, out_shape=[jax.ShapeDtypeStruct((R, D_RNN), F32)] * 3,
        scratch_shapes=[pltpu.VMEM((TP, CG), F32)] * 5,
        compiler_params=_params("arbitrary", "arbitrary"),
    )(xr, xg, cw["conv_w"], cw["conv_b"], cw["w4"], cw["b4"], cw["lam"])


def _rnn_bwd(dy, xr, xg, hf, hb, cw):
    def body(dy_ref, xr_ref, xg_ref, hf_ref, hb_ref, cw_ref, cb_ref, w4_ref, b4_ref, lam_ref,
             dxr_ref, dxg_ref, dcw_ref, dcb_ref, dw4_ref, db4_ref, dlam_ref,
             xc_s, af_s, ab_s, dhs_s, dhs2_s, lf_s, lb_s, daf_s, dab_s, dxc_s):
        @pl.when(pl.program_id(1) == 0)
        def _():
            for r in (dcw_ref, dcb_ref, dw4_ref, db4_ref, dlam_ref):
                r[...] = jnp.zeros_like(r)

        x = xr_ref[...]
        xc_s[...] = _conv(x, *_shifts(x), cw_ref, cb_ref)
        lam = lam_ref[...]

        def chunk1(i, _):
            rows = pl.ds(pl.multiple_of(i * TC, TC), TC)
            xc = xc_s[rows, :]
            pre = _nn(xc.astype(BF), w4_ref[...]) + b4_ref[...]
            a_f, _, a_b, _ = _gates(i * TC, xc, *_split4(pre), lam[:, :CG], lam[:, CG:])
            af_s[rows, :] = a_f
            ab_s[rows, :] = a_b
            _, vjp_y = jax.vjp(lambda h, g: h * jax.nn.gelu(g), hf_ref[rows, :] + hb_ref[rows, :], xg_ref[rows, :])
            dhs, dxg = vjp_y(dy_ref[rows, :])
            dhs_s[rows, :] = dhs
            dhs2_s[rows, :] = dhs
            dxg_ref[rows, :] = dxg
            return 0

        lax.fori_loop(0, N_TC, chunk1, 0)
        t = lax.broadcasted_iota(jnp.int32, (TP, CG), 0)
        af_s[...] = pltpu.roll(af_s[...], TP - 1, 0)
        ab_s[...] = pltpu.roll(ab_s[...], 1, 0)
        _scan_pair(ab_s, dhs_s, lb_s, af_s, dhs2_s, lf_s)
        daf_s[...] = lf_s[...] * jnp.where(t >= 1, pltpu.roll(hf_ref[...], 1, 0), 0.0)
        dab_s[...] = lb_s[...] * jnp.where(t < TP - 1, pltpu.roll(hb_ref[...], TP - 1, 0), 0.0)

        def chunk2(i, _):
            rows = pl.ds(pl.multiple_of(i * TC, TC), TC)
            xc = xc_s[rows, :]
            xcb = xc.astype(BF)
            pre = _nn(xcb, w4_ref[...]) + b4_ref[...]
            dxc, dpres, dlams = _gates_bwd(i * TC, xc, _split4(pre), (lam[:, :CG], lam[:, CG:]),
                                           (daf_s[rows, :], lf_s[rows, :], dab_s[rows, :], lb_s[rows, :]))
            dpre = jnp.concatenate(dpres, axis=1)
            dpreb = dpre.astype(BF)
            dxc_s[rows, :] = dxc + _nt(dpreb, w4_ref[...])
            dw4_ref[...] += _tn(xcb, dpreb)
            db4_ref[...] += jnp.sum(dpre, axis=0, keepdims=True)
            dlam_ref[...] += jnp.concatenate(dlams, axis=1)
            return 0

        lax.fori_loop(0, N_TC, chunk2, 0)
        dxc = dxc_s[...]
        dcb_ref[...] += jnp.sum(dxc, axis=0, keepdims=True)
        for tap, xs in enumerate(_shifts(x)[:2] + (x,) + _shifts(x)[2:]):
            dcw_ref[tap:tap + 1, :] += jnp.sum(xs * dxc, axis=0, keepdims=True)
        dxr_ref[...] = (cw_ref[0:1, :] * jnp.where(t < TP - 2, pltpu.roll(dxc, TP - 2, 0), 0.0)
                        + cw_ref[1:2, :] * jnp.where(t < TP - 1, pltpu.roll(dxc, TP - 1, 0), 0.0)
                        + cw_ref[2:3, :] * dxc
                        + cw_ref[3:4, :] * jnp.where(t >= 1, pltpu.roll(dxc, 1, 0), 0.0))

    sp = _rnn_specs()
    return pl.pallas_call(
        body, grid=(N_CG, NB), name="rnn_bwd",
        in_specs=[sp["seq"]] * 5 + [sp["cw"], sp["cb"], sp["w4"], sp["b4"], sp["lam"]],
        out_specs=[sp["seq"], sp["seq"], sp["cw"], sp["cb"], sp["w4"], sp["b4"], sp["lam"]],
        out_shape=[jax.ShapeDtypeStruct((R, D_RNN), F32), jax.ShapeDtypeStruct((R, D_RNN), F32),
                   jax.ShapeDtypeStruct((CONV_W, D_RNN), F32), jax.ShapeDtypeStruct((1, D_RNN), F32),
                   jax.ShapeDtypeStruct((N_CG, CG, 4 * CG), F32), jax.ShapeDtypeStruct((N_CG, 1, 4 * CG), F32),
                   jax.ShapeDtypeStruct((N_CG, 1, 2 * CG), F32)],
        scratch_shapes=[pltpu.VMEM((TP, CG), F32)] * 10,
        compiler_params=_params("arbitrary", "arbitrary"),
    )(dy, xr, xg, hf, hb, cw["conv_w"], cw["conv_b"], cw["w4"], cw["b4"], cw["lam"])


TD = 256
STAGE_D_VMEM = 58 * 1024 * 1024


def _stage_d(hp, o, y, tgt, cw):
    def body(hp_ref, o_ref, y_ref, tgt_ref, ga, gr, wout, ln2, wg, wu, wd,
             do_ref, dy_ref, dh1_ref, mix_ref, dh1b_ref, hn2_ref, dg_ref, du_ref, act_ref, dh2b_ref,
             loss_ref, dga_ref, dgr_ref, dln2_ref):
        i = pl.program_id(0)

        @pl.when(i == 0)
        def _():
            for r in (loss_ref, dga_ref, dgr_ref, dln2_ref):
                r[...] = jnp.zeros_like(r)

        mix_a, vjp_a = jax.vjp(lambda x, g: _rms(x, g, D_ATTN), o_ref[...], ga[...])
        mix_r, vjp_r = jax.vjp(lambda x, g: _rms(x, g, D_RNN), y_ref[...], gr[...])
        mab, mrb = mix_a.astype(BF), mix_r.astype(BF)
        mix_ref[:, :D_ATTN] = mab
        mix_ref[:, D_ATTN:] = mrb
        h1 = hp_ref[...] + _nn(mab, wout[:D_ATTN, :]) + _nn(mrb, wout[D_ATTN:, :])
        hn2, vjp_ln2 = jax.vjp(lambda x, g: _rms(x, g, D), h1, ln2[...])
        hn2b = hn2.astype(BF)
        hn2_ref[...] = hn2b
        act, vjp_act = jax.vjp(lambda g, u: jax.nn.silu(g) * u, _nt(hn2b, wg[...]), _nt(hn2b, wu[...]))
        actb = act.astype(BF)
        act_ref[...] = actb
        h2 = h1 + _nn(actb, wd[...])
        row = i * TD + lax.broadcasted_iota(jnp.int32, (TD, 1), 0)
        t = jnp.where(row >= TP, row - TP, row)
        err = jnp.where((t >= N_META) & (t < T), h2 - tgt_ref[...], 0.0)
        loss_ref[...] += jnp.sum(err * err) * (0.5 / D)
        dh2b = (err * (1.0 / D)).astype(BF)
        dh2b_ref[...] = dh2b
        dg, du = vjp_act(_nt(dh2b, wd[...]))
        dgb, dub = dg.astype(BF), du.astype(BF)
        dg_ref[...] = dgb
        du_ref[...] = dub
        dh1n, dln2 = vjp_ln2(_nn(dgb, wg[...]) + _nn(dub, wu[...]))
        dh1 = err * (1.0 / D) + dh1n
        dh1_ref[...] = dh1
        dh1b = dh1.astype(BF)
        dh1b_ref[...] = dh1b
        dmix = _nt(dh1b, wout[...])
        do, dga = vjp_a(dmix[:, :D_ATTN])
        dyr, dgr = vjp_r(dmix[:, D_ATTN:])
        do_ref[...] = do
        dy_ref[...] = dyr
        dga_ref[...] += dga
        dgr_ref[...] += dgr
        dln2_ref[...] += dln2

    rs = lambda n: _row_spec(n, TD)
    acc = lambda n: pl.BlockSpec((1, n), lambda i: (0, 0))
    return pl.pallas_call(
        body, grid=(R // TD,), name="stage_d",
        in_specs=[rs(D), rs(D_ATTN), rs(D_RNN), rs(D), _const_spec((1, D_ATTN)), _const_spec((1, D_RNN)),
                  _const_spec((D, D)), _const_spec((1, D)), _const_spec((D_FF, D)), _const_spec((D_FF, D)),
                  _const_spec((D_FF, D))],
        out_specs=[rs(D_ATTN), rs(D_RNN), rs(D), rs(D), rs(D), rs(D), rs(D_FF), rs(D_FF), rs(D_FF), rs(D),
                   acc(1), acc(D_ATTN), acc(D_RNN), acc(D)],
        out_shape=[jax.ShapeDtypeStruct((R, D_ATTN), F32), jax.ShapeDtypeStruct((R, D_RNN), F32),
                   jax.ShapeDtypeStruct((R, D), F32), jax.ShapeDtypeStruct((R, D), BF),
                   jax.ShapeDtypeStruct((R, D), BF), jax.ShapeDtypeStruct((R, D), BF),
                   jax.ShapeDtypeStruct((R, D_FF), BF), jax.ShapeDtypeStruct((R, D_FF), BF),
                   jax.ShapeDtypeStruct((R, D_FF), BF), jax.ShapeDtypeStruct((R, D), BF),
                   jax.ShapeDtypeStruct((1, 1), F32), jax.ShapeDtypeStruct((1, D_ATTN), F32),
                   jax.ShapeDtypeStruct((1, D_RNN), F32), jax.ShapeDtypeStruct((1, D), F32)],
        compiler_params=_params("arbitrary", vmem=STAGE_D_VMEM),
    )(hp, o, y, tgt, cw["ga"], cw["gr"], cw["wout"], cw["ln2_g"], cw["wg"], cw["wu"], cw["wd"])


TW = 2176


def _wgrad(a, b, name, tk=None):
    ka, nb = a.shape[1], b.shape[1]
    tk = ka if tk is None else tk

    def body(a_ref, b_ref, o_ref):
        @pl.when(pl.program_id(1) == 0)
        def _():
            o_ref[...] = jnp.zeros_like(o_ref)

        o_ref[...] += _tn(a_ref[...].astype(BF), b_ref[...].astype(BF))

    return pl.pallas_call(
        body, grid=(ka // tk, R // TW), name=name,
        in_specs=[pl.BlockSpec((TW, tk), lambda k, r: (r, k)), pl.BlockSpec((TW, nb), lambda k, r: (r, 0))],
        out_specs=pl.BlockSpec((tk, nb), lambda k, r: (k, 0)),
        out_shape=jax.ShapeDtypeStruct((ka, nb), F32),
        compiler_params=_params("arbitrary", "arbitrary"),
    )(a, b)


def _rope_tables():
    half = QK_ROPE // 2
    freqs = 1.0 / (ROPE_THETA ** (jnp.arange(half, dtype=F32) / half))
    ang = jnp.arange(TP, dtype=F32)[:, None] * freqs[None, :]
    ones = jnp.ones((TP, QK_NOPE), F32)
    zeros = jnp.zeros((TP, QK_NOPE), F32)
    pad1 = jnp.ones((TP, HP - QK_HEAD), F32)
    pad0 = jnp.zeros((TP, HP - QK_HEAD), F32)
    cs = jnp.concatenate([ones, jnp.cos(ang), jnp.cos(ang), pad1], axis=1)
    sn = jnp.concatenate([zeros, jnp.sin(ang), jnp.sin(ang), pad0], axis=1)
    return jnp.tile(cs, (NB, 1)), jnp.tile(sn, (NB, 1))


def _pad_rows(a, lo, hi):
    return jnp.pad(a, ((0, 0), (lo, hi), (0, 0)))


def _compute_weights(w):
    win_t = w["w_in_t"]
    kr = win_t[O_KR:O_KR + QK_ROPE]
    win = jnp.concatenate([win_t[:O_KR], jnp.zeros((QK_NOPE, D), F32), kr,
                           jnp.zeros((HP - QK_HEAD, D), F32), win_t[O_KR + QK_ROPE:]], axis=0)
    wq = _pad_rows(w["w_uq_t"].reshape(N_HEADS, QK_HEAD, Q_LORA), 0, HP - QK_HEAD)
    wkv = w["w_ukv_t"].reshape(N_HEADS, QK_NOPE + V_HEAD, KV_LORA)
    wk = _pad_rows(wkv[:, :QK_NOPE], 0, HP - QK_NOPE)
    wv = wkv[:, QK_NOPE:].reshape(D_ATTN, KV_LORA)
    gates = jnp.stack([w["lru_wa"][0], w["lru_wi"][0], w["lru_wa"][1], w["lru_wi"][1]])
    blk = gates.reshape(4, N_CG, 2, RNN_BW, RNN_BW)
    dense = jnp.einsum("tcaij,ab->tcaibj", blk, jnp.eye(2, dtype=F32)).reshape(4, N_CG, CG, CG)
    w4 = dense.transpose(1, 2, 0, 3).reshape(N_CG, CG, 4 * CG)
    bias = jnp.stack([w["lru_ba"][0], w["lru_bi"][0], w["lru_ba"][1], w["lru_bi"][1]])
    b4 = bias.reshape(4, N_CG, CG).transpose(1, 0, 2).reshape(N_CG, 1, 4 * CG)
    lam = w["lru_lambda"].reshape(2, N_CG, CG).transpose(1, 0, 2).reshape(N_CG, 1, 2 * CG)
    pad_g = lambda g: jnp.pad(g.reshape(1, QK_HEAD), ((0, 0), (0, HP - QK_HEAD)))
    return dict(
        ln1_g=w["ln1_g"].reshape(1, D), win=win.astype(BF), qa_g=w["q_a_norm_g"].reshape(1, Q_LORA),
        wq=wq.astype(BF).reshape(N_HEADS * HP, Q_LORA), kva_g=w["kv_a_norm_g"].reshape(1, KV_LORA),
        wk=wk.astype(BF).reshape(N_HEADS * HP, KV_LORA), wv=wv.astype(BF),
        q_g=pad_g(w["q_norm_g"]), k_g=pad_g(w["k_norm_g"]),
        conv_w=w["conv_w"].reshape(CONV_W, D_RNN), conv_b=w["conv_b"].reshape(1, D_RNN),
        w4=w4.astype(BF), b4=b4, lam=lam,
        ga=w["attn_out_g"].reshape(1, D_ATTN), gr=w["rnn_out_g"].reshape(1, D_RNN), ln2_g=w["ln2_g"].reshape(1, D),
    )


def _local_step(x, target, meta, w, late_weights, early_grads):
    cw = _compute_weights(w)
    cs, sn = _rope_tables()
    hp = jnp.concatenate([jnp.broadcast_to(meta[None], (NB, N_META, D)), x,
                          jnp.zeros((NB, TP - T, D), F32)], axis=1).reshape(R, D)
    tgt = _pad_rows(target, N_META, TP - T).reshape(R, D)

    pa, xr, xg, q, k, v = _stage_a_fwd(hp, cs, sn, cw)
    o, lse = _attn_fwd(q, k, v)
    y, hf, hb = _rnn_fwd(xr, xg, cw)
    late = late_weights([o, y])
    cw.update(wout=late["w_out"], wg=late["w_gate_t"], wu=late["w_up_t"], wd=late["w_down"])
    (do, dy, dh1, mixb, dh1b, hn2b, dgb, dub, actb, dh2b, loss, dga, dgr, dln2) = _stage_d(hp, o, y, tgt, cw)
    dwout = _wgrad(mixb, dh1b, "wgrad_out")
    dwg = _wgrad(dgb, hn2b, "wgrad_gate", tk=D_FF // 2)
    dwu = _wgrad(dub, hn2b, "wgrad_up", tk=D_FF // 2)
    dwd = _wgrad(actb, dh2b, "wgrad_down", tk=D_FF // 2)
    zero = early_grads(dict(w_out=dwout, w_gate=dwg, w_up=dwu, w_down=dwd))
    cw["conv_b"] = cw["conv_b"] + zero
    dxr, dxg, dcw, dcb, dw4, db4, dlam = _rnn_bwd(dy, xr, xg, hf, hb, cw)
    dq, dk, dv = _attn_bwd(q, k, v, o, lse, do)
    (dhp, dpb, dqrawb, dkrawb, hn1b, cqnb, ckvnb, dln1, dqag, dkvag, dqg, dkg) = _stage_a_bwd(
        dq, dk, dv, dxr, dxg, dh1, hp, pa, cs, sn, cw)

    dwin = _wgrad(dpb, hn1b, "wgrad_in", tk=PC // 2)
    dwq = _wgrad(dqrawb, cqnb, "wgrad_uq")
    dwk = _wgrad(dkrawb, ckvnb, "wgrad_uk")
    dwv = _wgrad(dv, ckvnb, "wgrad_uv")

    dwin_t = jnp.concatenate([dwin[:O_KR], dwin[O_KR + QK_NOPE:O_KR + QK_HEAD], dwin[O_XR:]], axis=0)
    dwq_t = dwq.reshape(N_HEADS, HP, Q_LORA)[:, :QK_HEAD].reshape(N_HEADS * QK_HEAD, Q_LORA)
    dwkv_t = jnp.concatenate([dwk.reshape(N_HEADS, HP, KV_LORA)[:, :QK_NOPE],
                              dwv.reshape(N_HEADS, V_HEAD, KV_LORA)], axis=1).reshape(2 * D_ATTN, KV_LORA)
    d4 = dw4.reshape(N_CG, 2, RNN_BW, 4, 2, RNN_BW)
    dgates = jnp.stack([d4[:, 0, :, :, 0, :], d4[:, 1, :, :, 1, :]], axis=1)
    dgates = dgates.transpose(3, 0, 1, 2, 4).reshape(4, N_HEADS, RNN_BW, RNN_BW)
    dbias = db4.reshape(N_CG, 4, CG).transpose(1, 0, 2).reshape(4, D_RNN)
    dhp3 = dhp.reshape(NB, TP, D)
    grads = dict(
        meta_tokens=jnp.sum(dhp3[:, :N_META], axis=0),
        ln1_g=dln1, w_in_t=dwin_t, q_a_norm_g=dqag, w_uq_t=dwq_t, kv_a_norm_g=dkvag, w_ukv_t=dwkv_t,
        q_norm_g=dqg[:, :QK_HEAD], k_norm_g=dkg[:, :QK_HEAD], conv_w=dcw[None], conv_b=dcb,
        lru_wa=jnp.stack([dgates[0], dgates[2]])[None], lru_ba=jnp.stack([dbias[0], dbias[2]])[None],
        lru_wi=jnp.stack([dgates[1], dgates[3]])[None], lru_bi=jnp.stack([dbias[1], dbias[3]])[None],
        lru_lambda=dlam.reshape(N_CG, 2, CG).transpose(1, 0, 2).reshape(1, 2, D_RNN),
        attn_out_g=dga, rnn_out_g=dgr, ln2_g=dln2,
    )
    return loss[0, 0], dhp3[:, N_META:T], grads, [dhp, dwin]


_ANY = pl.BlockSpec(memory_space=pl.ANY)


def _place():
    return lax.axis_index("x"), lax.axis_index("y"), lax.axis_index("c")


def _other_chips(x, y):
    return [(1 - x, y), (x, 1 - y), (1 - x, 1 - y)]


def _all_gather(arrs, name):
    n_arr = len(arrs)

    def body(*refs):
        x_refs, out_refs, zero_ref = refs[:n_arr], refs[n_arr:2 * n_arr], refs[2 * n_arr]
        send_sems, recv_sems, local_sems = refs[2 * n_arr + 1:]
        x, y, c = _place()
        me, sibling = (x, y, c), (x, y, 1 - c)
        chips = _other_chips(x, y)
        zero_ref[...] = jnp.zeros_like(zero_ref)

        def rows(a, px, py, pc):
            m = arrs[a].shape[0]
            return out_refs[a].at[pl.ds((4 * px + 2 * py + pc) * m, m), :]

        def copy(a, k, block, to, src=None):
            return pltpu.make_async_remote_copy(
                src_ref=rows(a, *block) if src is None else src, dst_ref=rows(a, *block),
                send_sem=send_sems.at[7 * a + k], recv_sem=recv_sems.at[7 * a + k], device_id=to, device_id_type=MESH)

        mine = [pltpu.make_async_copy(x_refs[a], rows(a, *me), local_sems.at[a]) for a in range(n_arr)]
        first, passed = [], []
        for a in range(n_arr):
            first.append(copy(a, 0, me, sibling, src=x_refs[a]))
            first += [copy(a, 1 + j, me, (*chip, c), src=x_refs[a]) for j, chip in enumerate(chips)]
        for cp in mine + first:
            cp.start()
        for a in range(n_arr):
            for j, chip in enumerate(chips):
                copy(a, 1 + j, (*chip, c), me).wait_recv()
                passed.append(copy(a, 4 + j, (*chip, c), sibling))
                passed[-1].start()
        for a in range(n_arr):
            copy(a, 0, sibling, me).wait_recv()
            for j, chip in enumerate(chips):
                copy(a, 4 + j, (*chip, 1 - c), me).wait_recv()
        for cp in first + passed:
            cp.wait_send()
        for cp in mine:
            cp.wait()

    outs = pl.pallas_call(
        body, name=name,
        out_shape=[jax.ShapeDtypeStruct((8 * a.shape[0], a.shape[1]), a.dtype) for a in arrs]
        + [jax.ShapeDtypeStruct((8, LANES), F32)],
        in_specs=[_ANY] * n_arr, out_specs=[_ANY] * n_arr + [pl.BlockSpec(memory_space=pltpu.VMEM)],
        scratch_shapes=[pltpu.SemaphoreType.DMA((7 * n_arr,)), pltpu.SemaphoreType.DMA((7 * n_arr,)),
                        pltpu.SemaphoreType.DMA((n_arr,))],
    )(*arrs)
    return outs[:n_arr], outs[n_arr]


def _pair_exchange(big, whole, name):
    n_s, _, m, n = big.shape
    n_copies = n_s + len(whole)

    def body(*refs):
        big_ref, whole_refs = refs[0], refs[1:1 + len(whole)]
        rbig_ref, rwhole_refs = refs[1 + len(whole)], refs[2 + len(whole):2 + 2 * len(whole)]
        send_sems, recv_sems = refs[-2:]
        x, y, c = _place()
        sibling = (x, y, 1 - c)
        copies = [pltpu.make_async_remote_copy(
            src_ref=big_ref.at[s, 1 - c], dst_ref=rbig_ref.at[s], send_sem=send_sems.at[s], recv_sem=recv_sems.at[s],
            device_id=sibling, device_id_type=MESH) for s in range(n_s)]
        copies += [pltpu.make_async_remote_copy(
            src_ref=a, dst_ref=r, send_sem=send_sems.at[n_s + i], recv_sem=recv_sems.at[n_s + i],
            device_id=sibling, device_id_type=MESH) for i, (a, r) in enumerate(zip(whole_refs, rwhole_refs))]
        for cp in copies:
            cp.start()
        for cp in copies:
            cp.wait()

    return pl.pallas_call(
        body, name=name,
        out_shape=[jax.ShapeDtypeStruct((n_s, m, n), big.dtype)] + [jax.ShapeDtypeStruct(a.shape, a.dtype) for a in whole],
        in_specs=[_ANY] * (1 + len(whole)), out_specs=[_ANY] * (1 + len(whole)),
        scratch_shapes=[pltpu.SemaphoreType.DMA((n_copies,)), pltpu.SemaphoreType.DMA((n_copies,))],
    )(big, *whole)


def _chip_exchange(big, small, name):
    _, m, n = big.shape
    ms = small.shape[0]

    def body(big_ref, small_ref, rbig_ref, rsmall_ref, send_sems, recv_sems):
        x, y, c = _place()
        copies = []
        for j, (tx, ty) in enumerate(_other_chips(x, y)):
            copies.append(pltpu.make_async_remote_copy(
                src_ref=big_ref.at[2 * tx + ty], dst_ref=rbig_ref.at[j], send_sem=send_sems.at[j],
                recv_sem=recv_sems.at[j], device_id=(tx, ty, c), device_id_type=MESH))
            copies.append(pltpu.make_async_remote_copy(
                src_ref=small_ref, dst_ref=rsmall_ref.at[j], send_sem=send_sems.at[3 + j],
                recv_sem=recv_sems.at[3 + j], device_id=(tx, ty, c), device_id_type=MESH))
        for cp in copies:
            cp.start()
        for cp in copies:
            cp.wait()

    return pl.pallas_call(
        body, name=name,
        out_shape=[jax.ShapeDtypeStruct((3, m, n), big.dtype), jax.ShapeDtypeStruct((3, ms, n), small.dtype)],
        in_specs=[_ANY, _ANY], out_specs=[_ANY, _ANY],
        scratch_shapes=[pltpu.SemaphoreType.DMA((6,)), pltpu.SemaphoreType.DMA((6,))],
    )(big, small)


def _pair_swap(arrs, name):
    k = len(arrs)

    def body(*refs):
        send_sems, recv_sems = refs[-2:]
        x, y, c = _place()
        copies = [pltpu.make_async_remote_copy(
            src_ref=refs[i], dst_ref=refs[k + i], send_sem=send_sems.at[i], recv_sem=recv_sems.at[i],
            device_id=(x, y, 1 - c), device_id_type=MESH) for i in range(k)]
        for cp in copies:
            cp.start()
        for cp in copies:
            cp.wait()

    return pl.pallas_call(
        body, name=name, out_shape=[jax.ShapeDtypeStruct(a.shape, a.dtype) for a in arrs], in_specs=[_ANY] * k,
        out_specs=[_ANY] * k, scratch_shapes=[pltpu.SemaphoreType.DMA((k,)), pltpu.SemaphoreType.DMA((k,))],
    )(*arrs)


_HBM = pl.BlockSpec(memory_space=pltpu.HBM)
_SEM = pl.BlockSpec(memory_space=pltpu.SEMAPHORE)
_EFFECT = pltpu.SideEffectType.DATAFLOW_SIDE_EFFECTING


def _chip_copies(src_ref, land_ref, sems, src_at, land_at, sending):
    x, y, c = _place()
    copies = []
    for j, (tx, ty) in enumerate(_other_chips(x, y)):
        owner = (x, y) if sending else (tx, ty)
        copies.append(pltpu.make_async_remote_copy(
            src_ref=src_at(src_ref, tx, ty), dst_ref=land_at(land_ref, j, *owner, c), send_sem=sems[j],
            recv_sem=sems[3 + j], device_id=(tx, ty, c), device_id_type=MESH))
    return copies


def _chips_start(name, src, land, src_at, land_at):
    def body(src_ref, land_ref, *outs):
        for cp in _chip_copies(src_ref, land_ref, outs[:6], src_at, land_at, True):
            cp.start()
        outs[8][...] = jnp.zeros_like(outs[8])

    outs = pl.pallas_call(
        body, name=name,
        out_shape=(pltpu.SemaphoreType.DMA(()),) * 6 + (pltpu.HBM(src.shape, src.dtype), pltpu.HBM(land.shape, land.dtype),
                                                        jax.ShapeDtypeStruct((8, LANES), F32)),
        in_specs=(_HBM, _HBM), out_specs=(_SEM,) * 6 + (_HBM, _HBM, pl.BlockSpec(memory_space=pltpu.VMEM)),
        input_output_aliases={0: 6, 1: 7},
        compiler_params=pltpu.CompilerParams(has_side_effects=_EFFECT),
    )(pltpu.with_memory_space_constraint(src, pltpu.HBM), pltpu.with_memory_space_constraint(land, pltpu.HBM))
    return outs[:6], outs[6], outs[7], outs[8]


def _chips_wait(name, sems, src, land, after, src_at, land_at):
    def body(src_ref, land_ref, *rest):
        for cp in _chip_copies(src_ref, land_ref, rest[:6], src_at, land_at, False):
            cp.wait_send()
            cp.wait_recv()

    return pl.pallas_call(
        body, name=name, out_shape=(pltpu.HBM(src.shape, src.dtype), pltpu.HBM(land.shape, land.dtype)),
        in_specs=(_HBM, _HBM) + (_SEM,) * 6 + (_ANY,) * len(after), out_specs=(_HBM, _HBM),
        input_output_aliases={0: 0, 1: 1}, compiler_params=pltpu.CompilerParams(has_side_effects=_EFFECT),
    )(src, land, *sems, *after)


def _gather_finish(land, pack, m):
    def body(land_ref, pack_ref, out_ref, stage, send_sems, recv_sems, load_sems, store_sems):
        x, y, c = _place()

        def rows(px, py, pc, ref=out_ref):
            return ref.at[pl.ds((4 * px + 2 * py + pc) * m, m), :]

        copies = [pltpu.make_async_remote_copy(
            src_ref=rows(tx, ty, c, land_ref), dst_ref=rows(tx, ty, c), send_sem=send_sems.at[j], recv_sem=recv_sems.at[j],
            device_id=(x, y, 1 - c), device_id_type=MESH) for j, (tx, ty) in enumerate(_other_chips(x, y))]
        loads = [pltpu.make_async_copy(pack_ref.at[pl.ds(h * m, m), :], stage.at[h], load_sems.at[h]) for h in range(2)]
        stores = [pltpu.make_async_copy(stage.at[h], rows(x, y, h), store_sems.at[h]) for h in range(2)]
        for cp in copies + loads:
            cp.start()
        for h in range(2):
            loads[h].wait()
            stores[h].start()
        for j, (tx, ty) in enumerate(_other_chips(x, y)):
            copies[j].wait_send()
            pltpu.make_async_remote_copy(
                src_ref=rows(tx, ty, 1 - c), dst_ref=rows(tx, ty, 1 - c), send_sem=send_sems.at[j],
                recv_sem=recv_sems.at[j], device_id=(x, y, 1 - c), device_id_type=MESH).wait_recv()
        for cp in stores:
            cp.wait()

    return pl.pallas_call(
        body, name="gather_late_finish", out_shape=jax.ShapeDtypeStruct(land.shape, land.dtype),
        in_specs=[_ANY, _ANY], out_specs=_ANY, input_output_aliases={0: 0},
        scratch_shapes=[pltpu.VMEM((2, m, land.shape[1]), land.dtype), pltpu.SemaphoreType.DMA((3,)),
                        pltpu.SemaphoreType.DMA((3,)), pltpu.SemaphoreType.DMA((2,)), pltpu.SemaphoreType.DMA((2,))],
    )(land, pack)


def _row_tile(rows, cap=512):
    for t in range(cap - cap % 8, 7, -8):
        if rows % t == 0:
            return t
    return rows


def _elementwise(fn, n_out, name, *arrs, out_dtype=F32):
    rows, cols = arrs[0].shape
    tr = _row_tile(rows)
    n_in = len(arrs)

    def body(*refs):
        outs = fn(*[r[...].astype(F32) for r in refs[:n_in]])
        for r, o in zip(refs[n_in:], outs):
            r[...] = o.astype(out_dtype)

    spec = pl.BlockSpec((tr, cols), lambda i: (i, 0))
    return pl.pallas_call(
        body, grid=(rows // tr,), name=name, in_specs=[spec] * n_in, out_specs=[spec] * n_out,
        out_shape=[jax.ShapeDtypeStruct((rows, cols), out_dtype)] * n_out, compiler_params=_params("arbitrary"),
    )(*arrs)


def _pair_sum(gpack, rbig, ci, name):
    n_s, _, m, n = gpack.shape
    tr = _row_tile(m)

    def body(c_ref, g_ref, r_ref, o_ref):
        o_ref[...] = (g_ref[...] + r_ref[...]).astype(BF)

    return pl.pallas_call(
        body, name=name, out_shape=jax.ShapeDtypeStruct((n_s, m, n), BF),
        grid_spec=pltpu.PrefetchScalarGridSpec(
            num_scalar_prefetch=1, grid=(n_s, m // tr),
            in_specs=[pl.BlockSpec((None, None, tr, n), lambda s, i, c: (s, c[0], i, 0)),
                      pl.BlockSpec((None, tr, n), lambda s, i, c: (s, i, 0))],
            out_specs=pl.BlockSpec((None, tr, n), lambda s, i, c: (s, i, 0))),
        compiler_params=_params("arbitrary", "arbitrary"),
    )(ci.reshape(1), gpack, rbig)


def _chip_sum(sums, landed, chip, name):
    _, m, n = sums.shape
    tr = _row_tile(m)

    def body(c_ref, own_ref, r0_ref, r1_ref, r2_ref, o_ref):
        f = lambda r: r[...].astype(F32)
        o_ref[...] = _add4(f(own_ref), f(r0_ref), f(r1_ref), f(r2_ref))[0]

    slot = lambda j: pl.BlockSpec((None, tr, n), lambda i, c: (j, i, 0))
    return pl.pallas_call(
        body, name=name, out_shape=jax.ShapeDtypeStruct((m, n), F32),
        grid_spec=pltpu.PrefetchScalarGridSpec(
            num_scalar_prefetch=1, grid=(m // tr,),
            in_specs=[pl.BlockSpec((None, tr, n), lambda i, c: (c[0], i, 0)), slot(0), slot(1), slot(2)],
            out_specs=pl.BlockSpec((tr, n), lambda i, c: (i, 0))),
        compiler_params=_params("arbitrary"),
    )(chip.reshape(1), sums, landed, landed, landed)


def _add2(a, b):
    return (a + b,)


def _add4(own, r0, r1, r2):
    return ((own + r2) + (r0 + r1),)


def _adamw_math(w, g, m, v):
    m = ADAM_B1 * m + (1.0 - ADAM_B1) * g
    v = ADAM_B2 * v + (1.0 - ADAM_B2) * (g * g)
    m_hat = m / (1.0 - ADAM_B1 ** ADAM_STEP)
    v_hat = v / (1.0 - ADAM_B2 ** ADAM_STEP)
    delta = -ADAM_LR * (m_hat / (jnp.sqrt(v_hat) + ADAM_EPS) + ADAM_WD * w)
    return delta, m, v


WEIGHTS = ["meta_tokens", "ln1_g", "w_in", "q_a_norm_g", "w_uq", "kv_a_norm_g", "w_ukv", "q_norm_g", "k_norm_g",
           "conv_w", "conv_b", "lru_wa", "lru_ba", "lru_wi", "lru_bi", "lru_lambda", "attn_out_g", "rnn_out_g",
           "w_out", "ln2_g", "w_gate", "w_up", "w_down"]
BIG = ["w_in", "w_uq", "w_ukv", "w_out", "w_gate", "w_up", "w_down"]
BIG_T = {"w_in": True, "w_uq": True, "w_ukv": True, "w_out": False, "w_gate": True, "w_up": True, "w_down": False}
BIG_ROWS = {"w_in": 424, "w_uq": 72, "w_ukv": 64, "w_out": 256, "w_gate": 704, "w_up": 704, "w_down": 704}
EARLY = ["w_in", "w_uq", "w_ukv"]
LATE = ["w_out", "w_gate", "w_up", "w_down"]
EARLY_ROWS = 576
LATE_ROWS = 2368
SMALL_SHARDED = ["meta_tokens", "conv_w", "lru_ba", "lru_bi", "lru_lambda"]
SMALL = [n for n in WEIGHTS if n not in BIG]
SMALL_PACK_ROWS = 160
SMALL_ADAM_ROWS = 144


def _offsets(names):
    off, o = {}, 0
    for n in names:
        off[n] = o
        o += BIG_ROWS[n]
    return off


def _shard_pack(names, src, rows):
    parts = [_to_pack_piece(n, src[n]) for n in names]
    used = sum(BIG_ROWS[n] for n in names)
    if rows > used:
        parts.append(jnp.zeros((rows - used, D), F32))
    return jnp.concatenate(parts, axis=0)


def _grad_pack(names, g, rows):
    parts = [g[n].reshape(N_CHIPS, BIG_ROWS[n], D) for n in names]
    used = sum(BIG_ROWS[n] for n in names)
    if rows > used:
        parts.append(jnp.zeros((N_CHIPS, rows - used, D), F32))
    return jnp.concatenate(parts, axis=1).reshape(N_CHIPS, 2, rows // 2, D)


def _both_halves(mine, other, ci):
    return jnp.where(ci == 0, jnp.concatenate([mine, other], axis=0), jnp.concatenate([other, mine], axis=0))


def _to_pack_piece(name, shard):
    a = shard[0].T if BIG_T[name] else shard[0]
    return a.reshape(BIG_ROWS[name], D)


def _from_pack_piece(name, piece, shard_shape):
    _, k, n = shard_shape
    return piece.reshape(n, k).T[None] if BIG_T[name] else piece.reshape(k, n)[None]


def _flat_pack(arrs, rows):
    flat = jnp.concatenate([a.reshape(-1) for a in arrs])
    return jnp.pad(flat, (0, rows * D - flat.shape[0])).reshape(rows, D)


def _flat_unpack(pack, shapes):
    flat, out, o = pack.reshape(-1), [], 0
    for s in shapes:
        n = math.prod(s)
        out.append(flat[o:o + n].reshape(s))
        o += n
    return out


def kernel(x, meta_tokens, ln1_g, w_in, q_a_norm_g, w_uq, kv_a_norm_g, w_ukv, q_norm_g, k_norm_g, conv_w, conv_b, lru_wa, lru_ba, lru_wi, lru_bi, lru_lambda, attn_out_g, rnn_out_g, w_out, ln2_g, w_gate, w_up, w_down, loss_target, m_meta_tokens, m_ln1_g, m_w_in, m_q_a_norm_g, m_w_uq, m_kv_a_norm_g, m_w_ukv, m_q_norm_g, m_k_norm_g, m_conv_w, m_conv_b, m_lru_wa, m_lru_ba, m_lru_wi, m_lru_bi, m_lru_lambda, m_attn_out_g, m_rnn_out_g, m_w_out, m_ln2_g, m_w_gate, m_w_up, m_w_down, v_meta_tokens, v_ln1_g, v_w_in, v_q_a_norm_g, v_w_uq, v_kv_a_norm_g, v_w_ukv, v_q_norm_g, v_k_norm_g, v_conv_w, v_conv_b, v_lru_wa, v_lru_ba, v_lru_wi, v_lru_bi, v_lru_lambda, v_attn_out_g, v_rnn_out_g, v_w_out, v_ln2_g, v_w_gate, v_w_up, v_w_down):
    wts = dict(zip(WEIGHTS, (meta_tokens, ln1_g, w_in, q_a_norm_g, w_uq, kv_a_norm_g, w_ukv, q_norm_g, k_norm_g, conv_w, conv_b, lru_wa, lru_ba, lru_wi, lru_bi, lru_lambda, attn_out_g, rnn_out_g, w_out, ln2_g, w_gate, w_up, w_down)))
    mom = dict(zip(WEIGHTS, (m_meta_tokens, m_ln1_g, m_w_in, m_q_a_norm_g, m_w_uq, m_kv_a_norm_g, m_w_ukv, m_q_norm_g, m_k_norm_g, m_conv_w, m_conv_b, m_lru_wa, m_lru_ba, m_lru_wi, m_lru_bi, m_lru_lambda, m_attn_out_g, m_rnn_out_g, m_w_out, m_ln2_g, m_w_gate, m_w_up, m_w_down)))
    var = dict(zip(WEIGHTS, (v_meta_tokens, v_ln1_g, v_w_in, v_q_a_norm_g, v_w_uq, v_kv_a_norm_g, v_w_ukv, v_q_norm_g, v_k_norm_g, v_conv_w, v_conv_b, v_lru_wa, v_lru_ba, v_lru_wi, v_lru_bi, v_lru_lambda, v_attn_out_g, v_rnn_out_g, v_w_out, v_ln2_g, v_w_gate, v_w_up, v_w_down)))
    xi, yi, ci = _place()
    chip = 2 * xi + yi
    off_e, off_l = _offsets(EARLY), _offsets(LATE)
    half_e, half_l = EARLY_ROWS // 2, LATE_ROWS // 2
    block_rows = lambda ref, j, px, py, c: ref.at[pl.ds((4 * px + 2 * py + c) * half_l, half_l), :]
    whole = lambda ref, tx, ty: ref
    shard_of = lambda ref, tx, ty: ref.at[2 * tx + ty]
    slot = lambda ref, j, px, py, c: ref.at[j]

    pack_e = _shard_pack(EARLY, wts, EARLY_ROWS).astype(BF)
    spack = jnp.concatenate([meta_tokens[:, :LANES], meta_tokens[:, LANES:], conv_w[0], lru_ba[0], lru_bi[0],
                             lru_lambda[0], jnp.zeros((6, LANES), F32)], axis=0)
    (ge, gs), gathered = _all_gather([lax.dynamic_slice_in_dim(pack_e, ci * half_e, half_e, axis=0),
                                      lax.dynamic_slice_in_dim(spack, ci * 24, 24, axis=0)], "gather_early")
    ge = ge.reshape(N_CHIPS, EARLY_ROWS, D)
    gs = gs.reshape(N_CHIPS, 48, LANES)
    full = {n: ge[:, off_e[n]:off_e[n] + BIG_ROWS[n]] for n in EARLY}
    pack_l = (_shard_pack(LATE, wts, LATE_ROWS) + gathered[0, 0]).astype(BF)
    sems_l, src_l, land_l, tied = _chips_start(
        "gather_late_start", lax.dynamic_slice_in_dim(pack_l, ci * half_l, half_l, axis=0),
        lax.empty((8 * half_l, D), BF), whole, block_rows)

    def late_weights(after):
        _, land = _chips_wait("gather_late_wait", sems_l, src_l, land_l, after, whole, block_rows)
        gl = _gather_finish(land, pack_l, half_l).reshape(N_CHIPS, LATE_ROWS, D)
        part = lambda n: gl[:, off_l[n]:off_l[n] + BIG_ROWS[n]].reshape(N_CHIPS * BIG_ROWS[n], D)
        return dict(w_out=part("w_out"), w_gate_t=part("w_gate"), w_up_t=part("w_up"), w_down=part("w_down"))

    late = {}

    def early_grads(g_late):
        gpack = _grad_pack(LATE, g_late, LATE_ROWS)
        (rbig,) = _pair_exchange(gpack, [], "grad_pair_exchange_late")
        chip_big = _pair_sum(gpack, rbig, ci, "grad_pair_sum_late")
        late["sems"], late["src"], late["land"], zeros = _chips_start(
            "grad_chip_late_start", chip_big, lax.empty((3, half_l, D), BF), shard_of, slot)
        return zeros[0, 0]

    cols = lambda a: a.transpose(1, 0, 2).reshape(a.shape[1], N_CHIPS * a.shape[2])
    meta_full = cols(jnp.concatenate([gs[:, 0:16], gs[:, 16:32]], axis=2))
    w = dict(
        w_in_t=full["w_in"].reshape(IN_COLS, D), w_uq_t=full["w_uq"].reshape(N_HEADS * QK_HEAD, Q_LORA),
        w_ukv_t=full["w_ukv"].reshape(2 * D_ATTN, KV_LORA),
        ln1_g=ln1_g, q_a_norm_g=q_a_norm_g, kv_a_norm_g=kv_a_norm_g, q_norm_g=q_norm_g, k_norm_g=k_norm_g,
        conv_w=cols(gs[:, 32:36]), conv_b=conv_b, lru_wa=lru_wa[0], lru_ba=cols(gs[:, 36:38]), lru_wi=lru_wi[0],
        lru_bi=cols(gs[:, 38:40]), lru_lambda=cols(gs[:, 40:42]), attn_out_g=attn_out_g, rnn_out_g=rnn_out_g,
        ln2_g=ln2_g,
    )

    loss_local, grad_x, g, last = _local_step(x, loss_target, meta_full + tied[0, 0], w, late_weights, early_grads)
    loss = lax.psum(loss_local, ("x", "y", "c"))

    gpack = _grad_pack(EARLY, {"w_in": g["w_in_t"], "w_uq": g["w_uq_t"], "w_ukv": g["w_ukv_t"]}, EARLY_ROWS)
    full_shapes = {n: wts[n].shape for n in SMALL}
    full_shapes.update(meta_tokens=(N_META, D), conv_w=(1, CONV_W, D_RNN), lru_ba=(1, 2, D_RNN), lru_bi=(1, 2, D_RNN),
                       lru_lambda=(1, 2, D_RNN))
    gsmall = _flat_pack([g[n] for n in SMALL], SMALL_PACK_ROWS)
    rbig, rsmall = _pair_exchange(gpack, [gsmall], "grad_pair_exchange")
    chip_big = _pair_sum(gpack, rbig, ci, "grad_pair_sum")
    (chip_small,) = _elementwise(_add2, 1, "grad_pair_sum_small", gsmall, rsmall)
    xbig, xsmall = _chip_exchange(chip_big, chip_small, "grad_chip_exchange")
    sum_e = _chip_sum(chip_big, xbig, chip, "grad_chip_sum")
    (small_sum,) = _elementwise(_add4, 1, "grad_chip_sum_small", chip_small, xsmall[0], xsmall[1], xsmall[2])
    src, land = _chips_wait("grad_chip_late_wait", late["sems"], late["src"], late["land"], last + [sum_e], shard_of, slot)
    sum_l = _chip_sum(src, land, chip, "grad_chip_sum_late")
    other_e, other_l = _pair_swap([sum_e, sum_l], "grad_pair_swap")
    gshard_e, gshard_l = _both_halves(sum_e, other_e, ci), _both_halves(sum_l, other_l, ci)

    grads = {n: _from_pack_piece(n, gshard_e[off_e[n]:off_e[n] + BIG_ROWS[n]], wts[n].shape) for n in EARLY}
    grads.update({n: _from_pack_piece(n, gshard_l[off_l[n]:off_l[n] + BIG_ROWS[n]], wts[n].shape) for n in LATE})
    small_full = dict(zip(SMALL, _flat_unpack(small_sum, [full_shapes[n] for n in SMALL])))
    for n in SMALL:
        a = small_full[n]
        if n in SMALL_SHARDED:
            width = wts[n].shape[-1]
            a = lax.dynamic_slice_in_dim(a, chip * width, width, axis=a.ndim - 1)
        grads[n] = a.reshape(wts[n].shape)

    delta, new_m, new_v = {}, {}, {}
    for n in BIG:
        two_d = lambda a: a.reshape(a.shape[-2], a.shape[-1])
        d_, m_, v_ = _elementwise(_adamw_math, 3, "adamw_" + n, two_d(wts[n]), two_d(grads[n]), two_d(mom[n]), two_d(var[n]))
        delta[n], new_m[n], new_v[n] = (a.reshape(wts[n].shape) for a in (d_, m_, v_))
    packs = [_flat_pack([src[n] for n in SMALL], SMALL_ADAM_ROWS) for src in (wts, grads, mom, var)]
    outs = _elementwise(_adamw_math, 3, "adamw_small", *packs)
    for dst, o in zip((delta, new_m, new_v), outs):
        dst.update(zip(SMALL, _flat_unpack(o, [wts[n].shape for n in SMALL])))

    return (loss, grad_x, *[grads[n] for n in WEIGHTS], *[delta[n] for n in WEIGHTS],
            *[new_m[n] for n in WEIGHTS], *[new_v[n] for n in WEIGHTS])
```

```python
import functools
import math

import jax
import jax.numpy as jnp
from jax import lax
from jax.experimental import pallas as pl
from jax.experimental.pallas import tpu as pltpu

F32 = jnp.float32
BF = jnp.bfloat16
MESH = pl.DeviceIdType.MESH

D = 1024
SEQ = 2048
N_META = 16
T = N_META + SEQ
N_HEADS = 8
QK_NOPE = 64
QK_ROPE = 32
QK_HEAD = 96
V_HEAD = 64
Q_LORA = 384
KV_LORA = 256
D_ATTN = 512
D_RNN = 512
RNN_BW = 64
CONV_W = 4
LRU_C = 8.0
ROPE_THETA = 10000.0
D_FF = 2816
EPS = 1e-6
IN_COLS = 1696
ADAM_LR, ADAM_B1, ADAM_B2, ADAM_EPS, ADAM_WD, ADAM_STEP = 0.001, 0.9, 0.999, 1e-08, 0.01, 10

LANES = 128
TP = 2176
NB = 2
R = NB * TP
TR = 256
TQ = 544
HP = LANES
PC = 1792
O_CKV, O_KR, O_XR, O_XG = 384, 640, 768, 1280
CG = 128
N_CG = D_RNN // CG
VMEM_LIMIT = 56 * 1024 * 1024
N_CHIPS = 4
SCALE = QK_HEAD ** -0.5
KEY_MASK = -30000.0
LOG2_E = 1.4426950408889634
SCALE_LOG2 = SCALE * LOG2_E


def _nt(a, b):
    return lax.dot_general(a, b, (((1,), (1,)), ((), ())), preferred_element_type=F32)


def _nn(a, b):
    return jnp.dot(a, b, preferred_element_type=F32)


def _tn(a, b):
    return lax.dot_general(a, b, (((0,), (0,)), ((), ())), preferred_element_type=F32)


def _rms(x, g, n):
    ms = jnp.sum(x * x, axis=-1, keepdims=True) * (1.0 / n)
    return x * lax.rsqrt(ms + EPS) * g


def _rot_impl(x):
    lane = lax.broadcasted_iota(jnp.int32, x.shape, 1)
    left = pltpu.roll(x, HP - 16, 1)
    right = pltpu.roll(x, 16, 1)
    lo = (lane >= QK_NOPE) & (lane < QK_NOPE + 16)
    hi = (lane >= QK_NOPE + 16) & (lane < QK_HEAD)
    return jnp.where(lo, -left, jnp.where(hi, right, 0.0))


@jax.custom_vjp
def _rot(x):
    return _rot_impl(x)


def _rot_fwd(x):
    return _rot_impl(x), None


def _rot_bwd(_, g):
    return (-_rot_impl(g),)


_rot.defvjp(_rot_fwd, _rot_bwd)


def _head(x, g, cs, sn):
    n = _rms(x, g, QK_HEAD)
    return n * cs + _rot(n) * sn


def _head_bwd(x, g, cs, sn, dout):
    rs = lax.rsqrt(jnp.sum(x * x, axis=-1, keepdims=True) * (1.0 / QK_HEAD) + EPS)
    xh = x * rs
    dn = dout * cs - _rot_impl(dout * sn)
    gdn = g * dn
    t = jnp.sum(gdn * xh, axis=-1, keepdims=True) * (1.0 / QK_HEAD)
    return rs * (gdn - xh * t), jnp.sum(dn * xh, axis=0, keepdims=True)


def _const_spec(shape):
    return pl.BlockSpec(shape, lambda *_: (0,) * len(shape), pipeline_mode=pl.Buffered(1))


def _row_spec(n, tr=TR):
    return pl.BlockSpec((tr, n), lambda i: (i, 0))


def _params(*sem, vmem=VMEM_LIMIT):
    return pltpu.CompilerParams(dimension_semantics=sem, vmem_limit_bytes=vmem)


def _stage_a_fwd(hp, cs, sn, cw):
    def body(hp_ref, cs_ref, sn_ref, ln1, win, qag, wq, kvag, wk, wv, qg, kg,
             pa_ref, xr_ref, xg_ref, q_ref, k_ref, v_ref):
        hn = _rms(hp_ref[...], ln1[...], D).astype(BF)
        p = _nt(hn, win[...])
        pa_ref[...] = p[:, :O_XR]
        xr_ref[...] = p[:, O_XR:O_XG]
        xg_ref[...] = p[:, O_XG:]
        cqn = _rms(p[:, :O_CKV], qag[...], Q_LORA).astype(BF)
        ckvn = _rms(p[:, O_CKV:O_KR], kvag[...], KV_LORA).astype(BF)
        kr = p[:, O_KR:O_XR]
        c, s = cs_ref[...], sn_ref[...]
        mask_lane = lax.broadcasted_iota(jnp.int32, (1, HP), 1) == QK_HEAD
        row = pl.program_id(0) * TR + lax.broadcasted_iota(jnp.int32, (TR, 1), 0)
        key_mask = jnp.where(jnp.where(row >= TP, row - TP, row) < T, 0.0, KEY_MASK)
        qraw = _nt(cqn, wq[...])
        kraw = _nt(ckvn, wk[...])
        for h in range(N_HEADS):
            sl = slice(h * HP, (h + 1) * HP)
            q_ref[:, sl] = jnp.where(mask_lane, 1.0, _head(qraw[:, sl], qg[...], c, s)).astype(BF)
            k_ref[:, sl] = jnp.where(mask_lane, key_mask, _head(kraw[:, sl] + kr, kg[...], c, s)).astype(BF)
        v_ref[...] = _nt(ckvn, wv[...]).astype(BF)

    return pl.pallas_call(
        body, grid=(R // TR,), name="stage_a_fwd",
        in_specs=[_row_spec(D), _row_spec(HP), _row_spec(HP), _const_spec((1, D)), _const_spec((PC, D)),
                  _const_spec((1, Q_LORA)), _const_spec((N_HEADS * HP, Q_LORA)), _const_spec((1, KV_LORA)),
                  _const_spec((N_HEADS * HP, KV_LORA)), _const_spec((D_ATTN, KV_LORA)), _const_spec((1, HP)),
                  _const_spec((1, HP))],
        out_specs=[_row_spec(O_XR), _row_spec(D_RNN), _row_spec(D_RNN), _row_spec(N_HEADS * HP),
                   _row_spec(N_HEADS * HP), _row_spec(D_ATTN)],
        out_shape=[jax.ShapeDtypeStruct((R, O_XR), F32), jax.ShapeDtypeStruct((R, D_RNN), F32),
                   jax.ShapeDtypeStruct((R, D_RNN), F32), jax.ShapeDtypeStruct((R, N_HEADS * HP), BF),
                   jax.ShapeDtypeStruct((R, N_HEADS * HP), BF), jax.ShapeDtypeStruct((R, D_ATTN), BF)],
        compiler_params=_params("arbitrary"),
    )(hp, cs, sn, cw["ln1_g"], cw["win"], cw["qa_g"], cw["wq"], cw["kva_g"], cw["wk"], cw["wv"], cw["q_g"], cw["k_g"])


def _stage_a_bwd(dq, dk, dv, dxr, dxg, dh1, hp, pa, cs, sn, cw):
    def body(dq_ref, dk_ref, dv_ref, dxr_ref, dxg_ref, dh1_ref, hp_ref, pa_ref, cs_ref, sn_ref,
             ln1, win, qag, wq, kvag, wk, wv, qg, kg,
             dhp_ref, dp_ref, dqraw_ref, dkraw_ref, hn_ref, cqn_ref, ckvn_ref,
             dln1_ref, dqag_ref, dkvag_ref, dqg_ref, dkg_ref):
        @pl.when(pl.program_id(0) == 0)
        def _():
            for r in (dln1_ref, dqag_ref, dkvag_ref, dqg_ref, dkg_ref):
                r[...] = jnp.zeros_like(r)

        hn, vjp_ln1 = jax.vjp(lambda h, g: _rms(h, g, D), hp_ref[...], ln1[...])
        hn_ref[...] = hn.astype(BF)
        pa_v = pa_ref[...]
        cqn, vjp_qa = jax.vjp(lambda x, g: _rms(x, g, Q_LORA), pa_v[:, :O_CKV], qag[...])
        ckvn, vjp_kva = jax.vjp(lambda x, g: _rms(x, g, KV_LORA), pa_v[:, O_CKV:O_KR], kvag[...])
        kr = pa_v[:, O_KR:O_XR]
        cqnb, ckvnb = cqn.astype(BF), ckvn.astype(BF)
        cqn_ref[...] = cqnb
        ckvn_ref[...] = ckvnb
        c, s = cs_ref[...], sn_ref[...]
        lane = lax.broadcasted_iota(jnp.int32, (1, HP), 1)
        rope_lanes = ((lane >= QK_NOPE) & (lane < QK_HEAD)).astype(F32)
        dkr = jnp.zeros((TR, HP), F32)
        dqg = jnp.zeros((1, HP), F32)
        dkg = jnp.zeros((1, HP), F32)
        qraw = _nt(cqnb, wq[...])
        kraw = _nt(ckvnb, wk[...])
        for h in range(N_HEADS):
            sl = slice(h * HP, (h + 1) * HP)
            dqraw, dg = _head_bwd(qraw[:, sl], qg[...], c, s, dq_ref[:, sl])
            dqg = dqg + dg
            dqraw_ref[:, sl] = dqraw.astype(BF)
            dkraw, dg = _head_bwd(kraw[:, sl] + kr, kg[...], c, s, dk_ref[:, sl])
            dkg = dkg + dg
            dkraw_ref[:, sl] = dkraw.astype(BF)
            dkr = dkr + dkraw * rope_lanes
        dcq, dqag = vjp_qa(_nn(dqraw_ref[...], wq[...]))
        dckv, dkvag = vjp_kva(_nn(dkraw_ref[...], wk[...]) + _nn(dv_ref[...].astype(BF), wv[...]))
        dpb = jnp.concatenate([dcq, dckv, dkr, dxr_ref[...], dxg_ref[...]], axis=1).astype(BF)
        dp_ref[...] = dpb
        dh, dln1 = vjp_ln1(_nn(dpb, win[...]))
        dhp_ref[...] = dh + dh1_ref[...]
        dln1_ref[...] += dln1
        dqag_ref[...] += dqag
        dkvag_ref[...] += dkvag
        dqg_ref[...] += dqg
        dkg_ref[...] += dkg

    acc = lambda n: pl.BlockSpec((1, n), lambda i: (0, 0))
    return pl.pallas_call(
        body, grid=(R // TR,), name="stage_a_bwd",
        in_specs=[_row_spec(N_HEADS * HP), _row_spec(N_HEADS * HP), _row_spec(D_ATTN), _row_spec(D_RNN),
                  _row_spec(D_RNN), _row_spec(D), _row_spec(D), _row_spec(O_XR), _row_spec(HP), _row_spec(HP),
                  _const_spec((1, D)), _const_spec((PC, D)), _const_spec((1, Q_LORA)),
                  _const_spec((N_HEADS * HP, Q_LORA)), _const_spec((1, KV_LORA)),
                  _const_spec((N_HEADS * HP, KV_LORA)), _const_spec((D_ATTN, KV_LORA)), _const_spec((1, HP)),
                  _const_spec((1, HP))],
        out_specs=[_row_spec(D), _row_spec(PC), _row_spec(N_HEADS * HP), _row_spec(N_HEADS * HP), _row_spec(D),
                   _row_spec(Q_LORA), _row_spec(KV_LORA), acc(D), acc(Q_LORA), acc(KV_LORA), acc(HP), acc(HP)],
        out_shape=[jax.ShapeDtypeStruct((R, D), F32), jax.ShapeDtypeStruct((R, PC), BF),
                   jax.ShapeDtypeStruct((R, N_HEADS * HP), BF), jax.ShapeDtypeStruct((R, N_HEADS * HP), BF),
                   jax.ShapeDtypeStruct((R, D), BF), jax.ShapeDtypeStruct((R, Q_LORA), BF),
                   jax.ShapeDtypeStruct((R, KV_LORA), BF), jax.ShapeDtypeStruct((1, D), F32),
                   jax.ShapeDtypeStruct((1, Q_LORA), F32), jax.ShapeDtypeStruct((1, KV_LORA), F32),
                   jax.ShapeDtypeStruct((1, HP), F32), jax.ShapeDtypeStruct((1, HP), F32)],
        compiler_params=_params("arbitrary"),
    )(dq, dk, dv, dxr, dxg, dh1, hp, pa, cs, sn, cw["ln1_g"], cw["win"], cw["qa_g"], cw["wq"], cw["kva_g"],
      cw["wk"], cw["wv"], cw["q_g"], cw["k_g"])


def _head_mask(half, dtype):
    lane = lax.broadcasted_iota(jnp.int32, (1, 2 * V_HEAD), 1)
    return ((lane >= V_HEAD) == (half == 1)).astype(dtype)


_ATTN_GRID = (NB, N_HEADS // 2, TP // TQ)
_Q_SPEC = pl.BlockSpec((TQ, 2 * HP), lambda b, j, i: (b * (TP // TQ) + i, j))
_K_SPEC = pl.BlockSpec((TP, 2 * HP), lambda b, j, i: (b, j))
_V_SPEC = pl.BlockSpec((TP, 2 * V_HEAD), lambda b, j, i: (b, j))
_O_SPEC = pl.BlockSpec((TQ, 2 * V_HEAD), lambda b, j, i: (b * (TP // TQ) + i, j))
_LSE_SPEC = pl.BlockSpec((None, TQ, 2), lambda b, j, i: (j, b * (TP // TQ) + i, 0))


def _attn_fwd(q, k, v):
    def body(q_ref, k_ref, v_ref, o_ref, lse_ref):
        v2 = v_ref[...]
        o = jnp.zeros((TQ, 2 * V_HEAD), F32)
        lse = []
        for hh in range(2):
            sl = slice(hh * HP, (hh + 1) * HP)
            raw = _nt(q_ref[:, sl], k_ref[:, sl])
            m = jnp.max(raw, axis=-1, keepdims=True)
            e = jnp.exp2((raw - m) * SCALE_LOG2)
            l = jnp.sum(e, axis=-1, keepdims=True)
            o = o + _nn(e.astype(BF), v2 * _head_mask(hh, BF)) * (1.0 / l)
            lse.append(m * SCALE_LOG2 + jnp.log(l) * LOG2_E)
        o_ref[...] = o
        lane = lax.broadcasted_iota(jnp.int32, (TQ, 2), 1)
        lse_ref[...] = jnp.where(lane == 0, lse[0], lse[1])

    return pl.pallas_call(
        body, grid=_ATTN_GRID, name="attn_fwd", in_specs=[_Q_SPEC, _K_SPEC, _V_SPEC], out_specs=[_O_SPEC, _LSE_SPEC],
        out_shape=[jax.ShapeDtypeStruct((R, D_ATTN), F32), jax.ShapeDtypeStruct((N_HEADS // 2, R, 2), F32)],
        compiler_params=_params("arbitrary", "arbitrary", "arbitrary"),
    )(q, k, v)


def _attn_bwd(q, k, v, o, lse, do):
    def body(q_ref, k_ref, v_ref, o_ref, lse_ref, do_ref, dq_ref, dk_ref, dv_ref):
        @pl.when(pl.program_id(2) == 0)
        def _():
            dk_ref[...] = jnp.zeros_like(dk_ref)
            dv_ref[...] = jnp.zeros_like(dv_ref)

        do = do_ref[...]
        dob = do.astype(BF)
        do_o = do * o_ref[...]
        v2 = v_ref[...]
        dv_sum = jnp.zeros((TP, 2 * V_HEAD), F32)
        for hh in range(2):
            sl = slice(hh * HP, (hh + 1) * HP)
            qb, kb = q_ref[:, sl], k_ref[:, sl]
            p = jnp.exp2(_nt(qb, kb) * SCALE_LOG2 - lse_ref[:, hh:hh + 1])
            dp = _nt(dob, v2 * _head_mask(hh, BF))
            delta = jnp.sum(do_o * _head_mask(hh, F32), axis=-1, keepdims=True)
            dsb = (p * (dp - delta) * SCALE).astype(BF)
            dq_ref[:, sl] = _nn(dsb, kb)
            dk_ref[:, sl] += _tn(dsb, qb)
            dv_sum = dv_sum + _tn(p.astype(BF), dob) * _head_mask(hh, F32)
        dv_ref[...] += dv_sum

    return pl.pallas_call(
        body, grid=_ATTN_GRID, name="attn_bwd", in_specs=[_Q_SPEC, _K_SPEC, _V_SPEC, _O_SPEC, _LSE_SPEC, _O_SPEC],
        out_specs=[_Q_SPEC, _K_SPEC, _V_SPEC],
        out_shape=[jax.ShapeDtypeStruct((R, N_HEADS * HP), F32), jax.ShapeDtypeStruct((R, N_HEADS * HP), F32),
                   jax.ShapeDtypeStruct((R, D_ATTN), F32)],
        compiler_params=_params("arbitrary", "arbitrary", "arbitrary"),
    )(q, k, v, o, lse, do)


def _tile_prefix(a_ref, b_ref, reverse):
    tiles = (TP // 8, 8, CG)
    r8 = lax.broadcasted_iota(jnp.int32, tiles, 1)
    a, b = a_ref[...].reshape(tiles), b_ref[...].reshape(tiles)
    for s in (1, 2, 4):
        shift = 8 - s if reverse else s
        keep = (r8 < 8 - s) if reverse else (r8 >= s)
        b = jnp.where(keep, a * pltpu.roll(b, shift, 1) + b, b)
        a = jnp.where(keep, a * pltpu.roll(a, shift, 1), a)
    a_ref[...] = a.reshape(TP, CG)
    b_ref[...] = b.reshape(TP, CG)


def _scan_pair(af_ref, bf_ref, hf_ref, ab_ref, bb_ref, hb_ref):
    _tile_prefix(af_ref, bf_ref, False)
    _tile_prefix(ab_ref, bb_ref, True)
    n_tiles = TP // 8

    def step(i, carry):
        cf, cb = carry
        rf = pl.multiple_of(i * 8, 8)
        rb = pl.multiple_of((n_tiles - 1 - i) * 8, 8)
        hf_ref[pl.ds(rf, 8), :] = bf_ref[pl.ds(rf, 8), :] + af_ref[pl.ds(rf, 8), :] * cf
        hb_ref[pl.ds(rb, 8), :] = bb_ref[pl.ds(rb, 8), :] + ab_ref[pl.ds(rb, 8), :] * cb
        cf = bf_ref[pl.ds(rf + 7, 1), :] + af_ref[pl.ds(rf + 7, 1), :] * cf
        cb = bb_ref[pl.ds(rb, 1), :] + ab_ref[pl.ds(rb, 1), :] * cb
        return cf, cb

    zero = jnp.zeros((1, CG), F32)
    lax.fori_loop(0, n_tiles, step, (zero, zero), unroll=8)


def _shifts(x):
    t = lax.broadcasted_iota(jnp.int32, x.shape, 0)
    xm2 = jnp.where(t >= 2, pltpu.roll(x, 2, 0), 0.0)
    xm1 = jnp.where(t >= 1, pltpu.roll(x, 1, 0), 0.0)
    xp1 = jnp.where(t < TP - 1, pltpu.roll(x, TP - 1, 0), 0.0)
    return xm2, xm1, xp1


def _softplus(z):
    e = jnp.exp(-jnp.abs(z))
    small = e * (1.0 - e * (0.5 - e * (1.0 / 3.0)))
    return jnp.maximum(z, 0.0) + jnp.where(e < 0.01, small, jnp.log(1.0 + e))


def _neg_expm1(x):
    series = -x * (1.0 + x * 0.5 * (1.0 + x * (1.0 / 3.0) * (1.0 + x * 0.25)))
    return jnp.where(x > -0.05, series, 1.0 - jnp.exp(x))


def _gates(row0, xc, pa_f, pi_f, pa_b, pi_b, lam_f, lam_b):
    t = row0 + lax.broadcasted_iota(jnp.int32, xc.shape, 0)
    valid = t < T
    out = []
    for pa, pi_, lam in ((pa_f, pi_f, lam_f), (pa_b, pi_b, lam_b)):
        r = jax.nn.sigmoid(pa)
        gate_i = jax.nn.sigmoid(pi_)
        log_a = -LRU_C * r * _softplus(-lam)
        a = jnp.exp(log_a)
        mult = jnp.sqrt(jnp.maximum(_neg_expm1(2.0 * log_a), 0.0))
        out += [a, jnp.where(valid, mult * (gate_i * xc), 0.0)]
    return tuple(out)


def _gates_bwd(row0, xc, pres, lams, cots):
    t = row0 + lax.broadcasted_iota(jnp.int32, xc.shape, 0)
    valid = t < T
    dxc = jnp.zeros_like(xc)
    dpres, dlams = [], []
    for d in range(2):
        pa, pi_, lam = pres[2 * d], pres[2 * d + 1], lams[d]
        da, db = cots[2 * d], jnp.where(valid, cots[2 * d + 1], 0.0)
        r = jax.nn.sigmoid(pa)
        gate_i = jax.nn.sigmoid(pi_)
        sp = _softplus(-lam)
        log_a = -LRU_C * r * sp
        a = jnp.exp(log_a)
        m2 = jnp.maximum(_neg_expm1(2.0 * log_a), 0.0)
        mult = jnp.sqrt(m2)
        dxc = dxc + db * (mult * gate_i)
        d_gate = db * (mult * xc)
        d_m2 = jnp.where(m2 > 0.0, db * (gate_i * xc) * (0.5 * lax.rsqrt(m2)), 0.0)
        d_log_a = da * a - 2.0 * d_m2 * (a * a)
        dpres += [d_log_a * (-LRU_C * sp) * (r * (1.0 - r)), d_gate * (gate_i * (1.0 - gate_i))]
        d_sp = jnp.sum(d_log_a * (-LRU_C * r), axis=0, keepdims=True)
        dlams.append(-d_sp * jax.nn.sigmoid(-lam))
    return dxc, dpres, dlams


def _rnn_specs():
    seq = pl.BlockSpec((TP, CG), lambda g, b: (b, g))
    return dict(
        seq=seq,
        cw=pl.BlockSpec((CONV_W, CG), lambda g, b: (0, g)),
        cb=pl.BlockSpec((1, CG), lambda g, b: (0, g)),
        w4=pl.BlockSpec((None, CG, 4 * CG), lambda g, b: (g, 0, 0)),
        b4=pl.BlockSpec((None, 1, 4 * CG), lambda g, b: (g, 0, 0)),
        lam=pl.BlockSpec((None, 1, 2 * CG), lambda g, b: (g, 0, 0)),
    )


def _conv(x, xm2, xm1, xp1, cw_ref, cb_ref):
    return cw_ref[0:1, :] * xm2 + cw_ref[1:2, :] * xm1 + cw_ref[2:3, :] * x + cw_ref[3:4, :] * xp1 + cb_ref[...]


TC = 128
N_TC = TP // TC


def _split4(pre):
    return pre[:, :CG], pre[:, CG:2 * CG], pre[:, 2 * CG:3 * CG], pre[:, 3 * CG:]


def _rnn_fwd(xr, xg, cw):
    def body(xr_ref, xg_ref, cw_ref, cb_ref, w4_ref, b4_ref, lam_ref, y_ref, hf_ref, hb_ref, xc_s, af, bf, ab, bb):
        x = xr_ref[...]
        xc_s[...] = _conv(x, *_shifts(x), cw_ref, cb_ref)
        lam = lam_ref[...]

        def chunk(i, _):
            rows = pl.ds(pl.multiple_of(i * TC, TC), TC)
            xc = xc_s[rows, :]
            pre = _nn(xc.astype(BF), w4_ref[...]) + b4_ref[...]
            a_f, b_f, a_b, b_b = _gates(i * TC, xc, *_split4(pre), lam[:, :CG], lam[:, CG:])
            af[rows, :] = a_f
            bf[rows, :] = b_f
            ab[rows, :] = a_b
            bb[rows, :] = b_b
            return 0

        lax.fori_loop(0, N_TC, chunk, 0)
        _scan_pair(af, bf, hf_ref, ab, bb, hb_ref)
        y_ref[...] = (hf_ref[...] + hb_ref[...]) * jax.nn.gelu(xg_ref[...])

    sp = _rnn_specs()
    return pl.pallas_call(
        body, grid=(N_CG, NB), name="rnn_fwd",
        in_specs=[sp["seq"], sp["seq"], sp["cw"], sp["cb"], sp["w4"], sp["b4"], sp["lam"]],
        out_specs=[sp["seq"]] * 3, out_shape=[jax.ShapeDtypeStruct((R, D_RNN), F32)] * 3,
        scratch_shapes=[pltpu.VMEM((TP, CG), F32)] * 5,
        compiler_params=_params("arbitrary", "arbitrary"),
    )(xr, xg, cw["conv_w"], cw["conv_b"], cw["w4"], cw["b4"], cw["lam"])


def _rnn_bwd(dy, xr, xg, hf, hb, cw):
    def body(dy_ref, xr_ref, xg_ref, hf_ref, hb_ref, cw_ref, cb_ref, w4_ref, b4_ref, lam_ref,
             dxr_ref, dxg_ref, dcw_ref, dcb_ref, dw4_ref, db4_ref, dlam_ref,
             xc_s, af_s, ab_s, dhs_s, dhs2_s, lf_s, lb_s, daf_s, dab_s, dxc_s):
        @pl.when(pl.program_id(1) == 0)
        def _():
            for r in (dcw_ref, dcb_ref, dw4_ref, db4_ref, dlam_ref):
                r[...] = jnp.zeros_like(r)

        x = xr_ref[...]
        xc_s[...] = _conv(x, *_shifts(x), cw_ref, cb_ref)
        lam = lam_ref[...]

        def chunk1(i, _):
            rows = pl.ds(pl.multiple_of(i * TC, TC), TC)
            xc = xc_s[rows, :]
            pre = _nn(xc.astype(BF), w4_ref[...]) + b4_ref[...]
            a_f, _, a_b, _ = _gates(i * TC, xc, *_split4(pre), lam[:, :CG], lam[:, CG:])
            af_s[rows, :] = a_f
            ab_s[rows, :] = a_b
            _, vjp_y = jax.vjp(lambda h, g: h * jax.nn.gelu(g), hf_ref[rows, :] + hb_ref[rows, :], xg_ref[rows, :])
            dhs, dxg = vjp_y(dy_ref[rows, :])
            dhs_s[rows, :] = dhs
            dhs2_s[rows, :] = dhs
            dxg_ref[rows, :] = dxg
            return 0

        lax.fori_loop(0, N_TC, chunk1, 0)
        t = lax.broadcasted_iota(jnp.int32, (TP, CG), 0)
        af_s[...] = pltpu.roll(af_s[...], TP - 1, 0)
        ab_s[...] = pltpu.roll(ab_s[...], 1, 0)
        _scan_pair(ab_s, dhs_s, lb_s, af_s, dhs2_s, lf_s)
        daf_s[...] = lf_s[...] * jnp.where(t >= 1, pltpu.roll(hf_ref[...], 1, 0), 0.0)
        dab_s[...] = lb_s[...] * jnp.where(t < TP - 1, pltpu.roll(hb_ref[...], TP - 1, 0), 0.0)

        def chunk2(i, _):
            rows = pl.ds(pl.multiple_of(i * TC, TC), TC)
            xc = xc_s[rows, :]
            xcb = xc.astype(BF)
            pre = _nn(xcb, w4_ref[...]) + b4_ref[...]
            dxc, dpres, dlams = _gates_bwd(i * TC, xc, _split4(pre), (lam[:, :CG], lam[:, CG:]),
                                           (daf_s[rows, :], lf_s[rows, :], dab_s[rows, :], lb_s[rows, :]))
            dpre = jnp.concatenate(dpres, axis=1)
            dpreb = dpre.astype(BF)
            dxc_s[rows, :] = dxc + _nt(dpreb, w4_ref[...])
            dw4_ref[...] += _tn(xcb, dpreb)
            db4_ref[...] += jnp.sum(dpre, axis=0, keepdims=True)
            dlam_ref[...] += jnp.concatenate(dlams, axis=1)
            return 0

        lax.fori_loop(0, N_TC, chunk2, 0)
        dxc = dxc_s[...]
        dcb_ref[...] += jnp.sum(dxc, axis=0, keepdims=True)
        for tap, xs in enumerate(_shifts(x)[:2] + (x,) + _shifts(x)[2:]):
            dcw_ref[tap:tap + 1, :] += jnp.sum(xs * dxc, axis=0, keepdims=True)
        dxr_ref[...] = (cw_ref[0:1, :] * jnp.where(t < TP - 2, pltpu.roll(dxc, TP - 2, 0), 0.0)
                        + cw_ref[1:2, :] * jnp.where(t < TP - 1, pltpu.roll(dxc, TP - 1, 0), 0.0)
                        + cw_ref[2:3, :] * dxc
                        + cw_ref[3:4, :] * jnp.where(t >= 1, pltpu.roll(dxc, 1, 0), 0.0))

    sp = _rnn_specs()
    return pl.pallas_call(
        body, grid=(N_CG, NB), name="rnn_bwd",
        in_specs=[sp["seq"]] * 5 + [sp["cw"], sp["cb"], sp["w4"], sp["b4"], sp["lam"]],
        out_specs=[sp["seq"], sp["seq"], sp["cw"], sp["cb"], sp["w4"], sp["b4"], sp["lam"]],
        out_shape=[jax.ShapeDtypeStruct((R, D_RNN), F32), jax.ShapeDtypeStruct((R, D_RNN), F32),
                   jax.ShapeDtypeStruct((CONV_W, D_RNN), F32), jax.ShapeDtypeStruct((1, D_RNN), F32),
                   jax.ShapeDtypeStruct((N_CG, CG, 4 * CG), F32), jax.ShapeDtypeStruct((N_CG, 1, 4 * CG), F32),
                   jax.ShapeDtypeStruct((N_CG, 1, 2 * CG), F32)],
        scratch_shapes=[pltpu.VMEM((TP, CG), F32)] * 10,
        compiler_params=_params("arbitrary", "arbitrary"),
    )(dy, xr, xg, hf, hb, cw["conv_w"], cw["conv_b"], cw["w4"], cw["b4"], cw["lam"])


TD = 256
STAGE_D_VMEM = 58 * 1024 * 1024


def _stage_d(hp, o, y, tgt, cw):
    def body(hp_ref, o_ref, y_ref, tgt_ref, ga, gr, wout, ln2, wg, wu, wd,
             do_ref, dy_ref, dh1_ref, mix_ref, dh1b_ref, hn2_ref, dg_ref, du_ref, act_ref, dh2b_ref,
             loss_ref, dga_ref, dgr_ref, dln2_ref):
        i = pl.program_id(0)

        @pl.when(i == 0)
        def _():
            for r in (loss_ref, dga_ref, dgr_ref, dln2_ref):
                r[...] = jnp.zeros_like(r)

        mix_a, vjp_a = jax.vjp(lambda x, g: _rms(x, g, D_ATTN), o_ref[...], ga[...])
        mix_r, vjp_r = jax.vjp(lambda x, g: _rms(x, g, D_RNN), y_ref[...], gr[...])
        mab, mrb = mix_a.astype(BF), mix_r.astype(BF)
        mix_ref[:, :D_ATTN] = mab
        mix_ref[:, D_ATTN:] = mrb
        h1 = hp_ref[...] + _nn(mab, wout[:D_ATTN, :]) + _nn(mrb, wout[D_ATTN:, :])
        hn2, vjp_ln2 = jax.vjp(lambda x, g: _rms(x, g, D), h1, ln2[...])
        hn2b = hn2.astype(BF)
        hn2_ref[...] = hn2b
        act, vjp_act = jax.vjp(lambda g, u: jax.nn.silu(g) * u, _nt(hn2b, wg[...]), _nt(hn2b, wu[...]))
        actb = act.astype(BF)
        act_ref[...] = actb
        h2 = h1 + _nn(actb, wd[...])
        row = i * TD + lax.broadcasted_iota(jnp.int32, (TD, 1), 0)
        t = jnp.where(row >= TP, row - TP, row)
        err = jnp.where((t >= N_META) & (t < T), h2 - tgt_ref[...], 0.0)
        loss_ref[...] += jnp.sum(err * err) * (0.5 / D)
        dh2b = (err * (1.0 / D)).astype(BF)
        dh2b_ref[...] = dh2b
        dg, du = vjp_act(_nt(dh2b, wd[...]))
        dgb, dub = dg.astype(BF), du.astype(BF)
        dg_ref[...] = dgb
        du_ref[...] = dub
        dh1n, dln2 = vjp_ln2(_nn(dgb, wg[...]) + _nn(dub, wu[...]))
        dh1 = err * (1.0 / D) + dh1n
        dh1_ref[...] = dh1
        dh1b = dh1.astype(BF)
        dh1b_ref[...] = dh1b
        dmix = _nt(dh1b, wout[...])
        do, dga = vjp_a(dmix[:, :D_ATTN])
        dyr, dgr = vjp_r(dmix[:, D_ATTN:])
        do_ref[...] = do
        dy_ref[...] = dyr
        dga_ref[...] += dga
        dgr_ref[...] += dgr
        dln2_ref[...] += dln2

    rs = lambda n: _row_spec(n, TD)
    acc = lambda n: pl.BlockSpec((1, n), lambda i: (0, 0))
    return pl.pallas_call(
        body, grid=(R // TD,), name="stage_d",
        in_specs=[rs(D), rs(D_ATTN), rs(D_RNN), rs(D), _const_spec((1, D_ATTN)), _const_spec((1, D_RNN)),
                  _const_spec((D, D)), _const_spec((1, D)), _const_spec((D_FF, D)), _const_spec((D_FF, D)),
                  _const_spec((D_FF, D))],
        out_specs=[rs(D_ATTN), rs(D_RNN), rs(D), rs(D), rs(D), rs(D), rs(D_FF), rs(D_FF), rs(D_FF), rs(D),
                   acc(1), acc(D_ATTN), acc(D_RNN), acc(D)],
        out_shape=[jax.ShapeDtypeStruct((R, D_ATTN), F32), jax.ShapeDtypeStruct((R, D_RNN), F32),
                   jax.ShapeDtypeStruct((R, D), F32), jax.ShapeDtypeStruct((R, D), BF),
                   jax.ShapeDtypeStruct((R, D), BF), jax.ShapeDtypeStruct((R, D), BF),
                   jax.ShapeDtypeStruct((R, D_FF), BF), jax.ShapeDtypeStruct((R, D_FF), BF),
                   jax.ShapeDtypeStruct((R, D_FF), BF), jax.ShapeDtypeStruct((R, D), BF),
                   jax.ShapeDtypeStruct((1, 1), F32), jax.ShapeDtypeStruct((1, D_ATTN), F32),
                   jax.ShapeDtypeStruct((1, D_RNN), F32), jax.ShapeDtypeStruct((1, D), F32)],
        compiler_params=_params("arbitrary", vmem=STAGE_D_VMEM),
    )(hp, o, y, tgt, cw["ga"], cw["gr"], cw["wout"], cw["ln2_g"], cw["wg"], cw["wu"], cw["wd"])


TW = 2176


def _wgrad(a, b, name, tk=None):
    ka, nb = a.shape[1], b.shape[1]
    tk = ka if tk is None else tk

    def body(a_ref, b_ref, o_ref):
        @pl.when(pl.program_id(1) == 0)
        def _():
            o_ref[...] = jnp.zeros_like(o_ref)

        o_ref[...] += _tn(a_ref[...].astype(BF), b_ref[...].astype(BF))

    return pl.pallas_call(
        body, grid=(ka // tk, R // TW), name=name,
        in_specs=[pl.BlockSpec((TW, tk), lambda k, r: (r, k)), pl.BlockSpec((TW, nb), lambda k, r: (r, 0))],
        out_specs=pl.BlockSpec((tk, nb), lambda k, r: (k, 0)),
        out_shape=jax.ShapeDtypeStruct((ka, nb), F32),
        compiler_params=_params("arbitrary", "arbitrary"),
    )(a, b)


def _rope_tables():
    half = QK_ROPE // 2
    freqs = 1.0 / (ROPE_THETA ** (jnp.arange(half, dtype=F32) / half))
    ang = jnp.arange(TP, dtype=F32)[:, None] * freqs[None, :]
    ones = jnp.ones((TP, QK_NOPE), F32)
    zeros = jnp.zeros((TP, QK_NOPE), F32)
    pad1 = jnp.ones((TP, HP - QK_HEAD), F32)
    pad0 = jnp.zeros((TP, HP - QK_HEAD), F32)
    cs = jnp.concatenate([ones, jnp.cos(ang), jnp.cos(ang), pad1], axis=1)
    sn = jnp.concatenate([zeros, jnp.sin(ang), jnp.sin(ang), pad0], axis=1)
    return jnp.tile(cs, (NB, 1)), jnp.tile(sn, (NB, 1))


def _pad_rows(a, lo, hi):
    return jnp.pad(a, ((0, 0), (lo, hi), (0, 0)))


def _compute_weights(w):
    win_t = w["w_in_t"]
    kr = win_t[O_KR:O_KR + QK_ROPE]
    win = jnp.concatenate([win_t[:O_KR], jnp.zeros((QK_NOPE, D), F32), kr,
                           jnp.zeros((HP - QK_HEAD, D), F32), win_t[O_KR + QK_ROPE:]], axis=0)
    wq = _pad_rows(w["w_uq_t"].reshape(N_HEADS, QK_HEAD, Q_LORA), 0, HP - QK_HEAD)
    wkv = w["w_ukv_t"].reshape(N_HEADS, QK_NOPE + V_HEAD, KV_LORA)
    wk = _pad_rows(wkv[:, :QK_NOPE], 0, HP - QK_NOPE)
    wv = wkv[:, QK_NOPE:].reshape(D_ATTN, KV_LORA)
    gates = jnp.stack([w["lru_wa"][0], w["lru_wi"][0], w["lru_wa"][1], w["lru_wi"][1]])
    blk = gates.reshape(4, N_CG, 2, RNN_BW, RNN_BW)
    dense = jnp.einsum("tcaij,ab->tcaibj", blk, jnp.eye(2, dtype=F32)).reshape(4, N_CG, CG, CG)
    w4 = dense.transpose(1, 2, 0, 3).reshape(N_CG, CG, 4 * CG)
    bias = jnp.stack([w["lru_ba"][0], w["lru_bi"][0], w["lru_ba"][1], w["lru_bi"][1]])
    b4 = bias.reshape(4, N_CG, CG).transpose(1, 0, 2).reshape(N_CG, 1, 4 * CG)
    lam = w["lru_lambda"].reshape(2, N_CG, CG).transpose(1, 0, 2).reshape(N_CG, 1, 2 * CG)
    pad_g = lambda g: jnp.pad(g.reshape(1, QK_HEAD), ((0, 0), (0, HP - QK_HEAD)))
    return dict(
        ln1_g=w["ln1_g"].reshape(1, D), win=win.astype(BF), qa_g=w["q_a_norm_g"].reshape(1, Q_LORA),
        wq=wq.astype(BF).reshape(N_HEADS * HP, Q_LORA), kva_g=w["kv_a_norm_g"].reshape(1, KV_LORA),
        wk=wk.astype(BF).reshape(N_HEADS * HP, KV_LORA), wv=wv.astype(BF),
        q_g=pad_g(w["q_norm_g"]), k_g=pad_g(w["k_norm_g"]),
        conv_w=w["conv_w"].reshape(CONV_W, D_RNN), conv_b=w["conv_b"].reshape(1, D_RNN),
        w4=w4.astype(BF), b4=b4, lam=lam,
        ga=w["attn_out_g"].reshape(1, D_ATTN), gr=w["rnn_out_g"].reshape(1, D_RNN), ln2_g=w["ln2_g"].reshape(1, D),
    )


def _local_step(x, target, meta, w, late_weights, early_grads, mid_grads):
    cw = _compute_weights(w)
    cs, sn = _rope_tables()
    hp = jnp.concatenate([jnp.broadcast_to(meta[None], (NB, N_META, D)), x,
                          jnp.zeros((NB, TP - T, D), F32)], axis=1).reshape(R, D)
    tgt = _pad_rows(target, N_META, TP - T).reshape(R, D)

    pa, xr, xg, q, k, v = _stage_a_fwd(hp, cs, sn, cw)
    o, lse = _attn_fwd(q, k, v)
    y, hf, hb = _rnn_fwd(xr, xg, cw)
    late = late_weights([o, y])
    cw.update(wout=late["w_out"], wg=late["w_gate_t"], wu=late["w_up_t"], wd=late["w_down"])
    (do, dy, dh1, mixb, dh1b, hn2b, dgb, dub, actb, dh2b, loss, dga, dgr, dln2) = _stage_d(hp, o, y, tgt, cw)
    dwout = _wgrad(mixb, dh1b, "wgrad_out")
    dwg = _wgrad(dgb, hn2b, "wgrad_gate", tk=D_FF // 2)
    dwu = _wgrad(dub, hn2b, "wgrad_up", tk=D_FF // 2)
    dwd = _wgrad(actb, dh2b, "wgrad_down", tk=D_FF // 2)
    zero = early_grads(dict(w_out=dwout, w_gate=dwg, w_up=dwu, w_down=dwd))
    cw["conv_b"] = cw["conv_b"] + zero
    dxr, dxg, dcw, dcb, dw4, db4, dlam = _rnn_bwd(dy, xr, xg, hf, hb, cw)
    zero = mid_grads([dxr])
    dq, dk, dv = _attn_bwd(q, k, v, o, lse, do)
    (dhp, dpb, dqrawb, dkrawb, hn1b, cqnb, ckvnb, dln1, dqag, dkvag, dqg, dkg) = _stage_a_bwd(
        dq, dk, dv, dxr, dxg, dh1, hp, pa, cs, sn, dict(cw, qa_g=cw["qa_g"] + zero))

    dwin = _wgrad(dpb, hn1b, "wgrad_in", tk=PC // 2)
    dwq = _wgrad(dqrawb, cqnb, "wgrad_uq")
    dwk = _wgrad(dkrawb, ckvnb, "wgrad_uk")
    dwv = _wgrad(dv, ckvnb, "wgrad_uv")

    dwin_t = jnp.concatenate([dwin[:O_KR], dwin[O_KR + QK_NOPE:O_KR + QK_HEAD], dwin[O_XR:]], axis=0)
    dwq_t = dwq.reshape(N_HEADS, HP, Q_LORA)[:, :QK_HEAD].reshape(N_HEADS * QK_HEAD, Q_LORA)
    dwkv_t = jnp.concatenate([dwk.reshape(N_HEADS, HP, KV_LORA)[:, :QK_NOPE],
                              dwv.reshape(N_HEADS, V_HEAD, KV_LORA)], axis=1).reshape(2 * D_ATTN, KV_LORA)
    d4 = dw4.reshape(N_CG, 2, RNN_BW, 4, 2, RNN_BW)
    dgates = jnp.stack([d4[:, 0, :, :, 0, :], d4[:, 1, :, :, 1, :]], axis=1)
    dgates = dgates.transpose(3, 0, 1, 2, 4).reshape(4, N_HEADS, RNN_BW, RNN_BW)
    dbias = db4.reshape(N_CG, 4, CG).transpose(1, 0, 2).reshape(4, D_RNN)
    dhp3 = dhp.reshape(NB, TP, D)
    grads = dict(
        meta_tokens=jnp.sum(dhp3[:, :N_META], axis=0),
        ln1_g=dln1, w_in_t=dwin_t, q_a_norm_g=dqag, w_uq_t=dwq_t, kv_a_norm_g=dkvag, w_ukv_t=dwkv_t,
        q_norm_g=dqg[:, :QK_HEAD], k_norm_g=dkg[:, :QK_HEAD], conv_w=dcw[None], conv_b=dcb,
        lru_wa=jnp.stack([dgates[0], dgates[2]])[None], lru_ba=jnp.stack([dbias[0], dbias[2]])[None],
        lru_wi=jnp.stack([dgates[1], dgates[3]])[None], lru_bi=jnp.stack([dbias[1], dbias[3]])[None],
        lru_lambda=dlam.reshape(N_CG, 2, CG).transpose(1, 0, 2).reshape(1, 2, D_RNN),
        attn_out_g=dga, rnn_out_g=dgr, ln2_g=dln2,
    )
    return loss[0, 0], dhp3[:, N_META:T], grads, [dhp, dwin]


_ANY = pl.BlockSpec(memory_space=pl.ANY)


def _place():
    return lax.axis_index("x"), lax.axis_index("y"), lax.axis_index("c")


def _other_chips(x, y):
    return [(1 - x, y), (x, 1 - y), (1 - x, 1 - y)]


def _all_gather(arrs, name):
    n_arr = len(arrs)

    def body(*refs):
        x_refs, out_refs, zero_ref = refs[:n_arr], refs[n_arr:2 * n_arr], refs[2 * n_arr]
        send_sems, recv_sems, local_sems = refs[2 * n_arr + 1:]
        x, y, c = _place()
        me, sibling = (x, y, c), (x, y, 1 - c)
        chips = _other_chips(x, y)
        zero_ref[...] = jnp.zeros_like(zero_ref)

        def rows(a, px, py, pc):
            m = arrs[a].shape[0]
            return out_refs[a].at[pl.ds((4 * px + 2 * py + pc) * m, m), :]

        def copy(a, k, block, to, src=None):
            return pltpu.make_async_remote_copy(
                src_ref=rows(a, *block) if src is None else src, dst_ref=rows(a, *block),
                send_sem=send_sems.at[7 * a + k], recv_sem=recv_sems.at[7 * a + k], device_id=to, device_id_type=MESH)

        mine = [pltpu.make_async_copy(x_refs[a], rows(a, *me), local_sems.at[a]) for a in range(n_arr)]
        first, passed = [], []
        for a in range(n_arr):
            first.append(copy(a, 0, me, sibling, src=x_refs[a]))
            first += [copy(a, 1 + j, me, (*chip, c), src=x_refs[a]) for j, chip in enumerate(chips)]
        for cp in mine + first:
            cp.start()
        for a in range(n_arr):
            for j, chip in enumerate(chips):
                copy(a, 1 + j, (*chip, c), me).wait_recv()
                passed.append(copy(a, 4 + j, (*chip, c), sibling))
                passed[-1].start()
        for a in range(n_arr):
            copy(a, 0, sibling, me).wait_recv()
            for j, chip in enumerate(chips):
                copy(a, 4 + j, (*chip, 1 - c), me).wait_recv()
        for cp in first + passed:
            cp.wait_send()
        for cp in mine:
            cp.wait()

    outs = pl.pallas_call(
        body, name=name,
        out_shape=[jax.ShapeDtypeStruct((8 * a.shape[0], a.shape[1]), a.dtype) for a in arrs]
        + [jax.ShapeDtypeStruct((8, LANES), F32)],
        in_specs=[_ANY] * n_arr, out_specs=[_ANY] * n_arr + [pl.BlockSpec(memory_space=pltpu.VMEM)],
        scratch_shapes=[pltpu.SemaphoreType.DMA((7 * n_arr,)), pltpu.SemaphoreType.DMA((7 * n_arr,)),
                        pltpu.SemaphoreType.DMA((n_arr,))],
    )(*arrs)
    return outs[:n_arr], outs[n_arr]


def _pair_exchange(big, whole, name):
    n_s, _, m, n = big.shape
    n_copies = n_s + len(whole)

    def body(*refs):
        big_ref, whole_refs = refs[0], refs[1:1 + len(whole)]
        rbig_ref, rwhole_refs = refs[1 + len(whole)], refs[2 + len(whole):2 + 2 * len(whole)]
        send_sems, recv_sems = refs[-2:]
        x, y, c = _place()
        sibling = (x, y, 1 - c)
        copies = [pltpu.make_async_remote_copy(
            src_ref=big_ref.at[s, 1 - c], dst_ref=rbig_ref.at[s], send_sem=send_sems.at[s], recv_sem=recv_sems.at[s],
            device_id=sibling, device_id_type=MESH) for s in range(n_s)]
        copies += [pltpu.make_async_remote_copy(
            src_ref=a, dst_ref=r, send_sem=send_sems.at[n_s + i], recv_sem=recv_sems.at[n_s + i],
            device_id=sibling, device_id_type=MESH) for i, (a, r) in enumerate(zip(whole_refs, rwhole_refs))]
        for cp in copies:
            cp.start()
        for cp in copies:
            cp.wait()

    return pl.pallas_call(
        body, name=name,
        out_shape=[jax.ShapeDtypeStruct((n_s, m, n), big.dtype)] + [jax.ShapeDtypeStruct(a.shape, a.dtype) for a in whole],
        in_specs=[_ANY] * (1 + len(whole)), out_specs=[_ANY] * (1 + len(whole)),
        scratch_shapes=[pltpu.SemaphoreType.DMA((n_copies,)), pltpu.SemaphoreType.DMA((n_copies,))],
    )(big, *whole)


def _chip_exchange(big, small, name):
    _, m, n = big.shape
    ms = small.shape[0]

    def body(big_ref, small_ref, rbig_ref, rsmall_ref, send_sems, recv_sems):
        x, y, c = _place()
        copies = []
        for j, (tx, ty) in enumerate(_other_chips(x, y)):
            copies.append(pltpu.make_async_remote_copy(
                src_ref=big_ref.at[2 * tx + ty], dst_ref=rbig_ref.at[j], send_sem=send_sems.at[j],
                recv_sem=recv_sems.at[j], device_id=(tx, ty, c), device_id_type=MESH))
            copies.append(pltpu.make_async_remote_copy(
                src_ref=small_ref, dst_ref=rsmall_ref.at[j], send_sem=send_sems.at[3 + j],
                recv_sem=recv_sems.at[3 + j], device_id=(tx, ty, c), device_id_type=MESH))
        for cp in copies:
            cp.start()
        for cp in copies:
            cp.wait()

    return pl.pallas_call(
        body, name=name,
        out_shape=[jax.ShapeDtypeStruct((3, m, n), big.dtype), jax.ShapeDtypeStruct((3, ms, n), small.dtype)],
        in_specs=[_ANY, _ANY], out_specs=[_ANY, _ANY],
        scratch_shapes=[pltpu.SemaphoreType.DMA((6,)), pltpu.SemaphoreType.DMA((6,))],
    )(big, small)


def _pair_swap(arrs, name):
    k = len(arrs)

    def body(*refs):
        send_sems, recv_sems = refs[-2:]
        x, y, c = _place()
        copies = [pltpu.make_async_remote_copy(
            src_ref=refs[i], dst_ref=refs[k + i], send_sem=send_sems.at[i], recv_sem=recv_sems.at[i],
            device_id=(x, y, 1 - c), device_id_type=MESH) for i in range(k)]
        for cp in copies:
            cp.start()
        for cp in copies:
            cp.wait()

    return pl.pallas_call(
        body, name=name, out_shape=[jax.ShapeDtypeStruct(a.shape, a.dtype) for a in arrs], in_specs=[_ANY] * k,
        out_specs=[_ANY] * k, scratch_shapes=[pltpu.SemaphoreType.DMA((k,)), pltpu.SemaphoreType.DMA((k,))],
    )(*arrs)


_HBM = pl.BlockSpec(memory_space=pltpu.HBM)
_SEM = pl.BlockSpec(memory_space=pltpu.SEMAPHORE)
_EFFECT = pltpu.SideEffectType.DATAFLOW_SIDE_EFFECTING


def _split_copies(src_ref, land_ref, sems, plan, sending):
    n = len(sems) // 2
    return [pltpu.make_async_remote_copy(src_ref=s, dst_ref=d, send_sem=sems[k], recv_sem=sems[n + k], device_id=to,
                                         device_id_type=MESH)
            for k, (s, d, to) in enumerate(plan(src_ref, land_ref, sending))]


def _to_chips(src_at, land_at):
    def plan(src_ref, land_ref, sending):
        x, y, c = _place()
        return [(src_at(src_ref, tx, ty), land_at(land_ref, j, *((x, y) if sending else (tx, ty)), c), (tx, ty, c))
                for j, (tx, ty) in enumerate(_other_chips(x, y))]
    return plan


def _to_sibling(src_ref, land_ref, sending):
    x, y, c = _place()
    return [(src_ref.at[s, 1 - c], land_ref.at[s], (x, y, 1 - c)) for s in range(N_CHIPS)]


def _split_start(name, src, land, plan, n):
    def body(src_ref, land_ref, *outs):
        for cp in _split_copies(src_ref, land_ref, outs[:2 * n], plan, True):
            cp.start()
        outs[2 * n + 2][...] = jnp.zeros_like(outs[2 * n + 2])

    outs = pl.pallas_call(
        body, name=name,
        out_shape=(pltpu.SemaphoreType.DMA(()),) * (2 * n) + (
            pltpu.HBM(src.shape, src.dtype), pltpu.HBM(land.shape, land.dtype), jax.ShapeDtypeStruct((8, LANES), F32)),
        in_specs=(_HBM, _HBM), out_specs=(_SEM,) * (2 * n) + (_HBM, _HBM, pl.BlockSpec(memory_space=pltpu.VMEM)),
        input_output_aliases={0: 2 * n, 1: 2 * n + 1},
        compiler_params=pltpu.CompilerParams(has_side_effects=_EFFECT),
    )(pltpu.with_memory_space_constraint(src, pltpu.HBM), pltpu.with_memory_space_constraint(land, pltpu.HBM))
    return outs[:2 * n], outs[2 * n], outs[2 * n + 1], outs[2 * n + 2]


def _split_wait(name, sems, src, land, after, plan):
    def body(src_ref, land_ref, *rest):
        for cp in _split_copies(src_ref, land_ref, rest[:len(sems)], plan, False):
            cp.wait_send()
            cp.wait_recv()

    return pl.pallas_call(
        body, name=name, out_shape=(pltpu.HBM(src.shape, src.dtype), pltpu.HBM(land.shape, land.dtype)),
        in_specs=(_HBM, _HBM) + (_SEM,) * len(sems) + (_ANY,) * len(after), out_specs=(_HBM, _HBM),
        input_output_aliases={0: 0, 1: 1}, compiler_params=pltpu.CompilerParams(has_side_effects=_EFFECT),
    )(src, land, *sems, *after)


def _gather_finish(land, pack, m):
    def body(land_ref, pack_ref, out_ref, stage, send_sems, recv_sems, load_sems, store_sems):
        x, y, c = _place()

        def rows(px, py, pc, ref=out_ref):
            return ref.at[pl.ds((4 * px + 2 * py + pc) * m, m), :]

        copies = [pltpu.make_async_remote_copy(
            src_ref=rows(tx, ty, c, land_ref), dst_ref=rows(tx, ty, c), send_sem=send_sems.at[j], recv_sem=recv_sems.at[j],
            device_id=(x, y, 1 - c), device_id_type=MESH) for j, (tx, ty) in enumerate(_other_chips(x, y))]
        loads = [pltpu.make_async_copy(pack_ref.at[pl.ds(h * m, m), :], stage.at[h], load_sems.at[h]) for h in range(2)]
        stores = [pltpu.make_async_copy(stage.at[h], rows(x, y, h), store_sems.at[h]) for h in range(2)]
        for cp in copies + loads:
            cp.start()
        for h in range(2):
            loads[h].wait()
            stores[h].start()
        for j, (tx, ty) in enumerate(_other_chips(x, y)):
            copies[j].wait_send()
            pltpu.make_async_remote_copy(
                src_ref=rows(tx, ty, 1 - c), dst_ref=rows(tx, ty, 1 - c), send_sem=send_sems.at[j],
                recv_sem=recv_sems.at[j], device_id=(x, y, 1 - c), device_id_type=MESH).wait_recv()
        for cp in stores:
            cp.wait()

    return pl.pallas_call(
        body, name="gather_late_finish", out_shape=jax.ShapeDtypeStruct(land.shape, land.dtype),
        in_specs=[_ANY, _ANY], out_specs=_ANY, input_output_aliases={0: 0},
        scratch_shapes=[pltpu.VMEM((2, m, land.shape[1]), land.dtype), pltpu.SemaphoreType.DMA((3,)),
                        pltpu.SemaphoreType.DMA((3,)), pltpu.SemaphoreType.DMA((2,)), pltpu.SemaphoreType.DMA((2,))],
    )(land, pack)


def _row_tile(rows, cap=512):
    for t in range(cap - cap % 8, 7, -8):
        if rows % t == 0:
            return t
    return rows


def _elementwise(fn, n_out, name, *arrs, out_dtype=F32):
    rows, cols = arrs[0].shape
    tr = _row_tile(rows)
    n_in = len(arrs)

    def body(*refs):
        outs = fn(*[r[...].astype(F32) for r in refs[:n_in]])
        for r, o in zip(refs[n_in:], outs):
            r[...] = o.astype(out_dtype)

    spec = pl.BlockSpec((tr, cols), lambda i: (i, 0))
    return pl.pallas_call(
        body, grid=(rows // tr,), name=name, in_specs=[spec] * n_in, out_specs=[spec] * n_out,
        out_shape=[jax.ShapeDtypeStruct((rows, cols), out_dtype)] * n_out, compiler_params=_params("arbitrary"),
    )(*arrs)


def _pair_sum(gpack, rbig, ci, name):
    n_s, _, m, n = gpack.shape
    tr = _row_tile(m)

    def body(c_ref, g_ref, r_ref, o_ref):
        o_ref[...] = (g_ref[...] + r_ref[...]).astype(BF)

    return pl.pallas_call(
        body, name=name, out_shape=jax.ShapeDtypeStruct((n_s, m, n), BF),
        grid_spec=pltpu.PrefetchScalarGridSpec(
            num_scalar_prefetch=1, grid=(n_s, m // tr),
            in_specs=[pl.BlockSpec((None, None, tr, n), lambda s, i, c: (s, c[0], i, 0)),
                      pl.BlockSpec((None, tr, n), lambda s, i, c: (s, i, 0))],
            out_specs=pl.BlockSpec((None, tr, n), lambda s, i, c: (s, i, 0))),
        compiler_params=_params("arbitrary", "arbitrary"),
    )(ci.reshape(1), gpack, rbig)


def _chip_sum(sums, landed, chip, name):
    _, m, n = sums.shape
    tr = _row_tile(m)

    def body(c_ref, own_ref, r0_ref, r1_ref, r2_ref, o_ref):
        f = lambda r: r[...].astype(F32)
        o_ref[...] = _add4(f(own_ref), f(r0_ref), f(r1_ref), f(r2_ref))[0]

    slot = lambda j: pl.BlockSpec((None, tr, n), lambda i, c: (j, i, 0))
    return pl.pallas_call(
        body, name=name, out_shape=jax.ShapeDtypeStruct((m, n), F32),
        grid_spec=pltpu.PrefetchScalarGridSpec(
            num_scalar_prefetch=1, grid=(m // tr,),
            in_specs=[pl.BlockSpec((None, tr, n), lambda i, c: (c[0], i, 0)), slot(0), slot(1), slot(2)],
            out_specs=pl.BlockSpec((tr, n), lambda i, c: (i, 0))),
        compiler_params=_params("arbitrary"),
    )(chip.reshape(1), sums, landed, landed, landed)


def _add2(a, b):
    return (a + b,)


def _add4(own, r0, r1, r2):
    return ((own + r2) + (r0 + r1),)


def _adamw_math(w, g, m, v):
    m = ADAM_B1 * m + (1.0 - ADAM_B1) * g
    v = ADAM_B2 * v + (1.0 - ADAM_B2) * (g * g)
    m_hat = m / (1.0 - ADAM_B1 ** ADAM_STEP)
    v_hat = v / (1.0 - ADAM_B2 ** ADAM_STEP)
    delta = -ADAM_LR * (m_hat / (jnp.sqrt(v_hat) + ADAM_EPS) + ADAM_WD * w)
    return delta, m, v


WEIGHTS = ["meta_tokens", "ln1_g", "w_in", "q_a_norm_g", "w_uq", "kv_a_norm_g", "w_ukv", "q_norm_g", "k_norm_g",
           "conv_w", "conv_b", "lru_wa", "lru_ba", "lru_wi", "lru_bi", "lru_lambda", "attn_out_g", "rnn_out_g",
           "w_out", "ln2_g", "w_gate", "w_up", "w_down"]
BIG = ["w_in", "w_uq", "w_ukv", "w_out", "w_gate", "w_up", "w_down"]
BIG_T = {"w_in": True, "w_uq": True, "w_ukv": True, "w_out": False, "w_gate": True, "w_up": True, "w_down": False}
BIG_ROWS = {"w_in": 424, "w_uq": 72, "w_ukv": 64, "w_out": 256, "w_gate": 704, "w_up": 704, "w_down": 704}
EARLY = ["w_in", "w_uq", "w_ukv"]
LATE = ["w_out", "w_gate", "w_up", "w_down"]
EARLY_ROWS = 576
LATE_ROWS = 2368
SMALL_SHARDED = ["meta_tokens", "conv_w", "lru_ba", "lru_bi", "lru_lambda"]
SMALL = [n for n in WEIGHTS if n not in BIG]
SMALL_PACK_ROWS = 160
SMALL_ADAM_ROWS = 144


def _offsets(names):
    off, o = {}, 0
    for n in names:
        off[n] = o
        o += BIG_ROWS[n]
    return off


def _shard_pack(names, src, rows):
    parts = [_to_pack_piece(n, src[n]) for n in names]
    used = sum(BIG_ROWS[n] for n in names)
    if rows > used:
        parts.append(jnp.zeros((rows - used, D), F32))
    return jnp.concatenate(parts, axis=0)


def _grad_pack(names, g, rows):
    parts = [g[n].reshape(N_CHIPS, BIG_ROWS[n], D) for n in names]
    used = sum(BIG_ROWS[n] for n in names)
    if rows > used:
        parts.append(jnp.zeros((N_CHIPS, rows - used, D), F32))
    return jnp.concatenate(parts, axis=1).reshape(N_CHIPS, 2, rows // 2, D)


def _both_halves(mine, other, ci):
    return jnp.where(ci == 0, jnp.concatenate([mine, other], axis=0), jnp.concatenate([other, mine], axis=0))


def _to_pack_piece(name, shard):
    a = shard[0].T if BIG_T[name] else shard[0]
    return a.reshape(BIG_ROWS[name], D)


def _from_pack_piece(name, piece, shard_shape):
    _, k, n = shard_shape
    return piece.reshape(n, k).T[None] if BIG_T[name] else piece.reshape(k, n)[None]


def _flat_pack(arrs, rows):
    flat = jnp.concatenate([a.reshape(-1) for a in arrs])
    return jnp.pad(flat, (0, rows * D - flat.shape[0])).reshape(rows, D)


def _flat_unpack(pack, shapes):
    flat, out, o = pack.reshape(-1), [], 0
    for s in shapes:
        n = math.prod(s)
        out.append(flat[o:o + n].reshape(s))
        o += n
    return out


def kernel(x, meta_tokens, ln1_g, w_in, q_a_norm_g, w_uq, kv_a_norm_g, w_ukv, q_norm_g, k_norm_g, conv_w, conv_b, lru_wa, lru_ba, lru_wi, lru_bi, lru_lambda, attn_out_g, rnn_out_g, w_out, ln2_g, w_gate, w_up, w_down, loss_target, m_meta_tokens, m_ln1_g, m_w_in, m_q_a_norm_g, m_w_uq, m_kv_a_norm_g, m_w_ukv, m_q_norm_g, m_k_norm_g, m_conv_w, m_conv_b, m_lru_wa, m_lru_ba, m_lru_wi, m_lru_bi, m_lru_lambda, m_attn_out_g, m_rnn_out_g, m_w_out, m_ln2_g, m_w_gate, m_w_up, m_w_down, v_meta_tokens, v_ln1_g, v_w_in, v_q_a_norm_g, v_w_uq, v_kv_a_norm_g, v_w_ukv, v_q_norm_g, v_k_norm_g, v_conv_w, v_conv_b, v_lru_wa, v_lru_ba, v_lru_wi, v_lru_bi, v_lru_lambda, v_attn_out_g, v_rnn_out_g, v_w_out, v_ln2_g, v_w_gate, v_w_up, v_w_down):
    wts = dict(zip(WEIGHTS, (meta_tokens, ln1_g, w_in, q_a_norm_g, w_uq, kv_a_norm_g, w_ukv, q_norm_g, k_norm_g, conv_w, conv_b, lru_wa, lru_ba, lru_wi, lru_bi, lru_lambda, attn_out_g, rnn_out_g, w_out, ln2_g, w_gate, w_up, w_down)))
    mom = dict(zip(WEIGHTS, (m_meta_tokens, m_ln1_g, m_w_in, m_q_a_norm_g, m_w_uq, m_kv_a_norm_g, m_w_ukv, m_q_norm_g, m_k_norm_g, m_conv_w, m_conv_b, m_lru_wa, m_lru_ba, m_lru_wi, m_lru_bi, m_lru_lambda, m_attn_out_g, m_rnn_out_g, m_w_out, m_ln2_g, m_w_gate, m_w_up, m_w_down)))
    var = dict(zip(WEIGHTS, (v_meta_tokens, v_ln1_g, v_w_in, v_q_a_norm_g, v_w_uq, v_kv_a_norm_g, v_w_ukv, v_q_norm_g, v_k_norm_g, v_conv_w, v_conv_b, v_lru_wa, v_lru_ba, v_lru_wi, v_lru_bi, v_lru_lambda, v_attn_out_g, v_rnn_out_g, v_w_out, v_ln2_g, v_w_gate, v_w_up, v_w_down)))
    xi, yi, ci = _place()
    chip = 2 * xi + yi
    off_e, off_l = _offsets(EARLY), _offsets(LATE)
    half_e, half_l = EARLY_ROWS // 2, LATE_ROWS // 2
    gather_plan = _to_chips(lambda ref, tx, ty: ref,
                            lambda ref, j, px, py, c: ref.at[pl.ds((4 * px + 2 * py + c) * half_l, half_l), :])
    scatter_plan = _to_chips(lambda ref, tx, ty: ref.at[2 * tx + ty], lambda ref, j, px, py, c: ref.at[j])

    pack_e = _shard_pack(EARLY, wts, EARLY_ROWS).astype(BF)
    spack = jnp.concatenate([meta_tokens[:, :LANES], meta_tokens[:, LANES:], conv_w[0], lru_ba[0], lru_bi[0],
                             lru_lambda[0], jnp.zeros((6, LANES), F32)], axis=0)
    (ge, gs), gathered = _all_gather([lax.dynamic_slice_in_dim(pack_e, ci * half_e, half_e, axis=0),
                                      lax.dynamic_slice_in_dim(spack, ci * 24, 24, axis=0)], "gather_early")
    ge = ge.reshape(N_CHIPS, EARLY_ROWS, D)
    gs = gs.reshape(N_CHIPS, 48, LANES)
    full = {n: ge[:, off_e[n]:off_e[n] + BIG_ROWS[n]] for n in EARLY}
    pack_l = (_shard_pack(LATE, wts, LATE_ROWS) + gathered[0, 0]).astype(BF)
    sems_l, src_l, land_l, tied = _split_start(
        "gather_late_start", lax.dynamic_slice_in_dim(pack_l, ci * half_l, half_l, axis=0),
        lax.empty((8 * half_l, D), BF), gather_plan, 3)

    def late_weights(after):
        _, land = _split_wait("gather_late_wait", sems_l, src_l, land_l, after, gather_plan)
        gl = _gather_finish(land, pack_l, half_l).reshape(N_CHIPS, LATE_ROWS, D)
        part = lambda n: gl[:, off_l[n]:off_l[n] + BIG_ROWS[n]].reshape(N_CHIPS * BIG_ROWS[n], D)
        return dict(w_out=part("w_out"), w_gate_t=part("w_gate"), w_up_t=part("w_up"), w_down=part("w_down"))

    pair, late = {}, {}

    def early_grads(g_late):
        gpack = _grad_pack(LATE, g_late, LATE_ROWS)
        pair["sems"], pair["src"], pair["land"], zeros = _split_start(
            "grad_pair_late_start", gpack, lax.empty((N_CHIPS, half_l, D), F32), _to_sibling, N_CHIPS)
        return zeros[0, 0]

    def mid_grads(after):
        gpack, rbig = _split_wait("grad_pair_late_wait", pair["sems"], pair["src"], pair["land"], after, _to_sibling)
        chip_big = _pair_sum(gpack, rbig, ci, "grad_pair_sum_late")
        late["sems"], late["src"], late["land"], zeros = _split_start(
            "grad_chip_late_start", chip_big, lax.empty((3, half_l, D), BF), scatter_plan, 3)
        return zeros[0, 0]

    cols = lambda a: a.transpose(1, 0, 2).reshape(a.shape[1], N_CHIPS * a.shape[2])
    meta_full = cols(jnp.concatenate([gs[:, 0:16], gs[:, 16:32]], axis=2))
    w = dict(
        w_in_t=full["w_in"].reshape(IN_COLS, D), w_uq_t=full["w_uq"].reshape(N_HEADS * QK_HEAD, Q_LORA),
        w_ukv_t=full["w_ukv"].reshape(2 * D_ATTN, KV_LORA),
        ln1_g=ln1_g, q_a_norm_g=q_a_norm_g, kv_a_norm_g=kv_a_norm_g, q_norm_g=q_norm_g, k_norm_g=k_norm_g,
        conv_w=cols(gs[:, 32:36]), conv_b=conv_b, lru_wa=lru_wa[0], lru_ba=cols(gs[:, 36:38]), lru_wi=lru_wi[0],
        lru_bi=cols(gs[:, 38:40]), lru_lambda=cols(gs[:, 40:42]), attn_out_g=attn_out_g, rnn_out_g=rnn_out_g,
        ln2_g=ln2_g,
    )

    loss_local, grad_x, g, last = _local_step(x, loss_target, meta_full + tied[0, 0], w, late_weights, early_grads,
                                              mid_grads)
    loss = lax.psum(loss_local, ("x", "y", "c"))

    gpack = _grad_pack(EARLY, {"w_in": g["w_in_t"], "w_uq": g["w_uq_t"], "w_ukv": g["w_ukv_t"]}, EARLY_ROWS)
    full_shapes = {n: wts[n].shape for n in SMALL}
    full_shapes.update(meta_tokens=(N_META, D), conv_w=(1, CONV_W, D_RNN), lru_ba=(1, 2, D_RNN), lru_bi=(1, 2, D_RNN),
                       lru_lambda=(1, 2, D_RNN))
    gsmall = _flat_pack([g[n] for n in SMALL], SMALL_PACK_ROWS)
    rbig, rsmall = _pair_exchange(gpack, [gsmall], "grad_pair_exchange")
    chip_big = _pair_sum(gpack, rbig, ci, "grad_pair_sum")
    (chip_small,) = _elementwise(_add2, 1, "grad_pair_sum_small", gsmall, rsmall)
    xbig, xsmall = _chip_exchange(chip_big, chip_small, "grad_chip_exchange")
    sum_e = _chip_sum(chip_big, xbig, chip, "grad_chip_sum")
    (small_sum,) = _elementwise(_add4, 1, "grad_chip_sum_small", chip_small, xsmall[0], xsmall[1], xsmall[2])
    src, land = _split_wait("grad_chip_late_wait", late["sems"], late["src"], late["land"], last + [sum_e], scatter_plan)
    sum_l = _chip_sum(src, land, chip, "grad_chip_sum_late")
    other_e, other_l = _pair_swap([sum_e, sum_l], "grad_pair_swap")
    gshard_e, gshard_l = _both_halves(sum_e, other_e, ci), _both_halves(sum_l, other_l, ci)

    grads = {n: _from_pack_piece(n, gshard_e[off_e[n]:off_e[n] + BIG_ROWS[n]], wts[n].shape) for n in EARLY}
    grads.update({n: _from_pack_piece(n, gshard_l[off_l[n]:off_l[n] + BIG_ROWS[n]], wts[n].shape) for n in LATE})
    small_full = dict(zip(SMALL, _flat_unpack(small_sum, [full_shapes[n] for n in SMALL])))
    for n in SMALL:
        a = small_full[n]
        if n in SMALL_SHARDED:
            width = wts[n].shape[-1]
            a = lax.dynamic_slice_in_dim(a, chip * width, width, axis=a.ndim - 1)
        grads[n] = a.reshape(wts[n].shape)

    delta, new_m, new_v = {}, {}, {}
    for n in BIG:
        two_d = lambda a: a.reshape(a.shape[-2], a.shape[-1])
        d_, m_, v_ = _elementwise(_adamw_math, 3, "adamw_" + n, two_d(wts[n]), two_d(grads[n]), two_d(mom[n]), two_d(var[n]))
        delta[n], new_m[n], new_v[n] = (a.reshape(wts[n].shape) for a in (d_, m_, v_))
    packs = [_flat_pack([src[n] for n in SMALL], SMALL_ADAM_ROWS) for src in (wts, grads, mom, var)]
    outs = _elementwise(_adamw_math, 3, "adamw_small", *packs)
    for dst, o in zip((delta, new_m, new_v), outs):
        dst.update(zip(SMALL, _flat_unpack(o, [wts[n].shape for n in SMALL])))

    return (loss, grad_x, *[grads[n] for n in WEIGHTS], *[delta[n] for n in WEIGHTS],
            *[new_m[n] for n in WEIGHTS], *[new_v[n] for n in WEIGHTS])
```

```python
import functools
import math

import jax
import jax.numpy as jnp
from jax import lax
from jax.experimental import pallas as pl
from jax.experimental.pallas import tpu as pltpu

F32 = jnp.float32
BF = jnp.bfloat16
MESH = pl.DeviceIdType.MESH

D = 1024
SEQ = 2048
N_META = 16
T = N_META + SEQ
N_HEADS = 8
QK_NOPE = 64
QK_ROPE = 32
QK_HEAD = 96
V_HEAD = 64
Q_LORA = 384
KV_LORA = 256
D_ATTN = 512
D_RNN = 512
RNN_BW = 64
CONV_W = 4
LRU_C = 8.0
ROPE_THETA = 10000.0
D_FF = 2816
EPS = 1e-6
IN_COLS = 1696
ADAM_LR, ADAM_B1, ADAM_B2, ADAM_EPS, ADAM_WD, ADAM_STEP = 0.001, 0.9, 0.999, 1e-08, 0.01, 10

LANES = 128
TP = 2176
NB = 2
R = NB * TP
TR = 256
TRF = 544
TQ = 544
HP = LANES
PC = 1792
O_CKV, O_KR, O_XR, O_XG = 384, 640, 768, 1280
CG = 128
N_CG = D_RNN // CG
VMEM_LIMIT = 56 * 1024 * 1024
N_CHIPS = 4
SCALE = QK_HEAD ** -0.5
KEY_MASK = -30000.0
LOG2_E = 1.4426950408889634
SCALE_LOG2 = SCALE * LOG2_E


def _nt(a, b):
    return lax.dot_general(a, b, (((1,), (1,)), ((), ())), preferred_element_type=F32)


def _nn(a, b):
    return jnp.dot(a, b, preferred_element_type=F32)


def _tn(a, b):
    return lax.dot_general(a, b, (((0,), (0,)), ((), ())), preferred_element_type=F32)


def _rms(x, g, n):
    ms = jnp.sum(x * x, axis=-1, keepdims=True) * (1.0 / n)
    return x * lax.rsqrt(ms + EPS) * g


def _rot_impl(x):
    lane = lax.broadcasted_iota(jnp.int32, x.shape, 1)
    left = pltpu.roll(x, HP - 16, 1)
    right = pltpu.roll(x, 16, 1)
    lo = (lane >= QK_NOPE) & (lane < QK_NOPE + 16)
    hi = (lane >= QK_NOPE + 16) & (lane < QK_HEAD)
    return jnp.where(lo, -left, jnp.where(hi, right, 0.0))


@jax.custom_vjp
def _rot(x):
    return _rot_impl(x)


def _rot_fwd(x):
    return _rot_impl(x), None


def _rot_bwd(_, g):
    return (-_rot_impl(g),)


_rot.defvjp(_rot_fwd, _rot_bwd)


def _head(x, g, cs, sn):
    n = _rms(x, g, QK_HEAD)
    return n * cs + _rot(n) * sn


def _head_bwd(x, g, cs, sn, dout):
    rs = lax.rsqrt(jnp.sum(x * x, axis=-1, keepdims=True) * (1.0 / QK_HEAD) + EPS)
    xh = x * rs
    dn = dout * cs - _rot_impl(dout * sn)
    gdn = g * dn
    t = jnp.sum(gdn * xh, axis=-1, keepdims=True) * (1.0 / QK_HEAD)
    return rs * (gdn - xh * t), jnp.sum(dn * xh, axis=0, keepdims=True)


def _const_spec(shape):
    return pl.BlockSpec(shape, lambda *_: (0,) * len(shape), pipeline_mode=pl.Buffered(1))


def _row_spec(n, tr=TR):
    return pl.BlockSpec((tr, n), lambda i: (i, 0))


def _params(*sem, vmem=VMEM_LIMIT):
    return pltpu.CompilerParams(dimension_semantics=sem, vmem_limit_bytes=vmem)


def _stage_a_fwd(hp, cs, sn, cw):
    def body(hp_ref, cs_ref, sn_ref, ln1, win, qag, wq, kvag, wk, wv, qg, kg,
             pa_ref, xr_ref, xg_ref, q_ref, k_ref, v_ref):
        hn = _rms(hp_ref[...], ln1[...], D).astype(BF)
        p = _nt(hn, win[...])
        pa_ref[...] = p[:, :O_XR]
        xr_ref[...] = p[:, O_XR:O_XG]
        xg_ref[...] = p[:, O_XG:]
        cqn = _rms(p[:, :O_CKV], qag[...], Q_LORA).astype(BF)
        ckvn = _rms(p[:, O_CKV:O_KR], kvag[...], KV_LORA).astype(BF)
        kr = p[:, O_KR:O_XR]
        c, s = cs_ref[...], sn_ref[...]
        mask_lane = lax.broadcasted_iota(jnp.int32, (1, HP), 1) == QK_HEAD
        row = pl.program_id(0) * TRF + lax.broadcasted_iota(jnp.int32, (TRF, 1), 0)
        key_mask = jnp.where(jnp.where(row >= TP, row - TP, row) < T, 0.0, KEY_MASK)
        qraw = _nt(cqn, wq[...])
        kraw = _nt(ckvn, wk[...])
        for h in range(N_HEADS):
            sl = slice(h * HP, (h + 1) * HP)
            q_ref[:, sl] = jnp.where(mask_lane, 1.0, _head(qraw[:, sl], qg[...], c, s)).astype(BF)
            k_ref[:, sl] = jnp.where(mask_lane, key_mask, _head(kraw[:, sl] + kr, kg[...], c, s)).astype(BF)
        v_ref[...] = _nt(ckvn, wv[...]).astype(BF)

    rs = lambda n: _row_spec(n, TRF)
    return pl.pallas_call(
        body, grid=(R // TRF,), name="stage_a_fwd",
        in_specs=[rs(D), rs(HP), rs(HP), _const_spec((1, D)), _const_spec((PC, D)),
                  _const_spec((1, Q_LORA)), _const_spec((N_HEADS * HP, Q_LORA)), _const_spec((1, KV_LORA)),
                  _const_spec((N_HEADS * HP, KV_LORA)), _const_spec((D_ATTN, KV_LORA)), _const_spec((1, HP)),
                  _const_spec((1, HP))],
        out_specs=[rs(O_XR), rs(D_RNN), rs(D_RNN), rs(N_HEADS * HP), rs(N_HEADS * HP), rs(D_ATTN)],
        out_shape=[jax.ShapeDtypeStruct((R, O_XR), F32), jax.ShapeDtypeStruct((R, D_RNN), F32),
                   jax.ShapeDtypeStruct((R, D_RNN), F32), jax.ShapeDtypeStruct((R, N_HEADS * HP), BF),
                   jax.ShapeDtypeStruct((R, N_HEADS * HP), BF), jax.ShapeDtypeStruct((R, D_ATTN), BF)],
        compiler_params=_params("arbitrary"),
    )(hp, cs, sn, cw["ln1_g"], cw["win"], cw["qa_g"], cw["wq"], cw["kva_g"], cw["wk"], cw["wv"], cw["q_g"], cw["k_g"])


def _stage_a_bwd(dq, dk, dv, dxr, dxg, dh1, hp, pa, cs, sn, cw):
    def body(dq_ref, dk_ref, dv_ref, dxr_ref, dxg_ref, dh1_ref, hp_ref, pa_ref, cs_ref, sn_ref,
             ln1, win, qag, wq, kvag, wk, wv, qg, kg,
             dhp_ref, dp_ref, dqraw_ref, dkraw_ref, hn_ref, cqn_ref, ckvn_ref,
             dln1_ref, dqag_ref, dkvag_ref, dqg_ref, dkg_ref):
        @pl.when(pl.program_id(0) == 0)
        def _():
            for r in (dln1_ref, dqag_ref, dkvag_ref, dqg_ref, dkg_ref):
                r[...] = jnp.zeros_like(r)

        hn, vjp_ln1 = jax.vjp(lambda h, g: _rms(h, g, D), hp_ref[...], ln1[...])
        hn_ref[...] = hn.astype(BF)
        pa_v = pa_ref[...]
        cqn, vjp_qa = jax.vjp(lambda x, g: _rms(x, g, Q_LORA), pa_v[:, :O_CKV], qag[...])
        ckvn, vjp_kva = jax.vjp(lambda x, g: _rms(x, g, KV_LORA), pa_v[:, O_CKV:O_KR], kvag[...])
        kr = pa_v[:, O_KR:O_XR]
        cqnb, ckvnb = cqn.astype(BF), ckvn.astype(BF)
        cqn_ref[...] = cqnb
        ckvn_ref[...] = ckvnb
        c, s = cs_ref[...], sn_ref[...]
        lane = lax.broadcasted_iota(jnp.int32, (1, HP), 1)
        rope_lanes = ((lane >= QK_NOPE) & (lane < QK_HEAD)).astype(F32)
        dkr = jnp.zeros((TR, HP), F32)
        dqg = jnp.zeros((1, HP), F32)
        dkg = jnp.zeros((1, HP), F32)
        qraw = _nt(cqnb, wq[...])
        kraw = _nt(ckvnb, wk[...])
        for h in range(N_HEADS):
            sl = slice(h * HP, (h + 1) * HP)
            dqraw, dg = _head_bwd(qraw[:, sl], qg[...], c, s, dq_ref[:, sl])
            dqg = dqg + dg
            dqraw_ref[:, sl] = dqraw.astype(BF)
            dkraw, dg = _head_bwd(kraw[:, sl] + kr, kg[...], c, s, dk_ref[:, sl])
            dkg = dkg + dg
            dkraw_ref[:, sl] = dkraw.astype(BF)
            dkr = dkr + dkraw * rope_lanes
        dcq, dqag = vjp_qa(_nn(dqraw_ref[...], wq[...]))
        dckv, dkvag = vjp_kva(_nn(dkraw_ref[...], wk[...]) + _nn(dv_ref[...].astype(BF), wv[...]))
        dpb = jnp.concatenate([dcq, dckv, dkr, dxr_ref[...], dxg_ref[...]], axis=1).astype(BF)
        dp_ref[...] = dpb
        dh, dln1 = vjp_ln1(_nn(dpb, win[...]))
        dhp_ref[...] = dh + dh1_ref[...]
        dln1_ref[...] += dln1
        dqag_ref[...] += dqag
        dkvag_ref[...] += dkvag
        dqg_ref[...] += dqg
        dkg_ref[...] += dkg

    acc = lambda n: pl.BlockSpec((1, n), lambda i: (0, 0))
    return pl.pallas_call(
        body, grid=(R // TR,), name="stage_a_bwd",
        in_specs=[_row_spec(N_HEADS * HP), _row_spec(N_HEADS * HP), _row_spec(D_ATTN), _row_spec(D_RNN),
                  _row_spec(D_RNN), _row_spec(D), _row_spec(D), _row_spec(O_XR), _row_spec(HP), _row_spec(HP),
                  _const_spec((1, D)), _const_spec((PC, D)), _const_spec((1, Q_LORA)),
                  _const_spec((N_HEADS * HP, Q_LORA)), _const_spec((1, KV_LORA)),
                  _const_spec((N_HEADS * HP, KV_LORA)), _const_spec((D_ATTN, KV_LORA)), _const_spec((1, HP)),
                  _const_spec((1, HP))],
        out_specs=[_row_spec(D), _row_spec(PC), _row_spec(N_HEADS * HP), _row_spec(N_HEADS * HP), _row_spec(D),
                   _row_spec(Q_LORA), _row_spec(KV_LORA), acc(D), acc(Q_LORA), acc(KV_LORA), acc(HP), acc(HP)],
        out_shape=[jax.ShapeDtypeStruct((R, D), F32), jax.ShapeDtypeStruct((R, PC), BF),
                   jax.ShapeDtypeStruct((R, N_HEADS * HP), BF), jax.ShapeDtypeStruct((R, N_HEADS * HP), BF),
                   jax.ShapeDtypeStruct((R, D), BF), jax.ShapeDtypeStruct((R, Q_LORA), BF),
                   jax.ShapeDtypeStruct((R, KV_LORA), BF), jax.ShapeDtypeStruct((1, D), F32),
                   jax.ShapeDtypeStruct((1, Q_LORA), F32), jax.ShapeDtypeStruct((1, KV_LORA), F32),
                   jax.ShapeDtypeStruct((1, HP), F32), jax.ShapeDtypeStruct((1, HP), F32)],
        compiler_params=_params("arbitrary"),
    )(dq, dk, dv, dxr, dxg, dh1, hp, pa, cs, sn, cw["ln1_g"], cw["win"], cw["qa_g"], cw["wq"], cw["kva_g"],
      cw["wk"], cw["wv"], cw["q_g"], cw["k_g"])


def _head_mask(half, dtype):
    lane = lax.broadcasted_iota(jnp.int32, (1, 2 * V_HEAD), 1)
    return ((lane >= V_HEAD) == (half == 1)).astype(dtype)


_ATTN_GRID = (NB, N_HEADS // 2, TP // TQ)
_Q_SPEC = pl.BlockSpec((TQ, 2 * HP), lambda b, j, i: (b * (TP // TQ) + i, j))
_K_SPEC = pl.BlockSpec((TP, 2 * HP), lambda b, j, i: (b, j))
_V_SPEC = pl.BlockSpec((TP, 2 * V_HEAD), lambda b, j, i: (b, j))
_O_SPEC = pl.BlockSpec((TQ, 2 * V_HEAD), lambda b, j, i: (b * (TP // TQ) + i, j))
_LSE_SPEC = pl.BlockSpec((None, TQ, 2), lambda b, j, i: (j, b * (TP // TQ) + i, 0))


def _attn_fwd(q, k, v):
    def body(q_ref, k_ref, v_ref, o_ref, lse_ref):
        v2 = v_ref[...]
        o = jnp.zeros((TQ, 2 * V_HEAD), F32)
        lse = []
        for hh in range(2):
            sl = slice(hh * HP, (hh + 1) * HP)
            raw = _nt(q_ref[:, sl], k_ref[:, sl])
            m = jnp.max(raw, axis=-1, keepdims=True)
            e = jnp.exp2((raw - m) * SCALE_LOG2)
            l = jnp.sum(e, axis=-1, keepdims=True)
            o = o + _nn(e.astype(BF), v2 * _head_mask(hh, BF)) * (1.0 / l)
            lse.append(m * SCALE_LOG2 + jnp.log(l) * LOG2_E)
        o_ref[...] = o
        lane = lax.broadcasted_iota(jnp.int32, (TQ, 2), 1)
        lse_ref[...] = jnp.where(lane == 0, lse[0], lse[1])

    return pl.pallas_call(
        body, grid=_ATTN_GRID, name="attn_fwd", in_specs=[_Q_SPEC, _K_SPEC, _V_SPEC], out_specs=[_O_SPEC, _LSE_SPEC],
        out_shape=[jax.ShapeDtypeStruct((R, D_ATTN), F32), jax.ShapeDtypeStruct((N_HEADS // 2, R, 2), F32)],
        compiler_params=_params("arbitrary", "arbitrary", "arbitrary"),
    )(q, k, v)


def _attn_bwd(q, k, v, o, lse, do):
    def body(q_ref, k_ref, v_ref, o_ref, lse_ref, do_ref, dq_ref, dk_ref, dv_ref):
        @pl.when(pl.program_id(2) == 0)
        def _():
            dk_ref[...] = jnp.zeros_like(dk_ref)
            dv_ref[...] = jnp.zeros_like(dv_ref)

        do = do_ref[...]
        dob = do.astype(BF)
        do_o = do * o_ref[...]
        v2 = v_ref[...]
        dv_sum = jnp.zeros((TP, 2 * V_HEAD), F32)
        for hh in range(2):
            sl = slice(hh * HP, (hh + 1) * HP)
            qb, kb = q_ref[:, sl], k_ref[:, sl]
            p = jnp.exp2(_nt(qb, kb) * SCALE_LOG2 - lse_ref[:, hh:hh + 1])
            dp = _nt(dob, v2 * _head_mask(hh, BF))
            delta = jnp.sum(do_o * _head_mask(hh, F32), axis=-1, keepdims=True)
            dsb = (p * (dp - delta) * SCALE).astype(BF)
            dq_ref[:, sl] = _nn(dsb, kb)
            dk_ref[:, sl] += _tn(dsb, qb)
            dv_sum = dv_sum + _tn(p.astype(BF), dob) * _head_mask(hh, F32)
        dv_ref[...] += dv_sum

    return pl.pallas_call(
        body, grid=_ATTN_GRID, name="attn_bwd", in_specs=[_Q_SPEC, _K_SPEC, _V_SPEC, _O_SPEC, _LSE_SPEC, _O_SPEC],
        out_specs=[_Q_SPEC, _K_SPEC, _V_SPEC],
        out_shape=[jax.ShapeDtypeStruct((R, N_HEADS * HP), F32), jax.ShapeDtypeStruct((R, N_HEADS * HP), F32),
                   jax.ShapeDtypeStruct((R, D_ATTN), F32)],
        compiler_params=_params("arbitrary", "arbitrary", "arbitrary"),
    )(q, k, v, o, lse, do)


def _tile_prefix(a_ref, b_ref, reverse):
    tiles = (TP // 8, 8, CG)
    r8 = lax.broadcasted_iota(jnp.int32, tiles, 1)
    a, b = a_ref[...].reshape(tiles), b_ref[...].reshape(tiles)
    for s in (1, 2, 4):
        shift = 8 - s if reverse else s
        keep = (r8 < 8 - s) if reverse else (r8 >= s)
        b = jnp.where(keep, a * pltpu.roll(b, shift, 1) + b, b)
        a = jnp.where(keep, a * pltpu.roll(a, shift, 1), a)
    a_ref[...] = a.reshape(TP, CG)
    b_ref[...] = b.reshape(TP, CG)


def _scan_pair(af_ref, bf_ref, hf_ref, ab_ref, bb_ref, hb_ref):
    _tile_prefix(af_ref, bf_ref, False)
    _tile_prefix(ab_ref, bb_ref, True)
    n_tiles = TP // 8

    def step(i, carry):
        cf, cb = carry
        rf = pl.multiple_of(i * 8, 8)
        rb = pl.multiple_of((n_tiles - 1 - i) * 8, 8)
        hf_ref[pl.ds(rf, 8), :] = bf_ref[pl.ds(rf, 8), :] + af_ref[pl.ds(rf, 8), :] * cf
        hb_ref[pl.ds(rb, 8), :] = bb_ref[pl.ds(rb, 8), :] + ab_ref[pl.ds(rb, 8), :] * cb
        cf = bf_ref[pl.ds(rf + 7, 1), :] + af_ref[pl.ds(rf + 7, 1), :] * cf
        cb = bb_ref[pl.ds(rb, 1), :] + ab_ref[pl.ds(rb, 1), :] * cb
        return cf, cb

    zero = jnp.zeros((1, CG), F32)
    lax.fori_loop(0, n_tiles, step, (zero, zero), unroll=8)


def _shifts(x):
    t = lax.broadcasted_iota(jnp.int32, x.shape, 0)
    xm2 = jnp.where(t >= 2, pltpu.roll(x, 2, 0), 0.0)
    xm1 = jnp.where(t >= 1, pltpu.roll(x, 1, 0), 0.0)
    xp1 = jnp.where(t < TP - 1, pltpu.roll(x, TP - 1, 0), 0.0)
    return xm2, xm1, xp1


def _softplus(z):
    e = jnp.exp(-jnp.abs(z))
    small = e * (1.0 - e * (0.5 - e * (1.0 / 3.0)))
    return jnp.maximum(z, 0.0) + jnp.where(e < 0.01, small, jnp.log(1.0 + e))


def _neg_expm1(x):
    series = -x * (1.0 + x * 0.5 * (1.0 + x * (1.0 / 3.0) * (1.0 + x * 0.25)))
    return jnp.where(x > -0.05, series, 1.0 - jnp.exp(x))


def _gates(row0, xc, pa_f, pi_f, pa_b, pi_b, lam_f, lam_b):
    t = row0 + lax.broadcasted_iota(jnp.int32, xc.shape, 0)
    valid = t < T
    out = []
    for pa, pi_, lam in ((pa_f, pi_f, lam_f), (pa_b, pi_b, lam_b)):
        r = jax.nn.sigmoid(pa)
        gate_i = jax.nn.sigmoid(pi_)
        log_a = -LRU_C * r * _softplus(-lam)
        a = jnp.exp(log_a)
        mult = jnp.sqrt(jnp.maximum(_neg_expm1(2.0 * log_a), 0.0))
        out += [a, jnp.where(valid, mult * (gate_i * xc), 0.0)]
    return tuple(out)


def _gates_bwd(row0, xc, pres, lams, cots):
    t = row0 + lax.broadcasted_iota(jnp.int32, xc.shape, 0)
    valid = t < T
    dxc = jnp.zeros_like(xc)
    dpres, dlams = [], []
    for d in range(2):
        pa, pi_, lam = pres[2 * d], pres[2 * d + 1], lams[d]
        da, db = cots[2 * d], jnp.where(valid, cots[2 * d + 1], 0.0)
        r = jax.nn.sigmoid(pa)
        gate_i = jax.nn.sigmoid(pi_)
        sp = _softplus(-lam)
        log_a = -LRU_C * r * sp
        a = jnp.exp(log_a)
        m2 = jnp.maximum(_neg_expm1(2.0 * log_a), 0.0)
        mult = jnp.sqrt(m2)
        dxc = dxc + db * (mult * gate_i)
        d_gate = db * (mult * xc)
        d_m2 = jnp.where(m2 > 0.0, db * (gate_i * xc) * (0.5 * lax.rsqrt(m2)), 0.0)
        d_log_a = da * a - 2.0 * d_m2 * (a * a)
        dpres += [d_log_a * (-LRU_C * sp) * (r * (1.0 - r)), d_gate * (gate_i * (1.0 - gate_i))]
        d_sp = jnp.sum(d_log_a * (-LRU_C * r), axis=0, keepdims=True)
        dlams.append(-d_sp * jax.nn.sigmoid(-lam))
    return dxc, dpres, dlams


def _rnn_specs():
    seq = pl.BlockSpec((TP, CG), lambda g, b: (b, g))
    return dict(
        seq=seq,
        cw=pl.BlockSpec((CONV_W, CG), lambda g, b: (0, g)),
        cb=pl.BlockSpec((1, CG), lambda g, b: (0, g)),
        w4=pl.BlockSpec((None, CG, 4 * CG), lambda g, b: (g, 0, 0)),
        b4=pl.BlockSpec((None, 1, 4 * CG), lambda g, b: (g, 0, 0)),
        lam=pl.BlockSpec((None, 1, 2 * CG), lambda g, b: (g, 0, 0)),
    )


def _conv(x, xm2, xm1, xp1, cw_ref, cb_ref):
    return cw_ref[0:1, :] * xm2 + cw_ref[1:2, :] * xm1 + cw_ref[2:3, :] * x + cw_ref[3:4, :] * xp1 + cb_ref[...]


TC = 128
N_TC = TP // TC


def _split4(pre):
    return pre[:, :CG], pre[:, CG:2 * CG], pre[:, 2 * CG:3 * CG], pre[:, 3 * CG:]


def _rnn_fwd(xr, xg, cw):
    def body(xr_ref, xg_ref, cw_ref, cb_ref, w4_ref, b4_ref, lam_ref, y_ref, hf_ref, hb_ref, xc_s, af, bf, ab, bb):
        x = xr_ref[...]
        xc_s[...] = _conv(x, *_shifts(x), cw_ref, cb_ref)
        lam = lam_ref[...]

        def chunk(i, _):
            rows = pl.ds(pl.multiple_of(i * TC, TC), TC)
            xc = xc_s[rows, :]
            pre = _nn(xc.astype(BF), w4_ref[...]) + b4_ref[...]
            a_f, b_f, a_b, b_b = _gates(i * TC, xc, *_split4(pre), lam[:, :CG], lam[:, CG:])
            af[rows, :] = a_f
            bf[rows, :] = b_f
            ab[rows, :] = a_b
            bb[rows, :] = b_b
            return 0

        lax.fori_loop(0, N_TC, chunk, 0)
        _scan_pair(af, bf, hf_ref, ab, bb, hb_ref)
        y_ref[...] = (hf_ref[...] + hb_ref[...]) * jax.nn.gelu(xg_ref[...])

    sp = _rnn_specs()
    return pl.pallas_call(
        body, grid=(N_CG, NB), name="rnn_fwd",
        in_specs=[sp["seq"], sp["seq"], sp["cw"], sp["cb"], sp["w4"], sp["b4"], sp["lam"]],
        out_specs=[sp["seq"]] * 3, out_shape=[jax.ShapeDtypeStruct((R, D_RNN), F32)] * 3,
        scratch_shapes=[pltpu.VMEM((TP, CG), F32)] * 5,
        compiler_params=_params("arbitrary", "arbitrary"),
    )(xr, xg, cw["conv_w"], cw["conv_b"], cw["w4"], cw["b4"], cw["lam"])


def _rnn_bwd(dy, xr, xg, hf, hb, cw):
    def body(dy_ref, xr_ref, xg_ref, hf_ref, hb_ref, cw_ref, cb_ref, w4_ref, b4_ref, lam_ref,
             dxr_ref, dxg_ref, dcw_ref, dcb_ref, dw4_ref, db4_ref, dlam_ref,
             xc_s, af_s, ab_s, dhs_s, dhs2_s, lf_s, lb_s, daf_s, dab_s, dxc_s):
        @pl.when(pl.program_id(1) == 0)
        def _():
            for r in (dcw_ref, dcb_ref, dw4_ref, db4_ref, dlam_ref):
                r[...] = jnp.zeros_like(r)

        x = xr_ref[...]
        xc_s[...] = _conv(x, *_shifts(x), cw_ref, cb_ref)
        lam = lam_ref[...]

        def chunk1(i, _):
            rows = pl.ds(pl.multiple_of(i * TC, TC), TC)
            xc = xc_s[rows, :]
            pre = _nn(xc.astype(BF), w4_ref[...]) + b4_ref[...]
            a_f, _, a_b, _ = _gates(i * TC, xc, *_split4(pre), lam[:, :CG], lam[:, CG:])
            af_s[rows, :] = a_f
            ab_s[rows, :] = a_b
            _, vjp_y = jax.vjp(lambda h, g: h * jax.nn.gelu(g), hf_ref[rows, :] + hb_ref[rows, :], xg_ref[rows, :])
            dhs, dxg = vjp_y(dy_ref[rows, :])
            dhs_s[rows, :] = dhs
            dhs2_s[rows, :] = dhs
            dxg_ref[rows, :] = dxg
            return 0

        lax.fori_loop(0, N_TC, chunk1, 0)
        t = lax.broadcasted_iota(jnp.int32, (TP, CG), 0)
        af_s[...] = pltpu.roll(af_s[...], TP - 1, 0)
        ab_s[...] = pltpu.roll(ab_s[...], 1, 0)
        _scan_pair(ab_s, dhs_s, lb_s, af_s, dhs2_s, lf_s)
        daf_s[...] = lf_s[...] * jnp.where(t >= 1, pltpu.roll(hf_ref[...], 1, 0), 0.0)
        dab_s[...] = lb_s[...] * jnp.where(t < TP - 1, pltpu.roll(hb_ref[...], TP - 1, 0), 0.0)

        def chunk2(i, _):
            rows = pl.ds(pl.multiple_of(i * TC, TC), TC)
            xc = xc_s[rows, :]
            xcb = xc.astype(BF)
            pre = _nn(xcb, w4_ref[...]) + b4_ref[...]
            dxc, dpres, dlams = _gates_bwd(i * TC, xc, _split4(pre), (lam[:, :CG], lam[:, CG:]),
                                           (daf_s[rows, :], lf_s[rows, :], dab_s[rows, :], lb_s[rows, :]))
            dpre = jnp.concatenate(dpres, axis=1)
            dpreb = dpre.astype(BF)
            dxc_s[rows, :] = dxc + _nt(dpreb, w4_ref[...])
            dw4_ref[...] += _tn(xcb, dpreb)
            db4_ref[...] += jnp.sum(dpre, axis=0, keepdims=True)
            dlam_ref[...] += jnp.concatenate(dlams, axis=1)
            return 0

        lax.fori_loop(0, N_TC, chunk2, 0)
        dxc = dxc_s[...]
        dcb_ref[...] += jnp.sum(dxc, axis=0, keepdims=True)
        for tap, xs in enumerate(_shifts(x)[:2] + (x,) + _shifts(x)[2:]):
            dcw_ref[tap:tap + 1, :] += jnp.sum(xs * dxc, axis=0, keepdims=True)
        dxr_ref[...] = (cw_ref[0:1, :] * jnp.where(t < TP - 2, pltpu.roll(dxc, TP - 2, 0), 0.0)
                        + cw_ref[1:2, :] * jnp.where(t < TP - 1, pltpu.roll(dxc, TP - 1, 0), 0.0)
                        + cw_ref[2:3, :] * dxc
                        + cw_ref[3:4, :] * jnp.where(t >= 1, pltpu.roll(dxc, 1, 0), 0.0))

    sp = _rnn_specs()
    return pl.pallas_call(
        body, grid=(N_CG, NB), name="rnn_bwd",
        in_specs=[sp["seq"]] * 5 + [sp["cw"], sp["cb"], sp["w4"], sp["b4"], sp["lam"]],
        out_specs=[sp["seq"], sp["seq"], sp["cw"], sp["cb"], sp["w4"], sp["b4"], sp["lam"]],
        out_shape=[jax.ShapeDtypeStruct((R, D_RNN), F32), jax.ShapeDtypeStruct((R, D_RNN), F32),
                   jax.ShapeDtypeStruct((CONV_W, D_RNN), F32), jax.ShapeDtypeStruct((1, D_RNN), F32),
                   jax.ShapeDtypeStruct((N_CG, CG, 4 * CG), F32), jax.ShapeDtypeStruct((N_CG, 1, 4 * CG), F32),
                   jax.ShapeDtypeStruct((N_CG, 1, 2 * CG), F32)],
        scratch_shapes=[pltpu.VMEM((TP, CG), F32)] * 10,
        compiler_params=_params("arbitrary", "arbitrary"),
    )(dy, xr, xg, hf, hb, cw["conv_w"], cw["conv_b"], cw["w4"], cw["b4"], cw["lam"])


TD = 256
STAGE_D_VMEM = 58 * 1024 * 1024


def _stage_d(hp, o, y, tgt, cw):
    def body(hp_ref, o_ref, y_ref, tgt_ref, ga, gr, wout, ln2, wg, wu, wd,
             do_ref, dy_ref, dh1_ref, mix_ref, dh1b_ref, hn2_ref, dg_ref, du_ref, act_ref, dh2b_ref,
             loss_ref, dga_ref, dgr_ref, dln2_ref):
        i = pl.program_id(0)

        @pl.when(i == 0)
        def _():
            for r in (loss_ref, dga_ref, dgr_ref, dln2_ref):
                r[...] = jnp.zeros_like(r)

        mix_a, vjp_a = jax.vjp(lambda x, g: _rms(x, g, D_ATTN), o_ref[...], ga[...])
        mix_r, vjp_r = jax.vjp(lambda x, g: _rms(x, g, D_RNN), y_ref[...], gr[...])
        mab, mrb = mix_a.astype(BF), mix_r.astype(BF)
        mix_ref[:, :D_ATTN] = mab
        mix_ref[:, D_ATTN:] = mrb
        h1 = hp_ref[...] + _nn(mab, wout[:D_ATTN, :]) + _nn(mrb, wout[D_ATTN:, :])
        hn2, vjp_ln2 = jax.vjp(lambda x, g: _rms(x, g, D), h1, ln2[...])
        hn2b = hn2.astype(BF)
        hn2_ref[...] = hn2b
        act, vjp_act = jax.vjp(lambda g, u: jax.nn.silu(g) * u, _nt(hn2b, wg[...]), _nt(hn2b, wu[...]))
        actb = act.astype(BF)
        act_ref[...] = actb
        h2 = h1 + _nn(actb, wd[...])
        row = i * TD + lax.broadcasted_iota(jnp.int32, (TD, 1), 0)
        t = jnp.where(row >= TP, row - TP, row)
        err = jnp.where((t >= N_META) & (t < T), h2 - tgt_ref[...], 0.0)
        loss_ref[...] += jnp.sum(err * err) * (0.5 / D)
        dh2b = (err * (1.0 / D)).astype(BF)
        dh2b_ref[...] = dh2b
        dg, du = vjp_act(_nt(dh2b, wd[...]))
        dgb, dub = dg.astype(BF), du.astype(BF)
        dg_ref[...] = dgb
        du_ref[...] = dub
        dh1n, dln2 = vjp_ln2(_nn(dgb, wg[...]) + _nn(dub, wu[...]))
        dh1 = err * (1.0 / D) + dh1n
        dh1_ref[...] = dh1
        dh1b = dh1.astype(BF)
        dh1b_ref[...] = dh1b
        dmix = _nt(dh1b, wout[...])
        do, dga = vjp_a(dmix[:, :D_ATTN])
        dyr, dgr = vjp_r(dmix[:, D_ATTN:])
        do_ref[...] = do
        dy_ref[...] = dyr
        dga_ref[...] += dga
        dgr_ref[...] += dgr
        dln2_ref[...] += dln2

    rs = lambda n: _row_spec(n, TD)
    acc = lambda n: pl.BlockSpec((1, n), lambda i: (0, 0))
    return pl.pallas_call(
        body, grid=(R // TD,), name="stage_d",
        in_specs=[rs(D), rs(D_ATTN), rs(D_RNN), rs(D), _const_spec((1, D_ATTN)), _const_spec((1, D_RNN)),
                  _const_spec((D, D)), _const_spec((1, D)), _const_spec((D_FF, D)), _const_spec((D_FF, D)),
                  _const_spec((D_FF, D))],
        out_specs=[rs(D_ATTN), rs(D_RNN), rs(D), rs(D), rs(D), rs(D), rs(D_FF), rs(D_FF), rs(D_FF), rs(D),
                   acc(1), acc(D_ATTN), acc(D_RNN), acc(D)],
        out_shape=[jax.ShapeDtypeStruct((R, D_ATTN), F32), jax.ShapeDtypeStruct((R, D_RNN), F32),
                   jax.ShapeDtypeStruct((R, D), F32), jax.ShapeDtypeStruct((R, D), BF),
                   jax.ShapeDtypeStruct((R, D), BF), jax.ShapeDtypeStruct((R, D), BF),
                   jax.ShapeDtypeStruct((R, D_FF), BF), jax.ShapeDtypeStruct((R, D_FF), BF),
                   jax.ShapeDtypeStruct((R, D_FF), BF), jax.ShapeDtypeStruct((R, D), BF),
                   jax.ShapeDtypeStruct((1, 1), F32), jax.ShapeDtypeStruct((1, D_ATTN), F32),
                   jax.ShapeDtypeStruct((1, D_RNN), F32), jax.ShapeDtypeStruct((1, D), F32)],
        compiler_params=_params("arbitrary", vmem=STAGE_D_VMEM),
    )(hp, o, y, tgt, cw["ga"], cw["gr"], cw["wout"], cw["ln2_g"], cw["wg"], cw["wu"], cw["wd"])


TW = 2176


def _wgrad(a, b, name, tk=None):
    ka, nb = a.shape[1], b.shape[1]
    tk = ka if tk is None else tk

    def body(a_ref, b_ref, o_ref):
        @pl.when(pl.program_id(1) == 0)
        def _():
            o_ref[...] = jnp.zeros_like(o_ref)

        o_ref[...] += _tn(a_ref[...].astype(BF), b_ref[...].astype(BF))

    return pl.pallas_call(
        body, grid=(ka // tk, R // TW), name=name,
        in_specs=[pl.BlockSpec((TW, tk), lambda k, r: (r, k)), pl.BlockSpec((TW, nb), lambda k, r: (r, 0))],
        out_specs=pl.BlockSpec((tk, nb), lambda k, r: (k, 0)),
        out_shape=jax.ShapeDtypeStruct((ka, nb), F32),
        compiler_params=_params("arbitrary", "arbitrary"),
    )(a, b)


def _rope_tables():
    half = QK_ROPE // 2
    freqs = 1.0 / (ROPE_THETA ** (jnp.arange(half, dtype=F32) / half))
    ang = jnp.arange(TP, dtype=F32)[:, None] * freqs[None, :]
    ones = jnp.ones((TP, QK_NOPE), F32)
    zeros = jnp.zeros((TP, QK_NOPE), F32)
    pad1 = jnp.ones((TP, HP - QK_HEAD), F32)
    pad0 = jnp.zeros((TP, HP - QK_HEAD), F32)
    cs = jnp.concatenate([ones, jnp.cos(ang), jnp.cos(ang), pad1], axis=1)
    sn = jnp.concatenate([zeros, jnp.sin(ang), jnp.sin(ang), pad0], axis=1)
    return jnp.tile(cs, (NB, 1)), jnp.tile(sn, (NB, 1))


def _pad_rows(a, lo, hi):
    return jnp.pad(a, ((0, 0), (lo, hi), (0, 0)))


def _compute_weights(w):
    win_t = w["w_in_t"]
    kr = win_t[O_KR:O_KR + QK_ROPE]
    win = jnp.concatenate([win_t[:O_KR], jnp.zeros((QK_NOPE, D), F32), kr,
                           jnp.zeros((HP - QK_HEAD, D), F32), win_t[O_KR + QK_ROPE:]], axis=0)
    wq = _pad_rows(w["w_uq_t"].reshape(N_HEADS, QK_HEAD, Q_LORA), 0, HP - QK_HEAD)
    wkv = w["w_ukv_t"].reshape(N_HEADS, QK_NOPE + V_HEAD, KV_LORA)
    wk = _pad_rows(wkv[:, :QK_NOPE], 0, HP - QK_NOPE)
    wv = wkv[:, QK_NOPE:].reshape(D_ATTN, KV_LORA)
    gates = jnp.stack([w["lru_wa"][0], w["lru_wi"][0], w["lru_wa"][1], w["lru_wi"][1]])
    blk = gates.reshape(4, N_CG, 2, RNN_BW, RNN_BW)
    dense = jnp.einsum("tcaij,ab->tcaibj", blk, jnp.eye(2, dtype=F32)).reshape(4, N_CG, CG, CG)
    w4 = dense.transpose(1, 2, 0, 3).reshape(N_CG, CG, 4 * CG)
    bias = jnp.stack([w["lru_ba"][0], w["lru_bi"][0], w["lru_ba"][1], w["lru_bi"][1]])
    b4 = bias.reshape(4, N_CG, CG).transpose(1, 0, 2).reshape(N_CG, 1, 4 * CG)
    lam = w["lru_lambda"].reshape(2, N_CG, CG).transpose(1, 0, 2).reshape(N_CG, 1, 2 * CG)
    pad_g = lambda g: jnp.pad(g.reshape(1, QK_HEAD), ((0, 0), (0, HP - QK_HEAD)))
    return dict(
        ln1_g=w["ln1_g"].reshape(1, D), win=win.astype(BF), qa_g=w["q_a_norm_g"].reshape(1, Q_LORA),
        wq=wq.astype(BF).reshape(N_HEADS * HP, Q_LORA), kva_g=w["kv_a_norm_g"].reshape(1, KV_LORA),
        wk=wk.astype(BF).reshape(N_HEADS * HP, KV_LORA), wv=wv.astype(BF),
        q_g=pad_g(w["q_norm_g"]), k_g=pad_g(w["k_norm_g"]),
        conv_w=w["conv_w"].reshape(CONV_W, D_RNN), conv_b=w["conv_b"].reshape(1, D_RNN),
        w4=w4.astype(BF), b4=b4, lam=lam,
        ga=w["attn_out_g"].reshape(1, D_ATTN), gr=w["rnn_out_g"].reshape(1, D_RNN), ln2_g=w["ln2_g"].reshape(1, D),
    )


def _local_step(x, target, meta, w, late_weights, early_grads, mid_grads):
    cw = _compute_weights(w)
    cs, sn = _rope_tables()
    hp = jnp.concatenate([jnp.broadcast_to(meta[None], (NB, N_META, D)), x,
                          jnp.zeros((NB, TP - T, D), F32)], axis=1).reshape(R, D)
    tgt = _pad_rows(target, N_META, TP - T).reshape(R, D)

    pa, xr, xg, q, k, v = _stage_a_fwd(hp, cs, sn, cw)
    o, lse = _attn_fwd(q, k, v)
    y, hf, hb = _rnn_fwd(xr, xg, cw)
    late = late_weights([o, y])
    cw.update(wout=late["w_out"], wg=late["w_gate_t"], wu=late["w_up_t"], wd=late["w_down"])
    (do, dy, dh1, mixb, dh1b, hn2b, dgb, dub, actb, dh2b, loss, dga, dgr, dln2) = _stage_d(hp, o, y, tgt, cw)
    dwout = _wgrad(mixb, dh1b, "wgrad_out")
    dwg = _wgrad(dgb, hn2b, "wgrad_gate", tk=D_FF // 2)
    dwu = _wgrad(dub, hn2b, "wgrad_up", tk=D_FF // 2)
    dwd = _wgrad(actb, dh2b, "wgrad_down", tk=D_FF // 2)
    zero = early_grads(dict(w_out=dwout, w_gate=dwg, w_up=dwu, w_down=dwd))
    cw["conv_b"] = cw["conv_b"] + zero
    dxr, dxg, dcw, dcb, dw4, db4, dlam = _rnn_bwd(dy, xr, xg, hf, hb, cw)
    zero = mid_grads([dxr])
    dq, dk, dv = _attn_bwd(q, k, v, o, lse, do)
    (dhp, dpb, dqrawb, dkrawb, hn1b, cqnb, ckvnb, dln1, dqag, dkvag, dqg, dkg) = _stage_a_bwd(
        dq, dk, dv, dxr, dxg, dh1, hp, pa, cs, sn, dict(cw, qa_g=cw["qa_g"] + zero))

    dwin = _wgrad(dpb, hn1b, "wgrad_in", tk=PC // 2)
    dwq = _wgrad(dqrawb, cqnb, "wgrad_uq")
    dwk = _wgrad(dkrawb, ckvnb, "wgrad_uk")
    dwv = _wgrad(dv, ckvnb, "wgrad_uv")

    dwin_t = jnp.concatenate([dwin[:O_KR], dwin[O_KR + QK_NOPE:O_KR + QK_HEAD], dwin[O_XR:]], axis=0)
    dwq_t = dwq.reshape(N_HEADS, HP, Q_LORA)[:, :QK_HEAD].reshape(N_HEADS * QK_HEAD, Q_LORA)
    dwkv_t = jnp.concatenate([dwk.reshape(N_HEADS, HP, KV_LORA)[:, :QK_NOPE],
                              dwv.reshape(N_HEADS, V_HEAD, KV_LORA)], axis=1).reshape(2 * D_ATTN, KV_LORA)
    d4 = dw4.reshape(N_CG, 2, RNN_BW, 4, 2, RNN_BW)
    dgates = jnp.stack([d4[:, 0, :, :, 0, :], d4[:, 1, :, :, 1, :]], axis=1)
    dgates = dgates.transpose(3, 0, 1, 2, 4).reshape(4, N_HEADS, RNN_BW, RNN_BW)
    dbias = db4.reshape(N_CG, 4, CG).transpose(1, 0, 2).reshape(4, D_RNN)
    dhp3 = dhp.reshape(NB, TP, D)
    grads = dict(
        meta_tokens=jnp.sum(dhp3[:, :N_META], axis=0),
        ln1_g=dln1, w_in_t=dwin_t, q_a_norm_g=dqag, w_uq_t=dwq_t, kv_a_norm_g=dkvag, w_ukv_t=dwkv_t,
        q_norm_g=dqg[:, :QK_HEAD], k_norm_g=dkg[:, :QK_HEAD], conv_w=dcw[None], conv_b=dcb,
        lru_wa=jnp.stack([dgates[0], dgates[2]])[None], lru_ba=jnp.stack([dbias[0], dbias[2]])[None],
        lru_wi=jnp.stack([dgates[1], dgates[3]])[None], lru_bi=jnp.stack([dbias[1], dbias[3]])[None],
        lru_lambda=dlam.reshape(N_CG, 2, CG).transpose(1, 0, 2).reshape(1, 2, D_RNN),
        attn_out_g=dga, rnn_out_g=dgr, ln2_g=dln2,
    )
    return loss[0, 0], dhp3[:, N_META:T], grads, [dhp, dwin]


_ANY = pl.BlockSpec(memory_space=pl.ANY)


def _place():
    return lax.axis_index("x"), lax.axis_index("y"), lax.axis_index("c")


def _other_chips(x, y):
    return [(1 - x, y), (x, 1 - y), (1 - x, 1 - y)]


def _all_gather(arrs, name):
    n_arr = len(arrs)

    def body(*refs):
        x_refs, out_refs, zero_ref = refs[:n_arr], refs[n_arr:2 * n_arr], refs[2 * n_arr]
        send_sems, recv_sems, local_sems = refs[2 * n_arr + 1:]
        x, y, c = _place()
        me, sibling = (x, y, c), (x, y, 1 - c)
        chips = _other_chips(x, y)
        zero_ref[...] = jnp.zeros_like(zero_ref)

        def rows(a, px, py, pc):
            m = arrs[a].shape[0]
            return out_refs[a].at[pl.ds((4 * px + 2 * py + pc) * m, m), :]

        def copy(a, k, block, to, src=None):
            return pltpu.make_async_remote_copy(
                src_ref=rows(a, *block) if src is None else src, dst_ref=rows(a, *block),
                send_sem=send_sems.at[7 * a + k], recv_sem=recv_sems.at[7 * a + k], device_id=to, device_id_type=MESH)

        mine = [pltpu.make_async_copy(x_refs[a], rows(a, *me), local_sems.at[a]) for a in range(n_arr)]
        first, passed = [], []
        for a in range(n_arr):
            first.append(copy(a, 0, me, sibling, src=x_refs[a]))
            first += [copy(a, 1 + j, me, (*chip, c), src=x_refs[a]) for j, chip in enumerate(chips)]
        for cp in mine + first:
            cp.start()
        for a in range(n_arr):
            for j, chip in enumerate(chips):
                copy(a, 1 + j, (*chip, c), me).wait_recv()
                passed.append(copy(a, 4 + j, (*chip, c), sibling))
                passed[-1].start()
        for a in range(n_arr):
            copy(a, 0, sibling, me).wait_recv()
            for j, chip in enumerate(chips):
                copy(a, 4 + j, (*chip, 1 - c), me).wait_recv()
        for cp in first + passed:
            cp.wait_send()
        for cp in mine:
            cp.wait()

    outs = pl.pallas_call(
        body, name=name,
        out_shape=[jax.ShapeDtypeStruct((8 * a.shape[0], a.shape[1]), a.dtype) for a in arrs]
        + [jax.ShapeDtypeStruct((8, LANES), F32)],
        in_specs=[_ANY] * n_arr, out_specs=[_ANY] * n_arr + [pl.BlockSpec(memory_space=pltpu.VMEM)],
        scratch_shapes=[pltpu.SemaphoreType.DMA((7 * n_arr,)), pltpu.SemaphoreType.DMA((7 * n_arr,)),
                        pltpu.SemaphoreType.DMA((n_arr,))],
    )(*arrs)
    return outs[:n_arr], outs[n_arr]


def _pair_exchange(big, whole, name):
    n_s, _, m, n = big.shape
    n_copies = n_s + len(whole)

    def body(*refs):
        big_ref, whole_refs = refs[0], refs[1:1 + len(whole)]
        rbig_ref, rwhole_refs = refs[1 + len(whole)], refs[2 + len(whole):2 + 2 * len(whole)]
        send_sems, recv_sems = refs[-2:]
        x, y, c = _place()
        sibling = (x, y, 1 - c)
        copies = [pltpu.make_async_remote_copy(
            src_ref=big_ref.at[s, 1 - c], dst_ref=rbig_ref.at[s], send_sem=send_sems.at[s], recv_sem=recv_sems.at[s],
            device_id=sibling, device_id_type=MESH) for s in range(n_s)]
        copies += [pltpu.make_async_remote_copy(
            src_ref=a, dst_ref=r, send_sem=send_sems.at[n_s + i], recv_sem=recv_sems.at[n_s + i],
            device_id=sibling, device_id_type=MESH) for i, (a, r) in enumerate(zip(whole_refs, rwhole_refs))]
        for cp in copies:
            cp.start()
        for cp in copies:
            cp.wait()

    return pl.pallas_call(
        body, name=name,
        out_shape=[jax.ShapeDtypeStruct((n_s, m, n), big.dtype)] + [jax.ShapeDtypeStruct(a.shape, a.dtype) for a in whole],
        in_specs=[_ANY] * (1 + len(whole)), out_specs=[_ANY] * (1 + len(whole)),
        scratch_shapes=[pltpu.SemaphoreType.DMA((n_copies,)), pltpu.SemaphoreType.DMA((n_copies,))],
    )(big, *whole)


def _chip_exchange(big, small, name):
    _, m, n = big.shape
    ms = small.shape[0]

    def body(big_ref, small_ref, rbig_ref, rsmall_ref, send_sems, recv_sems):
        x, y, c = _place()
        copies = []
        for j, (tx, ty) in enumerate(_other_chips(x, y)):
            copies.append(pltpu.make_async_remote_copy(
                src_ref=big_ref.at[2 * tx + ty], dst_ref=rbig_ref.at[j], send_sem=send_sems.at[j],
                recv_sem=recv_sems.at[j], device_id=(tx, ty, c), device_id_type=MESH))
            copies.append(pltpu.make_async_remote_copy(
                src_ref=small_ref, dst_ref=rsmall_ref.at[j], send_sem=send_sems.at[3 + j],
                recv_sem=recv_sems.at[3 + j], device_id=(tx, ty, c), device_id_type=MESH))
        for cp in copies:
            cp.start()
        for cp in copies:
            cp.wait()

    return pl.pallas_call(
        body, name=name,
        out_shape=[jax.ShapeDtypeStruct((3, m, n), big.dtype), jax.ShapeDtypeStruct((3, ms, n), small.dtype)],
        in_specs=[_ANY, _ANY], out_specs=[_ANY, _ANY],
        scratch_shapes=[pltpu.SemaphoreType.DMA((6,)), pltpu.SemaphoreType.DMA((6,))],
    )(big, small)


def _pair_swap(arrs, name):
    k = len(arrs)

    def body(*refs):
        send_sems, recv_sems = refs[-2:]
        x, y, c = _place()
        copies = [pltpu.make_async_remote_copy(
            src_ref=refs[i], dst_ref=refs[k + i], send_sem=send_sems.at[i], recv_sem=recv_sems.at[i],
            device_id=(x, y, 1 - c), device_id_type=MESH) for i in range(k)]
        for cp in copies:
            cp.start()
        for cp in copies:
            cp.wait()

    return pl.pallas_call(
        body, name=name, out_shape=[jax.ShapeDtypeStruct(a.shape, a.dtype) for a in arrs], in_specs=[_ANY] * k,
        out_specs=[_ANY] * k, scratch_shapes=[pltpu.SemaphoreType.DMA((k,)), pltpu.SemaphoreType.DMA((k,))],
    )(*arrs)


_HBM = pl.BlockSpec(memory_space=pltpu.HBM)
_SEM = pl.BlockSpec(memory_space=pltpu.SEMAPHORE)
_EFFECT = pltpu.SideEffectType.DATAFLOW_SIDE_EFFECTING


def _split_copies(src_ref, land_ref, sems, plan, sending):
    n = len(sems) // 2
    return [pltpu.make_async_remote_copy(src_ref=s, dst_ref=d, send_sem=sems[k], recv_sem=sems[n + k], device_id=to,
                                         device_id_type=MESH)
            for k, (s, d, to) in enumerate(plan(src_ref, land_ref, sending))]


def _to_chips(src_at, land_at):
    def plan(src_ref, land_ref, sending):
        x, y, c = _place()
        return [(src_at(src_ref, tx, ty), land_at(land_ref, j, *((x, y) if sending else (tx, ty)), c), (tx, ty, c))
                for j, (tx, ty) in enumerate(_other_chips(x, y))]
    return plan


def _to_sibling(src_ref, land_ref, sending):
    x, y, c = _place()
    return [(src_ref.at[s, 1 - c], land_ref.at[s], (x, y, 1 - c)) for s in range(N_CHIPS)]


def _split_start(name, src, land, plan, n):
    def body(src_ref, land_ref, *outs):
        for cp in _split_copies(src_ref, land_ref, outs[:2 * n], plan, True):
            cp.start()
        outs[2 * n + 2][...] = jnp.zeros_like(outs[2 * n + 2])

    outs = pl.pallas_call(
        body, name=name,
        out_shape=(pltpu.SemaphoreType.DMA(()),) * (2 * n) + (
            pltpu.HBM(src.shape, src.dtype), pltpu.HBM(land.shape, land.dtype), jax.ShapeDtypeStruct((8, LANES), F32)),
        in_specs=(_HBM, _HBM), out_specs=(_SEM,) * (2 * n) + (_HBM, _HBM, pl.BlockSpec(memory_space=pltpu.VMEM)),
        input_output_aliases={0: 2 * n, 1: 2 * n + 1},
        compiler_params=pltpu.CompilerParams(has_side_effects=_EFFECT),
    )(pltpu.with_memory_space_constraint(src, pltpu.HBM), pltpu.with_memory_space_constraint(land, pltpu.HBM))
    return outs[:2 * n], outs[2 * n], outs[2 * n + 1], outs[2 * n + 2]


def _split_wait(name, sems, src, land, after, plan):
    def body(src_ref, land_ref, *rest):
        for cp in _split_copies(src_ref, land_ref, rest[:len(sems)], plan, False):
            cp.wait_send()
            cp.wait_recv()

    return pl.pallas_call(
        body, name=name, out_shape=(pltpu.HBM(src.shape, src.dtype), pltpu.HBM(land.shape, land.dtype)),
        in_specs=(_HBM, _HBM) + (_SEM,) * len(sems) + (_ANY,) * len(after), out_specs=(_HBM, _HBM),
        input_output_aliases={0: 0, 1: 1}, compiler_params=pltpu.CompilerParams(has_side_effects=_EFFECT),
    )(src, land, *sems, *after)


def _gather_finish(land, pack, m):
    def body(land_ref, pack_ref, out_ref, stage, send_sems, recv_sems, load_sems, store_sems):
        x, y, c = _place()

        def rows(px, py, pc, ref=out_ref):
            return ref.at[pl.ds((4 * px + 2 * py + pc) * m, m), :]

        copies = [pltpu.make_async_remote_copy(
            src_ref=rows(tx, ty, c, land_ref), dst_ref=rows(tx, ty, c), send_sem=send_sems.at[j], recv_sem=recv_sems.at[j],
            device_id=(x, y, 1 - c), device_id_type=MESH) for j, (tx, ty) in enumerate(_other_chips(x, y))]
        loads = [pltpu.make_async_copy(pack_ref.at[pl.ds(h * m, m), :], stage.at[h], load_sems.at[h]) for h in range(2)]
        stores = [pltpu.make_async_copy(stage.at[h], rows(x, y, h), store_sems.at[h]) for h in range(2)]
        for cp in copies + loads:
            cp.start()
        for h in range(2):
            loads[h].wait()
            stores[h].start()
        for j, (tx, ty) in enumerate(_other_chips(x, y)):
            copies[j].wait_send()
            pltpu.make_async_remote_copy(
                src_ref=rows(tx, ty, 1 - c), dst_ref=rows(tx, ty, 1 - c), send_sem=send_sems.at[j],
                recv_sem=recv_sems.at[j], device_id=(x, y, 1 - c), device_id_type=MESH).wait_recv()
        for cp in stores:
            cp.wait()

    return pl.pallas_call(
        body, name="gather_late_finish", out_shape=jax.ShapeDtypeStruct(land.shape, land.dtype),
        in_specs=[_ANY, _ANY], out_specs=_ANY, input_output_aliases={0: 0},
        scratch_shapes=[pltpu.VMEM((2, m, land.shape[1]), land.dtype), pltpu.SemaphoreType.DMA((3,)),
                        pltpu.SemaphoreType.DMA((3,)), pltpu.SemaphoreType.DMA((2,)), pltpu.SemaphoreType.DMA((2,))],
    )(land, pack)


def _row_tile(rows, cap=512):
    for t in range(cap - cap % 8, 7, -8):
        if rows % t == 0:
            return t
    return rows


def _elementwise(fn, n_out, name, *arrs, out_dtype=F32):
    rows, cols = arrs[0].shape
    tr = _row_tile(rows)
    n_in = len(arrs)

    def body(*refs):
        outs = fn(*[r[...].astype(F32) for r in refs[:n_in]])
        for r, o in zip(refs[n_in:], outs):
            r[...] = o.astype(out_dtype)

    spec = pl.BlockSpec((tr, cols), lambda i: (i, 0))
    return pl.pallas_call(
        body, grid=(rows // tr,), name=name, in_specs=[spec] * n_in, out_specs=[spec] * n_out,
        out_shape=[jax.ShapeDtypeStruct((rows, cols), out_dtype)] * n_out, compiler_params=_params("arbitrary"),
    )(*arrs)


def _pair_sum(gpack, rbig, ci, name):
    n_s, _, m, n = gpack.shape
    tr = _row_tile(m)

    def body(c_ref, g_ref, r_ref, o_ref):
        o_ref[...] = (g_ref[...] + r_ref[...]).astype(BF)

    return pl.pallas_call(
        body, name=name, out_shape=jax.ShapeDtypeStruct((n_s, m, n), BF),
        grid_spec=pltpu.PrefetchScalarGridSpec(
            num_scalar_prefetch=1, grid=(n_s, m // tr),
            in_specs=[pl.BlockSpec((None, None, tr, n), lambda s, i, c: (s, c[0], i, 0)),
                      pl.BlockSpec((None, tr, n), lambda s, i, c: (s, i, 0))],
            out_specs=pl.BlockSpec((None, tr, n), lambda s, i, c: (s, i, 0))),
        compiler_params=_params("arbitrary", "arbitrary"),
    )(ci.reshape(1), gpack, rbig)


def _chip_sum(sums, landed, chip, name):
    _, m, n = sums.shape
    tr = _row_tile(m)

    def body(c_ref, own_ref, r0_ref, r1_ref, r2_ref, o_ref):
        f = lambda r: r[...].astype(F32)
        o_ref[...] = _add4(f(own_ref), f(r0_ref), f(r1_ref), f(r2_ref))[0]

    slot = lambda j: pl.BlockSpec((None, tr, n), lambda i, c: (j, i, 0))
    return pl.pallas_call(
        body, name=name, out_shape=jax.ShapeDtypeStruct((m, n), F32),
        grid_spec=pltpu.PrefetchScalarGridSpec(
            num_scalar_prefetch=1, grid=(m // tr,),
            in_specs=[pl.BlockSpec((None, tr, n), lambda i, c: (c[0], i, 0)), slot(0), slot(1), slot(2)],
            out_specs=pl.BlockSpec((tr, n), lambda i, c: (i, 0))),
        compiler_params=_params("arbitrary"),
    )(chip.reshape(1), sums, landed, landed, landed)


def _add2(a, b):
    return (a + b,)


def _add4(own, r0, r1, r2):
    return ((own + r2) + (r0 + r1),)


def _adamw_math(w, g, m, v):
    m = ADAM_B1 * m + (1.0 - ADAM_B1) * g
    v = ADAM_B2 * v + (1.0 - ADAM_B2) * (g * g)
    m_hat = m / (1.0 - ADAM_B1 ** ADAM_STEP)
    v_hat = v / (1.0 - ADAM_B2 ** ADAM_STEP)
    delta = -ADAM_LR * (m_hat / (jnp.sqrt(v_hat) + ADAM_EPS) + ADAM_WD * w)
    return delta, m, v


WEIGHTS = ["meta_tokens", "ln1_g", "w_in", "q_a_norm_g", "w_uq", "kv_a_norm_g", "w_ukv", "q_norm_g", "k_norm_g",
           "conv_w", "conv_b", "lru_wa", "lru_ba", "lru_wi", "lru_bi", "lru_lambda", "attn_out_g", "rnn_out_g",
           "w_out", "ln2_g", "w_gate", "w_up", "w_down"]
BIG = ["w_in", "w_uq", "w_ukv", "w_out", "w_gate", "w_up", "w_down"]
BIG_T = {"w_in": True, "w_uq": True, "w_ukv": True, "w_out": False, "w_gate": True, "w_up": True, "w_down": False}
BIG_ROWS = {"w_in": 424, "w_uq": 72, "w_ukv": 64, "w_out": 256, "w_gate": 704, "w_up": 704, "w_down": 704}
EARLY = ["w_in", "w_uq", "w_ukv"]
LATE = ["w_out", "w_gate", "w_up", "w_down"]
EARLY_ROWS = 576
LATE_ROWS = 2368
SMALL_SHARDED = ["meta_tokens", "conv_w", "lru_ba", "lru_bi", "lru_lambda"]
SMALL = [n for n in WEIGHTS if n not in BIG]
SMALL_PACK_ROWS = 160
SMALL_ADAM_ROWS = 144


def _offsets(names):
    off, o = {}, 0
    for n in names:
        off[n] = o
        o += BIG_ROWS[n]
    return off


def _shard_pack(names, src, rows):
    parts = [_to_pack_piece(n, src[n]) for n in names]
    used = sum(BIG_ROWS[n] for n in names)
    if rows > used:
        parts.append(jnp.zeros((rows - used, D), F32))
    return jnp.concatenate(parts, axis=0)


def _grad_pack(names, g, rows):
    parts = [g[n].reshape(N_CHIPS, BIG_ROWS[n], D) for n in names]
    used = sum(BIG_ROWS[n] for n in names)
    if rows > used:
        parts.append(jnp.zeros((N_CHIPS, rows - used, D), F32))
    return jnp.concatenate(parts, axis=1).reshape(N_CHIPS, 2, rows // 2, D)


def _both_halves(mine, other, ci):
    return jnp.where(ci == 0, jnp.concatenate([mine, other], axis=0), jnp.concatenate([other, mine], axis=0))


def _to_pack_piece(name, shard):
    a = shard[0].T if BIG_T[name] else shard[0]
    return a.reshape(BIG_ROWS[name], D)


def _flat_pack(arrs, rows):
    flat = jnp.concatenate([a.reshape(-1) for a in arrs])
    return jnp.pad(flat, (0, rows * D - flat.shape[0])).reshape(rows, D)


def _flat_unpack(pack, shapes):
    flat, out, o = pack.reshape(-1), [], 0
    for s in shapes:
        n = math.prod(s)
        out.append(flat[o:o + n].reshape(s))
        o += n
    return out


def kernel(x, meta_tokens, ln1_g, w_in, q_a_norm_g, w_uq, kv_a_norm_g, w_ukv, q_norm_g, k_norm_g, conv_w, conv_b, lru_wa, lru_ba, lru_wi, lru_bi, lru_lambda, attn_out_g, rnn_out_g, w_out, ln2_g, w_gate, w_up, w_down, loss_target, m_meta_tokens, m_ln1_g, m_w_in, m_q_a_norm_g, m_w_uq, m_kv_a_norm_g, m_w_ukv, m_q_norm_g, m_k_norm_g, m_conv_w, m_conv_b, m_lru_wa, m_lru_ba, m_lru_wi, m_lru_bi, m_lru_lambda, m_attn_out_g, m_rnn_out_g, m_w_out, m_ln2_g, m_w_gate, m_w_up, m_w_down, v_meta_tokens, v_ln1_g, v_w_in, v_q_a_norm_g, v_w_uq, v_kv_a_norm_g, v_w_ukv, v_q_norm_g, v_k_norm_g, v_conv_w, v_conv_b, v_lru_wa, v_lru_ba, v_lru_wi, v_lru_bi, v_lru_lambda, v_attn_out_g, v_rnn_out_g, v_w_out, v_ln2_g, v_w_gate, v_w_up, v_w_down):
    wts = dict(zip(WEIGHTS, (meta_tokens, ln1_g, w_in, q_a_norm_g, w_uq, kv_a_norm_g, w_ukv, q_norm_g, k_norm_g, conv_w, conv_b, lru_wa, lru_ba, lru_wi, lru_bi, lru_lambda, attn_out_g, rnn_out_g, w_out, ln2_g, w_gate, w_up, w_down)))
    mom = dict(zip(WEIGHTS, (m_meta_tokens, m_ln1_g, m_w_in, m_q_a_norm_g, m_w_uq, m_kv_a_norm_g, m_w_ukv, m_q_norm_g, m_k_norm_g, m_conv_w, m_conv_b, m_lru_wa, m_lru_ba, m_lru_wi, m_lru_bi, m_lru_lambda, m_attn_out_g, m_rnn_out_g, m_w_out, m_ln2_g, m_w_gate, m_w_up, m_w_down)))
    var = dict(zip(WEIGHTS, (v_meta_tokens, v_ln1_g, v_w_in, v_q_a_norm_g, v_w_uq, v_kv_a_norm_g, v_w_ukv, v_q_norm_g, v_k_norm_g, v_conv_w, v_conv_b, v_lru_wa, v_lru_ba, v_lru_wi, v_lru_bi, v_lru_lambda, v_attn_out_g, v_rnn_out_g, v_w_out, v_ln2_g, v_w_gate, v_w_up, v_w_down)))
    xi, yi, ci = _place()
    chip = 2 * xi + yi
    off_e, off_l = _offsets(EARLY), _offsets(LATE)
    half_e, half_l = EARLY_ROWS // 2, LATE_ROWS // 2
    gather_plan = _to_chips(lambda ref, tx, ty: ref,
                            lambda ref, j, px, py, c: ref.at[pl.ds((4 * px + 2 * py + c) * half_l, half_l), :])
    scatter_plan = _to_chips(lambda ref, tx, ty: ref.at[2 * tx + ty], lambda ref, j, px, py, c: ref.at[j])

    pack_e = _shard_pack(EARLY, wts, EARLY_ROWS).astype(BF)
    spack = jnp.concatenate([meta_tokens[:, :LANES], meta_tokens[:, LANES:], conv_w[0], lru_ba[0], lru_bi[0],
                             lru_lambda[0], jnp.zeros((6, LANES), F32)], axis=0)
    (ge, gs), gathered = _all_gather([lax.dynamic_slice_in_dim(pack_e, ci * half_e, half_e, axis=0),
                                      lax.dynamic_slice_in_dim(spack, ci * 24, 24, axis=0)], "gather_early")
    ge = ge.reshape(N_CHIPS, EARLY_ROWS, D)
    gs = gs.reshape(N_CHIPS, 48, LANES)
    full = {n: ge[:, off_e[n]:off_e[n] + BIG_ROWS[n]] for n in EARLY}
    pack_l = (_shard_pack(LATE, wts, LATE_ROWS) + gathered[0, 0]).astype(BF)
    sems_l, src_l, land_l, tied = _split_start(
        "gather_late_start", lax.dynamic_slice_in_dim(pack_l, ci * half_l, half_l, axis=0),
        lax.empty((8 * half_l, D), BF), gather_plan, 3)

    def late_weights(after):
        _, land = _split_wait("gather_late_wait", sems_l, src_l, land_l, after, gather_plan)
        gl = _gather_finish(land, pack_l, half_l).reshape(N_CHIPS, LATE_ROWS, D)
        part = lambda n: gl[:, off_l[n]:off_l[n] + BIG_ROWS[n]].reshape(N_CHIPS * BIG_ROWS[n], D)
        return dict(w_out=part("w_out"), w_gate_t=part("w_gate"), w_up_t=part("w_up"), w_down=part("w_down"))

    pair, late = {}, {}

    def early_grads(g_late):
        gpack = _grad_pack(LATE, g_late, LATE_ROWS)
        pair["sems"], pair["src"], pair["land"], zeros = _split_start(
            "grad_pair_late_start", gpack, lax.empty((N_CHIPS, half_l, D), F32), _to_sibling, N_CHIPS)
        return zeros[0, 0]

    def mid_grads(after):
        gpack, rbig = _split_wait("grad_pair_late_wait", pair["sems"], pair["src"], pair["land"], after, _to_sibling)
        chip_big = _pair_sum(gpack, rbig, ci, "grad_pair_sum_late")
        late["sems"], late["src"], late["land"], zeros = _split_start(
            "grad_chip_late_start", chip_big, lax.empty((3, half_l, D), BF), scatter_plan, 3)
        return zeros[0, 0]

    cols = lambda a: a.transpose(1, 0, 2).reshape(a.shape[1], N_CHIPS * a.shape[2])
    meta_full = cols(jnp.concatenate([gs[:, 0:16], gs[:, 16:32]], axis=2))
    w = dict(
        w_in_t=full["w_in"].reshape(IN_COLS, D), w_uq_t=full["w_uq"].reshape(N_HEADS * QK_HEAD, Q_LORA),
        w_ukv_t=full["w_ukv"].reshape(2 * D_ATTN, KV_LORA),
        ln1_g=ln1_g, q_a_norm_g=q_a_norm_g, kv_a_norm_g=kv_a_norm_g, q_norm_g=q_norm_g, k_norm_g=k_norm_g,
        conv_w=cols(gs[:, 32:36]), conv_b=conv_b, lru_wa=lru_wa[0], lru_ba=cols(gs[:, 36:38]), lru_wi=lru_wi[0],
        lru_bi=cols(gs[:, 38:40]), lru_lambda=cols(gs[:, 40:42]), attn_out_g=attn_out_g, rnn_out_g=rnn_out_g,
        ln2_g=ln2_g,
    )

    loss_local, grad_x, g, last = _local_step(x, loss_target, meta_full + tied[0, 0], w, late_weights, early_grads,
                                              mid_grads)
    loss = lax.psum(loss_local, ("x", "y", "c"))

    gpack = _grad_pack(EARLY, {"w_in": g["w_in_t"], "w_uq": g["w_uq_t"], "w_ukv": g["w_ukv_t"]}, EARLY_ROWS)
    full_shapes = {n: wts[n].shape for n in SMALL}
    full_shapes.update(meta_tokens=(N_META, D), conv_w=(1, CONV_W, D_RNN), lru_ba=(1, 2, D_RNN), lru_bi=(1, 2, D_RNN),
                       lru_lambda=(1, 2, D_RNN))
    gsmall = _flat_pack([g[n] for n in SMALL], SMALL_PACK_ROWS)
    rbig, rsmall = _pair_exchange(gpack, [gsmall], "grad_pair_exchange")
    chip_big = _pair_sum(gpack, rbig, ci, "grad_pair_sum")
    (chip_small,) = _elementwise(_add2, 1, "grad_pair_sum_small", gsmall, rsmall)
    xbig, xsmall = _chip_exchange(chip_big, chip_small, "grad_chip_exchange")
    sum_e = _chip_sum(chip_big, xbig, chip, "grad_chip_sum")
    (small_sum,) = _elementwise(_add4, 1, "grad_chip_sum_small", chip_small, xsmall[0], xsmall[1], xsmall[2])
    src, land = _split_wait("grad_chip_late_wait", late["sems"], late["src"], late["land"], last + [sum_e], scatter_plan)
    sum_l = _chip_sum(src, land, chip, "grad_chip_sum_late")
    other_e, other_l = _pair_swap([sum_e, sum_l], "grad_pair_swap")
    gshard_e, gshard_l = _both_halves(sum_e, other_e, ci), _both_halves(sum_l, other_l, ci)

    pieces = {n: gshard_e[off_e[n]:off_e[n] + BIG_ROWS[n]] for n in EARLY}
    pieces.update({n: gshard_l[off_l[n]:off_l[n] + BIG_ROWS[n]] for n in LATE})
    grads, delta, new_m, new_v = {}, {}, {}, {}
    for n in BIG:
        _, k, cols = wts[n].shape
        as_rows = (lambda a: a[0].T) if BIG_T[n] else (lambda a: a[0])
        back = (lambda a: a.T[None]) if BIG_T[n] else (lambda a: a[None])
        g2 = pieces[n].reshape((cols, k) if BIG_T[n] else (k, cols))
        d_, m_, v_ = _elementwise(_adamw_math, 3, "adamw_" + n, as_rows(wts[n]), g2, as_rows(mom[n]), as_rows(var[n]))
        grads[n], delta[n], new_m[n], new_v[n] = back(g2), back(d_), back(m_), back(v_)
    small_full = dict(zip(SMALL, _flat_unpack(small_sum, [full_shapes[n] for n in SMALL])))
    for n in SMALL:
        a = small_full[n]
        if n in SMALL_SHARDED:
            width = wts[n].shape[-1]
            a = lax.dynamic_slice_in_dim(a, chip * width, width, axis=a.ndim - 1)
        grads[n] = a.reshape(wts[n].shape)

    packs =[_flat_pack([src[n] for n in SMALL], SMALL_ADAM_ROWS) for src in (wts, grads, mom, var)]
    outs = _elementwise(_adamw_math, 3, "adamw_small", *packs)
    for dst, o in zip((delta, new_m, new_v), outs):
        dst.update(zip(SMALL, _flat_unpack(o, [wts[n].shape for n in SMALL])))

    return (loss, grad_x, *[grads[n] for n in WEIGHTS], *[delta[n] for n in WEIGHTS],
            *[new_m[n] for n in WEIGHTS], *[new_v[n] for n in WEIGHTS])
```

```python
import functools
import math

import jax
import jax.numpy as jnp
from jax import lax
from jax.experimental import pallas as pl
from jax.experimental.pallas import tpu as pltpu

F32 = jnp.float32
BF = jnp.bfloat16
MESH = pl.DeviceIdType.MESH

D = 1024
SEQ = 2048
N_META = 16
T = N_META + SEQ
N_HEADS = 8
QK_NOPE = 64
QK_ROPE = 32
QK_HEAD = 96
V_HEAD = 64
Q_LORA = 384
KV_LORA = 256
D_ATTN = 512
D_RNN = 512
RNN_BW = 64
CONV_W = 4
LRU_C = 8.0
ROPE_THETA = 10000.0
D_FF = 2816
EPS = 1e-6
IN_COLS = 1696
ADAM_LR, ADAM_B1, ADAM_B2, ADAM_EPS, ADAM_WD, ADAM_STEP = 0.001, 0.9, 0.999, 1e-08, 0.01, 10

LANES = 128
TP = 2176
NB = 2
R = NB * TP
TR = 256
TRF = 256
TQ = 544
HP = LANES
PC = 1792
O_CKV, O_KR, O_XR, O_XG = 384, 640, 768, 1280
CG = 128
N_CG = D_RNN // CG
VMEM_LIMIT = 56 * 1024 * 1024
N_CHIPS = 4
SCALE = QK_HEAD ** -0.5
KEY_MASK = -30000.0
LOG2_E = 1.4426950408889634
SCALE_LOG2 = SCALE * LOG2_E


def _nt(a, b):
    return lax.dot_general(a, b, (((1,), (1,)), ((), ())), preferred_element_type=F32)


def _nn(a, b):
    return jnp.dot(a, b, preferred_element_type=F32)


def _tn(a, b):
    return lax.dot_general(a, b, (((0,), (0,)), ((), ())), preferred_element_type=F32)


def _rms(x, g, n):
    ms = jnp.sum(x * x, axis=-1, keepdims=True) * (1.0 / n)
    return x * lax.rsqrt(ms + EPS) * g


def _rot_impl(x):
    lane = lax.broadcasted_iota(jnp.int32, x.shape, 1)
    left = pltpu.roll(x, HP - 16, 1)
    right = pltpu.roll(x, 16, 1)
    lo = (lane >= QK_NOPE) & (lane < QK_NOPE + 16)
    hi = (lane >= QK_NOPE + 16) & (lane < QK_HEAD)
    return jnp.where(lo, -left, jnp.where(hi, right, 0.0))


@jax.custom_vjp
def _rot(x):
    return _rot_impl(x)


def _rot_fwd(x):
    return _rot_impl(x), None


def _rot_bwd(_, g):
    return (-_rot_impl(g),)


_rot.defvjp(_rot_fwd, _rot_bwd)


def _head(x, g, cs, sn):
    n = _rms(x, g, QK_HEAD)
    return n * cs + _rot(n) * sn


def _head_bwd(x, g, cs, sn, dout):
    rs = lax.rsqrt(jnp.sum(x * x, axis=-1, keepdims=True) * (1.0 / QK_HEAD) + EPS)
    xh = x * rs
    dn = dout * cs - _rot_impl(dout * sn)
    gdn = g * dn
    t = jnp.sum(gdn * xh, axis=-1, keepdims=True) * (1.0 / QK_HEAD)
    return rs * (gdn - xh * t), jnp.sum(dn * xh, axis=0, keepdims=True)


def _const_spec(shape):
    return pl.BlockSpec(shape, lambda *_: (0,) * len(shape), pipeline_mode=pl.Buffered(1))


def _row_spec(n, tr=TR):
    return pl.BlockSpec((tr, n), lambda i: (i, 0))


def _params(*sem, vmem=VMEM_LIMIT):
    return pltpu.CompilerParams(dimension_semantics=sem, vmem_limit_bytes=vmem)


def _stage_a_fwd(hp, cs, sn, cw):
    def body(hp_ref, cs_ref, sn_ref, ln1, win, qag, wq, kvag, wk, wv, qg, kg,
             pa_ref, xr_ref, xg_ref, q_ref, k_ref, v_ref):
        hn = _rms(hp_ref[...], ln1[...], D).astype(BF)
        p = _nt(hn, win[...])
        pa_ref[...] = p[:, :O_XR]
        xr_ref[...] = p[:, O_XR:O_XG]
        xg_ref[...] = p[:, O_XG:]
        cqn = _rms(p[:, :O_CKV], qag[...], Q_LORA).astype(BF)
        ckvn = _rms(p[:, O_CKV:O_KR], kvag[...], KV_LORA).astype(BF)
        kr = p[:, O_KR:O_XR]
        c, s = cs_ref[...], sn_ref[...]
        mask_lane = lax.broadcasted_iota(jnp.int32, (1, HP), 1) == QK_HEAD
        row = pl.program_id(0) * TRF + lax.broadcasted_iota(jnp.int32, (TRF, 1), 0)
        key_mask = jnp.where(jnp.where(row >= TP, row - TP, row) < T, 0.0, KEY_MASK)
        qraw = _nt(cqn, wq[...])
        kraw = _nt(ckvn, wk[...])
        for h in range(N_HEADS):
            sl = slice(h * HP, (h + 1) * HP)
            q_ref[:, sl] = jnp.where(mask_lane, 1.0, _head(qraw[:, sl], qg[...], c, s)).astype(BF)
            k_ref[:, sl] = jnp.where(mask_lane, key_mask, _head(kraw[:, sl] + kr, kg[...], c, s)).astype(BF)
        v_ref[...] = _nt(ckvn, wv[...]).astype(BF)

    rs = lambda n: _row_spec(n, TRF)
    return pl.pallas_call(
        body, grid=(R // TRF,), name="stage_a_fwd",
        in_specs=[rs(D), rs(HP), rs(HP), _const_spec((1, D)), _const_spec((PC, D)),
                  _const_spec((1, Q_LORA)), _const_spec((N_HEADS * HP, Q_LORA)), _const_spec((1, KV_LORA)),
                  _const_spec((N_HEADS * HP, KV_LORA)), _const_spec((D_ATTN, KV_LORA)), _const_spec((1, HP)),
                  _const_spec((1, HP))],
        out_specs=[rs(O_XR), rs(D_RNN), rs(D_RNN), rs(N_HEADS * HP), rs(N_HEADS * HP), rs(D_ATTN)],
        out_shape=[jax.ShapeDtypeStruct((R, O_XR), F32), jax.ShapeDtypeStruct((R, D_RNN), F32),
                   jax.ShapeDtypeStruct((R, D_RNN), F32), jax.ShapeDtypeStruct((R, N_HEADS * HP), BF),
                   jax.ShapeDtypeStruct((R, N_HEADS * HP), BF), jax.ShapeDtypeStruct((R, D_ATTN), BF)],
        compiler_params=_params("arbitrary"),
    )(hp, cs, sn, cw["ln1_g"], cw["win"], cw["qa_g"], cw["wq"], cw["kva_g"], cw["wk"], cw["wv"], cw["q_g"], cw["k_g"])


def _stage_a_bwd(dq, dk, dv, dxr, dxg, dh1, hp, pa, cs, sn, cw):
    def body(dq_ref, dk_ref, dv_ref, dxr_ref, dxg_ref, dh1_ref, hp_ref, pa_ref, cs_ref, sn_ref,
             ln1, win, qag, wq, kvag, wk, wv, qg, kg,
             dhp_ref, dp_ref, dqraw_ref, dkraw_ref, hn_ref, cqn_ref, ckvn_ref,
             dln1_ref, dqag_ref, dkvag_ref, dqg_ref, dkg_ref):
        @pl.when(pl.program_id(0) == 0)
        def _():
            for r in (dln1_ref, dqag_ref, dkvag_ref, dqg_ref, dkg_ref):
                r[...] = jnp.zeros_like(r)

        hn, vjp_ln1 = jax.vjp(lambda h, g: _rms(h, g, D), hp_ref[...], ln1[...])
        hn_ref[...] = hn.astype(BF)
        pa_v = pa_ref[...]
        cqn, vjp_qa = jax.vjp(lambda x, g: _rms(x, g, Q_LORA), pa_v[:, :O_CKV], qag[...])
        ckvn, vjp_kva = jax.vjp(lambda x, g: _rms(x, g, KV_LORA), pa_v[:, O_CKV:O_KR], kvag[...])
        kr = pa_v[:, O_KR:O_XR]
        cqnb, ckvnb = cqn.astype(BF), ckvn.astype(BF)
        cqn_ref[...] = cqnb
        ckvn_ref[...] = ckvnb
        c, s = cs_ref[...], sn_ref[...]
        lane = lax.broadcasted_iota(jnp.int32, (1, HP), 1)
        rope_lanes = ((lane >= QK_NOPE) & (lane < QK_HEAD)).astype(F32)
        dkr = jnp.zeros((TR, HP), F32)
        dqg = jnp.zeros((1, HP), F32)
        dkg = jnp.zeros((1, HP), F32)
        qraw = _nt(cqnb, wq[...])
        kraw = _nt(ckvnb, wk[...])
        for h in range(N_HEADS):
            sl = slice(h * HP, (h + 1) * HP)
            dqraw, dg = _head_bwd(qraw[:, sl], qg[...], c, s, dq_ref[:, sl])
            dqg = dqg + dg
            dqraw_ref[:, sl] = dqraw.astype(BF)
            dkraw, dg = _head_bwd(kraw[:, sl] + kr, kg[...], c, s, dk_ref[:, sl])
            dkg = dkg + dg
            dkraw_ref[:, sl] = dkraw.astype(BF)
            dkr = dkr + dkraw * rope_lanes
        dcq, dqag = vjp_qa(_nn(dqraw_ref[...], wq[...]))
        dckv, dkvag = vjp_kva(_nn(dkraw_ref[...], wk[...]) + _nn(dv_ref[...].astype(BF), wv[...]))
        dpb = jnp.concatenate([dcq, dckv, dkr, dxr_ref[...], dxg_ref[...]], axis=1).astype(BF)
        dp_ref[...] = dpb
        dh, dln1 = vjp_ln1(_nn(dpb, win[...]))
        dhp_ref[...] = dh + dh1_ref[...]
        dln1_ref[...] += dln1
        dqag_ref[...] += dqag
        dkvag_ref[...] += dkvag
        dqg_ref[...] += dqg
        dkg_ref[...] += dkg

    acc = lambda n: pl.BlockSpec((1, n), lambda i: (0, 0))
    return pl.pallas_call(
        body, grid=(R // TR,), name="stage_a_bwd",
        in_specs=[_row_spec(N_HEADS * HP), _row_spec(N_HEADS * HP), _row_spec(D_ATTN), _row_spec(D_RNN),
                  _row_spec(D_RNN), _row_spec(D), _row_spec(D), _row_spec(O_XR), _row_spec(HP), _row_spec(HP),
                  _const_spec((1, D)), _const_spec((PC, D)), _const_spec((1, Q_LORA)),
                  _const_spec((N_HEADS * HP, Q_LORA)), _const_spec((1, KV_LORA)),
                  _const_spec((N_HEADS * HP, KV_LORA)), _const_spec((D_ATTN, KV_LORA)), _const_spec((1, HP)),
                  _const_spec((1, HP))],
        out_specs=[_row_spec(D), _row_spec(PC), _row_spec(N_HEADS * HP), _row_spec(N_HEADS * HP), _row_spec(D),
                   _row_spec(Q_LORA), _row_spec(KV_LORA), acc(D), acc(Q_LORA), acc(KV_LORA), acc(HP), acc(HP)],
        out_shape=[jax.ShapeDtypeStruct((R, D), F32), jax.ShapeDtypeStruct((R, PC), BF),
                   jax.ShapeDtypeStruct((R, N_HEADS * HP), BF), jax.ShapeDtypeStruct((R, N_HEADS * HP), BF),
                   jax.ShapeDtypeStruct((R, D), BF), jax.ShapeDtypeStruct((R, Q_LORA), BF),
                   jax.ShapeDtypeStruct((R, KV_LORA), BF), jax.ShapeDtypeStruct((1, D), F32),
                   jax.ShapeDtypeStruct((1, Q_LORA), F32), jax.ShapeDtypeStruct((1, KV_LORA), F32),
                   jax.ShapeDtypeStruct((1, HP), F32), jax.ShapeDtypeStruct((1, HP), F32)],
        compiler_params=_params("arbitrary"),
    )(dq, dk, dv, dxr, dxg, dh1, hp, pa, cs, sn, cw["ln1_g"], cw["win"], cw["qa_g"], cw["wq"], cw["kva_g"],
      cw["wk"], cw["wv"], cw["q_g"], cw["k_g"])


def _head_mask(half, dtype):
    lane = lax.broadcasted_iota(jnp.int32, (1, 2 * V_HEAD), 1)
    return ((lane >= V_HEAD) == (half == 1)).astype(dtype)


_ATTN_GRID = (NB, N_HEADS // 2, TP // TQ)
_Q_SPEC = pl.BlockSpec((TQ, 2 * HP), lambda b, j, i: (b * (TP // TQ) + i, j))
_K_SPEC = pl.BlockSpec((TP, 2 * HP), lambda b, j, i: (b, j))
_V_SPEC = pl.BlockSpec((TP, 2 * V_HEAD), lambda b, j, i: (b, j))
_O_SPEC = pl.BlockSpec((TQ, 2 * V_HEAD), lambda b, j, i: (b * (TP // TQ) + i, j))
_LSE_SPEC = pl.BlockSpec((None, TQ, 2), lambda b, j, i: (j, b * (TP // TQ) + i, 0))


def _attn_fwd(q, k, v):
    def body(q_ref, k_ref, v_ref, o_ref, lse_ref):
        v2 = v_ref[...]
        o = jnp.zeros((TQ, 2 * V_HEAD), F32)
        lse = []
        for hh in range(2):
            sl = slice(hh * HP, (hh + 1) * HP)
            raw = _nt(q_ref[:, sl], k_ref[:, sl])
            m = jnp.max(raw, axis=-1, keepdims=True)
            e = jnp.exp2((raw - m) * SCALE_LOG2)
            l = jnp.sum(e, axis=-1, keepdims=True)
            o = o + _nn(e.astype(BF), v2 * _head_mask(hh, BF)) * (1.0 / l)
            lse.append(m * SCALE_LOG2 + jnp.log(l) * LOG2_E)
        o_ref[...] = o
        lane = lax.broadcasted_iota(jnp.int32, (TQ, 2), 1)
        lse_ref[...] = jnp.where(lane == 0, lse[0], lse[1])

    return pl.pallas_call(
        body, grid=_ATTN_GRID, name="attn_fwd", in_specs=[_Q_SPEC, _K_SPEC, _V_SPEC], out_specs=[_O_SPEC, _LSE_SPEC],
        out_shape=[jax.ShapeDtypeStruct((R, D_ATTN), F32), jax.ShapeDtypeStruct((N_HEADS // 2, R, 2), F32)],
        compiler_params=_params("arbitrary", "arbitrary", "arbitrary"),
    )(q, k, v)


def _attn_bwd(q, k, v, o, lse, do):
    def body(q_ref, k_ref, v_ref, o_ref, lse_ref, do_ref, dq_ref, dk_ref, dv_ref):
        @pl.when(pl.program_id(2) == 0)
        def _():
            dk_ref[...] = jnp.zeros_like(dk_ref)
            dv_ref[...] = jnp.zeros_like(dv_ref)

        do = do_ref[...]
        dob = do.astype(BF)
        do_o = do * o_ref[...]
        v2 = v_ref[...]
        dv_sum = jnp.zeros((TP, 2 * V_HEAD), F32)
        for hh in range(2):
            sl = slice(hh * HP, (hh + 1) * HP)
            qb, kb = q_ref[:, sl], k_ref[:, sl]
            p = jnp.exp2(_nt(qb, kb) * SCALE_LOG2 - lse_ref[:, hh:hh + 1])
            dp = _nt(dob, v2 * _head_mask(hh, BF))
            delta = jnp.sum(do_o * _head_mask(hh, F32), axis=-1, keepdims=True)
            dsb = (p * (dp - delta) * SCALE).astype(BF)
            dq_ref[:, sl] = _nn(dsb, kb)
            dk_ref[:, sl] += _tn(dsb, qb)
            dv_sum = dv_sum + _tn(p.astype(BF), dob) * _head_mask(hh, F32)
        dv_ref[...] += dv_sum

    return pl.pallas_call(
        body, grid=_ATTN_GRID, name="attn_bwd", in_specs=[_Q_SPEC, _K_SPEC, _V_SPEC, _O_SPEC, _LSE_SPEC, _O_SPEC],
        out_specs=[_Q_SPEC, _K_SPEC, _V_SPEC],
        out_shape=[jax.ShapeDtypeStruct((R, N_HEADS * HP), F32), jax.ShapeDtypeStruct((R, N_HEADS * HP), F32),
                   jax.ShapeDtypeStruct((R, D_ATTN), F32)],
        compiler_params=_params("arbitrary", "arbitrary", "arbitrary"),
    )(q, k, v, o, lse, do)


def _tile_prefix(a_ref, b_ref, reverse):
    tiles = (TP // 8, 8, CG)
    r8 = lax.broadcasted_iota(jnp.int32, tiles, 1)
    a, b = a_ref[...].reshape(tiles), b_ref[...].reshape(tiles)
    for s in (1, 2, 4):
        shift = 8 - s if reverse else s
        keep = (r8 < 8 - s) if reverse else (r8 >= s)
        b = jnp.where(keep, a * pltpu.roll(b, shift, 1) + b, b)
        a = jnp.where(keep, a * pltpu.roll(a, shift, 1), a)
    a_ref[...] = a.reshape(TP, CG)
    b_ref[...] = b.reshape(TP, CG)


def _scan_pair(af_ref, bf_ref, hf_ref, ab_ref, bb_ref, hb_ref):
    _tile_prefix(af_ref, bf_ref, False)
    _tile_prefix(ab_ref, bb_ref, True)
    n_tiles = TP // 8

    def step(i, carry):
        cf, cb = carry
        rf = pl.multiple_of(i * 8, 8)
        rb = pl.multiple_of((n_tiles - 1 - i) * 8, 8)
        hf_ref[pl.ds(rf, 8), :] = bf_ref[pl.ds(rf, 8), :] + af_ref[pl.ds(rf, 8), :] * cf
        hb_ref[pl.ds(rb, 8), :] = bb_ref[pl.ds(rb, 8), :] + ab_ref[pl.ds(rb, 8), :] * cb
        cf = bf_ref[pl.ds(rf + 7, 1), :] + af_ref[pl.ds(rf + 7, 1), :] * cf
        cb = bb_ref[pl.ds(rb, 1), :] + ab_ref[pl.ds(rb, 1), :] * cb
        return cf, cb

    zero = jnp.zeros((1, CG), F32)
    lax.fori_loop(0, n_tiles, step, (zero, zero), unroll=8)


def _shifts(x):
    t = lax.broadcasted_iota(jnp.int32, x.shape, 0)
    xm2 = jnp.where(t >= 2, pltpu.roll(x, 2, 0), 0.0)
    xm1 = jnp.where(t >= 1, pltpu.roll(x, 1, 0), 0.0)
    xp1 = jnp.where(t < TP - 1, pltpu.roll(x, TP - 1, 0), 0.0)
    return xm2, xm1, xp1


def _softplus(z):
    e = jnp.exp(-jnp.abs(z))
    small = e * (1.0 - e * (0.5 - e * (1.0 / 3.0)))
    return jnp.maximum(z, 0.0) + jnp.where(e < 0.01, small, jnp.log(1.0 + e))


def _neg_expm1(x):
    series = -x * (1.0 + x * 0.5 * (1.0 + x * (1.0 / 3.0) * (1.0 + x * 0.25)))
    return jnp.where(x > -0.05, series, 1.0 - jnp.exp(x))


def _gates(row0, xc, pa_f, pi_f, pa_b, pi_b, lam_f, lam_b):
    t = row0 + lax.broadcasted_iota(jnp.int32, xc.shape, 0)
    valid = t < T
    out = []
    for pa, pi_, lam in ((pa_f, pi_f, lam_f), (pa_b, pi_b, lam_b)):
        r = jax.nn.sigmoid(pa)
        gate_i = jax.nn.sigmoid(pi_)
        log_a = -LRU_C * r * _softplus(-lam)
        a = jnp.exp(log_a)
        mult = jnp.sqrt(jnp.maximum(_neg_expm1(2.0 * log_a), 0.0))
        out += [a, jnp.where(valid, mult * (gate_i * xc), 0.0)]
    return tuple(out)


def _gates_bwd(row0, xc, pres, lams, cots):
    t = row0 + lax.broadcasted_iota(jnp.int32, xc.shape, 0)
    valid = t < T
    dxc = jnp.zeros_like(xc)
    dpres, dlams = [], []
    for d in range(2):
        pa, pi_, lam = pres[2 * d], pres[2 * d + 1], lams[d]
        da, db = cots[2 * d], jnp.where(valid, cots[2 * d + 1], 0.0)
        r = jax.nn.sigmoid(pa)
        gate_i = jax.nn.sigmoid(pi_)
        sp = _softplus(-lam)
        log_a = -LRU_C * r * sp
        a = jnp.exp(log_a)
        m2 = jnp.maximum(_neg_expm1(2.0 * log_a), 0.0)
        mult = jnp.sqrt(m2)
        dxc = dxc + db * (mult * gate_i)
        d_gate = db * (mult * xc)
        d_m2 = jnp.where(m2 > 0.0, db * (gate_i * xc) * (0.5 * lax.rsqrt(m2)), 0.0)
        d_log_a = da * a - 2.0 * d_m2 * (a * a)
        dpres += [d_log_a * (-LRU_C * sp) * (r * (1.0 - r)), d_gate * (gate_i * (1.0 - gate_i))]
        d_sp = jnp.sum(d_log_a * (-LRU_C * r), axis=0, keepdims=True)
        dlams.append(-d_sp * jax.nn.sigmoid(-lam))
    return dxc, dpres, dlams


def _rnn_specs():
    seq = pl.BlockSpec((TP, CG), lambda g, b: (b, g))
    return dict(
        seq=seq,
        cw=pl.BlockSpec((CONV_W, CG), lambda g, b: (0, g)),
        cb=pl.BlockSpec((1, CG), lambda g, b: (0, g)),
        w4=pl.BlockSpec((None, CG, 4 * CG), lambda g, b: (g, 0, 0)),
        b4=pl.BlockSpec((None, 1, 4 * CG), lambda g, b: (g, 0, 0)),
        lam=pl.BlockSpec((None, 1, 2 * CG), lambda g, b: (g, 0, 0)),
    )


def _conv(x, xm2, xm1, xp1, cw_ref, cb_ref):
    return cw_ref[0:1, :] * xm2 + cw_ref[1:2, :] * xm1 + cw_ref[2:3, :] * x + cw_ref[3:4, :] * xp1 + cb_ref[...]


TC = 128
N_TC = TP // TC


def _split4(pre):
    return pre[:, :CG], pre[:, CG:2 * CG], pre[:, 2 * CG:3 * CG], pre[:, 3 * CG:]


def _rnn_fwd(xr, xg, cw):
    def body(xr_ref, xg_ref, cw_ref, cb_ref, w4_ref, b4_ref, lam_ref, y_ref, hf_ref, hb_ref, xc_s, af, bf, ab, bb):
        x = xr_ref[...]
        xc_s[...] = _conv(x, *_shifts(x), cw_ref, cb_ref)
        lam = lam_ref[...]

        def chunk(i, _):
            rows = pl.ds(pl.multiple_of(i * TC, TC), TC)
            xc = xc_s[rows, :]
            pre = _nn(xc.astype(BF), w4_ref[...]) + b4_ref[...]
            a_f, b_f, a_b, b_b = _gates(i * TC, xc, *_split4(pre), lam[:, :CG], lam[:, CG:])
            af[rows, :] = a_f
            bf[rows, :] = b_f
            ab[rows, :] = a_b
            bb[rows, :] = b_b
            return 0

        lax.fori_loop(0, N_TC, chunk, 0)
        _scan_pair(af, bf, hf_ref, ab, bb, hb_ref)
        y_ref[...] = (hf_ref[...] + hb_ref[...]) * jax.nn.gelu(xg_ref[...])

    sp = _rnn_specs()
    return pl.pallas_call(
        body, grid=(N_CG, NB), name="rnn_fwd",
        in_specs=[sp["seq"], sp["seq"], sp["cw"], sp["cb"], sp["w4"], sp["b4"], sp["lam"]],
        out_specs=[sp["seq"]] * 3, out_shape=[jax.ShapeDtypeStruct((R, D_RNN), F32)] * 3,
        scratch_shapes=[pltpu.VMEM((TP, CG), F32)] * 5,
        compiler_params=_params("arbitrary", "arbitrary"),
    )(xr, xg, cw["conv_w"], cw["conv_b"], cw["w4"], cw["b4"], cw["lam"])


def _rnn_bwd(dy, xr, xg, hf, hb, cw):
    def body(dy_ref, xr_ref, xg_ref, hf_ref, hb_ref, cw_ref, cb_ref, w4_ref, b4_ref, lam_ref,
             dxr_ref, dxg_ref, dcw_ref, dcb_ref, dw4_ref, db4_ref, dlam_ref,
             xc_s, af_s, ab_s, dhs_s, dhs2_s, lf_s, lb_s, daf_s, dab_s, dxc_s):
        @pl.when(pl.program_id(1) == 0)
        def _():
            for r in (dcw_ref, dcb_ref, dw4_ref, db4_ref, dlam_ref):
                r[...] = jnp.zeros_like(r)

        x = xr_ref[...]
        xc_s[...] = _conv(x, *_shifts(x), cw_ref, cb_ref)
        lam = lam_ref[...]

        def chunk1(i, _):
            rows = pl.ds(pl.multiple_of(i * TC, TC), TC)
            xc = xc_s[rows, :]
            pre = _nn(xc.astype(BF), w4_ref[...]) + b4_ref[...]
            a_f, _, a_b, _ = _gates(i * TC, xc, *_split4(pre), lam[:, :CG], lam[:, CG:])
            af_s[rows, :] = a_f
            ab_s[rows, :] = a_b
            _, vjp_y = jax.vjp(lambda h, g: h * jax.nn.gelu(g), hf_ref[rows, :] + hb_ref[rows, :], xg_ref[rows, :])
            dhs, dxg = vjp_y(dy_ref[rows, :])
            dhs_s[rows, :] = dhs
            dhs2_s[rows, :] = dhs
            dxg_ref[rows, :] = dxg
            return 0

        lax.fori_loop(0, N_TC, chunk1, 0)
        t = lax.broadcasted_iota(jnp.int32, (TP, CG), 0)
        af_s[...] = pltpu.roll(af_s[...], TP - 1, 0)
        ab_s[...] = pltpu.roll(ab_s[...], 1, 0)
        _scan_pair(ab_s, dhs_s, lb_s, af_s, dhs2_s, lf_s)
        daf_s[...] = lf_s[...] * jnp.where(t >= 1, pltpu.roll(hf_ref[...], 1, 0), 0.0)
        dab_s[...] = lb_s[...] * jnp.where(t < TP - 1, pltpu.roll(hb_ref[...], TP - 1, 0), 0.0)

        def chunk2(i, _):
            rows = pl.ds(pl.multiple_of(i * TC, TC), TC)
            xc = xc_s[rows, :]
            xcb = xc.astype(BF)
            pre = _nn(xcb, w4_ref[...]) + b4_ref[...]
            dxc, dpres, dlams = _gates_bwd(i * TC, xc, _split4(pre), (lam[:, :CG], lam[:, CG:]),
                                           (daf_s[rows, :], lf_s[rows, :], dab_s[rows, :], lb_s[rows, :]))
            dpre = jnp.concatenate(dpres, axis=1)
            dpreb = dpre.astype(BF)
            dxc_s[rows, :] = dxc + _nt(dpreb, w4_ref[...])
            dw4_ref[...] += _tn(xcb, dpreb)
            db4_ref[...] += jnp.sum(dpre, axis=0, keepdims=True)
            dlam_ref[...] += jnp.concatenate(dlams, axis=1)
            return 0

        lax.fori_loop(0, N_TC, chunk2, 0)
        dxc = dxc_s[...]
        dcb_ref[...] += jnp.sum(dxc, axis=0, keepdims=True)
        for tap, xs in enumerate(_shifts(x)[:2] + (x,) + _shifts(x)[2:]):
            dcw_ref[tap:tap + 1, :] += jnp.sum(xs * dxc, axis=0, keepdims=True)
        dxr_ref[...] = (cw_ref[0:1, :] * jnp.where(t < TP - 2, pltpu.roll(dxc, TP - 2, 0), 0.0)
                        + cw_ref[1:2, :] * jnp.where(t < TP - 1, pltpu.roll(dxc, TP - 1, 0), 0.0)
                        + cw_ref[2:3, :] * dxc
                        + cw_ref[3:4, :] * jnp.where(t >= 1, pltpu.roll(dxc, 1, 0), 0.0))

    sp = _rnn_specs()
    return pl.pallas_call(
        body, grid=(N_CG, NB), name="rnn_bwd",
        in_specs=[sp["seq"]] * 5 + [sp["cw"], sp["cb"], sp["w4"], sp["b4"], sp["lam"]],
        out_specs=[sp["seq"], sp["seq"], sp["cw"], sp["cb"], sp["w4"], sp["b4"], sp["lam"]],
        out_shape=[jax.ShapeDtypeStruct((R, D_RNN), F32), jax.ShapeDtypeStruct((R, D_RNN), F32),
                   jax.ShapeDtypeStruct((CONV_W, D_RNN), F32), jax.ShapeDtypeStruct((1, D_RNN), F32),
                   jax.ShapeDtypeStruct((N_CG, CG, 4 * CG), F32), jax.ShapeDtypeStruct((N_CG, 1, 4 * CG), F32),
                   jax.ShapeDtypeStruct((N_CG, 1, 2 * CG), F32)],
        scratch_shapes=[pltpu.VMEM((TP, CG), F32)] * 10,
        compiler_params=_params("arbitrary", "arbitrary"),
    )(dy, xr, xg, hf, hb, cw["conv_w"], cw["conv_b"], cw["w4"], cw["b4"], cw["lam"])


TD = 256
STAGE_D_VMEM = 58 * 1024 * 1024


def _stage_d(hp, o, y, tgt, cw):
    def body(hp_ref, o_ref, y_ref, tgt_ref, ga, gr, wout, ln2, wg, wu, wd,
             do_ref, dy_ref, dh1_ref, mix_ref, dh1b_ref, hn2_ref, dg_ref, du_ref, act_ref, dh2b_ref,
             loss_ref, dga_ref, dgr_ref, dln2_ref):
        i = pl.program_id(0)

        @pl.when(i == 0)
        def _():
            for r in (loss_ref, dga_ref, dgr_ref, dln2_ref):
                r[...] = jnp.zeros_like(r)

        mix_a, vjp_a = jax.vjp(lambda x, g: _rms(x, g, D_ATTN), o_ref[...], ga[...])
        mix_r, vjp_r = jax.vjp(lambda x, g: _rms(x, g, D_RNN), y_ref[...], gr[...])
        mab, mrb = mix_a.astype(BF), mix_r.astype(BF)
        mix_ref[:, :D_ATTN] = mab
        mix_ref[:, D_ATTN:] = mrb
        h1 = hp_ref[...] + _nn(mab, wout[:D_ATTN, :]) + _nn(mrb, wout[D_ATTN:, :])
        hn2, vjp_ln2 = jax.vjp(lambda x, g: _rms(x, g, D), h1, ln2[...])
        hn2b = hn2.astype(BF)
        hn2_ref[...] = hn2b
        act, vjp_act = jax.vjp(lambda g, u: jax.nn.silu(g) * u, _nt(hn2b, wg[...]), _nt(hn2b, wu[...]))
        actb = act.astype(BF)
        act_ref[...] = actb
        h2 = h1 + _nn(actb, wd[...])
        row = i * TD + lax.broadcasted_iota(jnp.int32, (TD, 1), 0)
        t = jnp.where(row >= TP, row - TP, row)
        err = jnp.where((t >= N_META) & (t < T), h2 - tgt_ref[...], 0.0)
        loss_ref[...] += jnp.sum(err * err) * (0.5 / D)
        dh2b = (err * (1.0 / D)).astype(BF)
        dh2b_ref[...] = dh2b
        dg, du = vjp_act(_nt(dh2b, wd[...]))
        dgb, dub = dg.astype(BF), du.astype(BF)
        dg_ref[...] = dgb
        du_ref[...] = dub
        dh1n, dln2 = vjp_ln2(_nn(dgb, wg[...]) + _nn(dub, wu[...]))
        dh1 = err * (1.0 / D) + dh1n
        dh1_ref[...] = dh1
        dh1b = dh1.astype(BF)
        dh1b_ref[...] = dh1b
        dmix = _nt(dh1b, wout[...])
        do, dga = vjp_a(dmix[:, :D_ATTN])
        dyr, dgr = vjp_r(dmix[:, D_ATTN:])
        do_ref[...] = do
        dy_ref[...] = dyr
        dga_ref[...] += dga
        dgr_ref[...] += dgr
        dln2_ref[...] += dln2

    rs = lambda n: _row_spec(n, TD)
    acc = lambda n: pl.BlockSpec((1, n), lambda i: (0, 0))
    return pl.pallas_call(
        body, grid=(R // TD,), name="stage_d",
        in_specs=[rs(D), rs(D_ATTN), rs(D_RNN), rs(D), _const_spec((1, D_ATTN)), _const_spec((1, D_RNN)),
                  _const_spec((D, D)), _const_spec((1, D)), _const_spec((D_FF, D)), _const_spec((D_FF, D)),
                  _const_spec((D_FF, D))],
        out_specs=[rs(D_ATTN), rs(D_RNN), rs(D), rs(D), rs(D), rs(D), rs(D_FF), rs(D_FF), rs(D_FF), rs(D),
                   acc(1), acc(D_ATTN), acc(D_RNN), acc(D)],
        out_shape=[jax.ShapeDtypeStruct((R, D_ATTN), F32), jax.ShapeDtypeStruct((R, D_RNN), F32),
                   jax.ShapeDtypeStruct((R, D), F32), jax.ShapeDtypeStruct((R, D), BF),
                   jax.ShapeDtypeStruct((R, D), BF), jax.ShapeDtypeStruct((R, D), BF),
                   jax.ShapeDtypeStruct((R, D_FF), BF), jax.ShapeDtypeStruct((R, D_FF), BF),
                   jax.ShapeDtypeStruct((R, D_FF), BF), jax.ShapeDtypeStruct((R, D), BF),
                   jax.ShapeDtypeStruct((1, 1), F32), jax.ShapeDtypeStruct((1, D_ATTN), F32),
                   jax.ShapeDtypeStruct((1, D_RNN), F32), jax.ShapeDtypeStruct((1, D), F32)],
        compiler_params=_params("arbitrary", vmem=STAGE_D_VMEM),
    )(hp, o, y, tgt, cw["ga"], cw["gr"], cw["wout"], cw["ln2_g"], cw["wg"], cw["wu"], cw["wd"])


TW = 2176


def _wgrad(a, b, name, tk=None):
    ka, nb = a.shape[1], b.shape[1]
    tk = ka if tk is None else tk

    def body(a_ref, b_ref, o_ref):
        @pl.when(pl.program_id(1) == 0)
        def _():
            o_ref[...] = jnp.zeros_like(o_ref)

        o_ref[...] += _tn(a_ref[...].astype(BF), b_ref[...].astype(BF))

    return pl.pallas_call(
        body, grid=(ka // tk, R // TW), name=name,
        in_specs=[pl.BlockSpec((TW, tk), lambda k, r: (r, k)), pl.BlockSpec((TW, nb), lambda k, r: (r, 0))],
        out_specs=pl.BlockSpec((tk, nb), lambda k, r: (k, 0)),
        out_shape=jax.ShapeDtypeStruct((ka, nb), F32),
        compiler_params=_params("arbitrary", "arbitrary"),
    )(a, b)


def _rope_tables():
    half = QK_ROPE // 2
    freqs = 1.0 / (ROPE_THETA ** (jnp.arange(half, dtype=F32) / half))
    ang = jnp.arange(TP, dtype=F32)[:, None] * freqs[None, :]
    ones = jnp.ones((TP, QK_NOPE), F32)
    zeros = jnp.zeros((TP, QK_NOPE), F32)
    pad1 = jnp.ones((TP, HP - QK_HEAD), F32)
    pad0 = jnp.zeros((TP, HP - QK_HEAD), F32)
    cs = jnp.concatenate([ones, jnp.cos(ang), jnp.cos(ang), pad1], axis=1)
    sn = jnp.concatenate([zeros, jnp.sin(ang), jnp.sin(ang), pad0], axis=1)
    return jnp.tile(cs, (NB, 1)), jnp.tile(sn, (NB, 1))


def _pad_rows(a, lo, hi):
    return jnp.pad(a, ((0, 0), (lo, hi), (0, 0)))


def _compute_weights(w):
    win_t = w["w_in_t"]
    kr = win_t[O_KR:O_KR + QK_ROPE]
    win = jnp.concatenate([win_t[:O_KR], jnp.zeros((QK_NOPE, D), F32), kr,
                           jnp.zeros((HP - QK_HEAD, D), F32), win_t[O_KR + QK_ROPE:]], axis=0)
    wq = _pad_rows(w["w_uq_t"].reshape(N_HEADS, QK_HEAD, Q_LORA), 0, HP - QK_HEAD)
    wkv = w["w_ukv_t"].reshape(N_HEADS, QK_NOPE + V_HEAD, KV_LORA)
    wk = _pad_rows(wkv[:, :QK_NOPE], 0, HP - QK_NOPE)
    wv = wkv[:, QK_NOPE:].reshape(D_ATTN, KV_LORA)
    gates = jnp.stack([w["lru_wa"][0], w["lru_wi"][0], w["lru_wa"][1], w["lru_wi"][1]])
    blk = gates.reshape(4, N_CG, 2, RNN_BW, RNN_BW)
    dense = jnp.einsum("tcaij,ab->tcaibj", blk, jnp.eye(2, dtype=F32)).reshape(4, N_CG, CG, CG)
    w4 = dense.transpose(1, 2, 0, 3).reshape(N_CG, CG, 4 * CG)
    bias = jnp.stack([w["lru_ba"][0], w["lru_bi"][0], w["lru_ba"][1], w["lru_bi"][1]])
    b4 = bias.reshape(4, N_CG, CG).transpose(1, 0, 2).reshape(N_CG, 1, 4 * CG)
    lam = w["lru_lambda"].reshape(2, N_CG, CG).transpose(1, 0, 2).reshape(N_CG, 1, 2 * CG)
    pad_g = lambda g: jnp.pad(g.reshape(1, QK_HEAD), ((0, 0), (0, HP - QK_HEAD)))
    return dict(
        ln1_g=w["ln1_g"].reshape(1, D), win=win.astype(BF), qa_g=w["q_a_norm_g"].reshape(1, Q_LORA),
        wq=wq.astype(BF).reshape(N_HEADS * HP, Q_LORA), kva_g=w["kv_a_norm_g"].reshape(1, KV_LORA),
        wk=wk.astype(BF).reshape(N_HEADS * HP, KV_LORA), wv=wv.astype(BF),
        q_g=pad_g(w["q_norm_g"]), k_g=pad_g(w["k_norm_g"]),
        conv_w=w["conv_w"].reshape(CONV_W, D_RNN), conv_b=w["conv_b"].reshape(1, D_RNN),
        w4=w4.astype(BF), b4=b4, lam=lam,
        ga=w["attn_out_g"].reshape(1, D_ATTN), gr=w["rnn_out_g"].reshape(1, D_RNN), ln2_g=w["ln2_g"].reshape(1, D),
    )


def _local_step(x, target, meta, w, late_weights, early_grads, mid_grads):
    cw = _compute_weights(w)
    cs, sn = _rope_tables()
    hp = jnp.concatenate([jnp.broadcast_to(meta[None], (NB, N_META, D)), x,
                          jnp.zeros((NB, TP - T, D), F32)], axis=1).reshape(R, D)
    tgt = _pad_rows(target, N_META, TP - T).reshape(R, D)

    pa, xr, xg, q, k, v = _stage_a_fwd(hp, cs, sn, cw)
    o, lse = _attn_fwd(q, k, v)
    y, hf, hb = _rnn_fwd(xr, xg, cw)
    late = late_weights([o, y])
    cw.update(wout=late["w_out"], wg=late["w_gate_t"], wu=late["w_up_t"], wd=late["w_down"])
    (do, dy, dh1, mixb, dh1b, hn2b, dgb, dub, actb, dh2b, loss, dga, dgr, dln2) = _stage_d(hp, o, y, tgt, cw)
    dwout = _wgrad(mixb, dh1b, "wgrad_out")
    dwg = _wgrad(dgb, hn2b, "wgrad_gate", tk=D_FF // 2)
    dwu = _wgrad(dub, hn2b, "wgrad_up", tk=D_FF // 2)
    dwd = _wgrad(actb, dh2b, "wgrad_down", tk=D_FF // 2)
    zero = early_grads(dict(w_out=dwout, w_gate=dwg, w_up=dwu, w_down=dwd))
    cw["conv_b"] = cw["conv_b"] + zero
    dxr, dxg, dcw, dcb, dw4, db4, dlam = _rnn_bwd(dy, xr, xg, hf, hb, cw)
    zero = mid_grads([dxr])
    dq, dk, dv = _attn_bwd(q, k, v, o, lse, do)
    (dhp, dpb, dqrawb, dkrawb, hn1b, cqnb, ckvnb, dln1, dqag, dkvag, dqg, dkg) = _stage_a_bwd(
        dq, dk, dv, dxr, dxg, dh1, hp, pa, cs, sn, dict(cw, qa_g=cw["qa_g"] + zero))

    dwin = _wgrad(dpb, hn1b, "wgrad_in", tk=PC // 2)
    dwq = _wgrad(dqrawb, cqnb, "wgrad_uq")
    dwk = _wgrad(dkrawb, ckvnb, "wgrad_uk")
    dwv = _wgrad(dv, ckvnb, "wgrad_uv")

    dwin_t = jnp.concatenate([dwin[:O_KR], dwin[O_KR + QK_NOPE:O_KR + QK_HEAD], dwin[O_XR:]], axis=0)
    dwq_t = dwq.reshape(N_HEADS, HP, Q_LORA)[:, :QK_HEAD].reshape(N_HEADS * QK_HEAD, Q_LORA)
    dwkv_t = jnp.concatenate([dwk.reshape(N_HEADS, HP, KV_LORA)[:, :QK_NOPE],
                              dwv.reshape(N_HEADS, V_HEAD, KV_LORA)], axis=1).reshape(2 * D_ATTN, KV_LORA)
    d4 = dw4.reshape(N_CG, 2, RNN_BW, 4, 2, RNN_BW)
    dgates = jnp.stack([d4[:, 0, :, :, 0, :], d4[:, 1, :, :, 1, :]], axis=1)
    dgates = dgates.transpose(3, 0, 1, 2, 4).reshape(4, N_HEADS, RNN_BW, RNN_BW)
    dbias = db4.reshape(N_CG, 4, CG).transpose(1, 0, 2).reshape(4, D_RNN)
    dhp3 = dhp.reshape(NB, TP, D)
    grads = dict(
        meta_tokens=jnp.sum(dhp3[:, :N_META], axis=0),
        ln1_g=dln1, w_in_t=dwin_t, q_a_norm_g=dqag, w_uq_t=dwq_t, kv_a_norm_g=dkvag, w_ukv_t=dwkv_t,
        q_norm_g=dqg[:, :QK_HEAD], k_norm_g=dkg[:, :QK_HEAD], conv_w=dcw[None], conv_b=dcb,
        lru_wa=jnp.stack([dgates[0], dgates[2]])[None], lru_ba=jnp.stack([dbias[0], dbias[2]])[None],
        lru_wi=jnp.stack([dgates[1], dgates[3]])[None], lru_bi=jnp.stack([dbias[1], dbias[3]])[None],
        lru_lambda=dlam.reshape(N_CG, 2, CG).transpose(1, 0, 2).reshape(1, 2, D_RNN),
        attn_out_g=dga, rnn_out_g=dgr, ln2_g=dln2,
    )
    return loss[0, 0], dhp3[:, N_META:T], grads, [dhp, dwin]


_ANY = pl.BlockSpec(memory_space=pl.ANY)


def _place():
    return lax.axis_index("x"), lax.axis_index("y"), lax.axis_index("c")


def _other_chips(x, y):
    return [(1 - x, y), (x, 1 - y), (1 - x, 1 - y)]


def _all_gather(arrs, name):
    n_arr = len(arrs)

    def body(*refs):
        x_refs, out_refs, zero_ref = refs[:n_arr], refs[n_arr:2 * n_arr], refs[2 * n_arr]
        send_sems, recv_sems, local_sems = refs[2 * n_arr + 1:]
        x, y, c = _place()
        me, sibling = (x, y, c), (x, y, 1 - c)
        chips = _other_chips(x, y)
        zero_ref[...] = jnp.zeros_like(zero_ref)

        def rows(a, px, py, pc):
            m = arrs[a].shape[0]
            return out_refs[a].at[pl.ds((4 * px + 2 * py + pc) * m, m), :]

        def copy(a, k, block, to, src=None):
            return pltpu.make_async_remote_copy(
                src_ref=rows(a, *block) if src is None else src, dst_ref=rows(a, *block),
                send_sem=send_sems.at[7 * a + k], recv_sem=recv_sems.at[7 * a + k], device_id=to, device_id_type=MESH)

        mine = [pltpu.make_async_copy(x_refs[a], rows(a, *me), local_sems.at[a]) for a in range(n_arr)]
        first, passed = [], []
        for a in range(n_arr):
            first.append(copy(a, 0, me, sibling, src=x_refs[a]))
            first += [copy(a, 1 + j, me, (*chip, c), src=x_refs[a]) for j, chip in enumerate(chips)]
        for cp in mine + first:
            cp.start()
        for a in range(n_arr):
            for j, chip in enumerate(chips):
                copy(a, 1 + j, (*chip, c), me).wait_recv()
                passed.append(copy(a, 4 + j, (*chip, c), sibling))
                passed[-1].start()
        for a in range(n_arr):
            copy(a, 0, sibling, me).wait_recv()
            for j, chip in enumerate(chips):
                copy(a, 4 + j, (*chip, 1 - c), me).wait_recv()
        for cp in first + passed:
            cp.wait_send()
        for cp in mine:
            cp.wait()

    outs = pl.pallas_call(
        body, name=name,
        out_shape=[jax.ShapeDtypeStruct((8 * a.shape[0], a.shape[1]), a.dtype) for a in arrs]
        + [jax.ShapeDtypeStruct((8, LANES), F32)],
        in_specs=[_ANY] * n_arr, out_specs=[_ANY] * n_arr + [pl.BlockSpec(memory_space=pltpu.VMEM)],
        scratch_shapes=[pltpu.SemaphoreType.DMA((7 * n_arr,)), pltpu.SemaphoreType.DMA((7 * n_arr,)),
                        pltpu.SemaphoreType.DMA((n_arr,))],
    )(*arrs)
    return outs[:n_arr], outs[n_arr]


def _pair_exchange(big, whole, name):
    n_s, _, m, n = big.shape
    n_copies = n_s + len(whole)

    def body(*refs):
        big_ref, whole_refs = refs[0], refs[1:1 + len(whole)]
        rbig_ref, rwhole_refs = refs[1 + len(whole)], refs[2 + len(whole):2 + 2 * len(whole)]
        send_sems, recv_sems = refs[-2:]
        x, y, c = _place()
        sibling = (x, y, 1 - c)
        copies = [pltpu.make_async_remote_copy(
            src_ref=big_ref.at[s, 1 - c], dst_ref=rbig_ref.at[s], send_sem=send_sems.at[s], recv_sem=recv_sems.at[s],
            device_id=sibling, device_id_type=MESH) for s in range(n_s)]
        copies += [pltpu.make_async_remote_copy(
            src_ref=a, dst_ref=r, send_sem=send_sems.at[n_s + i], recv_sem=recv_sems.at[n_s + i],
            device_id=sibling, device_id_type=MESH) for i, (a, r) in enumerate(zip(whole_refs, rwhole_refs))]
        for cp in copies:
            cp.start()
        for cp in copies:
            cp.wait()

    return pl.pallas_call(
        body, name=name,
        out_shape=[jax.ShapeDtypeStruct((n_s, m, n), big.dtype)] + [jax.ShapeDtypeStruct(a.shape, a.dtype) for a in whole],
        in_specs=[_ANY] * (1 + len(whole)), out_specs=[_ANY] * (1 + len(whole)),
        scratch_shapes=[pltpu.SemaphoreType.DMA((n_copies,)), pltpu.SemaphoreType.DMA((n_copies,))],
    )(big, *whole)


def _pair_swap(arrs, name):
    k = len(arrs)

    def body(*refs):
        send_sems, recv_sems = refs[-2:]
        x, y, c = _place()
        copies = [pltpu.make_async_remote_copy(
            src_ref=refs[i], dst_ref=refs[k + i], send_sem=send_sems.at[i], recv_sem=recv_sems.at[i],
            device_id=(x, y, 1 - c), device_id_type=MESH) for i in range(k)]
        for cp in copies:
            cp.start()
        for cp in copies:
            cp.wait()

    return pl.pallas_call(
        body, name=name, out_shape=[jax.ShapeDtypeStruct(a.shape, a.dtype) for a in arrs], in_specs=[_ANY] * k,
        out_specs=[_ANY] * k, scratch_shapes=[pltpu.SemaphoreType.DMA((k,)), pltpu.SemaphoreType.DMA((k,))],
    )(*arrs)


_HBM = pl.BlockSpec(memory_space=pltpu.HBM)
_SEM = pl.BlockSpec(memory_space=pltpu.SEMAPHORE)
_EFFECT = pltpu.SideEffectType.DATAFLOW_SIDE_EFFECTING


def _split_copies(src_ref, land_ref, sems, plan, sending):
    n = len(sems) // 2
    return [pltpu.make_async_remote_copy(src_ref=s, dst_ref=d, send_sem=sems[k], recv_sem=sems[n + k], device_id=to,
                                         device_id_type=MESH)
            for k, (s, d, to) in enumerate(plan(src_ref, land_ref, sending))]


def _to_chips(src_at, land_at):
    def plan(src_ref, land_ref, sending):
        x, y, c = _place()
        return [(src_at(src_ref, tx, ty), land_at(land_ref, j, *((x, y) if sending else (tx, ty)), c), (tx, ty, c))
                for j, (tx, ty) in enumerate(_other_chips(x, y))]
    return plan


def _to_sibling(src_ref, land_ref, sending):
    x, y, c = _place()
    return [(src_ref.at[s, 1 - c], land_ref.at[s], (x, y, 1 - c)) for s in range(N_CHIPS)]


def _split_start(name, src, land, plan, n):
    def body(src_ref, land_ref, *outs):
        for cp in _split_copies(src_ref, land_ref, outs[:2 * n], plan, True):
            cp.start()
        outs[2 * n + 2][...] = jnp.zeros_like(outs[2 * n + 2])

    outs = pl.pallas_call(
        body, name=name,
        out_shape=(pltpu.SemaphoreType.DMA(()),) * (2 * n) + (
            pltpu.HBM(src.shape, src.dtype), pltpu.HBM(land.shape, land.dtype), jax.ShapeDtypeStruct((8, LANES), F32)),
        in_specs=(_HBM, _HBM), out_specs=(_SEM,) * (2 * n) + (_HBM, _HBM, pl.BlockSpec(memory_space=pltpu.VMEM)),
        input_output_aliases={0: 2 * n, 1: 2 * n + 1},
        compiler_params=pltpu.CompilerParams(has_side_effects=_EFFECT),
    )(pltpu.with_memory_space_constraint(src, pltpu.HBM), pltpu.with_memory_space_constraint(land, pltpu.HBM))
    return outs[:2 * n], outs[2 * n], outs[2 * n + 1], outs[2 * n + 2]


def _split_wait(name, sems, src, land, after, plan):
    def body(src_ref, land_ref, *rest):
        for cp in _split_copies(src_ref, land_ref, rest[:len(sems)], plan, False):
            cp.wait_send()
            cp.wait_recv()

    return pl.pallas_call(
        body, name=name, out_shape=(pltpu.HBM(src.shape, src.dtype), pltpu.HBM(land.shape, land.dtype)),
        in_specs=(_HBM, _HBM) + (_SEM,) * len(sems) + (_ANY,) * len(after), out_specs=(_HBM, _HBM),
        input_output_aliases={0: 0, 1: 1}, compiler_params=pltpu.CompilerParams(has_side_effects=_EFFECT),
    )(src, land, *sems, *after)


def _gather_finish(land, pack, m):
    def body(land_ref, pack_ref, out_ref, stage, send_sems, recv_sems, load_sems, store_sems):
        x, y, c = _place()

        def rows(px, py, pc, ref=out_ref):
            return ref.at[pl.ds((4 * px + 2 * py + pc) * m, m), :]

        copies = [pltpu.make_async_remote_copy(
            src_ref=rows(tx, ty, c, land_ref), dst_ref=rows(tx, ty, c), send_sem=send_sems.at[j], recv_sem=recv_sems.at[j],
            device_id=(x, y, 1 - c), device_id_type=MESH) for j, (tx, ty) in enumerate(_other_chips(x, y))]
        loads = [pltpu.make_async_copy(pack_ref.at[pl.ds(h * m, m), :], stage.at[h], load_sems.at[h]) for h in range(2)]
        stores = [pltpu.make_async_copy(stage.at[h], rows(x, y, h), store_sems.at[h]) for h in range(2)]
        for cp in copies + loads:
            cp.start()
        for h in range(2):
            loads[h].wait()
            stores[h].start()
        for j, (tx, ty) in enumerate(_other_chips(x, y)):
            copies[j].wait_send()
            pltpu.make_async_remote_copy(
                src_ref=rows(tx, ty, 1 - c), dst_ref=rows(tx, ty, 1 - c), send_sem=send_sems.at[j],
                recv_sem=recv_sems.at[j], device_id=(x, y, 1 - c), device_id_type=MESH).wait_recv()
        for cp in stores:
            cp.wait()

    return pl.pallas_call(
        body, name="gather_late_finish", out_shape=jax.ShapeDtypeStruct(land.shape, land.dtype),
        in_specs=[_ANY, _ANY], out_specs=_ANY, input_output_aliases={0: 0},
        scratch_shapes=[pltpu.VMEM((2, m, land.shape[1]), land.dtype), pltpu.SemaphoreType.DMA((3,)),
                        pltpu.SemaphoreType.DMA((3,)), pltpu.SemaphoreType.DMA((2,)), pltpu.SemaphoreType.DMA((2,))],
    )(land, pack)


def _row_tile(rows, cap=512):
    for t in range(cap - cap % 8, 7, -8):
        if rows % t == 0:
            return t
    return rows


def _elementwise(fn, n_out, name, *arrs, out_dtype=F32):
    rows, cols = arrs[0].shape
    tr = _row_tile(rows)
    n_in = len(arrs)

    def body(*refs):
        outs = fn(*[r[...].astype(F32) for r in refs[:n_in]])
        for r, o in zip(refs[n_in:], outs):
            r[...] = o.astype(out_dtype)

    spec = pl.BlockSpec((tr, cols), lambda i: (i, 0))
    return pl.pallas_call(
        body, grid=(rows // tr,), name=name, in_specs=[spec] * n_in, out_specs=[spec] * n_out,
        out_shape=[jax.ShapeDtypeStruct((rows, cols), out_dtype)] * n_out, compiler_params=_params("arbitrary"),
    )(*arrs)


def _pair_sum(gpack, rbig, ci, name):
    n_s, _, m, n = gpack.shape
    tr = _row_tile(m)

    def body(c_ref, g_ref, r_ref, o_ref):
        o_ref[...] = (g_ref[...] + r_ref[...]).astype(BF)

    return pl.pallas_call(
        body, name=name, out_shape=jax.ShapeDtypeStruct((n_s, m, n), BF),
        grid_spec=pltpu.PrefetchScalarGridSpec(
            num_scalar_prefetch=1, grid=(n_s, m // tr),
            in_specs=[pl.BlockSpec((None, None, tr, n), lambda s, i, c: (s, c[0], i, 0)),
                      pl.BlockSpec((None, tr, n), lambda s, i, c: (s, i, 0))],
            out_specs=pl.BlockSpec((None, tr, n), lambda s, i, c: (s, i, 0))),
        compiler_params=_params("arbitrary", "arbitrary"),
    )(ci.reshape(1), gpack, rbig)


def _chip_sum(sums, landed, chip, name):
    _, m, n = sums.shape
    tr = _row_tile(m)

    def body(c_ref, own_ref, r0_ref, r1_ref, r2_ref, o_ref):
        f = lambda r: r[...].astype(F32)
        o_ref[...] = _add4(f(own_ref), f(r0_ref), f(r1_ref), f(r2_ref))[0]

    slot = lambda j: pl.BlockSpec((None, tr, n), lambda i, c: (j, i, 0))
    return pl.pallas_call(
        body, name=name, out_shape=jax.ShapeDtypeStruct((m, n), F32),
        grid_spec=pltpu.PrefetchScalarGridSpec(
            num_scalar_prefetch=1, grid=(m // tr,),
            in_specs=[pl.BlockSpec((None, tr, n), lambda i, c: (c[0], i, 0)), slot(0), slot(1), slot(2)],
            out_specs=pl.BlockSpec((tr, n), lambda i, c: (i, 0))),
        compiler_params=_params("arbitrary"),
    )(chip.reshape(1), sums, landed, landed, landed)


def _add2(a, b):
    return (a + b,)


def _add4(own, r0, r1, r2):
    return ((own + r2) + (r0 + r1),)


def _adamw_math(w, g, m, v):
    m = ADAM_B1 * m + (1.0 - ADAM_B1) * g
    v = ADAM_B2 * v + (1.0 - ADAM_B2) * (g * g)
    m_hat = m / (1.0 - ADAM_B1 ** ADAM_STEP)
    v_hat = v / (1.0 - ADAM_B2 ** ADAM_STEP)
    delta = -ADAM_LR * (m_hat / (jnp.sqrt(v_hat) + ADAM_EPS) + ADAM_WD * w)
    return delta, m, v


WEIGHTS = ["meta_tokens", "ln1_g", "w_in", "q_a_norm_g", "w_uq", "kv_a_norm_g", "w_ukv", "q_norm_g", "k_norm_g",
           "conv_w", "conv_b", "lru_wa", "lru_ba", "lru_wi", "lru_bi", "lru_lambda", "attn_out_g", "rnn_out_g",
           "w_out", "ln2_g", "w_gate", "w_up", "w_down"]
BIG = ["w_in", "w_uq", "w_ukv", "w_out", "w_gate", "w_up", "w_down"]
BIG_T = {"w_in": True, "w_uq": True, "w_ukv": True, "w_out": False, "w_gate": True, "w_up": True, "w_down": False}
BIG_ROWS = {"w_in": 424, "w_uq": 72, "w_ukv": 64, "w_out": 256, "w_gate": 704, "w_up": 704, "w_down": 704}
EARLY = ["w_in", "w_uq", "w_ukv"]
LATE = ["w_out", "w_gate", "w_up", "w_down"]
EARLY_ROWS = 576
LATE_ROWS = 2368
SMALL_SHARDED = ["meta_tokens", "conv_w", "lru_ba", "lru_bi", "lru_lambda"]
SMALL = [n for n in WEIGHTS if n not in BIG]
SMALL_PACK_ROWS = 160
SMALL_ADAM_ROWS = 144


def _offsets(names):
    off, o = {}, 0
    for n in names:
        off[n] = o
        o += BIG_ROWS[n]
    return off


def _shard_pack(names, src, rows):
    parts = [_to_pack_piece(n, src[n]) for n in names]
    used = sum(BIG_ROWS[n] for n in names)
    if rows > used:
        parts.append(jnp.zeros((rows - used, D), F32))
    return jnp.concatenate(parts, axis=0)


def _grad_pack(names, g, rows):
    parts = [g[n].reshape(N_CHIPS, BIG_ROWS[n], D) for n in names]
    used = sum(BIG_ROWS[n] for n in names)
    if rows > used:
        parts.append(jnp.zeros((N_CHIPS, rows - used, D), F32))
    return jnp.concatenate(parts, axis=1).reshape(N_CHIPS, 2, rows // 2, D)


def _both_halves(mine, other, ci):
    return jnp.where(ci == 0, jnp.concatenate([mine, other], axis=0), jnp.concatenate([other, mine], axis=0))


def _to_pack_piece(name, shard):
    a = shard[0].T if BIG_T[name] else shard[0]
    return a.reshape(BIG_ROWS[name], D)


def _flat_pack(arrs, rows):
    flat = jnp.concatenate([a.reshape(-1) for a in arrs])
    return jnp.pad(flat, (0, rows * D - flat.shape[0])).reshape(rows, D)


def _flat_unpack(pack, shapes):
    flat, out, o = pack.reshape(-1), [], 0
    for s in shapes:
        n = math.prod(s)
        out.append(flat[o:o + n].reshape(s))
        o += n
    return out


def kernel(x, meta_tokens, ln1_g, w_in, q_a_norm_g, w_uq, kv_a_norm_g, w_ukv, q_norm_g, k_norm_g, conv_w, conv_b, lru_wa, lru_ba, lru_wi, lru_bi, lru_lambda, attn_out_g, rnn_out_g, w_out, ln2_g, w_gate, w_up, w_down, loss_target, m_meta_tokens, m_ln1_g, m_w_in, m_q_a_norm_g, m_w_uq, m_kv_a_norm_g, m_w_ukv, m_q_norm_g, m_k_norm_g, m_conv_w, m_conv_b, m_lru_wa, m_lru_ba, m_lru_wi, m_lru_bi, m_lru_lambda, m_attn_out_g, m_rnn_out_g, m_w_out, m_ln2_g, m_w_gate, m_w_up, m_w_down, v_meta_tokens, v_ln1_g, v_w_in, v_q_a_norm_g, v_w_uq, v_kv_a_norm_g, v_w_ukv, v_q_norm_g, v_k_norm_g, v_conv_w, v_conv_b, v_lru_wa, v_lru_ba, v_lru_wi, v_lru_bi, v_lru_lambda, v_attn_out_g, v_rnn_out_g, v_w_out, v_ln2_g, v_w_gate, v_w_up, v_w_down):
    wts = dict(zip(WEIGHTS, (meta_tokens, ln1_g, w_in, q_a_norm_g, w_uq, kv_a_norm_g, w_ukv, q_norm_g, k_norm_g, conv_w, conv_b, lru_wa, lru_ba, lru_wi, lru_bi, lru_lambda, attn_out_g, rnn_out_g, w_out, ln2_g, w_gate, w_up, w_down)))
    mom = dict(zip(WEIGHTS, (m_meta_tokens, m_ln1_g, m_w_in, m_q_a_norm_g, m_w_uq, m_kv_a_norm_g, m_w_ukv, m_q_norm_g, m_k_norm_g, m_conv_w, m_conv_b, m_lru_wa, m_lru_ba, m_lru_wi, m_lru_bi, m_lru_lambda, m_attn_out_g, m_rnn_out_g, m_w_out, m_ln2_g, m_w_gate, m_w_up, m_w_down)))
    var = dict(zip(WEIGHTS, (v_meta_tokens, v_ln1_g, v_w_in, v_q_a_norm_g, v_w_uq, v_kv_a_norm_g, v_w_ukv, v_q_norm_g, v_k_norm_g, v_conv_w, v_conv_b, v_lru_wa, v_lru_ba, v_lru_wi, v_lru_bi, v_lru_lambda, v_attn_out_g, v_rnn_out_g, v_w_out, v_ln2_g, v_w_gate, v_w_up, v_w_down)))
    xi, yi, ci = _place()
    chip = 2 * xi + yi
    off_e, off_l = _offsets(EARLY), _offsets(LATE)
    half_e, half_l = EARLY_ROWS // 2, LATE_ROWS // 2
    gather_plan = _to_chips(lambda ref, tx, ty: ref,
                            lambda ref, j, px, py, c: ref.at[pl.ds((4 * px + 2 * py + c) * half_l, half_l), :])
    scatter_plan = _to_chips(lambda ref, tx, ty: ref.at[2 * tx + ty], lambda ref, j, px, py, c: ref.at[j])

    pack_e = _shard_pack(EARLY, wts, EARLY_ROWS).astype(BF)
    spack = jnp.concatenate([meta_tokens[:, :LANES], meta_tokens[:, LANES:], conv_w[0], lru_ba[0], lru_bi[0],
                             lru_lambda[0], jnp.zeros((6, LANES), F32)], axis=0)
    (ge, gs), gathered = _all_gather([lax.dynamic_slice_in_dim(pack_e, ci * half_e, half_e, axis=0),
                                      lax.dynamic_slice_in_dim(spack, ci * 24, 24, axis=0)], "gather_early")
    ge = ge.reshape(N_CHIPS, EARLY_ROWS, D)
    gs = gs.reshape(N_CHIPS, 48, LANES)
    full = {n: ge[:, off_e[n]:off_e[n] + BIG_ROWS[n]] for n in EARLY}
    pack_l = (_shard_pack(LATE, wts, LATE_ROWS) + gathered[0, 0]).astype(BF)
    sems_l, src_l, land_l, tied = _split_start(
        "gather_late_start", lax.dynamic_slice_in_dim(pack_l, ci * half_l, half_l, axis=0),
        lax.empty((8 * half_l, D), BF), gather_plan, 3)

    def late_weights(after):
        _, land = _split_wait("gather_late_wait", sems_l, src_l, land_l, after, gather_plan)
        gl = _gather_finish(land, pack_l, half_l).reshape(N_CHIPS, LATE_ROWS, D)
        part = lambda n: gl[:, off_l[n]:off_l[n] + BIG_ROWS[n]].reshape(N_CHIPS * BIG_ROWS[n], D)
        return dict(w_out=part("w_out"), w_gate_t=part("w_gate"), w_up_t=part("w_up"), w_down=part("w_down"))

    pair, late = {}, {}

    def early_grads(g_late):
        gpack = _grad_pack(LATE, g_late, LATE_ROWS)
        pair["sems"], pair["src"], pair["land"], zeros = _split_start(
            "grad_pair_late_start", gpack, lax.empty((N_CHIPS, half_l, D), F32), _to_sibling, N_CHIPS)
        return zeros[0, 0]

    def mid_grads(after):
        gpack, rbig = _split_wait("grad_pair_late_wait", pair["sems"], pair["src"], pair["land"], after, _to_sibling)
        chip_big = _pair_sum(gpack, rbig, ci, "grad_pair_sum_late")
        late["sems"], late["src"], late["land"], zeros = _split_start(
            "grad_chip_late_start", chip_big, lax.empty((3, half_l, D), BF), scatter_plan, 3)
        return zeros[0, 0]

    cols = lambda a: a.transpose(1, 0, 2).reshape(a.shape[1], N_CHIPS * a.shape[2])
    meta_full = cols(jnp.concatenate([gs[:, 0:16], gs[:, 16:32]], axis=2))
    w = dict(
        w_in_t=full["w_in"].reshape(IN_COLS, D), w_uq_t=full["w_uq"].reshape(N_HEADS * QK_HEAD, Q_LORA),
        w_ukv_t=full["w_ukv"].reshape(2 * D_ATTN, KV_LORA),
        ln1_g=ln1_g, q_a_norm_g=q_a_norm_g, kv_a_norm_g=kv_a_norm_g, q_norm_g=q_norm_g, k_norm_g=k_norm_g,
        conv_w=cols(gs[:, 32:36]), conv_b=conv_b, lru_wa=lru_wa[0], lru_ba=cols(gs[:, 36:38]), lru_wi=lru_wi[0],
        lru_bi=cols(gs[:, 38:40]), lru_lambda=cols(gs[:, 40:42]), attn_out_g=attn_out_g, rnn_out_g=rnn_out_g,
        ln2_g=ln2_g,
    )

    loss_local, grad_x, g, last = _local_step(x, loss_target, meta_full + tied[0, 0], w, late_weights, early_grads,
                                              mid_grads)

    gpack = _grad_pack(EARLY, {"w_in": g["w_in_t"], "w_uq": g["w_uq_t"], "w_ukv": g["w_ukv_t"]}, EARLY_ROWS)
    full_shapes = {n: wts[n].shape for n in SMALL}
    full_shapes.update(meta_tokens=(N_META, D), conv_w=(1, CONV_W, D_RNN), lru_ba=(1, 2, D_RNN), lru_bi=(1, 2, D_RNN),
                       lru_lambda=(1, 2, D_RNN))
    gsmall = _flat_pack([g[n] for n in SMALL] + [loss_local], SMALL_PACK_ROWS)
    rbig, rsmall = _pair_exchange(gpack, [gsmall], "grad_pair_exchange")
    chip_big = _pair_sum(gpack, rbig, ci, "grad_pair_sum")
    (chip_small,) = _elementwise(_add2, 1, "grad_pair_sum_small", gsmall, rsmall)
    everywhere = _to_chips(lambda ref, tx, ty: ref, lambda ref, j, px, py, c: ref.at[j])
    sems_e, src_e, land_e, zero_e = _split_start(
        "grad_chip_early_start", chip_big, lax.empty((3, half_e, D), BF), scatter_plan, 3)
    sems_s, src_s, land_s, zero_s = _split_start(
        "grad_small_start", chip_small, lax.empty((3, SMALL_PACK_ROWS, D), F32), everywhere, 3)

    grads, delta, new_m, new_v = {}, {}, {}, {}

    def adamw_big(names, offsets, gshard):
        for n in names:
            _, k, cols = wts[n].shape
            as_rows = (lambda a: a[0].T) if BIG_T[n] else (lambda a: a[0])
            back = (lambda a: a.T[None]) if BIG_T[n] else (lambda a: a[None])
            g2 = gshard[offsets[n]:offsets[n] + BIG_ROWS[n]].reshape((cols, k) if BIG_T[n] else (k, cols))
            d_, m_, v_ = _elementwise(_adamw_math, 3, "adamw_" + n, as_rows(wts[n]), g2, as_rows(mom[n]), as_rows(var[n]))
            grads[n], delta[n], new_m[n], new_v[n] = back(g2), back(d_), back(m_), back(v_)
        return d_

    src, land = _split_wait("grad_chip_late_wait", late["sems"], late["src"], late["land"], last + [zero_e, zero_s],
                            scatter_plan)
    sum_l = _chip_sum(src, land, chip, "grad_chip_sum_late")
    (other_l,) = _pair_swap([sum_l], "grad_pair_swap_late")
    done_late = adamw_big(LATE, off_l, _both_halves(sum_l, other_l, ci))
    src_e, land_e = _split_wait("grad_chip_early_wait", sems_e, src_e, land_e, [done_late], scatter_plan)
    src_s, land_s = _split_wait("grad_small_wait", sems_s, src_s, land_s, [done_late], everywhere)
    sum_e = _chip_sum(src_e, land_e, chip, "grad_chip_sum")
    (small_sum,) = _elementwise(_add4, 1, "grad_chip_sum_small", src_s, land_s[0], land_s[1], land_s[2])
    (other_e,) = _pair_swap([sum_e], "grad_pair_swap_early")
    adamw_big(EARLY, off_e, _both_halves(sum_e, other_e, ci))
    *small_grads, loss = _flat_unpack(small_sum, [full_shapes[n] for n in SMALL] + [()])
    small_full = dict(zip(SMALL, small_grads))
    for n in SMALL:
        a = small_full[n]
        if n in SMALL_SHARDED:
            width = wts[n].shape[-1]
            a = lax.dynamic_slice_in_dim(a, chip * width, width, axis=a.ndim - 1)
        grads[n] = a.reshape(wts[n].shape)

    packs =[_flat_pack([src[n] for n in SMALL], SMALL_ADAM_ROWS) for src in (wts, grads, mom, var)]
    outs = _elementwise(_adamw_math, 3, "adamw_small", *packs)
    for dst, o in zip((delta, new_m, new_v), outs):
        dst.update(zip(SMALL, _flat_unpack(o, [wts[n].shape for n in SMALL])))

    return (loss, grad_x, *[grads[n] for n in WEIGHTS], *[delta[n] for n in WEIGHTS],
            *[new_m[n] for n in WEIGHTS], *[new_v[n] for n in WEIGHTS])
```

```python
import functools
import math

import jax
import jax.numpy as jnp
from jax import lax
from jax.experimental import pallas as pl
from jax.experimental.pallas import tpu as pltpu

F32 = jnp.float32
BF = jnp.bfloat16
MESH = pl.DeviceIdType.MESH

D = 1024
SEQ = 2048
N_META = 16
T = N_META + SEQ
N_HEADS = 8
QK_NOPE = 64
QK_ROPE = 32
QK_HEAD = 96
V_HEAD = 64
Q_LORA = 384
KV_LORA = 256
D_ATTN = 512
D_RNN = 512
RNN_BW = 64
CONV_W = 4
LRU_C = 8.0
ROPE_THETA = 10000.0
D_FF = 2816
EPS = 1e-6
IN_COLS = 1696
ADAM_LR, ADAM_B1, ADAM_B2, ADAM_EPS, ADAM_WD, ADAM_STEP = 0.001, 0.9, 0.999, 1e-08, 0.01, 10

LANES = 128
TP = 2176
NB = 2
R = NB * TP
TR = 256
TRF = 256
TQ = 544
HP = LANES
PC = 1792
O_CKV, O_KR, O_XR, O_XG = 384, 640, 768, 1280
CG = 128
N_CG = D_RNN // CG
VMEM_LIMIT = 56 * 1024 * 1024
N_CHIPS = 4
SCALE = QK_HEAD ** -0.5
KEY_MASK = -30000.0
LOG2_E = 1.4426950408889634
SCALE_LOG2 = SCALE * LOG2_E


def _nt(a, b):
    return lax.dot_general(a, b, (((1,), (1,)), ((), ())), preferred_element_type=F32)


def _nn(a, b):
    return jnp.dot(a, b, preferred_element_type=F32)


def _tn(a, b):
    return lax.dot_general(a, b, (((0,), (0,)), ((), ())), preferred_element_type=F32)


def _rms(x, g, n):
    ms = jnp.sum(x * x, axis=-1, keepdims=True) * (1.0 / n)
    return x * lax.rsqrt(ms + EPS) * g


def _rot_impl(x):
    lane = lax.broadcasted_iota(jnp.int32, x.shape, 1)
    left = pltpu.roll(x, HP - 16, 1)
    right = pltpu.roll(x, 16, 1)
    lo = (lane >= QK_NOPE) & (lane < QK_NOPE + 16)
    hi = (lane >= QK_NOPE + 16) & (lane < QK_HEAD)
    return jnp.where(lo, -left, jnp.where(hi, right, 0.0))


@jax.custom_vjp
def _rot(x):
    return _rot_impl(x)


def _rot_fwd(x):
    return _rot_impl(x), None


def _rot_bwd(_, g):
    return (-_rot_impl(g),)


_rot.defvjp(_rot_fwd, _rot_bwd)


def _head(x, g, cs, sn):
    n = _rms(x, g, QK_HEAD)
    return n * cs + _rot(n) * sn


def _head_bwd(x, g, cs, sn, dout):
    rs = lax.rsqrt(jnp.sum(x * x, axis=-1, keepdims=True) * (1.0 / QK_HEAD) + EPS)
    xh = x * rs
    dn = dout * cs - _rot_impl(dout * sn)
    gdn = g * dn
    t = jnp.sum(gdn * xh, axis=-1, keepdims=True) * (1.0 / QK_HEAD)
    return rs * (gdn - xh * t), jnp.sum(dn * xh, axis=0, keepdims=True)


def _const_spec(shape):
    return pl.BlockSpec(shape, lambda *_: (0,) * len(shape), pipeline_mode=pl.Buffered(1))


def _row_spec(n, tr=TR):
    return pl.BlockSpec((tr, n), lambda i: (i, 0))


def _params(*sem, vmem=VMEM_LIMIT):
    return pltpu.CompilerParams(dimension_semantics=sem, vmem_limit_bytes=vmem)


def _stage_a_fwd(hp, cs, sn, cw):
    def body(hp_ref, cs_ref, sn_ref, ln1, win, qag, wq, kvag, wk, wv, qg, kg,
             pa_ref, xr_ref, xg_ref, q_ref, k_ref, v_ref):
        hn = _rms(hp_ref[...], ln1[...], D).astype(BF)
        p = _nt(hn, win[...])
        pa_ref[...] = p[:, :O_XR]
        xr_ref[...] = p[:, O_XR:O_XG]
        xg_ref[...] = p[:, O_XG:]
        cqn = _rms(p[:, :O_CKV], qag[...], Q_LORA).astype(BF)
        ckvn = _rms(p[:, O_CKV:O_KR], kvag[...], KV_LORA).astype(BF)
        kr = p[:, O_KR:O_XR]
        c, s = cs_ref[...], sn_ref[...]
        mask_lane = lax.broadcasted_iota(jnp.int32, (1, HP), 1) == QK_HEAD
        row = pl.program_id(0) * TRF + lax.broadcasted_iota(jnp.int32, (TRF, 1), 0)
        key_mask = jnp.where(jnp.where(row >= TP, row - TP, row) < T, 0.0, KEY_MASK)
        qraw = _nt(cqn, wq[...])
        kraw = _nt(ckvn, wk[...])
        for h in range(N_HEADS):
            sl = slice(h * HP, (h + 1) * HP)
            q_ref[:, sl] = jnp.where(mask_lane, 1.0, _head(qraw[:, sl], qg[...], c, s)).astype(BF)
            k_ref[:, sl] = jnp.where(mask_lane, key_mask, _head(kraw[:, sl] + kr, kg[...], c, s)).astype(BF)
        v_ref[...] = _nt(ckvn, wv[...]).astype(BF)

    rs = lambda n: _row_spec(n, TRF)
    return pl.pallas_call(
        body, grid=(R // TRF,), name="stage_a_fwd",
        in_specs=[rs(D), rs(HP), rs(HP), _const_spec((1, D)), _const_spec((PC, D)),
                  _const_spec((1, Q_LORA)), _const_spec((N_HEADS * HP, Q_LORA)), _const_spec((1, KV_LORA)),
                  _const_spec((N_HEADS * HP, KV_LORA)), _const_spec((D_ATTN, KV_LORA)), _const_spec((1, HP)),
                  _const_spec((1, HP))],
        out_specs=[rs(O_XR), rs(D_RNN), rs(D_RNN), rs(N_HEADS * HP), rs(N_HEADS * HP), rs(D_ATTN)],
        out_shape=[jax.ShapeDtypeStruct((R, O_XR), F32), jax.ShapeDtypeStruct((R, D_RNN), F32),
                   jax.ShapeDtypeStruct((R, D_RNN), F32), jax.ShapeDtypeStruct((R, N_HEADS * HP), BF),
                   jax.ShapeDtypeStruct((R, N_HEADS * HP), BF), jax.ShapeDtypeStruct((R, D_ATTN), BF)],
        compiler_params=_params("arbitrary"),
    )(hp, cs, sn, cw["ln1_g"], cw["win"], cw["qa_g"], cw["wq"], cw["kva_g"], cw["wk"], cw["wv"], cw["q_g"], cw["k_g"])


def _stage_a_bwd(dq, dk, dv, dxr, dxg, dh1, hp, pa, cs, sn, cw):
    def body(dq_ref, dk_ref, dv_ref, dxr_ref, dxg_ref, dh1_ref, hp_ref, pa_ref, cs_ref, sn_ref,
             ln1, win, qag, wq, kvag, wk, wv, qg, kg,
             dhp_ref, dp_ref, dqraw_ref, dkraw_ref, hn_ref, cqn_ref, ckvn_ref,
             dln1_ref, dqag_ref, dkvag_ref, dqg_ref, dkg_ref):
        @pl.when(pl.program_id(0) == 0)
        def _():
            for r in (dln1_ref, dqag_ref, dkvag_ref, dqg_ref, dkg_ref):
                r[...] = jnp.zeros_like(r)

        hn, vjp_ln1 = jax.vjp(lambda h, g: _rms(h, g, D), hp_ref[...], ln1[...])
        hn_ref[...] = hn.astype(BF)
        pa_v = pa_ref[...]
        cqn, vjp_qa = jax.vjp(lambda x, g: _rms(x, g, Q_LORA), pa_v[:, :O_CKV], qag[...])
        ckvn, vjp_kva = jax.vjp(lambda x, g: _rms(x, g, KV_LORA), pa_v[:, O_CKV:O_KR], kvag[...])
        kr = pa_v[:, O_KR:O_XR]
        cqnb, ckvnb = cqn.astype(BF), ckvn.astype(BF)
        cqn_ref[...] = cqnb
        ckvn_ref[...] = ckvnb
        c, s = cs_ref[...], sn_ref[...]
        lane = lax.broadcasted_iota(jnp.int32, (1, HP), 1)
        rope_lanes = ((lane >= QK_NOPE) & (lane < QK_HEAD)).astype(F32)
        dkr = jnp.zeros((TR, HP), F32)
        dqg = jnp.zeros((1, HP), F32)
        dkg = jnp.zeros((1, HP), F32)
        qraw = _nt(cqnb, wq[...])
        kraw = _nt(ckvnb, wk[...])
        for h in range(N_HEADS):
            sl = slice(h * HP, (h + 1) * HP)
            dqraw, dg = _head_bwd(qraw[:, sl], qg[...], c, s, dq_ref[:, sl])
            dqg = dqg + dg
            dqraw_ref[:, sl] = dqraw.astype(BF)
            dkraw, dg = _head_bwd(kraw[:, sl] + kr, kg[...], c, s, dk_ref[:, sl])
            dkg = dkg + dg
            dkraw_ref[:, sl] = dkraw.astype(BF)
            dkr = dkr + dkraw * rope_lanes
        dcq, dqag = vjp_qa(_nn(dqraw_ref[...], wq[...]))
        dckv, dkvag = vjp_kva(_nn(dkraw_ref[...], wk[...]) + _nn(dv_ref[...].astype(BF), wv[...]))
        dpb = jnp.concatenate([dcq, dckv, dkr, dxr_ref[...], dxg_ref[...]], axis=1).astype(BF)
        dp_ref[...] = dpb
        dh, dln1 = vjp_ln1(_nn(dpb, win[...]))
        dhp_ref[...] = dh + dh1_ref[...]
        dln1_ref[...] += dln1
        dqag_ref[...] += dqag
        dkvag_ref[...] += dkvag
        dqg_ref[...] += dqg
        dkg_ref[...] += dkg

    acc = lambda n: pl.BlockSpec((1, n), lambda i: (0, 0))
    return pl.pallas_call(
        body, grid=(R // TR,), name="stage_a_bwd",
        in_specs=[_row_spec(N_HEADS * HP), _row_spec(N_HEADS * HP), _row_spec(D_ATTN), _row_spec(D_RNN),
                  _row_spec(D_RNN), _row_spec(D), _row_spec(D), _row_spec(O_XR), _row_spec(HP), _row_spec(HP),
                  _const_spec((1, D)), _const_spec((PC, D)), _const_spec((1, Q_LORA)),
                  _const_spec((N_HEADS * HP, Q_LORA)), _const_spec((1, KV_LORA)),
                  _const_spec((N_HEADS * HP, KV_LORA)), _const_spec((D_ATTN, KV_LORA)), _const_spec((1, HP)),
                  _const_spec((1, HP))],
        out_specs=[_row_spec(D), _row_spec(PC), _row_spec(N_HEADS * HP), _row_spec(N_HEADS * HP), _row_spec(D),
                   _row_spec(Q_LORA), _row_spec(KV_LORA), acc(D), acc(Q_LORA), acc(KV_LORA), acc(HP), acc(HP)],
        out_shape=[jax.ShapeDtypeStruct((R, D), F32), jax.ShapeDtypeStruct((R, PC), BF),
                   jax.ShapeDtypeStruct((R, N_HEADS * HP), BF), jax.ShapeDtypeStruct((R, N_HEADS * HP), BF),
                   jax.ShapeDtypeStruct((R, D), BF), jax.ShapeDtypeStruct((R, Q_LORA), BF),
                   jax.ShapeDtypeStruct((R, KV_LORA), BF), jax.ShapeDtypeStruct((1, D), F32),
                   jax.ShapeDtypeStruct((1, Q_LORA), F32), jax.ShapeDtypeStruct((1, KV_LORA), F32),
                   jax.ShapeDtypeStruct((1, HP), F32), jax.ShapeDtypeStruct((1, HP), F32)],
        compiler_params=_params("arbitrary"),
    )(dq, dk, dv, dxr, dxg, dh1, hp, pa, cs, sn, cw["ln1_g"], cw["win"], cw["qa_g"], cw["wq"], cw["kva_g"],
      cw["wk"], cw["wv"], cw["q_g"], cw["k_g"])


def _head_mask(half, dtype):
    lane = lax.broadcasted_iota(jnp.int32, (1, 2 * V_HEAD), 1)
    return ((lane >= V_HEAD) == (half == 1)).astype(dtype)


def _attn_specs(tq):
    n_q = TP // tq
    return (NB, N_HEADS // 2, n_q), dict(
        q=pl.BlockSpec((tq, 2 * HP), lambda b, j, i: (b * n_q + i, j)),
        k=pl.BlockSpec((TP, 2 * HP), lambda b, j, i: (b, j)),
        v=pl.BlockSpec((TP, 2 * V_HEAD), lambda b, j, i: (b, j)),
        o=pl.BlockSpec((tq, 2 * V_HEAD), lambda b, j, i: (b * n_q + i, j)),
        lse=pl.BlockSpec((None, tq, 2), lambda b, j, i: (j, b * n_q + i, 0)))


TQF = 1088


def _attn_fwd(q, k, v):
    def body(q_ref, k_ref, v_ref, o_ref, lse_ref):
        v2 = v_ref[...]
        o = jnp.zeros((TQF, 2 * V_HEAD), F32)
        lse = []
        for hh in range(2):
            sl = slice(hh * HP, (hh + 1) * HP)
            raw = _nt(q_ref[:, sl], k_ref[:, sl])
            m = jnp.max(raw, axis=-1, keepdims=True)
            e = jnp.exp2((raw - m) * SCALE_LOG2)
            l = jnp.sum(e, axis=-1, keepdims=True)
            o = o + _nn(e.astype(BF), v2 * _head_mask(hh, BF)) * (1.0 / l)
            lse.append(m * SCALE_LOG2 + jnp.log(l) * LOG2_E)
        o_ref[...] = o
        lane = lax.broadcasted_iota(jnp.int32, (TQF, 2), 1)
        lse_ref[...] = jnp.where(lane == 0, lse[0], lse[1])

    grid, sp = _attn_specs(TQF)
    return pl.pallas_call(
        body, grid=grid, name="attn_fwd", in_specs=[sp["q"], sp["k"], sp["v"]], out_specs=[sp["o"], sp["lse"]],
        out_shape=[jax.ShapeDtypeStruct((R, D_ATTN), F32), jax.ShapeDtypeStruct((N_HEADS // 2, R, 2), F32)],
        compiler_params=_params("arbitrary", "arbitrary", "arbitrary"),
    )(q, k, v)


def _attn_bwd(q, k, v, o, lse, do):
    def body(q_ref, k_ref, v_ref, o_ref, lse_ref, do_ref, dq_ref, dk_ref, dv_ref):
        @pl.when(pl.program_id(2) == 0)
        def _():
            dk_ref[...] = jnp.zeros_like(dk_ref)
            dv_ref[...] = jnp.zeros_like(dv_ref)

        do = do_ref[...]
        dob = do.astype(BF)
        do_o = do * o_ref[...]
        v2 = v_ref[...]
        dv_sum = jnp.zeros((TP, 2 * V_HEAD), F32)
        for hh in range(2):
            sl = slice(hh * HP, (hh + 1) * HP)
            qb, kb = q_ref[:, sl], k_ref[:, sl]
            p = jnp.exp2(_nt(qb, kb) * SCALE_LOG2 - lse_ref[:, hh:hh + 1])
            dp = _nt(dob, v2 * _head_mask(hh, BF))
            delta = jnp.sum(do_o * _head_mask(hh, F32), axis=-1, keepdims=True)
            dsb = (p * (dp - delta) * SCALE).astype(BF)
            dq_ref[:, sl] = _nn(dsb, kb)
            dk_ref[:, sl] += _tn(dsb, qb)
            dv_sum = dv_sum + _tn(p.astype(BF), dob) * _head_mask(hh, F32)
        dv_ref[...] += dv_sum

    grid, sp = _attn_specs(TQ)
    return pl.pallas_call(
        body, grid=grid, name="attn_bwd", in_specs=[sp["q"], sp["k"], sp["v"], sp["o"], sp["lse"], sp["o"]],
        out_specs=[sp["q"], sp["k"], sp["v"]],
        out_shape=[jax.ShapeDtypeStruct((R, N_HEADS * HP), F32), jax.ShapeDtypeStruct((R, N_HEADS * HP), F32),
                   jax.ShapeDtypeStruct((R, D_ATTN), F32)],
        compiler_params=_params("arbitrary", "arbitrary", "arbitrary"),
    )(q, k, v, o, lse, do)


def _tile_prefix(a_ref, b_ref, reverse):
    tiles = (TP // 8, 8, CG)
    r8 = lax.broadcasted_iota(jnp.int32, tiles, 1)
    a, b = a_ref[...].reshape(tiles), b_ref[...].reshape(tiles)
    for s in (1, 2, 4):
        shift = 8 - s if reverse else s
        keep = (r8 < 8 - s) if reverse else (r8 >= s)
        b = jnp.where(keep, a * pltpu.roll(b, shift, 1) + b, b)
        a = jnp.where(keep, a * pltpu.roll(a, shift, 1), a)
    a_ref[...] = a.reshape(TP, CG)
    b_ref[...] = b.reshape(TP, CG)


def _scan_pair(af_ref, bf_ref, hf_ref, ab_ref, bb_ref, hb_ref):
    _tile_prefix(af_ref, bf_ref, False)
    _tile_prefix(ab_ref, bb_ref, True)
    n_tiles = TP // 8

    def step(i, carry):
        cf, cb = carry
        rf = pl.multiple_of(i * 8, 8)
        rb = pl.multiple_of((n_tiles - 1 - i) * 8, 8)
        hf_ref[pl.ds(rf, 8), :] = bf_ref[pl.ds(rf, 8), :] + af_ref[pl.ds(rf, 8), :] * cf
        hb_ref[pl.ds(rb, 8), :] = bb_ref[pl.ds(rb, 8), :] + ab_ref[pl.ds(rb, 8), :] * cb
        cf = bf_ref[pl.ds(rf + 7, 1), :] + af_ref[pl.ds(rf + 7, 1), :] * cf
        cb = bb_ref[pl.ds(rb, 1), :] + ab_ref[pl.ds(rb, 1), :] * cb
        return cf, cb

    zero = jnp.zeros((1, CG), F32)
    lax.fori_loop(0, n_tiles, step, (zero, zero), unroll=8)


def _shifts(x):
    t = lax.broadcasted_iota(jnp.int32, x.shape, 0)
    xm2 = jnp.where(t >= 2, pltpu.roll(x, 2, 0), 0.0)
    xm1 = jnp.where(t >= 1, pltpu.roll(x, 1, 0), 0.0)
    xp1 = jnp.where(t < TP - 1, pltpu.roll(x, TP - 1, 0), 0.0)
    return xm2, xm1, xp1


def _softplus(z):
    e = jnp.exp(-jnp.abs(z))
    small = e * (1.0 - e * (0.5 - e * (1.0 / 3.0)))
    return jnp.maximum(z, 0.0) + jnp.where(e < 0.01, small, jnp.log(1.0 + e))


def _sigmoid(x):
    return 0.5 * jnp.tanh(0.5 * x) + 0.5


def _one_minus_sq(log_a, a):
    x = 2.0 * log_a
    series = -x * (1.0 + x * 0.5 * (1.0 + x * (1.0 / 3.0) * (1.0 + x * 0.25)))
    return jnp.where(x > -0.05, series, 1.0 - a * a)


def _gates(row0, xc, pa_f, pi_f, pa_b, pi_b, lam_f, lam_b):
    t = row0 + lax.broadcasted_iota(jnp.int32, xc.shape, 0)
    valid = t < T
    out = []
    for pa, pi_, lam in ((pa_f, pi_f, lam_f), (pa_b, pi_b, lam_b)):
        r = _sigmoid(pa)
        gate_i = _sigmoid(pi_)
        log_a = -LRU_C * r * _softplus(-lam)
        a = jnp.exp(log_a)
        mult = jnp.sqrt(jnp.maximum(_one_minus_sq(log_a, a), 0.0))
        out += [a, jnp.where(valid, mult * (gate_i * xc), 0.0)]
    return tuple(out)


def _gates_bwd(row0, xc, pres, lams, cots):
    t = row0 + lax.broadcasted_iota(jnp.int32, xc.shape, 0)
    valid = t < T
    dxc = jnp.zeros_like(xc)
    dpres, dlams = [], []
    for d in range(2):
        pa, pi_, lam = pres[2 * d], pres[2 * d + 1], lams[d]
        da, db = cots[2 * d], jnp.where(valid, cots[2 * d + 1], 0.0)
        r = _sigmoid(pa)
        gate_i = _sigmoid(pi_)
        sp = _softplus(-lam)
        log_a = -LRU_C * r * sp
        a = jnp.exp(log_a)
        m2 = jnp.maximum(_one_minus_sq(log_a, a), 0.0)
        mult = jnp.sqrt(m2)
        dxc = dxc + db * (mult * gate_i)
        d_gate = db * (mult * xc)
        d_m2 = jnp.where(m2 > 0.0, db * (gate_i * xc) * (0.5 * lax.rsqrt(m2)), 0.0)
        d_log_a = da * a - 2.0 * d_m2 * (a * a)
        dpres += [d_log_a * (-LRU_C * sp) * (r * (1.0 - r)), d_gate * (gate_i * (1.0 - gate_i))]
        d_sp = jnp.sum(d_log_a * (-LRU_C * r), axis=0, keepdims=True)
        dlams.append(-d_sp * jax.nn.sigmoid(-lam))
    return dxc, dpres, dlams


def _rnn_specs():
    seq = pl.BlockSpec((TP, CG), lambda g, b: (b, g))
    return dict(
        seq=seq,
        cw=pl.BlockSpec((CONV_W, CG), lambda g, b: (0, g)),
        cb=pl.BlockSpec((1, CG), lambda g, b: (0, g)),
        w4=pl.BlockSpec((None, CG, 4 * CG), lambda g, b: (g, 0, 0)),
        b4=pl.BlockSpec((None, 1, 4 * CG), lambda g, b: (g, 0, 0)),
        lam=pl.BlockSpec((None, 1, 2 * CG), lambda g, b: (g, 0, 0)),
    )


def _conv(x, xm2, xm1, xp1, cw_ref, cb_ref):
    return cw_ref[0:1, :] * xm2 + cw_ref[1:2, :] * xm1 + cw_ref[2:3, :] * x + cw_ref[3:4, :] * xp1 + cb_ref[...]


TC = 128
N_TC = TP // TC


def _split4(pre):
    return pre[:, :CG], pre[:, CG:2 * CG], pre[:, 2 * CG:3 * CG], pre[:, 3 * CG:]


def _rnn_fwd(xr, xg, cw):
    def body(xr_ref, xg_ref, cw_ref, cb_ref, w4_ref, b4_ref, lam_ref, y_ref, hf_ref, hb_ref, xc_s, af, bf, ab, bb):
        x = xr_ref[...]
        xc_s[...] = _conv(x, *_shifts(x), cw_ref, cb_ref)
        lam = lam_ref[...]

        def chunk(i, _):
            rows = pl.ds(pl.multiple_of(i * TC, TC), TC)
            xc = xc_s[rows, :]
            pre = _nn(xc.astype(BF), w4_ref[...]) + b4_ref[...]
            a_f, b_f, a_b, b_b = _gates(i * TC, xc, *_split4(pre), lam[:, :CG], lam[:, CG:])
            af[rows, :] = a_f
            bf[rows, :] = b_f
            ab[rows, :] = a_b
            bb[rows, :] = b_b
            return 0

        lax.fori_loop(0, N_TC, chunk, 0)
        _scan_pair(af, bf, hf_ref, ab, bb, hb_ref)
        y_ref[...] = (hf_ref[...] + hb_ref[...]) * jax.nn.gelu(xg_ref[...])

    sp = _rnn_specs()
    return pl.pallas_call(
        body, grid=(N_CG, NB), name="rnn_fwd",
        in_specs=[sp["seq"], sp["seq"], sp["cw"], sp["cb"], sp["w4"], sp["b4"], sp["lam"]],
        out_specs=[sp["seq"]] * 3, out_shape=[jax.ShapeDtypeStruct((R, D_RNN), F32)] * 3,
        scratch_shapes=[pltpu.VMEM((TP, CG), F32)] * 5,
        compiler_params=_params("arbitrary", "arbitrary"),
    )(xr, xg, cw["conv_w"], cw["conv_b"], cw["w4"], cw["b4"], cw["lam"])


def _rnn_bwd(dy, xr, xg, hf, hb, cw):
    def body(dy_ref, xr_ref, xg_ref, hf_ref, hb_ref, cw_ref, cb_ref, w4_ref, b4_ref, lam_ref,
             dxr_ref, dxg_ref, dcw_ref, dcb_ref, dw4_ref, db4_ref, dlam_ref,
             xc_s, af_s, ab_s, dhs_s, dhs2_s, lf_s, lb_s, daf_s, dab_s, dxc_s):
        @pl.when(pl.program_id(1) == 0)
        def _():
            for r in (dcw_ref, dcb_ref, dw4_ref, db4_ref, dlam_ref):
                r[...] = jnp.zeros_like(r)

        x = xr_ref[...]
        xc_s[...] = _conv(x, *_shifts(x), cw_ref, cb_ref)
        lam = lam_ref[...]

        def chunk1(i, _):
            rows = pl.ds(pl.multiple_of(i * TC, TC), TC)
            xc = xc_s[rows, :]
            pre = _nn(xc.astype(BF), w4_ref[...]) + b4_ref[...]
            a_f, _, a_b, _ = _gates(i * TC, xc, *_split4(pre), lam[:, :CG], lam[:, CG:])
            af_s[rows, :] = a_f
            ab_s[rows, :] = a_b
            _, vjp_y = jax.vjp(lambda h, g: h * jax.nn.gelu(g), hf_ref[rows, :] + hb_ref[rows, :], xg_ref[rows, :])
            dhs, dxg = vjp_y(dy_ref[rows, :])
            dhs_s[rows, :] = dhs
            dhs2_s[rows, :] = dhs
            dxg_ref[rows, :] = dxg
            return 0

        lax.fori_loop(0, N_TC, chunk1, 0)
        t = lax.broadcasted_iota(jnp.int32, (TP, CG), 0)
        af_s[...] = pltpu.roll(af_s[...], TP - 1, 0)
        ab_s[...] = pltpu.roll(ab_s[...], 1, 0)
        _scan_pair(ab_s, dhs_s, lb_s, af_s, dhs2_s, lf_s)
        daf_s[...] = lf_s[...] * jnp.where(t >= 1, pltpu.roll(hf_ref[...], 1, 0), 0.0)
        dab_s[...] = lb_s[...] * jnp.where(t < TP - 1, pltpu.roll(hb_ref[...], TP - 1, 0), 0.0)

        def chunk2(i, _):
            rows = pl.ds(pl.multiple_of(i * TC, TC), TC)
            xc = xc_s[rows, :]
            xcb = xc.astype(BF)
            pre = _nn(xcb, w4_ref[...]) + b4_ref[...]
            dxc, dpres, dlams = _gates_bwd(i * TC, xc, _split4(pre), (lam[:, :CG], lam[:, CG:]),
                                           (daf_s[rows, :], lf_s[rows, :], dab_s[rows, :], lb_s[rows, :]))
            dpre = jnp.concatenate(dpres, axis=1)
            dpreb = dpre.astype(BF)
            dxc_s[rows, :] = dxc + _nt(dpreb, w4_ref[...])
            dw4_ref[...] += _tn(xcb, dpreb)
            db4_ref[...] += jnp.sum(dpre, axis=0, keepdims=True)
            dlam_ref[...] += jnp.concatenate(dlams, axis=1)
            return 0

        lax.fori_loop(0, N_TC, chunk2, 0)
        dxc = dxc_s[...]
        dcb_ref[...] += jnp.sum(dxc, axis=0, keepdims=True)
        for tap, xs in enumerate(_shifts(x)[:2] + (x,) + _shifts(x)[2:]):
            dcw_ref[tap:tap + 1, :] += jnp.sum(xs * dxc, axis=0, keepdims=True)
        dxr_ref[...] = (cw_ref[0:1, :] * jnp.where(t < TP - 2, pltpu.roll(dxc, TP - 2, 0), 0.0)
                        + cw_ref[1:2, :] * jnp.where(t < TP - 1, pltpu.roll(dxc, TP - 1, 0), 0.0)
                        + cw_ref[2:3, :] * dxc
                        + cw_ref[3:4, :] * jnp.where(t >= 1, pltpu.roll(dxc, 1, 0), 0.0))

    sp = _rnn_specs()
    return pl.pallas_call(
        body, grid=(N_CG, NB), name="rnn_bwd",
        in_specs=[sp["seq"]] * 5 + [sp["cw"], sp["cb"], sp["w4"], sp["b4"], sp["lam"]],
        out_specs=[sp["seq"], sp["seq"], sp["cw"], sp["cb"], sp["w4"], sp["b4"], sp["lam"]],
        out_shape=[jax.ShapeDtypeStruct((R, D_RNN), F32), jax.ShapeDtypeStruct((R, D_RNN), F32),
                   jax.ShapeDtypeStruct((CONV_W, D_RNN), F32), jax.ShapeDtypeStruct((1, D_RNN), F32),
                   jax.ShapeDtypeStruct((N_CG, CG, 4 * CG), F32), jax.ShapeDtypeStruct((N_CG, 1, 4 * CG), F32),
                   jax.ShapeDtypeStruct((N_CG, 1, 2 * CG), F32)],
        scratch_shapes=[pltpu.VMEM((TP, CG), F32)] * 10,
        compiler_params=_params("arbitrary", "arbitrary"),
    )(dy, xr, xg, hf, hb, cw["conv_w"], cw["conv_b"], cw["w4"], cw["b4"], cw["lam"])


TD = 256
STAGE_D_VMEM = 58 * 1024 * 1024


def _stage_d(hp, o, y, tgt, cw):
    def body(hp_ref, o_ref, y_ref, tgt_ref, ga, gr, wout, ln2, wg, wu, wd,
             do_ref, dy_ref, dh1_ref, mix_ref, dh1b_ref, hn2_ref, dg_ref, du_ref, act_ref, dh2b_ref,
             loss_ref, dga_ref, dgr_ref, dln2_ref):
        i = pl.program_id(0)

        @pl.when(i == 0)
        def _():
            for r in (loss_ref, dga_ref, dgr_ref, dln2_ref):
                r[...] = jnp.zeros_like(r)

        mix_a, vjp_a = jax.vjp(lambda x, g: _rms(x, g, D_ATTN), o_ref[...], ga[...])
        mix_r, vjp_r = jax.vjp(lambda x, g: _rms(x, g, D_RNN), y_ref[...], gr[...])
        mab, mrb = mix_a.astype(BF), mix_r.astype(BF)
        mix_ref[:, :D_ATTN] = mab
        mix_ref[:, D_ATTN:] = mrb
        h1 = hp_ref[...] + _nn(mab, wout[:D_ATTN, :]) + _nn(mrb, wout[D_ATTN:, :])
        hn2, vjp_ln2 = jax.vjp(lambda x, g: _rms(x, g, D), h1, ln2[...])
        hn2b = hn2.astype(BF)
        hn2_ref[...] = hn2b
        act, vjp_act = jax.vjp(lambda g, u: jax.nn.silu(g) * u, _nt(hn2b, wg[...]), _nt(hn2b, wu[...]))
        actb = act.astype(BF)
        act_ref[...] = actb
        h2 = h1 + _nn(actb, wd[...])
        row = i * TD + lax.broadcasted_iota(jnp.int32, (TD, 1), 0)
        t = jnp.where(row >= TP, row - TP, row)
        err = jnp.where((t >= N_META) & (t < T), h2 - tgt_ref[...], 0.0)
        loss_ref[...] += jnp.sum(err * err) * (0.5 / D)
        dh2b = (err * (1.0 / D)).astype(BF)
        dh2b_ref[...] = dh2b
        dg, du = vjp_act(_nt(dh2b, wd[...]))
        dgb, dub = dg.astype(BF), du.astype(BF)
        dg_ref[...] = dgb
        du_ref[...] = dub
        dh1n, dln2 = vjp_ln2(_nn(dgb, wg[...]) + _nn(dub, wu[...]))
        dh1 = err * (1.0 / D) + dh1n
        dh1_ref[...] = dh1
        dh1b = dh1.astype(BF)
        dh1b_ref[...] = dh1b
        dmix = _nt(dh1b, wout[...])
        do, dga = vjp_a(dmix[:, :D_ATTN])
        dyr, dgr = vjp_r(dmix[:, D_ATTN:])
        do_ref[...] = do
        dy_ref[...] = dyr
        dga_ref[...] += dga
        dgr_ref[...] += dgr
        dln2_ref[...] += dln2

    rs = lambda n: _row_spec(n, TD)
    acc = lambda n: pl.BlockSpec((1, n), lambda i: (0, 0))
    return pl.pallas_call(
        body, grid=(R // TD,), name="stage_d",
        in_specs=[rs(D), rs(D_ATTN), rs(D_RNN), rs(D), _const_spec((1, D_ATTN)), _const_spec((1, D_RNN)),
                  _const_spec((D, D)), _const_spec((1, D)), _const_spec((D_FF, D)), _const_spec((D_FF, D)),
                  _const_spec((D_FF, D))],
        out_specs=[rs(D_ATTN), rs(D_RNN), rs(D), rs(D), rs(D), rs(D), rs(D_FF), rs(D_FF), rs(D_FF), rs(D),
                   acc(1), acc(D_ATTN), acc(D_RNN), acc(D)],
        out_shape=[jax.ShapeDtypeStruct((R, D_ATTN), F32), jax.ShapeDtypeStruct((R, D_RNN), F32),
                   jax.ShapeDtypeStruct((R, D), F32), jax.ShapeDtypeStruct((R, D), BF),
                   jax.ShapeDtypeStruct((R, D), BF), jax.ShapeDtypeStruct((R, D), BF),
                   jax.ShapeDtypeStruct((R, D_FF), BF), jax.ShapeDtypeStruct((R, D_FF), BF),
                   jax.ShapeDtypeStruct((R, D_FF), BF), jax.ShapeDtypeStruct((R, D), BF),
                   jax.ShapeDtypeStruct((1, 1), F32), jax.ShapeDtypeStruct((1, D_ATTN), F32),
                   jax.ShapeDtypeStruct((1, D_RNN), F32), jax.ShapeDtypeStruct((1, D), F32)],
        compiler_params=_params("arbitrary", vmem=STAGE_D_VMEM),
    )(hp, o, y, tgt, cw["ga"], cw["gr"], cw["wout"], cw["ln2_g"], cw["wg"], cw["wu"], cw["wd"])


TW = 2176


def _wgrad(a, b, name, tk=None):
    ka, nb = a.shape[1], b.shape[1]
    tk = ka if tk is None else tk

    def body(a_ref, b_ref, o_ref):
        @pl.when(pl.program_id(1) == 0)
        def _():
            o_ref[...] = jnp.zeros_like(o_ref)

        o_ref[...] += _tn(a_ref[...].astype(BF), b_ref[...].astype(BF))

    return pl.pallas_call(
        body, grid=(ka // tk, R // TW), name=name,
        in_specs=[pl.BlockSpec((TW, tk), lambda k, r: (r, k)), pl.BlockSpec((TW, nb), lambda k, r: (r, 0))],
        out_specs=pl.BlockSpec((tk, nb), lambda k, r: (k, 0)),
        out_shape=jax.ShapeDtypeStruct((ka, nb), F32),
        compiler_params=_params("arbitrary", "arbitrary"),
    )(a, b)


def _rope_tables():
    half = QK_ROPE // 2
    freqs = 1.0 / (ROPE_THETA ** (jnp.arange(half, dtype=F32) / half))
    ang = jnp.arange(TP, dtype=F32)[:, None] * freqs[None, :]
    ones = jnp.ones((TP, QK_NOPE), F32)
    zeros = jnp.zeros((TP, QK_NOPE), F32)
    pad1 = jnp.ones((TP, HP - QK_HEAD), F32)
    pad0 = jnp.zeros((TP, HP - QK_HEAD), F32)
    cs = jnp.concatenate([ones, jnp.cos(ang), jnp.cos(ang), pad1], axis=1)
    sn = jnp.concatenate([zeros, jnp.sin(ang), jnp.sin(ang), pad0], axis=1)
    return jnp.tile(cs, (NB, 1)), jnp.tile(sn, (NB, 1))


def _pad_rows(a, lo, hi):
    return jnp.pad(a, ((0, 0), (lo, hi), (0, 0)))


def _compute_weights(w):
    win_t = w["w_in_t"]
    kr = win_t[O_KR:O_KR + QK_ROPE]
    win = jnp.concatenate([win_t[:O_KR], jnp.zeros((QK_NOPE, D), F32), kr,
                           jnp.zeros((HP - QK_HEAD, D), F32), win_t[O_KR + QK_ROPE:]], axis=0)
    wq = _pad_rows(w["w_uq_t"].reshape(N_HEADS, QK_HEAD, Q_LORA), 0, HP - QK_HEAD)
    wkv = w["w_ukv_t"].reshape(N_HEADS, QK_NOPE + V_HEAD, KV_LORA)
    wk = _pad_rows(wkv[:, :QK_NOPE], 0, HP - QK_NOPE)
    wv = wkv[:, QK_NOPE:].reshape(D_ATTN, KV_LORA)
    gates = jnp.stack([w["lru_wa"][0], w["lru_wi"][0], w["lru_wa"][1], w["lru_wi"][1]])
    blk = gates.reshape(4, N_CG, 2, RNN_BW, RNN_BW)
    dense = jnp.einsum("tcaij,ab->tcaibj", blk, jnp.eye(2, dtype=F32)).reshape(4, N_CG, CG, CG)
    w4 = dense.transpose(1, 2, 0, 3).reshape(N_CG, CG, 4 * CG)
    bias = jnp.stack([w["lru_ba"][0], w["lru_bi"][0], w["lru_ba"][1], w["lru_bi"][1]])
    b4 = bias.reshape(4, N_CG, CG).transpose(1, 0, 2).reshape(N_CG, 1, 4 * CG)
    lam = w["lru_lambda"].reshape(2, N_CG, CG).transpose(1, 0, 2).reshape(N_CG, 1, 2 * CG)
    pad_g = lambda g: jnp.pad(g.reshape(1, QK_HEAD), ((0, 0), (0, HP - QK_HEAD)))
    return dict(
        ln1_g=w["ln1_g"].reshape(1, D), win=win.astype(BF), qa_g=w["q_a_norm_g"].reshape(1, Q_LORA),
        wq=wq.astype(BF).reshape(N_HEADS * HP, Q_LORA), kva_g=w["kv_a_norm_g"].reshape(1, KV_LORA),
        wk=wk.astype(BF).reshape(N_HEADS * HP, KV_LORA), wv=wv.astype(BF),
        q_g=pad_g(w["q_norm_g"]), k_g=pad_g(w["k_norm_g"]),
        conv_w=w["conv_w"].reshape(CONV_W, D_RNN), conv_b=w["conv_b"].reshape(1, D_RNN),
        w4=w4.astype(BF), b4=b4, lam=lam,
        ga=w["attn_out_g"].reshape(1, D_ATTN), gr=w["rnn_out_g"].reshape(1, D_RNN), ln2_g=w["ln2_g"].reshape(1, D),
    )


def _local_step(x, target, meta, w, late_weights, early_grads, mid_grads):
    cw = _compute_weights(w)
    cs, sn = _rope_tables()
    hp = jnp.concatenate([jnp.broadcast_to(meta[None], (NB, N_META, D)), x,
                          jnp.zeros((NB, TP - T, D), F32)], axis=1).reshape(R, D)
    tgt = _pad_rows(target, N_META, TP - T).reshape(R, D)

    pa, xr, xg, q, k, v = _stage_a_fwd(hp, cs, sn, cw)
    o, lse = _attn_fwd(q, k, v)
    y, hf, hb = _rnn_fwd(xr, xg, cw)
    late = late_weights([o, y])
    cw.update(wout=late["w_out"], wg=late["w_gate_t"], wu=late["w_up_t"], wd=late["w_down"])
    (do, dy, dh1, mixb, dh1b, hn2b, dgb, dub, actb, dh2b, loss, dga, dgr, dln2) = _stage_d(hp, o, y, tgt, cw)
    dwout = _wgrad(mixb, dh1b, "wgrad_out")
    dwg = _wgrad(dgb, hn2b, "wgrad_gate", tk=D_FF // 2)
    dwu = _wgrad(dub, hn2b, "wgrad_up", tk=D_FF // 2)
    dwd = _wgrad(actb, dh2b, "wgrad_down", tk=D_FF // 2)
    zero = early_grads(dict(w_out=dwout, w_gate=dwg, w_up=dwu, w_down=dwd))
    cw["conv_b"] = cw["conv_b"] + zero
    dxr, dxg, dcw, dcb, dw4, db4, dlam = _rnn_bwd(dy, xr, xg, hf, hb, cw)
    zero = mid_grads([dxr])
    dq, dk, dv = _attn_bwd(q, k, v, o, lse, do)
    (dhp, dpb, dqrawb, dkrawb, hn1b, cqnb, ckvnb, dln1, dqag, dkvag, dqg, dkg) = _stage_a_bwd(
        dq, dk, dv, dxr, dxg, dh1, hp, pa, cs, sn, dict(cw, qa_g=cw["qa_g"] + zero))

    dwin = _wgrad(dpb, hn1b, "wgrad_in", tk=PC // 2)
    dwq = _wgrad(dqrawb, cqnb, "wgrad_uq")
    dwk = _wgrad(dkrawb, ckvnb, "wgrad_uk")
    dwv = _wgrad(dv, ckvnb, "wgrad_uv")

    dwin_t = jnp.concatenate([dwin[:O_KR], dwin[O_KR + QK_NOPE:O_KR + QK_HEAD], dwin[O_XR:]], axis=0)
    dwq_t = dwq.reshape(N_HEADS, HP, Q_LORA)[:, :QK_HEAD].reshape(N_HEADS * QK_HEAD, Q_LORA)
    dwkv_t = jnp.concatenate([dwk.reshape(N_HEADS, HP, KV_LORA)[:, :QK_NOPE],
                              dwv.reshape(N_HEADS, V_HEAD, KV_LORA)], axis=1).reshape(2 * D_ATTN, KV_LORA)
    d4 = dw4.reshape(N_CG, 2, RNN_BW, 4, 2, RNN_BW)
    dgates = jnp.stack([d4[:, 0, :, :, 0, :], d4[:, 1, :, :, 1, :]], axis=1)
    dgates = dgates.transpose(3, 0, 1, 2, 4).reshape(4, N_HEADS, RNN_BW, RNN_BW)
    dbias = db4.reshape(N_CG, 4, CG).transpose(1, 0, 2).reshape(4, D_RNN)
    dhp3 = dhp.reshape(NB, TP, D)
    grads = dict(
        meta_tokens=jnp.sum(dhp3[:, :N_META], axis=0),
        ln1_g=dln1, w_in_t=dwin_t, q_a_norm_g=dqag, w_uq_t=dwq_t, kv_a_norm_g=dkvag, w_ukv_t=dwkv_t,
        q_norm_g=dqg[:, :QK_HEAD], k_norm_g=dkg[:, :QK_HEAD], conv_w=dcw[None], conv_b=dcb,
        lru_wa=jnp.stack([dgates[0], dgates[2]])[None], lru_ba=jnp.stack([dbias[0], dbias[2]])[None],
        lru_wi=jnp.stack([dgates[1], dgates[3]])[None], lru_bi=jnp.stack([dbias[1], dbias[3]])[None],
        lru_lambda=dlam.reshape(N_CG, 2, CG).transpose(1, 0, 2).reshape(1, 2, D_RNN),
        attn_out_g=dga, rnn_out_g=dgr, ln2_g=dln2,
    )
    return loss[0, 0], dhp3[:, N_META:T], grads, [dhp, dwin]


_ANY = pl.BlockSpec(memory_space=pl.ANY)


def _place():
    return lax.axis_index("x"), lax.axis_index("y"), lax.axis_index("c")


def _other_chips(x, y):
    return [(1 - x, y), (x, 1 - y), (1 - x, 1 - y)]


def _all_gather(arrs, name):
    n_arr = len(arrs)

    def body(*refs):
        x_refs, out_refs, zero_ref = refs[:n_arr], refs[n_arr:2 * n_arr], refs[2 * n_arr]
        send_sems, recv_sems, local_sems = refs[2 * n_arr + 1:]
        x, y, c = _place()
        me, sibling = (x, y, c), (x, y, 1 - c)
        chips = _other_chips(x, y)
        zero_ref[...] = jnp.zeros_like(zero_ref)

        def rows(a, px, py, pc):
            m = arrs[a].shape[0]
            return out_refs[a].at[pl.ds((4 * px + 2 * py + pc) * m, m), :]

        def copy(a, k, block, to, src=None):
            return pltpu.make_async_remote_copy(
                src_ref=rows(a, *block) if src is None else src, dst_ref=rows(a, *block),
                send_sem=send_sems.at[7 * a + k], recv_sem=recv_sems.at[7 * a + k], device_id=to, device_id_type=MESH)

        mine = [pltpu.make_async_copy(x_refs[a], rows(a, *me), local_sems.at[a]) for a in range(n_arr)]
        first, passed = [], []
        for a in range(n_arr):
            first.append(copy(a, 0, me, sibling, src=x_refs[a]))
            first += [copy(a, 1 + j, me, (*chip, c), src=x_refs[a]) for j, chip in enumerate(chips)]
        for cp in mine + first:
            cp.start()
        for a in range(n_arr):
            for j, chip in enumerate(chips):
                copy(a, 1 + j, (*chip, c), me).wait_recv()
                passed.append(copy(a, 4 + j, (*chip, c), sibling))
                passed[-1].start()
        for a in range(n_arr):
            copy(a, 0, sibling, me).wait_recv()
            for j, chip in enumerate(chips):
                copy(a, 4 + j, (*chip, 1 - c), me).wait_recv()
        for cp in first + passed:
            cp.wait_send()
        for cp in mine:
            cp.wait()

    outs = pl.pallas_call(
        body, name=name,
        out_shape=[jax.ShapeDtypeStruct((8 * a.shape[0], a.shape[1]), a.dtype) for a in arrs]
        + [jax.ShapeDtypeStruct((8, LANES), F32)],
        in_specs=[_ANY] * n_arr, out_specs=[_ANY] * n_arr + [pl.BlockSpec(memory_space=pltpu.VMEM)],
        scratch_shapes=[pltpu.SemaphoreType.DMA((7 * n_arr,)), pltpu.SemaphoreType.DMA((7 * n_arr,)),
                        pltpu.SemaphoreType.DMA((n_arr,))],
    )(*arrs)
    return outs[:n_arr], outs[n_arr]


def _pair_exchange(big, whole, name):
    n_s, _, m, n = big.shape
    n_copies = n_s + len(whole)

    def body(*refs):
        big_ref, whole_refs = refs[0], refs[1:1 + len(whole)]
        rbig_ref, rwhole_refs = refs[1 + len(whole)], refs[2 + len(whole):2 + 2 * len(whole)]
        send_sems, recv_sems = refs[-2:]
        x, y, c = _place()
        sibling = (x, y, 1 - c)
        copies = [pltpu.make_async_remote_copy(
            src_ref=big_ref.at[s, 1 - c], dst_ref=rbig_ref.at[s], send_sem=send_sems.at[s], recv_sem=recv_sems.at[s],
            device_id=sibling, device_id_type=MESH) for s in range(n_s)]
        copies += [pltpu.make_async_remote_copy(
            src_ref=a, dst_ref=r, send_sem=send_sems.at[n_s + i], recv_sem=recv_sems.at[n_s + i],
            device_id=sibling, device_id_type=MESH) for i, (a, r) in enumerate(zip(whole_refs, rwhole_refs))]
        for cp in copies:
            cp.start()
        for cp in copies:
            cp.wait()

    return pl.pallas_call(
        body, name=name,
        out_shape=[jax.ShapeDtypeStruct((n_s, m, n), big.dtype)] + [jax.ShapeDtypeStruct(a.shape, a.dtype) for a in whole],
        in_specs=[_ANY] * (1 + len(whole)), out_specs=[_ANY] * (1 + len(whole)),
        scratch_shapes=[pltpu.SemaphoreType.DMA((n_copies,)), pltpu.SemaphoreType.DMA((n_copies,))],
    )(big, *whole)


def _pair_swap(arrs, name):
    k = len(arrs)

    def body(*refs):
        send_sems, recv_sems = refs[-2:]
        x, y, c = _place()
        copies = [pltpu.make_async_remote_copy(
            src_ref=refs[i], dst_ref=refs[k + i], send_sem=send_sems.at[i], recv_sem=recv_sems.at[i],
            device_id=(x, y, 1 - c), device_id_type=MESH) for i in range(k)]
        for cp in copies:
            cp.start()
        for cp in copies:
            cp.wait()

    return pl.pallas_call(
        body, name=name, out_shape=[jax.ShapeDtypeStruct(a.shape, a.dtype) for a in arrs], in_specs=[_ANY] * k,
        out_specs=[_ANY] * k, scratch_shapes=[pltpu.SemaphoreType.DMA((k,)), pltpu.SemaphoreType.DMA((k,))],
    )(*arrs)


_HBM = pl.BlockSpec(memory_space=pltpu.HBM)
_SEM = pl.BlockSpec(memory_space=pltpu.SEMAPHORE)
_EFFECT = pltpu.SideEffectType.DATAFLOW_SIDE_EFFECTING


def _split_copies(src_ref, land_ref, sems, plan, sending):
    n = len(sems) // 2
    return [pltpu.make_async_remote_copy(src_ref=s, dst_ref=d, send_sem=sems[k], recv_sem=sems[n + k], device_id=to,
                                         device_id_type=MESH)
            for k, (s, d, to) in enumerate(plan(src_ref, land_ref, sending))]


def _to_chips(src_at, land_at):
    def plan(src_ref, land_ref, sending):
        x, y, c = _place()
        return [(src_at(src_ref, tx, ty), land_at(land_ref, j, *((x, y) if sending else (tx, ty)), c), (tx, ty, c))
                for j, (tx, ty) in enumerate(_other_chips(x, y))]
    return plan


def _to_sibling(src_ref, land_ref, sending):
    x, y, c = _place()
    return [(src_ref.at[s, 1 - c], land_ref.at[s], (x, y, 1 - c)) for s in range(N_CHIPS)]


def _split_start(name, src, land, plan, n):
    def body(src_ref, land_ref, *outs):
        for cp in _split_copies(src_ref, land_ref, outs[:2 * n], plan, True):
            cp.start()
        outs[2 * n + 2][...] = jnp.zeros_like(outs[2 * n + 2])

    outs = pl.pallas_call(
        body, name=name,
        out_shape=(pltpu.SemaphoreType.DMA(()),) * (2 * n) + (
            pltpu.HBM(src.shape, src.dtype), pltpu.HBM(land.shape, land.dtype), jax.ShapeDtypeStruct((8, LANES), F32)),
        in_specs=(_HBM, _HBM), out_specs=(_SEM,) * (2 * n) + (_HBM, _HBM, pl.BlockSpec(memory_space=pltpu.VMEM)),
        input_output_aliases={0: 2 * n, 1: 2 * n + 1},
        compiler_params=pltpu.CompilerParams(has_side_effects=_EFFECT),
    )(pltpu.with_memory_space_constraint(src, pltpu.HBM), pltpu.with_memory_space_constraint(land, pltpu.HBM))
    return outs[:2 * n], outs[2 * n], outs[2 * n + 1], outs[2 * n + 2]


def _split_wait(name, sems, src, land, after, plan):
    def body(src_ref, land_ref, *rest):
        for cp in _split_copies(src_ref, land_ref, rest[:len(sems)], plan, False):
            cp.wait_send()
            cp.wait_recv()

    return pl.pallas_call(
        body, name=name, out_shape=(pltpu.HBM(src.shape, src.dtype), pltpu.HBM(land.shape, land.dtype)),
        in_specs=(_HBM, _HBM) + (_SEM,) * len(sems) + (_ANY,) * len(after), out_specs=(_HBM, _HBM),
        input_output_aliases={0: 0, 1: 1}, compiler_params=pltpu.CompilerParams(has_side_effects=_EFFECT),
    )(src, land, *sems, *after)


def _gather_finish(land, pack, m):
    def body(land_ref, pack_ref, out_ref, stage, send_sems, recv_sems, load_sems, store_sems):
        x, y, c = _place()

        def rows(px, py, pc, ref=out_ref):
            return ref.at[pl.ds((4 * px + 2 * py + pc) * m, m), :]

        copies = [pltpu.make_async_remote_copy(
            src_ref=rows(tx, ty, c, land_ref), dst_ref=rows(tx, ty, c), send_sem=send_sems.at[j], recv_sem=recv_sems.at[j],
            device_id=(x, y, 1 - c), device_id_type=MESH) for j, (tx, ty) in enumerate(_other_chips(x, y))]
        loads = [pltpu.make_async_copy(pack_ref.at[pl.ds(h * m, m), :], stage.at[h], load_sems.at[h]) for h in range(2)]
        stores = [pltpu.make_async_copy(stage.at[h], rows(x, y, h), store_sems.at[h]) for h in range(2)]
        for cp in copies + loads:
            cp.start()
        for h in range(2):
            loads[h].wait()
            stores[h].start()
        for j, (tx, ty) in enumerate(_other_chips(x, y)):
            copies[j].wait_send()
            pltpu.make_async_remote_copy(
                src_ref=rows(tx, ty, 1 - c), dst_ref=rows(tx, ty, 1 - c), send_sem=send_sems.at[j],
                recv_sem=recv_sems.at[j], device_id=(x, y, 1 - c), device_id_type=MESH).wait_recv()
        for cp in stores:
            cp.wait()

    return pl.pallas_call(
        body, name="gather_late_finish", out_shape=jax.ShapeDtypeStruct(land.shape, land.dtype),
        in_specs=[_ANY, _ANY], out_specs=_ANY, input_output_aliases={0: 0},
        scratch_shapes=[pltpu.VMEM((2, m, land.shape[1]), land.dtype), pltpu.SemaphoreType.DMA((3,)),
                        pltpu.SemaphoreType.DMA((3,)), pltpu.SemaphoreType.DMA((2,)), pltpu.SemaphoreType.DMA((2,))],
    )(land, pack)


def _row_tile(rows, cap=512):
    for t in range(cap - cap % 8, 7, -8):
        if rows % t == 0:
            return t
    return rows


def _elementwise(fn, n_out, name, *arrs, out_dtype=F32):
    rows, cols = arrs[0].shape
    tr = _row_tile(rows)
    n_in = len(arrs)

    def body(*refs):
        outs = fn(*[r[...].astype(F32) for r in refs[:n_in]])
        for r, o in zip(refs[n_in:], outs):
            r[...] = o.astype(out_dtype)

    spec = pl.BlockSpec((tr, cols), lambda i: (i, 0))
    return pl.pallas_call(
        body, grid=(rows // tr,), name=name, in_specs=[spec] * n_in, out_specs=[spec] * n_out,
        out_shape=[jax.ShapeDtypeStruct((rows, cols), out_dtype)] * n_out, compiler_params=_params("arbitrary"),
    )(*arrs)


def _pair_sum(gpack, rbig, ci, name):
    n_s, _, m, n = gpack.shape
    tr = _row_tile(m)

    def body(c_ref, g_ref, r_ref, o_ref):
        o_ref[...] = (g_ref[...] + r_ref[...]).astype(BF)

    return pl.pallas_call(
        body, name=name, out_shape=jax.ShapeDtypeStruct((n_s, m, n), BF),
        grid_spec=pltpu.PrefetchScalarGridSpec(
            num_scalar_prefetch=1, grid=(n_s, m // tr),
            in_specs=[pl.BlockSpec((None, None, tr, n), lambda s, i, c: (s, c[0], i, 0)),
                      pl.BlockSpec((None, tr, n), lambda s, i, c: (s, i, 0))],
            out_specs=pl.BlockSpec((None, tr, n), lambda s, i, c: (s, i, 0))),
        compiler_params=_params("arbitrary", "arbitrary"),
    )(ci.reshape(1), gpack, rbig)


def _chip_sum(sums, landed, chip, name):
    _, m, n = sums.shape
    tr = _row_tile(m)

    def body(c_ref, own_ref, r0_ref, r1_ref, r2_ref, o_ref):
        f = lambda r: r[...].astype(F32)
        o_ref[...] = _add4(f(own_ref), f(r0_ref), f(r1_ref), f(r2_ref))[0]

    slot = lambda j: pl.BlockSpec((None, tr, n), lambda i, c: (j, i, 0))
    return pl.pallas_call(
        body, name=name, out_shape=jax.ShapeDtypeStruct((m, n), F32),
        grid_spec=pltpu.PrefetchScalarGridSpec(
            num_scalar_prefetch=1, grid=(m // tr,),
            in_specs=[pl.BlockSpec((None, tr, n), lambda i, c: (c[0], i, 0)), slot(0), slot(1), slot(2)],
            out_specs=pl.BlockSpec((tr, n), lambda i, c: (i, 0))),
        compiler_params=_params("arbitrary"),
    )(chip.reshape(1), sums, landed, landed, landed)


def _add2(a, b):
    return (a + b,)


def _add4(own, r0, r1, r2):
    return ((own + r2) + (r0 + r1),)


def _adamw_math(w, g, m, v):
    m = ADAM_B1 * m + (1.0 - ADAM_B1) * g
    v = ADAM_B2 * v + (1.0 - ADAM_B2) * (g * g)
    m_hat = m / (1.0 - ADAM_B1 ** ADAM_STEP)
    v_hat = v / (1.0 - ADAM_B2 ** ADAM_STEP)
    delta = -ADAM_LR * (m_hat / (jnp.sqrt(v_hat) + ADAM_EPS) + ADAM_WD * w)
    return delta, m, v


WEIGHTS = ["meta_tokens", "ln1_g", "w_in", "q_a_norm_g", "w_uq", "kv_a_norm_g", "w_ukv", "q_norm_g", "k_norm_g",
           "conv_w", "conv_b", "lru_wa", "lru_ba", "lru_wi", "lru_bi", "lru_lambda", "attn_out_g", "rnn_out_g",
           "w_out", "ln2_g", "w_gate", "w_up", "w_down"]
BIG = ["w_in", "w_uq", "w_ukv", "w_out", "w_gate", "w_up", "w_down"]
BIG_T = {"w_in": True, "w_uq": True, "w_ukv": True, "w_out": False, "w_gate": True, "w_up": True, "w_down": False}
BIG_ROWS = {"w_in": 424, "w_uq": 72, "w_ukv": 64, "w_out": 256, "w_gate": 704, "w_up": 704, "w_down": 704}
EARLY = ["w_in", "w_uq", "w_ukv"]
LATE = ["w_out", "w_gate", "w_up", "w_down"]
EARLY_ROWS = 576
LATE_ROWS = 2368
SMALL_SHARDED = ["meta_tokens", "conv_w", "lru_ba", "lru_bi", "lru_lambda"]
SMALL = [n for n in WEIGHTS if n not in BIG]
SMALL_PACK_ROWS = 160
SMALL_ADAM_ROWS = 144


def _offsets(names):
    off, o = {}, 0
    for n in names:
        off[n] = o
        o += BIG_ROWS[n]
    return off


def _shard_pack(names, src, rows):
    parts = [_to_pack_piece(n, src[n]) for n in names]
    used = sum(BIG_ROWS[n] for n in names)
    if rows > used:
        parts.append(jnp.zeros((rows - used, D), F32))
    return jnp.concatenate(parts, axis=0)


def _grad_pack(names, g, rows):
    parts = [g[n].reshape(N_CHIPS, BIG_ROWS[n], D) for n in names]
    used = sum(BIG_ROWS[n] for n in names)
    if rows > used:
        parts.append(jnp.zeros((N_CHIPS, rows - used, D), F32))
    return jnp.concatenate(parts, axis=1).reshape(N_CHIPS, 2, rows // 2, D)


def _both_halves(mine, other, ci):
    return jnp.where(ci == 0, jnp.concatenate([mine, other], axis=0), jnp.concatenate([other, mine], axis=0))


def _to_pack_piece(name, shard):
    a = shard[0].T if BIG_T[name] else shard[0]
    return a.reshape(BIG_ROWS[name], D)


def _flat_pack(arrs, rows):
    flat = jnp.concatenate([a.reshape(-1) for a in arrs])
    return jnp.pad(flat, (0, rows * D - flat.shape[0])).reshape(rows, D)


def _flat_unpack(pack, shapes):
    flat, out, o = pack.reshape(-1), [], 0
    for s in shapes:
        n = math.prod(s)
        out.append(flat[o:o + n].reshape(s))
        o += n
    return out


def kernel(x, meta_tokens, ln1_g, w_in, q_a_norm_g, w_uq, kv_a_norm_g, w_ukv, q_norm_g, k_norm_g, conv_w, conv_b, lru_wa, lru_ba, lru_wi, lru_bi, lru_lambda, attn_out_g, rnn_out_g, w_out, ln2_g, w_gate, w_up, w_down, loss_target, m_meta_tokens, m_ln1_g, m_w_in, m_q_a_norm_g, m_w_uq, m_kv_a_norm_g, m_w_ukv, m_q_norm_g, m_k_norm_g, m_conv_w, m_conv_b, m_lru_wa, m_lru_ba, m_lru_wi, m_lru_bi, m_lru_lambda, m_attn_out_g, m_rnn_out_g, m_w_out, m_ln2_g, m_w_gate, m_w_up, m_w_down, v_meta_tokens, v_ln1_g, v_w_in, v_q_a_norm_g, v_w_uq, v_kv_a_norm_g, v_w_ukv, v_q_norm_g, v_k_norm_g, v_conv_w, v_conv_b, v_lru_wa, v_lru_ba, v_lru_wi, v_lru_bi, v_lru_lambda, v_attn_out_g, v_rnn_out_g, v_w_out, v_ln2_g, v_w_gate, v_w_up, v_w_down):
    wts = dict(zip(WEIGHTS, (meta_tokens, ln1_g, w_in, q_a_norm_g, w_uq, kv_a_norm_g, w_ukv, q_norm_g, k_norm_g, conv_w, conv_b, lru_wa, lru_ba, lru_wi, lru_bi, lru_lambda, attn_out_g, rnn_out_g, w_out, ln2_g, w_gate, w_up, w_down)))
    mom = dict(zip(WEIGHTS, (m_meta_tokens, m_ln1_g, m_w_in, m_q_a_norm_g, m_w_uq, m_kv_a_norm_g, m_w_ukv, m_q_norm_g, m_k_norm_g, m_conv_w, m_conv_b, m_lru_wa, m_lru_ba, m_lru_wi, m_lru_bi, m_lru_lambda, m_attn_out_g, m_rnn_out_g, m_w_out, m_ln2_g, m_w_gate, m_w_up, m_w_down)))
    var = dict(zip(WEIGHTS, (v_meta_tokens, v_ln1_g, v_w_in, v_q_a_norm_g, v_w_uq, v_kv_a_norm_g, v_w_ukv, v_q_norm_g, v_k_norm_g, v_conv_w, v_conv_b, v_lru_wa, v_lru_ba, v_lru_wi, v_lru_bi, v_lru_lambda, v_attn_out_g, v_rnn_out_g, v_w_out, v_ln2_g, v_w_gate, v_w_up, v_w_down)))
    xi, yi, ci = _place()
    chip = 2 * xi + yi
    off_e, off_l = _offsets(EARLY), _offsets(LATE)
    half_e, half_l = EARLY_ROWS // 2, LATE_ROWS // 2
    gather_plan = _to_chips(lambda ref, tx, ty: ref,
                            lambda ref, j, px, py, c: ref.at[pl.ds((4 * px + 2 * py + c) * half_l, half_l), :])
    scatter_plan = _to_chips(lambda ref, tx, ty: ref.at[2 * tx + ty], lambda ref, j, px, py, c: ref.at[j])

    pack_e = _shard_pack(EARLY, wts, EARLY_ROWS).astype(BF)
    spack = jnp.concatenate([meta_tokens[:, :LANES], meta_tokens[:, LANES:], conv_w[0], lru_ba[0], lru_bi[0],
                             lru_lambda[0], jnp.zeros((6, LANES), F32)], axis=0)
    (ge, gs), gathered = _all_gather([lax.dynamic_slice_in_dim(pack_e, ci * half_e, half_e, axis=0),
                                      lax.dynamic_slice_in_dim(spack, ci * 24, 24, axis=0)], "gather_early")
    ge = ge.reshape(N_CHIPS, EARLY_ROWS, D)
    gs = gs.reshape(N_CHIPS, 48, LANES)
    full = {n: ge[:, off_e[n]:off_e[n] + BIG_ROWS[n]] for n in EARLY}
    pack_l = (_shard_pack(LATE, wts, LATE_ROWS) + gathered[0, 0]).astype(BF)
    sems_l, src_l, land_l, tied = _split_start(
        "gather_late_start", lax.dynamic_slice_in_dim(pack_l, ci * half_l, half_l, axis=0),
        lax.empty((8 * half_l, D), BF), gather_plan, 3)

    def late_weights(after):
        _, land = _split_wait("gather_late_wait", sems_l, src_l, land_l, after, gather_plan)
        gl = _gather_finish(land, pack_l, half_l).reshape(N_CHIPS, LATE_ROWS, D)
        part = lambda n: gl[:, off_l[n]:off_l[n] + BIG_ROWS[n]].reshape(N_CHIPS * BIG_ROWS[n], D)
        return dict(w_out=part("w_out"), w_gate_t=part("w_gate"), w_up_t=part("w_up"), w_down=part("w_down"))

    pair, late = {}, {}

    def early_grads(g_late):
        gpack = _grad_pack(LATE, g_late, LATE_ROWS)
        pair["sems"], pair["src"], pair["land"], zeros = _split_start(
            "grad_pair_late_start", gpack, lax.empty((N_CHIPS, half_l, D), F32), _to_sibling, N_CHIPS)
        return zeros[0, 0]

    def mid_grads(after):
        gpack, rbig = _split_wait("grad_pair_late_wait", pair["sems"], pair["src"], pair["land"], after, _to_sibling)
        chip_big = _pair_sum(gpack, rbig, ci, "grad_pair_sum_late")
        late["sems"], late["src"], late["land"], zeros = _split_start(
            "grad_chip_late_start", chip_big, lax.empty((3, half_l, D), BF), scatter_plan, 3)
        return zeros[0, 0]

    cols = lambda a: a.transpose(1, 0, 2).reshape(a.shape[1], N_CHIPS * a.shape[2])
    meta_full = cols(jnp.concatenate([gs[:, 0:16], gs[:, 16:32]], axis=2))
    w = dict(
        w_in_t=full["w_in"].reshape(IN_COLS, D), w_uq_t=full["w_uq"].reshape(N_HEADS * QK_HEAD, Q_LORA),
        w_ukv_t=full["w_ukv"].reshape(2 * D_ATTN, KV_LORA),
        ln1_g=ln1_g, q_a_norm_g=q_a_norm_g, kv_a_norm_g=kv_a_norm_g, q_norm_g=q_norm_g, k_norm_g=k_norm_g,
        conv_w=cols(gs[:, 32:36]), conv_b=conv_b, lru_wa=lru_wa[0], lru_ba=cols(gs[:, 36:38]), lru_wi=lru_wi[0],
        lru_bi=cols(gs[:, 38:40]), lru_lambda=cols(gs[:, 40:42]), attn_out_g=attn_out_g, rnn_out_g=rnn_out_g,
        ln2_g=ln2_g,
    )

    loss_local, grad_x, g, last = _local_step(x, loss_target, meta_full + tied[0, 0], w, late_weights, early_grads,
                                              mid_grads)

    gpack = _grad_pack(EARLY, {"w_in": g["w_in_t"], "w_uq": g["w_uq_t"], "w_ukv": g["w_ukv_t"]}, EARLY_ROWS)
    full_shapes = {n: wts[n].shape for n in SMALL}
    full_shapes.update(meta_tokens=(N_META, D), conv_w=(1, CONV_W, D_RNN), lru_ba=(1, 2, D_RNN), lru_bi=(1, 2, D_RNN),
                       lru_lambda=(1, 2, D_RNN))
    gsmall = _flat_pack([g[n] for n in SMALL] + [loss_local], SMALL_PACK_ROWS)
    rbig, rsmall = _pair_exchange(gpack, [gsmall], "grad_pair_exchange")
    chip_big = _pair_sum(gpack, rbig, ci, "grad_pair_sum")
    (chip_small,) = _elementwise(_add2, 1, "grad_pair_sum_small", gsmall, rsmall)
    everywhere = _to_chips(lambda ref, tx, ty: ref, lambda ref, j, px, py, c: ref.at[j])
    sems_e, src_e, land_e, zero_e = _split_start(
        "grad_chip_early_start", chip_big, lax.empty((3, half_e, D), BF), scatter_plan, 3)
    sems_s, src_s, land_s, zero_s = _split_start(
        "grad_small_start", chip_small, lax.empty((3, SMALL_PACK_ROWS, D), F32), everywhere, 3)

    grads, delta, new_m, new_v = {}, {}, {}, {}

    def adamw_big(names, offsets, gshard):
        for n in names:
            _, k, cols = wts[n].shape
            as_rows = (lambda a: a[0].T) if BIG_T[n] else (lambda a: a[0])
            back = (lambda a: a.T[None]) if BIG_T[n] else (lambda a: a[None])
            g2 = gshard[offsets[n]:offsets[n] + BIG_ROWS[n]].reshape((cols, k) if BIG_T[n] else (k, cols))
            d_, m_, v_ = _elementwise(_adamw_math, 3, "adamw_" + n, as_rows(wts[n]), g2, as_rows(mom[n]), as_rows(var[n]))
            grads[n], delta[n], new_m[n], new_v[n] = back(g2), back(d_), back(m_), back(v_)
        return d_

    src, land = _split_wait("grad_chip_late_wait", late["sems"], late["src"], late["land"], last + [zero_e, zero_s],
                            scatter_plan)
    sum_l = _chip_sum(src, land, chip, "grad_chip_sum_late")
    (other_l,) = _pair_swap([sum_l], "grad_pair_swap_late")
    done_late = adamw_big(LATE, off_l, _both_halves(sum_l, other_l, ci))
    src_e, land_e = _split_wait("grad_chip_early_wait", sems_e, src_e, land_e, [done_late], scatter_plan)
    src_s, land_s = _split_wait("grad_small_wait", sems_s, src_s, land_s, [done_late], everywhere)
    sum_e = _chip_sum(src_e, land_e, chip, "grad_chip_sum")
    (small_sum,) = _elementwise(_add4, 1, "grad_chip_sum_small", src_s, land_s[0], land_s[1], land_s[2])
    (other_e,) = _pair_swap([sum_e], "grad_pair_swap_early")
    adamw_big(EARLY, off_e, _both_halves(sum_e, other_e, ci))
    *small_grads, loss = _flat_unpack(small_sum, [full_shapes[n] for n in SMALL] + [()])
    small_full = dict(zip(SMALL, small_grads))
    for n in SMALL:
        a = small_full[n]
        if n in SMALL_SHARDED:
            width = wts[n].shape[-1]
            a = lax.dynamic_slice_in_dim(a, chip * width, width, axis=a.ndim - 1)
        grads[n] = a.reshape(wts[n].shape)

    packs =[_flat_pack([src[n] for n in SMALL], SMALL_ADAM_ROWS) for src in (wts, grads, mom, var)]
    outs = _elementwise(_adamw_math, 3, "adamw_small", *packs)
    for dst, o in zip((delta, new_m, new_v), outs):
        dst.update(zip(SMALL, _flat_unpack(o, [wts[n].shape for n in SMALL])))

    return (loss, grad_x, *[grads[n] for n in WEIGHTS], *[delta[n] for n in WEIGHTS],
            *[new_m[n] for n in WEIGHTS], *[new_v[n] for n in WEIGHTS])
```

```python
import functools
import math

import jax
import jax.numpy as jnp
from jax import lax
from jax.experimental import pallas as pl
from jax.experimental.pallas import tpu as pltpu

F32 = jnp.float32
BF = jnp.bfloat16
MESH = pl.DeviceIdType.MESH

D = 1024
SEQ = 2048
N_META = 16
T = N_META + SEQ
N_HEADS = 8
QK_NOPE = 64
QK_ROPE = 32
QK_HEAD = 96
V_HEAD = 64
Q_LORA = 384
KV_LORA = 256
D_ATTN = 512
D_RNN = 512
RNN_BW = 64
CONV_W = 4
LRU_C = 8.0
ROPE_THETA = 10000.0
D_FF = 2816
EPS = 1e-6
IN_COLS = 1696
ADAM_LR, ADAM_B1, ADAM_B2, ADAM_EPS, ADAM_WD, ADAM_STEP = 0.001, 0.9, 0.999, 1e-08, 0.01, 10

LANES = 128
TP = 2176
NB = 2
R = NB * TP
TR = 256
TRF = 256
TQ = 544
HP = LANES
PC = 1792
O_CKV, O_KR, O_XR, O_XG = 384, 640, 768, 1280
CG = 128
N_CG = D_RNN // CG
VMEM_LIMIT = 56 * 1024 * 1024
N_CHIPS = 4
SCALE = QK_HEAD ** -0.5
KEY_MASK = -30000.0
LOG2_E = 1.4426950408889634
SCALE_LOG2 = SCALE * LOG2_E


def _nt(a, b):
    return lax.dot_general(a, b, (((1,), (1,)), ((), ())), preferred_element_type=F32)


def _nn(a, b):
    return jnp.dot(a, b, preferred_element_type=F32)


def _tn(a, b):
    return lax.dot_general(a, b, (((0,), (0,)), ((), ())), preferred_element_type=F32)


def _rms(x, g, n):
    ms = jnp.sum(x * x, axis=-1, keepdims=True) * (1.0 / n)
    return x * lax.rsqrt(ms + EPS) * g


def _rot_impl(x):
    lane = lax.broadcasted_iota(jnp.int32, x.shape, 1)
    left = pltpu.roll(x, HP - 16, 1)
    right = pltpu.roll(x, 16, 1)
    lo = (lane >= QK_NOPE) & (lane < QK_NOPE + 16)
    hi = (lane >= QK_NOPE + 16) & (lane < QK_HEAD)
    return jnp.where(lo, -left, jnp.where(hi, right, 0.0))


@jax.custom_vjp
def _rot(x):
    return _rot_impl(x)


def _rot_fwd(x):
    return _rot_impl(x), None


def _rot_bwd(_, g):
    return (-_rot_impl(g),)


_rot.defvjp(_rot_fwd, _rot_bwd)


def _head(x, g, cs, sn):
    n = _rms(x, g, QK_HEAD)
    return n * cs + _rot(n) * sn


def _head_bwd(x, g, cs, sn, dout):
    rs = lax.rsqrt(jnp.sum(x * x, axis=-1, keepdims=True) * (1.0 / QK_HEAD) + EPS)
    xh = x * rs
    dn = dout * cs - _rot_impl(dout * sn)
    gdn = g * dn
    t = jnp.sum(gdn * xh, axis=-1, keepdims=True) * (1.0 / QK_HEAD)
    return rs * (gdn - xh * t), jnp.sum(dn * xh, axis=0, keepdims=True)


def _const_spec(shape):
    return pl.BlockSpec(shape, lambda *_: (0,) * len(shape), pipeline_mode=pl.Buffered(1))


def _row_spec(n, tr=TR):
    return pl.BlockSpec((tr, n), lambda i: (i, 0))


def _params(*sem, vmem=VMEM_LIMIT):
    return pltpu.CompilerParams(dimension_semantics=sem, vmem_limit_bytes=vmem)


def _stage_a_fwd(hp, cs, sn, cw):
    def body(hp_ref, cs_ref, sn_ref, ln1, win, qag, wq, kvag, wk, wv, qg, kg,
             pa_ref, xr_ref, xg_ref, q_ref, k_ref, v_ref):
        hn = _rms(hp_ref[...], ln1[...], D).astype(BF)
        p = _nt(hn, win[...])
        pa_ref[...] = p[:, :O_XR]
        xr_ref[...] = p[:, O_XR:O_XG]
        xg_ref[...] = p[:, O_XG:]
        cqn = _rms(p[:, :O_CKV], qag[...], Q_LORA).astype(BF)
        ckvn = _rms(p[:, O_CKV:O_KR], kvag[...], KV_LORA).astype(BF)
        kr = p[:, O_KR:O_XR]
        c, s = cs_ref[...], sn_ref[...]
        mask_lane = lax.broadcasted_iota(jnp.int32, (1, HP), 1) == QK_HEAD
        row = pl.program_id(0) * TRF + lax.broadcasted_iota(jnp.int32, (TRF, 1), 0)
        key_mask = jnp.where(jnp.where(row >= TP, row - TP, row) < T, 0.0, KEY_MASK)
        qraw = _nt(cqn, wq[...])
        kraw = _nt(ckvn, wk[...])
        for h in range(N_HEADS):
            sl = slice(h * HP, (h + 1) * HP)
            q_ref[:, sl] = jnp.where(mask_lane, 1.0, _head(qraw[:, sl], qg[...], c, s)).astype(BF)
            k_ref[:, sl] = jnp.where(mask_lane, key_mask, _head(kraw[:, sl] + kr, kg[...], c, s)).astype(BF)
        v_ref[...] = _nt(ckvn, wv[...]).astype(BF)

    rs = lambda n: _row_spec(n, TRF)
    return pl.pallas_call(
        body, grid=(R // TRF,), name="stage_a_fwd",
        in_specs=[rs(D), rs(HP), rs(HP), _const_spec((1, D)), _const_spec((PC, D)),
                  _const_spec((1, Q_LORA)), _const_spec((N_HEADS * HP, Q_LORA)), _const_spec((1, KV_LORA)),
                  _const_spec((N_HEADS * HP, KV_LORA)), _const_spec((D_ATTN, KV_LORA)), _const_spec((1, HP)),
                  _const_spec((1, HP))],
        out_specs=[rs(O_XR), rs(D_RNN), rs(D_RNN), rs(N_HEADS * HP), rs(N_HEADS * HP), rs(D_ATTN)],
        out_shape=[jax.ShapeDtypeStruct((R, O_XR), F32), jax.ShapeDtypeStruct((R, D_RNN), F32),
                   jax.ShapeDtypeStruct((R, D_RNN), F32), jax.ShapeDtypeStruct((R, N_HEADS * HP), BF),
                   jax.ShapeDtypeStruct((R, N_HEADS * HP), BF), jax.ShapeDtypeStruct((R, D_ATTN), BF)],
        compiler_params=_params("arbitrary"),
    )(hp, cs, sn, cw["ln1_g"], cw["win"], cw["qa_g"], cw["wq"], cw["kva_g"], cw["wk"], cw["wv"], cw["q_g"], cw["k_g"])


def _stage_a_bwd(dq, dk, dv, dxr, dxg, dh1, hp, pa, cs, sn, cw):
    def body(dq_ref, dk_ref, dv_ref, dxr_ref, dxg_ref, dh1_ref, hp_ref, pa_ref, cs_ref, sn_ref,
             ln1, win, qag, wq, kvag, wk, wv, qg, kg,
             dhp_ref, dp_ref, dqraw_ref, dkraw_ref, hn_ref, cqn_ref, ckvn_ref,
             dln1_ref, dqag_ref, dkvag_ref, dqg_ref, dkg_ref):
        @pl.when(pl.program_id(0) == 0)
        def _():
            for r in (dln1_ref, dqag_ref, dkvag_ref, dqg_ref, dkg_ref):
                r[...] = jnp.zeros_like(r)

        hn, vjp_ln1 = jax.vjp(lambda h, g: _rms(h, g, D), hp_ref[...], ln1[...])
        hn_ref[...] = hn.astype(BF)
        pa_v = pa_ref[...]
        cqn, vjp_qa = jax.vjp(lambda x, g: _rms(x, g, Q_LORA), pa_v[:, :O_CKV], qag[...])
        ckvn, vjp_kva = jax.vjp(lambda x, g: _rms(x, g, KV_LORA), pa_v[:, O_CKV:O_KR], kvag[...])
        kr = pa_v[:, O_KR:O_XR]
        cqnb, ckvnb = cqn.astype(BF), ckvn.astype(BF)
        cqn_ref[...] = cqnb
        ckvn_ref[...] = ckvnb
        c, s = cs_ref[...], sn_ref[...]
        lane = lax.broadcasted_iota(jnp.int32, (1, HP), 1)
        rope_lanes = ((lane >= QK_NOPE) & (lane < QK_HEAD)).astype(F32)
        dkr = jnp.zeros((TR, HP), F32)
        dqg = jnp.zeros((1, HP), F32)
        dkg = jnp.zeros((1, HP), F32)
        qraw = _nt(cqnb, wq[...])
        kraw = _nt(ckvnb, wk[...])
        for h in range(N_HEADS):
            sl = slice(h * HP, (h + 1) * HP)
            dqraw, dg = _head_bwd(qraw[:, sl], qg[...], c, s, dq_ref[:, sl])
            dqg = dqg + dg
            dqraw_ref[:, sl] = dqraw.astype(BF)
            dkraw, dg = _head_bwd(kraw[:, sl] + kr, kg[...], c, s, dk_ref[:, sl])
            dkg = dkg + dg
            dkraw_ref[:, sl] = dkraw.astype(BF)
            dkr = dkr + dkraw * rope_lanes
        dcq, dqag = vjp_qa(_nn(dqraw_ref[...], wq[...]))
        dckv, dkvag = vjp_kva(_nn(dkraw_ref[...], wk[...]) + _nn(dv_ref[...].astype(BF), wv[...]))
        dpb = jnp.concatenate([dcq, dckv, dkr, dxr_ref[...], dxg_ref[...]], axis=1).astype(BF)
        dp_ref[...] = dpb
        dh, dln1 = vjp_ln1(_nn(dpb, win[...]))
        dhp_ref[...] = dh + dh1_ref[...]
        dln1_ref[...] += dln1
        dqag_ref[...] += dqag
        dkvag_ref[...] += dkvag
        dqg_ref[...] += dqg
        dkg_ref[...] += dkg

    acc = lambda n: pl.BlockSpec((1, n), lambda i: (0, 0))
    return pl.pallas_call(
        body, grid=(R // TR,), name="stage_a_bwd",
        in_specs=[_row_spec(N_HEADS * HP), _row_spec(N_HEADS * HP), _row_spec(D_ATTN), _row_spec(D_RNN),
                  _row_spec(D_RNN), _row_spec(D), _row_spec(D), _row_spec(O_XR), _row_spec(HP), _row_spec(HP),
                  _const_spec((1, D)), _const_spec((PC, D)), _const_spec((1, Q_LORA)),
                  _const_spec((N_HEADS * HP, Q_LORA)), _const_spec((1, KV_LORA)),
                  _const_spec((N_HEADS * HP, KV_LORA)), _const_spec((D_ATTN, KV_LORA)), _const_spec((1, HP)),
                  _const_spec((1, HP))],
        out_specs=[_row_spec(D), _row_spec(PC), _row_spec(N_HEADS * HP), _row_spec(N_HEADS * HP), _row_spec(D),
                   _row_spec(Q_LORA), _row_spec(KV_LORA), acc(D), acc(Q_LORA), acc(KV_LORA), acc(HP), acc(HP)],
        out_shape=[jax.ShapeDtypeStruct((R, D), F32), jax.ShapeDtypeStruct((R, PC), BF),
                   jax.ShapeDtypeStruct((R, N_HEADS * HP), BF), jax.ShapeDtypeStruct((R, N_HEADS * HP), BF),
                   jax.ShapeDtypeStruct((R, D), BF), jax.ShapeDtypeStruct((R, Q_LORA), BF),
                   jax.ShapeDtypeStruct((R, KV_LORA), BF), jax.ShapeDtypeStruct((1, D), F32),
                   jax.ShapeDtypeStruct((1, Q_LORA), F32), jax.ShapeDtypeStruct((1, KV_LORA), F32),
                   jax.ShapeDtypeStruct((1, HP), F32), jax.ShapeDtypeStruct((1, HP), F32)],
        compiler_params=_params("arbitrary"),
    )(dq, dk, dv, dxr, dxg, dh1, hp, pa, cs, sn, cw["ln1_g"], cw["win"], cw["qa_g"], cw["wq"], cw["kva_g"],
      cw["wk"], cw["wv"], cw["q_g"], cw["k_g"])


def _head_mask(half, dtype):
    lane = lax.broadcasted_iota(jnp.int32, (1, 2 * V_HEAD), 1)
    return ((lane >= V_HEAD) == (half == 1)).astype(dtype)


def _attn_specs(tq):
    n_q = TP // tq
    return (NB, N_HEADS // 2, n_q), dict(
        q=pl.BlockSpec((tq, 2 * HP), lambda b, j, i: (b * n_q + i, j)),
        k=pl.BlockSpec((TP, 2 * HP), lambda b, j, i: (b, j)),
        v=pl.BlockSpec((TP, 2 * V_HEAD), lambda b, j, i: (b, j)),
        o=pl.BlockSpec((tq, 2 * V_HEAD), lambda b, j, i: (b * n_q + i, j)),
        lse=pl.BlockSpec((None, tq, 2), lambda b, j, i: (j, b * n_q + i, 0)))


TQF = 1088


def _attn_fwd(q, k, v):
    def body(q_ref, k_ref, v_ref, o_ref, lse_ref):
        v2 = v_ref[...]
        o = jnp.zeros((TQF, 2 * V_HEAD), F32)
        lse = []
        for hh in range(2):
            sl = slice(hh * HP, (hh + 1) * HP)
            raw = _nt(q_ref[:, sl], k_ref[:, sl])
            m = jnp.max(raw, axis=-1, keepdims=True)
            e = jnp.exp2((raw - m) * SCALE_LOG2)
            l = jnp.sum(e, axis=-1, keepdims=True)
            o = o + _nn(e.astype(BF), v2 * _head_mask(hh, BF)) * (1.0 / l)
            lse.append(m * SCALE_LOG2 + jnp.log(l) * LOG2_E)
        o_ref[...] = o
        lane = lax.broadcasted_iota(jnp.int32, (TQF, 2), 1)
        lse_ref[...] = jnp.where(lane == 0, lse[0], lse[1])

    grid, sp = _attn_specs(TQF)
    return pl.pallas_call(
        body, grid=grid, name="attn_fwd", in_specs=[sp["q"], sp["k"], sp["v"]], out_specs=[sp["o"], sp["lse"]],
        out_shape=[jax.ShapeDtypeStruct((R, D_ATTN), F32), jax.ShapeDtypeStruct((N_HEADS // 2, R, 2), F32)],
        compiler_params=_params("arbitrary", "arbitrary", "arbitrary"),
    )(q, k, v)


def _attn_bwd(q, k, v, o, lse, do):
    def body(q_ref, k_ref, v_ref, o_ref, lse_ref, do_ref, dq_ref, dk_ref, dv_ref):
        @pl.when(pl.program_id(2) == 0)
        def _():
            dk_ref[...] = jnp.zeros_like(dk_ref)
            dv_ref[...] = jnp.zeros_like(dv_ref)

        do = do_ref[...]
        dob = do.astype(BF)
        do_o = do * o_ref[...]
        v2 = v_ref[...]
        dv_sum = jnp.zeros((TP, 2 * V_HEAD), F32)
        for hh in range(2):
            sl = slice(hh * HP, (hh + 1) * HP)
            qb, kb = q_ref[:, sl], k_ref[:, sl]
            p = jnp.exp2(_nt(qb, kb) * SCALE_LOG2 - lse_ref[:, hh:hh + 1])
            dp = _nt(dob, v2 * _head_mask(hh, BF))
            delta = jnp.sum(do_o * _head_mask(hh, F32), axis=-1, keepdims=True)
            dsb = (p * (dp - delta) * SCALE).astype(BF)
            dq_ref[:, sl] = _nn(dsb, kb)
            dk_ref[:, sl] += _tn(dsb, qb)
            dv_sum = dv_sum + _tn(p.astype(BF), dob) * _head_mask(hh, F32)
        dv_ref[...] += dv_sum

    grid, sp = _attn_specs(TQ)
    return pl.pallas_call(
        body, grid=grid, name="attn_bwd", in_specs=[sp["q"], sp["k"], sp["v"], sp["o"], sp["lse"], sp["o"]],
        out_specs=[sp["q"], sp["k"], sp["v"]],
        out_shape=[jax.ShapeDtypeStruct((R, N_HEADS * HP), F32), jax.ShapeDtypeStruct((R, N_HEADS * HP), F32),
                   jax.ShapeDtypeStruct((R, D_ATTN), F32)],
        compiler_params=_params("arbitrary", "arbitrary", "arbitrary"),
    )(q, k, v, o, lse, do)


def _tile_prefix(a_ref, b_ref, reverse):
    tiles = (TP // 8, 8, CG)
    r8 = lax.broadcasted_iota(jnp.int32, tiles, 1)
    a, b = a_ref[...].reshape(tiles), b_ref[...].reshape(tiles)
    for s in (1, 2, 4):
        shift = 8 - s if reverse else s
        keep = (r8 < 8 - s) if reverse else (r8 >= s)
        b = jnp.where(keep, a * pltpu.roll(b, shift, 1) + b, b)
        a = jnp.where(keep, a * pltpu.roll(a, shift, 1), a)
    a_ref[...] = a.reshape(TP, CG)
    b_ref[...] = b.reshape(TP, CG)


def _scan_pair(af_ref, bf_ref, hf_ref, ab_ref, bb_ref, hb_ref):
    _tile_prefix(af_ref, bf_ref, False)
    _tile_prefix(ab_ref, bb_ref, True)
    n_tiles = TP // 8

    def step(i, carry):
        cf, cb = carry
        rf = pl.multiple_of(i * 8, 8)
        rb = pl.multiple_of((n_tiles - 1 - i) * 8, 8)
        hf_ref[pl.ds(rf, 8), :] = bf_ref[pl.ds(rf, 8), :] + af_ref[pl.ds(rf, 8), :] * cf
        hb_ref[pl.ds(rb, 8), :] = bb_ref[pl.ds(rb, 8), :] + ab_ref[pl.ds(rb, 8), :] * cb
        cf = bf_ref[pl.ds(rf + 7, 1), :] + af_ref[pl.ds(rf + 7, 1), :] * cf
        cb = bb_ref[pl.ds(rb, 1), :] + ab_ref[pl.ds(rb, 1), :] * cb
        return cf, cb

    zero = jnp.zeros((1, CG), F32)
    lax.fori_loop(0, n_tiles, step, (zero, zero), unroll=8)


def _shifts(x):
    t = lax.broadcasted_iota(jnp.int32, x.shape, 0)
    xm2 = jnp.where(t >= 2, pltpu.roll(x, 2, 0), 0.0)
    xm1 = jnp.where(t >= 1, pltpu.roll(x, 1, 0), 0.0)
    xp1 = jnp.where(t < TP - 1, pltpu.roll(x, TP - 1, 0), 0.0)
    return xm2, xm1, xp1


def _softplus(z):
    e = jnp.exp(-jnp.abs(z))
    small = e * (1.0 - e * (0.5 - e * (1.0 / 3.0)))
    return jnp.maximum(z, 0.0) + jnp.where(e < 0.01, small, jnp.log(1.0 + e))


def _sigmoid(x):
    return 0.5 * jnp.tanh(0.5 * x) + 0.5


def _one_minus_sq(log_a, a):
    x = 2.0 * log_a
    series = -x * (1.0 + x * 0.5 * (1.0 + x * (1.0 / 3.0) * (1.0 + x * 0.25)))
    return jnp.where(x > -0.05, series, 1.0 - a * a)


def _gates(row0, xc, pa_f, pi_f, pa_b, pi_b, lam_f, lam_b):
    t = row0 + lax.broadcasted_iota(jnp.int32, xc.shape, 0)
    valid = t < T
    out = []
    for pa, pi_, lam in ((pa_f, pi_f, lam_f), (pa_b, pi_b, lam_b)):
        r = _sigmoid(pa)
        gate_i = _sigmoid(pi_)
        log_a = -LRU_C * r * _softplus(-lam)
        a = jnp.exp(log_a)
        mult = jnp.sqrt(jnp.maximum(_one_minus_sq(log_a, a), 0.0))
        out += [a, jnp.where(valid, mult * (gate_i * xc), 0.0)]
    return tuple(out)


def _gates_bwd(row0, xc, pres, lams, cots):
    t = row0 + lax.broadcasted_iota(jnp.int32, xc.shape, 0)
    valid = t < T
    dxc = jnp.zeros_like(xc)
    dpres, dlams = [], []
    for d in range(2):
        pa, pi_, lam = pres[2 * d], pres[2 * d + 1], lams[d]
        da, db = cots[2 * d], jnp.where(valid, cots[2 * d + 1], 0.0)
        r = _sigmoid(pa)
        gate_i = _sigmoid(pi_)
        sp = _softplus(-lam)
        log_a = -LRU_C * r * sp
        a = jnp.exp(log_a)
        m2 = jnp.maximum(_one_minus_sq(log_a, a), 0.0)
        mult = jnp.sqrt(m2)
        dxc = dxc + db * (mult * gate_i)
        d_gate = db * (mult * xc)
        d_m2 = jnp.where(m2 > 0.0, db * (gate_i * xc) * (0.5 * lax.rsqrt(m2)), 0.0)
        d_log_a = da * a - 2.0 * d_m2 * (a * a)
        dpres += [d_log_a * (-LRU_C * sp) * (r * (1.0 - r)), d_gate * (gate_i * (1.0 - gate_i))]
        d_sp = jnp.sum(d_log_a * (-LRU_C * r), axis=0, keepdims=True)
        dlams.append(-d_sp * jax.nn.sigmoid(-lam))
    return dxc, dpres, dlams


def _rnn_specs():
    seq = pl.BlockSpec((TP, CG), lambda g, b: (b, g))
    return dict(
        seq=seq,
        cw=pl.BlockSpec((CONV_W, CG), lambda g, b: (0, g)),
        cb=pl.BlockSpec((1, CG), lambda g, b: (0, g)),
        w4=pl.BlockSpec((None, CG, 4 * CG), lambda g, b: (g, 0, 0)),
        b4=pl.BlockSpec((None, 1, 4 * CG), lambda g, b: (g, 0, 0)),
        lam=pl.BlockSpec((None, 1, 2 * CG), lambda g, b: (g, 0, 0)),
    )


def _conv(x, xm2, xm1, xp1, cw_ref, cb_ref):
    return cw_ref[0:1, :] * xm2 + cw_ref[1:2, :] * xm1 + cw_ref[2:3, :] * x + cw_ref[3:4, :] * xp1 + cb_ref[...]


TC = 128
N_TC = TP // TC


def _split4(pre):
    return pre[:, :CG], pre[:, CG:2 * CG], pre[:, 2 * CG:3 * CG], pre[:, 3 * CG:]


def _rnn_fwd(xr, xg, cw):
    def body(xr_ref, xg_ref, cw_ref, cb_ref, w4_ref, b4_ref, lam_ref, y_ref, hf_ref, hb_ref, xc_s, af, bf, ab, bb):
        x = xr_ref[...]
        xc_s[...] = _conv(x, *_shifts(x), cw_ref, cb_ref)
        lam = lam_ref[...]

        def chunk(i, _):
            rows = pl.ds(pl.multiple_of(i * TC, TC), TC)
            xc = xc_s[rows, :]
            pre = _nn(xc.astype(BF), w4_ref[...]) + b4_ref[...]
            a_f, b_f, a_b, b_b = _gates(i * TC, xc, *_split4(pre), lam[:, :CG], lam[:, CG:])
            af[rows, :] = a_f
            bf[rows, :] = b_f
            ab[rows, :] = a_b
            bb[rows, :] = b_b
            return 0

        lax.fori_loop(0, N_TC, chunk, 0)
        _scan_pair(af, bf, hf_ref, ab, bb, hb_ref)
        y_ref[...] = (hf_ref[...] + hb_ref[...]) * jax.nn.gelu(xg_ref[...])

    sp = _rnn_specs()
    return pl.pallas_call(
        body, grid=(N_CG, NB), name="rnn_fwd",
        in_specs=[sp["seq"], sp["seq"], sp["cw"], sp["cb"], sp["w4"], sp["b4"], sp["lam"]],
        out_specs=[sp["seq"]] * 3, out_shape=[jax.ShapeDtypeStruct((R, D_RNN), F32)] * 3,
        scratch_shapes=[pltpu.VMEM((TP, CG), F32)] * 5,
        compiler_params=_params("arbitrary", "arbitrary"),
    )(xr, xg, cw["conv_w"], cw["conv_b"], cw["w4"], cw["b4"], cw["lam"])


def _rnn_bwd(dy, xr, xg, hf, hb, cw):
    def body(dy_ref, xr_ref, xg_ref, hf_ref, hb_ref, cw_ref, cb_ref, w4_ref, b4_ref, lam_ref,
             dxr_ref, dxg_ref, dcw_ref, dcb_ref, dw4_ref, db4_ref, dlam_ref,
             xc_s, af_s, ab_s, dhs_s, dhs2_s, lf_s, lb_s, daf_s, dab_s, dxc_s):
        @pl.when(pl.program_id(1) == 0)
        def _():
            for r in (dcw_ref, dcb_ref, dw4_ref, db4_ref, dlam_ref):
                r[...] = jnp.zeros_like(r)

        x = xr_ref[...]
        xc_s[...] = _conv(x, *_shifts(x), cw_ref, cb_ref)
        lam = lam_ref[...]

        def chunk1(i, _):
            rows = pl.ds(pl.multiple_of(i * TC, TC), TC)
            xc = xc_s[rows, :]
            pre = _nn(xc.astype(BF), w4_ref[...]) + b4_ref[...]
            a_f, _, a_b, _ = _gates(i * TC, xc, *_split4(pre), lam[:, :CG], lam[:, CG:])
            af_s[rows, :] = a_f
            ab_s[rows, :] = a_b
            _, vjp_y = jax.vjp(lambda h, g: h * jax.nn.gelu(g), hf_ref[rows, :] + hb_ref[rows, :], xg_ref[rows, :])
            dhs, dxg = vjp_y(dy_ref[rows, :])
            dhs_s[rows, :] = dhs
            dhs2_s[rows, :] = dhs
            dxg_ref[rows, :] = dxg
            return 0

        lax.fori_loop(0, N_TC, chunk1, 0)
        t = lax.broadcasted_iota(jnp.int32, (TP, CG), 0)
        af_s[...] = pltpu.roll(af_s[...], TP - 1, 0)
        ab_s[...] = pltpu.roll(ab_s[...], 1, 0)
        _scan_pair(ab_s, dhs_s, lb_s, af_s, dhs2_s, lf_s)
        daf_s[...] = lf_s[...] * jnp.where(t >= 1, pltpu.roll(hf_ref[...], 1, 0), 0.0)
        dab_s[...] = lb_s[...] * jnp.where(t < TP - 1, pltpu.roll(hb_ref[...], TP - 1, 0), 0.0)

        def chunk2(i, _):
            rows = pl.ds(pl.multiple_of(i * TC, TC), TC)
            xc = xc_s[rows, :]
            xcb = xc.astype(BF)
            pre = _nn(xcb, w4_ref[...]) + b4_ref[...]
            dxc, dpres, dlams = _gates_bwd(i * TC, xc, _split4(pre), (lam[:, :CG], lam[:, CG:]),
                                           (daf_s[rows, :], lf_s[rows, :], dab_s[rows, :], lb_s[rows, :]))
            dpre = jnp.concatenate(dpres, axis=1)
            dpreb = dpre.astype(BF)
            dxc_s[rows, :] = dxc + _nt(dpreb, w4_ref[...])
            dw4_ref[...] += _tn(xcb, dpreb)
            db4_ref[...] += jnp.sum(dpre, axis=0, keepdims=True)
            dlam_ref[...] += jnp.concatenate(dlams, axis=1)
            return 0

        lax.fori_loop(0, N_TC, chunk2, 0)
        dxc = dxc_s[...]
        dcb_ref[...] += jnp.sum(dxc, axis=0, keepdims=True)
        for tap, xs in enumerate(_shifts(x)[:2] + (x,) + _shifts(x)[2:]):
            dcw_ref[tap:tap + 1, :] += jnp.sum(xs * dxc, axis=0, keepdims=True)
        dxr_ref[...] = (cw_ref[0:1, :] * jnp.where(t < TP - 2, pltpu.roll(dxc, TP - 2, 0), 0.0)
                        + cw_ref[1:2, :] * jnp.where(t < TP - 1, pltpu.roll(dxc, TP - 1, 0), 0.0)
                        + cw_ref[2:3, :] * dxc
                        + cw_ref[3:4, :] * jnp.where(t >= 1, pltpu.roll(dxc, 1, 0), 0.0))

    sp = _rnn_specs()
    return pl.pallas_call(
        body, grid=(N_CG, NB), name="rnn_bwd",
        in_specs=[sp["seq"]] * 5 + [sp["cw"], sp["cb"], sp["w4"], sp["b4"], sp["lam"]],
        out_specs=[sp["seq"], sp["seq"], sp["cw"], sp["cb"], sp["w4"], sp["b4"], sp["lam"]],
        out_shape=[jax.ShapeDtypeStruct((R, D_RNN), F32), jax.ShapeDtypeStruct((R, D_RNN), F32),
                   jax.ShapeDtypeStruct((CONV_W, D_RNN), F32), jax.ShapeDtypeStruct((1, D_RNN), F32),
                   jax.ShapeDtypeStruct((N_CG, CG, 4 * CG), F32), jax.ShapeDtypeStruct((N_CG, 1, 4 * CG), F32),
                   jax.ShapeDtypeStruct((N_CG, 1, 2 * CG), F32)],
        scratch_shapes=[pltpu.VMEM((TP, CG), F32)] * 10,
        compiler_params=_params("arbitrary", "arbitrary"),
    )(dy, xr, xg, hf, hb, cw["conv_w"], cw["conv_b"], cw["w4"], cw["b4"], cw["lam"])


TD = 256
STAGE_D_VMEM = 58 * 1024 * 1024


def _stage_d(hp, o, y, tgt, cw):
    def body(hp_ref, o_ref, y_ref, tgt_ref, ga, gr, wout, ln2, wg, wu, wd,
             do_ref, dy_ref, dh1_ref, mix_ref, dh1b_ref, hn2_ref, dg_ref, du_ref, act_ref, dh2b_ref,
             loss_ref, dga_ref, dgr_ref, dln2_ref):
        i = pl.program_id(0)

        @pl.when(i == 0)
        def _():
            for r in (loss_ref, dga_ref, dgr_ref, dln2_ref):
                r[...] = jnp.zeros_like(r)

        mix_a, vjp_a = jax.vjp(lambda x, g: _rms(x, g, D_ATTN), o_ref[...], ga[...])
        mix_r, vjp_r = jax.vjp(lambda x, g: _rms(x, g, D_RNN), y_ref[...], gr[...])
        mab, mrb = mix_a.astype(BF), mix_r.astype(BF)
        mix_ref[:, :D_ATTN] = mab
        mix_ref[:, D_ATTN:] = mrb
        h1 = hp_ref[...] + _nn(mab, wout[:D_ATTN, :]) + _nn(mrb, wout[D_ATTN:, :])
        hn2, vjp_ln2 = jax.vjp(lambda x, g: _rms(x, g, D), h1, ln2[...])
        hn2b = hn2.astype(BF)
        hn2_ref[...] = hn2b
        act, vjp_act = jax.vjp(lambda g, u: jax.nn.silu(g) * u, _nt(hn2b, wg[...]), _nt(hn2b, wu[...]))
        actb = act.astype(BF)
        act_ref[...] = actb
        h2 = h1 + _nn(actb, wd[...])
        row = i * TD + lax.broadcasted_iota(jnp.int32, (TD, 1), 0)
        t = jnp.where(row >= TP, row - TP, row)
        err = jnp.where((t >= N_META) & (t < T), h2 - tgt_ref[...], 0.0)
        loss_ref[...] += jnp.sum(err * err) * (0.5 / D)
        dh2b = (err * (1.0 / D)).astype(BF)
        dh2b_ref[...] = dh2b
        dg, du = vjp_act(_nt(dh2b, wd[...]))
        dgb, dub = dg.astype(BF), du.astype(BF)
        dg_ref[...] = dgb
        du_ref[...] = dub
        dh1n, dln2 = vjp_ln2(_nn(dgb, wg[...]) + _nn(dub, wu[...]))
        dh1 = err * (1.0 / D) + dh1n
        dh1_ref[...] = dh1
        dh1b = dh1.astype(BF)
        dh1b_ref[...] = dh1b
        dmix = _nt(dh1b, wout[...])
        do, dga = vjp_a(dmix[:, :D_ATTN])
        dyr, dgr = vjp_r(dmix[:, D_ATTN:])
        do_ref[...] = do
        dy_ref[...] = dyr
        dga_ref[...] += dga
        dgr_ref[...] += dgr
        dln2_ref[...] += dln2

    rs = lambda n: _row_spec(n, TD)
    acc = lambda n: pl.BlockSpec((1, n), lambda i: (0, 0))
    return pl.pallas_call(
        body, grid=(R // TD,), name="stage_d",
        in_specs=[rs(D), rs(D_ATTN), rs(D_RNN), rs(D), _const_spec((1, D_ATTN)), _const_spec((1, D_RNN)),
                  _const_spec((D, D)), _const_spec((1, D)), _const_spec((D_FF, D)), _const_spec((D_FF, D)),
                  _const_spec((D_FF, D))],
        out_specs=[rs(D_ATTN), rs(D_RNN), rs(D), rs(D), rs(D), rs(D), rs(D_FF), rs(D_FF), rs(D_FF), rs(D),
                   acc(1), acc(D_ATTN), acc(D_RNN), acc(D)],
        out_shape=[jax.ShapeDtypeStruct((R, D_ATTN), F32), jax.ShapeDtypeStruct((R, D_RNN), F32),
                   jax.ShapeDtypeStruct((R, D), F32), jax.ShapeDtypeStruct((R, D), BF),
                   jax.ShapeDtypeStruct((R, D), BF), jax.ShapeDtypeStruct((R, D), BF),
                   jax.ShapeDtypeStruct((R, D_FF), BF), jax.ShapeDtypeStruct((R, D_FF), BF),
                   jax.ShapeDtypeStruct((R, D_FF), BF), jax.ShapeDtypeStruct((R, D), BF),
                   jax.ShapeDtypeStruct((1, 1), F32), jax.ShapeDtypeStruct((1, D_ATTN), F32),
                   jax.ShapeDtypeStruct((1, D_RNN), F32), jax.ShapeDtypeStruct((1, D), F32)],
        compiler_params=_params("arbitrary", vmem=STAGE_D_VMEM),
    )(hp, o, y, tgt, cw["ga"], cw["gr"], cw["wout"], cw["ln2_g"], cw["wg"], cw["wu"], cw["wd"])


TW = 2176


def _wgrad(a, b, name, tk=None):
    ka, nb = a.shape[1], b.shape[1]
    tk = ka if tk is None else tk

    def body(a_ref, b_ref, o_ref):
        @pl.when(pl.program_id(1) == 0)
        def _():
            o_ref[...] = jnp.zeros_like(o_ref)

        o_ref[...] += _tn(a_ref[...].astype(BF), b_ref[...].astype(BF))

    return pl.pallas_call(
        body, grid=(ka // tk, R // TW), name=name,
        in_specs=[pl.BlockSpec((TW, tk), lambda k, r: (r, k)), pl.BlockSpec((TW, nb), lambda k, r: (r, 0))],
        out_specs=pl.BlockSpec((tk, nb), lambda k, r: (k, 0)),
        out_shape=jax.ShapeDtypeStruct((ka, nb), F32),
        compiler_params=_params("arbitrary", "arbitrary"),
    )(a, b)


def _rope_tables():
    half = QK_ROPE // 2
    freqs = 1.0 / (ROPE_THETA ** (jnp.arange(half, dtype=F32) / half))
    ang = jnp.arange(TP, dtype=F32)[:, None] * freqs[None, :]
    ones = jnp.ones((TP, QK_NOPE), F32)
    zeros = jnp.zeros((TP, QK_NOPE), F32)
    pad1 = jnp.ones((TP, HP - QK_HEAD), F32)
    pad0 = jnp.zeros((TP, HP - QK_HEAD), F32)
    cs = jnp.concatenate([ones, jnp.cos(ang), jnp.cos(ang), pad1], axis=1)
    sn = jnp.concatenate([zeros, jnp.sin(ang), jnp.sin(ang), pad0], axis=1)
    return jnp.tile(cs, (NB, 1)), jnp.tile(sn, (NB, 1))


def _pad_rows(a, lo, hi):
    return jnp.pad(a, ((0, 0), (lo, hi), (0, 0)))


def _compute_weights(w):
    win_t = w["w_in_t"]
    kr = win_t[O_KR:O_KR + QK_ROPE]
    win = jnp.concatenate([win_t[:O_KR], jnp.zeros((QK_NOPE, D), F32), kr,
                           jnp.zeros((HP - QK_HEAD, D), F32), win_t[O_KR + QK_ROPE:]], axis=0)
    wq = _pad_rows(w["w_uq_t"].reshape(N_HEADS, QK_HEAD, Q_LORA), 0, HP - QK_HEAD)
    wkv = w["w_ukv_t"].reshape(N_HEADS, QK_NOPE + V_HEAD, KV_LORA)
    wk = _pad_rows(wkv[:, :QK_NOPE], 0, HP - QK_NOPE)
    wv = wkv[:, QK_NOPE:].reshape(D_ATTN, KV_LORA)
    gates = jnp.stack([w["lru_wa"][0], w["lru_wi"][0], w["lru_wa"][1], w["lru_wi"][1]])
    blk = gates.reshape(4, N_CG, 2, RNN_BW, RNN_BW)
    dense = jnp.einsum("tcaij,ab->tcaibj", blk, jnp.eye(2, dtype=F32)).reshape(4, N_CG, CG, CG)
    w4 = dense.transpose(1, 2, 0, 3).reshape(N_CG, CG, 4 * CG)
    bias = jnp.stack([w["lru_ba"][0], w["lru_bi"][0], w["lru_ba"][1], w["lru_bi"][1]])
    b4 = bias.reshape(4, N_CG, CG).transpose(1, 0, 2).reshape(N_CG, 1, 4 * CG)
    lam = w["lru_lambda"].reshape(2, N_CG, CG).transpose(1, 0, 2).reshape(N_CG, 1, 2 * CG)
    pad_g = lambda g: jnp.pad(g.reshape(1, QK_HEAD), ((0, 0), (0, HP - QK_HEAD)))
    return dict(
        ln1_g=w["ln1_g"].reshape(1, D), win=win.astype(BF), qa_g=w["q_a_norm_g"].reshape(1, Q_LORA),
        wq=wq.astype(BF).reshape(N_HEADS * HP, Q_LORA), kva_g=w["kv_a_norm_g"].reshape(1, KV_LORA),
        wk=wk.astype(BF).reshape(N_HEADS * HP, KV_LORA), wv=wv.astype(BF),
        q_g=pad_g(w["q_norm_g"]), k_g=pad_g(w["k_norm_g"]),
        conv_w=w["conv_w"].reshape(CONV_W, D_RNN), conv_b=w["conv_b"].reshape(1, D_RNN),
        w4=w4.astype(BF), b4=b4, lam=lam,
        ga=w["attn_out_g"].reshape(1, D_ATTN), gr=w["rnn_out_g"].reshape(1, D_RNN), ln2_g=w["ln2_g"].reshape(1, D),
    )


def _local_step(x, target, meta, w, late_weights, early_grads, mid_grads):
    cw = _compute_weights(w)
    cs, sn = _rope_tables()
    hp = jnp.concatenate([jnp.broadcast_to(meta[None], (NB, N_META, D)), x,
                          jnp.zeros((NB, TP - T, D), F32)], axis=1).reshape(R, D)
    tgt = _pad_rows(target, N_META, TP - T).reshape(R, D)

    pa, xr, xg, q, k, v = _stage_a_fwd(hp, cs, sn, cw)
    o, lse = _attn_fwd(q, k, v)
    y, hf, hb = _rnn_fwd(xr, xg, cw)
    late = late_weights([o, y])
    cw.update(wout=late["w_out"], wg=late["w_gate_t"], wu=late["w_up_t"], wd=late["w_down"])
    (do, dy, dh1, mixb, dh1b, hn2b, dgb, dub, actb, dh2b, loss, dga, dgr, dln2) = _stage_d(hp, o, y, tgt, cw)
    dwout = _wgrad(mixb, dh1b, "wgrad_out")
    dwg = _wgrad(dgb, hn2b, "wgrad_gate", tk=D_FF // 2)
    dwu = _wgrad(dub, hn2b, "wgrad_up", tk=D_FF // 2)
    dwd = _wgrad(actb, dh2b, "wgrad_down", tk=D_FF // 2)
    zero = early_grads(dict(w_out=dwout, w_gate=dwg, w_up=dwu, w_down=dwd))
    cw["conv_b"] = cw["conv_b"] + zero
    dxr, dxg, dcw, dcb, dw4, db4, dlam = _rnn_bwd(dy, xr, xg, hf, hb, cw)
    zero = mid_grads([dxr])
    dq, dk, dv = _attn_bwd(q, k, v, o, lse, do)
    (dhp, dpb, dqrawb, dkrawb, hn1b, cqnb, ckvnb, dln1, dqag, dkvag, dqg, dkg) = _stage_a_bwd(
        dq, dk, dv, dxr, dxg, dh1, hp, pa, cs, sn, dict(cw, qa_g=cw["qa_g"] + zero))

    dwin = _wgrad(dpb, hn1b, "wgrad_in", tk=PC // 2)
    dwq = _wgrad(dqrawb, cqnb, "wgrad_uq")
    dwk = _wgrad(dkrawb, ckvnb, "wgrad_uk")
    dwv = _wgrad(dv, ckvnb, "wgrad_uv")

    dwin_t = jnp.concatenate([dwin[:O_KR], dwin[O_KR + QK_NOPE:O_KR + QK_HEAD], dwin[O_XR:]], axis=0)
    dwq_t = dwq.reshape(N_HEADS, HP, Q_LORA)[:, :QK_HEAD].reshape(N_HEADS * QK_HEAD, Q_LORA)
    dwkv_t = jnp.concatenate([dwk.reshape(N_HEADS, HP, KV_LORA)[:, :QK_NOPE],
                              dwv.reshape(N_HEADS, V_HEAD, KV_LORA)], axis=1).reshape(2 * D_ATTN, KV_LORA)
    d4 = dw4.reshape(N_CG, 2, RNN_BW, 4, 2, RNN_BW)
    dgates = jnp.stack([d4[:, 0, :, :, 0, :], d4[:, 1, :, :, 1, :]], axis=1)
    dgates = dgates.transpose(3, 0, 1, 2, 4).reshape(4, N_HEADS, RNN_BW, RNN_BW)
    dbias = db4.reshape(N_CG, 4, CG).transpose(1, 0, 2).reshape(4, D_RNN)
    dhp3 = dhp.reshape(NB, TP, D)
    grads = dict(
        meta_tokens=jnp.sum(dhp3[:, :N_META], axis=0),
        ln1_g=dln1, w_in_t=dwin_t, q_a_norm_g=dqag, w_uq_t=dwq_t, kv_a_norm_g=dkvag, w_ukv_t=dwkv_t,
        q_norm_g=dqg[:, :QK_HEAD], k_norm_g=dkg[:, :QK_HEAD], conv_w=dcw[None], conv_b=dcb,
        lru_wa=jnp.stack([dgates[0], dgates[2]])[None], lru_ba=jnp.stack([dbias[0], dbias[2]])[None],
        lru_wi=jnp.stack([dgates[1], dgates[3]])[None], lru_bi=jnp.stack([dbias[1], dbias[3]])[None],
        lru_lambda=dlam.reshape(N_CG, 2, CG).transpose(1, 0, 2).reshape(1, 2, D_RNN),
        attn_out_g=dga, rnn_out_g=dgr, ln2_g=dln2,
    )
    return loss[0, 0], dhp3[:, N_META:T], grads, [dhp, dwin]


_ANY = pl.BlockSpec(memory_space=pl.ANY)


def _place():
    return lax.axis_index("x"), lax.axis_index("y"), lax.axis_index("c")


def _other_chips(x, y):
    return [(1 - x, y), (x, 1 - y), (1 - x, 1 - y)]


def _all_gather(arrs, name):
    n_arr = len(arrs)

    def body(*refs):
        x_refs, out_refs, zero_ref = refs[:n_arr], refs[n_arr:2 * n_arr], refs[2 * n_arr]
        send_sems, recv_sems, local_sems = refs[2 * n_arr + 1:]
        x, y, c = _place()
        me, sibling = (x, y, c), (x, y, 1 - c)
        chips = _other_chips(x, y)
        zero_ref[...] = jnp.zeros_like(zero_ref)

        def rows(a, px, py, pc):
            m = arrs[a].shape[0]
            return out_refs[a].at[pl.ds((4 * px + 2 * py + pc) * m, m), :]

        def copy(a, k, block, to, src=None):
            return pltpu.make_async_remote_copy(
                src_ref=rows(a, *block) if src is None else src, dst_ref=rows(a, *block),
                send_sem=send_sems.at[7 * a + k], recv_sem=recv_sems.at[7 * a + k], device_id=to, device_id_type=MESH)

        mine = [pltpu.make_async_copy(x_refs[a], rows(a, *me), local_sems.at[a]) for a in range(n_arr)]
        first, passed = [], []
        for a in range(n_arr):
            first.append(copy(a, 0, me, sibling, src=x_refs[a]))
            first += [copy(a, 1 + j, me, (*chip, c), src=x_refs[a]) for j, chip in enumerate(chips)]
        for cp in mine + first:
            cp.start()
        for a in range(n_arr):
            for j, chip in enumerate(chips):
                copy(a, 1 + j, (*chip, c), me).wait_recv()
                passed.append(copy(a, 4 + j, (*chip, c), sibling))
                passed[-1].start()
        for a in range(n_arr):
            copy(a, 0, sibling, me).wait_recv()
            for j, chip in enumerate(chips):
                copy(a, 4 + j, (*chip, 1 - c), me).wait_recv()
        for cp in first + passed:
            cp.wait_send()
        for cp in mine:
            cp.wait()

    outs = pl.pallas_call(
        body, name=name,
        out_shape=[jax.ShapeDtypeStruct((8 * a.shape[0], a.shape[1]), a.dtype) for a in arrs]
        + [jax.ShapeDtypeStruct((8, LANES), F32)],
        in_specs=[_ANY] * n_arr, out_specs=[_ANY] * n_arr + [pl.BlockSpec(memory_space=pltpu.VMEM)],
        scratch_shapes=[pltpu.SemaphoreType.DMA((7 * n_arr,)), pltpu.SemaphoreType.DMA((7 * n_arr,)),
                        pltpu.SemaphoreType.DMA((n_arr,))],
    )(*arrs)
    return outs[:n_arr], outs[n_arr]


def _pair_exchange(big, whole, name):
    n_s, _, m, n = big.shape
    n_copies = n_s + len(whole)

    def body(*refs):
        big_ref, whole_refs = refs[0], refs[1:1 + len(whole)]
        rbig_ref, rwhole_refs = refs[1 + len(whole)], refs[2 + len(whole):2 + 2 * len(whole)]
        send_sems, recv_sems = refs[-2:]
        x, y, c = _place()
        sibling = (x, y, 1 - c)
        copies = [pltpu.make_async_remote_copy(
            src_ref=big_ref.at[s, 1 - c], dst_ref=rbig_ref.at[s], send_sem=send_sems.at[s], recv_sem=recv_sems.at[s],
            device_id=sibling, device_id_type=MESH) for s in range(n_s)]
        copies += [pltpu.make_async_remote_copy(
            src_ref=a, dst_ref=r, send_sem=send_sems.at[n_s + i], recv_sem=recv_sems.at[n_s + i],
            device_id=sibling, device_id_type=MESH) for i, (a, r) in enumerate(zip(whole_refs, rwhole_refs))]
        for cp in copies:
            cp.start()
        for cp in copies:
            cp.wait()

    return pl.pallas_call(
        body, name=name,
        out_shape=[jax.ShapeDtypeStruct((n_s, m, n), big.dtype)] + [jax.ShapeDtypeStruct(a.shape, a.dtype) for a in whole],
        in_specs=[_ANY] * (1 + len(whole)), out_specs=[_ANY] * (1 + len(whole)),
        scratch_shapes=[pltpu.SemaphoreType.DMA((n_copies,)), pltpu.SemaphoreType.DMA((n_copies,))],
    )(big, *whole)


_HBM = pl.BlockSpec(memory_space=pltpu.HBM)
_SEM = pl.BlockSpec(memory_space=pltpu.SEMAPHORE)
_EFFECT = pltpu.SideEffectType.DATAFLOW_SIDE_EFFECTING


def _split_copies(src_refs, land_refs, sems, plan, sending):
    n = len(sems) // 2
    return [pltpu.make_async_remote_copy(src_ref=s, dst_ref=d, send_sem=sems[k], recv_sem=sems[n + k], device_id=to,
                                         device_id_type=MESH)
            for k, (s, d, to) in enumerate(plan(src_refs, land_refs, sending))]


def _to_chips(src_at, land_at):
    def plan(src_refs, land_refs, sending):
        x, y, c = _place()
        return [(src_at(s, tx, ty, c), land_at(l, j, *((x, y) if sending else (tx, ty)), c), (tx, ty, c))
                for s, l in zip(src_refs, land_refs) for j, (tx, ty) in enumerate(_other_chips(x, y))]
    return plan


def _to_sibling(src_refs, land_refs, sending):
    x, y, c = _place()
    return [(s.at[k, 1 - c], l.at[k], (x, y, 1 - c)) for s, l in zip(src_refs, land_refs) for k in range(N_CHIPS)]


def _split_start(name, srcs, lands, plan, n):
    srcs, lands = list(srcs), list(lands)
    k = len(srcs)

    def body(*refs):
        outs = refs[2 * k:]
        for cp in _split_copies(refs[:k], refs[k:2 * k], outs[:2 * n], plan, True):
            cp.start()
        outs[2 * n + 2 * k][...] = jnp.zeros_like(outs[2 * n + 2 * k])

    outs = pl.pallas_call(
        body, name=name,
        out_shape=(pltpu.SemaphoreType.DMA(()),) * (2 * n) + tuple(pltpu.HBM(a.shape, a.dtype) for a in srcs + lands)
        + (jax.ShapeDtypeStruct((8, LANES), F32),),
        in_specs=(_HBM,) * (2 * k),
        out_specs=(_SEM,) * (2 * n) + (_HBM,) * (2 * k) + (pl.BlockSpec(memory_space=pltpu.VMEM),),
        input_output_aliases={i: 2 * n + i for i in range(2 * k)},
        compiler_params=pltpu.CompilerParams(has_side_effects=_EFFECT),
    )(*[pltpu.with_memory_space_constraint(a, pltpu.HBM) for a in srcs + lands])
    return outs[:2 * n], list(outs[2 * n:2 * n + k]), list(outs[2 * n + k:2 * n + 2 * k]), outs[2 * n + 2 * k]


def _split_wait(name, sems, srcs, lands, after, plan):
    srcs, lands = list(srcs), list(lands)
    k = len(srcs)

    def body(*refs):
        for cp in _split_copies(refs[:k], refs[k:2 * k], refs[2 * k:2 * k + len(sems)], plan, False):
            cp.wait_send()
            cp.wait_recv()

    outs = pl.pallas_call(
        body, name=name, out_shape=tuple(pltpu.HBM(a.shape, a.dtype) for a in srcs + lands),
        in_specs=(_HBM,) * (2 * k) + (_SEM,) * len(sems) + (_ANY,) * len(after), out_specs=(_HBM,) * (2 * k),
        input_output_aliases={i: i for i in range(2 * k)}, compiler_params=pltpu.CompilerParams(has_side_effects=_EFFECT),
    )(*srcs, *lands, *sems, *after)
    return list(outs[:k]), list(outs[k:])


def _gather_finish(lands, pieces):
    k = len(lands)

    def body(*refs):
        land_refs, piece_refs, out_refs, stages = refs[:k], refs[k:2 * k], refs[2 * k:3 * k], refs[3 * k:4 * k]
        send_sems, recv_sems, load_sems, store_sems = refs[4 * k:]
        x, y, c = _place()
        sibling = (x, y, 1 - c)
        remote, loads, stores, arrivals = [], [], [], []
        for a in range(k):
            m = lands[a].shape[0] // 8

            def rows(px, py, pc, ref, m=m):
                return ref.at[pl.ds((4 * px + 2 * py + pc) * m, m), :]

            for j, (tx, ty) in enumerate(_other_chips(x, y)):
                sems = dict(send_sem=send_sems.at[3 * a + j], recv_sem=recv_sems.at[3 * a + j], device_id=sibling,
                            device_id_type=MESH)
                remote.append(pltpu.make_async_remote_copy(
                    src_ref=rows(tx, ty, c, land_refs[a]), dst_ref=rows(tx, ty, c, out_refs[a]), **sems))
                arrivals.append(pltpu.make_async_remote_copy(
                    src_ref=rows(tx, ty, 1 - c, out_refs[a]), dst_ref=rows(tx, ty, 1 - c, out_refs[a]), **sems))
            for h in range(2):
                loads.append(pltpu.make_async_copy(piece_refs[a].at[pl.ds(h * m, m), :], stages[a].at[h],
                                                   load_sems.at[2 * a + h]))
                stores.append(pltpu.make_async_copy(stages[a].at[h], rows(x, y, h, out_refs[a]), store_sems.at[2 * a + h]))
        for cp in remote + loads:
            cp.start()
        for ld, st in zip(loads, stores):
            ld.wait()
            st.start()
        for cp, arrival in zip(remote, arrivals):
            cp.wait_send()
            arrival.wait_recv()
        for cp in stores:
            cp.wait()

    return pl.pallas_call(
        body, name="gather_late_finish", out_shape=[jax.ShapeDtypeStruct(a.shape, a.dtype) for a in lands],
        in_specs=[_ANY] * (2 * k), out_specs=[_ANY] * k, input_output_aliases={i: i for i in range(k)},
        scratch_shapes=[pltpu.VMEM((2, a.shape[0] // 8, a.shape[1]), a.dtype) for a in lands]
        + [pltpu.SemaphoreType.DMA((3 * k,)), pltpu.SemaphoreType.DMA((3 * k,)), pltpu.SemaphoreType.DMA((2 * k,)),
           pltpu.SemaphoreType.DMA((2 * k,))],
    )(*lands, *pieces)


def _pair_fill(bufs, name):
    k = len(bufs)

    def body(*refs):
        send_sems, recv_sems = refs[-2:]
        x, y, c = _place()
        copies = [pltpu.make_async_remote_copy(
            src_ref=refs[i].at[c], dst_ref=refs[k + i].at[c], send_sem=send_sems.at[i], recv_sem=recv_sems.at[i],
            device_id=(x, y, 1 - c), device_id_type=MESH) for i in range(k)]
        for cp in copies:
            cp.start()
        for i, cp in enumerate(copies):
            cp.wait_send()
            pltpu.make_async_remote_copy(
                src_ref=refs[i].at[1 - c], dst_ref=refs[k + i].at[1 - c], send_sem=send_sems.at[i],
                recv_sem=recv_sems.at[i], device_id=(x, y, 1 - c), device_id_type=MESH).wait_recv()

    return pl.pallas_call(
        body, name=name, out_shape=[jax.ShapeDtypeStruct(a.shape, a.dtype) for a in bufs], in_specs=[_ANY] * k,
        out_specs=[_ANY] * k, input_output_aliases={i: i for i in range(k)},
        scratch_shapes=[pltpu.SemaphoreType.DMA((k,)), pltpu.SemaphoreType.DMA((k,))],
    )(*bufs)


def _row_tile(rows, cap=512):
    for t in range(cap - cap % 8, 7, -8):
        if rows % t == 0:
            return t
    return rows


def _elementwise(fn, n_out, name, *arrs, out_dtype=F32):
    rows, cols = arrs[0].shape
    tr = _row_tile(rows)
    n_in = len(arrs)

    def body(*refs):
        outs = fn(*[r[...].astype(F32) for r in refs[:n_in]])
        for r, o in zip(refs[n_in:], outs):
            r[...] = o.astype(out_dtype)

    spec = pl.BlockSpec((tr, cols), lambda i: (i, 0))
    return pl.pallas_call(
        body, grid=(rows // tr,), name=name, in_specs=[spec] * n_in, out_specs=[spec] * n_out,
        out_shape=[jax.ShapeDtypeStruct((rows, cols), out_dtype)] * n_out, compiler_params=_params("arbitrary"),
    )(*arrs)


def _pair_sums(gpacks, rbigs, ci, name):
    k = len(gpacks)

    def body(c_ref, *refs):
        for g_ref, r_ref, o_ref in zip(refs[:k], refs[k:2 * k], refs[2 * k:]):
            o_ref[...] = (g_ref[...] + r_ref[...]).astype(BF)

    half = lambda a: pl.BlockSpec((None,) + a.shape[1:], lambda s, c: (s, 0, 0))
    return pl.pallas_call(
        body, name=name, out_shape=[jax.ShapeDtypeStruct(r.shape, BF) for r in rbigs],
        grid_spec=pltpu.PrefetchScalarGridSpec(
            num_scalar_prefetch=1, grid=(N_CHIPS,),
            in_specs=[pl.BlockSpec((None, None) + g.shape[2:], lambda s, c: (s, c[0], 0, 0)) for g in gpacks]
            + [half(r) for r in rbigs],
            out_specs=[half(r) for r in rbigs]),
        compiler_params=_params("arbitrary"),
    )(ci.reshape(1), *gpacks, *rbigs)


def _chip_sums(sums, landed, chip, ci, name):
    k = len(sums)

    def body(p_ref, *refs):
        for own_ref, land_ref, o_ref in zip(refs[:k], refs[k:2 * k], refs[2 * k:]):
            f = lambda v: v.astype(F32)
            o_ref[...] = _add4(f(own_ref[...]), f(land_ref[0]), f(land_ref[1]), f(land_ref[2]))[0]

    return pl.pallas_call(
        body, name=name, out_shape=[jax.ShapeDtypeStruct((2,) + s.shape[1:], F32) for s in sums],
        grid_spec=pltpu.PrefetchScalarGridSpec(
            num_scalar_prefetch=1, grid=(1,),
            in_specs=[pl.BlockSpec((None,) + s.shape[1:], lambda i, p: (p[0], 0, 0)) for s in sums]
            + [pl.BlockSpec(l.shape, lambda i, p: (0, 0, 0)) for l in landed],
            out_specs=[pl.BlockSpec((None,) + s.shape[1:], lambda i, p: (p[1], 0, 0)) for s in sums]),
        compiler_params=_params("arbitrary"),
    )(jnp.stack([chip, ci]), *sums, *landed)


def _add2(a, b):
    return (a + b,)


def _add4(own, r0, r1, r2):
    return ((own + r2) + (r0 + r1),)


def _adamw_math(w, g, m, v):
    m = ADAM_B1 * m + (1.0 - ADAM_B1) * g
    v = ADAM_B2 * v + (1.0 - ADAM_B2) * (g * g)
    m_hat = m / (1.0 - ADAM_B1 ** ADAM_STEP)
    v_hat = v / (1.0 - ADAM_B2 ** ADAM_STEP)
    delta = -ADAM_LR * (m_hat / (jnp.sqrt(v_hat) + ADAM_EPS) + ADAM_WD * w)
    return delta, m, v


WEIGHTS = ["meta_tokens", "ln1_g", "w_in", "q_a_norm_g", "w_uq", "kv_a_norm_g", "w_ukv", "q_norm_g", "k_norm_g",
           "conv_w", "conv_b", "lru_wa", "lru_ba", "lru_wi", "lru_bi", "lru_lambda", "attn_out_g", "rnn_out_g",
           "w_out", "ln2_g", "w_gate", "w_up", "w_down"]
BIG = ["w_in", "w_uq", "w_ukv", "w_out", "w_gate", "w_up", "w_down"]
BIG_T = {"w_in": True, "w_uq": True, "w_ukv": True, "w_out": False, "w_gate": True, "w_up": True, "w_down": False}
BIG_ROWS = {"w_in": 424, "w_uq": 72, "w_ukv": 64, "w_out": 256, "w_gate": 704, "w_up": 704, "w_down": 704}
EARLY = ["w_in", "w_uq", "w_ukv"]
LATE = ["w_out", "w_gate", "w_up", "w_down"]
EARLY_ROWS = 576
SMALL_SHARDED = ["meta_tokens", "conv_w", "lru_ba", "lru_bi", "lru_lambda"]
SMALL = [n for n in WEIGHTS if n not in BIG]
SMALL_PACK_ROWS = 160
SMALL_ADAM_ROWS = 144


def _offsets(names):
    off, o = {}, 0
    for n in names:
        off[n] = o
        o += BIG_ROWS[n]
    return off


def _shard_pack(names, src, rows):
    parts = [_to_pack_piece(n, src[n]) for n in names]
    used = sum(BIG_ROWS[n] for n in names)
    if rows > used:
        parts.append(jnp.zeros((rows - used, D), F32))
    return jnp.concatenate(parts, axis=0)


def _grad_pack(names, g, rows):
    parts = [g[n].reshape(N_CHIPS, BIG_ROWS[n], D) for n in names]
    used = sum(BIG_ROWS[n] for n in names)
    if rows > used:
        parts.append(jnp.zeros((N_CHIPS, rows - used, D), F32))
    return jnp.concatenate(parts, axis=1).reshape(N_CHIPS, 2, rows // 2, D)


def _to_pack_piece(name, shard):
    a = shard[0].T if BIG_T[name] else shard[0]
    return a.reshape(BIG_ROWS[name], D)


def _flat_pack(arrs, rows):
    flat = jnp.concatenate([a.reshape(-1) for a in arrs])
    return jnp.pad(flat, (0, rows * D - flat.shape[0])).reshape(rows, D)


def _flat_unpack(pack, shapes):
    flat, out, o = pack.reshape(-1), [], 0
    for s in shapes:
        n = math.prod(s)
        out.append(flat[o:o + n].reshape(s))
        o += n
    return out


def kernel(x, meta_tokens, ln1_g, w_in, q_a_norm_g, w_uq, kv_a_norm_g, w_ukv, q_norm_g, k_norm_g, conv_w, conv_b, lru_wa, lru_ba, lru_wi, lru_bi, lru_lambda, attn_out_g, rnn_out_g, w_out, ln2_g, w_gate, w_up, w_down, loss_target, m_meta_tokens, m_ln1_g, m_w_in, m_q_a_norm_g, m_w_uq, m_kv_a_norm_g, m_w_ukv, m_q_norm_g, m_k_norm_g, m_conv_w, m_conv_b, m_lru_wa, m_lru_ba, m_lru_wi, m_lru_bi, m_lru_lambda, m_attn_out_g, m_rnn_out_g, m_w_out, m_ln2_g, m_w_gate, m_w_up, m_w_down, v_meta_tokens, v_ln1_g, v_w_in, v_q_a_norm_g, v_w_uq, v_kv_a_norm_g, v_w_ukv, v_q_norm_g, v_k_norm_g, v_conv_w, v_conv_b, v_lru_wa, v_lru_ba, v_lru_wi, v_lru_bi, v_lru_lambda, v_attn_out_g, v_rnn_out_g, v_w_out, v_ln2_g, v_w_gate, v_w_up, v_w_down):
    wts = dict(zip(WEIGHTS, (meta_tokens, ln1_g, w_in, q_a_norm_g, w_uq, kv_a_norm_g, w_ukv, q_norm_g, k_norm_g, conv_w, conv_b, lru_wa, lru_ba, lru_wi, lru_bi, lru_lambda, attn_out_g, rnn_out_g, w_out, ln2_g, w_gate, w_up, w_down)))
    mom = dict(zip(WEIGHTS, (m_meta_tokens, m_ln1_g, m_w_in, m_q_a_norm_g, m_w_uq, m_kv_a_norm_g, m_w_ukv, m_q_norm_g, m_k_norm_g, m_conv_w, m_conv_b, m_lru_wa, m_lru_ba, m_lru_wi, m_lru_bi, m_lru_lambda, m_attn_out_g, m_rnn_out_g, m_w_out, m_ln2_g, m_w_gate, m_w_up, m_w_down)))
    var = dict(zip(WEIGHTS, (v_meta_tokens, v_ln1_g, v_w_in, v_q_a_norm_g, v_w_uq, v_kv_a_norm_g, v_w_ukv, v_q_norm_g, v_k_norm_g, v_conv_w, v_conv_b, v_lru_wa, v_lru_ba, v_lru_wi, v_lru_bi, v_lru_lambda, v_attn_out_g, v_rnn_out_g, v_w_out, v_ln2_g, v_w_gate, v_w_up, v_w_down)))
    xi, yi, ci = _place()
    chip = 2 * xi + yi
    off_e = _offsets(EARLY)
    half_e = EARLY_ROWS // 2
    gather_plan = _to_chips(lambda ref, tx, ty, c: ref.at[pl.ds(c * (ref.shape[0] // 2), ref.shape[0] // 2), :],
                            lambda ref, j, px, py, c: ref.at[pl.ds((4 * px + 2 * py + c) * (ref.shape[0] // 8),
                                                                   ref.shape[0] // 8), :])
    scatter_plan = _to_chips(lambda ref, tx, ty, c: ref.at[2 * tx + ty], lambda ref, j, px, py, c: ref.at[j])
    everywhere = _to_chips(lambda ref, tx, ty, c: ref, lambda ref, j, px, py, c: ref.at[j])
    n_late = len(LATE)

    pack_e = _shard_pack(EARLY, wts, EARLY_ROWS).astype(BF)
    spack = jnp.concatenate([meta_tokens[:, :LANES], meta_tokens[:, LANES:], conv_w[0], lru_ba[0], lru_bi[0],
                             lru_lambda[0], jnp.zeros((6, LANES), F32)], axis=0)
    (ge, gs), gathered = _all_gather([lax.dynamic_slice_in_dim(pack_e, ci * half_e, half_e, axis=0),
                                      lax.dynamic_slice_in_dim(spack, ci * 24, 24, axis=0)], "gather_early")
    ge = ge.reshape(N_CHIPS, EARLY_ROWS, D)
    gs = gs.reshape(N_CHIPS, 48, LANES)
    full = {n: ge[:, off_e[n]:off_e[n] + BIG_ROWS[n]] for n in EARLY}
    pieces_l = [(_to_pack_piece(n, wts[n]) + gathered[0, 0]).astype(BF) for n in LATE]
    sems_l, src_l, land_l, tied = _split_start(
        "gather_late_start", pieces_l, [lax.empty((N_CHIPS * BIG_ROWS[n], D), BF) for n in LATE], gather_plan, 3 * n_late)

    def late_weights(after):
        pieces, lands = _split_wait("gather_late_wait", sems_l, src_l, land_l, after, gather_plan)
        w_out_, w_gate_, w_up_, w_down_ = _gather_finish(lands, pieces)
        return dict(w_out=w_out_, w_gate_t=w_gate_, w_up_t=w_up_, w_down=w_down_)

    pair, late = {}, {}

    def early_grads(g_late):
        halves = [g_late[n].reshape(N_CHIPS, 2, BIG_ROWS[n] // 2, D) for n in LATE]
        pair["sems"], pair["src"], pair["land"], zeros = _split_start(
            "grad_pair_late_start", halves, [lax.empty((N_CHIPS, BIG_ROWS[n] // 2, D), F32) for n in LATE], _to_sibling,
            N_CHIPS * n_late)
        return zeros[0, 0]

    def mid_grads(after):
        halves, landed = _split_wait("grad_pair_late_wait", pair["sems"], pair["src"], pair["land"], after, _to_sibling)
        chip_sums = _pair_sums(halves, landed, ci, "grad_pair_sum_late")
        late["sems"], late["src"], late["land"], zeros = _split_start(
            "grad_chip_late_start", chip_sums, [lax.empty((3, BIG_ROWS[n] // 2, D), BF) for n in LATE], scatter_plan,
            3 * n_late)
        return zeros[0, 0]

    cols = lambda a: a.transpose(1, 0, 2).reshape(a.shape[1], N_CHIPS * a.shape[2])
    meta_full = cols(jnp.concatenate([gs[:, 0:16], gs[:, 16:32]], axis=2))
    w = dict(
        w_in_t=full["w_in"].reshape(IN_COLS, D), w_uq_t=full["w_uq"].reshape(N_HEADS * QK_HEAD, Q_LORA),
        w_ukv_t=full["w_ukv"].reshape(2 * D_ATTN, KV_LORA),
        ln1_g=ln1_g, q_a_norm_g=q_a_norm_g, kv_a_norm_g=kv_a_norm_g, q_norm_g=q_norm_g, k_norm_g=k_norm_g,
        conv_w=cols(gs[:, 32:36]), conv_b=conv_b, lru_wa=lru_wa[0], lru_ba=cols(gs[:, 36:38]), lru_wi=lru_wi[0],
        lru_bi=cols(gs[:, 38:40]), lru_lambda=cols(gs[:, 40:42]), attn_out_g=attn_out_g, rnn_out_g=rnn_out_g,
        ln2_g=ln2_g,
    )

    loss_local, grad_x, g, last = _local_step(x, loss_target, meta_full + tied[0, 0], w, late_weights, early_grads,
                                              mid_grads)

    gpack = _grad_pack(EARLY, {"w_in": g["w_in_t"], "w_uq": g["w_uq_t"], "w_ukv": g["w_ukv_t"]}, EARLY_ROWS)
    full_shapes = {n: wts[n].shape for n in SMALL}
    full_shapes.update(meta_tokens=(N_META, D), conv_w=(1, CONV_W, D_RNN), lru_ba=(1, 2, D_RNN), lru_bi=(1, 2, D_RNN),
                       lru_lambda=(1, 2, D_RNN))
    gsmall = _flat_pack([g[n] for n in SMALL] + [loss_local], SMALL_PACK_ROWS)
    rbig, rsmall = _pair_exchange(gpack, [gsmall], "grad_pair_exchange")
    chip_big = _pair_sums([gpack], [rbig], ci, "grad_pair_sum")
    (chip_small,) = _elementwise(_add2, 1, "grad_pair_sum_small", gsmall, rsmall)
    sems_e, src_e, land_e, zero_e = _split_start(
        "grad_chip_early_start", chip_big, [lax.empty((3, half_e, D), BF)], scatter_plan, 3)
    sems_s, src_s, land_s, zero_s = _split_start(
        "grad_small_start", [chip_small], [lax.empty((3, SMALL_PACK_ROWS, D), F32)], everywhere, 3)

    grads, delta, new_m, new_v = {}, {}, {}, {}

    def adamw_big(n, gshard):
        _, k, cols = wts[n].shape
        as_rows = (lambda a: a[0].T) if BIG_T[n] else (lambda a: a[0])
        back = (lambda a: a.T[None]) if BIG_T[n] else (lambda a: a[None])
        g2 = gshard.reshape((cols, k) if BIG_T[n] else (k, cols))
        d_, m_, v_ = _elementwise(_adamw_math, 3, "adamw_" + n, as_rows(wts[n]), g2, as_rows(mom[n]), as_rows(var[n]))
        grads[n], delta[n], new_m[n], new_v[n] = back(g2), back(d_), back(m_), back(v_)
        return d_

    sums, landed = _split_wait("grad_chip_late_wait", late["sems"], late["src"], late["land"], last + [zero_e, zero_s],
                               scatter_plan)
    shards_l = _pair_fill(_chip_sums(sums, landed, chip, ci, "grad_chip_sum_late"), "grad_pair_fill_late")
    done_late = [adamw_big(n, buf) for n, buf in zip(LATE, shards_l)][-1]
    src_e, land_e = _split_wait("grad_chip_early_wait", sems_e, src_e, land_e, [done_late], scatter_plan)
    src_s, land_s = _split_wait("grad_small_wait", sems_s, src_s, land_s, [done_late], everywhere)
    (shard_e,) = _pair_fill(_chip_sums(src_e, land_e, chip, ci, "grad_chip_sum"), "grad_pair_fill_early")
    shard_e = shard_e.reshape(EARLY_ROWS, D)
    for n in EARLY:
        adamw_big(n, shard_e[off_e[n]:off_e[n] + BIG_ROWS[n]])
    (small_sum,) = _elementwise(_add4, 1, "grad_chip_sum_small", src_s[0], land_s[0][0], land_s[0][1], land_s[0][2])
    *small_grads, loss = _flat_unpack(small_sum, [full_shapes[n] for n in SMALL] + [()])
    small_full = dict(zip(SMALL, small_grads))
    for n in SMALL:
        a = small_full[n]
        if n in SMALL_SHARDED:
            width = wts[n].shape[-1]
            a = lax.dynamic_slice_in_dim(a, chip * width, width, axis=a.ndim - 1)
        grads[n] = a.reshape(wts[n].shape)

    packs =[_flat_pack([src[n] for n in SMALL], SMALL_ADAM_ROWS) for src in (wts, grads, mom, var)]
    outs = _elementwise(_adamw_math, 3, "adamw_small", *packs)
    for dst, o in zip((delta, new_m, new_v), outs):
        dst.update(zip(SMALL, _flat_unpack(o, [wts[n].shape for n in SMALL])))

    return (loss, grad_x, *[grads[n] for n in WEIGHTS], *[delta[n] for n in WEIGHTS],
            *[new_m[n] for n in WEIGHTS], *[new_v[n] for n in WEIGHTS])
```

```python
import functools
import math

import jax
import jax.numpy as jnp
from jax import lax
from jax.experimental import pallas as pl
from jax.experimental.pallas import tpu as pltpu

F32 = jnp.float32
BF = jnp.bfloat16
MESH = pl.DeviceIdType.MESH

D = 1024
SEQ = 2048
N_META = 16
T = N_META + SEQ
N_HEADS = 8
QK_NOPE = 64
QK_ROPE = 32
QK_HEAD = 96
V_HEAD = 64
Q_LORA = 384
KV_LORA = 256
D_ATTN = 512
D_RNN = 512
RNN_BW = 64
CONV_W = 4
LRU_C = 8.0
ROPE_THETA = 10000.0
D_FF = 2816
EPS = 1e-6
IN_COLS = 1696
ADAM_LR, ADAM_B1, ADAM_B2, ADAM_EPS, ADAM_WD, ADAM_STEP = 0.001, 0.9, 0.999, 1e-08, 0.01, 10

LANES = 128
TP = 2176
NB = 2
R = NB * TP
TR = 256
TRF = 256
TQ = 544
HP = LANES
PC = 1792
O_CKV, O_KR, O_XR, O_XG = 384, 640, 768, 1280
CG = 128
N_CG = D_RNN // CG
VMEM_LIMIT = 56 * 1024 * 1024
N_CHIPS = 4
SCALE = QK_HEAD ** -0.5
KEY_MASK = -30000.0
LOG2_E = 1.4426950408889634
SCALE_LOG2 = SCALE * LOG2_E


def _nt(a, b):
    return lax.dot_general(a, b, (((1,), (1,)), ((), ())), preferred_element_type=F32)


def _nn(a, b):
    return jnp.dot(a, b, preferred_element_type=F32)


def _tn(a, b):
    return lax.dot_general(a, b, (((0,), (0,)), ((), ())), preferred_element_type=F32)


def _rms(x, g, n):
    ms = jnp.sum(x * x, axis=-1, keepdims=True) * (1.0 / n)
    return x * lax.rsqrt(ms + EPS) * g


def _rot_impl(x):
    lane = lax.broadcasted_iota(jnp.int32, x.shape, 1)
    left = pltpu.roll(x, HP - 16, 1)
    right = pltpu.roll(x, 16, 1)
    lo = (lane >= QK_NOPE) & (lane < QK_NOPE + 16)
    hi = (lane >= QK_NOPE + 16) & (lane < QK_HEAD)
    return jnp.where(lo, -left, jnp.where(hi, right, 0.0))


@jax.custom_vjp
def _rot(x):
    return _rot_impl(x)


def _rot_fwd(x):
    return _rot_impl(x), None


def _rot_bwd(_, g):
    return (-_rot_impl(g),)


_rot.defvjp(_rot_fwd, _rot_bwd)


def _head(x, g, cs, sn):
    n = _rms(x, g, QK_HEAD)
    return n * cs + _rot(n) * sn


def _head_bwd(x, g, cs, sn, dout):
    rs = lax.rsqrt(jnp.sum(x * x, axis=-1, keepdims=True) * (1.0 / QK_HEAD) + EPS)
    xh = x * rs
    dn = dout * cs - _rot_impl(dout * sn)
    gdn = g * dn
    t = jnp.sum(gdn * xh, axis=-1, keepdims=True) * (1.0 / QK_HEAD)
    return rs * (gdn - xh * t), jnp.sum(dn * xh, axis=0, keepdims=True)


def _const_spec(shape):
    return pl.BlockSpec(shape, lambda *_: (0,) * len(shape), pipeline_mode=pl.Buffered(1))


def _row_spec(n, tr=TR):
    return pl.BlockSpec((tr, n), lambda i: (i, 0))


def _params(*sem, vmem=VMEM_LIMIT):
    return pltpu.CompilerParams(dimension_semantics=sem, vmem_limit_bytes=vmem)


def _stage_a_fwd(hp, cs, sn, cw):
    def body(hp_ref, cs_ref, sn_ref, ln1, win, qag, wq, kvag, wk, wv, qg, kg,
             pa_ref, xr_ref, xg_ref, q_ref, k_ref, v_ref):
        hn = _rms(hp_ref[...], ln1[...], D).astype(BF)
        p = _nt(hn, win[...])
        pa_ref[...] = p[:, :O_XR]
        xr_ref[...] = p[:, O_XR:O_XG]
        xg_ref[...] = p[:, O_XG:]
        cqn = _rms(p[:, :O_CKV], qag[...], Q_LORA).astype(BF)
        ckvn = _rms(p[:, O_CKV:O_KR], kvag[...], KV_LORA).astype(BF)
        kr = p[:, O_KR:O_XR]
        c, s = cs_ref[...], sn_ref[...]
        mask_lane = lax.broadcasted_iota(jnp.int32, (1, HP), 1) == QK_HEAD
        row = pl.program_id(0) * TRF + lax.broadcasted_iota(jnp.int32, (TRF, 1), 0)
        key_mask = jnp.where(jnp.where(row >= TP, row - TP, row) < T, 0.0, KEY_MASK)
        qraw = _nt(cqn, wq[...])
        kraw = _nt(ckvn, wk[...])
        for h in range(N_HEADS):
            sl = slice(h * HP, (h + 1) * HP)
            q_ref[:, sl] = jnp.where(mask_lane, 1.0, _head(qraw[:, sl], qg[...], c, s)).astype(BF)
            k_ref[:, sl] = jnp.where(mask_lane, key_mask, _head(kraw[:, sl] + kr, kg[...], c, s)).astype(BF)
        v_ref[...] = _nt(ckvn, wv[...]).astype(BF)

    rs = lambda n: _row_spec(n, TRF)
    return pl.pallas_call(
        body, grid=(R // TRF,), name="stage_a_fwd",
        in_specs=[rs(D), rs(HP), rs(HP), _const_spec((1, D)), _const_spec((PC, D)),
                  _const_spec((1, Q_LORA)), _const_spec((N_HEADS * HP, Q_LORA)), _const_spec((1, KV_LORA)),
                  _const_spec((N_HEADS * HP, KV_LORA)), _const_spec((D_ATTN, KV_LORA)), _const_spec((1, HP)),
                  _const_spec((1, HP))],
        out_specs=[rs(O_XR), rs(D_RNN), rs(D_RNN), rs(N_HEADS * HP), rs(N_HEADS * HP), rs(D_ATTN)],
        out_shape=[jax.ShapeDtypeStruct((R, O_XR), F32), jax.ShapeDtypeStruct((R, D_RNN), F32),
                   jax.ShapeDtypeStruct((R, D_RNN), F32), jax.ShapeDtypeStruct((R, N_HEADS * HP), BF),
                   jax.ShapeDtypeStruct((R, N_HEADS * HP), BF), jax.ShapeDtypeStruct((R, D_ATTN), BF)],
        compiler_params=_params("arbitrary"),
    )(hp, cs, sn, cw["ln1_g"], cw["win"], cw["qa_g"], cw["wq"], cw["kva_g"], cw["wk"], cw["wv"], cw["q_g"], cw["k_g"])


def _stage_a_bwd(dq, dk, dv, dxr, dxg, dh1, hp, pa, cs, sn, cw):
    def body(dq_ref, dk_ref, dv_ref, dxr_ref, dxg_ref, dh1_ref, hp_ref, pa_ref, cs_ref, sn_ref,
             ln1, win, qag, wq, kvag, wk, wv, qg, kg,
             dhp_ref, dp_ref, dqraw_ref, dkraw_ref, hn_ref, cqn_ref, ckvn_ref,
             dln1_ref, dqag_ref, dkvag_ref, dqg_ref, dkg_ref):
        @pl.when(pl.program_id(0) == 0)
        def _():
            for r in (dln1_ref, dqag_ref, dkvag_ref, dqg_ref, dkg_ref):
                r[...] = jnp.zeros_like(r)

        hn, vjp_ln1 = jax.vjp(lambda h, g: _rms(h, g, D), hp_ref[...], ln1[...])
        hn_ref[...] = hn.astype(BF)
        pa_v = pa_ref[...]
        cqn, vjp_qa = jax.vjp(lambda x, g: _rms(x, g, Q_LORA), pa_v[:, :O_CKV], qag[...])
        ckvn, vjp_kva = jax.vjp(lambda x, g: _rms(x, g, KV_LORA), pa_v[:, O_CKV:O_KR], kvag[...])
        kr = pa_v[:, O_KR:O_XR]
        cqnb, ckvnb = cqn.astype(BF), ckvn.astype(BF)
        cqn_ref[...] = cqnb
        ckvn_ref[...] = ckvnb
        c, s = cs_ref[...], sn_ref[...]
        lane = lax.broadcasted_iota(jnp.int32, (1, HP), 1)
        rope_lanes = ((lane >= QK_NOPE) & (lane < QK_HEAD)).astype(F32)
        dkr = jnp.zeros((TR, HP), F32)
        dqg = jnp.zeros((1, HP), F32)
        dkg = jnp.zeros((1, HP), F32)
        qraw = _nt(cqnb, wq[...])
        kraw = _nt(ckvnb, wk[...])
        for h in range(N_HEADS):
            sl = slice(h * HP, (h + 1) * HP)
            dqraw, dg = _head_bwd(qraw[:, sl], qg[...], c, s, dq_ref[:, sl])
            dqg = dqg + dg
            dqraw_ref[:, sl] = dqraw.astype(BF)
            dkraw, dg = _head_bwd(kraw[:, sl] + kr, kg[...], c, s, dk_ref[:, sl])
            dkg = dkg + dg
            dkraw_ref[:, sl] = dkraw.astype(BF)
            dkr = dkr + dkraw * rope_lanes
        dcq, dqag = vjp_qa(_nn(dqraw_ref[...], wq[...]))
        dckv, dkvag = vjp_kva(_nn(dkraw_ref[...], wk[...]) + _nn(dv_ref[...].astype(BF), wv[...]))
        dpb = jnp.concatenate([dcq, dckv, dkr, dxr_ref[...], dxg_ref[...]], axis=1).astype(BF)
        dp_ref[...] = dpb
        dh, dln1 = vjp_ln1(_nn(dpb, win[...]))
        dhp_ref[...] = dh + dh1_ref[...]
        dln1_ref[...] += dln1
        dqag_ref[...] += dqag
        dkvag_ref[...] += dkvag
        dqg_ref[...] += dqg
        dkg_ref[...] += dkg

    acc = lambda n: pl.BlockSpec((1, n), lambda i: (0, 0))
    return pl.pallas_call(
        body, grid=(R // TR,), name="stage_a_bwd",
        in_specs=[_row_spec(N_HEADS * HP), _row_spec(N_HEADS * HP), _row_spec(D_ATTN), _row_spec(D_RNN),
                  _row_spec(D_RNN), _row_spec(D), _row_spec(D), _row_spec(O_XR), _row_spec(HP), _row_spec(HP),
                  _const_spec((1, D)), _const_spec((PC, D)), _const_spec((1, Q_LORA)),
                  _const_spec((N_HEADS * HP, Q_LORA)), _const_spec((1, KV_LORA)),
                  _const_spec((N_HEADS * HP, KV_LORA)), _const_spec((D_ATTN, KV_LORA)), _const_spec((1, HP)),
                  _const_spec((1, HP))],
        out_specs=[_row_spec(D), _row_spec(PC), _row_spec(N_HEADS * HP), _row_spec(N_HEADS * HP), _row_spec(D),
                   _row_spec(Q_LORA), _row_spec(KV_LORA), acc(D), acc(Q_LORA), acc(KV_LORA), acc(HP), acc(HP)],
        out_shape=[jax.ShapeDtypeStruct((R, D), F32), jax.ShapeDtypeStruct((R, PC), BF),
                   jax.ShapeDtypeStruct((R, N_HEADS * HP), BF), jax.ShapeDtypeStruct((R, N_HEADS * HP), BF),
                   jax.ShapeDtypeStruct((R, D), BF), jax.ShapeDtypeStruct((R, Q_LORA), BF),
                   jax.ShapeDtypeStruct((R, KV_LORA), BF), jax.ShapeDtypeStruct((1, D), F32),
                   jax.ShapeDtypeStruct((1, Q_LORA), F32), jax.ShapeDtypeStruct((1, KV_LORA), F32),
                   jax.ShapeDtypeStruct((1, HP), F32), jax.ShapeDtypeStruct((1, HP), F32)],
        compiler_params=_params("arbitrary"),
    )(dq, dk, dv, dxr, dxg, dh1, hp, pa, cs, sn, cw["ln1_g"], cw["win"], cw["qa_g"], cw["wq"], cw["kva_g"],
      cw["wk"], cw["wv"], cw["q_g"], cw["k_g"])


def _head_mask(half, dtype):
    lane = lax.broadcasted_iota(jnp.int32, (1, 2 * V_HEAD), 1)
    return ((lane >= V_HEAD) == (half == 1)).astype(dtype)


def _attn_specs(tq):
    n_q = TP // tq
    return (NB, N_HEADS // 2, n_q), dict(
        q=pl.BlockSpec((tq, 2 * HP), lambda b, j, i: (b * n_q + i, j)),
        k=pl.BlockSpec((TP, 2 * HP), lambda b, j, i: (b, j)),
        v=pl.BlockSpec((TP, 2 * V_HEAD), lambda b, j, i: (b, j)),
        o=pl.BlockSpec((tq, 2 * V_HEAD), lambda b, j, i: (b * n_q + i, j)),
        lse=pl.BlockSpec((None, tq, 2), lambda b, j, i: (j, b * n_q + i, 0)))


TQF = 1088


def _attn_fwd(q, k, v):
    def body(q_ref, k_ref, v_ref, o_ref, lse_ref):
        v2 = v_ref[...]
        o = jnp.zeros((TQF, 2 * V_HEAD), F32)
        lse = []
        for hh in range(2):
            sl = slice(hh * HP, (hh + 1) * HP)
            raw = _nt(q_ref[:, sl], k_ref[:, sl])
            m = jnp.max(raw, axis=-1, keepdims=True)
            e = jnp.exp2((raw - m) * SCALE_LOG2)
            l = jnp.sum(e, axis=-1, keepdims=True)
            o = o + _nn(e.astype(BF), v2 * _head_mask(hh, BF)) * (1.0 / l)
            lse.append(m * SCALE_LOG2 + jnp.log(l) * LOG2_E)
        o_ref[...] = o
        lane = lax.broadcasted_iota(jnp.int32, (TQF, 2), 1)
        lse_ref[...] = jnp.where(lane == 0, lse[0], lse[1])

    grid, sp = _attn_specs(TQF)
    return pl.pallas_call(
        body, grid=grid, name="attn_fwd", in_specs=[sp["q"], sp["k"], sp["v"]], out_specs=[sp["o"], sp["lse"]],
        out_shape=[jax.ShapeDtypeStruct((R, D_ATTN), F32), jax.ShapeDtypeStruct((N_HEADS // 2, R, 2), F32)],
        compiler_params=_params("arbitrary", "arbitrary", "arbitrary"),
    )(q, k, v)


def _attn_bwd(q, k, v, o, lse, do):
    def body(q_ref, k_ref, v_ref, o_ref, lse_ref, do_ref, dq_ref, dk_ref, dv_ref):
        @pl.when(pl.program_id(2) == 0)
        def _():
            dk_ref[...] = jnp.zeros_like(dk_ref)
            dv_ref[...] = jnp.zeros_like(dv_ref)

        do = do_ref[...]
        dob = do.astype(BF)
        do_o = do * o_ref[...]
        v2 = v_ref[...]
        dv_sum = jnp.zeros((TP, 2 * V_HEAD), F32)
        for hh in range(2):
            sl = slice(hh * HP, (hh + 1) * HP)
            qb, kb = q_ref[:, sl], k_ref[:, sl]
            p = jnp.exp2(_nt(qb, kb) * SCALE_LOG2 - lse_ref[:, hh:hh + 1])
            dp = _nt(dob, v2 * _head_mask(hh, BF))
            delta = jnp.sum(do_o * _head_mask(hh, F32), axis=-1, keepdims=True)
            dsb = (p * (dp - delta) * SCALE).astype(BF)
            dq_ref[:, sl] = _nn(dsb, kb)
            dk_ref[:, sl] += _tn(dsb, qb)
            dv_sum = dv_sum + _tn(p.astype(BF), dob) * _head_mask(hh, F32)
        dv_ref[...] += dv_sum

    grid, sp = _attn_specs(TQ)
    return pl.pallas_call(
        body, grid=grid, name="attn_bwd", in_specs=[sp["q"], sp["k"], sp["v"], sp["o"], sp["lse"], sp["o"]],
        out_specs=[sp["q"], sp["k"], sp["v"]],
        out_shape=[jax.ShapeDtypeStruct((R, N_HEADS * HP), F32), jax.ShapeDtypeStruct((R, N_HEADS * HP), F32),
                   jax.ShapeDtypeStruct((R, D_ATTN), F32)],
        compiler_params=_params("arbitrary", "arbitrary", "arbitrary"),
    )(q, k, v, o, lse, do)


def _tile_prefix(a_ref, b_ref, reverse):
    tiles = (TP // 8, 8, CG)
    r8 = lax.broadcasted_iota(jnp.int32, tiles, 1)
    a, b = a_ref[...].reshape(tiles), b_ref[...].reshape(tiles)
    for s in (1, 2, 4):
        shift = 8 - s if reverse else s
        keep = (r8 < 8 - s) if reverse else (r8 >= s)
        b = jnp.where(keep, a * pltpu.roll(b, shift, 1) + b, b)
        a = jnp.where(keep, a * pltpu.roll(a, shift, 1), a)
    a_ref[...] = a.reshape(TP, CG)
    b_ref[...] = b.reshape(TP, CG)


def _scan_pair(af_ref, bf_ref, hf_ref, ab_ref, bb_ref, hb_ref):
    _tile_prefix(af_ref, bf_ref, False)
    _tile_prefix(ab_ref, bb_ref, True)
    n_tiles = TP // 8

    def step(i, carry):
        cf, cb = carry
        rf = pl.multiple_of(i * 8, 8)
        rb = pl.multiple_of((n_tiles - 1 - i) * 8, 8)
        hf_ref[pl.ds(rf, 8), :] = bf_ref[pl.ds(rf, 8), :] + af_ref[pl.ds(rf, 8), :] * cf
        hb_ref[pl.ds(rb, 8), :] = bb_ref[pl.ds(rb, 8), :] + ab_ref[pl.ds(rb, 8), :] * cb
        cf = bf_ref[pl.ds(rf + 7, 1), :] + af_ref[pl.ds(rf + 7, 1), :] * cf
        cb = bb_ref[pl.ds(rb, 1), :] + ab_ref[pl.ds(rb, 1), :] * cb
        return cf, cb

    zero = jnp.zeros((1, CG), F32)
    lax.fori_loop(0, n_tiles, step, (zero, zero), unroll=8)


def _shifts(x):
    t = lax.broadcasted_iota(jnp.int32, x.shape, 0)
    xm2 = jnp.where(t >= 2, pltpu.roll(x, 2, 0), 0.0)
    xm1 = jnp.where(t >= 1, pltpu.roll(x, 1, 0), 0.0)
    xp1 = jnp.where(t < TP - 1, pltpu.roll(x, TP - 1, 0), 0.0)
    return xm2, xm1, xp1


def _softplus(z):
    e = jnp.exp(-jnp.abs(z))
    small = e * (1.0 - e * (0.5 - e * (1.0 / 3.0)))
    return jnp.maximum(z, 0.0) + jnp.where(e < 0.01, small, jnp.log(1.0 + e))


def _sigmoid(x):
    return 0.5 * jnp.tanh(0.5 * x) + 0.5


def _one_minus_sq(log_a, a):
    x = 2.0 * log_a
    series = -x * (1.0 + x * 0.5 * (1.0 + x * (1.0 / 3.0) * (1.0 + x * 0.25)))
    return jnp.where(x > -0.05, series, 1.0 - a * a)


def _gates(row0, xc, pa_f, pi_f, pa_b, pi_b, lam_f, lam_b):
    t = row0 + lax.broadcasted_iota(jnp.int32, xc.shape, 0)
    valid = t < T
    out = []
    for pa, pi_, lam in ((pa_f, pi_f, lam_f), (pa_b, pi_b, lam_b)):
        r = _sigmoid(pa)
        gate_i = _sigmoid(pi_)
        log_a = -LRU_C * r * _softplus(-lam)
        a = jnp.exp(log_a)
        mult = jnp.sqrt(jnp.maximum(_one_minus_sq(log_a, a), 0.0))
        out += [a, jnp.where(valid, mult * (gate_i * xc), 0.0)]
    return tuple(out)


def _gates_bwd(row0, xc, pres, lams, cots):
    t = row0 + lax.broadcasted_iota(jnp.int32, xc.shape, 0)
    valid = t < T
    dxc = jnp.zeros_like(xc)
    dpres, dlams = [], []
    for d in range(2):
        pa, pi_, lam = pres[2 * d], pres[2 * d + 1], lams[d]
        da, db = cots[2 * d], jnp.where(valid, cots[2 * d + 1], 0.0)
        r = _sigmoid(pa)
        gate_i = _sigmoid(pi_)
        sp = _softplus(-lam)
        log_a = -LRU_C * r * sp
        a = jnp.exp(log_a)
        m2 = jnp.maximum(_one_minus_sq(log_a, a), 0.0)
        mult = jnp.sqrt(m2)
        dxc = dxc + db * (mult * gate_i)
        d_gate = db * (mult * xc)
        d_m2 = jnp.where(m2 > 0.0, db * (gate_i * xc) * (0.5 * lax.rsqrt(m2)), 0.0)
        d_log_a = da * a - 2.0 * d_m2 * (a * a)
        dpres += [d_log_a * (-LRU_C * sp) * (r * (1.0 - r)), d_gate * (gate_i * (1.0 - gate_i))]
        d_sp = jnp.sum(d_log_a * (-LRU_C * r), axis=0, keepdims=True)
        dlams.append(-d_sp * jax.nn.sigmoid(-lam))
    return dxc, dpres, dlams


def _rnn_specs():
    seq = pl.BlockSpec((TP, CG), lambda g, b: (b, g))
    return dict(
        seq=seq,
        cw=pl.BlockSpec((CONV_W, CG), lambda g, b: (0, g)),
        cb=pl.BlockSpec((1, CG), lambda g, b: (0, g)),
        w4=pl.BlockSpec((None, CG, 4 * CG), lambda g, b: (g, 0, 0)),
        b4=pl.BlockSpec((None, 1, 4 * CG), lambda g, b: (g, 0, 0)),
        lam=pl.BlockSpec((None, 1, 2 * CG), lambda g, b: (g, 0, 0)),
    )


def _conv(x, xm2, xm1, xp1, cw_ref, cb_ref):
    return cw_ref[0:1, :] * xm2 + cw_ref[1:2, :] * xm1 + cw_ref[2:3, :] * x + cw_ref[3:4, :] * xp1 + cb_ref[...]


TC = 128
N_TC = TP // TC


def _split4(pre):
    return pre[:, :CG], pre[:, CG:2 * CG], pre[:, 2 * CG:3 * CG], pre[:, 3 * CG:]


def _rnn_fwd(xr, xg, cw):
    def body(xr_ref, xg_ref, cw_ref, cb_ref, w4_ref, b4_ref, lam_ref, y_ref, hf_ref, hb_ref, xc_s, af, bf, ab, bb):
        x = xr_ref[...]
        xc_s[...] = _conv(x, *_shifts(x), cw_ref, cb_ref)
        lam = lam_ref[...]

        def chunk(i, _):
            rows = pl.ds(pl.multiple_of(i * TC, TC), TC)
            xc = xc_s[rows, :]
            pre = _nn(xc.astype(BF), w4_ref[...]) + b4_ref[...]
            a_f, b_f, a_b, b_b = _gates(i * TC, xc, *_split4(pre), lam[:, :CG], lam[:, CG:])
            af[rows, :] = a_f
            bf[rows, :] = b_f
            ab[rows, :] = a_b
            bb[rows, :] = b_b
            return 0

        lax.fori_loop(0, N_TC, chunk, 0)
        _scan_pair(af, bf, hf_ref, ab, bb, hb_ref)
        y_ref[...] = (hf_ref[...] + hb_ref[...]) * jax.nn.gelu(xg_ref[...])

    sp = _rnn_specs()
    return pl.pallas_call(
        body, grid=(N_CG, NB), name="rnn_fwd",
        in_specs=[sp["seq"], sp["seq"], sp["cw"], sp["cb"], sp["w4"], sp["b4"], sp["lam"]],
        out_specs=[sp["seq"]] * 3, out_shape=[jax.ShapeDtypeStruct((R, D_RNN), F32)] * 3,
        scratch_shapes=[pltpu.VMEM((TP, CG), F32)] * 5,
        compiler_params=_params("arbitrary", "arbitrary"),
    )(xr, xg, cw["conv_w"], cw["conv_b"], cw["w4"], cw["b4"], cw["lam"])


def _rnn_bwd(dy, xr, xg, hf, hb, cw):
    def body(dy_ref, xr_ref, xg_ref, hf_ref, hb_ref, cw_ref, cb_ref, w4_ref, b4_ref, lam_ref,
             dxr_ref, dxg_ref, dcw_ref, dcb_ref, dw4_ref, db4_ref, dlam_ref,
             xc_s, af_s, ab_s, dhs_s, dhs2_s, lf_s, lb_s, daf_s, dab_s, dxc_s):
        @pl.when(pl.program_id(1) == 0)
        def _():
            for r in (dcw_ref, dcb_ref, dw4_ref, db4_ref, dlam_ref):
                r[...] = jnp.zeros_like(r)

        x = xr_ref[...]
        xc_s[...] = _conv(x, *_shifts(x), cw_ref, cb_ref)
        lam = lam_ref[...]

        def chunk1(i, _):
            rows = pl.ds(pl.multiple_of(i * TC, TC), TC)
            xc = xc_s[rows, :]
            pre = _nn(xc.astype(BF), w4_ref[...]) + b4_ref[...]
            a_f, _, a_b, _ = _gates(i * TC, xc, *_split4(pre), lam[:, :CG], lam[:, CG:])
            af_s[rows, :] = a_f
            ab_s[rows, :] = a_b
            _, vjp_y = jax.vjp(lambda h, g: h * jax.nn.gelu(g), hf_ref[rows, :] + hb_ref[rows, :], xg_ref[rows, :])
            dhs, dxg = vjp_y(dy_ref[rows, :])
            dhs_s[rows, :] = dhs
            dhs2_s[rows, :] = dhs
            dxg_ref[rows, :] = dxg
            return 0

        lax.fori_loop(0, N_TC, chunk1, 0)
        t = lax.broadcasted_iota(jnp.int32, (TP, CG), 0)
        af_s[...] = pltpu.roll(af_s[...], TP - 1, 0)
        ab_s[...] = pltpu.roll(ab_s[...], 1, 0)
        _scan_pair(ab_s, dhs_s, lb_s, af_s, dhs2_s, lf_s)
        daf_s[...] = lf_s[...] * jnp.where(t >= 1, pltpu.roll(hf_ref[...], 1, 0), 0.0)
        dab_s[...] = lb_s[...] * jnp.where(t < TP - 1, pltpu.roll(hb_ref[...], TP - 1, 0), 0.0)

        def chunk2(i, _):
            rows = pl.ds(pl.multiple_of(i * TC, TC), TC)
            xc = xc_s[rows, :]
            xcb = xc.astype(BF)
            pre = _nn(xcb, w4_ref[...]) + b4_ref[...]
            dxc, dpres, dlams = _gates_bwd(i * TC, xc, _split4(pre), (lam[:, :CG], lam[:, CG:]),
                                           (daf_s[rows, :], lf_s[rows, :], dab_s[rows, :], lb_s[rows, :]))
            dpre = jnp.concatenate(dpres, axis=1)
            dpreb = dpre.astype(BF)
            dxc_s[rows, :] = dxc + _nt(dpreb, w4_ref[...])
            dw4_ref[...] += _tn(xcb, dpreb)
            db4_ref[...] += jnp.sum(dpre, axis=0, keepdims=True)
            dlam_ref[...] += jnp.concatenate(dlams, axis=1)
            return 0

        lax.fori_loop(0, N_TC, chunk2, 0)
        dxc = dxc_s[...]
        dcb_ref[...] += jnp.sum(dxc, axis=0, keepdims=True)
        for tap, xs in enumerate(_shifts(x)[:2] + (x,) + _shifts(x)[2:]):
            dcw_ref[tap:tap + 1, :] += jnp.sum(xs * dxc, axis=0, keepdims=True)
        dxr_ref[...] = (cw_ref[0:1, :] * jnp.where(t < TP - 2, pltpu.roll(dxc, TP - 2, 0), 0.0)
                        + cw_ref[1:2, :] * jnp.where(t < TP - 1, pltpu.roll(dxc, TP - 1, 0), 0.0)
                        + cw_ref[2:3, :] * dxc
                        + cw_ref[3:4, :] * jnp.where(t >= 1, pltpu.roll(dxc, 1, 0), 0.0))

    sp = _rnn_specs()
    return pl.pallas_call(
        body, grid=(N_CG, NB), name="rnn_bwd",
        in_specs=[sp["seq"]] * 5 + [sp["cw"], sp["cb"], sp["w4"], sp["b4"], sp["lam"]],
        out_specs=[sp["seq"], sp["seq"], sp["cw"], sp["cb"], sp["w4"], sp["b4"], sp["lam"]],
        out_shape=[jax.ShapeDtypeStruct((R, D_RNN), F32), jax.ShapeDtypeStruct((R, D_RNN), F32),
                   jax.ShapeDtypeStruct((CONV_W, D_RNN), F32), jax.ShapeDtypeStruct((1, D_RNN), F32),
                   jax.ShapeDtypeStruct((N_CG, CG, 4 * CG), F32), jax.ShapeDtypeStruct((N_CG, 1, 4 * CG), F32),
                   jax.ShapeDtypeStruct((N_CG, 1, 2 * CG), F32)],
        scratch_shapes=[pltpu.VMEM((TP, CG), F32)] * 10,
        compiler_params=_params("arbitrary", "arbitrary"),
    )(dy, xr, xg, hf, hb, cw["conv_w"], cw["conv_b"], cw["w4"], cw["b4"], cw["lam"])


TD = 256
STAGE_D_VMEM = 58 * 1024 * 1024


def _stage_d(hp, o, y, tgt, cw):
    def body(hp_ref, o_ref, y_ref, tgt_ref, ga, gr, wout, ln2, wg, wu, wd,
             do_ref, dy_ref, dh1_ref, mix_ref, dh1b_ref, hn2_ref, dg_ref, du_ref, act_ref, dh2b_ref,
             loss_ref, dga_ref, dgr_ref, dln2_ref):
        i = pl.program_id(0)

        @pl.when(i == 0)
        def _():
            for r in (loss_ref, dga_ref, dgr_ref, dln2_ref):
                r[...] = jnp.zeros_like(r)

        mix_a, vjp_a = jax.vjp(lambda x, g: _rms(x, g, D_ATTN), o_ref[...], ga[...])
        mix_r, vjp_r = jax.vjp(lambda x, g: _rms(x, g, D_RNN), y_ref[...], gr[...])
        mab, mrb = mix_a.astype(BF), mix_r.astype(BF)
        mix_ref[:, :D_ATTN] = mab
        mix_ref[:, D_ATTN:] = mrb
        h1 = hp_ref[...] + _nn(mab, wout[:D_ATTN, :]) + _nn(mrb, wout[D_ATTN:, :])
        hn2, vjp_ln2 = jax.vjp(lambda x, g: _rms(x, g, D), h1, ln2[...])
        hn2b = hn2.astype(BF)
        hn2_ref[...] = hn2b
        act, vjp_act = jax.vjp(lambda g, u: jax.nn.silu(g) * u, _nt(hn2b, wg[...]), _nt(hn2b, wu[...]))
        actb = act.astype(BF)
        act_ref[...] = actb
        h2 = h1 + _nn(actb, wd[...])
        row = i * TD + lax.broadcasted_iota(jnp.int32, (TD, 1), 0)
        t = jnp.where(row >= TP, row - TP, row)
        err = jnp.where((t >= N_META) & (t < T), h2 - tgt_ref[...], 0.0)
        loss_ref[...] += jnp.sum(err * err) * (0.5 / D)
        dh2b = (err * (1.0 / D)).astype(BF)
        dh2b_ref[...] = dh2b
        dg, du = vjp_act(_nt(dh2b, wd[...]))
        dgb, dub = dg.astype(BF), du.astype(BF)
        dg_ref[...] = dgb
        du_ref[...] = dub
        dh1n, dln2 = vjp_ln2(_nn(dgb, wg[...]) + _nn(dub, wu[...]))
        dh1 = err * (1.0 / D) + dh1n
        dh1_ref[...] = dh1
        dh1b = dh1.astype(BF)
        dh1b_ref[...] = dh1b
        dmix = _nt(dh1b, wout[...])
        do, dga = vjp_a(dmix[:, :D_ATTN])
        dyr, dgr = vjp_r(dmix[:, D_ATTN:])
        do_ref[...] = do
        dy_ref[...] = dyr
        dga_ref[...] += dga
        dgr_ref[...] += dgr
        dln2_ref[...] += dln2

    rs = lambda n: _row_spec(n, TD)
    acc = lambda n: pl.BlockSpec((1, n), lambda i: (0, 0))
    return pl.pallas_call(
        body, grid=(R // TD,), name="stage_d",
        in_specs=[rs(D), rs(D_ATTN), rs(D_RNN), rs(D), _const_spec((1, D_ATTN)), _const_spec((1, D_RNN)),
                  _const_spec((D, D)), _const_spec((1, D)), _const_spec((D_FF, D)), _const_spec((D_FF, D)),
                  _const_spec((D_FF, D))],
        out_specs=[rs(D_ATTN), rs(D_RNN), rs(D), rs(D), rs(D), rs(D), rs(D_FF), rs(D_FF), rs(D_FF), rs(D),
                   acc(1), acc(D_ATTN), acc(D_RNN), acc(D)],
        out_shape=[jax.ShapeDtypeStruct((R, D_ATTN), F32), jax.ShapeDtypeStruct((R, D_RNN), F32),
                   jax.ShapeDtypeStruct((R, D), F32), jax.ShapeDtypeStruct((R, D), BF),
                   jax.ShapeDtypeStruct((R, D), BF), jax.ShapeDtypeStruct((R, D), BF),
                   jax.ShapeDtypeStruct((R, D_FF), BF), jax.ShapeDtypeStruct((R, D_FF), BF),
                   jax.ShapeDtypeStruct((R, D_FF), BF), jax.ShapeDtypeStruct((R, D), BF),
                   jax.ShapeDtypeStruct((1, 1), F32), jax.ShapeDtypeStruct((1, D_ATTN), F32),
                   jax.ShapeDtypeStruct((1, D_RNN), F32), jax.ShapeDtypeStruct((1, D), F32)],
        compiler_params=_params("arbitrary", vmem=STAGE_D_VMEM),
    )(hp, o, y, tgt, cw["ga"], cw["gr"], cw["wout"], cw["ln2_g"], cw["wg"], cw["wu"], cw["wd"])


TW = 2176


def _wgrad(a, b, name, tk=None):
    ka, nb = a.shape[1], b.shape[1]
    tk = ka if tk is None else tk

    def body(a_ref, b_ref, o_ref):
        @pl.when(pl.program_id(1) == 0)
        def _():
            o_ref[...] = jnp.zeros_like(o_ref)

        o_ref[...] += _tn(a_ref[...].astype(BF), b_ref[...].astype(BF))

    return pl.pallas_call(
        body, grid=(ka // tk, R // TW), name=name,
        in_specs=[pl.BlockSpec((TW, tk), lambda k, r: (r, k)), pl.BlockSpec((TW, nb), lambda k, r: (r, 0))],
        out_specs=pl.BlockSpec((tk, nb), lambda k, r: (k, 0)),
        out_shape=jax.ShapeDtypeStruct((ka, nb), F32),
        compiler_params=_params("arbitrary", "arbitrary"),
    )(a, b)


def _wgrad_heads(dq, dk, dv, cqn, ckvn):
    def body(dq_ref, dk_ref, dv_ref, cqn_ref, ckvn_ref, oq_ref, ok_ref, ov_ref):
        @pl.when(pl.program_id(0) == 0)
        def _():
            for r in (oq_ref, ok_ref, ov_ref):
                r[...] = jnp.zeros_like(r)

        ckvnb = ckvn_ref[...]
        oq_ref[...] += _tn(dq_ref[...], cqn_ref[...])
        ok_ref[...] += _tn(dk_ref[...], ckvnb)
        ov_ref[...] += _tn(dv_ref[...].astype(BF), ckvnb)

    rows = lambda a: pl.BlockSpec((TW, a.shape[1]), lambda r: (r, 0))
    full = lambda m, n: pl.BlockSpec((m, n), lambda r: (0, 0))
    shapes = [(dq.shape[1], cqn.shape[1]), (dk.shape[1], ckvn.shape[1]), (dv.shape[1], ckvn.shape[1])]
    return pl.pallas_call(
        body, grid=(R // TW,), name="wgrad_heads", in_specs=[rows(a) for a in (dq, dk, dv, cqn, ckvn)],
        out_specs=[full(*s) for s in shapes], out_shape=[jax.ShapeDtypeStruct(s, F32) for s in shapes],
        compiler_params=_params("arbitrary"),
    )(dq, dk, dv, cqn, ckvn)


def _rope_tables():
    half = QK_ROPE // 2
    freqs = 1.0 / (ROPE_THETA ** (jnp.arange(half, dtype=F32) / half))
    ang = jnp.arange(TP, dtype=F32)[:, None] * freqs[None, :]
    ones = jnp.ones((TP, QK_NOPE), F32)
    zeros = jnp.zeros((TP, QK_NOPE), F32)
    pad1 = jnp.ones((TP, HP - QK_HEAD), F32)
    pad0 = jnp.zeros((TP, HP - QK_HEAD), F32)
    cs = jnp.concatenate([ones, jnp.cos(ang), jnp.cos(ang), pad1], axis=1)
    sn = jnp.concatenate([zeros, jnp.sin(ang), jnp.sin(ang), pad0], axis=1)
    return jnp.tile(cs, (NB, 1)), jnp.tile(sn, (NB, 1))


def _pad_rows(a, lo, hi):
    return jnp.pad(a, ((0, 0), (lo, hi), (0, 0)))


def _compute_weights(w):
    win_t = w["w_in_t"]
    kr = win_t[O_KR:O_KR + QK_ROPE]
    win = jnp.concatenate([win_t[:O_KR], jnp.zeros((QK_NOPE, D), F32), kr,
                           jnp.zeros((HP - QK_HEAD, D), F32), win_t[O_KR + QK_ROPE:]], axis=0)
    wq = _pad_rows(w["w_uq_t"].reshape(N_HEADS, QK_HEAD, Q_LORA), 0, HP - QK_HEAD)
    wkv = w["w_ukv_t"].reshape(N_HEADS, QK_NOPE + V_HEAD, KV_LORA)
    wk = _pad_rows(wkv[:, :QK_NOPE], 0, HP - QK_NOPE)
    wv = wkv[:, QK_NOPE:].reshape(D_ATTN, KV_LORA)
    gates = jnp.stack([w["lru_wa"][0], w["lru_wi"][0], w["lru_wa"][1], w["lru_wi"][1]])
    blk = gates.reshape(4, N_CG, 2, RNN_BW, RNN_BW)
    dense = jnp.einsum("tcaij,ab->tcaibj", blk, jnp.eye(2, dtype=F32)).reshape(4, N_CG, CG, CG)
    w4 = dense.transpose(1, 2, 0, 3).reshape(N_CG, CG, 4 * CG)
    bias = jnp.stack([w["lru_ba"][0], w["lru_bi"][0], w["lru_ba"][1], w["lru_bi"][1]])
    b4 = bias.reshape(4, N_CG, CG).transpose(1, 0, 2).reshape(N_CG, 1, 4 * CG)
    lam = w["lru_lambda"].reshape(2, N_CG, CG).transpose(1, 0, 2).reshape(N_CG, 1, 2 * CG)
    pad_g = lambda g: jnp.pad(g.reshape(1, QK_HEAD), ((0, 0), (0, HP - QK_HEAD)))
    return dict(
        ln1_g=w["ln1_g"].reshape(1, D), win=win.astype(BF), qa_g=w["q_a_norm_g"].reshape(1, Q_LORA),
        wq=wq.astype(BF).reshape(N_HEADS * HP, Q_LORA), kva_g=w["kv_a_norm_g"].reshape(1, KV_LORA),
        wk=wk.astype(BF).reshape(N_HEADS * HP, KV_LORA), wv=wv.astype(BF),
        q_g=pad_g(w["q_norm_g"]), k_g=pad_g(w["k_norm_g"]),
        conv_w=w["conv_w"].reshape(CONV_W, D_RNN), conv_b=w["conv_b"].reshape(1, D_RNN),
        w4=w4.astype(BF), b4=b4, lam=lam,
        ga=w["attn_out_g"].reshape(1, D_ATTN), gr=w["rnn_out_g"].reshape(1, D_RNN), ln2_g=w["ln2_g"].reshape(1, D),
    )


def _local_step(x, target, meta, w, late_weights, early_grads, mid_grads):
    cw = _compute_weights(w)
    cs, sn = _rope_tables()
    hp = jnp.concatenate([jnp.broadcast_to(meta[None], (NB, N_META, D)), x,
                          jnp.zeros((NB, TP - T, D), F32)], axis=1).reshape(R, D)
    tgt = _pad_rows(target, N_META, TP - T).reshape(R, D)

    pa, xr, xg, q, k, v = _stage_a_fwd(hp, cs, sn, cw)
    o, lse = _attn_fwd(q, k, v)
    y, hf, hb = _rnn_fwd(xr, xg, cw)
    late = late_weights([o, y])
    cw.update(wout=late["w_out"], wg=late["w_gate_t"], wu=late["w_up_t"], wd=late["w_down"])
    (do, dy, dh1, mixb, dh1b, hn2b, dgb, dub, actb, dh2b, loss, dga, dgr, dln2) = _stage_d(hp, o, y, tgt, cw)
    dwout = _wgrad(mixb, dh1b, "wgrad_out")
    dwg = _wgrad(dgb, hn2b, "wgrad_gate", tk=D_FF // 2)
    dwu = _wgrad(dub, hn2b, "wgrad_up", tk=D_FF // 2)
    dwd = _wgrad(actb, dh2b, "wgrad_down", tk=D_FF // 2)
    zero = early_grads(dict(w_out=dwout, w_gate=dwg, w_up=dwu, w_down=dwd))
    cw["conv_b"] = cw["conv_b"] + zero
    dxr, dxg, dcw, dcb, dw4, db4, dlam = _rnn_bwd(dy, xr, xg, hf, hb, cw)
    zero = mid_grads([dxr])
    dq, dk, dv = _attn_bwd(q, k, v, o, lse, do)
    (dhp, dpb, dqrawb, dkrawb, hn1b, cqnb, ckvnb, dln1, dqag, dkvag, dqg, dkg) = _stage_a_bwd(
        dq, dk, dv, dxr, dxg, dh1, hp, pa, cs, sn, dict(cw, qa_g=cw["qa_g"] + zero))

    dwin = _wgrad(dpb, hn1b, "wgrad_in", tk=PC // 2)
    dwq, dwk, dwv = _wgrad_heads(dqrawb, dkrawb, dv, cqnb, ckvnb)

    dwin_t = jnp.concatenate([dwin[:O_KR], dwin[O_KR + QK_NOPE:O_KR + QK_HEAD], dwin[O_XR:]], axis=0)
    dwq_t = dwq.reshape(N_HEADS, HP, Q_LORA)[:, :QK_HEAD].reshape(N_HEADS * QK_HEAD, Q_LORA)
    dwkv_t = jnp.concatenate([dwk.reshape(N_HEADS, HP, KV_LORA)[:, :QK_NOPE],
                              dwv.reshape(N_HEADS, V_HEAD, KV_LORA)], axis=1).reshape(2 * D_ATTN, KV_LORA)
    d4 = dw4.reshape(N_CG, 2, RNN_BW, 4, 2, RNN_BW)
    dgates = jnp.stack([d4[:, 0, :, :, 0, :], d4[:, 1, :, :, 1, :]], axis=1)
    dgates = dgates.transpose(3, 0, 1, 2, 4).reshape(4, N_HEADS, RNN_BW, RNN_BW)
    dbias = db4.reshape(N_CG, 4, CG).transpose(1, 0, 2).reshape(4, D_RNN)
    dhp3 = dhp.reshape(NB, TP, D)
    grads = dict(
        meta_tokens=jnp.sum(dhp3[:, :N_META], axis=0),
        ln1_g=dln1, w_in_t=dwin_t, q_a_norm_g=dqag, w_uq_t=dwq_t, kv_a_norm_g=dkvag, w_ukv_t=dwkv_t,
        q_norm_g=dqg[:, :QK_HEAD], k_norm_g=dkg[:, :QK_HEAD], conv_w=dcw[None], conv_b=dcb,
        lru_wa=jnp.stack([dgates[0], dgates[2]])[None], lru_ba=jnp.stack([dbias[0], dbias[2]])[None],
        lru_wi=jnp.stack([dgates[1], dgates[3]])[None], lru_bi=jnp.stack([dbias[1], dbias[3]])[None],
        lru_lambda=dlam.reshape(N_CG, 2, CG).transpose(1, 0, 2).reshape(1, 2, D_RNN),
        attn_out_g=dga, rnn_out_g=dgr, ln2_g=dln2,
    )
    return loss[0, 0], dhp3[:, N_META:T], grads, [dhp, dwin]


_ANY = pl.BlockSpec(memory_space=pl.ANY)


def _place():
    return lax.axis_index("x"), lax.axis_index("y"), lax.axis_index("c")


def _other_chips(x, y):
    return [(1 - x, y), (x, 1 - y), (1 - x, 1 - y)]


def _all_gather(arrs, name):
    n_arr = len(arrs)

    def body(*refs):
        x_refs, out_refs, zero_ref = refs[:n_arr], refs[n_arr:2 * n_arr], refs[2 * n_arr]
        send_sems, recv_sems, local_sems = refs[2 * n_arr + 1:]
        x, y, c = _place()
        me, sibling = (x, y, c), (x, y, 1 - c)
        chips = _other_chips(x, y)
        zero_ref[...] = jnp.zeros_like(zero_ref)

        def rows(a, px, py, pc):
            m = arrs[a].shape[0]
            return out_refs[a].at[pl.ds((4 * px + 2 * py + pc) * m, m), :]

        def copy(a, k, block, to, src=None):
            return pltpu.make_async_remote_copy(
                src_ref=rows(a, *block) if src is None else src, dst_ref=rows(a, *block),
                send_sem=send_sems.at[7 * a + k], recv_sem=recv_sems.at[7 * a + k], device_id=to, device_id_type=MESH)

        mine = [pltpu.make_async_copy(x_refs[a], rows(a, *me), local_sems.at[a]) for a in range(n_arr)]
        first, passed = [], []
        for a in range(n_arr):
            first.append(copy(a, 0, me, sibling, src=x_refs[a]))
            first += [copy(a, 1 + j, me, (*chip, c), src=x_refs[a]) for j, chip in enumerate(chips)]
        for cp in mine + first:
            cp.start()
        for a in range(n_arr):
            for j, chip in enumerate(chips):
                copy(a, 1 + j, (*chip, c), me).wait_recv()
                passed.append(copy(a, 4 + j, (*chip, c), sibling))
                passed[-1].start()
        for a in range(n_arr):
            copy(a, 0, sibling, me).wait_recv()
            for j, chip in enumerate(chips):
                copy(a, 4 + j, (*chip, 1 - c), me).wait_recv()
        for cp in first + passed:
            cp.wait_send()
        for cp in mine:
            cp.wait()

    outs = pl.pallas_call(
        body, name=name,
        out_shape=[jax.ShapeDtypeStruct((8 * a.shape[0], a.shape[1]), a.dtype) for a in arrs]
        + [jax.ShapeDtypeStruct((8, LANES), F32)],
        in_specs=[_ANY] * n_arr, out_specs=[_ANY] * n_arr + [pl.BlockSpec(memory_space=pltpu.VMEM)],
        scratch_shapes=[pltpu.SemaphoreType.DMA((7 * n_arr,)), pltpu.SemaphoreType.DMA((7 * n_arr,)),
                        pltpu.SemaphoreType.DMA((n_arr,))],
    )(*arrs)
    return outs[:n_arr], outs[n_arr]


def _pair_exchange(big, whole, name):
    n_s, _, m, n = big.shape
    n_copies = n_s + len(whole)

    def body(*refs):
        big_ref, whole_refs = refs[0], refs[1:1 + len(whole)]
        rbig_ref, rwhole_refs = refs[1 + len(whole)], refs[2 + len(whole):2 + 2 * len(whole)]
        send_sems, recv_sems = refs[-2:]
        x, y, c = _place()
        sibling = (x, y, 1 - c)
        copies = [pltpu.make_async_remote_copy(
            src_ref=big_ref.at[s, 1 - c], dst_ref=rbig_ref.at[s], send_sem=send_sems.at[s], recv_sem=recv_sems.at[s],
            device_id=sibling, device_id_type=MESH) for s in range(n_s)]
        copies += [pltpu.make_async_remote_copy(
            src_ref=a, dst_ref=r, send_sem=send_sems.at[n_s + i], recv_sem=recv_sems.at[n_s + i],
            device_id=sibling, device_id_type=MESH) for i, (a, r) in enumerate(zip(whole_refs, rwhole_refs))]
        for cp in copies:
            cp.start()
        for cp in copies:
            cp.wait()

    return pl.pallas_call(
        body, name=name,
        out_shape=[jax.ShapeDtypeStruct((n_s, m, n), big.dtype)] + [jax.ShapeDtypeStruct(a.shape, a.dtype) for a in whole],
        in_specs=[_ANY] * (1 + len(whole)), out_specs=[_ANY] * (1 + len(whole)),
        scratch_shapes=[pltpu.SemaphoreType.DMA((n_copies,)), pltpu.SemaphoreType.DMA((n_copies,))],
    )(big, *whole)


_HBM = pl.BlockSpec(memory_space=pltpu.HBM)
_SEM = pl.BlockSpec(memory_space=pltpu.SEMAPHORE)
_EFFECT = pltpu.SideEffectType.DATAFLOW_SIDE_EFFECTING


def _split_copies(src_refs, land_refs, sems, plan, sending):
    n = len(sems) // 2
    return [pltpu.make_async_remote_copy(src_ref=s, dst_ref=d, send_sem=sems[k], recv_sem=sems[n + k], device_id=to,
                                         device_id_type=MESH)
            for k, (s, d, to) in enumerate(plan(src_refs, land_refs, sending))]


def _to_chips(src_at, land_at):
    def plan(src_refs, land_refs, sending):
        x, y, c = _place()
        return [(src_at(s, tx, ty, c), land_at(l, j, *((x, y) if sending else (tx, ty)), c), (tx, ty, c))
                for s, l in zip(src_refs, land_refs) for j, (tx, ty) in enumerate(_other_chips(x, y))]
    return plan


def _to_sibling(src_refs, land_refs, sending):
    x, y, c = _place()
    return [(s.at[k, 1 - c], l.at[k], (x, y, 1 - c)) for s, l in zip(src_refs, land_refs) for k in range(N_CHIPS)]


def _split_start(name, srcs, lands, plan, n):
    srcs, lands = list(srcs), list(lands)
    k = len(srcs)

    def body(*refs):
        outs = refs[2 * k:]
        for cp in _split_copies(refs[:k], refs[k:2 * k], outs[:2 * n], plan, True):
            cp.start()
        outs[2 * n + 2 * k][...] = jnp.zeros_like(outs[2 * n + 2 * k])

    outs = pl.pallas_call(
        body, name=name,
        out_shape=(pltpu.SemaphoreType.DMA(()),) * (2 * n) + tuple(pltpu.HBM(a.shape, a.dtype) for a in srcs + lands)
        + (jax.ShapeDtypeStruct((8, LANES), F32),),
        in_specs=(_HBM,) * (2 * k),
        out_specs=(_SEM,) * (2 * n) + (_HBM,) * (2 * k) + (pl.BlockSpec(memory_space=pltpu.VMEM),),
        input_output_aliases={i: 2 * n + i for i in range(2 * k)},
        compiler_params=pltpu.CompilerParams(has_side_effects=_EFFECT),
    )(*[pltpu.with_memory_space_constraint(a, pltpu.HBM) for a in srcs + lands])
    return outs[:2 * n], list(outs[2 * n:2 * n + k]), list(outs[2 * n + k:2 * n + 2 * k]), outs[2 * n + 2 * k]


def _split_wait(name, sems, srcs, lands, after, plan):
    srcs, lands = list(srcs), list(lands)
    k = len(srcs)

    def body(*refs):
        for cp in _split_copies(refs[:k], refs[k:2 * k], refs[2 * k:2 * k + len(sems)], plan, False):
            cp.wait_send()
            cp.wait_recv()

    outs = pl.pallas_call(
        body, name=name, out_shape=tuple(pltpu.HBM(a.shape, a.dtype) for a in srcs + lands),
        in_specs=(_HBM,) * (2 * k) + (_SEM,) * len(sems) + (_ANY,) * len(after), out_specs=(_HBM,) * (2 * k),
        input_output_aliases={i: i for i in range(2 * k)}, compiler_params=pltpu.CompilerParams(has_side_effects=_EFFECT),
    )(*srcs, *lands, *sems, *after)
    return list(outs[:k]), list(outs[k:])


def _gather_finish(lands, pieces):
    k = len(lands)

    def body(*refs):
        land_refs, piece_refs, out_refs, stages = refs[:k], refs[k:2 * k], refs[2 * k:3 * k], refs[3 * k:4 * k]
        send_sems, recv_sems, load_sems, store_sems = refs[4 * k:]
        x, y, c = _place()
        sibling = (x, y, 1 - c)
        remote, loads, stores, arrivals = [], [], [], []
        for a in range(k):
            m = lands[a].shape[0] // 8

            def rows(px, py, pc, ref, m=m):
                return ref.at[pl.ds((4 * px + 2 * py + pc) * m, m), :]

            for j, (tx, ty) in enumerate(_other_chips(x, y)):
                sems = dict(send_sem=send_sems.at[3 * a + j], recv_sem=recv_sems.at[3 * a + j], device_id=sibling,
                            device_id_type=MESH)
                remote.append(pltpu.make_async_remote_copy(
                    src_ref=rows(tx, ty, c, land_refs[a]), dst_ref=rows(tx, ty, c, out_refs[a]), **sems))
                arrivals.append(pltpu.make_async_remote_copy(
                    src_ref=rows(tx, ty, 1 - c, out_refs[a]), dst_ref=rows(tx, ty, 1 - c, out_refs[a]), **sems))
            for h in range(2):
                loads.append(pltpu.make_async_copy(piece_refs[a].at[pl.ds(h * m, m), :], stages[a].at[h],
                                                   load_sems.at[2 * a + h]))
                stores.append(pltpu.make_async_copy(stages[a].at[h], rows(x, y, h, out_refs[a]), store_sems.at[2 * a + h]))
        for cp in remote + loads:
            cp.start()
        for ld, st in zip(loads, stores):
            ld.wait()
            st.start()
        for cp, arrival in zip(remote, arrivals):
            cp.wait_send()
            arrival.wait_recv()
        for cp in stores:
            cp.wait()

    return pl.pallas_call(
        body, name="gather_late_finish", out_shape=[jax.ShapeDtypeStruct(a.shape, a.dtype) for a in lands],
        in_specs=[_ANY] * (2 * k), out_specs=[_ANY] * k, input_output_aliases={i: i for i in range(k)},
        scratch_shapes=[pltpu.VMEM((2, a.shape[0] // 8, a.shape[1]), a.dtype) for a in lands]
        + [pltpu.SemaphoreType.DMA((3 * k,)), pltpu.SemaphoreType.DMA((3 * k,)), pltpu.SemaphoreType.DMA((2 * k,)),
           pltpu.SemaphoreType.DMA((2 * k,))],
    )(*lands, *pieces)


def _pair_fill(bufs, name):
    k = len(bufs)

    def body(*refs):
        send_sems, recv_sems = refs[-2:]
        x, y, c = _place()
        copies = [pltpu.make_async_remote_copy(
            src_ref=refs[i].at[c], dst_ref=refs[k + i].at[c], send_sem=send_sems.at[i], recv_sem=recv_sems.at[i],
            device_id=(x, y, 1 - c), device_id_type=MESH) for i in range(k)]
        for cp in copies:
            cp.start()
        for i, cp in enumerate(copies):
            cp.wait_send()
            pltpu.make_async_remote_copy(
                src_ref=refs[i].at[1 - c], dst_ref=refs[k + i].at[1 - c], send_sem=send_sems.at[i],
                recv_sem=recv_sems.at[i], device_id=(x, y, 1 - c), device_id_type=MESH).wait_recv()

    return pl.pallas_call(
        body, name=name, out_shape=[jax.ShapeDtypeStruct(a.shape, a.dtype) for a in bufs], in_specs=[_ANY] * k,
        out_specs=[_ANY] * k, input_output_aliases={i: i for i in range(k)},
        scratch_shapes=[pltpu.SemaphoreType.DMA((k,)), pltpu.SemaphoreType.DMA((k,))],
    )(*bufs)


def _row_tile(rows, cap=512):
    for t in range(cap - cap % 8, 7, -8):
        if rows % t == 0:
            return t
    return rows


def _elementwise(fn, n_out, name, *arrs, out_dtype=F32):
    rows, cols = arrs[0].shape
    tr = _row_tile(rows)
    n_in = len(arrs)

    def body(*refs):
        outs = fn(*[r[...].astype(F32) for r in refs[:n_in]])
        for r, o in zip(refs[n_in:], outs):
            r[...] = o.astype(out_dtype)

    spec = pl.BlockSpec((tr, cols), lambda i: (i, 0))
    return pl.pallas_call(
        body, grid=(rows // tr,), name=name, in_specs=[spec] * n_in, out_specs=[spec] * n_out,
        out_shape=[jax.ShapeDtypeStruct((rows, cols), out_dtype)] * n_out, compiler_params=_params("arbitrary"),
    )(*arrs)


def _pair_sums(gpacks, rbigs, ci, name):
    k = len(gpacks)

    def body(c_ref, *refs):
        for g_ref, r_ref, o_ref in zip(refs[:k], refs[k:2 * k], refs[2 * k:]):
            o_ref[...] = (g_ref[...] + r_ref[...]).astype(BF)

    half = lambda a: pl.BlockSpec((None,) + a.shape[1:], lambda s, c: (s, 0, 0))
    return pl.pallas_call(
        body, name=name, out_shape=[jax.ShapeDtypeStruct(r.shape, BF) for r in rbigs],
        grid_spec=pltpu.PrefetchScalarGridSpec(
            num_scalar_prefetch=1, grid=(N_CHIPS,),
            in_specs=[pl.BlockSpec((None, None) + g.shape[2:], lambda s, c: (s, c[0], 0, 0)) for g in gpacks]
            + [half(r) for r in rbigs],
            out_specs=[half(r) for r in rbigs]),
        compiler_params=_params("arbitrary"),
    )(ci.reshape(1), *gpacks, *rbigs)


def _chip_sums(sums, landed, chip, ci, name):
    k = len(sums)

    def body(p_ref, *refs):
        for own_ref, land_ref, o_ref in zip(refs[:k], refs[k:2 * k], refs[2 * k:]):
            f = lambda v: v.astype(F32)
            o_ref[...] = _add4(f(own_ref[...]), f(land_ref[0]), f(land_ref[1]), f(land_ref[2]))[0]

    return pl.pallas_call(
        body, name=name, out_shape=[jax.ShapeDtypeStruct((2,) + s.shape[1:], F32) for s in sums],
        grid_spec=pltpu.PrefetchScalarGridSpec(
            num_scalar_prefetch=1, grid=(1,),
            in_specs=[pl.BlockSpec((None,) + s.shape[1:], lambda i, p: (p[0], 0, 0)) for s in sums]
            + [pl.BlockSpec(l.shape, lambda i, p: (0, 0, 0)) for l in landed],
            out_specs=[pl.BlockSpec((None,) + s.shape[1:], lambda i, p: (p[1], 0, 0)) for s in sums]),
        compiler_params=_params("arbitrary"),
    )(jnp.stack([chip, ci]), *sums, *landed)


def _add2(a, b):
    return (a + b,)


def _add4(own, r0, r1, r2):
    return ((own + r2) + (r0 + r1),)


def _adamw_small(ws, gs, ms, vs):
    k = len(ws)

    def body(*refs):
        for i in range(k):
            outs = _adamw_math(*[refs[j * k + i][...] for j in range(4)])
            for j, o in enumerate(outs):
                refs[(4 + j) * k + i][...] = o

    return pl.pallas_call(
        body, name="adamw_small", out_shape=[jax.ShapeDtypeStruct(w.shape, F32) for w in ws] * 3,
    )(*ws, *gs, *ms, *vs)


def _adamw_math(w, g, m, v):
    m = ADAM_B1 * m + (1.0 - ADAM_B1) * g
    v = ADAM_B2 * v + (1.0 - ADAM_B2) * (g * g)
    m_hat = m / (1.0 - ADAM_B1 ** ADAM_STEP)
    v_hat = v / (1.0 - ADAM_B2 ** ADAM_STEP)
    delta = -ADAM_LR * (m_hat / (jnp.sqrt(v_hat) + ADAM_EPS) + ADAM_WD * w)
    return delta, m, v


WEIGHTS = ["meta_tokens", "ln1_g", "w_in", "q_a_norm_g", "w_uq", "kv_a_norm_g", "w_ukv", "q_norm_g", "k_norm_g",
           "conv_w", "conv_b", "lru_wa", "lru_ba", "lru_wi", "lru_bi", "lru_lambda", "attn_out_g", "rnn_out_g",
           "w_out", "ln2_g", "w_gate", "w_up", "w_down"]
BIG = ["w_in", "w_uq", "w_ukv", "w_out", "w_gate", "w_up", "w_down"]
BIG_T = {"w_in": True, "w_uq": True, "w_ukv": True, "w_out": False, "w_gate": True, "w_up": True, "w_down": False}
BIG_ROWS = {"w_in": 424, "w_uq": 72, "w_ukv": 64, "w_out": 256, "w_gate": 704, "w_up": 704, "w_down": 704}
EARLY = ["w_in", "w_uq", "w_ukv"]
LATE = ["w_out", "w_gate", "w_up", "w_down"]
EARLY_ROWS = 576
SMALL_SHARDED = ["meta_tokens", "conv_w", "lru_ba", "lru_bi", "lru_lambda"]
SMALL = [n for n in WEIGHTS if n not in BIG]
SMALL_PACK_ROWS = 160


def _offsets(names):
    off, o = {}, 0
    for n in names:
        off[n] = o
        o += BIG_ROWS[n]
    return off


def _shard_pack(names, src, rows):
    parts = [_to_pack_piece(n, src[n]) for n in names]
    used = sum(BIG_ROWS[n] for n in names)
    if rows > used:
        parts.append(jnp.zeros((rows - used, D), F32))
    return jnp.concatenate(parts, axis=0)


def _grad_pack(names, g, rows):
    parts = [g[n].reshape(N_CHIPS, BIG_ROWS[n], D) for n in names]
    used = sum(BIG_ROWS[n] for n in names)
    if rows > used:
        parts.append(jnp.zeros((N_CHIPS, rows - used, D), F32))
    return jnp.concatenate(parts, axis=1).reshape(N_CHIPS, 2, rows // 2, D)


def _to_pack_piece(name, shard):
    a = shard[0].T if BIG_T[name] else shard[0]
    return a.reshape(BIG_ROWS[name], D)


def _flat_pack(arrs, rows):
    flat = jnp.concatenate([a.reshape(-1) for a in arrs])
    return jnp.pad(flat, (0, rows * D - flat.shape[0])).reshape(rows, D)


def _flat_unpack(pack, shapes):
    flat, out, o = pack.reshape(-1), [], 0
    for s in shapes:
        n = math.prod(s)
        out.append(flat[o:o + n].reshape(s))
        o += n
    return out


def kernel(x, meta_tokens, ln1_g, w_in, q_a_norm_g, w_uq, kv_a_norm_g, w_ukv, q_norm_g, k_norm_g, conv_w, conv_b, lru_wa, lru_ba, lru_wi, lru_bi, lru_lambda, attn_out_g, rnn_out_g, w_out, ln2_g, w_gate, w_up, w_down, loss_target, m_meta_tokens, m_ln1_g, m_w_in, m_q_a_norm_g, m_w_uq, m_kv_a_norm_g, m_w_ukv, m_q_norm_g, m_k_norm_g, m_conv_w, m_conv_b, m_lru_wa, m_lru_ba, m_lru_wi, m_lru_bi, m_lru_lambda, m_attn_out_g, m_rnn_out_g, m_w_out, m_ln2_g, m_w_gate, m_w_up, m_w_down, v_meta_tokens, v_ln1_g, v_w_in, v_q_a_norm_g, v_w_uq, v_kv_a_norm_g, v_w_ukv, v_q_norm_g, v_k_norm_g, v_conv_w, v_conv_b, v_lru_wa, v_lru_ba, v_lru_wi, v_lru_bi, v_lru_lambda, v_attn_out_g, v_rnn_out_g, v_w_out, v_ln2_g, v_w_gate, v_w_up, v_w_down):
    wts = dict(zip(WEIGHTS, (meta_tokens, ln1_g, w_in, q_a_norm_g, w_uq, kv_a_norm_g, w_ukv, q_norm_g, k_norm_g, conv_w, conv_b, lru_wa, lru_ba, lru_wi, lru_bi, lru_lambda, attn_out_g, rnn_out_g, w_out, ln2_g, w_gate, w_up, w_down)))
    mom = dict(zip(WEIGHTS, (m_meta_tokens, m_ln1_g, m_w_in, m_q_a_norm_g, m_w_uq, m_kv_a_norm_g, m_w_ukv, m_q_norm_g, m_k_norm_g, m_conv_w, m_conv_b, m_lru_wa, m_lru_ba, m_lru_wi, m_lru_bi, m_lru_lambda, m_attn_out_g, m_rnn_out_g, m_w_out, m_ln2_g, m_w_gate, m_w_up, m_w_down)))
    var = dict(zip(WEIGHTS, (v_meta_tokens, v_ln1_g, v_w_in, v_q_a_norm_g, v_w_uq, v_kv_a_norm_g, v_w_ukv, v_q_norm_g, v_k_norm_g, v_conv_w, v_conv_b, v_lru_wa, v_lru_ba, v_lru_wi, v_lru_bi, v_lru_lambda, v_attn_out_g, v_rnn_out_g, v_w_out, v_ln2_g, v_w_gate, v_w_up, v_w_down)))
    xi, yi, ci = _place()
    chip = 2 * xi + yi
    off_e = _offsets(EARLY)
    half_e = EARLY_ROWS // 2
    gather_plan = _to_chips(lambda ref, tx, ty, c: ref.at[pl.ds(c * (ref.shape[0] // 2), ref.shape[0] // 2), :],
                            lambda ref, j, px, py, c: ref.at[pl.ds((4 * px + 2 * py + c) * (ref.shape[0] // 8),
                                                                   ref.shape[0] // 8), :])
    scatter_plan = _to_chips(lambda ref, tx, ty, c: ref.at[2 * tx + ty], lambda ref, j, px, py, c: ref.at[j])
    everywhere = _to_chips(lambda ref, tx, ty, c: ref, lambda ref, j, px, py, c: ref.at[j])
    n_late = len(LATE)

    pack_e = _shard_pack(EARLY, wts, EARLY_ROWS).astype(BF)
    spack = jnp.concatenate([meta_tokens[:, :LANES], meta_tokens[:, LANES:], conv_w[0], lru_ba[0], lru_bi[0],
                             lru_lambda[0], jnp.zeros((6, LANES), F32)], axis=0)
    (ge, gs), gathered = _all_gather([lax.dynamic_slice_in_dim(pack_e, ci * half_e, half_e, axis=0),
                                      lax.dynamic_slice_in_dim(spack, ci * 24, 24, axis=0)], "gather_early")
    ge = ge.reshape(N_CHIPS, EARLY_ROWS, D)
    gs = gs.reshape(N_CHIPS, 48, LANES)
    full = {n: ge[:, off_e[n]:off_e[n] + BIG_ROWS[n]] for n in EARLY}
    pieces_l = [(_to_pack_piece(n, wts[n]) + gathered[0, 0]).astype(BF) for n in LATE]
    sems_l, src_l, land_l, tied = _split_start(
        "gather_late_start", pieces_l, [lax.empty((N_CHIPS * BIG_ROWS[n], D), BF) for n in LATE], gather_plan, 3 * n_late)

    def late_weights(after):
        pieces, lands = _split_wait("gather_late_wait", sems_l, src_l, land_l, after, gather_plan)
        w_out_, w_gate_, w_up_, w_down_ = _gather_finish(lands, pieces)
        return dict(w_out=w_out_, w_gate_t=w_gate_, w_up_t=w_up_, w_down=w_down_)

    pair, late = {}, {}

    def early_grads(g_late):
        halves = [g_late[n].reshape(N_CHIPS, 2, BIG_ROWS[n] // 2, D) for n in LATE]
        pair["sems"], pair["src"], pair["land"], zeros = _split_start(
            "grad_pair_late_start", halves, [lax.empty((N_CHIPS, BIG_ROWS[n] // 2, D), F32) for n in LATE], _to_sibling,
            N_CHIPS * n_late)
        return zeros[0, 0]

    def mid_grads(after):
        halves, landed = _split_wait("grad_pair_late_wait", pair["sems"], pair["src"], pair["land"], after, _to_sibling)
        chip_sums = _pair_sums(halves, landed, ci, "grad_pair_sum_late")
        late["sems"], late["src"], late["land"], zeros = _split_start(
            "grad_chip_late_start", chip_sums, [lax.empty((3, BIG_ROWS[n] // 2, D), BF) for n in LATE], scatter_plan,
            3 * n_late)
        return zeros[0, 0]

    cols = lambda a: a.transpose(1, 0, 2).reshape(a.shape[1], N_CHIPS * a.shape[2])
    meta_full = cols(jnp.concatenate([gs[:, 0:16], gs[:, 16:32]], axis=2))
    w = dict(
        w_in_t=full["w_in"].reshape(IN_COLS, D), w_uq_t=full["w_uq"].reshape(N_HEADS * QK_HEAD, Q_LORA),
        w_ukv_t=full["w_ukv"].reshape(2 * D_ATTN, KV_LORA),
        ln1_g=ln1_g, q_a_norm_g=q_a_norm_g, kv_a_norm_g=kv_a_norm_g, q_norm_g=q_norm_g, k_norm_g=k_norm_g,
        conv_w=cols(gs[:, 32:36]), conv_b=conv_b, lru_wa=lru_wa[0], lru_ba=cols(gs[:, 36:38]), lru_wi=lru_wi[0],
        lru_bi=cols(gs[:, 38:40]), lru_lambda=cols(gs[:, 40:42]), attn_out_g=attn_out_g, rnn_out_g=rnn_out_g,
        ln2_g=ln2_g,
    )

    loss_local, grad_x, g, last = _local_step(x, loss_target, meta_full + tied[0, 0], w, late_weights, early_grads,
                                              mid_grads)

    gpack = _grad_pack(EARLY, {"w_in": g["w_in_t"], "w_uq": g["w_uq_t"], "w_ukv": g["w_ukv_t"]}, EARLY_ROWS)
    full_shapes = {n: wts[n].shape for n in SMALL}
    full_shapes.update(meta_tokens=(N_META, D), conv_w=(1, CONV_W, D_RNN), lru_ba=(1, 2, D_RNN), lru_bi=(1, 2, D_RNN),
                       lru_lambda=(1, 2, D_RNN))
    gsmall = _flat_pack([g[n] for n in SMALL] + [loss_local], SMALL_PACK_ROWS)
    rbig, rsmall = _pair_exchange(gpack, [gsmall], "grad_pair_exchange")
    chip_big = _pair_sums([gpack], [rbig], ci, "grad_pair_sum")
    (chip_small,) = _elementwise(_add2, 1, "grad_pair_sum_small", gsmall, rsmall)
    sems_e, src_e, land_e, zero_e = _split_start(
        "grad_chip_early_start", chip_big, [lax.empty((3, half_e, D), BF)], scatter_plan, 3)
    sems_s, src_s, land_s, zero_s = _split_start(
        "grad_small_start", [chip_small], [lax.empty((3, SMALL_PACK_ROWS, D), F32)], everywhere, 3)

    grads, delta, new_m, new_v = {}, {}, {}, {}

    def adamw_big(n, gshard):
        _, k, cols = wts[n].shape
        as_rows = (lambda a: a[0].T) if BIG_T[n] else (lambda a: a[0])
        back = (lambda a: a.T[None]) if BIG_T[n] else (lambda a: a[None])
        g2 = gshard.reshape((cols, k) if BIG_T[n] else (k, cols))
        d_, m_, v_ = _elementwise(_adamw_math, 3, "adamw_" + n, as_rows(wts[n]), g2, as_rows(mom[n]), as_rows(var[n]))
        grads[n], delta[n], new_m[n], new_v[n] = back(g2), back(d_), back(m_), back(v_)
        return d_

    sums, landed = _split_wait("grad_chip_late_wait", late["sems"], late["src"], late["land"], last + [zero_e, zero_s],
                               scatter_plan)
    shards_l = _pair_fill(_chip_sums(sums, landed, chip, ci, "grad_chip_sum_late"), "grad_pair_fill_late")
    done_late = [adamw_big(n, buf) for n, buf in zip(LATE, shards_l)][-1]
    src_e, land_e = _split_wait("grad_chip_early_wait", sems_e, src_e, land_e, [done_late], scatter_plan)
    src_s, land_s = _split_wait("grad_small_wait", sems_s, src_s, land_s, [done_late], everywhere)
    (shard_e,) = _pair_fill(_chip_sums(src_e, land_e, chip, ci, "grad_chip_sum"), "grad_pair_fill_early")
    shard_e = shard_e.reshape(EARLY_ROWS, D)
    for n in EARLY:
        adamw_big(n, shard_e[off_e[n]:off_e[n] + BIG_ROWS[n]])
    (small_sum,) = _elementwise(_add4, 1, "grad_chip_sum_small", src_s[0], land_s[0][0], land_s[0][1], land_s[0][2])
    *small_grads, loss = _flat_unpack(small_sum, [full_shapes[n] for n in SMALL] + [()])
    small_full = dict(zip(SMALL, small_grads))
    for n in SMALL:
        a = small_full[n]
        if n in SMALL_SHARDED:
            width = wts[n].shape[-1]
            a = lax.dynamic_slice_in_dim(a, chip * width, width, axis=a.ndim - 1)
        grads[n] = a.reshape(wts[n].shape)

    rows_of = lambda a: a.reshape(-1, a.shape[-1])
    outs = _adamw_small(*[[rows_of(src[n]) for n in SMALL] for src in (wts, grads, mom, var)])
    for j, dst in enumerate((delta, new_m, new_v)):
        dst.update({n: outs[j * len(SMALL) + i].reshape(wts[n].shape) for i, n in enumerate(SMALL)})

    return (loss, grad_x, *[grads[n] for n in WEIGHTS], *[delta[n] for n in WEIGHTS],
            *[new_m[n] for n in WEIGHTS], *[new_v[n] for n in WEIGHTS])
```

```python
import functools
import math

import jax
import jax.numpy as jnp
from jax import lax
from jax.experimental import pallas as pl
from jax.experimental.pallas import tpu as pltpu

F32 = jnp.float32
BF = jnp.bfloat16
MESH = pl.DeviceIdType.MESH

D = 1024
SEQ = 2048
N_META = 16
T = N_META + SEQ
N_HEADS = 8
QK_NOPE = 64
QK_ROPE = 32
QK_HEAD = 96
V_HEAD = 64
Q_LORA = 384
KV_LORA = 256
D_ATTN = 512
D_RNN = 512
RNN_BW = 64
CONV_W = 4
LRU_C = 8.0
ROPE_THETA = 10000.0
D_FF = 2816
EPS = 1e-6
IN_COLS = 1696
ADAM_LR, ADAM_B1, ADAM_B2, ADAM_EPS, ADAM_WD, ADAM_STEP = 0.001, 0.9, 0.999, 1e-08, 0.01, 10

LANES = 128
TP = 2176
NB = 2
R = NB * TP
TR = 256
TRF = 256
TQ = 544
HP = LANES
PC = 1792
O_CKV, O_KR, O_XR, O_XG = 384, 640, 768, 1280
CG = 128
N_CG = D_RNN // CG
VMEM_LIMIT = 56 * 1024 * 1024
N_CHIPS = 4
SCALE = QK_HEAD ** -0.5
KEY_MASK = -30000.0
LOG2_E = 1.4426950408889634
SCALE_LOG2 = SCALE * LOG2_E


def _nt(a, b):
    return lax.dot_general(a, b, (((1,), (1,)), ((), ())), preferred_element_type=F32)


def _nn(a, b):
    return jnp.dot(a, b, preferred_element_type=F32)


def _tn(a, b):
    return lax.dot_general(a, b, (((0,), (0,)), ((), ())), preferred_element_type=F32)


def _rms(x, g, n):
    ms = jnp.sum(x * x, axis=-1, keepdims=True) * (1.0 / n)
    return x * lax.rsqrt(ms + EPS) * g


def _lane_sum(y):
    return jnp.sum(y, axis=-1, keepdims=True)


def _rot(x):
    lane = lax.broadcasted_iota(jnp.int32, x.shape, 1)
    left = pltpu.roll(x, HP - 16, 1)
    right = pltpu.roll(x, 16, 1)
    lo = (lane >= QK_NOPE) & (lane < QK_NOPE + 16)
    hi = (lane >= QK_NOPE + 16) & (lane < QK_HEAD)
    return jnp.where(lo, -left, jnp.where(hi, right, 0.0))


def _head(x, g, cs, sn):
    n = x * lax.rsqrt(_lane_sum(x * x) * (1.0 / QK_HEAD) + EPS) * g
    return n * cs + _rot(n) * sn


def _head_bwd(x, g, cs, sn, dout):
    rs = lax.rsqrt(_lane_sum(x * x) * (1.0 / QK_HEAD) + EPS)
    xh = x * rs
    dn = dout * cs - _rot(dout * sn)
    gdn = g * dn
    t = _lane_sum(gdn * xh) * (1.0 / QK_HEAD)
    return rs * (gdn - xh * t), jnp.sum(dn * xh, axis=0, keepdims=True)


def _const_spec(shape):
    return pl.BlockSpec(shape, lambda *_: (0,) * len(shape), pipeline_mode=pl.Buffered(1))


def _row_spec(n, tr=TR):
    return pl.BlockSpec((tr, n), lambda i: (i, 0))


def _params(*sem, vmem=VMEM_LIMIT):
    return pltpu.CompilerParams(dimension_semantics=sem, vmem_limit_bytes=vmem)


def _stage_a_fwd(hp, cs, sn, cw):
    def body(hp_ref, cs_ref, sn_ref, ln1, win, qag, wq, kvag, wk, wv, qg, kg,
             pa_ref, xr_ref, xg_ref, q_ref, k_ref, v_ref):
        hn = _rms(hp_ref[...], ln1[...], D).astype(BF)
        p = _nt(hn, win[...])
        pa_ref[...] = p[:, :O_XR]
        xr_ref[...] = p[:, O_XR:O_XG]
        xg_ref[...] = p[:, O_XG:]
        cqn = _rms(p[:, :O_CKV], qag[...], Q_LORA).astype(BF)
        ckvn = _rms(p[:, O_CKV:O_KR], kvag[...], KV_LORA).astype(BF)
        kr = p[:, O_KR:O_XR]
        c, s = cs_ref[...], sn_ref[...]
        mask_lane = lax.broadcasted_iota(jnp.int32, (1, HP), 1) == QK_HEAD
        row = pl.program_id(0) * TRF + lax.broadcasted_iota(jnp.int32, (TRF, 1), 0)
        key_mask = jnp.where(jnp.where(row >= TP, row - TP, row) < T, 0.0, KEY_MASK)
        qraw = _nt(cqn, wq[...])
        kraw = _nt(ckvn, wk[...])
        for h in range(N_HEADS):
            sl = slice(h * HP, (h + 1) * HP)
            q_ref[:, sl] = jnp.where(mask_lane, 1.0, _head(qraw[:, sl], qg[...], c, s)).astype(BF)
            k_ref[:, sl] = jnp.where(mask_lane, key_mask, _head(kraw[:, sl] + kr, kg[...], c, s)).astype(BF)
        v_ref[...] = _nt(ckvn, wv[...]).astype(BF)

    rs = lambda n: _row_spec(n, TRF)
    return pl.pallas_call(
        body, grid=(R // TRF,), name="stage_a_fwd",
        in_specs=[rs(D), rs(HP), rs(HP), _const_spec((1, D)), _const_spec((PC, D)),
                  _const_spec((1, Q_LORA)), _const_spec((N_HEADS * HP, Q_LORA)), _const_spec((1, KV_LORA)),
                  _const_spec((N_HEADS * HP, KV_LORA)), _const_spec((D_ATTN, KV_LORA)), _const_spec((1, HP)),
                  _const_spec((1, HP))],
        out_specs=[rs(O_XR), rs(D_RNN), rs(D_RNN), rs(N_HEADS * HP), rs(N_HEADS * HP), rs(D_ATTN)],
        out_shape=[jax.ShapeDtypeStruct((R, O_XR), F32), jax.ShapeDtypeStruct((R, D_RNN), F32),
                   jax.ShapeDtypeStruct((R, D_RNN), F32), jax.ShapeDtypeStruct((R, N_HEADS * HP), BF),
                   jax.ShapeDtypeStruct((R, N_HEADS * HP), BF), jax.ShapeDtypeStruct((R, D_ATTN), BF)],
        compiler_params=_params("arbitrary"),
    )(hp, cs, sn, cw["ln1_g"], cw["win"], cw["qa_g"], cw["wq"], cw["kva_g"], cw["wk"], cw["wv"], cw["q_g"], cw["k_g"])


def _stage_a_bwd(dq, dk, dv, dxr, dxg, dh1, hp, pa, cs, sn, cw):
    def body(dq_ref, dk_ref, dv_ref, dxr_ref, dxg_ref, dh1_ref, hp_ref, pa_ref, cs_ref, sn_ref,
             ln1, win, qag, wq, kvag, wk, wv, qg, kg,
             dhp_ref, dp_ref, dqraw_ref, dkraw_ref, hn_ref, cqn_ref, ckvn_ref,
             dln1_ref, dqag_ref, dkvag_ref, dqg_ref, dkg_ref):
        @pl.when(pl.program_id(0) == 0)
        def _():
            for r in (dln1_ref, dqag_ref, dkvag_ref, dqg_ref, dkg_ref):
                r[...] = jnp.zeros_like(r)

        hn, vjp_ln1 = jax.vjp(lambda h, g: _rms(h, g, D), hp_ref[...], ln1[...])
        hn_ref[...] = hn.astype(BF)
        pa_v = pa_ref[...]
        cqn, vjp_qa = jax.vjp(lambda x, g: _rms(x, g, Q_LORA), pa_v[:, :O_CKV], qag[...])
        ckvn, vjp_kva = jax.vjp(lambda x, g: _rms(x, g, KV_LORA), pa_v[:, O_CKV:O_KR], kvag[...])
        kr = pa_v[:, O_KR:O_XR]
        cqnb, ckvnb = cqn.astype(BF), ckvn.astype(BF)
        cqn_ref[...] = cqnb
        ckvn_ref[...] = ckvnb
        c, s = cs_ref[...], sn_ref[...]
        lane = lax.broadcasted_iota(jnp.int32, (1, HP), 1)
        rope_lanes = ((lane >= QK_NOPE) & (lane < QK_HEAD)).astype(F32)
        dkr = jnp.zeros((TR, HP), F32)
        dqg = jnp.zeros((1, HP), F32)
        dkg = jnp.zeros((1, HP), F32)
        qraw = _nt(cqnb, wq[...])
        kraw = _nt(ckvnb, wk[...])
        for h in range(N_HEADS):
            sl = slice(h * HP, (h + 1) * HP)
            dqraw, dg = _head_bwd(qraw[:, sl], qg[...], c, s, dq_ref[:, sl])
            dqg = dqg + dg
            dqraw_ref[:, sl] = dqraw.astype(BF)
            dkraw, dg = _head_bwd(kraw[:, sl] + kr, kg[...], c, s, dk_ref[:, sl])
            dkg = dkg + dg
            dkraw_ref[:, sl] = dkraw.astype(BF)
            dkr = dkr + dkraw * rope_lanes
        dcq, dqag = vjp_qa(_nn(dqraw_ref[...], wq[...]))
        dckv, dkvag = vjp_kva(_nn(dkraw_ref[...], wk[...]) + _nn(dv_ref[...].astype(BF), wv[...]))
        dpb = jnp.concatenate([dcq, dckv, dkr, dxr_ref[...], dxg_ref[...]], axis=1).astype(BF)
        dp_ref[...] = dpb
        dh, dln1 = vjp_ln1(_nn(dpb, win[...]))
        dhp_ref[...] = dh + dh1_ref[...]
        dln1_ref[...] += dln1
        dqag_ref[...] += dqag
        dkvag_ref[...] += dkvag
        dqg_ref[...] += dqg
        dkg_ref[...] += dkg

    acc = lambda n: pl.BlockSpec((1, n), lambda i: (0, 0))
    return pl.pallas_call(
        body, grid=(R // TR,), name="stage_a_bwd",
        in_specs=[_row_spec(N_HEADS * HP), _row_spec(N_HEADS * HP), _row_spec(D_ATTN), _row_spec(D_RNN),
                  _row_spec(D_RNN), _row_spec(D), _row_spec(D), _row_spec(O_XR), _row_spec(HP), _row_spec(HP),
                  _const_spec((1, D)), _const_spec((PC, D)), _const_spec((1, Q_LORA)),
                  _const_spec((N_HEADS * HP, Q_LORA)), _const_spec((1, KV_LORA)),
                  _const_spec((N_HEADS * HP, KV_LORA)), _const_spec((D_ATTN, KV_LORA)), _const_spec((1, HP)),
                  _const_spec((1, HP))],
        out_specs=[_row_spec(D), _row_spec(PC), _row_spec(N_HEADS * HP), _row_spec(N_HEADS * HP), _row_spec(D),
                   _row_spec(Q_LORA), _row_spec(KV_LORA), acc(D), acc(Q_LORA), acc(KV_LORA), acc(HP), acc(HP)],
        out_shape=[jax.ShapeDtypeStruct((R, D), F32), jax.ShapeDtypeStruct((R, PC), BF),
                   jax.ShapeDtypeStruct((R, N_HEADS * HP), BF), jax.ShapeDtypeStruct((R, N_HEADS * HP), BF),
                   jax.ShapeDtypeStruct((R, D), BF), jax.ShapeDtypeStruct((R, Q_LORA), BF),
                   jax.ShapeDtypeStruct((R, KV_LORA), BF), jax.ShapeDtypeStruct((1, D), F32),
                   jax.ShapeDtypeStruct((1, Q_LORA), F32), jax.ShapeDtypeStruct((1, KV_LORA), F32),
                   jax.ShapeDtypeStruct((1, HP), F32), jax.ShapeDtypeStruct((1, HP), F32)],
        compiler_params=_params("arbitrary"),
    )(dq, dk, dv, dxr, dxg, dh1, hp, pa, cs, sn, cw["ln1_g"], cw["win"], cw["qa_g"], cw["wq"], cw["kva_g"],
      cw["wk"], cw["wv"], cw["q_g"], cw["k_g"])


def _head_mask(half, dtype):
    lane = lax.broadcasted_iota(jnp.int32, (1, 2 * V_HEAD), 1)
    return ((lane >= V_HEAD) == (half == 1)).astype(dtype)


def _attn_specs(tq):
    n_q = TP // tq
    return (NB, N_HEADS // 2, n_q), dict(
        q=pl.BlockSpec((tq, 2 * HP), lambda b, j, i: (b * n_q + i, j)),
        k=pl.BlockSpec((TP, 2 * HP), lambda b, j, i: (b, j)),
        v=pl.BlockSpec((TP, 2 * V_HEAD), lambda b, j, i: (b, j)),
        o=pl.BlockSpec((tq, 2 * V_HEAD), lambda b, j, i: (b * n_q + i, j)),
        lse=pl.BlockSpec((None, tq, 2), lambda b, j, i: (j, b * n_q + i, 0)))


TQF = 1088


def _attn_fwd(q, k, v):
    def body(q_ref, k_ref, v_ref, o_ref, lse_ref):
        v2 = v_ref[...]
        o = jnp.zeros((TQF, 2 * V_HEAD), F32)
        lse = []
        for hh in range(2):
            sl = slice(hh * HP, (hh + 1) * HP)
            raw = _nt(q_ref[:, sl], k_ref[:, sl])
            m = jnp.max(raw, axis=-1, keepdims=True)
            e = jnp.exp2((raw - m) * SCALE_LOG2)
            l = jnp.sum(e, axis=-1, keepdims=True)
            o = o + _nn(e.astype(BF), v2 * _head_mask(hh, BF)) * (1.0 / l)
            lse.append(m * SCALE_LOG2 + jnp.log(l) * LOG2_E)
        o_ref[...] = o
        lane = lax.broadcasted_iota(jnp.int32, (TQF, 2), 1)
        lse_ref[...] = jnp.where(lane == 0, lse[0], lse[1])

    grid, sp = _attn_specs(TQF)
    return pl.pallas_call(
        body, grid=grid, name="attn_fwd", in_specs=[sp["q"], sp["k"], sp["v"]], out_specs=[sp["o"], sp["lse"]],
        out_shape=[jax.ShapeDtypeStruct((R, D_ATTN), F32), jax.ShapeDtypeStruct((N_HEADS // 2, R, 2), F32)],
        compiler_params=_params("arbitrary", "arbitrary", "arbitrary"),
    )(q, k, v)


def _attn_bwd(q, k, v, o, lse, do):
    def body(q_ref, k_ref, v_ref, o_ref, lse_ref, do_ref, dq_ref, dk_ref, dv_ref):
        @pl.when(pl.program_id(2) == 0)
        def _():
            dk_ref[...] = jnp.zeros_like(dk_ref)
            dv_ref[...] = jnp.zeros_like(dv_ref)

        do = do_ref[...]
        dob = do.astype(BF)
        do_o = do * o_ref[...]
        v2 = v_ref[...]
        dv_sum = jnp.zeros((TP, 2 * V_HEAD), F32)
        for hh in range(2):
            sl = slice(hh * HP, (hh + 1) * HP)
            qb, kb = q_ref[:, sl], k_ref[:, sl]
            p = jnp.exp2(_nt(qb, kb) * SCALE_LOG2 - lse_ref[:, hh:hh + 1])
            dp = _nt(dob, v2 * _head_mask(hh, BF))
            delta = jnp.sum(do_o * _head_mask(hh, F32), axis=-1, keepdims=True)
            dsb = (p * (dp - delta) * SCALE).astype(BF)
            dq_ref[:, sl] = _nn(dsb, kb)
            dk_ref[:, sl] += _tn(dsb, qb)
            dv_sum = dv_sum + _tn(p.astype(BF), dob) * _head_mask(hh, F32)
        dv_ref[...] += dv_sum

    grid, sp = _attn_specs(TQ)
    return pl.pallas_call(
        body, grid=grid, name="attn_bwd", in_specs=[sp["q"], sp["k"], sp["v"], sp["o"], sp["lse"], sp["o"]],
        out_specs=[sp["q"], sp["k"], sp["v"]],
        out_shape=[jax.ShapeDtypeStruct((R, N_HEADS * HP), F32), jax.ShapeDtypeStruct((R, N_HEADS * HP), F32),
                   jax.ShapeDtypeStruct((R, D_ATTN), F32)],
        compiler_params=_params("arbitrary", "arbitrary", "arbitrary"),
    )(q, k, v, o, lse, do)


def _tile_prefix(a_ref, b_ref, reverse):
    tiles = (TP // 8, 8, CG)
    r8 = lax.broadcasted_iota(jnp.int32, tiles, 1)
    a, b = a_ref[...].reshape(tiles), b_ref[...].reshape(tiles)
    for s in (1, 2, 4):
        shift = 8 - s if reverse else s
        keep = (r8 < 8 - s) if reverse else (r8 >= s)
        b = jnp.where(keep, a * pltpu.roll(b, shift, 1) + b, b)
        a = jnp.where(keep, a * pltpu.roll(a, shift, 1), a)
    a_ref[...] = a.reshape(TP, CG)
    b_ref[...] = b.reshape(TP, CG)


def _scan_pair(af_ref, bf_ref, hf_ref, ab_ref, bb_ref, hb_ref):
    _tile_prefix(af_ref, bf_ref, False)
    _tile_prefix(ab_ref, bb_ref, True)
    n_tiles = TP // 8

    def step(i, carry):
        cf, cb = carry
        rf = pl.multiple_of(i * 8, 8)
        rb = pl.multiple_of((n_tiles - 1 - i) * 8, 8)
        hf_ref[pl.ds(rf, 8), :] = bf_ref[pl.ds(rf, 8), :] + af_ref[pl.ds(rf, 8), :] * cf
        hb_ref[pl.ds(rb, 8), :] = bb_ref[pl.ds(rb, 8), :] + ab_ref[pl.ds(rb, 8), :] * cb
        cf = bf_ref[pl.ds(rf + 7, 1), :] + af_ref[pl.ds(rf + 7, 1), :] * cf
        cb = bb_ref[pl.ds(rb, 1), :] + ab_ref[pl.ds(rb, 1), :] * cb
        return cf, cb

    zero = jnp.zeros((1, CG), F32)
    lax.fori_loop(0, n_tiles, step, (zero, zero), unroll=8)


def _shifts(x):
    t = lax.broadcasted_iota(jnp.int32, x.shape, 0)
    xm2 = jnp.where(t >= 2, pltpu.roll(x, 2, 0), 0.0)
    xm1 = jnp.where(t >= 1, pltpu.roll(x, 1, 0), 0.0)
    xp1 = jnp.where(t < TP - 1, pltpu.roll(x, TP - 1, 0), 0.0)
    return xm2, xm1, xp1


def _softplus(z):
    e = jnp.exp(-jnp.abs(z))
    small = e * (1.0 - e * (0.5 - e * (1.0 / 3.0)))
    return jnp.maximum(z, 0.0) + jnp.where(e < 0.01, small, jnp.log(1.0 + e))


def _sigmoid(x):
    return 0.5 * jnp.tanh(0.5 * x) + 0.5


def _one_minus_sq(log_a, a):
    x = 2.0 * log_a
    series = -x * (1.0 + x * 0.5 * (1.0 + x * (1.0 / 3.0) * (1.0 + x * 0.25)))
    return jnp.where(x > -0.05, series, 1.0 - a * a)


def _gates(row0, xc, pa_f, pi_f, pa_b, pi_b, lam_f, lam_b):
    t = row0 + lax.broadcasted_iota(jnp.int32, xc.shape, 0)
    valid = t < T
    out = []
    for pa, pi_, lam in ((pa_f, pi_f, lam_f), (pa_b, pi_b, lam_b)):
        r = _sigmoid(pa)
        gate_i = _sigmoid(pi_)
        log_a = -LRU_C * r * _softplus(-lam)
        a = jnp.exp(log_a)
        mult = jnp.sqrt(jnp.maximum(_one_minus_sq(log_a, a), 0.0))
        out += [a, jnp.where(valid, mult * (gate_i * xc), 0.0)]
    return tuple(out)


def _gates_bwd(row0, xc, pres, lams, cots):
    t = row0 + lax.broadcasted_iota(jnp.int32, xc.shape, 0)
    valid = t < T
    dxc = jnp.zeros_like(xc)
    dpres, dlams = [], []
    for d in range(2):
        pa, pi_, lam = pres[2 * d], pres[2 * d + 1], lams[d]
        da, db = cots[2 * d], jnp.where(valid, cots[2 * d + 1], 0.0)
        r = _sigmoid(pa)
        gate_i = _sigmoid(pi_)
        sp = _softplus(-lam)
        log_a = -LRU_C * r * sp
        a = jnp.exp(log_a)
        m2 = jnp.maximum(_one_minus_sq(log_a, a), 0.0)
        mult = jnp.sqrt(m2)
        dxc = dxc + db * (mult * gate_i)
        d_gate = db * (mult * xc)
        d_m2 = jnp.where(m2 > 0.0, db * (gate_i * xc) * (0.5 * lax.rsqrt(m2)), 0.0)
        d_log_a = da * a - 2.0 * d_m2 * (a * a)
        dpres += [d_log_a * (-LRU_C * sp) * (r * (1.0 - r)), d_gate * (gate_i * (1.0 - gate_i))]
        d_sp = jnp.sum(d_log_a * (-LRU_C * r), axis=0, keepdims=True)
        dlams.append(-d_sp * jax.nn.sigmoid(-lam))
    return dxc, dpres, dlams


def _rnn_specs():
    seq = pl.BlockSpec((TP, CG), lambda g, b: (b, g))
    return dict(
        seq=seq,
        cw=pl.BlockSpec((CONV_W, CG), lambda g, b: (0, g)),
        cb=pl.BlockSpec((1, CG), lambda g, b: (0, g)),
        w4=pl.BlockSpec((None, CG, 4 * CG), lambda g, b: (g, 0, 0)),
        b4=pl.BlockSpec((None, 1, 4 * CG), lambda g, b: (g, 0, 0)),
        lam=pl.BlockSpec((None, 1, 2 * CG), lambda g, b: (g, 0, 0)),
    )


def _conv(x, xm2, xm1, xp1, cw_ref, cb_ref):
    return cw_ref[0:1, :] * xm2 + cw_ref[1:2, :] * xm1 + cw_ref[2:3, :] * x + cw_ref[3:4, :] * xp1 + cb_ref[...]


TC = 128
N_TC = TP // TC


def _split4(pre):
    return pre[:, :CG], pre[:, CG:2 * CG], pre[:, 2 * CG:3 * CG], pre[:, 3 * CG:]


def _rnn_fwd(xr, xg, cw):
    def body(xr_ref, xg_ref, cw_ref, cb_ref, w4_ref, b4_ref, lam_ref, y_ref, hf_ref, hb_ref, xc_s, af, bf, ab, bb):
        x = xr_ref[...]
        xc_s[...] = _conv(x, *_shifts(x), cw_ref, cb_ref)
        lam = lam_ref[...]

        def chunk(i, _):
            rows = pl.ds(pl.multiple_of(i * TC, TC), TC)
            xc = xc_s[rows, :]
            pre = _nn(xc.astype(BF), w4_ref[...]) + b4_ref[...]
            a_f, b_f, a_b, b_b = _gates(i * TC, xc, *_split4(pre), lam[:, :CG], lam[:, CG:])
            af[rows, :] = a_f
            bf[rows, :] = b_f
            ab[rows, :] = a_b
            bb[rows, :] = b_b
            return 0

        lax.fori_loop(0, N_TC, chunk, 0)
        _scan_pair(af, bf, hf_ref, ab, bb, hb_ref)
        y_ref[...] = (hf_ref[...] + hb_ref[...]) * jax.nn.gelu(xg_ref[...])

    sp = _rnn_specs()
    return pl.pallas_call(
        body, grid=(N_CG, NB), name="rnn_fwd",
        in_specs=[sp["seq"], sp["seq"], sp["cw"], sp["cb"], sp["w4"], sp["b4"], sp["lam"]],
        out_specs=[sp["seq"]] * 3, out_shape=[jax.ShapeDtypeStruct((R, D_RNN), F32)] * 3,
        scratch_shapes=[pltpu.VMEM((TP, CG), F32)] * 5,
        compiler_params=_params("arbitrary", "arbitrary"),
    )(xr, xg, cw["conv_w"], cw["conv_b"], cw["w4"], cw["b4"], cw["lam"])


def _rnn_bwd(dy, xr, xg, hf, hb, cw):
    def body(dy_ref, xr_ref, xg_ref, hf_ref, hb_ref, cw_ref, cb_ref, w4_ref, b4_ref, lam_ref,
             dxr_ref, dxg_ref, dcw_ref, dcb_ref, dw4_ref, db4_ref, dlam_ref,
             xc_s, af_s, ab_s, dhs_s, dhs2_s, lf_s, lb_s, daf_s, dab_s, dxc_s):
        @pl.when(pl.program_id(1) == 0)
        def _():
            for r in (dcw_ref, dcb_ref, dw4_ref, db4_ref, dlam_ref):
                r[...] = jnp.zeros_like(r)

        x = xr_ref[...]
        xc_s[...] = _conv(x, *_shifts(x), cw_ref, cb_ref)
        lam = lam_ref[...]

        def chunk1(i, _):
            rows = pl.ds(pl.multiple_of(i * TC, TC), TC)
            xc = xc_s[rows, :]
            pre = _nn(xc.astype(BF), w4_ref[...]) + b4_ref[...]
            a_f, _, a_b, _ = _gates(i * TC, xc, *_split4(pre), lam[:, :CG], lam[:, CG:])
            af_s[rows, :] = a_f
            ab_s[rows, :] = a_b
            _, vjp_y = jax.vjp(lambda h, g: h * jax.nn.gelu(g), hf_ref[rows, :] + hb_ref[rows, :], xg_ref[rows, :])
            dhs, dxg = vjp_y(dy_ref[rows, :])
            dhs_s[rows, :] = dhs
            dhs2_s[rows, :] = dhs
            dxg_ref[rows, :] = dxg
            return 0

        lax.fori_loop(0, N_TC, chunk1, 0)
        t = lax.broadcasted_iota(jnp.int32, (TP, CG), 0)
        af_s[...] = pltpu.roll(af_s[...], TP - 1, 0)
        ab_s[...] = pltpu.roll(ab_s[...], 1, 0)
        _scan_pair(ab_s, dhs_s, lb_s, af_s, dhs2_s, lf_s)
        daf_s[...] = lf_s[...] * jnp.where(t >= 1, pltpu.roll(hf_ref[...], 1, 0), 0.0)
        dab_s[...] = lb_s[...] * jnp.where(t < TP - 1, pltpu.roll(hb_ref[...], TP - 1, 0), 0.0)

        def chunk2(i, _):
            rows = pl.ds(pl.multiple_of(i * TC, TC), TC)
            xc = xc_s[rows, :]
            xcb = xc.astype(BF)
            pre = _nn(xcb, w4_ref[...]) + b4_ref[...]
            dxc, dpres, dlams = _gates_bwd(i * TC, xc, _split4(pre), (lam[:, :CG], lam[:, CG:]),
                                           (daf_s[rows, :], lf_s[rows, :], dab_s[rows, :], lb_s[rows, :]))
            dpre = jnp.concatenate(dpres, axis=1)
            dpreb = dpre.astype(BF)
            dxc_s[rows, :] = dxc + _nt(dpreb, w4_ref[...])
            dw4_ref[...] += _tn(xcb, dpreb)
            db4_ref[...] += jnp.sum(dpre, axis=0, keepdims=True)
            dlam_ref[...] += jnp.concatenate(dlams, axis=1)
            return 0

        lax.fori_loop(0, N_TC, chunk2, 0)
        dxc = dxc_s[...]
        dcb_ref[...] += jnp.sum(dxc, axis=0, keepdims=True)
        for tap, xs in enumerate(_shifts(x)[:2] + (x,) + _shifts(x)[2:]):
            dcw_ref[tap:tap + 1, :] += jnp.sum(xs * dxc, axis=0, keepdims=True)
        dxr_ref[...] = (cw_ref[0:1, :] * jnp.where(t < TP - 2, pltpu.roll(dxc, TP - 2, 0), 0.0)
                        + cw_ref[1:2, :] * jnp.where(t < TP - 1, pltpu.roll(dxc, TP - 1, 0), 0.0)
                        + cw_ref[2:3, :] * dxc
                        + cw_ref[3:4, :] * jnp.where(t >= 1, pltpu.roll(dxc, 1, 0), 0.0))

    sp = _rnn_specs()
    return pl.pallas_call(
        body, grid=(N_CG, NB), name="rnn_bwd",
        in_specs=[sp["seq"]] * 5 + [sp["cw"], sp["cb"], sp["w4"], sp["b4"], sp["lam"]],
        out_specs=[sp["seq"], sp["seq"], sp["cw"], sp["cb"], sp["w4"], sp["b4"], sp["lam"]],
        out_shape=[jax.ShapeDtypeStruct((R, D_RNN), F32), jax.ShapeDtypeStruct((R, D_RNN), F32),
                   jax.ShapeDtypeStruct((CONV_W, D_RNN), F32), jax.ShapeDtypeStruct((1, D_RNN), F32),
                   jax.ShapeDtypeStruct((N_CG, CG, 4 * CG), F32), jax.ShapeDtypeStruct((N_CG, 1, 4 * CG), F32),
                   jax.ShapeDtypeStruct((N_CG, 1, 2 * CG), F32)],
        scratch_shapes=[pltpu.VMEM((TP, CG), F32)] * 10,
        compiler_params=_params("arbitrary", "arbitrary"),
    )(dy, xr, xg, hf, hb, cw["conv_w"], cw["conv_b"], cw["w4"], cw["b4"], cw["lam"])


TD = 256
STAGE_D_VMEM = 58 * 1024 * 1024


def _stage_d(hp, o, y, tgt, cw):
    def body(hp_ref, o_ref, y_ref, tgt_ref, ga, gr, wout, ln2, wg, wu, wd,
             do_ref, dy_ref, dh1_ref, mix_ref, dh1b_ref, hn2_ref, dg_ref, du_ref, act_ref, dh2b_ref,
             loss_ref, dga_ref, dgr_ref, dln2_ref):
        i = pl.program_id(0)

        @pl.when(i == 0)
        def _():
            for r in (loss_ref, dga_ref, dgr_ref, dln2_ref):
                r[...] = jnp.zeros_like(r)

        mix_a, vjp_a = jax.vjp(lambda x, g: _rms(x, g, D_ATTN), o_ref[...], ga[...])
        mix_r, vjp_r = jax.vjp(lambda x, g: _rms(x, g, D_RNN), y_ref[...], gr[...])
        mab, mrb = mix_a.astype(BF), mix_r.astype(BF)
        mix_ref[:, :D_ATTN] = mab
        mix_ref[:, D_ATTN:] = mrb
        h1 = hp_ref[...] + _nn(mab, wout[:D_ATTN, :]) + _nn(mrb, wout[D_ATTN:, :])
        hn2, vjp_ln2 = jax.vjp(lambda x, g: _rms(x, g, D), h1, ln2[...])
        hn2b = hn2.astype(BF)
        hn2_ref[...] = hn2b
        act, vjp_act = jax.vjp(lambda g, u: jax.nn.silu(g) * u, _nt(hn2b, wg[...]), _nt(hn2b, wu[...]))
        actb = act.astype(BF)
        act_ref[...] = actb
        h2 = h1 + _nn(actb, wd[...])
        row = i * TD + lax.broadcasted_iota(jnp.int32, (TD, 1), 0)
        t = jnp.where(row >= TP, row - TP, row)
        err = jnp.where((t >= N_META) & (t < T), h2 - tgt_ref[...], 0.0)
        loss_ref[...] += jnp.sum(err * err) * (0.5 / D)
        dh2b = (err * (1.0 / D)).astype(BF)
        dh2b_ref[...] = dh2b
        dg, du = vjp_act(_nt(dh2b, wd[...]))
        dgb, dub = dg.astype(BF), du.astype(BF)
        dg_ref[...] = dgb
        du_ref[...] = dub
        dh1n, dln2 = vjp_ln2(_nn(dgb, wg[...]) + _nn(dub, wu[...]))
        dh1 = err * (1.0 / D) + dh1n
        dh1_ref[...] = dh1
        dh1b = dh1.astype(BF)
        dh1b_ref[...] = dh1b
        dmix = _nt(dh1b, wout[...])
        do, dga = vjp_a(dmix[:, :D_ATTN])
        dyr, dgr = vjp_r(dmix[:, D_ATTN:])
        do_ref[...] = do
        dy_ref[...] = dyr
        dga_ref[...] += dga
        dgr_ref[...] += dgr
        dln2_ref[...] += dln2

    rs = lambda n: _row_spec(n, TD)
    acc = lambda n: pl.BlockSpec((1, n), lambda i: (0, 0))
    return pl.pallas_call(
        body, grid=(R // TD,), name="stage_d",
        in_specs=[rs(D), rs(D_ATTN), rs(D_RNN), rs(D), _const_spec((1, D_ATTN)), _const_spec((1, D_RNN)),
                  _const_spec((D, D)), _const_spec((1, D)), _const_spec((D_FF, D)), _const_spec((D_FF, D)),
                  _const_spec((D_FF, D))],
        out_specs=[rs(D_ATTN), rs(D_RNN), rs(D), rs(D), rs(D), rs(D), rs(D_FF), rs(D_FF), rs(D_FF), rs(D),
                   acc(1), acc(D_ATTN), acc(D_RNN), acc(D)],
        out_shape=[jax.ShapeDtypeStruct((R, D_ATTN), F32), jax.ShapeDtypeStruct((R, D_RNN), F32),
                   jax.ShapeDtypeStruct((R, D), F32), jax.ShapeDtypeStruct((R, D), BF),
                   jax.ShapeDtypeStruct((R, D), BF), jax.ShapeDtypeStruct((R, D), BF),
                   jax.ShapeDtypeStruct((R, D_FF), BF), jax.ShapeDtypeStruct((R, D_FF), BF),
                   jax.ShapeDtypeStruct((R, D_FF), BF), jax.ShapeDtypeStruct((R, D), BF),
                   jax.ShapeDtypeStruct((1, 1), F32), jax.ShapeDtypeStruct((1, D_ATTN), F32),
                   jax.ShapeDtypeStruct((1, D_RNN), F32), jax.ShapeDtypeStruct((1, D), F32)],
        compiler_params=_params("arbitrary", vmem=STAGE_D_VMEM),
    )(hp, o, y, tgt, cw["ga"], cw["gr"], cw["wout"], cw["ln2_g"], cw["wg"], cw["wu"], cw["wd"])


TW = 2176


def _wgrad(a, b, name, tk=None):
    ka, nb = a.shape[1], b.shape[1]
    tk = ka if tk is None else tk

    def body(a_ref, b_ref, o_ref):
        @pl.when(pl.program_id(1) == 0)
        def _():
            o_ref[...] = jnp.zeros_like(o_ref)

        o_ref[...] += _tn(a_ref[...].astype(BF), b_ref[...].astype(BF))

    return pl.pallas_call(
        body, grid=(ka // tk, R // TW), name=name,
        in_specs=[pl.BlockSpec((TW, tk), lambda k, r: (r, k)), pl.BlockSpec((TW, nb), lambda k, r: (r, 0))],
        out_specs=pl.BlockSpec((tk, nb), lambda k, r: (k, 0)),
        out_shape=jax.ShapeDtypeStruct((ka, nb), F32),
        compiler_params=_params("arbitrary", "arbitrary"),
    )(a, b)


def _wgrad_heads(dq, dk, dv, cqn, ckvn):
    def body(dq_ref, dk_ref, dv_ref, cqn_ref, ckvn_ref, oq_ref, ok_ref, ov_ref):
        @pl.when(pl.program_id(0) == 0)
        def _():
            for r in (oq_ref, ok_ref, ov_ref):
                r[...] = jnp.zeros_like(r)

        ckvnb = ckvn_ref[...]
        oq_ref[...] += _tn(dq_ref[...], cqn_ref[...])
        ok_ref[...] += _tn(dk_ref[...], ckvnb)
        ov_ref[...] += _tn(dv_ref[...].astype(BF), ckvnb)

    rows = lambda a: pl.BlockSpec((TW, a.shape[1]), lambda r: (r, 0))
    full = lambda m, n: pl.BlockSpec((m, n), lambda r: (0, 0))
    shapes = [(dq.shape[1], cqn.shape[1]), (dk.shape[1], ckvn.shape[1]), (dv.shape[1], ckvn.shape[1])]
    return pl.pallas_call(
        body, grid=(R // TW,), name="wgrad_heads", in_specs=[rows(a) for a in (dq, dk, dv, cqn, ckvn)],
        out_specs=[full(*s) for s in shapes], out_shape=[jax.ShapeDtypeStruct(s, F32) for s in shapes],
        compiler_params=_params("arbitrary"),
    )(dq, dk, dv, cqn, ckvn)


def _rope_tables():
    half = QK_ROPE // 2
    freqs = 1.0 / (ROPE_THETA ** (jnp.arange(half, dtype=F32) / half))
    ang = jnp.arange(TP, dtype=F32)[:, None] * freqs[None, :]
    ones = jnp.ones((TP, QK_NOPE), F32)
    zeros = jnp.zeros((TP, QK_NOPE), F32)
    pad1 = jnp.ones((TP, HP - QK_HEAD), F32)
    pad0 = jnp.zeros((TP, HP - QK_HEAD), F32)
    cs = jnp.concatenate([ones, jnp.cos(ang), jnp.cos(ang), pad1], axis=1)
    sn = jnp.concatenate([zeros, jnp.sin(ang), jnp.sin(ang), pad0], axis=1)
    return jnp.tile(cs, (NB, 1)), jnp.tile(sn, (NB, 1))


def _pad_rows(a, lo, hi):
    return jnp.pad(a, ((0, 0), (lo, hi), (0, 0)))


def _pad_target(target):
    return _pad_rows(target, N_META, TP - T).reshape(R, D)


def _compute_weights(w):
    win_t = w["w_in_t"]
    kr = win_t[O_KR:O_KR + QK_ROPE]
    win = jnp.concatenate([win_t[:O_KR], jnp.zeros((QK_NOPE, D), F32), kr,
                           jnp.zeros((HP - QK_HEAD, D), F32), win_t[O_KR + QK_ROPE:]], axis=0)
    wq = _pad_rows(w["w_uq_t"].reshape(N_HEADS, QK_HEAD, Q_LORA), 0, HP - QK_HEAD)
    wkv = w["w_ukv_t"].reshape(N_HEADS, QK_NOPE + V_HEAD, KV_LORA)
    wk = _pad_rows(wkv[:, :QK_NOPE], 0, HP - QK_NOPE)
    wv = wkv[:, QK_NOPE:].reshape(D_ATTN, KV_LORA)
    gates = jnp.stack([w["lru_wa"][0], w["lru_wi"][0], w["lru_wa"][1], w["lru_wi"][1]])
    blk = gates.reshape(4, N_CG, 2, RNN_BW, RNN_BW)
    dense = jnp.einsum("tcaij,ab->tcaibj", blk, jnp.eye(2, dtype=F32)).reshape(4, N_CG, CG, CG)
    w4 = dense.transpose(1, 2, 0, 3).reshape(N_CG, CG, 4 * CG)
    bias = jnp.stack([w["lru_ba"][0], w["lru_bi"][0], w["lru_ba"][1], w["lru_bi"][1]])
    b4 = bias.reshape(4, N_CG, CG).transpose(1, 0, 2).reshape(N_CG, 1, 4 * CG)
    lam = w["lru_lambda"].reshape(2, N_CG, CG).transpose(1, 0, 2).reshape(N_CG, 1, 2 * CG)
    pad_g = lambda g: jnp.pad(g.reshape(1, QK_HEAD), ((0, 0), (0, HP - QK_HEAD)))
    return dict(
        ln1_g=w["ln1_g"].reshape(1, D), win=win.astype(BF), qa_g=w["q_a_norm_g"].reshape(1, Q_LORA),
        wq=wq.astype(BF).reshape(N_HEADS * HP, Q_LORA), kva_g=w["kv_a_norm_g"].reshape(1, KV_LORA),
        wk=wk.astype(BF).reshape(N_HEADS * HP, KV_LORA), wv=wv.astype(BF),
        q_g=pad_g(w["q_norm_g"]), k_g=pad_g(w["k_norm_g"]),
        conv_w=w["conv_w"].reshape(CONV_W, D_RNN), conv_b=w["conv_b"].reshape(1, D_RNN),
        w4=w4.astype(BF), b4=b4, lam=lam,
        ga=w["attn_out_g"].reshape(1, D_ATTN), gr=w["rnn_out_g"].reshape(1, D_RNN), ln2_g=w["ln2_g"].reshape(1, D),
    )


def _local_step(x, target, meta, w, late_weights, early_grads, mid_grads):
    cw = _compute_weights(w)
    cs, sn = _rope_tables()
    hp = jnp.concatenate([jnp.broadcast_to(meta[None], (NB, N_META, D)), x,
                          jnp.zeros((NB, TP - T, D), F32)], axis=1).reshape(R, D)
    tgt = target if target.ndim == 2 else _pad_target(target)

    pa, xr, xg, q, k, v = _stage_a_fwd(hp, cs, sn, cw)
    o, lse = _attn_fwd(q, k, v)
    y, hf, hb = _rnn_fwd(xr, xg, cw)
    late = late_weights([o, y])
    cw.update(wout=late["w_out"], wg=late["w_gate_t"], wu=late["w_up_t"], wd=late["w_down"])
    (do, dy, dh1, mixb, dh1b, hn2b, dgb, dub, actb, dh2b, loss, dga, dgr, dln2) = _stage_d(hp, o, y, tgt, cw)
    dwout = _wgrad(mixb, dh1b, "wgrad_out")
    dwg = _wgrad(dgb, hn2b, "wgrad_gate", tk=D_FF // 2)
    dwu = _wgrad(dub, hn2b, "wgrad_up", tk=D_FF // 2)
    dwd = _wgrad(actb, dh2b, "wgrad_down", tk=D_FF // 2)
    zero = early_grads(dict(w_out=dwout, w_gate=dwg, w_up=dwu, w_down=dwd))
    cw["conv_b"] = cw["conv_b"] + zero
    dxr, dxg, dcw, dcb, dw4, db4, dlam = _rnn_bwd(dy, xr, xg, hf, hb, cw)
    zero = mid_grads([dxr])
    dq, dk, dv = _attn_bwd(q, k, v, o, lse, do)
    (dhp, dpb, dqrawb, dkrawb, hn1b, cqnb, ckvnb, dln1, dqag, dkvag, dqg, dkg) = _stage_a_bwd(
        dq, dk, dv, dxr, dxg, dh1, hp, pa, cs, sn, dict(cw, qa_g=cw["qa_g"] + zero))

    dwin = _wgrad(dpb, hn1b, "wgrad_in", tk=PC // 2)
    dwq, dwk, dwv = _wgrad_heads(dqrawb, dkrawb, dv, cqnb, ckvnb)

    dwin_t = jnp.concatenate([dwin[:O_KR], dwin[O_KR + QK_NOPE:O_KR + QK_HEAD], dwin[O_XR:]], axis=0)
    dwq_t = dwq.reshape(N_HEADS, HP, Q_LORA)[:, :QK_HEAD].reshape(N_HEADS * QK_HEAD, Q_LORA)
    dwkv_t = jnp.concatenate([dwk.reshape(N_HEADS, HP, KV_LORA)[:, :QK_NOPE],
                              dwv.reshape(N_HEADS, V_HEAD, KV_LORA)], axis=1).reshape(2 * D_ATTN, KV_LORA)
    d4 = dw4.reshape(N_CG, 2, RNN_BW, 4, 2, RNN_BW)
    dgates = jnp.stack([d4[:, 0, :, :, 0, :], d4[:, 1, :, :, 1, :]], axis=1)
    dgates = dgates.transpose(3, 0, 1, 2, 4).reshape(4, N_HEADS, RNN_BW, RNN_BW)
    dbias = db4.reshape(N_CG, 4, CG).transpose(1, 0, 2).reshape(4, D_RNN)
    dhp3 = dhp.reshape(NB, TP, D)
    grads = dict(
        meta_tokens=jnp.sum(dhp3[:, :N_META], axis=0),
        ln1_g=dln1, w_in_t=dwin_t, q_a_norm_g=dqag, w_uq_t=dwq_t, kv_a_norm_g=dkvag, w_ukv_t=dwkv_t,
        q_norm_g=dqg[:, :QK_HEAD], k_norm_g=dkg[:, :QK_HEAD], conv_w=dcw[None], conv_b=dcb,
        lru_wa=jnp.stack([dgates[0], dgates[2]])[None], lru_ba=jnp.stack([dbias[0], dbias[2]])[None],
        lru_wi=jnp.stack([dgates[1], dgates[3]])[None], lru_bi=jnp.stack([dbias[1], dbias[3]])[None],
        lru_lambda=dlam.reshape(N_CG, 2, CG).transpose(1, 0, 2).reshape(1, 2, D_RNN),
        attn_out_g=dga, rnn_out_g=dgr, ln2_g=dln2,
    )
    return loss[0, 0], dhp3[:, N_META:T], grads, [dhp, dwin]


_ANY = pl.BlockSpec(memory_space=pl.ANY)


def _place():
    return lax.axis_index("x"), lax.axis_index("y"), lax.axis_index("c")


def _other_chips(x, y):
    return [(1 - x, y), (x, 1 - y), (1 - x, 1 - y)]


def _pair_exchange(big, whole, name):
    n_s, _, m, n = big.shape
    n_copies = n_s + len(whole)

    def body(*refs):
        big_ref, whole_refs = refs[0], refs[1:1 + len(whole)]
        rbig_ref, rwhole_refs = refs[1 + len(whole)], refs[2 + len(whole):2 + 2 * len(whole)]
        send_sems, recv_sems = refs[-2:]
        x, y, c = _place()
        sibling = (x, y, 1 - c)
        copies = [pltpu.make_async_remote_copy(
            src_ref=big_ref.at[s, 1 - c], dst_ref=rbig_ref.at[s], send_sem=send_sems.at[s], recv_sem=recv_sems.at[s],
            device_id=sibling, device_id_type=MESH) for s in range(n_s)]
        copies += [pltpu.make_async_remote_copy(
            src_ref=a, dst_ref=r, send_sem=send_sems.at[n_s + i], recv_sem=recv_sems.at[n_s + i],
            device_id=sibling, device_id_type=MESH) for i, (a, r) in enumerate(zip(whole_refs, rwhole_refs))]
        for cp in copies:
            cp.start()
        for cp in copies:
            cp.wait()

    return pl.pallas_call(
        body, name=name,
        out_shape=[jax.ShapeDtypeStruct((n_s, m, n), big.dtype)] + [jax.ShapeDtypeStruct(a.shape, a.dtype) for a in whole],
        in_specs=[_ANY] * (1 + len(whole)), out_specs=[_ANY] * (1 + len(whole)),
        scratch_shapes=[pltpu.SemaphoreType.DMA((n_copies,)), pltpu.SemaphoreType.DMA((n_copies,))],
    )(big, *whole)


_HBM = pl.BlockSpec(memory_space=pltpu.HBM)
_SEM = pl.BlockSpec(memory_space=pltpu.SEMAPHORE)
_EFFECT = pltpu.SideEffectType.DATAFLOW_SIDE_EFFECTING


def _split_copies(src_refs, land_refs, sems, plan, sending):
    n = len(sems) // 2
    return [pltpu.make_async_remote_copy(src_ref=s, dst_ref=d, send_sem=sems[k], recv_sem=sems[n + k], device_id=to,
                                         device_id_type=MESH)
            for k, (s, d, to) in enumerate(plan(src_refs, land_refs, sending))]


def _to_chips(src_at, land_at):
    def plan(src_refs, land_refs, sending):
        x, y, c = _place()
        return [(src_at(s, tx, ty, c), land_at(l, j, *((x, y) if sending else (tx, ty)), c), (tx, ty, c))
                for s, l in zip(src_refs, land_refs) for j, (tx, ty) in enumerate(_other_chips(x, y))]
    return plan


def _to_sibling(src_refs, land_refs, sending):
    x, y, c = _place()
    return [(s.at[k, 1 - c], l.at[k], (x, y, 1 - c)) for s, l in zip(src_refs, land_refs) for k in range(N_CHIPS)]


def _split_start(name, srcs, lands, plan, n, after=()):
    srcs, lands, after = list(srcs), list(lands), list(after)
    k = len(srcs)

    def body(*refs):
        outs = refs[2 * k + len(after):]
        for cp in _split_copies(refs[:k], refs[k:2 * k], outs[:2 * n], plan, True):
            cp.start()
        outs[2 * n + 2 * k][...] = jnp.zeros_like(outs[2 * n + 2 * k])

    outs = pl.pallas_call(
        body, name=name,
        out_shape=(pltpu.SemaphoreType.DMA(()),) * (2 * n) + tuple(pltpu.HBM(a.shape, a.dtype) for a in srcs + lands)
        + (jax.ShapeDtypeStruct((8, LANES), F32),),
        in_specs=(_HBM,) * (2 * k) + (_ANY,) * len(after),
        out_specs=(_SEM,) * (2 * n) + (_HBM,) * (2 * k) + (pl.BlockSpec(memory_space=pltpu.VMEM),),
        input_output_aliases={i: 2 * n + i for i in range(2 * k)},
        compiler_params=pltpu.CompilerParams(has_side_effects=_EFFECT),
    )(*[pltpu.with_memory_space_constraint(a, pltpu.HBM) for a in srcs + lands], *after)
    return outs[:2 * n], list(outs[2 * n:2 * n + k]), list(outs[2 * n + k:2 * n + 2 * k]), outs[2 * n + 2 * k]


def _split_wait(name, sems, srcs, lands, after, plan):
    srcs, lands = list(srcs), list(lands)
    k = len(srcs)

    def body(*refs):
        for cp in _split_copies(refs[:k], refs[k:2 * k], refs[2 * k:2 * k + len(sems)], plan, False):
            cp.wait_send()
            cp.wait_recv()

    outs = pl.pallas_call(
        body, name=name, out_shape=tuple(pltpu.HBM(a.shape, a.dtype) for a in srcs + lands),
        in_specs=(_HBM,) * (2 * k) + (_SEM,) * len(sems) + (_ANY,) * len(after), out_specs=(_HBM,) * (2 * k),
        input_output_aliases={i: i for i in range(2 * k)}, compiler_params=pltpu.CompilerParams(has_side_effects=_EFFECT),
    )(*srcs, *lands, *sems, *after)
    return list(outs[:k]), list(outs[k:])


def _gather_finish(lands, pieces, name):
    k = len(lands)

    def body(*refs):
        land_refs, piece_refs, out_refs, stages = refs[:k], refs[k:2 * k], refs[2 * k:3 * k], refs[3 * k:4 * k]
        send_sems, recv_sems, load_sems, store_sems = refs[4 * k:]
        x, y, c = _place()
        sibling = (x, y, 1 - c)
        remote, loads, stores, arrivals = [], [], [], []
        for a in range(k):
            m = lands[a].shape[0] // 8

            def rows(px, py, pc, ref, m=m):
                return ref.at[pl.ds((4 * px + 2 * py + pc) * m, m), :]

            for j, (tx, ty) in enumerate(_other_chips(x, y)):
                sems = dict(send_sem=send_sems.at[3 * a + j], recv_sem=recv_sems.at[3 * a + j], device_id=sibling,
                            device_id_type=MESH)
                remote.append(pltpu.make_async_remote_copy(
                    src_ref=rows(tx, ty, c, land_refs[a]), dst_ref=rows(tx, ty, c, out_refs[a]), **sems))
                arrivals.append(pltpu.make_async_remote_copy(
                    src_ref=rows(tx, ty, 1 - c, out_refs[a]), dst_ref=rows(tx, ty, 1 - c, out_refs[a]), **sems))
            for h in range(2):
                loads.append(pltpu.make_async_copy(piece_refs[a].at[pl.ds(h * m, m), :], stages[a].at[h],
                                                   load_sems.at[2 * a + h]))
                stores.append(pltpu.make_async_copy(stages[a].at[h], rows(x, y, h, out_refs[a]), store_sems.at[2 * a + h]))
        for cp in remote + loads:
            cp.start()
        for ld, st in zip(loads, stores):
            ld.wait()
            st.start()
        for cp, arrival in zip(remote, arrivals):
            cp.wait_send()
            arrival.wait_recv()
        for cp in stores:
            cp.wait()

    return pl.pallas_call(
        body, name=name, out_shape=[jax.ShapeDtypeStruct(a.shape, a.dtype) for a in lands],
        in_specs=[_ANY] * (2 * k), out_specs=[_ANY] * k, input_output_aliases={i: i for i in range(k)},
        scratch_shapes=[pltpu.VMEM((2, a.shape[0] // 8, a.shape[1]), a.dtype) for a in lands]
        + [pltpu.SemaphoreType.DMA((3 * k,)), pltpu.SemaphoreType.DMA((3 * k,)), pltpu.SemaphoreType.DMA((2 * k,)),
           pltpu.SemaphoreType.DMA((2 * k,))],
    )(*lands, *pieces)


def _pair_fill(bufs, name):
    k = len(bufs)

    def body(*refs):
        send_sems, recv_sems = refs[-2:]
        x, y, c = _place()
        copies = [pltpu.make_async_remote_copy(
            src_ref=refs[i].at[c], dst_ref=refs[k + i].at[c], send_sem=send_sems.at[i], recv_sem=recv_sems.at[i],
            device_id=(x, y, 1 - c), device_id_type=MESH) for i in range(k)]
        for cp in copies:
            cp.start()
        for i, cp in enumerate(copies):
            cp.wait_send()
            pltpu.make_async_remote_copy(
                src_ref=refs[i].at[1 - c], dst_ref=refs[k + i].at[1 - c], send_sem=send_sems.at[i],
                recv_sem=recv_sems.at[i], device_id=(x, y, 1 - c), device_id_type=MESH).wait_recv()

    return pl.pallas_call(
        body, name=name, out_shape=[jax.ShapeDtypeStruct(a.shape, a.dtype) for a in bufs], in_specs=[_ANY] * k,
        out_specs=[_ANY] * k, input_output_aliases={i: i for i in range(k)},
        scratch_shapes=[pltpu.SemaphoreType.DMA((k,)), pltpu.SemaphoreType.DMA((k,))],
    )(*bufs)


def _row_tile(rows, cap=512):
    for t in range(cap - cap % 8, 7, -8):
        if rows % t == 0:
            return t
    return rows


def _elementwise(fn, n_out, name, *arrs, out_dtype=F32):
    rows, cols = arrs[0].shape
    tr = _row_tile(rows)
    n_in = len(arrs)

    def body(*refs):
        outs = fn(*[r[...].astype(F32) for r in refs[:n_in]])
        for r, o in zip(refs[n_in:], outs):
            r[...] = o.astype(out_dtype)

    spec = pl.BlockSpec((tr, cols), lambda i: (i, 0))
    return pl.pallas_call(
        body, grid=(rows // tr,), name=name, in_specs=[spec] * n_in, out_specs=[spec] * n_out,
        out_shape=[jax.ShapeDtypeStruct((rows, cols), out_dtype)] * n_out, compiler_params=_params("arbitrary"),
    )(*arrs)


def _pair_sums(gpacks, rbigs, ci, name):
    k = len(gpacks)

    def body(c_ref, *refs):
        for g_ref, r_ref, o_ref in zip(refs[:k], refs[k:2 * k], refs[2 * k:]):
            o_ref[...] = (g_ref[...] + r_ref[...]).astype(BF)

    half = lambda a: pl.BlockSpec((None,) + a.shape[1:], lambda s, c: (s, 0, 0))
    return pl.pallas_call(
        body, name=name, out_shape=[jax.ShapeDtypeStruct(r.shape, BF) for r in rbigs],
        grid_spec=pltpu.PrefetchScalarGridSpec(
            num_scalar_prefetch=1, grid=(N_CHIPS,),
            in_specs=[pl.BlockSpec((None, None) + g.shape[2:], lambda s, c: (s, c[0], 0, 0)) for g in gpacks]
            + [half(r) for r in rbigs],
            out_specs=[half(r) for r in rbigs]),
        compiler_params=_params("arbitrary"),
    )(ci.reshape(1), *gpacks, *rbigs)


def _chip_sums(sums, landed, chip, ci, name):
    k = len(sums)

    def body(p_ref, *refs):
        for own_ref, land_ref, o_ref in zip(refs[:k], refs[k:2 * k], refs[2 * k:]):
            f = lambda v: v.astype(F32)
            o_ref[...] = _add4(f(own_ref[...]), f(land_ref[0]), f(land_ref[1]), f(land_ref[2]))[0]

    return pl.pallas_call(
        body, name=name, out_shape=[jax.ShapeDtypeStruct((2,) + s.shape[1:], F32) for s in sums],
        grid_spec=pltpu.PrefetchScalarGridSpec(
            num_scalar_prefetch=1, grid=(1,),
            in_specs=[pl.BlockSpec((None,) + s.shape[1:], lambda i, p: (p[0], 0, 0)) for s in sums]
            + [pl.BlockSpec(l.shape, lambda i, p: (0, 0, 0)) for l in landed],
            out_specs=[pl.BlockSpec((None,) + s.shape[1:], lambda i, p: (p[1], 0, 0)) for s in sums]),
        compiler_params=_params("arbitrary"),
    )(jnp.stack([chip, ci]), *sums, *landed)


def _add2(a, b):
    return (a + b,)


def _add4(own, r0, r1, r2):
    return ((own + r2) + (r0 + r1),)


def _adamw_small(ws, gs, ms, vs):
    k = len(ws)

    def body(*refs):
        for i in range(k):
            outs = _adamw_math(*[refs[j * k + i][...] for j in range(4)])
            for j, o in enumerate(outs):
                refs[(4 + j) * k + i][...] = o

    return pl.pallas_call(
        body, name="adamw_small", out_shape=[jax.ShapeDtypeStruct(w.shape, F32) for w in ws] * 3,
    )(*ws, *gs, *ms, *vs)


def _adamw_math(w, g, m, v):
    m = ADAM_B1 * m + (1.0 - ADAM_B1) * g
    v = ADAM_B2 * v + (1.0 - ADAM_B2) * (g * g)
    m_hat = m / (1.0 - ADAM_B1 ** ADAM_STEP)
    v_hat = v / (1.0 - ADAM_B2 ** ADAM_STEP)
    delta = -ADAM_LR * (m_hat / (jnp.sqrt(v_hat) + ADAM_EPS) + ADAM_WD * w)
    return delta, m, v


WEIGHTS = ["meta_tokens", "ln1_g", "w_in", "q_a_norm_g", "w_uq", "kv_a_norm_g", "w_ukv", "q_norm_g", "k_norm_g",
           "conv_w", "conv_b", "lru_wa", "lru_ba", "lru_wi", "lru_bi", "lru_lambda", "attn_out_g", "rnn_out_g",
           "w_out", "ln2_g", "w_gate", "w_up", "w_down"]
BIG = ["w_in", "w_uq", "w_ukv", "w_out", "w_gate", "w_up", "w_down"]
BIG_T = {"w_in": True, "w_uq": True, "w_ukv": True, "w_out": False, "w_gate": True, "w_up": True, "w_down": False}
BIG_ROWS = {"w_in": 424, "w_uq": 72, "w_ukv": 64, "w_out": 256, "w_gate": 704, "w_up": 704, "w_down": 704}
EARLY = ["w_in", "w_uq", "w_ukv"]
LATE = ["w_out", "w_gate", "w_up", "w_down"]
EARLY_ROWS = 576
SMALL_SHARDED = ["meta_tokens", "conv_w", "lru_ba", "lru_bi", "lru_lambda"]
SMALL = [n for n in WEIGHTS if n not in BIG]
SMALL_PACK_ROWS = 160


def _offsets(names):
    off, o = {}, 0
    for n in names:
        off[n] = o
        o += BIG_ROWS[n]
    return off


def _shard_pack(names, src, rows):
    parts = [_to_pack_piece(n, src[n]) for n in names]
    used = sum(BIG_ROWS[n] for n in names)
    if rows > used:
        parts.append(jnp.zeros((rows - used, D), F32))
    return jnp.concatenate(parts, axis=0)


def _grad_pack(names, g, rows):
    parts = [g[n].reshape(N_CHIPS, BIG_ROWS[n], D) for n in names]
    used = sum(BIG_ROWS[n] for n in names)
    if rows > used:
        parts.append(jnp.zeros((N_CHIPS, rows - used, D), F32))
    return jnp.concatenate(parts, axis=1).reshape(N_CHIPS, 2, rows // 2, D)


def _to_pack_piece(name, shard):
    a = shard[0].T if BIG_T[name] else shard[0]
    return a.reshape(BIG_ROWS[name], D)


def _flat_pack(arrs, rows):
    flat = jnp.concatenate([a.reshape(-1) for a in arrs])
    return jnp.pad(flat, (0, rows * D - flat.shape[0])).reshape(rows, D)


def _flat_unpack(pack, shapes):
    flat, out, o = pack.reshape(-1), [], 0
    for s in shapes:
        n = math.prod(s)
        out.append(flat[o:o + n].reshape(s))
        o += n
    return out


def kernel(x, meta_tokens, ln1_g, w_in, q_a_norm_g, w_uq, kv_a_norm_g, w_ukv, q_norm_g, k_norm_g, conv_w, conv_b, lru_wa, lru_ba, lru_wi, lru_bi, lru_lambda, attn_out_g, rnn_out_g, w_out, ln2_g, w_gate, w_up, w_down, loss_target, m_meta_tokens, m_ln1_g, m_w_in, m_q_a_norm_g, m_w_uq, m_kv_a_norm_g, m_w_ukv, m_q_norm_g, m_k_norm_g, m_conv_w, m_conv_b, m_lru_wa, m_lru_ba, m_lru_wi, m_lru_bi, m_lru_lambda, m_attn_out_g, m_rnn_out_g, m_w_out, m_ln2_g, m_w_gate, m_w_up, m_w_down, v_meta_tokens, v_ln1_g, v_w_in, v_q_a_norm_g, v_w_uq, v_kv_a_norm_g, v_w_ukv, v_q_norm_g, v_k_norm_g, v_conv_w, v_conv_b, v_lru_wa, v_lru_ba, v_lru_wi, v_lru_bi, v_lru_lambda, v_attn_out_g, v_rnn_out_g, v_w_out, v_ln2_g, v_w_gate, v_w_up, v_w_down):
    wts = dict(zip(WEIGHTS, (meta_tokens, ln1_g, w_in, q_a_norm_g, w_uq, kv_a_norm_g, w_ukv, q_norm_g, k_norm_g, conv_w, conv_b, lru_wa, lru_ba, lru_wi, lru_bi, lru_lambda, attn_out_g, rnn_out_g, w_out, ln2_g, w_gate, w_up, w_down)))
    mom = dict(zip(WEIGHTS, (m_meta_tokens, m_ln1_g, m_w_in, m_q_a_norm_g, m_w_uq, m_kv_a_norm_g, m_w_ukv, m_q_norm_g, m_k_norm_g, m_conv_w, m_conv_b, m_lru_wa, m_lru_ba, m_lru_wi, m_lru_bi, m_lru_lambda, m_attn_out_g, m_rnn_out_g, m_w_out, m_ln2_g, m_w_gate, m_w_up, m_w_down)))
    var = dict(zip(WEIGHTS, (v_meta_tokens, v_ln1_g, v_w_in, v_q_a_norm_g, v_w_uq, v_kv_a_norm_g, v_w_ukv, v_q_norm_g, v_k_norm_g, v_conv_w, v_conv_b, v_lru_wa, v_lru_ba, v_lru_wi, v_lru_bi, v_lru_lambda, v_attn_out_g, v_rnn_out_g, v_w_out, v_ln2_g, v_w_gate, v_w_up, v_w_down)))
    xi, yi, ci = _place()
    chip = 2 * xi + yi
    off_e = _offsets(EARLY)
    half_e = EARLY_ROWS // 2
    gather_plan = _to_chips(lambda ref, tx, ty, c: ref.at[pl.ds(c * (ref.shape[0] // 2), ref.shape[0] // 2), :],
                            lambda ref, j, px, py, c: ref.at[pl.ds((4 * px + 2 * py + c) * (ref.shape[0] // 8),
                                                                   ref.shape[0] // 8), :])
    scatter_plan = _to_chips(lambda ref, tx, ty, c: ref.at[2 * tx + ty], lambda ref, j, px, py, c: ref.at[j])
    everywhere = _to_chips(lambda ref, tx, ty, c: ref, lambda ref, j, px, py, c: ref.at[j])
    n_late = len(LATE)

    pack_e = _shard_pack(EARLY, wts, EARLY_ROWS).astype(BF)
    spack = jnp.concatenate([meta_tokens[:, :LANES], meta_tokens[:, LANES:], conv_w[0], lru_ba[0], lru_bi[0],
                             lru_lambda[0], jnp.zeros((6, LANES), F32)], axis=0)
    sems_g, src_g, land_g, _ = _split_start(
        "gather_early_start", [pack_e, spack], [lax.empty((N_CHIPS * EARLY_ROWS, D), BF), lax.empty((N_CHIPS * 48, LANES), F32)],
        gather_plan, 6)
    tgt_padded = _pad_target(loss_target)
    pieces_l = [_to_pack_piece(n, wts[n]).astype(BF) for n in LATE]
    src_g, land_g = _split_wait("gather_early_wait", sems_g, src_g, land_g, [tgt_padded] + pieces_l, gather_plan)
    ge, gs = _gather_finish(land_g, src_g, "gather_early_finish")
    ge = ge.reshape(N_CHIPS, EARLY_ROWS, D)
    gs = gs.reshape(N_CHIPS, 48, LANES)
    full = {n: ge[:, off_e[n]:off_e[n] + BIG_ROWS[n]] for n in EARLY}
    sems_l, src_l, land_l, tied = _split_start(
        "gather_late_start", pieces_l, [lax.empty((N_CHIPS * BIG_ROWS[n], D), BF) for n in LATE], gather_plan, 3 * n_late,
        after=[ge])

    def late_weights(after):
        pieces, lands = _split_wait("gather_late_wait", sems_l, src_l, land_l, after, gather_plan)
        w_out_, w_gate_, w_up_, w_down_ = _gather_finish(lands, pieces, "gather_late_finish")
        return dict(w_out=w_out_, w_gate_t=w_gate_, w_up_t=w_up_, w_down=w_down_)

    pair, late = {}, {}

    def early_grads(g_late):
        halves = [g_late[n].reshape(N_CHIPS, 2, BIG_ROWS[n] // 2, D) for n in LATE]
        pair["sems"], pair["src"], pair["land"], zeros = _split_start(
            "grad_pair_late_start", halves, [lax.empty((N_CHIPS, BIG_ROWS[n] // 2, D), F32) for n in LATE], _to_sibling,
            N_CHIPS * n_late)
        return zeros[0, 0]

    def mid_grads(after):
        halves, landed = _split_wait("grad_pair_late_wait", pair["sems"], pair["src"], pair["land"], after, _to_sibling)
        chip_sums = _pair_sums(halves, landed, ci, "grad_pair_sum_late")
        late["sems"], late["src"], late["land"], zeros = _split_start(
            "grad_chip_late_start", chip_sums, [lax.empty((3, BIG_ROWS[n] // 2, D), BF) for n in LATE], scatter_plan,
            3 * n_late)
        return zeros[0, 0]

    cols = lambda a: a.transpose(1, 0, 2).reshape(a.shape[1], N_CHIPS * a.shape[2])
    meta_full = cols(jnp.concatenate([gs[:, 0:16], gs[:, 16:32]], axis=2))
    w = dict(
        w_in_t=full["w_in"].reshape(IN_COLS, D), w_uq_t=full["w_uq"].reshape(N_HEADS * QK_HEAD, Q_LORA),
        w_ukv_t=full["w_ukv"].reshape(2 * D_ATTN, KV_LORA),
        ln1_g=ln1_g, q_a_norm_g=q_a_norm_g, kv_a_norm_g=kv_a_norm_g, q_norm_g=q_norm_g, k_norm_g=k_norm_g,
        conv_w=cols(gs[:, 32:36]), conv_b=conv_b, lru_wa=lru_wa[0], lru_ba=cols(gs[:, 36:38]), lru_wi=lru_wi[0],
        lru_bi=cols(gs[:, 38:40]), lru_lambda=cols(gs[:, 40:42]), attn_out_g=attn_out_g, rnn_out_g=rnn_out_g,
        ln2_g=ln2_g,
    )

    loss_local, grad_x, g, last = _local_step(x, tgt_padded, meta_full + tied[0, 0], w, late_weights, early_grads,
                                              mid_grads)

    gpack = _grad_pack(EARLY, {"w_in": g["w_in_t"], "w_uq": g["w_uq_t"], "w_ukv": g["w_ukv_t"]}, EARLY_ROWS)
    full_shapes = {n: wts[n].shape for n in SMALL}
    full_shapes.update(meta_tokens=(N_META, D), conv_w=(1, CONV_W, D_RNN), lru_ba=(1, 2, D_RNN), lru_bi=(1, 2, D_RNN),
                       lru_lambda=(1, 2, D_RNN))
    gsmall = _flat_pack([g[n] for n in SMALL] + [loss_local], SMALL_PACK_ROWS)
    rbig, rsmall = _pair_exchange(gpack, [gsmall], "grad_pair_exchange")
    chip_big = _pair_sums([gpack], [rbig], ci, "grad_pair_sum")
    (chip_small,) = _elementwise(_add2, 1, "grad_pair_sum_small", gsmall, rsmall)
    sems_e, src_e, land_e, zero_e = _split_start(
        "grad_chip_early_start", chip_big, [lax.empty((3, half_e, D), BF)], scatter_plan, 3)
    sems_s, src_s, land_s, zero_s = _split_start(
        "grad_small_start", [chip_small], [lax.empty((3, SMALL_PACK_ROWS, D), F32)], everywhere, 3)

    grads, delta, new_m, new_v = {}, {}, {}, {}

    def adamw_big(n, gshard):
        _, k, cols = wts[n].shape
        as_rows = (lambda a: a[0].T) if BIG_T[n] else (lambda a: a[0])
        back = (lambda a: a.T[None]) if BIG_T[n] else (lambda a: a[None])
        g2 = gshard.reshape((cols, k) if BIG_T[n] else (k, cols))
        d_, m_, v_ = _elementwise(_adamw_math, 3, "adamw_" + n, as_rows(wts[n]), g2, as_rows(mom[n]), as_rows(var[n]))
        grads[n], delta[n], new_m[n], new_v[n] = back(g2), back(d_), back(m_), back(v_)
        return d_

    sums, landed = _split_wait("grad_chip_late_wait", late["sems"], late["src"], late["land"], last + [zero_e, zero_s],
                               scatter_plan)
    shards_l = _pair_fill(_chip_sums(sums, landed, chip, ci, "grad_chip_sum_late"), "grad_pair_fill_late")
    done_late = [adamw_big(n, buf) for n, buf in zip(LATE, shards_l)][-1]
    src_e, land_e = _split_wait("grad_chip_early_wait", sems_e, src_e, land_e, [done_late], scatter_plan)
    src_s, land_s = _split_wait("grad_small_wait", sems_s, src_s, land_s, [done_late], everywhere)
    (shard_e,) = _pair_fill(_chip_sums(src_e, land_e, chip, ci, "grad_chip_sum"), "grad_pair_fill_early")
    shard_e = shard_e.reshape(EARLY_ROWS, D)
    for n in EARLY:
        adamw_big(n, shard_e[off_e[n]:off_e[n] + BIG_ROWS[n]])
    (small_sum,) = _elementwise(_add4, 1, "grad_chip_sum_small", src_s[0], land_s[0][0], land_s[0][1], land_s[0][2])
    *small_grads, loss = _flat_unpack(small_sum, [full_shapes[n] for n in SMALL] + [()])
    small_full = dict(zip(SMALL, small_grads))
    for n in SMALL:
        a = small_full[n]
        if n in SMALL_SHARDED:
            width = wts[n].shape[-1]
            a = lax.dynamic_slice_in_dim(a, chip * width, width, axis=a.ndim - 1)
        grads[n] = a.reshape(wts[n].shape)

    rows_of = lambda a: a.reshape(-1, a.shape[-1])
    outs = _adamw_small(*[[rows_of(src[n]) for n in SMALL] for src in (wts, grads, mom, var)])
    for j, dst in enumerate((delta, new_m, new_v)):
        dst.update({n: outs[j * len(SMALL) + i].reshape(wts[n].shape) for i, n in enumerate(SMALL)})

    return (loss, grad_x, *[grads[n] for n in WEIGHTS], *[delta[n] for n in WEIGHTS],
            *[new_m[n] for n in WEIGHTS], *[new_v[n] for n in WEIGHTS])
```

```python
import functools
import math

import jax
import jax.numpy as jnp
from jax import lax
from jax.experimental import pallas as pl
from jax.experimental.pallas import tpu as pltpu

F32 = jnp.float32
BF = jnp.bfloat16
MESH = pl.DeviceIdType.MESH

D = 1024
SEQ = 2048
N_META = 16
T = N_META + SEQ
N_HEADS = 8
QK_NOPE = 64
QK_ROPE = 32
QK_HEAD = 96
V_HEAD = 64
Q_LORA = 384
KV_LORA = 256
D_ATTN = 512
D_RNN = 512
RNN_BW = 64
CONV_W = 4
LRU_C = 8.0
ROPE_THETA = 10000.0
D_FF = 2816
EPS = 1e-6
IN_COLS = 1696
ADAM_LR, ADAM_B1, ADAM_B2, ADAM_EPS, ADAM_WD, ADAM_STEP = 0.001, 0.9, 0.999, 1e-08, 0.01, 10

LANES = 128
TP = 2176
NB = 2
R = NB * TP
TR = 256
TRF = 256
TQ = 1088
HP = LANES
PC = 1792
O_CKV, O_KR, O_XR, O_XG = 384, 640, 768, 1280
CG = 128
N_CG = D_RNN // CG
VMEM_LIMIT = 56 * 1024 * 1024
N_CHIPS = 4
SCALE = QK_HEAD ** -0.5
KEY_MASK = -30000.0
LOG2_E = 1.4426950408889634
SCALE_LOG2 = SCALE * LOG2_E


def _nt(a, b):
    return lax.dot_general(a, b, (((1,), (1,)), ((), ())), preferred_element_type=F32)


def _nn(a, b):
    return jnp.dot(a, b, preferred_element_type=F32)


def _tn(a, b):
    return lax.dot_general(a, b, (((0,), (0,)), ((), ())), preferred_element_type=F32)


def _rms(x, g, n):
    ms = jnp.sum(x * x, axis=-1, keepdims=True) * (1.0 / n)
    return x * lax.rsqrt(ms + EPS) * g


def _lane_sum(y):
    return jnp.sum(y, axis=-1, keepdims=True)


def _rot(x):
    lane = lax.broadcasted_iota(jnp.int32, x.shape, 1)
    left = pltpu.roll(x, HP - 16, 1)
    right = pltpu.roll(x, 16, 1)
    lo = (lane >= QK_NOPE) & (lane < QK_NOPE + 16)
    hi = (lane >= QK_NOPE + 16) & (lane < QK_HEAD)
    return jnp.where(lo, -left, jnp.where(hi, right, 0.0))


def _head(x, g, cs, sn):
    n = x * lax.rsqrt(_lane_sum(x * x) * (1.0 / QK_HEAD) + EPS) * g
    return n * cs + _rot(n) * sn


def _head_bwd(x, g, cs, sn, dout):
    rs = lax.rsqrt(_lane_sum(x * x) * (1.0 / QK_HEAD) + EPS)
    xh = x * rs
    dn = dout * cs - _rot(dout * sn)
    gdn = g * dn
    t = _lane_sum(gdn * xh) * (1.0 / QK_HEAD)
    return rs * (gdn - xh * t), jnp.sum(dn * xh, axis=0, keepdims=True)


def _const_spec(shape):
    return pl.BlockSpec(shape, lambda *_: (0,) * len(shape), pipeline_mode=pl.Buffered(1))


def _row_spec(n, tr=TR):
    return pl.BlockSpec((tr, n), lambda i: (i, 0))


def _params(*sem, vmem=VMEM_LIMIT):
    return pltpu.CompilerParams(dimension_semantics=sem, vmem_limit_bytes=vmem)


def _stage_a_fwd(hp, cs, sn, cw):
    def body(hp_ref, cs_ref, sn_ref, ln1, win, qag, wq, kvag, wk, wv, qg, kg,
             pa_ref, xr_ref, xg_ref, q_ref, k_ref, v_ref):
        hn = _rms(hp_ref[...], ln1[...], D).astype(BF)
        p = _nt(hn, win[...])
        pa_ref[...] = p[:, :O_XR]
        xr_ref[...] = p[:, O_XR:O_XG]
        xg_ref[...] = p[:, O_XG:]
        cqn = _rms(p[:, :O_CKV], qag[...], Q_LORA).astype(BF)
        ckvn = _rms(p[:, O_CKV:O_KR], kvag[...], KV_LORA).astype(BF)
        kr = p[:, O_KR:O_XR]
        c, s = cs_ref[...], sn_ref[...]
        mask_lane = lax.broadcasted_iota(jnp.int32, (1, HP), 1) == QK_HEAD
        row = pl.program_id(0) * TRF + lax.broadcasted_iota(jnp.int32, (TRF, 1), 0)
        key_mask = jnp.where(jnp.where(row >= TP, row - TP, row) < T, 0.0, KEY_MASK)
        qraw = _nt(cqn, wq[...])
        kraw = _nt(ckvn, wk[...])
        for h in range(N_HEADS):
            sl = slice(h * HP, (h + 1) * HP)
            q_ref[:, sl] = jnp.where(mask_lane, 1.0, _head(qraw[:, sl], qg[...], c, s)).astype(BF)
            k_ref[:, sl] = jnp.where(mask_lane, key_mask, _head(kraw[:, sl] + kr, kg[...], c, s)).astype(BF)
        v_ref[...] = _nt(ckvn, wv[...]).astype(BF)

    rs = lambda n: _row_spec(n, TRF)
    return pl.pallas_call(
        body, grid=(R // TRF,), name="stage_a_fwd",
        in_specs=[rs(D), rs(HP), rs(HP), _const_spec((1, D)), _const_spec((PC, D)),
                  _const_spec((1, Q_LORA)), _const_spec((N_HEADS * HP, Q_LORA)), _const_spec((1, KV_LORA)),
                  _const_spec((N_HEADS * HP, KV_LORA)), _const_spec((D_ATTN, KV_LORA)), _const_spec((1, HP)),
                  _const_spec((1, HP))],
        out_specs=[rs(O_XR), rs(D_RNN), rs(D_RNN), rs(N_HEADS * HP), rs(N_HEADS * HP), rs(D_ATTN)],
        out_shape=[jax.ShapeDtypeStruct((R, O_XR), F32), jax.ShapeDtypeStruct((R, D_RNN), F32),
                   jax.ShapeDtypeStruct((R, D_RNN), F32), jax.ShapeDtypeStruct((R, N_HEADS * HP), BF),
                   jax.ShapeDtypeStruct((R, N_HEADS * HP), BF), jax.ShapeDtypeStruct((R, D_ATTN), BF)],
        compiler_params=_params("arbitrary"),
    )(hp, cs, sn, cw["ln1_g"], cw["win"], cw["qa_g"], cw["wq"], cw["kva_g"], cw["wk"], cw["wv"], cw["q_g"], cw["k_g"])


def _stage_a_bwd(dq, dk, dv, dxr, dxg, dh1, hp, pa, cs, sn, cw):
    def body(dq_ref, dk_ref, dv_ref, dxr_ref, dxg_ref, dh1_ref, hp_ref, pa_ref, cs_ref, sn_ref,
             ln1, win, qag, wq, kvag, wk, wv, qg, kg,
             dhp_ref, dp_ref, dqraw_ref, dkraw_ref, hn_ref, cqn_ref, ckvn_ref,
             dln1_ref, dqag_ref, dkvag_ref, dqg_ref, dkg_ref):
        @pl.when(pl.program_id(0) == 0)
        def _():
            for r in (dln1_ref, dqag_ref, dkvag_ref, dqg_ref, dkg_ref):
                r[...] = jnp.zeros_like(r)

        hn, vjp_ln1 = jax.vjp(lambda h, g: _rms(h, g, D), hp_ref[...], ln1[...])
        hn_ref[...] = hn.astype(BF)
        pa_v = pa_ref[...]
        cqn, vjp_qa = jax.vjp(lambda x, g: _rms(x, g, Q_LORA), pa_v[:, :O_CKV], qag[...])
        ckvn, vjp_kva = jax.vjp(lambda x, g: _rms(x, g, KV_LORA), pa_v[:, O_CKV:O_KR], kvag[...])
        kr = pa_v[:, O_KR:O_XR]
        cqnb, ckvnb = cqn.astype(BF), ckvn.astype(BF)
        cqn_ref[...] = cqnb
        ckvn_ref[...] = ckvnb
        c, s = cs_ref[...], sn_ref[...]
        lane = lax.broadcasted_iota(jnp.int32, (1, HP), 1)
        rope_lanes = ((lane >= QK_NOPE) & (lane < QK_HEAD)).astype(F32)
        dkr = jnp.zeros((TR, HP), F32)
        dqg = jnp.zeros((1, HP), F32)
        dkg = jnp.zeros((1, HP), F32)
        qraw = _nt(cqnb, wq[...])
        kraw = _nt(ckvnb, wk[...])
        for h in range(N_HEADS):
            sl = slice(h * HP, (h + 1) * HP)
            dqraw, dg = _head_bwd(qraw[:, sl], qg[...], c, s, dq_ref[:, sl])
            dqg = dqg + dg
            dqraw_ref[:, sl] = dqraw.astype(BF)
            dkraw, dg = _head_bwd(kraw[:, sl] + kr, kg[...], c, s, dk_ref[:, sl])
            dkg = dkg + dg
            dkraw_ref[:, sl] = dkraw.astype(BF)
            dkr = dkr + dkraw * rope_lanes
        dcq, dqag = vjp_qa(_nn(dqraw_ref[...], wq[...]))
        dckv, dkvag = vjp_kva(_nn(dkraw_ref[...], wk[...]) + _nn(dv_ref[...].astype(BF), wv[...]))
        dpb = jnp.concatenate([dcq, dckv, dkr, dxr_ref[...], dxg_ref[...]], axis=1).astype(BF)
        dp_ref[...] = dpb
        dh, dln1 = vjp_ln1(_nn(dpb, win[...]))
        dhp_ref[...] = dh + dh1_ref[...]
        dln1_ref[...] += dln1
        dqag_ref[...] += dqag
        dkvag_ref[...] += dkvag
        dqg_ref[...] += dqg
        dkg_ref[...] += dkg

    acc = lambda n: pl.BlockSpec((1, n), lambda i: (0, 0))
    return pl.pallas_call(
        body, grid=(R // TR,), name="stage_a_bwd",
        in_specs=[_row_spec(N_HEADS * HP), _row_spec(N_HEADS * HP), _row_spec(D_ATTN), _row_spec(D_RNN),
                  _row_spec(D_RNN), _row_spec(D), _row_spec(D), _row_spec(O_XR), _row_spec(HP), _row_spec(HP),
                  _const_spec((1, D)), _const_spec((PC, D)), _const_spec((1, Q_LORA)),
                  _const_spec((N_HEADS * HP, Q_LORA)), _const_spec((1, KV_LORA)),
                  _const_spec((N_HEADS * HP, KV_LORA)), _const_spec((D_ATTN, KV_LORA)), _const_spec((1, HP)),
                  _const_spec((1, HP))],
        out_specs=[_row_spec(D), _row_spec(PC), _row_spec(N_HEADS * HP), _row_spec(N_HEADS * HP), _row_spec(D),
                   _row_spec(Q_LORA), _row_spec(KV_LORA), acc(D), acc(Q_LORA), acc(KV_LORA), acc(HP), acc(HP)],
        out_shape=[jax.ShapeDtypeStruct((R, D), F32), jax.ShapeDtypeStruct((R, PC), BF),
                   jax.ShapeDtypeStruct((R, N_HEADS * HP), BF), jax.ShapeDtypeStruct((R, N_HEADS * HP), BF),
                   jax.ShapeDtypeStruct((R, D), BF), jax.ShapeDtypeStruct((R, Q_LORA), BF),
                   jax.ShapeDtypeStruct((R, KV_LORA), BF), jax.ShapeDtypeStruct((1, D), F32),
                   jax.ShapeDtypeStruct((1, Q_LORA), F32), jax.ShapeDtypeStruct((1, KV_LORA), F32),
                   jax.ShapeDtypeStruct((1, HP), F32), jax.ShapeDtypeStruct((1, HP), F32)],
        compiler_params=_params("arbitrary"),
    )(dq, dk, dv, dxr, dxg, dh1, hp, pa, cs, sn, cw["ln1_g"], cw["win"], cw["qa_g"], cw["wq"], cw["kva_g"],
      cw["wk"], cw["wv"], cw["q_g"], cw["k_g"])


def _head_mask(half, dtype):
    lane = lax.broadcasted_iota(jnp.int32, (1, 2 * V_HEAD), 1)
    return ((lane >= V_HEAD) == (half == 1)).astype(dtype)


def _attn_specs(tq):
    n_q = TP // tq
    return (NB, N_HEADS // 2, n_q), dict(
        q=pl.BlockSpec((tq, 2 * HP), lambda b, j, i: (b * n_q + i, j)),
        k=pl.BlockSpec((TP, 2 * HP), lambda b, j, i: (b, j)),
        v=pl.BlockSpec((TP, 2 * V_HEAD), lambda b, j, i: (b, j)),
        o=pl.BlockSpec((tq, 2 * V_HEAD), lambda b, j, i: (b * n_q + i, j)),
        lse=pl.BlockSpec((None, tq, 2), lambda b, j, i: (j, b * n_q + i, 0)))


TQF = 1088


def _attn_fwd(q, k, v):
    def body(q_ref, k_ref, v_ref, o_ref, lse_ref):
        v2 = v_ref[...]
        o = jnp.zeros((TQF, 2 * V_HEAD), F32)
        lse = []
        for hh in range(2):
            sl = slice(hh * HP, (hh + 1) * HP)
            raw = _nt(q_ref[:, sl], k_ref[:, sl])
            m = jnp.max(raw, axis=-1, keepdims=True)
            e = jnp.exp2((raw - m) * SCALE_LOG2)
            l = jnp.sum(e, axis=-1, keepdims=True)
            o = o + _nn(e.astype(BF), v2 * _head_mask(hh, BF)) * (1.0 / l)
            lse.append(m * SCALE_LOG2 + jnp.log(l) * LOG2_E)
        o_ref[...] = o
        lane = lax.broadcasted_iota(jnp.int32, (TQF, 2), 1)
        lse_ref[...] = jnp.where(lane == 0, lse[0], lse[1])

    grid, sp = _attn_specs(TQF)
    return pl.pallas_call(
        body, grid=grid, name="attn_fwd", in_specs=[sp["q"], sp["k"], sp["v"]], out_specs=[sp["o"], sp["lse"]],
        out_shape=[jax.ShapeDtypeStruct((R, D_ATTN), F32), jax.ShapeDtypeStruct((N_HEADS // 2, R, 2), F32)],
        compiler_params=_params("arbitrary", "arbitrary", "arbitrary"),
    )(q, k, v)


def _attn_bwd(q, k, v, o, lse, do):
    def body(q_ref, k_ref, v_ref, o_ref, lse_ref, do_ref, dq_ref, dk_ref, dv_ref):
        @pl.when(pl.program_id(2) == 0)
        def _():
            dk_ref[...] = jnp.zeros_like(dk_ref)
            dv_ref[...] = jnp.zeros_like(dv_ref)

        do = do_ref[...]
        dob = do.astype(BF)
        do_o = do * o_ref[...]
        v2 = v_ref[...]
        dv_sum = jnp.zeros((TP, 2 * V_HEAD), F32)
        for hh in range(2):
            sl = slice(hh * HP, (hh + 1) * HP)
            qb, kb = q_ref[:, sl], k_ref[:, sl]
            p = jnp.exp2(_nt(qb, kb) * SCALE_LOG2 - lse_ref[:, hh:hh + 1])
            dp = _nt(dob, v2 * _head_mask(hh, BF))
            delta = jnp.sum(do_o * _head_mask(hh, F32), axis=-1, keepdims=True)
            dsb = (p * (dp - delta) * SCALE).astype(BF)
            dq_ref[:, sl] = _nn(dsb, kb)
            dk_ref[:, sl] += _tn(dsb, qb)
            dv_sum = dv_sum + _tn(p.astype(BF), dob) * _head_mask(hh, F32)
        dv_ref[...] += dv_sum

    grid, sp = _attn_specs(TQ)
    return pl.pallas_call(
        body, grid=grid, name="attn_bwd", in_specs=[sp["q"], sp["k"], sp["v"], sp["o"], sp["lse"], sp["o"]],
        out_specs=[sp["q"], sp["k"], sp["v"]],
        out_shape=[jax.ShapeDtypeStruct((R, N_HEADS * HP), F32), jax.ShapeDtypeStruct((R, N_HEADS * HP), F32),
                   jax.ShapeDtypeStruct((R, D_ATTN), F32)],
        compiler_params=_params("arbitrary", "arbitrary", "arbitrary"),
    )(q, k, v, o, lse, do)


def _tile_prefix(a_ref, b_ref, reverse):
    tiles = (TP // 8, 8, CG)
    r8 = lax.broadcasted_iota(jnp.int32, tiles, 1)
    a, b = a_ref[...].reshape(tiles), b_ref[...].reshape(tiles)
    for s in (1, 2, 4):
        shift = 8 - s if reverse else s
        keep = (r8 < 8 - s) if reverse else (r8 >= s)
        b = jnp.where(keep, a * pltpu.roll(b, shift, 1) + b, b)
        a = jnp.where(keep, a * pltpu.roll(a, shift, 1), a)
    a_ref[...] = a.reshape(TP, CG)
    b_ref[...] = b.reshape(TP, CG)


def _scan_pair(af_ref, bf_ref, hf_ref, ab_ref, bb_ref, hb_ref):
    _tile_prefix(af_ref, bf_ref, False)
    _tile_prefix(ab_ref, bb_ref, True)
    n_tiles = TP // 8

    def step(i, carry):
        cf, cb = carry
        rf = pl.multiple_of(i * 8, 8)
        rb = pl.multiple_of((n_tiles - 1 - i) * 8, 8)
        hf_ref[pl.ds(rf, 8), :] = bf_ref[pl.ds(rf, 8), :] + af_ref[pl.ds(rf, 8), :] * cf
        hb_ref[pl.ds(rb, 8), :] = bb_ref[pl.ds(rb, 8), :] + ab_ref[pl.ds(rb, 8), :] * cb
        cf = bf_ref[pl.ds(rf + 7, 1), :] + af_ref[pl.ds(rf + 7, 1), :] * cf
        cb = bb_ref[pl.ds(rb, 1), :] + ab_ref[pl.ds(rb, 1), :] * cb
        return cf, cb

    zero = jnp.zeros((1, CG), F32)
    lax.fori_loop(0, n_tiles, step, (zero, zero), unroll=8)


def _shifts(x):
    t = lax.broadcasted_iota(jnp.int32, x.shape, 0)
    xm2 = jnp.where(t >= 2, pltpu.roll(x, 2, 0), 0.0)
    xm1 = jnp.where(t >= 1, pltpu.roll(x, 1, 0), 0.0)
    xp1 = jnp.where(t < TP - 1, pltpu.roll(x, TP - 1, 0), 0.0)
    return xm2, xm1, xp1


def _softplus(z):
    e = jnp.exp(-jnp.abs(z))
    small = e * (1.0 - e * (0.5 - e * (1.0 / 3.0)))
    return jnp.maximum(z, 0.0) + jnp.where(e < 0.01, small, jnp.log(1.0 + e))


def _sigmoid(x):
    return 0.5 * jnp.tanh(0.5 * x) + 0.5


def _one_minus_sq(log_a, a):
    x = 2.0 * log_a
    series = -x * (1.0 + x * 0.5 * (1.0 + x * (1.0 / 3.0) * (1.0 + x * 0.25)))
    return jnp.where(x > -0.05, series, 1.0 - a * a)


def _gates(row0, xc, pa_f, pi_f, pa_b, pi_b, lam_f, lam_b):
    t = row0 + lax.broadcasted_iota(jnp.int32, xc.shape, 0)
    valid = t < T
    out = []
    for pa, pi_, lam in ((pa_f, pi_f, lam_f), (pa_b, pi_b, lam_b)):
        r = _sigmoid(pa)
        gate_i = _sigmoid(pi_)
        log_a = -LRU_C * r * _softplus(-lam)
        a = jnp.exp(log_a)
        mult = jnp.sqrt(jnp.maximum(_one_minus_sq(log_a, a), 0.0))
        out += [a, jnp.where(valid, mult * (gate_i * xc), 0.0)]
    return tuple(out)


def _gates_bwd(row0, xc, pres, lams, cots):
    t = row0 + lax.broadcasted_iota(jnp.int32, xc.shape, 0)
    valid = t < T
    dxc = jnp.zeros_like(xc)
    dpres, dlams = [], []
    for d in range(2):
        pa, pi_, lam = pres[2 * d], pres[2 * d + 1], lams[d]
        da, db = cots[2 * d], jnp.where(valid, cots[2 * d + 1], 0.0)
        r = _sigmoid(pa)
        gate_i = _sigmoid(pi_)
        sp = _softplus(-lam)
        log_a = -LRU_C * r * sp
        a = jnp.exp(log_a)
        m2 = jnp.maximum(_one_minus_sq(log_a, a), 0.0)
        mult = jnp.sqrt(m2)
        dxc = dxc + db * (mult * gate_i)
        d_gate = db * (mult * xc)
        d_m2 = jnp.where(m2 > 0.0, db * (gate_i * xc) * (0.5 * lax.rsqrt(m2)), 0.0)
        d_log_a = da * a - 2.0 * d_m2 * (a * a)
        dpres += [d_log_a * (-LRU_C * sp) * (r * (1.0 - r)), d_gate * (gate_i * (1.0 - gate_i))]
        d_sp = jnp.sum(d_log_a * (-LRU_C * r), axis=0, keepdims=True)
        dlams.append(-d_sp * jax.nn.sigmoid(-lam))
    return dxc, dpres, dlams


def _rnn_specs():
    seq = pl.BlockSpec((TP, CG), lambda g, b: (b, g))
    return dict(
        seq=seq,
        cw=pl.BlockSpec((CONV_W, CG), lambda g, b: (0, g)),
        cb=pl.BlockSpec((1, CG), lambda g, b: (0, g)),
        w4=pl.BlockSpec((None, CG, 4 * CG), lambda g, b: (g, 0, 0)),
        b4=pl.BlockSpec((None, 1, 4 * CG), lambda g, b: (g, 0, 0)),
        lam=pl.BlockSpec((None, 1, 2 * CG), lambda g, b: (g, 0, 0)),
    )


def _conv(x, xm2, xm1, xp1, cw_ref, cb_ref):
    return cw_ref[0:1, :] * xm2 + cw_ref[1:2, :] * xm1 + cw_ref[2:3, :] * x + cw_ref[3:4, :] * xp1 + cb_ref[...]


TC = 128
N_TC = TP // TC


def _split4(pre):
    return pre[:, :CG], pre[:, CG:2 * CG], pre[:, 2 * CG:3 * CG], pre[:, 3 * CG:]


def _rnn_fwd(xr, xg, cw):
    def body(xr_ref, xg_ref, cw_ref, cb_ref, w4_ref, b4_ref, lam_ref, y_ref, hf_ref, hb_ref, xc_s, af, bf, ab, bb):
        x = xr_ref[...]
        xc_s[...] = _conv(x, *_shifts(x), cw_ref, cb_ref)
        lam = lam_ref[...]

        def chunk(i, _):
            rows = pl.ds(pl.multiple_of(i * TC, TC), TC)
            xc = xc_s[rows, :]
            pre = _nn(xc.astype(BF), w4_ref[...]) + b4_ref[...]
            a_f, b_f, a_b, b_b = _gates(i * TC, xc, *_split4(pre), lam[:, :CG], lam[:, CG:])
            af[rows, :] = a_f
            bf[rows, :] = b_f
            ab[rows, :] = a_b
            bb[rows, :] = b_b
            return 0

        lax.fori_loop(0, N_TC, chunk, 0)
        _scan_pair(af, bf, hf_ref, ab, bb, hb_ref)
        y_ref[...] = (hf_ref[...] + hb_ref[...]) * jax.nn.gelu(xg_ref[...])

    sp = _rnn_specs()
    return pl.pallas_call(
        body, grid=(N_CG, NB), name="rnn_fwd",
        in_specs=[sp["seq"], sp["seq"], sp["cw"], sp["cb"], sp["w4"], sp["b4"], sp["lam"]],
        out_specs=[sp["seq"]] * 3, out_shape=[jax.ShapeDtypeStruct((R, D_RNN), F32)] * 3,
        scratch_shapes=[pltpu.VMEM((TP, CG), F32)] * 5,
        compiler_params=_params("arbitrary", "arbitrary"),
    )(xr, xg, cw["conv_w"], cw["conv_b"], cw["w4"], cw["b4"], cw["lam"])


def _rnn_bwd(dy, xr, xg, hf, hb, cw):
    def body(dy_ref, xr_ref, xg_ref, hf_ref, hb_ref, cw_ref, cb_ref, w4_ref, b4_ref, lam_ref,
             dxr_ref, dxg_ref, dcw_ref, dcb_ref, dw4_ref, db4_ref, dlam_ref,
             xc_s, af_s, ab_s, dhs_s, dhs2_s, lf_s, lb_s, daf_s, dab_s, dxc_s):
        @pl.when(pl.program_id(1) == 0)
        def _():
            for r in (dcw_ref, dcb_ref, dw4_ref, db4_ref, dlam_ref):
                r[...] = jnp.zeros_like(r)

        x = xr_ref[...]
        xc_s[...] = _conv(x, *_shifts(x), cw_ref, cb_ref)
        lam = lam_ref[...]

        def chunk1(i, _):
            rows = pl.ds(pl.multiple_of(i * TC, TC), TC)
            xc = xc_s[rows, :]
            pre = _nn(xc.astype(BF), w4_ref[...]) + b4_ref[...]
            a_f, _, a_b, _ = _gates(i * TC, xc, *_split4(pre), lam[:, :CG], lam[:, CG:])
            af_s[rows, :] = a_f
            ab_s[rows, :] = a_b
            _, vjp_y = jax.vjp(lambda h, g: h * jax.nn.gelu(g), hf_ref[rows, :] + hb_ref[rows, :], xg_ref[rows, :])
            dhs, dxg = vjp_y(dy_ref[rows, :])
            dhs_s[rows, :] = dhs
            dhs2_s[rows, :] = dhs
            dxg_ref[rows, :] = dxg
            return 0

        lax.fori_loop(0, N_TC, chunk1, 0)
        t = lax.broadcasted_iota(jnp.int32, (TP, CG), 0)
        af_s[...] = pltpu.roll(af_s[...], TP - 1, 0)
        ab_s[...] = pltpu.roll(ab_s[...], 1, 0)
        _scan_pair(ab_s, dhs_s, lb_s, af_s, dhs2_s, lf_s)
        daf_s[...] = lf_s[...] * jnp.where(t >= 1, pltpu.roll(hf_ref[...], 1, 0), 0.0)
        dab_s[...] = lb_s[...] * jnp.where(t < TP - 1, pltpu.roll(hb_ref[...], TP - 1, 0), 0.0)

        def chunk2(i, _):
            rows = pl.ds(pl.multiple_of(i * TC, TC), TC)
            xc = xc_s[rows, :]
            xcb = xc.astype(BF)
            pre = _nn(xcb, w4_ref[...]) + b4_ref[...]
            dxc, dpres, dlams = _gates_bwd(i * TC, xc, _split4(pre), (lam[:, :CG], lam[:, CG:]),
                                           (daf_s[rows, :], lf_s[rows, :], dab_s[rows, :], lb_s[rows, :]))
            dpre = jnp.concatenate(dpres, axis=1)
            dpreb = dpre.astype(BF)
            dxc_s[rows, :] = dxc + _nt(dpreb, w4_ref[...])
            dw4_ref[...] += _tn(xcb, dpreb)
            db4_ref[...] += jnp.sum(dpre, axis=0, keepdims=True)
            dlam_ref[...] += jnp.concatenate(dlams, axis=1)
            return 0

        lax.fori_loop(0, N_TC, chunk2, 0)
        dxc = dxc_s[...]
        dcb_ref[...] += jnp.sum(dxc, axis=0, keepdims=True)
        for tap, xs in enumerate(_shifts(x)[:2] + (x,) + _shifts(x)[2:]):
            dcw_ref[tap:tap + 1, :] += jnp.sum(xs * dxc, axis=0, keepdims=True)
        dxr_ref[...] = (cw_ref[0:1, :] * jnp.where(t < TP - 2, pltpu.roll(dxc, TP - 2, 0), 0.0)
                        + cw_ref[1:2, :] * jnp.where(t < TP - 1, pltpu.roll(dxc, TP - 1, 0), 0.0)
                        + cw_ref[2:3, :] * dxc
                        + cw_ref[3:4, :] * jnp.where(t >= 1, pltpu.roll(dxc, 1, 0), 0.0))

    sp = _rnn_specs()
    return pl.pallas_call(
        body, grid=(N_CG, NB), name="rnn_bwd",
        in_specs=[sp["seq"]] * 5 + [sp["cw"], sp["cb"], sp["w4"], sp["b4"], sp["lam"]],
        out_specs=[sp["seq"], sp["seq"], sp["cw"], sp["cb"], sp["w4"], sp["b4"], sp["lam"]],
        out_shape=[jax.ShapeDtypeStruct((R, D_RNN), F32), jax.ShapeDtypeStruct((R, D_RNN), F32),
                   jax.ShapeDtypeStruct((CONV_W, D_RNN), F32), jax.ShapeDtypeStruct((1, D_RNN), F32),
                   jax.ShapeDtypeStruct((N_CG, CG, 4 * CG), F32), jax.ShapeDtypeStruct((N_CG, 1, 4 * CG), F32),
                   jax.ShapeDtypeStruct((N_CG, 1, 2 * CG), F32)],
        scratch_shapes=[pltpu.VMEM((TP, CG), F32)] * 10,
        compiler_params=_params("arbitrary", "arbitrary"),
    )(dy, xr, xg, hf, hb, cw["conv_w"], cw["conv_b"], cw["w4"], cw["b4"], cw["lam"])


TD = 256
STAGE_D_VMEM = 58 * 1024 * 1024


def _stage_d(hp, o, y, tgt, cw):
    def body(hp_ref, o_ref, y_ref, tgt_ref, ga, gr, wout, ln2, wg, wu, wd,
             do_ref, dy_ref, dh1_ref, mix_ref, dh1b_ref, hn2_ref, dg_ref, du_ref, act_ref, dh2b_ref,
             loss_ref, dga_ref, dgr_ref, dln2_ref):
        i = pl.program_id(0)

        @pl.when(i == 0)
        def _():
            for r in (loss_ref, dga_ref, dgr_ref, dln2_ref):
                r[...] = jnp.zeros_like(r)

        mix_a, vjp_a = jax.vjp(lambda x, g: _rms(x, g, D_ATTN), o_ref[...], ga[...])
        mix_r, vjp_r = jax.vjp(lambda x, g: _rms(x, g, D_RNN), y_ref[...], gr[...])
        mab, mrb = mix_a.astype(BF), mix_r.astype(BF)
        mix_ref[:, :D_ATTN] = mab
        mix_ref[:, D_ATTN:] = mrb
        h1 = hp_ref[...] + _nn(mab, wout[:D_ATTN, :]) + _nn(mrb, wout[D_ATTN:, :])
        hn2, vjp_ln2 = jax.vjp(lambda x, g: _rms(x, g, D), h1, ln2[...])
        hn2b = hn2.astype(BF)
        hn2_ref[...] = hn2b
        act, vjp_act = jax.vjp(lambda g, u: jax.nn.silu(g) * u, _nt(hn2b, wg[...]), _nt(hn2b, wu[...]))
        actb = act.astype(BF)
        act_ref[...] = actb
        h2 = h1 + _nn(actb, wd[...])
        row = i * TD + lax.broadcasted_iota(jnp.int32, (TD, 1), 0)
        t = jnp.where(row >= TP, row - TP, row)
        err = jnp.where((t >= N_META) & (t < T), h2 - tgt_ref[...], 0.0)
        loss_ref[...] += jnp.sum(err * err) * (0.5 / D)
        dh2b = (err * (1.0 / D)).astype(BF)
        dh2b_ref[...] = dh2b
        dg, du = vjp_act(_nt(dh2b, wd[...]))
        dgb, dub = dg.astype(BF), du.astype(BF)
        dg_ref[...] = dgb
        du_ref[...] = dub
        dh1n, dln2 = vjp_ln2(_nn(dgb, wg[...]) + _nn(dub, wu[...]))
        dh1 = err * (1.0 / D) + dh1n
        dh1_ref[...] = dh1
        dh1b = dh1.astype(BF)
        dh1b_ref[...] = dh1b
        dmix = _nt(dh1b, wout[...])
        do, dga = vjp_a(dmix[:, :D_ATTN])
        dyr, dgr = vjp_r(dmix[:, D_ATTN:])
        do_ref[...] = do
        dy_ref[...] = dyr
        dga_ref[...] += dga
        dgr_ref[...] += dgr
        dln2_ref[...] += dln2

    rs = lambda n: _row_spec(n, TD)
    acc = lambda n: pl.BlockSpec((1, n), lambda i: (0, 0))
    return pl.pallas_call(
        body, grid=(R // TD,), name="stage_d",
        in_specs=[rs(D), rs(D_ATTN), rs(D_RNN), rs(D), _const_spec((1, D_ATTN)), _const_spec((1, D_RNN)),
                  _const_spec((D, D)), _const_spec((1, D)), _const_spec((D_FF, D)), _const_spec((D_FF, D)),
                  _const_spec((D_FF, D))],
        out_specs=[rs(D_ATTN), rs(D_RNN), rs(D), rs(D), rs(D), rs(D), rs(D_FF), rs(D_FF), rs(D_FF), rs(D),
                   acc(1), acc(D_ATTN), acc(D_RNN), acc(D)],
        out_shape=[jax.ShapeDtypeStruct((R, D_ATTN), F32), jax.ShapeDtypeStruct((R, D_RNN), F32),
                   jax.ShapeDtypeStruct((R, D), F32), jax.ShapeDtypeStruct((R, D), BF),
                   jax.ShapeDtypeStruct((R, D), BF), jax.ShapeDtypeStruct((R, D), BF),
                   jax.ShapeDtypeStruct((R, D_FF), BF), jax.ShapeDtypeStruct((R, D_FF), BF),
                   jax.ShapeDtypeStruct((R, D_FF), BF), jax.ShapeDtypeStruct((R, D), BF),
                   jax.ShapeDtypeStruct((1, 1), F32), jax.ShapeDtypeStruct((1, D_ATTN), F32),
                   jax.ShapeDtypeStruct((1, D_RNN), F32), jax.ShapeDtypeStruct((1, D), F32)],
        compiler_params=_params("arbitrary", vmem=STAGE_D_VMEM),
    )(hp, o, y, tgt, cw["ga"], cw["gr"], cw["wout"], cw["ln2_g"], cw["wg"], cw["wu"], cw["wd"])


TW = 2176


def _wgrad(a, b, name, tk=None):
    ka, nb = a.shape[1], b.shape[1]
    tk = ka if tk is None else tk

    def body(a_ref, b_ref, o_ref):
        @pl.when(pl.program_id(1) == 0)
        def _():
            o_ref[...] = jnp.zeros_like(o_ref)

        o_ref[...] += _tn(a_ref[...].astype(BF), b_ref[...].astype(BF))

    return pl.pallas_call(
        body, grid=(ka // tk, R // TW), name=name,
        in_specs=[pl.BlockSpec((TW, tk), lambda k, r: (r, k)), pl.BlockSpec((TW, nb), lambda k, r: (r, 0))],
        out_specs=pl.BlockSpec((tk, nb), lambda k, r: (k, 0)),
        out_shape=jax.ShapeDtypeStruct((ka, nb), F32),
        compiler_params=_params("arbitrary", "arbitrary"),
    )(a, b)


def _wgrad_heads(dq, dk, dv, cqn, ckvn):
    def body(dq_ref, dk_ref, dv_ref, cqn_ref, ckvn_ref, oq_ref, ok_ref, ov_ref):
        @pl.when(pl.program_id(0) == 0)
        def _():
            for r in (oq_ref, ok_ref, ov_ref):
                r[...] = jnp.zeros_like(r)

        ckvnb = ckvn_ref[...]
        oq_ref[...] += _tn(dq_ref[...], cqn_ref[...])
        ok_ref[...] += _tn(dk_ref[...], ckvnb)
        ov_ref[...] += _tn(dv_ref[...].astype(BF), ckvnb)

    rows = lambda a: pl.BlockSpec((TW, a.shape[1]), lambda r: (r, 0))
    full = lambda m, n: pl.BlockSpec((m, n), lambda r: (0, 0))
    shapes = [(dq.shape[1], cqn.shape[1]), (dk.shape[1], ckvn.shape[1]), (dv.shape[1], ckvn.shape[1])]
    return pl.pallas_call(
        body, grid=(R // TW,), name="wgrad_heads", in_specs=[rows(a) for a in (dq, dk, dv, cqn, ckvn)],
        out_specs=[full(*s) for s in shapes], out_shape=[jax.ShapeDtypeStruct(s, F32) for s in shapes],
        compiler_params=_params("arbitrary"),
    )(dq, dk, dv, cqn, ckvn)


def _rope_tables():
    half = QK_ROPE // 2
    freqs = 1.0 / (ROPE_THETA ** (jnp.arange(half, dtype=F32) / half))
    ang = jnp.arange(TP, dtype=F32)[:, None] * freqs[None, :]
    ones = jnp.ones((TP, QK_NOPE), F32)
    zeros = jnp.zeros((TP, QK_NOPE), F32)
    pad1 = jnp.ones((TP, HP - QK_HEAD), F32)
    pad0 = jnp.zeros((TP, HP - QK_HEAD), F32)
    cs = jnp.concatenate([ones, jnp.cos(ang), jnp.cos(ang), pad1], axis=1)
    sn = jnp.concatenate([zeros, jnp.sin(ang), jnp.sin(ang), pad0], axis=1)
    return jnp.tile(cs, (NB, 1)), jnp.tile(sn, (NB, 1))


def _pad_rows(a, lo, hi):
    return jnp.pad(a, ((0, 0), (lo, hi), (0, 0)))


def _pad_target(target):
    return _pad_rows(target, N_META, TP - T).reshape(R, D)


def _compute_weights(w):
    win_t = w["w_in_t"]
    kr = win_t[O_KR:O_KR + QK_ROPE]
    win = jnp.concatenate([win_t[:O_KR], jnp.zeros((QK_NOPE, D), F32), kr,
                           jnp.zeros((HP - QK_HEAD, D), F32), win_t[O_KR + QK_ROPE:]], axis=0)
    wq = _pad_rows(w["w_uq_t"].reshape(N_HEADS, QK_HEAD, Q_LORA), 0, HP - QK_HEAD)
    wkv = w["w_ukv_t"].reshape(N_HEADS, QK_NOPE + V_HEAD, KV_LORA)
    wk = _pad_rows(wkv[:, :QK_NOPE], 0, HP - QK_NOPE)
    wv = wkv[:, QK_NOPE:].reshape(D_ATTN, KV_LORA)
    gates = jnp.stack([w["lru_wa"][0], w["lru_wi"][0], w["lru_wa"][1], w["lru_wi"][1]])
    blk = gates.reshape(4, N_CG, 2, RNN_BW, RNN_BW)
    dense = jnp.einsum("tcaij,ab->tcaibj", blk, jnp.eye(2, dtype=F32)).reshape(4, N_CG, CG, CG)
    w4 = dense.transpose(1, 2, 0, 3).reshape(N_CG, CG, 4 * CG)
    bias = jnp.stack([w["lru_ba"][0], w["lru_bi"][0], w["lru_ba"][1], w["lru_bi"][1]])
    b4 = bias.reshape(4, N_CG, CG).transpose(1, 0, 2).reshape(N_CG, 1, 4 * CG)
    lam = w["lru_lambda"].reshape(2, N_CG, CG).transpose(1, 0, 2).reshape(N_CG, 1, 2 * CG)
    pad_g = lambda g: jnp.pad(g.reshape(1, QK_HEAD), ((0, 0), (0, HP - QK_HEAD)))
    return dict(
        ln1_g=w["ln1_g"].reshape(1, D), win=win.astype(BF), qa_g=w["q_a_norm_g"].reshape(1, Q_LORA),
        wq=wq.astype(BF).reshape(N_HEADS * HP, Q_LORA), kva_g=w["kv_a_norm_g"].reshape(1, KV_LORA),
        wk=wk.astype(BF).reshape(N_HEADS * HP, KV_LORA), wv=wv.astype(BF),
        q_g=pad_g(w["q_norm_g"]), k_g=pad_g(w["k_norm_g"]),
        conv_w=w["conv_w"].reshape(CONV_W, D_RNN), conv_b=w["conv_b"].reshape(1, D_RNN),
        w4=w4.astype(BF), b4=b4, lam=lam,
        ga=w["attn_out_g"].reshape(1, D_ATTN), gr=w["rnn_out_g"].reshape(1, D_RNN), ln2_g=w["ln2_g"].reshape(1, D),
    )


def _local_step(x, target, meta, w, late_weights, early_grads, mid_grads):
    cw = _compute_weights(w)
    cs, sn = _rope_tables()
    hp = jnp.concatenate([jnp.broadcast_to(meta[None], (NB, N_META, D)), x,
                          jnp.zeros((NB, TP - T, D), F32)], axis=1).reshape(R, D)
    tgt = target if target.ndim == 2 else _pad_target(target)

    pa, xr, xg, q, k, v = _stage_a_fwd(hp, cs, sn, cw)
    o, lse = _attn_fwd(q, k, v)
    y, hf, hb = _rnn_fwd(xr, xg, cw)
    late = late_weights([o, y])
    cw.update(wout=late["w_out"], wg=late["w_gate_t"], wu=late["w_up_t"], wd=late["w_down"])
    (do, dy, dh1, mixb, dh1b, hn2b, dgb, dub, actb, dh2b, loss, dga, dgr, dln2) = _stage_d(hp, o, y, tgt, cw)
    dwout = _wgrad(mixb, dh1b, "wgrad_out")
    dwg = _wgrad(dgb, hn2b, "wgrad_gate", tk=D_FF // 2)
    dwu = _wgrad(dub, hn2b, "wgrad_up", tk=D_FF // 2)
    dwd = _wgrad(actb, dh2b, "wgrad_down", tk=D_FF // 2)
    zero = early_grads(dict(w_out=dwout, w_gate=dwg, w_up=dwu, w_down=dwd))
    cw["conv_b"] = cw["conv_b"] + zero
    dxr, dxg, dcw, dcb, dw4, db4, dlam = _rnn_bwd(dy, xr, xg, hf, hb, cw)
    zero = mid_grads([dxr])
    dq, dk, dv = _attn_bwd(q, k, v, o, lse, do)
    (dhp, dpb, dqrawb, dkrawb, hn1b, cqnb, ckvnb, dln1, dqag, dkvag, dqg, dkg) = _stage_a_bwd(
        dq, dk, dv, dxr, dxg, dh1, hp, pa, cs, sn, dict(cw, qa_g=cw["qa_g"] + zero))

    dwin = _wgrad(dpb, hn1b, "wgrad_in", tk=PC // 2)
    dwq, dwk, dwv = _wgrad_heads(dqrawb, dkrawb, dv, cqnb, ckvnb)

    dwin_t = jnp.concatenate([dwin[:O_KR], dwin[O_KR + QK_NOPE:O_KR + QK_HEAD], dwin[O_XR:]], axis=0)
    dwq_t = dwq.reshape(N_HEADS, HP, Q_LORA)[:, :QK_HEAD].reshape(N_HEADS * QK_HEAD, Q_LORA)
    dwkv_t = jnp.concatenate([dwk.reshape(N_HEADS, HP, KV_LORA)[:, :QK_NOPE],
                              dwv.reshape(N_HEADS, V_HEAD, KV_LORA)], axis=1).reshape(2 * D_ATTN, KV_LORA)
    d4 = dw4.reshape(N_CG, 2, RNN_BW, 4, 2, RNN_BW)
    dgates = jnp.stack([d4[:, 0, :, :, 0, :], d4[:, 1, :, :, 1, :]], axis=1)
    dgates = dgates.transpose(3, 0, 1, 2, 4).reshape(4, N_HEADS, RNN_BW, RNN_BW)
    dbias = db4.reshape(N_CG, 4, CG).transpose(1, 0, 2).reshape(4, D_RNN)
    dhp3 = dhp.reshape(NB, TP, D)
    grads = dict(
        meta_tokens=jnp.sum(dhp3[:, :N_META], axis=0),
        ln1_g=dln1, w_in_t=dwin_t, q_a_norm_g=dqag, w_uq_t=dwq_t, kv_a_norm_g=dkvag, w_ukv_t=dwkv_t,
        q_norm_g=dqg[:, :QK_HEAD], k_norm_g=dkg[:, :QK_HEAD], conv_w=dcw[None], conv_b=dcb,
        lru_wa=jnp.stack([dgates[0], dgates[2]])[None], lru_ba=jnp.stack([dbias[0], dbias[2]])[None],
        lru_wi=jnp.stack([dgates[1], dgates[3]])[None], lru_bi=jnp.stack([dbias[1], dbias[3]])[None],
        lru_lambda=dlam.reshape(N_CG, 2, CG).transpose(1, 0, 2).reshape(1, 2, D_RNN),
        attn_out_g=dga, rnn_out_g=dgr, ln2_g=dln2,
    )
    return loss[0, 0], dhp3[:, N_META:T], grads, [dhp, dwin]


_ANY = pl.BlockSpec(memory_space=pl.ANY)


def _place():
    return lax.axis_index("x"), lax.axis_index("y"), lax.axis_index("c")


def _other_chips(x, y):
    return [(1 - x, y), (x, 1 - y), (1 - x, 1 - y)]


def _pair_exchange(big, whole, name):
    n_s, _, m, n = big.shape
    n_copies = n_s + len(whole)

    def body(*refs):
        big_ref, whole_refs = refs[0], refs[1:1 + len(whole)]
        rbig_ref, rwhole_refs = refs[1 + len(whole)], refs[2 + len(whole):2 + 2 * len(whole)]
        send_sems, recv_sems = refs[-2:]
        x, y, c = _place()
        sibling = (x, y, 1 - c)
        copies = [pltpu.make_async_remote_copy(
            src_ref=big_ref.at[s, 1 - c], dst_ref=rbig_ref.at[s], send_sem=send_sems.at[s], recv_sem=recv_sems.at[s],
            device_id=sibling, device_id_type=MESH) for s in range(n_s)]
        copies += [pltpu.make_async_remote_copy(
            src_ref=a, dst_ref=r, send_sem=send_sems.at[n_s + i], recv_sem=recv_sems.at[n_s + i],
            device_id=sibling, device_id_type=MESH) for i, (a, r) in enumerate(zip(whole_refs, rwhole_refs))]
        for cp in copies:
            cp.start()
        for cp in copies:
            cp.wait()

    return pl.pallas_call(
        body, name=name,
        out_shape=[jax.ShapeDtypeStruct((n_s, m, n), big.dtype)] + [jax.ShapeDtypeStruct(a.shape, a.dtype) for a in whole],
        in_specs=[_ANY] * (1 + len(whole)), out_specs=[_ANY] * (1 + len(whole)),
        scratch_shapes=[pltpu.SemaphoreType.DMA((n_copies,)), pltpu.SemaphoreType.DMA((n_copies,))],
    )(big, *whole)


_HBM = pl.BlockSpec(memory_space=pltpu.HBM)
_SEM = pl.BlockSpec(memory_space=pltpu.SEMAPHORE)
_EFFECT = pltpu.SideEffectType.DATAFLOW_SIDE_EFFECTING


def _split_copies(src_refs, land_refs, sems, plan, sending):
    n = len(sems) // 2
    return [pltpu.make_async_remote_copy(src_ref=s, dst_ref=d, send_sem=sems[k], recv_sem=sems[n + k], device_id=to,
                                         device_id_type=MESH)
            for k, (s, d, to) in enumerate(plan(src_refs, land_refs, sending))]


def _to_chips(src_at, land_at):
    def plan(src_refs, land_refs, sending):
        x, y, c = _place()
        return [(src_at(s, tx, ty, c), land_at(l, j, *((x, y) if sending else (tx, ty)), c), (tx, ty, c))
                for s, l in zip(src_refs, land_refs) for j, (tx, ty) in enumerate(_other_chips(x, y))]
    return plan


def _to_sibling(src_refs, land_refs, sending):
    x, y, c = _place()
    return [(s.at[k, 1 - c], l.at[k], (x, y, 1 - c)) for s, l in zip(src_refs, land_refs) for k in range(N_CHIPS)]


def _split_start(name, srcs, lands, plan, n, after=()):
    srcs, lands, after = list(srcs), list(lands), list(after)
    k = len(srcs)

    def body(*refs):
        outs = refs[2 * k + len(after):]
        for cp in _split_copies(refs[:k], refs[k:2 * k], outs[:2 * n], plan, True):
            cp.start()
        outs[2 * n + 2 * k][...] = jnp.zeros_like(outs[2 * n + 2 * k])

    outs = pl.pallas_call(
        body, name=name,
        out_shape=(pltpu.SemaphoreType.DMA(()),) * (2 * n) + tuple(pltpu.HBM(a.shape, a.dtype) for a in srcs + lands)
        + (jax.ShapeDtypeStruct((8, LANES), F32),),
        in_specs=(_HBM,) * (2 * k) + (_ANY,) * len(after),
        out_specs=(_SEM,) * (2 * n) + (_HBM,) * (2 * k) + (pl.BlockSpec(memory_space=pltpu.VMEM),),
        input_output_aliases={i: 2 * n + i for i in range(2 * k)},
        compiler_params=pltpu.CompilerParams(has_side_effects=_EFFECT),
    )(*[pltpu.with_memory_space_constraint(a, pltpu.HBM) for a in srcs + lands], *after)
    return outs[:2 * n], list(outs[2 * n:2 * n + k]), list(outs[2 * n + k:2 * n + 2 * k]), outs[2 * n + 2 * k]


def _split_wait(name, sems, srcs, lands, after, plan):
    srcs, lands = list(srcs), list(lands)
    k = len(srcs)

    def body(*refs):
        for cp in _split_copies(refs[:k], refs[k:2 * k], refs[2 * k:2 * k + len(sems)], plan, False):
            cp.wait_send()
            cp.wait_recv()

    outs = pl.pallas_call(
        body, name=name, out_shape=tuple(pltpu.HBM(a.shape, a.dtype) for a in srcs + lands),
        in_specs=(_HBM,) * (2 * k) + (_SEM,) * len(sems) + (_ANY,) * len(after), out_specs=(_HBM,) * (2 * k),
        input_output_aliases={i: i for i in range(2 * k)}, compiler_params=pltpu.CompilerParams(has_side_effects=_EFFECT),
    )(*srcs, *lands, *sems, *after)
    return list(outs[:k]), list(outs[k:])


def _gather_finish(lands, pieces, name):
    k = len(lands)

    def body(*refs):
        land_refs, piece_refs, out_refs, stages = refs[:k], refs[k:2 * k], refs[2 * k:3 * k], refs[3 * k:4 * k]
        send_sems, recv_sems, load_sems, store_sems = refs[4 * k:]
        x, y, c = _place()
        sibling = (x, y, 1 - c)
        remote, loads, stores, arrivals = [], [], [], []
        for a in range(k):
            m = lands[a].shape[0] // 8

            def rows(px, py, pc, ref, m=m):
                return ref.at[pl.ds((4 * px + 2 * py + pc) * m, m), :]

            for j, (tx, ty) in enumerate(_other_chips(x, y)):
                sems = dict(send_sem=send_sems.at[3 * a + j], recv_sem=recv_sems.at[3 * a + j], device_id=sibling,
                            device_id_type=MESH)
                remote.append(pltpu.make_async_remote_copy(
                    src_ref=rows(tx, ty, c, land_refs[a]), dst_ref=rows(tx, ty, c, out_refs[a]), **sems))
                arrivals.append(pltpu.make_async_remote_copy(
                    src_ref=rows(tx, ty, 1 - c, out_refs[a]), dst_ref=rows(tx, ty, 1 - c, out_refs[a]), **sems))
            for h in range(2):
                loads.append(pltpu.make_async_copy(piece_refs[a].at[pl.ds(h * m, m), :], stages[a].at[h],
                                                   load_sems.at[2 * a + h]))
                stores.append(pltpu.make_async_copy(stages[a].at[h], rows(x, y, h, out_refs[a]), store_sems.at[2 * a + h]))
        for cp in remote + loads:
            cp.start()
        for ld, st in zip(loads, stores):
            ld.wait()
            st.start()
        for cp, arrival in zip(remote, arrivals):
            cp.wait_send()
            arrival.wait_recv()
        for cp in stores:
            cp.wait()

    return pl.pallas_call(
        body, name=name, out_shape=[jax.ShapeDtypeStruct(a.shape, a.dtype) for a in lands],
        in_specs=[_ANY] * (2 * k), out_specs=[_ANY] * k, input_output_aliases={i: i for i in range(k)},
        scratch_shapes=[pltpu.VMEM((2, a.shape[0] // 8, a.shape[1]), a.dtype) for a in lands]
        + [pltpu.SemaphoreType.DMA((3 * k,)), pltpu.SemaphoreType.DMA((3 * k,)), pltpu.SemaphoreType.DMA((2 * k,)),
           pltpu.SemaphoreType.DMA((2 * k,))],
    )(*lands, *pieces)


def _pair_fill(bufs, name):
    k = len(bufs)

    def body(*refs):
        send_sems, recv_sems = refs[-2:]
        x, y, c = _place()
        copies = [pltpu.make_async_remote_copy(
            src_ref=refs[i].at[c], dst_ref=refs[k + i].at[c], send_sem=send_sems.at[i], recv_sem=recv_sems.at[i],
            device_id=(x, y, 1 - c), device_id_type=MESH) for i in range(k)]
        for cp in copies:
            cp.start()
        for i, cp in enumerate(copies):
            cp.wait_send()
            pltpu.make_async_remote_copy(
                src_ref=refs[i].at[1 - c], dst_ref=refs[k + i].at[1 - c], send_sem=send_sems.at[i],
                recv_sem=recv_sems.at[i], device_id=(x, y, 1 - c), device_id_type=MESH).wait_recv()

    return pl.pallas_call(
        body, name=name, out_shape=[jax.ShapeDtypeStruct(a.shape, a.dtype) for a in bufs], in_specs=[_ANY] * k,
        out_specs=[_ANY] * k, input_output_aliases={i: i for i in range(k)},
        scratch_shapes=[pltpu.SemaphoreType.DMA((k,)), pltpu.SemaphoreType.DMA((k,))],
    )(*bufs)


def _row_tile(rows, cap=512):
    for t in range(cap - cap % 8, 7, -8):
        if rows % t == 0:
            return t
    return rows


def _elementwise(fn, n_out, name, *arrs, out_dtype=F32):
    rows, cols = arrs[0].shape
    tr = _row_tile(rows)
    n_in = len(arrs)

    def body(*refs):
        outs = fn(*[r[...].astype(F32) for r in refs[:n_in]])
        for r, o in zip(refs[n_in:], outs):
            r[...] = o.astype(out_dtype)

    spec = pl.BlockSpec((tr, cols), lambda i: (i, 0))
    return pl.pallas_call(
        body, grid=(rows // tr,), name=name, in_specs=[spec] * n_in, out_specs=[spec] * n_out,
        out_shape=[jax.ShapeDtypeStruct((rows, cols), out_dtype)] * n_out, compiler_params=_params("arbitrary"),
    )(*arrs)


def _pair_sums(gpacks, rbigs, ci, name):
    k = len(gpacks)

    def body(c_ref, *refs):
        for g_ref, r_ref, o_ref in zip(refs[:k], refs[k:2 * k], refs[2 * k:]):
            o_ref[...] = (g_ref[...] + r_ref[...]).astype(BF)

    half = lambda a: pl.BlockSpec((None,) + a.shape[1:], lambda s, c: (s, 0, 0))
    return pl.pallas_call(
        body, name=name, out_shape=[jax.ShapeDtypeStruct(r.shape, BF) for r in rbigs],
        grid_spec=pltpu.PrefetchScalarGridSpec(
            num_scalar_prefetch=1, grid=(N_CHIPS,),
            in_specs=[pl.BlockSpec((None, None) + g.shape[2:], lambda s, c: (s, c[0], 0, 0)) for g in gpacks]
            + [half(r) for r in rbigs],
            out_specs=[half(r) for r in rbigs]),
        compiler_params=_params("arbitrary"),
    )(ci.reshape(1), *gpacks, *rbigs)


def _chip_sums(sums, landed, chip, ci, name):
    k = len(sums)

    def body(p_ref, *refs):
        for own_ref, land_ref, o_ref in zip(refs[:k], refs[k:2 * k], refs[2 * k:]):
            f = lambda v: v.astype(F32)
            o_ref[...] = _add4(f(own_ref[...]), f(land_ref[0]), f(land_ref[1]), f(land_ref[2]))[0]

    return pl.pallas_call(
        body, name=name, out_shape=[jax.ShapeDtypeStruct((2,) + s.shape[1:], F32) for s in sums],
        grid_spec=pltpu.PrefetchScalarGridSpec(
            num_scalar_prefetch=1, grid=(1,),
            in_specs=[pl.BlockSpec((None,) + s.shape[1:], lambda i, p: (p[0], 0, 0)) for s in sums]
            + [pl.BlockSpec(l.shape, lambda i, p: (0, 0, 0)) for l in landed],
            out_specs=[pl.BlockSpec((None,) + s.shape[1:], lambda i, p: (p[1], 0, 0)) for s in sums]),
        compiler_params=_params("arbitrary"),
    )(jnp.stack([chip, ci]), *sums, *landed)


def _add2(a, b):
    return (a + b,)


def _add4(own, r0, r1, r2):
    return ((own + r2) + (r0 + r1),)


def _adamw_small(ws, gs, ms, vs):
    k = len(ws)

    def body(*refs):
        for i in range(k):
            outs = _adamw_math(*[refs[j * k + i][...] for j in range(4)])
            for j, o in enumerate(outs):
                refs[(4 + j) * k + i][...] = o

    return pl.pallas_call(
        body, name="adamw_small", out_shape=[jax.ShapeDtypeStruct(w.shape, F32) for w in ws] * 3,
    )(*ws, *gs, *ms, *vs)


def _adamw_math(w, g, m, v):
    m = ADAM_B1 * m + (1.0 - ADAM_B1) * g
    v = ADAM_B2 * v + (1.0 - ADAM_B2) * (g * g)
    m_hat = m / (1.0 - ADAM_B1 ** ADAM_STEP)
    v_hat = v / (1.0 - ADAM_B2 ** ADAM_STEP)
    delta = -ADAM_LR * (m_hat / (jnp.sqrt(v_hat) + ADAM_EPS) + ADAM_WD * w)
    return delta, m, v


WEIGHTS = ["meta_tokens", "ln1_g", "w_in", "q_a_norm_g", "w_uq", "kv_a_norm_g", "w_ukv", "q_norm_g", "k_norm_g",
           "conv_w", "conv_b", "lru_wa", "lru_ba", "lru_wi", "lru_bi", "lru_lambda", "attn_out_g", "rnn_out_g",
           "w_out", "ln2_g", "w_gate", "w_up", "w_down"]
BIG = ["w_in", "w_uq", "w_ukv", "w_out", "w_gate", "w_up", "w_down"]
BIG_T = {"w_in": True, "w_uq": True, "w_ukv": True, "w_out": False, "w_gate": True, "w_up": True, "w_down": False}
BIG_ROWS = {"w_in": 424, "w_uq": 72, "w_ukv": 64, "w_out": 256, "w_gate": 704, "w_up": 704, "w_down": 704}
EARLY = ["w_in", "w_uq", "w_ukv"]
LATE = ["w_out", "w_gate", "w_up", "w_down"]
EARLY_ROWS = 576
SMALL_SHARDED = ["meta_tokens", "conv_w", "lru_ba", "lru_bi", "lru_lambda"]
SMALL = [n for n in WEIGHTS if n not in BIG]
SMALL_PACK_ROWS = 160


def _offsets(names):
    off, o = {}, 0
    for n in names:
        off[n] = o
        o += BIG_ROWS[n]
    return off


def _shard_pack(names, src, rows):
    parts = [_to_pack_piece(n, src[n]) for n in names]
    used = sum(BIG_ROWS[n] for n in names)
    if rows > used:
        parts.append(jnp.zeros((rows - used, D), F32))
    return jnp.concatenate(parts, axis=0)


def _grad_pack(names, g, rows):
    parts = [g[n].reshape(N_CHIPS, BIG_ROWS[n], D) for n in names]
    used = sum(BIG_ROWS[n] for n in names)
    if rows > used:
        parts.append(jnp.zeros((N_CHIPS, rows - used, D), F32))
    return jnp.concatenate(parts, axis=1).reshape(N_CHIPS, 2, rows // 2, D)


def _to_pack_piece(name, shard):
    a = shard[0].T if BIG_T[name] else shard[0]
    return a.reshape(BIG_ROWS[name], D)


def _flat_pack(arrs, rows):
    flat = jnp.concatenate([a.reshape(-1) for a in arrs])
    return jnp.pad(flat, (0, rows * D - flat.shape[0])).reshape(rows, D)


def _flat_unpack(pack, shapes):
    flat, out, o = pack.reshape(-1), [], 0
    for s in shapes:
        n = math.prod(s)
        out.append(flat[o:o + n].reshape(s))
        o += n
    return out


def kernel(x, meta_tokens, ln1_g, w_in, q_a_norm_g, w_uq, kv_a_norm_g, w_ukv, q_norm_g, k_norm_g, conv_w, conv_b, lru_wa, lru_ba, lru_wi, lru_bi, lru_lambda, attn_out_g, rnn_out_g, w_out, ln2_g, w_gate, w_up, w_down, loss_target, m_meta_tokens, m_ln1_g, m_w_in, m_q_a_norm_g, m_w_uq, m_kv_a_norm_g, m_w_ukv, m_q_norm_g, m_k_norm_g, m_conv_w, m_conv_b, m_lru_wa, m_lru_ba, m_lru_wi, m_lru_bi, m_lru_lambda, m_attn_out_g, m_rnn_out_g, m_w_out, m_ln2_g, m_w_gate, m_w_up, m_w_down, v_meta_tokens, v_ln1_g, v_w_in, v_q_a_norm_g, v_w_uq, v_kv_a_norm_g, v_w_ukv, v_q_norm_g, v_k_norm_g, v_conv_w, v_conv_b, v_lru_wa, v_lru_ba, v_lru_wi, v_lru_bi, v_lru_lambda, v_attn_out_g, v_rnn_out_g, v_w_out, v_ln2_g, v_w_gate, v_w_up, v_w_down):
    wts = dict(zip(WEIGHTS, (meta_tokens, ln1_g, w_in, q_a_norm_g, w_uq, kv_a_norm_g, w_ukv, q_norm_g, k_norm_g, conv_w, conv_b, lru_wa, lru_ba, lru_wi, lru_bi, lru_lambda, attn_out_g, rnn_out_g, w_out, ln2_g, w_gate, w_up, w_down)))
    mom = dict(zip(WEIGHTS, (m_meta_tokens, m_ln1_g, m_w_in, m_q_a_norm_g, m_w_uq, m_kv_a_norm_g, m_w_ukv, m_q_norm_g, m_k_norm_g, m_conv_w, m_conv_b, m_lru_wa, m_lru_ba, m_lru_wi, m_lru_bi, m_lru_lambda, m_attn_out_g, m_rnn_out_g, m_w_out, m_ln2_g, m_w_gate, m_w_up, m_w_down)))
    var = dict(zip(WEIGHTS, (v_meta_tokens, v_ln1_g, v_w_in, v_q_a_norm_g, v_w_uq, v_kv_a_norm_g, v_w_ukv, v_q_norm_g, v_k_norm_g, v_conv_w, v_conv_b, v_lru_wa, v_lru_ba, v_lru_wi, v_lru_bi, v_lru_lambda, v_attn_out_g, v_rnn_out_g, v_w_out, v_ln2_g, v_w_gate, v_w_up, v_w_down)))
    xi, yi, ci = _place()
    chip = 2 * xi + yi
    off_e = _offsets(EARLY)
    half_e = EARLY_ROWS // 2
    gather_plan = _to_chips(lambda ref, tx, ty, c: ref.at[pl.ds(c * (ref.shape[0] // 2), ref.shape[0] // 2), :],
                            lambda ref, j, px, py, c: ref.at[pl.ds((4 * px + 2 * py + c) * (ref.shape[0] // 8),
                                                                   ref.shape[0] // 8), :])
    scatter_plan = _to_chips(lambda ref, tx, ty, c: ref.at[2 * tx + ty], lambda ref, j, px, py, c: ref.at[j])
    everywhere = _to_chips(lambda ref, tx, ty, c: ref, lambda ref, j, px, py, c: ref.at[j])
    n_late = len(LATE)

    pack_e = _shard_pack(EARLY, wts, EARLY_ROWS).astype(BF)
    spack = jnp.concatenate([meta_tokens[:, :LANES], meta_tokens[:, LANES:], conv_w[0], lru_ba[0], lru_bi[0],
                             lru_lambda[0], jnp.zeros((6, LANES), F32)], axis=0)
    sems_g, src_g, land_g, _ = _split_start(
        "gather_early_start", [pack_e, spack], [lax.empty((N_CHIPS * EARLY_ROWS, D), BF), lax.empty((N_CHIPS * 48, LANES), F32)],
        gather_plan, 6)
    tgt_padded = _pad_target(loss_target)
    pieces_l = [_to_pack_piece(n, wts[n]).astype(BF) for n in LATE]
    src_g, land_g = _split_wait("gather_early_wait", sems_g, src_g, land_g, [tgt_padded] + pieces_l, gather_plan)
    ge, gs = _gather_finish(land_g, src_g, "gather_early_finish")
    ge = ge.reshape(N_CHIPS, EARLY_ROWS, D)
    gs = gs.reshape(N_CHIPS, 48, LANES)
    full = {n: ge[:, off_e[n]:off_e[n] + BIG_ROWS[n]] for n in EARLY}
    sems_l, src_l, land_l, tied = _split_start(
        "gather_late_start", pieces_l, [lax.empty((N_CHIPS * BIG_ROWS[n], D), BF) for n in LATE], gather_plan, 3 * n_late,
        after=[ge])

    def late_weights(after):
        pieces, lands = _split_wait("gather_late_wait", sems_l, src_l, land_l, after, gather_plan)
        w_out_, w_gate_, w_up_, w_down_ = _gather_finish(lands, pieces, "gather_late_finish")
        return dict(w_out=w_out_, w_gate_t=w_gate_, w_up_t=w_up_, w_down=w_down_)

    pair, late = {}, {}

    def early_grads(g_late):
        halves = [g_late[n].reshape(N_CHIPS, 2, BIG_ROWS[n] // 2, D) for n in LATE]
        pair["sems"], pair["src"], pair["land"], zeros = _split_start(
            "grad_pair_late_start", halves, [lax.empty((N_CHIPS, BIG_ROWS[n] // 2, D), F32) for n in LATE], _to_sibling,
            N_CHIPS * n_late)
        return zeros[0, 0]

    def mid_grads(after):
        halves, landed = _split_wait("grad_pair_late_wait", pair["sems"], pair["src"], pair["land"], after, _to_sibling)
        chip_sums = _pair_sums(halves, landed, ci, "grad_pair_sum_late")
        late["sems"], late["src"], late["land"], zeros = _split_start(
            "grad_chip_late_start", chip_sums, [lax.empty((3, BIG_ROWS[n] // 2, D), BF) for n in LATE], scatter_plan,
            3 * n_late)
        return zeros[0, 0]

    cols = lambda a: a.transpose(1, 0, 2).reshape(a.shape[1], N_CHIPS * a.shape[2])
    meta_full = cols(jnp.concatenate([gs[:, 0:16], gs[:, 16:32]], axis=2))
    w = dict(
        w_in_t=full["w_in"].reshape(IN_COLS, D), w_uq_t=full["w_uq"].reshape(N_HEADS * QK_HEAD, Q_LORA),
        w_ukv_t=full["w_ukv"].reshape(2 * D_ATTN, KV_LORA),
        ln1_g=ln1_g, q_a_norm_g=q_a_norm_g, kv_a_norm_g=kv_a_norm_g, q_norm_g=q_norm_g, k_norm_g=k_norm_g,
        conv_w=cols(gs[:, 32:36]), conv_b=conv_b, lru_wa=lru_wa[0], lru_ba=cols(gs[:, 36:38]), lru_wi=lru_wi[0],
        lru_bi=cols(gs[:, 38:40]), lru_lambda=cols(gs[:, 40:42]), attn_out_g=attn_out_g, rnn_out_g=rnn_out_g,
        ln2_g=ln2_g,
    )

    loss_local, grad_x, g, last = _local_step(x, tgt_padded, meta_full + tied[0, 0], w, late_weights, early_grads,
                                              mid_grads)

    gpack = _grad_pack(EARLY, {"w_in": g["w_in_t"], "w_uq": g["w_uq_t"], "w_ukv": g["w_ukv_t"]}, EARLY_ROWS)
    full_shapes = {n: wts[n].shape for n in SMALL}
    full_shapes.update(meta_tokens=(N_META, D), conv_w=(1, CONV_W, D_RNN), lru_ba=(1, 2, D_RNN), lru_bi=(1, 2, D_RNN),
                       lru_lambda=(1, 2, D_RNN))
    gsmall = _flat_pack([g[n] for n in SMALL] + [loss_local], SMALL_PACK_ROWS)
    rbig, rsmall = _pair_exchange(gpack, [gsmall], "grad_pair_exchange")
    chip_big = _pair_sums([gpack], [rbig], ci, "grad_pair_sum")
    (chip_small,) = _elementwise(_add2, 1, "grad_pair_sum_small", gsmall, rsmall)
    sems_e, src_e, land_e, zero_e = _split_start(
        "grad_chip_early_start", chip_big, [lax.empty((3, half_e, D), BF)], scatter_plan, 3)
    sems_s, src_s, land_s, zero_s = _split_start(
        "grad_small_start", [chip_small], [lax.empty((3, SMALL_PACK_ROWS, D), F32)], everywhere, 3)

    grads, delta, new_m, new_v = {}, {}, {}, {}

    def adamw_big(n, gshard):
        _, k, cols = wts[n].shape
        as_rows = (lambda a: a[0].T) if BIG_T[n] else (lambda a: a[0])
        back = (lambda a: a.T[None]) if BIG_T[n] else (lambda a: a[None])
        g2 = gshard.reshape((cols, k) if BIG_T[n] else (k, cols))
        d_, m_, v_ = _elementwise(_adamw_math, 3, "adamw_" + n, as_rows(wts[n]), g2, as_rows(mom[n]), as_rows(var[n]))
        grads[n], delta[n], new_m[n], new_v[n] = back(g2), back(d_), back(m_), back(v_)
        return d_

    sums, landed = _split_wait("grad_chip_late_wait", late["sems"], late["src"], late["land"], last + [zero_e, zero_s],
                               scatter_plan)
    shards_l = _pair_fill(_chip_sums(sums, landed, chip, ci, "grad_chip_sum_late"), "grad_pair_fill_late")
    done_late = [adamw_big(n, buf) for n, buf in zip(LATE, shards_l)][-1]
    src_e, land_e = _split_wait("grad_chip_early_wait", sems_e, src_e, land_e, [done_late], scatter_plan)
    src_s, land_s = _split_wait("grad_small_wait", sems_s, src_s, land_s, [done_late], everywhere)
    (shard_e,) = _pair_fill(_chip_sums(src_e, land_e, chip, ci, "grad_chip_sum"), "grad_pair_fill_early")
    shard_e = shard_e.reshape(EARLY_ROWS, D)
    for n in EARLY:
        adamw_big(n, shard_e[off_e[n]:off_e[n] + BIG_ROWS[n]])
    (small_sum,) = _elementwise(_add4, 1, "grad_chip_sum_small", src_s[0], land_s[0][0], land_s[0][1], land_s[0][2])
    *small_grads, loss = _flat_unpack(small_sum, [full_shapes[n] for n in SMALL] + [()])
    small_full = dict(zip(SMALL, small_grads))
    for n in SMALL:
        a = small_full[n]
        if n in SMALL_SHARDED:
            width = wts[n].shape[-1]
            a = lax.dynamic_slice_in_dim(a, chip * width, width, axis=a.ndim - 1)
        grads[n] = a.reshape(wts[n].shape)

    rows_of = lambda a: a.reshape(-1, a.shape[-1])
    outs = _adamw_small(*[[rows_of(src[n]) for n in SMALL] for src in (wts, grads, mom, var)])
    for j, dst in enumerate((delta, new_m, new_v)):
        dst.update({n: outs[j * len(SMALL) + i].reshape(wts[n].shape) for i, n in enumerate(SMALL)})

    return (loss, grad_x, *[grads[n] for n in WEIGHTS], *[delta[n] for n in WEIGHTS],
            *[new_m[n] for n in WEIGHTS], *[new_v[n] for n in WEIGHTS])
```

```python
import math

import jax
import jax.numpy as jnp
from jax import lax
from jax.experimental import pallas as pl
from jax.experimental.pallas import tpu as pltpu

F32 = jnp.float32
BF = jnp.bfloat16
MESH = pl.DeviceIdType.MESH

D = 1024
SEQ = 2048
N_META = 16
T = N_META + SEQ
N_HEADS = 8
QK_NOPE = 64
QK_ROPE = 32
QK_HEAD = 96
V_HEAD = 64
Q_LORA = 384
KV_LORA = 256
D_ATTN = 512
D_RNN = 512
RNN_BW = 64
CONV_W = 4
LRU_C = 8.0
ROPE_THETA = 10000.0
D_FF = 2816
EPS = 1e-6
IN_COLS = 1696
ADAM_LR, ADAM_B1, ADAM_B2, ADAM_EPS, ADAM_WD, ADAM_STEP = 0.001, 0.9, 0.999, 1e-08, 0.01, 10

LANES = 128
TP = 2176
NB = 2
R = NB * TP
TR = 256
TRF = 256
TQ = 1088
HP = LANES
PC = 1792
O_CKV, O_KR, O_XR, O_XG = 384, 640, 768, 1280
CG = 128
N_CG = D_RNN // CG
VMEM_LIMIT = 56 * 1024 * 1024
N_CHIPS = 4
SCALE = QK_HEAD ** -0.5
KEY_MASK = -30000.0
LOG2_E = 1.4426950408889634
SCALE_LOG2 = SCALE * LOG2_E


def _nt(a, b):
    return lax.dot_general(a, b, (((1,), (1,)), ((), ())), preferred_element_type=F32)


def _nn(a, b):
    return jnp.dot(a, b, preferred_element_type=F32)


def _tn(a, b):
    return lax.dot_general(a, b, (((0,), (0,)), ((), ())), preferred_element_type=F32)


def _rms(x, g, n):
    ms = jnp.sum(x * x, axis=-1, keepdims=True) * (1.0 / n)
    return x * lax.rsqrt(ms + EPS) * g


def _lane_sum(y):
    return jnp.sum(y, axis=-1, keepdims=True)


def _rot(x):
    lane = lax.broadcasted_iota(jnp.int32, x.shape, 1)
    left = pltpu.roll(x, HP - 16, 1)
    right = pltpu.roll(x, 16, 1)
    lo = (lane >= QK_NOPE) & (lane < QK_NOPE + 16)
    hi = (lane >= QK_NOPE + 16) & (lane < QK_HEAD)
    return jnp.where(lo, -left, jnp.where(hi, right, 0.0))


def _head(x, g, cs, sn):
    n = x * lax.rsqrt(_lane_sum(x * x) * (1.0 / QK_HEAD) + EPS) * g
    return n * cs + _rot(n) * sn


def _head_bwd(x, g, cs, sn, dout):
    rs = lax.rsqrt(_lane_sum(x * x) * (1.0 / QK_HEAD) + EPS)
    xh = x * rs
    dn = dout * cs - _rot(dout * sn)
    gdn = g * dn
    t = _lane_sum(gdn * xh) * (1.0 / QK_HEAD)
    return rs * (gdn - xh * t), jnp.sum(dn * xh, axis=0, keepdims=True)


def _const_spec(shape):
    return pl.BlockSpec(shape, lambda *_: (0,) * len(shape), pipeline_mode=pl.Buffered(1))


def _row_spec(n, tr=TR):
    return pl.BlockSpec((tr, n), lambda i: (i, 0))


def _params(*sem, vmem=VMEM_LIMIT):
    return pltpu.CompilerParams(dimension_semantics=sem, vmem_limit_bytes=vmem)


def _stage_a_fwd(hp, cs, sn, cw):
    def body(hp_ref, cs_ref, sn_ref, ln1, win, qag, wq, kvag, wk, wv, qg, kg,
             pa_ref, xr_ref, xg_ref, q_ref, k_ref, v_ref):
        hn = _rms(hp_ref[...], ln1[...], D).astype(BF)
        p = _nt(hn, win[...])
        pa_ref[...] = p[:, :O_XR]
        xr_ref[...] = p[:, O_XR:O_XG]
        xg_ref[...] = p[:, O_XG:]
        cqn = _rms(p[:, :O_CKV], qag[...], Q_LORA).astype(BF)
        ckvn = _rms(p[:, O_CKV:O_KR], kvag[...], KV_LORA).astype(BF)
        kr = p[:, O_KR:O_XR]
        c, s = cs_ref[...], sn_ref[...]
        mask_lane = lax.broadcasted_iota(jnp.int32, (1, HP), 1) == QK_HEAD
        row = pl.program_id(0) * TRF + lax.broadcasted_iota(jnp.int32, (TRF, 1), 0)
        key_mask = jnp.where(jnp.where(row >= TP, row - TP, row) < T, 0.0, KEY_MASK)
        qraw = _nt(cqn, wq[...])
        kraw = _nt(ckvn, wk[...])
        for h in range(N_HEADS):
            sl = slice(h * HP, (h + 1) * HP)
            q_ref[:, sl] = jnp.where(mask_lane, 1.0, _head(qraw[:, sl], qg[...], c, s)).astype(BF)
            k_ref[:, sl] = jnp.where(mask_lane, key_mask, _head(kraw[:, sl] + kr, kg[...], c, s)).astype(BF)
        v_ref[...] = _nt(ckvn, wv[...]).astype(BF)

    rs = lambda n: _row_spec(n, TRF)
    return pl.pallas_call(
        body, grid=(R // TRF,), name="stage_a_fwd",
        in_specs=[rs(D), rs(HP), rs(HP), _const_spec((1, D)), _const_spec((PC, D)),
                  _const_spec((1, Q_LORA)), _const_spec((N_HEADS * HP, Q_LORA)), _const_spec((1, KV_LORA)),
                  _const_spec((N_HEADS * HP, KV_LORA)), _const_spec((D_ATTN, KV_LORA)), _const_spec((1, HP)),
                  _const_spec((1, HP))],
        out_specs=[rs(O_XR), rs(D_RNN), rs(D_RNN), rs(N_HEADS * HP), rs(N_HEADS * HP), rs(D_ATTN)],
        out_shape=[jax.ShapeDtypeStruct((R, O_XR), F32), jax.ShapeDtypeStruct((R, D_RNN), F32),
                   jax.ShapeDtypeStruct((R, D_RNN), F32), jax.ShapeDtypeStruct((R, N_HEADS * HP), BF),
                   jax.ShapeDtypeStruct((R, N_HEADS * HP), BF), jax.ShapeDtypeStruct((R, D_ATTN), BF)],
        compiler_params=_params("arbitrary"),
    )(hp, cs, sn, cw["ln1_g"], cw["win"], cw["qa_g"], cw["wq"], cw["kva_g"], cw["wk"], cw["wv"], cw["q_g"], cw["k_g"])


def _stage_a_bwd(dq, dk, dv, dxr, dxg, dh1, hp, pa, cs, sn, cw):
    def body(dq_ref, dk_ref, dv_ref, dxr_ref, dxg_ref, dh1_ref, hp_ref, pa_ref, cs_ref, sn_ref,
             ln1, win, qag, wq, kvag, wk, wv, qg, kg,
             dhp_ref, dp_ref, dqraw_ref, dkraw_ref, hn_ref, cqn_ref, ckvn_ref,
             dln1_ref, dqag_ref, dkvag_ref, dqg_ref, dkg_ref):
        @pl.when(pl.program_id(0) == 0)
        def _():
            for r in (dln1_ref, dqag_ref, dkvag_ref, dqg_ref, dkg_ref):
                r[...] = jnp.zeros_like(r)

        hn, vjp_ln1 = jax.vjp(lambda h, g: _rms(h, g, D), hp_ref[...], ln1[...])
        hn_ref[...] = hn.astype(BF)
        pa_v = pa_ref[...]
        cqn, vjp_qa = jax.vjp(lambda x, g: _rms(x, g, Q_LORA), pa_v[:, :O_CKV], qag[...])
        ckvn, vjp_kva = jax.vjp(lambda x, g: _rms(x, g, KV_LORA), pa_v[:, O_CKV:O_KR], kvag[...])
        kr = pa_v[:, O_KR:O_XR]
        cqnb, ckvnb = cqn.astype(BF), ckvn.astype(BF)
        cqn_ref[...] = cqnb
        ckvn_ref[...] = ckvnb
        c, s = cs_ref[...], sn_ref[...]
        lane = lax.broadcasted_iota(jnp.int32, (1, HP), 1)
        rope_lanes = ((lane >= QK_NOPE) & (lane < QK_HEAD)).astype(F32)
        dkr = jnp.zeros((TR, HP), F32)
        dqg = jnp.zeros((1, HP), F32)
        dkg = jnp.zeros((1, HP), F32)
        qraw = _nt(cqnb, wq[...])
        kraw = _nt(ckvnb, wk[...])
        for h in range(N_HEADS):
            sl = slice(h * HP, (h + 1) * HP)
            dqraw, dg = _head_bwd(qraw[:, sl], qg[...], c, s, dq_ref[:, sl])
            dqg = dqg + dg
            dqraw_ref[:, sl] = dqraw.astype(BF)
            dkraw, dg = _head_bwd(kraw[:, sl] + kr, kg[...], c, s, dk_ref[:, sl])
            dkg = dkg + dg
            dkraw_ref[:, sl] = dkraw.astype(BF)
            dkr = dkr + dkraw * rope_lanes
        dcq, dqag = vjp_qa(_nn(dqraw_ref[...], wq[...]))
        dckv, dkvag = vjp_kva(_nn(dkraw_ref[...], wk[...]) + _nn(dv_ref[...].astype(BF), wv[...]))
        dpb = jnp.concatenate([dcq, dckv, dkr, dxr_ref[...], dxg_ref[...]], axis=1).astype(BF)
        dp_ref[...] = dpb
        dh, dln1 = vjp_ln1(_nn(dpb, win[...]))
        dhp_ref[...] = dh + dh1_ref[...]
        dln1_ref[...] += dln1
        dqag_ref[...] += dqag
        dkvag_ref[...] += dkvag
        dqg_ref[...] += dqg
        dkg_ref[...] += dkg

    acc = lambda n: pl.BlockSpec((1, n), lambda i: (0, 0))
    return pl.pallas_call(
        body, grid=(R // TR,), name="stage_a_bwd",
        in_specs=[_row_spec(N_HEADS * HP), _row_spec(N_HEADS * HP), _row_spec(D_ATTN), _row_spec(D_RNN),
                  _row_spec(D_RNN), _row_spec(D), _row_spec(D), _row_spec(O_XR), _row_spec(HP), _row_spec(HP),
                  _const_spec((1, D)), _const_spec((PC, D)), _const_spec((1, Q_LORA)),
                  _const_spec((N_HEADS * HP, Q_LORA)), _const_spec((1, KV_LORA)),
                  _const_spec((N_HEADS * HP, KV_LORA)), _const_spec((D_ATTN, KV_LORA)), _const_spec((1, HP)),
                  _const_spec((1, HP))],
        out_specs=[_row_spec(D), _row_spec(PC), _row_spec(N_HEADS * HP), _row_spec(N_HEADS * HP), _row_spec(D),
                   _row_spec(Q_LORA), _row_spec(KV_LORA), acc(D), acc(Q_LORA), acc(KV_LORA), acc(HP), acc(HP)],
        out_shape=[jax.ShapeDtypeStruct((R, D), F32), jax.ShapeDtypeStruct((R, PC), BF),
                   jax.ShapeDtypeStruct((R, N_HEADS * HP), BF), jax.ShapeDtypeStruct((R, N_HEADS * HP), BF),
                   jax.ShapeDtypeStruct((R, D), BF), jax.ShapeDtypeStruct((R, Q_LORA), BF),
                   jax.ShapeDtypeStruct((R, KV_LORA), BF), jax.ShapeDtypeStruct((1, D), F32),
                   jax.ShapeDtypeStruct((1, Q_LORA), F32), jax.ShapeDtypeStruct((1, KV_LORA), F32),
                   jax.ShapeDtypeStruct((1, HP), F32), jax.ShapeDtypeStruct((1, HP), F32)],
        compiler_params=_params("arbitrary"),
    )(dq, dk, dv, dxr, dxg, dh1, hp, pa, cs, sn, cw["ln1_g"], cw["win"], cw["qa_g"], cw["wq"], cw["kva_g"],
      cw["wk"], cw["wv"], cw["q_g"], cw["k_g"])


def _head_mask(half, dtype):
    lane = lax.broadcasted_iota(jnp.int32, (1, 2 * V_HEAD), 1)
    return ((lane >= V_HEAD) == (half == 1)).astype(dtype)


def _attn_specs(tq):
    n_q = TP // tq
    return (NB, N_HEADS // 2, n_q), dict(
        q=pl.BlockSpec((tq, 2 * HP), lambda b, j, i: (b * n_q + i, j)),
        k=pl.BlockSpec((TP, 2 * HP), lambda b, j, i: (b, j)),
        v=pl.BlockSpec((TP, 2 * V_HEAD), lambda b, j, i: (b, j)),
        o=pl.BlockSpec((tq, 2 * V_HEAD), lambda b, j, i: (b * n_q + i, j)),
        lse=pl.BlockSpec((None, tq, 2), lambda b, j, i: (j, b * n_q + i, 0)))


TQF = 1088


def _attn_fwd(q, k, v):
    def body(q_ref, k_ref, v_ref, o_ref, lse_ref):
        v2 = v_ref[...]
        o = jnp.zeros((TQF, 2 * V_HEAD), F32)
        lse = []
        for hh in range(2):
            sl = slice(hh * HP, (hh + 1) * HP)
            raw = _nt(q_ref[:, sl], k_ref[:, sl])
            m = jnp.max(raw, axis=-1, keepdims=True)
            e = jnp.exp2((raw - m) * SCALE_LOG2)
            l = jnp.sum(e, axis=-1, keepdims=True)
            o = o + _nn(e.astype(BF), v2 * _head_mask(hh, BF)) * (1.0 / l)
            lse.append(m * SCALE_LOG2 + jnp.log(l) * LOG2_E)
        o_ref[...] = o
        lane = lax.broadcasted_iota(jnp.int32, (TQF, 2), 1)
        lse_ref[...] = jnp.where(lane == 0, lse[0], lse[1])

    grid, sp = _attn_specs(TQF)
    return pl.pallas_call(
        body, grid=grid, name="attn_fwd", in_specs=[sp["q"], sp["k"], sp["v"]], out_specs=[sp["o"], sp["lse"]],
        out_shape=[jax.ShapeDtypeStruct((R, D_ATTN), F32), jax.ShapeDtypeStruct((N_HEADS // 2, R, 2), F32)],
        compiler_params=_params("arbitrary", "arbitrary", "arbitrary"),
    )(q, k, v)


def _attn_bwd(q, k, v, o, lse, do):
    def body(q_ref, k_ref, v_ref, o_ref, lse_ref, do_ref, dq_ref, dk_ref, dv_ref):
        @pl.when(pl.program_id(2) == 0)
        def _():
            dk_ref[...] = jnp.zeros_like(dk_ref)
            dv_ref[...] = jnp.zeros_like(dv_ref)

        do = do_ref[...]
        dob = do.astype(BF)
        do_o = do * o_ref[...]
        v2 = v_ref[...]
        dv_sum = jnp.zeros((TP, 2 * V_HEAD), F32)
        for hh in range(2):
            sl = slice(hh * HP, (hh + 1) * HP)
            qb, kb = q_ref[:, sl], k_ref[:, sl]
            p = jnp.exp2(_nt(qb, kb) * SCALE_LOG2 - lse_ref[:, hh:hh + 1])
            dp = _nt(dob, v2 * _head_mask(hh, BF))
            delta = jnp.sum(do_o * _head_mask(hh, F32), axis=-1, keepdims=True)
            dsb = (p * (dp - delta) * SCALE).astype(BF)
            dq_ref[:, sl] = _nn(dsb, kb)
            dk_ref[:, sl] += _tn(dsb, qb)
            dv_sum = dv_sum + _tn(p.astype(BF), dob) * _head_mask(hh, F32)
        dv_ref[...] += dv_sum

    grid, sp = _attn_specs(TQ)
    return pl.pallas_call(
        body, grid=grid, name="attn_bwd", in_specs=[sp["q"], sp["k"], sp["v"], sp["o"], sp["lse"], sp["o"]],
        out_specs=[sp["q"], sp["k"], sp["v"]],
        out_shape=[jax.ShapeDtypeStruct((R, N_HEADS * HP), F32), jax.ShapeDtypeStruct((R, N_HEADS * HP), F32),
                   jax.ShapeDtypeStruct((R, D_ATTN), F32)],
        compiler_params=_params("arbitrary", "arbitrary", "arbitrary"),
    )(q, k, v, o, lse, do)


def _tile_prefix(a_ref, b_ref, reverse):
    tiles = (TP // 8, 8, CG)
    r8 = lax.broadcasted_iota(jnp.int32, tiles, 1)
    a, b = a_ref[...].reshape(tiles), b_ref[...].reshape(tiles)
    for s in (1, 2, 4):
        shift = 8 - s if reverse else s
        keep = (r8 < 8 - s) if reverse else (r8 >= s)
        b = jnp.where(keep, a * pltpu.roll(b, shift, 1) + b, b)
        a = jnp.where(keep, a * pltpu.roll(a, shift, 1), a)
    a_ref[...] = a.reshape(TP, CG)
    b_ref[...] = b.reshape(TP, CG)


def _scan_pair(af_ref, bf_ref, hf_ref, ab_ref, bb_ref, hb_ref):
    _tile_prefix(af_ref, bf_ref, False)
    _tile_prefix(ab_ref, bb_ref, True)
    n_tiles = TP // 8

    def step(i, carry):
        cf, cb = carry
        rf = pl.multiple_of(i * 8, 8)
        rb = pl.multiple_of((n_tiles - 1 - i) * 8, 8)
        hf_ref[pl.ds(rf, 8), :] = bf_ref[pl.ds(rf, 8), :] + af_ref[pl.ds(rf, 8), :] * cf
        hb_ref[pl.ds(rb, 8), :] = bb_ref[pl.ds(rb, 8), :] + ab_ref[pl.ds(rb, 8), :] * cb
        cf = bf_ref[pl.ds(rf + 7, 1), :] + af_ref[pl.ds(rf + 7, 1), :] * cf
        cb = bb_ref[pl.ds(rb, 1), :] + ab_ref[pl.ds(rb, 1), :] * cb
        return cf, cb

    zero = jnp.zeros((1, CG), F32)
    lax.fori_loop(0, n_tiles, step, (zero, zero), unroll=8)


def _shifts(x):
    t = lax.broadcasted_iota(jnp.int32, x.shape, 0)
    xm2 = jnp.where(t >= 2, pltpu.roll(x, 2, 0), 0.0)
    xm1 = jnp.where(t >= 1, pltpu.roll(x, 1, 0), 0.0)
    xp1 = jnp.where(t < TP - 1, pltpu.roll(x, TP - 1, 0), 0.0)
    return xm2, xm1, xp1


def _softplus(z):
    e = jnp.exp(-jnp.abs(z))
    small = e * (1.0 - e * (0.5 - e * (1.0 / 3.0)))
    return jnp.maximum(z, 0.0) + jnp.where(e < 0.01, small, jnp.log(1.0 + e))


def _sigmoid(x):
    return 0.5 * jnp.tanh(0.5 * x) + 0.5


def _one_minus_sq(log_a, a):
    x = 2.0 * log_a
    series = -x * (1.0 + x * 0.5 * (1.0 + x * (1.0 / 3.0) * (1.0 + x * 0.25)))
    return jnp.where(x > -0.05, series, 1.0 - a * a)


def _gates(row0, xc, pa_f, pi_f, pa_b, pi_b, lam_f, lam_b):
    t = row0 + lax.broadcasted_iota(jnp.int32, xc.shape, 0)
    valid = t < T
    out = []
    for pa, pi_, lam in ((pa_f, pi_f, lam_f), (pa_b, pi_b, lam_b)):
        r = _sigmoid(pa)
        gate_i = _sigmoid(pi_)
        log_a = -LRU_C * r * _softplus(-lam)
        a = jnp.exp(log_a)
        mult = jnp.sqrt(jnp.maximum(_one_minus_sq(log_a, a), 0.0))
        out += [a, jnp.where(valid, mult * (gate_i * xc), 0.0)]
    return tuple(out)


def _gates_bwd(row0, xc, pres, lams, cots):
    t = row0 + lax.broadcasted_iota(jnp.int32, xc.shape, 0)
    valid = t < T
    dxc = jnp.zeros_like(xc)
    dpres, dlams = [], []
    for d in range(2):
        pa, pi_, lam = pres[2 * d], pres[2 * d + 1], lams[d]
        da, db = cots[2 * d], jnp.where(valid, cots[2 * d + 1], 0.0)
        r = _sigmoid(pa)
        gate_i = _sigmoid(pi_)
        sp = _softplus(-lam)
        log_a = -LRU_C * r * sp
        a = jnp.exp(log_a)
        m2 = jnp.maximum(_one_minus_sq(log_a, a), 0.0)
        mult = jnp.sqrt(m2)
        dxc = dxc + db * (mult * gate_i)
        d_gate = db * (mult * xc)
        d_m2 = jnp.where(m2 > 0.0, db * (gate_i * xc) * (0.5 * lax.rsqrt(m2)), 0.0)
        d_log_a = da * a - 2.0 * d_m2 * (a * a)
        dpres += [d_log_a * (-LRU_C * sp) * (r * (1.0 - r)), d_gate * (gate_i * (1.0 - gate_i))]
        d_sp = jnp.sum(d_log_a * (-LRU_C * r), axis=0, keepdims=True)
        dlams.append(-d_sp * jax.nn.sigmoid(-lam))
    return dxc, dpres, dlams


def _rnn_specs():
    seq = pl.BlockSpec((TP, CG), lambda g, b: (b, g))
    return dict(
        seq=seq,
        cw=pl.BlockSpec((CONV_W, CG), lambda g, b: (0, g)),
        cb=pl.BlockSpec((1, CG), lambda g, b: (0, g)),
        w4=pl.BlockSpec((None, CG, 4 * CG), lambda g, b: (g, 0, 0)),
        b4=pl.BlockSpec((None, 1, 4 * CG), lambda g, b: (g, 0, 0)),
        lam=pl.BlockSpec((None, 1, 2 * CG), lambda g, b: (g, 0, 0)),
    )


def _conv(x, xm2, xm1, xp1, cw_ref, cb_ref):
    return cw_ref[0:1, :] * xm2 + cw_ref[1:2, :] * xm1 + cw_ref[2:3, :] * x + cw_ref[3:4, :] * xp1 + cb_ref[...]


TC = 128
N_TC = TP // TC


def _split4(pre):
    return pre[:, :CG], pre[:, CG:2 * CG], pre[:, 2 * CG:3 * CG], pre[:, 3 * CG:]


def _rnn_fwd(xr, xg, cw):
    def body(xr_ref, xg_ref, cw_ref, cb_ref, w4_ref, b4_ref, lam_ref, y_ref, hf_ref, hb_ref, af_ref, ab_ref, xc_ref,
             af, bf, ab, bb):
        x = xr_ref[...]
        xc_ref[...] = _conv(x, *_shifts(x), cw_ref, cb_ref)
        lam = lam_ref[...]

        def chunk(i, _):
            rows = pl.ds(pl.multiple_of(i * TC, TC), TC)
            xc = xc_ref[rows, :]
            pre = _nn(xc.astype(BF), w4_ref[...]) + b4_ref[...]
            a_f, b_f, a_b, b_b = _gates(i * TC, xc, *_split4(pre), lam[:, :CG], lam[:, CG:])
            af[rows, :] = a_f
            bf[rows, :] = b_f
            ab[rows, :] = a_b
            bb[rows, :] = b_b
            af_ref[rows, :] = a_f
            ab_ref[rows, :] = a_b
            return 0

        lax.fori_loop(0, N_TC, chunk, 0)
        _scan_pair(af, bf, hf_ref, ab, bb, hb_ref)
        y_ref[...] = (hf_ref[...] + hb_ref[...]) * jax.nn.gelu(xg_ref[...])

    sp = _rnn_specs()
    return pl.pallas_call(
        body, grid=(N_CG, NB), name="rnn_fwd",
        in_specs=[sp["seq"], sp["seq"], sp["cw"], sp["cb"], sp["w4"], sp["b4"], sp["lam"]],
        out_specs=[sp["seq"]] * 6, out_shape=[jax.ShapeDtypeStruct((R, D_RNN), F32)] * 6,
        scratch_shapes=[pltpu.VMEM((TP, CG), F32)] * 4,
        compiler_params=_params("arbitrary", "arbitrary"),
    )(xr, xg, cw["conv_w"], cw["conv_b"], cw["w4"], cw["b4"], cw["lam"])


def _rnn_bwd(dy, xr, xg, hf, hb, af, ab, xc, cw):
    def body(dy_ref, xr_ref, xg_ref, hf_ref, hb_ref, af_ref, ab_ref, xc_s, cw_ref, cb_ref, w4_ref, b4_ref, lam_ref,
             dxr_ref, dxg_ref, dcw_ref, dcb_ref, dw4_ref, db4_ref, dlam_ref,
             af_s, ab_s, dhs_s, dhs2_s, lf_s, lb_s, daf_s, dab_s, dxc_s):
        @pl.when(pl.program_id(1) == 0)
        def _():
            for r in (dcw_ref, dcb_ref, dw4_ref, db4_ref, dlam_ref):
                r[...] = jnp.zeros_like(r)

        lam = lam_ref[...]

        def chunk1(i, _):
            rows = pl.ds(pl.multiple_of(i * TC, TC), TC)
            _, vjp_y = jax.vjp(lambda h, g: h * jax.nn.gelu(g), hf_ref[rows, :] + hb_ref[rows, :], xg_ref[rows, :])
            dhs, dxg = vjp_y(dy_ref[rows, :])
            dhs_s[rows, :] = dhs
            dhs2_s[rows, :] = dhs
            dxg_ref[rows, :] = dxg
            return 0

        lax.fori_loop(0, N_TC, chunk1, 0)
        t = lax.broadcasted_iota(jnp.int32, (TP, CG), 0)
        af_s[...] = pltpu.roll(af_ref[...], TP - 1, 0)
        ab_s[...] = pltpu.roll(ab_ref[...], 1, 0)
        _scan_pair(ab_s, dhs_s, lb_s, af_s, dhs2_s, lf_s)
        daf_s[...] = lf_s[...] * jnp.where(t >= 1, pltpu.roll(hf_ref[...], 1, 0), 0.0)
        dab_s[...] = lb_s[...] * jnp.where(t < TP - 1, pltpu.roll(hb_ref[...], TP - 1, 0), 0.0)

        def chunk2(i, _):
            rows = pl.ds(pl.multiple_of(i * TC, TC), TC)
            xc = xc_s[rows, :]
            xcb = xc.astype(BF)
            pre = _nn(xcb, w4_ref[...]) + b4_ref[...]
            dxc, dpres, dlams = _gates_bwd(i * TC, xc, _split4(pre), (lam[:, :CG], lam[:, CG:]),
                                           (daf_s[rows, :], lf_s[rows, :], dab_s[rows, :], lb_s[rows, :]))
            dpre = jnp.concatenate(dpres, axis=1)
            dpreb = dpre.astype(BF)
            dxc_s[rows, :] = dxc + _nt(dpreb, w4_ref[...])
            dw4_ref[...] += _tn(xcb, dpreb)
            db4_ref[...] += jnp.sum(dpre, axis=0, keepdims=True)
            dlam_ref[...] += jnp.concatenate(dlams, axis=1)
            return 0

        lax.fori_loop(0, N_TC, chunk2, 0)
        dxc = dxc_s[...]
        x = xr_ref[...]
        taps = (jnp.where(t < TP - 2, pltpu.roll(dxc, TP - 2, 0), 0.0), jnp.where(t < TP - 1, pltpu.roll(dxc, TP - 1, 0), 0.0),
                dxc, jnp.where(t >= 1, pltpu.roll(dxc, 1, 0), 0.0))
        dcb_ref[...] += jnp.sum(dxc, axis=0, keepdims=True)
        dxr = jnp.zeros_like(dxc)
        for tap, shifted in enumerate(taps):
            dcw_ref[tap:tap + 1, :] += jnp.sum(x * shifted, axis=0, keepdims=True)
            dxr = dxr + cw_ref[tap:tap + 1, :] * shifted
        dxr_ref[...] = dxr

    sp = _rnn_specs()
    return pl.pallas_call(
        body, grid=(N_CG, NB), name="rnn_bwd",
        in_specs=[sp["seq"]] * 8 + [sp["cw"], sp["cb"], sp["w4"], sp["b4"], sp["lam"]],
        out_specs=[sp["seq"], sp["seq"], sp["cw"], sp["cb"], sp["w4"], sp["b4"], sp["lam"]],
        out_shape=[jax.ShapeDtypeStruct((R, D_RNN), F32), jax.ShapeDtypeStruct((R, D_RNN), F32),
                   jax.ShapeDtypeStruct((CONV_W, D_RNN), F32), jax.ShapeDtypeStruct((1, D_RNN), F32),
                   jax.ShapeDtypeStruct((N_CG, CG, 4 * CG), F32), jax.ShapeDtypeStruct((N_CG, 1, 4 * CG), F32),
                   jax.ShapeDtypeStruct((N_CG, 1, 2 * CG), F32)],
        scratch_shapes=[pltpu.VMEM((TP, CG), F32)] * 9,
        compiler_params=_params("arbitrary", "arbitrary"),
    )(dy, xr, xg, hf, hb, af, ab, xc, cw["conv_w"], cw["conv_b"], cw["w4"], cw["b4"], cw["lam"])


TD = 256
STAGE_D_VMEM = 58 * 1024 * 1024


def _stage_d(hp, o, y, tgt, cw):
    def body(hp_ref, o_ref, y_ref, tgt_ref, ga, gr, wout, ln2, wg, wu, wd,
             do_ref, dy_ref, dh1_ref, mix_ref, dh1b_ref, hn2_ref, dg_ref, du_ref, act_ref, dh2b_ref,
             loss_ref, dga_ref, dgr_ref, dln2_ref):
        i = pl.program_id(0)

        @pl.when(i == 0)
        def _():
            for r in (loss_ref, dga_ref, dgr_ref, dln2_ref):
                r[...] = jnp.zeros_like(r)

        mix_a, vjp_a = jax.vjp(lambda x, g: _rms(x, g, D_ATTN), o_ref[...], ga[...])
        mix_r, vjp_r = jax.vjp(lambda x, g: _rms(x, g, D_RNN), y_ref[...], gr[...])
        mab, mrb = mix_a.astype(BF), mix_r.astype(BF)
        mix_ref[:, :D_ATTN] = mab
        mix_ref[:, D_ATTN:] = mrb
        h1 = hp_ref[...] + _nn(mab, wout[:D_ATTN, :]) + _nn(mrb, wout[D_ATTN:, :])
        hn2, vjp_ln2 = jax.vjp(lambda x, g: _rms(x, g, D), h1, ln2[...])
        hn2b = hn2.astype(BF)
        hn2_ref[...] = hn2b
        act, vjp_act = jax.vjp(lambda g, u: jax.nn.silu(g) * u, _nt(hn2b, wg[...]), _nt(hn2b, wu[...]))
        actb = act.astype(BF)
        act_ref[...] = actb
        h2 = h1 + _nn(actb, wd[...])
        row = i * TD + lax.broadcasted_iota(jnp.int32, (TD, 1), 0)
        t = jnp.where(row >= TP, row - TP, row)
        err = jnp.where((t >= N_META) & (t < T), h2 - tgt_ref[...], 0.0)
        loss_ref[...] += jnp.sum(err * err) * (0.5 / D)
        dh2b = (err * (1.0 / D)).astype(BF)
        dh2b_ref[...] = dh2b
        dg, du = vjp_act(_nt(dh2b, wd[...]))
        dgb, dub = dg.astype(BF), du.astype(BF)
        dg_ref[...] = dgb
        du_ref[...] = dub
        dh1n, dln2 = vjp_ln2(_nn(dgb, wg[...]) + _nn(dub, wu[...]))
        dh1 = err * (1.0 / D) + dh1n
        dh1_ref[...] = dh1
        dh1b = dh1.astype(BF)
        dh1b_ref[...] = dh1b
        dmix = _nt(dh1b, wout[...])
        do, dga = vjp_a(dmix[:, :D_ATTN])
        dyr, dgr = vjp_r(dmix[:, D_ATTN:])
        do_ref[...] = do
        dy_ref[...] = dyr
        dga_ref[...] += dga
        dgr_ref[...] += dgr
        dln2_ref[...] += dln2

    rs = lambda n: _row_spec(n, TD)
    acc = lambda n: pl.BlockSpec((1, n), lambda i: (0, 0))
    return pl.pallas_call(
        body, grid=(R // TD,), name="stage_d",
        in_specs=[rs(D), rs(D_ATTN), rs(D_RNN), rs(D), _const_spec((1, D_ATTN)), _const_spec((1, D_RNN)),
                  _const_spec((D, D)), _const_spec((1, D)), _const_spec((D_FF, D)), _const_spec((D_FF, D)),
                  _const_spec((D_FF, D))],
        out_specs=[rs(D_ATTN), rs(D_RNN), rs(D), rs(D), rs(D), rs(D), rs(D_FF), rs(D_FF), rs(D_FF), rs(D),
                   acc(1), acc(D_ATTN), acc(D_RNN), acc(D)],
        out_shape=[jax.ShapeDtypeStruct((R, D_ATTN), F32), jax.ShapeDtypeStruct((R, D_RNN), F32),
                   jax.ShapeDtypeStruct((R, D), F32), jax.ShapeDtypeStruct((R, D), BF),
                   jax.ShapeDtypeStruct((R, D), BF), jax.ShapeDtypeStruct((R, D), BF),
                   jax.ShapeDtypeStruct((R, D_FF), BF), jax.ShapeDtypeStruct((R, D_FF), BF),
                   jax.ShapeDtypeStruct((R, D_FF), BF), jax.ShapeDtypeStruct((R, D), BF),
                   jax.ShapeDtypeStruct((1, 1), F32), jax.ShapeDtypeStruct((1, D_ATTN), F32),
                   jax.ShapeDtypeStruct((1, D_RNN), F32), jax.ShapeDtypeStruct((1, D), F32)],
        compiler_params=_params("arbitrary", vmem=STAGE_D_VMEM),
    )(hp, o, y, tgt, cw["ga"], cw["gr"], cw["wout"], cw["ln2_g"], cw["wg"], cw["wu"], cw["wd"])


TW = 2176


def _wgrad(a, b, name, tk=None):
    ka, nb = a.shape[1], b.shape[1]
    tk = ka if tk is None else tk

    def body(a_ref, b_ref, o_ref):
        @pl.when(pl.program_id(1) == 0)
        def _():
            o_ref[...] = jnp.zeros_like(o_ref)

        o_ref[...] += _tn(a_ref[...].astype(BF), b_ref[...].astype(BF))

    return pl.pallas_call(
        body, grid=(ka // tk, R // TW), name=name,
        in_specs=[pl.BlockSpec((TW, tk), lambda k, r: (r, k)), pl.BlockSpec((TW, nb), lambda k, r: (r, 0))],
        out_specs=pl.BlockSpec((tk, nb), lambda k, r: (k, 0)),
        out_shape=jax.ShapeDtypeStruct((ka, nb), F32),
        compiler_params=_params("arbitrary", "arbitrary"),
    )(a, b)


def _wgrad_heads(dq, dk, dv, cqn, ckvn):
    def body(dq_ref, dk_ref, dv_ref, cqn_ref, ckvn_ref, oq_ref, ok_ref, ov_ref):
        @pl.when(pl.program_id(0) == 0)
        def _():
            for r in (oq_ref, ok_ref, ov_ref):
                r[...] = jnp.zeros_like(r)

        ckvnb = ckvn_ref[...]
        oq_ref[...] += _tn(dq_ref[...], cqn_ref[...])
        ok_ref[...] += _tn(dk_ref[...], ckvnb)
        ov_ref[...] += _tn(dv_ref[...].astype(BF), ckvnb)

    rows = lambda a: pl.BlockSpec((TW, a.shape[1]), lambda r: (r, 0))
    full = lambda m, n: pl.BlockSpec((m, n), lambda r: (0, 0))
    shapes = [(dq.shape[1], cqn.shape[1]), (dk.shape[1], ckvn.shape[1]), (dv.shape[1], ckvn.shape[1])]
    return pl.pallas_call(
        body, grid=(R // TW,), name="wgrad_heads", in_specs=[rows(a) for a in (dq, dk, dv, cqn, ckvn)],
        out_specs=[full(*s) for s in shapes], out_shape=[jax.ShapeDtypeStruct(s, F32) for s in shapes],
        compiler_params=_params("arbitrary"),
    )(dq, dk, dv, cqn, ckvn)


def _rope_tables():
    half = QK_ROPE // 2
    freqs = 1.0 / (ROPE_THETA ** (jnp.arange(half, dtype=F32) / half))
    ang = jnp.arange(TP, dtype=F32)[:, None] * freqs[None, :]
    ones = jnp.ones((TP, QK_NOPE), F32)
    zeros = jnp.zeros((TP, QK_NOPE), F32)
    pad1 = jnp.ones((TP, HP - QK_HEAD), F32)
    pad0 = jnp.zeros((TP, HP - QK_HEAD), F32)
    cs = jnp.concatenate([ones, jnp.cos(ang), jnp.cos(ang), pad1], axis=1)
    sn = jnp.concatenate([zeros, jnp.sin(ang), jnp.sin(ang), pad0], axis=1)
    return jnp.tile(cs, (NB, 1)), jnp.tile(sn, (NB, 1))


def _pad_rows(a, lo, hi):
    return jnp.pad(a, ((0, 0), (lo, hi), (0, 0)))


def _pad_target(target):
    return _pad_rows(target, N_META, TP - T).reshape(R, D)


def _compute_weights(w):
    win_t = w["w_in_t"]
    kr = win_t[O_KR:O_KR + QK_ROPE]
    win = jnp.concatenate([win_t[:O_KR], jnp.zeros((QK_NOPE, D), F32), kr,
                           jnp.zeros((HP - QK_HEAD, D), F32), win_t[O_KR + QK_ROPE:]], axis=0)
    wq = _pad_rows(w["w_uq_t"].reshape(N_HEADS, QK_HEAD, Q_LORA), 0, HP - QK_HEAD)
    wkv = w["w_ukv_t"].reshape(N_HEADS, QK_NOPE + V_HEAD, KV_LORA)
    wk = _pad_rows(wkv[:, :QK_NOPE], 0, HP - QK_NOPE)
    wv = wkv[:, QK_NOPE:].reshape(D_ATTN, KV_LORA)
    gates = jnp.stack([w["lru_wa"][0], w["lru_wi"][0], w["lru_wa"][1], w["lru_wi"][1]])
    blk = gates.reshape(4, N_CG, 2, RNN_BW, RNN_BW)
    dense = jnp.einsum("tcaij,ab->tcaibj", blk, jnp.eye(2, dtype=F32)).reshape(4, N_CG, CG, CG)
    w4 = dense.transpose(1, 2, 0, 3).reshape(N_CG, CG, 4 * CG)
    bias = jnp.stack([w["lru_ba"][0], w["lru_bi"][0], w["lru_ba"][1], w["lru_bi"][1]])
    b4 = bias.reshape(4, N_CG, CG).transpose(1, 0, 2).reshape(N_CG, 1, 4 * CG)
    lam = w["lru_lambda"].reshape(2, N_CG, CG).transpose(1, 0, 2).reshape(N_CG, 1, 2 * CG)
    pad_g = lambda g: jnp.pad(g.reshape(1, QK_HEAD), ((0, 0), (0, HP - QK_HEAD)))
    return dict(
        ln1_g=w["ln1_g"].reshape(1, D), win=win.astype(BF), qa_g=w["q_a_norm_g"].reshape(1, Q_LORA),
        wq=wq.astype(BF).reshape(N_HEADS * HP, Q_LORA), kva_g=w["kv_a_norm_g"].reshape(1, KV_LORA),
        wk=wk.astype(BF).reshape(N_HEADS * HP, KV_LORA), wv=wv.astype(BF),
        q_g=pad_g(w["q_norm_g"]), k_g=pad_g(w["k_norm_g"]),
        conv_w=w["conv_w"].reshape(CONV_W, D_RNN), conv_b=w["conv_b"].reshape(1, D_RNN),
        w4=w4.astype(BF), b4=b4, lam=lam,
        ga=w["attn_out_g"].reshape(1, D_ATTN), gr=w["rnn_out_g"].reshape(1, D_RNN), ln2_g=w["ln2_g"].reshape(1, D),
    )


def _local_step(x, target, meta, w, late_weights, early_grads, mid_grads):
    cw = _compute_weights(w)
    cs, sn = _rope_tables()
    hp = jnp.concatenate([jnp.broadcast_to(meta[None], (NB, N_META, D)), x,
                          jnp.zeros((NB, TP - T, D), F32)], axis=1).reshape(R, D)
    tgt = target if target.ndim == 2 else _pad_target(target)

    pa, xr, xg, q, k, v = _stage_a_fwd(hp, cs, sn, cw)
    o, lse = _attn_fwd(q, k, v)
    y, hf, hb, af, ab, xc = _rnn_fwd(xr, xg, cw)
    late = late_weights([o, y])
    cw.update(wout=late["w_out"], wg=late["w_gate_t"], wu=late["w_up_t"], wd=late["w_down"])
    (do, dy, dh1, mixb, dh1b, hn2b, dgb, dub, actb, dh2b, loss, dga, dgr, dln2) = _stage_d(hp, o, y, tgt, cw)
    dwout = _wgrad(mixb, dh1b, "wgrad_out")
    dwg = _wgrad(dgb, hn2b, "wgrad_gate", tk=D_FF // 2)
    dwu = _wgrad(dub, hn2b, "wgrad_up", tk=D_FF // 2)
    dwd = _wgrad(actb, dh2b, "wgrad_down", tk=D_FF // 2)
    zero = early_grads(dict(w_out=dwout, w_gate=dwg, w_up=dwu, w_down=dwd))
    cw["conv_b"] = cw["conv_b"] + zero
    dxr, dxg, dcw, dcb, dw4, db4, dlam = _rnn_bwd(dy, xr, xg, hf, hb, af, ab, xc, cw)
    zero = mid_grads([dxr])
    dq, dk, dv = _attn_bwd(q, k, v, o, lse, do)
    (dhp, dpb, dqrawb, dkrawb, hn1b, cqnb, ckvnb, dln1, dqag, dkvag, dqg, dkg) = _stage_a_bwd(
        dq, dk, dv, dxr, dxg, dh1, hp, pa, cs, sn, dict(cw, qa_g=cw["qa_g"] + zero))

    dwin = _wgrad(dpb, hn1b, "wgrad_in", tk=PC // 2)
    dwq, dwk, dwv = _wgrad_heads(dqrawb, dkrawb, dv, cqnb, ckvnb)

    dwin_t = jnp.concatenate([dwin[:O_KR], dwin[O_KR + QK_NOPE:O_KR + QK_HEAD], dwin[O_XR:]], axis=0)
    dwq_t = dwq.reshape(N_HEADS, HP, Q_LORA)[:, :QK_HEAD].reshape(N_HEADS * QK_HEAD, Q_LORA)
    dwkv_t = jnp.concatenate([dwk.reshape(N_HEADS, HP, KV_LORA)[:, :QK_NOPE],
                              dwv.reshape(N_HEADS, V_HEAD, KV_LORA)], axis=1).reshape(2 * D_ATTN, KV_LORA)
    d4 = dw4.reshape(N_CG, 2, RNN_BW, 4, 2, RNN_BW)
    dgates = jnp.stack([d4[:, 0, :, :, 0, :], d4[:, 1, :, :, 1, :]], axis=1)
    dgates = dgates.transpose(3, 0, 1, 2, 4).reshape(4, N_HEADS, RNN_BW, RNN_BW)
    dbias = db4.reshape(N_CG, 4, CG).transpose(1, 0, 2).reshape(4, D_RNN)
    dhp3 = dhp.reshape(NB, TP, D)
    grads = dict(
        meta_tokens=jnp.sum(dhp3[:, :N_META], axis=0),
        ln1_g=dln1, w_in_t=dwin_t, q_a_norm_g=dqag, w_uq_t=dwq_t, kv_a_norm_g=dkvag, w_ukv_t=dwkv_t,
        q_norm_g=dqg[:, :QK_HEAD], k_norm_g=dkg[:, :QK_HEAD], conv_w=dcw[None], conv_b=dcb,
        lru_wa=jnp.stack([dgates[0], dgates[2]])[None], lru_ba=jnp.stack([dbias[0], dbias[2]])[None],
        lru_wi=jnp.stack([dgates[1], dgates[3]])[None], lru_bi=jnp.stack([dbias[1], dbias[3]])[None],
        lru_lambda=dlam.reshape(N_CG, 2, CG).transpose(1, 0, 2).reshape(1, 2, D_RNN),
        attn_out_g=dga, rnn_out_g=dgr, ln2_g=dln2,
    )
    return loss[0, 0], dhp3[:, N_META:T], grads, [dhp, dwin]


_ANY = pl.BlockSpec(memory_space=pl.ANY)


def _place():
    return lax.axis_index("x"), lax.axis_index("y"), lax.axis_index("c")


def _other_chips(x, y):
    return [(1 - x, y), (x, 1 - y), (1 - x, 1 - y)]


def _pair_exchange(big, whole, name):
    n_s, _, m, n = big.shape
    n_copies = n_s + len(whole)

    def body(*refs):
        big_ref, whole_refs = refs[0], refs[1:1 + len(whole)]
        rbig_ref, rwhole_refs = refs[1 + len(whole)], refs[2 + len(whole):2 + 2 * len(whole)]
        send_sems, recv_sems = refs[-2:]
        x, y, c = _place()
        sibling = (x, y, 1 - c)
        copies = [pltpu.make_async_remote_copy(
            src_ref=big_ref.at[s, 1 - c], dst_ref=rbig_ref.at[s], send_sem=send_sems.at[s], recv_sem=recv_sems.at[s],
            device_id=sibling, device_id_type=MESH) for s in range(n_s)]
        copies += [pltpu.make_async_remote_copy(
            src_ref=a, dst_ref=r, send_sem=send_sems.at[n_s + i], recv_sem=recv_sems.at[n_s + i],
            device_id=sibling, device_id_type=MESH) for i, (a, r) in enumerate(zip(whole_refs, rwhole_refs))]
        for cp in copies:
            cp.start()
        for cp in copies:
            cp.wait()

    return pl.pallas_call(
        body, name=name,
        out_shape=[jax.ShapeDtypeStruct((n_s, m, n), big.dtype)] + [jax.ShapeDtypeStruct(a.shape, a.dtype) for a in whole],
        in_specs=[_ANY] * (1 + len(whole)), out_specs=[_ANY] * (1 + len(whole)),
        scratch_shapes=[pltpu.SemaphoreType.DMA((n_copies,)), pltpu.SemaphoreType.DMA((n_copies,))],
    )(big, *whole)


_HBM = pl.BlockSpec(memory_space=pltpu.HBM)
_SEM = pl.BlockSpec(memory_space=pltpu.SEMAPHORE)
_EFFECT = pltpu.SideEffectType.DATAFLOW_SIDE_EFFECTING


def _split_copies(src_refs, land_refs, sems, plan, sending):
    n = len(sems) // 2
    return [pltpu.make_async_remote_copy(src_ref=s, dst_ref=d, send_sem=sems[k], recv_sem=sems[n + k], device_id=to,
                                         device_id_type=MESH)
            for k, (s, d, to) in enumerate(plan(src_refs, land_refs, sending))]


def _to_chips(src_at, land_at):
    def plan(src_refs, land_refs, sending):
        x, y, c = _place()
        return [(src_at(s, tx, ty, c), land_at(l, j, *((x, y) if sending else (tx, ty)), c), (tx, ty, c))
                for s, l in zip(src_refs, land_refs) for j, (tx, ty) in enumerate(_other_chips(x, y))]
    return plan


def _to_sibling(src_refs, land_refs, sending):
    x, y, c = _place()
    return [(s.at[k, 1 - c], l.at[k], (x, y, 1 - c)) for s, l in zip(src_refs, land_refs) for k in range(N_CHIPS)]


def _split_start(name, srcs, lands, plan, n, after=()):
    srcs, lands, after = list(srcs), list(lands), list(after)
    k = len(srcs)

    def body(*refs):
        outs = refs[2 * k + len(after):]
        for cp in _split_copies(refs[:k], refs[k:2 * k], outs[:2 * n], plan, True):
            cp.start()
        outs[2 * n + 2 * k][...] = jnp.zeros_like(outs[2 * n + 2 * k])

    outs = pl.pallas_call(
        body, name=name,
        out_shape=(pltpu.SemaphoreType.DMA(()),) * (2 * n) + tuple(pltpu.HBM(a.shape, a.dtype) for a in srcs + lands)
        + (jax.ShapeDtypeStruct((8, LANES), F32),),
        in_specs=(_HBM,) * (2 * k) + (_ANY,) * len(after),
        out_specs=(_SEM,) * (2 * n) + (_HBM,) * (2 * k) + (pl.BlockSpec(memory_space=pltpu.VMEM),),
        input_output_aliases={i: 2 * n + i for i in range(2 * k)},
        compiler_params=pltpu.CompilerParams(has_side_effects=_EFFECT),
    )(*[pltpu.with_memory_space_constraint(a, pltpu.HBM) for a in srcs + lands], *after)
    return outs[:2 * n], list(outs[2 * n:2 * n + k]), list(outs[2 * n + k:2 * n + 2 * k]), outs[2 * n + 2 * k]


def _split_wait(name, sems, srcs, lands, after, plan):
    srcs, lands = list(srcs), list(lands)
    k = len(srcs)

    def body(*refs):
        for cp in _split_copies(refs[:k], refs[k:2 * k], refs[2 * k:2 * k + len(sems)], plan, False):
            cp.wait_send()
            cp.wait_recv()

    outs = pl.pallas_call(
        body, name=name, out_shape=tuple(pltpu.HBM(a.shape, a.dtype) for a in srcs + lands),
        in_specs=(_HBM,) * (2 * k) + (_SEM,) * len(sems) + (_ANY,) * len(after), out_specs=(_HBM,) * (2 * k),
        input_output_aliases={i: i for i in range(2 * k)}, compiler_params=pltpu.CompilerParams(has_side_effects=_EFFECT),
    )(*srcs, *lands, *sems, *after)
    return list(outs[:k]), list(outs[k:])


def _gather_finish(lands, pieces, name):
    k = len(lands)

    def body(*refs):
        land_refs, piece_refs, out_refs, stages = refs[:k], refs[k:2 * k], refs[2 * k:3 * k], refs[3 * k:4 * k]
        send_sems, recv_sems, load_sems, store_sems = refs[4 * k:]
        x, y, c = _place()
        sibling = (x, y, 1 - c)
        remote, loads, stores, arrivals = [], [], [], []
        for a in range(k):
            m = lands[a].shape[0] // 8

            def rows(px, py, pc, ref, m=m):
                return ref.at[pl.ds((4 * px + 2 * py + pc) * m, m), :]

            for j, (tx, ty) in enumerate(_other_chips(x, y)):
                sems = dict(send_sem=send_sems.at[3 * a + j], recv_sem=recv_sems.at[3 * a + j], device_id=sibling,
                            device_id_type=MESH)
                remote.append(pltpu.make_async_remote_copy(
                    src_ref=rows(tx, ty, c, land_refs[a]), dst_ref=rows(tx, ty, c, out_refs[a]), **sems))
                arrivals.append(pltpu.make_async_remote_copy(
                    src_ref=rows(tx, ty, 1 - c, out_refs[a]), dst_ref=rows(tx, ty, 1 - c, out_refs[a]), **sems))
            for h in range(2):
                loads.append(pltpu.make_async_copy(piece_refs[a].at[pl.ds(h * m, m), :], stages[a].at[h],
                                                   load_sems.at[2 * a + h]))
                stores.append(pltpu.make_async_copy(stages[a].at[h], rows(x, y, h, out_refs[a]), store_sems.at[2 * a + h]))
        for cp in remote + loads:
            cp.start()
        for ld, st in zip(loads, stores):
            ld.wait()
            st.start()
        for cp, arrival in zip(remote, arrivals):
            cp.wait_send()
            arrival.wait_recv()
        for cp in stores:
            cp.wait()

    return pl.pallas_call(
        body, name=name, out_shape=[jax.ShapeDtypeStruct(a.shape, a.dtype) for a in lands],
        in_specs=[_ANY] * (2 * k), out_specs=[_ANY] * k, input_output_aliases={i: i for i in range(k)},
        scratch_shapes=[pltpu.VMEM((2, a.shape[0] // 8, a.shape[1]), a.dtype) for a in lands]
        + [pltpu.SemaphoreType.DMA((3 * k,)), pltpu.SemaphoreType.DMA((3 * k,)), pltpu.SemaphoreType.DMA((2 * k,)),
           pltpu.SemaphoreType.DMA((2 * k,))],
    )(*lands, *pieces)


def _pair_fill(bufs, name):
    k = len(bufs)

    def body(*refs):
        send_sems, recv_sems = refs[-2:]
        x, y, c = _place()
        copies = [pltpu.make_async_remote_copy(
            src_ref=refs[i].at[c], dst_ref=refs[k + i].at[c], send_sem=send_sems.at[i], recv_sem=recv_sems.at[i],
            device_id=(x, y, 1 - c), device_id_type=MESH) for i in range(k)]
        for cp in copies:
            cp.start()
        for i, cp in enumerate(copies):
            cp.wait_send()
            pltpu.make_async_remote_copy(
                src_ref=refs[i].at[1 - c], dst_ref=refs[k + i].at[1 - c], send_sem=send_sems.at[i],
                recv_sem=recv_sems.at[i], device_id=(x, y, 1 - c), device_id_type=MESH).wait_recv()

    return pl.pallas_call(
        body, name=name, out_shape=[jax.ShapeDtypeStruct(a.shape, a.dtype) for a in bufs], in_specs=[_ANY] * k,
        out_specs=[_ANY] * k, input_output_aliases={i: i for i in range(k)},
        scratch_shapes=[pltpu.SemaphoreType.DMA((k,)), pltpu.SemaphoreType.DMA((k,))],
    )(*bufs)


def _row_tile(rows, cap=512):
    for t in range(cap - cap % 8, 7, -8):
        if rows % t == 0:
            return t
    return rows


def _elementwise(fn, n_out, name, *arrs, out_dtype=F32):
    rows, cols = arrs[0].shape
    tr = _row_tile(rows)
    n_in = len(arrs)

    def body(*refs):
        outs = fn(*[r[...].astype(F32) for r in refs[:n_in]])
        for r, o in zip(refs[n_in:], outs):
            r[...] = o.astype(out_dtype)

    spec = pl.BlockSpec((tr, cols), lambda i: (i, 0))
    return pl.pallas_call(
        body, grid=(rows // tr,), name=name, in_specs=[spec] * n_in, out_specs=[spec] * n_out,
        out_shape=[jax.ShapeDtypeStruct((rows, cols), out_dtype)] * n_out, compiler_params=_params("arbitrary"),
    )(*arrs)


def _pair_sums(gpacks, rbigs, ci, name):
    k = len(gpacks)

    def body(c_ref, *refs):
        for g_ref, r_ref, o_ref in zip(refs[:k], refs[k:2 * k], refs[2 * k:]):
            o_ref[...] = (g_ref[...] + r_ref[...]).astype(BF)

    half = lambda a: pl.BlockSpec((None,) + a.shape[1:], lambda s, c: (s, 0, 0))
    return pl.pallas_call(
        body, name=name, out_shape=[jax.ShapeDtypeStruct(r.shape, BF) for r in rbigs],
        grid_spec=pltpu.PrefetchScalarGridSpec(
            num_scalar_prefetch=1, grid=(N_CHIPS,),
            in_specs=[pl.BlockSpec((None, None) + g.shape[2:], lambda s, c: (s, c[0], 0, 0)) for g in gpacks]
            + [half(r) for r in rbigs],
            out_specs=[half(r) for r in rbigs]),
        compiler_params=_params("arbitrary"),
    )(ci.reshape(1), *gpacks, *rbigs)


def _chip_sums(sums, landed, chip, ci, name):
    k = len(sums)

    def body(p_ref, *refs):
        for own_ref, land_ref, o_ref in zip(refs[:k], refs[k:2 * k], refs[2 * k:]):
            f = lambda v: v.astype(F32)
            o_ref[...] = _add4(f(own_ref[...]), f(land_ref[0]), f(land_ref[1]), f(land_ref[2]))[0]

    return pl.pallas_call(
        body, name=name, out_shape=[jax.ShapeDtypeStruct((2,) + s.shape[1:], F32) for s in sums],
        grid_spec=pltpu.PrefetchScalarGridSpec(
            num_scalar_prefetch=1, grid=(1,),
            in_specs=[pl.BlockSpec((None,) + s.shape[1:], lambda i, p: (p[0], 0, 0)) for s in sums]
            + [pl.BlockSpec(l.shape, lambda i, p: (0, 0, 0)) for l in landed],
            out_specs=[pl.BlockSpec((None,) + s.shape[1:], lambda i, p: (p[1], 0, 0)) for s in sums]),
        compiler_params=_params("arbitrary"),
    )(jnp.stack([chip, ci]), *sums, *landed)


def _add2(a, b):
    return (a + b,)


def _add4(own, r0, r1, r2):
    return ((own + r2) + (r0 + r1),)


def _adamw_small(ws, gs, ms, vs):
    k = len(ws)

    def body(*refs):
        for i in range(k):
            outs = _adamw_math(*[refs[j * k + i][...] for j in range(4)])
            for j, o in enumerate(outs):
                refs[(4 + j) * k + i][...] = o

    return pl.pallas_call(
        body, name="adamw_small", out_shape=[jax.ShapeDtypeStruct(w.shape, F32) for w in ws] * 3,
    )(*ws, *gs, *ms, *vs)


def _adamw_math(w, g, m, v):
    m = ADAM_B1 * m + (1.0 - ADAM_B1) * g
    v = ADAM_B2 * v + (1.0 - ADAM_B2) * (g * g)
    m_hat = m / (1.0 - ADAM_B1 ** ADAM_STEP)
    v_hat = v / (1.0 - ADAM_B2 ** ADAM_STEP)
    delta = -ADAM_LR * (m_hat / (jnp.sqrt(v_hat) + ADAM_EPS) + ADAM_WD * w)
    return delta, m, v


WEIGHTS = ["meta_tokens", "ln1_g", "w_in", "q_a_norm_g", "w_uq", "kv_a_norm_g", "w_ukv", "q_norm_g", "k_norm_g",
           "conv_w", "conv_b", "lru_wa", "lru_ba", "lru_wi", "lru_bi", "lru_lambda", "attn_out_g", "rnn_out_g",
           "w_out", "ln2_g", "w_gate", "w_up", "w_down"]
BIG = ["w_in", "w_uq", "w_ukv", "w_out", "w_gate", "w_up", "w_down"]
BIG_T = {"w_in": True, "w_uq": True, "w_ukv": True, "w_out": False, "w_gate": True, "w_up": True, "w_down": False}
BIG_ROWS = {"w_in": 424, "w_uq": 72, "w_ukv": 64, "w_out": 256, "w_gate": 704, "w_up": 704, "w_down": 704}
EARLY = ["w_in", "w_uq", "w_ukv"]
LATE = ["w_out", "w_gate", "w_up", "w_down"]
EARLY_ROWS = 576
SMALL_SHARDED = ["meta_tokens", "conv_w", "lru_ba", "lru_bi", "lru_lambda"]
SMALL = [n for n in WEIGHTS if n not in BIG]
SMALL_PACK_ROWS = 160


def _offsets(names):
    off, o = {}, 0
    for n in names:
        off[n] = o
        o += BIG_ROWS[n]
    return off


def _shard_pack(names, src, rows):
    parts = [_to_pack_piece(n, src[n]) for n in names]
    used = sum(BIG_ROWS[n] for n in names)
    if rows > used:
        parts.append(jnp.zeros((rows - used, D), F32))
    return jnp.concatenate(parts, axis=0)


def _grad_pack(names, g, rows):
    parts = [g[n].reshape(N_CHIPS, BIG_ROWS[n], D) for n in names]
    used = sum(BIG_ROWS[n] for n in names)
    if rows > used:
        parts.append(jnp.zeros((N_CHIPS, rows - used, D), F32))
    return jnp.concatenate(parts, axis=1).reshape(N_CHIPS, 2, rows // 2, D)


def _to_pack_piece(name, shard):
    a = shard[0].T if BIG_T[name] else shard[0]
    return a.reshape(BIG_ROWS[name], D)


def _flat_pack(arrs, rows):
    flat = jnp.concatenate([a.reshape(-1) for a in arrs])
    return jnp.pad(flat, (0, rows * D - flat.shape[0])).reshape(rows, D)


def _flat_unpack(pack, shapes):
    flat, out, o = pack.reshape(-1), [], 0
    for s in shapes:
        n = math.prod(s)
        out.append(flat[o:o + n].reshape(s))
        o += n
    return out


def kernel(x, meta_tokens, ln1_g, w_in, q_a_norm_g, w_uq, kv_a_norm_g, w_ukv, q_norm_g, k_norm_g, conv_w, conv_b, lru_wa, lru_ba, lru_wi, lru_bi, lru_lambda, attn_out_g, rnn_out_g, w_out, ln2_g, w_gate, w_up, w_down, loss_target, m_meta_tokens, m_ln1_g, m_w_in, m_q_a_norm_g, m_w_uq, m_kv_a_norm_g, m_w_ukv, m_q_norm_g, m_k_norm_g, m_conv_w, m_conv_b, m_lru_wa, m_lru_ba, m_lru_wi, m_lru_bi, m_lru_lambda, m_attn_out_g, m_rnn_out_g, m_w_out, m_ln2_g, m_w_gate, m_w_up, m_w_down, v_meta_tokens, v_ln1_g, v_w_in, v_q_a_norm_g, v_w_uq, v_kv_a_norm_g, v_w_ukv, v_q_norm_g, v_k_norm_g, v_conv_w, v_conv_b, v_lru_wa, v_lru_ba, v_lru_wi, v_lru_bi, v_lru_lambda, v_attn_out_g, v_rnn_out_g, v_w_out, v_ln2_g, v_w_gate, v_w_up, v_w_down):
    wts = dict(zip(WEIGHTS, (meta_tokens, ln1_g, w_in, q_a_norm_g, w_uq, kv_a_norm_g, w_ukv, q_norm_g, k_norm_g, conv_w, conv_b, lru_wa, lru_ba, lru_wi, lru_bi, lru_lambda, attn_out_g, rnn_out_g, w_out, ln2_g, w_gate, w_up, w_down)))
    mom = dict(zip(WEIGHTS, (m_meta_tokens, m_ln1_g, m_w_in, m_q_a_norm_g, m_w_uq, m_kv_a_norm_g, m_w_ukv, m_q_norm_g, m_k_norm_g, m_conv_w, m_conv_b, m_lru_wa, m_lru_ba, m_lru_wi, m_lru_bi, m_lru_lambda, m_attn_out_g, m_rnn_out_g, m_w_out, m_ln2_g, m_w_gate, m_w_up, m_w_down)))
    var = dict(zip(WEIGHTS, (v_meta_tokens, v_ln1_g, v_w_in, v_q_a_norm_g, v_w_uq, v_kv_a_norm_g, v_w_ukv, v_q_norm_g, v_k_norm_g, v_conv_w, v_conv_b, v_lru_wa, v_lru_ba, v_lru_wi, v_lru_bi, v_lru_lambda, v_attn_out_g, v_rnn_out_g, v_w_out, v_ln2_g, v_w_gate, v_w_up, v_w_down)))
    xi, yi, ci = _place()
    chip = 2 * xi + yi
    off_e = _offsets(EARLY)
    half_e = EARLY_ROWS // 2
    gather_plan = _to_chips(lambda ref, tx, ty, c: ref.at[pl.ds(c * (ref.shape[0] // 2), ref.shape[0] // 2), :],
                            lambda ref, j, px, py, c: ref.at[pl.ds((4 * px + 2 * py + c) * (ref.shape[0] // 8),
                                                                   ref.shape[0] // 8), :])
    scatter_plan = _to_chips(lambda ref, tx, ty, c: ref.at[2 * tx + ty], lambda ref, j, px, py, c: ref.at[j])
    everywhere = _to_chips(lambda ref, tx, ty, c: ref, lambda ref, j, px, py, c: ref.at[j])
    n_late = len(LATE)

    pack_e = _shard_pack(EARLY, wts, EARLY_ROWS).astype(BF)
    spack = jnp.concatenate([meta_tokens[:, :LANES], meta_tokens[:, LANES:], conv_w[0], lru_ba[0], lru_bi[0],
                             lru_lambda[0], jnp.zeros((6, LANES), F32)], axis=0)
    sems_g, src_g, land_g, _ = _split_start(
        "gather_early_start", [pack_e, spack], [lax.empty((N_CHIPS * EARLY_ROWS, D), BF), lax.empty((N_CHIPS * 48, LANES), F32)],
        gather_plan, 6)
    tgt_padded = _pad_target(loss_target)
    pieces_l = [_to_pack_piece(n, wts[n]).astype(BF) for n in LATE]
    src_g, land_g = _split_wait("gather_early_wait", sems_g, src_g, land_g, [tgt_padded] + pieces_l, gather_plan)
    ge, gs = _gather_finish(land_g, src_g, "gather_early_finish")
    ge = ge.reshape(N_CHIPS, EARLY_ROWS, D)
    gs = gs.reshape(N_CHIPS, 48, LANES)
    full = {n: ge[:, off_e[n]:off_e[n] + BIG_ROWS[n]] for n in EARLY}
    sems_l, src_l, land_l, tied = _split_start(
        "gather_late_start", pieces_l, [lax.empty((N_CHIPS * BIG_ROWS[n], D), BF) for n in LATE], gather_plan, 3 * n_late,
        after=[ge])

    def late_weights(after):
        pieces, lands = _split_wait("gather_late_wait", sems_l, src_l, land_l, after, gather_plan)
        w_out_, w_gate_, w_up_, w_down_ = _gather_finish(lands, pieces, "gather_late_finish")
        return dict(w_out=w_out_, w_gate_t=w_gate_, w_up_t=w_up_, w_down=w_down_)

    pair, late = {}, {}

    def early_grads(g_late):
        halves = [g_late[n].reshape(N_CHIPS, 2, BIG_ROWS[n] // 2, D) for n in LATE]
        pair["sems"], pair["src"], pair["land"], zeros = _split_start(
            "grad_pair_late_start", halves, [lax.empty((N_CHIPS, BIG_ROWS[n] // 2, D), F32) for n in LATE], _to_sibling,
            N_CHIPS * n_late)
        return zeros[0, 0]

    def mid_grads(after):
        halves, landed = _split_wait("grad_pair_late_wait", pair["sems"], pair["src"], pair["land"], after, _to_sibling)
        chip_sums = _pair_sums(halves, landed, ci, "grad_pair_sum_late")
        late["sems"], late["src"], late["land"], zeros = _split_start(
            "grad_chip_late_start", chip_sums, [lax.empty((3, BIG_ROWS[n] // 2, D), BF) for n in LATE], scatter_plan,
            3 * n_late)
        return zeros[0, 0]

    cols = lambda a: a.transpose(1, 0, 2).reshape(a.shape[1], N_CHIPS * a.shape[2])
    meta_full = cols(jnp.concatenate([gs[:, 0:16], gs[:, 16:32]], axis=2))
    w = dict(
        w_in_t=full["w_in"].reshape(IN_COLS, D), w_uq_t=full["w_uq"].reshape(N_HEADS * QK_HEAD, Q_LORA),
        w_ukv_t=full["w_ukv"].reshape(2 * D_ATTN, KV_LORA),
        ln1_g=ln1_g, q_a_norm_g=q_a_norm_g, kv_a_norm_g=kv_a_norm_g, q_norm_g=q_norm_g, k_norm_g=k_norm_g,
        conv_w=cols(gs[:, 32:36]), conv_b=conv_b, lru_wa=lru_wa[0], lru_ba=cols(gs[:, 36:38]), lru_wi=lru_wi[0],
        lru_bi=cols(gs[:, 38:40]), lru_lambda=cols(gs[:, 40:42]), attn_out_g=attn_out_g, rnn_out_g=rnn_out_g,
        ln2_g=ln2_g,
    )

    loss_local, grad_x, g, last = _local_step(x, tgt_padded, meta_full + tied[0, 0], w, late_weights, early_grads,
                                              mid_grads)

    gpack = _grad_pack(EARLY, {"w_in": g["w_in_t"], "w_uq": g["w_uq_t"], "w_ukv": g["w_ukv_t"]}, EARLY_ROWS)
    full_shapes = {n: wts[n].shape for n in SMALL}
    full_shapes.update(meta_tokens=(N_META, D), conv_w=(1, CONV_W, D_RNN), lru_ba=(1, 2, D_RNN), lru_bi=(1, 2, D_RNN),
                       lru_lambda=(1, 2, D_RNN))
    gsmall = _flat_pack([g[n] for n in SMALL] + [loss_local], SMALL_PACK_ROWS)
    rbig, rsmall = _pair_exchange(gpack, [gsmall], "grad_pair_exchange")
    chip_big = _pair_sums([gpack], [rbig], ci, "grad_pair_sum")
    (chip_small,) = _elementwise(_add2, 1, "grad_pair_sum_small", gsmall, rsmall)
    sems_e, src_e, land_e, zero_e = _split_start(
        "grad_chip_early_start", chip_big, [lax.empty((3, half_e, D), BF)], scatter_plan, 3)
    sems_s, src_s, land_s, zero_s = _split_start(
        "grad_small_start", [chip_small], [lax.empty((3, SMALL_PACK_ROWS, D), F32)], everywhere, 3)

    grads, delta, new_m, new_v = {}, {}, {}, {}

    def adamw_big(n, gshard):
        _, k, cols = wts[n].shape
        as_rows = (lambda a: a[0].T) if BIG_T[n] else (lambda a: a[0])
        back = (lambda a: a.T[None]) if BIG_T[n] else (lambda a: a[None])
        g2 = gshard.reshape((cols, k) if BIG_T[n] else (k, cols))
        d_, m_, v_ = _elementwise(_adamw_math, 3, "adamw_" + n, as_rows(wts[n]), g2, as_rows(mom[n]), as_rows(var[n]))
        grads[n], delta[n], new_m[n], new_v[n] = back(g2), back(d_), back(m_), back(v_)
        return d_

    sums, landed = _split_wait("grad_chip_late_wait", late["sems"], late["src"], late["land"], last + [zero_e, zero_s],
                               scatter_plan)
    shards_l = _pair_fill(_chip_sums(sums, landed, chip, ci, "grad_chip_sum_late"), "grad_pair_fill_late")
    done_late = [adamw_big(n, buf) for n, buf in zip(LATE, shards_l)][-1]
    src_e, land_e = _split_wait("grad_chip_early_wait", sems_e, src_e, land_e, [done_late], scatter_plan)
    src_s, land_s = _split_wait("grad_small_wait", sems_s, src_s, land_s, [done_late], everywhere)
    (shard_e,) = _pair_fill(_chip_sums(src_e, land_e, chip, ci, "grad_chip_sum"), "grad_pair_fill_early")
    shard_e = shard_e.reshape(EARLY_ROWS, D)
    for n in EARLY:
        adamw_big(n, shard_e[off_e[n]:off_e[n] + BIG_ROWS[n]])
    (small_sum,) = _elementwise(_add4, 1, "grad_chip_sum_small", src_s[0], land_s[0][0], land_s[0][1], land_s[0][2])
    *small_grads, loss = _flat_unpack(small_sum, [full_shapes[n] for n in SMALL] + [()])
    small_full = dict(zip(SMALL, small_grads))
    for n in SMALL:
        a = small_full[n]
        if n in SMALL_SHARDED:
            width = wts[n].shape[-1]
            a = lax.dynamic_slice_in_dim(a, chip * width, width, axis=a.ndim - 1)
        grads[n] = a.reshape(wts[n].shape)

    rows_of = lambda a: a.reshape(-1, a.shape[-1])
    outs = _adamw_small(*[[rows_of(src[n]) for n in SMALL] for src in (wts, grads, mom, var)])
    for j, dst in enumerate((delta, new_m, new_v)):
        dst.update({n: outs[j * len(SMALL) + i].reshape(wts[n].shape) for i, n in enumerate(SMALL)})

    return (loss, grad_x, *[grads[n] for n in WEIGHTS], *[delta[n] for n in WEIGHTS],
            *[new_m[n] for n in WEIGHTS], *[new_v[n] for n in WEIGHTS])
```

```python
import math

import jax
import jax.numpy as jnp
from jax import lax
from jax.experimental import pallas as pl
from jax.experimental.pallas import tpu as pltpu

F32 = jnp.float32
BF = jnp.bfloat16
MESH = pl.DeviceIdType.MESH

D = 1024
SEQ = 2048
N_META = 16
T = N_META + SEQ
N_HEADS = 8
QK_NOPE = 64
QK_ROPE = 32
QK_HEAD = 96
V_HEAD = 64
Q_LORA = 384
KV_LORA = 256
D_ATTN = 512
D_RNN = 512
RNN_BW = 64
CONV_W = 4
LRU_C = 8.0
ROPE_THETA = 10000.0
D_FF = 2816
EPS = 1e-6
IN_COLS = 1696
ADAM_LR, ADAM_B1, ADAM_B2, ADAM_EPS, ADAM_WD, ADAM_STEP = 0.001, 0.9, 0.999, 1e-08, 0.01, 10

LANES = 128
TP = 2176
NB = 2
R = NB * TP
TR = 256
TRF = 256
TQ = 1088
HP = LANES
PC = 1792
O_CKV, O_KR, O_XR, O_XG = 384, 640, 768, 1280
CG = 128
N_CG = D_RNN // CG
VMEM_LIMIT = 56 * 1024 * 1024
N_CHIPS = 4
SCALE = QK_HEAD ** -0.5
KEY_MASK = -30000.0
LOG2_E = 1.4426950408889634
SCALE_LOG2 = SCALE * LOG2_E


def _nt(a, b):
    return lax.dot_general(a, b, (((1,), (1,)), ((), ())), preferred_element_type=F32)


def _nn(a, b):
    return jnp.dot(a, b, preferred_element_type=F32)


def _tn(a, b):
    return lax.dot_general(a, b, (((0,), (0,)), ((), ())), preferred_element_type=F32)


def _rms(x, g, n):
    ms = jnp.sum(x * x, axis=-1, keepdims=True) * (1.0 / n)
    return x * lax.rsqrt(ms + EPS) * g


def _lane_sum(y):
    return jnp.sum(y, axis=-1, keepdims=True)


def _rot(x):
    lane = lax.broadcasted_iota(jnp.int32, x.shape, 1)
    left = pltpu.roll(x, HP - 16, 1)
    right = pltpu.roll(x, 16, 1)
    lo = (lane >= QK_NOPE) & (lane < QK_NOPE + 16)
    hi = (lane >= QK_NOPE + 16) & (lane < QK_HEAD)
    return jnp.where(lo, -left, jnp.where(hi, right, 0.0))


def _head(x, g, cs, sn):
    n = x * lax.rsqrt(_lane_sum(x * x) * (1.0 / QK_HEAD) + EPS) * g
    return n * cs + _rot(n) * sn


def _head_bwd(x, g, cs, sn, dout):
    rs = lax.rsqrt(_lane_sum(x * x) * (1.0 / QK_HEAD) + EPS)
    xh = x * rs
    dn = dout * cs - _rot(dout * sn)
    gdn = g * dn
    t = _lane_sum(gdn * xh) * (1.0 / QK_HEAD)
    return rs * (gdn - xh * t), jnp.sum(dn * xh, axis=0, keepdims=True)


def _const_spec(shape):
    return pl.BlockSpec(shape, lambda *_: (0,) * len(shape), pipeline_mode=pl.Buffered(1))


def _row_spec(n, tr=TR):
    return pl.BlockSpec((tr, n), lambda i: (i, 0))


def _params(*sem, vmem=VMEM_LIMIT):
    return pltpu.CompilerParams(dimension_semantics=sem, vmem_limit_bytes=vmem)


def _stage_a_fwd(hp, cs, sn, cw):
    def body(hp_ref, cs_ref, sn_ref, ln1, win, qag, wq, kvag, wk, wv, qg, kg,
             pa_ref, xr_ref, xg_ref, q_ref, k_ref, v_ref):
        hn = _rms(hp_ref[...], ln1[...], D).astype(BF)
        p = _nt(hn, win[...])
        pa_ref[...] = p[:, :O_XR]
        xr_ref[...] = p[:, O_XR:O_XG]
        xg_ref[...] = p[:, O_XG:]
        cqn = _rms(p[:, :O_CKV], qag[...], Q_LORA).astype(BF)
        ckvn = _rms(p[:, O_CKV:O_KR], kvag[...], KV_LORA).astype(BF)
        kr = p[:, O_KR:O_XR]
        c, s = cs_ref[...], sn_ref[...]
        mask_lane = lax.broadcasted_iota(jnp.int32, (1, HP), 1) == QK_HEAD
        row = pl.program_id(0) * TRF + lax.broadcasted_iota(jnp.int32, (TRF, 1), 0)
        key_mask = jnp.where(jnp.where(row >= TP, row - TP, row) < T, 0.0, KEY_MASK)
        qraw = _nt(cqn, wq[...])
        kraw = _nt(ckvn, wk[...])
        for h in range(N_HEADS):
            sl = slice(h * HP, (h + 1) * HP)
            q_ref[:, sl] = jnp.where(mask_lane, 1.0, _head(qraw[:, sl], qg[...], c, s)).astype(BF)
            k_ref[:, sl] = jnp.where(mask_lane, key_mask, _head(kraw[:, sl] + kr, kg[...], c, s)).astype(BF)
        v_ref[...] = _nt(ckvn, wv[...]).astype(BF)

    rs = lambda n: _row_spec(n, TRF)
    return pl.pallas_call(
        body, grid=(R // TRF,), name="stage_a_fwd",
        in_specs=[rs(D), rs(HP), rs(HP), _const_spec((1, D)), _const_spec((PC, D)),
                  _const_spec((1, Q_LORA)), _const_spec((N_HEADS * HP, Q_LORA)), _const_spec((1, KV_LORA)),
                  _const_spec((N_HEADS * HP, KV_LORA)), _const_spec((D_ATTN, KV_LORA)), _const_spec((1, HP)),
                  _const_spec((1, HP))],
        out_specs=[rs(O_XR), rs(D_RNN), rs(D_RNN), rs(N_HEADS * HP), rs(N_HEADS * HP), rs(D_ATTN)],
        out_shape=[jax.ShapeDtypeStruct((R, O_XR), F32), jax.ShapeDtypeStruct((R, D_RNN), F32),
                   jax.ShapeDtypeStruct((R, D_RNN), F32), jax.ShapeDtypeStruct((R, N_HEADS * HP), BF),
                   jax.ShapeDtypeStruct((R, N_HEADS * HP), BF), jax.ShapeDtypeStruct((R, D_ATTN), BF)],
        compiler_params=_params("arbitrary"),
    )(hp, cs, sn, cw["ln1_g"], cw["win"], cw["qa_g"], cw["wq"], cw["kva_g"], cw["wk"], cw["wv"], cw["q_g"], cw["k_g"])


def _stage_a_bwd(dq, dk, dv, dxr, dxg, dh1, hp, pa, cs, sn, cw):
    def body(dq_ref, dk_ref, dv_ref, dxr_ref, dxg_ref, dh1_ref, hp_ref, pa_ref, cs_ref, sn_ref,
             ln1, win, qag, wq, kvag, wk, wv, qg, kg,
             dhp_ref, dp_ref, dqraw_ref, dkraw_ref, hn_ref, cqn_ref, ckvn_ref,
             dln1_ref, dqag_ref, dkvag_ref, dqg_ref, dkg_ref):
        @pl.when(pl.program_id(0) == 0)
        def _():
            for r in (dln1_ref, dqag_ref, dkvag_ref, dqg_ref, dkg_ref):
                r[...] = jnp.zeros_like(r)

        hn, vjp_ln1 = jax.vjp(lambda h, g: _rms(h, g, D), hp_ref[...], ln1[...])
        hn_ref[...] = hn.astype(BF)
        pa_v = pa_ref[...]
        cqn, vjp_qa = jax.vjp(lambda x, g: _rms(x, g, Q_LORA), pa_v[:, :O_CKV], qag[...])
        ckvn, vjp_kva = jax.vjp(lambda x, g: _rms(x, g, KV_LORA), pa_v[:, O_CKV:O_KR], kvag[...])
        kr = pa_v[:, O_KR:O_XR]
        cqnb, ckvnb = cqn.astype(BF), ckvn.astype(BF)
        cqn_ref[...] = cqnb
        ckvn_ref[...] = ckvnb
        c, s = cs_ref[...], sn_ref[...]
        lane = lax.broadcasted_iota(jnp.int32, (1, HP), 1)
        rope_lanes = ((lane >= QK_NOPE) & (lane < QK_HEAD)).astype(F32)
        dkr = jnp.zeros((TR, HP), F32)
        dqg = jnp.zeros((1, HP), F32)
        dkg = jnp.zeros((1, HP), F32)
        qraw = _nt(cqnb, wq[...])
        kraw = _nt(ckvnb, wk[...])
        for h in range(N_HEADS):
            sl = slice(h * HP, (h + 1) * HP)
            dqraw, dg = _head_bwd(qraw[:, sl], qg[...], c, s, dq_ref[:, sl])
            dqg = dqg + dg
            dqraw_ref[:, sl] = dqraw.astype(BF)
            dkraw, dg = _head_bwd(kraw[:, sl] + kr, kg[...], c, s, dk_ref[:, sl])
            dkg = dkg + dg
            dkraw_ref[:, sl] = dkraw.astype(BF)
            dkr = dkr + dkraw * rope_lanes
        dcq, dqag = vjp_qa(_nn(dqraw_ref[...], wq[...]))
        dckv, dkvag = vjp_kva(_nn(dkraw_ref[...], wk[...]) + _nn(dv_ref[...].astype(BF), wv[...]))
        dpb = jnp.concatenate([dcq, dckv, dkr, dxr_ref[...], dxg_ref[...]], axis=1).astype(BF)
        dp_ref[...] = dpb
        dh, dln1 = vjp_ln1(_nn(dpb, win[...]))
        dhp_ref[...] = dh + dh1_ref[...]
        dln1_ref[...] += dln1
        dqag_ref[...] += dqag
        dkvag_ref[...] += dkvag
        dqg_ref[...] += dqg
        dkg_ref[...] += dkg

    acc = lambda n: pl.BlockSpec((1, n), lambda i: (0, 0))
    return pl.pallas_call(
        body, grid=(R // TR,), name="stage_a_bwd",
        in_specs=[_row_spec(N_HEADS * HP), _row_spec(N_HEADS * HP), _row_spec(D_ATTN), _row_spec(D_RNN),
                  _row_spec(D_RNN), _row_spec(D), _row_spec(D), _row_spec(O_XR), _row_spec(HP), _row_spec(HP),
                  _const_spec((1, D)), _const_spec((PC, D)), _const_spec((1, Q_LORA)),
                  _const_spec((N_HEADS * HP, Q_LORA)), _const_spec((1, KV_LORA)),
                  _const_spec((N_HEADS * HP, KV_LORA)), _const_spec((D_ATTN, KV_LORA)), _const_spec((1, HP)),
                  _const_spec((1, HP))],
        out_specs=[_row_spec(D), _row_spec(PC), _row_spec(N_HEADS * HP), _row_spec(N_HEADS * HP), _row_spec(D),
                   _row_spec(Q_LORA), _row_spec(KV_LORA), acc(D), acc(Q_LORA), acc(KV_LORA), acc(HP), acc(HP)],
        out_shape=[jax.ShapeDtypeStruct((R, D), F32), jax.ShapeDtypeStruct((R, PC), BF),
                   jax.ShapeDtypeStruct((R, N_HEADS * HP), BF), jax.ShapeDtypeStruct((R, N_HEADS * HP), BF),
                   jax.ShapeDtypeStruct((R, D), BF), jax.ShapeDtypeStruct((R, Q_LORA), BF),
                   jax.ShapeDtypeStruct((R, KV_LORA), BF), jax.ShapeDtypeStruct((1, D), F32),
                   jax.ShapeDtypeStruct((1, Q_LORA), F32), jax.ShapeDtypeStruct((1, KV_LORA), F32),
                   jax.ShapeDtypeStruct((1, HP), F32), jax.ShapeDtypeStruct((1, HP), F32)],
        compiler_params=_params("arbitrary"),
    )(dq, dk, dv, dxr, dxg, dh1, hp, pa, cs, sn, cw["ln1_g"], cw["win"], cw["qa_g"], cw["wq"], cw["kva_g"],
      cw["wk"], cw["wv"], cw["q_g"], cw["k_g"])


def _head_mask(half, dtype):
    lane = lax.broadcasted_iota(jnp.int32, (1, 2 * V_HEAD), 1)
    return ((lane >= V_HEAD) == (half == 1)).astype(dtype)


def _attn_specs(tq):
    n_q = TP // tq
    return (NB, N_HEADS // 2, n_q), dict(
        q=pl.BlockSpec((tq, 2 * HP), lambda b, j, i: (b * n_q + i, j)),
        k=pl.BlockSpec((TP, 2 * HP), lambda b, j, i: (b, j)),
        v=pl.BlockSpec((TP, 2 * V_HEAD), lambda b, j, i: (b, j)),
        o=pl.BlockSpec((tq, 2 * V_HEAD), lambda b, j, i: (b * n_q + i, j)),
        lse=pl.BlockSpec((None, tq, 2), lambda b, j, i: (j, b * n_q + i, 0)))


TQF = 1088


def _attn_fwd(q, k, v):
    def body(q_ref, k_ref, v_ref, o_ref, lse_ref):
        v2 = v_ref[...]
        o = jnp.zeros((TQF, 2 * V_HEAD), F32)
        lse = []
        for hh in range(2):
            sl = slice(hh * HP, (hh + 1) * HP)
            raw = _nt(q_ref[:, sl], k_ref[:, sl])
            m = jnp.max(raw, axis=-1, keepdims=True)
            e = jnp.exp2((raw - m) * SCALE_LOG2)
            l = jnp.sum(e, axis=-1, keepdims=True)
            o = o + _nn(e.astype(BF), v2 * _head_mask(hh, BF)) * (1.0 / l)
            lse.append(m * SCALE_LOG2 + jnp.log(l) * LOG2_E)
        o_ref[...] = o
        lane = lax.broadcasted_iota(jnp.int32, (TQF, 2), 1)
        lse_ref[...] = jnp.where(lane == 0, lse[0], lse[1])

    grid, sp = _attn_specs(TQF)
    return pl.pallas_call(
        body, grid=grid, name="attn_fwd", in_specs=[sp["q"], sp["k"], sp["v"]], out_specs=[sp["o"], sp["lse"]],
        out_shape=[jax.ShapeDtypeStruct((R, D_ATTN), F32), jax.ShapeDtypeStruct((N_HEADS // 2, R, 2), F32)],
        compiler_params=_params("arbitrary", "arbitrary", "arbitrary"),
    )(q, k, v)


def _attn_bwd(q, k, v, o, lse, do):
    def body(q_ref, k_ref, v_ref, o_ref, lse_ref, do_ref, dq_ref, dk_ref, dv_ref):
        @pl.when(pl.program_id(2) == 0)
        def _():
            dk_ref[...] = jnp.zeros_like(dk_ref)
            dv_ref[...] = jnp.zeros_like(dv_ref)

        do = do_ref[...]
        dob = do.astype(BF)
        do_o = do * o_ref[...]
        v2 = v_ref[...]
        dv_sum = jnp.zeros((TP, 2 * V_HEAD), F32)
        for hh in range(2):
            sl = slice(hh * HP, (hh + 1) * HP)
            qb, kb = q_ref[:, sl], k_ref[:, sl]
            p = jnp.exp2(_nt(qb, kb) * SCALE_LOG2 - lse_ref[:, hh:hh + 1])
            dp = _nt(dob, v2 * _head_mask(hh, BF))
            delta = jnp.sum(do_o * _head_mask(hh, F32), axis=-1, keepdims=True)
            dsb = (p * (dp - delta) * SCALE).astype(BF)
            dq_ref[:, sl] = _nn(dsb, kb)
            dk_ref[:, sl] += _tn(dsb, qb)
            dv_sum = dv_sum + _tn(p.astype(BF), dob) * _head_mask(hh, F32)
        dv_ref[...] += dv_sum

    grid, sp = _attn_specs(TQ)
    return pl.pallas_call(
        body, grid=grid, name="attn_bwd", in_specs=[sp["q"], sp["k"], sp["v"], sp["o"], sp["lse"], sp["o"]],
        out_specs=[sp["q"], sp["k"], sp["v"]],
        out_shape=[jax.ShapeDtypeStruct((R, N_HEADS * HP), F32), jax.ShapeDtypeStruct((R, N_HEADS * HP), F32),
                   jax.ShapeDtypeStruct((R, D_ATTN), F32)],
        compiler_params=_params("arbitrary", "arbitrary", "arbitrary"),
    )(q, k, v, o, lse, do)


def _tile_prefix(a_ref, b_ref, reverse):
    tiles = (TP // 8, 8, CG)
    r8 = lax.broadcasted_iota(jnp.int32, tiles, 1)
    a, b = a_ref[...].reshape(tiles), b_ref[...].reshape(tiles)
    for s in (1, 2, 4):
        shift = 8 - s if reverse else s
        keep = (r8 < 8 - s) if reverse else (r8 >= s)
        b = jnp.where(keep, a * pltpu.roll(b, shift, 1) + b, b)
        a = jnp.where(keep, a * pltpu.roll(a, shift, 1), a)
    a_ref[...] = a.reshape(TP, CG)
    b_ref[...] = b.reshape(TP, CG)


def _scan_pair(af_ref, bf_ref, hf_ref, ab_ref, bb_ref, hb_ref):
    _tile_prefix(af_ref, bf_ref, False)
    _tile_prefix(ab_ref, bb_ref, True)
    n_tiles = TP // 8

    def step(i, carry):
        cf, cb = carry
        rf = pl.multiple_of(i * 8, 8)
        rb = pl.multiple_of((n_tiles - 1 - i) * 8, 8)
        hf_ref[pl.ds(rf, 8), :] = bf_ref[pl.ds(rf, 8), :] + af_ref[pl.ds(rf, 8), :] * cf
        hb_ref[pl.ds(rb, 8), :] = bb_ref[pl.ds(rb, 8), :] + ab_ref[pl.ds(rb, 8), :] * cb
        cf = bf_ref[pl.ds(rf + 7, 1), :] + af_ref[pl.ds(rf + 7, 1), :] * cf
        cb = bb_ref[pl.ds(rb, 1), :] + ab_ref[pl.ds(rb, 1), :] * cb
        return cf, cb

    zero = jnp.zeros((1, CG), F32)
    lax.fori_loop(0, n_tiles, step, (zero, zero), unroll=8)


def _shifts(x):
    t = lax.broadcasted_iota(jnp.int32, x.shape, 0)
    xm2 = jnp.where(t >= 2, pltpu.roll(x, 2, 0), 0.0)
    xm1 = jnp.where(t >= 1, pltpu.roll(x, 1, 0), 0.0)
    xp1 = jnp.where(t < TP - 1, pltpu.roll(x, TP - 1, 0), 0.0)
    return xm2, xm1, xp1


def _softplus(z):
    e = jnp.exp(-jnp.abs(z))
    small = e * (1.0 - e * (0.5 - e * (1.0 / 3.0)))
    return jnp.maximum(z, 0.0) + jnp.where(e < 0.01, small, jnp.log(1.0 + e))


def _sigmoid(x):
    return 0.5 * jnp.tanh(0.5 * x) + 0.5


def _one_minus_sq(log_a, a):
    x = 2.0 * log_a
    series = -x * (1.0 + x * 0.5 * (1.0 + x * (1.0 / 3.0) * (1.0 + x * 0.25)))
    return jnp.where(x > -0.05, series, 1.0 - a * a)


def _gates(row0, xc, pa_f, pi_f, pa_b, pi_b, lam_f, lam_b):
    t = row0 + lax.broadcasted_iota(jnp.int32, xc.shape, 0)
    valid = t < T
    out = []
    for pa, pi_, lam in ((pa_f, pi_f, lam_f), (pa_b, pi_b, lam_b)):
        r = _sigmoid(pa)
        gate_i = _sigmoid(pi_)
        log_a = -LRU_C * r * _softplus(-lam)
        a = jnp.exp(log_a)
        mult = jnp.sqrt(jnp.maximum(_one_minus_sq(log_a, a), 0.0))
        out += [a, jnp.where(valid, mult * (gate_i * xc), 0.0)]
    return tuple(out)


def _gates_bwd(row0, xc, pres, lams, cots):
    t = row0 + lax.broadcasted_iota(jnp.int32, xc.shape, 0)
    valid = t < T
    dxc = jnp.zeros_like(xc)
    dpres, dlams = [], []
    for d in range(2):
        pa, pi_, lam = pres[2 * d], pres[2 * d + 1], lams[d]
        da, db = cots[2 * d], jnp.where(valid, cots[2 * d + 1], 0.0)
        r = _sigmoid(pa)
        gate_i = _sigmoid(pi_)
        sp = _softplus(-lam)
        log_a = -LRU_C * r * sp
        a = jnp.exp(log_a)
        m2 = jnp.maximum(_one_minus_sq(log_a, a), 0.0)
        mult = jnp.sqrt(m2)
        dxc = dxc + db * (mult * gate_i)
        d_gate = db * (mult * xc)
        d_m2 = jnp.where(m2 > 0.0, db * (gate_i * xc) * (0.5 * lax.rsqrt(m2)), 0.0)
        d_log_a = da * a - 2.0 * d_m2 * (a * a)
        dpres += [d_log_a * (-LRU_C * sp) * (r * (1.0 - r)), d_gate * (gate_i * (1.0 - gate_i))]
        d_sp = jnp.sum(d_log_a * (-LRU_C * r), axis=0, keepdims=True)
        dlams.append(-d_sp * jax.nn.sigmoid(-lam))
    return dxc, dpres, dlams


def _rnn_specs():
    seq = pl.BlockSpec((TP, CG), lambda g, b: (b, g))
    return dict(
        seq=seq,
        cw=pl.BlockSpec((CONV_W, CG), lambda g, b: (0, g)),
        cb=pl.BlockSpec((1, CG), lambda g, b: (0, g)),
        w4=pl.BlockSpec((None, CG, 4 * CG), lambda g, b: (g, 0, 0)),
        b4=pl.BlockSpec((None, 1, 4 * CG), lambda g, b: (g, 0, 0)),
        lam=pl.BlockSpec((None, 1, 2 * CG), lambda g, b: (g, 0, 0)),
    )


def _conv(x, xm2, xm1, xp1, cw_ref, cb_ref):
    return cw_ref[0:1, :] * xm2 + cw_ref[1:2, :] * xm1 + cw_ref[2:3, :] * x + cw_ref[3:4, :] * xp1 + cb_ref[...]


TC = 128
N_TC = TP // TC


def _split4(pre):
    return pre[:, :CG], pre[:, CG:2 * CG], pre[:, 2 * CG:3 * CG], pre[:, 3 * CG:]


def _rnn_fwd(xr, xg, cw):
    def body(xr_ref, xg_ref, cw_ref, cb_ref, w4_ref, b4_ref, lam_ref, y_ref, hf_ref, hb_ref, af_ref, ab_ref, xc_ref,
             af, bf, ab, bb):
        x = xr_ref[...]
        xc_ref[...] = _conv(x, *_shifts(x), cw_ref, cb_ref)
        lam = lam_ref[...]

        def chunk(i, _):
            rows = pl.ds(pl.multiple_of(i * TC, TC), TC)
            xc = xc_ref[rows, :]
            pre = _nn(xc.astype(BF), w4_ref[...]) + b4_ref[...]
            a_f, b_f, a_b, b_b = _gates(i * TC, xc, *_split4(pre), lam[:, :CG], lam[:, CG:])
            af[rows, :] = a_f
            bf[rows, :] = b_f
            ab[rows, :] = a_b
            bb[rows, :] = b_b
            af_ref[rows, :] = a_f
            ab_ref[rows, :] = a_b
            return 0

        lax.fori_loop(0, N_TC, chunk, 0)
        _scan_pair(af, bf, hf_ref, ab, bb, hb_ref)
        y_ref[...] = (hf_ref[...] + hb_ref[...]) * jax.nn.gelu(xg_ref[...])

    sp = _rnn_specs()
    return pl.pallas_call(
        body, grid=(N_CG, NB), name="rnn_fwd",
        in_specs=[sp["seq"], sp["seq"], sp["cw"], sp["cb"], sp["w4"], sp["b4"], sp["lam"]],
        out_specs=[sp["seq"]] * 6, out_shape=[jax.ShapeDtypeStruct((R, D_RNN), F32)] * 6,
        scratch_shapes=[pltpu.VMEM((TP, CG), F32)] * 4,
        compiler_params=_params("arbitrary", "arbitrary"),
    )(xr, xg, cw["conv_w"], cw["conv_b"], cw["w4"], cw["b4"], cw["lam"])


def _rnn_bwd(dy, xr, xg, hf, hb, af, ab, xc, cw):
    def body(dy_ref, xr_ref, xg_ref, hf_ref, hb_ref, af_ref, ab_ref, xc_s, cw_ref, cb_ref, w4_ref, b4_ref, lam_ref,
             dxr_ref, dxg_ref, dcw_ref, dcb_ref, dw4_ref, db4_ref, dlam_ref,
             af_s, ab_s, dhs_s, dhs2_s, lf_s, lb_s, daf_s, dab_s, dxc_s):
        @pl.when(pl.program_id(1) == 0)
        def _():
            for r in (dcw_ref, dcb_ref, dw4_ref, db4_ref, dlam_ref):
                r[...] = jnp.zeros_like(r)

        lam = lam_ref[...]

        def chunk1(i, _):
            rows = pl.ds(pl.multiple_of(i * TC, TC), TC)
            _, vjp_y = jax.vjp(lambda h, g: h * jax.nn.gelu(g), hf_ref[rows, :] + hb_ref[rows, :], xg_ref[rows, :])
            dhs, dxg = vjp_y(dy_ref[rows, :])
            dhs_s[rows, :] = dhs
            dhs2_s[rows, :] = dhs
            dxg_ref[rows, :] = dxg
            return 0

        lax.fori_loop(0, N_TC, chunk1, 0)
        t = lax.broadcasted_iota(jnp.int32, (TP, CG), 0)
        af_s[...] = pltpu.roll(af_ref[...], TP - 1, 0)
        ab_s[...] = pltpu.roll(ab_ref[...], 1, 0)
        _scan_pair(ab_s, dhs_s, lb_s, af_s, dhs2_s, lf_s)
        daf_s[...] = lf_s[...] * jnp.where(t >= 1, pltpu.roll(hf_ref[...], 1, 0), 0.0)
        dab_s[...] = lb_s[...] * jnp.where(t < TP - 1, pltpu.roll(hb_ref[...], TP - 1, 0), 0.0)

        def chunk2(i, _):
            rows = pl.ds(pl.multiple_of(i * TC, TC), TC)
            xc = xc_s[rows, :]
            xcb = xc.astype(BF)
            pre = _nn(xcb, w4_ref[...]) + b4_ref[...]
            dxc, dpres, dlams = _gates_bwd(i * TC, xc, _split4(pre), (lam[:, :CG], lam[:, CG:]),
                                           (daf_s[rows, :], lf_s[rows, :], dab_s[rows, :], lb_s[rows, :]))
            dpre = jnp.concatenate(dpres, axis=1)
            dpreb = dpre.astype(BF)
            dxc_s[rows, :] = dxc + _nt(dpreb, w4_ref[...])
            dw4_ref[...] += _tn(xcb, dpreb)
            db4_ref[...] += jnp.sum(dpre, axis=0, keepdims=True)
            dlam_ref[...] += jnp.concatenate(dlams, axis=1)
            return 0

        lax.fori_loop(0, N_TC, chunk2, 0)
        dxc = dxc_s[...]
        x = xr_ref[...]
        taps = (jnp.where(t < TP - 2, pltpu.roll(dxc, TP - 2, 0), 0.0), jnp.where(t < TP - 1, pltpu.roll(dxc, TP - 1, 0), 0.0),
                dxc, jnp.where(t >= 1, pltpu.roll(dxc, 1, 0), 0.0))
        dcb_ref[...] += jnp.sum(dxc, axis=0, keepdims=True)
        dxr = jnp.zeros_like(dxc)
        for tap, shifted in enumerate(taps):
            dcw_ref[tap:tap + 1, :] += jnp.sum(x * shifted, axis=0, keepdims=True)
            dxr = dxr + cw_ref[tap:tap + 1, :] * shifted
        dxr_ref[...] = dxr

    sp = _rnn_specs()
    return pl.pallas_call(
        body, grid=(N_CG, NB), name="rnn_bwd",
        in_specs=[sp["seq"]] * 8 + [sp["cw"], sp["cb"], sp["w4"], sp["b4"], sp["lam"]],
        out_specs=[sp["seq"], sp["seq"], sp["cw"], sp["cb"], sp["w4"], sp["b4"], sp["lam"]],
        out_shape=[jax.ShapeDtypeStruct((R, D_RNN), F32), jax.ShapeDtypeStruct((R, D_RNN), F32),
                   jax.ShapeDtypeStruct((CONV_W, D_RNN), F32), jax.ShapeDtypeStruct((1, D_RNN), F32),
                   jax.ShapeDtypeStruct((N_CG, CG, 4 * CG), F32), jax.ShapeDtypeStruct((N_CG, 1, 4 * CG), F32),
                   jax.ShapeDtypeStruct((N_CG, 1, 2 * CG), F32)],
        scratch_shapes=[pltpu.VMEM((TP, CG), F32)] * 9,
        compiler_params=_params("arbitrary", "arbitrary"),
    )(dy, xr, xg, hf, hb, af, ab, xc, cw["conv_w"], cw["conv_b"], cw["w4"], cw["b4"], cw["lam"])


TD = 256
STAGE_D_VMEM = 58 * 1024 * 1024


def _stage_d(hp, o, y, tgt, cw):
    def body(hp_ref, o_ref, y_ref, tgt_ref, ga, gr, wout, ln2, wg, wu, wd,
             do_ref, dy_ref, dh1_ref, mix_ref, dh1b_ref, hn2_ref, dg_ref, du_ref, act_ref, dh2b_ref,
             loss_ref, dga_ref, dgr_ref, dln2_ref):
        i = pl.program_id(0)

        @pl.when(i == 0)
        def _():
            for r in (loss_ref, dga_ref, dgr_ref, dln2_ref):
                r[...] = jnp.zeros_like(r)

        mix_a, vjp_a = jax.vjp(lambda x, g: _rms(x, g, D_ATTN), o_ref[...], ga[...])
        mix_r, vjp_r = jax.vjp(lambda x, g: _rms(x, g, D_RNN), y_ref[...], gr[...])
        mab, mrb = mix_a.astype(BF), mix_r.astype(BF)
        mix_ref[:, :D_ATTN] = mab
        mix_ref[:, D_ATTN:] = mrb
        h1 = hp_ref[...] + _nn(mab, wout[:D_ATTN, :]) + _nn(mrb, wout[D_ATTN:, :])
        hn2, vjp_ln2 = jax.vjp(lambda x, g: _rms(x, g, D), h1, ln2[...])
        hn2b = hn2.astype(BF)
        hn2_ref[...] = hn2b
        act, vjp_act = jax.vjp(lambda g, u: jax.nn.silu(g) * u, _nt(hn2b, wg[...]), _nt(hn2b, wu[...]))
        actb = act.astype(BF)
        act_ref[...] = actb
        h2 = h1 + _nn(actb, wd[...])
        row = i * TD + lax.broadcasted_iota(jnp.int32, (TD, 1), 0)
        t = jnp.where(row >= TP, row - TP, row)
        err = jnp.where((t >= N_META) & (t < T), h2 - tgt_ref[...], 0.0)
        loss_ref[...] += jnp.sum(err * err) * (0.5 / D)
        dh2b = (err * (1.0 / D)).astype(BF)
        dh2b_ref[...] = dh2b
        dg, du = vjp_act(_nt(dh2b, wd[...]))
        dgb, dub = dg.astype(BF), du.astype(BF)
        dg_ref[...] = dgb
        du_ref[...] = dub
        dh1n, dln2 = vjp_ln2(_nn(dgb, wg[...]) + _nn(dub, wu[...]))
        dh1 = err * (1.0 / D) + dh1n
        dh1_ref[...] = dh1
        dh1b = dh1.astype(BF)
        dh1b_ref[...] = dh1b
        dmix = _nt(dh1b, wout[...])
        do, dga = vjp_a(dmix[:, :D_ATTN])
        dyr, dgr = vjp_r(dmix[:, D_ATTN:])
        do_ref[...] = do
        dy_ref[...] = dyr
        dga_ref[...] += dga
        dgr_ref[...] += dgr
        dln2_ref[...] += dln2

    rs = lambda n: _row_spec(n, TD)
    acc = lambda n: pl.BlockSpec((1, n), lambda i: (0, 0))
    return pl.pallas_call(
        body, grid=(R // TD,), name="stage_d",
        in_specs=[rs(D), rs(D_ATTN), rs(D_RNN), rs(D), _const_spec((1, D_ATTN)), _const_spec((1, D_RNN)),
                  _const_spec((D, D)), _const_spec((1, D)), _const_spec((D_FF, D)), _const_spec((D_FF, D)),
                  _const_spec((D_FF, D))],
        out_specs=[rs(D_ATTN), rs(D_RNN), rs(D), rs(D), rs(D), rs(D), rs(D_FF), rs(D_FF), rs(D_FF), rs(D),
                   acc(1), acc(D_ATTN), acc(D_RNN), acc(D)],
        out_shape=[jax.ShapeDtypeStruct((R, D_ATTN), F32), jax.ShapeDtypeStruct((R, D_RNN), F32),
                   jax.ShapeDtypeStruct((R, D), F32), jax.ShapeDtypeStruct((R, D), BF),
                   jax.ShapeDtypeStruct((R, D), BF), jax.ShapeDtypeStruct((R, D), BF),
                   jax.ShapeDtypeStruct((R, D_FF), BF), jax.ShapeDtypeStruct((R, D_FF), BF),
                   jax.ShapeDtypeStruct((R, D_FF), BF), jax.ShapeDtypeStruct((R, D), BF),
                   jax.ShapeDtypeStruct((1, 1), F32), jax.ShapeDtypeStruct((1, D_ATTN), F32),
                   jax.ShapeDtypeStruct((1, D_RNN), F32), jax.ShapeDtypeStruct((1, D), F32)],
        compiler_params=_params("arbitrary", vmem=STAGE_D_VMEM),
    )(hp, o, y, tgt, cw["ga"], cw["gr"], cw["wout"], cw["ln2_g"], cw["wg"], cw["wu"], cw["wd"])


TW = 2176


def _wgrad(a, b, name, tk=None):
    ka, nb = a.shape[1], b.shape[1]
    tk = ka if tk is None else tk

    def body(a_ref, b_ref, o_ref):
        @pl.when(pl.program_id(1) == 0)
        def _():
            o_ref[...] = jnp.zeros_like(o_ref)

        o_ref[...] += _tn(a_ref[...].astype(BF), b_ref[...].astype(BF))

    return pl.pallas_call(
        body, grid=(ka // tk, R // TW), name=name,
        in_specs=[pl.BlockSpec((TW, tk), lambda k, r: (r, k)), pl.BlockSpec((TW, nb), lambda k, r: (r, 0))],
        out_specs=pl.BlockSpec((tk, nb), lambda k, r: (k, 0)),
        out_shape=jax.ShapeDtypeStruct((ka, nb), F32),
        compiler_params=_params("arbitrary", "arbitrary"),
    )(a, b)


def _wgrad_heads(dq, dk, dv, cqn, ckvn):
    def body(dq_ref, dk_ref, dv_ref, cqn_ref, ckvn_ref, oq_ref, ok_ref, ov_ref):
        @pl.when(pl.program_id(0) == 0)
        def _():
            for r in (oq_ref, ok_ref, ov_ref):
                r[...] = jnp.zeros_like(r)

        ckvnb = ckvn_ref[...]
        oq_ref[...] += _tn(dq_ref[...], cqn_ref[...])
        ok_ref[...] += _tn(dk_ref[...], ckvnb)
        ov_ref[...] += _tn(dv_ref[...].astype(BF), ckvnb)

    rows = lambda a: pl.BlockSpec((TW, a.shape[1]), lambda r: (r, 0))
    full = lambda m, n: pl.BlockSpec((m, n), lambda r: (0, 0))
    shapes = [(dq.shape[1], cqn.shape[1]), (dk.shape[1], ckvn.shape[1]), (dv.shape[1], ckvn.shape[1])]
    return pl.pallas_call(
        body, grid=(R // TW,), name="wgrad_heads", in_specs=[rows(a) for a in (dq, dk, dv, cqn, ckvn)],
        out_specs=[full(*s) for s in shapes], out_shape=[jax.ShapeDtypeStruct(s, F32) for s in shapes],
        compiler_params=_params("arbitrary"),
    )(dq, dk, dv, cqn, ckvn)


def _rope_tables():
    half = QK_ROPE // 2
    freqs = 1.0 / (ROPE_THETA ** (jnp.arange(half, dtype=F32) / half))
    ang = jnp.arange(TP, dtype=F32)[:, None] * freqs[None, :]
    ones = jnp.ones((TP, QK_NOPE), F32)
    zeros = jnp.zeros((TP, QK_NOPE), F32)
    pad1 = jnp.ones((TP, HP - QK_HEAD), F32)
    pad0 = jnp.zeros((TP, HP - QK_HEAD), F32)
    cs = jnp.concatenate([ones, jnp.cos(ang), jnp.cos(ang), pad1], axis=1)
    sn = jnp.concatenate([zeros, jnp.sin(ang), jnp.sin(ang), pad0], axis=1)
    return jnp.tile(cs, (NB, 1)), jnp.tile(sn, (NB, 1))


def _pad_rows(a, lo, hi):
    return jnp.pad(a, ((0, 0), (lo, hi), (0, 0)))


def _pad_target(target):
    return _pad_rows(target, N_META, TP - T).reshape(R, D)


def _compute_weights(w):
    win_t = w["w_in_t"]
    kr = win_t[O_KR:O_KR + QK_ROPE]
    win = jnp.concatenate([win_t[:O_KR], jnp.zeros((QK_NOPE, D), F32), kr,
                           jnp.zeros((HP - QK_HEAD, D), F32), win_t[O_KR + QK_ROPE:]], axis=0)
    wq = _pad_rows(w["w_uq_t"].reshape(N_HEADS, QK_HEAD, Q_LORA), 0, HP - QK_HEAD)
    wkv = w["w_ukv_t"].reshape(N_HEADS, QK_NOPE + V_HEAD, KV_LORA)
    wk = _pad_rows(wkv[:, :QK_NOPE], 0, HP - QK_NOPE)
    wv = wkv[:, QK_NOPE:].reshape(D_ATTN, KV_LORA)
    gates = jnp.stack([w["lru_wa"][0], w["lru_wi"][0], w["lru_wa"][1], w["lru_wi"][1]])
    blk = gates.reshape(4, N_CG, 2, RNN_BW, RNN_BW)
    dense = jnp.einsum("tcaij,ab->tcaibj", blk, jnp.eye(2, dtype=F32)).reshape(4, N_CG, CG, CG)
    w4 = dense.transpose(1, 2, 0, 3).reshape(N_CG, CG, 4 * CG)
    bias = jnp.stack([w["lru_ba"][0], w["lru_bi"][0], w["lru_ba"][1], w["lru_bi"][1]])
    b4 = bias.reshape(4, N_CG, CG).transpose(1, 0, 2).reshape(N_CG, 1, 4 * CG)
    lam = w["lru_lambda"].reshape(2, N_CG, CG).transpose(1, 0, 2).reshape(N_CG, 1, 2 * CG)
    pad_g = lambda g: jnp.pad(g.reshape(1, QK_HEAD), ((0, 0), (0, HP - QK_HEAD)))
    return dict(
        ln1_g=w["ln1_g"].reshape(1, D), win=win.astype(BF), qa_g=w["q_a_norm_g"].reshape(1, Q_LORA),
        wq=wq.astype(BF).reshape(N_HEADS * HP, Q_LORA), kva_g=w["kv_a_norm_g"].reshape(1, KV_LORA),
        wk=wk.astype(BF).reshape(N_HEADS * HP, KV_LORA), wv=wv.astype(BF),
        q_g=pad_g(w["q_norm_g"]), k_g=pad_g(w["k_norm_g"]),
        conv_w=w["conv_w"].reshape(CONV_W, D_RNN), conv_b=w["conv_b"].reshape(1, D_RNN),
        w4=w4.astype(BF), b4=b4, lam=lam,
        ga=w["attn_out_g"].reshape(1, D_ATTN), gr=w["rnn_out_g"].reshape(1, D_RNN), ln2_g=w["ln2_g"].reshape(1, D),
    )


def _local_step(x, target, meta, w, late_forward, late_weights, early_grads, mid_grads):
    cw = _compute_weights(w)
    cs, sn = _rope_tables()
    hp = jnp.concatenate([jnp.broadcast_to(meta[None], (NB, N_META, D)), x,
                          jnp.zeros((NB, TP - T, D), F32)], axis=1).reshape(R, D)
    tgt = target if target.ndim == 2 else _pad_target(target)

    pa, xr, xg, q, k, v = _stage_a_fwd(hp, cs, sn, cw)
    o, lse = _attn_fwd(q, k, v)
    cw["conv_b"] = cw["conv_b"] + late_forward([o])
    y, hf, hb, af, ab, xc = _rnn_fwd(xr, xg, cw)
    late = late_weights([y])
    cw.update(wout=late["w_out"], wg=late["w_gate_t"], wu=late["w_up_t"], wd=late["w_down"])
    (do, dy, dh1, mixb, dh1b, hn2b, dgb, dub, actb, dh2b, loss, dga, dgr, dln2) = _stage_d(hp, o, y, tgt, cw)
    dwout = _wgrad(mixb, dh1b, "wgrad_out")
    dwg = _wgrad(dgb, hn2b, "wgrad_gate", tk=D_FF // 2)
    dwu = _wgrad(dub, hn2b, "wgrad_up", tk=D_FF // 2)
    dwd = _wgrad(actb, dh2b, "wgrad_down", tk=D_FF // 2)
    zero = early_grads(dict(w_out=dwout, w_gate=dwg, w_up=dwu, w_down=dwd))
    cw["conv_b"] = cw["conv_b"] + zero
    dxr, dxg, dcw, dcb, dw4, db4, dlam = _rnn_bwd(dy, xr, xg, hf, hb, af, ab, xc, cw)
    zero = mid_grads([dxr])
    dq, dk, dv = _attn_bwd(q, k, v, o, lse, do)
    (dhp, dpb, dqrawb, dkrawb, hn1b, cqnb, ckvnb, dln1, dqag, dkvag, dqg, dkg) = _stage_a_bwd(
        dq, dk, dv, dxr, dxg, dh1, hp, pa, cs, sn, dict(cw, qa_g=cw["qa_g"] + zero))

    dwin = _wgrad(dpb, hn1b, "wgrad_in", tk=PC // 2)
    dwq, dwk, dwv = _wgrad_heads(dqrawb, dkrawb, dv, cqnb, ckvnb)

    dwin_t = jnp.concatenate([dwin[:O_KR], dwin[O_KR + QK_NOPE:O_KR + QK_HEAD], dwin[O_XR:]], axis=0)
    dwq_t = dwq.reshape(N_HEADS, HP, Q_LORA)[:, :QK_HEAD].reshape(N_HEADS * QK_HEAD, Q_LORA)
    dwkv_t = jnp.concatenate([dwk.reshape(N_HEADS, HP, KV_LORA)[:, :QK_NOPE],
                              dwv.reshape(N_HEADS, V_HEAD, KV_LORA)], axis=1).reshape(2 * D_ATTN, KV_LORA)
    d4 = dw4.reshape(N_CG, 2, RNN_BW, 4, 2, RNN_BW)
    dgates = jnp.stack([d4[:, 0, :, :, 0, :], d4[:, 1, :, :, 1, :]], axis=1)
    dgates = dgates.transpose(3, 0, 1, 2, 4).reshape(4, N_HEADS, RNN_BW, RNN_BW)
    dbias = db4.reshape(N_CG, 4, CG).transpose(1, 0, 2).reshape(4, D_RNN)
    dhp3 = dhp.reshape(NB, TP, D)
    grads = dict(
        meta_tokens=jnp.sum(dhp3[:, :N_META], axis=0),
        ln1_g=dln1, w_in_t=dwin_t, q_a_norm_g=dqag, w_uq_t=dwq_t, kv_a_norm_g=dkvag, w_ukv_t=dwkv_t,
        q_norm_g=dqg[:, :QK_HEAD], k_norm_g=dkg[:, :QK_HEAD], conv_w=dcw[None], conv_b=dcb,
        lru_wa=jnp.stack([dgates[0], dgates[2]])[None], lru_ba=jnp.stack([dbias[0], dbias[2]])[None],
        lru_wi=jnp.stack([dgates[1], dgates[3]])[None], lru_bi=jnp.stack([dbias[1], dbias[3]])[None],
        lru_lambda=dlam.reshape(N_CG, 2, CG).transpose(1, 0, 2).reshape(1, 2, D_RNN),
        attn_out_g=dga, rnn_out_g=dgr, ln2_g=dln2,
    )
    return loss[0, 0], dhp3[:, N_META:T], grads, [dhp, dwin]


_ANY = pl.BlockSpec(memory_space=pl.ANY)


def _place():
    return lax.axis_index("x"), lax.axis_index("y"), lax.axis_index("c")


def _other_chips(x, y):
    return [(1 - x, y), (x, 1 - y), (1 - x, 1 - y)]


def _pair_exchange(big, whole, name):
    n_s, _, m, n = big.shape
    n_copies = n_s + len(whole)

    def body(*refs):
        big_ref, whole_refs = refs[0], refs[1:1 + len(whole)]
        rbig_ref, rwhole_refs = refs[1 + len(whole)], refs[2 + len(whole):2 + 2 * len(whole)]
        send_sems, recv_sems = refs[-2:]
        x, y, c = _place()
        sibling = (x, y, 1 - c)
        copies = [pltpu.make_async_remote_copy(
            src_ref=big_ref.at[s, 1 - c], dst_ref=rbig_ref.at[s], send_sem=send_sems.at[s], recv_sem=recv_sems.at[s],
            device_id=sibling, device_id_type=MESH) for s in range(n_s)]
        copies += [pltpu.make_async_remote_copy(
            src_ref=a, dst_ref=r, send_sem=send_sems.at[n_s + i], recv_sem=recv_sems.at[n_s + i],
            device_id=sibling, device_id_type=MESH) for i, (a, r) in enumerate(zip(whole_refs, rwhole_refs))]
        for cp in copies:
            cp.start()
        for cp in copies:
            cp.wait()

    return pl.pallas_call(
        body, name=name,
        out_shape=[jax.ShapeDtypeStruct((n_s, m, n), big.dtype)] + [jax.ShapeDtypeStruct(a.shape, a.dtype) for a in whole],
        in_specs=[_ANY] * (1 + len(whole)), out_specs=[_ANY] * (1 + len(whole)),
        scratch_shapes=[pltpu.SemaphoreType.DMA((n_copies,)), pltpu.SemaphoreType.DMA((n_copies,))],
    )(big, *whole)


_HBM = pl.BlockSpec(memory_space=pltpu.HBM)
_SEM = pl.BlockSpec(memory_space=pltpu.SEMAPHORE)
_EFFECT = pltpu.SideEffectType.DATAFLOW_SIDE_EFFECTING


def _split_copies(src_refs, land_refs, sems, plan, sending):
    n = len(sems) // 2
    return [pltpu.make_async_remote_copy(src_ref=s, dst_ref=d, send_sem=sems[k], recv_sem=sems[n + k], device_id=to,
                                         device_id_type=MESH)
            for k, (s, d, to) in enumerate(plan(src_refs, land_refs, sending))]


def _to_chips(src_at, land_at):
    def plan(src_refs, land_refs, sending):
        x, y, c = _place()
        return [(src_at(s, tx, ty, c), land_at(l, j, *((x, y) if sending else (tx, ty)), c), (tx, ty, c))
                for s, l in zip(src_refs, land_refs) for j, (tx, ty) in enumerate(_other_chips(x, y))]
    return plan


def _to_sibling(src_refs, land_refs, sending):
    x, y, c = _place()
    return [(s.at[k, 1 - c], l.at[k], (x, y, 1 - c)) for s, l in zip(src_refs, land_refs) for k in range(N_CHIPS)]


def _split_start(name, srcs, lands, plan, n, after=()):
    srcs, lands, after = list(srcs), list(lands), list(after)
    k, kb = len(srcs), len(srcs) + len(lands)

    def body(*refs):
        outs = refs[kb + len(after):]
        for cp in _split_copies(refs[:k], refs[k:kb], outs[:2 * n], plan, True):
            cp.start()
        outs[2 * n + kb][...] = jnp.zeros_like(outs[2 * n + kb])

    outs = pl.pallas_call(
        body, name=name,
        out_shape=(pltpu.SemaphoreType.DMA(()),) * (2 * n) + tuple(pltpu.HBM(a.shape, a.dtype) for a in srcs + lands)
        + (jax.ShapeDtypeStruct((8, LANES), F32),),
        in_specs=(_HBM,) * kb + (_ANY,) * len(after),
        out_specs=(_SEM,) * (2 * n) + (_HBM,) * kb + (pl.BlockSpec(memory_space=pltpu.VMEM),),
        input_output_aliases={i: 2 * n + i for i in range(kb)},
        compiler_params=pltpu.CompilerParams(has_side_effects=_EFFECT),
    )(*[pltpu.with_memory_space_constraint(a, pltpu.HBM) for a in srcs + lands], *after)
    return outs[:2 * n], list(outs[2 * n:2 * n + k]), list(outs[2 * n + k:2 * n + kb]), outs[2 * n + kb]


def _split_wait(name, sems, srcs, lands, after, plan):
    srcs, lands = list(srcs), list(lands)
    k, kb = len(srcs), len(srcs) + len(lands)

    def body(*refs):
        for cp in _split_copies(refs[:k], refs[k:kb], refs[kb:kb + len(sems)], plan, False):
            cp.wait_send()
            cp.wait_recv()

    outs = pl.pallas_call(
        body, name=name, out_shape=tuple(pltpu.HBM(a.shape, a.dtype) for a in srcs + lands),
        in_specs=(_HBM,) * kb + (_SEM,) * len(sems) + (_ANY,) * len(after), out_specs=(_HBM,) * kb,
        input_output_aliases={i: i for i in range(kb)}, compiler_params=pltpu.CompilerParams(has_side_effects=_EFFECT),
    )(*srcs, *lands, *sems, *after)
    return list(outs[:k]), list(outs[k:])


def _forward_landed(src_refs, land_refs, sending):
    x, y, c = _place()
    copies = []
    for ref in src_refs:
        m = ref.shape[0] // 8
        for tx, ty in _other_chips(x, y):
            rows = ref.at[pl.ds((4 * tx + 2 * ty + (c if sending else 1 - c)) * m, m), :]
            copies.append((rows, rows, (x, y, 1 - c)))
    return copies


def _place_own(pieces):
    k = len(pieces)

    def body(*refs):
        piece_refs, out_refs, stages = refs[:k], refs[k:2 * k], refs[2 * k:3 * k]
        load_sems, store_sems = refs[3 * k:]
        x, y, _ = _place()
        loads = [pltpu.make_async_copy(piece_refs[a], stages[a], load_sems.at[a]) for a in range(k)]
        stores = [pltpu.make_async_copy(
            stages[a], out_refs[a].at[pl.ds((2 * x + y) * pieces[a].shape[0], pieces[a].shape[0]), :], store_sems.at[a])
            for a in range(k)]
        for cp in loads:
            cp.start()
        for ld, st in zip(loads, stores):
            ld.wait()
            st.start()
        for cp in stores:
            cp.wait()

    return pl.pallas_call(
        body, name="gather_late_place_own",
        out_shape=[jax.ShapeDtypeStruct((N_CHIPS * p.shape[0], p.shape[1]), p.dtype) for p in pieces],
        in_specs=[_ANY] * k, out_specs=[_ANY] * k,
        scratch_shapes=[pltpu.VMEM(p.shape, p.dtype) for p in pieces]
        + [pltpu.SemaphoreType.DMA((k,)), pltpu.SemaphoreType.DMA((k,))],
    )(*pieces)


def _gather_finish(lands, pieces, name):
    k = len(lands)

    def body(*refs):
        land_refs, piece_refs, out_refs, stages = refs[:k], refs[k:2 * k], refs[2 * k:3 * k], refs[3 * k:4 * k]
        send_sems, recv_sems, load_sems, store_sems = refs[4 * k:]
        x, y, c = _place()
        sibling = (x, y, 1 - c)
        remote, loads, stores, arrivals = [], [], [], []
        for a in range(k):
            m = lands[a].shape[0] // 8

            def rows(px, py, pc, ref, m=m):
                return ref.at[pl.ds((4 * px + 2 * py + pc) * m, m), :]

            for j, (tx, ty) in enumerate(_other_chips(x, y)):
                sems = dict(send_sem=send_sems.at[3 * a + j], recv_sem=recv_sems.at[3 * a + j], device_id=sibling,
                            device_id_type=MESH)
                remote.append(pltpu.make_async_remote_copy(
                    src_ref=rows(tx, ty, c, land_refs[a]), dst_ref=rows(tx, ty, c, out_refs[a]), **sems))
                arrivals.append(pltpu.make_async_remote_copy(
                    src_ref=rows(tx, ty, 1 - c, out_refs[a]), dst_ref=rows(tx, ty, 1 - c, out_refs[a]), **sems))
            for h in range(2):
                loads.append(pltpu.make_async_copy(piece_refs[a].at[pl.ds(h * m, m), :], stages[a].at[h],
                                                   load_sems.at[2 * a + h]))
                stores.append(pltpu.make_async_copy(stages[a].at[h], rows(x, y, h, out_refs[a]), store_sems.at[2 * a + h]))
        for cp in remote + loads:
            cp.start()
        for ld, st in zip(loads, stores):
            ld.wait()
            st.start()
        for cp, arrival in zip(remote, arrivals):
            cp.wait_send()
            arrival.wait_recv()
        for cp in stores:
            cp.wait()

    return pl.pallas_call(
        body, name=name, out_shape=[jax.ShapeDtypeStruct(a.shape, a.dtype) for a in lands],
        in_specs=[_ANY] * (2 * k), out_specs=[_ANY] * k, input_output_aliases={i: i for i in range(k)},
        scratch_shapes=[pltpu.VMEM((2, a.shape[0] // 8, a.shape[1]), a.dtype) for a in lands]
        + [pltpu.SemaphoreType.DMA((3 * k,)), pltpu.SemaphoreType.DMA((3 * k,)), pltpu.SemaphoreType.DMA((2 * k,)),
           pltpu.SemaphoreType.DMA((2 * k,))],
    )(*lands, *pieces)


def _pair_fill(bufs, name):
    k = len(bufs)

    def body(*refs):
        send_sems, recv_sems = refs[-2:]
        x, y, c = _place()
        copies = [pltpu.make_async_remote_copy(
            src_ref=refs[i].at[c], dst_ref=refs[k + i].at[c], send_sem=send_sems.at[i], recv_sem=recv_sems.at[i],
            device_id=(x, y, 1 - c), device_id_type=MESH) for i in range(k)]
        for cp in copies:
            cp.start()
        for i, cp in enumerate(copies):
            cp.wait_send()
            pltpu.make_async_remote_copy(
                src_ref=refs[i].at[1 - c], dst_ref=refs[k + i].at[1 - c], send_sem=send_sems.at[i],
                recv_sem=recv_sems.at[i], device_id=(x, y, 1 - c), device_id_type=MESH).wait_recv()

    return pl.pallas_call(
        body, name=name, out_shape=[jax.ShapeDtypeStruct(a.shape, a.dtype) for a in bufs], in_specs=[_ANY] * k,
        out_specs=[_ANY] * k, input_output_aliases={i: i for i in range(k)},
        scratch_shapes=[pltpu.SemaphoreType.DMA((k,)), pltpu.SemaphoreType.DMA((k,))],
    )(*bufs)


def _row_tile(rows, cap=512):
    for t in range(cap - cap % 8, 7, -8):
        if rows % t == 0:
            return t
    return rows


def _elementwise(fn, n_out, name, *arrs, out_dtype=F32):
    rows, cols = arrs[0].shape
    tr = _row_tile(rows)
    n_in = len(arrs)

    def body(*refs):
        outs = fn(*[r[...].astype(F32) for r in refs[:n_in]])
        for r, o in zip(refs[n_in:], outs):
            r[...] = o.astype(out_dtype)

    spec = pl.BlockSpec((tr, cols), lambda i: (i, 0))
    return pl.pallas_call(
        body, grid=(rows // tr,), name=name, in_specs=[spec] * n_in, out_specs=[spec] * n_out,
        out_shape=[jax.ShapeDtypeStruct((rows, cols), out_dtype)] * n_out, compiler_params=_params("arbitrary"),
    )(*arrs)


def _pair_sums(gpacks, rbigs, ci, name):
    k = len(gpacks)

    def body(c_ref, *refs):
        for g_ref, r_ref, o_ref in zip(refs[:k], refs[k:2 * k], refs[2 * k:]):
            o_ref[...] = (g_ref[...] + r_ref[...]).astype(BF)

    half = lambda a: pl.BlockSpec((None,) + a.shape[1:], lambda s, c: (s, 0, 0))
    return pl.pallas_call(
        body, name=name, out_shape=[jax.ShapeDtypeStruct(r.shape, BF) for r in rbigs],
        grid_spec=pltpu.PrefetchScalarGridSpec(
            num_scalar_prefetch=1, grid=(N_CHIPS,),
            in_specs=[pl.BlockSpec((None, None) + g.shape[2:], lambda s, c: (s, c[0], 0, 0)) for g in gpacks]
            + [half(r) for r in rbigs],
            out_specs=[half(r) for r in rbigs]),
        compiler_params=_params("arbitrary"),
    )(ci.reshape(1), *gpacks, *rbigs)


def _chip_sums(sums, landed, chip, ci, name):
    k = len(sums)

    def body(p_ref, *refs):
        for own_ref, land_ref, o_ref in zip(refs[:k], refs[k:2 * k], refs[2 * k:]):
            f = lambda v: v.astype(F32)
            o_ref[...] = _add4(f(own_ref[...]), f(land_ref[0]), f(land_ref[1]), f(land_ref[2]))[0]

    return pl.pallas_call(
        body, name=name, out_shape=[jax.ShapeDtypeStruct((2,) + s.shape[1:], F32) for s in sums],
        grid_spec=pltpu.PrefetchScalarGridSpec(
            num_scalar_prefetch=1, grid=(1,),
            in_specs=[pl.BlockSpec((None,) + s.shape[1:], lambda i, p: (p[0], 0, 0)) for s in sums]
            + [pl.BlockSpec(l.shape, lambda i, p: (0, 0, 0)) for l in landed],
            out_specs=[pl.BlockSpec((None,) + s.shape[1:], lambda i, p: (p[1], 0, 0)) for s in sums]),
        compiler_params=_params("arbitrary"),
    )(jnp.stack([chip, ci]), *sums, *landed)


def _add2(a, b):
    return (a + b,)


def _add4(own, r0, r1, r2):
    return ((own + r2) + (r0 + r1),)


def _adamw_small(ws, gs, ms, vs):
    k = len(ws)

    def body(*refs):
        for i in range(k):
            outs = _adamw_math(*[refs[j * k + i][...] for j in range(4)])
            for j, o in enumerate(outs):
                refs[(4 + j) * k + i][...] = o

    return pl.pallas_call(
        body, name="adamw_small", out_shape=[jax.ShapeDtypeStruct(w.shape, F32) for w in ws] * 3,
    )(*ws, *gs, *ms, *vs)


def _adamw_math(w, g, m, v):
    m = ADAM_B1 * m + (1.0 - ADAM_B1) * g
    v = ADAM_B2 * v + (1.0 - ADAM_B2) * (g * g)
    m_hat = m / (1.0 - ADAM_B1 ** ADAM_STEP)
    v_hat = v / (1.0 - ADAM_B2 ** ADAM_STEP)
    delta = -ADAM_LR * (m_hat / (jnp.sqrt(v_hat) + ADAM_EPS) + ADAM_WD * w)
    return delta, m, v


WEIGHTS = ["meta_tokens", "ln1_g", "w_in", "q_a_norm_g", "w_uq", "kv_a_norm_g", "w_ukv", "q_norm_g", "k_norm_g",
           "conv_w", "conv_b", "lru_wa", "lru_ba", "lru_wi", "lru_bi", "lru_lambda", "attn_out_g", "rnn_out_g",
           "w_out", "ln2_g", "w_gate", "w_up", "w_down"]
BIG = ["w_in", "w_uq", "w_ukv", "w_out", "w_gate", "w_up", "w_down"]
BIG_T = {"w_in": True, "w_uq": True, "w_ukv": True, "w_out": False, "w_gate": True, "w_up": True, "w_down": False}
BIG_ROWS = {"w_in": 424, "w_uq": 72, "w_ukv": 64, "w_out": 256, "w_gate": 704, "w_up": 704, "w_down": 704}
EARLY = ["w_in", "w_uq", "w_ukv"]
LATE = ["w_out", "w_gate", "w_up", "w_down"]
EARLY_ROWS = 576
SMALL_SHARDED = ["meta_tokens", "conv_w", "lru_ba", "lru_bi", "lru_lambda"]
SMALL = [n for n in WEIGHTS if n not in BIG]
SMALL_PACK_ROWS = 160


def _offsets(names):
    off, o = {}, 0
    for n in names:
        off[n] = o
        o += BIG_ROWS[n]
    return off


def _shard_pack(names, src, rows):
    parts = [_to_pack_piece(n, src[n]) for n in names]
    used = sum(BIG_ROWS[n] for n in names)
    if rows > used:
        parts.append(jnp.zeros((rows - used, D), F32))
    return jnp.concatenate(parts, axis=0)


def _grad_pack(names, g, rows):
    parts = [g[n].reshape(N_CHIPS, BIG_ROWS[n], D) for n in names]
    used = sum(BIG_ROWS[n] for n in names)
    if rows > used:
        parts.append(jnp.zeros((N_CHIPS, rows - used, D), F32))
    return jnp.concatenate(parts, axis=1).reshape(N_CHIPS, 2, rows // 2, D)


def _to_pack_piece(name, shard):
    a = shard[0].T if BIG_T[name] else shard[0]
    return a.reshape(BIG_ROWS[name], D)


def _flat_pack(arrs, rows):
    flat = jnp.concatenate([a.reshape(-1) for a in arrs])
    return jnp.pad(flat, (0, rows * D - flat.shape[0])).reshape(rows, D)


def _flat_unpack(pack, shapes):
    flat, out, o = pack.reshape(-1), [], 0
    for s in shapes:
        n = math.prod(s)
        out.append(flat[o:o + n].reshape(s))
        o += n
    return out


def kernel(x, meta_tokens, ln1_g, w_in, q_a_norm_g, w_uq, kv_a_norm_g, w_ukv, q_norm_g, k_norm_g, conv_w, conv_b, lru_wa, lru_ba, lru_wi, lru_bi, lru_lambda, attn_out_g, rnn_out_g, w_out, ln2_g, w_gate, w_up, w_down, loss_target, m_meta_tokens, m_ln1_g, m_w_in, m_q_a_norm_g, m_w_uq, m_kv_a_norm_g, m_w_ukv, m_q_norm_g, m_k_norm_g, m_conv_w, m_conv_b, m_lru_wa, m_lru_ba, m_lru_wi, m_lru_bi, m_lru_lambda, m_attn_out_g, m_rnn_out_g, m_w_out, m_ln2_g, m_w_gate, m_w_up, m_w_down, v_meta_tokens, v_ln1_g, v_w_in, v_q_a_norm_g, v_w_uq, v_kv_a_norm_g, v_w_ukv, v_q_norm_g, v_k_norm_g, v_conv_w, v_conv_b, v_lru_wa, v_lru_ba, v_lru_wi, v_lru_bi, v_lru_lambda, v_attn_out_g, v_rnn_out_g, v_w_out, v_ln2_g, v_w_gate, v_w_up, v_w_down):
    wts = dict(zip(WEIGHTS, (meta_tokens, ln1_g, w_in, q_a_norm_g, w_uq, kv_a_norm_g, w_ukv, q_norm_g, k_norm_g, conv_w, conv_b, lru_wa, lru_ba, lru_wi, lru_bi, lru_lambda, attn_out_g, rnn_out_g, w_out, ln2_g, w_gate, w_up, w_down)))
    mom = dict(zip(WEIGHTS, (m_meta_tokens, m_ln1_g, m_w_in, m_q_a_norm_g, m_w_uq, m_kv_a_norm_g, m_w_ukv, m_q_norm_g, m_k_norm_g, m_conv_w, m_conv_b, m_lru_wa, m_lru_ba, m_lru_wi, m_lru_bi, m_lru_lambda, m_attn_out_g, m_rnn_out_g, m_w_out, m_ln2_g, m_w_gate, m_w_up, m_w_down)))
    var = dict(zip(WEIGHTS, (v_meta_tokens, v_ln1_g, v_w_in, v_q_a_norm_g, v_w_uq, v_kv_a_norm_g, v_w_ukv, v_q_norm_g, v_k_norm_g, v_conv_w, v_conv_b, v_lru_wa, v_lru_ba, v_lru_wi, v_lru_bi, v_lru_lambda, v_attn_out_g, v_rnn_out_g, v_w_out, v_ln2_g, v_w_gate, v_w_up, v_w_down)))
    xi, yi, ci = _place()
    chip = 2 * xi + yi
    off_e = _offsets(EARLY)
    half_e = EARLY_ROWS // 2
    gather_plan = _to_chips(lambda ref, tx, ty, c: ref.at[pl.ds(c * (ref.shape[0] // 2), ref.shape[0] // 2), :],
                            lambda ref, j, px, py, c: ref.at[pl.ds((4 * px + 2 * py + c) * (ref.shape[0] // 8),
                                                                   ref.shape[0] // 8), :])
    scatter_plan = _to_chips(lambda ref, tx, ty, c: ref.at[2 * tx + ty], lambda ref, j, px, py, c: ref.at[j])
    everywhere = _to_chips(lambda ref, tx, ty, c: ref, lambda ref, j, px, py, c: ref.at[j])
    n_late = len(LATE)

    pack_e = _shard_pack(EARLY, wts, EARLY_ROWS).astype(BF)
    spack = jnp.concatenate([meta_tokens[:, :LANES], meta_tokens[:, LANES:], conv_w[0], lru_ba[0], lru_bi[0],
                             lru_lambda[0], jnp.zeros((6, LANES), F32)], axis=0)
    sems_g, src_g, land_g, _ = _split_start(
        "gather_early_start", [pack_e, spack], [lax.empty((N_CHIPS * EARLY_ROWS, D), BF), lax.empty((N_CHIPS * 48, LANES), F32)],
        gather_plan, 6)
    tgt_padded = _pad_target(loss_target)
    pieces_l = [_to_pack_piece(n, wts[n]).astype(BF) for n in LATE]
    lands_l = list(_place_own(pieces_l))
    src_g, land_g = _split_wait("gather_early_wait", sems_g, src_g, land_g, [tgt_padded] + lands_l, gather_plan)
    ge, gs = _gather_finish(land_g, src_g, "gather_early_finish")
    ge = ge.reshape(N_CHIPS, EARLY_ROWS, D)
    gs = gs.reshape(N_CHIPS, 48, LANES)
    full = {n: ge[:, off_e[n]:off_e[n] + BIG_ROWS[n]] for n in EARLY}
    sems_l, src_l, land_l, tied = _split_start("gather_late_start", pieces_l, lands_l, gather_plan, 3 * n_late, after=[ge])

    forward, pair, late = {}, {}, {}

    def late_forward(after):
        _, landed = _split_wait("gather_late_wait", sems_l, src_l, land_l, after, gather_plan)
        forward["sems"], forward["src"], _, zeros = _split_start(
            "gather_late_forward_start", landed, [], _forward_landed, 3 * n_late)
        return zeros[0, 0]

    def late_weights(after):
        (w_out_, w_gate_, w_up_, w_down_), _ = _split_wait(
            "gather_late_forward_wait", forward["sems"], forward["src"], [], after, _forward_landed)
        return dict(w_out=w_out_, w_gate_t=w_gate_, w_up_t=w_up_, w_down=w_down_)

    def early_grads(g_late):
        halves = [g_late[n].reshape(N_CHIPS, 2, BIG_ROWS[n] // 2, D) for n in LATE]
        pair["sems"], pair["src"], pair["land"], zeros = _split_start(
            "grad_pair_late_start", halves, [lax.empty((N_CHIPS, BIG_ROWS[n] // 2, D), F32) for n in LATE], _to_sibling,
            N_CHIPS * n_late)
        return zeros[0, 0]

    def mid_grads(after):
        halves, landed = _split_wait("grad_pair_late_wait", pair["sems"], pair["src"], pair["land"], after, _to_sibling)
        chip_sums = _pair_sums(halves, landed, ci, "grad_pair_sum_late")
        late["sems"], late["src"], late["land"], zeros = _split_start(
            "grad_chip_late_start", chip_sums, [lax.empty((3, BIG_ROWS[n] // 2, D), BF) for n in LATE], scatter_plan,
            3 * n_late)
        return zeros[0, 0]

    cols = lambda a: a.transpose(1, 0, 2).reshape(a.shape[1], N_CHIPS * a.shape[2])
    meta_full = cols(jnp.concatenate([gs[:, 0:16], gs[:, 16:32]], axis=2))
    w = dict(
        w_in_t=full["w_in"].reshape(IN_COLS, D), w_uq_t=full["w_uq"].reshape(N_HEADS * QK_HEAD, Q_LORA),
        w_ukv_t=full["w_ukv"].reshape(2 * D_ATTN, KV_LORA),
        ln1_g=ln1_g, q_a_norm_g=q_a_norm_g, kv_a_norm_g=kv_a_norm_g, q_norm_g=q_norm_g, k_norm_g=k_norm_g,
        conv_w=cols(gs[:, 32:36]), conv_b=conv_b, lru_wa=lru_wa[0], lru_ba=cols(gs[:, 36:38]), lru_wi=lru_wi[0],
        lru_bi=cols(gs[:, 38:40]), lru_lambda=cols(gs[:, 40:42]), attn_out_g=attn_out_g, rnn_out_g=rnn_out_g,
        ln2_g=ln2_g,
    )

    loss_local, grad_x, g, last = _local_step(x, tgt_padded, meta_full + tied[0, 0], w, late_forward, late_weights,
                                              early_grads, mid_grads)

    gpack = _grad_pack(EARLY, {"w_in": g["w_in_t"], "w_uq": g["w_uq_t"], "w_ukv": g["w_ukv_t"]}, EARLY_ROWS)
    full_shapes = {n: wts[n].shape for n in SMALL}
    full_shapes.update(meta_tokens=(N_META, D), conv_w=(1, CONV_W, D_RNN), lru_ba=(1, 2, D_RNN), lru_bi=(1, 2, D_RNN),
                       lru_lambda=(1, 2, D_RNN))
    gsmall = _flat_pack([g[n] for n in SMALL] + [loss_local], SMALL_PACK_ROWS)
    rbig, rsmall = _pair_exchange(gpack, [gsmall], "grad_pair_exchange")
    chip_big = _pair_sums([gpack], [rbig], ci, "grad_pair_sum")
    (chip_small,) = _elementwise(_add2, 1, "grad_pair_sum_small", gsmall, rsmall)
    sems_e, src_e, land_e, zero_e = _split_start(
        "grad_chip_early_start", chip_big, [lax.empty((3, half_e, D), BF)], scatter_plan, 3)
    sems_s, src_s, land_s, zero_s = _split_start(
        "grad_small_start", [chip_small], [lax.empty((3, SMALL_PACK_ROWS, D), F32)], everywhere, 3)

    grads, delta, new_m, new_v = {}, {}, {}, {}

    def adamw_big(n, gshard):
        _, k, cols = wts[n].shape
        as_rows = (lambda a: a[0].T) if BIG_T[n] else (lambda a: a[0])
        back = (lambda a: a.T[None]) if BIG_T[n] else (lambda a: a[None])
        g2 = gshard.reshape((cols, k) if BIG_T[n] else (k, cols))
        d_, m_, v_ = _elementwise(_adamw_math, 3, "adamw_" + n, as_rows(wts[n]), g2, as_rows(mom[n]), as_rows(var[n]))
        grads[n], delta[n], new_m[n], new_v[n] = back(g2), back(d_), back(m_), back(v_)
        return d_

    sums, landed = _split_wait("grad_chip_late_wait", late["sems"], late["src"], late["land"], last + [zero_e, zero_s],
                               scatter_plan)
    shards_l = _pair_fill(_chip_sums(sums, landed, chip, ci, "grad_chip_sum_late"), "grad_pair_fill_late")
    done_late = [adamw_big(n, buf) for n, buf in zip(LATE, shards_l)][-1]
    src_e, land_e = _split_wait("grad_chip_early_wait", sems_e, src_e, land_e, [done_late], scatter_plan)
    src_s, land_s = _split_wait("grad_small_wait", sems_s, src_s, land_s, [done_late], everywhere)
    (shard_e,) = _pair_fill(_chip_sums(src_e, land_e, chip, ci, "grad_chip_sum"), "grad_pair_fill_early")
    shard_e = shard_e.reshape(EARLY_ROWS, D)
    for n in EARLY:
        adamw_big(n, shard_e[off_e[n]:off_e[n] + BIG_ROWS[n]])
    (small_sum,) = _elementwise(_add4, 1, "grad_chip_sum_small", src_s[0], land_s[0][0], land_s[0][1], land_s[0][2])
    *small_grads, loss = _flat_unpack(small_sum, [full_shapes[n] for n in SMALL] + [()])
    small_full = dict(zip(SMALL, small_grads))
    for n in SMALL:
        a = small_full[n]
        if n in SMALL_SHARDED:
            width = wts[n].shape[-1]
            a = lax.dynamic_slice_in_dim(a, chip * width, width, axis=a.ndim - 1)
        grads[n] = a.reshape(wts[n].shape)

    rows_of = lambda a: a.reshape(-1, a.shape[-1])
    outs = _adamw_small(*[[rows_of(src[n]) for n in SMALL] for src in (wts, grads, mom, var)])
    for j, dst in enumerate((delta, new_m, new_v)):
        dst.update({n: outs[j * len(SMALL) + i].reshape(wts[n].shape) for i, n in enumerate(SMALL)})

    return (loss, grad_x, *[grads[n] for n in WEIGHTS], *[delta[n] for n in WEIGHTS],
            *[new_m[n] for n in WEIGHTS], *[new_v[n] for n in WEIGHTS])
```

```python
import math

import jax
import jax.numpy as jnp
from jax import lax
from jax.experimental import pallas as pl
from jax.experimental.pallas import tpu as pltpu

F32 = jnp.float32
BF = jnp.bfloat16
MESH = pl.DeviceIdType.MESH

D = 1024
SEQ = 2048
N_META = 16
T = N_META + SEQ
N_HEADS = 8
QK_NOPE = 64
QK_ROPE = 32
QK_HEAD = 96
V_HEAD = 64
Q_LORA = 384
KV_LORA = 256
D_ATTN = 512
D_RNN = 512
RNN_BW = 64
CONV_W = 4
LRU_C = 8.0
ROPE_THETA = 10000.0
D_FF = 2816
EPS = 1e-6
IN_COLS = 1696
ADAM_LR, ADAM_B1, ADAM_B2, ADAM_EPS, ADAM_WD, ADAM_STEP = 0.001, 0.9, 0.999, 1e-08, 0.01, 10

LANES = 128
TP = 2176
NB = 2
R = NB * TP
TR = 256
TRF = 256
TQ = 1088
HP = LANES
PC = 1792
O_CKV, O_KR, O_XR, O_XG = 384, 640, 768, 1280
CG = 128
N_CG = D_RNN // CG
VMEM_LIMIT = 56 * 1024 * 1024
N_CHIPS = 4
SCALE = QK_HEAD ** -0.5
KEY_MASK = -30000.0
LOG2_E = 1.4426950408889634
SCALE_LOG2 = SCALE * LOG2_E


def _nt(a, b):
    return lax.dot_general(a, b, (((1,), (1,)), ((), ())), preferred_element_type=F32)


def _nn(a, b):
    return jnp.dot(a, b, preferred_element_type=F32)


def _tn(a, b):
    return lax.dot_general(a, b, (((0,), (0,)), ((), ())), preferred_element_type=F32)


def _rms(x, g, n):
    ms = jnp.sum(x * x, axis=-1, keepdims=True) * (1.0 / n)
    return x * lax.rsqrt(ms + EPS) * g


def _lane_sum(y):
    return jnp.sum(y, axis=-1, keepdims=True)


def _rot(x):
    lane = lax.broadcasted_iota(jnp.int32, x.shape, 1)
    left = pltpu.roll(x, HP - 16, 1)
    right = pltpu.roll(x, 16, 1)
    lo = (lane >= QK_NOPE) & (lane < QK_NOPE + 16)
    hi = (lane >= QK_NOPE + 16) & (lane < QK_HEAD)
    return jnp.where(lo, -left, jnp.where(hi, right, 0.0))


def _head(x, g, cs, sn):
    n = x * lax.rsqrt(_lane_sum(x * x) * (1.0 / QK_HEAD) + EPS) * g
    return n * cs + _rot(n) * sn


def _head_bwd(x, g, cs, sn, dout):
    rs = lax.rsqrt(_lane_sum(x * x) * (1.0 / QK_HEAD) + EPS)
    xh = x * rs
    dn = dout * cs - _rot(dout * sn)
    gdn = g * dn
    t = _lane_sum(gdn * xh) * (1.0 / QK_HEAD)
    return rs * (gdn - xh * t), jnp.sum(dn * xh, axis=0, keepdims=True)


def _const_spec(shape):
    return pl.BlockSpec(shape, lambda *_: (0,) * len(shape), pipeline_mode=pl.Buffered(1))


def _row_spec(n, tr=TR):
    return pl.BlockSpec((tr, n), lambda i: (i, 0))


def _params(*sem, vmem=VMEM_LIMIT):
    return pltpu.CompilerParams(dimension_semantics=sem, vmem_limit_bytes=vmem)


def _stage_a_fwd(hp, cs, sn, cw):
    def body(hp_ref, cs_ref, sn_ref, ln1, win, qag, wq, kvag, wk, wv, qg, kg,
             pa_ref, xr_ref, xg_ref, q_ref, k_ref, v_ref):
        hn = _rms(hp_ref[...], ln1[...], D).astype(BF)
        p = _nt(hn, win[...])
        pa_ref[...] = p[:, :O_XR]
        xr_ref[...] = p[:, O_XR:O_XG]
        xg_ref[...] = p[:, O_XG:]
        cqn = _rms(p[:, :O_CKV], qag[...], Q_LORA).astype(BF)
        ckvn = _rms(p[:, O_CKV:O_KR], kvag[...], KV_LORA).astype(BF)
        kr = p[:, O_KR:O_XR]
        c, s = cs_ref[...], sn_ref[...]
        mask_lane = lax.broadcasted_iota(jnp.int32, (1, HP), 1) == QK_HEAD
        row = pl.program_id(0) * TRF + lax.broadcasted_iota(jnp.int32, (TRF, 1), 0)
        key_mask = jnp.where(jnp.where(row >= TP, row - TP, row) < T, 0.0, KEY_MASK)
        qraw = _nt(cqn, wq[...])
        kraw = _nt(ckvn, wk[...])
        for h in range(N_HEADS):
            sl = slice(h * HP, (h + 1) * HP)
            q_ref[:, sl] = jnp.where(mask_lane, 1.0, _head(qraw[:, sl], qg[...], c, s)).astype(BF)
            k_ref[:, sl] = jnp.where(mask_lane, key_mask, _head(kraw[:, sl] + kr, kg[...], c, s)).astype(BF)
        v_ref[...] = _nt(ckvn, wv[...]).astype(BF)

    rs = lambda n: _row_spec(n, TRF)
    return pl.pallas_call(
        body, grid=(R // TRF,), name="stage_a_fwd",
        in_specs=[rs(D), rs(HP), rs(HP), _const_spec((1, D)), _const_spec((PC, D)),
                  _const_spec((1, Q_LORA)), _const_spec((N_HEADS * HP, Q_LORA)), _const_spec((1, KV_LORA)),
                  _const_spec((N_HEADS * HP, KV_LORA)), _const_spec((D_ATTN, KV_LORA)), _const_spec((1, HP)),
                  _const_spec((1, HP))],
        out_specs=[rs(O_XR), rs(D_RNN), rs(D_RNN), rs(N_HEADS * HP), rs(N_HEADS * HP), rs(D_ATTN)],
        out_shape=[jax.ShapeDtypeStruct((R, O_XR), F32), jax.ShapeDtypeStruct((R, D_RNN), F32),
                   jax.ShapeDtypeStruct((R, D_RNN), F32), jax.ShapeDtypeStruct((R, N_HEADS * HP), BF),
                   jax.ShapeDtypeStruct((R, N_HEADS * HP), BF), jax.ShapeDtypeStruct((R, D_ATTN), BF)],
        compiler_params=_params("arbitrary"),
    )(hp, cs, sn, cw["ln1_g"], cw["win"], cw["qa_g"], cw["wq"], cw["kva_g"], cw["wk"], cw["wv"], cw["q_g"], cw["k_g"])


def _stage_a_bwd(dq, dk, dv, dxr, dxg, dh1, hp, pa, cs, sn, cw):
    def body(dq_ref, dk_ref, dv_ref, dxr_ref, dxg_ref, dh1_ref, hp_ref, pa_ref, cs_ref, sn_ref,
             ln1, win, qag, wq, kvag, wk, wv, qg, kg,
             dhp_ref, dp_ref, dqraw_ref, dkraw_ref, hn_ref, cqn_ref, ckvn_ref,
             dln1_ref, dqag_ref, dkvag_ref, dqg_ref, dkg_ref):
        @pl.when(pl.program_id(0) == 0)
        def _():
            for r in (dln1_ref, dqag_ref, dkvag_ref, dqg_ref, dkg_ref):
                r[...] = jnp.zeros_like(r)

        hn, vjp_ln1 = jax.vjp(lambda h, g: _rms(h, g, D), hp_ref[...], ln1[...])
        hn_ref[...] = hn.astype(BF)
        pa_v = pa_ref[...]
        cqn, vjp_qa = jax.vjp(lambda x, g: _rms(x, g, Q_LORA), pa_v[:, :O_CKV], qag[...])
        ckvn, vjp_kva = jax.vjp(lambda x, g: _rms(x, g, KV_LORA), pa_v[:, O_CKV:O_KR], kvag[...])
        kr = pa_v[:, O_KR:O_XR]
        cqnb, ckvnb = cqn.astype(BF), ckvn.astype(BF)
        cqn_ref[...] = cqnb
        ckvn_ref[...] = ckvnb
        c, s = cs_ref[...], sn_ref[...]
        lane = lax.broadcasted_iota(jnp.int32, (1, HP), 1)
        rope_lanes = ((lane >= QK_NOPE) & (lane < QK_HEAD)).astype(F32)
        dkr = jnp.zeros((TR, HP), F32)
        dqg = jnp.zeros((1, HP), F32)
        dkg = jnp.zeros((1, HP), F32)
        qraw = _nt(cqnb, wq[...])
        kraw = _nt(ckvnb, wk[...])
        for h in range(N_HEADS):
            sl = slice(h * HP, (h + 1) * HP)
            dqraw, dg = _head_bwd(qraw[:, sl], qg[...], c, s, dq_ref[:, sl])
            dqg = dqg + dg
            dqraw_ref[:, sl] = dqraw.astype(BF)
            dkraw, dg = _head_bwd(kraw[:, sl] + kr, kg[...], c, s, dk_ref[:, sl])
            dkg = dkg + dg
            dkraw_ref[:, sl] = dkraw.astype(BF)
            dkr = dkr + dkraw * rope_lanes
        dcq, dqag = vjp_qa(_nn(dqraw_ref[...], wq[...]))
        dckv, dkvag = vjp_kva(_nn(dkraw_ref[...], wk[...]) + _nn(dv_ref[...].astype(BF), wv[...]))
        dpb = jnp.concatenate([dcq, dckv, dkr, dxr_ref[...], dxg_ref[...]], axis=1).astype(BF)
        dp_ref[...] = dpb
        dh, dln1 = vjp_ln1(_nn(dpb, win[...]))
        dhp_ref[...] = dh + dh1_ref[...]
        dln1_ref[...] += dln1
        dqag_ref[...] += dqag
        dkvag_ref[...] += dkvag
        dqg_ref[...] += dqg
        dkg_ref[...] += dkg

    acc = lambda n: pl.BlockSpec((1, n), lambda i: (0, 0))
    return pl.pallas_call(
        body, grid=(R // TR,), name="stage_a_bwd",
        in_specs=[_row_spec(N_HEADS * HP), _row_spec(N_HEADS * HP), _row_spec(D_ATTN), _row_spec(D_RNN),
                  _row_spec(D_RNN), _row_spec(D), _row_spec(D), _row_spec(O_XR), _row_spec(HP), _row_spec(HP),
                  _const_spec((1, D)), _const_spec((PC, D)), _const_spec((1, Q_LORA)),
                  _const_spec((N_HEADS * HP, Q_LORA)), _const_spec((1, KV_LORA)),
                  _const_spec((N_HEADS * HP, KV_LORA)), _const_spec((D_ATTN, KV_LORA)), _const_spec((1, HP)),
                  _const_spec((1, HP))],
        out_specs=[_row_spec(D), _row_spec(PC), _row_spec(N_HEADS * HP), _row_spec(N_HEADS * HP), _row_spec(D),
                   _row_spec(Q_LORA), _row_spec(KV_LORA), acc(D), acc(Q_LORA), acc(KV_LORA), acc(HP), acc(HP)],
        out_shape=[jax.ShapeDtypeStruct((R, D), F32), jax.ShapeDtypeStruct((R, PC), BF),
                   jax.ShapeDtypeStruct((R, N_HEADS * HP), BF), jax.ShapeDtypeStruct((R, N_HEADS * HP), BF),
                   jax.ShapeDtypeStruct((R, D), BF), jax.ShapeDtypeStruct((R, Q_LORA), BF),
                   jax.ShapeDtypeStruct((R, KV_LORA), BF), jax.ShapeDtypeStruct((1, D), F32),
                   jax.ShapeDtypeStruct((1, Q_LORA), F32), jax.ShapeDtypeStruct((1, KV_LORA), F32),
                   jax.ShapeDtypeStruct((1, HP), F32), jax.ShapeDtypeStruct((1, HP), F32)],
        compiler_params=_params("arbitrary"),
    )(dq, dk, dv, dxr, dxg, dh1, hp, pa, cs, sn, cw["ln1_g"], cw["win"], cw["qa_g"], cw["wq"], cw["kva_g"],
      cw["wk"], cw["wv"], cw["q_g"], cw["k_g"])


def _head_mask(half, dtype):
    lane = lax.broadcasted_iota(jnp.int32, (1, 2 * V_HEAD), 1)
    return ((lane >= V_HEAD) == (half == 1)).astype(dtype)


def _attn_specs(tq):
    n_q = TP // tq
    return (NB, N_HEADS // 2, n_q), dict(
        q=pl.BlockSpec((tq, 2 * HP), lambda b, j, i: (b * n_q + i, j)),
        k=pl.BlockSpec((TP, 2 * HP), lambda b, j, i: (b, j)),
        v=pl.BlockSpec((TP, 2 * V_HEAD), lambda b, j, i: (b, j)),
        o=pl.BlockSpec((tq, 2 * V_HEAD), lambda b, j, i: (b * n_q + i, j)),
        lse=pl.BlockSpec((None, tq, 2), lambda b, j, i: (j, b * n_q + i, 0)))


TQF = 1088


def _attn_fwd(q, k, v):
    def body(q_ref, k_ref, v_ref, o_ref, lse_ref):
        v2 = v_ref[...]
        o = jnp.zeros((TQF, 2 * V_HEAD), F32)
        lse = []
        for hh in range(2):
            sl = slice(hh * HP, (hh + 1) * HP)
            raw = _nt(q_ref[:, sl], k_ref[:, sl])
            m = jnp.max(raw, axis=-1, keepdims=True)
            e = jnp.exp2((raw - m) * SCALE_LOG2)
            l = jnp.sum(e, axis=-1, keepdims=True)
            o = o + _nn(e.astype(BF), v2 * _head_mask(hh, BF)) * (1.0 / l)
            lse.append(m * SCALE_LOG2 + jnp.log(l) * LOG2_E)
        o_ref[...] = o
        lane = lax.broadcasted_iota(jnp.int32, (TQF, 2), 1)
        lse_ref[...] = jnp.where(lane == 0, lse[0], lse[1])

    grid, sp = _attn_specs(TQF)
    return pl.pallas_call(
        body, grid=grid, name="attn_fwd", in_specs=[sp["q"], sp["k"], sp["v"]], out_specs=[sp["o"], sp["lse"]],
        out_shape=[jax.ShapeDtypeStruct((R, D_ATTN), F32), jax.ShapeDtypeStruct((N_HEADS // 2, R, 2), F32)],
        compiler_params=_params("arbitrary", "arbitrary", "arbitrary"),
    )(q, k, v)


def _attn_bwd(q, k, v, o, lse, do):
    def body(q_ref, k_ref, v_ref, o_ref, lse_ref, do_ref, dq_ref, dk_ref, dv_ref):
        @pl.when(pl.program_id(2) == 0)
        def _():
            dk_ref[...] = jnp.zeros_like(dk_ref)
            dv_ref[...] = jnp.zeros_like(dv_ref)

        do = do_ref[...]
        dob = do.astype(BF)
        do_o = do * o_ref[...]
        v2 = v_ref[...]
        dv_sum = jnp.zeros((TP, 2 * V_HEAD), F32)
        for hh in range(2):
            sl = slice(hh * HP, (hh + 1) * HP)
            qb, kb = q_ref[:, sl], k_ref[:, sl]
            p = jnp.exp2(_nt(qb, kb) * SCALE_LOG2 - lse_ref[:, hh:hh + 1])
            dp = _nt(dob, v2 * _head_mask(hh, BF))
            delta = jnp.sum(do_o * _head_mask(hh, F32), axis=-1, keepdims=True)
            dsb = (p * (dp - delta) * SCALE).astype(BF)
            dq_ref[:, sl] = _nn(dsb, kb)
            dk_ref[:, sl] += _tn(dsb, qb)
            dv_sum = dv_sum + _tn(p.astype(BF), dob) * _head_mask(hh, F32)
        dv_ref[...] += dv_sum

    grid, sp = _attn_specs(TQ)
    return pl.pallas_call(
        body, grid=grid, name="attn_bwd", in_specs=[sp["q"], sp["k"], sp["v"], sp["o"], sp["lse"], sp["o"]],
        out_specs=[sp["q"], sp["k"], sp["v"]],
        out_shape=[jax.ShapeDtypeStruct((R, N_HEADS * HP), F32), jax.ShapeDtypeStruct((R, N_HEADS * HP), F32),
                   jax.ShapeDtypeStruct((R, D_ATTN), F32)],
        compiler_params=_params("arbitrary", "arbitrary", "arbitrary"),
    )(q, k, v, o, lse, do)


SEG = TP // 8


def _scan_pair(af_ref, bf_ref, hf_ref, ab_ref, bb_ref, hb_ref, pf_ref, pb_ref):
    def step(j, carry):
        hf, pf, hb, pb = carry
        rows_f, rows_b = pl.ds(j, 8, stride=SEG), pl.ds(SEG - 1 - j, 8, stride=SEG)
        a = af_ref[rows_f, :]
        hf, pf = a * hf + bf_ref[rows_f, :], a * pf
        hf_ref[rows_f, :] = hf
        pf_ref[rows_f, :] = pf
        a = ab_ref[rows_b, :]
        hb, pb = a * hb + bb_ref[rows_b, :], a * pb
        hb_ref[rows_b, :] = hb
        pb_ref[rows_b, :] = pb
        return hf, pf, hb, pb

    zero, one = jnp.zeros((8, CG), F32), jnp.ones((8, CG), F32)
    hf, pf, hb, pb = lax.fori_loop(0, SEG, step, (zero, one, zero, one), unroll=4)
    seg = lax.broadcasted_iota(jnp.int32, (8, CG), 0)
    cf, cb = zero, zero
    for s in range(1, 8):
        cf = jnp.where(seg == s, pltpu.roll(hf + pf * cf, 1, 0), cf)
        cb = jnp.where(seg == 7 - s, pltpu.roll(hb + pb * cb, 7, 0), cb)
    for s in range(8):
        rows = slice(s * SEG, (s + 1) * SEG)
        hf_ref[rows, :] = hf_ref[rows, :] + pf_ref[rows, :] * cf[s:s + 1, :]
        hb_ref[rows, :] = hb_ref[rows, :] + pb_ref[rows, :] * cb[s:s + 1, :]


def _shifts(x):
    t = lax.broadcasted_iota(jnp.int32, x.shape, 0)
    xm2 = jnp.where(t >= 2, pltpu.roll(x, 2, 0), 0.0)
    xm1 = jnp.where(t >= 1, pltpu.roll(x, 1, 0), 0.0)
    xp1 = jnp.where(t < TP - 1, pltpu.roll(x, TP - 1, 0), 0.0)
    return xm2, xm1, xp1


def _softplus(z):
    e = jnp.exp(-jnp.abs(z))
    small = e * (1.0 - e * (0.5 - e * (1.0 / 3.0)))
    return jnp.maximum(z, 0.0) + jnp.where(e < 0.01, small, jnp.log(1.0 + e))


def _sigmoid(x):
    return 0.5 * jnp.tanh(0.5 * x) + 0.5


def _one_minus_sq(log_a, a):
    x = 2.0 * log_a
    series = -x * (1.0 + x * 0.5 * (1.0 + x * (1.0 / 3.0) * (1.0 + x * 0.25)))
    return jnp.where(x > -0.05, series, 1.0 - a * a)


def _gates(row0, xc, pa_f, pi_f, pa_b, pi_b, lam_f, lam_b):
    t = row0 + lax.broadcasted_iota(jnp.int32, xc.shape, 0)
    valid = t < T
    out = []
    for pa, pi_, lam in ((pa_f, pi_f, lam_f), (pa_b, pi_b, lam_b)):
        r = _sigmoid(pa)
        gate_i = _sigmoid(pi_)
        log_a = -LRU_C * r * _softplus(-lam)
        a = jnp.exp(log_a)
        mult = jnp.sqrt(jnp.maximum(_one_minus_sq(log_a, a), 0.0))
        out += [a, jnp.where(valid, mult * (gate_i * xc), 0.0)]
    return tuple(out)


def _gates_bwd(row0, xc, pres, lams, cots):
    t = row0 + lax.broadcasted_iota(jnp.int32, xc.shape, 0)
    valid = t < T
    dxc = jnp.zeros_like(xc)
    dpres, dlams = [], []
    for d in range(2):
        pa, pi_, lam = pres[2 * d], pres[2 * d + 1], lams[d]
        da, db = cots[2 * d], jnp.where(valid, cots[2 * d + 1], 0.0)
        r = _sigmoid(pa)
        gate_i = _sigmoid(pi_)
        sp = _softplus(-lam)
        log_a = -LRU_C * r * sp
        a = jnp.exp(log_a)
        m2 = jnp.maximum(_one_minus_sq(log_a, a), 0.0)
        mult = jnp.sqrt(m2)
        dxc = dxc + db * (mult * gate_i)
        d_gate = db * (mult * xc)
        d_m2 = jnp.where(m2 > 0.0, db * (gate_i * xc) * (0.5 * lax.rsqrt(m2)), 0.0)
        d_log_a = da * a - 2.0 * d_m2 * (a * a)
        dpres += [d_log_a * (-LRU_C * sp) * (r * (1.0 - r)), d_gate * (gate_i * (1.0 - gate_i))]
        d_sp = jnp.sum(d_log_a * (-LRU_C * r), axis=0, keepdims=True)
        dlams.append(-d_sp * jax.nn.sigmoid(-lam))
    return dxc, dpres, dlams


def _rnn_specs():
    seq = pl.BlockSpec((TP, CG), lambda g, b: (b, g))
    return dict(
        seq=seq,
        cw=pl.BlockSpec((CONV_W, CG), lambda g, b: (0, g)),
        cb=pl.BlockSpec((1, CG), lambda g, b: (0, g)),
        w4=pl.BlockSpec((None, CG, 4 * CG), lambda g, b: (g, 0, 0)),
        b4=pl.BlockSpec((None, 1, 4 * CG), lambda g, b: (g, 0, 0)),
        lam=pl.BlockSpec((None, 1, 2 * CG), lambda g, b: (g, 0, 0)),
    )


def _conv(x, xm2, xm1, xp1, cw_ref, cb_ref):
    return cw_ref[0:1, :] * xm2 + cw_ref[1:2, :] * xm1 + cw_ref[2:3, :] * x + cw_ref[3:4, :] * xp1 + cb_ref[...]


TC = 128
N_TC = TP // TC


def _split4(pre):
    return pre[:, :CG], pre[:, CG:2 * CG], pre[:, 2 * CG:3 * CG], pre[:, 3 * CG:]


def _rnn_fwd(xr, xg, cw):
    def body(xr_ref, xg_ref, cw_ref, cb_ref, w4_ref, b4_ref, lam_ref, y_ref, hf_ref, hb_ref, af_ref, ab_ref, xc_ref,
             af, bf, ab, bb, pf, pb):
        x = xr_ref[...]
        xc_ref[...] = _conv(x, *_shifts(x), cw_ref, cb_ref)
        lam = lam_ref[...]

        def chunk(i, _):
            rows = pl.ds(pl.multiple_of(i * TC, TC), TC)
            xc = xc_ref[rows, :]
            pre = _nn(xc.astype(BF), w4_ref[...]) + b4_ref[...]
            a_f, b_f, a_b, b_b = _gates(i * TC, xc, *_split4(pre), lam[:, :CG], lam[:, CG:])
            af[rows, :] = a_f
            bf[rows, :] = b_f
            ab[rows, :] = a_b
            bb[rows, :] = b_b
            af_ref[rows, :] = a_f
            ab_ref[rows, :] = a_b
            return 0

        lax.fori_loop(0, N_TC, chunk, 0)
        _scan_pair(af, bf, hf_ref, ab, bb, hb_ref, pf, pb)
        y_ref[...] = (hf_ref[...] + hb_ref[...]) * jax.nn.gelu(xg_ref[...])

    sp = _rnn_specs()
    return pl.pallas_call(
        body, grid=(N_CG, NB), name="rnn_fwd",
        in_specs=[sp["seq"], sp["seq"], sp["cw"], sp["cb"], sp["w4"], sp["b4"], sp["lam"]],
        out_specs=[sp["seq"]] * 6, out_shape=[jax.ShapeDtypeStruct((R, D_RNN), F32)] * 6,
        scratch_shapes=[pltpu.VMEM((TP, CG), F32)] * 6,
        compiler_params=_params("arbitrary", "arbitrary"),
    )(xr, xg, cw["conv_w"], cw["conv_b"], cw["w4"], cw["b4"], cw["lam"])


def _rnn_bwd(dy, xr, xg, hf, hb, af, ab, xc, cw):
    def body(dy_ref, xr_ref, xg_ref, hf_ref, hb_ref, af_ref, ab_ref, xc_s, cw_ref, cb_ref, w4_ref, b4_ref, lam_ref,
             dxr_ref, dxg_ref, dcw_ref, dcb_ref, dw4_ref, db4_ref, dlam_ref,
             af_s, ab_s, dhs_s, lf_s, lb_s, daf_s, dab_s, dxc_s):
        @pl.when(pl.program_id(1) == 0)
        def _():
            for r in (dcw_ref, dcb_ref, dw4_ref, db4_ref, dlam_ref):
                r[...] = jnp.zeros_like(r)

        lam = lam_ref[...]

        def chunk1(i, _):
            rows = pl.ds(pl.multiple_of(i * TC, TC), TC)
            _, vjp_y = jax.vjp(lambda h, g: h * jax.nn.gelu(g), hf_ref[rows, :] + hb_ref[rows, :], xg_ref[rows, :])
            dhs, dxg = vjp_y(dy_ref[rows, :])
            dhs_s[rows, :] = dhs
            dxg_ref[rows, :] = dxg
            return 0

        lax.fori_loop(0, N_TC, chunk1, 0)
        t = lax.broadcasted_iota(jnp.int32, (TP, CG), 0)
        af_s[...] = pltpu.roll(af_ref[...], TP - 1, 0)
        ab_s[...] = pltpu.roll(ab_ref[...], 1, 0)
        _scan_pair(ab_s, dhs_s, lb_s, af_s, dhs_s, lf_s, dab_s, daf_s)
        daf_s[...] = lf_s[...] * jnp.where(t >= 1, pltpu.roll(hf_ref[...], 1, 0), 0.0)
        dab_s[...] = lb_s[...] * jnp.where(t < TP - 1, pltpu.roll(hb_ref[...], TP - 1, 0), 0.0)

        def chunk2(i, _):
            rows = pl.ds(pl.multiple_of(i * TC, TC), TC)
            xc = xc_s[rows, :]
            xcb = xc.astype(BF)
            pre = _nn(xcb, w4_ref[...]) + b4_ref[...]
            dxc, dpres, dlams = _gates_bwd(i * TC, xc, _split4(pre), (lam[:, :CG], lam[:, CG:]),
                                           (daf_s[rows, :], lf_s[rows, :], dab_s[rows, :], lb_s[rows, :]))
            dpre = jnp.concatenate(dpres, axis=1)
            dpreb = dpre.astype(BF)
            dxc_s[rows, :] = dxc + _nt(dpreb, w4_ref[...])
            dw4_ref[...] += _tn(xcb, dpreb)
            db4_ref[...] += jnp.sum(dpre, axis=0, keepdims=True)
            dlam_ref[...] += jnp.concatenate(dlams, axis=1)
            return 0

        lax.fori_loop(0, N_TC, chunk2, 0)
        dxc = dxc_s[...]
        x = xr_ref[...]
        taps = (jnp.where(t < TP - 2, pltpu.roll(dxc, TP - 2, 0), 0.0), jnp.where(t < TP - 1, pltpu.roll(dxc, TP - 1, 0), 0.0),
                dxc, jnp.where(t >= 1, pltpu.roll(dxc, 1, 0), 0.0))
        dcb_ref[...] += jnp.sum(dxc, axis=0, keepdims=True)
        dxr = jnp.zeros_like(dxc)
        for tap, shifted in enumerate(taps):
            dcw_ref[tap:tap + 1, :] += jnp.sum(x * shifted, axis=0, keepdims=True)
            dxr = dxr + cw_ref[tap:tap + 1, :] * shifted
        dxr_ref[...] = dxr

    sp = _rnn_specs()
    return pl.pallas_call(
        body, grid=(N_CG, NB), name="rnn_bwd",
        in_specs=[sp["seq"]] * 8 + [sp["cw"], sp["cb"], sp["w4"], sp["b4"], sp["lam"]],
        out_specs=[sp["seq"], sp["seq"], sp["cw"], sp["cb"], sp["w4"], sp["b4"], sp["lam"]],
        out_shape=[jax.ShapeDtypeStruct((R, D_RNN), F32), jax.ShapeDtypeStruct((R, D_RNN), F32),
                   jax.ShapeDtypeStruct((CONV_W, D_RNN), F32), jax.ShapeDtypeStruct((1, D_RNN), F32),
                   jax.ShapeDtypeStruct((N_CG, CG, 4 * CG), F32), jax.ShapeDtypeStruct((N_CG, 1, 4 * CG), F32),
                   jax.ShapeDtypeStruct((N_CG, 1, 2 * CG), F32)],
        scratch_shapes=[pltpu.VMEM((TP, CG), F32)] * 8,
        compiler_params=_params("arbitrary", "arbitrary"),
    )(dy, xr, xg, hf, hb, af, ab, xc, cw["conv_w"], cw["conv_b"], cw["w4"], cw["b4"], cw["lam"])


TD = 256
STAGE_D_VMEM = 58 * 1024 * 1024


def _stage_d(hp, o, y, tgt, cw):
    def body(hp_ref, o_ref, y_ref, tgt_ref, ga, gr, wout, ln2, wg, wu, wd,
             do_ref, dy_ref, dh1_ref, mix_ref, dh1b_ref, hn2_ref, dg_ref, du_ref, act_ref, dh2b_ref,
             loss_ref, dga_ref, dgr_ref, dln2_ref):
        i = pl.program_id(0)

        @pl.when(i == 0)
        def _():
            for r in (loss_ref, dga_ref, dgr_ref, dln2_ref):
                r[...] = jnp.zeros_like(r)

        mix_a, vjp_a = jax.vjp(lambda x, g: _rms(x, g, D_ATTN), o_ref[...], ga[...])
        mix_r, vjp_r = jax.vjp(lambda x, g: _rms(x, g, D_RNN), y_ref[...], gr[...])
        mab, mrb = mix_a.astype(BF), mix_r.astype(BF)
        mix_ref[:, :D_ATTN] = mab
        mix_ref[:, D_ATTN:] = mrb
        h1 = hp_ref[...] + _nn(mab, wout[:D_ATTN, :]) + _nn(mrb, wout[D_ATTN:, :])
        hn2, vjp_ln2 = jax.vjp(lambda x, g: _rms(x, g, D), h1, ln2[...])
        hn2b = hn2.astype(BF)
        hn2_ref[...] = hn2b
        act, vjp_act = jax.vjp(lambda g, u: jax.nn.silu(g) * u, _nt(hn2b, wg[...]), _nt(hn2b, wu[...]))
        actb = act.astype(BF)
        act_ref[...] = actb
        h2 = h1 + _nn(actb, wd[...])
        row = i * TD + lax.broadcasted_iota(jnp.int32, (TD, 1), 0)
        t = jnp.where(row >= TP, row - TP, row)
        err = jnp.where((t >= N_META) & (t < T), h2 - tgt_ref[...], 0.0)
        loss_ref[...] += jnp.sum(err * err) * (0.5 / D)
        dh2b = (err * (1.0 / D)).astype(BF)
        dh2b_ref[...] = dh2b
        dg, du = vjp_act(_nt(dh2b, wd[...]))
        dgb, dub = dg.astype(BF), du.astype(BF)
        dg_ref[...] = dgb
        du_ref[...] = dub
        dh1n, dln2 = vjp_ln2(_nn(dgb, wg[...]) + _nn(dub, wu[...]))
        dh1 = err * (1.0 / D) + dh1n
        dh1_ref[...] = dh1
        dh1b = dh1.astype(BF)
        dh1b_ref[...] = dh1b
        dmix = _nt(dh1b, wout[...])
        do, dga = vjp_a(dmix[:, :D_ATTN])
        dyr, dgr = vjp_r(dmix[:, D_ATTN:])
        do_ref[...] = do
        dy_ref[...] = dyr
        dga_ref[...] += dga
        dgr_ref[...] += dgr
        dln2_ref[...] += dln2

    rs = lambda n: _row_spec(n, TD)
    acc = lambda n: pl.BlockSpec((1, n), lambda i: (0, 0))
    return pl.pallas_call(
        body, grid=(R // TD,), name="stage_d",
        in_specs=[rs(D), rs(D_ATTN), rs(D_RNN), rs(D), _const_spec((1, D_ATTN)), _const_spec((1, D_RNN)),
                  _const_spec((D, D)), _const_spec((1, D)), _const_spec((D_FF, D)), _const_spec((D_FF, D)),
                  _const_spec((D_FF, D))],
        out_specs=[rs(D_ATTN), rs(D_RNN), rs(D), rs(D), rs(D), rs(D), rs(D_FF), rs(D_FF), rs(D_FF), rs(D),
                   acc(1), acc(D_ATTN), acc(D_RNN), acc(D)],
        out_shape=[jax.ShapeDtypeStruct((R, D_ATTN), F32), jax.ShapeDtypeStruct((R, D_RNN), F32),
                   jax.ShapeDtypeStruct((R, D), F32), jax.ShapeDtypeStruct((R, D), BF),
                   jax.ShapeDtypeStruct((R, D), BF), jax.ShapeDtypeStruct((R, D), BF),
                   jax.ShapeDtypeStruct((R, D_FF), BF), jax.ShapeDtypeStruct((R, D_FF), BF),
                   jax.ShapeDtypeStruct((R, D_FF), BF), jax.ShapeDtypeStruct((R, D), BF),
                   jax.ShapeDtypeStruct((1, 1), F32), jax.ShapeDtypeStruct((1, D_ATTN), F32),
                   jax.ShapeDtypeStruct((1, D_RNN), F32), jax.ShapeDtypeStruct((1, D), F32)],
        compiler_params=_params("arbitrary", vmem=STAGE_D_VMEM),
    )(hp, o, y, tgt, cw["ga"], cw["gr"], cw["wout"], cw["ln2_g"], cw["wg"], cw["wu"], cw["wd"])


TW = 2176


def _wgrad(a, b, name, tk=None):
    ka, nb = a.shape[1], b.shape[1]
    tk = ka if tk is None else tk

    def body(a_ref, b_ref, o_ref):
        @pl.when(pl.program_id(1) == 0)
        def _():
            o_ref[...] = jnp.zeros_like(o_ref)

        o_ref[...] += _tn(a_ref[...].astype(BF), b_ref[...].astype(BF))

    return pl.pallas_call(
        body, grid=(ka // tk, R // TW), name=name,
        in_specs=[pl.BlockSpec((TW, tk), lambda k, r: (r, k)), pl.BlockSpec((TW, nb), lambda k, r: (r, 0))],
        out_specs=pl.BlockSpec((tk, nb), lambda k, r: (k, 0)),
        out_shape=jax.ShapeDtypeStruct((ka, nb), F32),
        compiler_params=_params("arbitrary", "arbitrary"),
    )(a, b)


def _wgrad_heads(dq, dk, dv, cqn, ckvn):
    def body(dq_ref, dk_ref, dv_ref, cqn_ref, ckvn_ref, oq_ref, ok_ref, ov_ref):
        @pl.when(pl.program_id(0) == 0)
        def _():
            for r in (oq_ref, ok_ref, ov_ref):
                r[...] = jnp.zeros_like(r)

        ckvnb = ckvn_ref[...]
        oq_ref[...] += _tn(dq_ref[...], cqn_ref[...])
        ok_ref[...] += _tn(dk_ref[...], ckvnb)
        ov_ref[...] += _tn(dv_ref[...].astype(BF), ckvnb)

    rows = lambda a: pl.BlockSpec((TW, a.shape[1]), lambda r: (r, 0))
    full = lambda m, n: pl.BlockSpec((m, n), lambda r: (0, 0))
    shapes = [(dq.shape[1], cqn.shape[1]), (dk.shape[1], ckvn.shape[1]), (dv.shape[1], ckvn.shape[1])]
    return pl.pallas_call(
        body, grid=(R // TW,), name="wgrad_heads", in_specs=[rows(a) for a in (dq, dk, dv, cqn, ckvn)],
        out_specs=[full(*s) for s in shapes], out_shape=[jax.ShapeDtypeStruct(s, F32) for s in shapes],
        compiler_params=_params("arbitrary"),
    )(dq, dk, dv, cqn, ckvn)


def _rope_tables():
    half = QK_ROPE // 2
    freqs = 1.0 / (ROPE_THETA ** (jnp.arange(half, dtype=F32) / half))
    ang = jnp.arange(TP, dtype=F32)[:, None] * freqs[None, :]
    ones = jnp.ones((TP, QK_NOPE), F32)
    zeros = jnp.zeros((TP, QK_NOPE), F32)
    pad1 = jnp.ones((TP, HP - QK_HEAD), F32)
    pad0 = jnp.zeros((TP, HP - QK_HEAD), F32)
    cs = jnp.concatenate([ones, jnp.cos(ang), jnp.cos(ang), pad1], axis=1)
    sn = jnp.concatenate([zeros, jnp.sin(ang), jnp.sin(ang), pad0], axis=1)
    return jnp.tile(cs, (NB, 1)), jnp.tile(sn, (NB, 1))


def _pad_rows(a, lo, hi):
    return jnp.pad(a, ((0, 0), (lo, hi), (0, 0)))


def _pad_target(target):
    return _pad_rows(target, N_META, TP - T).reshape(R, D)


def _compute_weights(w):
    win_t = w["w_in_t"]
    kr = win_t[O_KR:O_KR + QK_ROPE]
    win = jnp.concatenate([win_t[:O_KR], jnp.zeros((QK_NOPE, D), F32), kr,
                           jnp.zeros((HP - QK_HEAD, D), F32), win_t[O_KR + QK_ROPE:]], axis=0)
    wq = _pad_rows(w["w_uq_t"].reshape(N_HEADS, QK_HEAD, Q_LORA), 0, HP - QK_HEAD)
    wkv = w["w_ukv_t"].reshape(N_HEADS, QK_NOPE + V_HEAD, KV_LORA)
    wk = _pad_rows(wkv[:, :QK_NOPE], 0, HP - QK_NOPE)
    wv = wkv[:, QK_NOPE:].reshape(D_ATTN, KV_LORA)
    gates = jnp.stack([w["lru_wa"][0], w["lru_wi"][0], w["lru_wa"][1], w["lru_wi"][1]])
    blk = gates.reshape(4, N_CG, 2, RNN_BW, RNN_BW)
    dense = jnp.einsum("tcaij,ab->tcaibj", blk, jnp.eye(2, dtype=F32)).reshape(4, N_CG, CG, CG)
    w4 = dense.transpose(1, 2, 0, 3).reshape(N_CG, CG, 4 * CG)
    bias = jnp.stack([w["lru_ba"][0], w["lru_bi"][0], w["lru_ba"][1], w["lru_bi"][1]])
    b4 = bias.reshape(4, N_CG, CG).transpose(1, 0, 2).reshape(N_CG, 1, 4 * CG)
    lam = w["lru_lambda"].reshape(2, N_CG, CG).transpose(1, 0, 2).reshape(N_CG, 1, 2 * CG)
    pad_g = lambda g: jnp.pad(g.reshape(1, QK_HEAD), ((0, 0), (0, HP - QK_HEAD)))
    return dict(
        ln1_g=w["ln1_g"].reshape(1, D), win=win.astype(BF), qa_g=w["q_a_norm_g"].reshape(1, Q_LORA),
        wq=wq.astype(BF).reshape(N_HEADS * HP, Q_LORA), kva_g=w["kv_a_norm_g"].reshape(1, KV_LORA),
        wk=wk.astype(BF).reshape(N_HEADS * HP, KV_LORA), wv=wv.astype(BF),
        q_g=pad_g(w["q_norm_g"]), k_g=pad_g(w["k_norm_g"]),
        conv_w=w["conv_w"].reshape(CONV_W, D_RNN), conv_b=w["conv_b"].reshape(1, D_RNN),
        w4=w4.astype(BF), b4=b4, lam=lam,
        ga=w["attn_out_g"].reshape(1, D_ATTN), gr=w["rnn_out_g"].reshape(1, D_RNN), ln2_g=w["ln2_g"].reshape(1, D),
    )


def _local_step(x, target, meta, w, late_forward, late_weights, early_grads, mid_grads):
    cw = _compute_weights(w)
    cs, sn = _rope_tables()
    hp = jnp.concatenate([jnp.broadcast_to(meta[None], (NB, N_META, D)), x,
                          jnp.zeros((NB, TP - T, D), F32)], axis=1).reshape(R, D)
    tgt = target if target.ndim == 2 else _pad_target(target)

    pa, xr, xg, q, k, v = _stage_a_fwd(hp, cs, sn, cw)
    o, lse = _attn_fwd(q, k, v)
    cw["conv_b"] = cw["conv_b"] + late_forward([o])
    y, hf, hb, af, ab, xc = _rnn_fwd(xr, xg, cw)
    late = late_weights([y])
    cw.update(wout=late["w_out"], wg=late["w_gate_t"], wu=late["w_up_t"], wd=late["w_down"])
    (do, dy, dh1, mixb, dh1b, hn2b, dgb, dub, actb, dh2b, loss, dga, dgr, dln2) = _stage_d(hp, o, y, tgt, cw)
    dwout = _wgrad(mixb, dh1b, "wgrad_out")
    dwg = _wgrad(dgb, hn2b, "wgrad_gate", tk=D_FF // 2)
    dwu = _wgrad(dub, hn2b, "wgrad_up", tk=D_FF // 2)
    dwd = _wgrad(actb, dh2b, "wgrad_down", tk=D_FF // 2)
    zero = early_grads(dict(w_out=dwout, w_gate=dwg, w_up=dwu, w_down=dwd))
    cw["conv_b"] = cw["conv_b"] + zero
    dxr, dxg, dcw, dcb, dw4, db4, dlam = _rnn_bwd(dy, xr, xg, hf, hb, af, ab, xc, cw)
    zero = mid_grads([dxr])
    dq, dk, dv = _attn_bwd(q, k, v, o, lse, do)
    (dhp, dpb, dqrawb, dkrawb, hn1b, cqnb, ckvnb, dln1, dqag, dkvag, dqg, dkg) = _stage_a_bwd(
        dq, dk, dv, dxr, dxg, dh1, hp, pa, cs, sn, dict(cw, qa_g=cw["qa_g"] + zero))

    dwin = _wgrad(dpb, hn1b, "wgrad_in", tk=PC // 2)
    dwq, dwk, dwv = _wgrad_heads(dqrawb, dkrawb, dv, cqnb, ckvnb)

    dwin_t = jnp.concatenate([dwin[:O_KR], dwin[O_KR + QK_NOPE:O_KR + QK_HEAD], dwin[O_XR:]], axis=0)
    dwq_t = dwq.reshape(N_HEADS, HP, Q_LORA)[:, :QK_HEAD].reshape(N_HEADS * QK_HEAD, Q_LORA)
    dwkv_t = jnp.concatenate([dwk.reshape(N_HEADS, HP, KV_LORA)[:, :QK_NOPE],
                              dwv.reshape(N_HEADS, V_HEAD, KV_LORA)], axis=1).reshape(2 * D_ATTN, KV_LORA)
    d4 = dw4.reshape(N_CG, 2, RNN_BW, 4, 2, RNN_BW)
    dgates = jnp.stack([d4[:, 0, :, :, 0, :], d4[:, 1, :, :, 1, :]], axis=1)
    dgates = dgates.transpose(3, 0, 1, 2, 4).reshape(4, N_HEADS, RNN_BW, RNN_BW)
    dbias = db4.reshape(N_CG, 4, CG).transpose(1, 0, 2).reshape(4, D_RNN)
    dhp3 = dhp.reshape(NB, TP, D)
    grads = dict(
        meta_tokens=jnp.sum(dhp3[:, :N_META], axis=0),
        ln1_g=dln1, w_in_t=dwin_t, q_a_norm_g=dqag, w_uq_t=dwq_t, kv_a_norm_g=dkvag, w_ukv_t=dwkv_t,
        q_norm_g=dqg[:, :QK_HEAD], k_norm_g=dkg[:, :QK_HEAD], conv_w=dcw[None], conv_b=dcb,
        lru_wa=jnp.stack([dgates[0], dgates[2]])[None], lru_ba=jnp.stack([dbias[0], dbias[2]])[None],
        lru_wi=jnp.stack([dgates[1], dgates[3]])[None], lru_bi=jnp.stack([dbias[1], dbias[3]])[None],
        lru_lambda=dlam.reshape(N_CG, 2, CG).transpose(1, 0, 2).reshape(1, 2, D_RNN),
        attn_out_g=dga, rnn_out_g=dgr, ln2_g=dln2,
    )
    return loss[0, 0], dhp3[:, N_META:T], grads, [dhp, dwin]


_ANY = pl.BlockSpec(memory_space=pl.ANY)


def _place():
    return lax.axis_index("x"), lax.axis_index("y"), lax.axis_index("c")


def _other_chips(x, y):
    return [(1 - x, y), (x, 1 - y), (1 - x, 1 - y)]


def _pair_exchange(big, whole, name):
    n_s, _, m, n = big.shape
    n_copies = n_s + len(whole)

    def body(*refs):
        big_ref, whole_refs = refs[0], refs[1:1 + len(whole)]
        rbig_ref, rwhole_refs = refs[1 + len(whole)], refs[2 + len(whole):2 + 2 * len(whole)]
        send_sems, recv_sems = refs[-2:]
        x, y, c = _place()
        sibling = (x, y, 1 - c)
        copies = [pltpu.make_async_remote_copy(
            src_ref=big_ref.at[s, 1 - c], dst_ref=rbig_ref.at[s], send_sem=send_sems.at[s], recv_sem=recv_sems.at[s],
            device_id=sibling, device_id_type=MESH) for s in range(n_s)]
        copies += [pltpu.make_async_remote_copy(
            src_ref=a, dst_ref=r, send_sem=send_sems.at[n_s + i], recv_sem=recv_sems.at[n_s + i],
            device_id=sibling, device_id_type=MESH) for i, (a, r) in enumerate(zip(whole_refs, rwhole_refs))]
        for cp in copies:
            cp.start()
        for cp in copies:
            cp.wait()

    return pl.pallas_call(
        body, name=name,
        out_shape=[jax.ShapeDtypeStruct((n_s, m, n), big.dtype)] + [jax.ShapeDtypeStruct(a.shape, a.dtype) for a in whole],
        in_specs=[_ANY] * (1 + len(whole)), out_specs=[_ANY] * (1 + len(whole)),
        scratch_shapes=[pltpu.SemaphoreType.DMA((n_copies,)), pltpu.SemaphoreType.DMA((n_copies,))],
    )(big, *whole)


_HBM = pl.BlockSpec(memory_space=pltpu.HBM)
_SEM = pl.BlockSpec(memory_space=pltpu.SEMAPHORE)
_EFFECT = pltpu.SideEffectType.DATAFLOW_SIDE_EFFECTING


def _split_copies(src_refs, land_refs, sems, plan, sending):
    n = len(sems) // 2
    return [pltpu.make_async_remote_copy(src_ref=s, dst_ref=d, send_sem=sems[k], recv_sem=sems[n + k], device_id=to,
                                         device_id_type=MESH)
            for k, (s, d, to) in enumerate(plan(src_refs, land_refs, sending))]


def _to_chips(src_at, land_at):
    def plan(src_refs, land_refs, sending):
        x, y, c = _place()
        return [(src_at(s, tx, ty, c), land_at(l, j, *((x, y) if sending else (tx, ty)), c), (tx, ty, c))
                for s, l in zip(src_refs, land_refs) for j, (tx, ty) in enumerate(_other_chips(x, y))]
    return plan


def _to_sibling(src_refs, land_refs, sending):
    x, y, c = _place()
    return [(s.at[k, 1 - c], l.at[k], (x, y, 1 - c)) for s, l in zip(src_refs, land_refs) for k in range(N_CHIPS)]


def _split_start(name, srcs, lands, plan, n, after=()):
    srcs, lands, after = list(srcs), list(lands), list(after)
    k, kb = len(srcs), len(srcs) + len(lands)

    def body(*refs):
        outs = refs[kb + len(after):]
        for cp in _split_copies(refs[:k], refs[k:kb], outs[:2 * n], plan, True):
            cp.start()
        outs[2 * n + kb][...] = jnp.zeros_like(outs[2 * n + kb])

    outs = pl.pallas_call(
        body, name=name,
        out_shape=(pltpu.SemaphoreType.DMA(()),) * (2 * n) + tuple(pltpu.HBM(a.shape, a.dtype) for a in srcs + lands)
        + (jax.ShapeDtypeStruct((8, LANES), F32),),
        in_specs=(_HBM,) * kb + (_ANY,) * len(after),
        out_specs=(_SEM,) * (2 * n) + (_HBM,) * kb + (pl.BlockSpec(memory_space=pltpu.VMEM),),
        input_output_aliases={i: 2 * n + i for i in range(kb)},
        compiler_params=pltpu.CompilerParams(has_side_effects=_EFFECT),
    )(*[pltpu.with_memory_space_constraint(a, pltpu.HBM) for a in srcs + lands], *after)
    return outs[:2 * n], list(outs[2 * n:2 * n + k]), list(outs[2 * n + k:2 * n + kb]), outs[2 * n + kb]


def _split_wait(name, sems, srcs, lands, after, plan):
    srcs, lands = list(srcs), list(lands)
    k, kb = len(srcs), len(srcs) + len(lands)

    def body(*refs):
        for cp in _split_copies(refs[:k], refs[k:kb], refs[kb:kb + len(sems)], plan, False):
            cp.wait_send()
            cp.wait_recv()

    outs = pl.pallas_call(
        body, name=name, out_shape=tuple(pltpu.HBM(a.shape, a.dtype) for a in srcs + lands),
        in_specs=(_HBM,) * kb + (_SEM,) * len(sems) + (_ANY,) * len(after), out_specs=(_HBM,) * kb,
        input_output_aliases={i: i for i in range(kb)}, compiler_params=pltpu.CompilerParams(has_side_effects=_EFFECT),
    )(*srcs, *lands, *sems, *after)
    return list(outs[:k]), list(outs[k:])


def _forward_landed(src_refs, land_refs, sending):
    x, y, c = _place()
    copies = []
    for ref in src_refs:
        m = ref.shape[0] // 8
        for tx, ty in _other_chips(x, y):
            rows = ref.at[pl.ds((4 * tx + 2 * ty + (c if sending else 1 - c)) * m, m), :]
            copies.append((rows, rows, (x, y, 1 - c)))
    return copies


def _place_own(pieces):
    k = len(pieces)

    def body(*refs):
        piece_refs, out_refs, stages = refs[:k], refs[k:2 * k], refs[2 * k:3 * k]
        load_sems, store_sems = refs[3 * k:]
        x, y, _ = _place()
        loads = [pltpu.make_async_copy(piece_refs[a], stages[a], load_sems.at[a]) for a in range(k)]
        stores = [pltpu.make_async_copy(
            stages[a], out_refs[a].at[pl.ds((2 * x + y) * pieces[a].shape[0], pieces[a].shape[0]), :], store_sems.at[a])
            for a in range(k)]
        for cp in loads:
            cp.start()
        for ld, st in zip(loads, stores):
            ld.wait()
            st.start()
        for cp in stores:
            cp.wait()

    return pl.pallas_call(
        body, name="gather_late_place_own",
        out_shape=[jax.ShapeDtypeStruct((N_CHIPS * p.shape[0], p.shape[1]), p.dtype) for p in pieces],
        in_specs=[_ANY] * k, out_specs=[_ANY] * k,
        scratch_shapes=[pltpu.VMEM(p.shape, p.dtype) for p in pieces]
        + [pltpu.SemaphoreType.DMA((k,)), pltpu.SemaphoreType.DMA((k,))],
    )(*pieces)


def _gather_finish(lands, pieces, name):
    k = len(lands)

    def body(*refs):
        land_refs, piece_refs, out_refs, stages = refs[:k], refs[k:2 * k], refs[2 * k:3 * k], refs[3 * k:4 * k]
        send_sems, recv_sems, load_sems, store_sems = refs[4 * k:]
        x, y, c = _place()
        sibling = (x, y, 1 - c)
        remote, loads, stores, arrivals = [], [], [], []
        for a in range(k):
            m = lands[a].shape[0] // 8

            def rows(px, py, pc, ref, m=m):
                return ref.at[pl.ds((4 * px + 2 * py + pc) * m, m), :]

            for j, (tx, ty) in enumerate(_other_chips(x, y)):
                sems = dict(send_sem=send_sems.at[3 * a + j], recv_sem=recv_sems.at[3 * a + j], device_id=sibling,
                            device_id_type=MESH)
                remote.append(pltpu.make_async_remote_copy(
                    src_ref=rows(tx, ty, c, land_refs[a]), dst_ref=rows(tx, ty, c, out_refs[a]), **sems))
                arrivals.append(pltpu.make_async_remote_copy(
                    src_ref=rows(tx, ty, 1 - c, out_refs[a]), dst_ref=rows(tx, ty, 1 - c, out_refs[a]), **sems))
            for h in range(2):
                loads.append(pltpu.make_async_copy(piece_refs[a].at[pl.ds(h * m, m), :], stages[a].at[h],
                                                   load_sems.at[2 * a + h]))
                stores.append(pltpu.make_async_copy(stages[a].at[h], rows(x, y, h, out_refs[a]), store_sems.at[2 * a + h]))
        for cp in remote + loads:
            cp.start()
        for ld, st in zip(loads, stores):
            ld.wait()
            st.start()
        for cp, arrival in zip(remote, arrivals):
            cp.wait_send()
            arrival.wait_recv()
        for cp in stores:
            cp.wait()

    return pl.pallas_call(
        body, name=name, out_shape=[jax.ShapeDtypeStruct(a.shape, a.dtype) for a in lands],
        in_specs=[_ANY] * (2 * k), out_specs=[_ANY] * k, input_output_aliases={i: i for i in range(k)},
        scratch_shapes=[pltpu.VMEM((2, a.shape[0] // 8, a.shape[1]), a.dtype) for a in lands]
        + [pltpu.SemaphoreType.DMA((3 * k,)), pltpu.SemaphoreType.DMA((3 * k,)), pltpu.SemaphoreType.DMA((2 * k,)),
           pltpu.SemaphoreType.DMA((2 * k,))],
    )(*lands, *pieces)


def _pair_fill(bufs, name):
    k = len(bufs)

    def body(*refs):
        send_sems, recv_sems = refs[-2:]
        x, y, c = _place()
        copies = [pltpu.make_async_remote_copy(
            src_ref=refs[i].at[c], dst_ref=refs[k + i].at[c], send_sem=send_sems.at[i], recv_sem=recv_sems.at[i],
            device_id=(x, y, 1 - c), device_id_type=MESH) for i in range(k)]
        for cp in copies:
            cp.start()
        for i, cp in enumerate(copies):
            cp.wait_send()
            pltpu.make_async_remote_copy(
                src_ref=refs[i].at[1 - c], dst_ref=refs[k + i].at[1 - c], send_sem=send_sems.at[i],
                recv_sem=recv_sems.at[i], device_id=(x, y, 1 - c), device_id_type=MESH).wait_recv()

    return pl.pallas_call(
        body, name=name, out_shape=[jax.ShapeDtypeStruct(a.shape, a.dtype) for a in bufs], in_specs=[_ANY] * k,
        out_specs=[_ANY] * k, input_output_aliases={i: i for i in range(k)},
        scratch_shapes=[pltpu.SemaphoreType.DMA((k,)), pltpu.SemaphoreType.DMA((k,))],
    )(*bufs)


def _row_tile(rows, cap=512):
    for t in range(cap - cap % 8, 7, -8):
        if rows % t == 0:
            return t
    return rows


def _elementwise(fn, n_out, name, *arrs, out_dtype=F32):
    rows, cols = arrs[0].shape
    tr = _row_tile(rows)
    n_in = len(arrs)

    def body(*refs):
        outs = fn(*[r[...].astype(F32) for r in refs[:n_in]])
        for r, o in zip(refs[n_in:], outs):
            r[...] = o.astype(out_dtype)

    spec = pl.BlockSpec((tr, cols), lambda i: (i, 0))
    return pl.pallas_call(
        body, grid=(rows // tr,), name=name, in_specs=[spec] * n_in, out_specs=[spec] * n_out,
        out_shape=[jax.ShapeDtypeStruct((rows, cols), out_dtype)] * n_out, compiler_params=_params("arbitrary"),
    )(*arrs)


def _pair_sums(gpacks, rbigs, ci, name):
    k = len(gpacks)

    def body(c_ref, *refs):
        for g_ref, r_ref, o_ref in zip(refs[:k], refs[k:2 * k], refs[2 * k:]):
            o_ref[...] = (g_ref[...] + r_ref[...]).astype(BF)

    half = lambda a: pl.BlockSpec((None,) + a.shape[1:], lambda s, c: (s, 0, 0))
    return pl.pallas_call(
        body, name=name, out_shape=[jax.ShapeDtypeStruct(r.shape, BF) for r in rbigs],
        grid_spec=pltpu.PrefetchScalarGridSpec(
            num_scalar_prefetch=1, grid=(N_CHIPS,),
            in_specs=[pl.BlockSpec((None, None) + g.shape[2:], lambda s, c: (s, c[0], 0, 0)) for g in gpacks]
            + [half(r) for r in rbigs],
            out_specs=[half(r) for r in rbigs]),
        compiler_params=_params("arbitrary"),
    )(ci.reshape(1), *gpacks, *rbigs)


def _chip_sums(sums, landed, chip, ci, name):
    k = len(sums)

    def body(p_ref, *refs):
        for own_ref, land_ref, o_ref in zip(refs[:k], refs[k:2 * k], refs[2 * k:]):
            f = lambda v: v.astype(F32)
            o_ref[...] = _add4(f(own_ref[...]), f(land_ref[0]), f(land_ref[1]), f(land_ref[2]))[0]

    return pl.pallas_call(
        body, name=name, out_shape=[jax.ShapeDtypeStruct((2,) + s.shape[1:], F32) for s in sums],
        grid_spec=pltpu.PrefetchScalarGridSpec(
            num_scalar_prefetch=1, grid=(1,),
            in_specs=[pl.BlockSpec((None,) + s.shape[1:], lambda i, p: (p[0], 0, 0)) for s in sums]
            + [pl.BlockSpec(l.shape, lambda i, p: (0, 0, 0)) for l in landed],
            out_specs=[pl.BlockSpec((None,) + s.shape[1:], lambda i, p: (p[1], 0, 0)) for s in sums]),
        compiler_params=_params("arbitrary"),
    )(jnp.stack([chip, ci]), *sums, *landed)


def _add2(a, b):
    return (a + b,)


def _add4(own, r0, r1, r2):
    return ((own + r2) + (r0 + r1),)


def _adamw_small(ws, gs, ms, vs):
    k = len(ws)

    def body(*refs):
        for i in range(k):
            outs = _adamw_math(*[refs[j * k + i][...] for j in range(4)])
            for j, o in enumerate(outs):
                refs[(4 + j) * k + i][...] = o

    return pl.pallas_call(
        body, name="adamw_small", out_shape=[jax.ShapeDtypeStruct(w.shape, F32) for w in ws] * 3,
    )(*ws, *gs, *ms, *vs)


def _adamw_math(w, g, m, v):
    m = ADAM_B1 * m + (1.0 - ADAM_B1) * g
    v = ADAM_B2 * v + (1.0 - ADAM_B2) * (g * g)
    m_hat = m / (1.0 - ADAM_B1 ** ADAM_STEP)
    v_hat = v / (1.0 - ADAM_B2 ** ADAM_STEP)
    delta = -ADAM_LR * (m_hat / (jnp.sqrt(v_hat) + ADAM_EPS) + ADAM_WD * w)
    return delta, m, v


WEIGHTS = ["meta_tokens", "ln1_g", "w_in", "q_a_norm_g", "w_uq", "kv_a_norm_g", "w_ukv", "q_norm_g", "k_norm_g",
           "conv_w", "conv_b", "lru_wa", "lru_ba", "lru_wi", "lru_bi", "lru_lambda", "attn_out_g", "rnn_out_g",
           "w_out", "ln2_g", "w_gate", "w_up", "w_down"]
BIG = ["w_in", "w_uq", "w_ukv", "w_out", "w_gate", "w_up", "w_down"]
BIG_T = {"w_in": True, "w_uq": True, "w_ukv": True, "w_out": False, "w_gate": True, "w_up": True, "w_down": False}
BIG_ROWS = {"w_in": 424, "w_uq": 72, "w_ukv": 64, "w_out": 256, "w_gate": 704, "w_up": 704, "w_down": 704}
EARLY = ["w_in", "w_uq", "w_ukv"]
LATE = ["w_out", "w_gate", "w_up", "w_down"]
EARLY_ROWS = 576
SMALL_SHARDED = ["meta_tokens", "conv_w", "lru_ba", "lru_bi", "lru_lambda"]
SMALL = [n for n in WEIGHTS if n not in BIG]
SMALL_PACK_ROWS = 160


def _offsets(names):
    off, o = {}, 0
    for n in names:
        off[n] = o
        o += BIG_ROWS[n]
    return off


def _shard_pack(names, src, rows):
    parts = [_to_pack_piece(n, src[n]) for n in names]
    used = sum(BIG_ROWS[n] for n in names)
    if rows > used:
        parts.append(jnp.zeros((rows - used, D), F32))
    return jnp.concatenate(parts, axis=0)


def _grad_pack(names, g, rows):
    parts = [g[n].reshape(N_CHIPS, BIG_ROWS[n], D) for n in names]
    used = sum(BIG_ROWS[n] for n in names)
    if rows > used:
        parts.append(jnp.zeros((N_CHIPS, rows - used, D), F32))
    return jnp.concatenate(parts, axis=1).reshape(N_CHIPS, 2, rows // 2, D)


def _to_pack_piece(name, shard):
    a = shard[0].T if BIG_T[name] else shard[0]
    return a.reshape(BIG_ROWS[name], D)


def _flat_pack(arrs, rows):
    flat = jnp.concatenate([a.reshape(-1) for a in arrs])
    return jnp.pad(flat, (0, rows * D - flat.shape[0])).reshape(rows, D)


def _flat_unpack(pack, shapes):
    flat, out, o = pack.reshape(-1), [], 0
    for s in shapes:
        n = math.prod(s)
        out.append(flat[o:o + n].reshape(s))
        o += n
    return out


def kernel(x, meta_tokens, ln1_g, w_in, q_a_norm_g, w_uq, kv_a_norm_g, w_ukv, q_norm_g, k_norm_g, conv_w, conv_b, lru_wa, lru_ba, lru_wi, lru_bi, lru_lambda, attn_out_g, rnn_out_g, w_out, ln2_g, w_gate, w_up, w_down, loss_target, m_meta_tokens, m_ln1_g, m_w_in, m_q_a_norm_g, m_w_uq, m_kv_a_norm_g, m_w_ukv, m_q_norm_g, m_k_norm_g, m_conv_w, m_conv_b, m_lru_wa, m_lru_ba, m_lru_wi, m_lru_bi, m_lru_lambda, m_attn_out_g, m_rnn_out_g, m_w_out, m_ln2_g, m_w_gate, m_w_up, m_w_down, v_meta_tokens, v_ln1_g, v_w_in, v_q_a_norm_g, v_w_uq, v_kv_a_norm_g, v_w_ukv, v_q_norm_g, v_k_norm_g, v_conv_w, v_conv_b, v_lru_wa, v_lru_ba, v_lru_wi, v_lru_bi, v_lru_lambda, v_attn_out_g, v_rnn_out_g, v_w_out, v_ln2_g, v_w_gate, v_w_up, v_w_down):
    wts = dict(zip(WEIGHTS, (meta_tokens, ln1_g, w_in, q_a_norm_g, w_uq, kv_a_norm_g, w_ukv, q_norm_g, k_norm_g, conv_w, conv_b, lru_wa, lru_ba, lru_wi, lru_bi, lru_lambda, attn_out_g, rnn_out_g, w_out, ln2_g, w_gate, w_up, w_down)))
    mom = dict(zip(WEIGHTS, (m_meta_tokens, m_ln1_g, m_w_in, m_q_a_norm_g, m_w_uq, m_kv_a_norm_g, m_w_ukv, m_q_norm_g, m_k_norm_g, m_conv_w, m_conv_b, m_lru_wa, m_lru_ba, m_lru_wi, m_lru_bi, m_lru_lambda, m_attn_out_g, m_rnn_out_g, m_w_out, m_ln2_g, m_w_gate, m_w_up, m_w_down)))
    var = dict(zip(WEIGHTS, (v_meta_tokens, v_ln1_g, v_w_in, v_q_a_norm_g, v_w_uq, v_kv_a_norm_g, v_w_ukv, v_q_norm_g, v_k_norm_g, v_conv_w, v_conv_b, v_lru_wa, v_lru_ba, v_lru_wi, v_lru_bi, v_lru_lambda, v_attn_out_g, v_rnn_out_g, v_w_out, v_ln2_g, v_w_gate, v_w_up, v_w_down)))
    xi, yi, ci = _place()
    chip = 2 * xi + yi
    off_e = _offsets(EARLY)
    half_e = EARLY_ROWS // 2
    gather_plan = _to_chips(lambda ref, tx, ty, c: ref.at[pl.ds(c * (ref.shape[0] // 2), ref.shape[0] // 2), :],
                            lambda ref, j, px, py, c: ref.at[pl.ds((4 * px + 2 * py + c) * (ref.shape[0] // 8),
                                                                   ref.shape[0] // 8), :])
    scatter_plan = _to_chips(lambda ref, tx, ty, c: ref.at[2 * tx + ty], lambda ref, j, px, py, c: ref.at[j])
    everywhere = _to_chips(lambda ref, tx, ty, c: ref, lambda ref, j, px, py, c: ref.at[j])
    n_late = len(LATE)

    pack_e = _shard_pack(EARLY, wts, EARLY_ROWS).astype(BF)
    spack = jnp.concatenate([meta_tokens[:, :LANES], meta_tokens[:, LANES:], conv_w[0], lru_ba[0], lru_bi[0],
                             lru_lambda[0], jnp.zeros((6, LANES), F32)], axis=0)
    sems_g, src_g, land_g, _ = _split_start(
        "gather_early_start", [pack_e, spack], [lax.empty((N_CHIPS * EARLY_ROWS, D), BF), lax.empty((N_CHIPS * 48, LANES), F32)],
        gather_plan, 6)
    tgt_padded = _pad_target(loss_target)
    pieces_l = [_to_pack_piece(n, wts[n]).astype(BF) for n in LATE]
    lands_l = list(_place_own(pieces_l))
    src_g, land_g = _split_wait("gather_early_wait", sems_g, src_g, land_g, [tgt_padded] + lands_l, gather_plan)
    ge, gs = _gather_finish(land_g, src_g, "gather_early_finish")
    ge = ge.reshape(N_CHIPS, EARLY_ROWS, D)
    gs = gs.reshape(N_CHIPS, 48, LANES)
    full = {n: ge[:, off_e[n]:off_e[n] + BIG_ROWS[n]] for n in EARLY}
    sems_l, src_l, land_l, tied = _split_start("gather_late_start", pieces_l, lands_l, gather_plan, 3 * n_late, after=[ge])

    forward, pair, late = {}, {}, {}

    def late_forward(after):
        _, landed = _split_wait("gather_late_wait", sems_l, src_l, land_l, after, gather_plan)
        forward["sems"], forward["src"], _, zeros = _split_start(
            "gather_late_forward_start", landed, [], _forward_landed, 3 * n_late)
        return zeros[0, 0]

    def late_weights(after):
        (w_out_, w_gate_, w_up_, w_down_), _ = _split_wait(
            "gather_late_forward_wait", forward["sems"], forward["src"], [], after, _forward_landed)
        return dict(w_out=w_out_, w_gate_t=w_gate_, w_up_t=w_up_, w_down=w_down_)

    def early_grads(g_late):
        halves = [g_late[n].reshape(N_CHIPS, 2, BIG_ROWS[n] // 2, D) for n in LATE]
        pair["sems"], pair["src"], pair["land"], zeros = _split_start(
            "grad_pair_late_start", halves, [lax.empty((N_CHIPS, BIG_ROWS[n] // 2, D), F32) for n in LATE], _to_sibling,
            N_CHIPS * n_late)
        return zeros[0, 0]

    def mid_grads(after):
        halves, landed = _split_wait("grad_pair_late_wait", pair["sems"], pair["src"], pair["land"], after, _to_sibling)
        chip_sums = _pair_sums(halves, landed, ci, "grad_pair_sum_late")
        late["sems"], late["src"], late["land"], zeros = _split_start(
            "grad_chip_late_start", chip_sums, [lax.empty((3, BIG_ROWS[n] // 2, D), BF) for n in LATE], scatter_plan,
            3 * n_late)
        return zeros[0, 0]

    cols = lambda a: a.transpose(1, 0, 2).reshape(a.shape[1], N_CHIPS * a.shape[2])
    meta_full = cols(jnp.concatenate([gs[:, 0:16], gs[:, 16:32]], axis=2))
    w = dict(
        w_in_t=full["w_in"].reshape(IN_COLS, D), w_uq_t=full["w_uq"].reshape(N_HEADS * QK_HEAD, Q_LORA),
        w_ukv_t=full["w_ukv"].reshape(2 * D_ATTN, KV_LORA),
        ln1_g=ln1_g, q_a_norm_g=q_a_norm_g, kv_a_norm_g=kv_a_norm_g, q_norm_g=q_norm_g, k_norm_g=k_norm_g,
        conv_w=cols(gs[:, 32:36]), conv_b=conv_b, lru_wa=lru_wa[0], lru_ba=cols(gs[:, 36:38]), lru_wi=lru_wi[0],
        lru_bi=cols(gs[:, 38:40]), lru_lambda=cols(gs[:, 40:42]), attn_out_g=attn_out_g, rnn_out_g=rnn_out_g,
        ln2_g=ln2_g,
    )

    loss_local, grad_x, g, last = _local_step(x, tgt_padded, meta_full + tied[0, 0], w, late_forward, late_weights,
                                              early_grads, mid_grads)

    gpack = _grad_pack(EARLY, {"w_in": g["w_in_t"], "w_uq": g["w_uq_t"], "w_ukv": g["w_ukv_t"]}, EARLY_ROWS)
    full_shapes = {n: wts[n].shape for n in SMALL}
    full_shapes.update(meta_tokens=(N_META, D), conv_w=(1, CONV_W, D_RNN), lru_ba=(1, 2, D_RNN), lru_bi=(1, 2, D_RNN),
                       lru_lambda=(1, 2, D_RNN))
    gsmall = _flat_pack([g[n] for n in SMALL] + [loss_local], SMALL_PACK_ROWS)
    rbig, rsmall = _pair_exchange(gpack, [gsmall], "grad_pair_exchange")
    chip_big = _pair_sums([gpack], [rbig], ci, "grad_pair_sum")
    (chip_small,) = _elementwise(_add2, 1, "grad_pair_sum_small", gsmall, rsmall)
    sems_e, src_e, land_e, zero_e = _split_start(
        "grad_chip_early_start", chip_big, [lax.empty((3, half_e, D), BF)], scatter_plan, 3)
    sems_s, src_s, land_s, zero_s = _split_start(
        "grad_small_start", [chip_small], [lax.empty((3, SMALL_PACK_ROWS, D), F32)], everywhere, 3)

    grads, delta, new_m, new_v = {}, {}, {}, {}

    def adamw_big(n, gshard):
        _, k, cols = wts[n].shape
        as_rows = (lambda a: a[0].T) if BIG_T[n] else (lambda a: a[0])
        back = (lambda a: a.T[None]) if BIG_T[n] else (lambda a: a[None])
        g2 = gshard.reshape((cols, k) if BIG_T[n] else (k, cols))
        d_, m_, v_ = _elementwise(_adamw_math, 3, "adamw_" + n, as_rows(wts[n]), g2, as_rows(mom[n]), as_rows(var[n]))
        grads[n], delta[n], new_m[n], new_v[n] = back(g2), back(d_), back(m_), back(v_)
        return d_

    sums, landed = _split_wait("grad_chip_late_wait", late["sems"], late["src"], late["land"], last + [zero_e, zero_s],
                               scatter_plan)
    shards_l = _pair_fill(_chip_sums(sums, landed, chip, ci, "grad_chip_sum_late"), "grad_pair_fill_late")
    done_late = [adamw_big(n, buf) for n, buf in zip(LATE, shards_l)][-1]
    src_e, land_e = _split_wait("grad_chip_early_wait", sems_e, src_e, land_e, [done_late], scatter_plan)
    src_s, land_s = _split_wait("grad_small_wait", sems_s, src_s, land_s, [done_late], everywhere)
    (shard_e,) = _pair_fill(_chip_sums(src_e, land_e, chip, ci, "grad_chip_sum"), "grad_pair_fill_early")
    shard_e = shard_e.reshape(EARLY_ROWS, D)
    for n in EARLY:
        adamw_big(n, shard_e[off_e[n]:off_e[n] + BIG_ROWS[n]])
    (small_sum,) = _elementwise(_add4, 1, "grad_chip_sum_small", src_s[0], land_s[0][0], land_s[0][1], land_s[0][2])
    *small_grads, loss = _flat_unpack(small_sum, [full_shapes[n] for n in SMALL] + [()])
    small_full = dict(zip(SMALL, small_grads))
    for n in SMALL:
        a = small_full[n]
        if n in SMALL_SHARDED:
            width = wts[n].shape[-1]
            a = lax.dynamic_slice_in_dim(a, chip * width, width, axis=a.ndim - 1)
        grads[n] = a.reshape(wts[n].shape)

    rows_of = lambda a: a.reshape(-1, a.shape[-1])
    outs = _adamw_small(*[[rows_of(src[n]) for n in SMALL] for src in (wts, grads, mom, var)])
    for j, dst in enumerate((delta, new_m, new_v)):
        dst.update({n: outs[j * len(SMALL) + i].reshape(wts[n].shape) for i, n in enumerate(SMALL)})

    return (loss, grad_x, *[grads[n] for n in WEIGHTS], *[delta[n] for n in WEIGHTS],
            *[new_m[n] for n in WEIGHTS], *[new_v[n] for n in WEIGHTS])
```

```python
import math

import jax
import jax.numpy as jnp
from jax import lax
from jax.experimental import pallas as pl
from jax.experimental.pallas import tpu as pltpu

F32 = jnp.float32
BF = jnp.bfloat16
MESH = pl.DeviceIdType.MESH

D = 1024
SEQ = 2048
N_META = 16
T = N_META + SEQ
N_HEADS = 8
QK_NOPE = 64
QK_ROPE = 32
QK_HEAD = 96
V_HEAD = 64
Q_LORA = 384
KV_LORA = 256
D_ATTN = 512
D_RNN = 512
RNN_BW = 64
CONV_W = 4
LRU_C = 8.0
ROPE_THETA = 10000.0
D_FF = 2816
EPS = 1e-6
IN_COLS = 1696
ADAM_LR, ADAM_B1, ADAM_B2, ADAM_EPS, ADAM_WD, ADAM_STEP = 0.001, 0.9, 0.999, 1e-08, 0.01, 10

LANES = 128
TP = 2176
NB = 2
R = NB * TP
TR = 256
TRF = 256
TQ = 1088
HP = LANES
PC = 1792
O_CKV, O_KR, O_XR, O_XG = 384, 640, 768, 1280
CG = 128
N_CG = D_RNN // CG
VMEM_LIMIT = 56 * 1024 * 1024
N_CHIPS = 4
SCALE = QK_HEAD ** -0.5
KEY_MASK = -30000.0
LOG2_E = 1.4426950408889634
SCALE_LOG2 = SCALE * LOG2_E


def _nt(a, b):
    return lax.dot_general(a, b, (((1,), (1,)), ((), ())), preferred_element_type=F32)


def _nn(a, b):
    return jnp.dot(a, b, preferred_element_type=F32)


def _tn(a, b):
    return lax.dot_general(a, b, (((0,), (0,)), ((), ())), preferred_element_type=F32)


def _rms(x, g, n):
    ms = jnp.sum(x * x, axis=-1, keepdims=True) * (1.0 / n)
    return x * lax.rsqrt(ms + EPS) * g


def _lane_sum(y):
    return jnp.sum(y, axis=-1, keepdims=True)


def _rot(x):
    lane = lax.broadcasted_iota(jnp.int32, x.shape, 1)
    left = pltpu.roll(x, HP - 16, 1)
    right = pltpu.roll(x, 16, 1)
    lo = (lane >= QK_NOPE) & (lane < QK_NOPE + 16)
    hi = (lane >= QK_NOPE + 16) & (lane < QK_HEAD)
    return jnp.where(lo, -left, jnp.where(hi, right, 0.0))


def _head(x, g, cs, sn):
    n = x * lax.rsqrt(_lane_sum(x * x) * (1.0 / QK_HEAD) + EPS) * g
    return n * cs + _rot(n) * sn


def _head_bwd(x, g, cs, sn, dout):
    rs = lax.rsqrt(_lane_sum(x * x) * (1.0 / QK_HEAD) + EPS)
    xh = x * rs
    dn = dout * cs - _rot(dout * sn)
    gdn = g * dn
    t = _lane_sum(gdn * xh) * (1.0 / QK_HEAD)
    return rs * (gdn - xh * t), jnp.sum(dn * xh, axis=0, keepdims=True)


def _const_spec(shape):
    return pl.BlockSpec(shape, lambda *_: (0,) * len(shape), pipeline_mode=pl.Buffered(1))


def _row_spec(n, tr=TR):
    return pl.BlockSpec((tr, n), lambda i: (i, 0))


def _params(*sem, vmem=VMEM_LIMIT):
    return pltpu.CompilerParams(dimension_semantics=sem, vmem_limit_bytes=vmem)


def _stage_a_fwd(hp, cs, sn, cw):
    def body(hp_ref, cs_ref, sn_ref, ln1, win, qag, wq, kvag, wk, wv, qg, kg,
             pa_ref, xr_ref, xg_ref, q_ref, k_ref, v_ref):
        hn = _rms(hp_ref[...], ln1[...], D).astype(BF)
        p = _nt(hn, win[...])
        pa_ref[...] = p[:, :O_XR]
        xr_ref[...] = p[:, O_XR:O_XG]
        xg_ref[...] = p[:, O_XG:]
        cqn = _rms(p[:, :O_CKV], qag[...], Q_LORA).astype(BF)
        ckvn = _rms(p[:, O_CKV:O_KR], kvag[...], KV_LORA).astype(BF)
        kr = p[:, O_KR:O_XR]
        c, s = cs_ref[...], sn_ref[...]
        mask_lane = lax.broadcasted_iota(jnp.int32, (1, HP), 1) == QK_HEAD
        row = pl.program_id(0) * TRF + lax.broadcasted_iota(jnp.int32, (TRF, 1), 0)
        key_mask = jnp.where(jnp.where(row >= TP, row - TP, row) < T, 0.0, KEY_MASK)
        qraw = _nt(cqn, wq[...])
        kraw = _nt(ckvn, wk[...])
        for h in range(N_HEADS):
            sl = slice(h * HP, (h + 1) * HP)
            q_ref[:, sl] = jnp.where(mask_lane, 1.0, _head(qraw[:, sl], qg[...], c, s)).astype(BF)
            k_ref[:, sl] = jnp.where(mask_lane, key_mask, _head(kraw[:, sl] + kr, kg[...], c, s)).astype(BF)
        v_ref[...] = _nt(ckvn, wv[...]).astype(BF)

    rs = lambda n: _row_spec(n, TRF)
    return pl.pallas_call(
        body, grid=(R // TRF,), name="stage_a_fwd",
        in_specs=[rs(D), rs(HP), rs(HP), _const_spec((1, D)), _const_spec((PC, D)),
                  _const_spec((1, Q_LORA)), _const_spec((N_HEADS * HP, Q_LORA)), _const_spec((1, KV_LORA)),
                  _const_spec((N_HEADS * HP, KV_LORA)), _const_spec((D_ATTN, KV_LORA)), _const_spec((1, HP)),
                  _const_spec((1, HP))],
        out_specs=[rs(O_XR), rs(D_RNN), rs(D_RNN), rs(N_HEADS * HP), rs(N_HEADS * HP), rs(D_ATTN)],
        out_shape=[jax.ShapeDtypeStruct((R, O_XR), F32), jax.ShapeDtypeStruct((R, D_RNN), F32),
                   jax.ShapeDtypeStruct((R, D_RNN), F32), jax.ShapeDtypeStruct((R, N_HEADS * HP), BF),
                   jax.ShapeDtypeStruct((R, N_HEADS * HP), BF), jax.ShapeDtypeStruct((R, D_ATTN), BF)],
        compiler_params=_params("arbitrary"),
    )(hp, cs, sn, cw["ln1_g"], cw["win"], cw["qa_g"], cw["wq"], cw["kva_g"], cw["wk"], cw["wv"], cw["q_g"], cw["k_g"])


def _stage_a_bwd(dq, dk, dv, dxr, dxg, dh1, hp, pa, cs, sn, cw):
    def body(dq_ref, dk_ref, dv_ref, dxr_ref, dxg_ref, dh1_ref, hp_ref, pa_ref, cs_ref, sn_ref,
             ln1, win, qag, wq, kvag, wk, wv, qg, kg,
             dhp_ref, dp_ref, dqraw_ref, dkraw_ref, hn_ref, cqn_ref, ckvn_ref,
             dln1_ref, dqag_ref, dkvag_ref, dqg_ref, dkg_ref):
        @pl.when(pl.program_id(0) == 0)
        def _():
            for r in (dln1_ref, dqag_ref, dkvag_ref, dqg_ref, dkg_ref):
                r[...] = jnp.zeros_like(r)

        hn, vjp_ln1 = jax.vjp(lambda h, g: _rms(h, g, D), hp_ref[...], ln1[...])
        hn_ref[...] = hn.astype(BF)
        pa_v = pa_ref[...]
        cqn, vjp_qa = jax.vjp(lambda x, g: _rms(x, g, Q_LORA), pa_v[:, :O_CKV], qag[...])
        ckvn, vjp_kva = jax.vjp(lambda x, g: _rms(x, g, KV_LORA), pa_v[:, O_CKV:O_KR], kvag[...])
        kr = pa_v[:, O_KR:O_XR]
        cqnb, ckvnb = cqn.astype(BF), ckvn.astype(BF)
        cqn_ref[...] = cqnb
        ckvn_ref[...] = ckvnb
        c, s = cs_ref[...], sn_ref[...]
        lane = lax.broadcasted_iota(jnp.int32, (1, HP), 1)
        rope_lanes = ((lane >= QK_NOPE) & (lane < QK_HEAD)).astype(F32)
        dkr = jnp.zeros((TR, HP), F32)
        dqg = jnp.zeros((1, HP), F32)
        dkg = jnp.zeros((1, HP), F32)
        qraw = _nt(cqnb, wq[...])
        kraw = _nt(ckvnb, wk[...])
        for h in range(N_HEADS):
            sl = slice(h * HP, (h + 1) * HP)
            dqraw, dg = _head_bwd(qraw[:, sl], qg[...], c, s, dq_ref[:, sl])
            dqg = dqg + dg
            dqraw_ref[:, sl] = dqraw.astype(BF)
            dkraw, dg = _head_bwd(kraw[:, sl] + kr, kg[...], c, s, dk_ref[:, sl])
            dkg = dkg + dg
            dkraw_ref[:, sl] = dkraw.astype(BF)
            dkr = dkr + dkraw * rope_lanes
        dcq, dqag = vjp_qa(_nn(dqraw_ref[...], wq[...]))
        dckv, dkvag = vjp_kva(_nn(dkraw_ref[...], wk[...]) + _nn(dv_ref[...].astype(BF), wv[...]))
        dpb = jnp.concatenate([dcq, dckv, dkr, dxr_ref[...], dxg_ref[...]], axis=1).astype(BF)
        dp_ref[...] = dpb
        dh, dln1 = vjp_ln1(_nn(dpb, win[...]))
        dhp_ref[...] = dh + dh1_ref[...]
        dln1_ref[...] += dln1
        dqag_ref[...] += dqag
        dkvag_ref[...] += dkvag
        dqg_ref[...] += dqg
        dkg_ref[...] += dkg

    acc = lambda n: pl.BlockSpec((1, n), lambda i: (0, 0))
    return pl.pallas_call(
        body, grid=(R // TR,), name="stage_a_bwd",
        in_specs=[_row_spec(N_HEADS * HP), _row_spec(N_HEADS * HP), _row_spec(D_ATTN), _row_spec(D_RNN),
                  _row_spec(D_RNN), _row_spec(D), _row_spec(D), _row_spec(O_XR), _row_spec(HP), _row_spec(HP),
                  _const_spec((1, D)), _const_spec((PC, D)), _const_spec((1, Q_LORA)),
                  _const_spec((N_HEADS * HP, Q_LORA)), _const_spec((1, KV_LORA)),
                  _const_spec((N_HEADS * HP, KV_LORA)), _const_spec((D_ATTN, KV_LORA)), _const_spec((1, HP)),
                  _const_spec((1, HP))],
        out_specs=[_row_spec(D), _row_spec(PC), _row_spec(N_HEADS * HP), _row_spec(N_HEADS * HP), _row_spec(D),
                   _row_spec(Q_LORA), _row_spec(KV_LORA), acc(D), acc(Q_LORA), acc(KV_LORA), acc(HP), acc(HP)],
        out_shape=[jax.ShapeDtypeStruct((R, D), F32), jax.ShapeDtypeStruct((R, PC), BF),
                   jax.ShapeDtypeStruct((R, N_HEADS * HP), BF), jax.ShapeDtypeStruct((R, N_HEADS * HP), BF),
                   jax.ShapeDtypeStruct((R, D), BF), jax.ShapeDtypeStruct((R, Q_LORA), BF),
                   jax.ShapeDtypeStruct((R, KV_LORA), BF), jax.ShapeDtypeStruct((1, D), F32),
                   jax.ShapeDtypeStruct((1, Q_LORA), F32), jax.ShapeDtypeStruct((1, KV_LORA), F32),
                   jax.ShapeDtypeStruct((1, HP), F32), jax.ShapeDtypeStruct((1, HP), F32)],
        compiler_params=_params("arbitrary"),
    )(dq, dk, dv, dxr, dxg, dh1, hp, pa, cs, sn, cw["ln1_g"], cw["win"], cw["qa_g"], cw["wq"], cw["kva_g"],
      cw["wk"], cw["wv"], cw["q_g"], cw["k_g"])


def _head_mask(half, dtype):
    lane = lax.broadcasted_iota(jnp.int32, (1, 2 * V_HEAD), 1)
    return ((lane >= V_HEAD) == (half == 1)).astype(dtype)


def _attn_specs(tq):
    n_q = TP // tq
    return (NB, N_HEADS // 2, n_q), dict(
        q=pl.BlockSpec((tq, 2 * HP), lambda b, j, i: (b * n_q + i, j)),
        k=pl.BlockSpec((TP, 2 * HP), lambda b, j, i: (b, j)),
        v=pl.BlockSpec((TP, 2 * V_HEAD), lambda b, j, i: (b, j)),
        o=pl.BlockSpec((tq, 2 * V_HEAD), lambda b, j, i: (b * n_q + i, j)),
        lse=pl.BlockSpec((None, tq, 2), lambda b, j, i: (j, b * n_q + i, 0)))


TQF = 1088


def _attn_fwd(q, k, v):
    def body(q_ref, k_ref, v_ref, o_ref, lse_ref):
        v2 = v_ref[...]
        o = jnp.zeros((TQF, 2 * V_HEAD), F32)
        lse = []
        for hh in range(2):
            sl = slice(hh * HP, (hh + 1) * HP)
            raw = _nt(q_ref[:, sl], k_ref[:, sl])
            m = jnp.max(raw, axis=-1, keepdims=True)
            e = jnp.exp2((raw - m) * SCALE_LOG2)
            l = jnp.sum(e, axis=-1, keepdims=True)
            o = o + _nn(e.astype(BF), v2 * _head_mask(hh, BF)) * (1.0 / l)
            lse.append(m * SCALE_LOG2 + jnp.log(l) * LOG2_E)
        o_ref[...] = o
        lane = lax.broadcasted_iota(jnp.int32, (TQF, 2), 1)
        lse_ref[...] = jnp.where(lane == 0, lse[0], lse[1])

    grid, sp = _attn_specs(TQF)
    return pl.pallas_call(
        body, grid=grid, name="attn_fwd", in_specs=[sp["q"], sp["k"], sp["v"]], out_specs=[sp["o"], sp["lse"]],
        out_shape=[jax.ShapeDtypeStruct((R, D_ATTN), F32), jax.ShapeDtypeStruct((N_HEADS // 2, R, 2), F32)],
        compiler_params=_params("arbitrary", "arbitrary", "arbitrary"),
    )(q, k, v)


def _attn_bwd(q, k, v, o, lse, do):
    def body(q_ref, k_ref, v_ref, o_ref, lse_ref, do_ref, dq_ref, dk_ref, dv_ref):
        @pl.when(pl.program_id(2) == 0)
        def _():
            dk_ref[...] = jnp.zeros_like(dk_ref)
            dv_ref[...] = jnp.zeros_like(dv_ref)

        do = do_ref[...]
        dob = do.astype(BF)
        do_o = do * o_ref[...]
        v2 = v_ref[...]
        dv_sum = jnp.zeros((TP, 2 * V_HEAD), F32)
        for hh in range(2):
            sl = slice(hh * HP, (hh + 1) * HP)
            qb, kb = q_ref[:, sl], k_ref[:, sl]
            p = jnp.exp2(_nt(qb, kb) * SCALE_LOG2 - lse_ref[:, hh:hh + 1])
            dp = _nt(dob, v2 * _head_mask(hh, BF))
            delta = jnp.sum(do_o * _head_mask(hh, F32), axis=-1, keepdims=True)
            dsb = (p * (dp - delta) * SCALE).astype(BF)
            dq_ref[:, sl] = _nn(dsb, kb)
            dk_ref[:, sl] += _tn(dsb, qb)
            dv_sum = dv_sum + _tn(p.astype(BF), dob) * _head_mask(hh, F32)
        dv_ref[...] += dv_sum

    grid, sp = _attn_specs(TQ)
    return pl.pallas_call(
        body, grid=grid, name="attn_bwd", in_specs=[sp["q"], sp["k"], sp["v"], sp["o"], sp["lse"], sp["o"]],
        out_specs=[sp["q"], sp["k"], sp["v"]],
        out_shape=[jax.ShapeDtypeStruct((R, N_HEADS * HP), F32), jax.ShapeDtypeStruct((R, N_HEADS * HP), F32),
                   jax.ShapeDtypeStruct((R, D_ATTN), F32)],
        compiler_params=_params("arbitrary", "arbitrary", "arbitrary"),
    )(q, k, v, o, lse, do)


SEG = TP // 8


def _scan_pair(af_ref, bf_ref, hf_ref, ab_ref, bb_ref, hb_ref, pf_ref, pb_ref):
    unroll = 8

    def step(i, carry):
        hf, pf, hb, pb = carry
        for u in range(unroll):
            j = i * unroll + u
            rows_f, rows_b = pl.ds(j, 8, stride=SEG), pl.ds(SEG - 1 - j, 8, stride=SEG)
            a = af_ref[rows_f, :]
            hf, pf = a * hf + bf_ref[rows_f, :], a * pf
            hf_ref[rows_f, :] = hf
            pf_ref[rows_f, :] = pf
            a = ab_ref[rows_b, :]
            hb, pb = a * hb + bb_ref[rows_b, :], a * pb
            hb_ref[rows_b, :] = hb
            pb_ref[rows_b, :] = pb
        return hf, pf, hb, pb

    zero, one = jnp.zeros((8, CG), F32), jnp.ones((8, CG), F32)
    hf, pf, hb, pb = lax.fori_loop(0, SEG // unroll, step, (zero, one, zero, one))
    seg = lax.broadcasted_iota(jnp.int32, (8, CG), 0)
    cf, cb = zero, zero
    for s in range(1, 8):
        cf = jnp.where(seg == s, pltpu.roll(hf + pf * cf, 1, 0), cf)
        cb = jnp.where(seg == 7 - s, pltpu.roll(hb + pb * cb, 7, 0), cb)
    for s in range(8):
        rows = slice(s * SEG, (s + 1) * SEG)
        hf_ref[rows, :] = hf_ref[rows, :] + pf_ref[rows, :] * cf[s:s + 1, :]
        hb_ref[rows, :] = hb_ref[rows, :] + pb_ref[rows, :] * cb[s:s + 1, :]


def _shifts(x):
    t = lax.broadcasted_iota(jnp.int32, x.shape, 0)
    xm2 = jnp.where(t >= 2, pltpu.roll(x, 2, 0), 0.0)
    xm1 = jnp.where(t >= 1, pltpu.roll(x, 1, 0), 0.0)
    xp1 = jnp.where(t < TP - 1, pltpu.roll(x, TP - 1, 0), 0.0)
    return xm2, xm1, xp1


def _softplus(z):
    e = jnp.exp(-jnp.abs(z))
    small = e * (1.0 - e * (0.5 - e * (1.0 / 3.0)))
    return jnp.maximum(z, 0.0) + jnp.where(e < 0.01, small, jnp.log(1.0 + e))


def _sigmoid(x):
    return 0.5 * jnp.tanh(0.5 * x) + 0.5


def _one_minus_sq(log_a, a):
    x = 2.0 * log_a
    series = -x * (1.0 + x * 0.5 * (1.0 + x * (1.0 / 3.0) * (1.0 + x * 0.25)))
    return jnp.where(x > -0.05, series, 1.0 - a * a)


def _gates(row0, xc, pa_f, pi_f, pa_b, pi_b, lam_f, lam_b):
    t = row0 + lax.broadcasted_iota(jnp.int32, xc.shape, 0)
    valid = t < T
    out = []
    for pa, pi_, lam in ((pa_f, pi_f, lam_f), (pa_b, pi_b, lam_b)):
        r = _sigmoid(pa)
        gate_i = _sigmoid(pi_)
        log_a = -LRU_C * r * _softplus(-lam)
        a = jnp.exp(log_a)
        mult = jnp.sqrt(jnp.maximum(_one_minus_sq(log_a, a), 0.0))
        out += [a, jnp.where(valid, mult * (gate_i * xc), 0.0)]
    return tuple(out)


def _gates_bwd(row0, xc, pres, lams, cots):
    t = row0 + lax.broadcasted_iota(jnp.int32, xc.shape, 0)
    valid = t < T
    dxc = jnp.zeros_like(xc)
    dpres, dlams = [], []
    for d in range(2):
        pa, pi_, lam = pres[2 * d], pres[2 * d + 1], lams[d]
        da, db = cots[2 * d], jnp.where(valid, cots[2 * d + 1], 0.0)
        r = _sigmoid(pa)
        gate_i = _sigmoid(pi_)
        sp = _softplus(-lam)
        log_a = -LRU_C * r * sp
        a = jnp.exp(log_a)
        m2 = jnp.maximum(_one_minus_sq(log_a, a), 0.0)
        mult = jnp.sqrt(m2)
        dxc = dxc + db * (mult * gate_i)
        d_gate = db * (mult * xc)
        d_m2 = jnp.where(m2 > 0.0, db * (gate_i * xc) * (0.5 * lax.rsqrt(m2)), 0.0)
        d_log_a = da * a - 2.0 * d_m2 * (a * a)
        dpres += [d_log_a * (-LRU_C * sp) * (r * (1.0 - r)), d_gate * (gate_i * (1.0 - gate_i))]
        d_sp = jnp.sum(d_log_a * (-LRU_C * r), axis=0, keepdims=True)
        dlams.append(-d_sp * jax.nn.sigmoid(-lam))
    return dxc, dpres, dlams


def _rnn_specs():
    seq = pl.BlockSpec((TP, CG), lambda g, b: (b, g))
    return dict(
        seq=seq,
        cw=pl.BlockSpec((CONV_W, CG), lambda g, b: (0, g)),
        cb=pl.BlockSpec((1, CG), lambda g, b: (0, g)),
        w4=pl.BlockSpec((None, CG, 4 * CG), lambda g, b: (g, 0, 0)),
        b4=pl.BlockSpec((None, 1, 4 * CG), lambda g, b: (g, 0, 0)),
        lam=pl.BlockSpec((None, 1, 2 * CG), lambda g, b: (g, 0, 0)),
    )


def _conv(x, xm2, xm1, xp1, cw_ref, cb_ref):
    return cw_ref[0:1, :] * xm2 + cw_ref[1:2, :] * xm1 + cw_ref[2:3, :] * x + cw_ref[3:4, :] * xp1 + cb_ref[...]


TC = 128
N_TC = TP // TC


def _split4(pre):
    return pre[:, :CG], pre[:, CG:2 * CG], pre[:, 2 * CG:3 * CG], pre[:, 3 * CG:]


def _rnn_fwd(xr, xg, cw):
    def body(xr_ref, xg_ref, cw_ref, cb_ref, w4_ref, b4_ref, lam_ref, y_ref, hf_ref, hb_ref, af_ref, ab_ref, xc_ref,
             af, bf, ab, bb, pf, pb):
        x = xr_ref[...]
        xc_ref[...] = _conv(x, *_shifts(x), cw_ref, cb_ref)
        lam = lam_ref[...]

        def chunk(i, _):
            rows = pl.ds(pl.multiple_of(i * TC, TC), TC)
            xc = xc_ref[rows, :]
            pre = _nn(xc.astype(BF), w4_ref[...]) + b4_ref[...]
            a_f, b_f, a_b, b_b = _gates(i * TC, xc, *_split4(pre), lam[:, :CG], lam[:, CG:])
            af[rows, :] = a_f
            bf[rows, :] = b_f
            ab[rows, :] = a_b
            bb[rows, :] = b_b
            af_ref[rows, :] = a_f
            ab_ref[rows, :] = a_b
            return 0

        lax.fori_loop(0, N_TC, chunk, 0)
        _scan_pair(af, bf, hf_ref, ab, bb, hb_ref, pf, pb)
        y_ref[...] = (hf_ref[...] + hb_ref[...]) * jax.nn.gelu(xg_ref[...])

    sp = _rnn_specs()
    return pl.pallas_call(
        body, grid=(N_CG, NB), name="rnn_fwd",
        in_specs=[sp["seq"], sp["seq"], sp["cw"], sp["cb"], sp["w4"], sp["b4"], sp["lam"]],
        out_specs=[sp["seq"]] * 6, out_shape=[jax.ShapeDtypeStruct((R, D_RNN), F32)] * 6,
        scratch_shapes=[pltpu.VMEM((TP, CG), F32)] * 6,
        compiler_params=_params("arbitrary", "arbitrary"),
    )(xr, xg, cw["conv_w"], cw["conv_b"], cw["w4"], cw["b4"], cw["lam"])


def _rnn_bwd(dy, xr, xg, hf, hb, af, ab, xc, cw):
    def body(dy_ref, xr_ref, xg_ref, hf_ref, hb_ref, af_ref, ab_ref, xc_s, cw_ref, cb_ref, w4_ref, b4_ref, lam_ref,
             dxr_ref, dxg_ref, dcw_ref, dcb_ref, dw4_ref, db4_ref, dlam_ref,
             af_s, ab_s, dhs_s, lf_s, lb_s, daf_s, dab_s, dxc_s):
        @pl.when(pl.program_id(1) == 0)
        def _():
            for r in (dcw_ref, dcb_ref, dw4_ref, db4_ref, dlam_ref):
                r[...] = jnp.zeros_like(r)

        lam = lam_ref[...]

        def chunk1(i, _):
            rows = pl.ds(pl.multiple_of(i * TC, TC), TC)
            _, vjp_y = jax.vjp(lambda h, g: h * jax.nn.gelu(g), hf_ref[rows, :] + hb_ref[rows, :], xg_ref[rows, :])
            dhs, dxg = vjp_y(dy_ref[rows, :])
            dhs_s[rows, :] = dhs
            dxg_ref[rows, :] = dxg
            return 0

        lax.fori_loop(0, N_TC, chunk1, 0)
        t = lax.broadcasted_iota(jnp.int32, (TP, CG), 0)
        af_s[...] = pltpu.roll(af_ref[...], TP - 1, 0)
        ab_s[...] = pltpu.roll(ab_ref[...], 1, 0)
        _scan_pair(ab_s, dhs_s, lb_s, af_s, dhs_s, lf_s, dab_s, daf_s)
        daf_s[...] = lf_s[...] * jnp.where(t >= 1, pltpu.roll(hf_ref[...], 1, 0), 0.0)
        dab_s[...] = lb_s[...] * jnp.where(t < TP - 1, pltpu.roll(hb_ref[...], TP - 1, 0), 0.0)

        def chunk2(i, _):
            rows = pl.ds(pl.multiple_of(i * TC, TC), TC)
            xc = xc_s[rows, :]
            xcb = xc.astype(BF)
            pre = _nn(xcb, w4_ref[...]) + b4_ref[...]
            dxc, dpres, dlams = _gates_bwd(i * TC, xc, _split4(pre), (lam[:, :CG], lam[:, CG:]),
                                           (daf_s[rows, :], lf_s[rows, :], dab_s[rows, :], lb_s[rows, :]))
            dpre = jnp.concatenate(dpres, axis=1)
            dpreb = dpre.astype(BF)
            dxc_s[rows, :] = dxc + _nt(dpreb, w4_ref[...])
            dw4_ref[...] += _tn(xcb, dpreb)
            db4_ref[...] += jnp.sum(dpre, axis=0, keepdims=True)
            dlam_ref[...] += jnp.concatenate(dlams, axis=1)
            return 0

        lax.fori_loop(0, N_TC, chunk2, 0)
        dxc = dxc_s[...]
        x = xr_ref[...]
        taps = (jnp.where(t < TP - 2, pltpu.roll(dxc, TP - 2, 0), 0.0), jnp.where(t < TP - 1, pltpu.roll(dxc, TP - 1, 0), 0.0),
                dxc, jnp.where(t >= 1, pltpu.roll(dxc, 1, 0), 0.0))
        dcb_ref[...] += jnp.sum(dxc, axis=0, keepdims=True)
        dxr = jnp.zeros_like(dxc)
        for tap, shifted in enumerate(taps):
            dcw_ref[tap:tap + 1, :] += jnp.sum(x * shifted, axis=0, keepdims=True)
            dxr = dxr + cw_ref[tap:tap + 1, :] * shifted
        dxr_ref[...] = dxr

    sp = _rnn_specs()
    return pl.pallas_call(
        body, grid=(N_CG, NB), name="rnn_bwd",
        in_specs=[sp["seq"]] * 8 + [sp["cw"], sp["cb"], sp["w4"], sp["b4"], sp["lam"]],
        out_specs=[sp["seq"], sp["seq"], sp["cw"], sp["cb"], sp["w4"], sp["b4"], sp["lam"]],
        out_shape=[jax.ShapeDtypeStruct((R, D_RNN), F32), jax.ShapeDtypeStruct((R, D_RNN), F32),
                   jax.ShapeDtypeStruct((CONV_W, D_RNN), F32), jax.ShapeDtypeStruct((1, D_RNN), F32),
                   jax.ShapeDtypeStruct((N_CG, CG, 4 * CG), F32), jax.ShapeDtypeStruct((N_CG, 1, 4 * CG), F32),
                   jax.ShapeDtypeStruct((N_CG, 1, 2 * CG), F32)],
        scratch_shapes=[pltpu.VMEM((TP, CG), F32)] * 8,
        compiler_params=_params("arbitrary", "arbitrary"),
    )(dy, xr, xg, hf, hb, af, ab, xc, cw["conv_w"], cw["conv_b"], cw["w4"], cw["b4"], cw["lam"])


TD = 256
STAGE_D_VMEM = 58 * 1024 * 1024


def _stage_d(hp, o, y, tgt, cw):
    def body(hp_ref, o_ref, y_ref, tgt_ref, ga, gr, wout, ln2, wg, wu, wd,
             do_ref, dy_ref, dh1_ref, mix_ref, dh1b_ref, hn2_ref, dg_ref, du_ref, act_ref, dh2b_ref,
             loss_ref, dga_ref, dgr_ref, dln2_ref):
        i = pl.program_id(0)

        @pl.when(i == 0)
        def _():
            for r in (loss_ref, dga_ref, dgr_ref, dln2_ref):
                r[...] = jnp.zeros_like(r)

        mix_a, vjp_a = jax.vjp(lambda x, g: _rms(x, g, D_ATTN), o_ref[...], ga[...])
        mix_r, vjp_r = jax.vjp(lambda x, g: _rms(x, g, D_RNN), y_ref[...], gr[...])
        mab, mrb = mix_a.astype(BF), mix_r.astype(BF)
        mix_ref[:, :D_ATTN] = mab
        mix_ref[:, D_ATTN:] = mrb
        h1 = hp_ref[...] + _nn(mab, wout[:D_ATTN, :]) + _nn(mrb, wout[D_ATTN:, :])
        hn2, vjp_ln2 = jax.vjp(lambda x, g: _rms(x, g, D), h1, ln2[...])
        hn2b = hn2.astype(BF)
        hn2_ref[...] = hn2b
        act, vjp_act = jax.vjp(lambda g, u: jax.nn.silu(g) * u, _nt(hn2b, wg[...]), _nt(hn2b, wu[...]))
        actb = act.astype(BF)
        act_ref[...] = actb
        h2 = h1 + _nn(actb, wd[...])
        row = i * TD + lax.broadcasted_iota(jnp.int32, (TD, 1), 0)
        t = jnp.where(row >= TP, row - TP, row)
        err = jnp.where((t >= N_META) & (t < T), h2 - tgt_ref[...], 0.0)
        loss_ref[...] += jnp.sum(err * err) * (0.5 / D)
        dh2b = (err * (1.0 / D)).astype(BF)
        dh2b_ref[...] = dh2b
        dg, du = vjp_act(_nt(dh2b, wd[...]))
        dgb, dub = dg.astype(BF), du.astype(BF)
        dg_ref[...] = dgb
        du_ref[...] = dub
        dh1n, dln2 = vjp_ln2(_nn(dgb, wg[...]) + _nn(dub, wu[...]))
        dh1 = err * (1.0 / D) + dh1n
        dh1_ref[...] = dh1
        dh1b = dh1.astype(BF)
        dh1b_ref[...] = dh1b
        dmix = _nt(dh1b, wout[...])
        do, dga = vjp_a(dmix[:, :D_ATTN])
        dyr, dgr = vjp_r(dmix[:, D_ATTN:])
        do_ref[...] = do
        dy_ref[...] = dyr
        dga_ref[...] += dga
        dgr_ref[...] += dgr
        dln2_ref[...] += dln2

    rs = lambda n: _row_spec(n, TD)
    acc = lambda n: pl.BlockSpec((1, n), lambda i: (0, 0))
    return pl.pallas_call(
        body, grid=(R // TD,), name="stage_d",
        in_specs=[rs(D), rs(D_ATTN), rs(D_RNN), rs(D), _const_spec((1, D_ATTN)), _const_spec((1, D_RNN)),
                  _const_spec((D, D)), _const_spec((1, D)), _const_spec((D_FF, D)), _const_spec((D_FF, D)),
                  _const_spec((D_FF, D))],
        out_specs=[rs(D_ATTN), rs(D_RNN), rs(D), rs(D), rs(D), rs(D), rs(D_FF), rs(D_FF), rs(D_FF), rs(D),
                   acc(1), acc(D_ATTN), acc(D_RNN), acc(D)],
        out_shape=[jax.ShapeDtypeStruct((R, D_ATTN), F32), jax.ShapeDtypeStruct((R, D_RNN), F32),
                   jax.ShapeDtypeStruct((R, D), F32), jax.ShapeDtypeStruct((R, D), BF),
                   jax.ShapeDtypeStruct((R, D), BF), jax.ShapeDtypeStruct((R, D), BF),
                   jax.ShapeDtypeStruct((R, D_FF), BF), jax.ShapeDtypeStruct((R, D_FF), BF),
                   jax.ShapeDtypeStruct((R, D_FF), BF), jax.ShapeDtypeStruct((R, D), BF),
                   jax.ShapeDtypeStruct((1, 1), F32), jax.ShapeDtypeStruct((1, D_ATTN), F32),
                   jax.ShapeDtypeStruct((1, D_RNN), F32), jax.ShapeDtypeStruct((1, D), F32)],
        compiler_params=_params("arbitrary", vmem=STAGE_D_VMEM),
    )(hp, o, y, tgt, cw["ga"], cw["gr"], cw["wout"], cw["ln2_g"], cw["wg"], cw["wu"], cw["wd"])


TW = 2176


def _wgrad(a, b, name, tk=None):
    ka, nb = a.shape[1], b.shape[1]
    tk = ka if tk is None else tk

    def body(a_ref, b_ref, o_ref):
        @pl.when(pl.program_id(1) == 0)
        def _():
            o_ref[...] = jnp.zeros_like(o_ref)

        o_ref[...] += _tn(a_ref[...].astype(BF), b_ref[...].astype(BF))

    return pl.pallas_call(
        body, grid=(ka // tk, R // TW), name=name,
        in_specs=[pl.BlockSpec((TW, tk), lambda k, r: (r, k)), pl.BlockSpec((TW, nb), lambda k, r: (r, 0))],
        out_specs=pl.BlockSpec((tk, nb), lambda k, r: (k, 0)),
        out_shape=jax.ShapeDtypeStruct((ka, nb), F32),
        compiler_params=_params("arbitrary", "arbitrary"),
    )(a, b)


def _wgrad_heads(dq, dk, dv, cqn, ckvn):
    def body(dq_ref, dk_ref, dv_ref, cqn_ref, ckvn_ref, oq_ref, ok_ref, ov_ref):
        @pl.when(pl.program_id(0) == 0)
        def _():
            for r in (oq_ref, ok_ref, ov_ref):
                r[...] = jnp.zeros_like(r)

        ckvnb = ckvn_ref[...]
        oq_ref[...] += _tn(dq_ref[...], cqn_ref[...])
        ok_ref[...] += _tn(dk_ref[...], ckvnb)
        ov_ref[...] += _tn(dv_ref[...].astype(BF), ckvnb)

    rows = lambda a: pl.BlockSpec((TW, a.shape[1]), lambda r: (r, 0))
    full = lambda m, n: pl.BlockSpec((m, n), lambda r: (0, 0))
    shapes = [(dq.shape[1], cqn.shape[1]), (dk.shape[1], ckvn.shape[1]), (dv.shape[1], ckvn.shape[1])]
    return pl.pallas_call(
        body, grid=(R // TW,), name="wgrad_heads", in_specs=[rows(a) for a in (dq, dk, dv, cqn, ckvn)],
        out_specs=[full(*s) for s in shapes], out_shape=[jax.ShapeDtypeStruct(s, F32) for s in shapes],
        compiler_params=_params("arbitrary"),
    )(dq, dk, dv, cqn, ckvn)


def _rope_tables():
    half = QK_ROPE // 2
    freqs = 1.0 / (ROPE_THETA ** (jnp.arange(half, dtype=F32) / half))
    ang = jnp.arange(TP, dtype=F32)[:, None] * freqs[None, :]
    ones = jnp.ones((TP, QK_NOPE), F32)
    zeros = jnp.zeros((TP, QK_NOPE), F32)
    pad1 = jnp.ones((TP, HP - QK_HEAD), F32)
    pad0 = jnp.zeros((TP, HP - QK_HEAD), F32)
    cs = jnp.concatenate([ones, jnp.cos(ang), jnp.cos(ang), pad1], axis=1)
    sn = jnp.concatenate([zeros, jnp.sin(ang), jnp.sin(ang), pad0], axis=1)
    return jnp.tile(cs, (NB, 1)), jnp.tile(sn, (NB, 1))


def _pad_rows(a, lo, hi):
    return jnp.pad(a, ((0, 0), (lo, hi), (0, 0)))


def _pad_target(target):
    return _pad_rows(target, N_META, TP - T).reshape(R, D)


def _compute_weights(w):
    win_t = w["w_in_t"]
    kr = win_t[O_KR:O_KR + QK_ROPE]
    win = jnp.concatenate([win_t[:O_KR], jnp.zeros((QK_NOPE, D), F32), kr,
                           jnp.zeros((HP - QK_HEAD, D), F32), win_t[O_KR + QK_ROPE:]], axis=0)
    wq = _pad_rows(w["w_uq_t"].reshape(N_HEADS, QK_HEAD, Q_LORA), 0, HP - QK_HEAD)
    wkv = w["w_ukv_t"].reshape(N_HEADS, QK_NOPE + V_HEAD, KV_LORA)
    wk = _pad_rows(wkv[:, :QK_NOPE], 0, HP - QK_NOPE)
    wv = wkv[:, QK_NOPE:].reshape(D_ATTN, KV_LORA)
    gates = jnp.stack([w["lru_wa"][0], w["lru_wi"][0], w["lru_wa"][1], w["lru_wi"][1]])
    blk = gates.reshape(4, N_CG, 2, RNN_BW, RNN_BW)
    dense = jnp.einsum("tcaij,ab->tcaibj", blk, jnp.eye(2, dtype=F32)).reshape(4, N_CG, CG, CG)
    w4 = dense.transpose(1, 2, 0, 3).reshape(N_CG, CG, 4 * CG)
    bias = jnp.stack([w["lru_ba"][0], w["lru_bi"][0], w["lru_ba"][1], w["lru_bi"][1]])
    b4 = bias.reshape(4, N_CG, CG).transpose(1, 0, 2).reshape(N_CG, 1, 4 * CG)
    lam = w["lru_lambda"].reshape(2, N_CG, CG).transpose(1, 0, 2).reshape(N_CG, 1, 2 * CG)
    pad_g = lambda g: jnp.pad(g.reshape(1, QK_HEAD), ((0, 0), (0, HP - QK_HEAD)))
    return dict(
        ln1_g=w["ln1_g"].reshape(1, D), win=win.astype(BF), qa_g=w["q_a_norm_g"].reshape(1, Q_LORA),
        wq=wq.astype(BF).reshape(N_HEADS * HP, Q_LORA), kva_g=w["kv_a_norm_g"].reshape(1, KV_LORA),
        wk=wk.astype(BF).reshape(N_HEADS * HP, KV_LORA), wv=wv.astype(BF),
        q_g=pad_g(w["q_norm_g"]), k_g=pad_g(w["k_norm_g"]),
        conv_w=w["conv_w"].reshape(CONV_W, D_RNN), conv_b=w["conv_b"].reshape(1, D_RNN),
        w4=w4.astype(BF), b4=b4, lam=lam,
        ga=w["attn_out_g"].reshape(1, D_ATTN), gr=w["rnn_out_g"].reshape(1, D_RNN), ln2_g=w["ln2_g"].reshape(1, D),
    )


def _local_step(x, target, meta, w, late_forward, late_weights, early_grads, mid_grads):
    cw = _compute_weights(w)
    cs, sn = _rope_tables()
    hp = jnp.concatenate([jnp.broadcast_to(meta[None], (NB, N_META, D)), x,
                          jnp.zeros((NB, TP - T, D), F32)], axis=1).reshape(R, D)
    tgt = target if target.ndim == 2 else _pad_target(target)

    pa, xr, xg, q, k, v = _stage_a_fwd(hp, cs, sn, cw)
    o, lse = _attn_fwd(q, k, v)
    cw["conv_b"] = cw["conv_b"] + late_forward([o])
    y, hf, hb, af, ab, xc = _rnn_fwd(xr, xg, cw)
    late = late_weights([y])
    cw.update(wout=late["w_out"], wg=late["w_gate_t"], wu=late["w_up_t"], wd=late["w_down"])
    (do, dy, dh1, mixb, dh1b, hn2b, dgb, dub, actb, dh2b, loss, dga, dgr, dln2) = _stage_d(hp, o, y, tgt, cw)
    dwout = _wgrad(mixb, dh1b, "wgrad_out")
    dwg = _wgrad(dgb, hn2b, "wgrad_gate", tk=D_FF // 2)
    dwu = _wgrad(dub, hn2b, "wgrad_up", tk=D_FF // 2)
    dwd = _wgrad(actb, dh2b, "wgrad_down", tk=D_FF // 2)
    zero = early_grads(dict(w_out=dwout, w_gate=dwg, w_up=dwu, w_down=dwd))
    cw["conv_b"] = cw["conv_b"] + zero
    dxr, dxg, dcw, dcb, dw4, db4, dlam = _rnn_bwd(dy, xr, xg, hf, hb, af, ab, xc, cw)
    zero = mid_grads([dxr])
    dq, dk, dv = _attn_bwd(q, k, v, o, lse, do)
    (dhp, dpb, dqrawb, dkrawb, hn1b, cqnb, ckvnb, dln1, dqag, dkvag, dqg, dkg) = _stage_a_bwd(
        dq, dk, dv, dxr, dxg, dh1, hp, pa, cs, sn, dict(cw, qa_g=cw["qa_g"] + zero))

    dwin = _wgrad(dpb, hn1b, "wgrad_in", tk=PC // 2)
    dwq, dwk, dwv = _wgrad_heads(dqrawb, dkrawb, dv, cqnb, ckvnb)

    dwin_t = jnp.concatenate([dwin[:O_KR], dwin[O_KR + QK_NOPE:O_KR + QK_HEAD], dwin[O_XR:]], axis=0)
    dwq_t = dwq.reshape(N_HEADS, HP, Q_LORA)[:, :QK_HEAD].reshape(N_HEADS * QK_HEAD, Q_LORA)
    dwkv_t = jnp.concatenate([dwk.reshape(N_HEADS, HP, KV_LORA)[:, :QK_NOPE],
                              dwv.reshape(N_HEADS, V_HEAD, KV_LORA)], axis=1).reshape(2 * D_ATTN, KV_LORA)
    d4 = dw4.reshape(N_CG, 2, RNN_BW, 4, 2, RNN_BW)
    dgates = jnp.stack([d4[:, 0, :, :, 0, :], d4[:, 1, :, :, 1, :]], axis=1)
    dgates = dgates.transpose(3, 0, 1, 2, 4).reshape(4, N_HEADS, RNN_BW, RNN_BW)
    dbias = db4.reshape(N_CG, 4, CG).transpose(1, 0, 2).reshape(4, D_RNN)
    dhp3 = dhp.reshape(NB, TP, D)
    grads = dict(
        meta_tokens=jnp.sum(dhp3[:, :N_META], axis=0),
        ln1_g=dln1, w_in_t=dwin_t, q_a_norm_g=dqag, w_uq_t=dwq_t, kv_a_norm_g=dkvag, w_ukv_t=dwkv_t,
        q_norm_g=dqg[:, :QK_HEAD], k_norm_g=dkg[:, :QK_HEAD], conv_w=dcw[None], conv_b=dcb,
        lru_wa=jnp.stack([dgates[0], dgates[2]])[None], lru_ba=jnp.stack([dbias[0], dbias[2]])[None],
        lru_wi=jnp.stack([dgates[1], dgates[3]])[None], lru_bi=jnp.stack([dbias[1], dbias[3]])[None],
        lru_lambda=dlam.reshape(N_CG, 2, CG).transpose(1, 0, 2).reshape(1, 2, D_RNN),
        attn_out_g=dga, rnn_out_g=dgr, ln2_g=dln2,
    )
    return loss[0, 0], dhp3[:, N_META:T], grads, [dhp, dwin]


_ANY = pl.BlockSpec(memory_space=pl.ANY)


def _place():
    return lax.axis_index("x"), lax.axis_index("y"), lax.axis_index("c")


def _other_chips(x, y):
    return [(1 - x, y), (x, 1 - y), (1 - x, 1 - y)]


def _pair_exchange(big, whole, name):
    n_s, _, m, n = big.shape
    n_copies = n_s + len(whole)

    def body(*refs):
        big_ref, whole_refs = refs[0], refs[1:1 + len(whole)]
        rbig_ref, rwhole_refs = refs[1 + len(whole)], refs[2 + len(whole):2 + 2 * len(whole)]
        send_sems, recv_sems = refs[-2:]
        x, y, c = _place()
        sibling = (x, y, 1 - c)
        copies = [pltpu.make_async_remote_copy(
            src_ref=big_ref.at[s, 1 - c], dst_ref=rbig_ref.at[s], send_sem=send_sems.at[s], recv_sem=recv_sems.at[s],
            device_id=sibling, device_id_type=MESH) for s in range(n_s)]
        copies += [pltpu.make_async_remote_copy(
            src_ref=a, dst_ref=r, send_sem=send_sems.at[n_s + i], recv_sem=recv_sems.at[n_s + i],
            device_id=sibling, device_id_type=MESH) for i, (a, r) in enumerate(zip(whole_refs, rwhole_refs))]
        for cp in copies:
            cp.start()
        for cp in copies:
            cp.wait()

    return pl.pallas_call(
        body, name=name,
        out_shape=[jax.ShapeDtypeStruct((n_s, m, n), big.dtype)] + [jax.ShapeDtypeStruct(a.shape, a.dtype) for a in whole],
        in_specs=[_ANY] * (1 + len(whole)), out_specs=[_ANY] * (1 + len(whole)),
        scratch_shapes=[pltpu.SemaphoreType.DMA((n_copies,)), pltpu.SemaphoreType.DMA((n_copies,))],
    )(big, *whole)


_HBM = pl.BlockSpec(memory_space=pltpu.HBM)
_SEM = pl.BlockSpec(memory_space=pltpu.SEMAPHORE)
_EFFECT = pltpu.SideEffectType.DATAFLOW_SIDE_EFFECTING


def _split_copies(src_refs, land_refs, sems, plan, sending):
    n = len(sems) // 2
    return [pltpu.make_async_remote_copy(src_ref=s, dst_ref=d, send_sem=sems[k], recv_sem=sems[n + k], device_id=to,
                                         device_id_type=MESH)
            for k, (s, d, to) in enumerate(plan(src_refs, land_refs, sending))]


def _to_chips(src_at, land_at):
    def plan(src_refs, land_refs, sending):
        x, y, c = _place()
        return [(src_at(s, tx, ty, c), land_at(l, j, *((x, y) if sending else (tx, ty)), c), (tx, ty, c))
                for s, l in zip(src_refs, land_refs) for j, (tx, ty) in enumerate(_other_chips(x, y))]
    return plan


def _to_sibling(src_refs, land_refs, sending):
    x, y, c = _place()
    return [(s.at[k, 1 - c], l.at[k], (x, y, 1 - c)) for s, l in zip(src_refs, land_refs) for k in range(N_CHIPS)]


def _split_start(name, srcs, lands, plan, n, after=()):
    srcs, lands, after = list(srcs), list(lands), list(after)
    k, kb = len(srcs), len(srcs) + len(lands)

    def body(*refs):
        outs = refs[kb + len(after):]
        for cp in _split_copies(refs[:k], refs[k:kb], outs[:2 * n], plan, True):
            cp.start()
        outs[2 * n + kb][...] = jnp.zeros_like(outs[2 * n + kb])

    outs = pl.pallas_call(
        body, name=name,
        out_shape=(pltpu.SemaphoreType.DMA(()),) * (2 * n) + tuple(pltpu.HBM(a.shape, a.dtype) for a in srcs + lands)
        + (jax.ShapeDtypeStruct((8, LANES), F32),),
        in_specs=(_HBM,) * kb + (_ANY,) * len(after),
        out_specs=(_SEM,) * (2 * n) + (_HBM,) * kb + (pl.BlockSpec(memory_space=pltpu.VMEM),),
        input_output_aliases={i: 2 * n + i for i in range(kb)},
        compiler_params=pltpu.CompilerParams(has_side_effects=_EFFECT),
    )(*[pltpu.with_memory_space_constraint(a, pltpu.HBM) for a in srcs + lands], *after)
    return outs[:2 * n], list(outs[2 * n:2 * n + k]), list(outs[2 * n + k:2 * n + kb]), outs[2 * n + kb]


def _split_wait(name, sems, srcs, lands, after, plan):
    srcs, lands = list(srcs), list(lands)
    k, kb = len(srcs), len(srcs) + len(lands)

    def body(*refs):
        for cp in _split_copies(refs[:k], refs[k:kb], refs[kb:kb + len(sems)], plan, False):
            cp.wait_send()
            cp.wait_recv()

    outs = pl.pallas_call(
        body, name=name, out_shape=tuple(pltpu.HBM(a.shape, a.dtype) for a in srcs + lands),
        in_specs=(_HBM,) * kb + (_SEM,) * len(sems) + (_ANY,) * len(after), out_specs=(_HBM,) * kb,
        input_output_aliases={i: i for i in range(kb)}, compiler_params=pltpu.CompilerParams(has_side_effects=_EFFECT),
    )(*srcs, *lands, *sems, *after)
    return list(outs[:k]), list(outs[k:])


def _forward_landed(src_refs, land_refs, sending):
    x, y, c = _place()
    copies = []
    for ref in src_refs:
        m = ref.shape[0] // 8
        for tx, ty in _other_chips(x, y):
            rows = ref.at[pl.ds((4 * tx + 2 * ty + (c if sending else 1 - c)) * m, m), :]
            copies.append((rows, rows, (x, y, 1 - c)))
    return copies


def _place_own(pieces):
    k = len(pieces)

    def body(*refs):
        piece_refs, out_refs, stages = refs[:k], refs[k:2 * k], refs[2 * k:3 * k]
        load_sems, store_sems = refs[3 * k:]
        x, y, _ = _place()
        loads = [pltpu.make_async_copy(piece_refs[a], stages[a], load_sems.at[a]) for a in range(k)]
        stores = [pltpu.make_async_copy(
            stages[a], out_refs[a].at[pl.ds((2 * x + y) * pieces[a].shape[0], pieces[a].shape[0]), :], store_sems.at[a])
            for a in range(k)]
        for cp in loads:
            cp.start()
        for ld, st in zip(loads, stores):
            ld.wait()
            st.start()
        for cp in stores:
            cp.wait()

    return pl.pallas_call(
        body, name="gather_late_place_own",
        out_shape=[jax.ShapeDtypeStruct((N_CHIPS * p.shape[0], p.shape[1]), p.dtype) for p in pieces],
        in_specs=[_ANY] * k, out_specs=[_ANY] * k,
        scratch_shapes=[pltpu.VMEM(p.shape, p.dtype) for p in pieces]
        + [pltpu.SemaphoreType.DMA((k,)), pltpu.SemaphoreType.DMA((k,))],
    )(*pieces)


def _gather_finish(lands, pieces, name):
    k = len(lands)

    def body(*refs):
        land_refs, piece_refs, out_refs, stages = refs[:k], refs[k:2 * k], refs[2 * k:3 * k], refs[3 * k:4 * k]
        send_sems, recv_sems, load_sems, store_sems = refs[4 * k:]
        x, y, c = _place()
        sibling = (x, y, 1 - c)
        remote, loads, stores, arrivals = [], [], [], []
        for a in range(k):
            m = lands[a].shape[0] // 8

            def rows(px, py, pc, ref, m=m):
                return ref.at[pl.ds((4 * px + 2 * py + pc) * m, m), :]

            for j, (tx, ty) in enumerate(_other_chips(x, y)):
                sems = dict(send_sem=send_sems.at[3 * a + j], recv_sem=recv_sems.at[3 * a + j], device_id=sibling,
                            device_id_type=MESH)
                remote.append(pltpu.make_async_remote_copy(
                    src_ref=rows(tx, ty, c, land_refs[a]), dst_ref=rows(tx, ty, c, out_refs[a]), **sems))
                arrivals.append(pltpu.make_async_remote_copy(
                    src_ref=rows(tx, ty, 1 - c, out_refs[a]), dst_ref=rows(tx, ty, 1 - c, out_refs[a]), **sems))
            for h in range(2):
                loads.append(pltpu.make_async_copy(piece_refs[a].at[pl.ds(h * m, m), :], stages[a].at[h],
                                                   load_sems.at[2 * a + h]))
                stores.append(pltpu.make_async_copy(stages[a].at[h], rows(x, y, h, out_refs[a]), store_sems.at[2 * a + h]))
        for cp in remote + loads:
            cp.start()
        for ld, st in zip(loads, stores):
            ld.wait()
            st.start()
        for cp, arrival in zip(remote, arrivals):
            cp.wait_send()
            arrival.wait_recv()
        for cp in stores:
            cp.wait()

    return pl.pallas_call(
        body, name=name, out_shape=[jax.ShapeDtypeStruct(a.shape, a.dtype) for a in lands],
        in_specs=[_ANY] * (2 * k), out_specs=[_ANY] * k, input_output_aliases={i: i for i in range(k)},
        scratch_shapes=[pltpu.VMEM((2, a.shape[0] // 8, a.shape[1]), a.dtype) for a in lands]
        + [pltpu.SemaphoreType.DMA((3 * k,)), pltpu.SemaphoreType.DMA((3 * k,)), pltpu.SemaphoreType.DMA((2 * k,)),
           pltpu.SemaphoreType.DMA((2 * k,))],
    )(*lands, *pieces)


def _pair_fill(bufs, name):
    k = len(bufs)

    def body(*refs):
        send_sems, recv_sems = refs[-2:]
        x, y, c = _place()
        copies = [pltpu.make_async_remote_copy(
            src_ref=refs[i].at[c], dst_ref=refs[k + i].at[c], send_sem=send_sems.at[i], recv_sem=recv_sems.at[i],
            device_id=(x, y, 1 - c), device_id_type=MESH) for i in range(k)]
        for cp in copies:
            cp.start()
        for i, cp in enumerate(copies):
            cp.wait_send()
            pltpu.make_async_remote_copy(
                src_ref=refs[i].at[1 - c], dst_ref=refs[k + i].at[1 - c], send_sem=send_sems.at[i],
                recv_sem=recv_sems.at[i], device_id=(x, y, 1 - c), device_id_type=MESH).wait_recv()

    return pl.pallas_call(
        body, name=name, out_shape=[jax.ShapeDtypeStruct(a.shape, a.dtype) for a in bufs], in_specs=[_ANY] * k,
        out_specs=[_ANY] * k, input_output_aliases={i: i for i in range(k)},
        scratch_shapes=[pltpu.SemaphoreType.DMA((k,)), pltpu.SemaphoreType.DMA((k,))],
    )(*bufs)


def _row_tile(rows, cap=512):
    for t in range(cap - cap % 8, 7, -8):
        if rows % t == 0:
            return t
    return rows


def _elementwise(fn, n_out, name, *arrs, out_dtype=F32):
    rows, cols = arrs[0].shape
    tr = _row_tile(rows)
    n_in = len(arrs)

    def body(*refs):
        outs = fn(*[r[...].astype(F32) for r in refs[:n_in]])
        for r, o in zip(refs[n_in:], outs):
            r[...] = o.astype(out_dtype)

    spec = pl.BlockSpec((tr, cols), lambda i: (i, 0))
    return pl.pallas_call(
        body, grid=(rows // tr,), name=name, in_specs=[spec] * n_in, out_specs=[spec] * n_out,
        out_shape=[jax.ShapeDtypeStruct((rows, cols), out_dtype)] * n_out, compiler_params=_params("arbitrary"),
    )(*arrs)


def _pair_sums(gpacks, rbigs, ci, name):
    k = len(gpacks)

    def body(c_ref, *refs):
        for g_ref, r_ref, o_ref in zip(refs[:k], refs[k:2 * k], refs[2 * k:]):
            o_ref[...] = (g_ref[...] + r_ref[...]).astype(BF)

    half = lambda a: pl.BlockSpec((None,) + a.shape[1:], lambda s, c: (s, 0, 0))
    return pl.pallas_call(
        body, name=name, out_shape=[jax.ShapeDtypeStruct(r.shape, BF) for r in rbigs],
        grid_spec=pltpu.PrefetchScalarGridSpec(
            num_scalar_prefetch=1, grid=(N_CHIPS,),
            in_specs=[pl.BlockSpec((None, None) + g.shape[2:], lambda s, c: (s, c[0], 0, 0)) for g in gpacks]
            + [half(r) for r in rbigs],
            out_specs=[half(r) for r in rbigs]),
        compiler_params=_params("arbitrary"),
    )(ci.reshape(1), *gpacks, *rbigs)


def _chip_sums(sums, landed, chip, ci, name):
    k = len(sums)

    def body(p_ref, *refs):
        for own_ref, land_ref, o_ref in zip(refs[:k], refs[k:2 * k], refs[2 * k:]):
            f = lambda v: v.astype(F32)
            o_ref[...] = _add4(f(own_ref[...]), f(land_ref[0]), f(land_ref[1]), f(land_ref[2]))[0]

    return pl.pallas_call(
        body, name=name, out_shape=[jax.ShapeDtypeStruct((2,) + s.shape[1:], F32) for s in sums],
        grid_spec=pltpu.PrefetchScalarGridSpec(
            num_scalar_prefetch=1, grid=(1,),
            in_specs=[pl.BlockSpec((None,) + s.shape[1:], lambda i, p: (p[0], 0, 0)) for s in sums]
            + [pl.BlockSpec(l.shape, lambda i, p: (0, 0, 0)) for l in landed],
            out_specs=[pl.BlockSpec((None,) + s.shape[1:], lambda i, p: (p[1], 0, 0)) for s in sums]),
        compiler_params=_params("arbitrary"),
    )(jnp.stack([chip, ci]), *sums, *landed)


def _add2(a, b):
    return (a + b,)


def _add4(own, r0, r1, r2):
    return ((own + r2) + (r0 + r1),)


def _adamw_small(ws, gs, ms, vs):
    k = len(ws)

    def body(*refs):
        for i in range(k):
            outs = _adamw_math(*[refs[j * k + i][...] for j in range(4)])
            for j, o in enumerate(outs):
                refs[(4 + j) * k + i][...] = o

    return pl.pallas_call(
        body, name="adamw_small", out_shape=[jax.ShapeDtypeStruct(w.shape, F32) for w in ws] * 3,
    )(*ws, *gs, *ms, *vs)


def _adamw_math(w, g, m, v):
    m = ADAM_B1 * m + (1.0 - ADAM_B1) * g
    v = ADAM_B2 * v + (1.0 - ADAM_B2) * (g * g)
    m_hat = m / (1.0 - ADAM_B1 ** ADAM_STEP)
    v_hat = v / (1.0 - ADAM_B2 ** ADAM_STEP)
    delta = -ADAM_LR * (m_hat / (jnp.sqrt(v_hat) + ADAM_EPS) + ADAM_WD * w)
    return delta, m, v


WEIGHTS = ["meta_tokens", "ln1_g", "w_in", "q_a_norm_g", "w_uq", "kv_a_norm_g", "w_ukv", "q_norm_g", "k_norm_g",
           "conv_w", "conv_b", "lru_wa", "lru_ba", "lru_wi", "lru_bi", "lru_lambda", "attn_out_g", "rnn_out_g",
           "w_out", "ln2_g", "w_gate", "w_up", "w_down"]
BIG = ["w_in", "w_uq", "w_ukv", "w_out", "w_gate", "w_up", "w_down"]
BIG_T = {"w_in": True, "w_uq": True, "w_ukv": True, "w_out": False, "w_gate": True, "w_up": True, "w_down": False}
BIG_ROWS = {"w_in": 424, "w_uq": 72, "w_ukv": 64, "w_out": 256, "w_gate": 704, "w_up": 704, "w_down": 704}
EARLY = ["w_in", "w_uq", "w_ukv"]
LATE = ["w_out", "w_gate", "w_up", "w_down"]
EARLY_ROWS = 576
SMALL_SHARDED = ["meta_tokens", "conv_w", "lru_ba", "lru_bi", "lru_lambda"]
SMALL = [n for n in WEIGHTS if n not in BIG]
SMALL_PACK_ROWS = 160


def _offsets(names):
    off, o = {}, 0
    for n in names:
        off[n] = o
        o += BIG_ROWS[n]
    return off


def _shard_pack(names, src, rows):
    parts = [_to_pack_piece(n, src[n]) for n in names]
    used = sum(BIG_ROWS[n] for n in names)
    if rows > used:
        parts.append(jnp.zeros((rows - used, D), F32))
    return jnp.concatenate(parts, axis=0)


def _grad_pack(names, g, rows):
    parts = [g[n].reshape(N_CHIPS, BIG_ROWS[n], D) for n in names]
    used = sum(BIG_ROWS[n] for n in names)
    if rows > used:
        parts.append(jnp.zeros((N_CHIPS, rows - used, D), F32))
    return jnp.concatenate(parts, axis=1).reshape(N_CHIPS, 2, rows // 2, D)


def _to_pack_piece(name, shard):
    a = shard[0].T if BIG_T[name] else shard[0]
    return a.reshape(BIG_ROWS[name], D)


def _flat_pack(arrs, rows):
    flat = jnp.concatenate([a.reshape(-1) for a in arrs])
    return jnp.pad(flat, (0, rows * D - flat.shape[0])).reshape(rows, D)


def _flat_unpack(pack, shapes):
    flat, out, o = pack.reshape(-1), [], 0
    for s in shapes:
        n = math.prod(s)
        out.append(flat[o:o + n].reshape(s))
        o += n
    return out


def kernel(x, meta_tokens, ln1_g, w_in, q_a_norm_g, w_uq, kv_a_norm_g, w_ukv, q_norm_g, k_norm_g, conv_w, conv_b, lru_wa, lru_ba, lru_wi, lru_bi, lru_lambda, attn_out_g, rnn_out_g, w_out, ln2_g, w_gate, w_up, w_down, loss_target, m_meta_tokens, m_ln1_g, m_w_in, m_q_a_norm_g, m_w_uq, m_kv_a_norm_g, m_w_ukv, m_q_norm_g, m_k_norm_g, m_conv_w, m_conv_b, m_lru_wa, m_lru_ba, m_lru_wi, m_lru_bi, m_lru_lambda, m_attn_out_g, m_rnn_out_g, m_w_out, m_ln2_g, m_w_gate, m_w_up, m_w_down, v_meta_tokens, v_ln1_g, v_w_in, v_q_a_norm_g, v_w_uq, v_kv_a_norm_g, v_w_ukv, v_q_norm_g, v_k_norm_g, v_conv_w, v_conv_b, v_lru_wa, v_lru_ba, v_lru_wi, v_lru_bi, v_lru_lambda, v_attn_out_g, v_rnn_out_g, v_w_out, v_ln2_g, v_w_gate, v_w_up, v_w_down):
    wts = dict(zip(WEIGHTS, (meta_tokens, ln1_g, w_in, q_a_norm_g, w_uq, kv_a_norm_g, w_ukv, q_norm_g, k_norm_g, conv_w, conv_b, lru_wa, lru_ba, lru_wi, lru_bi, lru_lambda, attn_out_g, rnn_out_g, w_out, ln2_g, w_gate, w_up, w_down)))
    mom = dict(zip(WEIGHTS, (m_meta_tokens, m_ln1_g, m_w_in, m_q_a_norm_g, m_w_uq, m_kv_a_norm_g, m_w_ukv, m_q_norm_g, m_k_norm_g, m_conv_w, m_conv_b, m_lru_wa, m_lru_ba, m_lru_wi, m_lru_bi, m_lru_lambda, m_attn_out_g, m_rnn_out_g, m_w_out, m_ln2_g, m_w_gate, m_w_up, m_w_down)))
    var = dict(zip(WEIGHTS, (v_meta_tokens, v_ln1_g, v_w_in, v_q_a_norm_g, v_w_uq, v_kv_a_norm_g, v_w_ukv, v_q_norm_g, v_k_norm_g, v_conv_w, v_conv_b, v_lru_wa, v_lru_ba, v_lru_wi, v_lru_bi, v_lru_lambda, v_attn_out_g, v_rnn_out_g, v_w_out, v_ln2_g, v_w_gate, v_w_up, v_w_down)))
    xi, yi, ci = _place()
    chip = 2 * xi + yi
    off_e = _offsets(EARLY)
    half_e = EARLY_ROWS // 2
    gather_plan = _to_chips(lambda ref, tx, ty, c: ref.at[pl.ds(c * (ref.shape[0] // 2), ref.shape[0] // 2), :],
                            lambda ref, j, px, py, c: ref.at[pl.ds((4 * px + 2 * py + c) * (ref.shape[0] // 8),
                                                                   ref.shape[0] // 8), :])
    scatter_plan = _to_chips(lambda ref, tx, ty, c: ref.at[2 * tx + ty], lambda ref, j, px, py, c: ref.at[j])
    everywhere = _to_chips(lambda ref, tx, ty, c: ref, lambda ref, j, px, py, c: ref.at[j])
    n_late = len(LATE)

    pack_e = _shard_pack(EARLY, wts, EARLY_ROWS).astype(BF)
    spack = jnp.concatenate([meta_tokens[:, :LANES], meta_tokens[:, LANES:], conv_w[0], lru_ba[0], lru_bi[0],
                             lru_lambda[0], jnp.zeros((6, LANES), F32)], axis=0)
    sems_g, src_g, land_g, _ = _split_start(
        "gather_early_start", [pack_e, spack], [lax.empty((N_CHIPS * EARLY_ROWS, D), BF), lax.empty((N_CHIPS * 48, LANES), F32)],
        gather_plan, 6)
    tgt_padded = _pad_target(loss_target)
    pieces_l = [_to_pack_piece(n, wts[n]).astype(BF) for n in LATE]
    lands_l = list(_place_own(pieces_l))
    src_g, land_g = _split_wait("gather_early_wait", sems_g, src_g, land_g, [tgt_padded] + lands_l, gather_plan)
    ge, gs = _gather_finish(land_g, src_g, "gather_early_finish")
    ge = ge.reshape(N_CHIPS, EARLY_ROWS, D)
    gs = gs.reshape(N_CHIPS, 48, LANES)
    full = {n: ge[:, off_e[n]:off_e[n] + BIG_ROWS[n]] for n in EARLY}
    sems_l, src_l, land_l, tied = _split_start("gather_late_start", pieces_l, lands_l, gather_plan, 3 * n_late, after=[ge])

    forward, pair, late = {}, {}, {}

    def late_forward(after):
        _, landed = _split_wait("gather_late_wait", sems_l, src_l, land_l, after, gather_plan)
        forward["sems"], forward["src"], _, zeros = _split_start(
            "gather_late_forward_start", landed, [], _forward_landed, 3 * n_late)
        return zeros[0, 0]

    def late_weights(after):
        (w_out_, w_gate_, w_up_, w_down_), _ = _split_wait(
            "gather_late_forward_wait", forward["sems"], forward["src"], [], after, _forward_landed)
        return dict(w_out=w_out_, w_gate_t=w_gate_, w_up_t=w_up_, w_down=w_down_)

    def early_grads(g_late):
        halves = [g_late[n].reshape(N_CHIPS, 2, BIG_ROWS[n] // 2, D) for n in LATE]
        pair["sems"], pair["src"], pair["land"], zeros = _split_start(
            "grad_pair_late_start", halves, [lax.empty((N_CHIPS, BIG_ROWS[n] // 2, D), F32) for n in LATE], _to_sibling,
            N_CHIPS * n_late)
        return zeros[0, 0]

    def mid_grads(after):
        halves, landed = _split_wait("grad_pair_late_wait", pair["sems"], pair["src"], pair["land"], after, _to_sibling)
        chip_sums = _pair_sums(halves, landed, ci, "grad_pair_sum_late")
        late["sems"], late["src"], late["land"], zeros = _split_start(
            "grad_chip_late_start", chip_sums, [lax.empty((3, BIG_ROWS[n] // 2, D), BF) for n in LATE], scatter_plan,
            3 * n_late)
        return zeros[0, 0]

    cols = lambda a: a.transpose(1, 0, 2).reshape(a.shape[1], N_CHIPS * a.shape[2])
    meta_full = cols(jnp.concatenate([gs[:, 0:16], gs[:, 16:32]], axis=2))
    w = dict(
        w_in_t=full["w_in"].reshape(IN_COLS, D), w_uq_t=full["w_uq"].reshape(N_HEADS * QK_HEAD, Q_LORA),
        w_ukv_t=full["w_ukv"].reshape(2 * D_ATTN, KV_LORA),
        ln1_g=ln1_g, q_a_norm_g=q_a_norm_g, kv_a_norm_g=kv_a_norm_g, q_norm_g=q_norm_g, k_norm_g=k_norm_g,
        conv_w=cols(gs[:, 32:36]), conv_b=conv_b, lru_wa=lru_wa[0], lru_ba=cols(gs[:, 36:38]), lru_wi=lru_wi[0],
        lru_bi=cols(gs[:, 38:40]), lru_lambda=cols(gs[:, 40:42]), attn_out_g=attn_out_g, rnn_out_g=rnn_out_g,
        ln2_g=ln2_g,
    )

    loss_local, grad_x, g, last = _local_step(x, tgt_padded, meta_full + tied[0, 0], w, late_forward, late_weights,
                                              early_grads, mid_grads)

    gpack = _grad_pack(EARLY, {"w_in": g["w_in_t"], "w_uq": g["w_uq_t"], "w_ukv": g["w_ukv_t"]}, EARLY_ROWS)
    full_shapes = {n: wts[n].shape for n in SMALL}
    full_shapes.update(meta_tokens=(N_META, D), conv_w=(1, CONV_W, D_RNN), lru_ba=(1, 2, D_RNN), lru_bi=(1, 2, D_RNN),
                       lru_lambda=(1, 2, D_RNN))
    gsmall = _flat_pack([g[n] for n in SMALL] + [loss_local], SMALL_PACK_ROWS)
    rbig, rsmall = _pair_exchange(gpack, [gsmall], "grad_pair_exchange")
    chip_big = _pair_sums([gpack], [rbig], ci, "grad_pair_sum")
    (chip_small,) = _elementwise(_add2, 1, "grad_pair_sum_small", gsmall, rsmall)
    sems_e, src_e, land_e, zero_e = _split_start(
        "grad_chip_early_start", chip_big, [lax.empty((3, half_e, D), BF)], scatter_plan, 3)
    sems_s, src_s, land_s, zero_s = _split_start(
        "grad_small_start", [chip_small], [lax.empty((3, SMALL_PACK_ROWS, D), F32)], everywhere, 3)

    grads, delta, new_m, new_v = {}, {}, {}, {}

    def adamw_big(n, gshard):
        _, k, cols = wts[n].shape
        as_rows = (lambda a: a[0].T) if BIG_T[n] else (lambda a: a[0])
        back = (lambda a: a.T[None]) if BIG_T[n] else (lambda a: a[None])
        g2 = gshard.reshape((cols, k) if BIG_T[n] else (k, cols))
        d_, m_, v_ = _elementwise(_adamw_math, 3, "adamw_" + n, as_rows(wts[n]), g2, as_rows(mom[n]), as_rows(var[n]))
        grads[n], delta[n], new_m[n], new_v[n] = back(g2), back(d_), back(m_), back(v_)
        return d_

    sums, landed = _split_wait("grad_chip_late_wait", late["sems"], late["src"], late["land"], last + [zero_e, zero_s],
                               scatter_plan)
    shards_l = _pair_fill(_chip_sums(sums, landed, chip, ci, "grad_chip_sum_late"), "grad_pair_fill_late")
    done_late = [adamw_big(n, buf) for n, buf in zip(LATE, shards_l)][-1]
    src_e, land_e = _split_wait("grad_chip_early_wait", sems_e, src_e, land_e, [done_late], scatter_plan)
    src_s, land_s = _split_wait("grad_small_wait", sems_s, src_s, land_s, [done_late], everywhere)
    (shard_e,) = _pair_fill(_chip_sums(src_e, land_e, chip, ci, "grad_chip_sum"), "grad_pair_fill_early")
    shard_e = shard_e.reshape(EARLY_ROWS, D)
    for n in EARLY:
        adamw_big(n, shard_e[off_e[n]:off_e[n] + BIG_ROWS[n]])
    (small_sum,) = _elementwise(_add4, 1, "grad_chip_sum_small", src_s[0], land_s[0][0], land_s[0][1], land_s[0][2])
    *small_grads, loss = _flat_unpack(small_sum, [full_shapes[n] for n in SMALL] + [()])
    small_full = dict(zip(SMALL, small_grads))
    for n in SMALL:
        a = small_full[n]
        if n in SMALL_SHARDED:
            width = wts[n].shape[-1]
            a = lax.dynamic_slice_in_dim(a, chip * width, width, axis=a.ndim - 1)
        grads[n] = a.reshape(wts[n].shape)

    rows_of = lambda a: a.reshape(-1, a.shape[-1])
    outs = _adamw_small(*[[rows_of(src[n]) for n in SMALL] for src in (wts, grads, mom, var)])
    for j, dst in enumerate((delta, new_m, new_v)):
        dst.update({n: outs[j * len(SMALL) + i].reshape(wts[n].shape) for i, n in enumerate(SMALL)})

    return (loss, grad_x, *[grads[n] for n in WEIGHTS], *[delta[n] for n in WEIGHTS],
            *[new_m[n] for n in WEIGHTS], *[new_v[n] for n in WEIGHTS])
```

```python
import math

import jax
import jax.numpy as jnp
from jax import lax
from jax.experimental import pallas as pl
from jax.experimental.pallas import tpu as pltpu

F32 = jnp.float32
BF = jnp.bfloat16
MESH = pl.DeviceIdType.MESH

D = 1024
SEQ = 2048
N_META = 16
T = N_META + SEQ
N_HEADS = 8
QK_NOPE = 64
QK_ROPE = 32
QK_HEAD = 96
V_HEAD = 64
Q_LORA = 384
KV_LORA = 256
D_ATTN = 512
D_RNN = 512
RNN_BW = 64
CONV_W = 4
LRU_C = 8.0
ROPE_THETA = 10000.0
D_FF = 2816
EPS = 1e-6
IN_COLS = 1696
ADAM_LR, ADAM_B1, ADAM_B2, ADAM_EPS, ADAM_WD, ADAM_STEP = 0.001, 0.9, 0.999, 1e-08, 0.01, 10

LANES = 128
TP = 2176
NB = 2
R = NB * TP
TR = 256
TRF = 256
TQ = 1088
HP = LANES
PC = 1792
O_CKV, O_KR, O_XR, O_XG = 384, 640, 768, 1280
CG = 128
N_CG = D_RNN // CG
VMEM_LIMIT = 56 * 1024 * 1024
N_CHIPS = 4
SCALE = QK_HEAD ** -0.5
KEY_MASK = -30000.0
LOG2_E = 1.4426950408889634
SCALE_LOG2 = SCALE * LOG2_E


def _nt(a, b):
    return lax.dot_general(a, b, (((1,), (1,)), ((), ())), preferred_element_type=F32)


def _nn(a, b):
    return jnp.dot(a, b, preferred_element_type=F32)


def _tn(a, b):
    return lax.dot_general(a, b, (((0,), (0,)), ((), ())), preferred_element_type=F32)


def _rms(x, g, n):
    ms = jnp.sum(x * x, axis=-1, keepdims=True) * (1.0 / n)
    return x * lax.rsqrt(ms + EPS) * g


def _lane_sum(y):
    return jnp.sum(y, axis=-1, keepdims=True)


def _rot(x):
    lane = lax.broadcasted_iota(jnp.int32, x.shape, 1)
    left = pltpu.roll(x, HP - 16, 1)
    right = pltpu.roll(x, 16, 1)
    lo = (lane >= QK_NOPE) & (lane < QK_NOPE + 16)
    hi = (lane >= QK_NOPE + 16) & (lane < QK_HEAD)
    return jnp.where(lo, -left, jnp.where(hi, right, 0.0))


def _head(x, g, cs, sn):
    n = x * lax.rsqrt(_lane_sum(x * x) * (1.0 / QK_HEAD) + EPS) * g
    return n * cs + _rot(n) * sn


def _head_bwd(x, g, cs, sn, dout):
    rs = lax.rsqrt(_lane_sum(x * x) * (1.0 / QK_HEAD) + EPS)
    xh = x * rs
    dn = dout * cs - _rot(dout * sn)
    gdn = g * dn
    t = _lane_sum(gdn * xh) * (1.0 / QK_HEAD)
    return rs * (gdn - xh * t), jnp.sum(dn * xh, axis=0, keepdims=True)


def _const_spec(shape):
    return pl.BlockSpec(shape, lambda *_: (0,) * len(shape), pipeline_mode=pl.Buffered(1))


def _row_spec(n, tr=TR):
    return pl.BlockSpec((tr, n), lambda i: (i, 0))


def _params(*sem, vmem=VMEM_LIMIT):
    return pltpu.CompilerParams(dimension_semantics=sem, vmem_limit_bytes=vmem)


def _stage_a_fwd(hp, cs, sn, cw):
    def body(hp_ref, cs_ref, sn_ref, ln1, win, qag, wq, kvag, wk, wv, qg, kg,
             pa_ref, xr_ref, xg_ref, q_ref, k_ref, v_ref):
        hn = _rms(hp_ref[...], ln1[...], D).astype(BF)
        p = _nt(hn, win[...])
        pa_ref[...] = p[:, :O_XR]
        xr_ref[...] = p[:, O_XR:O_XG]
        xg_ref[...] = p[:, O_XG:]
        cqn = _rms(p[:, :O_CKV], qag[...], Q_LORA).astype(BF)
        ckvn = _rms(p[:, O_CKV:O_KR], kvag[...], KV_LORA).astype(BF)
        kr = p[:, O_KR:O_XR]
        c, s = cs_ref[...], sn_ref[...]
        mask_lane = lax.broadcasted_iota(jnp.int32, (1, HP), 1) == QK_HEAD
        row = pl.program_id(0) * TRF + lax.broadcasted_iota(jnp.int32, (TRF, 1), 0)
        key_mask = jnp.where(jnp.where(row >= TP, row - TP, row) < T, 0.0, KEY_MASK)
        qraw = _nt(cqn, wq[...])
        kraw = _nt(ckvn, wk[...])
        for h in range(N_HEADS):
            sl = slice(h * HP, (h + 1) * HP)
            q_ref[:, sl] = jnp.where(mask_lane, 1.0, _head(qraw[:, sl], qg[...], c, s)).astype(BF)
            k_ref[:, sl] = jnp.where(mask_lane, key_mask, _head(kraw[:, sl] + kr, kg[...], c, s)).astype(BF)
        v_ref[...] = _nt(ckvn, wv[...]).astype(BF)

    rs = lambda n: _row_spec(n, TRF)
    return pl.pallas_call(
        body, grid=(R // TRF,), name="stage_a_fwd",
        in_specs=[rs(D), rs(HP), rs(HP), _const_spec((1, D)), _const_spec((PC, D)),
                  _const_spec((1, Q_LORA)), _const_spec((N_HEADS * HP, Q_LORA)), _const_spec((1, KV_LORA)),
                  _const_spec((N_HEADS * HP, KV_LORA)), _const_spec((D_ATTN, KV_LORA)), _const_spec((1, HP)),
                  _const_spec((1, HP))],
        out_specs=[rs(O_XR), rs(D_RNN), rs(D_RNN), rs(N_HEADS * HP), rs(N_HEADS * HP), rs(D_ATTN)],
        out_shape=[jax.ShapeDtypeStruct((R, O_XR), F32), jax.ShapeDtypeStruct((R, D_RNN), F32),
                   jax.ShapeDtypeStruct((R, D_RNN), F32), jax.ShapeDtypeStruct((R, N_HEADS * HP), BF),
                   jax.ShapeDtypeStruct((R, N_HEADS * HP), BF), jax.ShapeDtypeStruct((R, D_ATTN), BF)],
        compiler_params=_params("arbitrary"),
    )(hp, cs, sn, cw["ln1_g"], cw["win"], cw["qa_g"], cw["wq"], cw["kva_g"], cw["wk"], cw["wv"], cw["q_g"], cw["k_g"])


def _stage_a_bwd(dq, dk, dv, dxr, dxg, dh1, hp, pa, cs, sn, cw):
    def body(dq_ref, dk_ref, dv_ref, dxr_ref, dxg_ref, dh1_ref, hp_ref, pa_ref, cs_ref, sn_ref,
             ln1, win, qag, wq, kvag, wk, wv, qg, kg,
             dhp_ref, dp_ref, dqraw_ref, dkraw_ref, hn_ref, cqn_ref, ckvn_ref,
             dln1_ref, dqag_ref, dkvag_ref, dqg_ref, dkg_ref):
        @pl.when(pl.program_id(0) == 0)
        def _():
            for r in (dln1_ref, dqag_ref, dkvag_ref, dqg_ref, dkg_ref):
                r[...] = jnp.zeros_like(r)

        hn, vjp_ln1 = jax.vjp(lambda h, g: _rms(h, g, D), hp_ref[...], ln1[...])
        hn_ref[...] = hn.astype(BF)
        pa_v = pa_ref[...]
        cqn, vjp_qa = jax.vjp(lambda x, g: _rms(x, g, Q_LORA), pa_v[:, :O_CKV], qag[...])
        ckvn, vjp_kva = jax.vjp(lambda x, g: _rms(x, g, KV_LORA), pa_v[:, O_CKV:O_KR], kvag[...])
        kr = pa_v[:, O_KR:O_XR]
        cqnb, ckvnb = cqn.astype(BF), ckvn.astype(BF)
        cqn_ref[...] = cqnb
        ckvn_ref[...] = ckvnb
        c, s = cs_ref[...], sn_ref[...]
        lane = lax.broadcasted_iota(jnp.int32, (1, HP), 1)
        rope_lanes = ((lane >= QK_NOPE) & (lane < QK_HEAD)).astype(F32)
        dkr = jnp.zeros((TR, HP), F32)
        dqg = jnp.zeros((1, HP), F32)
        dkg = jnp.zeros((1, HP), F32)
        qraw = _nt(cqnb, wq[...])
        kraw = _nt(ckvnb, wk[...])
        for h in range(N_HEADS):
            sl = slice(h * HP, (h + 1) * HP)
            dqraw, dg = _head_bwd(qraw[:, sl], qg[...], c, s, dq_ref[:, sl])
            dqg = dqg + dg
            dqraw_ref[:, sl] = dqraw.astype(BF)
            dkraw, dg = _head_bwd(kraw[:, sl] + kr, kg[...], c, s, dk_ref[:, sl])
            dkg = dkg + dg
            dkraw_ref[:, sl] = dkraw.astype(BF)
            dkr = dkr + dkraw * rope_lanes
        dcq, dqag = vjp_qa(_nn(dqraw_ref[...], wq[...]))
        dckv, dkvag = vjp_kva(_nn(dkraw_ref[...], wk[...]) + _nn(dv_ref[...].astype(BF), wv[...]))
        dpb = jnp.concatenate([dcq, dckv, dkr, dxr_ref[...], dxg_ref[...]], axis=1).astype(BF)
        dp_ref[...] = dpb
        dh, dln1 = vjp_ln1(_nn(dpb, win[...]))
        dhp_ref[...] = dh + dh1_ref[...]
        dln1_ref[...] += dln1
        dqag_ref[...] += dqag
        dkvag_ref[...] += dkvag
        dqg_ref[...] += dqg
        dkg_ref[...] += dkg

    acc = lambda n: pl.BlockSpec((1, n), lambda i: (0, 0))
    return pl.pallas_call(
        body, grid=(R // TR,), name="stage_a_bwd",
        in_specs=[_row_spec(N_HEADS * HP), _row_spec(N_HEADS * HP), _row_spec(D_ATTN), _row_spec(D_RNN),
                  _row_spec(D_RNN), _row_spec(D), _row_spec(D), _row_spec(O_XR), _row_spec(HP), _row_spec(HP),
                  _const_spec((1, D)), _const_spec((PC, D)), _const_spec((1, Q_LORA)),
                  _const_spec((N_HEADS * HP, Q_LORA)), _const_spec((1, KV_LORA)),
                  _const_spec((N_HEADS * HP, KV_LORA)), _const_spec((D_ATTN, KV_LORA)), _const_spec((1, HP)),
                  _const_spec((1, HP))],
        out_specs=[_row_spec(D), _row_spec(PC), _row_spec(N_HEADS * HP), _row_spec(N_HEADS * HP), _row_spec(D),
                   _row_spec(Q_LORA), _row_spec(KV_LORA), acc(D), acc(Q_LORA), acc(KV_LORA), acc(HP), acc(HP)],
        out_shape=[jax.ShapeDtypeStruct((R, D), F32), jax.ShapeDtypeStruct((R, PC), BF),
                   jax.ShapeDtypeStruct((R, N_HEADS * HP), BF), jax.ShapeDtypeStruct((R, N_HEADS * HP), BF),
                   jax.ShapeDtypeStruct((R, D), BF), jax.ShapeDtypeStruct((R, Q_LORA), BF),
                   jax.ShapeDtypeStruct((R, KV_LORA), BF), jax.ShapeDtypeStruct((1, D), F32),
                   jax.ShapeDtypeStruct((1, Q_LORA), F32), jax.ShapeDtypeStruct((1, KV_LORA), F32),
                   jax.ShapeDtypeStruct((1, HP), F32), jax.ShapeDtypeStruct((1, HP), F32)],
        compiler_params=_params("arbitrary"),
    )(dq, dk, dv, dxr, dxg, dh1, hp, pa, cs, sn, cw["ln1_g"], cw["win"], cw["qa_g"], cw["wq"], cw["kva_g"],
      cw["wk"], cw["wv"], cw["q_g"], cw["k_g"])


def _head_mask(half, dtype):
    lane = lax.broadcasted_iota(jnp.int32, (1, 2 * V_HEAD), 1)
    return ((lane >= V_HEAD) == (half == 1)).astype(dtype)


def _attn_specs(tq):
    n_q = TP // tq
    return (NB, N_HEADS // 2, n_q), dict(
        q=pl.BlockSpec((tq, 2 * HP), lambda b, j, i: (b * n_q + i, j)),
        k=pl.BlockSpec((TP, 2 * HP), lambda b, j, i: (b, j)),
        v=pl.BlockSpec((TP, 2 * V_HEAD), lambda b, j, i: (b, j)),
        o=pl.BlockSpec((tq, 2 * V_HEAD), lambda b, j, i: (b * n_q + i, j)),
        lse=pl.BlockSpec((None, tq, 2), lambda b, j, i: (j, b * n_q + i, 0)))


TQF = 1088


def _attn_fwd(q, k, v):
    def body(q_ref, k_ref, v_ref, o_ref, lse_ref):
        v2 = v_ref[...]
        o = jnp.zeros((TQF, 2 * V_HEAD), F32)
        lse = []
        for hh in range(2):
            sl = slice(hh * HP, (hh + 1) * HP)
            raw = _nt(q_ref[:, sl], k_ref[:, sl])
            m = jnp.max(raw, axis=-1, keepdims=True)
            e = jnp.exp2((raw - m) * SCALE_LOG2)
            l = jnp.sum(e, axis=-1, keepdims=True)
            o = o + _nn(e.astype(BF), v2 * _head_mask(hh, BF)) * (1.0 / l)
            lse.append(m * SCALE_LOG2 + jnp.log(l) * LOG2_E)
        o_ref[...] = o
        lane = lax.broadcasted_iota(jnp.int32, (TQF, 2), 1)
        lse_ref[...] = jnp.where(lane == 0, lse[0], lse[1])

    grid, sp = _attn_specs(TQF)
    return pl.pallas_call(
        body, grid=grid, name="attn_fwd", in_specs=[sp["q"], sp["k"], sp["v"]], out_specs=[sp["o"], sp["lse"]],
        out_shape=[jax.ShapeDtypeStruct((R, D_ATTN), F32), jax.ShapeDtypeStruct((N_HEADS // 2, R, 2), F32)],
        compiler_params=_params("arbitrary", "arbitrary", "arbitrary"),
    )(q, k, v)


def _attn_bwd(q, k, v, o, lse, do):
    def body(q_ref, k_ref, v_ref, o_ref, lse_ref, do_ref, dq_ref, dk_ref, dv_ref):
        @pl.when(pl.program_id(2) == 0)
        def _():
            dk_ref[...] = jnp.zeros_like(dk_ref)
            dv_ref[...] = jnp.zeros_like(dv_ref)

        do = do_ref[...]
        dob = do.astype(BF)
        do_o = do * o_ref[...]
        v2 = v_ref[...]
        dv_sum = jnp.zeros((TP, 2 * V_HEAD), F32)
        for hh in range(2):
            sl = slice(hh * HP, (hh + 1) * HP)
            qb, kb = q_ref[:, sl], k_ref[:, sl]
            p = jnp.exp2(_nt(qb, kb) * SCALE_LOG2 - lse_ref[:, hh:hh + 1])
            dp = _nt(dob, v2 * _head_mask(hh, BF))
            delta = jnp.sum(do_o * _head_mask(hh, F32), axis=-1, keepdims=True)
            dsb = (p * (dp - delta) * SCALE).astype(BF)
            dq_ref[:, sl] = _nn(dsb, kb)
            dk_ref[:, sl] += _tn(dsb, qb)
            dv_sum = dv_sum + _tn(p.astype(BF), dob) * _head_mask(hh, F32)
        dv_ref[...] += dv_sum

    grid, sp = _attn_specs(TQ)
    return pl.pallas_call(
        body, grid=grid, name="attn_bwd", in_specs=[sp["q"], sp["k"], sp["v"], sp["o"], sp["lse"], sp["o"]],
        out_specs=[sp["q"], sp["k"], sp["v"]],
        out_shape=[jax.ShapeDtypeStruct((R, N_HEADS * HP), F32), jax.ShapeDtypeStruct((R, N_HEADS * HP), F32),
                   jax.ShapeDtypeStruct((R, D_ATTN), F32)],
        compiler_params=_params("arbitrary", "arbitrary", "arbitrary"),
    )(q, k, v, o, lse, do)


SEG = TP // 8


def _scan_pair(af_ref, bf_ref, hf_ref, ab_ref, bb_ref, hb_ref, pf_ref, pb_ref):
    unroll = 8

    def step(i, carry):
        hf, pf, hb, pb = carry
        for u in range(unroll):
            j = i * unroll + u
            rows_f, rows_b = pl.ds(j, 8, stride=SEG), pl.ds(SEG - 1 - j, 8, stride=SEG)
            a = af_ref[rows_f, :]
            hf, pf = a * hf + bf_ref[rows_f, :], a * pf
            hf_ref[rows_f, :] = hf
            pf_ref[rows_f, :] = pf
            a = ab_ref[rows_b, :]
            hb, pb = a * hb + bb_ref[rows_b, :], a * pb
            hb_ref[rows_b, :] = hb
            pb_ref[rows_b, :] = pb
        return hf, pf, hb, pb

    zero, one = jnp.zeros((8, CG), F32), jnp.ones((8, CG), F32)
    hf, pf, hb, pb = lax.fori_loop(0, SEG // unroll, step, (zero, one, zero, one))
    seg = lax.broadcasted_iota(jnp.int32, (8, CG), 0)
    cf, cb = zero, zero
    for s in range(1, 8):
        cf = jnp.where(seg == s, pltpu.roll(hf + pf * cf, 1, 0), cf)
        cb = jnp.where(seg == 7 - s, pltpu.roll(hb + pb * cb, 7, 0), cb)
    for s in range(8):
        rows = slice(s * SEG, (s + 1) * SEG)
        hf_ref[rows, :] = hf_ref[rows, :] + pf_ref[rows, :] * cf[s:s + 1, :]
        hb_ref[rows, :] = hb_ref[rows, :] + pb_ref[rows, :] * cb[s:s + 1, :]


def _shifts(x):
    t = lax.broadcasted_iota(jnp.int32, x.shape, 0)
    xm2 = jnp.where(t >= 2, pltpu.roll(x, 2, 0), 0.0)
    xm1 = jnp.where(t >= 1, pltpu.roll(x, 1, 0), 0.0)
    xp1 = jnp.where(t < TP - 1, pltpu.roll(x, TP - 1, 0), 0.0)
    return xm2, xm1, xp1


def _softplus(z):
    e = jnp.exp(-jnp.abs(z))
    small = e * (1.0 - e * (0.5 - e * (1.0 / 3.0)))
    return jnp.maximum(z, 0.0) + jnp.where(e < 0.01, small, jnp.log(1.0 + e))


def _sigmoid(x):
    return 0.5 * jnp.tanh(0.5 * x) + 0.5


def _one_minus_sq(log_a, a):
    x = 2.0 * log_a
    series = -x * (1.0 + x * 0.5 * (1.0 + x * (1.0 / 3.0) * (1.0 + x * 0.25)))
    return jnp.where(x > -0.05, series, 1.0 - a * a)


def _gates(row0, xc, pa_f, pi_f, pa_b, pi_b, lam_f, lam_b):
    t = row0 + lax.broadcasted_iota(jnp.int32, xc.shape, 0)
    valid = t < T
    out = []
    for pa, pi_, lam in ((pa_f, pi_f, lam_f), (pa_b, pi_b, lam_b)):
        r = _sigmoid(pa)
        gate_i = _sigmoid(pi_)
        log_a = -LRU_C * r * _softplus(-lam)
        a = jnp.exp(log_a)
        mult = jnp.sqrt(jnp.maximum(_one_minus_sq(log_a, a), 0.0))
        out += [a, jnp.where(valid, mult * (gate_i * xc), 0.0)]
    return tuple(out)


def _gates_bwd(row0, xc, pres, lams, cots):
    t = row0 + lax.broadcasted_iota(jnp.int32, xc.shape, 0)
    valid = t < T
    dxc = jnp.zeros_like(xc)
    dpres, dlams = [], []
    for d in range(2):
        pa, pi_, lam = pres[2 * d], pres[2 * d + 1], lams[d]
        da, db = cots[2 * d], jnp.where(valid, cots[2 * d + 1], 0.0)
        r = _sigmoid(pa)
        gate_i = _sigmoid(pi_)
        sp = _softplus(-lam)
        log_a = -LRU_C * r * sp
        a = jnp.exp(log_a)
        m2 = jnp.maximum(_one_minus_sq(log_a, a), 0.0)
        mult = jnp.sqrt(m2)
        dxc = dxc + db * (mult * gate_i)
        d_gate = db * (mult * xc)
        d_m2 = jnp.where(m2 > 0.0, db * (gate_i * xc) * (0.5 * lax.rsqrt(m2)), 0.0)
        d_log_a = da * a - 2.0 * d_m2 * (a * a)
        dpres += [d_log_a * (-LRU_C * sp) * (r * (1.0 - r)), d_gate * (gate_i * (1.0 - gate_i))]
        d_sp = jnp.sum(d_log_a * (-LRU_C * r), axis=0, keepdims=True)
        dlams.append(-d_sp * jax.nn.sigmoid(-lam))
    return dxc, dpres, dlams


def _rnn_specs():
    seq = pl.BlockSpec((TP, CG), lambda g, b: (b, g))
    return dict(
        seq=seq,
        cw=pl.BlockSpec((CONV_W, CG), lambda g, b: (0, g)),
        cb=pl.BlockSpec((1, CG), lambda g, b: (0, g)),
        w4=pl.BlockSpec((None, CG, 4 * CG), lambda g, b: (g, 0, 0)),
        b4=pl.BlockSpec((None, 1, 4 * CG), lambda g, b: (g, 0, 0)),
        lam=pl.BlockSpec((None, 1, 2 * CG), lambda g, b: (g, 0, 0)),
    )


def _conv(x, xm2, xm1, xp1, cw_ref, cb_ref):
    return cw_ref[0:1, :] * xm2 + cw_ref[1:2, :] * xm1 + cw_ref[2:3, :] * x + cw_ref[3:4, :] * xp1 + cb_ref[...]


TC = 128
N_TC = TP // TC


def _split4(pre):
    return pre[:, :CG], pre[:, CG:2 * CG], pre[:, 2 * CG:3 * CG], pre[:, 3 * CG:]


def _rnn_fwd(xr, xg, cw):
    def body(xr_ref, xg_ref, cw_ref, cb_ref, w4_ref, b4_ref, lam_ref, y_ref, hf_ref, hb_ref, af_ref, ab_ref, xc_ref,
             af, bf, ab, bb, pf, pb):
        x = xr_ref[...]
        xc_ref[...] = _conv(x, *_shifts(x), cw_ref, cb_ref)
        lam = lam_ref[...]

        def chunk(i, _):
            rows = pl.ds(pl.multiple_of(i * TC, TC), TC)
            xc = xc_ref[rows, :]
            pre = _nn(xc.astype(BF), w4_ref[...]) + b4_ref[...]
            a_f, b_f, a_b, b_b = _gates(i * TC, xc, *_split4(pre), lam[:, :CG], lam[:, CG:])
            af[rows, :] = a_f
            bf[rows, :] = b_f
            ab[rows, :] = a_b
            bb[rows, :] = b_b
            af_ref[rows, :] = a_f
            ab_ref[rows, :] = a_b
            return 0

        lax.fori_loop(0, N_TC, chunk, 0)
        _scan_pair(af, bf, hf_ref, ab, bb, hb_ref, pf, pb)
        y_ref[...] = (hf_ref[...] + hb_ref[...]) * jax.nn.gelu(xg_ref[...])

    sp = _rnn_specs()
    return pl.pallas_call(
        body, grid=(N_CG, NB), name="rnn_fwd",
        in_specs=[sp["seq"], sp["seq"], sp["cw"], sp["cb"], sp["w4"], sp["b4"], sp["lam"]],
        out_specs=[sp["seq"]] * 6, out_shape=[jax.ShapeDtypeStruct((R, D_RNN), F32)] * 6,
        scratch_shapes=[pltpu.VMEM((TP, CG), F32)] * 6,
        compiler_params=_params("arbitrary", "arbitrary"),
    )(xr, xg, cw["conv_w"], cw["conv_b"], cw["w4"], cw["b4"], cw["lam"])


def _rnn_bwd(dy, xr, xg, hf, hb, af, ab, xc, cw):
    def body(dy_ref, xr_ref, xg_ref, hf_ref, hb_ref, af_ref, ab_ref, xc_s, cw_ref, cb_ref, w4_ref, b4_ref, lam_ref,
             dxr_ref, dxg_ref, dcw_ref, dcb_ref, dw4_ref, db4_ref, dlam_ref,
             af_s, ab_s, dhs_s, lf_s, lb_s, daf_s, dab_s, dxc_s):
        @pl.when(pl.program_id(1) == 0)
        def _():
            for r in (dcw_ref, dcb_ref, dw4_ref, db4_ref, dlam_ref):
                r[...] = jnp.zeros_like(r)

        lam = lam_ref[...]

        def chunk1(i, _):
            rows = pl.ds(pl.multiple_of(i * TC, TC), TC)
            _, vjp_y = jax.vjp(lambda h, g: h * jax.nn.gelu(g), hf_ref[rows, :] + hb_ref[rows, :], xg_ref[rows, :])
            dhs, dxg = vjp_y(dy_ref[rows, :])
            dhs_s[rows, :] = dhs
            dxg_ref[rows, :] = dxg
            return 0

        lax.fori_loop(0, N_TC, chunk1, 0)
        t = lax.broadcasted_iota(jnp.int32, (TP, CG), 0)
        af_s[...] = pltpu.roll(af_ref[...], TP - 1, 0)
        ab_s[...] = pltpu.roll(ab_ref[...], 1, 0)
        _scan_pair(ab_s, dhs_s, lb_s, af_s, dhs_s, lf_s, dab_s, daf_s)
        daf_s[...] = lf_s[...] * jnp.where(t >= 1, pltpu.roll(hf_ref[...], 1, 0), 0.0)
        dab_s[...] = lb_s[...] * jnp.where(t < TP - 1, pltpu.roll(hb_ref[...], TP - 1, 0), 0.0)

        def chunk2(i, _):
            rows = pl.ds(pl.multiple_of(i * TC, TC), TC)
            xc = xc_s[rows, :]
            xcb = xc.astype(BF)
            pre = _nn(xcb, w4_ref[...]) + b4_ref[...]
            dxc, dpres, dlams = _gates_bwd(i * TC, xc, _split4(pre), (lam[:, :CG], lam[:, CG:]),
                                           (daf_s[rows, :], lf_s[rows, :], dab_s[rows, :], lb_s[rows, :]))
            dpre = jnp.concatenate(dpres, axis=1)
            dpreb = dpre.astype(BF)
            dxc_s[rows, :] = dxc + _nt(dpreb, w4_ref[...])
            dw4_ref[...] += _tn(xcb, dpreb)
            db4_ref[...] += jnp.sum(dpre, axis=0, keepdims=True)
            dlam_ref[...] += jnp.concatenate(dlams, axis=1)
            return 0

        lax.fori_loop(0, N_TC, chunk2, 0)
        dxc = dxc_s[...]
        x = xr_ref[...]
        taps = (jnp.where(t < TP - 2, pltpu.roll(dxc, TP - 2, 0), 0.0), jnp.where(t < TP - 1, pltpu.roll(dxc, TP - 1, 0), 0.0),
                dxc, jnp.where(t >= 1, pltpu.roll(dxc, 1, 0), 0.0))
        dcb_ref[...] += jnp.sum(dxc, axis=0, keepdims=True)
        dxr = jnp.zeros_like(dxc)
        for tap, shifted in enumerate(taps):
            dcw_ref[tap:tap + 1, :] += jnp.sum(x * shifted, axis=0, keepdims=True)
            dxr = dxr + cw_ref[tap:tap + 1, :] * shifted
        dxr_ref[...] = dxr

    sp = _rnn_specs()
    return pl.pallas_call(
        body, grid=(N_CG, NB), name="rnn_bwd",
        in_specs=[sp["seq"]] * 8 + [sp["cw"], sp["cb"], sp["w4"], sp["b4"], sp["lam"]],
        out_specs=[sp["seq"], sp["seq"], sp["cw"], sp["cb"], sp["w4"], sp["b4"], sp["lam"]],
        out_shape=[jax.ShapeDtypeStruct((R, D_RNN), F32), jax.ShapeDtypeStruct((R, D_RNN), F32),
                   jax.ShapeDtypeStruct((CONV_W, D_RNN), F32), jax.ShapeDtypeStruct((1, D_RNN), F32),
                   jax.ShapeDtypeStruct((N_CG, CG, 4 * CG), F32), jax.ShapeDtypeStruct((N_CG, 1, 4 * CG), F32),
                   jax.ShapeDtypeStruct((N_CG, 1, 2 * CG), F32)],
        scratch_shapes=[pltpu.VMEM((TP, CG), F32)] * 8,
        compiler_params=_params("arbitrary", "arbitrary"),
    )(dy, xr, xg, hf, hb, af, ab, xc, cw["conv_w"], cw["conv_b"], cw["w4"], cw["b4"], cw["lam"])


TD = 256
STAGE_D_VMEM = 58 * 1024 * 1024


def _stage_d(hp, o, y, tgt, cw):
    def body(hp_ref, o_ref, y_ref, tgt_ref, ga, gr, wout, ln2, wg, wu, wd,
             do_ref, dy_ref, dh1_ref, mix_ref, dh1b_ref, hn2_ref, dg_ref, du_ref, act_ref, dh2b_ref,
             loss_ref, dga_ref, dgr_ref, dln2_ref):
        i = pl.program_id(0)

        @pl.when(i == 0)
        def _():
            for r in (loss_ref, dga_ref, dgr_ref, dln2_ref):
                r[...] = jnp.zeros_like(r)

        mix_a, vjp_a = jax.vjp(lambda x, g: _rms(x, g, D_ATTN), o_ref[...], ga[...])
        mix_r, vjp_r = jax.vjp(lambda x, g: _rms(x, g, D_RNN), y_ref[...], gr[...])
        mab, mrb = mix_a.astype(BF), mix_r.astype(BF)
        mix_ref[:, :D_ATTN] = mab
        mix_ref[:, D_ATTN:] = mrb
        h1 = hp_ref[...] + _nn(mab, wout[:D_ATTN, :]) + _nn(mrb, wout[D_ATTN:, :])
        hn2, vjp_ln2 = jax.vjp(lambda x, g: _rms(x, g, D), h1, ln2[...])
        hn2b = hn2.astype(BF)
        hn2_ref[...] = hn2b
        act, vjp_act = jax.vjp(lambda g, u: jax.nn.silu(g) * u, _nt(hn2b, wg[...]), _nt(hn2b, wu[...]))
        actb = act.astype(BF)
        act_ref[...] = actb
        h2 = h1 + _nn(actb, wd[...])
        row = i * TD + lax.broadcasted_iota(jnp.int32, (TD, 1), 0)
        t = jnp.where(row >= TP, row - TP, row)
        err = jnp.where((t >= N_META) & (t < T), h2 - tgt_ref[...], 0.0)
        loss_ref[...] += jnp.sum(err * err) * (0.5 / D)
        dh2b = (err * (1.0 / D)).astype(BF)
        dh2b_ref[...] = dh2b
        dg, du = vjp_act(_nt(dh2b, wd[...]))
        dgb, dub = dg.astype(BF), du.astype(BF)
        dg_ref[...] = dgb
        du_ref[...] = dub
        dh1n, dln2 = vjp_ln2(_nn(dgb, wg[...]) + _nn(dub, wu[...]))
        dh1 = err * (1.0 / D) + dh1n
        dh1_ref[...] = dh1
        dh1b = dh1.astype(BF)
        dh1b_ref[...] = dh1b
        dmix = _nt(dh1b, wout[...])
        do, dga = vjp_a(dmix[:, :D_ATTN])
        dyr, dgr = vjp_r(dmix[:, D_ATTN:])
        do_ref[...] = do
        dy_ref[...] = dyr
        dga_ref[...] += dga
        dgr_ref[...] += dgr
        dln2_ref[...] += dln2

    rs = lambda n: _row_spec(n, TD)
    acc = lambda n: pl.BlockSpec((1, n), lambda i: (0, 0))
    return pl.pallas_call(
        body, grid=(R // TD,), name="stage_d",
        in_specs=[rs(D), rs(D_ATTN), rs(D_RNN), rs(D), _const_spec((1, D_ATTN)), _const_spec((1, D_RNN)),
                  _const_spec((D, D)), _const_spec((1, D)), _const_spec((D_FF, D)), _const_spec((D_FF, D)),
                  _const_spec((D_FF, D))],
        out_specs=[rs(D_ATTN), rs(D_RNN), rs(D), rs(D), rs(D), rs(D), rs(D_FF), rs(D_FF), rs(D_FF), rs(D),
                   acc(1), acc(D_ATTN), acc(D_RNN), acc(D)],
        out_shape=[jax.ShapeDtypeStruct((R, D_ATTN), F32), jax.ShapeDtypeStruct((R, D_RNN), F32),
                   jax.ShapeDtypeStruct((R, D), F32), jax.ShapeDtypeStruct((R, D), BF),
                   jax.ShapeDtypeStruct((R, D), BF), jax.ShapeDtypeStruct((R, D), BF),
                   jax.ShapeDtypeStruct((R, D_FF), BF), jax.ShapeDtypeStruct((R, D_FF), BF),
                   jax.ShapeDtypeStruct((R, D_FF), BF), jax.ShapeDtypeStruct((R, D), BF),
                   jax.ShapeDtypeStruct((1, 1), F32), jax.ShapeDtypeStruct((1, D_ATTN), F32),
                   jax.ShapeDtypeStruct((1, D_RNN), F32), jax.ShapeDtypeStruct((1, D), F32)],
        compiler_params=_params("arbitrary", vmem=STAGE_D_VMEM),
    )(hp, o, y, tgt, cw["ga"], cw["gr"], cw["wout"], cw["ln2_g"], cw["wg"], cw["wu"], cw["wd"])


TW = 2176


def _wgrad(a, b, name, tk=None):
    ka, nb = a.shape[1], b.shape[1]
    tk = ka if tk is None else tk

    def body(a_ref, b_ref, o_ref):
        @pl.when(pl.program_id(1) == 0)
        def _():
            o_ref[...] = jnp.zeros_like(o_ref)

        o_ref[...] += _tn(a_ref[...].astype(BF), b_ref[...].astype(BF))

    return pl.pallas_call(
        body, grid=(ka // tk, R // TW), name=name,
        in_specs=[pl.BlockSpec((TW, tk), lambda k, r: (r, k)), pl.BlockSpec((TW, nb), lambda k, r: (r, 0))],
        out_specs=pl.BlockSpec((tk, nb), lambda k, r: (k, 0)),
        out_shape=jax.ShapeDtypeStruct((ka, nb), F32),
        compiler_params=_params("arbitrary", "arbitrary"),
    )(a, b)


def _wgrad_heads(dq, dk, dv, cqn, ckvn):
    def body(dq_ref, dk_ref, dv_ref, cqn_ref, ckvn_ref, oq_ref, ok_ref, ov_ref):
        @pl.when(pl.program_id(0) == 0)
        def _():
            for r in (oq_ref, ok_ref, ov_ref):
                r[...] = jnp.zeros_like(r)

        ckvnb = ckvn_ref[...]
        oq_ref[...] += _tn(dq_ref[...], cqn_ref[...])
        ok_ref[...] += _tn(dk_ref[...], ckvnb)
        ov_ref[...] += _tn(dv_ref[...].astype(BF), ckvnb)

    rows = lambda a: pl.BlockSpec((TW, a.shape[1]), lambda r: (r, 0))
    full = lambda m, n: pl.BlockSpec((m, n), lambda r: (0, 0))
    shapes = [(dq.shape[1], cqn.shape[1]), (dk.shape[1], ckvn.shape[1]), (dv.shape[1], ckvn.shape[1])]
    return pl.pallas_call(
        body, grid=(R // TW,), name="wgrad_heads", in_specs=[rows(a) for a in (dq, dk, dv, cqn, ckvn)],
        out_specs=[full(*s) for s in shapes], out_shape=[jax.ShapeDtypeStruct(s, F32) for s in shapes],
        compiler_params=_params("arbitrary"),
    )(dq, dk, dv, cqn, ckvn)


def _rope_tables():
    half = QK_ROPE // 2
    freqs = 1.0 / (ROPE_THETA ** (jnp.arange(half, dtype=F32) / half))
    ang = jnp.arange(TP, dtype=F32)[:, None] * freqs[None, :]
    ones = jnp.ones((TP, QK_NOPE), F32)
    zeros = jnp.zeros((TP, QK_NOPE), F32)
    pad1 = jnp.ones((TP, HP - QK_HEAD), F32)
    pad0 = jnp.zeros((TP, HP - QK_HEAD), F32)
    cs = jnp.concatenate([ones, jnp.cos(ang), jnp.cos(ang), pad1], axis=1)
    sn = jnp.concatenate([zeros, jnp.sin(ang), jnp.sin(ang), pad0], axis=1)
    return jnp.tile(cs, (NB, 1)), jnp.tile(sn, (NB, 1))


def _pad_rows(a, lo, hi):
    return jnp.pad(a, ((0, 0), (lo, hi), (0, 0)))


def _pad_target(target):
    return _pad_rows(target, N_META, TP - T).reshape(R, D)


def _compute_weights(w):
    win_t = w["w_in_t"]
    kr = win_t[O_KR:O_KR + QK_ROPE]
    win = jnp.concatenate([win_t[:O_KR], jnp.zeros((QK_NOPE, D), F32), kr,
                           jnp.zeros((HP - QK_HEAD, D), F32), win_t[O_KR + QK_ROPE:]], axis=0)
    wq = _pad_rows(w["w_uq_t"].reshape(N_HEADS, QK_HEAD, Q_LORA), 0, HP - QK_HEAD)
    wkv = w["w_ukv_t"].reshape(N_HEADS, QK_NOPE + V_HEAD, KV_LORA)
    wk = _pad_rows(wkv[:, :QK_NOPE], 0, HP - QK_NOPE)
    wv = wkv[:, QK_NOPE:].reshape(D_ATTN, KV_LORA)
    gates = jnp.stack([w["lru_wa"][0], w["lru_wi"][0], w["lru_wa"][1], w["lru_wi"][1]])
    blk = gates.reshape(4, N_CG, 2, RNN_BW, RNN_BW)
    dense = jnp.einsum("tcaij,ab->tcaibj", blk, jnp.eye(2, dtype=F32)).reshape(4, N_CG, CG, CG)
    w4 = dense.transpose(1, 2, 0, 3).reshape(N_CG, CG, 4 * CG)
    bias = jnp.stack([w["lru_ba"][0], w["lru_bi"][0], w["lru_ba"][1], w["lru_bi"][1]])
    b4 = bias.reshape(4, N_CG, CG).transpose(1, 0, 2).reshape(N_CG, 1, 4 * CG)
    lam = w["lru_lambda"].reshape(2, N_CG, CG).transpose(1, 0, 2).reshape(N_CG, 1, 2 * CG)
    pad_g = lambda g: jnp.pad(g.reshape(1, QK_HEAD), ((0, 0), (0, HP - QK_HEAD)))
    return dict(
        ln1_g=w["ln1_g"].reshape(1, D), win=win.astype(BF), qa_g=w["q_a_norm_g"].reshape(1, Q_LORA),
        wq=wq.astype(BF).reshape(N_HEADS * HP, Q_LORA), kva_g=w["kv_a_norm_g"].reshape(1, KV_LORA),
        wk=wk.astype(BF).reshape(N_HEADS * HP, KV_LORA), wv=wv.astype(BF),
        q_g=pad_g(w["q_norm_g"]), k_g=pad_g(w["k_norm_g"]),
        conv_w=w["conv_w"].reshape(CONV_W, D_RNN), conv_b=w["conv_b"].reshape(1, D_RNN),
        w4=w4.astype(BF), b4=b4, lam=lam,
        ga=w["attn_out_g"].reshape(1, D_ATTN), gr=w["rnn_out_g"].reshape(1, D_RNN), ln2_g=w["ln2_g"].reshape(1, D),
    )


def _local_step(x, target, meta, w, late_forward, late_weights, early_grads, mid_grads):
    cw = _compute_weights(w)
    cs, sn = _rope_tables()
    hp = jnp.concatenate([jnp.broadcast_to(meta[None], (NB, N_META, D)), x,
                          jnp.zeros((NB, TP - T, D), F32)], axis=1).reshape(R, D)
    tgt = target if target.ndim == 2 else _pad_target(target)

    pa, xr, xg, q, k, v = _stage_a_fwd(hp, cs, sn, cw)
    o, lse = _attn_fwd(q, k, v)
    cw["conv_b"] = cw["conv_b"] + late_forward([o])
    y, hf, hb, af, ab, xc = _rnn_fwd(xr, xg, cw)
    late = late_weights([y])
    cw.update(wout=late["w_out"], wg=late["w_gate_t"], wu=late["w_up_t"], wd=late["w_down"])
    (do, dy, dh1, mixb, dh1b, hn2b, dgb, dub, actb, dh2b, loss, dga, dgr, dln2) = _stage_d(hp, o, y, tgt, cw)
    dwout = _wgrad(mixb, dh1b, "wgrad_out")
    dwg = _wgrad(dgb, hn2b, "wgrad_gate", tk=D_FF // 2)
    dwu = _wgrad(dub, hn2b, "wgrad_up", tk=D_FF // 2)
    dwd = _wgrad(actb, dh2b, "wgrad_down", tk=D_FF // 2)
    zero = early_grads(dict(w_out=dwout, w_gate=dwg, w_up=dwu, w_down=dwd))
    cw["conv_b"] = cw["conv_b"] + zero
    dxr, dxg, dcw, dcb, dw4, db4, dlam = _rnn_bwd(dy, xr, xg, hf, hb, af, ab, xc, cw)
    zero = mid_grads([dxr])
    dq, dk, dv = _attn_bwd(q, k, v, o, lse, do)
    (dhp, dpb, dqrawb, dkrawb, hn1b, cqnb, ckvnb, dln1, dqag, dkvag, dqg, dkg) = _stage_a_bwd(
        dq, dk, dv, dxr, dxg, dh1, hp, pa, cs, sn, dict(cw, qa_g=cw["qa_g"] + zero))

    dwin = _wgrad(dpb, hn1b, "wgrad_in", tk=PC // 2)
    dwq, dwk, dwv = _wgrad_heads(dqrawb, dkrawb, dv, cqnb, ckvnb)

    dwin_t = jnp.concatenate([dwin[:O_KR], dwin[O_KR + QK_NOPE:O_KR + QK_HEAD], dwin[O_XR:]], axis=0)
    dwq_t = dwq.reshape(N_HEADS, HP, Q_LORA)[:, :QK_HEAD].reshape(N_HEADS * QK_HEAD, Q_LORA)
    dwkv_t = jnp.concatenate([dwk.reshape(N_HEADS, HP, KV_LORA)[:, :QK_NOPE],
                              dwv.reshape(N_HEADS, V_HEAD, KV_LORA)], axis=1).reshape(2 * D_ATTN, KV_LORA)
    d4 = dw4.reshape(N_CG, 2, RNN_BW, 4, 2, RNN_BW)
    dgates = jnp.stack([d4[:, 0, :, :, 0, :], d4[:, 1, :, :, 1, :]], axis=1)
    dgates = dgates.transpose(3, 0, 1, 2, 4).reshape(4, N_HEADS, RNN_BW, RNN_BW)
    dbias = db4.reshape(N_CG, 4, CG).transpose(1, 0, 2).reshape(4, D_RNN)
    dhp3 = dhp.reshape(NB, TP, D)
    grads = dict(
        meta_tokens=jnp.sum(dhp3[:, :N_META], axis=0),
        ln1_g=dln1, w_in_t=dwin_t, q_a_norm_g=dqag, w_uq_t=dwq_t, kv_a_norm_g=dkvag, w_ukv_t=dwkv_t,
        q_norm_g=dqg[:, :QK_HEAD], k_norm_g=dkg[:, :QK_HEAD], conv_w=dcw[None], conv_b=dcb,
        lru_wa=jnp.stack([dgates[0], dgates[2]])[None], lru_ba=jnp.stack([dbias[0], dbias[2]])[None],
        lru_wi=jnp.stack([dgates[1], dgates[3]])[None], lru_bi=jnp.stack([dbias[1], dbias[3]])[None],
        lru_lambda=dlam.reshape(N_CG, 2, CG).transpose(1, 0, 2).reshape(1, 2, D_RNN),
        attn_out_g=dga, rnn_out_g=dgr, ln2_g=dln2,
    )
    return loss[0, 0], dhp3[:, N_META:T], grads, [dhp, dwin]


_ANY = pl.BlockSpec(memory_space=pl.ANY)


def _place():
    return lax.axis_index("x"), lax.axis_index("y"), lax.axis_index("c")


def _other_chips(x, y):
    return [(1 - x, y), (x, 1 - y), (1 - x, 1 - y)]


def _pair_exchange(big, whole, name):
    n_s, _, m, n = big.shape
    n_copies = n_s + len(whole)

    def body(*refs):
        big_ref, whole_refs = refs[0], refs[1:1 + len(whole)]
        rbig_ref, rwhole_refs = refs[1 + len(whole)], refs[2 + len(whole):2 + 2 * len(whole)]
        send_sems, recv_sems = refs[-2:]
        x, y, c = _place()
        sibling = (x, y, 1 - c)
        copies = [pltpu.make_async_remote_copy(
            src_ref=big_ref.at[s, 1 - c], dst_ref=rbig_ref.at[s], send_sem=send_sems.at[s], recv_sem=recv_sems.at[s],
            device_id=sibling, device_id_type=MESH) for s in range(n_s)]
        copies += [pltpu.make_async_remote_copy(
            src_ref=a, dst_ref=r, send_sem=send_sems.at[n_s + i], recv_sem=recv_sems.at[n_s + i],
            device_id=sibling, device_id_type=MESH) for i, (a, r) in enumerate(zip(whole_refs, rwhole_refs))]
        for cp in copies:
            cp.start()
        for cp in copies:
            cp.wait()

    return pl.pallas_call(
        body, name=name,
        out_shape=[jax.ShapeDtypeStruct((n_s, m, n), big.dtype)] + [jax.ShapeDtypeStruct(a.shape, a.dtype) for a in whole],
        in_specs=[_ANY] * (1 + len(whole)), out_specs=[_ANY] * (1 + len(whole)),
        scratch_shapes=[pltpu.SemaphoreType.DMA((n_copies,)), pltpu.SemaphoreType.DMA((n_copies,))],
    )(big, *whole)


_HBM = pl.BlockSpec(memory_space=pltpu.HBM)
_SEM = pl.BlockSpec(memory_space=pltpu.SEMAPHORE)
_EFFECT = pltpu.SideEffectType.DATAFLOW_SIDE_EFFECTING


def _split_copies(src_refs, land_refs, sems, plan, sending):
    n = len(sems) // 2
    return [pltpu.make_async_remote_copy(src_ref=s, dst_ref=d, send_sem=sems[k], recv_sem=sems[n + k], device_id=to,
                                         device_id_type=MESH)
            for k, (s, d, to) in enumerate(plan(src_refs, land_refs, sending))]


def _to_chips(src_at, land_at):
    def plan(src_refs, land_refs, sending):
        x, y, c = _place()
        return [(src_at(s, tx, ty, c), land_at(l, j, *((x, y) if sending else (tx, ty)), c), (tx, ty, c))
                for s, l in zip(src_refs, land_refs) for j, (tx, ty) in enumerate(_other_chips(x, y))]
    return plan


def _to_sibling(src_refs, land_refs, sending):
    x, y, c = _place()
    return [(s.at[k, 1 - c], l.at[k], (x, y, 1 - c)) for s, l in zip(src_refs, land_refs) for k in range(N_CHIPS)]


def _split_start(name, srcs, lands, plan, n, after=()):
    srcs, lands, after = list(srcs), list(lands), list(after)
    k, kb = len(srcs), len(srcs) + len(lands)

    def body(*refs):
        outs = refs[kb + len(after):]
        for cp in _split_copies(refs[:k], refs[k:kb], outs[:2 * n], plan, True):
            cp.start()
        outs[2 * n + kb][...] = jnp.zeros_like(outs[2 * n + kb])

    outs = pl.pallas_call(
        body, name=name,
        out_shape=(pltpu.SemaphoreType.DMA(()),) * (2 * n) + tuple(pltpu.HBM(a.shape, a.dtype) for a in srcs + lands)
        + (jax.ShapeDtypeStruct((8, LANES), F32),),
        in_specs=(_HBM,) * kb + (_ANY,) * len(after),
        out_specs=(_SEM,) * (2 * n) + (_HBM,) * kb + (pl.BlockSpec(memory_space=pltpu.VMEM),),
        input_output_aliases={i: 2 * n + i for i in range(kb)},
        compiler_params=pltpu.CompilerParams(has_side_effects=_EFFECT),
    )(*[pltpu.with_memory_space_constraint(a, pltpu.HBM) for a in srcs + lands], *after)
    return outs[:2 * n], list(outs[2 * n:2 * n + k]), list(outs[2 * n + k:2 * n + kb]), outs[2 * n + kb]


def _split_wait(name, sems, srcs, lands, after, plan):
    srcs, lands = list(srcs), list(lands)
    k, kb = len(srcs), len(srcs) + len(lands)

    def body(*refs):
        for cp in _split_copies(refs[:k], refs[k:kb], refs[kb:kb + len(sems)], plan, False):
            cp.wait_send()
            cp.wait_recv()

    outs = pl.pallas_call(
        body, name=name, out_shape=tuple(pltpu.HBM(a.shape, a.dtype) for a in srcs + lands),
        in_specs=(_HBM,) * kb + (_SEM,) * len(sems) + (_ANY,) * len(after), out_specs=(_HBM,) * kb,
        input_output_aliases={i: i for i in range(kb)}, compiler_params=pltpu.CompilerParams(has_side_effects=_EFFECT),
    )(*srcs, *lands, *sems, *after)
    return list(outs[:k]), list(outs[k:])


def _forward_landed(src_refs, land_refs, sending):
    x, y, c = _place()
    copies = []
    for ref in src_refs:
        m = ref.shape[0] // 8
        for tx, ty in _other_chips(x, y):
            rows = ref.at[pl.ds((4 * tx + 2 * ty + (c if sending else 1 - c)) * m, m), :]
            copies.append((rows, rows, (x, y, 1 - c)))
    return copies


def _place_own(pieces):
    k = len(pieces)

    def body(*refs):
        piece_refs, out_refs, stages = refs[:k], refs[k:2 * k], refs[2 * k:3 * k]
        load_sems, store_sems = refs[3 * k:]
        x, y, _ = _place()
        loads = [pltpu.make_async_copy(piece_refs[a], stages[a], load_sems.at[a]) for a in range(k)]
        stores = [pltpu.make_async_copy(
            stages[a], out_refs[a].at[pl.ds((2 * x + y) * pieces[a].shape[0], pieces[a].shape[0]), :], store_sems.at[a])
            for a in range(k)]
        for cp in loads:
            cp.start()
        for ld, st in zip(loads, stores):
            ld.wait()
            st.start()
        for cp in stores:
            cp.wait()

    return pl.pallas_call(
        body, name="gather_late_place_own",
        out_shape=[jax.ShapeDtypeStruct((N_CHIPS * p.shape[0], p.shape[1]), p.dtype) for p in pieces],
        in_specs=[_ANY] * k, out_specs=[_ANY] * k,
        scratch_shapes=[pltpu.VMEM(p.shape, p.dtype) for p in pieces]
        + [pltpu.SemaphoreType.DMA((k,)), pltpu.SemaphoreType.DMA((k,))],
    )(*pieces)


def _gather_finish(lands, pieces, name):
    k = len(lands)

    def body(*refs):
        land_refs, piece_refs, out_refs, stages = refs[:k], refs[k:2 * k], refs[2 * k:3 * k], refs[3 * k:4 * k]
        send_sems, recv_sems, load_sems, store_sems = refs[4 * k:]
        x, y, c = _place()
        sibling = (x, y, 1 - c)
        remote, loads, stores, arrivals = [], [], [], []
        for a in range(k):
            m = lands[a].shape[0] // 8

            def rows(px, py, pc, ref, m=m):
                return ref.at[pl.ds((4 * px + 2 * py + pc) * m, m), :]

            for j, (tx, ty) in enumerate(_other_chips(x, y)):
                sems = dict(send_sem=send_sems.at[3 * a + j], recv_sem=recv_sems.at[3 * a + j], device_id=sibling,
                            device_id_type=MESH)
                remote.append(pltpu.make_async_remote_copy(
                    src_ref=rows(tx, ty, c, land_refs[a]), dst_ref=rows(tx, ty, c, out_refs[a]), **sems))
                arrivals.append(pltpu.make_async_remote_copy(
                    src_ref=rows(tx, ty, 1 - c, out_refs[a]), dst_ref=rows(tx, ty, 1 - c, out_refs[a]), **sems))
            for h in range(2):
                loads.append(pltpu.make_async_copy(piece_refs[a].at[pl.ds(h * m, m), :], stages[a].at[h],
                                                   load_sems.at[2 * a + h]))
                stores.append(pltpu.make_async_copy(stages[a].at[h], rows(x, y, h, out_refs[a]), store_sems.at[2 * a + h]))
        for cp in remote + loads:
            cp.start()
        for ld, st in zip(loads, stores):
            ld.wait()
            st.start()
        for cp, arrival in zip(remote, arrivals):
            cp.wait_send()
            arrival.wait_recv()
        for cp in stores:
            cp.wait()

    return pl.pallas_call(
        body, name=name, out_shape=[jax.ShapeDtypeStruct(a.shape, a.dtype) for a in lands],
        in_specs=[_ANY] * (2 * k), out_specs=[_ANY] * k, input_output_aliases={i: i for i in range(k)},
        scratch_shapes=[pltpu.VMEM((2, a.shape[0] // 8, a.shape[1]), a.dtype) for a in lands]
        + [pltpu.SemaphoreType.DMA((3 * k,)), pltpu.SemaphoreType.DMA((3 * k,)), pltpu.SemaphoreType.DMA((2 * k,)),
           pltpu.SemaphoreType.DMA((2 * k,))],
    )(*lands, *pieces)


def _pair_fill(bufs, name):
    k = len(bufs)

    def body(*refs):
        send_sems, recv_sems = refs[-2:]
        x, y, c = _place()
        copies = [pltpu.make_async_remote_copy(
            src_ref=refs[i].at[c], dst_ref=refs[k + i].at[c], send_sem=send_sems.at[i], recv_sem=recv_sems.at[i],
            device_id=(x, y, 1 - c), device_id_type=MESH) for i in range(k)]
        for cp in copies:
            cp.start()
        for i, cp in enumerate(copies):
            cp.wait_send()
            pltpu.make_async_remote_copy(
                src_ref=refs[i].at[1 - c], dst_ref=refs[k + i].at[1 - c], send_sem=send_sems.at[i],
                recv_sem=recv_sems.at[i], device_id=(x, y, 1 - c), device_id_type=MESH).wait_recv()

    return pl.pallas_call(
        body, name=name, out_shape=[jax.ShapeDtypeStruct(a.shape, a.dtype) for a in bufs], in_specs=[_ANY] * k,
        out_specs=[_ANY] * k, input_output_aliases={i: i for i in range(k)},
        scratch_shapes=[pltpu.SemaphoreType.DMA((k,)), pltpu.SemaphoreType.DMA((k,))],
    )(*bufs)


def _row_tile(rows, cap=512):
    for t in range(cap - cap % 8, 7, -8):
        if rows % t == 0:
            return t
    return rows


def _elementwise(fn, n_out, name, *arrs, out_dtype=F32):
    rows, cols = arrs[0].shape
    tr = _row_tile(rows)
    n_in = len(arrs)

    def body(*refs):
        outs = fn(*[r[...].astype(F32) for r in refs[:n_in]])
        for r, o in zip(refs[n_in:], outs):
            r[...] = o.astype(out_dtype)

    spec = pl.BlockSpec((tr, cols), lambda i: (i, 0))
    return pl.pallas_call(
        body, grid=(rows // tr,), name=name, in_specs=[spec] * n_in, out_specs=[spec] * n_out,
        out_shape=[jax.ShapeDtypeStruct((rows, cols), out_dtype)] * n_out, compiler_params=_params("arbitrary"),
    )(*arrs)


def _pair_sums(gpacks, rbigs, ci, name):
    k = len(gpacks)

    def body(c_ref, *refs):
        for g_ref, r_ref, o_ref in zip(refs[:k], refs[k:2 * k], refs[2 * k:]):
            o_ref[...] = (g_ref[...] + r_ref[...]).astype(BF)

    half = lambda a: pl.BlockSpec((None,) + a.shape[1:], lambda s, c: (s, 0, 0))
    return pl.pallas_call(
        body, name=name, out_shape=[jax.ShapeDtypeStruct(r.shape, BF) for r in rbigs],
        grid_spec=pltpu.PrefetchScalarGridSpec(
            num_scalar_prefetch=1, grid=(N_CHIPS,),
            in_specs=[pl.BlockSpec((None, None) + g.shape[2:], lambda s, c: (s, c[0], 0, 0)) for g in gpacks]
            + [half(r) for r in rbigs],
            out_specs=[half(r) for r in rbigs]),
        compiler_params=_params("arbitrary"),
    )(ci.reshape(1), *gpacks, *rbigs)


def _chip_sums(sums, landed, chip, ci, name):
    k = len(sums)

    def body(p_ref, *refs):
        for own_ref, land_ref, o_ref in zip(refs[:k], refs[k:2 * k], refs[2 * k:]):
            f = lambda v: v.astype(F32)
            o_ref[...] = _add4(f(own_ref[...]), f(land_ref[0]), f(land_ref[1]), f(land_ref[2]))[0]

    return pl.pallas_call(
        body, name=name, out_shape=[jax.ShapeDtypeStruct((2,) + s.shape[1:], F32) for s in sums],
        grid_spec=pltpu.PrefetchScalarGridSpec(
            num_scalar_prefetch=1, grid=(1,),
            in_specs=[pl.BlockSpec((None,) + s.shape[1:], lambda i, p: (p[0], 0, 0)) for s in sums]
            + [pl.BlockSpec(l.shape, lambda i, p: (0, 0, 0)) for l in landed],
            out_specs=[pl.BlockSpec((None,) + s.shape[1:], lambda i, p: (p[1], 0, 0)) for s in sums]),
        compiler_params=_params("arbitrary"),
    )(jnp.stack([chip, ci]), *sums, *landed)


def _add2(a, b):
    return (a + b,)


def _add4(own, r0, r1, r2):
    return ((own + r2) + (r0 + r1),)


def _adamw_rows(ws, gs, ms, vs, name):
    k = len(ws)
    steps = next(s for s in (4, 2, 1) if all(w.shape[0] % (8 * s) == 0 for w in ws))

    def body(*refs):
        for i in range(k):
            outs = _adamw_math(*[refs[j * k + i][...] for j in range(4)])
            for j, o in enumerate(outs):
                refs[(4 + j) * k + i][...] = o

    specs = [pl.BlockSpec((w.shape[0] // steps, w.shape[1]), lambda i: (i, 0)) for w in ws]
    return pl.pallas_call(
        body, grid=(steps,), name=name, in_specs=specs * 4, out_specs=specs * 3,
        out_shape=[jax.ShapeDtypeStruct(w.shape, F32) for w in ws] * 3, compiler_params=_params("arbitrary"),
    )(*ws, *gs, *ms, *vs)


def _adamw_small(ws, gs, ms, vs):
    k = len(ws)

    def body(*refs):
        for i in range(k):
            outs = _adamw_math(*[refs[j * k + i][...] for j in range(4)])
            for j, o in enumerate(outs):
                refs[(4 + j) * k + i][...] = o

    return pl.pallas_call(
        body, name="adamw_small", out_shape=[jax.ShapeDtypeStruct(w.shape, F32) for w in ws] * 3,
    )(*ws, *gs, *ms, *vs)


def _adamw_math(w, g, m, v):
    m = ADAM_B1 * m + (1.0 - ADAM_B1) * g
    v = ADAM_B2 * v + (1.0 - ADAM_B2) * (g * g)
    m_hat = m / (1.0 - ADAM_B1 ** ADAM_STEP)
    v_hat = v / (1.0 - ADAM_B2 ** ADAM_STEP)
    delta = -ADAM_LR * (m_hat / (jnp.sqrt(v_hat) + ADAM_EPS) + ADAM_WD * w)
    return delta, m, v


WEIGHTS = ["meta_tokens", "ln1_g", "w_in", "q_a_norm_g", "w_uq", "kv_a_norm_g", "w_ukv", "q_norm_g", "k_norm_g",
           "conv_w", "conv_b", "lru_wa", "lru_ba", "lru_wi", "lru_bi", "lru_lambda", "attn_out_g", "rnn_out_g",
           "w_out", "ln2_g", "w_gate", "w_up", "w_down"]
BIG = ["w_in", "w_uq", "w_ukv", "w_out", "w_gate", "w_up", "w_down"]
BIG_T = {"w_in": True, "w_uq": True, "w_ukv": True, "w_out": False, "w_gate": True, "w_up": True, "w_down": False}
BIG_ROWS = {"w_in": 424, "w_uq": 72, "w_ukv": 64, "w_out": 256, "w_gate": 704, "w_up": 704, "w_down": 704}
EARLY = ["w_in", "w_uq", "w_ukv"]
LATE = ["w_out", "w_gate", "w_up", "w_down"]
EARLY_ROWS = 576
SMALL_SHARDED = ["meta_tokens", "conv_w", "lru_ba", "lru_bi", "lru_lambda"]
SMALL = [n for n in WEIGHTS if n not in BIG]
SMALL_PACK_ROWS = 160


def _offsets(names):
    off, o = {}, 0
    for n in names:
        off[n] = o
        o += BIG_ROWS[n]
    return off


def _shard_pack(names, src, rows):
    parts = [_to_pack_piece(n, src[n]) for n in names]
    used = sum(BIG_ROWS[n] for n in names)
    if rows > used:
        parts.append(jnp.zeros((rows - used, D), F32))
    return jnp.concatenate(parts, axis=0)


def _grad_pack(names, g, rows):
    parts = [g[n].reshape(N_CHIPS, BIG_ROWS[n], D) for n in names]
    used = sum(BIG_ROWS[n] for n in names)
    if rows > used:
        parts.append(jnp.zeros((N_CHIPS, rows - used, D), F32))
    return jnp.concatenate(parts, axis=1).reshape(N_CHIPS, 2, rows // 2, D)


def _to_pack_piece(name, shard):
    a = shard[0].T if BIG_T[name] else shard[0]
    return a.reshape(BIG_ROWS[name], D)


def _flat_pack(arrs, rows):
    flat = jnp.concatenate([a.reshape(-1) for a in arrs])
    return jnp.pad(flat, (0, rows * D - flat.shape[0])).reshape(rows, D)


def _flat_unpack(pack, shapes):
    flat, out, o = pack.reshape(-1), [], 0
    for s in shapes:
        n = math.prod(s)
        out.append(flat[o:o + n].reshape(s))
        o += n
    return out


def kernel(x, meta_tokens, ln1_g, w_in, q_a_norm_g, w_uq, kv_a_norm_g, w_ukv, q_norm_g, k_norm_g, conv_w, conv_b, lru_wa, lru_ba, lru_wi, lru_bi, lru_lambda, attn_out_g, rnn_out_g, w_out, ln2_g, w_gate, w_up, w_down, loss_target, m_meta_tokens, m_ln1_g, m_w_in, m_q_a_norm_g, m_w_uq, m_kv_a_norm_g, m_w_ukv, m_q_norm_g, m_k_norm_g, m_conv_w, m_conv_b, m_lru_wa, m_lru_ba, m_lru_wi, m_lru_bi, m_lru_lambda, m_attn_out_g, m_rnn_out_g, m_w_out, m_ln2_g, m_w_gate, m_w_up, m_w_down, v_meta_tokens, v_ln1_g, v_w_in, v_q_a_norm_g, v_w_uq, v_kv_a_norm_g, v_w_ukv, v_q_norm_g, v_k_norm_g, v_conv_w, v_conv_b, v_lru_wa, v_lru_ba, v_lru_wi, v_lru_bi, v_lru_lambda, v_attn_out_g, v_rnn_out_g, v_w_out, v_ln2_g, v_w_gate, v_w_up, v_w_down):
    wts = dict(zip(WEIGHTS, (meta_tokens, ln1_g, w_in, q_a_norm_g, w_uq, kv_a_norm_g, w_ukv, q_norm_g, k_norm_g, conv_w, conv_b, lru_wa, lru_ba, lru_wi, lru_bi, lru_lambda, attn_out_g, rnn_out_g, w_out, ln2_g, w_gate, w_up, w_down)))
    mom = dict(zip(WEIGHTS, (m_meta_tokens, m_ln1_g, m_w_in, m_q_a_norm_g, m_w_uq, m_kv_a_norm_g, m_w_ukv, m_q_norm_g, m_k_norm_g, m_conv_w, m_conv_b, m_lru_wa, m_lru_ba, m_lru_wi, m_lru_bi, m_lru_lambda, m_attn_out_g, m_rnn_out_g, m_w_out, m_ln2_g, m_w_gate, m_w_up, m_w_down)))
    var = dict(zip(WEIGHTS, (v_meta_tokens, v_ln1_g, v_w_in, v_q_a_norm_g, v_w_uq, v_kv_a_norm_g, v_w_ukv, v_q_norm_g, v_k_norm_g, v_conv_w, v_conv_b, v_lru_wa, v_lru_ba, v_lru_wi, v_lru_bi, v_lru_lambda, v_attn_out_g, v_rnn_out_g, v_w_out, v_ln2_g, v_w_gate, v_w_up, v_w_down)))
    xi, yi, ci = _place()
    chip = 2 * xi + yi
    off_e = _offsets(EARLY)
    half_e = EARLY_ROWS // 2
    gather_plan = _to_chips(lambda ref, tx, ty, c: ref.at[pl.ds(c * (ref.shape[0] // 2), ref.shape[0] // 2), :],
                            lambda ref, j, px, py, c: ref.at[pl.ds((4 * px + 2 * py + c) * (ref.shape[0] // 8),
                                                                   ref.shape[0] // 8), :])
    scatter_plan = _to_chips(lambda ref, tx, ty, c: ref.at[2 * tx + ty], lambda ref, j, px, py, c: ref.at[j])
    everywhere = _to_chips(lambda ref, tx, ty, c: ref, lambda ref, j, px, py, c: ref.at[j])
    n_late = len(LATE)

    pack_e = _shard_pack(EARLY, wts, EARLY_ROWS).astype(BF)
    spack = jnp.concatenate([meta_tokens[:, :LANES], meta_tokens[:, LANES:], conv_w[0], lru_ba[0], lru_bi[0],
                             lru_lambda[0], jnp.zeros((6, LANES), F32)], axis=0)
    sems_g, src_g, land_g, _ = _split_start(
        "gather_early_start", [pack_e, spack], [lax.empty((N_CHIPS * EARLY_ROWS, D), BF), lax.empty((N_CHIPS * 48, LANES), F32)],
        gather_plan, 6)
    tgt_padded = _pad_target(loss_target)
    pieces_l = [_to_pack_piece(n, wts[n]).astype(BF) for n in LATE]
    lands_l = list(_place_own(pieces_l))
    src_g, land_g = _split_wait("gather_early_wait", sems_g, src_g, land_g, [tgt_padded] + lands_l, gather_plan)
    ge, gs = _gather_finish(land_g, src_g, "gather_early_finish")
    ge = ge.reshape(N_CHIPS, EARLY_ROWS, D)
    gs = gs.reshape(N_CHIPS, 48, LANES)
    full = {n: ge[:, off_e[n]:off_e[n] + BIG_ROWS[n]] for n in EARLY}
    sems_l, src_l, land_l, tied = _split_start("gather_late_start", pieces_l, lands_l, gather_plan, 3 * n_late, after=[ge])

    forward, pair, late = {}, {}, {}

    def late_forward(after):
        _, landed = _split_wait("gather_late_wait", sems_l, src_l, land_l, after, gather_plan)
        forward["sems"], forward["src"], _, zeros = _split_start(
            "gather_late_forward_start", landed, [], _forward_landed, 3 * n_late)
        return zeros[0, 0]

    def late_weights(after):
        (w_out_, w_gate_, w_up_, w_down_), _ = _split_wait(
            "gather_late_forward_wait", forward["sems"], forward["src"], [], after, _forward_landed)
        return dict(w_out=w_out_, w_gate_t=w_gate_, w_up_t=w_up_, w_down=w_down_)

    def early_grads(g_late):
        halves = [g_late[n].reshape(N_CHIPS, 2, BIG_ROWS[n] // 2, D) for n in LATE]
        pair["sems"], pair["src"], pair["land"], zeros = _split_start(
            "grad_pair_late_start", halves, [lax.empty((N_CHIPS, BIG_ROWS[n] // 2, D), F32) for n in LATE], _to_sibling,
            N_CHIPS * n_late)
        return zeros[0, 0]

    def mid_grads(after):
        halves, landed = _split_wait("grad_pair_late_wait", pair["sems"], pair["src"], pair["land"], after, _to_sibling)
        chip_sums = _pair_sums(halves, landed, ci, "grad_pair_sum_late")
        late["sems"], late["src"], late["land"], zeros = _split_start(
            "grad_chip_late_start", chip_sums, [lax.empty((3, BIG_ROWS[n] // 2, D), BF) for n in LATE], scatter_plan,
            3 * n_late)
        return zeros[0, 0]

    cols = lambda a: a.transpose(1, 0, 2).reshape(a.shape[1], N_CHIPS * a.shape[2])
    meta_full = cols(jnp.concatenate([gs[:, 0:16], gs[:, 16:32]], axis=2))
    w = dict(
        w_in_t=full["w_in"].reshape(IN_COLS, D), w_uq_t=full["w_uq"].reshape(N_HEADS * QK_HEAD, Q_LORA),
        w_ukv_t=full["w_ukv"].reshape(2 * D_ATTN, KV_LORA),
        ln1_g=ln1_g, q_a_norm_g=q_a_norm_g, kv_a_norm_g=kv_a_norm_g, q_norm_g=q_norm_g, k_norm_g=k_norm_g,
        conv_w=cols(gs[:, 32:36]), conv_b=conv_b, lru_wa=lru_wa[0], lru_ba=cols(gs[:, 36:38]), lru_wi=lru_wi[0],
        lru_bi=cols(gs[:, 38:40]), lru_lambda=cols(gs[:, 40:42]), attn_out_g=attn_out_g, rnn_out_g=rnn_out_g,
        ln2_g=ln2_g,
    )

    loss_local, grad_x, g, last = _local_step(x, tgt_padded, meta_full + tied[0, 0], w, late_forward, late_weights,
                                              early_grads, mid_grads)

    gpack = _grad_pack(EARLY, {"w_in": g["w_in_t"], "w_uq": g["w_uq_t"], "w_ukv": g["w_ukv_t"]}, EARLY_ROWS)
    full_shapes = {n: wts[n].shape for n in SMALL}
    full_shapes.update(meta_tokens=(N_META, D), conv_w=(1, CONV_W, D_RNN), lru_ba=(1, 2, D_RNN), lru_bi=(1, 2, D_RNN),
                       lru_lambda=(1, 2, D_RNN))
    gsmall = _flat_pack([g[n] for n in SMALL] + [loss_local], SMALL_PACK_ROWS)
    rbig, rsmall = _pair_exchange(gpack, [gsmall], "grad_pair_exchange")
    chip_big = _pair_sums([gpack], [rbig], ci, "grad_pair_sum")
    (chip_small,) = _elementwise(_add2, 1, "grad_pair_sum_small", gsmall, rsmall)
    early_plan = lambda srcs, lands, sending: (scatter_plan(srcs[:1], lands[:1], sending)
                                               + everywhere(srcs[1:], lands[1:], sending))
    sems_e, src_e, land_e, zero_e = _split_start(
        "grad_chip_early_start", list(chip_big) + [chip_small],
        [lax.empty((3, half_e, D), BF), lax.empty((3, SMALL_PACK_ROWS, D), F32)], early_plan, 6)

    grads, delta, new_m, new_v = {}, {}, {}, {}

    def adamw_big(names, gshards):
        as_rows = lambda n, a: a[0].T if BIG_T[n] else a[0]
        back = lambda n, a: a.T[None] if BIG_T[n] else a[None]
        ws, ms, vs = ([as_rows(n, src[n]) for n in names] for src in (wts, mom, var))
        g2 = [gs.reshape(w.shape) for w, gs in zip(ws, gshards)]
        outs = _adamw_rows(ws, g2, ms, vs, "adamw_" + names[0])
        for i, n in enumerate(names):
            grads[n] = back(n, g2[i])
            delta[n], new_m[n], new_v[n] = (back(n, outs[j * len(names) + i]) for j in range(3))
        return outs[0]

    sums, landed = _split_wait("grad_chip_late_wait", late["sems"], late["src"], late["land"], last + [zero_e], scatter_plan)
    shards_l = _pair_fill(_chip_sums(sums, landed, chip, ci, "grad_chip_sum_late"), "grad_pair_fill_late")
    done_late = adamw_big(LATE, shards_l)
    (src_e, src_s), (land_e, land_s) = _split_wait("grad_chip_early_wait", sems_e, src_e, land_e, [done_late], early_plan)
    (shard_e,) = _pair_fill(_chip_sums([src_e], [land_e], chip, ci, "grad_chip_sum"), "grad_pair_fill_early")
    shard_e = shard_e.reshape(EARLY_ROWS, D)
    for n in EARLY:
        adamw_big([n], [shard_e[off_e[n]:off_e[n] + BIG_ROWS[n]]])
    (small_sum,) = _elementwise(_add4, 1, "grad_chip_sum_small", src_s, land_s[0], land_s[1], land_s[2])
    *small_grads, loss = _flat_unpack(small_sum, [full_shapes[n] for n in SMALL] + [()])
    small_full = dict(zip(SMALL, small_grads))
    for n in SMALL:
        a = small_full[n]
        if n in SMALL_SHARDED:
            width = wts[n].shape[-1]
            a = lax.dynamic_slice_in_dim(a, chip * width, width, axis=a.ndim - 1)
        grads[n] = a.reshape(wts[n].shape)

    rows_of = lambda a: a.reshape(-1, a.shape[-1])
    outs = _adamw_small(*[[rows_of(src[n]) for n in SMALL] for src in (wts, grads, mom, var)])
    for j, dst in enumerate((delta, new_m, new_v)):
        dst.update({n: outs[j * len(SMALL) + i].reshape(wts[n].shape) for i, n in enumerate(SMALL)})

    return (loss, grad_x, *[grads[n] for n in WEIGHTS], *[delta[n] for n in WEIGHTS],
            *[new_m[n] for n in WEIGHTS], *[new_v[n] for n in WEIGHTS])
```

```python
import math

import jax
import jax.numpy as jnp
from jax import lax
from jax.experimental import pallas as pl
from jax.experimental.pallas import tpu as pltpu

F32 = jnp.float32
BF = jnp.bfloat16
MESH = pl.DeviceIdType.MESH

D = 1024
SEQ = 2048
N_META = 16
T = N_META + SEQ
N_HEADS = 8
QK_NOPE = 64
QK_ROPE = 32
QK_HEAD = 96
V_HEAD = 64
Q_LORA = 384
KV_LORA = 256
D_ATTN = 512
D_RNN = 512
RNN_BW = 64
CONV_W = 4
LRU_C = 8.0
ROPE_THETA = 10000.0
D_FF = 2816
EPS = 1e-6
IN_COLS = 1696
ADAM_LR, ADAM_B1, ADAM_B2, ADAM_EPS, ADAM_WD, ADAM_STEP = 0.001, 0.9, 0.999, 1e-08, 0.01, 10

LANES = 128
TP = 2176
NB = 2
R = NB * TP
TR = 256
TRF = 256
TQ = 1088
HP = LANES
PC = 1792
O_CKV, O_KR, O_XR, O_XG = 384, 640, 768, 1280
CG = 128
N_CG = D_RNN // CG
VMEM_LIMIT = 56 * 1024 * 1024
N_CHIPS = 4
SCALE = QK_HEAD ** -0.5
KEY_MASK = -30000.0
LOG2_E = 1.4426950408889634
SCALE_LOG2 = SCALE * LOG2_E


def _nt(a, b):
    return lax.dot_general(a, b, (((1,), (1,)), ((), ())), preferred_element_type=F32)


def _nn(a, b):
    return jnp.dot(a, b, preferred_element_type=F32)


def _tn(a, b):
    return lax.dot_general(a, b, (((0,), (0,)), ((), ())), preferred_element_type=F32)


def _rms(x, g, n):
    ms = jnp.sum(x * x, axis=-1, keepdims=True) * (1.0 / n)
    return x * lax.rsqrt(ms + EPS) * g


def _lane_sum(y):
    return jnp.sum(y, axis=-1, keepdims=True)


def _rot(x):
    lane = lax.broadcasted_iota(jnp.int32, x.shape, 1)
    left = pltpu.roll(x, HP - 16, 1)
    right = pltpu.roll(x, 16, 1)
    lo = (lane >= QK_NOPE) & (lane < QK_NOPE + 16)
    hi = (lane >= QK_NOPE + 16) & (lane < QK_HEAD)
    return jnp.where(lo, -left, jnp.where(hi, right, 0.0))


def _head(x, g, cs, sn):
    n = x * lax.rsqrt(_lane_sum(x * x) * (1.0 / QK_HEAD) + EPS) * g
    return n * cs + _rot(n) * sn


def _head_bwd(x, g, cs, sn, dout):
    rs = lax.rsqrt(_lane_sum(x * x) * (1.0 / QK_HEAD) + EPS)
    xh = x * rs
    dn = dout * cs - _rot(dout * sn)
    gdn = g * dn
    t = _lane_sum(gdn * xh) * (1.0 / QK_HEAD)
    return rs * (gdn - xh * t), jnp.sum(dn * xh, axis=0, keepdims=True)


def _const_spec(shape):
    return pl.BlockSpec(shape, lambda *_: (0,) * len(shape), pipeline_mode=pl.Buffered(1))


def _row_spec(n, tr=TR):
    return pl.BlockSpec((tr, n), lambda i: (i, 0))


def _params(*sem, vmem=VMEM_LIMIT):
    return pltpu.CompilerParams(dimension_semantics=sem, vmem_limit_bytes=vmem)


def _stage_a_fwd(hp, cs, sn, cw):
    def body(hp_ref, cs_ref, sn_ref, ln1, win, qag, wq, kvag, wk, wv, qg, kg,
             pa_ref, xr_ref, xg_ref, q_ref, k_ref, v_ref):
        hn = _rms(hp_ref[...], ln1[...], D).astype(BF)
        p = _nt(hn, win[...])
        pa_ref[...] = p[:, :O_XR]
        xr_ref[...] = p[:, O_XR:O_XG]
        xg_ref[...] = p[:, O_XG:]
        cqn = _rms(p[:, :O_CKV], qag[...], Q_LORA).astype(BF)
        ckvn = _rms(p[:, O_CKV:O_KR], kvag[...], KV_LORA).astype(BF)
        kr = p[:, O_KR:O_XR]
        c, s = cs_ref[...], sn_ref[...]
        mask_lane = lax.broadcasted_iota(jnp.int32, (1, HP), 1) == QK_HEAD
        row = pl.program_id(0) * TRF + lax.broadcasted_iota(jnp.int32, (TRF, 1), 0)
        key_mask = jnp.where(jnp.where(row >= TP, row - TP, row) < T, 0.0, KEY_MASK)
        qraw = _nt(cqn, wq[...])
        kraw = _nt(ckvn, wk[...])
        for h in range(N_HEADS):
            sl = slice(h * HP, (h + 1) * HP)
            q_ref[:, sl] = jnp.where(mask_lane, 1.0, _head(qraw[:, sl], qg[...], c, s)).astype(BF)
            k_ref[:, sl] = jnp.where(mask_lane, key_mask, _head(kraw[:, sl] + kr, kg[...], c, s)).astype(BF)
        v_ref[...] = _nt(ckvn, wv[...]).astype(BF)

    rs = lambda n: _row_spec(n, TRF)
    return pl.pallas_call(
        body, grid=(R // TRF,), name="stage_a_fwd",
        in_specs=[rs(D), rs(HP), rs(HP), _const_spec((1, D)), _const_spec((PC, D)),
                  _const_spec((1, Q_LORA)), _const_spec((N_HEADS * HP, Q_LORA)), _const_spec((1, KV_LORA)),
                  _const_spec((N_HEADS * HP, KV_LORA)), _const_spec((D_ATTN, KV_LORA)), _const_spec((1, HP)),
                  _const_spec((1, HP))],
        out_specs=[rs(O_XR), rs(D_RNN), rs(D_RNN), rs(N_HEADS * HP), rs(N_HEADS * HP), rs(D_ATTN)],
        out_shape=[jax.ShapeDtypeStruct((R, O_XR), F32), jax.ShapeDtypeStruct((R, D_RNN), F32),
                   jax.ShapeDtypeStruct((R, D_RNN), F32), jax.ShapeDtypeStruct((R, N_HEADS * HP), BF),
                   jax.ShapeDtypeStruct((R, N_HEADS * HP), BF), jax.ShapeDtypeStruct((R, D_ATTN), BF)],
        compiler_params=_params("arbitrary"),
    )(hp, cs, sn, cw["ln1_g"], cw["win"], cw["qa_g"], cw["wq"], cw["kva_g"], cw["wk"], cw["wv"], cw["q_g"], cw["k_g"])


def _stage_a_bwd(dq, dk, dv, dxr, dxg, dh1, hp, pa, cs, sn, cw):
    def body(dq_ref, dk_ref, dv_ref, dxr_ref, dxg_ref, dh1_ref, hp_ref, pa_ref, cs_ref, sn_ref,
             ln1, win, qag, wq, kvag, wk, wv, qg, kg,
             dhp_ref, dp_ref, dqraw_ref, dkraw_ref, hn_ref, cqn_ref, ckvn_ref,
             dln1_ref, dqag_ref, dkvag_ref, dqg_ref, dkg_ref):
        @pl.when(pl.program_id(0) == 0)
        def _():
            for r in (dln1_ref, dqag_ref, dkvag_ref, dqg_ref, dkg_ref):
                r[...] = jnp.zeros_like(r)

        hn, vjp_ln1 = jax.vjp(lambda h, g: _rms(h, g, D), hp_ref[...], ln1[...])
        hn_ref[...] = hn.astype(BF)
        pa_v = pa_ref[...]
        cqn, vjp_qa = jax.vjp(lambda x, g: _rms(x, g, Q_LORA), pa_v[:, :O_CKV], qag[...])
        ckvn, vjp_kva = jax.vjp(lambda x, g: _rms(x, g, KV_LORA), pa_v[:, O_CKV:O_KR], kvag[...])
        kr = pa_v[:, O_KR:O_XR]
        cqnb, ckvnb = cqn.astype(BF), ckvn.astype(BF)
        cqn_ref[...] = cqnb
        ckvn_ref[...] = ckvnb
        c, s = cs_ref[...], sn_ref[...]
        lane = lax.broadcasted_iota(jnp.int32, (1, HP), 1)
        rope_lanes = ((lane >= QK_NOPE) & (lane < QK_HEAD)).astype(F32)
        dkr = jnp.zeros((TR, HP), F32)
        dqg = jnp.zeros((1, HP), F32)
        dkg = jnp.zeros((1, HP), F32)
        qraw = _nt(cqnb, wq[...])
        kraw = _nt(ckvnb, wk[...])
        for h in range(N_HEADS):
            sl = slice(h * HP, (h + 1) * HP)
            dqraw, dg = _head_bwd(qraw[:, sl], qg[...], c, s, dq_ref[:, sl])
            dqg = dqg + dg
            dqraw_ref[:, sl] = dqraw.astype(BF)
            dkraw, dg = _head_bwd(kraw[:, sl] + kr, kg[...], c, s, dk_ref[:, sl])
            dkg = dkg + dg
            dkraw_ref[:, sl] = dkraw.astype(BF)
            dkr = dkr + dkraw * rope_lanes
        dcq, dqag = vjp_qa(_nn(dqraw_ref[...], wq[...]))
        dckv, dkvag = vjp_kva(_nn(dkraw_ref[...], wk[...]) + _nn(dv_ref[...].astype(BF), wv[...]))
        dpb = jnp.concatenate([dcq, dckv, dkr, dxr_ref[...], dxg_ref[...]], axis=1).astype(BF)
        dp_ref[...] = dpb
        dh, dln1 = vjp_ln1(_nn(dpb, win[...]))
        dhp_ref[...] = dh + dh1_ref[...]
        dln1_ref[...] += dln1
        dqag_ref[...] += dqag
        dkvag_ref[...] += dkvag
        dqg_ref[...] += dqg
        dkg_ref[...] += dkg

    acc = lambda n: pl.BlockSpec((1, n), lambda i: (0, 0))
    return pl.pallas_call(
        body, grid=(R // TR,), name="stage_a_bwd",
        in_specs=[_row_spec(N_HEADS * HP), _row_spec(N_HEADS * HP), _row_spec(D_ATTN), _row_spec(D_RNN),
                  _row_spec(D_RNN), _row_spec(D), _row_spec(D), _row_spec(O_XR), _row_spec(HP), _row_spec(HP),
                  _const_spec((1, D)), _const_spec((PC, D)), _const_spec((1, Q_LORA)),
                  _const_spec((N_HEADS * HP, Q_LORA)), _const_spec((1, KV_LORA)),
                  _const_spec((N_HEADS * HP, KV_LORA)), _const_spec((D_ATTN, KV_LORA)), _const_spec((1, HP)),
                  _const_spec((1, HP))],
        out_specs=[_row_spec(D), _row_spec(PC), _row_spec(N_HEADS * HP), _row_spec(N_HEADS * HP), _row_spec(D),
                   _row_spec(Q_LORA), _row_spec(KV_LORA), acc(D), acc(Q_LORA), acc(KV_LORA), acc(HP), acc(HP)],
        out_shape=[jax.ShapeDtypeStruct((R, D), F32), jax.ShapeDtypeStruct((R, PC), BF),
                   jax.ShapeDtypeStruct((R, N_HEADS * HP), BF), jax.ShapeDtypeStruct((R, N_HEADS * HP), BF),
                   jax.ShapeDtypeStruct((R, D), BF), jax.ShapeDtypeStruct((R, Q_LORA), BF),
                   jax.ShapeDtypeStruct((R, KV_LORA), BF), jax.ShapeDtypeStruct((1, D), F32),
                   jax.ShapeDtypeStruct((1, Q_LORA), F32), jax.ShapeDtypeStruct((1, KV_LORA), F32),
                   jax.ShapeDtypeStruct((1, HP), F32), jax.ShapeDtypeStruct((1, HP), F32)],
        compiler_params=_params("arbitrary"),
    )(dq, dk, dv, dxr, dxg, dh1, hp, pa, cs, sn, cw["ln1_g"], cw["win"], cw["qa_g"], cw["wq"], cw["kva_g"],
      cw["wk"], cw["wv"], cw["q_g"], cw["k_g"])


def _head_mask(half, dtype):
    lane = lax.broadcasted_iota(jnp.int32, (1, 2 * V_HEAD), 1)
    return ((lane >= V_HEAD) == (half == 1)).astype(dtype)


def _attn_specs(tq):
    n_q = TP // tq
    return (NB, N_HEADS // 2, n_q), dict(
        q=pl.BlockSpec((tq, 2 * HP), lambda b, j, i: (b * n_q + i, j)),
        k=pl.BlockSpec((TP, 2 * HP), lambda b, j, i: (b, j)),
        v=pl.BlockSpec((TP, 2 * V_HEAD), lambda b, j, i: (b, j)),
        o=pl.BlockSpec((tq, 2 * V_HEAD), lambda b, j, i: (b * n_q + i, j)),
        lse=pl.BlockSpec((None, tq, 2), lambda b, j, i: (j, b * n_q + i, 0)))


TQF = 1088


def _attn_fwd(q, k, v):
    def body(q_ref, k_ref, v_ref, o_ref, lse_ref):
        v2 = v_ref[...]
        o = jnp.zeros((TQF, 2 * V_HEAD), F32)
        lse = []
        for hh in range(2):
            sl = slice(hh * HP, (hh + 1) * HP)
            raw = _nt(q_ref[:, sl], k_ref[:, sl])
            m = jnp.max(raw, axis=-1, keepdims=True)
            e = jnp.exp2((raw - m) * SCALE_LOG2)
            l = jnp.sum(e, axis=-1, keepdims=True)
            o = o + _nn(e.astype(BF), v2 * _head_mask(hh, BF)) * (1.0 / l)
            lse.append(m * SCALE_LOG2 + jnp.log(l) * LOG2_E)
        o_ref[...] = o
        lane = lax.broadcasted_iota(jnp.int32, (TQF, 2), 1)
        lse_ref[...] = jnp.where(lane == 0, lse[0], lse[1])

    grid, sp = _attn_specs(TQF)
    return pl.pallas_call(
        body, grid=grid, name="attn_fwd", in_specs=[sp["q"], sp["k"], sp["v"]], out_specs=[sp["o"], sp["lse"]],
        out_shape=[jax.ShapeDtypeStruct((R, D_ATTN), F32), jax.ShapeDtypeStruct((N_HEADS // 2, R, 2), F32)],
        compiler_params=_params("arbitrary", "arbitrary", "arbitrary"),
    )(q, k, v)


def _attn_bwd(q, k, v, o, lse, do):
    def body(q_ref, k_ref, v_ref, o_ref, lse_ref, do_ref, dq_ref, dk_ref, dv_ref):
        @pl.when(pl.program_id(2) == 0)
        def _():
            dk_ref[...] = jnp.zeros_like(dk_ref)
            dv_ref[...] = jnp.zeros_like(dv_ref)

        do = do_ref[...]
        dob = do.astype(BF)
        do_o = do * o_ref[...]
        v2 = v_ref[...]
        dv_sum = jnp.zeros((TP, 2 * V_HEAD), F32)
        for hh in range(2):
            sl = slice(hh * HP, (hh + 1) * HP)
            qb, kb = q_ref[:, sl], k_ref[:, sl]
            p = jnp.exp2(_nt(qb, kb) * SCALE_LOG2 - lse_ref[:, hh:hh + 1])
            dp = _nt(dob, v2 * _head_mask(hh, BF))
            delta = jnp.sum(do_o * _head_mask(hh, F32), axis=-1, keepdims=True)
            dsb = (p * (dp - delta) * SCALE).astype(BF)
            dq_ref[:, sl] = _nn(dsb, kb)
            dk_ref[:, sl] += _tn(dsb, qb)
            dv_sum = dv_sum + _tn(p.astype(BF), dob) * _head_mask(hh, F32)
        dv_ref[...] += dv_sum

    grid, sp = _attn_specs(TQ)
    return pl.pallas_call(
        body, grid=grid, name="attn_bwd", in_specs=[sp["q"], sp["k"], sp["v"], sp["o"], sp["lse"], sp["o"]],
        out_specs=[sp["q"], sp["k"], sp["v"]],
        out_shape=[jax.ShapeDtypeStruct((R, N_HEADS * HP), F32), jax.ShapeDtypeStruct((R, N_HEADS * HP), F32),
                   jax.ShapeDtypeStruct((R, D_ATTN), F32)],
        compiler_params=_params("arbitrary", "arbitrary", "arbitrary"),
    )(q, k, v, o, lse, do)


SEG = TP // 8


def _scan_pair(af_ref, bf_ref, hf_ref, ab_ref, bb_ref, hb_ref, pf_ref, pb_ref):
    unroll = 8

    def step(i, carry):
        hf, pf, hb, pb = carry
        for u in range(unroll):
            j = i * unroll + u
            rows_f, rows_b = pl.ds(j, 8, stride=SEG), pl.ds(SEG - 1 - j, 8, stride=SEG)
            a = af_ref[rows_f, :]
            hf, pf = a * hf + bf_ref[rows_f, :], a * pf
            hf_ref[rows_f, :] = hf
            pf_ref[rows_f, :] = pf
            a = ab_ref[rows_b, :]
            hb, pb = a * hb + bb_ref[rows_b, :], a * pb
            hb_ref[rows_b, :] = hb
            pb_ref[rows_b, :] = pb
        return hf, pf, hb, pb

    zero, one = jnp.zeros((8, CG), F32), jnp.ones((8, CG), F32)
    hf, pf, hb, pb = lax.fori_loop(0, SEG // unroll, step, (zero, one, zero, one))
    seg = lax.broadcasted_iota(jnp.int32, (8, CG), 0)
    cf, cb = zero, zero
    for s in range(1, 8):
        cf = jnp.where(seg == s, pltpu.roll(hf + pf * cf, 1, 0), cf)
        cb = jnp.where(seg == 7 - s, pltpu.roll(hb + pb * cb, 7, 0), cb)
    for s in range(8):
        rows = slice(s * SEG, (s + 1) * SEG)
        hf_ref[rows, :] = hf_ref[rows, :] + pf_ref[rows, :] * cf[s:s + 1, :]
        hb_ref[rows, :] = hb_ref[rows, :] + pb_ref[rows, :] * cb[s:s + 1, :]


def _shifts(x):
    t = lax.broadcasted_iota(jnp.int32, x.shape, 0)
    xm2 = jnp.where(t >= 2, pltpu.roll(x, 2, 0), 0.0)
    xm1 = jnp.where(t >= 1, pltpu.roll(x, 1, 0), 0.0)
    xp1 = jnp.where(t < TP - 1, pltpu.roll(x, TP - 1, 0), 0.0)
    return xm2, xm1, xp1


def _softplus(z):
    e = jnp.exp(-jnp.abs(z))
    small = e * (1.0 - e * (0.5 - e * (1.0 / 3.0)))
    return jnp.maximum(z, 0.0) + jnp.where(e < 0.01, small, jnp.log(1.0 + e))


def _sigmoid(x):
    return 0.5 * jnp.tanh(0.5 * x) + 0.5


def _one_minus_sq(log_a, a):
    x = 2.0 * log_a
    series = -x * (1.0 + x * 0.5 * (1.0 + x * (1.0 / 3.0) * (1.0 + x * 0.25)))
    return jnp.where(x > -0.05, series, 1.0 - a * a)


def _gates(row0, xc, pa_f, pi_f, pa_b, pi_b, lam_f, lam_b):
    t = row0 + lax.broadcasted_iota(jnp.int32, xc.shape, 0)
    valid = t < T
    out = []
    for pa, pi_, lam in ((pa_f, pi_f, lam_f), (pa_b, pi_b, lam_b)):
        r = _sigmoid(pa)
        gate_i = _sigmoid(pi_)
        log_a = -LRU_C * r * _softplus(-lam)
        a = jnp.exp(log_a)
        mult = jnp.sqrt(jnp.maximum(_one_minus_sq(log_a, a), 0.0))
        out += [a, jnp.where(valid, mult * (gate_i * xc), 0.0)]
    return tuple(out)


def _gates_bwd(row0, xc, pres, lams, cots):
    t = row0 + lax.broadcasted_iota(jnp.int32, xc.shape, 0)
    valid = t < T
    dxc = jnp.zeros_like(xc)
    dpres, dlams = [], []
    for d in range(2):
        pa, pi_, lam = pres[2 * d], pres[2 * d + 1], lams[d]
        da, db = cots[2 * d], jnp.where(valid, cots[2 * d + 1], 0.0)
        r = _sigmoid(pa)
        gate_i = _sigmoid(pi_)
        sp = _softplus(-lam)
        log_a = -LRU_C * r * sp
        a = jnp.exp(log_a)
        m2 = jnp.maximum(_one_minus_sq(log_a, a), 0.0)
        mult = jnp.sqrt(m2)
        dxc = dxc + db * (mult * gate_i)
        d_gate = db * (mult * xc)
        d_m2 = jnp.where(m2 > 0.0, db * (gate_i * xc) * (0.5 * lax.rsqrt(m2)), 0.0)
        d_log_a = da * a - 2.0 * d_m2 * (a * a)
        dpres += [d_log_a * (-LRU_C * sp) * (r * (1.0 - r)), d_gate * (gate_i * (1.0 - gate_i))]
        d_sp = jnp.sum(d_log_a * (-LRU_C * r), axis=0, keepdims=True)
        dlams.append(-d_sp * jax.nn.sigmoid(-lam))
    return dxc, dpres, dlams


def _rnn_specs():
    seq = pl.BlockSpec((TP, CG), lambda g, b: (b, g))
    return dict(
        seq=seq,
        cw=pl.BlockSpec((CONV_W, CG), lambda g, b: (0, g)),
        cb=pl.BlockSpec((1, CG), lambda g, b: (0, g)),
        w4=pl.BlockSpec((None, CG, 4 * CG), lambda g, b: (g, 0, 0)),
        b4=pl.BlockSpec((None, 1, 4 * CG), lambda g, b: (g, 0, 0)),
        lam=pl.BlockSpec((None, 1, 2 * CG), lambda g, b: (g, 0, 0)),
    )


def _conv(x, xm2, xm1, xp1, cw_ref, cb_ref):
    return cw_ref[0:1, :] * xm2 + cw_ref[1:2, :] * xm1 + cw_ref[2:3, :] * x + cw_ref[3:4, :] * xp1 + cb_ref[...]


TC = 128
N_TC = TP // TC


def _split4(pre):
    return pre[:, :CG], pre[:, CG:2 * CG], pre[:, 2 * CG:3 * CG], pre[:, 3 * CG:]


def _rnn_fwd(xr, xg, cw):
    def body(xr_ref, xg_ref, cw_ref, cb_ref, w4_ref, b4_ref, lam_ref, y_ref, hf_ref, hb_ref, af_ref, ab_ref, xc_ref,
             af, bf, ab, bb, pf, pb):
        x = xr_ref[...]
        xc_ref[...] = _conv(x, *_shifts(x), cw_ref, cb_ref)
        lam = lam_ref[...]

        def chunk(i, _):
            rows = pl.ds(pl.multiple_of(i * TC, TC), TC)
            xc = xc_ref[rows, :]
            pre = _nn(xc.astype(BF), w4_ref[...]) + b4_ref[...]
            a_f, b_f, a_b, b_b = _gates(i * TC, xc, *_split4(pre), lam[:, :CG], lam[:, CG:])
            af[rows, :] = a_f
            bf[rows, :] = b_f
            ab[rows, :] = a_b
            bb[rows, :] = b_b
            af_ref[rows, :] = a_f
            ab_ref[rows, :] = a_b
            return 0

        lax.fori_loop(0, N_TC, chunk, 0)
        _scan_pair(af, bf, hf_ref, ab, bb, hb_ref, pf, pb)
        y_ref[...] = (hf_ref[...] + hb_ref[...]) * jax.nn.gelu(xg_ref[...])

    sp = _rnn_specs()
    return pl.pallas_call(
        body, grid=(N_CG, NB), name="rnn_fwd",
        in_specs=[sp["seq"], sp["seq"], sp["cw"], sp["cb"], sp["w4"], sp["b4"], sp["lam"]],
        out_specs=[sp["seq"]] * 6, out_shape=[jax.ShapeDtypeStruct((R, D_RNN), F32)] * 6,
        scratch_shapes=[pltpu.VMEM((TP, CG), F32)] * 6,
        compiler_params=_params("arbitrary", "arbitrary"),
    )(xr, xg, cw["conv_w"], cw["conv_b"], cw["w4"], cw["b4"], cw["lam"])


def _rnn_bwd(dy, xr, xg, hf, hb, af, ab, xc, cw):
    def body(dy_ref, xr_ref, xg_ref, hf_ref, hb_ref, af_ref, ab_ref, xc_s, cw_ref, cb_ref, w4_ref, b4_ref, lam_ref,
             dxr_ref, dxg_ref, dcw_ref, dcb_ref, dw4_ref, db4_ref, dlam_ref,
             af_s, ab_s, dhs_s, lf_s, lb_s, daf_s, dab_s, dxc_s):
        @pl.when(pl.program_id(1) == 0)
        def _():
            for r in (dcw_ref, dcb_ref, dw4_ref, db4_ref, dlam_ref):
                r[...] = jnp.zeros_like(r)

        lam = lam_ref[...]

        def chunk1(i, _):
            rows = pl.ds(pl.multiple_of(i * TC, TC), TC)
            _, vjp_y = jax.vjp(lambda h, g: h * jax.nn.gelu(g), hf_ref[rows, :] + hb_ref[rows, :], xg_ref[rows, :])
            dhs, dxg = vjp_y(dy_ref[rows, :])
            dhs_s[rows, :] = dhs
            dxg_ref[rows, :] = dxg
            return 0

        lax.fori_loop(0, N_TC, chunk1, 0)
        t = lax.broadcasted_iota(jnp.int32, (TP, CG), 0)
        af_s[...] = pltpu.roll(af_ref[...], TP - 1, 0)
        ab_s[...] = pltpu.roll(ab_ref[...], 1, 0)
        _scan_pair(ab_s, dhs_s, lb_s, af_s, dhs_s, lf_s, dab_s, daf_s)
        daf_s[...] = lf_s[...] * jnp.where(t >= 1, pltpu.roll(hf_ref[...], 1, 0), 0.0)
        dab_s[...] = lb_s[...] * jnp.where(t < TP - 1, pltpu.roll(hb_ref[...], TP - 1, 0), 0.0)

        def chunk2(i, _):
            rows = pl.ds(pl.multiple_of(i * TC, TC), TC)
            xc = xc_s[rows, :]
            xcb = xc.astype(BF)
            pre = _nn(xcb, w4_ref[...]) + b4_ref[...]
            dxc, dpres, dlams = _gates_bwd(i * TC, xc, _split4(pre), (lam[:, :CG], lam[:, CG:]),
                                           (daf_s[rows, :], lf_s[rows, :], dab_s[rows, :], lb_s[rows, :]))
            dpre = jnp.concatenate(dpres, axis=1)
            dpreb = dpre.astype(BF)
            dxc_s[rows, :] = dxc + _nt(dpreb, w4_ref[...])
            dw4_ref[...] += _tn(xcb, dpreb)
            db4_ref[...] += jnp.sum(dpre, axis=0, keepdims=True)
            dlam_ref[...] += jnp.concatenate(dlams, axis=1)
            return 0

        lax.fori_loop(0, N_TC, chunk2, 0)
        dxc = dxc_s[...]
        x = xr_ref[...]
        taps = (jnp.where(t < TP - 2, pltpu.roll(dxc, TP - 2, 0), 0.0), jnp.where(t < TP - 1, pltpu.roll(dxc, TP - 1, 0), 0.0),
                dxc, jnp.where(t >= 1, pltpu.roll(dxc, 1, 0), 0.0))
        dcb_ref[...] += jnp.sum(dxc, axis=0, keepdims=True)
        dxr = jnp.zeros_like(dxc)
        for tap, shifted in enumerate(taps):
            dcw_ref[tap:tap + 1, :] += jnp.sum(x * shifted, axis=0, keepdims=True)
            dxr = dxr + cw_ref[tap:tap + 1, :] * shifted
        dxr_ref[...] = dxr

    sp = _rnn_specs()
    return pl.pallas_call(
        body, grid=(N_CG, NB), name="rnn_bwd",
        in_specs=[sp["seq"]] * 8 + [sp["cw"], sp["cb"], sp["w4"], sp["b4"], sp["lam"]],
        out_specs=[sp["seq"], sp["seq"], sp["cw"], sp["cb"], sp["w4"], sp["b4"], sp["lam"]],
        out_shape=[jax.ShapeDtypeStruct((R, D_RNN), F32), jax.ShapeDtypeStruct((R, D_RNN), F32),
                   jax.ShapeDtypeStruct((CONV_W, D_RNN), F32), jax.ShapeDtypeStruct((1, D_RNN), F32),
                   jax.ShapeDtypeStruct((N_CG, CG, 4 * CG), F32), jax.ShapeDtypeStruct((N_CG, 1, 4 * CG), F32),
                   jax.ShapeDtypeStruct((N_CG, 1, 2 * CG), F32)],
        scratch_shapes=[pltpu.VMEM((TP, CG), F32)] * 8,
        compiler_params=_params("arbitrary", "arbitrary"),
    )(dy, xr, xg, hf, hb, af, ab, xc, cw["conv_w"], cw["conv_b"], cw["w4"], cw["b4"], cw["lam"])


TD = 256
STAGE_D_VMEM = 58 * 1024 * 1024


def _stage_d(hp, o, y, tgt, cw):
    def body(hp_ref, o_ref, y_ref, tgt_ref, ga, gr, wout, ln2, wg, wu, wd,
             do_ref, dy_ref, dh1_ref, mix_ref, dh1b_ref, hn2_ref, dg_ref, du_ref, act_ref, dh2b_ref,
             loss_ref, dga_ref, dgr_ref, dln2_ref):
        i = pl.program_id(0)

        @pl.when(i == 0)
        def _():
            for r in (loss_ref, dga_ref, dgr_ref, dln2_ref):
                r[...] = jnp.zeros_like(r)

        mix_a, vjp_a = jax.vjp(lambda x, g: _rms(x, g, D_ATTN), o_ref[...], ga[...])
        mix_r, vjp_r = jax.vjp(lambda x, g: _rms(x, g, D_RNN), y_ref[...], gr[...])
        mab, mrb = mix_a.astype(BF), mix_r.astype(BF)
        mix_ref[:, :D_ATTN] = mab
        mix_ref[:, D_ATTN:] = mrb
        h1 = hp_ref[...] + _nn(mab, wout[:D_ATTN, :]) + _nn(mrb, wout[D_ATTN:, :])
        hn2, vjp_ln2 = jax.vjp(lambda x, g: _rms(x, g, D), h1, ln2[...])
        hn2b = hn2.astype(BF)
        hn2_ref[...] = hn2b
        act, vjp_act = jax.vjp(lambda g, u: jax.nn.silu(g) * u, _nt(hn2b, wg[...]), _nt(hn2b, wu[...]))
        actb = act.astype(BF)
        act_ref[...] = actb
        h2 = h1 + _nn(actb, wd[...])
        row = i * TD + lax.broadcasted_iota(jnp.int32, (TD, 1), 0)
        t = jnp.where(row >= TP, row - TP, row)
        err = jnp.where((t >= N_META) & (t < T), h2 - tgt_ref[...], 0.0)
        loss_ref[...] += jnp.sum(err * err) * (0.5 / D)
        dh2b = (err * (1.0 / D)).astype(BF)
        dh2b_ref[...] = dh2b
        dg, du = vjp_act(_nt(dh2b, wd[...]))
        dgb, dub = dg.astype(BF), du.astype(BF)
        dg_ref[...] = dgb
        du_ref[...] = dub
        dh1n, dln2 = vjp_ln2(_nn(dgb, wg[...]) + _nn(dub, wu[...]))
        dh1 = err * (1.0 / D) + dh1n
        dh1_ref[...] = dh1
        dh1b = dh1.astype(BF)
        dh1b_ref[...] = dh1b
        dmix = _nt(dh1b, wout[...])
        do, dga = vjp_a(dmix[:, :D_ATTN])
        dyr, dgr = vjp_r(dmix[:, D_ATTN:])
        do_ref[...] = do
        dy_ref[...] = dyr
        dga_ref[...] += dga
        dgr_ref[...] += dgr
        dln2_ref[...] += dln2

    rs = lambda n: _row_spec(n, TD)
    acc = lambda n: pl.BlockSpec((1, n), lambda i: (0, 0))
    return pl.pallas_call(
        body, grid=(R // TD,), name="stage_d",
        in_specs=[rs(D), rs(D_ATTN), rs(D_RNN), rs(D), _const_spec((1, D_ATTN)), _const_spec((1, D_RNN)),
                  _const_spec((D, D)), _const_spec((1, D)), _const_spec((D_FF, D)), _const_spec((D_FF, D)),
                  _const_spec((D_FF, D))],
        out_specs=[rs(D_ATTN), rs(D_RNN), rs(D), rs(D), rs(D), rs(D), rs(D_FF), rs(D_FF), rs(D_FF), rs(D),
                   acc(1), acc(D_ATTN), acc(D_RNN), acc(D)],
        out_shape=[jax.ShapeDtypeStruct((R, D_ATTN), F32), jax.ShapeDtypeStruct((R, D_RNN), F32),
                   jax.ShapeDtypeStruct((R, D), F32), jax.ShapeDtypeStruct((R, D), BF),
                   jax.ShapeDtypeStruct((R, D), BF), jax.ShapeDtypeStruct((R, D), BF),
                   jax.ShapeDtypeStruct((R, D_FF), BF), jax.ShapeDtypeStruct((R, D_FF), BF),
                   jax.ShapeDtypeStruct((R, D_FF), BF), jax.ShapeDtypeStruct((R, D), BF),
                   jax.ShapeDtypeStruct((1, 1), F32), jax.ShapeDtypeStruct((1, D_ATTN), F32),
                   jax.ShapeDtypeStruct((1, D_RNN), F32), jax.ShapeDtypeStruct((1, D), F32)],
        compiler_params=_params("arbitrary", vmem=STAGE_D_VMEM),
    )(hp, o, y, tgt, cw["ga"], cw["gr"], cw["wout"], cw["ln2_g"], cw["wg"], cw["wu"], cw["wd"])


TW = 2176


def _wgrad(a, b, name, tk=None):
    ka, nb = a.shape[1], b.shape[1]
    tk = ka if tk is None else tk

    def body(a_ref, b_ref, o_ref):
        @pl.when(pl.program_id(1) == 0)
        def _():
            o_ref[...] = jnp.zeros_like(o_ref)

        o_ref[...] += _tn(a_ref[...].astype(BF), b_ref[...].astype(BF))

    return pl.pallas_call(
        body, grid=(ka // tk, R // TW), name=name,
        in_specs=[pl.BlockSpec((TW, tk), lambda k, r: (r, k)), pl.BlockSpec((TW, nb), lambda k, r: (r, 0))],
        out_specs=pl.BlockSpec((tk, nb), lambda k, r: (k, 0)),
        out_shape=jax.ShapeDtypeStruct((ka, nb), F32),
        compiler_params=_params("arbitrary", "arbitrary"),
    )(a, b)


def _wgrad_heads(dq, dk, dv, cqn, ckvn):
    def body(dq_ref, dk_ref, dv_ref, cqn_ref, ckvn_ref, oq_ref, ok_ref, ov_ref):
        @pl.when(pl.program_id(0) == 0)
        def _():
            for r in (oq_ref, ok_ref, ov_ref):
                r[...] = jnp.zeros_like(r)

        ckvnb = ckvn_ref[...]
        oq_ref[...] += _tn(dq_ref[...], cqn_ref[...])
        ok_ref[...] += _tn(dk_ref[...], ckvnb)
        ov_ref[...] += _tn(dv_ref[...].astype(BF), ckvnb)

    rows = lambda a: pl.BlockSpec((TW, a.shape[1]), lambda r: (r, 0))
    full = lambda m, n: pl.BlockSpec((m, n), lambda r: (0, 0))
    shapes = [(dq.shape[1], cqn.shape[1]), (dk.shape[1], ckvn.shape[1]), (dv.shape[1], ckvn.shape[1])]
    return pl.pallas_call(
        body, grid=(R // TW,), name="wgrad_heads", in_specs=[rows(a) for a in (dq, dk, dv, cqn, ckvn)],
        out_specs=[full(*s) for s in shapes], out_shape=[jax.ShapeDtypeStruct(s, F32) for s in shapes],
        compiler_params=_params("arbitrary"),
    )(dq, dk, dv, cqn, ckvn)


def _rope_tables():
    half = QK_ROPE // 2
    freqs = 1.0 / (ROPE_THETA ** (jnp.arange(half, dtype=F32) / half))
    ang = jnp.arange(TP, dtype=F32)[:, None] * freqs[None, :]
    ones = jnp.ones((TP, QK_NOPE), F32)
    zeros = jnp.zeros((TP, QK_NOPE), F32)
    pad1 = jnp.ones((TP, HP - QK_HEAD), F32)
    pad0 = jnp.zeros((TP, HP - QK_HEAD), F32)
    cs = jnp.concatenate([ones, jnp.cos(ang), jnp.cos(ang), pad1], axis=1)
    sn = jnp.concatenate([zeros, jnp.sin(ang), jnp.sin(ang), pad0], axis=1)
    return jnp.tile(cs, (NB, 1)), jnp.tile(sn, (NB, 1))


def _pad_rows(a, lo, hi):
    return jnp.pad(a, ((0, 0), (lo, hi), (0, 0)))


def _pad_target(target):
    return _pad_rows(target, N_META, TP - T).reshape(R, D)


def _compute_weights(w):
    win_t = w["w_in_t"]
    kr = win_t[O_KR:O_KR + QK_ROPE]
    win = jnp.concatenate([win_t[:O_KR], jnp.zeros((QK_NOPE, D), F32), kr,
                           jnp.zeros((HP - QK_HEAD, D), F32), win_t[O_KR + QK_ROPE:]], axis=0)
    wq = _pad_rows(w["w_uq_t"].reshape(N_HEADS, QK_HEAD, Q_LORA), 0, HP - QK_HEAD)
    wkv = w["w_ukv_t"].reshape(N_HEADS, QK_NOPE + V_HEAD, KV_LORA)
    wk = _pad_rows(wkv[:, :QK_NOPE], 0, HP - QK_NOPE)
    wv = wkv[:, QK_NOPE:].reshape(D_ATTN, KV_LORA)
    gates = jnp.stack([w["lru_wa"][0], w["lru_wi"][0], w["lru_wa"][1], w["lru_wi"][1]])
    blk = gates.reshape(4, N_CG, 2, RNN_BW, RNN_BW)
    dense = jnp.einsum("tcaij,ab->tcaibj", blk, jnp.eye(2, dtype=F32)).reshape(4, N_CG, CG, CG)
    w4 = dense.transpose(1, 2, 0, 3).reshape(N_CG, CG, 4 * CG)
    bias = jnp.stack([w["lru_ba"][0], w["lru_bi"][0], w["lru_ba"][1], w["lru_bi"][1]])
    b4 = bias.reshape(4, N_CG, CG).transpose(1, 0, 2).reshape(N_CG, 1, 4 * CG)
    lam = w["lru_lambda"].reshape(2, N_CG, CG).transpose(1, 0, 2).reshape(N_CG, 1, 2 * CG)
    pad_g = lambda g: jnp.pad(g.reshape(1, QK_HEAD), ((0, 0), (0, HP - QK_HEAD)))
    return dict(
        ln1_g=w["ln1_g"].reshape(1, D), win=win.astype(BF), qa_g=w["q_a_norm_g"].reshape(1, Q_LORA),
        wq=wq.astype(BF).reshape(N_HEADS * HP, Q_LORA), kva_g=w["kv_a_norm_g"].reshape(1, KV_LORA),
        wk=wk.astype(BF).reshape(N_HEADS * HP, KV_LORA), wv=wv.astype(BF),
        q_g=pad_g(w["q_norm_g"]), k_g=pad_g(w["k_norm_g"]),
        conv_w=w["conv_w"].reshape(CONV_W, D_RNN), conv_b=w["conv_b"].reshape(1, D_RNN),
        w4=w4.astype(BF), b4=b4, lam=lam,
        ga=w["attn_out_g"].reshape(1, D_ATTN), gr=w["rnn_out_g"].reshape(1, D_RNN), ln2_g=w["ln2_g"].reshape(1, D),
    )


def _local_step(x, target, meta, w, late_forward, late_weights, early_grads, mid_grads):
    cw = _compute_weights(w)
    cs, sn = _rope_tables()
    hp = jnp.concatenate([jnp.broadcast_to(meta[None], (NB, N_META, D)), x,
                          jnp.zeros((NB, TP - T, D), F32)], axis=1).reshape(R, D)
    tgt = target if target.ndim == 2 else _pad_target(target)

    pa, xr, xg, q, k, v = _stage_a_fwd(hp, cs, sn, cw)
    o, lse = _attn_fwd(q, k, v)
    cw["conv_b"] = cw["conv_b"] + late_forward([o])
    y, hf, hb, af, ab, xc = _rnn_fwd(xr, xg, cw)
    late = late_weights([y])
    cw.update(wout=late["w_out"], wg=late["w_gate_t"], wu=late["w_up_t"], wd=late["w_down"])
    (do, dy, dh1, mixb, dh1b, hn2b, dgb, dub, actb, dh2b, loss, dga, dgr, dln2) = _stage_d(hp, o, y, tgt, cw)
    dwout = _wgrad(mixb, dh1b, "wgrad_out")
    dwg = _wgrad(dgb, hn2b, "wgrad_gate", tk=D_FF // 2)
    dwu = _wgrad(dub, hn2b, "wgrad_up", tk=D_FF // 2)
    dwd = _wgrad(actb, dh2b, "wgrad_down", tk=D_FF // 2)
    zero = early_grads(dict(w_out=dwout, w_gate=dwg, w_up=dwu, w_down=dwd))
    cw["conv_b"] = cw["conv_b"] + zero
    dxr, dxg, dcw, dcb, dw4, db4, dlam = _rnn_bwd(dy, xr, xg, hf, hb, af, ab, xc, cw)
    zero = mid_grads([dxr])
    dq, dk, dv = _attn_bwd(q, k, v, o, lse, do)
    (dhp, dpb, dqrawb, dkrawb, hn1b, cqnb, ckvnb, dln1, dqag, dkvag, dqg, dkg) = _stage_a_bwd(
        dq, dk, dv, dxr, dxg, dh1, hp, pa, cs, sn, dict(cw, qa_g=cw["qa_g"] + zero))

    dwin = _wgrad(dpb, hn1b, "wgrad_in", tk=PC // 2)
    dwq, dwk, dwv = _wgrad_heads(dqrawb, dkrawb, dv, cqnb, ckvnb)

    dwin_t = jnp.concatenate([dwin[:O_KR], dwin[O_KR + QK_NOPE:O_KR + QK_HEAD], dwin[O_XR:]], axis=0)
    dwq_t = dwq.reshape(N_HEADS, HP, Q_LORA)[:, :QK_HEAD].reshape(N_HEADS * QK_HEAD, Q_LORA)
    dwkv_t = jnp.concatenate([dwk.reshape(N_HEADS, HP, KV_LORA)[:, :QK_NOPE],
                              dwv.reshape(N_HEADS, V_HEAD, KV_LORA)], axis=1).reshape(2 * D_ATTN, KV_LORA)
    d4 = dw4.reshape(N_CG, 2, RNN_BW, 4, 2, RNN_BW)
    dgates = jnp.stack([d4[:, 0, :, :, 0, :], d4[:, 1, :, :, 1, :]], axis=1)
    dgates = dgates.transpose(3, 0, 1, 2, 4).reshape(4, N_HEADS, RNN_BW, RNN_BW)
    dbias = db4.reshape(N_CG, 4, CG).transpose(1, 0, 2).reshape(4, D_RNN)
    dhp3 = dhp.reshape(NB, TP, D)
    grads = dict(
        meta_tokens=jnp.sum(dhp3[:, :N_META], axis=0),
        ln1_g=dln1, w_in_t=dwin_t, q_a_norm_g=dqag, w_uq_t=dwq_t, kv_a_norm_g=dkvag, w_ukv_t=dwkv_t,
        q_norm_g=dqg[:, :QK_HEAD], k_norm_g=dkg[:, :QK_HEAD], conv_w=dcw[None], conv_b=dcb,
        lru_wa=jnp.stack([dgates[0], dgates[2]])[None], lru_ba=jnp.stack([dbias[0], dbias[2]])[None],
        lru_wi=jnp.stack([dgates[1], dgates[3]])[None], lru_bi=jnp.stack([dbias[1], dbias[3]])[None],
        lru_lambda=dlam.reshape(N_CG, 2, CG).transpose(1, 0, 2).reshape(1, 2, D_RNN),
        attn_out_g=dga, rnn_out_g=dgr, ln2_g=dln2,
    )
    return loss[0, 0], dhp3[:, N_META:T], grads, [dhp, dwin]


_ANY = pl.BlockSpec(memory_space=pl.ANY)


def _place():
    return lax.axis_index("x"), lax.axis_index("y"), lax.axis_index("c")


def _other_chips(x, y):
    return [(1 - x, y), (x, 1 - y), (1 - x, 1 - y)]


def _pair_exchange(big, whole, name):
    n_s, _, m, n = big.shape
    n_copies = n_s + len(whole)

    def body(*refs):
        big_ref, whole_refs = refs[0], refs[1:1 + len(whole)]
        rbig_ref, rwhole_refs = refs[1 + len(whole)], refs[2 + len(whole):2 + 2 * len(whole)]
        send_sems, recv_sems = refs[-2:]
        x, y, c = _place()
        sibling = (x, y, 1 - c)
        copies = [pltpu.make_async_remote_copy(
            src_ref=big_ref.at[s, 1 - c], dst_ref=rbig_ref.at[s], send_sem=send_sems.at[s], recv_sem=recv_sems.at[s],
            device_id=sibling, device_id_type=MESH) for s in range(n_s)]
        copies += [pltpu.make_async_remote_copy(
            src_ref=a, dst_ref=r, send_sem=send_sems.at[n_s + i], recv_sem=recv_sems.at[n_s + i],
            device_id=sibling, device_id_type=MESH) for i, (a, r) in enumerate(zip(whole_refs, rwhole_refs))]
        for cp in copies:
            cp.start()
        for cp in copies:
            cp.wait()

    return pl.pallas_call(
        body, name=name,
        out_shape=[jax.ShapeDtypeStruct((n_s, m, n), big.dtype)] + [jax.ShapeDtypeStruct(a.shape, a.dtype) for a in whole],
        in_specs=[_ANY] * (1 + len(whole)), out_specs=[_ANY] * (1 + len(whole)),
        scratch_shapes=[pltpu.SemaphoreType.DMA((n_copies,)), pltpu.SemaphoreType.DMA((n_copies,))],
    )(big, *whole)


_HBM = pl.BlockSpec(memory_space=pltpu.HBM)
_SEM = pl.BlockSpec(memory_space=pltpu.SEMAPHORE)
_EFFECT = pltpu.SideEffectType.DATAFLOW_SIDE_EFFECTING


def _split_copies(src_refs, land_refs, sems, plan, sending):
    n = len(sems) // 2
    return [pltpu.make_async_remote_copy(src_ref=s, dst_ref=d, send_sem=sems[k], recv_sem=sems[n + k], device_id=to,
                                         device_id_type=MESH)
            for k, (s, d, to) in enumerate(plan(src_refs, land_refs, sending))]


def _to_chips(src_at, land_at):
    def plan(src_refs, land_refs, sending):
        x, y, c = _place()
        return [(src_at(s, tx, ty, c), land_at(l, j, *((x, y) if sending else (tx, ty)), c), (tx, ty, c))
                for s, l in zip(src_refs, land_refs) for j, (tx, ty) in enumerate(_other_chips(x, y))]
    return plan


def _to_sibling(src_refs, land_refs, sending):
    x, y, c = _place()
    return [(s.at[k, 1 - c], l.at[k], (x, y, 1 - c)) for s, l in zip(src_refs, land_refs) for k in range(N_CHIPS)]


def _split_start(name, srcs, lands, plan, n, after=()):
    srcs, lands, after = list(srcs), list(lands), list(after)
    k, kb = len(srcs), len(srcs) + len(lands)

    def body(*refs):
        outs = refs[kb + len(after):]
        for cp in _split_copies(refs[:k], refs[k:kb], outs[:2 * n], plan, True):
            cp.start()
        outs[2 * n + kb][...] = jnp.zeros_like(outs[2 * n + kb])

    outs = pl.pallas_call(
        body, name=name,
        out_shape=(pltpu.SemaphoreType.DMA(()),) * (2 * n) + tuple(pltpu.HBM(a.shape, a.dtype) for a in srcs + lands)
        + (jax.ShapeDtypeStruct((8, LANES), F32),),
        in_specs=(_HBM,) * kb + (_ANY,) * len(after),
        out_specs=(_SEM,) * (2 * n) + (_HBM,) * kb + (pl.BlockSpec(memory_space=pltpu.VMEM),),
        input_output_aliases={i: 2 * n + i for i in range(kb)},
        compiler_params=pltpu.CompilerParams(has_side_effects=_EFFECT),
    )(*[pltpu.with_memory_space_constraint(a, pltpu.HBM) for a in srcs + lands], *after)
    return outs[:2 * n], list(outs[2 * n:2 * n + k]), list(outs[2 * n + k:2 * n + kb]), outs[2 * n + kb]


def _split_wait(name, sems, srcs, lands, after, plan):
    srcs, lands = list(srcs), list(lands)
    k, kb = len(srcs), len(srcs) + len(lands)

    def body(*refs):
        for cp in _split_copies(refs[:k], refs[k:kb], refs[kb:kb + len(sems)], plan, False):
            cp.wait_send()
            cp.wait_recv()

    outs = pl.pallas_call(
        body, name=name, out_shape=tuple(pltpu.HBM(a.shape, a.dtype) for a in srcs + lands),
        in_specs=(_HBM,) * kb + (_SEM,) * len(sems) + (_ANY,) * len(after), out_specs=(_HBM,) * kb,
        input_output_aliases={i: i for i in range(kb)}, compiler_params=pltpu.CompilerParams(has_side_effects=_EFFECT),
    )(*srcs, *lands, *sems, *after)
    return list(outs[:k]), list(outs[k:])


def _forward_landed(src_refs, land_refs, sending):
    x, y, c = _place()
    copies = []
    for ref in src_refs:
        m = ref.shape[0] // 8
        for tx, ty in _other_chips(x, y):
            rows = ref.at[pl.ds((4 * tx + 2 * ty + (c if sending else 1 - c)) * m, m), :]
            copies.append((rows, rows, (x, y, 1 - c)))
    return copies


def _place_own(pieces):
    k = len(pieces)

    def body(*refs):
        piece_refs, out_refs, stages = refs[:k], refs[k:2 * k], refs[2 * k:3 * k]
        load_sems, store_sems = refs[3 * k:]
        x, y, _ = _place()
        loads = [pltpu.make_async_copy(piece_refs[a], stages[a], load_sems.at[a]) for a in range(k)]
        stores = [pltpu.make_async_copy(
            stages[a], out_refs[a].at[pl.ds((2 * x + y) * pieces[a].shape[0], pieces[a].shape[0]), :], store_sems.at[a])
            for a in range(k)]
        for cp in loads:
            cp.start()
        for ld, st in zip(loads, stores):
            ld.wait()
            st.start()
        for cp in stores:
            cp.wait()

    return pl.pallas_call(
        body, name="gather_late_place_own",
        out_shape=[jax.ShapeDtypeStruct((N_CHIPS * p.shape[0], p.shape[1]), p.dtype) for p in pieces],
        in_specs=[_ANY] * k, out_specs=[_ANY] * k,
        scratch_shapes=[pltpu.VMEM(p.shape, p.dtype) for p in pieces]
        + [pltpu.SemaphoreType.DMA((k,)), pltpu.SemaphoreType.DMA((k,))],
    )(*pieces)


def _gather_finish(lands, pieces, name):
    k = len(lands)

    def body(*refs):
        land_refs, piece_refs, out_refs, stages = refs[:k], refs[k:2 * k], refs[2 * k:3 * k], refs[3 * k:4 * k]
        send_sems, recv_sems, load_sems, store_sems = refs[4 * k:]
        x, y, c = _place()
        sibling = (x, y, 1 - c)
        remote, loads, stores, arrivals = [], [], [], []
        for a in range(k):
            m = lands[a].shape[0] // 8

            def rows(px, py, pc, ref, m=m):
                return ref.at[pl.ds((4 * px + 2 * py + pc) * m, m), :]

            for j, (tx, ty) in enumerate(_other_chips(x, y)):
                sems = dict(send_sem=send_sems.at[3 * a + j], recv_sem=recv_sems.at[3 * a + j], device_id=sibling,
                            device_id_type=MESH)
                remote.append(pltpu.make_async_remote_copy(
                    src_ref=rows(tx, ty, c, land_refs[a]), dst_ref=rows(tx, ty, c, out_refs[a]), **sems))
                arrivals.append(pltpu.make_async_remote_copy(
                    src_ref=rows(tx, ty, 1 - c, out_refs[a]), dst_ref=rows(tx, ty, 1 - c, out_refs[a]), **sems))
            for h in range(2):
                loads.append(pltpu.make_async_copy(piece_refs[a].at[pl.ds(h * m, m), :], stages[a].at[h],
                                                   load_sems.at[2 * a + h]))
                stores.append(pltpu.make_async_copy(stages[a].at[h], rows(x, y, h, out_refs[a]), store_sems.at[2 * a + h]))
        for cp in remote + loads:
            cp.start()
        for ld, st in zip(loads, stores):
            ld.wait()
            st.start()
        for cp, arrival in zip(remote, arrivals):
            cp.wait_send()
            arrival.wait_recv()
        for cp in stores:
            cp.wait()

    return pl.pallas_call(
        body, name=name, out_shape=[jax.ShapeDtypeStruct(a.shape, a.dtype) for a in lands],
        in_specs=[_ANY] * (2 * k), out_specs=[_ANY] * k, input_output_aliases={i: i for i in range(k)},
        scratch_shapes=[pltpu.VMEM((2, a.shape[0] // 8, a.shape[1]), a.dtype) for a in lands]
        + [pltpu.SemaphoreType.DMA((3 * k,)), pltpu.SemaphoreType.DMA((3 * k,)), pltpu.SemaphoreType.DMA((2 * k,)),
           pltpu.SemaphoreType.DMA((2 * k,))],
    )(*lands, *pieces)


def _pair_fill(bufs, name):
    k = len(bufs)

    def body(*refs):
        send_sems, recv_sems = refs[-2:]
        x, y, c = _place()
        copies = [pltpu.make_async_remote_copy(
            src_ref=refs[i].at[c], dst_ref=refs[k + i].at[c], send_sem=send_sems.at[i], recv_sem=recv_sems.at[i],
            device_id=(x, y, 1 - c), device_id_type=MESH) for i in range(k)]
        for cp in copies:
            cp.start()
        for i, cp in enumerate(copies):
            cp.wait_send()
            pltpu.make_async_remote_copy(
                src_ref=refs[i].at[1 - c], dst_ref=refs[k + i].at[1 - c], send_sem=send_sems.at[i],
                recv_sem=recv_sems.at[i], device_id=(x, y, 1 - c), device_id_type=MESH).wait_recv()

    return pl.pallas_call(
        body, name=name, out_shape=[jax.ShapeDtypeStruct(a.shape, a.dtype) for a in bufs], in_specs=[_ANY] * k,
        out_specs=[_ANY] * k, input_output_aliases={i: i for i in range(k)},
        scratch_shapes=[pltpu.SemaphoreType.DMA((k,)), pltpu.SemaphoreType.DMA((k,))],
    )(*bufs)


def _row_tile(rows, cap=512):
    for t in range(cap - cap % 8, 7, -8):
        if rows % t == 0:
            return t
    return rows


def _elementwise(fn, n_out, name, *arrs, out_dtype=F32):
    rows, cols = arrs[0].shape
    tr = _row_tile(rows)
    n_in = len(arrs)

    def body(*refs):
        outs = fn(*[r[...].astype(F32) for r in refs[:n_in]])
        for r, o in zip(refs[n_in:], outs):
            r[...] = o.astype(out_dtype)

    spec = pl.BlockSpec((tr, cols), lambda i: (i, 0))
    return pl.pallas_call(
        body, grid=(rows // tr,), name=name, in_specs=[spec] * n_in, out_specs=[spec] * n_out,
        out_shape=[jax.ShapeDtypeStruct((rows, cols), out_dtype)] * n_out, compiler_params=_params("arbitrary"),
    )(*arrs)


def _pair_sums(gpacks, rbigs, ci, name):
    k = len(gpacks)

    def body(c_ref, *refs):
        for g_ref, r_ref, o_ref in zip(refs[:k], refs[k:2 * k], refs[2 * k:]):
            o_ref[...] = (g_ref[...] + r_ref[...]).astype(BF)

    half = lambda a: pl.BlockSpec((None,) + a.shape[1:], lambda s, c: (s, 0, 0))
    return pl.pallas_call(
        body, name=name, out_shape=[jax.ShapeDtypeStruct(r.shape, BF) for r in rbigs],
        grid_spec=pltpu.PrefetchScalarGridSpec(
            num_scalar_prefetch=1, grid=(N_CHIPS,),
            in_specs=[pl.BlockSpec((None, None) + g.shape[2:], lambda s, c: (s, c[0], 0, 0)) for g in gpacks]
            + [half(r) for r in rbigs],
            out_specs=[half(r) for r in rbigs]),
        compiler_params=_params("arbitrary"),
    )(ci.reshape(1), *gpacks, *rbigs)


def _chip_sums(sums, landed, chip, ci, name):
    k = len(sums)

    def body(p_ref, *refs):
        for own_ref, land_ref, o_ref in zip(refs[:k], refs[k:2 * k], refs[2 * k:]):
            f = lambda v: v.astype(F32)
            o_ref[...] = _add4(f(own_ref[...]), f(land_ref[0]), f(land_ref[1]), f(land_ref[2]))[0]

    return pl.pallas_call(
        body, name=name, out_shape=[jax.ShapeDtypeStruct((2,) + s.shape[1:], F32) for s in sums],
        grid_spec=pltpu.PrefetchScalarGridSpec(
            num_scalar_prefetch=1, grid=(1,),
            in_specs=[pl.BlockSpec((None,) + s.shape[1:], lambda i, p: (p[0], 0, 0)) for s in sums]
            + [pl.BlockSpec(l.shape, lambda i, p: (0, 0, 0)) for l in landed],
            out_specs=[pl.BlockSpec((None,) + s.shape[1:], lambda i, p: (p[1], 0, 0)) for s in sums]),
        compiler_params=_params("arbitrary"),
    )(jnp.stack([chip, ci]), *sums, *landed)


def _add2(a, b):
    return (a + b,)


def _add4(own, r0, r1, r2):
    return ((own + r2) + (r0 + r1),)


def _adamw_rows(ws, gs, ms, vs, name):
    k = len(ws)
    steps = next(s for s in (4, 2, 1) if all(w.shape[0] % (8 * s) == 0 for w in ws))

    def body(*refs):
        for i in range(k):
            outs = _adamw_math(*[refs[j * k + i][...] for j in range(4)])
            for j, o in enumerate(outs):
                refs[(4 + j) * k + i][...] = o

    specs = [pl.BlockSpec((w.shape[0] // steps, w.shape[1]), lambda i: (i, 0)) for w in ws]
    return pl.pallas_call(
        body, grid=(steps,), name=name, in_specs=specs * 4, out_specs=specs * 3,
        out_shape=[jax.ShapeDtypeStruct(w.shape, F32) for w in ws] * 3, compiler_params=_params("arbitrary"),
    )(*ws, *gs, *ms, *vs)


def _adamw_small(ws, gs, ms, vs):
    k = len(ws)

    def body(*refs):
        for i in range(k):
            outs = _adamw_math(*[refs[j * k + i][...] for j in range(4)])
            for j, o in enumerate(outs):
                refs[(4 + j) * k + i][...] = o

    return pl.pallas_call(
        body, name="adamw_small", out_shape=[jax.ShapeDtypeStruct(w.shape, F32) for w in ws] * 3,
    )(*ws, *gs, *ms, *vs)


def _adamw_math(w, g, m, v):
    m = ADAM_B1 * m + (1.0 - ADAM_B1) * g
    v = ADAM_B2 * v + (1.0 - ADAM_B2) * (g * g)
    m_hat = m / (1.0 - ADAM_B1 ** ADAM_STEP)
    v_hat = v / (1.0 - ADAM_B2 ** ADAM_STEP)
    delta = -ADAM_LR * (m_hat / (jnp.sqrt(v_hat) + ADAM_EPS) + ADAM_WD * w)
    return delta, m, v


WEIGHTS = ["meta_tokens", "ln1_g", "w_in", "q_a_norm_g", "w_uq", "kv_a_norm_g", "w_ukv", "q_norm_g", "k_norm_g",
           "conv_w", "conv_b", "lru_wa", "lru_ba", "lru_wi", "lru_bi", "lru_lambda", "attn_out_g", "rnn_out_g",
           "w_out", "ln2_g", "w_gate", "w_up", "w_down"]
BIG = ["w_in", "w_uq", "w_ukv", "w_out", "w_gate", "w_up", "w_down"]
BIG_T = {"w_in": True, "w_uq": True, "w_ukv": True, "w_out": False, "w_gate": True, "w_up": True, "w_down": False}
BIG_ROWS = {"w_in": 424, "w_uq": 72, "w_ukv": 64, "w_out": 256, "w_gate": 704, "w_up": 704, "w_down": 704}
EARLY = ["w_in", "w_uq", "w_ukv"]
LATE = ["w_out", "w_gate", "w_up", "w_down"]
EARLY_ROWS = 576
SMALL_SHARDED = ["meta_tokens", "conv_w", "lru_ba", "lru_bi", "lru_lambda"]
SMALL = [n for n in WEIGHTS if n not in BIG]
SMALL_PACK_ROWS = 160


def _offsets(names):
    off, o = {}, 0
    for n in names:
        off[n] = o
        o += BIG_ROWS[n]
    return off


def _shard_pack(names, src, rows):
    parts = [_to_pack_piece(n, src[n]) for n in names]
    used = sum(BIG_ROWS[n] for n in names)
    if rows > used:
        parts.append(jnp.zeros((rows - used, D), F32))
    return jnp.concatenate(parts, axis=0)


def _grad_pack(names, g, rows):
    parts = [g[n].reshape(N_CHIPS, BIG_ROWS[n], D) for n in names]
    used = sum(BIG_ROWS[n] for n in names)
    if rows > used:
        parts.append(jnp.zeros((N_CHIPS, rows - used, D), F32))
    return jnp.concatenate(parts, axis=1).reshape(N_CHIPS, 2, rows // 2, D)


def _to_pack_piece(name, shard):
    a = shard[0].T if BIG_T[name] else shard[0]
    return a.reshape(BIG_ROWS[name], D)


def _flat_pack(arrs, rows):
    flat = jnp.concatenate([a.reshape(-1) for a in arrs])
    return jnp.pad(flat, (0, rows * D - flat.shape[0])).reshape(rows, D)


def _flat_unpack(pack, shapes):
    flat, out, o = pack.reshape(-1), [], 0
    for s in shapes:
        n = math.prod(s)
        out.append(flat[o:o + n].reshape(s))
        o += n
    return out


def kernel(x, meta_tokens, ln1_g, w_in, q_a_norm_g, w_uq, kv_a_norm_g, w_ukv, q_norm_g, k_norm_g, conv_w, conv_b, lru_wa, lru_ba, lru_wi, lru_bi, lru_lambda, attn_out_g, rnn_out_g, w_out, ln2_g, w_gate, w_up, w_down, loss_target, m_meta_tokens, m_ln1_g, m_w_in, m_q_a_norm_g, m_w_uq, m_kv_a_norm_g, m_w_ukv, m_q_norm_g, m_k_norm_g, m_conv_w, m_conv_b, m_lru_wa, m_lru_ba, m_lru_wi, m_lru_bi, m_lru_lambda, m_attn_out_g, m_rnn_out_g, m_w_out, m_ln2_g, m_w_gate, m_w_up, m_w_down, v_meta_tokens, v_ln1_g, v_w_in, v_q_a_norm_g, v_w_uq, v_kv_a_norm_g, v_w_ukv, v_q_norm_g, v_k_norm_g, v_conv_w, v_conv_b, v_lru_wa, v_lru_ba, v_lru_wi, v_lru_bi, v_lru_lambda, v_attn_out_g, v_rnn_out_g, v_w_out, v_ln2_g, v_w_gate, v_w_up, v_w_down):
    wts = dict(zip(WEIGHTS, (meta_tokens, ln1_g, w_in, q_a_norm_g, w_uq, kv_a_norm_g, w_ukv, q_norm_g, k_norm_g, conv_w, conv_b, lru_wa, lru_ba, lru_wi, lru_bi, lru_lambda, attn_out_g, rnn_out_g, w_out, ln2_g, w_gate, w_up, w_down)))
    mom = dict(zip(WEIGHTS, (m_meta_tokens, m_ln1_g, m_w_in, m_q_a_norm_g, m_w_uq, m_kv_a_norm_g, m_w_ukv, m_q_norm_g, m_k_norm_g, m_conv_w, m_conv_b, m_lru_wa, m_lru_ba, m_lru_wi, m_lru_bi, m_lru_lambda, m_attn_out_g, m_rnn_out_g, m_w_out, m_ln2_g, m_w_gate, m_w_up, m_w_down)))
    var = dict(zip(WEIGHTS, (v_meta_tokens, v_ln1_g, v_w_in, v_q_a_norm_g, v_w_uq, v_kv_a_norm_g, v_w_ukv, v_q_norm_g, v_k_norm_g, v_conv_w, v_conv_b, v_lru_wa, v_lru_ba, v_lru_wi, v_lru_bi, v_lru_lambda, v_attn_out_g, v_rnn_out_g, v_w_out, v_ln2_g, v_w_gate, v_w_up, v_w_down)))
    xi, yi, ci = _place()
    chip = 2 * xi + yi
    off_e = _offsets(EARLY)
    half_e = EARLY_ROWS // 2
    gather_plan = _to_chips(lambda ref, tx, ty, c: ref.at[pl.ds(c * (ref.shape[0] // 2), ref.shape[0] // 2), :],
                            lambda ref, j, px, py, c: ref.at[pl.ds((4 * px + 2 * py + c) * (ref.shape[0] // 8),
                                                                   ref.shape[0] // 8), :])
    scatter_plan = _to_chips(lambda ref, tx, ty, c: ref.at[2 * tx + ty], lambda ref, j, px, py, c: ref.at[j])
    everywhere = _to_chips(lambda ref, tx, ty, c: ref, lambda ref, j, px, py, c: ref.at[j])
    n_late = len(LATE)

    pack_e = _shard_pack(EARLY, wts, EARLY_ROWS).astype(BF)
    spack = jnp.concatenate([meta_tokens[:, :LANES], meta_tokens[:, LANES:], conv_w[0], lru_ba[0], lru_bi[0],
                             lru_lambda[0], jnp.zeros((6, LANES), F32)], axis=0)
    sems_g, src_g, land_g, _ = _split_start(
        "gather_early_start", [pack_e, spack], [lax.empty((N_CHIPS * EARLY_ROWS, D), BF), lax.empty((N_CHIPS * 48, LANES), F32)],
        gather_plan, 6)
    tgt_padded = _pad_target(loss_target)
    pieces_l = [_to_pack_piece(n, wts[n]).astype(BF) for n in LATE]
    lands_l = list(_place_own(pieces_l))
    src_g, land_g = _split_wait("gather_early_wait", sems_g, src_g, land_g, [tgt_padded] + lands_l, gather_plan)
    ge, gs = _gather_finish(land_g, src_g, "gather_early_finish")
    ge = ge.reshape(N_CHIPS, EARLY_ROWS, D)
    gs = gs.reshape(N_CHIPS, 48, LANES)
    full = {n: ge[:, off_e[n]:off_e[n] + BIG_ROWS[n]] for n in EARLY}
    sems_l, src_l, land_l, tied = _split_start("gather_late_start", pieces_l, lands_l, gather_plan, 3 * n_late, after=[ge])

    forward, pair, late = {}, {}, {}

    def late_forward(after):
        _, landed = _split_wait("gather_late_wait", sems_l, src_l, land_l, after, gather_plan)
        forward["sems"], forward["src"], _, zeros = _split_start(
            "gather_late_forward_start", landed, [], _forward_landed, 3 * n_late)
        return zeros[0, 0]

    def late_weights(after):
        (w_out_, w_gate_, w_up_, w_down_), _ = _split_wait(
            "gather_late_forward_wait", forward["sems"], forward["src"], [], after, _forward_landed)
        return dict(w_out=w_out_, w_gate_t=w_gate_, w_up_t=w_up_, w_down=w_down_)

    def early_grads(g_late):
        halves = [g_late[n].reshape(N_CHIPS, 2, BIG_ROWS[n] // 2, D) for n in LATE]
        pair["sems"], pair["src"], pair["land"], zeros = _split_start(
            "grad_pair_late_start", halves, [lax.empty((N_CHIPS, BIG_ROWS[n] // 2, D), F32) for n in LATE], _to_sibling,
            N_CHIPS * n_late)
        return zeros[0, 0]

    def mid_grads(after):
        halves, landed = _split_wait("grad_pair_late_wait", pair["sems"], pair["src"], pair["land"], after, _to_sibling)
        chip_sums = _pair_sums(halves, landed, ci, "grad_pair_sum_late")
        late["sems"], late["src"], late["land"], zeros = _split_start(
            "grad_chip_late_start", chip_sums, [lax.empty((3, BIG_ROWS[n] // 2, D), BF) for n in LATE], scatter_plan,
            3 * n_late)
        return zeros[0, 0]

    cols = lambda a: a.transpose(1, 0, 2).reshape(a.shape[1], N_CHIPS * a.shape[2])
    meta_full = cols(jnp.concatenate([gs[:, 0:16], gs[:, 16:32]], axis=2))
    w = dict(
        w_in_t=full["w_in"].reshape(IN_COLS, D), w_uq_t=full["w_uq"].reshape(N_HEADS * QK_HEAD, Q_LORA),
        w_ukv_t=full["w_ukv"].reshape(2 * D_ATTN, KV_LORA),
        ln1_g=ln1_g, q_a_norm_g=q_a_norm_g, kv_a_norm_g=kv_a_norm_g, q_norm_g=q_norm_g, k_norm_g=k_norm_g,
        conv_w=cols(gs[:, 32:36]), conv_b=conv_b, lru_wa=lru_wa[0], lru_ba=cols(gs[:, 36:38]), lru_wi=lru_wi[0],
        lru_bi=cols(gs[:, 38:40]), lru_lambda=cols(gs[:, 40:42]), attn_out_g=attn_out_g, rnn_out_g=rnn_out_g,
        ln2_g=ln2_g,
    )

    loss_local, grad_x, g, last = _local_step(x, tgt_padded, meta_full + tied[0, 0], w, late_forward, late_weights,
                                              early_grads, mid_grads)

    gpack = _grad_pack(EARLY, {"w_in": g["w_in_t"], "w_uq": g["w_uq_t"], "w_ukv": g["w_ukv_t"]}, EARLY_ROWS)
    full_shapes = {n: wts[n].shape for n in SMALL}
    full_shapes.update(meta_tokens=(N_META, D), conv_w=(1, CONV_W, D_RNN), lru_ba=(1, 2, D_RNN), lru_bi=(1, 2, D_RNN),
                       lru_lambda=(1, 2, D_RNN))
    gsmall = _flat_pack([g[n] for n in SMALL] + [loss_local], SMALL_PACK_ROWS)
    rbig, rsmall = _pair_exchange(gpack, [gsmall], "grad_pair_exchange")
    chip_big = _pair_sums([gpack], [rbig], ci, "grad_pair_sum")
    (chip_small,) = _elementwise(_add2, 1, "grad_pair_sum_small", gsmall, rsmall)
    early_plan = lambda srcs, lands, sending: (scatter_plan(srcs[:1], lands[:1], sending)
                                               + everywhere(srcs[1:], lands[1:], sending))
    sems_e, src_e, land_e, zero_e = _split_start(
        "grad_chip_early_start", list(chip_big) + [chip_small],
        [lax.empty((3, half_e, D), BF), lax.empty((3, SMALL_PACK_ROWS, D), F32)], early_plan, 6)

    grads, delta, new_m, new_v = {}, {}, {}, {}

    def adamw_big(names, gshards):
        as_rows = lambda n, a: a[0].T if BIG_T[n] else a[0]
        back = lambda n, a: a.T[None] if BIG_T[n] else a[None]
        ws, ms, vs = ([as_rows(n, src[n]) for n in names] for src in (wts, mom, var))
        g2 = [gs.reshape(w.shape) for w, gs in zip(ws, gshards)]
        outs = _adamw_rows(ws, g2, ms, vs, "adamw_" + names[0])
        for i, n in enumerate(names):
            grads[n] = back(n, g2[i])
            delta[n], new_m[n], new_v[n] = (back(n, outs[j * len(names) + i]) for j in range(3))
        return outs[0]

    sums, landed = _split_wait("grad_chip_late_wait", late["sems"], late["src"], late["land"], last + [zero_e], scatter_plan)
    shards_l = _pair_fill(_chip_sums(sums, landed, chip, ci, "grad_chip_sum_late"), "grad_pair_fill_late")
    done_late = adamw_big(LATE, shards_l)
    (src_e, src_s), (land_e, land_s) = _split_wait("grad_chip_early_wait", sems_e, src_e, land_e, [done_late, grad_x],
                                                   early_plan)
    (shard_e,) = _pair_fill(_chip_sums([src_e], [land_e], chip, ci, "grad_chip_sum"), "grad_pair_fill_early")
    shard_e = shard_e.reshape(EARLY_ROWS, D)
    for n in EARLY:
        adamw_big([n], [shard_e[off_e[n]:off_e[n] + BIG_ROWS[n]]])
    (small_sum,) = _elementwise(_add4, 1, "grad_chip_sum_small", src_s, land_s[0], land_s[1], land_s[2])
    *small_grads, loss = _flat_unpack(small_sum, [full_shapes[n] for n in SMALL] + [()])
    small_full = dict(zip(SMALL, small_grads))
    for n in SMALL:
        a = small_full[n]
        if n in SMALL_SHARDED:
            width = wts[n].shape[-1]
            a = lax.dynamic_slice_in_dim(a, chip * width, width, axis=a.ndim - 1)
        grads[n] = a.reshape(wts[n].shape)

    rows_of = lambda a: a.reshape(-1, a.shape[-1])
    outs = _adamw_small(*[[rows_of(src[n]) for n in SMALL] for src in (wts, grads, mom, var)])
    for j, dst in enumerate((delta, new_m, new_v)):
        dst.update({n: outs[j * len(SMALL) + i].reshape(wts[n].shape) for i, n in enumerate(SMALL)})

    return (loss, grad_x, *[grads[n] for n in WEIGHTS], *[delta[n] for n in WEIGHTS],
            *[new_m[n] for n in WEIGHTS], *[new_v[n] for n in WEIGHTS])
```

```python
import math

import jax
import jax.numpy as jnp
from jax import lax
from jax.experimental import pallas as pl
from jax.experimental.pallas import tpu as pltpu

F32 = jnp.float32
BF = jnp.bfloat16
MESH = pl.DeviceIdType.MESH

D = 1024
SEQ = 2048
N_META = 16
T = N_META + SEQ
N_HEADS = 8
QK_NOPE = 64
QK_ROPE = 32
QK_HEAD = 96
V_HEAD = 64
Q_LORA = 384
KV_LORA = 256
D_ATTN = 512
D_RNN = 512
RNN_BW = 64
CONV_W = 4
LRU_C = 8.0
ROPE_THETA = 10000.0
D_FF = 2816
EPS = 1e-6
IN_COLS = 1696
ADAM_LR, ADAM_B1, ADAM_B2, ADAM_EPS, ADAM_WD, ADAM_STEP = 0.001, 0.9, 0.999, 1e-08, 0.01, 10

LANES = 128
TP = 2176
NB = 2
R = NB * TP
TR = 256
TRF = 256
TQ = 1088
HP = LANES
PC = 1792
O_CKV, O_KR, O_XR, O_XG = 384, 640, 768, 1280
CG = 128
N_CG = D_RNN // CG
VMEM_LIMIT = 56 * 1024 * 1024
N_CHIPS = 4
SCALE = QK_HEAD ** -0.5
KEY_MASK = -30000.0
LOG2_E = 1.4426950408889634
SCALE_LOG2 = SCALE * LOG2_E


def _nt(a, b):
    return lax.dot_general(a, b, (((1,), (1,)), ((), ())), preferred_element_type=F32)


def _nn(a, b):
    return jnp.dot(a, b, preferred_element_type=F32)


def _tn(a, b):
    return lax.dot_general(a, b, (((0,), (0,)), ((), ())), preferred_element_type=F32)


def _rms(x, g, n):
    ms = jnp.sum(x * x, axis=-1, keepdims=True) * (1.0 / n)
    return x * lax.rsqrt(ms + EPS) * g


def _lane_sum(y):
    return jnp.sum(y, axis=-1, keepdims=True)


def _rot(x):
    lane = lax.broadcasted_iota(jnp.int32, x.shape, 1)
    left = pltpu.roll(x, HP - 16, 1)
    right = pltpu.roll(x, 16, 1)
    lo = (lane >= QK_NOPE) & (lane < QK_NOPE + 16)
    hi = (lane >= QK_NOPE + 16) & (lane < QK_HEAD)
    return jnp.where(lo, -left, jnp.where(hi, right, 0.0))


def _head(x, g, cs, sn):
    n = x * lax.rsqrt(_lane_sum(x * x) * (1.0 / QK_HEAD) + EPS) * g
    return n * cs + _rot(n) * sn


def _head_bwd(x, g, cs, sn, dout):
    rs = lax.rsqrt(_lane_sum(x * x) * (1.0 / QK_HEAD) + EPS)
    xh = x * rs
    dn = dout * cs - _rot(dout * sn)
    gdn = g * dn
    t = _lane_sum(gdn * xh) * (1.0 / QK_HEAD)
    return rs * (gdn - xh * t), jnp.sum(dn * xh, axis=0, keepdims=True)


def _const_spec(shape):
    return pl.BlockSpec(shape, lambda *_: (0,) * len(shape), pipeline_mode=pl.Buffered(1))


def _row_spec(n, tr=TR):
    return pl.BlockSpec((tr, n), lambda i: (i, 0))


def _params(*sem, vmem=VMEM_LIMIT):
    return pltpu.CompilerParams(dimension_semantics=sem, vmem_limit_bytes=vmem)


def _stage_a_fwd(hp, cs, sn, cw):
    def body(hp_ref, cs_ref, sn_ref, ln1, win, qag, wq, kvag, wk, wv, qg, kg,
             pa_ref, xr_ref, xg_ref, q_ref, k_ref, v_ref):
        hn = _rms(hp_ref[...], ln1[...], D).astype(BF)
        p = _nt(hn, win[...])
        pa_ref[...] = p[:, :O_XR]
        xr_ref[...] = p[:, O_XR:O_XG]
        xg_ref[...] = p[:, O_XG:]
        cqn = _rms(p[:, :O_CKV], qag[...], Q_LORA).astype(BF)
        ckvn = _rms(p[:, O_CKV:O_KR], kvag[...], KV_LORA).astype(BF)
        kr = p[:, O_KR:O_XR]
        c, s = cs_ref[...], sn_ref[...]
        mask_lane = lax.broadcasted_iota(jnp.int32, (1, HP), 1) == QK_HEAD
        row = pl.program_id(0) * TRF + lax.broadcasted_iota(jnp.int32, (TRF, 1), 0)
        key_mask = jnp.where(jnp.where(row >= TP, row - TP, row) < T, 0.0, KEY_MASK)
        qraw = _nt(cqn, wq[...])
        kraw = _nt(ckvn, wk[...])
        for h in range(N_HEADS):
            sl = slice(h * HP, (h + 1) * HP)
            q_ref[:, sl] = jnp.where(mask_lane, 1.0, _head(qraw[:, sl], qg[...], c, s)).astype(BF)
            k_ref[:, sl] = jnp.where(mask_lane, key_mask, _head(kraw[:, sl] + kr, kg[...], c, s)).astype(BF)
        v_ref[...] = _nt(ckvn, wv[...]).astype(BF)

    rs = lambda n: _row_spec(n, TRF)
    return pl.pallas_call(
        body, grid=(R // TRF,), name="stage_a_fwd",
        in_specs=[rs(D), rs(HP), rs(HP), _const_spec((1, D)), _const_spec((PC, D)),
                  _const_spec((1, Q_LORA)), _const_spec((N_HEADS * HP, Q_LORA)), _const_spec((1, KV_LORA)),
                  _const_spec((N_HEADS * HP, KV_LORA)), _const_spec((D_ATTN, KV_LORA)), _const_spec((1, HP)),
                  _const_spec((1, HP))],
        out_specs=[rs(O_XR), rs(D_RNN), rs(D_RNN), rs(N_HEADS * HP), rs(N_HEADS * HP), rs(D_ATTN)],
        out_shape=[jax.ShapeDtypeStruct((R, O_XR), F32), jax.ShapeDtypeStruct((R, D_RNN), F32),
                   jax.ShapeDtypeStruct((R, D_RNN), F32), jax.ShapeDtypeStruct((R, N_HEADS * HP), BF),
                   jax.ShapeDtypeStruct((R, N_HEADS * HP), BF), jax.ShapeDtypeStruct((R, D_ATTN), BF)],
        compiler_params=_params("arbitrary"),
    )(hp, cs, sn, cw["ln1_g"], cw["win"], cw["qa_g"], cw["wq"], cw["kva_g"], cw["wk"], cw["wv"], cw["q_g"], cw["k_g"])


def _stage_a_bwd(dq, dk, dv, dxr, dxg, dh1, hp, pa, cs, sn, cw):
    def body(dq_ref, dk_ref, dv_ref, dxr_ref, dxg_ref, dh1_ref, hp_ref, pa_ref, cs_ref, sn_ref,
             ln1, win, qag, wq, kvag, wk, wv, qg, kg,
             dhp_ref, dp_ref, dqraw_ref, dkraw_ref, hn_ref, cqn_ref, ckvn_ref,
             dln1_ref, dqag_ref, dkvag_ref, dqg_ref, dkg_ref):
        @pl.when(pl.program_id(0) == 0)
        def _():
            for r in (dln1_ref, dqag_ref, dkvag_ref, dqg_ref, dkg_ref):
                r[...] = jnp.zeros_like(r)

        hn, vjp_ln1 = jax.vjp(lambda h, g: _rms(h, g, D), hp_ref[...], ln1[...])
        hn_ref[...] = hn.astype(BF)
        pa_v = pa_ref[...]
        cqn, vjp_qa = jax.vjp(lambda x, g: _rms(x, g, Q_LORA), pa_v[:, :O_CKV], qag[...])
        ckvn, vjp_kva = jax.vjp(lambda x, g: _rms(x, g, KV_LORA), pa_v[:, O_CKV:O_KR], kvag[...])
        kr = pa_v[:, O_KR:O_XR]
        cqnb, ckvnb = cqn.astype(BF), ckvn.astype(BF)
        cqn_ref[...] = cqnb
        ckvn_ref[...] = ckvnb
        c, s = cs_ref[...], sn_ref[...]
        lane = lax.broadcasted_iota(jnp.int32, (1, HP), 1)
        rope_lanes = ((lane >= QK_NOPE) & (lane < QK_HEAD)).astype(F32)
        dkr = jnp.zeros((TR, HP), F32)
        dqg = jnp.zeros((1, HP), F32)
        dkg = jnp.zeros((1, HP), F32)
        qraw = _nt(cqnb, wq[...])
        kraw = _nt(ckvnb, wk[...])
        for h in range(N_HEADS):
            sl = slice(h * HP, (h + 1) * HP)
            dqraw, dg = _head_bwd(qraw[:, sl], qg[...], c, s, dq_ref[:, sl])
            dqg = dqg + dg
            dqraw_ref[:, sl] = dqraw.astype(BF)
            dkraw, dg = _head_bwd(kraw[:, sl] + kr, kg[...], c, s, dk_ref[:, sl])
            dkg = dkg + dg
            dkraw_ref[:, sl] = dkraw.astype(BF)
            dkr = dkr + dkraw * rope_lanes
        dcq, dqag = vjp_qa(_nn(dqraw_ref[...], wq[...]))
        dckv, dkvag = vjp_kva(_nn(dkraw_ref[...], wk[...]) + _nn(dv_ref[...].astype(BF), wv[...]))
        dpb = jnp.concatenate([dcq, dckv, dkr, dxr_ref[...], dxg_ref[...]], axis=1).astype(BF)
        dp_ref[...] = dpb
        dh, dln1 = vjp_ln1(_nn(dpb, win[...]))
        dhp_ref[...] = dh + dh1_ref[...]
        dln1_ref[...] += dln1
        dqag_ref[...] += dqag
        dkvag_ref[...] += dkvag
        dqg_ref[...] += dqg
        dkg_ref[...] += dkg

    acc = lambda n: pl.BlockSpec((1, n), lambda i: (0, 0))
    return pl.pallas_call(
        body, grid=(R // TR,), name="stage_a_bwd",
        in_specs=[_row_spec(N_HEADS * HP), _row_spec(N_HEADS * HP), _row_spec(D_ATTN), _row_spec(D_RNN),
                  _row_spec(D_RNN), _row_spec(D), _row_spec(D), _row_spec(O_XR), _row_spec(HP), _row_spec(HP),
                  _const_spec((1, D)), _const_spec((PC, D)), _const_spec((1, Q_LORA)),
                  _const_spec((N_HEADS * HP, Q_LORA)), _const_spec((1, KV_LORA)),
                  _const_spec((N_HEADS * HP, KV_LORA)), _const_spec((D_ATTN, KV_LORA)), _const_spec((1, HP)),
                  _const_spec((1, HP))],
        out_specs=[_row_spec(D), _row_spec(PC), _row_spec(N_HEADS * HP), _row_spec(N_HEADS * HP), _row_spec(D),
                   _row_spec(Q_LORA), _row_spec(KV_LORA), acc(D), acc(Q_LORA), acc(KV_LORA), acc(HP), acc(HP)],
        out_shape=[jax.ShapeDtypeStruct((R, D), F32), jax.ShapeDtypeStruct((R, PC), BF),
                   jax.ShapeDtypeStruct((R, N_HEADS * HP), BF), jax.ShapeDtypeStruct((R, N_HEADS * HP), BF),
                   jax.ShapeDtypeStruct((R, D), BF), jax.ShapeDtypeStruct((R, Q_LORA), BF),
                   jax.ShapeDtypeStruct((R, KV_LORA), BF), jax.ShapeDtypeStruct((1, D), F32),
                   jax.ShapeDtypeStruct((1, Q_LORA), F32), jax.ShapeDtypeStruct((1, KV_LORA), F32),
                   jax.ShapeDtypeStruct((1, HP), F32), jax.ShapeDtypeStruct((1, HP), F32)],
        compiler_params=_params("arbitrary"),
    )(dq, dk, dv, dxr, dxg, dh1, hp, pa, cs, sn, cw["ln1_g"], cw["win"], cw["qa_g"], cw["wq"], cw["kva_g"],
      cw["wk"], cw["wv"], cw["q_g"], cw["k_g"])


def _head_mask(half, dtype):
    lane = lax.broadcasted_iota(jnp.int32, (1, 2 * V_HEAD), 1)
    return ((lane >= V_HEAD) == (half == 1)).astype(dtype)


def _attn_specs(tq):
    n_q = TP // tq
    return (NB, N_HEADS // 2, n_q), dict(
        q=pl.BlockSpec((tq, 2 * HP), lambda b, j, i: (b * n_q + i, j)),
        k=pl.BlockSpec((TP, 2 * HP), lambda b, j, i: (b, j)),
        v=pl.BlockSpec((TP, 2 * V_HEAD), lambda b, j, i: (b, j)),
        o=pl.BlockSpec((tq, 2 * V_HEAD), lambda b, j, i: (b * n_q + i, j)),
        lse=pl.BlockSpec((None, tq, 2), lambda b, j, i: (j, b * n_q + i, 0)))


TQF = 1088


def _attn_fwd(q, k, v):
    def body(q_ref, k_ref, v_ref, o_ref, lse_ref):
        v2 = v_ref[...]
        o = jnp.zeros((TQF, 2 * V_HEAD), F32)
        lse = []
        for hh in range(2):
            sl = slice(hh * HP, (hh + 1) * HP)
            raw = _nt(q_ref[:, sl], k_ref[:, sl])
            m = jnp.max(raw, axis=-1, keepdims=True)
            e = jnp.exp2((raw - m) * SCALE_LOG2)
            l = jnp.sum(e, axis=-1, keepdims=True)
            o = o + _nn(e.astype(BF), v2 * _head_mask(hh, BF)) * (1.0 / l)
            lse.append(m * SCALE_LOG2 + jnp.log(l) * LOG2_E)
        o_ref[...] = o
        lane = lax.broadcasted_iota(jnp.int32, (TQF, 2), 1)
        lse_ref[...] = jnp.where(lane == 0, lse[0], lse[1])

    grid, sp = _attn_specs(TQF)
    return pl.pallas_call(
        body, grid=grid, name="attn_fwd", in_specs=[sp["q"], sp["k"], sp["v"]], out_specs=[sp["o"], sp["lse"]],
        out_shape=[jax.ShapeDtypeStruct((R, D_ATTN), F32), jax.ShapeDtypeStruct((N_HEADS // 2, R, 2), F32)],
        compiler_params=_params("arbitrary", "arbitrary", "arbitrary"),
    )(q, k, v)


def _attn_bwd(q, k, v, o, lse, do):
    def body(q_ref, k_ref, v_ref, o_ref, lse_ref, do_ref, dq_ref, dk_ref, dv_ref):
        @pl.when(pl.program_id(2) == 0)
        def _():
            dk_ref[...] = jnp.zeros_like(dk_ref)
            dv_ref[...] = jnp.zeros_like(dv_ref)

        do = do_ref[...]
        dob = do.astype(BF)
        do_o = do * o_ref[...]
        v2 = v_ref[...]
        dv_sum = jnp.zeros((TP, 2 * V_HEAD), F32)
        for hh in range(2):
            sl = slice(hh * HP, (hh + 1) * HP)
            qb, kb = q_ref[:, sl], k_ref[:, sl]
            p = jnp.exp2(_nt(qb, kb) * SCALE_LOG2 - lse_ref[:, hh:hh + 1])
            dp = _nt(dob, v2 * _head_mask(hh, BF))
            delta = jnp.sum(do_o * _head_mask(hh, F32), axis=-1, keepdims=True)
            dsb = (p * (dp - delta) * SCALE).astype(BF)
            dq_ref[:, sl] = _nn(dsb, kb)
            dk_ref[:, sl] += _tn(dsb, qb)
            dv_sum = dv_sum + _tn(p.astype(BF), dob) * _head_mask(hh, F32)
        dv_ref[...] += dv_sum

    grid, sp = _attn_specs(TQ)
    return pl.pallas_call(
        body, grid=grid, name="attn_bwd", in_specs=[sp["q"], sp["k"], sp["v"], sp["o"], sp["lse"], sp["o"]],
        out_specs=[sp["q"], sp["k"], sp["v"]],
        out_shape=[jax.ShapeDtypeStruct((R, N_HEADS * HP), F32), jax.ShapeDtypeStruct((R, N_HEADS * HP), F32),
                   jax.ShapeDtypeStruct((R, D_ATTN), F32)],
        compiler_params=_params("arbitrary", "arbitrary", "arbitrary"),
    )(q, k, v, o, lse, do)


SEG = TP // 8


def _scan_pair(af_ref, bf_ref, hf_ref, ab_ref, bb_ref, hb_ref, pf_ref, pb_ref):
    unroll = 8

    def step(i, carry):
        hf, pf, hb, pb = carry
        for u in range(unroll):
            j = i * unroll + u
            rows_f, rows_b = pl.ds(j, 8, stride=SEG), pl.ds(SEG - 1 - j, 8, stride=SEG)
            a = af_ref[rows_f, :]
            hf, pf = a * hf + bf_ref[rows_f, :], a * pf
            hf_ref[rows_f, :] = hf
            pf_ref[rows_f, :] = pf
            a = ab_ref[rows_b, :]
            hb, pb = a * hb + bb_ref[rows_b, :], a * pb
            hb_ref[rows_b, :] = hb
            pb_ref[rows_b, :] = pb
        return hf, pf, hb, pb

    zero, one = jnp.zeros((8, CG), F32), jnp.ones((8, CG), F32)
    hf, pf, hb, pb = lax.fori_loop(0, SEG // unroll, step, (zero, one, zero, one))
    seg = lax.broadcasted_iota(jnp.int32, (8, CG), 0)
    cf, cb = zero, zero
    for s in range(1, 8):
        cf = jnp.where(seg == s, pltpu.roll(hf + pf * cf, 1, 0), cf)
        cb = jnp.where(seg == 7 - s, pltpu.roll(hb + pb * cb, 7, 0), cb)
    for s in range(8):
        rows = slice(s * SEG, (s + 1) * SEG)
        hf_ref[rows, :] = hf_ref[rows, :] + pf_ref[rows, :] * cf[s:s + 1, :]
        hb_ref[rows, :] = hb_ref[rows, :] + pb_ref[rows, :] * cb[s:s + 1, :]


def _shifts(x):
    t = lax.broadcasted_iota(jnp.int32, x.shape, 0)
    xm2 = jnp.where(t >= 2, pltpu.roll(x, 2, 0), 0.0)
    xm1 = jnp.where(t >= 1, pltpu.roll(x, 1, 0), 0.0)
    xp1 = jnp.where(t < TP - 1, pltpu.roll(x, TP - 1, 0), 0.0)
    return xm2, xm1, xp1


def _softplus(z):
    e = jnp.exp(-jnp.abs(z))
    small = e * (1.0 - e * (0.5 - e * (1.0 / 3.0)))
    return jnp.maximum(z, 0.0) + jnp.where(e < 0.01, small, jnp.log(1.0 + e))


def _sigmoid(x):
    return 0.5 * jnp.tanh(0.5 * x) + 0.5


def _one_minus_sq(log_a, a):
    x = 2.0 * log_a
    series = -x * (1.0 + x * 0.5 * (1.0 + x * (1.0 / 3.0) * (1.0 + x * 0.25)))
    return jnp.where(x > -0.05, series, 1.0 - a * a)


def _gates(row0, xc, pa_f, pi_f, pa_b, pi_b, lam_f, lam_b):
    t = row0 + lax.broadcasted_iota(jnp.int32, xc.shape, 0)
    valid = t < T
    out = []
    for pa, pi_, lam in ((pa_f, pi_f, lam_f), (pa_b, pi_b, lam_b)):
        r = _sigmoid(pa)
        gate_i = _sigmoid(pi_)
        log_a = -LRU_C * r * _softplus(-lam)
        a = jnp.exp(log_a)
        mult = jnp.sqrt(jnp.maximum(_one_minus_sq(log_a, a), 0.0))
        out += [a, jnp.where(valid, mult * (gate_i * xc), 0.0)]
    return tuple(out)


def _gates_bwd(row0, xc, pres, lams, cots):
    t = row0 + lax.broadcasted_iota(jnp.int32, xc.shape, 0)
    valid = t < T
    dxc = jnp.zeros_like(xc)
    dpres, dlams = [], []
    for d in range(2):
        pa, pi_, lam = pres[2 * d], pres[2 * d + 1], lams[d]
        da, db = cots[2 * d], jnp.where(valid, cots[2 * d + 1], 0.0)
        r = _sigmoid(pa)
        gate_i = _sigmoid(pi_)
        sp = _softplus(-lam)
        log_a = -LRU_C * r * sp
        a = jnp.exp(log_a)
        m2 = jnp.maximum(_one_minus_sq(log_a, a), 0.0)
        mult = jnp.sqrt(m2)
        dxc = dxc + db * (mult * gate_i)
        d_gate = db * (mult * xc)
        d_m2 = jnp.where(m2 > 0.0, db * (gate_i * xc) * (0.5 * lax.rsqrt(m2)), 0.0)
        d_log_a = da * a - 2.0 * d_m2 * (a * a)
        dpres += [d_log_a * (-LRU_C * sp) * (r * (1.0 - r)), d_gate * (gate_i * (1.0 - gate_i))]
        d_sp = jnp.sum(d_log_a * (-LRU_C * r), axis=0, keepdims=True)
        dlams.append(-d_sp * jax.nn.sigmoid(-lam))
    return dxc, dpres, dlams


def _rnn_specs():
    seq = pl.BlockSpec((TP, CG), lambda g, b: (b, g))
    return dict(
        seq=seq,
        cw=pl.BlockSpec((CONV_W, CG), lambda g, b: (0, g)),
        cb=pl.BlockSpec((1, CG), lambda g, b: (0, g)),
        w4=pl.BlockSpec((None, CG, 4 * CG), lambda g, b: (g, 0, 0)),
        b4=pl.BlockSpec((None, 1, 4 * CG), lambda g, b: (g, 0, 0)),
        lam=pl.BlockSpec((None, 1, 2 * CG), lambda g, b: (g, 0, 0)),
    )


def _conv(x, xm2, xm1, xp1, cw_ref, cb_ref):
    return cw_ref[0:1, :] * xm2 + cw_ref[1:2, :] * xm1 + cw_ref[2:3, :] * x + cw_ref[3:4, :] * xp1 + cb_ref[...]


TC = 128
N_TC = TP // TC


def _split4(pre):
    return pre[:, :CG], pre[:, CG:2 * CG], pre[:, 2 * CG:3 * CG], pre[:, 3 * CG:]


def _rnn_fwd(xr, xg, cw):
    def body(xr_ref, xg_ref, cw_ref, cb_ref, w4_ref, b4_ref, lam_ref, y_ref, hf_ref, hb_ref, af_ref, ab_ref, xc_ref,
             af, bf, ab, bb, pf, pb):
        x = xr_ref[...]
        xc_ref[...] = _conv(x, *_shifts(x), cw_ref, cb_ref)
        lam = lam_ref[...]

        def chunk(i, _):
            rows = pl.ds(pl.multiple_of(i * TC, TC), TC)
            xc = xc_ref[rows, :]
            pre = _nn(xc.astype(BF), w4_ref[...]) + b4_ref[...]
            a_f, b_f, a_b, b_b = _gates(i * TC, xc, *_split4(pre), lam[:, :CG], lam[:, CG:])
            af[rows, :] = a_f
            bf[rows, :] = b_f
            ab[rows, :] = a_b
            bb[rows, :] = b_b
            af_ref[rows, :] = a_f
            ab_ref[rows, :] = a_b
            return 0

        lax.fori_loop(0, N_TC, chunk, 0)
        _scan_pair(af, bf, hf_ref, ab, bb, hb_ref, pf, pb)
        y_ref[...] = (hf_ref[...] + hb_ref[...]) * jax.nn.gelu(xg_ref[...])

    sp = _rnn_specs()
    return pl.pallas_call(
        body, grid=(N_CG, NB), name="rnn_fwd",
        in_specs=[sp["seq"], sp["seq"], sp["cw"], sp["cb"], sp["w4"], sp["b4"], sp["lam"]],
        out_specs=[sp["seq"]] * 6, out_shape=[jax.ShapeDtypeStruct((R, D_RNN), F32)] * 6,
        scratch_shapes=[pltpu.VMEM((TP, CG), F32)] * 6,
        compiler_params=_params("arbitrary", "arbitrary"),
    )(xr, xg, cw["conv_w"], cw["conv_b"], cw["w4"], cw["b4"], cw["lam"])


def _rnn_bwd(dy, xr, xg, hf, hb, af, ab, xc, cw):
    def body(dy_ref, xr_ref, xg_ref, hf_ref, hb_ref, af_ref, ab_ref, xc_s, cw_ref, cb_ref, w4_ref, b4_ref, lam_ref,
             dxr_ref, dxg_ref, dcw_ref, dcb_ref, dw4_ref, db4_ref, dlam_ref,
             af_s, ab_s, dhs_s, lf_s, lb_s, daf_s, dab_s, dxc_s):
        @pl.when(pl.program_id(1) == 0)
        def _():
            for r in (dcw_ref, dcb_ref, dw4_ref, db4_ref, dlam_ref):
                r[...] = jnp.zeros_like(r)

        lam = lam_ref[...]

        def chunk1(i, _):
            rows = pl.ds(pl.multiple_of(i * TC, TC), TC)
            _, vjp_y = jax.vjp(lambda h, g: h * jax.nn.gelu(g), hf_ref[rows, :] + hb_ref[rows, :], xg_ref[rows, :])
            dhs, dxg = vjp_y(dy_ref[rows, :])
            dhs_s[rows, :] = dhs
            dxg_ref[rows, :] = dxg
            return 0

        lax.fori_loop(0, N_TC, chunk1, 0)
        t = lax.broadcasted_iota(jnp.int32, (TP, CG), 0)
        af_s[...] = pltpu.roll(af_ref[...], TP - 1, 0)
        ab_s[...] = pltpu.roll(ab_ref[...], 1, 0)
        _scan_pair(ab_s, dhs_s, lb_s, af_s, dhs_s, lf_s, dab_s, daf_s)
        daf_s[...] = lf_s[...] * jnp.where(t >= 1, pltpu.roll(hf_ref[...], 1, 0), 0.0)
        dab_s[...] = lb_s[...] * jnp.where(t < TP - 1, pltpu.roll(hb_ref[...], TP - 1, 0), 0.0)

        def chunk2(i, _):
            rows = pl.ds(pl.multiple_of(i * TC, TC), TC)
            xc = xc_s[rows, :]
            xcb = xc.astype(BF)
            pre = _nn(xcb, w4_ref[...]) + b4_ref[...]
            dxc, dpres, dlams = _gates_bwd(i * TC, xc, _split4(pre), (lam[:, :CG], lam[:, CG:]),
                                           (daf_s[rows, :], lf_s[rows, :], dab_s[rows, :], lb_s[rows, :]))
            dpre = jnp.concatenate(dpres, axis=1)
            dpreb = dpre.astype(BF)
            dxc_s[rows, :] = dxc + _nt(dpreb, w4_ref[...])
            dw4_ref[...] += _tn(xcb, dpreb)
            db4_ref[...] += jnp.sum(dpre, axis=0, keepdims=True)
            dlam_ref[...] += jnp.concatenate(dlams, axis=1)
            return 0

        lax.fori_loop(0, N_TC, chunk2, 0)
        dxc = dxc_s[...]
        x = xr_ref[...]
        taps = (jnp.where(t < TP - 2, pltpu.roll(dxc, TP - 2, 0), 0.0), jnp.where(t < TP - 1, pltpu.roll(dxc, TP - 1, 0), 0.0),
                dxc, jnp.where(t >= 1, pltpu.roll(dxc, 1, 0), 0.0))
        dcb_ref[...] += jnp.sum(dxc, axis=0, keepdims=True)
        dxr = jnp.zeros_like(dxc)
        for tap, shifted in enumerate(taps):
            dcw_ref[tap:tap + 1, :] += jnp.sum(x * shifted, axis=0, keepdims=True)
            dxr = dxr + cw_ref[tap:tap + 1, :] * shifted
        dxr_ref[...] = dxr

    sp = _rnn_specs()
    return pl.pallas_call(
        body, grid=(N_CG, NB), name="rnn_bwd",
        in_specs=[sp["seq"]] * 8 + [sp["cw"], sp["cb"], sp["w4"], sp["b4"], sp["lam"]],
        out_specs=[sp["seq"], sp["seq"], sp["cw"], sp["cb"], sp["w4"], sp["b4"], sp["lam"]],
        out_shape=[jax.ShapeDtypeStruct((R, D_RNN), F32), jax.ShapeDtypeStruct((R, D_RNN), F32),
                   jax.ShapeDtypeStruct((CONV_W, D_RNN), F32), jax.ShapeDtypeStruct((1, D_RNN), F32),
                   jax.ShapeDtypeStruct((N_CG, CG, 4 * CG), F32), jax.ShapeDtypeStruct((N_CG, 1, 4 * CG), F32),
                   jax.ShapeDtypeStruct((N_CG, 1, 2 * CG), F32)],
        scratch_shapes=[pltpu.VMEM((TP, CG), F32)] * 8,
        compiler_params=_params("arbitrary", "arbitrary"),
    )(dy, xr, xg, hf, hb, af, ab, xc, cw["conv_w"], cw["conv_b"], cw["w4"], cw["b4"], cw["lam"])


TD = 256
STAGE_D_VMEM = 58 * 1024 * 1024


def _stage_d(hp, o, y, tgt, cw):
    def body(hp_ref, o_ref, y_ref, tgt_ref, ga, gr, wout, ln2, wg, wu, wd,
             do_ref, dy_ref, dh1_ref, mix_ref, dh1b_ref, hn2_ref, dg_ref, du_ref, act_ref, dh2b_ref,
             loss_ref, dga_ref, dgr_ref, dln2_ref):
        i = pl.program_id(0)

        @pl.when(i == 0)
        def _():
            for r in (loss_ref, dga_ref, dgr_ref, dln2_ref):
                r[...] = jnp.zeros_like(r)

        mix_a, vjp_a = jax.vjp(lambda x, g: _rms(x, g, D_ATTN), o_ref[...], ga[...])
        mix_r, vjp_r = jax.vjp(lambda x, g: _rms(x, g, D_RNN), y_ref[...], gr[...])
        mab, mrb = mix_a.astype(BF), mix_r.astype(BF)
        mix_ref[:, :D_ATTN] = mab
        mix_ref[:, D_ATTN:] = mrb
        h1 = hp_ref[...] + _nn(mab, wout[:D_ATTN, :]) + _nn(mrb, wout[D_ATTN:, :])
        hn2, vjp_ln2 = jax.vjp(lambda x, g: _rms(x, g, D), h1, ln2[...])
        hn2b = hn2.astype(BF)
        hn2_ref[...] = hn2b
        act, vjp_act = jax.vjp(lambda g, u: jax.nn.silu(g) * u, _nt(hn2b, wg[...]), _nt(hn2b, wu[...]))
        actb = act.astype(BF)
        act_ref[...] = actb
        h2 = h1 + _nn(actb, wd[...])
        row = i * TD + lax.broadcasted_iota(jnp.int32, (TD, 1), 0)
        t = jnp.where(row >= TP, row - TP, row)
        err = jnp.where((t >= N_META) & (t < T), h2 - tgt_ref[...], 0.0)
        loss_ref[...] += jnp.sum(err * err) * (0.5 / D)
        dh2b = (err * (1.0 / D)).astype(BF)
        dh2b_ref[...] = dh2b
        dg, du = vjp_act(_nt(dh2b, wd[...]))
        dgb, dub = dg.astype(BF), du.astype(BF)
        dg_ref[...] = dgb
        du_ref[...] = dub
        dh1n, dln2 = vjp_ln2(_nn(dgb, wg[...]) + _nn(dub, wu[...]))
        dh1 = err * (1.0 / D) + dh1n
        dh1_ref[...] = dh1
        dh1b = dh1.astype(BF)
        dh1b_ref[...] = dh1b
        dmix = _nt(dh1b, wout[...])
        do, dga = vjp_a(dmix[:, :D_ATTN])
        dyr, dgr = vjp_r(dmix[:, D_ATTN:])
        do_ref[...] = do
        dy_ref[...] = dyr
        dga_ref[...] += dga
        dgr_ref[...] += dgr
        dln2_ref[...] += dln2

    rs = lambda n: _row_spec(n, TD)
    acc = lambda n: pl.BlockSpec((1, n), lambda i: (0, 0))
    return pl.pallas_call(
        body, grid=(R // TD,), name="stage_d",
        in_specs=[rs(D), rs(D_ATTN), rs(D_RNN), rs(D), _const_spec((1, D_ATTN)), _const_spec((1, D_RNN)),
                  _const_spec((D, D)), _const_spec((1, D)), _const_spec((D_FF, D)), _const_spec((D_FF, D)),
                  _const_spec((D_FF, D))],
        out_specs=[rs(D_ATTN), rs(D_RNN), rs(D), rs(D), rs(D), rs(D), rs(D_FF), rs(D_FF), rs(D_FF), rs(D),
                   acc(1), acc(D_ATTN), acc(D_RNN), acc(D)],
        out_shape=[jax.ShapeDtypeStruct((R, D_ATTN), F32), jax.ShapeDtypeStruct((R, D_RNN), F32),
                   jax.ShapeDtypeStruct((R, D), F32), jax.ShapeDtypeStruct((R, D), BF),
                   jax.ShapeDtypeStruct((R, D), BF), jax.ShapeDtypeStruct((R, D), BF),
                   jax.ShapeDtypeStruct((R, D_FF), BF), jax.ShapeDtypeStruct((R, D_FF), BF),
                   jax.ShapeDtypeStruct((R, D_FF), BF), jax.ShapeDtypeStruct((R, D), BF),
                   jax.ShapeDtypeStruct((1, 1), F32), jax.ShapeDtypeStruct((1, D_ATTN), F32),
                   jax.ShapeDtypeStruct((1, D_RNN), F32), jax.ShapeDtypeStruct((1, D), F32)],
        compiler_params=_params("arbitrary", vmem=STAGE_D_VMEM),
    )(hp, o, y, tgt, cw["ga"], cw["gr"], cw["wout"], cw["ln2_g"], cw["wg"], cw["wu"], cw["wd"])


TW = 2176


def _wgrad(a, b, name, tk=None):
    ka, nb = a.shape[1], b.shape[1]
    tk = ka if tk is None else tk

    def body(a_ref, b_ref, o_ref):
        @pl.when(pl.program_id(1) == 0)
        def _():
            o_ref[...] = jnp.zeros_like(o_ref)

        o_ref[...] += _tn(a_ref[...].astype(BF), b_ref[...].astype(BF))

    return pl.pallas_call(
        body, grid=(ka // tk, R // TW), name=name,
        in_specs=[pl.BlockSpec((TW, tk), lambda k, r: (r, k)), pl.BlockSpec((TW, nb), lambda k, r: (r, 0))],
        out_specs=pl.BlockSpec((tk, nb), lambda k, r: (k, 0)),
        out_shape=jax.ShapeDtypeStruct((ka, nb), F32),
        compiler_params=_params("arbitrary", "arbitrary"),
    )(a, b)


def _wgrad_heads(dq, dk, dv, cqn, ckvn):
    def body(dq_ref, dk_ref, dv_ref, cqn_ref, ckvn_ref, oq_ref, ok_ref, ov_ref):
        @pl.when(pl.program_id(0) == 0)
        def _():
            for r in (oq_ref, ok_ref, ov_ref):
                r[...] = jnp.zeros_like(r)

        ckvnb = ckvn_ref[...]
        oq_ref[...] += _tn(dq_ref[...], cqn_ref[...])
        ok_ref[...] += _tn(dk_ref[...], ckvnb)
        ov_ref[...] += _tn(dv_ref[...].astype(BF), ckvnb)

    rows = lambda a: pl.BlockSpec((TW, a.shape[1]), lambda r: (r, 0))
    full = lambda m, n: pl.BlockSpec((m, n), lambda r: (0, 0))
    shapes = [(dq.shape[1], cqn.shape[1]), (dk.shape[1], ckvn.shape[1]), (dv.shape[1], ckvn.shape[1])]
    return pl.pallas_call(
        body, grid=(R // TW,), name="wgrad_heads", in_specs=[rows(a) for a in (dq, dk, dv, cqn, ckvn)],
        out_specs=[full(*s) for s in shapes], out_shape=[jax.ShapeDtypeStruct(s, F32) for s in shapes],
        compiler_params=_params("arbitrary"),
    )(dq, dk, dv, cqn, ckvn)


def _rope_tables():
    half = QK_ROPE // 2
    freqs = 1.0 / (ROPE_THETA ** (jnp.arange(half, dtype=F32) / half))
    ang = jnp.arange(TP, dtype=F32)[:, None] * freqs[None, :]
    ones = jnp.ones((TP, QK_NOPE), F32)
    zeros = jnp.zeros((TP, QK_NOPE), F32)
    pad1 = jnp.ones((TP, HP - QK_HEAD), F32)
    pad0 = jnp.zeros((TP, HP - QK_HEAD), F32)
    cs = jnp.concatenate([ones, jnp.cos(ang), jnp.cos(ang), pad1], axis=1)
    sn = jnp.concatenate([zeros, jnp.sin(ang), jnp.sin(ang), pad0], axis=1)
    return jnp.tile(cs, (NB, 1)), jnp.tile(sn, (NB, 1))


def _pad_rows(a, lo, hi):
    return jnp.pad(a, ((0, 0), (lo, hi), (0, 0)))


def _pad_target(target):
    return _pad_rows(target, N_META, TP - T).reshape(R, D)


def _gate_weights(lru_wa, lru_wi):
    gates = jnp.stack([lru_wa[0], lru_wi[0], lru_wa[1], lru_wi[1]])
    blk = gates.reshape(4, N_CG, 2, RNN_BW, RNN_BW)
    dense = jnp.einsum("tcaij,ab->tcaibj", blk, jnp.eye(2, dtype=F32)).reshape(4, N_CG, CG, CG)
    return dense.transpose(1, 2, 0, 3).reshape(N_CG, CG, 4 * CG).astype(BF)


def _compute_weights(w):
    win_t = w["w_in_t"]
    kr = win_t[O_KR:O_KR + QK_ROPE]
    win = jnp.concatenate([win_t[:O_KR], jnp.zeros((QK_NOPE, D), F32), kr,
                           jnp.zeros((HP - QK_HEAD, D), F32), win_t[O_KR + QK_ROPE:]], axis=0)
    wq = _pad_rows(w["w_uq_t"].reshape(N_HEADS, QK_HEAD, Q_LORA), 0, HP - QK_HEAD)
    wkv = w["w_ukv_t"].reshape(N_HEADS, QK_NOPE + V_HEAD, KV_LORA)
    wk = _pad_rows(wkv[:, :QK_NOPE], 0, HP - QK_NOPE)
    wv = wkv[:, QK_NOPE:].reshape(D_ATTN, KV_LORA)
    bias =jnp.stack([w["lru_ba"][0], w["lru_bi"][0], w["lru_ba"][1], w["lru_bi"][1]])
    b4 = bias.reshape(4, N_CG, CG).transpose(1, 0, 2).reshape(N_CG, 1, 4 * CG)
    lam = w["lru_lambda"].reshape(2, N_CG, CG).transpose(1, 0, 2).reshape(N_CG, 1, 2 * CG)
    pad_g = lambda g: jnp.pad(g.reshape(1, QK_HEAD), ((0, 0), (0, HP - QK_HEAD)))
    return dict(
        ln1_g=w["ln1_g"].reshape(1, D), win=win.astype(BF), qa_g=w["q_a_norm_g"].reshape(1, Q_LORA),
        wq=wq.astype(BF).reshape(N_HEADS * HP, Q_LORA), kva_g=w["kv_a_norm_g"].reshape(1, KV_LORA),
        wk=wk.astype(BF).reshape(N_HEADS * HP, KV_LORA), wv=wv.astype(BF),
        q_g=pad_g(w["q_norm_g"]), k_g=pad_g(w["k_norm_g"]),
        conv_w=w["conv_w"].reshape(CONV_W, D_RNN), conv_b=w["conv_b"].reshape(1, D_RNN),
        w4=w["w4"] if "w4" in w else _gate_weights(w["lru_wa"], w["lru_wi"]), b4=b4, lam=lam,
        ga=w["attn_out_g"].reshape(1, D_ATTN), gr=w["rnn_out_g"].reshape(1, D_RNN), ln2_g=w["ln2_g"].reshape(1, D),
    )


def _local_step(x, target, meta, w, late_forward, late_weights, early_grads, mid_grads):
    cw = _compute_weights(w)
    cs, sn = w["rope"] if "rope" in w else _rope_tables()
    hp = jnp.concatenate([jnp.broadcast_to(meta[None], (NB, N_META, D)), x,
                          jnp.zeros((NB, TP - T, D), F32)], axis=1).reshape(R, D)
    tgt = target if target.ndim == 2 else _pad_target(target)

    pa, xr, xg, q, k, v = _stage_a_fwd(hp, cs, sn, cw)
    o, lse = _attn_fwd(q, k, v)
    cw["conv_b"] = cw["conv_b"] + late_forward([o])
    y, hf, hb, af, ab, xc = _rnn_fwd(xr, xg, cw)
    late = late_weights([y])
    cw.update(wout=late["w_out"], wg=late["w_gate_t"], wu=late["w_up_t"], wd=late["w_down"])
    (do, dy, dh1, mixb, dh1b, hn2b, dgb, dub, actb, dh2b, loss, dga, dgr, dln2) = _stage_d(hp, o, y, tgt, cw)
    dwout = _wgrad(mixb, dh1b, "wgrad_out")
    dwg = _wgrad(dgb, hn2b, "wgrad_gate", tk=D_FF // 2)
    dwu = _wgrad(dub, hn2b, "wgrad_up", tk=D_FF // 2)
    dwd = _wgrad(actb, dh2b, "wgrad_down", tk=D_FF // 2)
    zero = early_grads(dict(w_out=dwout, w_gate=dwg, w_up=dwu, w_down=dwd))
    cw["conv_b"] = cw["conv_b"] + zero
    dxr, dxg, dcw, dcb, dw4, db4, dlam = _rnn_bwd(dy, xr, xg, hf, hb, af, ab, xc, cw)
    zero = mid_grads([dxr])
    dq, dk, dv = _attn_bwd(q, k, v, o, lse, do)
    (dhp, dpb, dqrawb, dkrawb, hn1b, cqnb, ckvnb, dln1, dqag, dkvag, dqg, dkg) = _stage_a_bwd(
        dq, dk, dv, dxr, dxg, dh1, hp, pa, cs, sn, dict(cw, qa_g=cw["qa_g"] + zero))

    dwin = _wgrad(dpb, hn1b, "wgrad_in", tk=PC // 2)
    dwq, dwk, dwv = _wgrad_heads(dqrawb, dkrawb, dv, cqnb, ckvnb)

    dwin_t = jnp.concatenate([dwin[:O_KR], dwin[O_KR + QK_NOPE:O_KR + QK_HEAD], dwin[O_XR:]], axis=0)
    dwq_t = dwq.reshape(N_HEADS, HP, Q_LORA)[:, :QK_HEAD].reshape(N_HEADS * QK_HEAD, Q_LORA)
    dwkv_t = jnp.concatenate([dwk.reshape(N_HEADS, HP, KV_LORA)[:, :QK_NOPE],
                              dwv.reshape(N_HEADS, V_HEAD, KV_LORA)], axis=1).reshape(2 * D_ATTN, KV_LORA)
    d4 = dw4.reshape(N_CG, 2, RNN_BW, 4, 2, RNN_BW)
    dgates = jnp.stack([d4[:, 0, :, :, 0, :], d4[:, 1, :, :, 1, :]], axis=1)
    dgates = dgates.transpose(3, 0, 1, 2, 4).reshape(4, N_HEADS, RNN_BW, RNN_BW)
    dbias = db4.reshape(N_CG, 4, CG).transpose(1, 0, 2).reshape(4, D_RNN)
    dhp3 = dhp.reshape(NB, TP, D)
    grads = dict(
        meta_tokens=jnp.sum(dhp3[:, :N_META], axis=0),
        ln1_g=dln1, w_in_t=dwin_t, q_a_norm_g=dqag, w_uq_t=dwq_t, kv_a_norm_g=dkvag, w_ukv_t=dwkv_t,
        q_norm_g=dqg[:, :QK_HEAD], k_norm_g=dkg[:, :QK_HEAD], conv_w=dcw[None], conv_b=dcb,
        lru_wa=jnp.stack([dgates[0], dgates[2]])[None], lru_ba=jnp.stack([dbias[0], dbias[2]])[None],
        lru_wi=jnp.stack([dgates[1], dgates[3]])[None], lru_bi=jnp.stack([dbias[1], dbias[3]])[None],
        lru_lambda=dlam.reshape(N_CG, 2, CG).transpose(1, 0, 2).reshape(1, 2, D_RNN),
        attn_out_g=dga, rnn_out_g=dgr, ln2_g=dln2,
    )
    return loss[0, 0], dhp3[:, N_META:T], grads, [dhp, dwin]


_ANY = pl.BlockSpec(memory_space=pl.ANY)


def _place():
    return lax.axis_index("x"), lax.axis_index("y"), lax.axis_index("c")


def _other_chips(x, y):
    return [(1 - x, y), (x, 1 - y), (1 - x, 1 - y)]


def _pair_exchange(big, whole, name):
    n_s, _, m, n = big.shape
    n_copies = n_s + len(whole)

    def body(*refs):
        big_ref, whole_refs = refs[0], refs[1:1 + len(whole)]
        rbig_ref, rwhole_refs = refs[1 + len(whole)], refs[2 + len(whole):2 + 2 * len(whole)]
        send_sems, recv_sems = refs[-2:]
        x, y, c = _place()
        sibling = (x, y, 1 - c)
        copies = [pltpu.make_async_remote_copy(
            src_ref=big_ref.at[s, 1 - c], dst_ref=rbig_ref.at[s], send_sem=send_sems.at[s], recv_sem=recv_sems.at[s],
            device_id=sibling, device_id_type=MESH) for s in range(n_s)]
        copies += [pltpu.make_async_remote_copy(
            src_ref=a, dst_ref=r, send_sem=send_sems.at[n_s + i], recv_sem=recv_sems.at[n_s + i],
            device_id=sibling, device_id_type=MESH) for i, (a, r) in enumerate(zip(whole_refs, rwhole_refs))]
        for cp in copies:
            cp.start()
        for cp in copies:
            cp.wait()

    return pl.pallas_call(
        body, name=name,
        out_shape=[jax.ShapeDtypeStruct((n_s, m, n), big.dtype)] + [jax.ShapeDtypeStruct(a.shape, a.dtype) for a in whole],
        in_specs=[_ANY] * (1 + len(whole)), out_specs=[_ANY] * (1 + len(whole)),
        scratch_shapes=[pltpu.SemaphoreType.DMA((n_copies,)), pltpu.SemaphoreType.DMA((n_copies,))],
    )(big, *whole)


_HBM = pl.BlockSpec(memory_space=pltpu.HBM)
_SEM = pl.BlockSpec(memory_space=pltpu.SEMAPHORE)
_EFFECT = pltpu.SideEffectType.DATAFLOW_SIDE_EFFECTING


def _split_copies(src_refs, land_refs, sems, plan, sending):
    n = len(sems) // 2
    return [pltpu.make_async_remote_copy(src_ref=s, dst_ref=d, send_sem=sems[k], recv_sem=sems[n + k], device_id=to,
                                         device_id_type=MESH)
            for k, (s, d, to) in enumerate(plan(src_refs, land_refs, sending))]


def _to_chips(src_at, land_at):
    def plan(src_refs, land_refs, sending):
        x, y, c = _place()
        return [(src_at(s, tx, ty, c), land_at(l, j, *((x, y) if sending else (tx, ty)), c), (tx, ty, c))
                for s, l in zip(src_refs, land_refs) for j, (tx, ty) in enumerate(_other_chips(x, y))]
    return plan


def _to_sibling(src_refs, land_refs, sending):
    x, y, c = _place()
    return [(s.at[k, 1 - c], l.at[k], (x, y, 1 - c)) for s, l in zip(src_refs, land_refs) for k in range(N_CHIPS)]


def _split_start(name, srcs, lands, plan, n, after=()):
    srcs, lands, after = list(srcs), list(lands), list(after)
    k, kb = len(srcs), len(srcs) + len(lands)

    def body(*refs):
        outs = refs[kb + len(after):]
        for cp in _split_copies(refs[:k], refs[k:kb], outs[:2 * n], plan, True):
            cp.start()
        outs[2 * n + kb][...] = jnp.zeros_like(outs[2 * n + kb])

    outs = pl.pallas_call(
        body, name=name,
        out_shape=(pltpu.SemaphoreType.DMA(()),) * (2 * n) + tuple(pltpu.HBM(a.shape, a.dtype) for a in srcs + lands)
        + (jax.ShapeDtypeStruct((8, LANES), F32),),
        in_specs=(_HBM,) * kb + (_ANY,) * len(after),
        out_specs=(_SEM,) * (2 * n) + (_HBM,) * kb + (pl.BlockSpec(memory_space=pltpu.VMEM),),
        input_output_aliases={i: 2 * n + i for i in range(kb)},
        compiler_params=pltpu.CompilerParams(has_side_effects=_EFFECT),
    )(*[pltpu.with_memory_space_constraint(a, pltpu.HBM) for a in srcs + lands], *after)
    return outs[:2 * n], list(outs[2 * n:2 * n + k]), list(outs[2 * n + k:2 * n + kb]), outs[2 * n + kb]


def _split_wait(name, sems, srcs, lands, after, plan):
    srcs, lands = list(srcs), list(lands)
    k, kb = len(srcs), len(srcs) + len(lands)

    def body(*refs):
        for cp in _split_copies(refs[:k], refs[k:kb], refs[kb:kb + len(sems)], plan, False):
            cp.wait_send()
            cp.wait_recv()

    outs = pl.pallas_call(
        body, name=name, out_shape=tuple(pltpu.HBM(a.shape, a.dtype) for a in srcs + lands),
        in_specs=(_HBM,) * kb + (_SEM,) * len(sems) + (_ANY,) * len(after), out_specs=(_HBM,) * kb,
        input_output_aliases={i: i for i in range(kb)}, compiler_params=pltpu.CompilerParams(has_side_effects=_EFFECT),
    )(*srcs, *lands, *sems, *after)
    return list(outs[:k]), list(outs[k:])


def _forward_landed(src_refs, land_refs, sending):
    x, y, c = _place()
    copies = []
    for ref in src_refs:
        m = ref.shape[0] // 8
        for tx, ty in _other_chips(x, y):
            rows = ref.at[pl.ds((4 * tx + 2 * ty + (c if sending else 1 - c)) * m, m), :]
            copies.append((rows, rows, (x, y, 1 - c)))
    return copies


def _place_own(pieces):
    k = len(pieces)

    def body(*refs):
        piece_refs, out_refs, stages = refs[:k], refs[k:2 * k], refs[2 * k:3 * k]
        load_sems, store_sems = refs[3 * k:]
        x, y, _ = _place()
        loads = [pltpu.make_async_copy(piece_refs[a], stages[a], load_sems.at[a]) for a in range(k)]
        stores = [pltpu.make_async_copy(
            stages[a], out_refs[a].at[pl.ds((2 * x + y) * pieces[a].shape[0], pieces[a].shape[0]), :], store_sems.at[a])
            for a in range(k)]
        for cp in loads:
            cp.start()
        for ld, st in zip(loads, stores):
            ld.wait()
            st.start()
        for cp in stores:
            cp.wait()

    return pl.pallas_call(
        body, name="gather_late_place_own",
        out_shape=[jax.ShapeDtypeStruct((N_CHIPS * p.shape[0], p.shape[1]), p.dtype) for p in pieces],
        in_specs=[_ANY] * k, out_specs=[_ANY] * k,
        scratch_shapes=[pltpu.VMEM(p.shape, p.dtype) for p in pieces]
        + [pltpu.SemaphoreType.DMA((k,)), pltpu.SemaphoreType.DMA((k,))],
    )(*pieces)


def _gather_finish(lands, pieces, name):
    k = len(lands)

    def body(*refs):
        land_refs, piece_refs, out_refs, stages = refs[:k], refs[k:2 * k], refs[2 * k:3 * k], refs[3 * k:4 * k]
        send_sems, recv_sems, load_sems, store_sems = refs[4 * k:]
        x, y, c = _place()
        sibling = (x, y, 1 - c)
        remote, loads, stores, arrivals = [], [], [], []
        for a in range(k):
            m = lands[a].shape[0] // 8

            def rows(px, py, pc, ref, m=m):
                return ref.at[pl.ds((4 * px + 2 * py + pc) * m, m), :]

            for j, (tx, ty) in enumerate(_other_chips(x, y)):
                sems = dict(send_sem=send_sems.at[3 * a + j], recv_sem=recv_sems.at[3 * a + j], device_id=sibling,
                            device_id_type=MESH)
                remote.append(pltpu.make_async_remote_copy(
                    src_ref=rows(tx, ty, c, land_refs[a]), dst_ref=rows(tx, ty, c, out_refs[a]), **sems))
                arrivals.append(pltpu.make_async_remote_copy(
                    src_ref=rows(tx, ty, 1 - c, out_refs[a]), dst_ref=rows(tx, ty, 1 - c, out_refs[a]), **sems))
            for h in range(2):
                loads.append(pltpu.make_async_copy(piece_refs[a].at[pl.ds(h * m, m), :], stages[a].at[h],
                                                   load_sems.at[2 * a + h]))
                stores.append(pltpu.make_async_copy(stages[a].at[h], rows(x, y, h, out_refs[a]), store_sems.at[2 * a + h]))
        for cp in remote + loads:
            cp.start()
        for ld, st in zip(loads, stores):
            ld.wait()
            st.start()
        for cp, arrival in zip(remote, arrivals):
            cp.wait_send()
            arrival.wait_recv()
        for cp in stores:
            cp.wait()

    return pl.pallas_call(
        body, name=name, out_shape=[jax.ShapeDtypeStruct(a.shape, a.dtype) for a in lands],
        in_specs=[_ANY] * (2 * k), out_specs=[_ANY] * k, input_output_aliases={i: i for i in range(k)},
        scratch_shapes=[pltpu.VMEM((2, a.shape[0] // 8, a.shape[1]), a.dtype) for a in lands]
        + [pltpu.SemaphoreType.DMA((3 * k,)), pltpu.SemaphoreType.DMA((3 * k,)), pltpu.SemaphoreType.DMA((2 * k,)),
           pltpu.SemaphoreType.DMA((2 * k,))],
    )(*lands, *pieces)


def _pair_fill(bufs, name):
    k = len(bufs)

    def body(*refs):
        send_sems, recv_sems = refs[-2:]
        x, y, c = _place()
        copies = [pltpu.make_async_remote_copy(
            src_ref=refs[i].at[c], dst_ref=refs[k + i].at[c], send_sem=send_sems.at[i], recv_sem=recv_sems.at[i],
            device_id=(x, y, 1 - c), device_id_type=MESH) for i in range(k)]
        for cp in copies:
            cp.start()
        for i, cp in enumerate(copies):
            cp.wait_send()
            pltpu.make_async_remote_copy(
                src_ref=refs[i].at[1 - c], dst_ref=refs[k + i].at[1 - c], send_sem=send_sems.at[i],
                recv_sem=recv_sems.at[i], device_id=(x, y, 1 - c), device_id_type=MESH).wait_recv()

    return pl.pallas_call(
        body, name=name, out_shape=[jax.ShapeDtypeStruct(a.shape, a.dtype) for a in bufs], in_specs=[_ANY] * k,
        out_specs=[_ANY] * k, input_output_aliases={i: i for i in range(k)},
        scratch_shapes=[pltpu.SemaphoreType.DMA((k,)), pltpu.SemaphoreType.DMA((k,))],
    )(*bufs)


def _row_tile(rows, cap=512):
    for t in range(cap - cap % 8, 7, -8):
        if rows % t == 0:
            return t
    return rows


def _elementwise(fn, n_out, name, *arrs, out_dtype=F32):
    rows, cols = arrs[0].shape
    tr = _row_tile(rows)
    n_in = len(arrs)

    def body(*refs):
        outs = fn(*[r[...].astype(F32) for r in refs[:n_in]])
        for r, o in zip(refs[n_in:], outs):
            r[...] = o.astype(out_dtype)

    spec = pl.BlockSpec((tr, cols), lambda i: (i, 0))
    return pl.pallas_call(
        body, grid=(rows // tr,), name=name, in_specs=[spec] * n_in, out_specs=[spec] * n_out,
        out_shape=[jax.ShapeDtypeStruct((rows, cols), out_dtype)] * n_out, compiler_params=_params("arbitrary"),
    )(*arrs)


def _pair_sums(gpacks, rbigs, ci, name):
    k = len(gpacks)

    def body(c_ref, *refs):
        for g_ref, r_ref, o_ref in zip(refs[:k], refs[k:2 * k], refs[2 * k:]):
            o_ref[...] = (g_ref[...] + r_ref[...]).astype(BF)

    half = lambda a: pl.BlockSpec((None,) + a.shape[1:], lambda s, c: (s, 0, 0))
    return pl.pallas_call(
        body, name=name, out_shape=[jax.ShapeDtypeStruct(r.shape, BF) for r in rbigs],
        grid_spec=pltpu.PrefetchScalarGridSpec(
            num_scalar_prefetch=1, grid=(N_CHIPS,),
            in_specs=[pl.BlockSpec((None, None) + g.shape[2:], lambda s, c: (s, c[0], 0, 0)) for g in gpacks]
            + [half(r) for r in rbigs],
            out_specs=[half(r) for r in rbigs]),
        compiler_params=_params("arbitrary"),
    )(ci.reshape(1), *gpacks, *rbigs)


def _chip_sums(sums, landed, chip, ci, name):
    k = len(sums)

    def body(p_ref, *refs):
        for own_ref, land_ref, o_ref in zip(refs[:k], refs[k:2 * k], refs[2 * k:]):
            f = lambda v: v.astype(F32)
            o_ref[...] = _add4(f(own_ref[...]), f(land_ref[0]), f(land_ref[1]), f(land_ref[2]))[0]

    return pl.pallas_call(
        body, name=name, out_shape=[jax.ShapeDtypeStruct((2,) + s.shape[1:], F32) for s in sums],
        grid_spec=pltpu.PrefetchScalarGridSpec(
            num_scalar_prefetch=1, grid=(1,),
            in_specs=[pl.BlockSpec((None,) + s.shape[1:], lambda i, p: (p[0], 0, 0)) for s in sums]
            + [pl.BlockSpec(l.shape, lambda i, p: (0, 0, 0)) for l in landed],
            out_specs=[pl.BlockSpec((None,) + s.shape[1:], lambda i, p: (p[1], 0, 0)) for s in sums]),
        compiler_params=_params("arbitrary"),
    )(jnp.stack([chip, ci]), *sums, *landed)


def _add2(a, b):
    return (a + b,)


def _add4(own, r0, r1, r2):
    return ((own + r2) + (r0 + r1),)


def _adamw_rows(ws, gs, ms, vs, name):
    k = len(ws)
    steps = next(s for s in (4, 2, 1) if all(w.shape[0] % (8 * s) == 0 for w in ws))

    def body(*refs):
        for i in range(k):
            outs = _adamw_math(*[refs[j * k + i][...] for j in range(4)])
            for j, o in enumerate(outs):
                refs[(4 + j) * k + i][...] = o

    specs = [pl.BlockSpec((w.shape[0] // steps, w.shape[1]), lambda i: (i, 0)) for w in ws]
    return pl.pallas_call(
        body, grid=(steps,), name=name, in_specs=specs * 4, out_specs=specs * 3,
        out_shape=[jax.ShapeDtypeStruct(w.shape, F32) for w in ws] * 3, compiler_params=_params("arbitrary"),
    )(*ws, *gs, *ms, *vs)


def _adamw_small(ws, gs, ms, vs):
    k = len(ws)

    def body(*refs):
        for i in range(k):
            outs = _adamw_math(*[refs[j * k + i][...] for j in range(4)])
            for j, o in enumerate(outs):
                refs[(4 + j) * k + i][...] = o

    return pl.pallas_call(
        body, name="adamw_small", out_shape=[jax.ShapeDtypeStruct(w.shape, F32) for w in ws] * 3,
    )(*ws, *gs, *ms, *vs)


def _adamw_math(w, g, m, v):
    m = ADAM_B1 * m + (1.0 - ADAM_B1) * g
    v = ADAM_B2 * v + (1.0 - ADAM_B2) * (g * g)
    m_hat = m / (1.0 - ADAM_B1 ** ADAM_STEP)
    v_hat = v / (1.0 - ADAM_B2 ** ADAM_STEP)
    delta = -ADAM_LR * (m_hat / (jnp.sqrt(v_hat) + ADAM_EPS) + ADAM_WD * w)
    return delta, m, v


WEIGHTS = ["meta_tokens", "ln1_g", "w_in", "q_a_norm_g", "w_uq", "kv_a_norm_g", "w_ukv", "q_norm_g", "k_norm_g",
           "conv_w", "conv_b", "lru_wa", "lru_ba", "lru_wi", "lru_bi", "lru_lambda", "attn_out_g", "rnn_out_g",
           "w_out", "ln2_g", "w_gate", "w_up", "w_down"]
BIG = ["w_in", "w_uq", "w_ukv", "w_out", "w_gate", "w_up", "w_down"]
BIG_T = {"w_in": True, "w_uq": True, "w_ukv": True, "w_out": False, "w_gate": True, "w_up": True, "w_down": False}
BIG_ROWS = {"w_in": 424, "w_uq": 72, "w_ukv": 64, "w_out": 256, "w_gate": 704, "w_up": 704, "w_down": 704}
EARLY = ["w_in", "w_uq", "w_ukv"]
LATE = ["w_out", "w_gate", "w_up", "w_down"]
EARLY_ROWS = 576
SMALL_SHARDED = ["meta_tokens", "conv_w", "lru_ba", "lru_bi", "lru_lambda"]
SMALL = [n for n in WEIGHTS if n not in BIG]
SMALL_PACK_ROWS = 160


def _offsets(names):
    off, o = {}, 0
    for n in names:
        off[n] = o
        o += BIG_ROWS[n]
    return off


def _shard_pack(names, src, rows):
    parts = [_to_pack_piece(n, src[n]) for n in names]
    used = sum(BIG_ROWS[n] for n in names)
    if rows > used:
        parts.append(jnp.zeros((rows - used, D), F32))
    return jnp.concatenate(parts, axis=0)


def _grad_pack(names, g, rows):
    parts = [g[n].reshape(N_CHIPS, BIG_ROWS[n], D) for n in names]
    used = sum(BIG_ROWS[n] for n in names)
    if rows > used:
        parts.append(jnp.zeros((N_CHIPS, rows - used, D), F32))
    return jnp.concatenate(parts, axis=1).reshape(N_CHIPS, 2, rows // 2, D)


def _to_pack_piece(name, shard):
    a = shard[0].T if BIG_T[name] else shard[0]
    return a.reshape(BIG_ROWS[name], D)


def _flat_pack(arrs, rows):
    flat = jnp.concatenate([a.reshape(-1) for a in arrs])
    return jnp.pad(flat, (0, rows * D - flat.shape[0])).reshape(rows, D)


def _flat_unpack(pack, shapes):
    flat, out, o = pack.reshape(-1), [], 0
    for s in shapes:
        n = math.prod(s)
        out.append(flat[o:o + n].reshape(s))
        o += n
    return out


def kernel(x, meta_tokens, ln1_g, w_in, q_a_norm_g, w_uq, kv_a_norm_g, w_ukv, q_norm_g, k_norm_g, conv_w, conv_b, lru_wa, lru_ba, lru_wi, lru_bi, lru_lambda, attn_out_g, rnn_out_g, w_out, ln2_g, w_gate, w_up, w_down, loss_target, m_meta_tokens, m_ln1_g, m_w_in, m_q_a_norm_g, m_w_uq, m_kv_a_norm_g, m_w_ukv, m_q_norm_g, m_k_norm_g, m_conv_w, m_conv_b, m_lru_wa, m_lru_ba, m_lru_wi, m_lru_bi, m_lru_lambda, m_attn_out_g, m_rnn_out_g, m_w_out, m_ln2_g, m_w_gate, m_w_up, m_w_down, v_meta_tokens, v_ln1_g, v_w_in, v_q_a_norm_g, v_w_uq, v_kv_a_norm_g, v_w_ukv, v_q_norm_g, v_k_norm_g, v_conv_w, v_conv_b, v_lru_wa, v_lru_ba, v_lru_wi, v_lru_bi, v_lru_lambda, v_attn_out_g, v_rnn_out_g, v_w_out, v_ln2_g, v_w_gate, v_w_up, v_w_down):
    wts = dict(zip(WEIGHTS, (meta_tokens, ln1_g, w_in, q_a_norm_g, w_uq, kv_a_norm_g, w_ukv, q_norm_g, k_norm_g, conv_w, conv_b, lru_wa, lru_ba, lru_wi, lru_bi, lru_lambda, attn_out_g, rnn_out_g, w_out, ln2_g, w_gate, w_up, w_down)))
    mom = dict(zip(WEIGHTS, (m_meta_tokens, m_ln1_g, m_w_in, m_q_a_norm_g, m_w_uq, m_kv_a_norm_g, m_w_ukv, m_q_norm_g, m_k_norm_g, m_conv_w, m_conv_b, m_lru_wa, m_lru_ba, m_lru_wi, m_lru_bi, m_lru_lambda, m_attn_out_g, m_rnn_out_g, m_w_out, m_ln2_g, m_w_gate, m_w_up, m_w_down)))
    var = dict(zip(WEIGHTS, (v_meta_tokens, v_ln1_g, v_w_in, v_q_a_norm_g, v_w_uq, v_kv_a_norm_g, v_w_ukv, v_q_norm_g, v_k_norm_g, v_conv_w, v_conv_b, v_lru_wa, v_lru_ba, v_lru_wi, v_lru_bi, v_lru_lambda, v_attn_out_g, v_rnn_out_g, v_w_out, v_ln2_g, v_w_gate, v_w_up, v_w_down)))
    xi, yi, ci = _place()
    chip = 2 * xi + yi
    off_e = _offsets(EARLY)
    half_e = EARLY_ROWS // 2
    gather_plan = _to_chips(lambda ref, tx, ty, c: ref.at[pl.ds(c * (ref.shape[0] // 2), ref.shape[0] // 2), :],
                            lambda ref, j, px, py, c: ref.at[pl.ds((4 * px + 2 * py + c) * (ref.shape[0] // 8),
                                                                   ref.shape[0] // 8), :])
    scatter_plan = _to_chips(lambda ref, tx, ty, c: ref.at[2 * tx + ty], lambda ref, j, px, py, c: ref.at[j])
    everywhere = _to_chips(lambda ref, tx, ty, c: ref, lambda ref, j, px, py, c: ref.at[j])
    n_late = len(LATE)

    pack_e = _shard_pack(EARLY, wts, EARLY_ROWS).astype(BF)
    spack = jnp.concatenate([meta_tokens[:, :LANES], meta_tokens[:, LANES:], conv_w[0], lru_ba[0], lru_bi[0],
                             lru_lambda[0], jnp.zeros((6, LANES), F32)], axis=0)
    sems_g, src_g, land_g, _ = _split_start(
        "gather_early_start", [pack_e, spack], [lax.empty((N_CHIPS * EARLY_ROWS, D), BF), lax.empty((N_CHIPS * 48, LANES), F32)],
        gather_plan, 6)
    tgt_padded = _pad_target(loss_target)
    pieces_l = [_to_pack_piece(n, wts[n]).astype(BF) for n in LATE]
    lands_l = list(_place_own(pieces_l))
    w4, rope = _gate_weights(lru_wa[0], lru_wi[0]), _rope_tables()
    src_g, land_g = _split_wait("gather_early_wait", sems_g, src_g, land_g, [tgt_padded, w4, *rope] + lands_l, gather_plan)
    ge, gs = _gather_finish(land_g, src_g, "gather_early_finish")
    ge = ge.reshape(N_CHIPS, EARLY_ROWS, D)
    gs = gs.reshape(N_CHIPS, 48, LANES)
    full = {n: ge[:, off_e[n]:off_e[n] + BIG_ROWS[n]] for n in EARLY}
    sems_l, src_l, land_l, tied = _split_start("gather_late_start", pieces_l, lands_l, gather_plan, 3 * n_late, after=[ge])

    forward, pair, late = {}, {}, {}

    def late_forward(after):
        _, landed = _split_wait("gather_late_wait", sems_l, src_l, land_l, after, gather_plan)
        forward["sems"], forward["src"], _, zeros = _split_start(
            "gather_late_forward_start", landed, [], _forward_landed, 3 * n_late)
        return zeros[0, 0]

    def late_weights(after):
        (w_out_, w_gate_, w_up_, w_down_), _ = _split_wait(
            "gather_late_forward_wait", forward["sems"], forward["src"], [], after, _forward_landed)
        return dict(w_out=w_out_, w_gate_t=w_gate_, w_up_t=w_up_, w_down=w_down_)

    def early_grads(g_late):
        halves = [g_late[n].reshape(N_CHIPS, 2, BIG_ROWS[n] // 2, D) for n in LATE]
        pair["sems"], pair["src"], pair["land"], zeros = _split_start(
            "grad_pair_late_start", halves, [lax.empty((N_CHIPS, BIG_ROWS[n] // 2, D), F32) for n in LATE], _to_sibling,
            N_CHIPS * n_late)
        return zeros[0, 0]

    def mid_grads(after):
        halves, landed = _split_wait("grad_pair_late_wait", pair["sems"], pair["src"], pair["land"], after, _to_sibling)
        chip_sums = _pair_sums(halves, landed, ci, "grad_pair_sum_late")
        late["sems"], late["src"], late["land"], zeros = _split_start(
            "grad_chip_late_start", chip_sums, [lax.empty((3, BIG_ROWS[n] // 2, D), BF) for n in LATE], scatter_plan,
            3 * n_late)
        return zeros[0, 0]

    cols = lambda a: a.transpose(1, 0, 2).reshape(a.shape[1], N_CHIPS * a.shape[2])
    meta_full = cols(jnp.concatenate([gs[:, 0:16], gs[:, 16:32]], axis=2))
    w = dict(
        w_in_t=full["w_in"].reshape(IN_COLS, D), w_uq_t=full["w_uq"].reshape(N_HEADS * QK_HEAD, Q_LORA),
        w_ukv_t=full["w_ukv"].reshape(2 * D_ATTN, KV_LORA),
        ln1_g=ln1_g, q_a_norm_g=q_a_norm_g, kv_a_norm_g=kv_a_norm_g, q_norm_g=q_norm_g, k_norm_g=k_norm_g,
        conv_w=cols(gs[:, 32:36]), conv_b=conv_b, lru_wa=lru_wa[0], lru_ba=cols(gs[:, 36:38]), lru_wi=lru_wi[0],
        lru_bi=cols(gs[:, 38:40]), lru_lambda=cols(gs[:, 40:42]), attn_out_g=attn_out_g, rnn_out_g=rnn_out_g,
        ln2_g=ln2_g, w4=w4, rope=rope,
    )

    loss_local, grad_x, g, last = _local_step(x, tgt_padded, meta_full + tied[0, 0], w, late_forward, late_weights,
                                              early_grads, mid_grads)

    gpack = _grad_pack(EARLY, {"w_in": g["w_in_t"], "w_uq": g["w_uq_t"], "w_ukv": g["w_ukv_t"]}, EARLY_ROWS)
    full_shapes = {n: wts[n].shape for n in SMALL}
    full_shapes.update(meta_tokens=(N_META, D), conv_w=(1, CONV_W, D_RNN), lru_ba=(1, 2, D_RNN), lru_bi=(1, 2, D_RNN),
                       lru_lambda=(1, 2, D_RNN))
    gsmall = _flat_pack([g[n] for n in SMALL] + [loss_local], SMALL_PACK_ROWS)
    rbig, rsmall = _pair_exchange(gpack, [gsmall], "grad_pair_exchange")
    chip_big = _pair_sums([gpack], [rbig], ci, "grad_pair_sum")
    (chip_small,) = _elementwise(_add2, 1, "grad_pair_sum_small", gsmall, rsmall)
    early_plan = lambda srcs, lands, sending: (scatter_plan(srcs[:1], lands[:1], sending)
                                               + everywhere(srcs[1:], lands[1:], sending))
    sems_e, src_e, land_e, zero_e = _split_start(
        "grad_chip_early_start", list(chip_big) + [chip_small],
        [lax.empty((3, half_e, D), BF), lax.empty((3, SMALL_PACK_ROWS, D), F32)], early_plan, 6)

    grads, delta, new_m, new_v = {}, {}, {}, {}

    def adamw_big(names, gshards):
        as_rows = lambda n, a: a[0].T if BIG_T[n] else a[0]
        back = lambda n, a: a.T[None] if BIG_T[n] else a[None]
        ws, ms, vs = ([as_rows(n, src[n]) for n in names] for src in (wts, mom, var))
        g2 = [gs.reshape(w.shape) for w, gs in zip(ws, gshards)]
        outs = _adamw_rows(ws, g2, ms, vs, "adamw_" + names[0])
        for i, n in enumerate(names):
            grads[n] = back(n, g2[i])
            delta[n], new_m[n], new_v[n] = (back(n, outs[j * len(names) + i]) for j in range(3))
        return outs[0]

    sums, landed = _split_wait("grad_chip_late_wait", late["sems"], late["src"], late["land"], last + [zero_e], scatter_plan)
    shards_l = _pair_fill(_chip_sums(sums, landed, chip, ci, "grad_chip_sum_late"), "grad_pair_fill_late")
    done_late = adamw_big(LATE, shards_l)
    (src_e, src_s), (land_e, land_s) = _split_wait("grad_chip_early_wait", sems_e, src_e, land_e, [done_late, grad_x],
                                                   early_plan)
    (shard_e,) = _pair_fill(_chip_sums([src_e], [land_e], chip, ci, "grad_chip_sum"), "grad_pair_fill_early")
    shard_e = shard_e.reshape(EARLY_ROWS, D)
    for n in EARLY:
        adamw_big([n], [shard_e[off_e[n]:off_e[n] + BIG_ROWS[n]]])
    (small_sum,) = _elementwise(_add4, 1, "grad_chip_sum_small", src_s, land_s[0], land_s[1], land_s[2])
    *small_grads, loss = _flat_unpack(small_sum, [full_shapes[n] for n in SMALL] + [()])
    small_full = dict(zip(SMALL, small_grads))
    for n in SMALL:
        a = small_full[n]
        if n in SMALL_SHARDED:
            width = wts[n].shape[-1]
            a = lax.dynamic_slice_in_dim(a, chip * width, width, axis=a.ndim - 1)
        grads[n] = a.reshape(wts[n].shape)

    rows_of = lambda a: a.reshape(-1, a.shape[-1])
    outs = _adamw_small(*[[rows_of(src[n]) for n in SMALL] for src in (wts, grads, mom, var)])
    for j, dst in enumerate((delta, new_m, new_v)):
        dst.update({n: outs[j * len(SMALL) + i].reshape(wts[n].shape) for i, n in enumerate(SMALL)})

    return (loss, grad_x, *[grads[n] for n in WEIGHTS], *[delta[n] for n in WEIGHTS],
            *[new_m[n] for n in WEIGHTS], *[new_v[n] for n in WEIGHTS])
```

```python
import math

import jax
import jax.numpy as jnp
from jax import lax
from jax.experimental import pallas as pl
from jax.experimental.pallas import tpu as pltpu

F32 = jnp.float32
BF = jnp.bfloat16
MESH = pl.DeviceIdType.MESH

D = 1024
SEQ = 2048
N_META = 16
T = N_META + SEQ
N_HEADS = 8
QK_NOPE = 64
QK_ROPE = 32
QK_HEAD = 96
V_HEAD = 64
Q_LORA = 384
KV_LORA = 256
D_ATTN = 512
D_RNN = 512
RNN_BW = 64
CONV_W = 4
LRU_C = 8.0
ROPE_THETA = 10000.0
D_FF = 2816
EPS = 1e-6
IN_COLS = 1696
ADAM_LR, ADAM_B1, ADAM_B2, ADAM_EPS, ADAM_WD, ADAM_STEP = 0.001, 0.9, 0.999, 1e-08, 0.01, 10

LANES = 128
TP = 2176
NB = 2
R = NB * TP
TR = 256
TRF = 256
TQ = 1088
HP = LANES
PC = 1792
O_CKV, O_KR, O_XR, O_XG = 384, 640, 768, 1280
CG = 128
N_CG = D_RNN // CG
VMEM_LIMIT = 56 * 1024 * 1024
N_CHIPS = 4
SCALE = QK_HEAD ** -0.5
KEY_MASK = -30000.0
LOG2_E = 1.4426950408889634
SCALE_LOG2 = SCALE * LOG2_E


def _nt(a, b):
    return lax.dot_general(a, b, (((1,), (1,)), ((), ())), preferred_element_type=F32)


def _nn(a, b):
    return jnp.dot(a, b, preferred_element_type=F32)


def _tn(a, b):
    return lax.dot_general(a, b, (((0,), (0,)), ((), ())), preferred_element_type=F32)


def _rms(x, g, n):
    ms = jnp.sum(x * x, axis=-1, keepdims=True) * (1.0 / n)
    return x * lax.rsqrt(ms + EPS) * g


def _lane_sum(y):
    return jnp.sum(y, axis=-1, keepdims=True)


def _rot(x):
    lane = lax.broadcasted_iota(jnp.int32, x.shape, 1)
    left = pltpu.roll(x, HP - 16, 1)
    right = pltpu.roll(x, 16, 1)
    lo = (lane >= QK_NOPE) & (lane < QK_NOPE + 16)
    hi = (lane >= QK_NOPE + 16) & (lane < QK_HEAD)
    return jnp.where(lo, -left, jnp.where(hi, right, 0.0))


def _head(x, g, cs, sn):
    n = x * lax.rsqrt(_lane_sum(x * x) * (1.0 / QK_HEAD) + EPS) * g
    return n * cs + _rot(n) * sn


def _head_bwd(x, g, cs, sn, dout):
    rs = lax.rsqrt(_lane_sum(x * x) * (1.0 / QK_HEAD) + EPS)
    xh = x * rs
    dn = dout * cs - _rot(dout * sn)
    gdn = g * dn
    t = _lane_sum(gdn * xh) * (1.0 / QK_HEAD)
    return rs * (gdn - xh * t), jnp.sum(dn * xh, axis=0, keepdims=True)


def _const_spec(shape):
    return pl.BlockSpec(shape, lambda *_: (0,) * len(shape), pipeline_mode=pl.Buffered(1))


def _row_spec(n, tr=TR):
    return pl.BlockSpec((tr, n), lambda i: (i, 0))


def _params(*sem, vmem=VMEM_LIMIT):
    return pltpu.CompilerParams(dimension_semantics=sem, vmem_limit_bytes=vmem)


def _stage_a_fwd(hp, cs, sn, cw):
    def body(hp_ref, cs_ref, sn_ref, ln1, win, qag, wq, kvag, wk, wv, qg, kg,
             pa_ref, xr_ref, xg_ref, q_ref, k_ref, v_ref):
        hn = _rms(hp_ref[...], ln1[...], D).astype(BF)
        p = _nt(hn, win[...])
        pa_ref[...] = p[:, :O_XR]
        xr_ref[...] = p[:, O_XR:O_XG]
        xg_ref[...] = p[:, O_XG:]
        cqn = _rms(p[:, :O_CKV], qag[...], Q_LORA).astype(BF)
        ckvn = _rms(p[:, O_CKV:O_KR], kvag[...], KV_LORA).astype(BF)
        kr = p[:, O_KR:O_XR]
        c, s = cs_ref[...], sn_ref[...]
        mask_lane = lax.broadcasted_iota(jnp.int32, (1, HP), 1) == QK_HEAD
        row = pl.program_id(0) * TRF + lax.broadcasted_iota(jnp.int32, (TRF, 1), 0)
        key_mask = jnp.where(jnp.where(row >= TP, row - TP, row) < T, 0.0, KEY_MASK)
        qraw = _nt(cqn, wq[...])
        kraw = _nt(ckvn, wk[...])
        for h in range(N_HEADS):
            sl = slice(h * HP, (h + 1) * HP)
            q_ref[:, sl] = jnp.where(mask_lane, 1.0, _head(qraw[:, sl], qg[...], c, s)).astype(BF)
            k_ref[:, sl] = jnp.where(mask_lane, key_mask, _head(kraw[:, sl] + kr, kg[...], c, s)).astype(BF)
        v_ref[...] = _nt(ckvn, wv[...]).astype(BF)

    rs = lambda n: _row_spec(n, TRF)
    return pl.pallas_call(
        body, grid=(R // TRF,), name="stage_a_fwd",
        in_specs=[rs(D), rs(HP), rs(HP), _const_spec((1, D)), _const_spec((PC, D)),
                  _const_spec((1, Q_LORA)), _const_spec((N_HEADS * HP, Q_LORA)), _const_spec((1, KV_LORA)),
                  _const_spec((N_HEADS * HP, KV_LORA)), _const_spec((D_ATTN, KV_LORA)), _const_spec((1, HP)),
                  _const_spec((1, HP))],
        out_specs=[rs(O_XR), rs(D_RNN), rs(D_RNN), rs(N_HEADS * HP), rs(N_HEADS * HP), rs(D_ATTN)],
        out_shape=[jax.ShapeDtypeStruct((R, O_XR), F32), jax.ShapeDtypeStruct((R, D_RNN), F32),
                   jax.ShapeDtypeStruct((R, D_RNN), F32), jax.ShapeDtypeStruct((R, N_HEADS * HP), BF),
                   jax.ShapeDtypeStruct((R, N_HEADS * HP), BF), jax.ShapeDtypeStruct((R, D_ATTN), BF)],
        compiler_params=_params("arbitrary"),
    )(hp, cs, sn, cw["ln1_g"], cw["win"], cw["qa_g"], cw["wq"], cw["kva_g"], cw["wk"], cw["wv"], cw["q_g"], cw["k_g"])


def _stage_a_bwd(dq, dk, dv, dxr, dxg, dh1, hp, pa, cs, sn, cw):
    def body(dq_ref, dk_ref, dv_ref, dxr_ref, dxg_ref, dh1_ref, hp_ref, pa_ref, cs_ref, sn_ref,
             ln1, win, qag, wq, kvag, wk, wv, qg, kg,
             dhp_ref, dp_ref, dqraw_ref, dkraw_ref, hn_ref, cqn_ref, ckvn_ref,
             dln1_ref, dqag_ref, dkvag_ref, dqg_ref, dkg_ref):
        @pl.when(pl.program_id(0) == 0)
        def _():
            for r in (dln1_ref, dqag_ref, dkvag_ref, dqg_ref, dkg_ref):
                r[...] = jnp.zeros_like(r)

        hn, vjp_ln1 = jax.vjp(lambda h, g: _rms(h, g, D), hp_ref[...], ln1[...])
        hn_ref[...] = hn.astype(BF)
        pa_v = pa_ref[...]
        cqn, vjp_qa = jax.vjp(lambda x, g: _rms(x, g, Q_LORA), pa_v[:, :O_CKV], qag[...])
        ckvn, vjp_kva = jax.vjp(lambda x, g: _rms(x, g, KV_LORA), pa_v[:, O_CKV:O_KR], kvag[...])
        kr = pa_v[:, O_KR:O_XR]
        cqnb, ckvnb = cqn.astype(BF), ckvn.astype(BF)
        cqn_ref[...] = cqnb
        ckvn_ref[...] = ckvnb
        c, s = cs_ref[...], sn_ref[...]
        lane = lax.broadcasted_iota(jnp.int32, (1, HP), 1)
        rope_lanes = ((lane >= QK_NOPE) & (lane < QK_HEAD)).astype(F32)
        dkr = jnp.zeros((TR, HP), F32)
        dqg = jnp.zeros((1, HP), F32)
        dkg = jnp.zeros((1, HP), F32)
        qraw = _nt(cqnb, wq[...])
        kraw = _nt(ckvnb, wk[...])
        for h in range(N_HEADS):
            sl = slice(h * HP, (h + 1) * HP)
            dqraw, dg = _head_bwd(qraw[:, sl], qg[...], c, s, dq_ref[:, sl])
            dqg = dqg + dg
            dqraw_ref[:, sl] = dqraw.astype(BF)
            dkraw, dg = _head_bwd(kraw[:, sl] + kr, kg[...], c, s, dk_ref[:, sl])
            dkg = dkg + dg
            dkraw_ref[:, sl] = dkraw.astype(BF)
            dkr = dkr + dkraw * rope_lanes
        dcq, dqag = vjp_qa(_nn(dqraw_ref[...], wq[...]))
        dckv, dkvag = vjp_kva(_nn(dkraw_ref[...], wk[...]) + _nn(dv_ref[...].astype(BF), wv[...]))
        dpb = jnp.concatenate([dcq, dckv, dkr, dxr_ref[...], dxg_ref[...]], axis=1).astype(BF)
        dp_ref[...] = dpb
        dh, dln1 = vjp_ln1(_nn(dpb, win[...]))
        dhp_ref[...] = dh + dh1_ref[...]
        dln1_ref[...] += dln1
        dqag_ref[...] += dqag
        dkvag_ref[...] += dkvag
        dqg_ref[...] += dqg
        dkg_ref[...] += dkg

    acc = lambda n: pl.BlockSpec((1, n), lambda i: (0, 0))
    return pl.pallas_call(
        body, grid=(R // TR,), name="stage_a_bwd",
        in_specs=[_row_spec(N_HEADS * HP), _row_spec(N_HEADS * HP), _row_spec(D_ATTN), _row_spec(D_RNN),
                  _row_spec(D_RNN), _row_spec(D), _row_spec(D), _row_spec(O_XR), _row_spec(HP), _row_spec(HP),
                  _const_spec((1, D)), _const_spec((PC, D)), _const_spec((1, Q_LORA)),
                  _const_spec((N_HEADS * HP, Q_LORA)), _const_spec((1, KV_LORA)),
                  _const_spec((N_HEADS * HP, KV_LORA)), _const_spec((D_ATTN, KV_LORA)), _const_spec((1, HP)),
                  _const_spec((1, HP))],
        out_specs=[_row_spec(D), _row_spec(PC), _row_spec(N_HEADS * HP), _row_spec(N_HEADS * HP), _row_spec(D),
                   _row_spec(Q_LORA), _row_spec(KV_LORA), acc(D), acc(Q_LORA), acc(KV_LORA), acc(HP), acc(HP)],
        out_shape=[jax.ShapeDtypeStruct((R, D), F32), jax.ShapeDtypeStruct((R, PC), BF),
                   jax.ShapeDtypeStruct((R, N_HEADS * HP), BF), jax.ShapeDtypeStruct((R, N_HEADS * HP), BF),
                   jax.ShapeDtypeStruct((R, D), BF), jax.ShapeDtypeStruct((R, Q_LORA), BF),
                   jax.ShapeDtypeStruct((R, KV_LORA), BF), jax.ShapeDtypeStruct((1, D), F32),
                   jax.ShapeDtypeStruct((1, Q_LORA), F32), jax.ShapeDtypeStruct((1, KV_LORA), F32),
                   jax.ShapeDtypeStruct((1, HP), F32), jax.ShapeDtypeStruct((1, HP), F32)],
        compiler_params=_params("arbitrary"),
    )(dq, dk, dv, dxr, dxg, dh1, hp, pa, cs, sn, cw["ln1_g"], cw["win"], cw["qa_g"], cw["wq"], cw["kva_g"],
      cw["wk"], cw["wv"], cw["q_g"], cw["k_g"])


def _head_mask(half, dtype):
    lane = lax.broadcasted_iota(jnp.int32, (1, 2 * V_HEAD), 1)
    return ((lane >= V_HEAD) == (half == 1)).astype(dtype)


def _attn_specs(tq):
    n_q = TP // tq
    return (NB, N_HEADS // 2, n_q), dict(
        q=pl.BlockSpec((tq, 2 * HP), lambda b, j, i: (b * n_q + i, j)),
        k=pl.BlockSpec((TP, 2 * HP), lambda b, j, i: (b, j)),
        v=pl.BlockSpec((TP, 2 * V_HEAD), lambda b, j, i: (b, j)),
        o=pl.BlockSpec((tq, 2 * V_HEAD), lambda b, j, i: (b * n_q + i, j)),
        lse=pl.BlockSpec((None, tq, 2), lambda b, j, i: (j, b * n_q + i, 0)))


TQF = 1088


def _attn_fwd(q, k, v):
    def body(q_ref, k_ref, v_ref, o_ref, lse_ref):
        v2 = v_ref[...]
        o = jnp.zeros((TQF, 2 * V_HEAD), F32)
        lse = []
        for hh in range(2):
            sl = slice(hh * HP, (hh + 1) * HP)
            raw = _nt(q_ref[:, sl], k_ref[:, sl])
            m = jnp.max(raw, axis=-1, keepdims=True)
            e = jnp.exp2((raw - m) * SCALE_LOG2)
            l = jnp.sum(e, axis=-1, keepdims=True)
            o = o + _nn(e.astype(BF), v2 * _head_mask(hh, BF)) * (1.0 / l)
            lse.append(m * SCALE_LOG2 + jnp.log(l) * LOG2_E)
        o_ref[...] = o
        lane = lax.broadcasted_iota(jnp.int32, (TQF, 2), 1)
        lse_ref[...] = jnp.where(lane == 0, lse[0], lse[1])

    grid, sp = _attn_specs(TQF)
    return pl.pallas_call(
        body, grid=grid, name="attn_fwd", in_specs=[sp["q"], sp["k"], sp["v"]], out_specs=[sp["o"], sp["lse"]],
        out_shape=[jax.ShapeDtypeStruct((R, D_ATTN), F32), jax.ShapeDtypeStruct((N_HEADS // 2, R, 2), F32)],
        compiler_params=_params("arbitrary", "arbitrary", "arbitrary"),
    )(q, k, v)


def _attn_bwd(q, k, v, o, lse, do):
    def body(q_ref, k_ref, v_ref, o_ref, lse_ref, do_ref, dq_ref, dk_ref, dv_ref):
        @pl.when(pl.program_id(2) == 0)
        def _():
            dk_ref[...] = jnp.zeros_like(dk_ref)
            dv_ref[...] = jnp.zeros_like(dv_ref)

        do = do_ref[...]
        dob = do.astype(BF)
        do_o = do * o_ref[...]
        v2 = v_ref[...]
        dv_sum = jnp.zeros((TP, 2 * V_HEAD), F32)
        for hh in range(2):
            sl = slice(hh * HP, (hh + 1) * HP)
            qb, kb = q_ref[:, sl], k_ref[:, sl]
            p = jnp.exp2(_nt(qb, kb) * SCALE_LOG2 - lse_ref[:, hh:hh + 1])
            dp = _nt(dob, v2 * _head_mask(hh, BF))
            delta = jnp.sum(do_o * _head_mask(hh, F32), axis=-1, keepdims=True)
            dsb = (p * (dp - delta)).astype(BF)
            dq_ref[:, sl] = _nn(dsb, kb) * SCALE
            dk_ref[:, sl] += _tn(dsb, qb) * SCALE
            dv_sum = dv_sum + _tn(p.astype(BF), dob) * _head_mask(hh, F32)
        dv_ref[...] += dv_sum

    grid, sp = _attn_specs(TQ)
    return pl.pallas_call(
        body, grid=grid, name="attn_bwd", in_specs=[sp["q"], sp["k"], sp["v"], sp["o"], sp["lse"], sp["o"]],
        out_specs=[sp["q"], sp["k"], sp["v"]],
        out_shape=[jax.ShapeDtypeStruct((R, N_HEADS * HP), F32), jax.ShapeDtypeStruct((R, N_HEADS * HP), F32),
                   jax.ShapeDtypeStruct((R, D_ATTN), F32)],
        compiler_params=_params("arbitrary", "arbitrary", "arbitrary"),
    )(q, k, v, o, lse, do)


SEG = TP // 8


def _scan_pair(af_ref, bf_ref, hf_ref, ab_ref, bb_ref, hb_ref, pf_ref, pb_ref):
    unroll = 8

    def step(i, carry):
        hf, pf, hb, pb = carry
        for u in range(unroll):
            j = i * unroll + u
            rows_f, rows_b = pl.ds(j, 8, stride=SEG), pl.ds(SEG - 1 - j, 8, stride=SEG)
            a = af_ref[rows_f, :]
            hf, pf = a * hf + bf_ref[rows_f, :], a * pf
            hf_ref[rows_f, :] = hf
            pf_ref[rows_f, :] = pf
            a = ab_ref[rows_b, :]
            hb, pb = a * hb + bb_ref[rows_b, :], a * pb
            hb_ref[rows_b, :] = hb
            pb_ref[rows_b, :] = pb
        return hf, pf, hb, pb

    zero, one = jnp.zeros((8, CG), F32), jnp.ones((8, CG), F32)
    hf, pf, hb, pb = lax.fori_loop(0, SEG // unroll, step, (zero, one, zero, one))
    seg = lax.broadcasted_iota(jnp.int32, (8, CG), 0)
    cf, cb = zero, zero
    for s in range(1, 8):
        cf = jnp.where(seg == s, pltpu.roll(hf + pf * cf, 1, 0), cf)
        cb = jnp.where(seg == 7 - s, pltpu.roll(hb + pb * cb, 7, 0), cb)
    for s in range(8):
        rows = slice(s * SEG, (s + 1) * SEG)
        hf_ref[rows, :] = hf_ref[rows, :] + pf_ref[rows, :] * cf[s:s + 1, :]
        hb_ref[rows, :] = hb_ref[rows, :] + pb_ref[rows, :] * cb[s:s + 1, :]


def _shifts(x):
    t = lax.broadcasted_iota(jnp.int32, x.shape, 0)
    xm2 = jnp.where(t >= 2, pltpu.roll(x, 2, 0), 0.0)
    xm1 = jnp.where(t >= 1, pltpu.roll(x, 1, 0), 0.0)
    xp1 = jnp.where(t < TP - 1, pltpu.roll(x, TP - 1, 0), 0.0)
    return xm2, xm1, xp1


def _softplus(z):
    e = jnp.exp(-jnp.abs(z))
    small = e * (1.0 - e * (0.5 - e * (1.0 / 3.0)))
    return jnp.maximum(z, 0.0) + jnp.where(e < 0.01, small, jnp.log(1.0 + e))


def _sigmoid(x):
    return 0.5 * jnp.tanh(0.5 * x) + 0.5


def _one_minus_sq(log_a, a):
    x = 2.0 * log_a
    series = -x * (1.0 + x * 0.5 * (1.0 + x * (1.0 / 3.0) * (1.0 + x * 0.25)))
    return jnp.where(x > -0.05, series, 1.0 - a * a)


def _gates(row0, xc, pa_f, pi_f, pa_b, pi_b, lam_f, lam_b):
    t = row0 + lax.broadcasted_iota(jnp.int32, xc.shape, 0)
    valid = t < T
    out = []
    for pa, pi_, lam in ((pa_f, pi_f, lam_f), (pa_b, pi_b, lam_b)):
        r = _sigmoid(pa)
        gate_i = _sigmoid(pi_)
        log_a = -LRU_C * r * _softplus(-lam)
        a = jnp.exp(log_a)
        mult = jnp.sqrt(jnp.maximum(_one_minus_sq(log_a, a), 0.0))
        out += [a, jnp.where(valid, mult * (gate_i * xc), 0.0)]
    return tuple(out)


def _gates_bwd(row0, xc, pres, lams, cots):
    t = row0 + lax.broadcasted_iota(jnp.int32, xc.shape, 0)
    valid = t < T
    dxc = jnp.zeros_like(xc)
    dpres, dlams = [], []
    for d in range(2):
        pa, pi_, lam = pres[2 * d], pres[2 * d + 1], lams[d]
        da, db = cots[2 * d], jnp.where(valid, cots[2 * d + 1], 0.0)
        r = _sigmoid(pa)
        gate_i = _sigmoid(pi_)
        sp = _softplus(-lam)
        log_a = -LRU_C * r * sp
        a = jnp.exp(log_a)
        m2 = jnp.maximum(_one_minus_sq(log_a, a), 0.0)
        mult = jnp.sqrt(m2)
        dxc = dxc + db * (mult * gate_i)
        d_gate = db * (mult * xc)
        d_m2 = jnp.where(m2 > 0.0, db * (gate_i * xc) * (0.5 * lax.rsqrt(m2)), 0.0)
        d_log_a = da * a - 2.0 * d_m2 * (a * a)
        dpres += [d_log_a * (-LRU_C * sp) * (r * (1.0 - r)), d_gate * (gate_i * (1.0 - gate_i))]
        d_sp = jnp.sum(d_log_a * (-LRU_C * r), axis=0, keepdims=True)
        dlams.append(-d_sp * jax.nn.sigmoid(-lam))
    return dxc, dpres, dlams


def _rnn_specs():
    seq = pl.BlockSpec((TP, CG), lambda g, b: (b, g))
    return dict(
        seq=seq,
        cw=pl.BlockSpec((CONV_W, CG), lambda g, b: (0, g)),
        cb=pl.BlockSpec((1, CG), lambda g, b: (0, g)),
        w4=pl.BlockSpec((None, CG, 4 * CG), lambda g, b: (g, 0, 0)),
        b4=pl.BlockSpec((None, 1, 4 * CG), lambda g, b: (g, 0, 0)),
        lam=pl.BlockSpec((None, 1, 2 * CG), lambda g, b: (g, 0, 0)),
    )


def _conv(x, xm2, xm1, xp1, cw_ref, cb_ref):
    return cw_ref[0:1, :] * xm2 + cw_ref[1:2, :] * xm1 + cw_ref[2:3, :] * x + cw_ref[3:4, :] * xp1 + cb_ref[...]


TC = 128
N_TC = TP // TC


def _split4(pre):
    return pre[:, :CG], pre[:, CG:2 * CG], pre[:, 2 * CG:3 * CG], pre[:, 3 * CG:]


def _rnn_fwd(xr, xg, cw):
    def body(xr_ref, xg_ref, cw_ref, cb_ref, w4_ref, b4_ref, lam_ref, y_ref, hf_ref, hb_ref, af_ref, ab_ref, xc_ref,
             af, bf, ab, bb, pf, pb):
        x = xr_ref[...]
        xc_ref[...] = _conv(x, *_shifts(x), cw_ref, cb_ref)
        lam = lam_ref[...]

        def chunk(i, _):
            rows = pl.ds(pl.multiple_of(i * TC, TC), TC)
            xc = xc_ref[rows, :]
            pre = _nn(xc.astype(BF), w4_ref[...]) + b4_ref[...]
            a_f, b_f, a_b, b_b = _gates(i * TC, xc, *_split4(pre), lam[:, :CG], lam[:, CG:])
            af[rows, :] = a_f
            bf[rows, :] = b_f
            ab[rows, :] = a_b
            bb[rows, :] = b_b
            af_ref[rows, :] = a_f
            ab_ref[rows, :] = a_b
            return 0

        lax.fori_loop(0, N_TC, chunk, 0)
        _scan_pair(af, bf, hf_ref, ab, bb, hb_ref, pf, pb)
        y_ref[...] = (hf_ref[...] + hb_ref[...]) * jax.nn.gelu(xg_ref[...])

    sp = _rnn_specs()
    return pl.pallas_call(
        body, grid=(N_CG, NB), name="rnn_fwd",
        in_specs=[sp["seq"], sp["seq"], sp["cw"], sp["cb"], sp["w4"], sp["b4"], sp["lam"]],
        out_specs=[sp["seq"]] * 6, out_shape=[jax.ShapeDtypeStruct((R, D_RNN), F32)] * 6,
        scratch_shapes=[pltpu.VMEM((TP, CG), F32)] * 6,
        compiler_params=_params("arbitrary", "arbitrary"),
    )(xr, xg, cw["conv_w"], cw["conv_b"], cw["w4"], cw["b4"], cw["lam"])


def _rnn_bwd(dy, xr, xg, hf, hb, af, ab, xc, cw):
    def body(dy_ref, xr_ref, xg_ref, hf_ref, hb_ref, af_ref, ab_ref, xc_s, cw_ref, cb_ref, w4_ref, b4_ref, lam_ref,
             dxr_ref, dxg_ref, dcw_ref, dcb_ref, dw4_ref, db4_ref, dlam_ref,
             af_s, ab_s, dhs_s, lf_s, lb_s, daf_s, dab_s, dxc_s):
        @pl.when(pl.program_id(1) == 0)
        def _():
            for r in (dcw_ref, dcb_ref, dw4_ref, db4_ref, dlam_ref):
                r[...] = jnp.zeros_like(r)

        lam = lam_ref[...]

        def chunk1(i, _):
            rows = pl.ds(pl.multiple_of(i * TC, TC), TC)
            _, vjp_y = jax.vjp(lambda h, g: h * jax.nn.gelu(g), hf_ref[rows, :] + hb_ref[rows, :], xg_ref[rows, :])
            dhs, dxg = vjp_y(dy_ref[rows, :])
            dhs_s[rows, :] = dhs
            dxg_ref[rows, :] = dxg
            return 0

        lax.fori_loop(0, N_TC, chunk1, 0)
        t = lax.broadcasted_iota(jnp.int32, (TP, CG), 0)
        af_s[...] = pltpu.roll(af_ref[...], TP - 1, 0)
        ab_s[...] = pltpu.roll(ab_ref[...], 1, 0)
        _scan_pair(ab_s, dhs_s, lb_s, af_s, dhs_s, lf_s, dab_s, daf_s)
        daf_s[...] = lf_s[...] * jnp.where(t >= 1, pltpu.roll(hf_ref[...], 1, 0), 0.0)
        dab_s[...] = lb_s[...] * jnp.where(t < TP - 1, pltpu.roll(hb_ref[...], TP - 1, 0), 0.0)

        def chunk2(i, _):
            rows = pl.ds(pl.multiple_of(i * TC, TC), TC)
            xc = xc_s[rows, :]
            xcb = xc.astype(BF)
            pre = _nn(xcb, w4_ref[...]) + b4_ref[...]
            dxc, dpres, dlams = _gates_bwd(i * TC, xc, _split4(pre), (lam[:, :CG], lam[:, CG:]),
                                           (daf_s[rows, :], lf_s[rows, :], dab_s[rows, :], lb_s[rows, :]))
            dpre = jnp.concatenate(dpres, axis=1)
            dpreb = dpre.astype(BF)
            dxc_s[rows, :] = dxc + _nt(dpreb, w4_ref[...])
            dw4_ref[...] += _tn(xcb, dpreb)
            db4_ref[...] += jnp.sum(dpre, axis=0, keepdims=True)
            dlam_ref[...] += jnp.concatenate(dlams, axis=1)
            return 0

        lax.fori_loop(0, N_TC, chunk2, 0)
        dxc = dxc_s[...]
        x = xr_ref[...]
        taps = (jnp.where(t < TP - 2, pltpu.roll(dxc, TP - 2, 0), 0.0), jnp.where(t < TP - 1, pltpu.roll(dxc, TP - 1, 0), 0.0),
                dxc, jnp.where(t >= 1, pltpu.roll(dxc, 1, 0), 0.0))
        dcb_ref[...] += jnp.sum(dxc, axis=0, keepdims=True)
        dxr = jnp.zeros_like(dxc)
        for tap, shifted in enumerate(taps):
            dcw_ref[tap:tap + 1, :] += jnp.sum(x * shifted, axis=0, keepdims=True)
            dxr = dxr + cw_ref[tap:tap + 1, :] * shifted
        dxr_ref[...] = dxr

    sp = _rnn_specs()
    return pl.pallas_call(
        body, grid=(N_CG, NB), name="rnn_bwd",
        in_specs=[sp["seq"]] * 8 + [sp["cw"], sp["cb"], sp["w4"], sp["b4"], sp["lam"]],
        out_specs=[sp["seq"], sp["seq"], sp["cw"], sp["cb"], sp["w4"], sp["b4"], sp["lam"]],
        out_shape=[jax.ShapeDtypeStruct((R, D_RNN), F32), jax.ShapeDtypeStruct((R, D_RNN), F32),
                   jax.ShapeDtypeStruct((CONV_W, D_RNN), F32), jax.ShapeDtypeStruct((1, D_RNN), F32),
                   jax.ShapeDtypeStruct((N_CG, CG, 4 * CG), F32), jax.ShapeDtypeStruct((N_CG, 1, 4 * CG), F32),
                   jax.ShapeDtypeStruct((N_CG, 1, 2 * CG), F32)],
        scratch_shapes=[pltpu.VMEM((TP, CG), F32)] * 8,
        compiler_params=_params("arbitrary", "arbitrary"),
    )(dy, xr, xg, hf, hb, af, ab, xc, cw["conv_w"], cw["conv_b"], cw["w4"], cw["b4"], cw["lam"])


TD = 256
STAGE_D_VMEM = 58 * 1024 * 1024


def _stage_d(hp, o, y, tgt, cw):
    def body(hp_ref, o_ref, y_ref, tgt_ref, ga, gr, wout, ln2, wg, wu, wd,
             do_ref, dy_ref, dh1_ref, mix_ref, dh1b_ref, hn2_ref, dg_ref, du_ref, act_ref, dh2b_ref,
             loss_ref, dga_ref, dgr_ref, dln2_ref):
        i = pl.program_id(0)

        @pl.when(i == 0)
        def _():
            for r in (loss_ref, dga_ref, dgr_ref, dln2_ref):
                r[...] = jnp.zeros_like(r)

        mix_a, vjp_a = jax.vjp(lambda x, g: _rms(x, g, D_ATTN), o_ref[...], ga[...])
        mix_r, vjp_r = jax.vjp(lambda x, g: _rms(x, g, D_RNN), y_ref[...], gr[...])
        mab, mrb = mix_a.astype(BF), mix_r.astype(BF)
        mix_ref[:, :D_ATTN] = mab
        mix_ref[:, D_ATTN:] = mrb
        h1 = hp_ref[...] + _nn(mab, wout[:D_ATTN, :]) + _nn(mrb, wout[D_ATTN:, :])
        hn2, vjp_ln2 = jax.vjp(lambda x, g: _rms(x, g, D), h1, ln2[...])
        hn2b = hn2.astype(BF)
        hn2_ref[...] = hn2b
        act, vjp_act = jax.vjp(lambda g, u: jax.nn.silu(g) * u, _nt(hn2b, wg[...]), _nt(hn2b, wu[...]))
        actb = act.astype(BF)
        act_ref[...] = actb
        h2 = h1 + _nn(actb, wd[...])
        row = i * TD + lax.broadcasted_iota(jnp.int32, (TD, 1), 0)
        t = jnp.where(row >= TP, row - TP, row)
        err = jnp.where((t >= N_META) & (t < T), h2 - tgt_ref[...], 0.0)
        loss_ref[...] += jnp.sum(err * err) * (0.5 / D)
        dh2b = (err * (1.0 / D)).astype(BF)
        dh2b_ref[...] = dh2b
        dg, du = vjp_act(_nt(dh2b, wd[...]))
        dgb, dub = dg.astype(BF), du.astype(BF)
        dg_ref[...] = dgb
        du_ref[...] = dub
        dh1n, dln2 = vjp_ln2(_nn(dgb, wg[...]) + _nn(dub, wu[...]))
        dh1 = err * (1.0 / D) + dh1n
        dh1_ref[...] = dh1
        dh1b = dh1.astype(BF)
        dh1b_ref[...] = dh1b
        dmix = _nt(dh1b, wout[...])
        do, dga = vjp_a(dmix[:, :D_ATTN])
        dyr, dgr = vjp_r(dmix[:, D_ATTN:])
        do_ref[...] = do
        dy_ref[...] = dyr
        dga_ref[...] += dga
        dgr_ref[...] += dgr
        dln2_ref[...] += dln2

    rs = lambda n: _row_spec(n, TD)
    acc = lambda n: pl.BlockSpec((1, n), lambda i: (0, 0))
    return pl.pallas_call(
        body, grid=(R // TD,), name="stage_d",
        in_specs=[rs(D), rs(D_ATTN), rs(D_RNN), rs(D), _const_spec((1, D_ATTN)), _const_spec((1, D_RNN)),
                  _const_spec((D, D)), _const_spec((1, D)), _const_spec((D_FF, D)), _const_spec((D_FF, D)),
                  _const_spec((D_FF, D))],
        out_specs=[rs(D_ATTN), rs(D_RNN), rs(D), rs(D), rs(D), rs(D), rs(D_FF), rs(D_FF), rs(D_FF), rs(D),
                   acc(1), acc(D_ATTN), acc(D_RNN), acc(D)],
        out_shape=[jax.ShapeDtypeStruct((R, D_ATTN), F32), jax.ShapeDtypeStruct((R, D_RNN), F32),
                   jax.ShapeDtypeStruct((R, D), F32), jax.ShapeDtypeStruct((R, D), BF),
                   jax.ShapeDtypeStruct((R, D), BF), jax.ShapeDtypeStruct((R, D), BF),
                   jax.ShapeDtypeStruct((R, D_FF), BF), jax.ShapeDtypeStruct((R, D_FF), BF),
                   jax.ShapeDtypeStruct((R, D_FF), BF), jax.ShapeDtypeStruct((R, D), BF),
                   jax.ShapeDtypeStruct((1, 1), F32), jax.ShapeDtypeStruct((1, D_ATTN), F32),
                   jax.ShapeDtypeStruct((1, D_RNN), F32), jax.ShapeDtypeStruct((1, D), F32)],
        compiler_params=_params("arbitrary", vmem=STAGE_D_VMEM),
    )(hp, o, y, tgt, cw["ga"], cw["gr"], cw["wout"], cw["ln2_g"], cw["wg"], cw["wu"], cw["wd"])


TW = 2176


def _wgrad(a, b, name, tk=None):
    ka, nb = a.shape[1], b.shape[1]
    tk = ka if tk is None else tk

    def body(a_ref, b_ref, o_ref):
        @pl.when(pl.program_id(1) == 0)
        def _():
            o_ref[...] = jnp.zeros_like(o_ref)

        o_ref[...] += _tn(a_ref[...].astype(BF), b_ref[...].astype(BF))

    return pl.pallas_call(
        body, grid=(ka // tk, R // TW), name=name,
        in_specs=[pl.BlockSpec((TW, tk), lambda k, r: (r, k)), pl.BlockSpec((TW, nb), lambda k, r: (r, 0))],
        out_specs=pl.BlockSpec((tk, nb), lambda k, r: (k, 0)),
        out_shape=jax.ShapeDtypeStruct((ka, nb), F32),
        compiler_params=_params("arbitrary", "arbitrary"),
    )(a, b)


def _wgrad_heads(dq, dk, dv, cqn, ckvn):
    def body(dq_ref, dk_ref, dv_ref, cqn_ref, ckvn_ref, oq_ref, ok_ref, ov_ref):
        @pl.when(pl.program_id(0) == 0)
        def _():
            for r in (oq_ref, ok_ref, ov_ref):
                r[...] = jnp.zeros_like(r)

        ckvnb = ckvn_ref[...]
        oq_ref[...] += _tn(dq_ref[...], cqn_ref[...])
        ok_ref[...] += _tn(dk_ref[...], ckvnb)
        ov_ref[...] += _tn(dv_ref[...].astype(BF), ckvnb)

    rows = lambda a: pl.BlockSpec((TW, a.shape[1]), lambda r: (r, 0))
    full = lambda m, n: pl.BlockSpec((m, n), lambda r: (0, 0))
    shapes = [(dq.shape[1], cqn.shape[1]), (dk.shape[1], ckvn.shape[1]), (dv.shape[1], ckvn.shape[1])]
    return pl.pallas_call(
        body, grid=(R // TW,), name="wgrad_heads", in_specs=[rows(a) for a in (dq, dk, dv, cqn, ckvn)],
        out_specs=[full(*s) for s in shapes], out_shape=[jax.ShapeDtypeStruct(s, F32) for s in shapes],
        compiler_params=_params("arbitrary"),
    )(dq, dk, dv, cqn, ckvn)


def _rope_tables():
    half = QK_ROPE // 2
    freqs = 1.0 / (ROPE_THETA ** (jnp.arange(half, dtype=F32) / half))
    ang = jnp.arange(TP, dtype=F32)[:, None] * freqs[None, :]
    ones = jnp.ones((TP, QK_NOPE), F32)
    zeros = jnp.zeros((TP, QK_NOPE), F32)
    pad1 = jnp.ones((TP, HP - QK_HEAD), F32)
    pad0 = jnp.zeros((TP, HP - QK_HEAD), F32)
    cs = jnp.concatenate([ones, jnp.cos(ang), jnp.cos(ang), pad1], axis=1)
    sn = jnp.concatenate([zeros, jnp.sin(ang), jnp.sin(ang), pad0], axis=1)
    return jnp.tile(cs, (NB, 1)), jnp.tile(sn, (NB, 1))


def _pad_rows(a, lo, hi):
    return jnp.pad(a, ((0, 0), (lo, hi), (0, 0)))


def _pad_target(target):
    return _pad_rows(target, N_META, TP - T).reshape(R, D)


def _gate_weights(lru_wa, lru_wi):
    gates = jnp.stack([lru_wa[0], lru_wi[0], lru_wa[1], lru_wi[1]])
    blk = gates.reshape(4, N_CG, 2, RNN_BW, RNN_BW)
    dense = jnp.einsum("tcaij,ab->tcaibj", blk, jnp.eye(2, dtype=F32)).reshape(4, N_CG, CG, CG)
    return dense.transpose(1, 2, 0, 3).reshape(N_CG, CG, 4 * CG).astype(BF)


def _compute_weights(w):
    win_t = w["w_in_t"]
    kr = win_t[O_KR:O_KR + QK_ROPE]
    win = jnp.concatenate([win_t[:O_KR], jnp.zeros((QK_NOPE, D), F32), kr,
                           jnp.zeros((HP - QK_HEAD, D), F32), win_t[O_KR + QK_ROPE:]], axis=0)
    wq = _pad_rows(w["w_uq_t"].reshape(N_HEADS, QK_HEAD, Q_LORA), 0, HP - QK_HEAD)
    wkv = w["w_ukv_t"].reshape(N_HEADS, QK_NOPE + V_HEAD, KV_LORA)
    wk = _pad_rows(wkv[:, :QK_NOPE], 0, HP - QK_NOPE)
    wv = wkv[:, QK_NOPE:].reshape(D_ATTN, KV_LORA)
    bias =jnp.stack([w["lru_ba"][0], w["lru_bi"][0], w["lru_ba"][1], w["lru_bi"][1]])
    b4 = bias.reshape(4, N_CG, CG).transpose(1, 0, 2).reshape(N_CG, 1, 4 * CG)
    lam = w["lru_lambda"].reshape(2, N_CG, CG).transpose(1, 0, 2).reshape(N_CG, 1, 2 * CG)
    pad_g = lambda g: jnp.pad(g.reshape(1, QK_HEAD), ((0, 0), (0, HP - QK_HEAD)))
    return dict(
        ln1_g=w["ln1_g"].reshape(1, D), win=win.astype(BF), qa_g=w["q_a_norm_g"].reshape(1, Q_LORA),
        wq=wq.astype(BF).reshape(N_HEADS * HP, Q_LORA), kva_g=w["kv_a_norm_g"].reshape(1, KV_LORA),
        wk=wk.astype(BF).reshape(N_HEADS * HP, KV_LORA), wv=wv.astype(BF),
        q_g=pad_g(w["q_norm_g"]), k_g=pad_g(w["k_norm_g"]),
        conv_w=w["conv_w"].reshape(CONV_W, D_RNN), conv_b=w["conv_b"].reshape(1, D_RNN),
        w4=w["w4"] if "w4" in w else _gate_weights(w["lru_wa"], w["lru_wi"]), b4=b4, lam=lam,
        ga=w["attn_out_g"].reshape(1, D_ATTN), gr=w["rnn_out_g"].reshape(1, D_RNN), ln2_g=w["ln2_g"].reshape(1, D),
    )


def _local_step(x, target, meta, w, late_forward, late_weights, early_grads, mid_grads):
    cw = _compute_weights(w)
    cs, sn = w["rope"] if "rope" in w else _rope_tables()
    hp = jnp.concatenate([jnp.broadcast_to(meta[None], (NB, N_META, D)), x,
                          jnp.zeros((NB, TP - T, D), F32)], axis=1).reshape(R, D)
    tgt = target if target.ndim == 2 else _pad_target(target)

    pa, xr, xg, q, k, v = _stage_a_fwd(hp, cs, sn, cw)
    o, lse = _attn_fwd(q, k, v)
    cw["conv_b"] = cw["conv_b"] + late_forward([o])
    y, hf, hb, af, ab, xc = _rnn_fwd(xr, xg, cw)
    late = late_weights([y])
    cw.update(wout=late["w_out"], wg=late["w_gate_t"], wu=late["w_up_t"], wd=late["w_down"])
    (do, dy, dh1, mixb, dh1b, hn2b, dgb, dub, actb, dh2b, loss, dga, dgr, dln2) = _stage_d(hp, o, y, tgt, cw)
    dwout = _wgrad(mixb, dh1b, "wgrad_out")
    dwg = _wgrad(dgb, hn2b, "wgrad_gate", tk=D_FF // 2)
    dwu = _wgrad(dub, hn2b, "wgrad_up", tk=D_FF // 2)
    dwd = _wgrad(actb, dh2b, "wgrad_down", tk=D_FF // 2)
    zero = early_grads(dict(w_out=dwout, w_gate=dwg, w_up=dwu, w_down=dwd))
    cw["conv_b"] = cw["conv_b"] + zero
    dxr, dxg, dcw, dcb, dw4, db4, dlam = _rnn_bwd(dy, xr, xg, hf, hb, af, ab, xc, cw)
    zero = mid_grads([dxr])
    dq, dk, dv = _attn_bwd(q, k, v, o, lse, do)
    (dhp, dpb, dqrawb, dkrawb, hn1b, cqnb, ckvnb, dln1, dqag, dkvag, dqg, dkg) = _stage_a_bwd(
        dq, dk, dv, dxr, dxg, dh1, hp, pa, cs, sn, dict(cw, qa_g=cw["qa_g"] + zero))

    dwin = _wgrad(dpb, hn1b, "wgrad_in", tk=PC // 2)
    dwq, dwk, dwv = _wgrad_heads(dqrawb, dkrawb, dv, cqnb, ckvnb)

    dwin_t = jnp.concatenate([dwin[:O_KR], dwin[O_KR + QK_NOPE:O_KR + QK_HEAD], dwin[O_XR:]], axis=0)
    dwq_t = dwq.reshape(N_HEADS, HP, Q_LORA)[:, :QK_HEAD].reshape(N_HEADS * QK_HEAD, Q_LORA)
    dwkv_t = jnp.concatenate([dwk.reshape(N_HEADS, HP, KV_LORA)[:, :QK_NOPE],
                              dwv.reshape(N_HEADS, V_HEAD, KV_LORA)], axis=1).reshape(2 * D_ATTN, KV_LORA)
    d4 = dw4.reshape(N_CG, 2, RNN_BW, 4, 2, RNN_BW)
    dgates = jnp.stack([d4[:, 0, :, :, 0, :], d4[:, 1, :, :, 1, :]], axis=1)
    dgates = dgates.transpose(3, 0, 1, 2, 4).reshape(4, N_HEADS, RNN_BW, RNN_BW)
    dbias = db4.reshape(N_CG, 4, CG).transpose(1, 0, 2).reshape(4, D_RNN)
    dhp3 = dhp.reshape(NB, TP, D)
    grads = dict(
        meta_tokens=jnp.sum(dhp3[:, :N_META], axis=0),
        ln1_g=dln1, w_in_t=dwin_t, q_a_norm_g=dqag, w_uq_t=dwq_t, kv_a_norm_g=dkvag, w_ukv_t=dwkv_t,
        q_norm_g=dqg[:, :QK_HEAD], k_norm_g=dkg[:, :QK_HEAD], conv_w=dcw[None], conv_b=dcb,
        lru_wa=jnp.stack([dgates[0], dgates[2]])[None], lru_ba=jnp.stack([dbias[0], dbias[2]])[None],
        lru_wi=jnp.stack([dgates[1], dgates[3]])[None], lru_bi=jnp.stack([dbias[1], dbias[3]])[None],
        lru_lambda=dlam.reshape(N_CG, 2, CG).transpose(1, 0, 2).reshape(1, 2, D_RNN),
        attn_out_g=dga, rnn_out_g=dgr, ln2_g=dln2,
    )
    return loss[0, 0], dhp3[:, N_META:T], grads, [dhp, dwin]


_ANY = pl.BlockSpec(memory_space=pl.ANY)


def _place():
    return lax.axis_index("x"), lax.axis_index("y"), lax.axis_index("c")


def _other_chips(x, y):
    return [(1 - x, y), (x, 1 - y), (1 - x, 1 - y)]


def _pair_exchange(big, whole, name):
    n_s, _, m, n = big.shape
    n_copies = n_s + len(whole)

    def body(*refs):
        big_ref, whole_refs = refs[0], refs[1:1 + len(whole)]
        rbig_ref, rwhole_refs = refs[1 + len(whole)], refs[2 + len(whole):2 + 2 * len(whole)]
        send_sems, recv_sems = refs[-2:]
        x, y, c = _place()
        sibling = (x, y, 1 - c)
        copies = [pltpu.make_async_remote_copy(
            src_ref=big_ref.at[s, 1 - c], dst_ref=rbig_ref.at[s], send_sem=send_sems.at[s], recv_sem=recv_sems.at[s],
            device_id=sibling, device_id_type=MESH) for s in range(n_s)]
        copies += [pltpu.make_async_remote_copy(
            src_ref=a, dst_ref=r, send_sem=send_sems.at[n_s + i], recv_sem=recv_sems.at[n_s + i],
            device_id=sibling, device_id_type=MESH) for i, (a, r) in enumerate(zip(whole_refs, rwhole_refs))]
        for cp in copies:
            cp.start()
        for cp in copies:
            cp.wait()

    return pl.pallas_call(
        body, name=name,
        out_shape=[jax.ShapeDtypeStruct((n_s, m, n), big.dtype)] + [jax.ShapeDtypeStruct(a.shape, a.dtype) for a in whole],
        in_specs=[_ANY] * (1 + len(whole)), out_specs=[_ANY] * (1 + len(whole)),
        scratch_shapes=[pltpu.SemaphoreType.DMA((n_copies,)), pltpu.SemaphoreType.DMA((n_copies,))],
    )(big, *whole)


_HBM = pl.BlockSpec(memory_space=pltpu.HBM)
_SEM = pl.BlockSpec(memory_space=pltpu.SEMAPHORE)
_EFFECT = pltpu.SideEffectType.DATAFLOW_SIDE_EFFECTING


def _split_copies(src_refs, land_refs, sems, plan, sending):
    n = len(sems) // 2
    return [pltpu.make_async_remote_copy(src_ref=s, dst_ref=d, send_sem=sems[k], recv_sem=sems[n + k], device_id=to,
                                         device_id_type=MESH)
            for k, (s, d, to) in enumerate(plan(src_refs, land_refs, sending))]


def _to_chips(src_at, land_at):
    def plan(src_refs, land_refs, sending):
        x, y, c = _place()
        return [(src_at(s, tx, ty, c), land_at(l, j, *((x, y) if sending else (tx, ty)), c), (tx, ty, c))
                for s, l in zip(src_refs, land_refs) for j, (tx, ty) in enumerate(_other_chips(x, y))]
    return plan


def _to_sibling(src_refs, land_refs, sending):
    x, y, c = _place()
    return [(s.at[k, 1 - c], l.at[k], (x, y, 1 - c)) for s, l in zip(src_refs, land_refs) for k in range(N_CHIPS)]


def _split_start(name, srcs, lands, plan, n, after=()):
    srcs, lands, after = list(srcs), list(lands), list(after)
    k, kb = len(srcs), len(srcs) + len(lands)

    def body(*refs):
        outs = refs[kb + len(after):]
        for cp in _split_copies(refs[:k], refs[k:kb], outs[:2 * n], plan, True):
            cp.start()
        outs[2 * n + kb][...] = jnp.zeros_like(outs[2 * n + kb])

    outs = pl.pallas_call(
        body, name=name,
        out_shape=(pltpu.SemaphoreType.DMA(()),) * (2 * n) + tuple(pltpu.HBM(a.shape, a.dtype) for a in srcs + lands)
        + (jax.ShapeDtypeStruct((8, LANES), F32),),
        in_specs=(_HBM,) * kb + (_ANY,) * len(after),
        out_specs=(_SEM,) * (2 * n) + (_HBM,) * kb + (pl.BlockSpec(memory_space=pltpu.VMEM),),
        input_output_aliases={i: 2 * n + i for i in range(kb)},
        compiler_params=pltpu.CompilerParams(has_side_effects=_EFFECT),
    )(*[pltpu.with_memory_space_constraint(a, pltpu.HBM) for a in srcs + lands], *after)
    return outs[:2 * n], list(outs[2 * n:2 * n + k]), list(outs[2 * n + k:2 * n + kb]), outs[2 * n + kb]


def _split_wait(name, sems, srcs, lands, after, plan):
    srcs, lands = list(srcs), list(lands)
    k, kb = len(srcs), len(srcs) + len(lands)

    def body(*refs):
        for cp in _split_copies(refs[:k], refs[k:kb], refs[kb:kb + len(sems)], plan, False):
            cp.wait_send()
            cp.wait_recv()

    outs = pl.pallas_call(
        body, name=name, out_shape=tuple(pltpu.HBM(a.shape, a.dtype) for a in srcs + lands),
        in_specs=(_HBM,) * kb + (_SEM,) * len(sems) + (_ANY,) * len(after), out_specs=(_HBM,) * kb,
        input_output_aliases={i: i for i in range(kb)}, compiler_params=pltpu.CompilerParams(has_side_effects=_EFFECT),
    )(*srcs, *lands, *sems, *after)
    return list(outs[:k]), list(outs[k:])


def _forward_landed(src_refs, land_refs, sending):
    x, y, c = _place()
    copies = []
    for ref in src_refs:
        m = ref.shape[0] // 8
        for tx, ty in _other_chips(x, y):
            rows = ref.at[pl.ds((4 * tx + 2 * ty + (c if sending else 1 - c)) * m, m), :]
            copies.append((rows, rows, (x, y, 1 - c)))
    return copies


def _place_own(pieces):
    k = len(pieces)

    def body(*refs):
        piece_refs, out_refs, stages = refs[:k], refs[k:2 * k], refs[2 * k:3 * k]
        load_sems, store_sems = refs[3 * k:]
        x, y, _ = _place()
        loads = [pltpu.make_async_copy(piece_refs[a], stages[a], load_sems.at[a]) for a in range(k)]
        stores = [pltpu.make_async_copy(
            stages[a], out_refs[a].at[pl.ds((2 * x + y) * pieces[a].shape[0], pieces[a].shape[0]), :], store_sems.at[a])
            for a in range(k)]
        for cp in loads:
            cp.start()
        for ld, st in zip(loads, stores):
            ld.wait()
            st.start()
        for cp in stores:
            cp.wait()

    return pl.pallas_call(
        body, name="gather_late_place_own",
        out_shape=[jax.ShapeDtypeStruct((N_CHIPS * p.shape[0], p.shape[1]), p.dtype) for p in pieces],
        in_specs=[_ANY] * k, out_specs=[_ANY] * k,
        scratch_shapes=[pltpu.VMEM(p.shape, p.dtype) for p in pieces]
        + [pltpu.SemaphoreType.DMA((k,)), pltpu.SemaphoreType.DMA((k,))],
    )(*pieces)


def _gather_finish(lands, pieces, name):
    k = len(lands)

    def body(*refs):
        land_refs, piece_refs, out_refs, stages = refs[:k], refs[k:2 * k], refs[2 * k:3 * k], refs[3 * k:4 * k]
        send_sems, recv_sems, load_sems, store_sems = refs[4 * k:]
        x, y, c = _place()
        sibling = (x, y, 1 - c)
        remote, loads, stores, arrivals = [], [], [], []
        for a in range(k):
            m = lands[a].shape[0] // 8

            def rows(px, py, pc, ref, m=m):
                return ref.at[pl.ds((4 * px + 2 * py + pc) * m, m), :]

            for j, (tx, ty) in enumerate(_other_chips(x, y)):
                sems = dict(send_sem=send_sems.at[3 * a + j], recv_sem=recv_sems.at[3 * a + j], device_id=sibling,
                            device_id_type=MESH)
                remote.append(pltpu.make_async_remote_copy(
                    src_ref=rows(tx, ty, c, land_refs[a]), dst_ref=rows(tx, ty, c, out_refs[a]), **sems))
                arrivals.append(pltpu.make_async_remote_copy(
                    src_ref=rows(tx, ty, 1 - c, out_refs[a]), dst_ref=rows(tx, ty, 1 - c, out_refs[a]), **sems))
            for h in range(2):
                loads.append(pltpu.make_async_copy(piece_refs[a].at[pl.ds(h * m, m), :], stages[a].at[h],
                                                   load_sems.at[2 * a + h]))
                stores.append(pltpu.make_async_copy(stages[a].at[h], rows(x, y, h, out_refs[a]), store_sems.at[2 * a + h]))
        for cp in remote + loads:
            cp.start()
        for ld, st in zip(loads, stores):
            ld.wait()
            st.start()
        for cp, arrival in zip(remote, arrivals):
            cp.wait_send()
            arrival.wait_recv()
        for cp in stores:
            cp.wait()

    return pl.pallas_call(
        body, name=name, out_shape=[jax.ShapeDtypeStruct(a.shape, a.dtype) for a in lands],
        in_specs=[_ANY] * (2 * k), out_specs=[_ANY] * k, input_output_aliases={i: i for i in range(k)},
        scratch_shapes=[pltpu.VMEM((2, a.shape[0] // 8, a.shape[1]), a.dtype) for a in lands]
        + [pltpu.SemaphoreType.DMA((3 * k,)), pltpu.SemaphoreType.DMA((3 * k,)), pltpu.SemaphoreType.DMA((2 * k,)),
           pltpu.SemaphoreType.DMA((2 * k,))],
    )(*lands, *pieces)


def _pair_fill(bufs, name):
    k = len(bufs)

    def body(*refs):
        send_sems, recv_sems = refs[-2:]
        x, y, c = _place()
        copies = [pltpu.make_async_remote_copy(
            src_ref=refs[i].at[c], dst_ref=refs[k + i].at[c], send_sem=send_sems.at[i], recv_sem=recv_sems.at[i],
            device_id=(x, y, 1 - c), device_id_type=MESH) for i in range(k)]
        for cp in copies:
            cp.start()
        for i, cp in enumerate(copies):
            cp.wait_send()
            pltpu.make_async_remote_copy(
                src_ref=refs[i].at[1 - c], dst_ref=refs[k + i].at[1 - c], send_sem=send_sems.at[i],
                recv_sem=recv_sems.at[i], device_id=(x, y, 1 - c), device_id_type=MESH).wait_recv()

    return pl.pallas_call(
        body, name=name, out_shape=[jax.ShapeDtypeStruct(a.shape, a.dtype) for a in bufs], in_specs=[_ANY] * k,
        out_specs=[_ANY] * k, input_output_aliases={i: i for i in range(k)},
        scratch_shapes=[pltpu.SemaphoreType.DMA((k,)), pltpu.SemaphoreType.DMA((k,))],
    )(*bufs)


def _row_tile(rows, cap=512):
    for t in range(cap - cap % 8, 7, -8):
        if rows % t == 0:
            return t
    return rows


def _elementwise(fn, n_out, name, *arrs, out_dtype=F32):
    rows, cols = arrs[0].shape
    tr = _row_tile(rows)
    n_in = len(arrs)

    def body(*refs):
        outs = fn(*[r[...].astype(F32) for r in refs[:n_in]])
        for r, o in zip(refs[n_in:], outs):
            r[...] = o.astype(out_dtype)

    spec = pl.BlockSpec((tr, cols), lambda i: (i, 0))
    return pl.pallas_call(
        body, grid=(rows // tr,), name=name, in_specs=[spec] * n_in, out_specs=[spec] * n_out,
        out_shape=[jax.ShapeDtypeStruct((rows, cols), out_dtype)] * n_out, compiler_params=_params("arbitrary"),
    )(*arrs)


def _pair_sums(gpacks, rbigs, ci, name):
    k = len(gpacks)

    def body(c_ref, *refs):
        for g_ref, r_ref, o_ref in zip(refs[:k], refs[k:2 * k], refs[2 * k:]):
            o_ref[...] = (g_ref[...] + r_ref[...]).astype(BF)

    half = lambda a: pl.BlockSpec((None,) + a.shape[1:], lambda s, c: (s, 0, 0))
    return pl.pallas_call(
        body, name=name, out_shape=[jax.ShapeDtypeStruct(r.shape, BF) for r in rbigs],
        grid_spec=pltpu.PrefetchScalarGridSpec(
            num_scalar_prefetch=1, grid=(N_CHIPS,),
            in_specs=[pl.BlockSpec((None, None) + g.shape[2:], lambda s, c: (s, c[0], 0, 0)) for g in gpacks]
            + [half(r) for r in rbigs],
            out_specs=[half(r) for r in rbigs]),
        compiler_params=_params("arbitrary"),
    )(ci.reshape(1), *gpacks, *rbigs)


def _chip_sums(sums, landed, chip, ci, name):
    k = len(sums)

    def body(p_ref, *refs):
        for own_ref, land_ref, o_ref in zip(refs[:k], refs[k:2 * k], refs[2 * k:]):
            f = lambda v: v.astype(F32)
            o_ref[...] = _add4(f(own_ref[...]), f(land_ref[0]), f(land_ref[1]), f(land_ref[2]))[0]

    return pl.pallas_call(
        body, name=name, out_shape=[jax.ShapeDtypeStruct((2,) + s.shape[1:], F32) for s in sums],
        grid_spec=pltpu.PrefetchScalarGridSpec(
            num_scalar_prefetch=1, grid=(1,),
            in_specs=[pl.BlockSpec((None,) + s.shape[1:], lambda i, p: (p[0], 0, 0)) for s in sums]
            + [pl.BlockSpec(l.shape, lambda i, p: (0, 0, 0)) for l in landed],
            out_specs=[pl.BlockSpec((None,) + s.shape[1:], lambda i, p: (p[1], 0, 0)) for s in sums]),
        compiler_params=_params("arbitrary"),
    )(jnp.stack([chip, ci]), *sums, *landed)


def _add2(a, b):
    return (a + b,)


def _add4(own, r0, r1, r2):
    return ((own + r2) + (r0 + r1),)


def _adamw_rows(ws, gs, ms, vs, name):
    k = len(ws)
    steps = next(s for s in (4, 2, 1) if all(w.shape[0] % (8 * s) == 0 for w in ws))

    def body(*refs):
        for i in range(k):
            outs = _adamw_math(*[refs[j * k + i][...] for j in range(4)])
            for j, o in enumerate(outs):
                refs[(4 + j) * k + i][...] = o

    specs = [pl.BlockSpec((w.shape[0] // steps, w.shape[1]), lambda i: (i, 0)) for w in ws]
    return pl.pallas_call(
        body, grid=(steps,), name=name, in_specs=specs * 4, out_specs=specs * 3,
        out_shape=[jax.ShapeDtypeStruct(w.shape, F32) for w in ws] * 3, compiler_params=_params("arbitrary"),
    )(*ws, *gs, *ms, *vs)


def _adamw_small(ws, gs, ms, vs):
    k = len(ws)

    def body(*refs):
        for i in range(k):
            outs = _adamw_math(*[refs[j * k + i][...] for j in range(4)])
            for j, o in enumerate(outs):
                refs[(4 + j) * k + i][...] = o

    return pl.pallas_call(
        body, name="adamw_small", out_shape=[jax.ShapeDtypeStruct(w.shape, F32) for w in ws] * 3,
    )(*ws, *gs, *ms, *vs)


def _adamw_math(w, g, m, v):
    m = ADAM_B1 * m + (1.0 - ADAM_B1) * g
    v = ADAM_B2 * v + (1.0 - ADAM_B2) * (g * g)
    m_hat = m / (1.0 - ADAM_B1 ** ADAM_STEP)
    v_hat = v / (1.0 - ADAM_B2 ** ADAM_STEP)
    delta = -ADAM_LR * (m_hat / (jnp.sqrt(v_hat) + ADAM_EPS) + ADAM_WD * w)
    return delta, m, v


WEIGHTS = ["meta_tokens", "ln1_g", "w_in", "q_a_norm_g", "w_uq", "kv_a_norm_g", "w_ukv", "q_norm_g", "k_norm_g",
           "conv_w", "conv_b", "lru_wa", "lru_ba", "lru_wi", "lru_bi", "lru_lambda", "attn_out_g", "rnn_out_g",
           "w_out", "ln2_g", "w_gate", "w_up", "w_down"]
BIG = ["w_in", "w_uq", "w_ukv", "w_out", "w_gate", "w_up", "w_down"]
BIG_T = {"w_in": True, "w_uq": True, "w_ukv": True, "w_out": False, "w_gate": True, "w_up": True, "w_down": False}
BIG_ROWS = {"w_in": 424, "w_uq": 72, "w_ukv": 64, "w_out": 256, "w_gate": 704, "w_up": 704, "w_down": 704}
EARLY = ["w_in", "w_uq", "w_ukv"]
LATE = ["w_out", "w_gate", "w_up", "w_down"]
EARLY_ROWS = 576
SMALL_SHARDED = ["meta_tokens", "conv_w", "lru_ba", "lru_bi", "lru_lambda"]
SMALL = [n for n in WEIGHTS if n not in BIG]
SMALL_PACK_ROWS = 160


def _offsets(names):
    off, o = {}, 0
    for n in names:
        off[n] = o
        o += BIG_ROWS[n]
    return off


def _shard_pack(names, src, rows):
    parts = [_to_pack_piece(n, src[n]) for n in names]
    used = sum(BIG_ROWS[n] for n in names)
    if rows > used:
        parts.append(jnp.zeros((rows - used, D), F32))
    return jnp.concatenate(parts, axis=0)


def _grad_pack(names, g, rows):
    parts = [g[n].reshape(N_CHIPS, BIG_ROWS[n], D) for n in names]
    used = sum(BIG_ROWS[n] for n in names)
    if rows > used:
        parts.append(jnp.zeros((N_CHIPS, rows - used, D), F32))
    return jnp.concatenate(parts, axis=1).reshape(N_CHIPS, 2, rows // 2, D)


def _to_pack_piece(name, shard):
    a = shard[0].T if BIG_T[name] else shard[0]
    return a.reshape(BIG_ROWS[name], D)


def _flat_pack(arrs, rows):
    flat = jnp.concatenate([a.reshape(-1) for a in arrs])
    return jnp.pad(flat, (0, rows * D - flat.shape[0])).reshape(rows, D)


def _flat_unpack(pack, shapes):
    flat, out, o = pack.reshape(-1), [], 0
    for s in shapes:
        n = math.prod(s)
        out.append(flat[o:o + n].reshape(s))
        o += n
    return out


def kernel(x, meta_tokens, ln1_g, w_in, q_a_norm_g, w_uq, kv_a_norm_g, w_ukv, q_norm_g, k_norm_g, conv_w, conv_b, lru_wa, lru_ba, lru_wi, lru_bi, lru_lambda, attn_out_g, rnn_out_g, w_out, ln2_g, w_gate, w_up, w_down, loss_target, m_meta_tokens, m_ln1_g, m_w_in, m_q_a_norm_g, m_w_uq, m_kv_a_norm_g, m_w_ukv, m_q_norm_g, m_k_norm_g, m_conv_w, m_conv_b, m_lru_wa, m_lru_ba, m_lru_wi, m_lru_bi, m_lru_lambda, m_attn_out_g, m_rnn_out_g, m_w_out, m_ln2_g, m_w_gate, m_w_up, m_w_down, v_meta_tokens, v_ln1_g, v_w_in, v_q_a_norm_g, v_w_uq, v_kv_a_norm_g, v_w_ukv, v_q_norm_g, v_k_norm_g, v_conv_w, v_conv_b, v_lru_wa, v_lru_ba, v_lru_wi, v_lru_bi, v_lru_lambda, v_attn_out_g, v_rnn_out_g, v_w_out, v_ln2_g, v_w_gate, v_w_up, v_w_down):
    wts = dict(zip(WEIGHTS, (meta_tokens, ln1_g, w_in, q_a_norm_g, w_uq, kv_a_norm_g, w_ukv, q_norm_g, k_norm_g, conv_w, conv_b, lru_wa, lru_ba, lru_wi, lru_bi, lru_lambda, attn_out_g, rnn_out_g, w_out, ln2_g, w_gate, w_up, w_down)))
    mom = dict(zip(WEIGHTS, (m_meta_tokens, m_ln1_g, m_w_in, m_q_a_norm_g, m_w_uq, m_kv_a_norm_g, m_w_ukv, m_q_norm_g, m_k_norm_g, m_conv_w, m_conv_b, m_lru_wa, m_lru_ba, m_lru_wi, m_lru_bi, m_lru_lambda, m_attn_out_g, m_rnn_out_g, m_w_out, m_ln2_g, m_w_gate, m_w_up, m_w_down)))
    var = dict(zip(WEIGHTS, (v_meta_tokens, v_ln1_g, v_w_in, v_q_a_norm_g, v_w_uq, v_kv_a_norm_g, v_w_ukv, v_q_norm_g, v_k_norm_g, v_conv_w, v_conv_b, v_lru_wa, v_lru_ba, v_lru_wi, v_lru_bi, v_lru_lambda, v_attn_out_g, v_rnn_out_g, v_w_out, v_ln2_g, v_w_gate, v_w_up, v_w_down)))
    xi, yi, ci = _place()
    chip = 2 * xi + yi
    off_e = _offsets(EARLY)
    half_e = EARLY_ROWS // 2
    gather_plan = _to_chips(lambda ref, tx, ty, c: ref.at[pl.ds(c * (ref.shape[0] // 2), ref.shape[0] // 2), :],
                            lambda ref, j, px, py, c: ref.at[pl.ds((4 * px + 2 * py + c) * (ref.shape[0] // 8),
                                                                   ref.shape[0] // 8), :])
    scatter_plan = _to_chips(lambda ref, tx, ty, c: ref.at[2 * tx + ty], lambda ref, j, px, py, c: ref.at[j])
    everywhere = _to_chips(lambda ref, tx, ty, c: ref, lambda ref, j, px, py, c: ref.at[j])
    n_late = len(LATE)

    pack_e = _shard_pack(EARLY, wts, EARLY_ROWS).astype(BF)
    spack = jnp.concatenate([meta_tokens[:, :LANES], meta_tokens[:, LANES:], conv_w[0], lru_ba[0], lru_bi[0],
                             lru_lambda[0], jnp.zeros((6, LANES), F32)], axis=0)
    sems_g, src_g, land_g, _ = _split_start(
        "gather_early_start", [pack_e, spack], [lax.empty((N_CHIPS * EARLY_ROWS, D), BF), lax.empty((N_CHIPS * 48, LANES), F32)],
        gather_plan, 6)
    tgt_padded = _pad_target(loss_target)
    pieces_l = [_to_pack_piece(n, wts[n]).astype(BF) for n in LATE]
    lands_l = list(_place_own(pieces_l))
    w4, rope = _gate_weights(lru_wa[0], lru_wi[0]), _rope_tables()
    src_g, land_g = _split_wait("gather_early_wait", sems_g, src_g, land_g, [tgt_padded, w4, *rope] + lands_l, gather_plan)
    ge, gs = _gather_finish(land_g, src_g, "gather_early_finish")
    ge = ge.reshape(N_CHIPS, EARLY_ROWS, D)
    gs = gs.reshape(N_CHIPS, 48, LANES)
    full = {n: ge[:, off_e[n]:off_e[n] + BIG_ROWS[n]] for n in EARLY}
    sems_l, src_l, land_l, tied = _split_start("gather_late_start", pieces_l, lands_l, gather_plan, 3 * n_late, after=[ge])

    forward, pair, late = {}, {}, {}

    def late_forward(after):
        _, landed = _split_wait("gather_late_wait", sems_l, src_l, land_l, after, gather_plan)
        forward["sems"], forward["src"], _, zeros = _split_start(
            "gather_late_forward_start", landed, [], _forward_landed, 3 * n_late)
        return zeros[0, 0]

    def late_weights(after):
        (w_out_, w_gate_, w_up_, w_down_), _ = _split_wait(
            "gather_late_forward_wait", forward["sems"], forward["src"], [], after, _forward_landed)
        return dict(w_out=w_out_, w_gate_t=w_gate_, w_up_t=w_up_, w_down=w_down_)

    def early_grads(g_late):
        halves = [g_late[n].reshape(N_CHIPS, 2, BIG_ROWS[n] // 2, D) for n in LATE]
        pair["sems"], pair["src"], pair["land"], zeros = _split_start(
            "grad_pair_late_start", halves, [lax.empty((N_CHIPS, BIG_ROWS[n] // 2, D), F32) for n in LATE], _to_sibling,
            N_CHIPS * n_late)
        return zeros[0, 0]

    def mid_grads(after):
        halves, landed = _split_wait("grad_pair_late_wait", pair["sems"], pair["src"], pair["land"], after, _to_sibling)
        chip_sums = _pair_sums(halves, landed, ci, "grad_pair_sum_late")
        late["sems"], late["src"], late["land"], zeros = _split_start(
            "grad_chip_late_start", chip_sums, [lax.empty((3, BIG_ROWS[n] // 2, D), BF) for n in LATE], scatter_plan,
            3 * n_late)
        return zeros[0, 0]

    cols = lambda a: a.transpose(1, 0, 2).reshape(a.shape[1], N_CHIPS * a.shape[2])
    meta_full = cols(jnp.concatenate([gs[:, 0:16], gs[:, 16:32]], axis=2))
    w = dict(
        w_in_t=full["w_in"].reshape(IN_COLS, D), w_uq_t=full["w_uq"].reshape(N_HEADS * QK_HEAD, Q_LORA),
        w_ukv_t=full["w_ukv"].reshape(2 * D_ATTN, KV_LORA),
        ln1_g=ln1_g, q_a_norm_g=q_a_norm_g, kv_a_norm_g=kv_a_norm_g, q_norm_g=q_norm_g, k_norm_g=k_norm_g,
        conv_w=cols(gs[:, 32:36]), conv_b=conv_b, lru_wa=lru_wa[0], lru_ba=cols(gs[:, 36:38]), lru_wi=lru_wi[0],
        lru_bi=cols(gs[:, 38:40]), lru_lambda=cols(gs[:, 40:42]), attn_out_g=attn_out_g, rnn_out_g=rnn_out_g,
        ln2_g=ln2_g, w4=w4, rope=rope,
    )

    loss_local, grad_x, g, last = _local_step(x, tgt_padded, meta_full + tied[0, 0], w, late_forward, late_weights,
                                              early_grads, mid_grads)

    gpack = _grad_pack(EARLY, {"w_in": g["w_in_t"], "w_uq": g["w_uq_t"], "w_ukv": g["w_ukv_t"]}, EARLY_ROWS)
    full_shapes = {n: wts[n].shape for n in SMALL}
    full_shapes.update(meta_tokens=(N_META, D), conv_w=(1, CONV_W, D_RNN), lru_ba=(1, 2, D_RNN), lru_bi=(1, 2, D_RNN),
                       lru_lambda=(1, 2, D_RNN))
    gsmall = _flat_pack([g[n] for n in SMALL] + [loss_local], SMALL_PACK_ROWS)
    rbig, rsmall = _pair_exchange(gpack, [gsmall], "grad_pair_exchange")
    chip_big = _pair_sums([gpack], [rbig], ci, "grad_pair_sum")
    (chip_small,) = _elementwise(_add2, 1, "grad_pair_sum_small", gsmall, rsmall)
    early_plan = lambda srcs, lands, sending: (scatter_plan(srcs[:1], lands[:1], sending)
                                               + everywhere(srcs[1:], lands[1:], sending))
    sems_e, src_e, land_e, zero_e = _split_start(
        "grad_chip_early_start", list(chip_big) + [chip_small],
        [lax.empty((3, half_e, D), BF), lax.empty((3, SMALL_PACK_ROWS, D), F32)], early_plan, 6)

    grads, delta, new_m, new_v = {}, {}, {}, {}

    def adamw_big(names, gshards):
        as_rows = lambda n, a: a[0].T if BIG_T[n] else a[0]
        back = lambda n, a: a.T[None] if BIG_T[n] else a[None]
        ws, ms, vs = ([as_rows(n, src[n]) for n in names] for src in (wts, mom, var))
        g2 = [gs.reshape(w.shape) for w, gs in zip(ws, gshards)]
        outs = _adamw_rows(ws, g2, ms, vs, "adamw_" + names[0])
        for i, n in enumerate(names):
            grads[n] = back(n, g2[i])
            delta[n], new_m[n], new_v[n] = (back(n, outs[j * len(names) + i]) for j in range(3))
        return outs[0]

    sums, landed = _split_wait("grad_chip_late_wait", late["sems"], late["src"], late["land"], last + [zero_e], scatter_plan)
    shards_l = _pair_fill(_chip_sums(sums, landed, chip, ci, "grad_chip_sum_late"), "grad_pair_fill_late")
    done_late = adamw_big(LATE, shards_l)
    (src_e, src_s), (land_e, land_s) = _split_wait("grad_chip_early_wait", sems_e, src_e, land_e, [done_late, grad_x],
                                                   early_plan)
    (shard_e,) = _pair_fill(_chip_sums([src_e], [land_e], chip, ci, "grad_chip_sum"), "grad_pair_fill_early")
    shard_e = shard_e.reshape(EARLY_ROWS, D)
    for n in EARLY:
        adamw_big([n], [shard_e[off_e[n]:off_e[n] + BIG_ROWS[n]]])
    (small_sum,) = _elementwise(_add4, 1, "grad_chip_sum_small", src_s, land_s[0], land_s[1], land_s[2])
    *small_grads, loss = _flat_unpack(small_sum, [full_shapes[n] for n in SMALL] + [()])
    small_full = dict(zip(SMALL, small_grads))
    for n in SMALL:
        a = small_full[n]
        if n in SMALL_SHARDED:
            width = wts[n].shape[-1]
            a = lax.dynamic_slice_in_dim(a, chip * width, width, axis=a.ndim - 1)
        grads[n] = a.reshape(wts[n].shape)

    rows_of = lambda a: a.reshape(-1, a.shape[-1])
    outs = _adamw_small(*[[rows_of(src[n]) for n in SMALL] for src in (wts, grads, mom, var)])
    for j, dst in enumerate((delta, new_m, new_v)):
        dst.update({n: outs[j * len(SMALL) + i].reshape(wts[n].shape) for i, n in enumerate(SMALL)})

    return (loss, grad_x, *[grads[n] for n in WEIGHTS], *[delta[n] for n in WEIGHTS],
            *[new_m[n] for n in WEIGHTS], *[new_v[n] for n in WEIGHTS])
```

```python
import math

import jax
import jax.numpy as jnp
from jax import lax
from jax.experimental import pallas as pl
from jax.experimental.pallas import tpu as pltpu

F32 = jnp.float32
BF = jnp.bfloat16
MESH = pl.DeviceIdType.MESH

D = 1024
SEQ = 2048
N_META = 16
T = N_META + SEQ
N_HEADS = 8
QK_NOPE = 64
QK_ROPE = 32
QK_HEAD = 96
V_HEAD = 64
Q_LORA = 384
KV_LORA = 256
D_ATTN = 512
D_RNN = 512
RNN_BW = 64
CONV_W = 4
LRU_C = 8.0
ROPE_THETA = 10000.0
D_FF = 2816
EPS = 1e-6
IN_COLS = 1696
ADAM_LR, ADAM_B1, ADAM_B2, ADAM_EPS, ADAM_WD, ADAM_STEP = 0.001, 0.9, 0.999, 1e-08, 0.01, 10

LANES = 128
TP = 2176
NB = 2
R = NB * TP
TR = 256
TRF = 272
TQ = 1088
HP = LANES
PC = 1792
O_CKV, O_KR, O_XR, O_XG = 384, 640, 768, 1280
CG = 128
N_CG = D_RNN // CG
VMEM_LIMIT = 56 * 1024 * 1024
N_CHIPS = 4
SCALE = QK_HEAD ** -0.5
KEY_MASK = -30000.0
LOG2_E = 1.4426950408889634
SCALE_LOG2 = SCALE * LOG2_E


def _nt(a, b):
    return lax.dot_general(a, b, (((1,), (1,)), ((), ())), preferred_element_type=F32)


def _nn(a, b):
    return jnp.dot(a, b, preferred_element_type=F32)


def _tn(a, b):
    return lax.dot_general(a, b, (((0,), (0,)), ((), ())), preferred_element_type=F32)


def _rms(x, g, n):
    ms = jnp.sum(x * x, axis=-1, keepdims=True) * (1.0 / n)
    return x * lax.rsqrt(ms + EPS) * g


def _lane_sum(y):
    return jnp.sum(y, axis=-1, keepdims=True)


def _rot(x):
    lane = lax.broadcasted_iota(jnp.int32, x.shape, 1)
    left = pltpu.roll(x, HP - 16, 1)
    right = pltpu.roll(x, 16, 1)
    lo = (lane >= QK_NOPE) & (lane < QK_NOPE + 16)
    hi = (lane >= QK_NOPE + 16) & (lane < QK_HEAD)
    return jnp.where(lo, -left, jnp.where(hi, right, 0.0))


def _head(x, g, cs, sn):
    n = x * lax.rsqrt(_lane_sum(x * x) * (1.0 / QK_HEAD) + EPS) * g
    return n * cs + _rot(n) * sn


def _head_bwd(x, g, cs, sn, dout):
    rs = lax.rsqrt(_lane_sum(x * x) * (1.0 / QK_HEAD) + EPS)
    xh = x * rs
    dn = dout * cs - _rot(dout * sn)
    gdn = g * dn
    t = _lane_sum(gdn * xh) * (1.0 / QK_HEAD)
    return rs * (gdn - xh * t), jnp.sum(dn * xh, axis=0, keepdims=True)


def _const_spec(shape):
    return pl.BlockSpec(shape, lambda *_: (0,) * len(shape), pipeline_mode=pl.Buffered(1))


def _row_spec(n, tr=TR):
    return pl.BlockSpec((tr, n), lambda i: (i, 0))


def _params(*sem, vmem=VMEM_LIMIT):
    return pltpu.CompilerParams(dimension_semantics=sem, vmem_limit_bytes=vmem)


def _stage_a_fwd(hp, cs, sn, cw):
    def body(hp_ref, cs_ref, sn_ref, ln1, win, qag, wq, kvag, wk, wv, qg, kg,
             pa_ref, xr_ref, xg_ref, q_ref, k_ref, v_ref):
        hn = _rms(hp_ref[...], ln1[...], D).astype(BF)
        p = _nt(hn, win[...])
        pa_ref[...] = p[:, :O_XR]
        xr_ref[...] = p[:, O_XR:O_XG]
        xg_ref[...] = p[:, O_XG:]
        cqn = _rms(p[:, :O_CKV], qag[...], Q_LORA).astype(BF)
        ckvn = _rms(p[:, O_CKV:O_KR], kvag[...], KV_LORA).astype(BF)
        kr = p[:, O_KR:O_XR]
        c, s = cs_ref[...], sn_ref[...]
        mask_lane = lax.broadcasted_iota(jnp.int32, (1, HP), 1) == QK_HEAD
        row = pl.program_id(0) * TRF + lax.broadcasted_iota(jnp.int32, (TRF, 1), 0)
        key_mask = jnp.where(jnp.where(row >= TP, row - TP, row) < T, 0.0, KEY_MASK)
        qraw = _nt(cqn, wq[...])
        kraw = _nt(ckvn, wk[...])
        for h in range(N_HEADS):
            sl = slice(h * HP, (h + 1) * HP)
            q_ref[:, sl] = jnp.where(mask_lane, 1.0, _head(qraw[:, sl], qg[...], c, s)).astype(BF)
            k_ref[:, sl] = jnp.where(mask_lane, key_mask, _head(kraw[:, sl] + kr, kg[...], c, s)).astype(BF)
        v_ref[...] = _nt(ckvn, wv[...]).astype(BF)

    rs = lambda n: _row_spec(n, TRF)
    return pl.pallas_call(
        body, grid=(R // TRF,), name="stage_a_fwd",
        in_specs=[rs(D), rs(HP), rs(HP), _const_spec((1, D)), _const_spec((PC, D)),
                  _const_spec((1, Q_LORA)), _const_spec((N_HEADS * HP, Q_LORA)), _const_spec((1, KV_LORA)),
                  _const_spec((N_HEADS * HP, KV_LORA)), _const_spec((D_ATTN, KV_LORA)), _const_spec((1, HP)),
                  _const_spec((1, HP))],
        out_specs=[rs(O_XR), rs(D_RNN), rs(D_RNN), rs(N_HEADS * HP), rs(N_HEADS * HP), rs(D_ATTN)],
        out_shape=[jax.ShapeDtypeStruct((R, O_XR), F32), jax.ShapeDtypeStruct((R, D_RNN), F32),
                   jax.ShapeDtypeStruct((R, D_RNN), F32), jax.ShapeDtypeStruct((R, N_HEADS * HP), BF),
                   jax.ShapeDtypeStruct((R, N_HEADS * HP), BF), jax.ShapeDtypeStruct((R, D_ATTN), BF)],
        compiler_params=_params("arbitrary"),
    )(hp, cs, sn, cw["ln1_g"], cw["win"], cw["qa_g"], cw["wq"], cw["kva_g"], cw["wk"], cw["wv"], cw["q_g"], cw["k_g"])


def _stage_a_bwd(dq, dk, dv, dxr, dxg, dh1, hp, pa, cs, sn, cw):
    def body(dq_ref, dk_ref, dv_ref, dxr_ref, dxg_ref, dh1_ref, hp_ref, pa_ref, cs_ref, sn_ref,
             ln1, win, qag, wq, kvag, wk, wv, qg, kg,
             dhp_ref, dp_ref, dqraw_ref, dkraw_ref, hn_ref, cqn_ref, ckvn_ref,
             dln1_ref, dqag_ref, dkvag_ref, dqg_ref, dkg_ref):
        @pl.when(pl.program_id(0) == 0)
        def _():
            for r in (dln1_ref, dqag_ref, dkvag_ref, dqg_ref, dkg_ref):
                r[...] = jnp.zeros_like(r)

        hn, vjp_ln1 = jax.vjp(lambda h, g: _rms(h, g, D), hp_ref[...], ln1[...])
        hn_ref[...] = hn.astype(BF)
        pa_v = pa_ref[...]
        cqn, vjp_qa = jax.vjp(lambda x, g: _rms(x, g, Q_LORA), pa_v[:, :O_CKV], qag[...])
        ckvn, vjp_kva = jax.vjp(lambda x, g: _rms(x, g, KV_LORA), pa_v[:, O_CKV:O_KR], kvag[...])
        kr = pa_v[:, O_KR:O_XR]
        cqnb, ckvnb = cqn.astype(BF), ckvn.astype(BF)
        cqn_ref[...] = cqnb
        ckvn_ref[...] = ckvnb
        c, s = cs_ref[...], sn_ref[...]
        lane = lax.broadcasted_iota(jnp.int32, (1, HP), 1)
        rope_lanes = ((lane >= QK_NOPE) & (lane < QK_HEAD)).astype(F32)
        dkr = jnp.zeros((TR, HP), F32)
        dqg = jnp.zeros((1, HP), F32)
        dkg = jnp.zeros((1, HP), F32)
        qraw = _nt(cqnb, wq[...])
        kraw = _nt(ckvnb, wk[...])
        for h in range(N_HEADS):
            sl = slice(h * HP, (h + 1) * HP)
            dqraw, dg = _head_bwd(qraw[:, sl], qg[...], c, s, dq_ref[:, sl])
            dqg = dqg + dg
            dqraw_ref[:, sl] = dqraw.astype(BF)
            dkraw, dg = _head_bwd(kraw[:, sl] + kr, kg[...], c, s, dk_ref[:, sl])
            dkg = dkg + dg
            dkraw_ref[:, sl] = dkraw.astype(BF)
            dkr = dkr + dkraw * rope_lanes
        dcq, dqag = vjp_qa(_nn(dqraw_ref[...], wq[...]))
        dckv, dkvag = vjp_kva(_nn(dkraw_ref[...], wk[...]) + _nn(dv_ref[...].astype(BF), wv[...]))
        dpb = jnp.concatenate([dcq, dckv, dkr, dxr_ref[...], dxg_ref[...]], axis=1).astype(BF)
        dp_ref[...] = dpb
        dh, dln1 = vjp_ln1(_nn(dpb, win[...]))
        dhp_ref[...] = dh + dh1_ref[...]
        dln1_ref[...] += dln1
        dqag_ref[...] += dqag
        dkvag_ref[...] += dkvag
        dqg_ref[...] += dqg
        dkg_ref[...] += dkg

    acc = lambda n: pl.BlockSpec((1, n), lambda i: (0, 0))
    return pl.pallas_call(
        body, grid=(R // TR,), name="stage_a_bwd",
        in_specs=[_row_spec(N_HEADS * HP), _row_spec(N_HEADS * HP), _row_spec(D_ATTN), _row_spec(D_RNN),
                  _row_spec(D_RNN), _row_spec(D), _row_spec(D), _row_spec(O_XR), _row_spec(HP), _row_spec(HP),
                  _const_spec((1, D)), _const_spec((PC, D)), _const_spec((1, Q_LORA)),
                  _const_spec((N_HEADS * HP, Q_LORA)), _const_spec((1, KV_LORA)),
                  _const_spec((N_HEADS * HP, KV_LORA)), _const_spec((D_ATTN, KV_LORA)), _const_spec((1, HP)),
                  _const_spec((1, HP))],
        out_specs=[_row_spec(D), _row_spec(PC), _row_spec(N_HEADS * HP), _row_spec(N_HEADS * HP), _row_spec(D),
                   _row_spec(Q_LORA), _row_spec(KV_LORA), acc(D), acc(Q_LORA), acc(KV_LORA), acc(HP), acc(HP)],
        out_shape=[jax.ShapeDtypeStruct((R, D), F32), jax.ShapeDtypeStruct((R, PC), BF),
                   jax.ShapeDtypeStruct((R, N_HEADS * HP), BF), jax.ShapeDtypeStruct((R, N_HEADS * HP), BF),
                   jax.ShapeDtypeStruct((R, D), BF), jax.ShapeDtypeStruct((R, Q_LORA), BF),
                   jax.ShapeDtypeStruct((R, KV_LORA), BF), jax.ShapeDtypeStruct((1, D), F32),
                   jax.ShapeDtypeStruct((1, Q_LORA), F32), jax.ShapeDtypeStruct((1, KV_LORA), F32),
                   jax.ShapeDtypeStruct((1, HP), F32), jax.ShapeDtypeStruct((1, HP), F32)],
        compiler_params=_params("arbitrary"),
    )(dq, dk, dv, dxr, dxg, dh1, hp, pa, cs, sn, cw["ln1_g"], cw["win"], cw["qa_g"], cw["wq"], cw["kva_g"],
      cw["wk"], cw["wv"], cw["q_g"], cw["k_g"])


def _head_mask(half, dtype):
    lane = lax.broadcasted_iota(jnp.int32, (1, 2 * V_HEAD), 1)
    return ((lane >= V_HEAD) == (half == 1)).astype(dtype)


def _attn_specs(tq):
    n_q = TP // tq
    return (NB, N_HEADS // 2, n_q), dict(
        q=pl.BlockSpec((tq, 2 * HP), lambda b, j, i: (b * n_q + i, j)),
        k=pl.BlockSpec((TP, 2 * HP), lambda b, j, i: (b, j)),
        v=pl.BlockSpec((TP, 2 * V_HEAD), lambda b, j, i: (b, j)),
        o=pl.BlockSpec((tq, 2 * V_HEAD), lambda b, j, i: (b * n_q + i, j)),
        lse=pl.BlockSpec((None, tq, 2), lambda b, j, i: (j, b * n_q + i, 0)))


TQF = 1088


def _attn_fwd(q, k, v):
    def body(q_ref, k_ref, v_ref, o_ref, lse_ref):
        v2 = v_ref[...]
        o = jnp.zeros((TQF, 2 * V_HEAD), F32)
        lse = []
        for hh in range(2):
            sl = slice(hh * HP, (hh + 1) * HP)
            raw = _nt(q_ref[:, sl], k_ref[:, sl])
            m = jnp.max(raw, axis=-1, keepdims=True)
            e = jnp.exp2((raw - m) * SCALE_LOG2)
            l = jnp.sum(e, axis=-1, keepdims=True)
            o = o + _nn(e.astype(BF), v2 * _head_mask(hh, BF)) * (1.0 / l)
            lse.append(m * SCALE_LOG2 + jnp.log(l) * LOG2_E)
        o_ref[...] = o
        lane = lax.broadcasted_iota(jnp.int32, (TQF, 2), 1)
        lse_ref[...] = jnp.where(lane == 0, lse[0], lse[1])

    grid, sp = _attn_specs(TQF)
    return pl.pallas_call(
        body, grid=grid, name="attn_fwd", in_specs=[sp["q"], sp["k"], sp["v"]], out_specs=[sp["o"], sp["lse"]],
        out_shape=[jax.ShapeDtypeStruct((R, D_ATTN), F32), jax.ShapeDtypeStruct((N_HEADS // 2, R, 2), F32)],
        compiler_params=_params("arbitrary", "arbitrary", "arbitrary"),
    )(q, k, v)


def _attn_bwd(q, k, v, o, lse, do):
    def body(q_ref, k_ref, v_ref, o_ref, lse_ref, do_ref, dq_ref, dk_ref, dv_ref):
        @pl.when(pl.program_id(2) == 0)
        def _():
            dk_ref[...] = jnp.zeros_like(dk_ref)
            dv_ref[...] = jnp.zeros_like(dv_ref)

        do = do_ref[...]
        dob = do.astype(BF)
        do_o = do * o_ref[...]
        v2 = v_ref[...]
        dv_sum = jnp.zeros((TP, 2 * V_HEAD), F32)
        for hh in range(2):
            sl = slice(hh * HP, (hh + 1) * HP)
            qb, kb = q_ref[:, sl], k_ref[:, sl]
            p = jnp.exp2(_nt(qb, kb) * SCALE_LOG2 - lse_ref[:, hh:hh + 1])
            dp = _nt(dob, v2 * _head_mask(hh, BF))
            delta = jnp.sum(do_o * _head_mask(hh, F32), axis=-1, keepdims=True)
            dsb = (p * (dp - delta)).astype(BF)
            dq_ref[:, sl] = _nn(dsb, kb) * SCALE
            dk_ref[:, sl] += _tn(dsb, qb) * SCALE
            dv_sum = dv_sum + _tn(p.astype(BF), dob) * _head_mask(hh, F32)
        dv_ref[...] += dv_sum

    grid, sp = _attn_specs(TQ)
    return pl.pallas_call(
        body, grid=grid, name="attn_bwd", in_specs=[sp["q"], sp["k"], sp["v"], sp["o"], sp["lse"], sp["o"]],
        out_specs=[sp["q"], sp["k"], sp["v"]],
        out_shape=[jax.ShapeDtypeStruct((R, N_HEADS * HP), F32), jax.ShapeDtypeStruct((R, N_HEADS * HP), F32),
                   jax.ShapeDtypeStruct((R, D_ATTN), F32)],
        compiler_params=_params("arbitrary", "arbitrary", "arbitrary"),
    )(q, k, v, o, lse, do)


SEG = TP // 8


def _scan_pair(af_ref, bf_ref, hf_ref, ab_ref, bb_ref, hb_ref, pf_ref, pb_ref):
    unroll = 8

    def step(i, carry):
        hf, pf, hb, pb = carry
        for u in range(unroll):
            j = i * unroll + u
            rows_f, rows_b = pl.ds(j, 8, stride=SEG), pl.ds(SEG - 1 - j, 8, stride=SEG)
            a = af_ref[rows_f, :]
            hf, pf = a * hf + bf_ref[rows_f, :], a * pf
            hf_ref[rows_f, :] = hf
            pf_ref[rows_f, :] = pf
            a = ab_ref[rows_b, :]
            hb, pb = a * hb + bb_ref[rows_b, :], a * pb
            hb_ref[rows_b, :] = hb
            pb_ref[rows_b, :] = pb
        return hf, pf, hb, pb

    zero, one = jnp.zeros((8, CG), F32), jnp.ones((8, CG), F32)
    hf, pf, hb, pb = lax.fori_loop(0, SEG // unroll, step, (zero, one, zero, one))
    seg = lax.broadcasted_iota(jnp.int32, (8, CG), 0)
    cf, cb = zero, zero
    for s in range(1, 8):
        cf = jnp.where(seg == s, pltpu.roll(hf + pf * cf, 1, 0), cf)
        cb = jnp.where(seg == 7 - s, pltpu.roll(hb + pb * cb, 7, 0), cb)
    for s in range(8):
        rows = slice(s * SEG, (s + 1) * SEG)
        hf_ref[rows, :] = hf_ref[rows, :] + pf_ref[rows, :] * cf[s:s + 1, :]
        hb_ref[rows, :] = hb_ref[rows, :] + pb_ref[rows, :] * cb[s:s + 1, :]


def _shifts(x):
    t = lax.broadcasted_iota(jnp.int32, x.shape, 0)
    xm2 = jnp.where(t >= 2, pltpu.roll(x, 2, 0), 0.0)
    xm1 = jnp.where(t >= 1, pltpu.roll(x, 1, 0), 0.0)
    xp1 = jnp.where(t < TP - 1, pltpu.roll(x, TP - 1, 0), 0.0)
    return xm2, xm1, xp1


def _softplus(z):
    e = jnp.exp(-jnp.abs(z))
    small = e * (1.0 - e * (0.5 - e * (1.0 / 3.0)))
    return jnp.maximum(z, 0.0) + jnp.where(e < 0.01, small, jnp.log(1.0 + e))


def _sigmoid(x):
    return 0.5 * jnp.tanh(0.5 * x) + 0.5


def _one_minus_sq(log_a, a):
    x = 2.0 * log_a
    series = -x * (1.0 + x * 0.5 * (1.0 + x * (1.0 / 3.0) * (1.0 + x * 0.25)))
    return jnp.where(x > -0.05, series, 1.0 - a * a)


def _gates(row0, xc, pa_f, pi_f, pa_b, pi_b, lam_f, lam_b):
    t = row0 + lax.broadcasted_iota(jnp.int32, xc.shape, 0)
    valid = t < T
    out = []
    for pa, pi_, lam in ((pa_f, pi_f, lam_f), (pa_b, pi_b, lam_b)):
        r = _sigmoid(pa)
        gate_i = _sigmoid(pi_)
        log_a = -LRU_C * r * _softplus(-lam)
        a = jnp.exp(log_a)
        mult = jnp.sqrt(jnp.maximum(_one_minus_sq(log_a, a), 0.0))
        out += [a, jnp.where(valid, mult * (gate_i * xc), 0.0)]
    return tuple(out)


def _gates_bwd(row0, xc, pres, lams, cots):
    t = row0 + lax.broadcasted_iota(jnp.int32, xc.shape, 0)
    valid = t < T
    dxc = jnp.zeros_like(xc)
    dpres, dlams = [], []
    for d in range(2):
        pa, pi_, lam = pres[2 * d], pres[2 * d + 1], lams[d]
        da, db = cots[2 * d], jnp.where(valid, cots[2 * d + 1], 0.0)
        r = _sigmoid(pa)
        gate_i = _sigmoid(pi_)
        sp = _softplus(-lam)
        log_a = -LRU_C * r * sp
        a = jnp.exp(log_a)
        m2 = jnp.maximum(_one_minus_sq(log_a, a), 0.0)
        mult = jnp.sqrt(m2)
        dxc = dxc + db * (mult * gate_i)
        d_gate = db * (mult * xc)
        d_m2 = jnp.where(m2 > 0.0, db * (gate_i * xc) * (0.5 * lax.rsqrt(m2)), 0.0)
        d_log_a = da * a - 2.0 * d_m2 * (a * a)
        dpres += [d_log_a * (-LRU_C * sp) * (r * (1.0 - r)), d_gate * (gate_i * (1.0 - gate_i))]
        d_sp = jnp.sum(d_log_a * (-LRU_C * r), axis=0, keepdims=True)
        dlams.append(-d_sp * jax.nn.sigmoid(-lam))
    return dxc, dpres, dlams


def _rnn_specs():
    seq = pl.BlockSpec((TP, CG), lambda g, b: (b, g))
    return dict(
        seq=seq,
        cw=pl.BlockSpec((CONV_W, CG), lambda g, b: (0, g)),
        cb=pl.BlockSpec((1, CG), lambda g, b: (0, g)),
        w4=pl.BlockSpec((None, CG, 4 * CG), lambda g, b: (g, 0, 0)),
        b4=pl.BlockSpec((None, 1, 4 * CG), lambda g, b: (g, 0, 0)),
        lam=pl.BlockSpec((None, 1, 2 * CG), lambda g, b: (g, 0, 0)),
    )


def _conv(x, xm2, xm1, xp1, cw_ref, cb_ref):
    return cw_ref[0:1, :] * xm2 + cw_ref[1:2, :] * xm1 + cw_ref[2:3, :] * x + cw_ref[3:4, :] * xp1 + cb_ref[...]


TC = 128
N_TC = TP // TC


def _split4(pre):
    return pre[:, :CG], pre[:, CG:2 * CG], pre[:, 2 * CG:3 * CG], pre[:, 3 * CG:]


def _rnn_fwd(xr, xg, cw):
    def body(xr_ref, xg_ref, cw_ref, cb_ref, w4_ref, b4_ref, lam_ref, y_ref, hf_ref, hb_ref, af_ref, ab_ref, xc_ref,
             af, bf, ab, bb, pf, pb):
        x = xr_ref[...]
        xc_ref[...] = _conv(x, *_shifts(x), cw_ref, cb_ref)
        lam = lam_ref[...]

        def chunk(i, _):
            rows = pl.ds(pl.multiple_of(i * TC, TC), TC)
            xc = xc_ref[rows, :]
            pre = _nn(xc.astype(BF), w4_ref[...]) + b4_ref[...]
            a_f, b_f, a_b, b_b = _gates(i * TC, xc, *_split4(pre), lam[:, :CG], lam[:, CG:])
            af[rows, :] = a_f
            bf[rows, :] = b_f
            ab[rows, :] = a_b
            bb[rows, :] = b_b
            af_ref[rows, :] = a_f
            ab_ref[rows, :] = a_b
            return 0

        lax.fori_loop(0, N_TC, chunk, 0)
        _scan_pair(af, bf, hf_ref, ab, bb, hb_ref, pf, pb)
        y_ref[...] = (hf_ref[...] + hb_ref[...]) * jax.nn.gelu(xg_ref[...])

    sp = _rnn_specs()
    return pl.pallas_call(
        body, grid=(N_CG, NB), name="rnn_fwd",
        in_specs=[sp["seq"], sp["seq"], sp["cw"], sp["cb"], sp["w4"], sp["b4"], sp["lam"]],
        out_specs=[sp["seq"]] * 6, out_shape=[jax.ShapeDtypeStruct((R, D_RNN), F32)] * 6,
        scratch_shapes=[pltpu.VMEM((TP, CG), F32)] * 6,
        compiler_params=_params("arbitrary", "arbitrary"),
    )(xr, xg, cw["conv_w"], cw["conv_b"], cw["w4"], cw["b4"], cw["lam"])


def _rnn_bwd(dy, xr, xg, hf, hb, af, ab, xc, cw):
    def body(dy_ref, xr_ref, xg_ref, hf_ref, hb_ref, af_ref, ab_ref, xc_s, cw_ref, cb_ref, w4_ref, b4_ref, lam_ref,
             dxr_ref, dxg_ref, dcw_ref, dcb_ref, dw4_ref, db4_ref, dlam_ref,
             af_s, ab_s, dhs_s, lf_s, lb_s, daf_s, dab_s, dxc_s):
        @pl.when(pl.program_id(1) == 0)
        def _():
            for r in (dcw_ref, dcb_ref, dw4_ref, db4_ref, dlam_ref):
                r[...] = jnp.zeros_like(r)

        lam = lam_ref[...]

        def chunk1(i, _):
            rows = pl.ds(pl.multiple_of(i * TC, TC), TC)
            _, vjp_y = jax.vjp(lambda h, g: h * jax.nn.gelu(g), hf_ref[rows, :] + hb_ref[rows, :], xg_ref[rows, :])
            dhs, dxg = vjp_y(dy_ref[rows, :])
            dhs_s[rows, :] = dhs
            dxg_ref[rows, :] = dxg
            return 0

        lax.fori_loop(0, N_TC, chunk1, 0)
        t = lax.broadcasted_iota(jnp.int32, (TP, CG), 0)
        af_s[...] = pltpu.roll(af_ref[...], TP - 1, 0)
        ab_s[...] = pltpu.roll(ab_ref[...], 1, 0)
        _scan_pair(ab_s, dhs_s, lb_s, af_s, dhs_s, lf_s, dab_s, daf_s)
        daf_s[...] = lf_s[...] * jnp.where(t >= 1, pltpu.roll(hf_ref[...], 1, 0), 0.0)
        dab_s[...] = lb_s[...] * jnp.where(t < TP - 1, pltpu.roll(hb_ref[...], TP - 1, 0), 0.0)

        def chunk2(i, _):
            rows = pl.ds(pl.multiple_of(i * TC, TC), TC)
            xc = xc_s[rows, :]
            xcb = xc.astype(BF)
            pre = _nn(xcb, w4_ref[...]) + b4_ref[...]
            dxc, dpres, dlams = _gates_bwd(i * TC, xc, _split4(pre), (lam[:, :CG], lam[:, CG:]),
                                           (daf_s[rows, :], lf_s[rows, :], dab_s[rows, :], lb_s[rows, :]))
            dpre = jnp.concatenate(dpres, axis=1)
            dpreb = dpre.astype(BF)
            dxc_s[rows, :] = dxc + _nt(dpreb, w4_ref[...])
            dw4_ref[...] += _tn(xcb, dpreb)
            db4_ref[...] += jnp.sum(dpre, axis=0, keepdims=True)
            dlam_ref[...] += jnp.concatenate(dlams, axis=1)
            return 0

        lax.fori_loop(0, N_TC, chunk2, 0)
        dxc = dxc_s[...]
        x = xr_ref[...]
        taps = (jnp.where(t < TP - 2, pltpu.roll(dxc, TP - 2, 0), 0.0), jnp.where(t < TP - 1, pltpu.roll(dxc, TP - 1, 0), 0.0),
                dxc, jnp.where(t >= 1, pltpu.roll(dxc, 1, 0), 0.0))
        dcb_ref[...] += jnp.sum(dxc, axis=0, keepdims=True)
        dxr = jnp.zeros_like(dxc)
        for tap, shifted in enumerate(taps):
            dcw_ref[tap:tap + 1, :] += jnp.sum(x * shifted, axis=0, keepdims=True)
            dxr = dxr + cw_ref[tap:tap + 1, :] * shifted
        dxr_ref[...] = dxr

    sp = _rnn_specs()
    return pl.pallas_call(
        body, grid=(N_CG, NB), name="rnn_bwd",
        in_specs=[sp["seq"]] * 8 + [sp["cw"], sp["cb"], sp["w4"], sp["b4"], sp["lam"]],
        out_specs=[sp["seq"], sp["seq"], sp["cw"], sp["cb"], sp["w4"], sp["b4"], sp["lam"]],
        out_shape=[jax.ShapeDtypeStruct((R, D_RNN), F32), jax.ShapeDtypeStruct((R, D_RNN), F32),
                   jax.ShapeDtypeStruct((CONV_W, D_RNN), F32), jax.ShapeDtypeStruct((1, D_RNN), F32),
                   jax.ShapeDtypeStruct((N_CG, CG, 4 * CG), F32), jax.ShapeDtypeStruct((N_CG, 1, 4 * CG), F32),
                   jax.ShapeDtypeStruct((N_CG, 1, 2 * CG), F32)],
        scratch_shapes=[pltpu.VMEM((TP, CG), F32)] * 8,
        compiler_params=_params("arbitrary", "arbitrary"),
    )(dy, xr, xg, hf, hb, af, ab, xc, cw["conv_w"], cw["conv_b"], cw["w4"], cw["b4"], cw["lam"])


TD = 256
STAGE_D_VMEM = 58 * 1024 * 1024


def _stage_d(hp, o, y, tgt, cw):
    def body(hp_ref, o_ref, y_ref, tgt_ref, ga, gr, wout, ln2, wg, wu, wd,
             do_ref, dy_ref, dh1_ref, mix_ref, dh1b_ref, hn2_ref, dg_ref, du_ref, act_ref, dh2b_ref,
             loss_ref, dga_ref, dgr_ref, dln2_ref):
        i = pl.program_id(0)

        @pl.when(i == 0)
        def _():
            for r in (loss_ref, dga_ref, dgr_ref, dln2_ref):
                r[...] = jnp.zeros_like(r)

        mix_a, vjp_a = jax.vjp(lambda x, g: _rms(x, g, D_ATTN), o_ref[...], ga[...])
        mix_r, vjp_r = jax.vjp(lambda x, g: _rms(x, g, D_RNN), y_ref[...], gr[...])
        mab, mrb = mix_a.astype(BF), mix_r.astype(BF)
        mix_ref[:, :D_ATTN] = mab
        mix_ref[:, D_ATTN:] = mrb
        h1 = hp_ref[...] + _nn(mab, wout[:D_ATTN, :]) + _nn(mrb, wout[D_ATTN:, :])
        hn2, vjp_ln2 = jax.vjp(lambda x, g: _rms(x, g, D), h1, ln2[...])
        hn2b = hn2.astype(BF)
        hn2_ref[...] = hn2b
        act, vjp_act = jax.vjp(lambda g, u: jax.nn.silu(g) * u, _nt(hn2b, wg[...]), _nt(hn2b, wu[...]))
        actb = act.astype(BF)
        act_ref[...] = actb
        h2 = h1 + _nn(actb, wd[...])
        row = i * TD + lax.broadcasted_iota(jnp.int32, (TD, 1), 0)
        t = jnp.where(row >= TP, row - TP, row)
        err = jnp.where((t >= N_META) & (t < T), h2 - tgt_ref[...], 0.0)
        loss_ref[...] += jnp.sum(err * err) * (0.5 / D)
        dh2b = (err * (1.0 / D)).astype(BF)
        dh2b_ref[...] = dh2b
        dg, du = vjp_act(_nt(dh2b, wd[...]))
        dgb, dub = dg.astype(BF), du.astype(BF)
        dg_ref[...] = dgb
        du_ref[...] = dub
        dh1n, dln2 = vjp_ln2(_nn(dgb, wg[...]) + _nn(dub, wu[...]))
        dh1 = err * (1.0 / D) + dh1n
        dh1_ref[...] = dh1
        dh1b = dh1.astype(BF)
        dh1b_ref[...] = dh1b
        dmix = _nt(dh1b, wout[...])
        do, dga = vjp_a(dmix[:, :D_ATTN])
        dyr, dgr = vjp_r(dmix[:, D_ATTN:])
        do_ref[...] = do
        dy_ref[...] = dyr
        dga_ref[...] += dga
        dgr_ref[...] += dgr
        dln2_ref[...] += dln2

    rs = lambda n: _row_spec(n, TD)
    acc = lambda n: pl.BlockSpec((1, n), lambda i: (0, 0))
    return pl.pallas_call(
        body, grid=(R // TD,), name="stage_d",
        in_specs=[rs(D), rs(D_ATTN), rs(D_RNN), rs(D), _const_spec((1, D_ATTN)), _const_spec((1, D_RNN)),
                  _const_spec((D, D)), _const_spec((1, D)), _const_spec((D_FF, D)), _const_spec((D_FF, D)),
                  _const_spec((D_FF, D))],
        out_specs=[rs(D_ATTN), rs(D_RNN), rs(D), rs(D), rs(D), rs(D), rs(D_FF), rs(D_FF), rs(D_FF), rs(D),
                   acc(1), acc(D_ATTN), acc(D_RNN), acc(D)],
        out_shape=[jax.ShapeDtypeStruct((R, D_ATTN), F32), jax.ShapeDtypeStruct((R, D_RNN), F32),
                   jax.ShapeDtypeStruct((R, D), F32), jax.ShapeDtypeStruct((R, D), BF),
                   jax.ShapeDtypeStruct((R, D), BF), jax.ShapeDtypeStruct((R, D), BF),
                   jax.ShapeDtypeStruct((R, D_FF), BF), jax.ShapeDtypeStruct((R, D_FF), BF),
                   jax.ShapeDtypeStruct((R, D_FF), BF), jax.ShapeDtypeStruct((R, D), BF),
                   jax.ShapeDtypeStruct((1, 1), F32), jax.ShapeDtypeStruct((1, D_ATTN), F32),
                   jax.ShapeDtypeStruct((1, D_RNN), F32), jax.ShapeDtypeStruct((1, D), F32)],
        compiler_params=_params("arbitrary", vmem=STAGE_D_VMEM),
    )(hp, o, y, tgt, cw["ga"], cw["gr"], cw["wout"], cw["ln2_g"], cw["wg"], cw["wu"], cw["wd"])


TW = 2176


def _wgrad(a, b, name, tk=None):
    ka, nb = a.shape[1], b.shape[1]
    tk = ka if tk is None else tk

    def body(a_ref, b_ref, o_ref):
        @pl.when(pl.program_id(1) == 0)
        def _():
            o_ref[...] = jnp.zeros_like(o_ref)

        o_ref[...] += _tn(a_ref[...].astype(BF), b_ref[...].astype(BF))

    return pl.pallas_call(
        body, grid=(ka // tk, R // TW), name=name,
        in_specs=[pl.BlockSpec((TW, tk), lambda k, r: (r, k)), pl.BlockSpec((TW, nb), lambda k, r: (r, 0))],
        out_specs=pl.BlockSpec((tk, nb), lambda k, r: (k, 0)),
        out_shape=jax.ShapeDtypeStruct((ka, nb), F32),
        compiler_params=_params("arbitrary", "arbitrary"),
    )(a, b)


def _wgrad_heads(dq, dk, dv, cqn, ckvn):
    def body(dq_ref, dk_ref, dv_ref, cqn_ref, ckvn_ref, oq_ref, ok_ref, ov_ref):
        @pl.when(pl.program_id(0) == 0)
        def _():
            for r in (oq_ref, ok_ref, ov_ref):
                r[...] = jnp.zeros_like(r)

        ckvnb = ckvn_ref[...]
        oq_ref[...] += _tn(dq_ref[...], cqn_ref[...])
        ok_ref[...] += _tn(dk_ref[...], ckvnb)
        ov_ref[...] += _tn(dv_ref[...].astype(BF), ckvnb)

    rows = lambda a: pl.BlockSpec((TW, a.shape[1]), lambda r: (r, 0))
    full = lambda m, n: pl.BlockSpec((m, n), lambda r: (0, 0))
    shapes = [(dq.shape[1], cqn.shape[1]), (dk.shape[1], ckvn.shape[1]), (dv.shape[1], ckvn.shape[1])]
    return pl.pallas_call(
        body, grid=(R // TW,), name="wgrad_heads", in_specs=[rows(a) for a in (dq, dk, dv, cqn, ckvn)],
        out_specs=[full(*s) for s in shapes], out_shape=[jax.ShapeDtypeStruct(s, F32) for s in shapes],
        compiler_params=_params("arbitrary"),
    )(dq, dk, dv, cqn, ckvn)


def _rope_tables():
    half = QK_ROPE // 2
    freqs = 1.0 / (ROPE_THETA ** (jnp.arange(half, dtype=F32) / half))
    ang = jnp.arange(TP, dtype=F32)[:, None] * freqs[None, :]
    ones = jnp.ones((TP, QK_NOPE), F32)
    zeros = jnp.zeros((TP, QK_NOPE), F32)
    pad1 = jnp.ones((TP, HP - QK_HEAD), F32)
    pad0 = jnp.zeros((TP, HP - QK_HEAD), F32)
    cs = jnp.concatenate([ones, jnp.cos(ang), jnp.cos(ang), pad1], axis=1)
    sn = jnp.concatenate([zeros, jnp.sin(ang), jnp.sin(ang), pad0], axis=1)
    return jnp.tile(cs, (NB, 1)), jnp.tile(sn, (NB, 1))


def _pad_rows(a, lo, hi):
    return jnp.pad(a, ((0, 0), (lo, hi), (0, 0)))


def _pad_target(target):
    return _pad_rows(target, N_META, TP - T).reshape(R, D)


def _gate_weights(lru_wa, lru_wi):
    gates = jnp.stack([lru_wa[0], lru_wi[0], lru_wa[1], lru_wi[1]])
    blk = gates.reshape(4, N_CG, 2, RNN_BW, RNN_BW)
    dense = jnp.einsum("tcaij,ab->tcaibj", blk, jnp.eye(2, dtype=F32)).reshape(4, N_CG, CG, CG)
    return dense.transpose(1, 2, 0, 3).reshape(N_CG, CG, 4 * CG).astype(BF)


def _compute_weights(w):
    win_t = w["w_in_t"]
    kr = win_t[O_KR:O_KR + QK_ROPE]
    win = jnp.concatenate([win_t[:O_KR], jnp.zeros((QK_NOPE, D), F32), kr,
                           jnp.zeros((HP - QK_HEAD, D), F32), win_t[O_KR + QK_ROPE:]], axis=0)
    wq = _pad_rows(w["w_uq_t"].reshape(N_HEADS, QK_HEAD, Q_LORA), 0, HP - QK_HEAD)
    wkv = w["w_ukv_t"].reshape(N_HEADS, QK_NOPE + V_HEAD, KV_LORA)
    wk = _pad_rows(wkv[:, :QK_NOPE], 0, HP - QK_NOPE)
    wv = wkv[:, QK_NOPE:].reshape(D_ATTN, KV_LORA)
    bias =jnp.stack([w["lru_ba"][0], w["lru_bi"][0], w["lru_ba"][1], w["lru_bi"][1]])
    b4 = bias.reshape(4, N_CG, CG).transpose(1, 0, 2).reshape(N_CG, 1, 4 * CG)
    lam = w["lru_lambda"].reshape(2, N_CG, CG).transpose(1, 0, 2).reshape(N_CG, 1, 2 * CG)
    pad_g = lambda g: jnp.pad(g.reshape(1, QK_HEAD), ((0, 0), (0, HP - QK_HEAD)))
    return dict(
        ln1_g=w["ln1_g"].reshape(1, D), win=win.astype(BF), qa_g=w["q_a_norm_g"].reshape(1, Q_LORA),
        wq=wq.astype(BF).reshape(N_HEADS * HP, Q_LORA), kva_g=w["kv_a_norm_g"].reshape(1, KV_LORA),
        wk=wk.astype(BF).reshape(N_HEADS * HP, KV_LORA), wv=wv.astype(BF),
        q_g=pad_g(w["q_norm_g"]), k_g=pad_g(w["k_norm_g"]),
        conv_w=w["conv_w"].reshape(CONV_W, D_RNN), conv_b=w["conv_b"].reshape(1, D_RNN),
        w4=w["w4"] if "w4" in w else _gate_weights(w["lru_wa"], w["lru_wi"]), b4=b4, lam=lam,
        ga=w["attn_out_g"].reshape(1, D_ATTN), gr=w["rnn_out_g"].reshape(1, D_RNN), ln2_g=w["ln2_g"].reshape(1, D),
    )


def _local_step(x, target, meta, w, late_forward, late_weights, early_grads, mid_grads):
    cw = _compute_weights(w)
    cs, sn = w["rope"] if "rope" in w else _rope_tables()
    hp = jnp.concatenate([jnp.broadcast_to(meta[None], (NB, N_META, D)), x,
                          jnp.zeros((NB, TP - T, D), F32)], axis=1).reshape(R, D)
    tgt = target if target.ndim == 2 else _pad_target(target)

    pa, xr, xg, q, k, v = _stage_a_fwd(hp, cs, sn, cw)
    o, lse = _attn_fwd(q, k, v)
    cw["conv_b"] = cw["conv_b"] + late_forward([o])
    y, hf, hb, af, ab, xc = _rnn_fwd(xr, xg, cw)
    late = late_weights([y])
    cw.update(wout=late["w_out"], wg=late["w_gate_t"], wu=late["w_up_t"], wd=late["w_down"])
    (do, dy, dh1, mixb, dh1b, hn2b, dgb, dub, actb, dh2b, loss, dga, dgr, dln2) = _stage_d(hp, o, y, tgt, cw)
    dwout = _wgrad(mixb, dh1b, "wgrad_out")
    dwg = _wgrad(dgb, hn2b, "wgrad_gate", tk=D_FF // 2)
    dwu = _wgrad(dub, hn2b, "wgrad_up", tk=D_FF // 2)
    dwd = _wgrad(actb, dh2b, "wgrad_down", tk=D_FF // 2)
    zero = early_grads(dict(w_out=dwout, w_gate=dwg, w_up=dwu, w_down=dwd))
    cw["conv_b"] = cw["conv_b"] + zero
    dxr, dxg, dcw, dcb, dw4, db4, dlam = _rnn_bwd(dy, xr, xg, hf, hb, af, ab, xc, cw)
    zero = mid_grads([dxr])
    dq, dk, dv = _attn_bwd(q, k, v, o, lse, do)
    (dhp, dpb, dqrawb, dkrawb, hn1b, cqnb, ckvnb, dln1, dqag, dkvag, dqg, dkg) = _stage_a_bwd(
        dq, dk, dv, dxr, dxg, dh1, hp, pa, cs, sn, dict(cw, qa_g=cw["qa_g"] + zero))

    dwin = _wgrad(dpb, hn1b, "wgrad_in")
    dwq, dwk, dwv = _wgrad_heads(dqrawb, dkrawb, dv, cqnb, ckvnb)

    dwin_t = jnp.concatenate([dwin[:O_KR], dwin[O_KR + QK_NOPE:O_KR + QK_HEAD], dwin[O_XR:]], axis=0)
    dwq_t = dwq.reshape(N_HEADS, HP, Q_LORA)[:, :QK_HEAD].reshape(N_HEADS * QK_HEAD, Q_LORA)
    dwkv_t = jnp.concatenate([dwk.reshape(N_HEADS, HP, KV_LORA)[:, :QK_NOPE],
                              dwv.reshape(N_HEADS, V_HEAD, KV_LORA)], axis=1).reshape(2 * D_ATTN, KV_LORA)
    d4 = dw4.reshape(N_CG, 2, RNN_BW, 4, 2, RNN_BW)
    dgates = jnp.stack([d4[:, 0, :, :, 0, :], d4[:, 1, :, :, 1, :]], axis=1)
    dgates = dgates.transpose(3, 0, 1, 2, 4).reshape(4, N_HEADS, RNN_BW, RNN_BW)
    dbias = db4.reshape(N_CG, 4, CG).transpose(1, 0, 2).reshape(4, D_RNN)
    dhp3 = dhp.reshape(NB, TP, D)
    grads = dict(
        meta_tokens=jnp.sum(dhp3[:, :N_META], axis=0),
        ln1_g=dln1, w_in_t=dwin_t, q_a_norm_g=dqag, w_uq_t=dwq_t, kv_a_norm_g=dkvag, w_ukv_t=dwkv_t,
        q_norm_g=dqg[:, :QK_HEAD], k_norm_g=dkg[:, :QK_HEAD], conv_w=dcw[None], conv_b=dcb,
        lru_wa=jnp.stack([dgates[0], dgates[2]])[None], lru_ba=jnp.stack([dbias[0], dbias[2]])[None],
        lru_wi=jnp.stack([dgates[1], dgates[3]])[None], lru_bi=jnp.stack([dbias[1], dbias[3]])[None],
        lru_lambda=dlam.reshape(N_CG, 2, CG).transpose(1, 0, 2).reshape(1, 2, D_RNN),
        attn_out_g=dga, rnn_out_g=dgr, ln2_g=dln2,
    )
    return loss[0, 0], dhp3[:, N_META:T], grads, [dhp, dwin]


_ANY = pl.BlockSpec(memory_space=pl.ANY)


def _place():
    return lax.axis_index("x"), lax.axis_index("y"), lax.axis_index("c")


def _other_chips(x, y):
    return [(1 - x, y), (x, 1 - y), (1 - x, 1 - y)]


def _pair_exchange(big, whole, name):
    n_s, _, m, n = big.shape
    n_copies = n_s + len(whole)

    def body(*refs):
        big_ref, whole_refs = refs[0], refs[1:1 + len(whole)]
        rbig_ref, rwhole_refs = refs[1 + len(whole)], refs[2 + len(whole):2 + 2 * len(whole)]
        send_sems, recv_sems = refs[-2:]
        x, y, c = _place()
        sibling = (x, y, 1 - c)
        copies = [pltpu.make_async_remote_copy(
            src_ref=big_ref.at[s, 1 - c], dst_ref=rbig_ref.at[s], send_sem=send_sems.at[s], recv_sem=recv_sems.at[s],
            device_id=sibling, device_id_type=MESH) for s in range(n_s)]
        copies += [pltpu.make_async_remote_copy(
            src_ref=a, dst_ref=r, send_sem=send_sems.at[n_s + i], recv_sem=recv_sems.at[n_s + i],
            device_id=sibling, device_id_type=MESH) for i, (a, r) in enumerate(zip(whole_refs, rwhole_refs))]
        for cp in copies:
            cp.start()
        for cp in copies:
            cp.wait()

    return pl.pallas_call(
        body, name=name,
        out_shape=[jax.ShapeDtypeStruct((n_s, m, n), big.dtype)] + [jax.ShapeDtypeStruct(a.shape, a.dtype) for a in whole],
        in_specs=[_ANY] * (1 + len(whole)), out_specs=[_ANY] * (1 + len(whole)),
        scratch_shapes=[pltpu.SemaphoreType.DMA((n_copies,)), pltpu.SemaphoreType.DMA((n_copies,))],
    )(big, *whole)


_HBM = pl.BlockSpec(memory_space=pltpu.HBM)
_SEM = pl.BlockSpec(memory_space=pltpu.SEMAPHORE)
_EFFECT = pltpu.SideEffectType.DATAFLOW_SIDE_EFFECTING


def _split_copies(src_refs, land_refs, sems, plan, sending):
    n = len(sems) // 2
    return [pltpu.make_async_remote_copy(src_ref=s, dst_ref=d, send_sem=sems[k], recv_sem=sems[n + k], device_id=to,
                                         device_id_type=MESH)
            for k, (s, d, to) in enumerate(plan(src_refs, land_refs, sending))]


def _to_chips(src_at, land_at):
    def plan(src_refs, land_refs, sending):
        x, y, c = _place()
        return [(src_at(s, tx, ty, c), land_at(l, j, *((x, y) if sending else (tx, ty)), c), (tx, ty, c))
                for s, l in zip(src_refs, land_refs) for j, (tx, ty) in enumerate(_other_chips(x, y))]
    return plan


def _to_sibling(src_refs, land_refs, sending):
    x, y, c = _place()
    return [(s.at[k, 1 - c], l.at[k], (x, y, 1 - c)) for s, l in zip(src_refs, land_refs) for k in range(N_CHIPS)]


def _split_start(name, srcs, lands, plan, n, after=()):
    srcs, lands, after = list(srcs), list(lands), list(after)
    k, kb = len(srcs), len(srcs) + len(lands)

    def body(*refs):
        outs = refs[kb + len(after):]
        for cp in _split_copies(refs[:k], refs[k:kb], outs[:2 * n], plan, True):
            cp.start()
        outs[2 * n + kb][...] = jnp.zeros_like(outs[2 * n + kb])

    outs = pl.pallas_call(
        body, name=name,
        out_shape=(pltpu.SemaphoreType.DMA(()),) * (2 * n) + tuple(pltpu.HBM(a.shape, a.dtype) for a in srcs + lands)
        + (jax.ShapeDtypeStruct((8, LANES), F32),),
        in_specs=(_HBM,) * kb + (_ANY,) * len(after),
        out_specs=(_SEM,) * (2 * n) + (_HBM,) * kb + (pl.BlockSpec(memory_space=pltpu.VMEM),),
        input_output_aliases={i: 2 * n + i for i in range(kb)},
        compiler_params=pltpu.CompilerParams(has_side_effects=_EFFECT),
    )(*[pltpu.with_memory_space_constraint(a, pltpu.HBM) for a in srcs + lands], *after)
    return outs[:2 * n], list(outs[2 * n:2 * n + k]), list(outs[2 * n + k:2 * n + kb]), outs[2 * n + kb]


def _split_wait(name, sems, srcs, lands, after, plan):
    srcs, lands = list(srcs), list(lands)
    k, kb = len(srcs), len(srcs) + len(lands)

    def body(*refs):
        for cp in _split_copies(refs[:k], refs[k:kb], refs[kb:kb + len(sems)], plan, False):
            cp.wait_send()
            cp.wait_recv()

    outs = pl.pallas_call(
        body, name=name, out_shape=tuple(pltpu.HBM(a.shape, a.dtype) for a in srcs + lands),
        in_specs=(_HBM,) * kb + (_SEM,) * len(sems) + (_ANY,) * len(after), out_specs=(_HBM,) * kb,
        input_output_aliases={i: i for i in range(kb)}, compiler_params=pltpu.CompilerParams(has_side_effects=_EFFECT),
    )(*srcs, *lands, *sems, *after)
    return list(outs[:k]), list(outs[k:])


def _forward_landed(src_refs, land_refs, sending):
    x, y, c = _place()
    copies = []
    for ref in src_refs:
        m = ref.shape[0] // 8
        for tx, ty in _other_chips(x, y):
            rows = ref.at[pl.ds((4 * tx + 2 * ty + (c if sending else 1 - c)) * m, m), :]
            copies.append((rows, rows, (x, y, 1 - c)))
    return copies


def _place_own(pieces):
    k = len(pieces)

    def body(*refs):
        piece_refs, out_refs, stages = refs[:k], refs[k:2 * k], refs[2 * k:3 * k]
        load_sems, store_sems = refs[3 * k:]
        x, y, _ = _place()
        loads = [pltpu.make_async_copy(piece_refs[a], stages[a], load_sems.at[a]) for a in range(k)]
        stores = [pltpu.make_async_copy(
            stages[a], out_refs[a].at[pl.ds((2 * x + y) * pieces[a].shape[0], pieces[a].shape[0]), :], store_sems.at[a])
            for a in range(k)]
        for cp in loads:
            cp.start()
        for ld, st in zip(loads, stores):
            ld.wait()
            st.start()
        for cp in stores:
            cp.wait()

    return pl.pallas_call(
        body, name="gather_late_place_own",
        out_shape=[jax.ShapeDtypeStruct((N_CHIPS * p.shape[0], p.shape[1]), p.dtype) for p in pieces],
        in_specs=[_ANY] * k, out_specs=[_ANY] * k,
        scratch_shapes=[pltpu.VMEM(p.shape, p.dtype) for p in pieces]
        + [pltpu.SemaphoreType.DMA((k,)), pltpu.SemaphoreType.DMA((k,))],
    )(*pieces)


def _gather_finish(lands, pieces, name):
    k = len(lands)

    def body(*refs):
        land_refs, piece_refs, out_refs, stages = refs[:k], refs[k:2 * k], refs[2 * k:3 * k], refs[3 * k:4 * k]
        send_sems, recv_sems, load_sems, store_sems = refs[4 * k:]
        x, y, c = _place()
        sibling = (x, y, 1 - c)
        remote, loads, stores, arrivals = [], [], [], []
        for a in range(k):
            m = lands[a].shape[0] // 8

            def rows(px, py, pc, ref, m=m):
                return ref.at[pl.ds((4 * px + 2 * py + pc) * m, m), :]

            for j, (tx, ty) in enumerate(_other_chips(x, y)):
                sems = dict(send_sem=send_sems.at[3 * a + j], recv_sem=recv_sems.at[3 * a + j], device_id=sibling,
                            device_id_type=MESH)
                remote.append(pltpu.make_async_remote_copy(
                    src_ref=rows(tx, ty, c, land_refs[a]), dst_ref=rows(tx, ty, c, out_refs[a]), **sems))
                arrivals.append(pltpu.make_async_remote_copy(
                    src_ref=rows(tx, ty, 1 - c, out_refs[a]), dst_ref=rows(tx, ty, 1 - c, out_refs[a]), **sems))
            for h in range(2):
                loads.append(pltpu.make_async_copy(piece_refs[a].at[pl.ds(h * m, m), :], stages[a].at[h],
                                                   load_sems.at[2 * a + h]))
                stores.append(pltpu.make_async_copy(stages[a].at[h], rows(x, y, h, out_refs[a]), store_sems.at[2 * a + h]))
        for cp in remote + loads:
            cp.start()
        for ld, st in zip(loads, stores):
            ld.wait()
            st.start()
        for cp, arrival in zip(remote, arrivals):
            cp.wait_send()
            arrival.wait_recv()
        for cp in stores:
            cp.wait()

    return pl.pallas_call(
        body, name=name, out_shape=[jax.ShapeDtypeStruct(a.shape, a.dtype) for a in lands],
        in_specs=[_ANY] * (2 * k), out_specs=[_ANY] * k, input_output_aliases={i: i for i in range(k)},
        scratch_shapes=[pltpu.VMEM((2, a.shape[0] // 8, a.shape[1]), a.dtype) for a in lands]
        + [pltpu.SemaphoreType.DMA((3 * k,)), pltpu.SemaphoreType.DMA((3 * k,)), pltpu.SemaphoreType.DMA((2 * k,)),
           pltpu.SemaphoreType.DMA((2 * k,))],
    )(*lands, *pieces)


def _pair_fill(bufs, name):
    k = len(bufs)

    def body(*refs):
        send_sems, recv_sems = refs[-2:]
        x, y, c = _place()
        copies = [pltpu.make_async_remote_copy(
            src_ref=refs[i].at[c], dst_ref=refs[k + i].at[c], send_sem=send_sems.at[i], recv_sem=recv_sems.at[i],
            device_id=(x, y, 1 - c), device_id_type=MESH) for i in range(k)]
        for cp in copies:
            cp.start()
        for i, cp in enumerate(copies):
            cp.wait_send()
            pltpu.make_async_remote_copy(
                src_ref=refs[i].at[1 - c], dst_ref=refs[k + i].at[1 - c], send_sem=send_sems.at[i],
                recv_sem=recv_sems.at[i], device_id=(x, y, 1 - c), device_id_type=MESH).wait_recv()

    return pl.pallas_call(
        body, name=name, out_shape=[jax.ShapeDtypeStruct(a.shape, a.dtype) for a in bufs], in_specs=[_ANY] * k,
        out_specs=[_ANY] * k, input_output_aliases={i: i for i in range(k)},
        scratch_shapes=[pltpu.SemaphoreType.DMA((k,)), pltpu.SemaphoreType.DMA((k,))],
    )(*bufs)


def _row_tile(rows, cap=512):
    for t in range(cap - cap % 8, 7, -8):
        if rows % t == 0:
            return t
    return rows


def _elementwise(fn, n_out, name, *arrs, out_dtype=F32):
    rows, cols = arrs[0].shape
    tr = _row_tile(rows)
    n_in = len(arrs)

    def body(*refs):
        outs = fn(*[r[...].astype(F32) for r in refs[:n_in]])
        for r, o in zip(refs[n_in:], outs):
            r[...] = o.astype(out_dtype)

    spec = pl.BlockSpec((tr, cols), lambda i: (i, 0))
    return pl.pallas_call(
        body, grid=(rows // tr,), name=name, in_specs=[spec] * n_in, out_specs=[spec] * n_out,
        out_shape=[jax.ShapeDtypeStruct((rows, cols), out_dtype)] * n_out, compiler_params=_params("arbitrary"),
    )(*arrs)


def _pair_sums(gpacks, rbigs, ci, name):
    k = len(gpacks)

    def body(c_ref, *refs):
        for g_ref, r_ref, o_ref in zip(refs[:k], refs[k:2 * k], refs[2 * k:]):
            o_ref[...] = (g_ref[...] + r_ref[...]).astype(BF)

    half = lambda a: pl.BlockSpec((None,) + a.shape[1:], lambda s, c: (s, 0, 0))
    return pl.pallas_call(
        body, name=name, out_shape=[jax.ShapeDtypeStruct(r.shape, BF) for r in rbigs],
        grid_spec=pltpu.PrefetchScalarGridSpec(
            num_scalar_prefetch=1, grid=(N_CHIPS,),
            in_specs=[pl.BlockSpec((None, None) + g.shape[2:], lambda s, c: (s, c[0], 0, 0)) for g in gpacks]
            + [half(r) for r in rbigs],
            out_specs=[half(r) for r in rbigs]),
        compiler_params=_params("arbitrary"),
    )(ci.reshape(1), *gpacks, *rbigs)


def _chip_sums(sums, landed, chip, ci, name):
    k = len(sums)

    def body(p_ref, *refs):
        for own_ref, land_ref, o_ref in zip(refs[:k], refs[k:2 * k], refs[2 * k:]):
            f = lambda v: v.astype(F32)
            o_ref[...] = _add4(f(own_ref[...]), f(land_ref[0]), f(land_ref[1]), f(land_ref[2]))[0]

    return pl.pallas_call(
        body, name=name, out_shape=[jax.ShapeDtypeStruct((2,) + s.shape[1:], F32) for s in sums],
        grid_spec=pltpu.PrefetchScalarGridSpec(
            num_scalar_prefetch=1, grid=(1,),
            in_specs=[pl.BlockSpec((None,) + s.shape[1:], lambda i, p: (p[0], 0, 0)) for s in sums]
            + [pl.BlockSpec(l.shape, lambda i, p: (0, 0, 0)) for l in landed],
            out_specs=[pl.BlockSpec((None,) + s.shape[1:], lambda i, p: (p[1], 0, 0)) for s in sums]),
        compiler_params=_params("arbitrary"),
    )(jnp.stack([chip, ci]), *sums, *landed)


def _add2(a, b):
    return (a + b,)


def _add4(own, r0, r1, r2):
    return ((own + r2) + (r0 + r1),)


def _adamw_rows(ws, gs, ms, vs, name):
    k = len(ws)
    steps = next(s for s in (4, 2, 1) if all(w.shape[0] % (8 * s) == 0 for w in ws))

    def body(*refs):
        for i in range(k):
            outs = _adamw_math(*[refs[j * k + i][...] for j in range(4)])
            for j, o in enumerate(outs):
                refs[(4 + j) * k + i][...] = o

    specs = [pl.BlockSpec((w.shape[0] // steps, w.shape[1]), lambda i: (i, 0)) for w in ws]
    return pl.pallas_call(
        body, grid=(steps,), name=name, in_specs=specs * 4, out_specs=specs * 3,
        out_shape=[jax.ShapeDtypeStruct(w.shape, F32) for w in ws] * 3, compiler_params=_params("arbitrary"),
    )(*ws, *gs, *ms, *vs)


def _adamw_small(ws, gs, ms, vs):
    k = len(ws)

    def body(*refs):
        for i in range(k):
            outs = _adamw_math(*[refs[j * k + i][...] for j in range(4)])
            for j, o in enumerate(outs):
                refs[(4 + j) * k + i][...] = o

    return pl.pallas_call(
        body, name="adamw_small", out_shape=[jax.ShapeDtypeStruct(w.shape, F32) for w in ws] * 3,
    )(*ws, *gs, *ms, *vs)


def _adamw_math(w, g, m, v):
    m = ADAM_B1 * m + (1.0 - ADAM_B1) * g
    v = ADAM_B2 * v + (1.0 - ADAM_B2) * (g * g)
    m_hat = m / (1.0 - ADAM_B1 ** ADAM_STEP)
    v_hat = v / (1.0 - ADAM_B2 ** ADAM_STEP)
    delta = -ADAM_LR * (m_hat / (jnp.sqrt(v_hat) + ADAM_EPS) + ADAM_WD * w)
    return delta, m, v


WEIGHTS = ["meta_tokens", "ln1_g", "w_in", "q_a_norm_g", "w_uq", "kv_a_norm_g", "w_ukv", "q_norm_g", "k_norm_g",
           "conv_w", "conv_b", "lru_wa", "lru_ba", "lru_wi", "lru_bi", "lru_lambda", "attn_out_g", "rnn_out_g",
           "w_out", "ln2_g", "w_gate", "w_up", "w_down"]
BIG = ["w_in", "w_uq", "w_ukv", "w_out", "w_gate", "w_up", "w_down"]
BIG_T = {"w_in": True, "w_uq": True, "w_ukv": True, "w_out": False, "w_gate": True, "w_up": True, "w_down": False}
BIG_ROWS = {"w_in": 424, "w_uq": 72, "w_ukv": 64, "w_out": 256, "w_gate": 704, "w_up": 704, "w_down": 704}
EARLY = ["w_in", "w_uq", "w_ukv"]
LATE = ["w_out", "w_gate", "w_up", "w_down"]
EARLY_ROWS = 576
SMALL_SHARDED = ["meta_tokens", "conv_w", "lru_ba", "lru_bi", "lru_lambda"]
SMALL = [n for n in WEIGHTS if n not in BIG]
SMALL_PACK_ROWS = 160


def _offsets(names):
    off, o = {}, 0
    for n in names:
        off[n] = o
        o += BIG_ROWS[n]
    return off


def _shard_pack(names, src, rows):
    parts = [_to_pack_piece(n, src[n]) for n in names]
    used = sum(BIG_ROWS[n] for n in names)
    if rows > used:
        parts.append(jnp.zeros((rows - used, D), F32))
    return jnp.concatenate(parts, axis=0)


def _grad_pack(names, g, rows):
    parts = [g[n].reshape(N_CHIPS, BIG_ROWS[n], D) for n in names]
    used = sum(BIG_ROWS[n] for n in names)
    if rows > used:
        parts.append(jnp.zeros((N_CHIPS, rows - used, D), F32))
    return jnp.concatenate(parts, axis=1).reshape(N_CHIPS, 2, rows // 2, D)


def _to_pack_piece(name, shard):
    a = shard[0].T if BIG_T[name] else shard[0]
    return a.reshape(BIG_ROWS[name], D)


def _flat_pack(arrs, rows):
    flat = jnp.concatenate([a.reshape(-1) for a in arrs])
    return jnp.pad(flat, (0, rows * D - flat.shape[0])).reshape(rows, D)


def _flat_unpack(pack, shapes):
    flat, out, o = pack.reshape(-1), [], 0
    for s in shapes:
        n = math.prod(s)
        out.append(flat[o:o + n].reshape(s))
        o += n
    return out


def kernel(x, meta_tokens, ln1_g, w_in, q_a_norm_g, w_uq, kv_a_norm_g, w_ukv, q_norm_g, k_norm_g, conv_w, conv_b, lru_wa, lru_ba, lru_wi, lru_bi, lru_lambda, attn_out_g, rnn_out_g, w_out, ln2_g, w_gate, w_up, w_down, loss_target, m_meta_tokens, m_ln1_g, m_w_in, m_q_a_norm_g, m_w_uq, m_kv_a_norm_g, m_w_ukv, m_q_norm_g, m_k_norm_g, m_conv_w, m_conv_b, m_lru_wa, m_lru_ba, m_lru_wi, m_lru_bi, m_lru_lambda, m_attn_out_g, m_rnn_out_g, m_w_out, m_ln2_g, m_w_gate, m_w_up, m_w_down, v_meta_tokens, v_ln1_g, v_w_in, v_q_a_norm_g, v_w_uq, v_kv_a_norm_g, v_w_ukv, v_q_norm_g, v_k_norm_g, v_conv_w, v_conv_b, v_lru_wa, v_lru_ba, v_lru_wi, v_lru_bi, v_lru_lambda, v_attn_out_g, v_rnn_out_g, v_w_out, v_ln2_g, v_w_gate, v_w_up, v_w_down):
    wts = dict(zip(WEIGHTS, (meta_tokens, ln1_g, w_in, q_a_norm_g, w_uq, kv_a_norm_g, w_ukv, q_norm_g, k_norm_g, conv_w, conv_b, lru_wa, lru_ba, lru_wi, lru_bi, lru_lambda, attn_out_g, rnn_out_g, w_out, ln2_g, w_gate, w_up, w_down)))
    mom = dict(zip(WEIGHTS, (m_meta_tokens, m_ln1_g, m_w_in, m_q_a_norm_g, m_w_uq, m_kv_a_norm_g, m_w_ukv, m_q_norm_g, m_k_norm_g, m_conv_w, m_conv_b, m_lru_wa, m_lru_ba, m_lru_wi, m_lru_bi, m_lru_lambda, m_attn_out_g, m_rnn_out_g, m_w_out, m_ln2_g, m_w_gate, m_w_up, m_w_down)))
    var = dict(zip(WEIGHTS, (v_meta_tokens, v_ln1_g, v_w_in, v_q_a_norm_g, v_w_uq, v_kv_a_norm_g, v_w_ukv, v_q_norm_g, v_k_norm_g, v_conv_w, v_conv_b, v_lru_wa, v_lru_ba, v_lru_wi, v_lru_bi, v_lru_lambda, v_attn_out_g, v_rnn_out_g, v_w_out, v_ln2_g, v_w_gate, v_w_up, v_w_down)))
    xi, yi, ci = _place()
    chip = 2 * xi + yi
    off_e = _offsets(EARLY)
    half_e = EARLY_ROWS // 2
    gather_plan = _to_chips(lambda ref, tx, ty, c: ref.at[pl.ds(c * (ref.shape[0] // 2), ref.shape[0] // 2), :],
                            lambda ref, j, px, py, c: ref.at[pl.ds((4 * px + 2 * py + c) * (ref.shape[0] // 8),
                                                                   ref.shape[0] // 8), :])
    scatter_plan = _to_chips(lambda ref, tx, ty, c: ref.at[2 * tx + ty], lambda ref, j, px, py, c: ref.at[j])
    everywhere = _to_chips(lambda ref, tx, ty, c: ref, lambda ref, j, px, py, c: ref.at[j])
    n_late = len(LATE)

    pack_e = _shard_pack(EARLY, wts, EARLY_ROWS).astype(BF)
    spack = jnp.concatenate([meta_tokens[:, :LANES], meta_tokens[:, LANES:], conv_w[0], lru_ba[0], lru_bi[0],
                             lru_lambda[0], jnp.zeros((6, LANES), F32)], axis=0)
    sems_g, src_g, land_g, _ = _split_start(
        "gather_early_start", [pack_e, spack], [lax.empty((N_CHIPS * EARLY_ROWS, D), BF), lax.empty((N_CHIPS * 48, LANES), F32)],
        gather_plan, 6)
    tgt_padded = _pad_target(loss_target)
    pieces_l = [_to_pack_piece(n, wts[n]).astype(BF) for n in LATE]
    lands_l = list(_place_own(pieces_l))
    w4, rope = _gate_weights(lru_wa[0], lru_wi[0]), _rope_tables()
    src_g, land_g = _split_wait("gather_early_wait", sems_g, src_g, land_g, [tgt_padded, w4, *rope] + lands_l, gather_plan)
    ge, gs = _gather_finish(land_g, src_g, "gather_early_finish")
    ge = ge.reshape(N_CHIPS, EARLY_ROWS, D)
    gs = gs.reshape(N_CHIPS, 48, LANES)
    full = {n: ge[:, off_e[n]:off_e[n] + BIG_ROWS[n]] for n in EARLY}
    sems_l, src_l, land_l, tied = _split_start("gather_late_start", pieces_l, lands_l, gather_plan, 3 * n_late, after=[ge])

    forward, pair, late = {}, {}, {}

    def late_forward(after):
        _, landed = _split_wait("gather_late_wait", sems_l, src_l, land_l, after, gather_plan)
        forward["sems"], forward["src"], _, zeros = _split_start(
            "gather_late_forward_start", landed, [], _forward_landed, 3 * n_late)
        return zeros[0, 0]

    def late_weights(after):
        (w_out_, w_gate_, w_up_, w_down_), _ = _split_wait(
            "gather_late_forward_wait", forward["sems"], forward["src"], [], after, _forward_landed)
        return dict(w_out=w_out_, w_gate_t=w_gate_, w_up_t=w_up_, w_down=w_down_)

    def early_grads(g_late):
        halves = [g_late[n].reshape(N_CHIPS, 2, BIG_ROWS[n] // 2, D) for n in LATE]
        pair["sems"], pair["src"], pair["land"], zeros = _split_start(
            "grad_pair_late_start", halves, [lax.empty((N_CHIPS, BIG_ROWS[n] // 2, D), F32) for n in LATE], _to_sibling,
            N_CHIPS * n_late)
        return zeros[0, 0]

    def mid_grads(after):
        halves, landed = _split_wait("grad_pair_late_wait", pair["sems"], pair["src"], pair["land"], after, _to_sibling)
        chip_sums = _pair_sums(halves, landed, ci, "grad_pair_sum_late")
        late["sems"], late["src"], late["land"], zeros = _split_start(
            "grad_chip_late_start", chip_sums, [lax.empty((3, BIG_ROWS[n] // 2, D), BF) for n in LATE], scatter_plan,
            3 * n_late)
        return zeros[0, 0]

    cols = lambda a: a.transpose(1, 0, 2).reshape(a.shape[1], N_CHIPS * a.shape[2])
    meta_full = cols(jnp.concatenate([gs[:, 0:16], gs[:, 16:32]], axis=2))
    w = dict(
        w_in_t=full["w_in"].reshape(IN_COLS, D), w_uq_t=full["w_uq"].reshape(N_HEADS * QK_HEAD, Q_LORA),
        w_ukv_t=full["w_ukv"].reshape(2 * D_ATTN, KV_LORA),
        ln1_g=ln1_g, q_a_norm_g=q_a_norm_g, kv_a_norm_g=kv_a_norm_g, q_norm_g=q_norm_g, k_norm_g=k_norm_g,
        conv_w=cols(gs[:, 32:36]), conv_b=conv_b, lru_wa=lru_wa[0], lru_ba=cols(gs[:, 36:38]), lru_wi=lru_wi[0],
        lru_bi=cols(gs[:, 38:40]), lru_lambda=cols(gs[:, 40:42]), attn_out_g=attn_out_g, rnn_out_g=rnn_out_g,
        ln2_g=ln2_g, w4=w4, rope=rope,
    )

    loss_local, grad_x, g, last = _local_step(x, tgt_padded, meta_full + tied[0, 0], w, late_forward, late_weights,
                                              early_grads, mid_grads)

    gpack = _grad_pack(EARLY, {"w_in": g["w_in_t"], "w_uq": g["w_uq_t"], "w_ukv": g["w_ukv_t"]}, EARLY_ROWS)
    full_shapes = {n: wts[n].shape for n in SMALL}
    full_shapes.update(meta_tokens=(N_META, D), conv_w=(1, CONV_W, D_RNN), lru_ba=(1, 2, D_RNN), lru_bi=(1, 2, D_RNN),
                       lru_lambda=(1, 2, D_RNN))
    gsmall = _flat_pack([g[n] for n in SMALL] + [loss_local], SMALL_PACK_ROWS)
    rbig, rsmall = _pair_exchange(gpack, [gsmall], "grad_pair_exchange")
    chip_big = _pair_sums([gpack], [rbig], ci, "grad_pair_sum")
    (chip_small,) = _elementwise(_add2, 1, "grad_pair_sum_small", gsmall, rsmall)
    early_plan = lambda srcs, lands, sending: (scatter_plan(srcs[:1], lands[:1], sending)
                                               + everywhere(srcs[1:], lands[1:], sending))
    sems_e, src_e, land_e, zero_e = _split_start(
        "grad_chip_early_start", list(chip_big) + [chip_small],
        [lax.empty((3, half_e, D), BF), lax.empty((3, SMALL_PACK_ROWS, D), F32)], early_plan, 6)

    grads, delta, new_m, new_v = {}, {}, {}, {}

    def adamw_big(names, gshards):
        as_rows = lambda n, a: a[0].T if BIG_T[n] else a[0]
        back = lambda n, a: a.T[None] if BIG_T[n] else a[None]
        ws, ms, vs = ([as_rows(n, src[n]) for n in names] for src in (wts, mom, var))
        g2 = [gs.reshape(w.shape) for w, gs in zip(ws, gshards)]
        outs = _adamw_rows(ws, g2, ms, vs, "adamw_" + names[0])
        for i, n in enumerate(names):
            grads[n] = back(n, g2[i])
            delta[n], new_m[n], new_v[n] = (back(n, outs[j * len(names) + i]) for j in range(3))
        return outs[0]

    sums, landed = _split_wait("grad_chip_late_wait", late["sems"], late["src"], late["land"], last + [zero_e], scatter_plan)
    shards_l = _pair_fill(_chip_sums(sums, landed, chip, ci, "grad_chip_sum_late"), "grad_pair_fill_late")
    done_late = adamw_big(LATE, shards_l)
    (src_e, src_s), (land_e, land_s) = _split_wait("grad_chip_early_wait", sems_e, src_e, land_e, [done_late, grad_x],
                                                   early_plan)
    (shard_e,) = _pair_fill(_chip_sums([src_e], [land_e], chip, ci, "grad_chip_sum"), "grad_pair_fill_early")
    shard_e = shard_e.reshape(EARLY_ROWS, D)
    for n in EARLY:
        adamw_big([n], [shard_e[off_e[n]:off_e[n] + BIG_ROWS[n]]])
    (small_sum,) = _elementwise(_add4, 1, "grad_chip_sum_small", src_s, land_s[0], land_s[1], land_s[2])
    *small_grads, loss = _flat_unpack(small_sum, [full_shapes[n] for n in SMALL] + [()])
    small_full = dict(zip(SMALL, small_grads))
    for n in SMALL:
        a = small_full[n]
        if n in SMALL_SHARDED:
            width = wts[n].shape[-1]
            a = lax.dynamic_slice_in_dim(a, chip * width, width, axis=a.ndim - 1)
        grads[n] = a.reshape(wts[n].shape)

    rows_of = lambda a: a.reshape(-1, a.shape[-1])
    outs = _adamw_small(*[[rows_of(src[n]) for n in SMALL] for src in (wts, grads, mom, var)])
    for j, dst in enumerate((delta, new_m, new_v)):
        dst.update({n: outs[j * len(SMALL) + i].reshape(wts[n].shape) for i, n in enumerate(SMALL)})

    return (loss, grad_x, *[grads[n] for n in WEIGHTS], *[delta[n] for n in WEIGHTS],
            *[new_m[n] for n in WEIGHTS], *[new_v[n] for n in WEIGHTS])
```

```python
import math

import jax
import jax.numpy as jnp
from jax import lax
from jax.experimental import pallas as pl
from jax.experimental.pallas import tpu as pltpu

F32 = jnp.float32
BF = jnp.bfloat16
MESH = pl.DeviceIdType.MESH

D = 1024
SEQ = 2048
N_META = 16
T = N_META + SEQ
N_HEADS = 8
QK_NOPE = 64
QK_ROPE = 32
QK_HEAD = 96
V_HEAD = 64
Q_LORA = 384
KV_LORA = 256
D_ATTN = 512
D_RNN = 512
RNN_BW = 64
CONV_W = 4
LRU_C = 8.0
ROPE_THETA = 10000.0
D_FF = 2816
EPS = 1e-6
IN_COLS = 1696
ADAM_LR, ADAM_B1, ADAM_B2, ADAM_EPS, ADAM_WD, ADAM_STEP = 0.001, 0.9, 0.999, 1e-08, 0.01, 10

LANES = 128
TP = 2176
NB = 2
R = NB * TP
TR = 256
TRF = 272
TQ = 1088
HP = LANES
PC = 1792
O_CKV, O_KR, O_XR, O_XG = 384, 640, 768, 1280
CG = 128
N_CG = D_RNN // CG
VMEM_LIMIT = 56 * 1024 * 1024
N_CHIPS = 4
SCALE = QK_HEAD ** -0.5
KEY_MASK = -30000.0
LOG2_E = 1.4426950408889634
SCALE_LOG2 = SCALE * LOG2_E


def _nt(a, b):
    return lax.dot_general(a, b, (((1,), (1,)), ((), ())), preferred_element_type=F32)


def _nn(a, b):
    return jnp.dot(a, b, preferred_element_type=F32)


def _tn(a, b):
    return lax.dot_general(a, b, (((0,), (0,)), ((), ())), preferred_element_type=F32)


def _rms(x, g, n):
    ms = jnp.sum(x * x, axis=-1, keepdims=True) * (1.0 / n)
    return x * lax.rsqrt(ms + EPS) * g


def _lane_sum(y):
    return jnp.sum(y, axis=-1, keepdims=True)


def _rot(x):
    lane = lax.broadcasted_iota(jnp.int32, x.shape, 1)
    left = pltpu.roll(x, HP - 16, 1)
    right = pltpu.roll(x, 16, 1)
    lo = (lane >= QK_NOPE) & (lane < QK_NOPE + 16)
    hi = (lane >= QK_NOPE + 16) & (lane < QK_HEAD)
    return jnp.where(lo, -left, jnp.where(hi, right, 0.0))


def _head(x, g, cs, sn):
    n = x * lax.rsqrt(_lane_sum(x * x) * (1.0 / QK_HEAD) + EPS) * g
    return n * cs + _rot(n) * sn


def _head_bwd(x, g, cs, sn, dout):
    rs = lax.rsqrt(_lane_sum(x * x) * (1.0 / QK_HEAD) + EPS)
    xh = x * rs
    dn = dout * cs - _rot(dout * sn)
    gdn = g * dn
    t = _lane_sum(gdn * xh) * (1.0 / QK_HEAD)
    return rs * (gdn - xh * t), jnp.sum(dn * xh, axis=0, keepdims=True)


def _const_spec(shape):
    return pl.BlockSpec(shape, lambda *_: (0,) * len(shape), pipeline_mode=pl.Buffered(1))


def _row_spec(n, tr=TR):
    return pl.BlockSpec((tr, n), lambda i: (i, 0))


def _params(*sem, vmem=VMEM_LIMIT):
    return pltpu.CompilerParams(dimension_semantics=sem, vmem_limit_bytes=vmem)


def _stage_a_fwd(hp, cs, sn, cw):
    def body(hp_ref, cs_ref, sn_ref, ln1, win, qag, wq, kvag, wk, wv, qg, kg,
             pa_ref, xr_ref, xg_ref, q_ref, k_ref, v_ref):
        hn = _rms(hp_ref[...], ln1[...], D).astype(BF)
        p = _nt(hn, win[...])
        pa_ref[...] = p[:, :O_XR]
        xr_ref[...] = p[:, O_XR:O_XG]
        xg_ref[...] = p[:, O_XG:]
        cqn = _rms(p[:, :O_CKV], qag[...], Q_LORA).astype(BF)
        ckvn = _rms(p[:, O_CKV:O_KR], kvag[...], KV_LORA).astype(BF)
        kr = p[:, O_KR:O_XR]
        c, s = cs_ref[...], sn_ref[...]
        mask_lane = lax.broadcasted_iota(jnp.int32, (1, HP), 1) == QK_HEAD
        row = pl.program_id(0) * TRF + lax.broadcasted_iota(jnp.int32, (TRF, 1), 0)
        key_mask = jnp.where(jnp.where(row >= TP, row - TP, row) < T, 0.0, KEY_MASK)
        qraw = _nt(cqn, wq[...])
        kraw = _nt(ckvn, wk[...])
        for h in range(N_HEADS):
            sl = slice(h * HP, (h + 1) * HP)
            q_ref[:, sl] = jnp.where(mask_lane, 1.0, _head(qraw[:, sl], qg[...], c, s)).astype(BF)
            k_ref[:, sl] = jnp.where(mask_lane, key_mask, _head(kraw[:, sl] + kr, kg[...], c, s)).astype(BF)
        v_ref[...] = _nt(ckvn, wv[...]).astype(BF)

    rs = lambda n: _row_spec(n, TRF)
    return pl.pallas_call(
        body, grid=(R // TRF,), name="stage_a_fwd",
        in_specs=[rs(D), rs(HP), rs(HP), _const_spec((1, D)), _const_spec((PC, D)),
                  _const_spec((1, Q_LORA)), _const_spec((N_HEADS * HP, Q_LORA)), _const_spec((1, KV_LORA)),
                  _const_spec((N_HEADS * HP, KV_LORA)), _const_spec((D_ATTN, KV_LORA)), _const_spec((1, HP)),
                  _const_spec((1, HP))],
        out_specs=[rs(O_XR), rs(D_RNN), rs(D_RNN), rs(N_HEADS * HP), rs(N_HEADS * HP), rs(D_ATTN)],
        out_shape=[jax.ShapeDtypeStruct((R, O_XR), F32), jax.ShapeDtypeStruct((R, D_RNN), F32),
                   jax.ShapeDtypeStruct((R, D_RNN), F32), jax.ShapeDtypeStruct((R, N_HEADS * HP), BF),
                   jax.ShapeDtypeStruct((R, N_HEADS * HP), BF), jax.ShapeDtypeStruct((R, D_ATTN), BF)],
        compiler_params=_params("arbitrary"),
    )(hp, cs, sn, cw["ln1_g"], cw["win"], cw["qa_g"], cw["wq"], cw["kva_g"], cw["wk"], cw["wv"], cw["q_g"], cw["k_g"])


def _stage_a_bwd(dq, dk, dv, dxr, dxg, dh1, hp, pa, cs, sn, cw):
    def body(dq_ref, dk_ref, dv_ref, dxr_ref, dxg_ref, dh1_ref, hp_ref, pa_ref, cs_ref, sn_ref,
             ln1, win, qag, wq, kvag, wk, wv, qg, kg,
             dhp_ref, dp_ref, dqraw_ref, dkraw_ref, hn_ref, cqn_ref, ckvn_ref,
             dln1_ref, dqag_ref, dkvag_ref, dqg_ref, dkg_ref):
        @pl.when(pl.program_id(0) == 0)
        def _():
            for r in (dln1_ref, dqag_ref, dkvag_ref, dqg_ref, dkg_ref):
                r[...] = jnp.zeros_like(r)

        hn, vjp_ln1 = jax.vjp(lambda h, g: _rms(h, g, D), hp_ref[...], ln1[...])
        hn_ref[...] = hn.astype(BF)
        pa_v = pa_ref[...]
        cqn, vjp_qa = jax.vjp(lambda x, g: _rms(x, g, Q_LORA), pa_v[:, :O_CKV], qag[...])
        ckvn, vjp_kva = jax.vjp(lambda x, g: _rms(x, g, KV_LORA), pa_v[:, O_CKV:O_KR], kvag[...])
        kr = pa_v[:, O_KR:O_XR]
        cqnb, ckvnb = cqn.astype(BF), ckvn.astype(BF)
        cqn_ref[...] = cqnb
        ckvn_ref[...] = ckvnb
        c, s = cs_ref[...], sn_ref[...]
        lane = lax.broadcasted_iota(jnp.int32, (1, HP), 1)
        rope_lanes = ((lane >= QK_NOPE) & (lane < QK_HEAD)).astype(F32)
        dkr = jnp.zeros((TR, HP), F32)
        dqg = jnp.zeros((1, HP), F32)
        dkg = jnp.zeros((1, HP), F32)
        qraw = _nt(cqnb, wq[...])
        kraw = _nt(ckvnb, wk[...])
        for h in range(N_HEADS):
            sl = slice(h * HP, (h + 1) * HP)
            dqraw, dg = _head_bwd(qraw[:, sl], qg[...], c, s, dq_ref[:, sl])
            dqg = dqg + dg
            dqraw_ref[:, sl] = dqraw.astype(BF)
            dkraw, dg = _head_bwd(kraw[:, sl] + kr, kg[...], c, s, dk_ref[:, sl])
            dkg = dkg + dg
            dkraw_ref[:, sl] = dkraw.astype(BF)
            dkr = dkr + dkraw * rope_lanes
        dcq, dqag = vjp_qa(_nn(dqraw_ref[...], wq[...]))
        dckv, dkvag = vjp_kva(_nn(dkraw_ref[...], wk[...]) + _nn(dv_ref[...].astype(BF), wv[...]))
        dpb = jnp.concatenate([dcq, dckv, dkr, dxr_ref[...], dxg_ref[...]], axis=1).astype(BF)
        dp_ref[...] = dpb
        dh, dln1 = vjp_ln1(_nn(dpb, win[...]))
        dhp_ref[...] = dh + dh1_ref[...]
        dln1_ref[...] += dln1
        dqag_ref[...] += dqag
        dkvag_ref[...] += dkvag
        dqg_ref[...] += dqg
        dkg_ref[...] += dkg

    acc = lambda n: pl.BlockSpec((1, n), lambda i: (0, 0))
    return pl.pallas_call(
        body, grid=(R // TR,), name="stage_a_bwd",
        in_specs=[_row_spec(N_HEADS * HP), _row_spec(N_HEADS * HP), _row_spec(D_ATTN), _row_spec(D_RNN),
                  _row_spec(D_RNN), _row_spec(D), _row_spec(D), _row_spec(O_XR), _row_spec(HP), _row_spec(HP),
                  _const_spec((1, D)), _const_spec((PC, D)), _const_spec((1, Q_LORA)),
                  _const_spec((N_HEADS * HP, Q_LORA)), _const_spec((1, KV_LORA)),
                  _const_spec((N_HEADS * HP, KV_LORA)), _const_spec((D_ATTN, KV_LORA)), _const_spec((1, HP)),
                  _const_spec((1, HP))],
        out_specs=[_row_spec(D), _row_spec(PC), _row_spec(N_HEADS * HP), _row_spec(N_HEADS * HP), _row_spec(D),
                   _row_spec(Q_LORA), _row_spec(KV_LORA), acc(D), acc(Q_LORA), acc(KV_LORA), acc(HP), acc(HP)],
        out_shape=[jax.ShapeDtypeStruct((R, D), F32), jax.ShapeDtypeStruct((R, PC), BF),
                   jax.ShapeDtypeStruct((R, N_HEADS * HP), BF), jax.ShapeDtypeStruct((R, N_HEADS * HP), BF),
                   jax.ShapeDtypeStruct((R, D), BF), jax.ShapeDtypeStruct((R, Q_LORA), BF),
                   jax.ShapeDtypeStruct((R, KV_LORA), BF), jax.ShapeDtypeStruct((1, D), F32),
                   jax.ShapeDtypeStruct((1, Q_LORA), F32), jax.ShapeDtypeStruct((1, KV_LORA), F32),
                   jax.ShapeDtypeStruct((1, HP), F32), jax.ShapeDtypeStruct((1, HP), F32)],
        compiler_params=_params("arbitrary"),
    )(dq, dk, dv, dxr, dxg, dh1, hp, pa, cs, sn, cw["ln1_g"], cw["win"], cw["qa_g"], cw["wq"], cw["kva_g"],
      cw["wk"], cw["wv"], cw["q_g"], cw["k_g"])


def _head_mask(half, dtype):
    lane = lax.broadcasted_iota(jnp.int32, (1, 2 * V_HEAD), 1)
    return ((lane >= V_HEAD) == (half == 1)).astype(dtype)


def _attn_specs(tq):
    n_q = TP // tq
    return (NB, N_HEADS // 2, n_q), dict(
        q=pl.BlockSpec((tq, 2 * HP), lambda b, j, i: (b * n_q + i, j)),
        k=pl.BlockSpec((TP, 2 * HP), lambda b, j, i: (b, j)),
        v=pl.BlockSpec((TP, 2 * V_HEAD), lambda b, j, i: (b, j)),
        o=pl.BlockSpec((tq, 2 * V_HEAD), lambda b, j, i: (b * n_q + i, j)),
        lse=pl.BlockSpec((None, tq, 2), lambda b, j, i: (j, b * n_q + i, 0)))


TQF = 1088


def _attn_fwd(q, k, v):
    def body(q_ref, k_ref, v_ref, o_ref, lse_ref):
        v2 = v_ref[...]
        o = jnp.zeros((TQF, 2 * V_HEAD), F32)
        lse = []
        for hh in range(2):
            sl = slice(hh * HP, (hh + 1) * HP)
            raw = _nt(q_ref[:, sl], k_ref[:, sl])
            m = jnp.max(raw, axis=-1, keepdims=True)
            e = jnp.exp2((raw - m) * SCALE_LOG2)
            l = jnp.sum(e, axis=-1, keepdims=True)
            o = o + _nn(e.astype(BF), v2 * _head_mask(hh, BF)) * (1.0 / l)
            lse.append(m * SCALE_LOG2 + jnp.log(l) * LOG2_E)
        o_ref[...] = o
        lane = lax.broadcasted_iota(jnp.int32, (TQF, 2), 1)
        lse_ref[...] = jnp.where(lane == 0, lse[0], lse[1])

    grid, sp = _attn_specs(TQF)
    return pl.pallas_call(
        body, grid=grid, name="attn_fwd", in_specs=[sp["q"], sp["k"], sp["v"]], out_specs=[sp["o"], sp["lse"]],
        out_shape=[jax.ShapeDtypeStruct((R, D_ATTN), F32), jax.ShapeDtypeStruct((N_HEADS // 2, R, 2), F32)],
        compiler_params=_params("arbitrary", "arbitrary", "arbitrary"),
    )(q, k, v)


def _attn_bwd(q, k, v, o, lse, do):
    def body(q_ref, k_ref, v_ref, o_ref, lse_ref, do_ref, dq_ref, dk_ref, dv_ref):
        @pl.when(pl.program_id(2) == 0)
        def _():
            dk_ref[...] = jnp.zeros_like(dk_ref)
            dv_ref[...] = jnp.zeros_like(dv_ref)

        do = do_ref[...]
        dob = do.astype(BF)
        do_o = do * o_ref[...]
        v2 = v_ref[...]
        dv_sum = jnp.zeros((TP, 2 * V_HEAD), F32)
        for hh in range(2):
            sl = slice(hh * HP, (hh + 1) * HP)
            qb, kb = q_ref[:, sl], k_ref[:, sl]
            p = jnp.exp2(_nt(qb, kb) * SCALE_LOG2 - lse_ref[:, hh:hh + 1])
            dp = _nt(dob, v2 * _head_mask(hh, BF))
            delta = jnp.sum(do_o * _head_mask(hh, F32), axis=-1, keepdims=True)
            dsb = (p * (dp - delta)).astype(BF)
            dq_ref[:, sl] = _nn(dsb, kb) * SCALE
            dk_ref[:, sl] += _tn(dsb, qb) * SCALE
            dv_sum = dv_sum + _tn(p.astype(BF), dob) * _head_mask(hh, F32)
        dv_ref[...] += dv_sum

    grid, sp = _attn_specs(TQ)
    return pl.pallas_call(
        body, grid=grid, name="attn_bwd", in_specs=[sp["q"], sp["k"], sp["v"], sp["o"], sp["lse"], sp["o"]],
        out_specs=[sp["q"], sp["k"], sp["v"]],
        out_shape=[jax.ShapeDtypeStruct((R, N_HEADS * HP), F32), jax.ShapeDtypeStruct((R, N_HEADS * HP), F32),
                   jax.ShapeDtypeStruct((R, D_ATTN), F32)],
        compiler_params=_params("arbitrary", "arbitrary", "arbitrary"),
    )(q, k, v, o, lse, do)


SEG = TP // 8


def _scan_pair(af_ref, bf_ref, hf_ref, ab_ref, bb_ref, hb_ref, pf_ref, pb_ref):
    unroll = 8

    def step(i, carry):
        hf, pf, hb, pb = carry
        for u in range(unroll):
            j = i * unroll + u
            rows_f, rows_b = pl.ds(j, 8, stride=SEG), pl.ds(SEG - 1 - j, 8, stride=SEG)
            a = af_ref[rows_f, :]
            hf, pf = a * hf + bf_ref[rows_f, :], a * pf
            hf_ref[rows_f, :] = hf
            pf_ref[rows_f, :] = pf
            a = ab_ref[rows_b, :]
            hb, pb = a * hb + bb_ref[rows_b, :], a * pb
            hb_ref[rows_b, :] = hb
            pb_ref[rows_b, :] = pb
        return hf, pf, hb, pb

    zero, one = jnp.zeros((8, CG), F32), jnp.ones((8, CG), F32)
    hf, pf, hb, pb = lax.fori_loop(0, SEG // unroll, step, (zero, one, zero, one))
    seg = lax.broadcasted_iota(jnp.int32, (8, CG), 0)
    cf, cb = zero, zero
    for s in range(1, 8):
        cf = jnp.where(seg == s, pltpu.roll(hf + pf * cf, 1, 0), cf)
        cb = jnp.where(seg == 7 - s, pltpu.roll(hb + pb * cb, 7, 0), cb)
    for s in range(8):
        rows = slice(s * SEG, (s + 1) * SEG)
        hf_ref[rows, :] = hf_ref[rows, :] + pf_ref[rows, :] * cf[s:s + 1, :]
        hb_ref[rows, :] = hb_ref[rows, :] + pb_ref[rows, :] * cb[s:s + 1, :]


def _shifts(x):
    t = lax.broadcasted_iota(jnp.int32, x.shape, 0)
    xm2 = jnp.where(t >= 2, pltpu.roll(x, 2, 0), 0.0)
    xm1 = jnp.where(t >= 1, pltpu.roll(x, 1, 0), 0.0)
    xp1 = jnp.where(t < TP - 1, pltpu.roll(x, TP - 1, 0), 0.0)
    return xm2, xm1, xp1


def _softplus(z):
    e = jnp.exp(-jnp.abs(z))
    small = e * (1.0 - e * (0.5 - e * (1.0 / 3.0)))
    return jnp.maximum(z, 0.0) + jnp.where(e < 0.01, small, jnp.log(1.0 + e))


def _sigmoid(x):
    return 0.5 * jnp.tanh(0.5 * x) + 0.5


def _one_minus_sq(log_a, a):
    x = 2.0 * log_a
    series = -x * (1.0 + x * 0.5 * (1.0 + x * (1.0 / 3.0) * (1.0 + x * 0.25)))
    return jnp.where(x > -0.05, series, 1.0 - a * a)


def _gates(row0, xc, pa_f, pi_f, pa_b, pi_b, lam_f, lam_b):
    t = row0 + lax.broadcasted_iota(jnp.int32, xc.shape, 0)
    valid = t < T
    out = []
    for pa, pi_, lam in ((pa_f, pi_f, lam_f), (pa_b, pi_b, lam_b)):
        r = _sigmoid(pa)
        gate_i = _sigmoid(pi_)
        log_a = -LRU_C * r * _softplus(-lam)
        a = jnp.exp(log_a)
        mult = jnp.sqrt(jnp.maximum(_one_minus_sq(log_a, a), 0.0))
        out += [a, jnp.where(valid, mult * (gate_i * xc), 0.0)]
    return tuple(out)


def _gates_bwd(row0, xc, pres, lams, cots):
    t = row0 + lax.broadcasted_iota(jnp.int32, xc.shape, 0)
    valid = t < T
    dxc = jnp.zeros_like(xc)
    dpres, dlams = [], []
    for d in range(2):
        pa, pi_, lam = pres[2 * d], pres[2 * d + 1], lams[d]
        da, db = cots[2 * d], jnp.where(valid, cots[2 * d + 1], 0.0)
        r = _sigmoid(pa)
        gate_i = _sigmoid(pi_)
        sp = _softplus(-lam)
        log_a = -LRU_C * r * sp
        a = jnp.exp(log_a)
        m2 = jnp.maximum(_one_minus_sq(log_a, a), 0.0)
        mult = jnp.sqrt(m2)
        dxc = dxc + db * (mult * gate_i)
        d_gate = db * (mult * xc)
        d_m2 = jnp.where(m2 > 0.0, db * (gate_i * xc) * (0.5 * lax.rsqrt(m2)), 0.0)
        d_log_a = da * a - 2.0 * d_m2 * (a * a)
        dpres += [d_log_a * (-LRU_C * sp) * (r * (1.0 - r)), d_gate * (gate_i * (1.0 - gate_i))]
        d_sp = jnp.sum(d_log_a * (-LRU_C * r), axis=0, keepdims=True)
        dlams.append(-d_sp * jax.nn.sigmoid(-lam))
    return dxc, dpres, dlams


def _rnn_specs():
    seq = pl.BlockSpec((TP, CG), lambda g, b: (b, g))
    return dict(
        seq=seq,
        cw=pl.BlockSpec((CONV_W, CG), lambda g, b: (0, g)),
        cb=pl.BlockSpec((1, CG), lambda g, b: (0, g)),
        w4=pl.BlockSpec((None, CG, 4 * CG), lambda g, b: (g, 0, 0)),
        b4=pl.BlockSpec((None, 1, 4 * CG), lambda g, b: (g, 0, 0)),
        lam=pl.BlockSpec((None, 1, 2 * CG), lambda g, b: (g, 0, 0)),
    )


def _conv(x, xm2, xm1, xp1, cw_ref, cb_ref):
    return cw_ref[0:1, :] * xm2 + cw_ref[1:2, :] * xm1 + cw_ref[2:3, :] * x + cw_ref[3:4, :] * xp1 + cb_ref[...]


TC = 128
N_TC = TP // TC


def _split4(pre):
    return pre[:, :CG], pre[:, CG:2 * CG], pre[:, 2 * CG:3 * CG], pre[:, 3 * CG:]


def _rnn_fwd(xr, xg, cw):
    def body(xr_ref, xg_ref, cw_ref, cb_ref, w4_ref, b4_ref, lam_ref, y_ref, hf_ref, hb_ref, af_ref, ab_ref, xc_ref,
             af, bf, ab, bb, pf, pb):
        x = xr_ref[...]
        xc_ref[...] = _conv(x, *_shifts(x), cw_ref, cb_ref)
        lam = lam_ref[...]

        def chunk(i, _):
            rows = pl.ds(pl.multiple_of(i * TC, TC), TC)
            xc = xc_ref[rows, :]
            pre = _nn(xc.astype(BF), w4_ref[...]) + b4_ref[...]
            a_f, b_f, a_b, b_b = _gates(i * TC, xc, *_split4(pre), lam[:, :CG], lam[:, CG:])
            af[rows, :] = a_f
            bf[rows, :] = b_f
            ab[rows, :] = a_b
            bb[rows, :] = b_b
            af_ref[rows, :] = a_f
            ab_ref[rows, :] = a_b
            return 0

        lax.fori_loop(0, N_TC, chunk, 0)
        _scan_pair(af, bf, hf_ref, ab, bb, hb_ref, pf, pb)
        y_ref[...] = (hf_ref[...] + hb_ref[...]) * jax.nn.gelu(xg_ref[...])

    sp = _rnn_specs()
    return pl.pallas_call(
        body, grid=(N_CG, NB), name="rnn_fwd",
        in_specs=[sp["seq"], sp["seq"], sp["cw"], sp["cb"], sp["w4"], sp["b4"], sp["lam"]],
        out_specs=[sp["seq"]] * 6, out_shape=[jax.ShapeDtypeStruct((R, D_RNN), F32)] * 6,
        scratch_shapes=[pltpu.VMEM((TP, CG), F32)] * 6,
        compiler_params=_params("arbitrary", "arbitrary"),
    )(xr, xg, cw["conv_w"], cw["conv_b"], cw["w4"], cw["b4"], cw["lam"])


def _rnn_bwd(dy, xr, xg, hf, hb, af, ab, xc, cw):
    def body(dy_ref, xr_ref, xg_ref, hf_ref, hb_ref, af_ref, ab_ref, xc_s, cw_ref, cb_ref, w4_ref, b4_ref, lam_ref,
             dxr_ref, dxg_ref, dcw_ref, dcb_ref, dw4_ref, db4_ref, dlam_ref,
             af_s, ab_s, dhs_s, lf_s, lb_s, daf_s, dab_s, dxc_s):
        @pl.when(pl.program_id(1) == 0)
        def _():
            for r in (dcw_ref, dcb_ref, dw4_ref, db4_ref, dlam_ref):
                r[...] = jnp.zeros_like(r)

        lam = lam_ref[...]

        def chunk1(i, _):
            rows = pl.ds(pl.multiple_of(i * TC, TC), TC)
            _, vjp_y = jax.vjp(lambda h, g: h * jax.nn.gelu(g), hf_ref[rows, :] + hb_ref[rows, :], xg_ref[rows, :])
            dhs, dxg = vjp_y(dy_ref[rows, :])
            dhs_s[rows, :] = dhs
            dxg_ref[rows, :] = dxg
            return 0

        lax.fori_loop(0, N_TC, chunk1, 0)
        t = lax.broadcasted_iota(jnp.int32, (TP, CG), 0)
        af_s[...] = pltpu.roll(af_ref[...], TP - 1, 0)
        ab_s[...] = pltpu.roll(ab_ref[...], 1, 0)
        _scan_pair(ab_s, dhs_s, lb_s, af_s, dhs_s, lf_s, dab_s, daf_s)
        daf_s[...] = lf_s[...] * jnp.where(t >= 1, pltpu.roll(hf_ref[...], 1, 0), 0.0)
        dab_s[...] = lb_s[...] * jnp.where(t < TP - 1, pltpu.roll(hb_ref[...], TP - 1, 0), 0.0)

        def chunk2(i, _):
            rows = pl.ds(pl.multiple_of(i * TC, TC), TC)
            xc = xc_s[rows, :]
            xcb = xc.astype(BF)
            pre = _nn(xcb, w4_ref[...]) + b4_ref[...]
            dxc, dpres, dlams = _gates_bwd(i * TC, xc, _split4(pre), (lam[:, :CG], lam[:, CG:]),
                                           (daf_s[rows, :], lf_s[rows, :], dab_s[rows, :], lb_s[rows, :]))
            dpre = jnp.concatenate(dpres, axis=1)
            dpreb = dpre.astype(BF)
            dxc_s[rows, :] = dxc + _nt(dpreb, w4_ref[...])
            dw4_ref[...] += _tn(xcb, dpreb)
            db4_ref[...] += jnp.sum(dpre, axis=0, keepdims=True)
            dlam_ref[...] += jnp.concatenate(dlams, axis=1)
            return 0

        lax.fori_loop(0, N_TC, chunk2, 0)
        dxc = dxc_s[...]
        x = xr_ref[...]
        taps = (jnp.where(t < TP - 2, pltpu.roll(dxc, TP - 2, 0), 0.0), jnp.where(t < TP - 1, pltpu.roll(dxc, TP - 1, 0), 0.0),
                dxc, jnp.where(t >= 1, pltpu.roll(dxc, 1, 0), 0.0))
        dcb_ref[...] += jnp.sum(dxc, axis=0, keepdims=True)
        dxr = jnp.zeros_like(dxc)
        for tap, shifted in enumerate(taps):
            dcw_ref[tap:tap + 1, :] += jnp.sum(x * shifted, axis=0, keepdims=True)
            dxr = dxr + cw_ref[tap:tap + 1, :] * shifted
        dxr_ref[...] = dxr

    sp = _rnn_specs()
    return pl.pallas_call(
        body, grid=(N_CG, NB), name="rnn_bwd",
        in_specs=[sp["seq"]] * 8 + [sp["cw"], sp["cb"], sp["w4"], sp["b4"], sp["lam"]],
        out_specs=[sp["seq"], sp["seq"], sp["cw"], sp["cb"], sp["w4"], sp["b4"], sp["lam"]],
        out_shape=[jax.ShapeDtypeStruct((R, D_RNN), F32), jax.ShapeDtypeStruct((R, D_RNN), F32),
                   jax.ShapeDtypeStruct((CONV_W, D_RNN), F32), jax.ShapeDtypeStruct((1, D_RNN), F32),
                   jax.ShapeDtypeStruct((N_CG, CG, 4 * CG), F32), jax.ShapeDtypeStruct((N_CG, 1, 4 * CG), F32),
                   jax.ShapeDtypeStruct((N_CG, 1, 2 * CG), F32)],
        scratch_shapes=[pltpu.VMEM((TP, CG), F32)] * 8,
        compiler_params=_params("arbitrary", "arbitrary"),
    )(dy, xr, xg, hf, hb, af, ab, xc, cw["conv_w"], cw["conv_b"], cw["w4"], cw["b4"], cw["lam"])


TD = 256
STAGE_D_VMEM = 58 * 1024 * 1024


def _stage_d(hp, o, y, tgt, cw):
    def body(hp_ref, o_ref, y_ref, tgt_ref, ga, gr, wout, ln2, wg, wu, wd,
             do_ref, dy_ref, dh1_ref, mix_ref, dh1b_ref, hn2_ref, dg_ref, du_ref, act_ref, dh2b_ref,
             loss_ref, dga_ref, dgr_ref, dln2_ref):
        i = pl.program_id(0)

        @pl.when(i == 0)
        def _():
            for r in (loss_ref, dga_ref, dgr_ref, dln2_ref):
                r[...] = jnp.zeros_like(r)

        mix_a, vjp_a = jax.vjp(lambda x, g: _rms(x, g, D_ATTN), o_ref[...], ga[...])
        mix_r, vjp_r = jax.vjp(lambda x, g: _rms(x, g, D_RNN), y_ref[...], gr[...])
        mab, mrb = mix_a.astype(BF), mix_r.astype(BF)
        mix_ref[:, :D_ATTN] = mab
        mix_ref[:, D_ATTN:] = mrb
        h1 = hp_ref[...] + _nn(mab, wout[:D_ATTN, :]) + _nn(mrb, wout[D_ATTN:, :])
        hn2, vjp_ln2 = jax.vjp(lambda x, g: _rms(x, g, D), h1, ln2[...])
        hn2b = hn2.astype(BF)
        hn2_ref[...] = hn2b
        act, vjp_act = jax.vjp(lambda g, u: jax.nn.silu(g) * u, _nt(hn2b, wg[...]), _nt(hn2b, wu[...]))
        actb = act.astype(BF)
        act_ref[...] = actb
        h2 = h1 + _nn(actb, wd[...])
        row = i * TD + lax.broadcasted_iota(jnp.int32, (TD, 1), 0)
        t = jnp.where(row >= TP, row - TP, row)
        err = jnp.where((t >= N_META) & (t < T), h2 - tgt_ref[...], 0.0)
        loss_ref[...] += jnp.sum(err * err) * (0.5 / D)
        dh2b = (err * (1.0 / D)).astype(BF)
        dh2b_ref[...] = dh2b
        dg, du = vjp_act(_nt(dh2b, wd[...]))
        dgb, dub = dg.astype(BF), du.astype(BF)
        dg_ref[...] = dgb
        du_ref[...] = dub
        dh1n, dln2 = vjp_ln2(_nn(dgb, wg[...]) + _nn(dub, wu[...]))
        dh1 = err * (1.0 / D) + dh1n
        dh1_ref[...] = dh1
        dh1b = dh1.astype(BF)
        dh1b_ref[...] = dh1b
        dmix = _nt(dh1b, wout[...])
        do, dga = vjp_a(dmix[:, :D_ATTN])
        dyr, dgr = vjp_r(dmix[:, D_ATTN:])
        do_ref[...] = do
        dy_ref[...] = dyr
        dga_ref[...] += dga
        dgr_ref[...] += dgr
        dln2_ref[...] += dln2

    rs = lambda n: _row_spec(n, TD)
    acc = lambda n: pl.BlockSpec((1, n), lambda i: (0, 0))
    return pl.pallas_call(
        body, grid=(R // TD,), name="stage_d",
        in_specs=[rs(D), rs(D_ATTN), rs(D_RNN), rs(D), _const_spec((1, D_ATTN)), _const_spec((1, D_RNN)),
                  _const_spec((D, D)), _const_spec((1, D)), _const_spec((D_FF, D)), _const_spec((D_FF, D)),
                  _const_spec((D_FF, D))],
        out_specs=[rs(D_ATTN), rs(D_RNN), rs(D), rs(D), rs(D), rs(D), rs(D_FF), rs(D_FF), rs(D_FF), rs(D),
                   acc(1), acc(D_ATTN), acc(D_RNN), acc(D)],
        out_shape=[jax.ShapeDtypeStruct((R, D_ATTN), F32), jax.ShapeDtypeStruct((R, D_RNN), F32),
                   jax.ShapeDtypeStruct((R, D), F32), jax.ShapeDtypeStruct((R, D), BF),
                   jax.ShapeDtypeStruct((R, D), BF), jax.ShapeDtypeStruct((R, D), BF),
                   jax.ShapeDtypeStruct((R, D_FF), BF), jax.ShapeDtypeStruct((R, D_FF), BF),
                   jax.ShapeDtypeStruct((R, D_FF), BF), jax.ShapeDtypeStruct((R, D), BF),
                   jax.ShapeDtypeStruct((1, 1), F32), jax.ShapeDtypeStruct((1, D_ATTN), F32),
                   jax.ShapeDtypeStruct((1, D_RNN), F32), jax.ShapeDtypeStruct((1, D), F32)],
        compiler_params=_params("arbitrary", vmem=STAGE_D_VMEM),
    )(hp, o, y, tgt, cw["ga"], cw["gr"], cw["wout"], cw["ln2_g"], cw["wg"], cw["wu"], cw["wd"])


TW = 2176


def _wgrad(a, b, name, tk=None):
    ka, nb = a.shape[1], b.shape[1]
    tk = ka if tk is None else tk

    def body(a_ref, b_ref, o_ref):
        @pl.when(pl.program_id(1) == 0)
        def _():
            o_ref[...] = jnp.zeros_like(o_ref)

        o_ref[...] += _tn(a_ref[...].astype(BF), b_ref[...].astype(BF))

    return pl.pallas_call(
        body, grid=(ka // tk, R // TW), name=name,
        in_specs=[pl.BlockSpec((TW, tk), lambda k, r: (r, k)), pl.BlockSpec((TW, nb), lambda k, r: (r, 0))],
        out_specs=pl.BlockSpec((tk, nb), lambda k, r: (k, 0)),
        out_shape=jax.ShapeDtypeStruct((ka, nb), F32),
        compiler_params=_params("arbitrary", "arbitrary"),
    )(a, b)


def _wgrad_heads(dq, dk, dv, cqn, ckvn):
    def body(dq_ref, dk_ref, dv_ref, cqn_ref, ckvn_ref, oq_ref, ok_ref, ov_ref):
        @pl.when(pl.program_id(0) == 0)
        def _():
            for r in (oq_ref, ok_ref, ov_ref):
                r[...] = jnp.zeros_like(r)

        ckvnb = ckvn_ref[...]
        oq_ref[...] += _tn(dq_ref[...], cqn_ref[...])
        ok_ref[...] += _tn(dk_ref[...], ckvnb)
        ov_ref[...] += _tn(dv_ref[...].astype(BF), ckvnb)

    rows = lambda a: pl.BlockSpec((TW, a.shape[1]), lambda r: (r, 0))
    full = lambda m, n: pl.BlockSpec((m, n), lambda r: (0, 0))
    shapes = [(dq.shape[1], cqn.shape[1]), (dk.shape[1], ckvn.shape[1]), (dv.shape[1], ckvn.shape[1])]
    return pl.pallas_call(
        body, grid=(R // TW,), name="wgrad_heads", in_specs=[rows(a) for a in (dq, dk, dv, cqn, ckvn)],
        out_specs=[full(*s) for s in shapes], out_shape=[jax.ShapeDtypeStruct(s, F32) for s in shapes],
        compiler_params=_params("arbitrary"),
    )(dq, dk, dv, cqn, ckvn)


def _rope_tables():
    half = QK_ROPE // 2
    freqs = 1.0 / (ROPE_THETA ** (jnp.arange(half, dtype=F32) / half))
    ang = jnp.arange(TP, dtype=F32)[:, None] * freqs[None, :]
    ones = jnp.ones((TP, QK_NOPE), F32)
    zeros = jnp.zeros((TP, QK_NOPE), F32)
    pad1 = jnp.ones((TP, HP - QK_HEAD), F32)
    pad0 = jnp.zeros((TP, HP - QK_HEAD), F32)
    cs = jnp.concatenate([ones, jnp.cos(ang), jnp.cos(ang), pad1], axis=1)
    sn = jnp.concatenate([zeros, jnp.sin(ang), jnp.sin(ang), pad0], axis=1)
    return jnp.tile(cs, (NB, 1)), jnp.tile(sn, (NB, 1))


def _pad_rows(a, lo, hi):
    return jnp.pad(a, ((0, 0), (lo, hi), (0, 0)))


def _pad_target(target):
    return _pad_rows(target, N_META, TP - T).reshape(R, D)


def _gate_weights(lru_wa, lru_wi):
    gates = jnp.stack([lru_wa[0], lru_wi[0], lru_wa[1], lru_wi[1]])
    blk = gates.reshape(4, N_CG, 2, RNN_BW, RNN_BW)
    dense = jnp.einsum("tcaij,ab->tcaibj", blk, jnp.eye(2, dtype=F32)).reshape(4, N_CG, CG, CG)
    return dense.transpose(1, 2, 0, 3).reshape(N_CG, CG, 4 * CG).astype(BF)


def _compute_weights(w):
    win_t = w["w_in_t"]
    kr = win_t[O_KR:O_KR + QK_ROPE]
    win = jnp.concatenate([win_t[:O_KR], jnp.zeros((QK_NOPE, D), F32), kr,
                           jnp.zeros((HP - QK_HEAD, D), F32), win_t[O_KR + QK_ROPE:]], axis=0)
    wq = _pad_rows(w["w_uq_t"].reshape(N_HEADS, QK_HEAD, Q_LORA), 0, HP - QK_HEAD)
    wkv = w["w_ukv_t"].reshape(N_HEADS, QK_NOPE + V_HEAD, KV_LORA)
    wk = _pad_rows(wkv[:, :QK_NOPE], 0, HP - QK_NOPE)
    wv = wkv[:, QK_NOPE:].reshape(D_ATTN, KV_LORA)
    bias =jnp.stack([w["lru_ba"][0], w["lru_bi"][0], w["lru_ba"][1], w["lru_bi"][1]])
    b4 = bias.reshape(4, N_CG, CG).transpose(1, 0, 2).reshape(N_CG, 1, 4 * CG)
    lam = w["lru_lambda"].reshape(2, N_CG, CG).transpose(1, 0, 2).reshape(N_CG, 1, 2 * CG)
    pad_g = lambda g: jnp.pad(g.reshape(1, QK_HEAD), ((0, 0), (0, HP - QK_HEAD)))
    return dict(
        ln1_g=w["ln1_g"].reshape(1, D), win=win.astype(BF), qa_g=w["q_a_norm_g"].reshape(1, Q_LORA),
        wq=wq.astype(BF).reshape(N_HEADS * HP, Q_LORA), kva_g=w["kv_a_norm_g"].reshape(1, KV_LORA),
        wk=wk.astype(BF).reshape(N_HEADS * HP, KV_LORA), wv=wv.astype(BF),
        q_g=pad_g(w["q_norm_g"]), k_g=pad_g(w["k_norm_g"]),
        conv_w=w["conv_w"].reshape(CONV_W, D_RNN), conv_b=w["conv_b"].reshape(1, D_RNN),
        w4=w["w4"] if "w4" in w else _gate_weights(w["lru_wa"], w["lru_wi"]), b4=b4, lam=lam,
        ga=w["attn_out_g"].reshape(1, D_ATTN), gr=w["rnn_out_g"].reshape(1, D_RNN), ln2_g=w["ln2_g"].reshape(1, D),
    )


def _local_step(x, target, meta, w, late_forward, late_weights, early_grads, mid_grads):
    cw = _compute_weights(w)
    cs, sn = w["rope"] if "rope" in w else _rope_tables()
    hp = jnp.concatenate([jnp.broadcast_to(meta[None], (NB, N_META, D)), x,
                          jnp.zeros((NB, TP - T, D), F32)], axis=1).reshape(R, D)
    tgt = target if target.ndim == 2 else _pad_target(target)

    pa, xr, xg, q, k, v = _stage_a_fwd(hp, cs, sn, cw)
    o, lse = _attn_fwd(q, k, v)
    cw["conv_b"] = cw["conv_b"] + late_forward([o])
    y, hf, hb, af, ab, xc = _rnn_fwd(xr, xg, cw)
    late = late_weights([y])
    cw.update(wout=late["w_out"], wg=late["w_gate_t"], wu=late["w_up_t"], wd=late["w_down"])
    (do, dy, dh1, mixb, dh1b, hn2b, dgb, dub, actb, dh2b, loss, dga, dgr, dln2) = _stage_d(hp, o, y, tgt, cw)
    dwout = _wgrad(mixb, dh1b, "wgrad_out")
    dwg = _wgrad(dgb, hn2b, "wgrad_gate", tk=D_FF // 2)
    dwu = _wgrad(dub, hn2b, "wgrad_up", tk=D_FF // 2)
    dwd = _wgrad(actb, dh2b, "wgrad_down", tk=D_FF // 2)
    zero = early_grads(dict(w_out=dwout, w_gate=dwg, w_up=dwu, w_down=dwd))
    cw["conv_b"] = cw["conv_b"] + zero
    dxr, dxg, dcw, dcb, dw4, db4, dlam = _rnn_bwd(dy, xr, xg, hf, hb, af, ab, xc, cw)
    zero = mid_grads([dxr])
    dq, dk, dv = _attn_bwd(q, k, v, o, lse, do)
    (dhp, dpb, dqrawb, dkrawb, hn1b, cqnb, ckvnb, dln1, dqag, dkvag, dqg, dkg) = _stage_a_bwd(
        dq, dk, dv, dxr, dxg, dh1, hp, pa, cs, sn, dict(cw, qa_g=cw["qa_g"] + zero))

    dwin = _wgrad(dpb, hn1b, "wgrad_in", tk=PC // 2)
    dwq, dwk, dwv = _wgrad_heads(dqrawb, dkrawb, dv, cqnb, ckvnb)

    dwin_t = jnp.concatenate([dwin[:O_KR], dwin[O_KR + QK_NOPE:O_KR + QK_HEAD], dwin[O_XR:]], axis=0)
    dwq_t = dwq.reshape(N_HEADS, HP, Q_LORA)[:, :QK_HEAD].reshape(N_HEADS * QK_HEAD, Q_LORA)
    dwkv_t = jnp.concatenate([dwk.reshape(N_HEADS, HP, KV_LORA)[:, :QK_NOPE],
                              dwv.reshape(N_HEADS, V_HEAD, KV_LORA)], axis=1).reshape(2 * D_ATTN, KV_LORA)
    d4 = dw4.reshape(N_CG, 2, RNN_BW, 4, 2, RNN_BW)
    dgates = jnp.stack([d4[:, 0, :, :, 0, :], d4[:, 1, :, :, 1, :]], axis=1)
    dgates = dgates.transpose(3, 0, 1, 2, 4).reshape(4, N_HEADS, RNN_BW, RNN_BW)
    dbias = db4.reshape(N_CG, 4, CG).transpose(1, 0, 2).reshape(4, D_RNN)
    dhp3 = dhp.reshape(NB, TP, D)
    grads = dict(
        meta_tokens=jnp.sum(dhp3[:, :N_META], axis=0),
        ln1_g=dln1, w_in_t=dwin_t, q_a_norm_g=dqag, w_uq_t=dwq_t, kv_a_norm_g=dkvag, w_ukv_t=dwkv_t,
        q_norm_g=dqg[:, :QK_HEAD], k_norm_g=dkg[:, :QK_HEAD], conv_w=dcw[None], conv_b=dcb,
        lru_wa=jnp.stack([dgates[0], dgates[2]])[None], lru_ba=jnp.stack([dbias[0], dbias[2]])[None],
        lru_wi=jnp.stack([dgates[1], dgates[3]])[None], lru_bi=jnp.stack([dbias[1], dbias[3]])[None],
        lru_lambda=dlam.reshape(N_CG, 2, CG).transpose(1, 0, 2).reshape(1, 2, D_RNN),
        attn_out_g=dga, rnn_out_g=dgr, ln2_g=dln2,
    )
    return loss[0, 0], dhp3[:, N_META:T], grads, [dhp, dwin]


_ANY = pl.BlockSpec(memory_space=pl.ANY)


def _place():
    return lax.axis_index("x"), lax.axis_index("y"), lax.axis_index("c")


def _other_chips(x, y):
    return [(1 - x, y), (x, 1 - y), (1 - x, 1 - y)]


def _pair_exchange(big, whole, name):
    n_s, _, m, n = big.shape
    n_copies = n_s + len(whole)

    def body(*refs):
        big_ref, whole_refs = refs[0], refs[1:1 + len(whole)]
        rbig_ref, rwhole_refs = refs[1 + len(whole)], refs[2 + len(whole):2 + 2 * len(whole)]
        send_sems, recv_sems = refs[-2:]
        x, y, c = _place()
        sibling = (x, y, 1 - c)
        copies = [pltpu.make_async_remote_copy(
            src_ref=big_ref.at[s, 1 - c], dst_ref=rbig_ref.at[s], send_sem=send_sems.at[s], recv_sem=recv_sems.at[s],
            device_id=sibling, device_id_type=MESH) for s in range(n_s)]
        copies += [pltpu.make_async_remote_copy(
            src_ref=a, dst_ref=r, send_sem=send_sems.at[n_s + i], recv_sem=recv_sems.at[n_s + i],
            device_id=sibling, device_id_type=MESH) for i, (a, r) in enumerate(zip(whole_refs, rwhole_refs))]
        for cp in copies:
            cp.start()
        for cp in copies:
            cp.wait()

    return pl.pallas_call(
        body, name=name,
        out_shape=[jax.ShapeDtypeStruct((n_s, m, n), big.dtype)] + [jax.ShapeDtypeStruct(a.shape, a.dtype) for a in whole],
        in_specs=[_ANY] * (1 + len(whole)), out_specs=[_ANY] * (1 + len(whole)),
        scratch_shapes=[pltpu.SemaphoreType.DMA((n_copies,)), pltpu.SemaphoreType.DMA((n_copies,))],
    )(big, *whole)


_HBM = pl.BlockSpec(memory_space=pltpu.HBM)
_SEM = pl.BlockSpec(memory_space=pltpu.SEMAPHORE)
_EFFECT = pltpu.SideEffectType.DATAFLOW_SIDE_EFFECTING


def _split_copies(src_refs, land_refs, sems, plan, sending):
    n = len(sems) // 2
    return [pltpu.make_async_remote_copy(src_ref=s, dst_ref=d, send_sem=sems[k], recv_sem=sems[n + k], device_id=to,
                                         device_id_type=MESH)
            for k, (s, d, to) in enumerate(plan(src_refs, land_refs, sending))]


def _to_chips(src_at, land_at):
    def plan(src_refs, land_refs, sending):
        x, y, c = _place()
        return [(src_at(s, tx, ty, c), land_at(l, j, *((x, y) if sending else (tx, ty)), c), (tx, ty, c))
                for s, l in zip(src_refs, land_refs) for j, (tx, ty) in enumerate(_other_chips(x, y))]
    return plan


def _to_sibling(src_refs, land_refs, sending):
    x, y, c = _place()
    return [(s.at[k, 1 - c], l.at[k], (x, y, 1 - c)) for s, l in zip(src_refs, land_refs) for k in range(N_CHIPS)]


def _split_start(name, srcs, lands, plan, n, after=()):
    srcs, lands, after = list(srcs), list(lands), list(after)
    k, kb = len(srcs), len(srcs) + len(lands)

    def body(*refs):
        outs = refs[kb + len(after):]
        for cp in _split_copies(refs[:k], refs[k:kb], outs[:2 * n], plan, True):
            cp.start()
        outs[2 * n + kb][...] = jnp.zeros_like(outs[2 * n + kb])

    outs = pl.pallas_call(
        body, name=name,
        out_shape=(pltpu.SemaphoreType.DMA(()),) * (2 * n) + tuple(pltpu.HBM(a.shape, a.dtype) for a in srcs + lands)
        + (jax.ShapeDtypeStruct((8, LANES), F32),),
        in_specs=(_HBM,) * kb + (_ANY,) * len(after),
        out_specs=(_SEM,) * (2 * n) + (_HBM,) * kb + (pl.BlockSpec(memory_space=pltpu.VMEM),),
        input_output_aliases={i: 2 * n + i for i in range(kb)},
        compiler_params=pltpu.CompilerParams(has_side_effects=_EFFECT),
    )(*[pltpu.with_memory_space_constraint(a, pltpu.HBM) for a in srcs + lands], *after)
    return outs[:2 * n], list(outs[2 * n:2 * n + k]), list(outs[2 * n + k:2 * n + kb]), outs[2 * n + kb]


def _split_wait(name, sems, srcs, lands, after, plan):
    srcs, lands = list(srcs), list(lands)
    k, kb = len(srcs), len(srcs) + len(lands)

    def body(*refs):
        for cp in _split_copies(refs[:k], refs[k:kb], refs[kb:kb + len(sems)], plan, False):
            cp.wait_send()
            cp.wait_recv()

    outs = pl.pallas_call(
        body, name=name, out_shape=tuple(pltpu.HBM(a.shape, a.dtype) for a in srcs + lands),
        in_specs=(_HBM,) * kb + (_SEM,) * len(sems) + (_ANY,) * len(after), out_specs=(_HBM,) * kb,
        input_output_aliases={i: i for i in range(kb)}, compiler_params=pltpu.CompilerParams(has_side_effects=_EFFECT),
    )(*srcs, *lands, *sems, *after)
    return list(outs[:k]), list(outs[k:])


def _forward_landed(src_refs, land_refs, sending):
    x, y, c = _place()
    copies = []
    for ref in src_refs:
        m = ref.shape[0] // 8
        for tx, ty in _other_chips(x, y):
            rows = ref.at[pl.ds((4 * tx + 2 * ty + (c if sending else 1 - c)) * m, m), :]
            copies.append((rows, rows, (x, y, 1 - c)))
    return copies


def _place_own(pieces):
    k = len(pieces)

    def body(*refs):
        piece_refs, out_refs, stages = refs[:k], refs[k:2 * k], refs[2 * k:3 * k]
        load_sems, store_sems = refs[3 * k:]
        x, y, _ = _place()
        loads = [pltpu.make_async_copy(piece_refs[a], stages[a], load_sems.at[a]) for a in range(k)]
        stores = [pltpu.make_async_copy(
            stages[a], out_refs[a].at[pl.ds((2 * x + y) * pieces[a].shape[0], pieces[a].shape[0]), :], store_sems.at[a])
            for a in range(k)]
        for cp in loads:
            cp.start()
        for ld, st in zip(loads, stores):
            ld.wait()
            st.start()
        for cp in stores:
            cp.wait()

    return pl.pallas_call(
        body, name="gather_late_place_own",
        out_shape=[jax.ShapeDtypeStruct((N_CHIPS * p.shape[0], p.shape[1]), p.dtype) for p in pieces],
        in_specs=[_ANY] * k, out_specs=[_ANY] * k,
        scratch_shapes=[pltpu.VMEM(p.shape, p.dtype) for p in pieces]
        + [pltpu.SemaphoreType.DMA((k,)), pltpu.SemaphoreType.DMA((k,))],
    )(*pieces)


def _gather_finish(lands, pieces, name):
    k = len(lands)

    def body(*refs):
        land_refs, piece_refs, out_refs, stages = refs[:k], refs[k:2 * k], refs[2 * k:3 * k], refs[3 * k:4 * k]
        send_sems, recv_sems, load_sems, store_sems = refs[4 * k:]
        x, y, c = _place()
        sibling = (x, y, 1 - c)
        remote, loads, stores, arrivals = [], [], [], []
        for a in range(k):
            m = lands[a].shape[0] // 8

            def rows(px, py, pc, ref, m=m):
                return ref.at[pl.ds((4 * px + 2 * py + pc) * m, m), :]

            for j, (tx, ty) in enumerate(_other_chips(x, y)):
                sems = dict(send_sem=send_sems.at[3 * a + j], recv_sem=recv_sems.at[3 * a + j], device_id=sibling,
                            device_id_type=MESH)
                remote.append(pltpu.make_async_remote_copy(
                    src_ref=rows(tx, ty, c, land_refs[a]), dst_ref=rows(tx, ty, c, out_refs[a]), **sems))
                arrivals.append(pltpu.make_async_remote_copy(
                    src_ref=rows(tx, ty, 1 - c, out_refs[a]), dst_ref=rows(tx, ty, 1 - c, out_refs[a]), **sems))
            for h in range(2):
                loads.append(pltpu.make_async_copy(piece_refs[a].at[pl.ds(h * m, m), :], stages[a].at[h],
                                                   load_sems.at[2 * a + h]))
                stores.append(pltpu.make_async_copy(stages[a].at[h], rows(x, y, h, out_refs[a]), store_sems.at[2 * a + h]))
        for cp in remote + loads:
            cp.start()
        for ld, st in zip(loads, stores):
            ld.wait()
            st.start()
        for cp, arrival in zip(remote, arrivals):
            cp.wait_send()
            arrival.wait_recv()
        for cp in stores:
            cp.wait()

    return pl.pallas_call(
        body, name=name, out_shape=[jax.ShapeDtypeStruct(a.shape, a.dtype) for a in lands],
        in_specs=[_ANY] * (2 * k), out_specs=[_ANY] * k, input_output_aliases={i: i for i in range(k)},
        scratch_shapes=[pltpu.VMEM((2, a.shape[0] // 8, a.shape[1]), a.dtype) for a in lands]
        + [pltpu.SemaphoreType.DMA((3 * k,)), pltpu.SemaphoreType.DMA((3 * k,)), pltpu.SemaphoreType.DMA((2 * k,)),
           pltpu.SemaphoreType.DMA((2 * k,))],
    )(*lands, *pieces)


def _pair_fill(bufs, name):
    k = len(bufs)

    def body(*refs):
        send_sems, recv_sems = refs[-2:]
        x, y, c = _place()
        copies = [pltpu.make_async_remote_copy(
            src_ref=refs[i].at[c], dst_ref=refs[k + i].at[c], send_sem=send_sems.at[i], recv_sem=recv_sems.at[i],
            device_id=(x, y, 1 - c), device_id_type=MESH) for i in range(k)]
        for cp in copies:
            cp.start()
        for i, cp in enumerate(copies):
            cp.wait_send()
            pltpu.make_async_remote_copy(
                src_ref=refs[i].at[1 - c], dst_ref=refs[k + i].at[1 - c], send_sem=send_sems.at[i],
                recv_sem=recv_sems.at[i], device_id=(x, y, 1 - c), device_id_type=MESH).wait_recv()

    return pl.pallas_call(
        body, name=name, out_shape=[jax.ShapeDtypeStruct(a.shape, a.dtype) for a in bufs], in_specs=[_ANY] * k,
        out_specs=[_ANY] * k, input_output_aliases={i: i for i in range(k)},
        scratch_shapes=[pltpu.SemaphoreType.DMA((k,)), pltpu.SemaphoreType.DMA((k,))],
    )(*bufs)


def _row_tile(rows, cap=512):
    for t in range(cap - cap % 8, 7, -8):
        if rows % t == 0:
            return t
    return rows


def _elementwise(fn, n_out, name, *arrs, out_dtype=F32):
    rows, cols = arrs[0].shape
    tr = _row_tile(rows)
    n_in = len(arrs)

    def body(*refs):
        outs = fn(*[r[...].astype(F32) for r in refs[:n_in]])
        for r, o in zip(refs[n_in:], outs):
            r[...] = o.astype(out_dtype)

    spec = pl.BlockSpec((tr, cols), lambda i: (i, 0))
    return pl.pallas_call(
        body, grid=(rows // tr,), name=name, in_specs=[spec] * n_in, out_specs=[spec] * n_out,
        out_shape=[jax.ShapeDtypeStruct((rows, cols), out_dtype)] * n_out, compiler_params=_params("arbitrary"),
    )(*arrs)


def _pair_sums(gpacks, rbigs, ci, name):
    k = len(gpacks)

    def body(c_ref, *refs):
        for g_ref, r_ref, o_ref in zip(refs[:k], refs[k:2 * k], refs[2 * k:]):
            o_ref[...] = (g_ref[...] + r_ref[...]).astype(BF)

    half = lambda a: pl.BlockSpec((None,) + a.shape[1:], lambda s, c: (s, 0, 0))
    return pl.pallas_call(
        body, name=name, out_shape=[jax.ShapeDtypeStruct(r.shape, BF) for r in rbigs],
        grid_spec=pltpu.PrefetchScalarGridSpec(
            num_scalar_prefetch=1, grid=(N_CHIPS,),
            in_specs=[pl.BlockSpec((None, None) + g.shape[2:], lambda s, c: (s, c[0], 0, 0)) for g in gpacks]
            + [half(r) for r in rbigs],
            out_specs=[half(r) for r in rbigs]),
        compiler_params=_params("arbitrary"),
    )(ci.reshape(1), *gpacks, *rbigs)


def _chip_sums(sums, landed, chip, ci, name):
    k = len(sums)

    def body(p_ref, *refs):
        for own_ref, land_ref, o_ref in zip(refs[:k], refs[k:2 * k], refs[2 * k:]):
            f = lambda v: v.astype(F32)
            o_ref[...] = _add4(f(own_ref[...]), f(land_ref[0]), f(land_ref[1]), f(land_ref[2]))[0]

    return pl.pallas_call(
        body, name=name, out_shape=[jax.ShapeDtypeStruct((2,) + s.shape[1:], F32) for s in sums],
        grid_spec=pltpu.PrefetchScalarGridSpec(
            num_scalar_prefetch=1, grid=(1,),
            in_specs=[pl.BlockSpec((None,) + s.shape[1:], lambda i, p: (p[0], 0, 0)) for s in sums]
            + [pl.BlockSpec(l.shape, lambda i, p: (0, 0, 0)) for l in landed],
            out_specs=[pl.BlockSpec((None,) + s.shape[1:], lambda i, p: (p[1], 0, 0)) for s in sums]),
        compiler_params=_params("arbitrary"),
    )(jnp.stack([chip, ci]), *sums, *landed)


def _add2(a, b):
    return (a + b,)


def _add4(own, r0, r1, r2):
    return ((own + r2) + (r0 + r1),)


def _adamw_rows(ws, gs, ms, vs, name):
    k = len(ws)
    steps = next(s for s in (4, 2, 1) if all(w.shape[0] % (8 * s) == 0 for w in ws))

    def body(*refs):
        for i in range(k):
            outs = _adamw_math(*[refs[j * k + i][...] for j in range(4)])
            for j, o in enumerate(outs):
                refs[(4 + j) * k + i][...] = o

    specs = [pl.BlockSpec((w.shape[0] // steps, w.shape[1]), lambda i: (i, 0)) for w in ws]
    return pl.pallas_call(
        body, grid=(steps,), name=name, in_specs=specs * 4, out_specs=specs * 3,
        out_shape=[jax.ShapeDtypeStruct(w.shape, F32) for w in ws] * 3, compiler_params=_params("arbitrary"),
    )(*ws, *gs, *ms, *vs)


def _adamw_small(ws, gs, ms, vs):
    k = len(ws)

    def body(*refs):
        for i in range(k):
            outs = _adamw_math(*[refs[j * k + i][...] for j in range(4)])
            for j, o in enumerate(outs):
                refs[(4 + j) * k + i][...] = o

    return pl.pallas_call(
        body, name="adamw_small", out_shape=[jax.ShapeDtypeStruct(w.shape, F32) for w in ws] * 3,
    )(*ws, *gs, *ms, *vs)


def _adamw_math(w, g, m, v):
    m = ADAM_B1 * m + (1.0 - ADAM_B1) * g
    v = ADAM_B2 * v + (1.0 - ADAM_B2) * (g * g)
    m_hat = m / (1.0 - ADAM_B1 ** ADAM_STEP)
    v_hat = v / (1.0 - ADAM_B2 ** ADAM_STEP)
    delta = -ADAM_LR * (m_hat / (jnp.sqrt(v_hat) + ADAM_EPS) + ADAM_WD * w)
    return delta, m, v


WEIGHTS = ["meta_tokens", "ln1_g", "w_in", "q_a_norm_g", "w_uq", "kv_a_norm_g", "w_ukv", "q_norm_g", "k_norm_g",
           "conv_w", "conv_b", "lru_wa", "lru_ba", "lru_wi", "lru_bi", "lru_lambda", "attn_out_g", "rnn_out_g",
           "w_out", "ln2_g", "w_gate", "w_up", "w_down"]
BIG = ["w_in", "w_uq", "w_ukv", "w_out", "w_gate", "w_up", "w_down"]
BIG_T = {"w_in": True, "w_uq": True, "w_ukv": True, "w_out": False, "w_gate": True, "w_up": True, "w_down": False}
BIG_ROWS = {"w_in": 424, "w_uq": 72, "w_ukv": 64, "w_out": 256, "w_gate": 704, "w_up": 704, "w_down": 704}
EARLY = ["w_in", "w_uq", "w_ukv"]
LATE = ["w_out", "w_gate", "w_up", "w_down"]
EARLY_ROWS = 576
SMALL_SHARDED = ["meta_tokens", "conv_w", "lru_ba", "lru_bi", "lru_lambda"]
SMALL = [n for n in WEIGHTS if n not in BIG]
SMALL_PACK_ROWS = 160


def _offsets(names):
    off, o = {}, 0
    for n in names:
        off[n] = o
        o += BIG_ROWS[n]
    return off


def _shard_pack(names, src, rows):
    parts = [_to_pack_piece(n, src[n]) for n in names]
    used = sum(BIG_ROWS[n] for n in names)
    if rows > used:
        parts.append(jnp.zeros((rows - used, D), F32))
    return jnp.concatenate(parts, axis=0)


def _grad_pack(names, g, rows):
    parts = [g[n].reshape(N_CHIPS, BIG_ROWS[n], D) for n in names]
    used = sum(BIG_ROWS[n] for n in names)
    if rows > used:
        parts.append(jnp.zeros((N_CHIPS, rows - used, D), F32))
    return jnp.concatenate(parts, axis=1).reshape(N_CHIPS, 2, rows // 2, D)


def _to_pack_piece(name, shard):
    a = shard[0].T if BIG_T[name] else shard[0]
    return a.reshape(BIG_ROWS[name], D)


def _flat_pack(arrs, rows):
    flat = jnp.concatenate([a.reshape(-1) for a in arrs])
    return jnp.pad(flat, (0, rows * D - flat.shape[0])).reshape(rows, D)


def _flat_unpack(pack, shapes):
    flat, out, o = pack.reshape(-1), [], 0
    for s in shapes:
        n = math.prod(s)
        out.append(flat[o:o + n].reshape(s))
        o += n
    return out


def kernel(x, meta_tokens, ln1_g, w_in, q_a_norm_g, w_uq, kv_a_norm_g, w_ukv, q_norm_g, k_norm_g, conv_w, conv_b, lru_wa, lru_ba, lru_wi, lru_bi, lru_lambda, attn_out_g, rnn_out_g, w_out, ln2_g, w_gate, w_up, w_down, loss_target, m_meta_tokens, m_ln1_g, m_w_in, m_q_a_norm_g, m_w_uq, m_kv_a_norm_g, m_w_ukv, m_q_norm_g, m_k_norm_g, m_conv_w, m_conv_b, m_lru_wa, m_lru_ba, m_lru_wi, m_lru_bi, m_lru_lambda, m_attn_out_g, m_rnn_out_g, m_w_out, m_ln2_g, m_w_gate, m_w_up, m_w_down, v_meta_tokens, v_ln1_g, v_w_in, v_q_a_norm_g, v_w_uq, v_kv_a_norm_g, v_w_ukv, v_q_norm_g, v_k_norm_g, v_conv_w, v_conv_b, v_lru_wa, v_lru_ba, v_lru_wi, v_lru_bi, v_lru_lambda, v_attn_out_g, v_rnn_out_g, v_w_out, v_ln2_g, v_w_gate, v_w_up, v_w_down):
    wts = dict(zip(WEIGHTS, (meta_tokens, ln1_g, w_in, q_a_norm_g, w_uq, kv_a_norm_g, w_ukv, q_norm_g, k_norm_g, conv_w, conv_b, lru_wa, lru_ba, lru_wi, lru_bi, lru_lambda, attn_out_g, rnn_out_g, w_out, ln2_g, w_gate, w_up, w_down)))
    mom = dict(zip(WEIGHTS, (m_meta_tokens, m_ln1_g, m_w_in, m_q_a_norm_g, m_w_uq, m_kv_a_norm_g, m_w_ukv, m_q_norm_g, m_k_norm_g, m_conv_w, m_conv_b, m_lru_wa, m_lru_ba, m_lru_wi, m_lru_bi, m_lru_lambda, m_attn_out_g, m_rnn_out_g, m_w_out, m_ln2_g, m_w_gate, m_w_up, m_w_down)))
    var = dict(zip(WEIGHTS, (v_meta_tokens, v_ln1_g, v_w_in, v_q_a_norm_g, v_w_uq, v_kv_a_norm_g, v_w_ukv, v_q_norm_g, v_k_norm_g, v_conv_w, v_conv_b, v_lru_wa, v_lru_ba, v_lru_wi, v_lru_bi, v_lru_lambda, v_attn_out_g, v_rnn_out_g, v_w_out, v_ln2_g, v_w_gate, v_w_up, v_w_down)))
    xi, yi, ci = _place()
    chip = 2 * xi + yi
    off_e = _offsets(EARLY)
    half_e = EARLY_ROWS // 2
    gather_plan = _to_chips(lambda ref, tx, ty, c: ref.at[pl.ds(c * (ref.shape[0] // 2), ref.shape[0] // 2), :],
                            lambda ref, j, px, py, c: ref.at[pl.ds((4 * px + 2 * py + c) * (ref.shape[0] // 8),
                                                                   ref.shape[0] // 8), :])
    scatter_plan = _to_chips(lambda ref, tx, ty, c: ref.at[2 * tx + ty], lambda ref, j, px, py, c: ref.at[j])
    everywhere = _to_chips(lambda ref, tx, ty, c: ref, lambda ref, j, px, py, c: ref.at[j])
    n_late = len(LATE)

    pack_e = _shard_pack(EARLY, wts, EARLY_ROWS).astype(BF)
    spack = jnp.concatenate([meta_tokens[:, :LANES], meta_tokens[:, LANES:], conv_w[0], lru_ba[0], lru_bi[0],
                             lru_lambda[0], jnp.zeros((6, LANES), F32)], axis=0)
    sems_g, src_g, land_g, _ = _split_start(
        "gather_early_start", [pack_e, spack], [lax.empty((N_CHIPS * EARLY_ROWS, D), BF), lax.empty((N_CHIPS * 48, LANES), F32)],
        gather_plan, 6)
    tgt_padded = _pad_target(loss_target)
    pieces_l = [_to_pack_piece(n, wts[n]).astype(BF) for n in LATE]
    lands_l = list(_place_own(pieces_l))
    w4, rope = _gate_weights(lru_wa[0], lru_wi[0]), _rope_tables()
    src_g, land_g = _split_wait("gather_early_wait", sems_g, src_g, land_g, [tgt_padded, w4, *rope] + lands_l, gather_plan)
    ge, gs = _gather_finish(land_g, src_g, "gather_early_finish")
    ge = ge.reshape(N_CHIPS, EARLY_ROWS, D)
    gs = gs.reshape(N_CHIPS, 48, LANES)
    full = {n: ge[:, off_e[n]:off_e[n] + BIG_ROWS[n]] for n in EARLY}
    sems_l, src_l, land_l, tied = _split_start("gather_late_start", pieces_l, lands_l, gather_plan, 3 * n_late, after=[ge])

    forward, pair, late = {}, {}, {}

    def late_forward(after):
        _, landed = _split_wait("gather_late_wait", sems_l, src_l, land_l, after, gather_plan)
        forward["sems"], forward["src"], _, zeros = _split_start(
            "gather_late_forward_start", landed, [], _forward_landed, 3 * n_late)
        return zeros[0, 0]

    def late_weights(after):
        (w_out_, w_gate_, w_up_, w_down_), _ = _split_wait(
            "gather_late_forward_wait", forward["sems"], forward["src"], [], after, _forward_landed)
        return dict(w_out=w_out_, w_gate_t=w_gate_, w_up_t=w_up_, w_down=w_down_)

    def early_grads(g_late):
        halves = [g_late[n].reshape(N_CHIPS, 2, BIG_ROWS[n] // 2, D) for n in LATE]
        pair["sems"], pair["src"], pair["land"], zeros = _split_start(
            "grad_pair_late_start", halves, [lax.empty((N_CHIPS, BIG_ROWS[n] // 2, D), F32) for n in LATE], _to_sibling,
            N_CHIPS * n_late)
        return zeros[0, 0]

    def mid_grads(after):
        halves, landed = _split_wait("grad_pair_late_wait", pair["sems"], pair["src"], pair["land"], after, _to_sibling)
        chip_sums = _pair_sums(halves, landed, ci, "grad_pair_sum_late")
        late["sems"], late["src"], late["land"], zeros = _split_start(
            "grad_chip_late_start", chip_sums, [lax.empty((3, BIG_ROWS[n] // 2, D), BF) for n in LATE], scatter_plan,
            3 * n_late)
        return zeros[0, 0]

    cols = lambda a: a.transpose(1, 0, 2).reshape(a.shape[1], N_CHIPS * a.shape[2])
    meta_full = cols(jnp.concatenate([gs[:, 0:16], gs[:, 16:32]], axis=2))
    w = dict(
        w_in_t=full["w_in"].reshape(IN_COLS, D), w_uq_t=full["w_uq"].reshape(N_HEADS * QK_HEAD, Q_LORA),
        w_ukv_t=full["w_ukv"].reshape(2 * D_ATTN, KV_LORA),
        ln1_g=ln1_g, q_a_norm_g=q_a_norm_g, kv_a_norm_g=kv_a_norm_g, q_norm_g=q_norm_g, k_norm_g=k_norm_g,
        conv_w=cols(gs[:, 32:36]), conv_b=conv_b, lru_wa=lru_wa[0], lru_ba=cols(gs[:, 36:38]), lru_wi=lru_wi[0],
        lru_bi=cols(gs[:, 38:40]), lru_lambda=cols(gs[:, 40:42]), attn_out_g=attn_out_g, rnn_out_g=rnn_out_g,
        ln2_g=ln2_g, w4=w4, rope=rope,
    )

    loss_local, grad_x, g, last = _local_step(x, tgt_padded, meta_full + tied[0, 0], w, late_forward, late_weights,
                                              early_grads, mid_grads)

    gpack = _grad_pack(EARLY, {"w_in": g["w_in_t"], "w_uq": g["w_uq_t"], "w_ukv": g["w_ukv_t"]}, EARLY_ROWS)
    full_shapes = {n: wts[n].shape for n in SMALL}
    full_shapes.update(meta_tokens=(N_META, D), conv_w=(1, CONV_W, D_RNN), lru_ba=(1, 2, D_RNN), lru_bi=(1, 2, D_RNN),
                       lru_lambda=(1, 2, D_RNN))
    gsmall = _flat_pack([g[n] for n in SMALL] + [loss_local], SMALL_PACK_ROWS)
    rbig, rsmall = _pair_exchange(gpack, [gsmall], "grad_pair_exchange")
    chip_big = _pair_sums([gpack], [rbig], ci, "grad_pair_sum")
    (chip_small,) = _elementwise(_add2, 1, "grad_pair_sum_small", gsmall, rsmall)
    early_plan = lambda srcs, lands, sending: (scatter_plan(srcs[:1], lands[:1], sending)
                                               + everywhere(srcs[1:], lands[1:], sending))
    sems_e, src_e, land_e, zero_e = _split_start(
        "grad_chip_early_start", list(chip_big) + [chip_small],
        [lax.empty((3, half_e, D), BF), lax.empty((3, SMALL_PACK_ROWS, D), F32)], early_plan, 6)

    grads, delta, new_m, new_v = {}, {}, {}, {}

    def adamw_big(names, gshards):
        as_rows = lambda n, a: a[0].T if BIG_T[n] else a[0]
        back = lambda n, a: a.T[None] if BIG_T[n] else a[None]
        ws, ms, vs = ([as_rows(n, src[n]) for n in names] for src in (wts, mom, var))
        g2 = [gs.reshape(w.shape) for w, gs in zip(ws, gshards)]
        outs = _adamw_rows(ws, g2, ms, vs, "adamw_" + names[0])
        for i, n in enumerate(names):
            grads[n] = back(n, g2[i])
            delta[n], new_m[n], new_v[n] = (back(n, outs[j * len(names) + i]) for j in range(3))
        return outs[0]

    sums, landed = _split_wait("grad_chip_late_wait", late["sems"], late["src"], late["land"], last + [zero_e], scatter_plan)
    shards_l = _pair_fill(_chip_sums(sums, landed, chip, ci, "grad_chip_sum_late"), "grad_pair_fill_late")
    done_late = adamw_big(LATE, shards_l)
    (src_e, src_s), (land_e, land_s) = _split_wait("grad_chip_early_wait", sems_e, src_e, land_e, [done_late, grad_x],
                                                   early_plan)
    (shard_e,) = _pair_fill(_chip_sums([src_e], [land_e], chip, ci, "grad_chip_sum"), "grad_pair_fill_early")
    shard_e = shard_e.reshape(EARLY_ROWS, D)
    for n in EARLY:
        adamw_big([n], [shard_e[off_e[n]:off_e[n] + BIG_ROWS[n]]])
    (small_sum,) = _elementwise(_add4, 1, "grad_chip_sum_small", src_s, land_s[0], land_s[1], land_s[2])
    *small_grads, loss = _flat_unpack(small_sum, [full_shapes[n] for n in SMALL] + [()])
    small_full = dict(zip(SMALL, small_grads))
    for n in SMALL:
        a = small_full[n]
        if n in SMALL_SHARDED:
            width = wts[n].shape[-1]
            a = lax.dynamic_slice_in_dim(a, chip * width, width, axis=a.ndim - 1)
        grads[n] = a.reshape(wts[n].shape)

    rows_of = lambda a: a.reshape(-1, a.shape[-1])
    outs = _adamw_small(*[[rows_of(src[n]) for n in SMALL] for src in (wts, grads, mom, var)])
    for j, dst in enumerate((delta, new_m, new_v)):
        dst.update({n: outs[j * len(SMALL) + i].reshape(wts[n].shape) for i, n in enumerate(SMALL)})

    return (loss, grad_x, *[grads[n] for n in WEIGHTS], *[delta[n] for n in WEIGHTS],
            *[new_m[n] for n in WEIGHTS], *[new_v[n] for n in WEIGHTS])
```
